```python
import jax, jax.numpy as jnp
from jax import lax
import numpy as np

D_MODEL = 1024
BATCH = 16
SEQ = 2048
DEPTH = 1

CTX_LEN = 256
GRID_W = 64
MIX_WIDTH = D_MODEL
NA_HEAD_DIM = 64
NA_WIDTH = MIX_WIDTH // 2
NA_HEADS = NA_WIDTH // NA_HEAD_DIM
NA_WIN_ROWS = 8
NA_WIN_COLS = 16
RET_HEADS = 4
RET_WIDTH = MIX_WIDTH - NA_WIDTH
RET_V_DIM = RET_WIDTH // RET_HEADS
RET_QK_DIM = RET_WIDTH // RET_HEADS
RET_QK_WIDTH = RET_HEADS * RET_QK_DIM
RET_CHUNK = 128
IN_WIDTH = 4 * NA_WIDTH + 2 * RET_QK_WIDTH + 2 * RET_WIDTH
ROPE_BASE = 10000.0
EPS = 1e-6

kernel_name = "hybrid_na_retention_dit_block"


def _rmsnorm(x, g):
    xf = x.astype(jnp.float32)
    xf = xf * lax.rsqrt(jnp.mean(xf * xf, axis=-1, keepdims=True) + EPS)
    return (xf * g.astype(jnp.float32)).astype(x.dtype)


def _heads(t, n_heads):
    b, l, _ = t.shape
    return t.reshape(b, l, n_heads, -1).transpose(0, 2, 1, 3)


def _merge(t):
    b, h, l, d = t.shape
    return t.transpose(0, 2, 1, 3).reshape(b, l, h * d)


def _split_proj(p):
    sizes = [NA_WIDTH] * 4 + [RET_QK_WIDTH, RET_QK_WIDTH, RET_WIDTH, RET_WIDTH]
    offs = [int(o) for o in np.cumsum(sizes)[:-1]]
    return jnp.split(p, offs, axis=-1)


def _axial_rotary(x):
    L, dk = x.shape[2], x.shape[-1]
    half = dk // 2
    nf = half // 2
    t = jnp.arange(L)
    row = (t // GRID_W).astype(jnp.float32)
    col = (t % GRID_W).astype(jnp.float32)
    inv = ROPE_BASE ** (-jnp.arange(nf, dtype=jnp.float32) / nf)
    ang = jnp.concatenate([row[:, None] * inv, col[:, None] * inv], axis=-1)
    cos, sin = jnp.cos(ang), jnp.sin(ang)
    xf = x.astype(jnp.float32)
    x1, x2 = xf[..., :half], xf[..., half:]
    out = jnp.concatenate([x1 * cos - x2 * sin, x1 * sin + x2 * cos], axis=-1)
    return out.astype(x.dtype)


def _neighbourhood_attention(q, k, v, k_ctx, v_ctx, rpb):
    B, H, L, dh = q.shape
    rows = L // GRID_W
    kr = min(NA_WIN_ROWS, rows)
    kc = NA_WIN_COLS
    r = jnp.arange(rows)
    r0 = jnp.clip(r - kr // 2, 0, rows - kr)
    row_idx = r0[:, None] + jnp.arange(kr)[None, :]
    cq = jnp.arange(GRID_W)
    c0 = jnp.clip(cq - kc // 2, 0, GRID_W - kc)
    ck = jnp.arange(GRID_W)
    col_in = (ck[None, :] >= c0[:, None]) & (ck[None, :] < c0[:, None] + kc)
    dr_i = row_idx - r[:, None] + NA_WIN_ROWS - 1
    dc_i = jnp.clip(ck[None, :] - cq[:, None] + kc - 1, 0, 2 * kc - 2)
    bias = rpb[:, dr_i[:, None, :, None], dc_i[None, :, None, :]]
    bias = bias.reshape(H, rows, GRID_W, kr * GRID_W).astype(jnp.float32)
    mask = jnp.broadcast_to(col_in[:, None, :], (GRID_W, kr, GRID_W)).reshape(GRID_W, kr * GRID_W)

    kg = k.reshape(B, H, rows, GRID_W, dh)[:, :, row_idx].reshape(B, H, rows, kr * GRID_W, dh)
    vg = v.reshape(B, H, rows, GRID_W, dh)[:, :, row_idx].reshape(B, H, rows, kr * GRID_W, dh)
    qr = q.reshape(B, H, rows, GRID_W, dh)
    scale = dh ** -0.5
    s_loc = jnp.einsum('bhrqd,bhrkd->bhrqk', qr, kg, preferred_element_type=jnp.float32) * scale + bias[None]
    s_loc = jnp.where(mask, s_loc, -jnp.inf)
    s_ctx = jnp.einsum('bhrqd,bhcd->bhrqc', qr, k_ctx, preferred_element_type=jnp.float32) * scale
    p = jax.nn.softmax(jnp.concatenate([s_loc, s_ctx], axis=-1), axis=-1)
    p_loc = p[..., :kr * GRID_W].astype(v.dtype)
    p_ctx = p[..., kr * GRID_W:].astype(v.dtype)
    o = jnp.einsum('bhrqk,bhrkd->bhrqd', p_loc, vg) + jnp.einsum('bhrqc,bhcd->bhrqd', p_ctx, v_ctx)
    return o.reshape(B, H, L, dh)


def _dense_attention(q, k, v):
    s = jnp.einsum('bhqd,bhkd->bhqk', q, k, preferred_element_type=jnp.float32) * (q.shape[-1] ** -0.5)
    p = jax.nn.softmax(s, axis=-1).astype(v.dtype)
    return jnp.einsum('bhqk,bhkd->bhqd', p, v)


def _retention_chunkwise(q, k, v, log_gamma, state0):
    B, H, L, dk = q.shape
    dv = v.shape[-1]
    C = RET_CHUNK
    n = L // C
    qc = q.astype(jnp.float32).reshape(B, H, n, C, dk)
    kc = k.astype(jnp.float32).reshape(B, H, n, C, dk)
    vc = v.astype(jnp.float32).reshape(B, H, n, C, dv)
    lg = log_gamma.astype(jnp.float32)
    i = jnp.arange(C, dtype=jnp.float32)
    dist = i[:, None] - i[None, :]
    decay = jnp.where(dist >= 0, jnp.exp(lg[:, None, None] * jnp.maximum(dist, 0.0)), 0.0)
    s = jnp.einsum('bhnid,bhnjd->bhnij', qc, kc) * decay[None, :, None]
    inner = jnp.einsum('bhnij,bhnje->bhnie', s, vc)
    k_decay = jnp.exp(lg[:, None] * (C - 1 - i))
    k_w = kc * k_decay[None, :, None, :, None]
    chunk_states = jnp.einsum('bhnjd,bhnje->nbhde', k_w, vc)
    chunk_decay = jnp.exp(lg * C)[None, :, None, None]

    def step(S, T):
        return chunk_decay * S + T, S

    S_final, S_prev = lax.scan(step, state0, chunk_states)
    q_decay = jnp.exp(lg[:, None] * (i + 1.0))
    q_w = qc * q_decay[None, :, None, :, None]
    cross = jnp.einsum('bhnid,nbhde->bhnie', q_w, S_prev)
    return (inner + cross).reshape(B, H, L, dv), S_final


def _retention_final_state(k, v, log_gamma):
    L = k.shape[2]
    w = jnp.exp(log_gamma.astype(jnp.float32)[:, None] * (L - 1 - jnp.arange(L, dtype=jnp.float32)))
    return jnp.einsum('bhld,bhle->bhde', k.astype(jnp.float32) * w[None, :, :, None], v.astype(jnp.float32))


def _head_rmsnorm(o, g):
    o = o * lax.rsqrt(jnp.mean(o * o, axis=-1, keepdims=True) + EPS)
    return _merge(o) * g.astype(jnp.float32)


def _mixer(h_lat, h_ctx, w_in, na_rpb, lg_f, lg_b, ret_norm_g, w_out, with_ctx_out):
    dt = h_lat.dtype
    na_q, na_k, na_v, na_g, r_q, r_k, r_v, r_g = _split_proj(h_lat @ w_in)
    cna_q, cna_k, cna_v, cna_g, cr_q, cr_k, cr_v, cr_g = _split_proj(h_ctx @ w_in)

    k_na_ctx = _heads(cna_k, NA_HEADS)
    v_na_ctx = _heads(cna_v, NA_HEADS)
    o_na = _neighbourhood_attention(_heads(na_q, NA_HEADS), _heads(na_k, NA_HEADS),
                                    _heads(na_v, NA_HEADS), k_na_ctx, v_na_ctx, na_rpb)
    y_na = _merge(o_na) * jax.nn.silu(na_g)

    kscale = RET_QK_DIM ** -0.5
    q = _axial_rotary(_heads(r_q, RET_HEADS))
    k = _axial_rotary(_heads(r_k, RET_HEADS)) * kscale
    v = _heads(r_v, RET_HEADS)
    k_ctx = _heads(cr_k, RET_HEADS) * kscale
    v_ctx = _heads(cr_v, RET_HEADS)
    S_f = _retention_final_state(k_ctx, v_ctx, lg_f)
    S_b = _retention_final_state(jnp.flip(k_ctx, 2), jnp.flip(v_ctx, 2), lg_b)
    o_f, _ = _retention_chunkwise(q, k, v, lg_f, S_f)
    o_b, _ = _retention_chunkwise(jnp.flip(q, 2), jnp.flip(k, 2), jnp.flip(v, 2), lg_b, S_b)
    o_ret = o_f + jnp.flip(o_b, 2)
    y_ret = _head_rmsnorm(o_ret, ret_norm_g).astype(dt) * jax.nn.silu(r_g)

    y_lat = jnp.concatenate([y_na, y_ret], axis=-1) @ w_out
    if not with_ctx_out:
        return y_lat, None

    o_cna = _dense_attention(_heads(cna_q, NA_HEADS), k_na_ctx, v_na_ctx)
    yc_na = _merge(o_cna) * jax.nn.silu(cna_g)
    q_ctx = _heads(cr_q, RET_HEADS)
    zeros = jnp.zeros(S_f.shape, jnp.float32)
    oc_f, _ = _retention_chunkwise(q_ctx, k_ctx, v_ctx, lg_f, zeros)
    oc_b, _ = _retention_chunkwise(jnp.flip(q_ctx, 2), jnp.flip(k_ctx, 2), jnp.flip(v_ctx, 2), lg_b, zeros)
    yc_ret = _head_rmsnorm(oc_f + jnp.flip(oc_b, 2), ret_norm_g).astype(dt) * jax.nn.silu(cr_g)
    y_ctx = jnp.concatenate([yc_na, yc_ret], axis=-1) @ w_out
    return y_lat, y_ctx


def _fwd_setup_inputs(seed: int = 0) -> dict:
    key = jax.random.key(seed)
    ks = jax.random.split(key, 16)
    f32 = jnp.float32
    D = D_MODEL
    x = jax.random.normal(ks[0], (BATCH, SEQ, D), f32)
    c = jax.random.normal(ks[1], (BATCH, D), f32)
    ctx = jax.random.normal(ks[2], (BATCH, CTX_LEN, D), f32)
    c_ctx = jax.random.normal(ks[3], (D,), f32)
    norm_g = 1.0 + 0.02 * jax.random.normal(ks[4], (DEPTH, D), f32)
    w_ada = 0.5 * D ** -0.5 * jax.random.normal(ks[5], (DEPTH, D, 3 * D), f32)
    b_ada = 0.01 * jax.random.normal(ks[6], (DEPTH, 3 * D), f32)
    w_in = D ** -0.5 * jax.random.normal(ks[7], (DEPTH, D, IN_WIDTH), f32)
    na_rpb = 0.1 * jax.random.normal(ks[8], (DEPTH, NA_HEADS, 2 * NA_WIN_ROWS - 1, 2 * NA_WIN_COLS - 1), f32)
    gam = 1.0 - 2.0 ** (-5.0 - np.arange(RET_HEADS, dtype=np.float32))
    base = jnp.asarray(np.log(-np.log(gam)).astype(np.float32))
    ret_decay_fwd = base[None] + 0.05 * jax.random.normal(ks[9], (DEPTH, RET_HEADS), f32)
    ret_decay_bwd = base[None] + 0.05 * jax.random.normal(ks[10], (DEPTH, RET_HEADS), f32)
    ret_norm_g = 1.0 + 0.02 * jax.random.normal(ks[11], (DEPTH, RET_WIDTH), f32)
    w_out = MIX_WIDTH ** -0.5 * jax.random.normal(ks[12], (DEPTH, MIX_WIDTH, D), f32)
    final_norm_g = 1.0 + 0.02 * jax.random.normal(ks[13], (D,), f32)
    return {"x": x, "c": c, "ctx": ctx, "c_ctx": c_ctx, "norm_g": norm_g, "w_ada": w_ada,
            "b_ada": b_ada, "w_in": w_in, "na_rpb": na_rpb, "ret_decay_fwd": ret_decay_fwd,
            "ret_decay_bwd": ret_decay_bwd, "ret_norm_g": ret_norm_g, "w_out": w_out,
            "final_norm_g": final_norm_g}


def _fwd_reference(x, c, ctx, c_ctx, norm_g, w_ada, b_ada, w_in, na_rpb, ret_decay_fwd,
              ret_decay_bwd, ret_norm_g, w_out, final_norm_g):
    D = D_MODEL
    for i in range(DEPTH):
        update_ctx = i < DEPTH - 1
        mod = jax.nn.silu(c) @ w_ada[i] + b_ada[i]
        mod_c = jax.nn.silu(c_ctx) @ w_ada[i] + b_ada[i]
        shift, scale, gate = mod[:, None, :D], mod[:, None, D:2 * D], mod[:, None, 2 * D:]
        shift_c, scale_c, gate_c = mod_c[:D], mod_c[D:2 * D], mod_c[2 * D:]
        h_lat = _rmsnorm(x, norm_g[i]) * (1.0 + scale) + shift
        h_ctx = _rmsnorm(ctx, norm_g[i]) * (1.0 + scale_c) + shift_c
        lg_f = -jnp.exp(ret_decay_fwd[i].astype(jnp.float32))
        lg_b = -jnp.exp(ret_decay_bwd[i].astype(jnp.float32))
        y_lat, y_ctx = _mixer(h_lat, h_ctx, w_in[i], na_rpb[i], lg_f, lg_b, ret_norm_g[i],
                              w_out[i], update_ctx)
        x = x + gate * y_lat
        if update_ctx:
            ctx = ctx + gate_c * y_ctx
    return _rmsnorm(x, final_norm_g)


import jax as _jax
import jax.numpy as _jnp

TWIN_FORMAT = 'train_step'
FWD_PARAMS = ['x', 'c', 'ctx', 'c_ctx', 'norm_g', 'w_ada', 'b_ada', 'w_in', 'na_rpb', 'ret_decay_fwd', 'ret_decay_bwd', 'ret_norm_g', 'w_out', 'final_norm_g']
TWIN_WEIGHTS = ['c_ctx', 'norm_g', 'w_ada', 'b_ada', 'w_in', 'na_rpb', 'ret_decay_fwd', 'ret_decay_bwd', 'ret_norm_g', 'w_out', 'final_norm_g']
TWIN_DIFF_INPUT = 'x'
TWIN_INPUTS = ['x', 'c', 'ctx', 'c_ctx', 'norm_g', 'w_ada', 'b_ada', 'w_in', 'na_rpb', 'ret_decay_fwd', 'ret_decay_bwd', 'ret_norm_g', 'w_out', 'final_norm_g', 'loss_target', 'm_c_ctx', 'm_norm_g', 'm_w_ada', 'm_b_ada', 'm_w_in', 'm_na_rpb', 'm_ret_decay_fwd', 'm_ret_decay_bwd', 'm_ret_norm_g', 'm_w_out', 'm_final_norm_g', 'v_c_ctx', 'v_norm_g', 'v_w_ada', 'v_b_ada', 'v_w_in', 'v_na_rpb', 'v_ret_decay_fwd', 'v_ret_decay_bwd', 'v_ret_norm_g', 'v_w_out', 'v_final_norm_g']
TWIN_OUTPUTS = ['loss', 'grad_x', 'grad_c_ctx', 'grad_norm_g', 'grad_w_ada', 'grad_b_ada', 'grad_w_in', 'grad_na_rpb', 'grad_ret_decay_fwd', 'grad_ret_decay_bwd', 'grad_ret_norm_g', 'grad_w_out', 'grad_final_norm_g', 'delta_c_ctx', 'delta_norm_g', 'delta_w_ada', 'delta_b_ada', 'delta_w_in', 'delta_na_rpb', 'delta_ret_decay_fwd', 'delta_ret_decay_bwd', 'delta_ret_norm_g', 'delta_w_out', 'delta_final_norm_g', 'new_m_c_ctx', 'new_m_norm_g', 'new_m_w_ada', 'new_m_b_ada', 'new_m_w_in', 'new_m_na_rpb', 'new_m_ret_decay_fwd', 'new_m_ret_decay_bwd', 'new_m_ret_norm_g', 'new_m_w_out', 'new_m_final_norm_g', 'new_v_c_ctx', 'new_v_norm_g', 'new_v_w_ada', 'new_v_b_ada', 'new_v_w_in', 'new_v_na_rpb', 'new_v_ret_decay_fwd', 'new_v_ret_decay_bwd', 'new_v_ret_norm_g', 'new_v_w_out', 'new_v_final_norm_g']
TWIN_LEAF_KINDS = {'loss': 'loss', 'grad_x': 'grad_x', 'grad_c_ctx': 'grad_w', 'grad_norm_g': 'grad_w', 'grad_w_ada': 'grad_w', 'grad_b_ada': 'grad_w', 'grad_w_in': 'grad_w', 'grad_na_rpb': 'grad_w', 'grad_ret_decay_fwd': 'grad_w', 'grad_ret_decay_bwd': 'grad_w', 'grad_ret_norm_g': 'grad_w', 'grad_w_out': 'grad_w', 'grad_final_norm_g': 'grad_w', 'delta_c_ctx': 'delta_w', 'delta_norm_g': 'delta_w', 'delta_w_ada': 'delta_w', 'delta_b_ada': 'delta_w', 'delta_w_in': 'delta_w', 'delta_na_rpb': 'delta_w', 'delta_ret_decay_fwd': 'delta_w', 'delta_ret_decay_bwd': 'delta_w', 'delta_ret_norm_g': 'delta_w', 'delta_w_out': 'delta_w', 'delta_final_norm_g': 'delta_w', 'new_m_c_ctx': 'new_m', 'new_m_norm_g': 'new_m', 'new_m_w_ada': 'new_m', 'new_m_b_ada': 'new_m', 'new_m_w_in': 'new_m', 'new_m_na_rpb': 'new_m', 'new_m_ret_decay_fwd': 'new_m', 'new_m_ret_decay_bwd': 'new_m', 'new_m_ret_norm_g': 'new_m', 'new_m_w_out': 'new_m', 'new_m_final_norm_g': 'new_m', 'new_v_c_ctx': 'new_v', 'new_v_norm_g': 'new_v', 'new_v_w_ada': 'new_v', 'new_v_b_ada': 'new_v', 'new_v_w_in': 'new_v', 'new_v_na_rpb': 'new_v', 'new_v_ret_decay_fwd': 'new_v', 'new_v_ret_decay_bwd': 'new_v', 'new_v_ret_norm_g': 'new_v', 'new_v_w_out': 'new_v', 'new_v_final_norm_g': 'new_v'}


def _forward(args):
    return _fwd_reference(*[args[k] for k in FWD_PARAMS])


def _output_shape():
    out = _jax.eval_shape(lambda: _forward(_fwd_setup_inputs(0)))
    return out.shape, out.dtype

N_MICROBATCH = 1
ADAM_LR = 0.001
ADAM_B1 = 0.9
ADAM_B2 = 0.999
ADAM_EPS = 1e-08
ADAM_WD = 0.01
ADAM_STEP = 10
PER_EXAMPLE_BATCH_AXIS = {'x': 0, 'c': 0, 'ctx': 0, 'loss_target': 0}
SHARED_INPUTS = []
_WEIGHT_DTYPES = {'c_ctx': _jnp.float32, 'norm_g': _jnp.float32, 'w_ada': _jnp.float32, 'b_ada': _jnp.float32, 'w_in': _jnp.float32, 'na_rpb': _jnp.float32, 'ret_decay_fwd': _jnp.float32, 'ret_decay_bwd': _jnp.float32, 'ret_norm_g': _jnp.float32, 'w_out': _jnp.float32, 'final_norm_g': _jnp.float32}
MOMENT_SCALE = {'c_ctx': 1.495272e-02, 'norm_g': 5.168164e-02, 'w_ada': 4.890620e-02, 'b_ada': 8.006432e-02, 'w_in': 2.795528e-02, 'na_rpb': 8.832277e-04, 'ret_decay_fwd': 2.107401e-01, 'ret_decay_bwd': 1.441961e-01, 'ret_norm_g': 3.812072e-02, 'w_out': 2.601733e-02, 'final_norm_g': 3.197953e+01}


def _to_microbatches(a, axis):
    t = _jnp.moveaxis(a, axis, 0)
    t = t.reshape((N_MICROBATCH, t.shape[0] // N_MICROBATCH) + t.shape[1:])
    return _jnp.moveaxis(t, 1, axis + 1)


def setup_inputs(seed: int = 0) -> dict:
    inp = _fwd_setup_inputs(seed)
    key = _jax.random.fold_in(_jax.random.key(seed), 7919)
    shape, _ = _output_shape()
    out = dict(inp)
    out["loss_target"] = _jax.random.normal(_jax.random.fold_in(key, 0), shape, _jnp.float32)
    for i, name in enumerate(TWIN_WEIGHTS):
        w = inp[name].astype(_jnp.float32)
        if MOMENT_SCALE is None:
            s = _jnp.sqrt(_jnp.mean(_jnp.square(w)) + 1e-30)
        else:
            s = MOMENT_SCALE[name]
        km, kv = _jax.random.split(_jax.random.fold_in(key, i + 1))
        out[name] = w
        out["m_" + name] = s * _jax.random.normal(km, w.shape, _jnp.float32)
        out["v_" + name] = (s * s) * _jax.random.uniform(kv, w.shape, _jnp.float32, 0.5, 1.5)
    if N_MICROBATCH > 1:
        for name, axis in PER_EXAMPLE_BATCH_AXIS.items():
            out[name] = _to_microbatches(out[name], axis)
    return {'x': out['x'], 'c': out['c'], 'ctx': out['ctx'], 'c_ctx': out['c_ctx'], 'norm_g': out['norm_g'], 'w_ada': out['w_ada'], 'b_ada': out['b_ada'], 'w_in': out['w_in'], 'na_rpb': out['na_rpb'], 'ret_decay_fwd': out['ret_decay_fwd'], 'ret_decay_bwd': out['ret_decay_bwd'], 'ret_norm_g': out['ret_norm_g'], 'w_out': out['w_out'], 'final_norm_g': out['final_norm_g'], 'loss_target': out['loss_target'], 'm_c_ctx': out['m_c_ctx'], 'm_norm_g': out['m_norm_g'], 'm_w_ada': out['m_w_ada'], 'm_b_ada': out['m_b_ada'], 'm_w_in': out['m_w_in'], 'm_na_rpb': out['m_na_rpb'], 'm_ret_decay_fwd': out['m_ret_decay_fwd'], 'm_ret_decay_bwd': out['m_ret_decay_bwd'], 'm_ret_norm_g': out['m_ret_norm_g'], 'm_w_out': out['m_w_out'], 'm_final_norm_g': out['m_final_norm_g'], 'v_c_ctx': out['v_c_ctx'], 'v_norm_g': out['v_norm_g'], 'v_w_ada': out['v_w_ada'], 'v_b_ada': out['v_b_ada'], 'v_w_in': out['v_w_in'], 'v_na_rpb': out['v_na_rpb'], 'v_ret_decay_fwd': out['v_ret_decay_fwd'], 'v_ret_decay_bwd': out['v_ret_decay_bwd'], 'v_ret_norm_g': out['v_ret_norm_g'], 'v_w_out': out['v_w_out'], 'v_final_norm_g': out['v_final_norm_g']}


def _loss(weights, diff, rest, loss_target):
    with _jax.named_scope("forward"):
        args = {**rest, TWIN_DIFF_INPUT: diff, **{k: w.astype(_WEIGHT_DTYPES[k]) for k, w in weights.items()}}
        y = _forward(args)
    with _jax.named_scope("loss_head"):
        err = _jnp.square(y.astype(_jnp.float32) - loss_target)
        return 0.5 * _jnp.sum(_jnp.mean(err, axis=-1)) if err.ndim else 0.5 * err


def _adamw(w, g, m, v):
    m = ADAM_B1 * m + (1.0 - ADAM_B1) * g
    v = ADAM_B2 * v + (1.0 - ADAM_B2) * _jnp.square(g)
    m_hat = m / (1.0 - ADAM_B1 ** ADAM_STEP)
    v_hat = v / (1.0 - ADAM_B2 ** ADAM_STEP)
    delta = -ADAM_LR * (m_hat / (_jnp.sqrt(v_hat) + ADAM_EPS) + ADAM_WD * w)
    return delta, m, v


def reference(x, c, ctx, c_ctx, norm_g, w_ada, b_ada, w_in, na_rpb, ret_decay_fwd, ret_decay_bwd, ret_norm_g, w_out, final_norm_g, loss_target, m_c_ctx, m_norm_g, m_w_ada, m_b_ada, m_w_in, m_na_rpb, m_ret_decay_fwd, m_ret_decay_bwd, m_ret_norm_g, m_w_out, m_final_norm_g, v_c_ctx, v_norm_g, v_w_ada, v_b_ada, v_w_in, v_na_rpb, v_ret_decay_fwd, v_ret_decay_bwd, v_ret_norm_g, v_w_out, v_final_norm_g):
    given = dict(x=x, c=c, ctx=ctx, c_ctx=c_ctx, norm_g=norm_g, w_ada=w_ada, b_ada=b_ada, w_in=w_in, na_rpb=na_rpb, ret_decay_fwd=ret_decay_fwd, ret_decay_bwd=ret_decay_bwd, ret_norm_g=ret_norm_g, w_out=w_out, final_norm_g=final_norm_g, loss_target=loss_target, m_c_ctx=m_c_ctx, m_norm_g=m_norm_g, m_w_ada=m_w_ada, m_b_ada=m_b_ada, m_w_in=m_w_in, m_na_rpb=m_na_rpb, m_ret_decay_fwd=m_ret_decay_fwd, m_ret_decay_bwd=m_ret_decay_bwd, m_ret_norm_g=m_ret_norm_g, m_w_out=m_w_out, m_final_norm_g=m_final_norm_g, v_c_ctx=v_c_ctx, v_norm_g=v_norm_g, v_w_ada=v_w_ada, v_b_ada=v_b_ada, v_w_in=v_w_in, v_na_rpb=v_na_rpb, v_ret_decay_fwd=v_ret_decay_fwd, v_ret_decay_bwd=v_ret_decay_bwd, v_ret_norm_g=v_ret_norm_g, v_w_out=v_w_out, v_final_norm_g=v_final_norm_g)
    weights = {n: given[n] for n in TWIN_WEIGHTS}
    shared = {n: given[n] for n in SHARED_INPUTS}
    per_example = {n: given[n] for n in ['x', 'c', 'ctx']}
    grad_fn = _jax.value_and_grad(_loss, argnums=(0, 1))

    def one_microbatch(ex, loss_target):
        ex = dict(ex)
        diff = ex.pop(TWIN_DIFF_INPUT)
        return grad_fn(weights, diff, {**shared, **ex}, loss_target)

    if N_MICROBATCH == 1:
        loss, (grad_w, grad_x) = one_microbatch(per_example, given["loss_target"])
    else:
        def body(carry, xs):
            loss_sum, grad_sum = carry
            l_k, (gw_k, gx_k) = one_microbatch(xs[0], xs[1])
            with _jax.named_scope("update"):
                return (loss_sum + l_k, _jax.tree.map(_jnp.add, grad_sum, gw_k)), gx_k

        init = (_jnp.zeros((), _jnp.float32), _jax.tree.map(_jnp.zeros_like, weights))
        (loss, grad_w), grad_x = _jax.lax.scan(body, init, (per_example, given["loss_target"]))
    with _jax.named_scope("update"):
        delta_w, new_m, new_v = {}, {}, {}
        for n in TWIN_WEIGHTS:
            delta_w[n], new_m[n], new_v[n] = _adamw(weights[n], grad_w[n], given["m_" + n], given["v_" + n])
    return (loss, grad_x, *[grad_w[n] for n in TWIN_WEIGHTS], *[delta_w[n] for n in TWIN_WEIGHTS],
            *[new_m[n] for n in TWIN_WEIGHTS], *[new_v[n] for n in TWIN_WEIGHTS])
```

```python
import functools

import numpy as np
import jax
import jax.numpy as jnp
from jax import lax
from jax.experimental import pallas as pl
from jax.experimental.pallas import tpu as pltpu

F32 = jnp.float32
BF16 = jnp.bfloat16
HIGHEST = lax.Precision.HIGHEST

D = 1024
GRID_W = 64
NA_DH = 64
RET_DK = 128
ROPE_BASE = 10000.0
EPS = 1e-6
NEG = -1e30
TQ = 256
TK = 512
KW = 12 * GRID_W
N_SHARD = 4
N_DEV = 8
SM_ROWS = 24

ADAM_LR = 0.001
ADAM_B1 = 0.9
ADAM_B2 = 0.999
ADAM_EPS = 1e-08
ADAM_WD = 0.01
ADAM_STEP = 10

MESH = pl.DeviceIdType.MESH
ANY = pl.BlockSpec(memory_space=pl.ANY)


def _params(sem=None, vmem_mb=48):
    return pltpu.CompilerParams(dimension_semantics=sem, vmem_limit_bytes=vmem_mb << 20)


def _dot(a, b):
    return jnp.dot(a, b, preferred_element_type=F32)


def _dot_nt(a, b):
    return lax.dot_general(a, b, (((1,), (1,)), ((), ())), preferred_element_type=F32)


def _dot_tn(a, b):
    return lax.dot_general(a, b, (((0,), (0,)), ((), ())), preferred_element_type=F32)


def _sigmoid(x):
    return 1.0 / (1.0 + jnp.exp(-x))


def _rope_tables(L, LC):
    half = RET_DK // 2
    nf = half // 2
    t = np.arange(L)
    row = (t // GRID_W).astype(np.float32)
    col = (t % GRID_W).astype(np.float32)
    inv = (np.float32(ROPE_BASE) ** (-np.arange(nf, dtype=np.float32) / np.float32(nf))).astype(np.float32)
    ang = np.concatenate([row[:, None] * inv, col[:, None] * inv], axis=-1).astype(np.float32)
    cos, sin = np.cos(ang).astype(np.float32), np.sin(ang).astype(np.float32)
    cos2 = np.concatenate([cos, cos], axis=-1)
    sin2 = np.concatenate([-sin, sin], axis=-1)
    cos2 = np.concatenate([cos2, np.ones((LC, RET_DK), np.float32)], axis=0)
    sin2 = np.concatenate([sin2, np.zeros((LC, RET_DK), np.float32)], axis=0)
    return jnp.asarray(cos2), jnp.asarray(sin2)


def _mod_call(c8, wada_f, b_ada):
    ws = wada_f.shape[2]

    def body(c_ref, w_ref, b_ref, o_ref):
        a = c_ref[...]
        a = (a * _sigmoid(a)).astype(BF16)
        for s in range(N_SHARD):
            o_ref[:, s * ws:(s + 1) * ws] = _dot(a, w_ref[s]) + b_ref[:, s * ws:(s + 1) * ws]

    return pl.pallas_call(
        body, name="ada_mod", out_shape=jax.ShapeDtypeStruct((8, 3 * D), F32),
        compiler_params=_params())(c8, wada_f, b_ada)


def _dc_masks():
    cq = lax.broadcasted_iota(jnp.int32, (GRID_W, GRID_W), 0)
    ck = lax.broadcasted_iota(jnp.int32, (GRID_W, GRID_W), 1)
    dc = jnp.clip(ck - cq + 15, 0, 30)
    c0 = jnp.clip(cq - 8, 0, GRID_W - 16)
    col_ok = (ck >= c0) & (ck < c0 + 16)
    return dc, col_ok


def _bias_blocks():
    out = []
    for typ, delta in enumerate((4, 0, -4)):
        for rq in range(4):
            for rkk in range(12):
                dr = rkk + delta - rq - 4
                if typ == 0:
                    ok = -rq <= dr <= 7 - rq
                elif typ == 1:
                    ok = -4 <= dr <= 3
                else:
                    ok = -4 - rq <= dr <= 3 - rq
                out.append((typ, rq, rkk, dr if ok else None))
    return out


def _bias_call(rpb_flat):
    nh = rpb_flat.shape[0]

    def body(r_ref, bias_ref, et_ref):
        dc, col_ok = _dc_masks()
        masks = [(dc == j).astype(F32) for j in range(31)]

        def per_h(h, carry):
            for dr in range(15):
                t = jnp.zeros((GRID_W, GRID_W), F32)
                for j in range(31):
                    t = t + masks[j] * r_ref[h, dr * 31 + j]
                et_ref[dr] = jnp.where(col_ok, t, NEG)
            neg = jnp.full((GRID_W, GRID_W), NEG, F32)
            for typ, rq, rkk, dr in _bias_blocks():
                blk = neg if dr is None else et_ref[dr + 7]
                bias_ref[h, typ, rq * 64:(rq + 1) * 64, rkk * 64:(rkk + 1) * 64] = blk
            return carry

        lax.fori_loop(0, nh, per_h, 0)

    return pl.pallas_call(
        body, name="rpb_bias",
        out_shape=jax.ShapeDtypeStruct((nh, 3, TQ, KW), F32),
        in_specs=[pl.BlockSpec(memory_space=pltpu.SMEM)],
        out_specs=pl.BlockSpec(memory_space=pltpu.VMEM),
        scratch_shapes=[pltpu.VMEM((15, GRID_W, GRID_W), F32)],
        compiler_params=_params())(rpb_flat)


def _small_reduce_call(dbias, dlg, B):
    nh = dbias.shape[0]

    def body(db_ref, dlg_ref, drpb_ref, dlgo_ref, p_ref):
        dc, _ = _dc_masks()
        masks = [(dc == j).astype(F32) for j in range(31)]
        ones = jnp.ones((8, GRID_W), F32)
        p_ref[...] = jnp.zeros_like(p_ref)
        drpb_ref[...] = jnp.zeros_like(drpb_ref)

        def per_h(h, carry):
            acc = {}
            for typ, rq, rkk, dr in _bias_blocks():
                if dr is None:
                    continue
                blk = db_ref[h, typ, rq * 64:(rq + 1) * 64, rkk * 64:(rkk + 1) * 64]
                acc[dr] = blk if dr not in acc else acc[dr] + blk
            for dr in range(-7, 8):
                t = acc[dr]
                for j in range(31):
                    p_ref[j:j + 1, :] = jnp.sum(t * masks[j], axis=0, keepdims=True)
                red = lax.dot_general(ones, p_ref[...], (((1,), (1,)), ((), ())),
                                      precision=HIGHEST, preferred_element_type=F32)
                drpb_ref[h, dr + 7:dr + 8, :] = red[0:1, :]
            return carry

        lax.fori_loop(0, nh, per_h, 0)
        x = dlg_ref[0]
        for b in range(1, B):
            x = x + dlg_ref[b]
        x = x.reshape(4 * 8, TK)
        dlgo_ref[...] = jnp.dot(x, jnp.ones((TK, 128), F32), precision=HIGHEST,
                                preferred_element_type=F32)

    return pl.pallas_call(
        body, name="small_reduce",
        out_shape=(jax.ShapeDtypeStruct((nh, 16, 32), F32), jax.ShapeDtypeStruct((32, 128), F32)),
        scratch_shapes=[pltpu.VMEM((32, GRID_W), F32)],
        compiler_params=_params())(dbias, dlg)


def _inproj_call(x, ctx, mod, norm_g, win_f, cos2, sin2):
    B, L, _ = x.shape
    LC = ctx.shape[1]
    T = L + LC
    nl = L // TQ
    assert LC == TQ and L % TQ == 0
    kscale = RET_DK ** -0.5

    def body(x_ref, ctx_ref, mod_ref, g_ref, w_ref, cos_ref, sin_ref, p_ref, h_ref):
        b = pl.program_id(0)
        t = pl.program_id(1)
        is_lat = t < nl
        xt = jnp.where(is_lat, x_ref[...], ctx_ref[...])
        mrow = mod_ref[pl.ds(jnp.where(is_lat, b, B), 1), :]
        shift, scale = mrow[:, 0:D], mrow[:, D:2 * D]
        rstd = lax.rsqrt(jnp.mean(xt * xt, axis=-1, keepdims=True) + EPS)
        hb = ((xt * rstd * g_ref[...]) * (1.0 + scale) + shift).astype(BF16)
        h_ref[...] = hb
        cs, sn = cos_ref[...], sin_ref[...]
        for sec in range(8):
            s, half = divmod(sec, 2)
            acc = _dot(hb, w_ref[s, :, half * 512:(half + 1) * 512])
            if sec == 0:
                acc = acc * (NA_DH ** -0.5)
            if sec in (4, 5):
                for j in range(4):
                    a = acc[:, j * 128:(j + 1) * 128]
                    r = a * cs + pltpu.roll(a, 64, 1) * sn
                    if sec == 5:
                        r = r * kscale
                    p_ref[:, sec * 512 + j * 128:sec * 512 + (j + 1) * 128] = r.astype(BF16)
            else:
                p_ref[:, sec * 512:(sec + 1) * 512] = acc.astype(BF16)

    return pl.pallas_call(
        body, name="in_proj", grid=(B, T // TQ),
        in_specs=[
            pl.BlockSpec((None, TQ, D), lambda b, t: (b, jnp.minimum(t, nl - 1), 0)),
            pl.BlockSpec((None, TQ, D), lambda b, t: (b, 0, 0)),
            pl.BlockSpec((8, 3 * D), lambda b, t: (0, 0)),
            pl.BlockSpec((1, D), lambda b, t: (0, 0)),
            pl.BlockSpec((N_SHARD, D, D), lambda b, t: (0, 0, 0)),
            pl.BlockSpec((TQ, RET_DK), lambda b, t: (t, 0)),
            pl.BlockSpec((TQ, RET_DK), lambda b, t: (t, 0)),
        ],
        out_specs=(pl.BlockSpec((None, TQ, 4 * D), lambda b, t: (b, t, 0)),
                   pl.BlockSpec((None, TQ, D), lambda b, t: (b, t, 0))),
        out_shape=(jax.ShapeDtypeStruct((B, T, 4 * D), BF16), jax.ShapeDtypeStruct((B, T, D), BF16)),
        compiler_params=_params(("arbitrary", "arbitrary")))(x, ctx, mod, norm_g, win_f, cos2, sin2)


def _na_specs(L, T, rows):
    nm = rows // 4
    q_spec = pl.BlockSpec((None, TQ, 128), lambda hp, b, m: (b, m, hp))
    k_spec = pl.BlockSpec((None, T, 128), lambda hp, b, m: (b, 0, 4 + hp))
    v_spec = pl.BlockSpec((None, T, 128), lambda hp, b, m: (b, 0, 8 + hp))
    g_spec = pl.BlockSpec((None, TQ, 128), lambda hp, b, m: (b, m, 12 + hp))
    bias_spec = pl.BlockSpec((2, 3, TQ, KW), lambda hp, b, m: (hp, 0, 0, 0))
    return nm, q_spec, k_spec, v_spec, g_spec, bias_spec


def _na_tile(m, nm, rows):
    typ = jnp.where(m == 0, 0, jnp.where(m == nm - 1, 2, 1))
    start = pl.multiple_of(jnp.clip(4 * m - 4, 0, rows - 12) * GRID_W, TQ)
    return typ, start


def _na_fwd_call(P, bias, L, LC):
    B, T, _ = P.shape
    rows = L // GRID_W
    nm, q_spec, k_spec, v_spec, g_spec, bias_spec = _na_specs(L, T, rows)

    def body(q_ref, k_ref, v_ref, g_ref, bias_ref, y_ref):
        typ, start = _na_tile(pl.program_id(2), nm, rows)
        for hh in range(2):
            ln = slice(hh * NA_DH, (hh + 1) * NA_DH)
            q = q_ref[:, ln]
            kw, vw = k_ref[pl.ds(start, KW), ln], v_ref[pl.ds(start, KW), ln]
            kc, vc = k_ref[L:L + LC, ln], v_ref[L:L + LC, ln]
            s1 = _dot_nt(q, kw) + bias_ref[hh, typ]
            s2 = _dot_nt(q, kc)
            mx = jnp.maximum(jnp.max(s1, axis=-1, keepdims=True), jnp.max(s2, axis=-1, keepdims=True))
            p1, p2 = jnp.exp(s1 - mx), jnp.exp(s2 - mx)
            inv = 1.0 / (jnp.sum(p1, axis=-1, keepdims=True) + jnp.sum(p2, axis=-1, keepdims=True))
            o = _dot((p1 * inv).astype(BF16), vw) + _dot((p2 * inv).astype(BF16), vc)
            g = g_ref[:, ln].astype(F32)
            y_ref[:, ln] = (o * (g * _sigmoid(g))).astype(BF16)

    return pl.pallas_call(
        body, name="na_fwd", grid=(4, B, nm),
        in_specs=[q_spec, k_spec, v_spec, g_spec, bias_spec],
        out_specs=pl.BlockSpec((None, TQ, 128), lambda hp, b, m: (b, m, hp)),
        out_shape=jax.ShapeDtypeStruct((B, L, 512), BF16),
        compiler_params=_params(("arbitrary",) * 3))(P, P, P, P, bias)


def _na_bwd_call(P, bias, dY, L, LC):
    B, T, _ = P.shape
    rows = L // GRID_W
    nm, q_spec, k_spec, v_spec, g_spec, bias_spec = _na_specs(L, T, rows)
    scale = NA_DH ** -0.5

    def body(q_ref, k_ref, v_ref, g_ref, bias_ref, dy_ref, dq_ref, dg_ref, dk_ref, dv_ref, db_ref):
        b, m = pl.program_id(1), pl.program_id(2)
        typ, start = _na_tile(m, nm, rows)

        @pl.when(m == 0)
        def _():
            dk_ref[...] = jnp.zeros_like(dk_ref)
            dv_ref[...] = jnp.zeros_like(dv_ref)

        @pl.when((m == 0) & (b == 0))
        def _():
            db_ref[...] = jnp.zeros_like(db_ref)

        for hh in range(2):
            ln = slice(hh * NA_DH, (hh + 1) * NA_DH)
            q = q_ref[:, ln]
            kw, vw = k_ref[pl.ds(start, KW), ln], v_ref[pl.ds(start, KW), ln]
            kc, vc = k_ref[L:L + LC, ln], v_ref[L:L + LC, ln]
            s1 = _dot_nt(q, kw) + bias_ref[hh, typ]
            s2 = _dot_nt(q, kc)
            mx = jnp.maximum(jnp.max(s1, axis=-1, keepdims=True), jnp.max(s2, axis=-1, keepdims=True))
            p1, p2 = jnp.exp(s1 - mx), jnp.exp(s2 - mx)
            inv = 1.0 / (jnp.sum(p1, axis=-1, keepdims=True) + jnp.sum(p2, axis=-1, keepdims=True))
            p1, p2 = p1 * inv, p2 * inv
            p1b, p2b = p1.astype(BF16), p2.astype(BF16)
            o = _dot(p1b, vw) + _dot(p2b, vc)
            g = g_ref[:, ln].astype(F32)
            sg = _sigmoid(g)
            dy = dy_ref[:, ln].astype(F32)
            dg_ref[:, ln] = (dy * o * (sg * (1.0 + g * (1.0 - sg)))).astype(BF16)
            do = (dy * (g * sg)).astype(BF16)
            dp1, dp2 = _dot_nt(do, vw), _dot_nt(do, vc)
            delta = jnp.sum(p1 * dp1, axis=-1, keepdims=True) + jnp.sum(p2 * dp2, axis=-1, keepdims=True)
            ds1, ds2 = p1 * (dp1 - delta), p2 * (dp2 - delta)
            db_ref[hh, typ] += ds1
            ds1b, ds2b = ds1.astype(BF16), ds2.astype(BF16)
            dq_ref[:, ln] = ((_dot(ds1b, kw) + _dot(ds2b, kc)) * scale).astype(BF16)
            dk_ref[pl.ds(start, KW), ln] += _dot_tn(ds1b, q)
            dv_ref[pl.ds(start, KW), ln] += _dot_tn(p1b, do)
            dk_ref[L:L + LC, ln] += _dot_tn(ds2b, q)
            dv_ref[L:L + LC, ln] += _dot_tn(p2b, do)

    tile = pl.BlockSpec((None, TQ, 128), lambda hp, b, m: (b, m, hp))
    kv_out = pl.BlockSpec((None, T, 128), lambda hp, b, m: (b, 0, hp))
    return pl.pallas_call(
        body, name="na_bwd", grid=(4, B, nm),
        in_specs=[q_spec, k_spec, v_spec, g_spec, bias_spec, tile],
        out_specs=(tile, tile, kv_out, kv_out, bias_spec),
        out_shape=(jax.ShapeDtypeStruct((B, L, 512), BF16), jax.ShapeDtypeStruct((B, L, 512), BF16),
                   jax.ShapeDtypeStruct((B, T, 512), F32), jax.ShapeDtypeStruct((B, T, 512), F32),
                   jax.ShapeDtypeStruct(bias.shape, F32)),
        compiler_params=_params(("arbitrary",) * 3))(P, P, P, P, bias, dY)


def _head_scalar(dec_ref, h):
    lane = lax.broadcasted_iota(jnp.int32, dec_ref.shape, 1)
    return -jnp.sum(jnp.where(lane == h, jnp.exp(dec_ref[...]), 0.0), axis=1, keepdims=True)


def _decay_lat(tpos, ks, lgf, lgb):
    spos = (ks + lax.broadcasted_iota(jnp.int32, (1, TK), 1)).astype(F32)
    dist = tpos - spos
    dm = jnp.exp(dist * jnp.where(dist > 0, lgf, -lgb)) * jnp.where(dist == 0, 2.0, 1.0)
    return dist, dm


def _decay_ctx(tpos, L, LC, lgf, lgb):
    jc = lax.broadcasted_iota(jnp.int32, (1, LC), 1).astype(F32)
    df = tpos + (float(LC) - jc)
    db = (float(L) - tpos) + jc
    return df, db, jnp.exp(lgf * df), jnp.exp(lgb * db)


def _ret_specs(T):
    q_spec = pl.BlockSpec((None, TQ, 128), lambda b, h, i: (b, i, 16 + h))
    k_spec = pl.BlockSpec((None, T, 128), lambda b, h, i: (b, 0, 20 + h))
    v_spec = pl.BlockSpec((None, T, 128), lambda b, h, i: (b, 0, 24 + h))
    g_spec = pl.BlockSpec((None, TQ, 128), lambda b, h, i: (b, i, 28 + h))
    dec_spec = pl.BlockSpec((1, 4), lambda b, h, i: (0, 0))
    gn_spec = pl.BlockSpec((1, 128), lambda b, h, i: (0, h))
    return q_spec, k_spec, v_spec, g_spec, dec_spec, gn_spec


def _ret_fwd_call(P, dec_f, dec_b, ret_norm_g, L, LC):
    B, T, _ = P.shape
    q_spec, k_spec, v_spec, g_spec, dec_spec, gn_spec = _ret_specs(T)

    def body(df_ref, db_ref, q_ref, k_ref, v_ref, g_ref, gn_ref, y_ref, o_ref):
        h, i = pl.program_id(1), pl.program_id(2)
        lgf, lgb = _head_scalar(df_ref, h), _head_scalar(db_ref, h)
        tpos = (i * TQ + lax.broadcasted_iota(jnp.int32, (TQ, 1), 0)).astype(F32)
        q = q_ref[...]

        def chunk(j, acc):
            ks = pl.multiple_of(j * TK, TK)
            kj, vj = k_ref[pl.ds(ks, TK), :], v_ref[pl.ds(ks, TK), :]
            _, dm = _decay_lat(tpos, ks, lgf, lgb)
            return acc + _dot((_dot_nt(q, kj) * dm).astype(BF16), vj)

        acc = lax.fori_loop(0, L // TK, chunk, jnp.zeros((TQ, RET_DK), F32))
        _, _, ef, eb = _decay_ctx(tpos, L, LC, lgf, lgb)
        acc = acc + _dot((_dot_nt(q, k_ref[L:L + LC, :]) * (ef + eb)).astype(BF16), v_ref[L:L + LC, :])
        o_ref[...] = acc
        rn = lax.rsqrt(jnp.mean(acc * acc, axis=-1, keepdims=True) + EPS)
        g = g_ref[...].astype(F32)
        y_ref[...] = ((acc * rn * gn_ref[...]).astype(F32) * (g * _sigmoid(g))).astype(BF16)

    tile = pl.BlockSpec((None, TQ, 128), lambda b, h, i: (b, i, h))
    return pl.pallas_call(
        body, name="ret_fwd", grid=(B, 4, L // TQ),
        in_specs=[dec_spec, dec_spec, q_spec, k_spec, v_spec, g_spec, gn_spec],
        out_specs=(tile, tile),
        out_shape=(jax.ShapeDtypeStruct((B, L, 512), BF16), jax.ShapeDtypeStruct((B, L, 512), F32)),
        compiler_params=_params(("arbitrary",) * 3))(dec_f, dec_b, P, P, P, P, ret_norm_g)


def _ret_bwd_call(P, dec_f, dec_b, ret_norm_g, o_ret, dY, cos2, sin2, L, LC):
    B, T, _ = P.shape
    ni = L // TQ
    kscale = RET_DK ** -0.5
    q_spec, k_spec, v_spec, g_spec, dec_spec, gn_spec = _ret_specs(T)

    def body(df_ref, db_ref, q_ref, k_ref, v_ref, g_ref, gn_ref, o_ref, dy_ref, cos_ref, sin_ref,
             dq_ref, dg_ref, dk_ref, dv_ref, dgn_ref, dlg_ref):
        h, i = pl.program_id(1), pl.program_id(2)
        lgf, lgb = _head_scalar(df_ref, h), _head_scalar(db_ref, h)
        tpos = (i * TQ + lax.broadcasted_iota(jnp.int32, (TQ, 1), 0)).astype(F32)

        @pl.when(i == 0)
        def _():
            dk_ref[...] = jnp.zeros_like(dk_ref)
            dv_ref[...] = jnp.zeros_like(dv_ref)
            dgn_ref[...] = jnp.zeros_like(dgn_ref)
            dlg_ref[...] = jnp.zeros_like(dlg_ref)

        q = q_ref[...]
        o = o_ref[...]
        g = g_ref[...].astype(F32)
        dy = dy_ref[...].astype(F32)
        gn = gn_ref[...]
        sg = _sigmoid(g)
        rn = lax.rsqrt(jnp.mean(o * o, axis=-1, keepdims=True) + EPS)
        nrm = o * rn
        dg_ref[...] = (dy * (nrm * gn) * (sg * (1.0 + g * (1.0 - sg)))).astype(BF16)
        dhn = dy * (g * sg)
        dgn_ref[...] += jnp.sum(dhn * nrm, axis=0, keepdims=True)
        dnrm = dhn * gn
        do = rn * (dnrm - nrm * jnp.mean(dnrm * nrm, axis=-1, keepdims=True))
        dob = do.astype(BF16)

        def chunk(j, dq):
            ks = pl.multiple_of(j * TK, TK)
            kj, vj = k_ref[pl.ds(ks, TK), :], v_ref[pl.ds(ks, TK), :]
            dist, dm = _decay_lat(tpos, ks, lgf, lgb)
            s = _dot_nt(q, kj)
            dsv = _dot_nt(dob, vj)
            dsb = (dsv * dm).astype(BF16)
            dk_ref[pl.ds(ks, TK), :] += _dot_tn(dsb, q)
            dv_ref[pl.ds(ks, TK), :] += _dot_tn((s * dm).astype(BF16), dob)
            xw = s * dsv * dm * jnp.abs(dist)
            tot = jnp.sum(xw, axis=0, keepdims=True)
            fwd = jnp.sum(jnp.where(dist > 0, xw, 0.0), axis=0, keepdims=True)
            dlg_ref[0:1, :] += fwd
            dlg_ref[1:2, :] += tot - fwd
            return dq + _dot(dsb, kj)

        dq = lax.fori_loop(0, L // TK, chunk, jnp.zeros((TQ, RET_DK), F32))
        kc, vc = k_ref[L:L + LC, :], v_ref[L:L + LC, :]
        dfc, dbc, ef, eb = _decay_ctx(tpos, L, LC, lgf, lgb)
        s = _dot_nt(q, kc)
        dsv = _dot_nt(dob, vc)
        dsb = (dsv * (ef + eb)).astype(BF16)
        dk_ref[L:L + LC, :] += _dot_tn(dsb, q)
        dv_ref[L:L + LC, :] += _dot_tn((s * (ef + eb)).astype(BF16), dob)
        a = s * dsv
        dlg_ref[0:1, 0:LC] += jnp.sum(a * ef * dfc, axis=0, keepdims=True)
        dlg_ref[1:2, 0:LC] += jnp.sum(a * eb * dbc, axis=0, keepdims=True)
        dq = dq + _dot(dsb, kc)
        cs, sn = cos_ref[pl.ds(pl.multiple_of(i * TQ, TQ), TQ), :], sin_ref[pl.ds(pl.multiple_of(i * TQ, TQ), TQ), :]
        dq_ref[...] = (dq * cs - pltpu.roll(dq, 64, 1) * sn).astype(BF16)

        @pl.when(i == ni - 1)
        def _():
            dk = dk_ref[...]
            dk_ref[...] = (dk * cos_ref[...] - pltpu.roll(dk, 64, 1) * sin_ref[...]) * kscale

    tile = pl.BlockSpec((None, TQ, 128), lambda b, h, i: (b, i, h))
    kv_out = pl.BlockSpec((None, T, 128), lambda b, h, i: (b, 0, h))
    tab = pl.BlockSpec((T, RET_DK), lambda b, h, i: (0, 0))
    return pl.pallas_call(
        body, name="ret_bwd", grid=(B, 4, ni),
        in_specs=[dec_spec, dec_spec, q_spec, k_spec, v_spec, g_spec, gn_spec, tile,
                  pl.BlockSpec((None, TQ, 128), lambda b, h, i: (b, i, 4 + h)), tab, tab],
        out_specs=(tile, tile, kv_out, kv_out,
                   pl.BlockSpec((None, 1, 128), lambda b, h, i: (b, 0, h)),
                   pl.BlockSpec((None, None, 8, TK), lambda b, h, i: (b, h, 0, 0))),
        out_shape=(jax.ShapeDtypeStruct((B, L, 512), BF16), jax.ShapeDtypeStruct((B, L, 512), BF16),
                   jax.ShapeDtypeStruct((B, T, 512), F32), jax.ShapeDtypeStruct((B, T, 512), F32),
                   jax.ShapeDtypeStruct((B, 1, 512), F32), jax.ShapeDtypeStruct((B, 4, 8, TK), F32)),
        compiler_params=_params(("arbitrary",) * 3))(
            dec_f, dec_b, P, P, P, P, ret_norm_g, o_ret, dY, cos2, sin2)


def _out_call(y_na, y_ret, x, target, mod, final_g, wout_f):
    B, L, _ = x.shape

    def body(yn_ref, yr_ref, x_ref, t_ref, mod_ref, gf_ref, w_ref, dy_ref, dx2_ref, dw_ref, sm_ref):
        b, i = pl.program_id(0), pl.program_id(1)

        @pl.when((b == 0) & (i == 0))
        def _():
            dw_ref[...] = jnp.zeros_like(dw_ref)
            sm_ref[...] = jnp.zeros_like(sm_ref)

        gate = mod_ref[pl.ds(b, 1), 2 * D:3 * D]
        gf = gf_ref[...]
        yn, yr = yn_ref[...], yr_ref[...]
        ylat = _dot(yn, w_ref[0:512, :]) + _dot(yr, w_ref[512:1024, :])
        x2 = x_ref[...] + gate * ylat
        r = lax.rsqrt(jnp.mean(x2 * x2, axis=-1, keepdims=True) + EPS)
        xr = x2 * r
        err = xr * gf - t_ref[...]
        sm_ref[1:2, :] += jnp.sum(err * err, axis=0, keepdims=True)
        dout = err * (1.0 / D)
        sm_ref[0:1, :] += jnp.sum(dout * xr, axis=0, keepdims=True)
        gd = dout * gf
        dx2 = r * (gd - xr * jnp.mean(gd * xr, axis=-1, keepdims=True))
        dx2_ref[...] = dx2
        sm_ref[pl.ds(2 + b, 1), :] += jnp.sum(dx2 * ylat, axis=0, keepdims=True)
        dyl = (gate * dx2).astype(BF16)
        dy_ref[:, 0:512] = _dot_nt(dyl, w_ref[0:512, :]).astype(BF16)
        dy_ref[:, 512:1024] = _dot_nt(dyl, w_ref[512:1024, :]).astype(BF16)
        dw_ref[0:512, :] += _dot_tn(yn, dyl)
        dw_ref[512:1024, :] += _dot_tn(yr, dyl)

    half = pl.BlockSpec((None, TQ, 512), lambda b, i: (b, i, 0))
    full = pl.BlockSpec((None, TQ, D), lambda b, i: (b, i, 0))
    return pl.pallas_call(
        body, name="out_proj_loss", grid=(B, L // TQ),
        in_specs=[half, half, full, full,
                  pl.BlockSpec((8, 3 * D), lambda b, i: (0, 0)),
                  pl.BlockSpec((1, D), lambda b, i: (0, 0)),
                  pl.BlockSpec((D, D), lambda b, i: (0, 0))],
        out_specs=(full, full, pl.BlockSpec((D, D), lambda b, i: (0, 0)),
                   pl.BlockSpec((8, D), lambda b, i: (0, 0))),
        out_shape=(jax.ShapeDtypeStruct((B, L, D), BF16), jax.ShapeDtypeStruct((B, L, D), F32),
                   jax.ShapeDtypeStruct((D, D), F32), jax.ShapeDtypeStruct((8, D), F32)),
        compiler_params=_params(("arbitrary",) * 2))(y_na, y_ret, x, target, mod, final_g, wout_f)


def _dh_call(dsec, win_f, x, ctx, dx2, mod, norm_g):
    B, L, _ = x.shape
    LC = ctx.shape[1]
    nl = L // TQ

    def body(d0, d1, d2, d3, d4, d5, d6, d7, w_ref, x_ref, ctx_ref, dx2_ref, mod_ref, g_ref,
             gx_ref, sm_ref):
        drefs = (d0, d1, d2, d3, d4, d5, d6, d7)
        b, t = pl.program_id(0), pl.program_id(1)
        is_lat = t < nl

        @pl.when((b == 0) & (t == 0))
        def _():
            sm_ref[...] = jnp.zeros_like(sm_ref)

        def dh_of(secs):
            acc = jnp.zeros((TQ, D), F32)
            for sec in secs:
                s, half = divmod(sec, 2)
                acc = acc + _dot_nt(drefs[sec][...].astype(BF16), w_ref[s, :, half * 512:(half + 1) * 512])
            return acc

        def norm_bwd(dh, xt, mrow):
            scale = mrow[:, D:2 * D]
            g = g_ref[...]
            rstd = lax.rsqrt(jnp.mean(xt * xt, axis=-1, keepdims=True) + EPS)
            xn = xt * rstd
            dshift = jnp.sum(dh, axis=0, keepdims=True)
            dscale = jnp.sum(dh * (xn * g), axis=0, keepdims=True)
            dhn = dh * (1.0 + scale)
            sm_ref[0:1, :] += jnp.sum(dhn * xn, axis=0, keepdims=True)
            dxn = dhn * g
            dx = rstd * (dxn - xn * jnp.mean(dxn * xn, axis=-1, keepdims=True))
            return dshift, dscale, dx

        @pl.when(is_lat)
        def _():
            dshift, dscale, dx = norm_bwd(dh_of(range(8)), x_ref[...], mod_ref[pl.ds(b, 1), :])
            sm_ref[pl.ds(3 + b, 1), :] += dshift
            sm_ref[pl.ds(3 + B + b, 1), :] += dscale
            gx_ref[...] = dx2_ref[...] + dx

        @pl.when(jnp.logical_not(is_lat))
        def _():
            dshift, dscale, _ = norm_bwd(dh_of((1, 2, 5, 6)), ctx_ref[...], mod_ref[B:B + 1, :])
            sm_ref[1:2, :] += dshift
            sm_ref[2:3, :] += dscale

    lat = lambda b, t: (b, jnp.minimum(t, nl - 1), 0)
    tok = lambda b, t: (b, t, 0)
    sec_specs = [pl.BlockSpec((None, TQ, 512), lat if sec in (0, 3, 4, 7) else tok) for sec in range(8)]
    return pl.pallas_call(
        body, name="dh_norm_bwd", grid=(B, nl + 1),
        in_specs=sec_specs + [
            pl.BlockSpec((N_SHARD, D, D), lambda b, t: (0, 0, 0)),
            pl.BlockSpec((None, TQ, D), lat),
            pl.BlockSpec((None, LC, D), lambda b, t: (b, 0, 0)),
            pl.BlockSpec((None, TQ, D), lat),
            pl.BlockSpec((8, 3 * D), lambda b, t: (0, 0)),
            pl.BlockSpec((1, D), lambda b, t: (0, 0))],
        out_specs=(pl.BlockSpec((None, TQ, D), lat), pl.BlockSpec((8, D), lambda b, t: (0, 0))),
        out_shape=(jax.ShapeDtypeStruct((B, L, D), F32), jax.ShapeDtypeStruct((8, D), F32)),
        compiler_params=_params(("arbitrary",) * 2))(*dsec, win_f, x, ctx, dx2, mod, norm_g)


def _dw_call(dsec, h, L):
    B, T, _ = h.shape
    nl = L // TQ

    def body(d0, d1, d2, d3, d4, d5, d6, d7, h_ref, dw_ref):
        drefs = (d0, d1, d2, d3, d4, d5, d6, d7)
        b, t = pl.program_id(0), pl.program_id(1)

        @pl.when((b == 0) & (t == 0))
        def _():
            dw_ref[...] = jnp.zeros_like(dw_ref)

        hb = h_ref[...]

        def add(secs):
            for sec in secs:
                s, half = divmod(sec, 2)
                dw_ref[s, :, half * 512:(half + 1) * 512] += _dot_tn(hb, drefs[sec][...].astype(BF16))

        @pl.when(t < nl)
        def _():
            add(range(8))

        @pl.when(t >= nl)
        def _():
            add((1, 2, 5, 6))

    lat = lambda b, t: (b, jnp.minimum(t, nl - 1), 0)
    tok = lambda b, t: (b, t, 0)
    sec_specs = [pl.BlockSpec((None, TQ, 512), lat if sec in (0, 3, 4, 7) else tok) for sec in range(8)]
    return pl.pallas_call(
        body, name="dw_in", grid=(B, nl + 1),
        in_specs=sec_specs + [pl.BlockSpec((None, TQ, D), tok)],
        out_specs=pl.BlockSpec((N_SHARD, D, D), lambda b, t: (0, 0, 0)),
        out_shape=jax.ShapeDtypeStruct((N_SHARD, D, D), F32),
        compiler_params=_params(("arbitrary",) * 2, vmem_mb=56))(*dsec, h)


def _mesh_pos():
    return lax.axis_index("x"), lax.axis_index("y"), lax.axis_index("c")


def _flip(v, f):
    return 1 - v if f else v


def _remote(src, dst, ssem, rsem, k, peer):
    return pltpu.make_async_remote_copy(src_ref=src, dst_ref=dst, send_sem=ssem.at[k], recv_sem=rsem.at[k],
                                        device_id=peer, device_id_type=MESH)


def _run_copies(local, sends, recvs):
    for cp in local + sends:
        cp.start()
    for cp in recvs:
        cp.wait_recv()
    for cp in sends:
        cp.wait_send()
    for cp in local:
        cp.wait()


def _gather_call(win_b, wout_b, wada_b, c):
    arrs = (win_b, wout_b, wada_b)

    def body(win, wout, wada, c_ref, win_f, wout_f, wada_f, c_all, ssem, rsem, lsem):
        x, y, cc = _mesh_pos()
        s, me = 2 * x + y, 4 * x + 2 * y + cc
        srcs, dsts = (win, wout, wada), (win_f, wout_f, wada_f)
        local = [pltpu.make_async_copy(srcs[a], dsts[a].at[s], lsem.at[a]) for a in range(3)]
        local.append(pltpu.make_async_copy(c_ref, c_all.at[me], lsem.at[3]))
        sends, recvs, k = [], [], 0
        for fx, fy in ((1, 0), (0, 1), (1, 1)):
            px, py = _flip(x, fx), _flip(y, fy)
            for a in range(3):
                sends.append(_remote(srcs[a], dsts[a].at[s], ssem, rsem, k, (px, py, cc)))
                recvs.append(_remote(srcs[a], dsts[a].at[2 * px + py], ssem, rsem, k, (px, py, cc)))
                k += 1
        for f in range(1, 8):
            px, py, pc = _flip(x, f & 4), _flip(y, f & 2), _flip(cc, f & 1)
            sends.append(_remote(c_ref, c_all.at[me], ssem, rsem, k, (px, py, pc)))
            recvs.append(_remote(c_ref, c_all.at[4 * px + 2 * py + pc], ssem, rsem, k, (px, py, pc)))
            k += 1
        _run_copies(local, sends, recvs)

    return pl.pallas_call(
        body, name="weight_gather",
        in_specs=[ANY] * 4, out_specs=(ANY,) * 4,
        out_shape=tuple(jax.ShapeDtypeStruct((N_SHARD,) + a.shape, a.dtype) for a in arrs)
        + (jax.ShapeDtypeStruct((N_DEV,) + c.shape, c.dtype),),
        scratch_shapes=[pltpu.SemaphoreType.DMA((16,)), pltpu.SemaphoreType.DMA((16,)),
                        pltpu.SemaphoreType.DMA((4,))],
        )(win_b, wout_b, wada_b, c)


def _scatter_call(dwin_p, dwout_p, small):
    def body(dwin, dwout, sm, rin, rout, sm_all, ssem, rsem, lsem):
        x, y, cc = _mesh_pos()
        s, me = 2 * x + y, 4 * x + 2 * y + cc
        srcs, dsts = (dwin, dwout), (rin, rout)
        local = [pltpu.make_async_copy(srcs[a].at[s], dsts[a].at[s], lsem.at[a]) for a in range(2)]
        local.append(pltpu.make_async_copy(sm, sm_all.at[me], lsem.at[2]))
        sends, recvs, k = [], [], 0
        for fx, fy in ((1, 0), (0, 1), (1, 1)):
            px, py = _flip(x, fx), _flip(y, fy)
            ps = 2 * px + py
            for a in range(2):
                sends.append(_remote(srcs[a].at[ps], dsts[a].at[s], ssem, rsem, k, (px, py, cc)))
                recvs.append(_remote(srcs[a].at[s], dsts[a].at[ps], ssem, rsem, k, (px, py, cc)))
                k += 1
        for f in range(1, 8):
            px, py, pc = _flip(x, f & 4), _flip(y, f & 2), _flip(cc, f & 1)
            sends.append(_remote(sm, sm_all.at[me], ssem, rsem, k, (px, py, pc)))
            recvs.append(_remote(sm, sm_all.at[4 * px + 2 * py + pc], ssem, rsem, k, (px, py, pc)))
            k += 1
        _run_copies(local, sends, recvs)

    return pl.pallas_call(
        body, name="grad_scatter",
        in_specs=[ANY] * 3, out_specs=(ANY,) * 3,
        out_shape=(jax.ShapeDtypeStruct(dwin_p.shape, F32), jax.ShapeDtypeStruct(dwout_p.shape, F32),
                   jax.ShapeDtypeStruct((N_DEV,) + small.shape, F32)),
        scratch_shapes=[pltpu.SemaphoreType.DMA((13,)), pltpu.SemaphoreType.DMA((13,)),
                        pltpu.SemaphoreType.DMA((3,))],
        )(dwin_p, dwout_p, small)


def _sibling_call(hin, hout):
    def body(hin_ref, hout_ref, gin, gout, ssem, rsem, lsem):
        x, y, cc = _mesh_pos()
        peer = (x, y, 1 - cc)
        srcs, dsts = (hin_ref, hout_ref), (gin, gout)
        local = [pltpu.make_async_copy(srcs[a], dsts[a].at[cc], lsem.at[a]) for a in range(2)]
        sends = [_remote(srcs[a], dsts[a].at[cc], ssem, rsem, a, peer) for a in range(2)]
        recvs = [_remote(srcs[a], dsts[a].at[1 - cc], ssem, rsem, a, peer) for a in range(2)]
        _run_copies(local, sends, recvs)

    return pl.pallas_call(
        body, name="grad_sibling",
        in_specs=[ANY] * 2, out_specs=(ANY,) * 2,
        out_shape=(jax.ShapeDtypeStruct((2,) + hin.shape, F32), jax.ShapeDtypeStruct((2,) + hout.shape, F32)),
        scratch_shapes=[pltpu.SemaphoreType.DMA((2,)), pltpu.SemaphoreType.DMA((2,)),
                        pltpu.SemaphoreType.DMA((2,))],
        )(hin, hout)


def _sum_slots_call(r, name):
    _, R, C = r.shape
    tr = 256

    def body(r_ref, o_ref):
        o_ref[...] = ((r_ref[0] + r_ref[1]) + r_ref[2]) + r_ref[3]

    return pl.pallas_call(
        body, name=name, grid=(R // tr,),
        in_specs=[pl.BlockSpec((N_SHARD, tr, C), lambda i: (0, i, 0))],
        out_specs=pl.BlockSpec((tr, C), lambda i: (i, 0)),
        out_shape=jax.ShapeDtypeStruct((R, C), F32),
        compiler_params=_params(("arbitrary",)))(r)


def _adamw(w, g, m, v):
    m = ADAM_B1 * m + (1.0 - ADAM_B1) * g
    v = ADAM_B2 * v + (1.0 - ADAM_B2) * (g * g)
    m_hat = m / (1.0 - ADAM_B1 ** ADAM_STEP)
    v_hat = v / (1.0 - ADAM_B2 ** ADAM_STEP)
    return -ADAM_LR * (m_hat / (jnp.sqrt(v_hat) + ADAM_EPS) + ADAM_WD * w), m, v


def _adam_pair_call(w, m, v, gpair, name):
    R, C = w.shape
    tr = 256

    def body(w_ref, m_ref, v_ref, g_ref, go_ref, d_ref, mo_ref, vo_ref):
        g = g_ref[0] + g_ref[1]
        go_ref[...] = g
        d_ref[...], mo_ref[...], vo_ref[...] = _adamw(w_ref[...], g, m_ref[...], v_ref[...])

    spec = pl.BlockSpec((tr, C), lambda i: (i, 0))
    return pl.pallas_call(
        body, name=name, grid=(R // tr,),
        in_specs=[spec, spec, spec, pl.BlockSpec((2, tr, C), lambda i: (0, i, 0))],
        out_specs=(spec,) * 4, out_shape=(jax.ShapeDtypeStruct((R, C), F32),) * 4,
        compiler_params=_params(("arbitrary",)))(w, m, v, gpair)


R_GF, R_NG, R_LOSS, R_RNG, R_LGF, R_LGB, R_SHIFT, R_SCALE, R_GATE, R_SHIFT_C, R_SCALE_C, R_RNG2, R_RPB = (
    0, 1, 2, 3, 4, 5, 6, 8, 10, 12, 13, 14, 16)
W_GF, W_NG, W_CCTX, W_RNG, W_DF, W_DB, W_BADA, W_RPB = 0, 1, 2, 3, 4, 5, 6, 9


def _small_final_call(sm_all, c_t, c_ctx, wada_f, wada, m_ada, v_ada, wsm, msm, vsm, B):
    ws = wada.shape[1]
    NB = N_DEV * B

    def body(sm_ref, ct_ref, cctx_ref, wf_ref, wa_ref, ma_ref, va_ref, w_ref, m_ref, v_ref,
             g_ref, d_ref, mo_ref, vo_ref, ga_ref, da_ref, mao_ref, vao_ref, loss_ref, dmod_ref):
        x, y, _ = _mesh_pos()
        s = 2 * x + y
        tot = sm_ref[0]
        for dv in range(1, N_DEV):
            tot = tot + sm_ref[dv]
        w = w_ref[...]
        for dv in range(N_DEV):
            for b in range(B):
                r = dv * B + b
                for part, row in enumerate((R_SHIFT, R_SCALE, R_GATE)):
                    dmod_ref[r:r + 1, part * D:(part + 1) * D] = sm_ref[dv, row + b:row + b + 1, :]
        dmod_ref[NB:NB + 1, 0:D] = tot[R_SHIFT_C:R_SHIFT_C + 1, :]
        dmod_ref[NB:NB + 1, D:2 * D] = tot[R_SCALE_C:R_SCALE_C + 1, :]
        dmod_ref[NB:NB + 1, 2 * D:3 * D] = jnp.zeros((1, D), F32)
        dmod_ref[NB + 1:, :] = jnp.zeros((dmod_ref.shape[0] - NB - 1, 3 * D), F32)
        dmod = dmod_ref[...]
        cc = cctx_ref[...]
        scc = _sigmoid(cc)
        ct = ct_ref[...]
        act_t = ct * _sigmoid(ct)
        dmc = dmod[NB:NB + 1, :].astype(BF16)
        dact = jnp.zeros((1, D), F32)
        for sh in range(N_SHARD):
            dact = dact + _dot_nt(dmc[:, sh * ws:(sh + 1) * ws], wf_ref[sh])
        g = jnp.zeros((16, D), F32)
        rows = lax.broadcasted_iota(jnp.int32, (16, D), 0)

        def put(g, row, val):
            return jnp.where(rows == row, val, g)

        g = put(g, W_GF, tot[R_GF:R_GF + 1, :])
        g = put(g, W_NG, tot[R_NG:R_NG + 1, :])
        g = put(g, W_CCTX, dact * (scc * (1.0 + cc * (1.0 - scc))))
        g = put(g, W_RNG, tot[R_RNG:R_RNG + 1, :] + tot[R_RNG2:R_RNG2 + 1, :])
        g = put(g, W_DF, tot[R_LGF:R_LGF + 1, :] * (-jnp.exp(w[W_DF:W_DF + 1, :])))
        g = put(g, W_DB, tot[R_LGB:R_LGB + 1, :] * (-jnp.exp(w[W_DB:W_DB + 1, :])))
        db = jnp.sum(dmod, axis=0, keepdims=True)
        for part in range(3):
            g = put(g, W_BADA + part, db[:, part * D:(part + 1) * D])
        for part in range(4):
            g = put(g, W_RPB + part, tot[R_RPB + part:R_RPB + part + 1, :])
        g_ref[...] = g
        d_ref[...], mo_ref[...], vo_ref[...] = _adamw(w, g, m_ref[...], v_ref[...])
        loss_ref[...] = jnp.broadcast_to(
            (0.5 / D) * jnp.sum(tot[R_LOSS:R_LOSS + 1, :], axis=1, keepdims=True), (8, 128))
        for sh in range(N_SHARD):
            @pl.when(s == sh)
            def _():
                ga = jnp.dot(act_t, dmod[:, sh * ws:(sh + 1) * ws], precision=HIGHEST,
                             preferred_element_type=F32)
                ga_ref[...] = ga
                da_ref[...], mao_ref[...], vao_ref[...] = _adamw(wa_ref[...], ga, ma_ref[...], va_ref[...])

    sh_small = jax.ShapeDtypeStruct((16, D), F32)
    sh_ada = jax.ShapeDtypeStruct(wada.shape, F32)
    return pl.pallas_call(
        body, name="small_final",
        out_shape=(sh_small,) * 4 + (sh_ada,) * 4 + (jax.ShapeDtypeStruct((8, 128), F32),),
        scratch_shapes=[pltpu.VMEM((NB + 8, 3 * D), F32)],
        compiler_params=_params(vmem_mb=56))(
            sm_all, c_t, c_ctx, wada_f, wada, m_ada, v_ada, wsm, msm, vsm)


def _local_step(x, c, ctx, c_ctx, norm_g, wada_f, b_ada, win_f, na_rpb, dec_f, dec_b, ret_norm_g,
                wout_f, final_g, target):
    B, L, _ = x.shape
    LC = ctx.shape[1]
    assert B == 2
    cos2, sin2 = _rope_tables(L, LC)
    c8 = jnp.concatenate([c, c_ctx[None, :], jnp.zeros((8 - B - 1, D), F32)], axis=0)
    mod = _mod_call(c8, wada_f, b_ada)
    bias = _bias_call(na_rpb.reshape(na_rpb.shape[0], -1))
    P, h = _inproj_call(x, ctx, mod, norm_g, win_f, cos2, sin2)
    y_na = _na_fwd_call(P, bias, L, LC)
    y_ret, o_ret = _ret_fwd_call(P, dec_f, dec_b, ret_norm_g, L, LC)
    dY, dx2, dwout_p, sm_out = _out_call(y_na, y_ret, x, target, mod, final_g, wout_f.reshape(D, D))
    dnq, dng, dnk, dnv, dbias = _na_bwd_call(P, bias, dY, L, LC)
    drq, drg, drk, drv, dgn, dlg = _ret_bwd_call(P, dec_f, dec_b, ret_norm_g, o_ret, dY, cos2, sin2, L, LC)
    dsec = (dnq, dnk, dnv, dng, drq, drk, drv, drg)
    grad_x, sm_dh = _dh_call(dsec, win_f, x, ctx, dx2, mod, norm_g)
    dwin_p = _dw_call(dsec, h, L)
    drpb, dlg_sum = _small_reduce_call(dbias, dlg, B)
    z = jnp.zeros((1, D), F32)
    pad = lambda v: jnp.pad(v.reshape(1, -1), ((0, 0), (0, D - v.size)))
    dlg_sum = dlg_sum.reshape(4, 8, 128)
    rpb_rows = jnp.pad(drpb[:, :15, :31].reshape(-1), (0, 4 * D - drpb.shape[0] * 465)).reshape(4, D)
    small = jnp.concatenate([
        sm_out[0:1], sm_dh[0:1], sm_out[1:2], pad(dgn[0]), pad(dlg_sum[:, 0, 0]), pad(dlg_sum[:, 1, 0]),
        sm_dh[3:5], sm_dh[5:7], sm_out[2:4], sm_dh[1:2], sm_dh[2:3], pad(dgn[1]), z, rpb_rows,
        jnp.zeros((SM_ROWS - 20, D), F32)], axis=0)
    return grad_x, dwin_p, dwout_p, small


def kernel(x, c, ctx, c_ctx, norm_g, w_ada, b_ada, w_in, na_rpb, ret_decay_fwd, ret_decay_bwd, ret_norm_g, w_out, final_norm_g, loss_target, m_c_ctx, m_norm_g, m_w_ada, m_b_ada, m_w_in, m_na_rpb, m_ret_decay_fwd, m_ret_decay_bwd, m_ret_norm_g, m_w_out, m_final_norm_g, v_c_ctx, v_norm_g, v_w_ada, v_b_ada, v_w_in, v_na_rpb, v_ret_decay_fwd, v_ret_decay_bwd, v_ret_norm_g, v_w_out, v_final_norm_g):
    B = x.shape[0]
    win_f, wout_f, wada_f, c_all = _gather_call(w_in[0].astype(BF16), w_out[0].astype(BF16),
                                                w_ada[0].astype(BF16), c)
    grad_x, dwin_p, dwout_p, small = _local_step(
        x, c, ctx, c_ctx, norm_g, wada_f, b_ada, win_f, na_rpb[0], ret_decay_fwd, ret_decay_bwd,
        ret_norm_g, wout_f, final_norm_g.reshape(1, D), loss_target)
    rin, rout, sm_all = _scatter_call(dwin_p, dwout_p.reshape(N_SHARD, D // N_SHARD, D), small)
    gin, gout = _sibling_call(_sum_slots_call(rin, "sum_dw_in"), _sum_slots_call(rout, "sum_dw_out"))
    g_win, d_win, nm_win, nv_win = _adam_pair_call(w_in[0], m_w_in[0], v_w_in[0], gin, "adam_w_in")
    g_wout, d_wout, nm_wout, nv_wout = _adam_pair_call(w_out[0], m_w_out[0], v_w_out[0], gout, "adam_w_out")

    def pack(gf, ng, cc, rng, df, db, bada, rpb):
        pad = lambda v: jnp.pad(v.reshape(1, -1), ((0, 0), (0, D - v.size)))
        return jnp.concatenate([
            gf.reshape(1, D), ng.reshape(1, D), cc.reshape(1, D), pad(rng), pad(df), pad(db),
            bada.reshape(3, D), jnp.pad(rpb.reshape(-1), (0, 4 * D - rpb.size)).reshape(4, D),
            jnp.zeros((3, D), F32)], axis=0)

    wsm = pack(final_norm_g, norm_g, c_ctx, ret_norm_g, ret_decay_fwd, ret_decay_bwd, b_ada, na_rpb)
    msm = pack(m_final_norm_g, m_norm_g, m_c_ctx, m_ret_norm_g, m_ret_decay_fwd, m_ret_decay_bwd, m_b_ada, m_na_rpb)
    vsm = pack(v_final_norm_g, v_norm_g, v_c_ctx, v_ret_norm_g, v_ret_decay_fwd, v_ret_decay_bwd, v_b_ada, v_na_rpb)
    c_t = jnp.concatenate([c_all.reshape(N_DEV * B, D), c_ctx.reshape(1, D), jnp.zeros((7, D), F32)], axis=0).T
    outs = _small_final_call(sm_all, c_t, c_ctx.reshape(1, D), wada_f,
                             w_ada[0], m_w_ada[0], v_w_ada[0], wsm, msm, vsm, B)
    smalls, adas, loss = outs[0:4], outs[4:8], outs[8][0, 0]

    def unpack(p):
        rw = ret_norm_g.shape[1]
        return dict(
            final_norm_g=p[W_GF], norm_g=p[W_NG:W_NG + 1], c_ctx=p[W_CCTX], ret_norm_g=p[W_RNG:W_RNG + 1, :rw],
            ret_decay_fwd=p[W_DF:W_DF + 1, :4], ret_decay_bwd=p[W_DB:W_DB + 1, :4],
            b_ada=p[W_BADA:W_BADA + 3].reshape(1, 3 * D),
            na_rpb=p[W_RPB:W_RPB + 4].reshape(-1)[:na_rpb.size].reshape(na_rpb.shape))

    res = []
    for p, ada, win_o, wout_o in zip(smalls, adas, (g_win, d_win, nm_win, nv_win),
                                     (g_wout, d_wout, nm_wout, nv_wout)):
        u = unpack(p)
        res.append([u["c_ctx"], u["norm_g"], ada[None], u["b_ada"], win_o[None], u["na_rpb"],
                    u["ret_decay_fwd"], u["ret_decay_bwd"], u["ret_norm_g"], wout_o[None], u["final_norm_g"]])
    return (loss, grad_x, *res[0], *res[1], *res[2], *res[3])
```

```python
import functools

import numpy as np
import jax
import jax.numpy as jnp
from jax import lax
from jax.experimental import pallas as pl
from jax.experimental.pallas import tpu as pltpu

F32 = jnp.float32
BF16 = jnp.bfloat16
HIGHEST = lax.Precision.HIGHEST

D = 1024
GRID_W = 64
NA_DH = 64
RET_DK = 128
ROPE_BASE = 10000.0
EPS = 1e-6
NEG = -1e30
TQ = 256
TK = 512
KW = 12 * GRID_W
N_SHARD = 4
N_DEV = 8
SM_ROWS = 24

ADAM_LR = 0.001
ADAM_B1 = 0.9
ADAM_B2 = 0.999
ADAM_EPS = 1e-08
ADAM_WD = 0.01
ADAM_STEP = 10

MESH = pl.DeviceIdType.MESH
ANY = pl.BlockSpec(memory_space=pl.ANY)


def _params(sem=None, vmem_mb=48):
    return pltpu.CompilerParams(dimension_semantics=sem, vmem_limit_bytes=vmem_mb << 20)


def _dot(a, b):
    return jnp.dot(a, b, preferred_element_type=F32)


def _dot_nt(a, b):
    return lax.dot_general(a, b, (((1,), (1,)), ((), ())), preferred_element_type=F32)


def _dot_tn(a, b):
    return lax.dot_general(a, b, (((0,), (0,)), ((), ())), preferred_element_type=F32)


def _sigmoid(x):
    return 1.0 / (1.0 + jnp.exp(-x))


def _rope_tables(L, LC):
    half = RET_DK // 2
    nf = half // 2
    t = np.arange(L)
    row = (t // GRID_W).astype(np.float32)
    col = (t % GRID_W).astype(np.float32)
    inv = (np.float32(ROPE_BASE) ** (-np.arange(nf, dtype=np.float32) / np.float32(nf))).astype(np.float32)
    ang = np.concatenate([row[:, None] * inv, col[:, None] * inv], axis=-1).astype(np.float32)
    cos, sin = np.cos(ang).astype(np.float32), np.sin(ang).astype(np.float32)
    cos2 = np.concatenate([cos, cos], axis=-1)
    sin2 = np.concatenate([-sin, sin], axis=-1)
    cos2 = np.concatenate([cos2, np.ones((LC, RET_DK), np.float32)], axis=0)
    sin2 = np.concatenate([sin2, np.zeros((LC, RET_DK), np.float32)], axis=0)
    return jnp.asarray(cos2), jnp.asarray(sin2)


def _mod_call(c8, wada_f, b_ada):
    ws = wada_f.shape[2]

    def body(c_ref, w_ref, b_ref, o_ref):
        a = c_ref[...]
        a = (a * _sigmoid(a)).astype(BF16)
        for s in range(N_SHARD):
            o_ref[:, s * ws:(s + 1) * ws] = _dot(a, w_ref[s]) + b_ref[:, s * ws:(s + 1) * ws]

    return pl.pallas_call(
        body, name="ada_mod", out_shape=jax.ShapeDtypeStruct((8, 3 * D), F32),
        compiler_params=_params())(c8, wada_f, b_ada)


def _dc_masks():
    cq = lax.broadcasted_iota(jnp.int32, (GRID_W, GRID_W), 0)
    ck = lax.broadcasted_iota(jnp.int32, (GRID_W, GRID_W), 1)
    dc = jnp.clip(ck - cq + 15, 0, 30)
    c0 = jnp.clip(cq - 8, 0, GRID_W - 16)
    col_ok = (ck >= c0) & (ck < c0 + 16)
    return dc, col_ok


def _bias_blocks():
    out = []
    for typ, delta in enumerate((4, 0, -4)):
        for rq in range(4):
            for rkk in range(12):
                dr = rkk + delta - rq - 4
                if typ == 0:
                    ok = -rq <= dr <= 7 - rq
                elif typ == 1:
                    ok = -4 <= dr <= 3
                else:
                    ok = -4 - rq <= dr <= 3 - rq
                out.append((typ, rq, rkk, dr if ok else None))
    return out


def _bias_call(rpb_flat):
    nh = rpb_flat.shape[0]

    def body(r_ref, bias_ref, et_ref):
        dc, col_ok = _dc_masks()
        masks = [(dc == j).astype(F32) for j in range(31)]

        def per_h(h, carry):
            for dr in range(15):
                t = jnp.zeros((GRID_W, GRID_W), F32)
                for j in range(31):
                    t = t + masks[j] * r_ref[h, dr * 31 + j]
                et_ref[dr] = jnp.where(col_ok, t, NEG)
            neg = jnp.full((GRID_W, GRID_W), NEG, F32)
            for typ, rq, rkk, dr in _bias_blocks():
                blk = neg if dr is None else et_ref[dr + 7]
                bias_ref[h, typ, rq * 64:(rq + 1) * 64, rkk * 64:(rkk + 1) * 64] = blk
            return carry

        lax.fori_loop(0, nh, per_h, 0)

    return pl.pallas_call(
        body, name="rpb_bias",
        out_shape=jax.ShapeDtypeStruct((nh, 3, TQ, KW), F32),
        in_specs=[pl.BlockSpec(memory_space=pltpu.SMEM)],
        out_specs=pl.BlockSpec(memory_space=pltpu.VMEM),
        scratch_shapes=[pltpu.VMEM((15, GRID_W, GRID_W), F32)],
        compiler_params=_params())(rpb_flat)


def _small_reduce_call(dbias, dlg, B):
    nh = dbias.shape[0]

    def body(db_ref, dlg_ref, drpb_ref, dlgo_ref, p_ref):
        dc, _ = _dc_masks()
        masks = [(dc == j).astype(F32) for j in range(31)]
        ones = jnp.ones((8, GRID_W), F32)
        p_ref[...] = jnp.zeros_like(p_ref)
        drpb_ref[...] = jnp.zeros_like(drpb_ref)

        def per_h(h, carry):
            acc = {}
            for typ, rq, rkk, dr in _bias_blocks():
                if dr is None:
                    continue
                blk = db_ref[h, typ, rq * 64:(rq + 1) * 64, rkk * 64:(rkk + 1) * 64]
                acc[dr] = blk if dr not in acc else acc[dr] + blk
            for dr in range(-7, 8):
                t = acc[dr]
                for j in range(31):
                    p_ref[j:j + 1, :] = jnp.sum(t * masks[j], axis=0, keepdims=True)
                red = lax.dot_general(ones, p_ref[...], (((1,), (1,)), ((), ())),
                                      precision=HIGHEST, preferred_element_type=F32)
                drpb_ref[h, dr + 7:dr + 8, :] = red[0:1, :]
            return carry

        lax.fori_loop(0, nh, per_h, 0)
        x = dlg_ref[0]
        for b in range(1, B):
            x = x + dlg_ref[b]
        x = x.reshape(4 * 8, TK)
        dlgo_ref[...] = jnp.dot(x, jnp.ones((TK, 128), F32), precision=HIGHEST,
                                preferred_element_type=F32)

    return pl.pallas_call(
        body, name="small_reduce",
        out_shape=(jax.ShapeDtypeStruct((nh, 16, 32), F32), jax.ShapeDtypeStruct((32, 128), F32)),
        scratch_shapes=[pltpu.VMEM((32, GRID_W), F32)],
        compiler_params=_params())(dbias, dlg)


def _inproj_call(x, ctx, mod, norm_g, win_f, cos2, sin2):
    B, L, _ = x.shape
    LC = ctx.shape[1]
    T = L + LC
    nl = L // TQ
    assert LC == TQ and L % TQ == 0
    kscale = RET_DK ** -0.5

    def body(x_ref, ctx_ref, mod_ref, g_ref, w_ref, cos_ref, sin_ref, p_ref, h_ref):
        b = pl.program_id(0)
        t = pl.program_id(1)
        is_lat = t < nl
        xt = jnp.where(is_lat, x_ref[...], ctx_ref[...])
        mrow = mod_ref[pl.ds(jnp.where(is_lat, b, B), 1), :]
        shift, scale = mrow[:, 0:D], mrow[:, D:2 * D]
        rstd = lax.rsqrt(jnp.mean(xt * xt, axis=-1, keepdims=True) + EPS)
        hb = ((xt * rstd * g_ref[...]) * (1.0 + scale) + shift).astype(BF16)
        h_ref[...] = hb
        cs, sn = cos_ref[...], sin_ref[...]
        for sec in range(8):
            s, half = divmod(sec, 2)
            acc = _dot(hb, w_ref[s, :, half * 512:(half + 1) * 512])
            if sec == 0:
                acc = acc * (NA_DH ** -0.5)
            if sec in (4, 5):
                for j in range(4):
                    a = acc[:, j * 128:(j + 1) * 128]
                    r = a * cs + pltpu.roll(a, 64, 1) * sn
                    if sec == 5:
                        r = r * kscale
                    p_ref[:, sec * 512 + j * 128:sec * 512 + (j + 1) * 128] = r.astype(BF16)
            else:
                p_ref[:, sec * 512:(sec + 1) * 512] = acc.astype(BF16)

    return pl.pallas_call(
        body, name="in_proj", grid=(B, T // TQ),
        in_specs=[
            pl.BlockSpec((None, TQ, D), lambda b, t: (b, jnp.minimum(t, nl - 1), 0)),
            pl.BlockSpec((None, TQ, D), lambda b, t: (b, 0, 0)),
            pl.BlockSpec((8, 3 * D), lambda b, t: (0, 0)),
            pl.BlockSpec((1, D), lambda b, t: (0, 0)),
            pl.BlockSpec((N_SHARD, D, D), lambda b, t: (0, 0, 0)),
            pl.BlockSpec((TQ, RET_DK), lambda b, t: (t, 0)),
            pl.BlockSpec((TQ, RET_DK), lambda b, t: (t, 0)),
        ],
        out_specs=(pl.BlockSpec((None, TQ, 4 * D), lambda b, t: (b, t, 0)),
                   pl.BlockSpec((None, TQ, D), lambda b, t: (b, t, 0))),
        out_shape=(jax.ShapeDtypeStruct((B, T, 4 * D), BF16), jax.ShapeDtypeStruct((B, T, D), BF16)),
        compiler_params=_params(("arbitrary", "arbitrary")))(x, ctx, mod, norm_g, win_f, cos2, sin2)


def _na_specs(L, T, rows):
    nm = rows // 4
    q_spec = pl.BlockSpec((None, TQ, 128), lambda hp, b, m: (b, m, hp))
    k_spec = pl.BlockSpec((None, T, 128), lambda hp, b, m: (b, 0, 4 + hp))
    v_spec = pl.BlockSpec((None, T, 128), lambda hp, b, m: (b, 0, 8 + hp))
    g_spec = pl.BlockSpec((None, TQ, 128), lambda hp, b, m: (b, m, 12 + hp))
    bias_spec = pl.BlockSpec((2, 3, TQ, KW), lambda hp, b, m: (hp, 0, 0, 0))
    return nm, q_spec, k_spec, v_spec, g_spec, bias_spec


def _na_tile(m, nm, rows):
    typ = jnp.where(m == 0, 0, jnp.where(m == nm - 1, 2, 1))
    start = pl.multiple_of(jnp.clip(4 * m - 4, 0, rows - 12) * GRID_W, TQ)
    return typ, start


def _na_fwd_call(P, bias, L, LC):
    B, T, _ = P.shape
    rows = L // GRID_W
    nm, q_spec, k_spec, v_spec, g_spec, bias_spec = _na_specs(L, T, rows)

    def body(q_ref, k_ref, v_ref, g_ref, bias_ref, y_ref):
        typ, start = _na_tile(pl.program_id(2), nm, rows)
        for hh in range(2):
            ln = slice(hh * NA_DH, (hh + 1) * NA_DH)
            q = q_ref[:, ln]
            kw, vw = k_ref[pl.ds(start, KW), ln], v_ref[pl.ds(start, KW), ln]
            kc, vc = k_ref[L:L + LC, ln], v_ref[L:L + LC, ln]
            s1 = _dot_nt(q, kw) + bias_ref[hh, typ]
            s2 = _dot_nt(q, kc)
            mx = jnp.maximum(jnp.max(s1, axis=-1, keepdims=True), jnp.max(s2, axis=-1, keepdims=True))
            p1, p2 = jnp.exp(s1 - mx), jnp.exp(s2 - mx)
            inv = 1.0 / (jnp.sum(p1, axis=-1, keepdims=True) + jnp.sum(p2, axis=-1, keepdims=True))
            o = _dot((p1 * inv).astype(BF16), vw) + _dot((p2 * inv).astype(BF16), vc)
            g = g_ref[:, ln].astype(F32)
            y_ref[:, ln] = (o * (g * _sigmoid(g))).astype(BF16)

    return pl.pallas_call(
        body, name="na_fwd", grid=(4, B, nm),
        in_specs=[q_spec, k_spec, v_spec, g_spec, bias_spec],
        out_specs=pl.BlockSpec((None, TQ, 128), lambda hp, b, m: (b, m, hp)),
        out_shape=jax.ShapeDtypeStruct((B, L, 512), BF16),
        compiler_params=_params(("arbitrary",) * 3))(P, P, P, P, bias)


def _na_bwd_call(P, bias, dY, L, LC):
    B, T, _ = P.shape
    rows = L // GRID_W
    nm, q_spec, k_spec, v_spec, g_spec, bias_spec = _na_specs(L, T, rows)
    scale = NA_DH ** -0.5

    def body(q_ref, k_ref, v_ref, g_ref, bias_ref, dy_ref, dq_ref, dg_ref, dk_ref, dv_ref, db_ref):
        b, m = pl.program_id(1), pl.program_id(2)
        typ, start = _na_tile(m, nm, rows)

        @pl.when(m == 0)
        def _():
            dk_ref[...] = jnp.zeros_like(dk_ref)
            dv_ref[...] = jnp.zeros_like(dv_ref)

        @pl.when((m == 0) & (b == 0))
        def _():
            db_ref[...] = jnp.zeros_like(db_ref)

        for hh in range(2):
            ln = slice(hh * NA_DH, (hh + 1) * NA_DH)
            q = q_ref[:, ln]
            kw, vw = k_ref[pl.ds(start, KW), ln], v_ref[pl.ds(start, KW), ln]
            kc, vc = k_ref[L:L + LC, ln], v_ref[L:L + LC, ln]
            s1 = _dot_nt(q, kw) + bias_ref[hh, typ]
            s2 = _dot_nt(q, kc)
            mx = jnp.maximum(jnp.max(s1, axis=-1, keepdims=True), jnp.max(s2, axis=-1, keepdims=True))
            p1, p2 = jnp.exp(s1 - mx), jnp.exp(s2 - mx)
            inv = 1.0 / (jnp.sum(p1, axis=-1, keepdims=True) + jnp.sum(p2, axis=-1, keepdims=True))
            p1, p2 = p1 * inv, p2 * inv
            p1b, p2b = p1.astype(BF16), p2.astype(BF16)
            o = _dot(p1b, vw) + _dot(p2b, vc)
            g = g_ref[:, ln].astype(F32)
            sg = _sigmoid(g)
            dy = dy_ref[:, ln].astype(F32)
            dg_ref[:, ln] = (dy * o * (sg * (1.0 + g * (1.0 - sg)))).astype(BF16)
            do = (dy * (g * sg)).astype(BF16)
            dp1, dp2 = _dot_nt(do, vw), _dot_nt(do, vc)
            delta = jnp.sum(p1 * dp1, axis=-1, keepdims=True) + jnp.sum(p2 * dp2, axis=-1, keepdims=True)
            ds1, ds2 = p1 * (dp1 - delta), p2 * (dp2 - delta)
            db_ref[hh, typ] += ds1
            ds1b, ds2b = ds1.astype(BF16), ds2.astype(BF16)
            dq_ref[:, ln] = ((_dot(ds1b, kw) + _dot(ds2b, kc)) * scale).astype(BF16)
            dk_ref[pl.ds(start, KW), ln] += _dot_tn(ds1b, q)
            dv_ref[pl.ds(start, KW), ln] += _dot_tn(p1b, do)
            dk_ref[L:L + LC, ln] += _dot_tn(ds2b, q)
            dv_ref[L:L + LC, ln] += _dot_tn(p2b, do)

    tile = pl.BlockSpec((None, TQ, 128), lambda hp, b, m: (b, m, hp))
    kv_out = pl.BlockSpec((None, T, 128), lambda hp, b, m: (b, 0, hp))
    return pl.pallas_call(
        body, name="na_bwd", grid=(4, B, nm),
        in_specs=[q_spec, k_spec, v_spec, g_spec, bias_spec, tile],
        out_specs=(tile, tile, kv_out, kv_out, bias_spec),
        out_shape=(jax.ShapeDtypeStruct((B, L, 512), BF16), jax.ShapeDtypeStruct((B, L, 512), BF16),
                   jax.ShapeDtypeStruct((B, T, 512), F32), jax.ShapeDtypeStruct((B, T, 512), F32),
                   jax.ShapeDtypeStruct(bias.shape, F32)),
        compiler_params=_params(("arbitrary",) * 3))(P, P, P, P, bias, dY)


def _head_scalar(dec_ref, h):
    lane = lax.broadcasted_iota(jnp.int32, dec_ref.shape, 1)
    return -jnp.sum(jnp.where(lane == h, jnp.exp(dec_ref[...]), 0.0), axis=1, keepdims=True)


def _decay_lat(tpos, ks, lgf, lgb):
    spos = (ks + lax.broadcasted_iota(jnp.int32, (1, TK), 1)).astype(F32)
    dist = tpos - spos
    dm = jnp.exp(dist * jnp.where(dist > 0, lgf, -lgb)) * jnp.where(dist == 0, 2.0, 1.0)
    return dist, dm


def _decay_ctx(tpos, L, LC, lgf, lgb):
    jc = lax.broadcasted_iota(jnp.int32, (1, LC), 1).astype(F32)
    df = tpos + (float(LC) - jc)
    db = (float(L) - tpos) + jc
    return df, db, jnp.exp(lgf * df), jnp.exp(lgb * db)


def _ret_specs(T):
    q_spec = pl.BlockSpec((None, TQ, 128), lambda b, h, i: (b, i, 16 + h))
    k_spec = pl.BlockSpec((None, T, 128), lambda b, h, i: (b, 0, 20 + h))
    v_spec = pl.BlockSpec((None, T, 128), lambda b, h, i: (b, 0, 24 + h))
    g_spec = pl.BlockSpec((None, TQ, 128), lambda b, h, i: (b, i, 28 + h))
    dec_spec = pl.BlockSpec((1, 4), lambda b, h, i: (0, 0))
    gn_spec = pl.BlockSpec((1, 128), lambda b, h, i: (0, h))
    return q_spec, k_spec, v_spec, g_spec, dec_spec, gn_spec


def _ret_fwd_call(P, dec_f, dec_b, ret_norm_g, L, LC):
    B, T, _ = P.shape
    q_spec, k_spec, v_spec, g_spec, dec_spec, gn_spec = _ret_specs(T)

    def body(df_ref, db_ref, q_ref, k_ref, v_ref, g_ref, gn_ref, y_ref, o_ref):
        h, i = pl.program_id(1), pl.program_id(2)
        lgf, lgb = _head_scalar(df_ref, h), _head_scalar(db_ref, h)
        tpos = (i * TQ + lax.broadcasted_iota(jnp.int32, (TQ, 1), 0)).astype(F32)
        q = q_ref[...]

        def chunk(j, acc):
            ks = pl.multiple_of(j * TK, TK)
            kj, vj = k_ref[pl.ds(ks, TK), :], v_ref[pl.ds(ks, TK), :]
            _, dm = _decay_lat(tpos, ks, lgf, lgb)
            return acc + _dot((_dot_nt(q, kj) * dm).astype(BF16), vj)

        acc = lax.fori_loop(0, L // TK, chunk, jnp.zeros((TQ, RET_DK), F32))
        _, _, ef, eb = _decay_ctx(tpos, L, LC, lgf, lgb)
        acc = acc + _dot((_dot_nt(q, k_ref[L:L + LC, :]) * (ef + eb)).astype(BF16), v_ref[L:L + LC, :])
        o_ref[...] = acc
        rn = lax.rsqrt(jnp.mean(acc * acc, axis=-1, keepdims=True) + EPS)
        g = g_ref[...].astype(F32)
        y_ref[...] = ((acc * rn * gn_ref[...]).astype(F32) * (g * _sigmoid(g))).astype(BF16)

    tile = pl.BlockSpec((None, TQ, 128), lambda b, h, i: (b, i, h))
    return pl.pallas_call(
        body, name="ret_fwd", grid=(B, 4, L // TQ),
        in_specs=[dec_spec, dec_spec, q_spec, k_spec, v_spec, g_spec, gn_spec],
        out_specs=(tile, tile),
        out_shape=(jax.ShapeDtypeStruct((B, L, 512), BF16), jax.ShapeDtypeStruct((B, L, 512), F32)),
        compiler_params=_params(("arbitrary",) * 3))(dec_f, dec_b, P, P, P, P, ret_norm_g)


def _ret_bwd_call(P, dec_f, dec_b, ret_norm_g, o_ret, dY, cos2, sin2, L, LC):
    B, T, _ = P.shape
    ni = L // TQ
    kscale = RET_DK ** -0.5
    q_spec, k_spec, v_spec, g_spec, dec_spec, gn_spec = _ret_specs(T)

    def body(df_ref, db_ref, q_ref, k_ref, v_ref, g_ref, gn_ref, o_ref, dy_ref, cos_ref, sin_ref,
             dq_ref, dg_ref, dk_ref, dv_ref, dgn_ref, dlg_ref):
        h, i = pl.program_id(1), pl.program_id(2)
        lgf, lgb = _head_scalar(df_ref, h), _head_scalar(db_ref, h)
        tpos = (i * TQ + lax.broadcasted_iota(jnp.int32, (TQ, 1), 0)).astype(F32)

        @pl.when(i == 0)
        def _():
            dk_ref[...] = jnp.zeros_like(dk_ref)
            dv_ref[...] = jnp.zeros_like(dv_ref)
            dgn_ref[...] = jnp.zeros_like(dgn_ref)
            dlg_ref[...] = jnp.zeros_like(dlg_ref)

        q = q_ref[...]
        o = o_ref[...]
        g = g_ref[...].astype(F32)
        dy = dy_ref[...].astype(F32)
        gn = gn_ref[...]
        sg = _sigmoid(g)
        rn = lax.rsqrt(jnp.mean(o * o, axis=-1, keepdims=True) + EPS)
        nrm = o * rn
        dg_ref[...] = (dy * (nrm * gn) * (sg * (1.0 + g * (1.0 - sg)))).astype(BF16)
        dhn = dy * (g * sg)
        dgn_ref[...] += jnp.sum(dhn * nrm, axis=0, keepdims=True)
        dnrm = dhn * gn
        do = rn * (dnrm - nrm * jnp.mean(dnrm * nrm, axis=-1, keepdims=True))
        dob = do.astype(BF16)

        def chunk(j, dq):
            ks = pl.multiple_of(j * TK, TK)
            kj, vj = k_ref[pl.ds(ks, TK), :], v_ref[pl.ds(ks, TK), :]
            dist, dm = _decay_lat(tpos, ks, lgf, lgb)
            s = _dot_nt(q, kj)
            dsv = _dot_nt(dob, vj)
            dsb = (dsv * dm).astype(BF16)
            dk_ref[pl.ds(ks, TK), :] += _dot_tn(dsb, q)
            dv_ref[pl.ds(ks, TK), :] += _dot_tn((s * dm).astype(BF16), dob)
            xw = s * dsv * dm * jnp.abs(dist)
            tot = jnp.sum(xw, axis=0, keepdims=True)
            fwd = jnp.sum(jnp.where(dist > 0, xw, 0.0), axis=0, keepdims=True)
            dlg_ref[0:1, :] += fwd
            dlg_ref[1:2, :] += tot - fwd
            return dq + _dot(dsb, kj)

        dq = lax.fori_loop(0, L // TK, chunk, jnp.zeros((TQ, RET_DK), F32))
        kc, vc = k_ref[L:L + LC, :], v_ref[L:L + LC, :]
        dfc, dbc, ef, eb = _decay_ctx(tpos, L, LC, lgf, lgb)
        s = _dot_nt(q, kc)
        dsv = _dot_nt(dob, vc)
        dsb = (dsv * (ef + eb)).astype(BF16)
        dk_ref[L:L + LC, :] += _dot_tn(dsb, q)
        dv_ref[L:L + LC, :] += _dot_tn((s * (ef + eb)).astype(BF16), dob)
        a = s * dsv
        dlg_ref[0:1, 0:LC] += jnp.sum(a * ef * dfc, axis=0, keepdims=True)
        dlg_ref[1:2, 0:LC] += jnp.sum(a * eb * dbc, axis=0, keepdims=True)
        dq = dq + _dot(dsb, kc)
        cs, sn = cos_ref[pl.ds(pl.multiple_of(i * TQ, TQ), TQ), :], sin_ref[pl.ds(pl.multiple_of(i * TQ, TQ), TQ), :]
        dq_ref[...] = (dq * cs - pltpu.roll(dq, 64, 1) * sn).astype(BF16)

        @pl.when(i == ni - 1)
        def _():
            dk = dk_ref[...]
            dk_ref[...] = (dk * cos_ref[...] - pltpu.roll(dk, 64, 1) * sin_ref[...]) * kscale

    tile = pl.BlockSpec((None, TQ, 128), lambda b, h, i: (b, i, h))
    kv_out = pl.BlockSpec((None, T, 128), lambda b, h, i: (b, 0, h))
    tab = pl.BlockSpec((T, RET_DK), lambda b, h, i: (0, 0))
    return pl.pallas_call(
        body, name="ret_bwd", grid=(B, 4, ni),
        in_specs=[dec_spec, dec_spec, q_spec, k_spec, v_spec, g_spec, gn_spec, tile,
                  pl.BlockSpec((None, TQ, 128), lambda b, h, i: (b, i, 4 + h)), tab, tab],
        out_specs=(tile, tile, kv_out, kv_out,
                   pl.BlockSpec((None, 1, 128), lambda b, h, i: (b, 0, h)),
                   pl.BlockSpec((None, None, 8, TK), lambda b, h, i: (b, h, 0, 0))),
        out_shape=(jax.ShapeDtypeStruct((B, L, 512), BF16), jax.ShapeDtypeStruct((B, L, 512), BF16),
                   jax.ShapeDtypeStruct((B, T, 512), F32), jax.ShapeDtypeStruct((B, T, 512), F32),
                   jax.ShapeDtypeStruct((B, 1, 512), F32), jax.ShapeDtypeStruct((B, 4, 8, TK), F32)),
        compiler_params=_params(("arbitrary",) * 3))(
            dec_f, dec_b, P, P, P, P, ret_norm_g, o_ret, dY, cos2, sin2)


def _out_call(y_na, y_ret, x, target, mod, final_g, wout_f):
    B, L, _ = x.shape

    def body(yn_ref, yr_ref, x_ref, t_ref, mod_ref, gf_ref, w_ref, dy_ref, dx2_ref, dw_ref, sm_ref):
        b, i = pl.program_id(0), pl.program_id(1)

        @pl.when((b == 0) & (i == 0))
        def _():
            dw_ref[...] = jnp.zeros_like(dw_ref)
            sm_ref[...] = jnp.zeros_like(sm_ref)

        gate = mod_ref[pl.ds(b, 1), 2 * D:3 * D]
        gf = gf_ref[...]
        yn, yr = yn_ref[...], yr_ref[...]
        ylat = _dot(yn, w_ref[0:512, :]) + _dot(yr, w_ref[512:1024, :])
        x2 = x_ref[...] + gate * ylat
        r = lax.rsqrt(jnp.mean(x2 * x2, axis=-1, keepdims=True) + EPS)
        xr = x2 * r
        err = xr * gf - t_ref[...]
        sm_ref[1:2, :] += jnp.sum(err * err, axis=0, keepdims=True)
        dout = err * (1.0 / D)
        sm_ref[0:1, :] += jnp.sum(dout * xr, axis=0, keepdims=True)
        gd = dout * gf
        dx2 = r * (gd - xr * jnp.mean(gd * xr, axis=-1, keepdims=True))
        dx2_ref[...] = dx2
        sm_ref[pl.ds(2 + b, 1), :] += jnp.sum(dx2 * ylat, axis=0, keepdims=True)
        dyl = (gate * dx2).astype(BF16)
        dy_ref[:, 0:512] = _dot_nt(dyl, w_ref[0:512, :]).astype(BF16)
        dy_ref[:, 512:1024] = _dot_nt(dyl, w_ref[512:1024, :]).astype(BF16)
        dw_ref[0:512, :] += _dot_tn(yn, dyl)
        dw_ref[512:1024, :] += _dot_tn(yr, dyl)

    half = pl.BlockSpec((None, TQ, 512), lambda b, i: (b, i, 0))
    full = pl.BlockSpec((None, TQ, D), lambda b, i: (b, i, 0))
    return pl.pallas_call(
        body, name="out_proj_loss", grid=(B, L // TQ),
        in_specs=[half, half, full, full,
                  pl.BlockSpec((8, 3 * D), lambda b, i: (0, 0)),
                  pl.BlockSpec((1, D), lambda b, i: (0, 0)),
                  pl.BlockSpec((D, D), lambda b, i: (0, 0))],
        out_specs=(full, full, pl.BlockSpec((D, D), lambda b, i: (0, 0)),
                   pl.BlockSpec((8, D), lambda b, i: (0, 0))),
        out_shape=(jax.ShapeDtypeStruct((B, L, D), BF16), jax.ShapeDtypeStruct((B, L, D), F32),
                   jax.ShapeDtypeStruct((D, D), F32), jax.ShapeDtypeStruct((8, D), F32)),
        compiler_params=_params(("arbitrary",) * 2))(y_na, y_ret, x, target, mod, final_g, wout_f)


def _dh_call(dsec, win_f, x, ctx, dx2, mod, norm_g):
    B, L, _ = x.shape
    LC = ctx.shape[1]
    nl = L // TQ

    def body(d0, d1, d2, d3, d4, d5, d6, d7, w_ref, x_ref, ctx_ref, dx2_ref, mod_ref, g_ref,
             gx_ref, sm_ref):
        drefs = (d0, d1, d2, d3, d4, d5, d6, d7)
        b, t = pl.program_id(0), pl.program_id(1)
        is_lat = t < nl

        @pl.when((b == 0) & (t == 0))
        def _():
            sm_ref[...] = jnp.zeros_like(sm_ref)

        def dh_of(secs):
            acc = jnp.zeros((TQ, D), F32)
            for sec in secs:
                s, half = divmod(sec, 2)
                acc = acc + _dot_nt(drefs[sec][...].astype(BF16), w_ref[s, :, half * 512:(half + 1) * 512])
            return acc

        def norm_bwd(dh, xt, mrow):
            scale = mrow[:, D:2 * D]
            g = g_ref[...]
            rstd = lax.rsqrt(jnp.mean(xt * xt, axis=-1, keepdims=True) + EPS)
            xn = xt * rstd
            dshift = jnp.sum(dh, axis=0, keepdims=True)
            dscale = jnp.sum(dh * (xn * g), axis=0, keepdims=True)
            dhn = dh * (1.0 + scale)
            sm_ref[0:1, :] += jnp.sum(dhn * xn, axis=0, keepdims=True)
            dxn = dhn * g
            dx = rstd * (dxn - xn * jnp.mean(dxn * xn, axis=-1, keepdims=True))
            return dshift, dscale, dx

        @pl.when(is_lat)
        def _():
            dshift, dscale, dx = norm_bwd(dh_of(range(8)), x_ref[...], mod_ref[pl.ds(b, 1), :])
            sm_ref[pl.ds(3 + b, 1), :] += dshift
            sm_ref[pl.ds(3 + B + b, 1), :] += dscale
            gx_ref[...] = dx2_ref[...] + dx

        @pl.when(jnp.logical_not(is_lat))
        def _():
            dshift, dscale, _ = norm_bwd(dh_of((1, 2, 5, 6)), ctx_ref[...], mod_ref[B:B + 1, :])
            sm_ref[1:2, :] += dshift
            sm_ref[2:3, :] += dscale

    lat = lambda b, t: (b, jnp.minimum(t, nl - 1), 0)
    tok = lambda b, t: (b, t, 0)
    sec_specs = [pl.BlockSpec((None, TQ, 512), lat if sec in (0, 3, 4, 7) else tok) for sec in range(8)]
    return pl.pallas_call(
        body, name="dh_norm_bwd", grid=(B, nl + 1),
        in_specs=sec_specs + [
            pl.BlockSpec((N_SHARD, D, D), lambda b, t: (0, 0, 0)),
            pl.BlockSpec((None, TQ, D), lat),
            pl.BlockSpec((None, LC, D), lambda b, t: (b, 0, 0)),
            pl.BlockSpec((None, TQ, D), lat),
            pl.BlockSpec((8, 3 * D), lambda b, t: (0, 0)),
            pl.BlockSpec((1, D), lambda b, t: (0, 0))],
        out_specs=(pl.BlockSpec((None, TQ, D), lat), pl.BlockSpec((8, D), lambda b, t: (0, 0))),
        out_shape=(jax.ShapeDtypeStruct((B, L, D), F32), jax.ShapeDtypeStruct((8, D), F32)),
        compiler_params=_params(("arbitrary",) * 2))(*dsec, win_f, x, ctx, dx2, mod, norm_g)


def _dw_call(dsec, h, L):
    B, T, _ = h.shape
    nl = L // TQ

    def body(d0, d1, d2, d3, d4, d5, d6, d7, h_ref, dw_ref, acc_ref):
        drefs = (d0, d1, d2, d3, d4, d5, d6, d7)
        b, t = pl.program_id(0), pl.program_id(1)

        @pl.when((b == 0) & (t == 0))
        def _():
            acc_ref[...] = jnp.zeros_like(acc_ref)

        hb = h_ref[...]

        def add(secs):
            for sec in secs:
                s, half = divmod(sec, 2)
                acc_ref[s, :, half * 512:(half + 1) * 512] += _dot_tn(hb, drefs[sec][...].astype(BF16))

        @pl.when(t < nl)
        def _():
            add(range(8))

        @pl.when(t >= nl)
        def _():
            add((1, 2, 5, 6))

        @pl.when((b == B - 1) & (t == nl))
        def _():
            dw_ref[...] = acc_ref[...].astype(BF16)

    lat = lambda b, t: (b, jnp.minimum(t, nl - 1), 0)
    tok = lambda b, t: (b, t, 0)
    sec_specs = [pl.BlockSpec((None, TQ, 512), lat if sec in (0, 3, 4, 7) else tok) for sec in range(8)]
    return pl.pallas_call(
        body, name="dw_in", grid=(B, nl + 1),
        in_specs=sec_specs + [pl.BlockSpec((None, TQ, D), tok)],
        out_specs=pl.BlockSpec((N_SHARD, D, D), lambda b, t: (0, 0, 0)),
        out_shape=jax.ShapeDtypeStruct((N_SHARD, D, D), BF16),
        scratch_shapes=[pltpu.VMEM((N_SHARD, D, D), F32)],
        compiler_params=_params(("arbitrary",) * 2, vmem_mb=56))(*dsec, h)


def _mesh_pos():
    return lax.axis_index("x"), lax.axis_index("y"), lax.axis_index("c")


def _flip(v, f):
    return 1 - v if f else v


def _remote(src, dst, ssem, rsem, k, peer):
    return pltpu.make_async_remote_copy(src_ref=src, dst_ref=dst, send_sem=ssem.at[k], recv_sem=rsem.at[k],
                                        device_id=peer, device_id_type=MESH)


def _other_chips(x, y):
    return [(_flip(x, fx), _flip(y, fy)) for fx, fy in ((1, 0), (0, 1), (1, 1))]


def _all_to_all_small(src, dst_all, ssem, rsem, k0, x, y, cc):
    me = 4 * x + 2 * y + cc
    sends, recvs = [], []
    for f in range(1, N_DEV):
        px, py, pc = _flip(x, f & 4), _flip(y, f & 2), _flip(cc, f & 1)
        sends.append(_remote(src, dst_all.at[me], ssem, rsem, k0 + f - 1, (px, py, pc)))
        recvs.append(_remote(src, dst_all.at[4 * px + 2 * py + pc], ssem, rsem, k0 + f - 1, (px, py, pc)))
    return sends, recvs


def _finish(local, sends, recvs):
    for cp in recvs:
        cp.wait_recv()
    for cp in sends:
        cp.wait_send()
    for cp in local:
        cp.wait()


def _gather_call(win_b, wout_b, wada_b, c):
    arrs = (win_b, wout_b, wada_b)
    hrs = [a.shape[0] // 2 for a in arrs]

    def body(win, wout, wada, c_ref, win_f, wout_f, wada_f, c_all, ssem, rsem, lsem):
        x, y, cc = _mesh_pos()
        s, me = 2 * x + y, 4 * x + 2 * y + cc
        sib = (x, y, 1 - cc)
        srcs, dsts = (win, wout, wada), (win_f, wout_f, wada_f)

        def half(a, shard, hc):
            return dsts[a].at[shard, pl.ds(hc * hrs[a], hrs[a])]

        local = [pltpu.make_async_copy(srcs[a], dsts[a].at[s], lsem.at[a]) for a in range(3)]
        local.append(pltpu.make_async_copy(c_ref, c_all.at[me], lsem.at[3]))
        ici_send, ici_recv, fwd_send, fwd_recv, k = [], [], [], [], 0
        for px, py in _other_chips(x, y):
            ps = 2 * px + py
            for a in range(3):
                mine = srcs[a].at[pl.ds(cc * hrs[a], hrs[a])]
                ici_send.append(_remote(mine, half(a, s, cc), ssem, rsem, k, (px, py, cc)))
                ici_recv.append(_remote(mine, half(a, ps, cc), ssem, rsem, k, (px, py, cc)))
                fwd_send.append(_remote(half(a, ps, cc), half(a, ps, cc), ssem, rsem, 9 + k, sib))
                fwd_recv.append(_remote(half(a, ps, 1 - cc), half(a, ps, 1 - cc), ssem, rsem, 9 + k, sib))
                k += 1
        c_send, c_recv = _all_to_all_small(c_ref, c_all, ssem, rsem, 18, x, y, cc)
        for cp in local + ici_send + c_send:
            cp.start()
        for got, fwd in zip(ici_recv, fwd_send):
            got.wait_recv()
            fwd.start()
        _finish(local, ici_send + fwd_send + c_send, fwd_recv + c_recv)

    return pl.pallas_call(
        body, name="weight_gather",
        in_specs=[ANY] * 4, out_specs=(ANY,) * 4,
        out_shape=tuple(jax.ShapeDtypeStruct((N_SHARD,) + a.shape, a.dtype) for a in arrs)
        + (jax.ShapeDtypeStruct((N_DEV,) + c.shape, c.dtype),),
        scratch_shapes=[pltpu.SemaphoreType.DMA((25,)), pltpu.SemaphoreType.DMA((25,)),
                        pltpu.SemaphoreType.DMA((4,))],
        )(win_b, wout_b, wada_b, c)


def _grad_halves_call(dwin_b, dwout_b):
    arrs = (dwin_b, dwout_b)
    hrs = [a.shape[1] // 2 for a in arrs]

    def body(din, dout, own_in, got_in, own_out, got_out, ssem, rsem, lsem):
        x, y, cc = _mesh_pos()
        sib = (x, y, 1 - cc)
        srcs, owns, gots = (din, dout), (own_in, own_out), (got_in, got_out)
        local = [pltpu.make_async_copy(srcs[a].at[:, pl.ds(cc * hrs[a], hrs[a])], owns[a], lsem.at[a])
                 for a in range(2)]
        sends = [_remote(srcs[a].at[:, pl.ds((1 - cc) * hrs[a], hrs[a])], gots[a], ssem, rsem, a, sib)
                 for a in range(2)]
        for cp in local + sends:
            cp.start()
        _finish(local, sends, sends)

    shapes = tuple(jax.ShapeDtypeStruct((N_SHARD, hrs[a], arrs[a].shape[2]), arrs[a].dtype) for a in range(2))
    return pl.pallas_call(
        body, name="grad_halves",
        in_specs=[ANY] * 2, out_specs=(ANY,) * 4,
        out_shape=(shapes[0], shapes[0], shapes[1], shapes[1]),
        scratch_shapes=[pltpu.SemaphoreType.DMA((2,)), pltpu.SemaphoreType.DMA((2,)),
                        pltpu.SemaphoreType.DMA((2,))],
        )(dwin_b, dwout_b)


def _grad_ici_call(cp_in, cp_out, small):
    def body(cin, cout, sm, rin, rout, sm_all, ssem, rsem, lsem):
        x, y, cc = _mesh_pos()
        s, me = 2 * x + y, 4 * x + 2 * y + cc
        srcs, dsts = (cin, cout), (rin, rout)
        local = [pltpu.make_async_copy(srcs[a].at[s], dsts[a].at[s], lsem.at[a]) for a in range(2)]
        local.append(pltpu.make_async_copy(sm, sm_all.at[me], lsem.at[2]))
        sends, recvs, k = [], [], 0
        for px, py in _other_chips(x, y):
            ps = 2 * px + py
            for a in range(2):
                sends.append(_remote(srcs[a].at[ps], dsts[a].at[s], ssem, rsem, k, (px, py, cc)))
                recvs.append(_remote(srcs[a].at[s], dsts[a].at[ps], ssem, rsem, k, (px, py, cc)))
                k += 1
        sm_send, sm_recv = _all_to_all_small(sm, sm_all, ssem, rsem, 6, x, y, cc)
        for cp in local + sends + sm_send:
            cp.start()
        _finish(local, sends + sm_send, recvs + sm_recv)

    return pl.pallas_call(
        body, name="grad_scatter",
        in_specs=[ANY] * 3, out_specs=(ANY,) * 3,
        out_shape=(jax.ShapeDtypeStruct(cp_in.shape, cp_in.dtype), jax.ShapeDtypeStruct(cp_out.shape, cp_out.dtype),
                   jax.ShapeDtypeStruct((N_DEV,) + small.shape, F32)),
        scratch_shapes=[pltpu.SemaphoreType.DMA((13,)), pltpu.SemaphoreType.DMA((13,)),
                        pltpu.SemaphoreType.DMA((3,))],
        )(cp_in, cp_out, small)


def _grad_sibling_call(hin, hout):
    def body(hin_ref, hout_ref, gin, gout, ssem, rsem, lsem):
        x, y, cc = _mesh_pos()
        sib = (x, y, 1 - cc)
        srcs, dsts = (hin_ref, hout_ref), (gin, gout)
        local = [pltpu.make_async_copy(srcs[a], dsts[a].at[cc], lsem.at[a]) for a in range(2)]
        sends = [_remote(srcs[a], dsts[a].at[cc], ssem, rsem, a, sib) for a in range(2)]
        recvs = [_remote(srcs[a], dsts[a].at[1 - cc], ssem, rsem, a, sib) for a in range(2)]
        for cp in local + sends:
            cp.start()
        _finish(local, sends, recvs)

    return pl.pallas_call(
        body, name="grad_sibling",
        in_specs=[ANY] * 2, out_specs=(ANY,) * 2,
        out_shape=(jax.ShapeDtypeStruct((2,) + hin.shape, F32), jax.ShapeDtypeStruct((2,) + hout.shape, F32)),
        scratch_shapes=[pltpu.SemaphoreType.DMA((2,)), pltpu.SemaphoreType.DMA((2,)),
                        pltpu.SemaphoreType.DMA((2,))],
        )(hin, hout)


def _sum_slots_call(r, name):
    _, R, C = r.shape
    tr = min(R, 256)

    def body(r_ref, o_ref):
        o_ref[...] = ((r_ref[0].astype(F32) + r_ref[1].astype(F32)) + r_ref[2].astype(F32)) + r_ref[3].astype(F32)

    return pl.pallas_call(
        body, name=name, grid=(R // tr,),
        in_specs=[pl.BlockSpec((N_SHARD, tr, C), lambda i: (0, i, 0))],
        out_specs=pl.BlockSpec((tr, C), lambda i: (i, 0)),
        out_shape=jax.ShapeDtypeStruct((R, C), F32),
        compiler_params=_params(("arbitrary",)))(r)


def _add_call(xs, name, out_dtype):
    R, C = xs[0].shape
    tr = min(R, 512)
    n = len(xs)

    def body(*refs):
        acc = refs[0][...].astype(F32)
        for r in refs[1:n]:
            acc = acc + r[...].astype(F32)
        refs[n][...] = acc.astype(out_dtype)

    spec = pl.BlockSpec((tr, C), lambda i: (i, 0))
    return pl.pallas_call(
        body, name=name, grid=(R // tr,), in_specs=[spec] * n, out_specs=spec,
        out_shape=jax.ShapeDtypeStruct((R, C), out_dtype),
        compiler_params=_params(("arbitrary",)))(*xs)


def _adamw(w, g, m, v):
    m = ADAM_B1 * m + (1.0 - ADAM_B1) * g
    v = ADAM_B2 * v + (1.0 - ADAM_B2) * (g * g)
    m_hat = m / (1.0 - ADAM_B1 ** ADAM_STEP)
    v_hat = v / (1.0 - ADAM_B2 ** ADAM_STEP)
    return -ADAM_LR * (m_hat / (jnp.sqrt(v_hat) + ADAM_EPS) + ADAM_WD * w), m, v


def _adam_call(w, m, v, g, name):
    R, C = w.shape
    tr = 256

    def body(w_ref, m_ref, v_ref, g_ref, d_ref, mo_ref, vo_ref):
        d_ref[...], mo_ref[...], vo_ref[...] = _adamw(w_ref[...], g_ref[...], m_ref[...], v_ref[...])

    spec = pl.BlockSpec((tr, C), lambda i: (i, 0))
    return pl.pallas_call(
        body, name=name, grid=(R // tr,), in_specs=[spec] * 4,
        out_specs=(spec,) * 3, out_shape=(jax.ShapeDtypeStruct((R, C), F32),) * 3,
        compiler_params=_params(("arbitrary",)))(w, m, v, g)


R_GF, R_NG, R_LOSS, R_RNG, R_LGF, R_LGB, R_SHIFT, R_SCALE, R_GATE, R_SHIFT_C, R_SCALE_C, R_RNG2, R_RPB = (
    0, 1, 2, 3, 4, 5, 6, 8, 10, 12, 13, 14, 16)
W_GF, W_NG, W_CCTX, W_RNG, W_DF, W_DB, W_BADA, W_RPB = 0, 1, 2, 3, 4, 5, 6, 9


def _small_final_call(sm_all, c_t, c_ctx, wada_f, wada, m_ada, v_ada, wsm, msm, vsm, B):
    ws = wada.shape[1]
    NB = N_DEV * B

    def body(sm_ref, ct_ref, cctx_ref, wf_ref, wa_ref, ma_ref, va_ref, w_ref, m_ref, v_ref,
             g_ref, d_ref, mo_ref, vo_ref, ga_ref, da_ref, mao_ref, vao_ref, loss_ref, dmod_ref):
        x, y, _ = _mesh_pos()
        s = 2 * x + y
        tot = sm_ref[0]
        for dv in range(1, N_DEV):
            tot = tot + sm_ref[dv]
        w = w_ref[...]
        for dv in range(N_DEV):
            for b in range(B):
                r = dv * B + b
                for part, row in enumerate((R_SHIFT, R_SCALE, R_GATE)):
                    dmod_ref[r:r + 1, part * D:(part + 1) * D] = sm_ref[dv, row + b:row + b + 1, :]
        dmod_ref[NB:NB + 1, 0:D] = tot[R_SHIFT_C:R_SHIFT_C + 1, :]
        dmod_ref[NB:NB + 1, D:2 * D] = tot[R_SCALE_C:R_SCALE_C + 1, :]
        dmod_ref[NB:NB + 1, 2 * D:3 * D] = jnp.zeros((1, D), F32)
        dmod_ref[NB + 1:, :] = jnp.zeros((dmod_ref.shape[0] - NB - 1, 3 * D), F32)
        dmod = dmod_ref[...]
        cc = cctx_ref[...]
        scc = _sigmoid(cc)
        ct = ct_ref[...]
        act_t = ct * _sigmoid(ct)
        dmc = dmod[NB:NB + 1, :].astype(BF16)
        dact = jnp.zeros((1, D), F32)
        for sh in range(N_SHARD):
            dact = dact + _dot_nt(dmc[:, sh * ws:(sh + 1) * ws], wf_ref[sh])
        g = jnp.zeros((16, D), F32)
        rows = lax.broadcasted_iota(jnp.int32, (16, D), 0)

        def put(g, row, val):
            return jnp.where(rows == row, val, g)

        g = put(g, W_GF, tot[R_GF:R_GF + 1, :])
        g = put(g, W_NG, tot[R_NG:R_NG + 1, :])
        g = put(g, W_CCTX, dact * (scc * (1.0 + cc * (1.0 - scc))))
        g = put(g, W_RNG, tot[R_RNG:R_RNG + 1, :] + tot[R_RNG2:R_RNG2 + 1, :])
        g = put(g, W_DF, tot[R_LGF:R_LGF + 1, :] * (-jnp.exp(w[W_DF:W_DF + 1, :])))
        g = put(g, W_DB, tot[R_LGB:R_LGB + 1, :] * (-jnp.exp(w[W_DB:W_DB + 1, :])))
        db = jnp.sum(dmod, axis=0, keepdims=True)
        for part in range(3):
            g = put(g, W_BADA + part, db[:, part * D:(part + 1) * D])
        for part in range(4):
            g = put(g, W_RPB + part, tot[R_RPB + part:R_RPB + part + 1, :])
        g_ref[...] = g
        d_ref[...], mo_ref[...], vo_ref[...] = _adamw(w, g, m_ref[...], v_ref[...])
        loss_ref[...] = jnp.broadcast_to(
            (0.5 / D) * jnp.sum(tot[R_LOSS:R_LOSS + 1, :], axis=1, keepdims=True), (8, 128))
        for sh in range(N_SHARD):
            @pl.when(s == sh)
            def _():
                ga = jnp.dot(act_t, dmod[:, sh * ws:(sh + 1) * ws], precision=HIGHEST,
                             preferred_element_type=F32)
                ga_ref[...] = ga
                da_ref[...], mao_ref[...], vao_ref[...] = _adamw(wa_ref[...], ga, ma_ref[...], va_ref[...])

    sh_small = jax.ShapeDtypeStruct((16, D), F32)
    sh_ada = jax.ShapeDtypeStruct(wada.shape, F32)
    return pl.pallas_call(
        body, name="small_final",
        out_shape=(sh_small,) * 4 + (sh_ada,) * 4 + (jax.ShapeDtypeStruct((8, 128), F32),),
        scratch_shapes=[pltpu.VMEM((NB + 8, 3 * D), F32)],
        compiler_params=_params(vmem_mb=56))(
            sm_all, c_t, c_ctx, wada_f, wada, m_ada, v_ada, wsm, msm, vsm)


def _local_step(x, c, ctx, c_ctx, norm_g, wada_f, b_ada, win_f, na_rpb, dec_f, dec_b, ret_norm_g,
                wout_f, final_g, target):
    B, L, _ = x.shape
    LC = ctx.shape[1]
    assert B == 2
    cos2, sin2 = _rope_tables(L, LC)
    c8 = jnp.concatenate([c, c_ctx[None, :], jnp.zeros((8 - B - 1, D), F32)], axis=0)
    mod = _mod_call(c8, wada_f, b_ada)
    bias = _bias_call(na_rpb.reshape(na_rpb.shape[0], -1))
    P, h = _inproj_call(x, ctx, mod, norm_g, win_f, cos2, sin2)
    y_na = _na_fwd_call(P, bias, L, LC)
    y_ret, o_ret = _ret_fwd_call(P, dec_f, dec_b, ret_norm_g, L, LC)
    dY, dx2, dwout_p, sm_out = _out_call(y_na, y_ret, x, target, mod, final_g, wout_f.reshape(D, D))
    dnq, dng, dnk, dnv, dbias = _na_bwd_call(P, bias, dY, L, LC)
    drq, drg, drk, drv, dgn, dlg = _ret_bwd_call(P, dec_f, dec_b, ret_norm_g, o_ret, dY, cos2, sin2, L, LC)
    dsec = (dnq, dnk, dnv, dng, drq, drk, drv, drg)
    grad_x, sm_dh = _dh_call(dsec, win_f, x, ctx, dx2, mod, norm_g)
    dwin_p = _dw_call(dsec, h, L)
    drpb, dlg_sum = _small_reduce_call(dbias, dlg, B)
    z = jnp.zeros((1, D), F32)
    pad = lambda v: jnp.pad(v.reshape(1, -1), ((0, 0), (0, D - v.size)))
    dlg_sum = dlg_sum.reshape(4, 8, 128)
    rpb_rows = jnp.pad(drpb[:, :15, :31].reshape(-1), (0, 4 * D - drpb.shape[0] * 465)).reshape(4, D)
    small = jnp.concatenate([
        sm_out[0:1], sm_dh[0:1], sm_out[1:2], pad(dgn[0]), pad(dlg_sum[:, 0, 0]), pad(dlg_sum[:, 1, 0]),
        sm_dh[3:5], sm_dh[5:7], sm_out[2:4], sm_dh[1:2], sm_dh[2:3], pad(dgn[1]), z, rpb_rows,
        jnp.zeros((SM_ROWS - 20, D), F32)], axis=0)
    return grad_x, dwin_p, dwout_p, small


def kernel(x, c, ctx, c_ctx, norm_g, w_ada, b_ada, w_in, na_rpb, ret_decay_fwd, ret_decay_bwd, ret_norm_g, w_out, final_norm_g, loss_target, m_c_ctx, m_norm_g, m_w_ada, m_b_ada, m_w_in, m_na_rpb, m_ret_decay_fwd, m_ret_decay_bwd, m_ret_norm_g, m_w_out, m_final_norm_g, v_c_ctx, v_norm_g, v_w_ada, v_b_ada, v_w_in, v_na_rpb, v_ret_decay_fwd, v_ret_decay_bwd, v_ret_norm_g, v_w_out, v_final_norm_g):
    B = x.shape[0]
    win_f, wout_f, wada_f, c_all = _gather_call(w_in[0].astype(BF16), w_out[0].astype(BF16),
                                                w_ada[0].astype(BF16), c)
    grad_x, dwin_p, dwout_p, small = _local_step(
        x, c, ctx, c_ctx, norm_g, wada_f, b_ada, win_f, na_rpb[0], ret_decay_fwd, ret_decay_bwd,
        ret_norm_g, wout_f, final_norm_g.reshape(1, D), loss_target)
    own_in, got_in, own_out, got_out = _grad_halves_call(
        dwin_p, dwout_p.astype(BF16).reshape(N_SHARD, D // N_SHARD, D))
    flat = lambda a: a.reshape(-1, a.shape[-1])
    cp_in = _add_call([flat(own_in), flat(got_in)], "chip_dw_in", BF16).reshape(own_in.shape)
    cp_out = _add_call([flat(own_out), flat(got_out)], "chip_dw_out", BF16).reshape(own_out.shape)
    rin, rout, sm_all = _grad_ici_call(cp_in, cp_out, small)
    gin, gout = _grad_sibling_call(_sum_slots_call(rin, "sum_dw_in"), _sum_slots_call(rout, "sum_dw_out"))
    g_win, g_wout = gin.reshape(w_in.shape[1:]), gout.reshape(w_out.shape[1:])
    d_win, nm_win, nv_win = _adam_call(w_in[0], m_w_in[0], v_w_in[0], g_win, "adam_w_in")
    d_wout, nm_wout, nv_wout = _adam_call(w_out[0], m_w_out[0], v_w_out[0], g_wout, "adam_w_out")

    def pack(gf, ng, cc, rng, df, db, bada, rpb):
        pad = lambda v: jnp.pad(v.reshape(1, -1), ((0, 0), (0, D - v.size)))
        return jnp.concatenate([
            gf.reshape(1, D), ng.reshape(1, D), cc.reshape(1, D), pad(rng), pad(df), pad(db),
            bada.reshape(3, D), jnp.pad(rpb.reshape(-1), (0, 4 * D - rpb.size)).reshape(4, D),
            jnp.zeros((3, D), F32)], axis=0)

    wsm = pack(final_norm_g, norm_g, c_ctx, ret_norm_g, ret_decay_fwd, ret_decay_bwd, b_ada, na_rpb)
    msm = pack(m_final_norm_g, m_norm_g, m_c_ctx, m_ret_norm_g, m_ret_decay_fwd, m_ret_decay_bwd, m_b_ada, m_na_rpb)
    vsm = pack(v_final_norm_g, v_norm_g, v_c_ctx, v_ret_norm_g, v_ret_decay_fwd, v_ret_decay_bwd, v_b_ada, v_na_rpb)
    c_t = jnp.concatenate([c_all.reshape(N_DEV * B, D), c_ctx.reshape(1, D), jnp.zeros((7, D), F32)], axis=0).T
    outs = _small_final_call(sm_all, c_t, c_ctx.reshape(1, D), wada_f,
                             w_ada[0], m_w_ada[0], v_w_ada[0], wsm, msm, vsm, B)
    smalls, adas, loss = outs[0:4], outs[4:8], outs[8][0, 0]

    def unpack(p):
        rw = ret_norm_g.shape[1]
        return dict(
            final_norm_g=p[W_GF], norm_g=p[W_NG:W_NG + 1], c_ctx=p[W_CCTX], ret_norm_g=p[W_RNG:W_RNG + 1, :rw],
            ret_decay_fwd=p[W_DF:W_DF + 1, :4], ret_decay_bwd=p[W_DB:W_DB + 1, :4],
            b_ada=p[W_BADA:W_BADA + 3].reshape(1, 3 * D),
            na_rpb=p[W_RPB:W_RPB + 4].reshape(-1)[:na_rpb.size].reshape(na_rpb.shape))

    res = []
    for p, ada, win_o, wout_o in zip(smalls, adas, (g_win, d_win, nm_win, nv_win),
                                     (g_wout, d_wout, nm_wout, nv_wout)):
        u = unpack(p)
        res.append([u["c_ctx"], u["norm_g"], ada[None], u["b_ada"], win_o[None], u["na_rpb"],
                    u["ret_decay_fwd"], u["ret_decay_bwd"], u["ret_norm_g"], wout_o[None], u["final_norm_g"]])
    return (loss, grad_x, *res[0], *res[1], *res[2], *res[3])
```

```python
import functools

import numpy as np
import jax
import jax.numpy as jnp
from jax import lax
from jax.experimental import pallas as pl
from jax.experimental.pallas import tpu as pltpu

F32 = jnp.float32
BF16 = jnp.bfloat16
HIGHEST = lax.Precision.HIGHEST

D = 1024
GRID_W = 64
NA_DH = 64
RET_DK = 128
ROPE_BASE = 10000.0
EPS = 1e-6
NEG = -1e30
TQ = 256
TK = 512
KW = 12 * GRID_W
N_SHARD = 4
N_DEV = 8
SM_ROWS = 24

ADAM_LR = 0.001
ADAM_B1 = 0.9
ADAM_B2 = 0.999
ADAM_EPS = 1e-08
ADAM_WD = 0.01
ADAM_STEP = 10

MESH = pl.DeviceIdType.MESH
ANY = pl.BlockSpec(memory_space=pl.ANY)


def _params(sem=None, vmem_mb=48):
    return pltpu.CompilerParams(dimension_semantics=sem, vmem_limit_bytes=vmem_mb << 20)


def _dot(a, b):
    return jnp.dot(a, b, preferred_element_type=F32)


def _dot_nt(a, b):
    return lax.dot_general(a, b, (((1,), (1,)), ((), ())), preferred_element_type=F32)


def _dot_tn(a, b):
    return lax.dot_general(a, b, (((0,), (0,)), ((), ())), preferred_element_type=F32)


def _sigmoid(x):
    return 1.0 / (1.0 + jnp.exp(-x))


def _rope_tables(L, LC):
    half = RET_DK // 2
    nf = half // 2
    t = np.arange(L)
    row = (t // GRID_W).astype(np.float32)
    col = (t % GRID_W).astype(np.float32)
    inv = (np.float32(ROPE_BASE) ** (-np.arange(nf, dtype=np.float32) / np.float32(nf))).astype(np.float32)
    ang = np.concatenate([row[:, None] * inv, col[:, None] * inv], axis=-1).astype(np.float32)
    cos, sin = np.cos(ang).astype(np.float32), np.sin(ang).astype(np.float32)
    cos2 = np.concatenate([cos, cos], axis=-1)
    sin2 = np.concatenate([-sin, sin], axis=-1)
    cos2 = np.concatenate([cos2, np.ones((LC, RET_DK), np.float32)], axis=0)
    sin2 = np.concatenate([sin2, np.zeros((LC, RET_DK), np.float32)], axis=0)
    return jnp.asarray(cos2), jnp.asarray(sin2)


def _mod_call(c8, wada_f, b_ada):
    ws = wada_f.shape[2]

    def body(c_ref, w_ref, b_ref, o_ref):
        a = c_ref[...]
        a = (a * _sigmoid(a)).astype(BF16)
        for s in range(N_SHARD):
            o_ref[:, s * ws:(s + 1) * ws] = _dot(a, w_ref[s]) + b_ref[:, s * ws:(s + 1) * ws]

    return pl.pallas_call(
        body, name="ada_mod", out_shape=jax.ShapeDtypeStruct((8, 3 * D), F32),
        compiler_params=_params())(c8, wada_f, b_ada)


def _dc_masks():
    cq = lax.broadcasted_iota(jnp.int32, (GRID_W, GRID_W), 0)
    ck = lax.broadcasted_iota(jnp.int32, (GRID_W, GRID_W), 1)
    dc = jnp.clip(ck - cq + 15, 0, 30)
    c0 = jnp.clip(cq - 8, 0, GRID_W - 16)
    col_ok = (ck >= c0) & (ck < c0 + 16)
    return dc, col_ok


def _bias_blocks():
    out = []
    for typ, delta in enumerate((4, 0, -4)):
        for rq in range(4):
            for rkk in range(12):
                dr = rkk + delta - rq - 4
                if typ == 0:
                    ok = -rq <= dr <= 7 - rq
                elif typ == 1:
                    ok = -4 <= dr <= 3
                else:
                    ok = -4 - rq <= dr <= 3 - rq
                out.append((typ, rq, rkk, dr if ok else None))
    return out


def _bias_call(rpb_flat):
    nh = rpb_flat.shape[0]

    def body(r_ref, bias_ref, et_ref):
        dc, col_ok = _dc_masks()
        masks = [(dc == j).astype(F32) for j in range(31)]

        def per_h(h, carry):
            for dr in range(15):
                t = jnp.zeros((GRID_W, GRID_W), F32)
                for j in range(31):
                    t = t + masks[j] * r_ref[h, dr * 31 + j]
                et_ref[dr] = jnp.where(col_ok, t, NEG)
            neg = jnp.full((GRID_W, GRID_W), NEG, F32)
            for typ, rq, rkk, dr in _bias_blocks():
                blk = neg if dr is None else et_ref[dr + 7]
                bias_ref[h, typ, rq * 64:(rq + 1) * 64, rkk * 64:(rkk + 1) * 64] = blk
            return carry

        lax.fori_loop(0, nh, per_h, 0)

    return pl.pallas_call(
        body, name="rpb_bias",
        out_shape=jax.ShapeDtypeStruct((nh, 3, TQ, KW), F32),
        in_specs=[pl.BlockSpec(memory_space=pltpu.SMEM)],
        out_specs=pl.BlockSpec(memory_space=pltpu.VMEM),
        scratch_shapes=[pltpu.VMEM((15, GRID_W, GRID_W), F32)],
        compiler_params=_params())(rpb_flat)


def _small_reduce_call(dbias, dlg, B):
    nh = dbias.shape[0]

    def body(db_ref, dlg_ref, drpb_ref, dlgo_ref, p_ref):
        dc, _ = _dc_masks()
        masks = [(dc == j).astype(F32) for j in range(31)]
        ones = jnp.ones((8, GRID_W), F32)
        p_ref[...] = jnp.zeros_like(p_ref)
        drpb_ref[...] = jnp.zeros_like(drpb_ref)

        def per_h(h, carry):
            acc = {}
            for typ, rq, rkk, dr in _bias_blocks():
                if dr is None:
                    continue
                blk = db_ref[h, typ, rq * 64:(rq + 1) * 64, rkk * 64:(rkk + 1) * 64]
                acc[dr] = blk if dr not in acc else acc[dr] + blk
            for dr in range(-7, 8):
                t = acc[dr]
                for j in range(31):
                    p_ref[j:j + 1, :] = jnp.sum(t * masks[j], axis=0, keepdims=True)
                red = lax.dot_general(ones, p_ref[...], (((1,), (1,)), ((), ())),
                                      precision=HIGHEST, preferred_element_type=F32)
                drpb_ref[h, dr + 7:dr + 8, :] = red[0:1, :]
            return carry

        lax.fori_loop(0, nh, per_h, 0)
        x = dlg_ref[0]
        for b in range(1, B):
            x = x + dlg_ref[b]
        x = x.reshape(4 * 8, TK)
        dlgo_ref[...] = jnp.dot(x, jnp.ones((TK, 128), F32), precision=HIGHEST,
                                preferred_element_type=F32)

    return pl.pallas_call(
        body, name="small_reduce",
        out_shape=(jax.ShapeDtypeStruct((nh, 16, 32), F32), jax.ShapeDtypeStruct((32, 128), F32)),
        scratch_shapes=[pltpu.VMEM((32, GRID_W), F32)],
        compiler_params=_params())(dbias, dlg)


def _inproj_call(x, ctx, mod, norm_g, win_f, cos2, sin2):
    B, L, _ = x.shape
    LC = ctx.shape[1]
    T = L + LC
    nl = L // TQ
    assert LC == TQ and L % TQ == 0
    kscale = RET_DK ** -0.5

    def body(x_ref, ctx_ref, mod_ref, g_ref, w_ref, cos_ref, sin_ref, p_ref, h_ref):
        b = pl.program_id(0)
        t = pl.program_id(1)
        is_lat = t < nl
        xt = jnp.where(is_lat, x_ref[...], ctx_ref[...])
        mrow = mod_ref[pl.ds(jnp.where(is_lat, b, B), 1), :]
        shift, scale = mrow[:, 0:D], mrow[:, D:2 * D]
        rstd = lax.rsqrt(jnp.mean(xt * xt, axis=-1, keepdims=True) + EPS)
        hb = ((xt * rstd * g_ref[...]) * (1.0 + scale) + shift).astype(BF16)
        h_ref[...] = hb
        cs, sn = cos_ref[...], sin_ref[...]
        for sec in range(8):
            s, half = divmod(sec, 2)
            acc = _dot(hb, w_ref[s, :, half * 512:(half + 1) * 512])
            if sec == 0:
                acc = acc * (NA_DH ** -0.5)
            if sec in (4, 5):
                for j in range(4):
                    a = acc[:, j * 128:(j + 1) * 128]
                    r = a * cs + pltpu.roll(a, 64, 1) * sn
                    if sec == 5:
                        r = r * kscale
                    p_ref[:, sec * 512 + j * 128:sec * 512 + (j + 1) * 128] = r.astype(BF16)
            else:
                p_ref[:, sec * 512:(sec + 1) * 512] = acc.astype(BF16)

    return pl.pallas_call(
        body, name="in_proj", grid=(B, T // TQ),
        in_specs=[
            pl.BlockSpec((None, TQ, D), lambda b, t: (b, jnp.minimum(t, nl - 1), 0)),
            pl.BlockSpec((None, TQ, D), lambda b, t: (b, 0, 0)),
            pl.BlockSpec((8, 3 * D), lambda b, t: (0, 0)),
            pl.BlockSpec((1, D), lambda b, t: (0, 0)),
            pl.BlockSpec((N_SHARD, D, D), lambda b, t: (0, 0, 0)),
            pl.BlockSpec((TQ, RET_DK), lambda b, t: (t, 0)),
            pl.BlockSpec((TQ, RET_DK), lambda b, t: (t, 0)),
        ],
        out_specs=(pl.BlockSpec((None, TQ, 4 * D), lambda b, t: (b, t, 0)),
                   pl.BlockSpec((None, TQ, D), lambda b, t: (b, t, 0))),
        out_shape=(jax.ShapeDtypeStruct((B, T, 4 * D), BF16), jax.ShapeDtypeStruct((B, T, D), BF16)),
        compiler_params=_params(("arbitrary", "arbitrary")))(x, ctx, mod, norm_g, win_f, cos2, sin2)


def _na_specs(L, T, rows):
    nm = rows // 4
    q_spec = pl.BlockSpec((None, TQ, 128), lambda hp, b, m: (b, m, hp))
    k_spec = pl.BlockSpec((None, T, 128), lambda hp, b, m: (b, 0, 4 + hp))
    v_spec = pl.BlockSpec((None, T, 128), lambda hp, b, m: (b, 0, 8 + hp))
    g_spec = pl.BlockSpec((None, TQ, 128), lambda hp, b, m: (b, m, 12 + hp))
    bias_spec = pl.BlockSpec((2, 3, TQ, KW), lambda hp, b, m: (hp, 0, 0, 0))
    return nm, q_spec, k_spec, v_spec, g_spec, bias_spec


def _na_tile(m, nm, rows):
    typ = jnp.where(m == 0, 0, jnp.where(m == nm - 1, 2, 1))
    start = pl.multiple_of(jnp.clip(4 * m - 4, 0, rows - 12) * GRID_W, TQ)
    return typ, start


def _na_fwd_call(P, bias, L, LC):
    B, T, _ = P.shape
    rows = L // GRID_W
    nm, q_spec, k_spec, v_spec, g_spec, bias_spec = _na_specs(L, T, rows)

    def body(q_ref, k_ref, v_ref, g_ref, bias_ref, y_ref):
        typ, start = _na_tile(pl.program_id(2), nm, rows)
        for hh in range(2):
            ln = slice(hh * NA_DH, (hh + 1) * NA_DH)
            q = q_ref[:, ln]
            kw, vw = k_ref[pl.ds(start, KW), ln], v_ref[pl.ds(start, KW), ln]
            kc, vc = k_ref[L:L + LC, ln], v_ref[L:L + LC, ln]
            s1 = _dot_nt(q, kw) + bias_ref[hh, typ]
            s2 = _dot_nt(q, kc)
            mx = jnp.maximum(jnp.max(s1, axis=-1, keepdims=True), jnp.max(s2, axis=-1, keepdims=True))
            p1, p2 = jnp.exp(s1 - mx), jnp.exp(s2 - mx)
            inv = 1.0 / (jnp.sum(p1, axis=-1, keepdims=True) + jnp.sum(p2, axis=-1, keepdims=True))
            o = _dot((p1 * inv).astype(BF16), vw) + _dot((p2 * inv).astype(BF16), vc)
            g = g_ref[:, ln].astype(F32)
            y_ref[:, ln] = (o * (g * _sigmoid(g))).astype(BF16)

    return pl.pallas_call(
        body, name="na_fwd", grid=(4, B, nm),
        in_specs=[q_spec, k_spec, v_spec, g_spec, bias_spec],
        out_specs=pl.BlockSpec((None, TQ, 128), lambda hp, b, m: (b, m, hp)),
        out_shape=jax.ShapeDtypeStruct((B, L, 512), BF16),
        compiler_params=_params(("arbitrary",) * 3))(P, P, P, P, bias)


def _na_bwd_call(P, bias, dY, L, LC):
    B, T, _ = P.shape
    rows = L // GRID_W
    nm, q_spec, k_spec, v_spec, g_spec, bias_spec = _na_specs(L, T, rows)
    scale = NA_DH ** -0.5

    def body(q_ref, k_ref, v_ref, g_ref, bias_ref, dy_ref, dq_ref, dg_ref, dk_ref, dv_ref, db_ref):
        b, m = pl.program_id(1), pl.program_id(2)
        typ, start = _na_tile(m, nm, rows)

        @pl.when(m == 0)
        def _():
            dk_ref[...] = jnp.zeros_like(dk_ref)
            dv_ref[...] = jnp.zeros_like(dv_ref)

        @pl.when((m == 0) & (b == 0))
        def _():
            db_ref[...] = jnp.zeros_like(db_ref)

        for hh in range(2):
            ln = slice(hh * NA_DH, (hh + 1) * NA_DH)
            q = q_ref[:, ln]
            kw, vw = k_ref[pl.ds(start, KW), ln], v_ref[pl.ds(start, KW), ln]
            kc, vc = k_ref[L:L + LC, ln], v_ref[L:L + LC, ln]
            s1 = _dot_nt(q, kw) + bias_ref[hh, typ]
            s2 = _dot_nt(q, kc)
            mx = jnp.maximum(jnp.max(s1, axis=-1, keepdims=True), jnp.max(s2, axis=-1, keepdims=True))
            p1, p2 = jnp.exp(s1 - mx), jnp.exp(s2 - mx)
            inv = 1.0 / (jnp.sum(p1, axis=-1, keepdims=True) + jnp.sum(p2, axis=-1, keepdims=True))
            p1, p2 = p1 * inv, p2 * inv
            p1b, p2b = p1.astype(BF16), p2.astype(BF16)
            o = _dot(p1b, vw) + _dot(p2b, vc)
            g = g_ref[:, ln].astype(F32)
            sg = _sigmoid(g)
            dy = dy_ref[:, ln].astype(F32)
            dg_ref[:, ln] = (dy * o * (sg * (1.0 + g * (1.0 - sg)))).astype(BF16)
            do = (dy * (g * sg)).astype(BF16)
            dp1, dp2 = _dot_nt(do, vw), _dot_nt(do, vc)
            delta = jnp.sum(p1 * dp1, axis=-1, keepdims=True) + jnp.sum(p2 * dp2, axis=-1, keepdims=True)
            ds1, ds2 = p1 * (dp1 - delta), p2 * (dp2 - delta)
            db_ref[hh, typ] += ds1
            ds1b, ds2b = ds1.astype(BF16), ds2.astype(BF16)
            dq_ref[:, ln] = ((_dot(ds1b, kw) + _dot(ds2b, kc)) * scale).astype(BF16)
            dk_ref[pl.ds(start, KW), ln] += _dot_tn(ds1b, q)
            dv_ref[pl.ds(start, KW), ln] += _dot_tn(p1b, do)
            dk_ref[L:L + LC, ln] += _dot_tn(ds2b, q)
            dv_ref[L:L + LC, ln] += _dot_tn(p2b, do)

    tile = pl.BlockSpec((None, TQ, 128), lambda hp, b, m: (b, m, hp))
    kv_out = pl.BlockSpec((None, T, 128), lambda hp, b, m: (b, 0, hp))
    return pl.pallas_call(
        body, name="na_bwd", grid=(4, B, nm),
        in_specs=[q_spec, k_spec, v_spec, g_spec, bias_spec, tile],
        out_specs=(tile, tile, kv_out, kv_out, bias_spec),
        out_shape=(jax.ShapeDtypeStruct((B, L, 512), BF16), jax.ShapeDtypeStruct((B, L, 512), BF16),
                   jax.ShapeDtypeStruct((B, T, 512), F32), jax.ShapeDtypeStruct((B, T, 512), F32),
                   jax.ShapeDtypeStruct(bias.shape, F32)),
        compiler_params=_params(("arbitrary",) * 3))(P, P, P, P, bias, dY)


def _head_scalar(dec_ref, h):
    lane = lax.broadcasted_iota(jnp.int32, dec_ref.shape, 1)
    return -jnp.sum(jnp.where(lane == h, jnp.exp(dec_ref[...]), 0.0), axis=1, keepdims=True)


def _decay_lat(tpos, ks, lgf, lgb):
    spos = (ks + lax.broadcasted_iota(jnp.int32, (1, TK), 1)).astype(F32)
    dist = tpos - spos
    dm = jnp.exp(dist * jnp.where(dist > 0, lgf, -lgb)) * jnp.where(dist == 0, 2.0, 1.0)
    return dist, dm


def _decay_ctx(tpos, L, LC, lgf, lgb):
    jc = lax.broadcasted_iota(jnp.int32, (1, LC), 1).astype(F32)
    df = tpos + (float(LC) - jc)
    db = (float(L) - tpos) + jc
    return df, db, jnp.exp(lgf * df), jnp.exp(lgb * db)


def _ret_specs(T):
    q_spec = pl.BlockSpec((None, TQ, 128), lambda b, h, i: (b, i, 16 + h))
    k_spec = pl.BlockSpec((None, T, 128), lambda b, h, i: (b, 0, 20 + h))
    v_spec = pl.BlockSpec((None, T, 128), lambda b, h, i: (b, 0, 24 + h))
    g_spec = pl.BlockSpec((None, TQ, 128), lambda b, h, i: (b, i, 28 + h))
    dec_spec = pl.BlockSpec((1, 4), lambda b, h, i: (0, 0))
    gn_spec = pl.BlockSpec((1, 128), lambda b, h, i: (0, h))
    return q_spec, k_spec, v_spec, g_spec, dec_spec, gn_spec


def _ret_fwd_call(P, dec_f, dec_b, ret_norm_g, L, LC):
    B, T, _ = P.shape
    q_spec, k_spec, v_spec, g_spec, dec_spec, gn_spec = _ret_specs(T)

    def body(df_ref, db_ref, q_ref, k_ref, v_ref, g_ref, gn_ref, y_ref, o_ref):
        h, i = pl.program_id(1), pl.program_id(2)
        lgf, lgb = _head_scalar(df_ref, h), _head_scalar(db_ref, h)
        tpos = (i * TQ + lax.broadcasted_iota(jnp.int32, (TQ, 1), 0)).astype(F32)
        q = q_ref[...]

        def chunk(j, acc):
            ks = pl.multiple_of(j * TK, TK)
            kj, vj = k_ref[pl.ds(ks, TK), :], v_ref[pl.ds(ks, TK), :]
            _, dm = _decay_lat(tpos, ks, lgf, lgb)
            return acc + _dot((_dot_nt(q, kj) * dm).astype(BF16), vj)

        acc = lax.fori_loop(0, L // TK, chunk, jnp.zeros((TQ, RET_DK), F32))
        _, _, ef, eb = _decay_ctx(tpos, L, LC, lgf, lgb)
        acc = acc + _dot((_dot_nt(q, k_ref[L:L + LC, :]) * (ef + eb)).astype(BF16), v_ref[L:L + LC, :])
        o_ref[...] = acc
        rn = lax.rsqrt(jnp.mean(acc * acc, axis=-1, keepdims=True) + EPS)
        g = g_ref[...].astype(F32)
        y_ref[...] = ((acc * rn * gn_ref[...]).astype(F32) * (g * _sigmoid(g))).astype(BF16)

    tile = pl.BlockSpec((None, TQ, 128), lambda b, h, i: (b, i, h))
    return pl.pallas_call(
        body, name="ret_fwd", grid=(B, 4, L // TQ),
        in_specs=[dec_spec, dec_spec, q_spec, k_spec, v_spec, g_spec, gn_spec],
        out_specs=(tile, tile),
        out_shape=(jax.ShapeDtypeStruct((B, L, 512), BF16), jax.ShapeDtypeStruct((B, L, 512), F32)),
        compiler_params=_params(("arbitrary",) * 3))(dec_f, dec_b, P, P, P, P, ret_norm_g)


def _ret_bwd_call(P, dec_f, dec_b, ret_norm_g, o_ret, dY, cos2, sin2, L, LC):
    B, T, _ = P.shape
    ni = L // TQ
    kscale = RET_DK ** -0.5
    q_spec, k_spec, v_spec, g_spec, dec_spec, gn_spec = _ret_specs(T)

    def body(df_ref, db_ref, q_ref, k_ref, v_ref, g_ref, gn_ref, o_ref, dy_ref, cos_ref, sin_ref,
             dq_ref, dg_ref, dk_ref, dv_ref, dgn_ref, dlg_ref):
        h, i = pl.program_id(1), pl.program_id(2)
        lgf, lgb = _head_scalar(df_ref, h), _head_scalar(db_ref, h)
        tpos = (i * TQ + lax.broadcasted_iota(jnp.int32, (TQ, 1), 0)).astype(F32)

        @pl.when(i == 0)
        def _():
            dk_ref[...] = jnp.zeros_like(dk_ref)
            dv_ref[...] = jnp.zeros_like(dv_ref)
            dgn_ref[...] = jnp.zeros_like(dgn_ref)
            dlg_ref[...] = jnp.zeros_like(dlg_ref)

        q = q_ref[...]
        o = o_ref[...]
        g = g_ref[...].astype(F32)
        dy = dy_ref[...].astype(F32)
        gn = gn_ref[...]
        sg = _sigmoid(g)
        rn = lax.rsqrt(jnp.mean(o * o, axis=-1, keepdims=True) + EPS)
        nrm = o * rn
        dg_ref[...] = (dy * (nrm * gn) * (sg * (1.0 + g * (1.0 - sg)))).astype(BF16)
        dhn = dy * (g * sg)
        dgn_ref[...] += jnp.sum(dhn * nrm, axis=0, keepdims=True)
        dnrm = dhn * gn
        do = rn * (dnrm - nrm * jnp.mean(dnrm * nrm, axis=-1, keepdims=True))
        dob = do.astype(BF16)

        def chunk(j, dq):
            ks = pl.multiple_of(j * TK, TK)
            kj, vj = k_ref[pl.ds(ks, TK), :], v_ref[pl.ds(ks, TK), :]
            dist, dm = _decay_lat(tpos, ks, lgf, lgb)
            s = _dot_nt(q, kj)
            dsv = _dot_nt(dob, vj)
            dsb = (dsv * dm).astype(BF16)
            dk_ref[pl.ds(ks, TK), :] += _dot_tn(dsb, q)
            dv_ref[pl.ds(ks, TK), :] += _dot_tn((s * dm).astype(BF16), dob)
            xw = s * dsv * dm * jnp.abs(dist)
            tot = jnp.sum(xw, axis=0, keepdims=True)
            fwd = jnp.sum(jnp.where(dist > 0, xw, 0.0), axis=0, keepdims=True)
            dlg_ref[0:1, :] += fwd
            dlg_ref[1:2, :] += tot - fwd
            return dq + _dot(dsb, kj)

        dq = lax.fori_loop(0, L // TK, chunk, jnp.zeros((TQ, RET_DK), F32))
        kc, vc = k_ref[L:L + LC, :], v_ref[L:L + LC, :]
        dfc, dbc, ef, eb = _decay_ctx(tpos, L, LC, lgf, lgb)
        s = _dot_nt(q, kc)
        dsv = _dot_nt(dob, vc)
        dsb = (dsv * (ef + eb)).astype(BF16)
        dk_ref[L:L + LC, :] += _dot_tn(dsb, q)
        dv_ref[L:L + LC, :] += _dot_tn((s * (ef + eb)).astype(BF16), dob)
        a = s * dsv
        dlg_ref[0:1, 0:LC] += jnp.sum(a * ef * dfc, axis=0, keepdims=True)
        dlg_ref[1:2, 0:LC] += jnp.sum(a * eb * dbc, axis=0, keepdims=True)
        dq = dq + _dot(dsb, kc)
        cs, sn = cos_ref[pl.ds(pl.multiple_of(i * TQ, TQ), TQ), :], sin_ref[pl.ds(pl.multiple_of(i * TQ, TQ), TQ), :]
        dq_ref[...] = (dq * cs - pltpu.roll(dq, 64, 1) * sn).astype(BF16)

        @pl.when(i == ni - 1)
        def _():
            dk = dk_ref[...]
            dk_ref[...] = (dk * cos_ref[...] - pltpu.roll(dk, 64, 1) * sin_ref[...]) * kscale

    tile = pl.BlockSpec((None, TQ, 128), lambda b, h, i: (b, i, h))
    kv_out = pl.BlockSpec((None, T, 128), lambda b, h, i: (b, 0, h))
    tab = pl.BlockSpec((T, RET_DK), lambda b, h, i: (0, 0))
    return pl.pallas_call(
        body, name="ret_bwd", grid=(B, 4, ni),
        in_specs=[dec_spec, dec_spec, q_spec, k_spec, v_spec, g_spec, gn_spec, tile,
                  pl.BlockSpec((None, TQ, 128), lambda b, h, i: (b, i, 4 + h)), tab, tab],
        out_specs=(tile, tile, kv_out, kv_out,
                   pl.BlockSpec((None, 1, 128), lambda b, h, i: (b, 0, h)),
                   pl.BlockSpec((None, None, 8, TK), lambda b, h, i: (b, h, 0, 0))),
        out_shape=(jax.ShapeDtypeStruct((B, L, 512), BF16), jax.ShapeDtypeStruct((B, L, 512), BF16),
                   jax.ShapeDtypeStruct((B, T, 512), F32), jax.ShapeDtypeStruct((B, T, 512), F32),
                   jax.ShapeDtypeStruct((B, 1, 512), F32), jax.ShapeDtypeStruct((B, 4, 8, TK), F32)),
        compiler_params=_params(("arbitrary",) * 3))(
            dec_f, dec_b, P, P, P, P, ret_norm_g, o_ret, dY, cos2, sin2)


def _out_call(y_na, y_ret, x, target, mod, final_g, wout_f):
    B, L, _ = x.shape

    def body(yn_ref, yr_ref, x_ref, t_ref, mod_ref, gf_ref, w_ref, dy_ref, dx2_ref, dw_ref, sm_ref):
        b, i = pl.program_id(0), pl.program_id(1)

        @pl.when((b == 0) & (i == 0))
        def _():
            dw_ref[...] = jnp.zeros_like(dw_ref)
            sm_ref[...] = jnp.zeros_like(sm_ref)

        gate = mod_ref[pl.ds(b, 1), 2 * D:3 * D]
        gf = gf_ref[...]
        yn, yr = yn_ref[...], yr_ref[...]
        ylat = _dot(yn, w_ref[0:512, :]) + _dot(yr, w_ref[512:1024, :])
        x2 = x_ref[...] + gate * ylat
        r = lax.rsqrt(jnp.mean(x2 * x2, axis=-1, keepdims=True) + EPS)
        xr = x2 * r
        err = xr * gf - t_ref[...]
        sm_ref[1:2, :] += jnp.sum(err * err, axis=0, keepdims=True)
        dout = err * (1.0 / D)
        sm_ref[0:1, :] += jnp.sum(dout * xr, axis=0, keepdims=True)
        gd = dout * gf
        dx2 = r * (gd - xr * jnp.mean(gd * xr, axis=-1, keepdims=True))
        dx2_ref[...] = dx2
        sm_ref[pl.ds(2 + b, 1), :] += jnp.sum(dx2 * ylat, axis=0, keepdims=True)
        dyl = (gate * dx2).astype(BF16)
        dy_ref[:, 0:512] = _dot_nt(dyl, w_ref[0:512, :]).astype(BF16)
        dy_ref[:, 512:1024] = _dot_nt(dyl, w_ref[512:1024, :]).astype(BF16)
        dw_ref[0:512, :] += _dot_tn(yn, dyl)
        dw_ref[512:1024, :] += _dot_tn(yr, dyl)

    half = pl.BlockSpec((None, TQ, 512), lambda b, i: (b, i, 0))
    full = pl.BlockSpec((None, TQ, D), lambda b, i: (b, i, 0))
    return pl.pallas_call(
        body, name="out_proj_loss", grid=(B, L // TQ),
        in_specs=[half, half, full, full,
                  pl.BlockSpec((8, 3 * D), lambda b, i: (0, 0)),
                  pl.BlockSpec((1, D), lambda b, i: (0, 0)),
                  pl.BlockSpec((D, D), lambda b, i: (0, 0))],
        out_specs=(full, full, pl.BlockSpec((D, D), lambda b, i: (0, 0)),
                   pl.BlockSpec((8, D), lambda b, i: (0, 0))),
        out_shape=(jax.ShapeDtypeStruct((B, L, D), BF16), jax.ShapeDtypeStruct((B, L, D), F32),
                   jax.ShapeDtypeStruct((D, D), F32), jax.ShapeDtypeStruct((8, D), F32)),
        compiler_params=_params(("arbitrary",) * 2))(y_na, y_ret, x, target, mod, final_g, wout_f)


def _dh_call(dsec, win_f, x, ctx, dx2, mod, norm_g):
    B, L, _ = x.shape
    LC = ctx.shape[1]
    nl = L // TQ

    def body(d0, d1, d2, d3, d4, d5, d6, d7, w_ref, x_ref, ctx_ref, dx2_ref, mod_ref, g_ref,
             gx_ref, sm_ref):
        drefs = (d0, d1, d2, d3, d4, d5, d6, d7)
        b, t = pl.program_id(0), pl.program_id(1)
        is_lat = t < nl

        @pl.when((b == 0) & (t == 0))
        def _():
            sm_ref[...] = jnp.zeros_like(sm_ref)

        def dh_of(secs):
            acc = jnp.zeros((TQ, D), F32)
            for sec in secs:
                s, half = divmod(sec, 2)
                acc = acc + _dot_nt(drefs[sec][...].astype(BF16), w_ref[s, :, half * 512:(half + 1) * 512])
            return acc

        def norm_bwd(dh, xt, mrow):
            scale = mrow[:, D:2 * D]
            g = g_ref[...]
            rstd = lax.rsqrt(jnp.mean(xt * xt, axis=-1, keepdims=True) + EPS)
            xn = xt * rstd
            dshift = jnp.sum(dh, axis=0, keepdims=True)
            dscale = jnp.sum(dh * (xn * g), axis=0, keepdims=True)
            dhn = dh * (1.0 + scale)
            sm_ref[0:1, :] += jnp.sum(dhn * xn, axis=0, keepdims=True)
            dxn = dhn * g
            dx = rstd * (dxn - xn * jnp.mean(dxn * xn, axis=-1, keepdims=True))
            return dshift, dscale, dx

        @pl.when(is_lat)
        def _():
            dshift, dscale, dx = norm_bwd(dh_of(range(8)), x_ref[...], mod_ref[pl.ds(b, 1), :])
            sm_ref[pl.ds(3 + b, 1), :] += dshift
            sm_ref[pl.ds(3 + B + b, 1), :] += dscale
            gx_ref[...] = dx2_ref[...] + dx

        @pl.when(jnp.logical_not(is_lat))
        def _():
            dshift, dscale, _ = norm_bwd(dh_of((1, 2, 5, 6)), ctx_ref[...], mod_ref[B:B + 1, :])
            sm_ref[1:2, :] += dshift
            sm_ref[2:3, :] += dscale

    lat = lambda b, t: (b, jnp.minimum(t, nl - 1), 0)
    tok = lambda b, t: (b, t, 0)
    sec_specs = [pl.BlockSpec((None, TQ, 512), lat if sec in (0, 3, 4, 7) else tok) for sec in range(8)]
    return pl.pallas_call(
        body, name="dh_norm_bwd", grid=(B, nl + 1),
        in_specs=sec_specs + [
            pl.BlockSpec((N_SHARD, D, D), lambda b, t: (0, 0, 0)),
            pl.BlockSpec((None, TQ, D), lat),
            pl.BlockSpec((None, LC, D), lambda b, t: (b, 0, 0)),
            pl.BlockSpec((None, TQ, D), lat),
            pl.BlockSpec((8, 3 * D), lambda b, t: (0, 0)),
            pl.BlockSpec((1, D), lambda b, t: (0, 0))],
        out_specs=(pl.BlockSpec((None, TQ, D), lat), pl.BlockSpec((8, D), lambda b, t: (0, 0))),
        out_shape=(jax.ShapeDtypeStruct((B, L, D), F32), jax.ShapeDtypeStruct((8, D), F32)),
        compiler_params=_params(("arbitrary",) * 2))(*dsec, win_f, x, ctx, dx2, mod, norm_g)


def _dw_call(dsec, h, L):
    B, T, _ = h.shape
    nl = L // TQ

    def body(d0, d1, d2, d3, d4, d5, d6, d7, h_ref, dw_ref, acc_ref):
        drefs = (d0, d1, d2, d3, d4, d5, d6, d7)
        b, t = pl.program_id(0), pl.program_id(1)

        @pl.when((b == 0) & (t == 0))
        def _():
            acc_ref[...] = jnp.zeros_like(acc_ref)

        hb = h_ref[...]

        def add(secs):
            for sec in secs:
                s, half = divmod(sec, 2)
                acc_ref[s, :, half * 512:(half + 1) * 512] += _dot_tn(hb, drefs[sec][...].astype(BF16))

        @pl.when(t < nl)
        def _():
            add(range(8))

        @pl.when(t >= nl)
        def _():
            add((1, 2, 5, 6))

        @pl.when((b == B - 1) & (t == nl))
        def _():
            dw_ref[...] = acc_ref[...].astype(BF16)

    lat = lambda b, t: (b, jnp.minimum(t, nl - 1), 0)
    tok = lambda b, t: (b, t, 0)
    sec_specs = [pl.BlockSpec((None, TQ, 512), lat if sec in (0, 3, 4, 7) else tok) for sec in range(8)]
    return pl.pallas_call(
        body, name="dw_in", grid=(B, nl + 1),
        in_specs=sec_specs + [pl.BlockSpec((None, TQ, D), tok)],
        out_specs=pl.BlockSpec((N_SHARD, D, D), lambda b, t: (0, 0, 0)),
        out_shape=jax.ShapeDtypeStruct((N_SHARD, D, D), BF16),
        scratch_shapes=[pltpu.VMEM((N_SHARD, D, D), F32)],
        compiler_params=_params(("arbitrary",) * 2, vmem_mb=56))(*dsec, h)


def _mesh_pos():
    return lax.axis_index("x"), lax.axis_index("y"), lax.axis_index("c")


def _flip(v, f):
    return 1 - v if f else v


def _remote(src, dst, ssem, rsem, k, peer):
    return pltpu.make_async_remote_copy(src_ref=src, dst_ref=dst, send_sem=ssem.at[k], recv_sem=rsem.at[k],
                                        device_id=peer, device_id_type=MESH)


def _other_chips(x, y):
    return [(_flip(x, fx), _flip(y, fy)) for fx, fy in ((1, 0), (0, 1), (1, 1))]


D2D_STREAMS = 8


def _row_chunks(src, dst, ssem, rsem, k, peer, rows, lead=None):
    step = rows // D2D_STREAMS
    out = []
    for r in range(D2D_STREAMS):
        idx = (pl.ds(r * step, step),) if lead is None else (lead, pl.ds(r * step, step))
        out.append(_remote(src.at[idx], dst.at[idx], ssem, rsem, k, peer))
    return out


def _all_to_all_small(src, dst_all, ssem, rsem, k0, x, y, cc):
    me = 4 * x + 2 * y + cc
    sends, recvs = [], []
    for f in range(1, N_DEV):
        px, py, pc = _flip(x, f & 4), _flip(y, f & 2), _flip(cc, f & 1)
        sends.append(_remote(src, dst_all.at[me], ssem, rsem, k0 + f - 1, (px, py, pc)))
        recvs.append(_remote(src, dst_all.at[4 * px + 2 * py + pc], ssem, rsem, k0 + f - 1, (px, py, pc)))
    return sends, recvs


def _finish(local, sends, recvs):
    for cp in recvs:
        cp.wait_recv()
    for cp in sends:
        cp.wait_send()
    for cp in local:
        cp.wait()


def _gather_call(win_b, wout_b, wada_b, c):
    arrs = (win_b, wout_b, wada_b)
    hrs = [a.shape[0] // 2 for a in arrs]

    def body(win, wout, wada, c_ref, win_f, wout_f, wada_f, c_all, ssem, rsem, lsem):
        x, y, cc = _mesh_pos()
        s, me = 2 * x + y, 4 * x + 2 * y + cc
        sib = (x, y, 1 - cc)
        srcs, dsts = (win, wout, wada), (win_f, wout_f, wada_f)

        def half(a, shard, hc):
            return dsts[a].at[shard, pl.ds(hc * hrs[a], hrs[a])]

        local = [pltpu.make_async_copy(srcs[a], dsts[a].at[s], lsem.at[a]) for a in range(3)]
        local.append(pltpu.make_async_copy(c_ref, c_all.at[me], lsem.at[3]))
        ici_send, ici_recv, fwd_send, fwd_recv, fwd_chunks, k = [], [], [], [], [], 0
        for px, py in _other_chips(x, y):
            ps = 2 * px + py
            for a in range(3):
                mine = srcs[a].at[pl.ds(cc * hrs[a], hrs[a])]
                ici_send.append(_remote(mine, half(a, s, cc), ssem, rsem, k, (px, py, cc)))
                ici_recv.append(_remote(mine, half(a, ps, cc), ssem, rsem, k, (px, py, cc)))
                fwd_send.append(_remote(half(a, ps, cc), half(a, ps, cc), ssem, rsem, 9 + k, sib))
                fwd_recv.append(_remote(half(a, ps, 1 - cc), half(a, ps, 1 - cc), ssem, rsem, 9 + k, sib))
                fwd_chunks.append(_row_chunks(half(a, ps, cc), half(a, ps, cc), ssem, rsem, 9 + k, sib, hrs[a]))
                k += 1
        c_send, c_recv = _all_to_all_small(c_ref, c_all, ssem, rsem, 18, x, y, cc)
        for cp in local + ici_send + c_send:
            cp.start()
        for got, chunks in zip(ici_recv, fwd_chunks):
            got.wait_recv()
            for cp in chunks:
                cp.start()
        _finish(local, ici_send + fwd_send + c_send, fwd_recv + c_recv)

    return pl.pallas_call(
        body, name="weight_gather",
        in_specs=[ANY] * 4, out_specs=(ANY,) * 4,
        out_shape=tuple(jax.ShapeDtypeStruct((N_SHARD,) + a.shape, a.dtype) for a in arrs)
        + (jax.ShapeDtypeStruct((N_DEV,) + c.shape, c.dtype),),
        scratch_shapes=[pltpu.SemaphoreType.DMA((25,)), pltpu.SemaphoreType.DMA((25,)),
                        pltpu.SemaphoreType.DMA((4,))],
        )(win_b, wout_b, wada_b, c)


def _grad_halves_call(dwin_b, dwout_b):
    arrs = (dwin_b, dwout_b)
    hrs = [a.shape[1] // 2 for a in arrs]

    def body(din, dout, own_in, got_in, own_out, got_out, ssem, rsem, lsem):
        x, y, cc = _mesh_pos()
        sib = (x, y, 1 - cc)
        srcs, owns, gots = (din, dout), (own_in, own_out), (got_in, got_out)
        local = [pltpu.make_async_copy(srcs[a].at[:, pl.ds(cc * hrs[a], hrs[a])], owns[a], lsem.at[a])
                 for a in range(2)]
        theirs = [srcs[a].at[:, pl.ds((1 - cc) * hrs[a], hrs[a])] for a in range(2)]
        sends = [_remote(theirs[a], gots[a], ssem, rsem, a, sib) for a in range(2)]
        for cp in local:
            cp.start()
        for a in range(2):
            for j in range(N_SHARD):
                for cp in _row_chunks(theirs[a], gots[a], ssem, rsem, a, sib, hrs[a], lead=j):
                    cp.start()
        _finish(local, sends, sends)

    shapes = tuple(jax.ShapeDtypeStruct((N_SHARD, hrs[a], arrs[a].shape[2]), arrs[a].dtype) for a in range(2))
    return pl.pallas_call(
        body, name="grad_halves",
        in_specs=[ANY] * 2, out_specs=(ANY,) * 4,
        out_shape=(shapes[0], shapes[0], shapes[1], shapes[1]),
        scratch_shapes=[pltpu.SemaphoreType.DMA((2,)), pltpu.SemaphoreType.DMA((2,)),
                        pltpu.SemaphoreType.DMA((2,))],
        )(dwin_b, dwout_b)


def _grad_ici_call(cp_in, cp_out, small):
    def body(cin, cout, sm, rin, rout, sm_all, ssem, rsem, lsem):
        x, y, cc = _mesh_pos()
        s, me = 2 * x + y, 4 * x + 2 * y + cc
        srcs, dsts = (cin, cout), (rin, rout)
        local = [pltpu.make_async_copy(srcs[a].at[s], dsts[a].at[s], lsem.at[a]) for a in range(2)]
        local.append(pltpu.make_async_copy(sm, sm_all.at[me], lsem.at[2]))
        sends, recvs, k = [], [], 0
        for px, py in _other_chips(x, y):
            ps = 2 * px + py
            for a in range(2):
                sends.append(_remote(srcs[a].at[ps], dsts[a].at[s], ssem, rsem, k, (px, py, cc)))
                recvs.append(_remote(srcs[a].at[s], dsts[a].at[ps], ssem, rsem, k, (px, py, cc)))
                k += 1
        sm_send, sm_recv = _all_to_all_small(sm, sm_all, ssem, rsem, 6, x, y, cc)
        for cp in local + sends + sm_send:
            cp.start()
        _finish(local, sends + sm_send, recvs + sm_recv)

    return pl.pallas_call(
        body, name="grad_scatter",
        in_specs=[ANY] * 3, out_specs=(ANY,) * 3,
        out_shape=(jax.ShapeDtypeStruct(cp_in.shape, cp_in.dtype), jax.ShapeDtypeStruct(cp_out.shape, cp_out.dtype),
                   jax.ShapeDtypeStruct((N_DEV,) + small.shape, F32)),
        scratch_shapes=[pltpu.SemaphoreType.DMA((13,)), pltpu.SemaphoreType.DMA((13,)),
                        pltpu.SemaphoreType.DMA((3,))],
        )(cp_in, cp_out, small)


def _grad_sibling_call(hin, hout):
    def body(hin_ref, hout_ref, gin, gout, ssem, rsem, lsem):
        x, y, cc = _mesh_pos()
        sib = (x, y, 1 - cc)
        srcs, dsts = (hin_ref, hout_ref), (gin, gout)
        local = [pltpu.make_async_copy(srcs[a], dsts[a].at[cc], lsem.at[a]) for a in range(2)]
        sends = [_remote(srcs[a], dsts[a].at[cc], ssem, rsem, a, sib) for a in range(2)]
        recvs = [_remote(srcs[a], dsts[a].at[1 - cc], ssem, rsem, a, sib) for a in range(2)]
        for cp in local:
            cp.start()
        for a in range(2):
            for cp in _row_chunks(srcs[a], dsts[a].at[cc], ssem, rsem, a, sib, srcs[a].shape[0]):
                cp.start()
        _finish(local, sends, recvs)

    return pl.pallas_call(
        body, name="grad_sibling",
        in_specs=[ANY] * 2, out_specs=(ANY,) * 2,
        out_shape=(jax.ShapeDtypeStruct((2,) + hin.shape, F32), jax.ShapeDtypeStruct((2,) + hout.shape, F32)),
        scratch_shapes=[pltpu.SemaphoreType.DMA((2,)), pltpu.SemaphoreType.DMA((2,)),
                        pltpu.SemaphoreType.DMA((2,))],
        )(hin, hout)


def _sum_slots_call(r, name):
    _, R, C = r.shape
    tr = min(R, 256)

    def body(r_ref, o_ref):
        o_ref[...] = ((r_ref[0].astype(F32) + r_ref[1].astype(F32)) + r_ref[2].astype(F32)) + r_ref[3].astype(F32)

    return pl.pallas_call(
        body, name=name, grid=(R // tr,),
        in_specs=[pl.BlockSpec((N_SHARD, tr, C), lambda i: (0, i, 0))],
        out_specs=pl.BlockSpec((tr, C), lambda i: (i, 0)),
        out_shape=jax.ShapeDtypeStruct((R, C), F32),
        compiler_params=_params(("arbitrary",)))(r)


def _add_call(xs, name, out_dtype):
    R, C = xs[0].shape
    tr = min(R, 512)
    n = len(xs)

    def body(*refs):
        acc = refs[0][...].astype(F32)
        for r in refs[1:n]:
            acc = acc + r[...].astype(F32)
        refs[n][...] = acc.astype(out_dtype)

    spec = pl.BlockSpec((tr, C), lambda i: (i, 0))
    return pl.pallas_call(
        body, name=name, grid=(R // tr,), in_specs=[spec] * n, out_specs=spec,
        out_shape=jax.ShapeDtypeStruct((R, C), out_dtype),
        compiler_params=_params(("arbitrary",)))(*xs)


def _adamw(w, g, m, v):
    m = ADAM_B1 * m + (1.0 - ADAM_B1) * g
    v = ADAM_B2 * v + (1.0 - ADAM_B2) * (g * g)
    m_hat = m / (1.0 - ADAM_B1 ** ADAM_STEP)
    v_hat = v / (1.0 - ADAM_B2 ** ADAM_STEP)
    return -ADAM_LR * (m_hat / (jnp.sqrt(v_hat) + ADAM_EPS) + ADAM_WD * w), m, v


def _adam_call(w, m, v, g, name):
    R, C = w.shape
    tr = 256

    def body(w_ref, m_ref, v_ref, g_ref, d_ref, mo_ref, vo_ref):
        d_ref[...], mo_ref[...], vo_ref[...] = _adamw(w_ref[...], g_ref[...], m_ref[...], v_ref[...])

    spec = pl.BlockSpec((tr, C), lambda i: (i, 0))
    return pl.pallas_call(
        body, name=name, grid=(R // tr,), in_specs=[spec] * 4,
        out_specs=(spec,) * 3, out_shape=(jax.ShapeDtypeStruct((R, C), F32),) * 3,
        compiler_params=_params(("arbitrary",)))(w, m, v, g)


R_GF, R_NG, R_LOSS, R_RNG, R_LGF, R_LGB, R_SHIFT, R_SCALE, R_GATE, R_SHIFT_C, R_SCALE_C, R_RNG2, R_RPB = (
    0, 1, 2, 3, 4, 5, 6, 8, 10, 12, 13, 14, 16)
W_GF, W_NG, W_CCTX, W_RNG, W_DF, W_DB, W_BADA, W_RPB = 0, 1, 2, 3, 4, 5, 6, 9


def _small_final_call(sm_all, c_t, c_ctx, wada_f, wada, m_ada, v_ada, wsm, msm, vsm, B):
    ws = wada.shape[1]
    NB = N_DEV * B

    def body(sm_ref, ct_ref, cctx_ref, wf_ref, wa_ref, ma_ref, va_ref, w_ref, m_ref, v_ref,
             g_ref, d_ref, mo_ref, vo_ref, ga_ref, da_ref, mao_ref, vao_ref, loss_ref, dmod_ref):
        x, y, _ = _mesh_pos()
        s = 2 * x + y
        tot = sm_ref[0]
        for dv in range(1, N_DEV):
            tot = tot + sm_ref[dv]
        w = w_ref[...]
        for dv in range(N_DEV):
            for b in range(B):
                r = dv * B + b
                for part, row in enumerate((R_SHIFT, R_SCALE, R_GATE)):
                    dmod_ref[r:r + 1, part * D:(part + 1) * D] = sm_ref[dv, row + b:row + b + 1, :]
        dmod_ref[NB:NB + 1, 0:D] = tot[R_SHIFT_C:R_SHIFT_C + 1, :]
        dmod_ref[NB:NB + 1, D:2 * D] = tot[R_SCALE_C:R_SCALE_C + 1, :]
        dmod_ref[NB:NB + 1, 2 * D:3 * D] = jnp.zeros((1, D), F32)
        dmod_ref[NB + 1:, :] = jnp.zeros((dmod_ref.shape[0] - NB - 1, 3 * D), F32)
        dmod = dmod_ref[...]
        cc = cctx_ref[...]
        scc = _sigmoid(cc)
        ct = ct_ref[...]
        act_t = ct * _sigmoid(ct)
        dmc = dmod[NB:NB + 1, :].astype(BF16)
        dact = jnp.zeros((1, D), F32)
        for sh in range(N_SHARD):
            dact = dact + _dot_nt(dmc[:, sh * ws:(sh + 1) * ws], wf_ref[sh])
        g = jnp.zeros((16, D), F32)
        rows = lax.broadcasted_iota(jnp.int32, (16, D), 0)

        def put(g, row, val):
            return jnp.where(rows == row, val, g)

        g = put(g, W_GF, tot[R_GF:R_GF + 1, :])
        g = put(g, W_NG, tot[R_NG:R_NG + 1, :])
        g = put(g, W_CCTX, dact * (scc * (1.0 + cc * (1.0 - scc))))
        g = put(g, W_RNG, tot[R_RNG:R_RNG + 1, :] + tot[R_RNG2:R_RNG2 + 1, :])
        g = put(g, W_DF, tot[R_LGF:R_LGF + 1, :] * (-jnp.exp(w[W_DF:W_DF + 1, :])))
        g = put(g, W_DB, tot[R_LGB:R_LGB + 1, :] * (-jnp.exp(w[W_DB:W_DB + 1, :])))
        db = jnp.sum(dmod, axis=0, keepdims=True)
        for part in range(3):
            g = put(g, W_BADA + part, db[:, part * D:(part + 1) * D])
        for part in range(4):
            g = put(g, W_RPB + part, tot[R_RPB + part:R_RPB + part + 1, :])
        g_ref[...] = g
        d_ref[...], mo_ref[...], vo_ref[...] = _adamw(w, g, m_ref[...], v_ref[...])
        loss_ref[...] = jnp.broadcast_to(
            (0.5 / D) * jnp.sum(tot[R_LOSS:R_LOSS + 1, :], axis=1, keepdims=True), (8, 128))
        for sh in range(N_SHARD):
            @pl.when(s == sh)
            def _():
                ga = jnp.dot(act_t, dmod[:, sh * ws:(sh + 1) * ws], precision=HIGHEST,
                             preferred_element_type=F32)
                ga_ref[...] = ga
                da_ref[...], mao_ref[...], vao_ref[...] = _adamw(wa_ref[...], ga, ma_ref[...], va_ref[...])

    sh_small = jax.ShapeDtypeStruct((16, D), F32)
    sh_ada = jax.ShapeDtypeStruct(wada.shape, F32)
    return pl.pallas_call(
        body, name="small_final",
        out_shape=(sh_small,) * 4 + (sh_ada,) * 4 + (jax.ShapeDtypeStruct((8, 128), F32),),
        scratch_shapes=[pltpu.VMEM((NB + 8, 3 * D), F32)],
        compiler_params=_params(vmem_mb=56))(
            sm_all, c_t, c_ctx, wada_f, wada, m_ada, v_ada, wsm, msm, vsm)


def _local_step(x, c, ctx, c_ctx, norm_g, wada_f, b_ada, win_f, na_rpb, dec_f, dec_b, ret_norm_g,
                wout_f, final_g, target):
    B, L, _ = x.shape
    LC = ctx.shape[1]
    assert B == 2
    cos2, sin2 = _rope_tables(L, LC)
    c8 = jnp.concatenate([c, c_ctx[None, :], jnp.zeros((8 - B - 1, D), F32)], axis=0)
    mod = _mod_call(c8, wada_f, b_ada)
    bias = _bias_call(na_rpb.reshape(na_rpb.shape[0], -1))
    P, h = _inproj_call(x, ctx, mod, norm_g, win_f, cos2, sin2)
    y_na = _na_fwd_call(P, bias, L, LC)
    y_ret, o_ret = _ret_fwd_call(P, dec_f, dec_b, ret_norm_g, L, LC)
    dY, dx2, dwout_p, sm_out = _out_call(y_na, y_ret, x, target, mod, final_g, wout_f.reshape(D, D))
    dnq, dng, dnk, dnv, dbias = _na_bwd_call(P, bias, dY, L, LC)
    drq, drg, drk, drv, dgn, dlg = _ret_bwd_call(P, dec_f, dec_b, ret_norm_g, o_ret, dY, cos2, sin2, L, LC)
    dsec = (dnq, dnk, dnv, dng, drq, drk, drv, drg)
    grad_x, sm_dh = _dh_call(dsec, win_f, x, ctx, dx2, mod, norm_g)
    dwin_p = _dw_call(dsec, h, L)
    drpb, dlg_sum = _small_reduce_call(dbias, dlg, B)
    z = jnp.zeros((1, D), F32)
    pad = lambda v: jnp.pad(v.reshape(1, -1), ((0, 0), (0, D - v.size)))
    dlg_sum = dlg_sum.reshape(4, 8, 128)
    rpb_rows = jnp.pad(drpb[:, :15, :31].reshape(-1), (0, 4 * D - drpb.shape[0] * 465)).reshape(4, D)
    small = jnp.concatenate([
        sm_out[0:1], sm_dh[0:1], sm_out[1:2], pad(dgn[0]), pad(dlg_sum[:, 0, 0]), pad(dlg_sum[:, 1, 0]),
        sm_dh[3:5], sm_dh[5:7], sm_out[2:4], sm_dh[1:2], sm_dh[2:3], pad(dgn[1]), z, rpb_rows,
        jnp.zeros((SM_ROWS - 20, D), F32)], axis=0)
    return grad_x, dwin_p, dwout_p, small


def kernel(x, c, ctx, c_ctx, norm_g, w_ada, b_ada, w_in, na_rpb, ret_decay_fwd, ret_decay_bwd, ret_norm_g, w_out, final_norm_g, loss_target, m_c_ctx, m_norm_g, m_w_ada, m_b_ada, m_w_in, m_na_rpb, m_ret_decay_fwd, m_ret_decay_bwd, m_ret_norm_g, m_w_out, m_final_norm_g, v_c_ctx, v_norm_g, v_w_ada, v_b_ada, v_w_in, v_na_rpb, v_ret_decay_fwd, v_ret_decay_bwd, v_ret_norm_g, v_w_out, v_final_norm_g):
    B = x.shape[0]
    win_f, wout_f, wada_f, c_all = _gather_call(w_in[0].astype(BF16), w_out[0].astype(BF16),
                                                w_ada[0].astype(BF16), c)
    grad_x, dwin_p, dwout_p, small = _local_step(
        x, c, ctx, c_ctx, norm_g, wada_f, b_ada, win_f, na_rpb[0], ret_decay_fwd, ret_decay_bwd,
        ret_norm_g, wout_f, final_norm_g.reshape(1, D), loss_target)
    own_in, got_in, own_out, got_out = _grad_halves_call(
        dwin_p, dwout_p.astype(BF16).reshape(N_SHARD, D // N_SHARD, D))
    flat = lambda a: a.reshape(-1, a.shape[-1])
    cp_in = _add_call([flat(own_in), flat(got_in)], "chip_dw_in", BF16).reshape(own_in.shape)
    cp_out = _add_call([flat(own_out), flat(got_out)], "chip_dw_out", BF16).reshape(own_out.shape)
    rin, rout, sm_all = _grad_ici_call(cp_in, cp_out, small)
    gin, gout = _grad_sibling_call(_sum_slots_call(rin, "sum_dw_in"), _sum_slots_call(rout, "sum_dw_out"))
    g_win, g_wout = gin.reshape(w_in.shape[1:]), gout.reshape(w_out.shape[1:])
    d_win, nm_win, nv_win = _adam_call(w_in[0], m_w_in[0], v_w_in[0], g_win, "adam_w_in")
    d_wout, nm_wout, nv_wout = _adam_call(w_out[0], m_w_out[0], v_w_out[0], g_wout, "adam_w_out")

    def pack(gf, ng, cc, rng, df, db, bada, rpb):
        pad = lambda v: jnp.pad(v.reshape(1, -1), ((0, 0), (0, D - v.size)))
        return jnp.concatenate([
            gf.reshape(1, D), ng.reshape(1, D), cc.reshape(1, D), pad(rng), pad(df), pad(db),
            bada.reshape(3, D), jnp.pad(rpb.reshape(-1), (0, 4 * D - rpb.size)).reshape(4, D),
            jnp.zeros((3, D), F32)], axis=0)

    wsm = pack(final_norm_g, norm_g, c_ctx, ret_norm_g, ret_decay_fwd, ret_decay_bwd, b_ada, na_rpb)
    msm = pack(m_final_norm_g, m_norm_g, m_c_ctx, m_ret_norm_g, m_ret_decay_fwd, m_ret_decay_bwd, m_b_ada, m_na_rpb)
    vsm = pack(v_final_norm_g, v_norm_g, v_c_ctx, v_ret_norm_g, v_ret_decay_fwd, v_ret_decay_bwd, v_b_ada, v_na_rpb)
    c_t = jnp.concatenate([c_all.reshape(N_DEV * B, D), c_ctx.reshape(1, D), jnp.zeros((7, D), F32)], axis=0).T
    outs = _small_final_call(sm_all, c_t, c_ctx.reshape(1, D), wada_f,
                             w_ada[0], m_w_ada[0], v_w_ada[0], wsm, msm, vsm, B)
    smalls, adas, loss = outs[0:4], outs[4:8], outs[8][0, 0]

    def unpack(p):
        rw = ret_norm_g.shape[1]
        return dict(
            final_norm_g=p[W_GF], norm_g=p[W_NG:W_NG + 1], c_ctx=p[W_CCTX], ret_norm_g=p[W_RNG:W_RNG + 1, :rw],
            ret_decay_fwd=p[W_DF:W_DF + 1, :4], ret_decay_bwd=p[W_DB:W_DB + 1, :4],
            b_ada=p[W_BADA:W_BADA + 3].reshape(1, 3 * D),
            na_rpb=p[W_RPB:W_RPB + 4].reshape(-1)[:na_rpb.size].reshape(na_rpb.shape))

    res = []
    for p, ada, win_o, wout_o in zip(smalls, adas, (g_win, d_win, nm_win, nv_win),
                                     (g_wout, d_wout, nm_wout, nv_wout)):
        u = unpack(p)
        res.append([u["c_ctx"], u["norm_g"], ada[None], u["b_ada"], win_o[None], u["na_rpb"],
                    u["ret_decay_fwd"], u["ret_decay_bwd"], u["ret_norm_g"], wout_o[None], u["final_norm_g"]])
    return (loss, grad_x, *res[0], *res[1], *res[2], *res[3])
```

```python
import functools

import numpy as np
import jax
import jax.numpy as jnp
from jax import lax
from jax.experimental import pallas as pl
from jax.experimental.pallas import tpu as pltpu

F32 = jnp.float32
BF16 = jnp.bfloat16
HIGHEST = lax.Precision.HIGHEST

D = 1024
GRID_W = 64
NA_DH = 64
RET_DK = 128
ROPE_BASE = 10000.0
EPS = 1e-6
NEG = -1e30
TQ = 256
TK = 512
KW = 12 * GRID_W
N_SHARD = 4
N_DEV = 8
SM_ROWS = 24

ADAM_LR = 0.001
ADAM_B1 = 0.9
ADAM_B2 = 0.999
ADAM_EPS = 1e-08
ADAM_WD = 0.01
ADAM_STEP = 10

MESH = pl.DeviceIdType.MESH
ANY = pl.BlockSpec(memory_space=pl.ANY)


def _params(sem=None, vmem_mb=48):
    return pltpu.CompilerParams(dimension_semantics=sem, vmem_limit_bytes=vmem_mb << 20)


def _dot(a, b):
    return jnp.dot(a, b, preferred_element_type=F32)


def _dot_nt(a, b):
    return lax.dot_general(a, b, (((1,), (1,)), ((), ())), preferred_element_type=F32)


def _dot_tn(a, b):
    return lax.dot_general(a, b, (((0,), (0,)), ((), ())), preferred_element_type=F32)


def _sigmoid(x):
    return 1.0 / (1.0 + jnp.exp(-x))


def _rope_tables(L, LC):
    half = RET_DK // 2
    nf = half // 2
    t = np.arange(L)
    row = (t // GRID_W).astype(np.float32)
    col = (t % GRID_W).astype(np.float32)
    inv = (np.float32(ROPE_BASE) ** (-np.arange(nf, dtype=np.float32) / np.float32(nf))).astype(np.float32)
    ang = np.concatenate([row[:, None] * inv, col[:, None] * inv], axis=-1).astype(np.float32)
    cos, sin = np.cos(ang).astype(np.float32), np.sin(ang).astype(np.float32)
    cos2 = np.concatenate([cos, cos], axis=-1)
    sin2 = np.concatenate([-sin, sin], axis=-1)
    cos2 = np.concatenate([cos2, np.ones((LC, RET_DK), np.float32)], axis=0)
    sin2 = np.concatenate([sin2, np.zeros((LC, RET_DK), np.float32)], axis=0)
    return jnp.asarray(cos2), jnp.asarray(sin2)


def _mod_call(c8, wada_f, b_ada):
    ws = wada_f.shape[2]

    def body(c_ref, w_ref, b_ref, o_ref):
        a = c_ref[...]
        a = (a * _sigmoid(a)).astype(BF16)
        for s in range(N_SHARD):
            o_ref[:, s * ws:(s + 1) * ws] = _dot(a, w_ref[s]) + b_ref[:, s * ws:(s + 1) * ws]

    return pl.pallas_call(
        body, name="ada_mod", out_shape=jax.ShapeDtypeStruct((8, 3 * D), F32),
        compiler_params=_params())(c8, wada_f, b_ada)


def _dc_masks():
    cq = lax.broadcasted_iota(jnp.int32, (GRID_W, GRID_W), 0)
    ck = lax.broadcasted_iota(jnp.int32, (GRID_W, GRID_W), 1)
    dc = jnp.clip(ck - cq + 15, 0, 30)
    c0 = jnp.clip(cq - 8, 0, GRID_W - 16)
    col_ok = (ck >= c0) & (ck < c0 + 16)
    return dc, col_ok


def _bias_blocks():
    out = []
    for typ, delta in enumerate((4, 0, -4)):
        for rq in range(4):
            for rkk in range(12):
                dr = rkk + delta - rq - 4
                if typ == 0:
                    ok = -rq <= dr <= 7 - rq
                elif typ == 1:
                    ok = -4 <= dr <= 3
                else:
                    ok = -4 - rq <= dr <= 3 - rq
                out.append((typ, rq, rkk, dr if ok else None))
    return out


def _bias_call(rpb_flat):
    nh = rpb_flat.shape[0]

    def body(r_ref, bias_ref, et_ref):
        dc, col_ok = _dc_masks()
        masks = [(dc == j).astype(F32) for j in range(31)]

        def per_h(h, carry):
            for dr in range(15):
                t = jnp.zeros((GRID_W, GRID_W), F32)
                for j in range(31):
                    t = t + masks[j] * r_ref[h, dr * 31 + j]
                et_ref[dr] = jnp.where(col_ok, t, NEG)
            neg = jnp.full((GRID_W, GRID_W), NEG, F32)
            for typ, rq, rkk, dr in _bias_blocks():
                blk = neg if dr is None else et_ref[dr + 7]
                bias_ref[h, typ, rq * 64:(rq + 1) * 64, rkk * 64:(rkk + 1) * 64] = blk
            return carry

        lax.fori_loop(0, nh, per_h, 0)

    return pl.pallas_call(
        body, name="rpb_bias",
        out_shape=jax.ShapeDtypeStruct((nh, 3, TQ, KW), F32),
        in_specs=[pl.BlockSpec(memory_space=pltpu.SMEM)],
        out_specs=pl.BlockSpec(memory_space=pltpu.VMEM),
        scratch_shapes=[pltpu.VMEM((15, GRID_W, GRID_W), F32)],
        compiler_params=_params())(rpb_flat)


def _small_reduce_call(dbias, dlg, B):
    nh = dbias.shape[0]

    def body(db_ref, dlg_ref, drpb_ref, dlgo_ref, p_ref):
        dc, _ = _dc_masks()
        masks = [(dc == j).astype(F32) for j in range(31)]
        ones = jnp.ones((8, GRID_W), F32)
        p_ref[...] = jnp.zeros_like(p_ref)
        drpb_ref[...] = jnp.zeros_like(drpb_ref)

        def per_h(h, carry):
            acc = {}
            for typ, rq, rkk, dr in _bias_blocks():
                if dr is None:
                    continue
                blk = db_ref[h, typ, rq * 64:(rq + 1) * 64, rkk * 64:(rkk + 1) * 64]
                acc[dr] = blk if dr not in acc else acc[dr] + blk
            for dr in range(-7, 8):
                t = acc[dr]
                for j in range(31):
                    p_ref[j:j + 1, :] = jnp.sum(t * masks[j], axis=0, keepdims=True)
                red = lax.dot_general(ones, p_ref[...], (((1,), (1,)), ((), ())),
                                      precision=HIGHEST, preferred_element_type=F32)
                drpb_ref[h, dr + 7:dr + 8, :] = red[0:1, :]
            return carry

        lax.fori_loop(0, nh, per_h, 0)
        x = dlg_ref[0]
        for b in range(1, B):
            x = x + dlg_ref[b]
        x = x.reshape(4 * 8, TK)
        dlgo_ref[...] = jnp.dot(x, jnp.ones((TK, 128), F32), precision=HIGHEST,
                                preferred_element_type=F32)

    return pl.pallas_call(
        body, name="small_reduce",
        out_shape=(jax.ShapeDtypeStruct((nh, 16, 32), F32), jax.ShapeDtypeStruct((32, 128), F32)),
        scratch_shapes=[pltpu.VMEM((32, GRID_W), F32)],
        compiler_params=_params())(dbias, dlg)


def _inproj_call(x, ctx, mod, norm_g, win_f, cos2, sin2):
    B, L, _ = x.shape
    LC = ctx.shape[1]
    T = L + LC
    nl = L // TQ
    assert LC == TQ and L % TQ == 0
    kscale = RET_DK ** -0.5

    def body(x_ref, ctx_ref, mod_ref, g_ref, w_ref, cos_ref, sin_ref, p_ref, h_ref):
        b = pl.program_id(0)
        t = pl.program_id(1)
        is_lat = t < nl
        xt = jnp.where(is_lat, x_ref[...], ctx_ref[...])
        mrow = mod_ref[pl.ds(jnp.where(is_lat, b, B), 1), :]
        shift, scale = mrow[:, 0:D], mrow[:, D:2 * D]
        rstd = lax.rsqrt(jnp.mean(xt * xt, axis=-1, keepdims=True) + EPS)
        hb = ((xt * rstd * g_ref[...]) * (1.0 + scale) + shift).astype(BF16)
        h_ref[...] = hb
        cs, sn = cos_ref[...], sin_ref[...]
        for sec in range(8):
            s, half = divmod(sec, 2)
            acc = _dot(hb, w_ref[s, :, half * 512:(half + 1) * 512])
            if sec == 0:
                acc = acc * (NA_DH ** -0.5)
            if sec in (4, 5):
                for j in range(4):
                    a = acc[:, j * 128:(j + 1) * 128]
                    r = a * cs + pltpu.roll(a, 64, 1) * sn
                    if sec == 5:
                        r = r * kscale
                    p_ref[:, sec * 512 + j * 128:sec * 512 + (j + 1) * 128] = r.astype(BF16)
            else:
                p_ref[:, sec * 512:(sec + 1) * 512] = acc.astype(BF16)

    return pl.pallas_call(
        body, name="in_proj", grid=(B, T // TQ),
        in_specs=[
            pl.BlockSpec((None, TQ, D), lambda b, t: (b, jnp.minimum(t, nl - 1), 0)),
            pl.BlockSpec((None, TQ, D), lambda b, t: (b, 0, 0)),
            pl.BlockSpec((8, 3 * D), lambda b, t: (0, 0)),
            pl.BlockSpec((1, D), lambda b, t: (0, 0)),
            pl.BlockSpec((N_SHARD, D, D), lambda b, t: (0, 0, 0)),
            pl.BlockSpec((TQ, RET_DK), lambda b, t: (t, 0)),
            pl.BlockSpec((TQ, RET_DK), lambda b, t: (t, 0)),
        ],
        out_specs=(pl.BlockSpec((None, TQ, 4 * D), lambda b, t: (b, t, 0)),
                   pl.BlockSpec((None, TQ, D), lambda b, t: (b, t, 0))),
        out_shape=(jax.ShapeDtypeStruct((B, T, 4 * D), BF16), jax.ShapeDtypeStruct((B, T, D), BF16)),
        compiler_params=_params(("arbitrary", "arbitrary")))(x, ctx, mod, norm_g, win_f, cos2, sin2)


def _na_specs(L, T, rows):
    nm = rows // 4
    q_spec = pl.BlockSpec((None, TQ, 128), lambda hp, b, m: (b, m, hp))
    k_spec = pl.BlockSpec((None, T, 128), lambda hp, b, m: (b, 0, 4 + hp))
    v_spec = pl.BlockSpec((None, T, 128), lambda hp, b, m: (b, 0, 8 + hp))
    g_spec = pl.BlockSpec((None, TQ, 128), lambda hp, b, m: (b, m, 12 + hp))
    bias_spec = pl.BlockSpec((2, 3, TQ, KW), lambda hp, b, m: (hp, 0, 0, 0))
    return nm, q_spec, k_spec, v_spec, g_spec, bias_spec


def _na_tile(m, nm, rows):
    typ = jnp.where(m == 0, 0, jnp.where(m == nm - 1, 2, 1))
    start = pl.multiple_of(jnp.clip(4 * m - 4, 0, rows - 12) * GRID_W, TQ)
    return typ, start


def _na_fwd_call(P, bias, L, LC):
    B, T, _ = P.shape
    rows = L // GRID_W
    nm, q_spec, k_spec, v_spec, g_spec, bias_spec = _na_specs(L, T, rows)

    def body(q_ref, k_ref, v_ref, g_ref, bias_ref, y_ref):
        typ, start = _na_tile(pl.program_id(2), nm, rows)
        for hh in range(2):
            ln = slice(hh * NA_DH, (hh + 1) * NA_DH)
            q = q_ref[:, ln]
            kw, vw = k_ref[pl.ds(start, KW), ln], v_ref[pl.ds(start, KW), ln]
            kc, vc = k_ref[L:L + LC, ln], v_ref[L:L + LC, ln]
            s1 = _dot_nt(q, kw) + bias_ref[hh, typ]
            s2 = _dot_nt(q, kc)
            mx = jnp.maximum(jnp.max(s1, axis=-1, keepdims=True), jnp.max(s2, axis=-1, keepdims=True))
            p1, p2 = jnp.exp(s1 - mx), jnp.exp(s2 - mx)
            inv = 1.0 / (jnp.sum(p1, axis=-1, keepdims=True) + jnp.sum(p2, axis=-1, keepdims=True))
            o = _dot((p1 * inv).astype(BF16), vw) + _dot((p2 * inv).astype(BF16), vc)
            g = g_ref[:, ln].astype(F32)
            y_ref[:, ln] = (o * (g * _sigmoid(g))).astype(BF16)

    return pl.pallas_call(
        body, name="na_fwd", grid=(4, B, nm),
        in_specs=[q_spec, k_spec, v_spec, g_spec, bias_spec],
        out_specs=pl.BlockSpec((None, TQ, 128), lambda hp, b, m: (b, m, hp)),
        out_shape=jax.ShapeDtypeStruct((B, L, 512), BF16),
        compiler_params=_params(("arbitrary",) * 3))(P, P, P, P, bias)


def _na_bwd_call(P, bias, dY, L, LC):
    B, T, _ = P.shape
    rows = L // GRID_W
    nm, q_spec, k_spec, v_spec, g_spec, bias_spec = _na_specs(L, T, rows)
    scale = NA_DH ** -0.5

    def body(q_ref, k_ref, v_ref, g_ref, bias_ref, dy_ref, dq_ref, dg_ref, dk_ref, dv_ref, db_ref):
        b, m = pl.program_id(1), pl.program_id(2)
        typ, start = _na_tile(m, nm, rows)

        @pl.when(m == 0)
        def _():
            dk_ref[...] = jnp.zeros_like(dk_ref)
            dv_ref[...] = jnp.zeros_like(dv_ref)

        @pl.when((m == 0) & (b == 0))
        def _():
            db_ref[...] = jnp.zeros_like(db_ref)

        for hh in range(2):
            ln = slice(hh * NA_DH, (hh + 1) * NA_DH)
            q = q_ref[:, ln]
            kw, vw = k_ref[pl.ds(start, KW), ln], v_ref[pl.ds(start, KW), ln]
            kc, vc = k_ref[L:L + LC, ln], v_ref[L:L + LC, ln]
            s1 = _dot_nt(q, kw) + bias_ref[hh, typ]
            s2 = _dot_nt(q, kc)
            mx = jnp.maximum(jnp.max(s1, axis=-1, keepdims=True), jnp.max(s2, axis=-1, keepdims=True))
            p1, p2 = jnp.exp(s1 - mx), jnp.exp(s2 - mx)
            inv = 1.0 / (jnp.sum(p1, axis=-1, keepdims=True) + jnp.sum(p2, axis=-1, keepdims=True))
            p1, p2 = p1 * inv, p2 * inv
            p1b, p2b = p1.astype(BF16), p2.astype(BF16)
            o = _dot(p1b, vw) + _dot(p2b, vc)
            g = g_ref[:, ln].astype(F32)
            sg = _sigmoid(g)
            dy = dy_ref[:, ln].astype(F32)
            dg_ref[:, ln] = (dy * o * (sg * (1.0 + g * (1.0 - sg)))).astype(BF16)
            do = (dy * (g * sg)).astype(BF16)
            dp1, dp2 = _dot_nt(do, vw), _dot_nt(do, vc)
            delta = jnp.sum(p1 * dp1, axis=-1, keepdims=True) + jnp.sum(p2 * dp2, axis=-1, keepdims=True)
            ds1, ds2 = p1 * (dp1 - delta), p2 * (dp2 - delta)
            db_ref[hh, typ] += ds1
            ds1b, ds2b = ds1.astype(BF16), ds2.astype(BF16)
            dq_ref[:, ln] = ((_dot(ds1b, kw) + _dot(ds2b, kc)) * scale).astype(BF16)
            dk_ref[pl.ds(start, KW), ln] += _dot_tn(ds1b, q)
            dv_ref[pl.ds(start, KW), ln] += _dot_tn(p1b, do)
            dk_ref[L:L + LC, ln] += _dot_tn(ds2b, q)
            dv_ref[L:L + LC, ln] += _dot_tn(p2b, do)

    tile = pl.BlockSpec((None, TQ, 128), lambda hp, b, m: (b, m, hp))
    kv_out = pl.BlockSpec((None, T, 128), lambda hp, b, m: (b, 0, hp))
    return pl.pallas_call(
        body, name="na_bwd", grid=(4, B, nm),
        in_specs=[q_spec, k_spec, v_spec, g_spec, bias_spec, tile],
        out_specs=(tile, tile, kv_out, kv_out, bias_spec),
        out_shape=(jax.ShapeDtypeStruct((B, L, 512), BF16), jax.ShapeDtypeStruct((B, L, 512), BF16),
                   jax.ShapeDtypeStruct((B, T, 512), F32), jax.ShapeDtypeStruct((B, T, 512), F32),
                   jax.ShapeDtypeStruct(bias.shape, F32)),
        compiler_params=_params(("arbitrary",) * 3))(P, P, P, P, bias, dY)


def _head_scalar(dec_ref, h):
    lane = lax.broadcasted_iota(jnp.int32, dec_ref.shape, 1)
    return -jnp.sum(jnp.where(lane == h, jnp.exp(dec_ref[...]), 0.0), axis=1, keepdims=True)


def _decay_lat(tpos, ks, lgf, lgb):
    spos = (ks + lax.broadcasted_iota(jnp.int32, (1, TK), 1)).astype(F32)
    dist = tpos - spos
    dm = jnp.exp(dist * jnp.where(dist > 0, lgf, -lgb)) * jnp.where(dist == 0, 2.0, 1.0)
    return dist, dm


def _decay_ctx(tpos, L, LC, lgf, lgb):
    jc = lax.broadcasted_iota(jnp.int32, (1, LC), 1).astype(F32)
    df = tpos + (float(LC) - jc)
    db = (float(L) - tpos) + jc
    return df, db, jnp.exp(lgf * df), jnp.exp(lgb * db)


def _ret_specs(T):
    q_spec = pl.BlockSpec((None, TQ, 128), lambda b, h, i: (b, i, 16 + h))
    k_spec = pl.BlockSpec((None, T, 128), lambda b, h, i: (b, 0, 20 + h))
    v_spec = pl.BlockSpec((None, T, 128), lambda b, h, i: (b, 0, 24 + h))
    g_spec = pl.BlockSpec((None, TQ, 128), lambda b, h, i: (b, i, 28 + h))
    dec_spec = pl.BlockSpec((1, 4), lambda b, h, i: (0, 0))
    gn_spec = pl.BlockSpec((1, 128), lambda b, h, i: (0, h))
    return q_spec, k_spec, v_spec, g_spec, dec_spec, gn_spec


def _ret_fwd_call(P, dec_f, dec_b, ret_norm_g, L, LC):
    B, T, _ = P.shape
    q_spec, k_spec, v_spec, g_spec, dec_spec, gn_spec = _ret_specs(T)

    def body(df_ref, db_ref, q_ref, k_ref, v_ref, g_ref, gn_ref, y_ref, o_ref):
        h, i = pl.program_id(1), pl.program_id(2)
        lgf, lgb = _head_scalar(df_ref, h), _head_scalar(db_ref, h)
        tpos = (i * TQ + lax.broadcasted_iota(jnp.int32, (TQ, 1), 0)).astype(F32)
        q = q_ref[...]

        def chunk(j, acc):
            ks = pl.multiple_of(j * TK, TK)
            kj, vj = k_ref[pl.ds(ks, TK), :], v_ref[pl.ds(ks, TK), :]
            _, dm = _decay_lat(tpos, ks, lgf, lgb)
            return acc + _dot((_dot_nt(q, kj) * dm).astype(BF16), vj)

        acc = lax.fori_loop(0, L // TK, chunk, jnp.zeros((TQ, RET_DK), F32))
        _, _, ef, eb = _decay_ctx(tpos, L, LC, lgf, lgb)
        acc = acc + _dot((_dot_nt(q, k_ref[L:L + LC, :]) * (ef + eb)).astype(BF16), v_ref[L:L + LC, :])
        o_ref[...] = acc
        rn = lax.rsqrt(jnp.mean(acc * acc, axis=-1, keepdims=True) + EPS)
        g = g_ref[...].astype(F32)
        y_ref[...] = ((acc * rn * gn_ref[...]).astype(F32) * (g * _sigmoid(g))).astype(BF16)

    tile = pl.BlockSpec((None, TQ, 128), lambda b, h, i: (b, i, h))
    return pl.pallas_call(
        body, name="ret_fwd", grid=(B, 4, L // TQ),
        in_specs=[dec_spec, dec_spec, q_spec, k_spec, v_spec, g_spec, gn_spec],
        out_specs=(tile, tile),
        out_shape=(jax.ShapeDtypeStruct((B, L, 512), BF16), jax.ShapeDtypeStruct((B, L, 512), F32)),
        compiler_params=_params(("arbitrary",) * 3))(dec_f, dec_b, P, P, P, P, ret_norm_g)


def _ret_bwd_call(P, dec_f, dec_b, ret_norm_g, o_ret, dY, cos2, sin2, L, LC):
    B, T, _ = P.shape
    ni = L // TQ
    kscale = RET_DK ** -0.5
    q_spec, k_spec, v_spec, g_spec, dec_spec, gn_spec = _ret_specs(T)

    def body(df_ref, db_ref, q_ref, k_ref, v_ref, g_ref, gn_ref, o_ref, dy_ref, cos_ref, sin_ref,
             dq_ref, dg_ref, dk_ref, dv_ref, dgn_ref, dlg_ref):
        h, i = pl.program_id(1), pl.program_id(2)
        lgf, lgb = _head_scalar(df_ref, h), _head_scalar(db_ref, h)
        tpos = (i * TQ + lax.broadcasted_iota(jnp.int32, (TQ, 1), 0)).astype(F32)

        @pl.when(i == 0)
        def _():
            dk_ref[...] = jnp.zeros_like(dk_ref)
            dv_ref[...] = jnp.zeros_like(dv_ref)
            dgn_ref[...] = jnp.zeros_like(dgn_ref)
            dlg_ref[...] = jnp.zeros_like(dlg_ref)

        q = q_ref[...]
        o = o_ref[...]
        g = g_ref[...].astype(F32)
        dy = dy_ref[...].astype(F32)
        gn = gn_ref[...]
        sg = _sigmoid(g)
        rn = lax.rsqrt(jnp.mean(o * o, axis=-1, keepdims=True) + EPS)
        nrm = o * rn
        dg_ref[...] = (dy * (nrm * gn) * (sg * (1.0 + g * (1.0 - sg)))).astype(BF16)
        dhn = dy * (g * sg)
        dgn_ref[...] += jnp.sum(dhn * nrm, axis=0, keepdims=True)
        dnrm = dhn * gn
        do = rn * (dnrm - nrm * jnp.mean(dnrm * nrm, axis=-1, keepdims=True))
        dob = do.astype(BF16)

        def chunk(j, dq):
            ks = pl.multiple_of(j * TK, TK)
            kj, vj = k_ref[pl.ds(ks, TK), :], v_ref[pl.ds(ks, TK), :]
            dist, dm = _decay_lat(tpos, ks, lgf, lgb)
            s = _dot_nt(q, kj)
            dsv = _dot_nt(dob, vj)
            dsb = (dsv * dm).astype(BF16)
            dk_ref[pl.ds(ks, TK), :] += _dot_tn(dsb, q)
            dv_ref[pl.ds(ks, TK), :] += _dot_tn((s * dm).astype(BF16), dob)
            xw = s * dsv * dm * jnp.abs(dist)
            tot = jnp.sum(xw, axis=0, keepdims=True)
            fwd = jnp.sum(jnp.where(dist > 0, xw, 0.0), axis=0, keepdims=True)
            dlg_ref[0:1, :] += fwd
            dlg_ref[1:2, :] += tot - fwd
            return dq + _dot(dsb, kj)

        dq = lax.fori_loop(0, L // TK, chunk, jnp.zeros((TQ, RET_DK), F32))
        kc, vc = k_ref[L:L + LC, :], v_ref[L:L + LC, :]
        dfc, dbc, ef, eb = _decay_ctx(tpos, L, LC, lgf, lgb)
        s = _dot_nt(q, kc)
        dsv = _dot_nt(dob, vc)
        dsb = (dsv * (ef + eb)).astype(BF16)
        dk_ref[L:L + LC, :] += _dot_tn(dsb, q)
        dv_ref[L:L + LC, :] += _dot_tn((s * (ef + eb)).astype(BF16), dob)
        a = s * dsv
        dlg_ref[0:1, 0:LC] += jnp.sum(a * ef * dfc, axis=0, keepdims=True)
        dlg_ref[1:2, 0:LC] += jnp.sum(a * eb * dbc, axis=0, keepdims=True)
        dq = dq + _dot(dsb, kc)
        cs, sn = cos_ref[pl.ds(pl.multiple_of(i * TQ, TQ), TQ), :], sin_ref[pl.ds(pl.multiple_of(i * TQ, TQ), TQ), :]
        dq_ref[...] = (dq * cs - pltpu.roll(dq, 64, 1) * sn).astype(BF16)

        @pl.when(i == ni - 1)
        def _():
            dk = dk_ref[...]
            dk_ref[...] = (dk * cos_ref[...] - pltpu.roll(dk, 64, 1) * sin_ref[...]) * kscale

    tile = pl.BlockSpec((None, TQ, 128), lambda b, h, i: (b, i, h))
    kv_out = pl.BlockSpec((None, T, 128), lambda b, h, i: (b, 0, h))
    tab = pl.BlockSpec((T, RET_DK), lambda b, h, i: (0, 0))
    return pl.pallas_call(
        body, name="ret_bwd", grid=(B, 4, ni),
        in_specs=[dec_spec, dec_spec, q_spec, k_spec, v_spec, g_spec, gn_spec, tile,
                  pl.BlockSpec((None, TQ, 128), lambda b, h, i: (b, i, 4 + h)), tab, tab],
        out_specs=(tile, tile, kv_out, kv_out,
                   pl.BlockSpec((None, 1, 128), lambda b, h, i: (b, 0, h)),
                   pl.BlockSpec((None, None, 8, TK), lambda b, h, i: (b, h, 0, 0))),
        out_shape=(jax.ShapeDtypeStruct((B, L, 512), BF16), jax.ShapeDtypeStruct((B, L, 512), BF16),
                   jax.ShapeDtypeStruct((B, T, 512), F32), jax.ShapeDtypeStruct((B, T, 512), F32),
                   jax.ShapeDtypeStruct((B, 1, 512), F32), jax.ShapeDtypeStruct((B, 4, 8, TK), F32)),
        compiler_params=_params(("arbitrary",) * 3))(
            dec_f, dec_b, P, P, P, P, ret_norm_g, o_ret, dY, cos2, sin2)


def _out_call(y_na, y_ret, x, target, mod, final_g, wout_f):
    B, L, _ = x.shape

    def body(yn_ref, yr_ref, x_ref, t_ref, mod_ref, gf_ref, w_ref, dy_ref, dx2_ref, dw_ref, sm_ref):
        b, i = pl.program_id(0), pl.program_id(1)

        @pl.when((b == 0) & (i == 0))
        def _():
            dw_ref[...] = jnp.zeros_like(dw_ref)
            sm_ref[...] = jnp.zeros_like(sm_ref)

        gate = mod_ref[pl.ds(b, 1), 2 * D:3 * D]
        gf = gf_ref[...]
        yn, yr = yn_ref[...], yr_ref[...]
        ylat = _dot(yn, w_ref[0:512, :]) + _dot(yr, w_ref[512:1024, :])
        x2 = x_ref[...] + gate * ylat
        r = lax.rsqrt(jnp.mean(x2 * x2, axis=-1, keepdims=True) + EPS)
        xr = x2 * r
        err = xr * gf - t_ref[...]
        sm_ref[1:2, :] += jnp.sum(err * err, axis=0, keepdims=True)
        dout = err * (1.0 / D)
        sm_ref[0:1, :] += jnp.sum(dout * xr, axis=0, keepdims=True)
        gd = dout * gf
        dx2 = r * (gd - xr * jnp.mean(gd * xr, axis=-1, keepdims=True))
        dx2_ref[...] = dx2
        sm_ref[pl.ds(2 + b, 1), :] += jnp.sum(dx2 * ylat, axis=0, keepdims=True)
        dyl = (gate * dx2).astype(BF16)
        dy_ref[:, 0:512] = _dot_nt(dyl, w_ref[0:512, :]).astype(BF16)
        dy_ref[:, 512:1024] = _dot_nt(dyl, w_ref[512:1024, :]).astype(BF16)
        dw_ref[0:512, :] += _dot_tn(yn, dyl)
        dw_ref[512:1024, :] += _dot_tn(yr, dyl)

    half = pl.BlockSpec((None, TQ, 512), lambda b, i: (b, i, 0))
    full = pl.BlockSpec((None, TQ, D), lambda b, i: (b, i, 0))
    return pl.pallas_call(
        body, name="out_proj_loss", grid=(B, L // TQ),
        in_specs=[half, half, full, full,
                  pl.BlockSpec((8, 3 * D), lambda b, i: (0, 0)),
                  pl.BlockSpec((1, D), lambda b, i: (0, 0)),
                  pl.BlockSpec((D, D), lambda b, i: (0, 0))],
        out_specs=(full, full, pl.BlockSpec((D, D), lambda b, i: (0, 0)),
                   pl.BlockSpec((8, D), lambda b, i: (0, 0))),
        out_shape=(jax.ShapeDtypeStruct((B, L, D), BF16), jax.ShapeDtypeStruct((B, L, D), F32),
                   jax.ShapeDtypeStruct((D, D), F32), jax.ShapeDtypeStruct((8, D), F32)),
        compiler_params=_params(("arbitrary",) * 2))(y_na, y_ret, x, target, mod, final_g, wout_f)


def _dh_call(dsec, win_f, x, ctx, dx2, mod, norm_g):
    B, L, _ = x.shape
    LC = ctx.shape[1]
    nl = L // TQ

    def body(d0, d1, d2, d3, d4, d5, d6, d7, w_ref, x_ref, ctx_ref, dx2_ref, mod_ref, g_ref,
             gx_ref, sm_ref):
        drefs = (d0, d1, d2, d3, d4, d5, d6, d7)
        b, t = pl.program_id(0), pl.program_id(1)
        is_lat = t < nl

        @pl.when((b == 0) & (t == 0))
        def _():
            sm_ref[...] = jnp.zeros_like(sm_ref)

        def dh_of(secs):
            acc = jnp.zeros((TQ, D), F32)
            for sec in secs:
                s, half = divmod(sec, 2)
                acc = acc + _dot_nt(drefs[sec][...].astype(BF16), w_ref[s, :, half * 512:(half + 1) * 512])
            return acc

        def norm_bwd(dh, xt, mrow):
            scale = mrow[:, D:2 * D]
            g = g_ref[...]
            rstd = lax.rsqrt(jnp.mean(xt * xt, axis=-1, keepdims=True) + EPS)
            xn = xt * rstd
            dshift = jnp.sum(dh, axis=0, keepdims=True)
            dscale = jnp.sum(dh * (xn * g), axis=0, keepdims=True)
            dhn = dh * (1.0 + scale)
            sm_ref[0:1, :] += jnp.sum(dhn * xn, axis=0, keepdims=True)
            dxn = dhn * g
            dx = rstd * (dxn - xn * jnp.mean(dxn * xn, axis=-1, keepdims=True))
            return dshift, dscale, dx

        @pl.when(is_lat)
        def _():
            dshift, dscale, dx = norm_bwd(dh_of(range(8)), x_ref[...], mod_ref[pl.ds(b, 1), :])
            sm_ref[pl.ds(3 + b, 1), :] += dshift
            sm_ref[pl.ds(3 + B + b, 1), :] += dscale
            gx_ref[...] = dx2_ref[...] + dx

        @pl.when(jnp.logical_not(is_lat))
        def _():
            dshift, dscale, _ = norm_bwd(dh_of((1, 2, 5, 6)), ctx_ref[...], mod_ref[B:B + 1, :])
            sm_ref[1:2, :] += dshift
            sm_ref[2:3, :] += dscale

    lat = lambda b, t: (b, jnp.minimum(t, nl - 1), 0)
    tok = lambda b, t: (b, t, 0)
    sec_specs = [pl.BlockSpec((None, TQ, 512), lat if sec in (0, 3, 4, 7) else tok) for sec in range(8)]
    return pl.pallas_call(
        body, name="dh_norm_bwd", grid=(B, nl + 1),
        in_specs=sec_specs + [
            pl.BlockSpec((N_SHARD, D, D), lambda b, t: (0, 0, 0)),
            pl.BlockSpec((None, TQ, D), lat),
            pl.BlockSpec((None, LC, D), lambda b, t: (b, 0, 0)),
            pl.BlockSpec((None, TQ, D), lat),
            pl.BlockSpec((8, 3 * D), lambda b, t: (0, 0)),
            pl.BlockSpec((1, D), lambda b, t: (0, 0))],
        out_specs=(pl.BlockSpec((None, TQ, D), lat), pl.BlockSpec((8, D), lambda b, t: (0, 0))),
        out_shape=(jax.ShapeDtypeStruct((B, L, D), F32), jax.ShapeDtypeStruct((8, D), F32)),
        compiler_params=_params(("arbitrary",) * 2))(*dsec, win_f, x, ctx, dx2, mod, norm_g)


def _dw_call(dsec, h, L):
    B, T, _ = h.shape
    nl = L // TQ

    def body(d0, d1, d2, d3, d4, d5, d6, d7, h_ref, dw_ref, acc_ref):
        drefs = (d0, d1, d2, d3, d4, d5, d6, d7)
        b, t = pl.program_id(0), pl.program_id(1)

        @pl.when((b == 0) & (t == 0))
        def _():
            acc_ref[...] = jnp.zeros_like(acc_ref)

        hb = h_ref[...]

        def add(secs):
            for sec in secs:
                s, half = divmod(sec, 2)
                acc_ref[s, :, half * 512:(half + 1) * 512] += _dot_tn(hb, drefs[sec][...].astype(BF16))

        @pl.when(t < nl)
        def _():
            add(range(8))

        @pl.when(t >= nl)
        def _():
            add((1, 2, 5, 6))

        @pl.when((b == B - 1) & (t == nl))
        def _():
            dw_ref[...] = acc_ref[...].astype(BF16)

    lat = lambda b, t: (b, jnp.minimum(t, nl - 1), 0)
    tok = lambda b, t: (b, t, 0)
    sec_specs = [pl.BlockSpec((None, TQ, 512), lat if sec in (0, 3, 4, 7) else tok) for sec in range(8)]
    return pl.pallas_call(
        body, name="dw_in", grid=(B, nl + 1),
        in_specs=sec_specs + [pl.BlockSpec((None, TQ, D), tok)],
        out_specs=pl.BlockSpec((N_SHARD, D, D), lambda b, t: (0, 0, 0)),
        out_shape=jax.ShapeDtypeStruct((N_SHARD, D, D), BF16),
        scratch_shapes=[pltpu.VMEM((N_SHARD, D, D), F32)],
        compiler_params=_params(("arbitrary",) * 2, vmem_mb=56))(*dsec, h)


def _mesh_pos():
    return lax.axis_index("x"), lax.axis_index("y"), lax.axis_index("c")


def _flip(v, f):
    return 1 - v if f else v


def _remote(src, dst, ssem, rsem, k, peer):
    return pltpu.make_async_remote_copy(src_ref=src, dst_ref=dst, send_sem=ssem.at[k], recv_sem=rsem.at[k],
                                        device_id=peer, device_id_type=MESH)


def _other_chips(x, y):
    return [(_flip(x, fx), _flip(y, fy)) for fx, fy in ((1, 0), (0, 1), (1, 1))]


D2D_STREAMS = 8


def _row_chunks(src, dst, ssem, rsem, k, peer, rows, lead=None):
    step = rows // D2D_STREAMS
    out = []
    for r in range(D2D_STREAMS):
        idx = (pl.ds(r * step, step),) if lead is None else (lead, pl.ds(r * step, step))
        out.append(_remote(src.at[idx], dst.at[idx], ssem, rsem, k, peer))
    return out


def _all_to_all_small(src, dst_all, ssem, rsem, k0, x, y, cc):
    me = 4 * x + 2 * y + cc
    sends, recvs = [], []
    for f in range(1, N_DEV):
        px, py, pc = _flip(x, f & 4), _flip(y, f & 2), _flip(cc, f & 1)
        sends.append(_remote(src, dst_all.at[me], ssem, rsem, k0 + f - 1, (px, py, pc)))
        recvs.append(_remote(src, dst_all.at[4 * px + 2 * py + pc], ssem, rsem, k0 + f - 1, (px, py, pc)))
    return sends, recvs


def _finish(local, sends, recvs):
    for cp in recvs:
        cp.wait_recv()
    for cp in sends:
        cp.wait_send()
    for cp in local:
        cp.wait()


def _gather_call(win_b, wout_b, wada_b, c):
    arrs = (win_b, wout_b, wada_b)
    hrs = [a.shape[0] // 2 for a in arrs]

    def body(win, wout, wada, c_ref, win_f, wout_f, wada_f, c_all, ssem, rsem, lsem):
        x, y, cc = _mesh_pos()
        s, me = 2 * x + y, 4 * x + 2 * y + cc
        sib = (x, y, 1 - cc)
        srcs, dsts = (win, wout, wada), (win_f, wout_f, wada_f)

        def half(a, shard, hc):
            return dsts[a].at[shard, pl.ds(hc * hrs[a], hrs[a])]

        local = [pltpu.make_async_copy(srcs[a], dsts[a].at[s], lsem.at[a]) for a in range(3)]
        local.append(pltpu.make_async_copy(c_ref, c_all.at[me], lsem.at[3]))
        ici_send, ici_recv, fwd_send, fwd_recv, fwd_chunks, k = [], [], [], [], [], 0
        for px, py in _other_chips(x, y):
            ps = 2 * px + py
            for a in range(3):
                mine = srcs[a].at[pl.ds(cc * hrs[a], hrs[a])]
                ici_send.append(_remote(mine, half(a, s, cc), ssem, rsem, k, (px, py, cc)))
                ici_recv.append(_remote(mine, half(a, ps, cc), ssem, rsem, k, (px, py, cc)))
                fwd_send.append(_remote(half(a, ps, cc), half(a, ps, cc), ssem, rsem, 9 + k, sib))
                fwd_recv.append(_remote(half(a, ps, 1 - cc), half(a, ps, 1 - cc), ssem, rsem, 9 + k, sib))
                fwd_chunks.append(_row_chunks(half(a, ps, cc), half(a, ps, cc), ssem, rsem, 9 + k, sib, hrs[a]))
                k += 1
        c_send, c_recv = _all_to_all_small(c_ref, c_all, ssem, rsem, 18, x, y, cc)
        for cp in local + ici_send + c_send:
            cp.start()
        for got, chunks in zip(ici_recv, fwd_chunks):
            got.wait_recv()
            for cp in chunks:
                cp.start()
        _finish(local, ici_send + fwd_send + c_send, fwd_recv + c_recv)

    return pl.pallas_call(
        body, name="weight_gather",
        in_specs=[pl.BlockSpec(memory_space=pltpu.VMEM)] * 4,
        out_specs=(pl.BlockSpec(memory_space=pltpu.VMEM),) * 4,
        out_shape=tuple(jax.ShapeDtypeStruct((N_SHARD,) + a.shape, a.dtype) for a in arrs)
        + (jax.ShapeDtypeStruct((N_DEV,) + c.shape, c.dtype),),
        scratch_shapes=[pltpu.SemaphoreType.DMA((25,)), pltpu.SemaphoreType.DMA((25,)),
                        pltpu.SemaphoreType.DMA((4,))],
        compiler_params=pltpu.CompilerParams(vmem_limit_bytes=48 << 20))(win_b, wout_b, wada_b, c)


def _grad_reduce_call(dwin_b, dwout_b, small):
    arrs = (dwin_b, dwout_b)
    hrs = [a.shape[1] // 2 for a in arrs]
    RC = 32

    def body(din, dout, sm, gin, gout, sm_all, got_in, got_out, cp_in, cp_out, sl_in, sl_out, h_in, h_out,
             ssem, rsem, lsem):
        x, y, cc = _mesh_pos()
        s, me = 2 * x + y, 4 * x + 2 * y + cc
        sib = (x, y, 1 - cc)
        srcs, gots, cps = (din, dout), (got_in, got_out), (cp_in, cp_out)
        sls, hs, gs = (sl_in, sl_out), (h_in, h_out), (gin, gout)
        halves = [_remote(srcs[a].at[:, pl.ds((1 - cc) * hrs[a], hrs[a])], gots[a], ssem, rsem, a, sib)
                  for a in range(2)]
        sm_send, sm_recv = _all_to_all_small(sm, sm_all, ssem, rsem, 10, x, y, cc)
        sm_own = pltpu.make_async_copy(sm, sm_all.at[me], lsem.at[0])
        for cp in halves + sm_send + [sm_own]:
            cp.start()
        for cp in halves:
            cp.wait_recv()
        for a in range(2):
            for j in range(N_SHARD):
                def add(i, carry, a=a, j=j):
                    r = pl.multiple_of(i * RC, RC)
                    mine = srcs[a][j, pl.ds(pl.multiple_of(cc * hrs[a] + r, RC), RC), :].astype(F32)
                    cps[a][j, pl.ds(r, RC), :] = (mine + gots[a][j, pl.ds(r, RC), :].astype(F32)).astype(BF16)
                    return carry
                lax.fori_loop(0, hrs[a] // RC, add, 0)
        own = [pltpu.make_async_copy(cps[a].at[s], sls[a].at[s], lsem.at[1 + a]) for a in range(2)]
        sends, recvs, k = [], [], 2
        for px, py in _other_chips(x, y):
            ps = 2 * px + py
            for a in range(2):
                sends.append(_remote(cps[a].at[ps], sls[a].at[s], ssem, rsem, k, (px, py, cc)))
                recvs.append(_remote(cps[a].at[s], sls[a].at[ps], ssem, rsem, k, (px, py, cc)))
                k += 1
        for cp in own + sends:
            cp.start()
        for cp in own:
            cp.wait()
        for cp in recvs:
            cp.wait_recv()
        for a in range(2):
            def total(i, carry, a=a):
                rows = pl.ds(pl.multiple_of(i * RC, RC), RC)
                sl = sls[a]
                hs[a][rows, :] = ((sl[0, rows, :].astype(F32) + sl[1, rows, :].astype(F32))
                                  + sl[2, rows, :].astype(F32)) + sl[3, rows, :].astype(F32)
                return carry
            lax.fori_loop(0, hrs[a] // RC, total, 0)
        mine = [pltpu.make_async_copy(hs[a], gs[a].at[cc], lsem.at[3 + a]) for a in range(2)]
        back = [_remote(hs[a], gs[a].at[cc], ssem, rsem, 8 + a, sib) for a in range(2)]
        back_recv = [_remote(hs[a], gs[a].at[1 - cc], ssem, rsem, 8 + a, sib) for a in range(2)]
        for cp in mine + back:
            cp.start()
        _finish(mine + [sm_own], halves + sends + back + sm_send, back_recv + sm_recv)

    vmem = pl.BlockSpec(memory_space=pltpu.VMEM)
    half_shapes = [(N_SHARD, hrs[a], arrs[a].shape[2]) for a in range(2)]
    scratch = []
    for dt in (BF16, BF16, BF16):
        scratch += [pltpu.VMEM(half_shapes[0], dt), pltpu.VMEM(half_shapes[1], dt)]
    scratch += [pltpu.VMEM(half_shapes[0][1:], F32), pltpu.VMEM(half_shapes[1][1:], F32)]
    return pl.pallas_call(
        body, name="grad_reduce",
        in_specs=[vmem] * 3, out_specs=(vmem,) * 3,
        out_shape=(jax.ShapeDtypeStruct((2,) + half_shapes[0][1:], F32),
                   jax.ShapeDtypeStruct((2,) + half_shapes[1][1:], F32),
                   jax.ShapeDtypeStruct((N_DEV,) + small.shape, F32)),
        scratch_shapes=scratch + [pltpu.SemaphoreType.DMA((17,)), pltpu.SemaphoreType.DMA((17,)),
                                  pltpu.SemaphoreType.DMA((5,))],
        compiler_params=pltpu.CompilerParams(vmem_limit_bytes=56 << 20))(dwin_b, dwout_b, small)


def _adamw(w, g, m, v):
    m = ADAM_B1 * m + (1.0 - ADAM_B1) * g
    v = ADAM_B2 * v + (1.0 - ADAM_B2) * (g * g)
    m_hat = m / (1.0 - ADAM_B1 ** ADAM_STEP)
    v_hat = v / (1.0 - ADAM_B2 ** ADAM_STEP)
    return -ADAM_LR * (m_hat / (jnp.sqrt(v_hat) + ADAM_EPS) + ADAM_WD * w), m, v


def _adam_call(w, m, v, g, name):
    R, C = w.shape
    tr = 256

    def body(w_ref, m_ref, v_ref, g_ref, d_ref, mo_ref, vo_ref):
        d_ref[...], mo_ref[...], vo_ref[...] = _adamw(w_ref[...], g_ref[...], m_ref[...], v_ref[...])

    spec = pl.BlockSpec((tr, C), lambda i: (i, 0))
    return pl.pallas_call(
        body, name=name, grid=(R // tr,), in_specs=[spec] * 4,
        out_specs=(spec,) * 3, out_shape=(jax.ShapeDtypeStruct((R, C), F32),) * 3,
        compiler_params=_params(("arbitrary",)))(w, m, v, g)


R_GF, R_NG, R_LOSS, R_RNG, R_LGF, R_LGB, R_SHIFT, R_SCALE, R_GATE, R_SHIFT_C, R_SCALE_C, R_RNG2, R_RPB = (
    0, 1, 2, 3, 4, 5, 6, 8, 10, 12, 13, 14, 16)
W_GF, W_NG, W_CCTX, W_RNG, W_DF, W_DB, W_BADA, W_RPB = 0, 1, 2, 3, 4, 5, 6, 9


def _small_final_call(sm_all, c_t, c_ctx, wada_f, wada, m_ada, v_ada, wsm, msm, vsm, B):
    ws = wada.shape[1]
    NB = N_DEV * B

    def body(sm_ref, ct_ref, cctx_ref, wf_ref, wa_ref, ma_ref, va_ref, w_ref, m_ref, v_ref,
             g_ref, d_ref, mo_ref, vo_ref, ga_ref, da_ref, mao_ref, vao_ref, loss_ref, dmod_ref):
        x, y, _ = _mesh_pos()
        s = 2 * x + y
        tot = sm_ref[0]
        for dv in range(1, N_DEV):
            tot = tot + sm_ref[dv]
        w = w_ref[...]
        for dv in range(N_DEV):
            for b in range(B):
                r = dv * B + b
                for part, row in enumerate((R_SHIFT, R_SCALE, R_GATE)):
                    dmod_ref[r:r + 1, part * D:(part + 1) * D] = sm_ref[dv, row + b:row + b + 1, :]
        dmod_ref[NB:NB + 1, 0:D] = tot[R_SHIFT_C:R_SHIFT_C + 1, :]
        dmod_ref[NB:NB + 1, D:2 * D] = tot[R_SCALE_C:R_SCALE_C + 1, :]
        dmod_ref[NB:NB + 1, 2 * D:3 * D] = jnp.zeros((1, D), F32)
        dmod_ref[NB + 1:, :] = jnp.zeros((dmod_ref.shape[0] - NB - 1, 3 * D), F32)
        dmod = dmod_ref[...]
        cc = cctx_ref[...]
        scc = _sigmoid(cc)
        ct = ct_ref[...]
        act_t = ct * _sigmoid(ct)
        dmc = dmod[NB:NB + 1, :].astype(BF16)
        dact = jnp.zeros((1, D), F32)
        for sh in range(N_SHARD):
            dact = dact + _dot_nt(dmc[:, sh * ws:(sh + 1) * ws], wf_ref[sh])
        g = jnp.zeros((16, D), F32)
        rows = lax.broadcasted_iota(jnp.int32, (16, D), 0)

        def put(g, row, val):
            return jnp.where(rows == row, val, g)

        g = put(g, W_GF, tot[R_GF:R_GF + 1, :])
        g = put(g, W_NG, tot[R_NG:R_NG + 1, :])
        g = put(g, W_CCTX, dact * (scc * (1.0 + cc * (1.0 - scc))))
        g = put(g, W_RNG, tot[R_RNG:R_RNG + 1, :] + tot[R_RNG2:R_RNG2 + 1, :])
        g = put(g, W_DF, tot[R_LGF:R_LGF + 1, :] * (-jnp.exp(w[W_DF:W_DF + 1, :])))
        g = put(g, W_DB, tot[R_LGB:R_LGB + 1, :] * (-jnp.exp(w[W_DB:W_DB + 1, :])))
        db = jnp.sum(dmod, axis=0, keepdims=True)
        for part in range(3):
            g = put(g, W_BADA + part, db[:, part * D:(part + 1) * D])
        for part in range(4):
            g = put(g, W_RPB + part, tot[R_RPB + part:R_RPB + part + 1, :])
        g_ref[...] = g
        d_ref[...], mo_ref[...], vo_ref[...] = _adamw(w, g, m_ref[...], v_ref[...])
        loss_ref[...] = jnp.broadcast_to(
            (0.5 / D) * jnp.sum(tot[R_LOSS:R_LOSS + 1, :], axis=1, keepdims=True), (8, 128))
        for sh in range(N_SHARD):
            @pl.when(s == sh)
            def _():
                ga = jnp.dot(act_t, dmod[:, sh * ws:(sh + 1) * ws], precision=HIGHEST,
                             preferred_element_type=F32)
                ga_ref[...] = ga
                da_ref[...], mao_ref[...], vao_ref[...] = _adamw(wa_ref[...], ga, ma_ref[...], va_ref[...])

    sh_small = jax.ShapeDtypeStruct((16, D), F32)
    sh_ada = jax.ShapeDtypeStruct(wada.shape, F32)
    return pl.pallas_call(
        body, name="small_final",
        out_shape=(sh_small,) * 4 + (sh_ada,) * 4 + (jax.ShapeDtypeStruct((8, 128), F32),),
        scratch_shapes=[pltpu.VMEM((NB + 8, 3 * D), F32)],
        compiler_params=_params(vmem_mb=56))(
            sm_all, c_t, c_ctx, wada_f, wada, m_ada, v_ada, wsm, msm, vsm)


def _local_step(x, c, ctx, c_ctx, norm_g, wada_f, b_ada, win_f, na_rpb, dec_f, dec_b, ret_norm_g,
                wout_f, final_g, target):
    B, L, _ = x.shape
    LC = ctx.shape[1]
    assert B == 2
    cos2, sin2 = _rope_tables(L, LC)
    c8 = jnp.concatenate([c, c_ctx[None, :], jnp.zeros((8 - B - 1, D), F32)], axis=0)
    mod = _mod_call(c8, wada_f, b_ada)
    bias = _bias_call(na_rpb.reshape(na_rpb.shape[0], -1))
    P, h = _inproj_call(x, ctx, mod, norm_g, win_f, cos2, sin2)
    y_na = _na_fwd_call(P, bias, L, LC)
    y_ret, o_ret = _ret_fwd_call(P, dec_f, dec_b, ret_norm_g, L, LC)
    dY, dx2, dwout_p, sm_out = _out_call(y_na, y_ret, x, target, mod, final_g, wout_f.reshape(D, D))
    dnq, dng, dnk, dnv, dbias = _na_bwd_call(P, bias, dY, L, LC)
    drq, drg, drk, drv, dgn, dlg = _ret_bwd_call(P, dec_f, dec_b, ret_norm_g, o_ret, dY, cos2, sin2, L, LC)
    dsec = (dnq, dnk, dnv, dng, drq, drk, drv, drg)
    grad_x, sm_dh = _dh_call(dsec, win_f, x, ctx, dx2, mod, norm_g)
    dwin_p = _dw_call(dsec, h, L)
    drpb, dlg_sum = _small_reduce_call(dbias, dlg, B)
    z = jnp.zeros((1, D), F32)
    pad = lambda v: jnp.pad(v.reshape(1, -1), ((0, 0), (0, D - v.size)))
    dlg_sum = dlg_sum.reshape(4, 8, 128)
    rpb_rows = jnp.pad(drpb[:, :15, :31].reshape(-1), (0, 4 * D - drpb.shape[0] * 465)).reshape(4, D)
    small = jnp.concatenate([
        sm_out[0:1], sm_dh[0:1], sm_out[1:2], pad(dgn[0]), pad(dlg_sum[:, 0, 0]), pad(dlg_sum[:, 1, 0]),
        sm_dh[3:5], sm_dh[5:7], sm_out[2:4], sm_dh[1:2], sm_dh[2:3], pad(dgn[1]), z, rpb_rows,
        jnp.zeros((SM_ROWS - 20, D), F32)], axis=0)
    return grad_x, dwin_p, dwout_p, small


def kernel(x, c, ctx, c_ctx, norm_g, w_ada, b_ada, w_in, na_rpb, ret_decay_fwd, ret_decay_bwd, ret_norm_g, w_out, final_norm_g, loss_target, m_c_ctx, m_norm_g, m_w_ada, m_b_ada, m_w_in, m_na_rpb, m_ret_decay_fwd, m_ret_decay_bwd, m_ret_norm_g, m_w_out, m_final_norm_g, v_c_ctx, v_norm_g, v_w_ada, v_b_ada, v_w_in, v_na_rpb, v_ret_decay_fwd, v_ret_decay_bwd, v_ret_norm_g, v_w_out, v_final_norm_g):
    B = x.shape[0]
    win_f, wout_f, wada_f, c_all = _gather_call(w_in[0].astype(BF16), w_out[0].astype(BF16),
                                                w_ada[0].astype(BF16), c)
    grad_x, dwin_p, dwout_p, small = _local_step(
        x, c, ctx, c_ctx, norm_g, wada_f, b_ada, win_f, na_rpb[0], ret_decay_fwd, ret_decay_bwd,
        ret_norm_g, wout_f, final_norm_g.reshape(1, D), loss_target)
    gin, gout, sm_all = _grad_reduce_call(
        dwin_p, dwout_p.astype(BF16).reshape(N_SHARD, D // N_SHARD, D), small)
    g_win, g_wout = gin.reshape(w_in.shape[1:]), gout.reshape(w_out.shape[1:])
    d_win, nm_win, nv_win = _adam_call(w_in[0], m_w_in[0], v_w_in[0], g_win, "adam_w_in")
    d_wout, nm_wout, nv_wout = _adam_call(w_out[0], m_w_out[0], v_w_out[0], g_wout, "adam_w_out")

    def pack(gf, ng, cc, rng, df, db, bada, rpb):
        pad = lambda v: jnp.pad(v.reshape(1, -1), ((0, 0), (0, D - v.size)))
        return jnp.concatenate([
            gf.reshape(1, D), ng.reshape(1, D), cc.reshape(1, D), pad(rng), pad(df), pad(db),
            bada.reshape(3, D), jnp.pad(rpb.reshape(-1), (0, 4 * D - rpb.size)).reshape(4, D),
            jnp.zeros((3, D), F32)], axis=0)

    wsm = pack(final_norm_g, norm_g, c_ctx, ret_norm_g, ret_decay_fwd, ret_decay_bwd, b_ada, na_rpb)
    msm = pack(m_final_norm_g, m_norm_g, m_c_ctx, m_ret_norm_g, m_ret_decay_fwd, m_ret_decay_bwd, m_b_ada, m_na_rpb)
    vsm = pack(v_final_norm_g, v_norm_g, v_c_ctx, v_ret_norm_g, v_ret_decay_fwd, v_ret_decay_bwd, v_b_ada, v_na_rpb)
    c_t = jnp.concatenate([c_all.reshape(N_DEV * B, D), c_ctx.reshape(1, D), jnp.zeros((7, D), F32)], axis=0).T
    outs = _small_final_call(sm_all, c_t, c_ctx.reshape(1, D), wada_f,
                             w_ada[0], m_w_ada[0], v_w_ada[0], wsm, msm, vsm, B)
    smalls, adas, loss = outs[0:4], outs[4:8], outs[8][0, 0]

    def unpack(p):
        rw = ret_norm_g.shape[1]
        return dict(
            final_norm_g=p[W_GF], norm_g=p[W_NG:W_NG + 1], c_ctx=p[W_CCTX], ret_norm_g=p[W_RNG:W_RNG + 1, :rw],
            ret_decay_fwd=p[W_DF:W_DF + 1, :4], ret_decay_bwd=p[W_DB:W_DB + 1, :4],
            b_ada=p[W_BADA:W_BADA + 3].reshape(1, 3 * D),
            na_rpb=p[W_RPB:W_RPB + 4].reshape(-1)[:na_rpb.size].reshape(na_rpb.shape))

    res = []
    for p, ada, win_o, wout_o in zip(smalls, adas, (g_win, d_win, nm_win, nv_win),
                                     (g_wout, d_wout, nm_wout, nv_wout)):
        u = unpack(p)
        res.append([u["c_ctx"], u["norm_g"], ada[None], u["b_ada"], win_o[None], u["na_rpb"],
                    u["ret_decay_fwd"], u["ret_decay_bwd"], u["ret_norm_g"], wout_o[None], u["final_norm_g"]])
    return (loss, grad_x, *res[0], *res[1], *res[2], *res[3])
```

```python
import functools

import numpy as np
import jax
import jax.numpy as jnp
from jax import lax
from jax.experimental import pallas as pl
from jax.experimental.pallas import tpu as pltpu

F32 = jnp.float32
BF16 = jnp.bfloat16
HIGHEST = lax.Precision.HIGHEST

D = 1024
GRID_W = 64
NA_DH = 64
RET_DK = 128
ROPE_BASE = 10000.0
EPS = 1e-6
NEG = -1e30
TQ = 256
TK = 512
KW = 12 * GRID_W
N_SHARD = 4
N_DEV = 8
SM_ROWS = 24

ADAM_LR = 0.001
ADAM_B1 = 0.9
ADAM_B2 = 0.999
ADAM_EPS = 1e-08
ADAM_WD = 0.01
ADAM_STEP = 10

MESH = pl.DeviceIdType.MESH
ANY = pl.BlockSpec(memory_space=pl.ANY)


def _params(sem=None, vmem_mb=48):
    return pltpu.CompilerParams(dimension_semantics=sem, vmem_limit_bytes=vmem_mb << 20)


def _dot(a, b):
    return jnp.dot(a, b, preferred_element_type=F32)


def _dot_nt(a, b):
    return lax.dot_general(a, b, (((1,), (1,)), ((), ())), preferred_element_type=F32)


def _dot_tn(a, b):
    return lax.dot_general(a, b, (((0,), (0,)), ((), ())), preferred_element_type=F32)


def _sigmoid(x):
    return 1.0 / (1.0 + jnp.exp(-x))


def _rope_tables(L, LC):
    half = RET_DK // 2
    nf = half // 2
    t = np.arange(L)
    row = (t // GRID_W).astype(np.float32)
    col = (t % GRID_W).astype(np.float32)
    inv = (np.float32(ROPE_BASE) ** (-np.arange(nf, dtype=np.float32) / np.float32(nf))).astype(np.float32)
    ang = np.concatenate([row[:, None] * inv, col[:, None] * inv], axis=-1).astype(np.float32)
    cos, sin = np.cos(ang).astype(np.float32), np.sin(ang).astype(np.float32)
    cos2 = np.concatenate([cos, cos], axis=-1)
    sin2 = np.concatenate([-sin, sin], axis=-1)
    cos2 = np.concatenate([cos2, np.ones((LC, RET_DK), np.float32)], axis=0)
    sin2 = np.concatenate([sin2, np.zeros((LC, RET_DK), np.float32)], axis=0)
    return jnp.asarray(cos2), jnp.asarray(sin2)


def _mod_call(c8, wada_f, b_ada):
    ws = wada_f.shape[2]

    def body(c_ref, w_ref, b_ref, o_ref):
        a = c_ref[...]
        a = (a * _sigmoid(a)).astype(BF16)
        for s in range(N_SHARD):
            o_ref[:, s * ws:(s + 1) * ws] = _dot(a, w_ref[s]) + b_ref[:, s * ws:(s + 1) * ws]

    return pl.pallas_call(
        body, name="ada_mod", out_shape=jax.ShapeDtypeStruct((8, 3 * D), F32),
        compiler_params=_params())(c8, wada_f, b_ada)


def _dc_masks():
    cq = lax.broadcasted_iota(jnp.int32, (GRID_W, GRID_W), 0)
    ck = lax.broadcasted_iota(jnp.int32, (GRID_W, GRID_W), 1)
    dc = jnp.clip(ck - cq + 15, 0, 30)
    c0 = jnp.clip(cq - 8, 0, GRID_W - 16)
    col_ok = (ck >= c0) & (ck < c0 + 16)
    return dc, col_ok


def _bias_blocks():
    out = []
    for typ, delta in enumerate((4, 0, -4)):
        for rq in range(4):
            for rkk in range(12):
                dr = rkk + delta - rq - 4
                if typ == 0:
                    ok = -rq <= dr <= 7 - rq
                elif typ == 1:
                    ok = -4 <= dr <= 3
                else:
                    ok = -4 - rq <= dr <= 3 - rq
                out.append((typ, rq, rkk, dr if ok else None))
    return out


def _bias_call(rpb_flat):
    nh = rpb_flat.shape[0]

    def body(r_ref, bias_ref, et_ref):
        dc, col_ok = _dc_masks()
        masks = [(dc == j).astype(F32) for j in range(31)]

        def per_h(h, carry):
            for dr in range(15):
                t = jnp.zeros((GRID_W, GRID_W), F32)
                for j in range(31):
                    t = t + masks[j] * r_ref[h, dr * 31 + j]
                et_ref[dr] = jnp.where(col_ok, t, NEG)
            neg = jnp.full((GRID_W, GRID_W), NEG, F32)
            for typ, rq, rkk, dr in _bias_blocks():
                blk = neg if dr is None else et_ref[dr + 7]
                bias_ref[h, typ, rq * 64:(rq + 1) * 64, rkk * 64:(rkk + 1) * 64] = blk
            return carry

        lax.fori_loop(0, nh, per_h, 0)

    return pl.pallas_call(
        body, name="rpb_bias",
        out_shape=jax.ShapeDtypeStruct((nh, 3, TQ, KW), F32),
        in_specs=[pl.BlockSpec(memory_space=pltpu.SMEM)],
        out_specs=pl.BlockSpec(memory_space=pltpu.VMEM),
        scratch_shapes=[pltpu.VMEM((15, GRID_W, GRID_W), F32)],
        compiler_params=_params())(rpb_flat)


def _small_reduce_call(dbias, dlg, B):
    nh = dbias.shape[0]

    def body(db_ref, dlg_ref, drpb_ref, dlgo_ref, p_ref):
        dc, _ = _dc_masks()
        masks = [(dc == j).astype(F32) for j in range(31)]
        ones = jnp.ones((8, GRID_W), F32)
        p_ref[...] = jnp.zeros_like(p_ref)
        drpb_ref[...] = jnp.zeros_like(drpb_ref)

        def per_h(h, carry):
            acc = {}
            for typ, rq, rkk, dr in _bias_blocks():
                if dr is None:
                    continue
                blk = db_ref[h, typ, rq * 64:(rq + 1) * 64, rkk * 64:(rkk + 1) * 64]
                acc[dr] = blk if dr not in acc else acc[dr] + blk
            for dr in range(-7, 8):
                t = acc[dr]
                for j in range(31):
                    p_ref[j:j + 1, :] = jnp.sum(t * masks[j], axis=0, keepdims=True)
                red = lax.dot_general(ones, p_ref[...], (((1,), (1,)), ((), ())),
                                      precision=HIGHEST, preferred_element_type=F32)
                drpb_ref[h, dr + 7:dr + 8, :] = red[0:1, :]
            return carry

        lax.fori_loop(0, nh, per_h, 0)
        x = dlg_ref[0]
        for b in range(1, B):
            x = x + dlg_ref[b]
        x = x.reshape(4 * 8, x.shape[-1])
        dlgo_ref[...] = jnp.dot(x, jnp.ones((x.shape[-1], 128), F32), precision=HIGHEST,
                                preferred_element_type=F32)

    return pl.pallas_call(
        body, name="small_reduce",
        out_shape=(jax.ShapeDtypeStruct((nh, 16, 32), F32), jax.ShapeDtypeStruct((32, 128), F32)),
        scratch_shapes=[pltpu.VMEM((32, GRID_W), F32)],
        compiler_params=_params())(dbias, dlg)


def _inproj_call(x, ctx, mod, norm_g, win_f, cos2, sin2):
    B, L, _ = x.shape
    LC = ctx.shape[1]
    T = L + LC
    nl = L // TQ
    assert LC == TQ and L % TQ == 0
    kscale = RET_DK ** -0.5

    def body(x_ref, ctx_ref, mod_ref, g_ref, w_ref, cos_ref, sin_ref, p_ref, h_ref):
        b = pl.program_id(0)
        t = pl.program_id(1)
        is_lat = t < nl
        xt = jnp.where(is_lat, x_ref[...], ctx_ref[...])
        mrow = mod_ref[pl.ds(jnp.where(is_lat, b, B), 1), :]
        shift, scale = mrow[:, 0:D], mrow[:, D:2 * D]
        rstd = lax.rsqrt(jnp.mean(xt * xt, axis=-1, keepdims=True) + EPS)
        hb = ((xt * rstd * g_ref[...]) * (1.0 + scale) + shift).astype(BF16)
        h_ref[...] = hb
        cs, sn = cos_ref[...], sin_ref[...]
        for sec in range(8):
            s, half = divmod(sec, 2)
            acc = _dot(hb, w_ref[s, :, half * 512:(half + 1) * 512])
            if sec == 0:
                acc = acc * (NA_DH ** -0.5)
            if sec in (4, 5):
                for j in range(4):
                    a = acc[:, j * 128:(j + 1) * 128]
                    r = a * cs + pltpu.roll(a, 64, 1) * sn
                    if sec == 5:
                        r = r * kscale
                    p_ref[:, sec * 512 + j * 128:sec * 512 + (j + 1) * 128] = r.astype(BF16)
            else:
                p_ref[:, sec * 512:(sec + 1) * 512] = acc.astype(BF16)

    return pl.pallas_call(
        body, name="in_proj", grid=(B, T // TQ),
        in_specs=[
            pl.BlockSpec((None, TQ, D), lambda b, t: (b, jnp.minimum(t, nl - 1), 0)),
            pl.BlockSpec((None, TQ, D), lambda b, t: (b, 0, 0)),
            pl.BlockSpec((8, 3 * D), lambda b, t: (0, 0)),
            pl.BlockSpec((1, D), lambda b, t: (0, 0)),
            pl.BlockSpec((N_SHARD, D, D), lambda b, t: (0, 0, 0)),
            pl.BlockSpec((TQ, RET_DK), lambda b, t: (t, 0)),
            pl.BlockSpec((TQ, RET_DK), lambda b, t: (t, 0)),
        ],
        out_specs=(pl.BlockSpec((None, TQ, 4 * D), lambda b, t: (b, t, 0)),
                   pl.BlockSpec((None, TQ, D), lambda b, t: (b, t, 0))),
        out_shape=(jax.ShapeDtypeStruct((B, T, 4 * D), BF16), jax.ShapeDtypeStruct((B, T, D), BF16)),
        compiler_params=_params(("arbitrary", "arbitrary")))(x, ctx, mod, norm_g, win_f, cos2, sin2)


def _na_specs(L, T, rows):
    nm = rows // 4
    q_spec = pl.BlockSpec((None, TQ, 128), lambda hp, b, m: (b, m, hp))
    k_spec = pl.BlockSpec((None, T, 128), lambda hp, b, m: (b, 0, 4 + hp))
    v_spec = pl.BlockSpec((None, T, 128), lambda hp, b, m: (b, 0, 8 + hp))
    g_spec = pl.BlockSpec((None, TQ, 128), lambda hp, b, m: (b, m, 12 + hp))
    bias_spec = pl.BlockSpec((2, 3, TQ, KW), lambda hp, b, m: (hp, 0, 0, 0))
    return nm, q_spec, k_spec, v_spec, g_spec, bias_spec


def _na_tile(m, nm, rows):
    typ = jnp.where(m == 0, 0, jnp.where(m == nm - 1, 2, 1))
    start = pl.multiple_of(jnp.clip(4 * m - 4, 0, rows - 12) * GRID_W, TQ)
    return typ, start


def _na_fwd_call(P, bias, L, LC):
    B, T, _ = P.shape
    rows = L // GRID_W
    nm, q_spec, k_spec, v_spec, g_spec, bias_spec = _na_specs(L, T, rows)

    def body(q_ref, k_ref, v_ref, g_ref, bias_ref, y_ref):
        typ, start = _na_tile(pl.program_id(2), nm, rows)
        for hh in range(2):
            ln = slice(hh * NA_DH, (hh + 1) * NA_DH)
            q = q_ref[:, ln]
            kw, vw = k_ref[pl.ds(start, KW), ln], v_ref[pl.ds(start, KW), ln]
            kc, vc = k_ref[L:L + LC, ln], v_ref[L:L + LC, ln]
            s1 = _dot_nt(q, kw) + bias_ref[hh, typ]
            s2 = _dot_nt(q, kc)
            mx = jnp.maximum(jnp.max(s1, axis=-1, keepdims=True), jnp.max(s2, axis=-1, keepdims=True))
            p1, p2 = jnp.exp(s1 - mx), jnp.exp(s2 - mx)
            inv = 1.0 / (jnp.sum(p1, axis=-1, keepdims=True) + jnp.sum(p2, axis=-1, keepdims=True))
            o = _dot((p1 * inv).astype(BF16), vw) + _dot((p2 * inv).astype(BF16), vc)
            g = g_ref[:, ln].astype(F32)
            y_ref[:, ln] = (o * (g * _sigmoid(g))).astype(BF16)

    return pl.pallas_call(
        body, name="na_fwd", grid=(4, B, nm),
        in_specs=[q_spec, k_spec, v_spec, g_spec, bias_spec],
        out_specs=pl.BlockSpec((None, TQ, 128), lambda hp, b, m: (b, m, hp)),
        out_shape=jax.ShapeDtypeStruct((B, L, 512), BF16),
        compiler_params=_params(("arbitrary",) * 3))(P, P, P, P, bias)


def _na_bwd_call(P, bias, dY, L, LC):
    B, T, _ = P.shape
    rows = L // GRID_W
    nm, q_spec, k_spec, v_spec, g_spec, bias_spec = _na_specs(L, T, rows)
    scale = NA_DH ** -0.5

    def body(q_ref, k_ref, v_ref, g_ref, bias_ref, dy_ref, dq_ref, dg_ref, dk_ref, dv_ref, db_ref):
        b, m = pl.program_id(1), pl.program_id(2)
        typ, start = _na_tile(m, nm, rows)

        @pl.when(m == 0)
        def _():
            dk_ref[...] = jnp.zeros_like(dk_ref)
            dv_ref[...] = jnp.zeros_like(dv_ref)

        @pl.when((m == 0) & (b == 0))
        def _():
            db_ref[...] = jnp.zeros_like(db_ref)

        for hh in range(2):
            ln = slice(hh * NA_DH, (hh + 1) * NA_DH)
            q = q_ref[:, ln]
            kw, vw = k_ref[pl.ds(start, KW), ln], v_ref[pl.ds(start, KW), ln]
            kc, vc = k_ref[L:L + LC, ln], v_ref[L:L + LC, ln]
            s1 = _dot_nt(q, kw) + bias_ref[hh, typ]
            s2 = _dot_nt(q, kc)
            mx = jnp.maximum(jnp.max(s1, axis=-1, keepdims=True), jnp.max(s2, axis=-1, keepdims=True))
            p1, p2 = jnp.exp(s1 - mx), jnp.exp(s2 - mx)
            inv = 1.0 / (jnp.sum(p1, axis=-1, keepdims=True) + jnp.sum(p2, axis=-1, keepdims=True))
            p1, p2 = p1 * inv, p2 * inv
            p1b, p2b = p1.astype(BF16), p2.astype(BF16)
            o = _dot(p1b, vw) + _dot(p2b, vc)
            g = g_ref[:, ln].astype(F32)
            sg = _sigmoid(g)
            dy = dy_ref[:, ln].astype(F32)
            dg_ref[:, ln] = (dy * o * (sg * (1.0 + g * (1.0 - sg)))).astype(BF16)
            do = (dy * (g * sg)).astype(BF16)
            dp1, dp2 = _dot_nt(do, vw), _dot_nt(do, vc)
            delta = jnp.sum(p1 * dp1, axis=-1, keepdims=True) + jnp.sum(p2 * dp2, axis=-1, keepdims=True)
            ds1, ds2 = p1 * (dp1 - delta), p2 * (dp2 - delta)
            db_ref[hh, typ] += ds1
            ds1b, ds2b = ds1.astype(BF16), ds2.astype(BF16)
            dq_ref[:, ln] = ((_dot(ds1b, kw) + _dot(ds2b, kc)) * scale).astype(BF16)
            dk_ref[pl.ds(start, KW), ln] += _dot_tn(ds1b, q)
            dv_ref[pl.ds(start, KW), ln] += _dot_tn(p1b, do)
            dk_ref[L:L + LC, ln] += _dot_tn(ds2b, q)
            dv_ref[L:L + LC, ln] += _dot_tn(p2b, do)

    tile = pl.BlockSpec((None, TQ, 128), lambda hp, b, m: (b, m, hp))
    kv_out = pl.BlockSpec((None, T, 128), lambda hp, b, m: (b, 0, hp))
    return pl.pallas_call(
        body, name="na_bwd", grid=(4, B, nm),
        in_specs=[q_spec, k_spec, v_spec, g_spec, bias_spec, tile],
        out_specs=(tile, tile, kv_out, kv_out, bias_spec),
        out_shape=(jax.ShapeDtypeStruct((B, L, 512), BF16), jax.ShapeDtypeStruct((B, L, 512), BF16),
                   jax.ShapeDtypeStruct((B, T, 512), F32), jax.ShapeDtypeStruct((B, T, 512), F32),
                   jax.ShapeDtypeStruct(bias.shape, F32)),
        compiler_params=_params(("arbitrary",) * 3))(P, P, P, P, bias, dY)


def _head_scalar(dec_ref, h):
    lane = lax.broadcasted_iota(jnp.int32, dec_ref.shape, 1)
    return -jnp.sum(jnp.where(lane == h, jnp.exp(dec_ref[...]), 0.0), axis=1, keepdims=True)


def _decay_lat(tpos, ks, lgf, lgb):
    spos = (ks + lax.broadcasted_iota(jnp.int32, (1, TK), 1)).astype(F32)
    dist = tpos - spos
    dm = jnp.exp(dist * jnp.where(dist > 0, lgf, -lgb)) * jnp.where(dist == 0, 2.0, 1.0)
    return dist, dm


def _decay_ctx(tpos, L, LC, lgf, lgb):
    jc = lax.broadcasted_iota(jnp.int32, (1, LC), 1).astype(F32)
    df = tpos + (float(LC) - jc)
    db = (float(L) - tpos) + jc
    return df, db, jnp.exp(lgf * df), jnp.exp(lgb * db)


def _ret_specs(T):
    q_spec = pl.BlockSpec((None, TQ, 128), lambda b, h, i: (b, i, 16 + h))
    k_spec = pl.BlockSpec((None, T, 128), lambda b, h, i: (b, 0, 20 + h))
    v_spec = pl.BlockSpec((None, T, 128), lambda b, h, i: (b, 0, 24 + h))
    g_spec = pl.BlockSpec((None, TQ, 128), lambda b, h, i: (b, i, 28 + h))
    dec_spec = pl.BlockSpec((1, 4), lambda b, h, i: (0, 0))
    gn_spec = pl.BlockSpec((1, 128), lambda b, h, i: (0, h))
    return q_spec, k_spec, v_spec, g_spec, dec_spec, gn_spec


def _ret_fwd_call(P, dec_f, dec_b, ret_norm_g, L, LC):
    B, T, _ = P.shape
    q_spec, k_spec, v_spec, g_spec, dec_spec, gn_spec = _ret_specs(T)

    def body(df_ref, db_ref, q_ref, k_ref, v_ref, g_ref, gn_ref, y_ref, o_ref):
        h, i = pl.program_id(1), pl.program_id(2)
        lgf, lgb = _head_scalar(df_ref, h), _head_scalar(db_ref, h)
        tpos = (i * TQ + lax.broadcasted_iota(jnp.int32, (TQ, 1), 0)).astype(F32)
        q = q_ref[...]

        def chunk(j, acc):
            ks = pl.multiple_of(j * TK, TK)
            kj, vj = k_ref[pl.ds(ks, TK), :], v_ref[pl.ds(ks, TK), :]
            _, dm = _decay_lat(tpos, ks, lgf, lgb)
            return acc + _dot((_dot_nt(q, kj) * dm).astype(BF16), vj)

        acc = lax.fori_loop(0, L // TK, chunk, jnp.zeros((TQ, RET_DK), F32))
        _, _, ef, eb = _decay_ctx(tpos, L, LC, lgf, lgb)
        acc = acc + _dot((_dot_nt(q, k_ref[L:L + LC, :]) * (ef + eb)).astype(BF16), v_ref[L:L + LC, :])
        o_ref[...] = acc
        rn = lax.rsqrt(jnp.mean(acc * acc, axis=-1, keepdims=True) + EPS)
        g = g_ref[...].astype(F32)
        y_ref[...] = ((acc * rn * gn_ref[...]).astype(F32) * (g * _sigmoid(g))).astype(BF16)

    tile = pl.BlockSpec((None, TQ, 128), lambda b, h, i: (b, i, h))
    return pl.pallas_call(
        body, name="ret_fwd", grid=(B, 4, L // TQ),
        in_specs=[dec_spec, dec_spec, q_spec, k_spec, v_spec, g_spec, gn_spec],
        out_specs=(tile, tile),
        out_shape=(jax.ShapeDtypeStruct((B, L, 512), BF16), jax.ShapeDtypeStruct((B, L, 512), F32)),
        compiler_params=_params(("arbitrary",) * 3))(dec_f, dec_b, P, P, P, P, ret_norm_g)


def _ret_bwd_call(P, dec_f, dec_b, ret_norm_g, o_ret, dY, cos2, sin2, L, LC):
    B, T, _ = P.shape
    ni = L // TQ
    kscale = RET_DK ** -0.5
    q_spec, k_spec, v_spec, g_spec, dec_spec, gn_spec = _ret_specs(T)

    def body(df_ref, db_ref, q_ref, k_ref, v_ref, g_ref, gn_ref, o_ref, dy_ref, cos_ref, sin_ref,
             dq_ref, dg_ref, dk_ref, dv_ref, dgn_ref, dlg_ref):
        h, i = pl.program_id(1), pl.program_id(2)
        lgf, lgb = _head_scalar(df_ref, h), _head_scalar(db_ref, h)
        tpos = (i * TQ + lax.broadcasted_iota(jnp.int32, (TQ, 1), 0)).astype(F32)

        @pl.when(i == 0)
        def _():
            dk_ref[...] = jnp.zeros_like(dk_ref)
            dv_ref[...] = jnp.zeros_like(dv_ref)
            dgn_ref[...] = jnp.zeros_like(dgn_ref)
            dlg_ref[...] = jnp.zeros_like(dlg_ref)

        q = q_ref[...]
        o = o_ref[...]
        g = g_ref[...].astype(F32)
        dy = dy_ref[...].astype(F32)
        gn = gn_ref[...]
        sg = _sigmoid(g)
        rn = lax.rsqrt(jnp.mean(o * o, axis=-1, keepdims=True) + EPS)
        nrm = o * rn
        dg_ref[...] = (dy * (nrm * gn) * (sg * (1.0 + g * (1.0 - sg)))).astype(BF16)
        dhn = dy * (g * sg)
        dgn_ref[...] += jnp.sum(dhn * nrm, axis=0, keepdims=True)
        dnrm = dhn * gn
        do = rn * (dnrm - nrm * jnp.mean(dnrm * nrm, axis=-1, keepdims=True))
        dob = do.astype(BF16)

        def chunk(j, dq):
            ks = pl.multiple_of(j * TK, TK)
            kj, vj = k_ref[pl.ds(ks, TK), :], v_ref[pl.ds(ks, TK), :]
            dist, dm = _decay_lat(tpos, ks, lgf, lgb)
            s = _dot_nt(q, kj)
            dsv = _dot_nt(dob, vj)
            dsb = (dsv * dm).astype(BF16)
            dk_ref[pl.ds(ks, TK), :] += _dot_tn(dsb, q)
            dv_ref[pl.ds(ks, TK), :] += _dot_tn((s * dm).astype(BF16), dob)
            xw = s * dsv * dm * jnp.abs(dist)
            tot = jnp.sum(xw, axis=0, keepdims=True)
            fwd = jnp.sum(jnp.where(dist > 0, xw, 0.0), axis=0, keepdims=True)
            dlg_ref[0:1, :] += fwd
            dlg_ref[1:2, :] += tot - fwd
            return dq + _dot(dsb, kj)

        dq = lax.fori_loop(0, L // TK, chunk, jnp.zeros((TQ, RET_DK), F32))
        kc, vc = k_ref[L:L + LC, :], v_ref[L:L + LC, :]
        dfc, dbc, ef, eb = _decay_ctx(tpos, L, LC, lgf, lgb)
        s = _dot_nt(q, kc)
        dsv = _dot_nt(dob, vc)
        dsb = (dsv * (ef + eb)).astype(BF16)
        dk_ref[L:L + LC, :] += _dot_tn(dsb, q)
        dv_ref[L:L + LC, :] += _dot_tn((s * (ef + eb)).astype(BF16), dob)
        a = s * dsv
        dlg_ref[0:1, 0:LC] += jnp.sum(a * ef * dfc, axis=0, keepdims=True)
        dlg_ref[1:2, 0:LC] += jnp.sum(a * eb * dbc, axis=0, keepdims=True)
        dq = dq + _dot(dsb, kc)
        cs, sn = cos_ref[pl.ds(pl.multiple_of(i * TQ, TQ), TQ), :], sin_ref[pl.ds(pl.multiple_of(i * TQ, TQ), TQ), :]
        dq_ref[...] = (dq * cs - pltpu.roll(dq, 64, 1) * sn).astype(BF16)

        @pl.when(i == ni - 1)
        def _():
            dk = dk_ref[...]
            dk_ref[...] = (dk * cos_ref[...] - pltpu.roll(dk, 64, 1) * sin_ref[...]) * kscale

    tile = pl.BlockSpec((None, TQ, 128), lambda b, h, i: (b, i, h))
    kv_out = pl.BlockSpec((None, T, 128), lambda b, h, i: (b, 0, h))
    tab = pl.BlockSpec((T, RET_DK), lambda b, h, i: (0, 0))
    return pl.pallas_call(
        body, name="ret_bwd", grid=(B, 4, ni),
        in_specs=[dec_spec, dec_spec, q_spec, k_spec, v_spec, g_spec, gn_spec, tile,
                  pl.BlockSpec((None, TQ, 128), lambda b, h, i: (b, i, 4 + h)), tab, tab],
        out_specs=(tile, tile, kv_out, kv_out,
                   pl.BlockSpec((None, 1, 128), lambda b, h, i: (b, 0, h)),
                   pl.BlockSpec((None, None, 8, TK), lambda b, h, i: (b, h, 0, 0))),
        out_shape=(jax.ShapeDtypeStruct((B, L, 512), BF16), jax.ShapeDtypeStruct((B, L, 512), BF16),
                   jax.ShapeDtypeStruct((B, T, 512), F32), jax.ShapeDtypeStruct((B, T, 512), F32),
                   jax.ShapeDtypeStruct((B, 1, 512), F32), jax.ShapeDtypeStruct((B, 4, 8, TK), F32)),
        compiler_params=_params(("arbitrary",) * 3))(
            dec_f, dec_b, P, P, P, P, ret_norm_g, o_ret, dY, cos2, sin2)


def _chunk_decay(lgf, lgb):
    tau = lax.broadcasted_iota(jnp.int32, (TQ, 1), 0).astype(F32)
    sig = lax.broadcasted_iota(jnp.int32, (1, TQ), 1).astype(F32)
    dist = tau - sig
    dm = jnp.exp(dist * jnp.where(dist > 0, lgf, -lgb)) * jnp.where(dist == 0, 2.0, 1.0)
    return tau, dist, dm


def _ret_states_call(P, dec_f, dec_b, L, LC):
    B, T, _ = P.shape
    n = L // TQ

    def body(df_ref, db_ref, k_ref, v_ref, sf_ref, sb_ref):
        h = pl.program_id(1)
        lgf, lgb = _head_scalar(df_ref, h), _head_scalar(db_ref, h)
        tau = lax.broadcasted_iota(jnp.int32, (TQ, 1), 0).astype(F32)
        jc = lax.broadcasted_iota(jnp.int32, (LC, 1), 0).astype(F32)
        wf, wb = jnp.exp(lgf * (TQ - 1.0 - tau)), jnp.exp(lgb * tau)
        gcf, gcb = jnp.exp(lgf * float(TQ)), jnp.exp(lgb * float(TQ))
        kc, vc = k_ref[L:L + LC, :].astype(F32), v_ref[L:L + LC, :]

        def chunk_state(i, w):
            ks = pl.multiple_of(i * TQ, TQ)
            return _dot_tn((k_ref[pl.ds(ks, TQ), :].astype(F32) * w).astype(BF16), v_ref[pl.ds(ks, TQ), :])

        def fwd(i, s):
            sf_ref[i] = s
            return gcf * s + chunk_state(i, wf)

        lax.fori_loop(0, n, fwd, _dot_tn((kc * jnp.exp(lgf * (LC - 1.0 - jc))).astype(BF16), vc))

        def bwd(r, s):
            i = n - 1 - r
            sb_ref[i] = s
            return gcb * s + chunk_state(i, wb)

        lax.fori_loop(0, n, bwd, _dot_tn((kc * jnp.exp(lgb * jc)).astype(BF16), vc))

    st = pl.BlockSpec((None, None, n, RET_DK, RET_DK), lambda b, h: (b, h, 0, 0, 0))
    return pl.pallas_call(
        body, name="ret_states", grid=(B, 4),
        in_specs=[pl.BlockSpec((1, 4), lambda b, h: (0, 0)), pl.BlockSpec((1, 4), lambda b, h: (0, 0)),
                  pl.BlockSpec((None, T, 128), lambda b, h: (b, 0, 20 + h)),
                  pl.BlockSpec((None, T, 128), lambda b, h: (b, 0, 24 + h))],
        out_specs=(st, st),
        out_shape=(jax.ShapeDtypeStruct((B, 4, n, RET_DK, RET_DK), F32),) * 2,
        compiler_params=_params(("arbitrary",) * 2))(dec_f, dec_b, P, P)


def _retc_fwd_call(P, sf, sb, dec_f, dec_b, ret_norm_g, L):
    B, T, _ = P.shape
    q_spec, _, _, g_spec, dec_spec, gn_spec = _ret_specs(T)
    k_spec = pl.BlockSpec((None, TQ, 128), lambda b, h, i: (b, i, 20 + h))
    v_spec = pl.BlockSpec((None, TQ, 128), lambda b, h, i: (b, i, 24 + h))
    st_spec = pl.BlockSpec((None, None, None, RET_DK, RET_DK), lambda b, h, i: (b, h, i, 0, 0))

    def body(df_ref, db_ref, q_ref, k_ref, v_ref, g_ref, gn_ref, sf_ref, sb_ref, y_ref, o_ref):
        h = pl.program_id(1)
        lgf, lgb = _head_scalar(df_ref, h), _head_scalar(db_ref, h)
        tau, _, dm = _chunk_decay(lgf, lgb)
        q = q_ref[...]
        qf = q.astype(F32)
        acc = _dot((_dot_nt(q, k_ref[...]) * dm).astype(BF16), v_ref[...])
        acc = acc + _dot((qf * jnp.exp(lgf * (tau + 1.0))).astype(BF16), sf_ref[...].astype(BF16))
        acc = acc + _dot((qf * jnp.exp(lgb * (TQ - tau))).astype(BF16), sb_ref[...].astype(BF16))
        o_ref[...] = acc
        rn = lax.rsqrt(jnp.mean(acc * acc, axis=-1, keepdims=True) + EPS)
        g = g_ref[...].astype(F32)
        y_ref[...] = ((acc * rn * gn_ref[...]) * (g * _sigmoid(g))).astype(BF16)

    tile = pl.BlockSpec((None, TQ, 128), lambda b, h, i: (b, i, h))
    return pl.pallas_call(
        body, name="ret_fwd", grid=(B, 4, L // TQ),
        in_specs=[dec_spec, dec_spec, q_spec, k_spec, v_spec, g_spec, gn_spec, st_spec, st_spec],
        out_specs=(tile, tile),
        out_shape=(jax.ShapeDtypeStruct((B, L, 512), BF16), jax.ShapeDtypeStruct((B, L, 512), F32)),
        compiler_params=_params(("arbitrary",) * 3))(dec_f, dec_b, P, P, P, P, ret_norm_g, sf, sb)


def _retc_bwd_call(P, sf, sb, dec_f, dec_b, ret_norm_g, o_ret, dY, cos2, sin2, L, LC):
    B, T, _ = P.shape
    n = L // TQ
    C = float(TQ)
    kscale = RET_DK ** -0.5
    q_spec, k_spec, v_spec, g_spec, dec_spec, gn_spec = _ret_specs(T)
    st_spec = pl.BlockSpec((None, None, n, RET_DK, RET_DK), lambda b, h, i: (b, h, 0, 0, 0))

    def body(df_ref, db_ref, q_ref, k_ref, v_ref, g_ref, gn_ref, o_ref, dy_ref, cos_ref, sin_ref, sf_ref, sb_ref,
             dq_ref, dg_ref, dk_ref, dv_ref, dgn_ref, dlg_ref, dsf_ref, dsb_ref):
        h, i = pl.program_id(1), pl.program_id(2)
        lgf, lgb = _head_scalar(df_ref, h), _head_scalar(db_ref, h)
        tau, dist, dm = _chunk_decay(lgf, lgb)

        @pl.when(i == 0)
        def _():
            dk_ref[...] = jnp.zeros_like(dk_ref)
            dv_ref[...] = jnp.zeros_like(dv_ref)
            dgn_ref[...] = jnp.zeros_like(dgn_ref)
            dlg_ref[...] = jnp.zeros_like(dlg_ref)

        def add_lg(row, x):
            cs = jnp.sum(x, axis=0, keepdims=True)
            tot = cs[:, 0:128]
            for part in range(1, x.shape[1] // 128):
                tot = tot + cs[:, part * 128:(part + 1) * 128]
            dlg_ref[row:row + 1, :] += tot

        q = q_ref[...]
        qf = q.astype(F32)
        o = o_ref[...]
        g = g_ref[...].astype(F32)
        dy = dy_ref[...].astype(F32)
        gn = gn_ref[...]
        sg = _sigmoid(g)
        rn = lax.rsqrt(jnp.mean(o * o, axis=-1, keepdims=True) + EPS)
        nrm = o * rn
        dg_ref[...] = (dy * (nrm * gn) * (sg * (1.0 + g * (1.0 - sg)))).astype(BF16)
        dhn = dy * (g * sg)
        dgn_ref[...] += jnp.sum(dhn * nrm, axis=0, keepdims=True)
        dnrm = dhn * gn
        do = rn * (dnrm - nrm * jnp.mean(dnrm * nrm, axis=-1, keepdims=True))
        dob = do.astype(BF16)
        rows = pl.ds(pl.multiple_of(i * TQ, TQ), TQ)
        ki, vi = k_ref[rows, :], v_ref[rows, :]
        s = _dot_nt(q, ki)
        dsv = _dot_nt(dob, vi)
        dsb = (dsv * dm).astype(BF16)
        dk_ref[rows, :] += _dot_tn(dsb, q)
        dv_ref[rows, :] += _dot_tn((s * dm).astype(BF16), dob)
        xw = s * dsv * dm * jnp.abs(dist)
        fpart = jnp.where(dist > 0, xw, 0.0)
        add_lg(0, fpart)
        add_lg(1, xw - fpart)
        dq = _dot(dsb, ki)
        af, ab = jnp.exp(lgf * (tau + 1.0)), jnp.exp(lgb * (C - tau))
        qa, qb = (qf * af).astype(BF16), (qf * ab).astype(BF16)
        sfi, sbi = sf_ref[i].astype(BF16), sb_ref[i].astype(BF16)
        dq = dq + af * _dot_nt(dob, sfi) + ab * _dot_nt(dob, sbi)
        dsf_ref[i] = _dot_tn(qa, dob)
        dsb_ref[i] = _dot_tn(qb, dob)
        add_lg(0, (tau + 1.0) * (_dot(qa, sfi) * do))
        add_lg(1, (C - tau) * (_dot(qb, sbi) * do))
        cs, sn = cos_ref[rows, :], sin_ref[rows, :]
        dq_ref[...] = (dq * cs - pltpu.roll(dq, 64, 1) * sn).astype(BF16)

        @pl.when(i == n - 1)
        def _():
            jc = lax.broadcasted_iota(jnp.int32, (LC, 1), 0).astype(F32)
            crow = pl.ds(L, LC)

            def through_state(rws, w, dw, gst, row):
                kk, vv = k_ref[rws, :].astype(F32), v_ref[rws, :]
                gb = gst.astype(BF16)
                vg = _dot_nt(vv, gb)
                kw = kk * w
                dk_ref[rws, :] += w * vg
                dv_ref[rws, :] += _dot(kw.astype(BF16), gb)
                add_lg(row, dw * (kw * vg))

            def scan(lg, gc, w, dw, st_ref, dst_ref, order, row):
                def step(r, gst):
                    j = order(r)
                    through_state(pl.ds(pl.multiple_of(j * TQ, TQ), TQ), w, dw, gst, row)
                    add_lg(row, (C * gc) * (gst * st_ref[j]))
                    return dst_ref[j] + gc * gst
                return lax.fori_loop(0, n, step, jnp.zeros((RET_DK, RET_DK), F32))

            gcf, gcb = jnp.exp(lgf * C), jnp.exp(lgb * C)
            g0 = scan(lgf, gcf, jnp.exp(lgf * (C - 1.0 - tau)), C - 1.0 - tau, sf_ref, dsf_ref,
                      lambda r: n - 1 - r, 0)
            through_state(crow, jnp.exp(lgf * (LC - 1.0 - jc)), LC - 1.0 - jc, g0, 0)
            g1 = scan(lgb, gcb, jnp.exp(lgb * tau), tau, sb_ref, dsb_ref, lambda r: r, 1)
            through_state(crow, jnp.exp(lgb * jc), jc, g1, 1)
            dk = dk_ref[...]
            dk_ref[...] = (dk * cos_ref[...] - pltpu.roll(dk, 64, 1) * sin_ref[...]) * kscale

    tile = pl.BlockSpec((None, TQ, 128), lambda b, h, i: (b, i, h))
    kv_out = pl.BlockSpec((None, T, 128), lambda b, h, i: (b, 0, h))
    tab = pl.BlockSpec((T, RET_DK), lambda b, h, i: (0, 0))
    return pl.pallas_call(
        body, name="ret_bwd", grid=(B, 4, n),
        in_specs=[dec_spec, dec_spec, q_spec, k_spec, v_spec, g_spec, gn_spec, tile,
                  pl.BlockSpec((None, TQ, 128), lambda b, h, i: (b, i, 4 + h)), tab, tab, st_spec, st_spec],
        out_specs=(tile, tile, kv_out, kv_out,
                   pl.BlockSpec((None, 1, 128), lambda b, h, i: (b, 0, h)),
                   pl.BlockSpec((None, None, 8, 128), lambda b, h, i: (b, h, 0, 0))),
        out_shape=(jax.ShapeDtypeStruct((B, L, 512), BF16), jax.ShapeDtypeStruct((B, L, 512), BF16),
                   jax.ShapeDtypeStruct((B, T, 512), F32), jax.ShapeDtypeStruct((B, T, 512), F32),
                   jax.ShapeDtypeStruct((B, 1, 512), F32), jax.ShapeDtypeStruct((B, 4, 8, 128), F32)),
        scratch_shapes=[pltpu.VMEM((n, RET_DK, RET_DK), F32), pltpu.VMEM((n, RET_DK, RET_DK), F32)],
        compiler_params=_params(("arbitrary",) * 3))(
            dec_f, dec_b, P, P, P, P, ret_norm_g, o_ret, dY, cos2, sin2, sf, sb)


def _out_call(y_na, y_ret, x, target, mod, final_g, wout_f):
    B, L, _ = x.shape

    def body(yn_ref, yr_ref, x_ref, t_ref, mod_ref, gf_ref, w_ref, dy_ref, dx2_ref, dw_ref, sm_ref):
        b, i = pl.program_id(0), pl.program_id(1)

        @pl.when((b == 0) & (i == 0))
        def _():
            dw_ref[...] = jnp.zeros_like(dw_ref)
            sm_ref[...] = jnp.zeros_like(sm_ref)

        gate = mod_ref[pl.ds(b, 1), 2 * D:3 * D]
        gf = gf_ref[...]
        yn, yr = yn_ref[...], yr_ref[...]
        ylat = _dot(yn, w_ref[0:512, :]) + _dot(yr, w_ref[512:1024, :])
        x2 = x_ref[...] + gate * ylat
        r = lax.rsqrt(jnp.mean(x2 * x2, axis=-1, keepdims=True) + EPS)
        xr = x2 * r
        err = xr * gf - t_ref[...]
        sm_ref[1:2, :] += jnp.sum(err * err, axis=0, keepdims=True)
        dout = err * (1.0 / D)
        sm_ref[0:1, :] += jnp.sum(dout * xr, axis=0, keepdims=True)
        gd = dout * gf
        dx2 = r * (gd - xr * jnp.mean(gd * xr, axis=-1, keepdims=True))
        dx2_ref[...] = dx2
        sm_ref[pl.ds(2 + b, 1), :] += jnp.sum(dx2 * ylat, axis=0, keepdims=True)
        dyl = (gate * dx2).astype(BF16)
        dy_ref[:, 0:512] = _dot_nt(dyl, w_ref[0:512, :]).astype(BF16)
        dy_ref[:, 512:1024] = _dot_nt(dyl, w_ref[512:1024, :]).astype(BF16)
        dw_ref[0:512, :] += _dot_tn(yn, dyl)
        dw_ref[512:1024, :] += _dot_tn(yr, dyl)

    half = pl.BlockSpec((None, TQ, 512), lambda b, i: (b, i, 0))
    full = pl.BlockSpec((None, TQ, D), lambda b, i: (b, i, 0))
    return pl.pallas_call(
        body, name="out_proj_loss", grid=(B, L // TQ),
        in_specs=[half, half, full, full,
                  pl.BlockSpec((8, 3 * D), lambda b, i: (0, 0)),
                  pl.BlockSpec((1, D), lambda b, i: (0, 0)),
                  pl.BlockSpec((D, D), lambda b, i: (0, 0))],
        out_specs=(full, full, pl.BlockSpec((D, D), lambda b, i: (0, 0)),
                   pl.BlockSpec((8, D), lambda b, i: (0, 0))),
        out_shape=(jax.ShapeDtypeStruct((B, L, D), BF16), jax.ShapeDtypeStruct((B, L, D), F32),
                   jax.ShapeDtypeStruct((D, D), F32), jax.ShapeDtypeStruct((8, D), F32)),
        compiler_params=_params(("arbitrary",) * 2))(y_na, y_ret, x, target, mod, final_g, wout_f)


def _dh_call(dsec, win_f, x, ctx, dx2, mod, norm_g):
    B, L, _ = x.shape
    LC = ctx.shape[1]
    nl = L // TQ

    def body(d0, d1, d2, d3, d4, d5, d6, d7, w_ref, x_ref, ctx_ref, dx2_ref, mod_ref, g_ref,
             gx_ref, sm_ref):
        drefs = (d0, d1, d2, d3, d4, d5, d6, d7)
        b, t = pl.program_id(0), pl.program_id(1)
        is_lat = t < nl

        @pl.when((b == 0) & (t == 0))
        def _():
            sm_ref[...] = jnp.zeros_like(sm_ref)

        def dh_of(secs):
            acc = jnp.zeros((TQ, D), F32)
            for sec in secs:
                s, half = divmod(sec, 2)
                acc = acc + _dot_nt(drefs[sec][...].astype(BF16), w_ref[s, :, half * 512:(half + 1) * 512])
            return acc

        def norm_bwd(dh, xt, mrow):
            scale = mrow[:, D:2 * D]
            g = g_ref[...]
            rstd = lax.rsqrt(jnp.mean(xt * xt, axis=-1, keepdims=True) + EPS)
            xn = xt * rstd
            dshift = jnp.sum(dh, axis=0, keepdims=True)
            dscale = jnp.sum(dh * (xn * g), axis=0, keepdims=True)
            dhn = dh * (1.0 + scale)
            sm_ref[0:1, :] += jnp.sum(dhn * xn, axis=0, keepdims=True)
            dxn = dhn * g
            dx = rstd * (dxn - xn * jnp.mean(dxn * xn, axis=-1, keepdims=True))
            return dshift, dscale, dx

        @pl.when(is_lat)
        def _():
            dshift, dscale, dx = norm_bwd(dh_of(range(8)), x_ref[...], mod_ref[pl.ds(b, 1), :])
            sm_ref[pl.ds(3 + b, 1), :] += dshift
            sm_ref[pl.ds(3 + B + b, 1), :] += dscale
            gx_ref[...] = dx2_ref[...] + dx

        @pl.when(jnp.logical_not(is_lat))
        def _():
            dshift, dscale, _ = norm_bwd(dh_of((1, 2, 5, 6)), ctx_ref[...], mod_ref[B:B + 1, :])
            sm_ref[1:2, :] += dshift
            sm_ref[2:3, :] += dscale

    lat = lambda b, t: (b, jnp.minimum(t, nl - 1), 0)
    tok = lambda b, t: (b, t, 0)
    sec_specs = [pl.BlockSpec((None, TQ, 512), lat if sec in (0, 3, 4, 7) else tok) for sec in range(8)]
    return pl.pallas_call(
        body, name="dh_norm_bwd", grid=(B, nl + 1),
        in_specs=sec_specs + [
            pl.BlockSpec((N_SHARD, D, D), lambda b, t: (0, 0, 0)),
            pl.BlockSpec((None, TQ, D), lat),
            pl.BlockSpec((None, LC, D), lambda b, t: (b, 0, 0)),
            pl.BlockSpec((None, TQ, D), lat),
            pl.BlockSpec((8, 3 * D), lambda b, t: (0, 0)),
            pl.BlockSpec((1, D), lambda b, t: (0, 0))],
        out_specs=(pl.BlockSpec((None, TQ, D), lat), pl.BlockSpec((8, D), lambda b, t: (0, 0))),
        out_shape=(jax.ShapeDtypeStruct((B, L, D), F32), jax.ShapeDtypeStruct((8, D), F32)),
        compiler_params=_params(("arbitrary",) * 2))(*dsec, win_f, x, ctx, dx2, mod, norm_g)


def _dw_call(dsec, h, L):
    B, T, _ = h.shape
    nl = L // TQ

    def body(d0, d1, d2, d3, d4, d5, d6, d7, h_ref, dw_ref, acc_ref):
        drefs = (d0, d1, d2, d3, d4, d5, d6, d7)
        b, t = pl.program_id(0), pl.program_id(1)

        @pl.when((b == 0) & (t == 0))
        def _():
            acc_ref[...] = jnp.zeros_like(acc_ref)

        hb = h_ref[...]

        def add(secs):
            for sec in secs:
                s, half = divmod(sec, 2)
                acc_ref[s, :, half * 512:(half + 1) * 512] += _dot_tn(hb, drefs[sec][...].astype(BF16))

        @pl.when(t < nl)
        def _():
            add(range(8))

        @pl.when(t >= nl)
        def _():
            add((1, 2, 5, 6))

        @pl.when((b == B - 1) & (t == nl))
        def _():
            dw_ref[...] = acc_ref[...].astype(BF16)

    lat = lambda b, t: (b, jnp.minimum(t, nl - 1), 0)
    tok = lambda b, t: (b, t, 0)
    sec_specs = [pl.BlockSpec((None, TQ, 512), lat if sec in (0, 3, 4, 7) else tok) for sec in range(8)]
    return pl.pallas_call(
        body, name="dw_in", grid=(B, nl + 1),
        in_specs=sec_specs + [pl.BlockSpec((None, TQ, D), tok)],
        out_specs=pl.BlockSpec((N_SHARD, D, D), lambda b, t: (0, 0, 0)),
        out_shape=jax.ShapeDtypeStruct((N_SHARD, D, D), BF16),
        scratch_shapes=[pltpu.VMEM((N_SHARD, D, D), F32)],
        compiler_params=_params(("arbitrary",) * 2, vmem_mb=56))(*dsec, h)


def _mesh_pos():
    return lax.axis_index("x"), lax.axis_index("y"), lax.axis_index("c")


def _flip(v, f):
    return 1 - v if f else v


def _remote(src, dst, ssem, rsem, k, peer):
    return pltpu.make_async_remote_copy(src_ref=src, dst_ref=dst, send_sem=ssem.at[k], recv_sem=rsem.at[k],
                                        device_id=peer, device_id_type=MESH)


def _other_chips(x, y):
    return [(_flip(x, fx), _flip(y, fy)) for fx, fy in ((1, 0), (0, 1), (1, 1))]


D2D_STREAMS = 8


def _row_chunks(src, dst, ssem, rsem, k, peer, rows, lead=None):
    step = rows // D2D_STREAMS
    out = []
    for r in range(D2D_STREAMS):
        idx = (pl.ds(r * step, step),) if lead is None else (lead, pl.ds(r * step, step))
        out.append(_remote(src.at[idx], dst.at[idx], ssem, rsem, k, peer))
    return out


def _all_to_all_small(src, dst_all, ssem, rsem, k0, x, y, cc):
    me = 4 * x + 2 * y + cc
    sends, recvs = [], []
    for f in range(1, N_DEV):
        px, py, pc = _flip(x, f & 4), _flip(y, f & 2), _flip(cc, f & 1)
        sends.append(_remote(src, dst_all.at[me], ssem, rsem, k0 + f - 1, (px, py, pc)))
        recvs.append(_remote(src, dst_all.at[4 * px + 2 * py + pc], ssem, rsem, k0 + f - 1, (px, py, pc)))
    return sends, recvs


def _finish(local, sends, recvs):
    for cp in recvs:
        cp.wait_recv()
    for cp in sends:
        cp.wait_send()
    for cp in local:
        cp.wait()


def _gather_call(win_b, wout_b, wada_b, c):
    arrs = (win_b, wout_b, wada_b)
    hrs = [a.shape[0] // 2 for a in arrs]

    def body(win, wout, wada, c_ref, win_f, wout_f, wada_f, c_all, ssem, rsem, lsem):
        x, y, cc = _mesh_pos()
        s, me = 2 * x + y, 4 * x + 2 * y + cc
        sib = (x, y, 1 - cc)
        srcs, dsts = (win, wout, wada), (win_f, wout_f, wada_f)

        def half(a, shard, hc):
            return dsts[a].at[shard, pl.ds(hc * hrs[a], hrs[a])]

        local = [pltpu.make_async_copy(srcs[a], dsts[a].at[s], lsem.at[a]) for a in range(3)]
        local.append(pltpu.make_async_copy(c_ref, c_all.at[me], lsem.at[3]))
        ici_send, ici_recv, fwd_send, fwd_recv, fwd_chunks, k = [], [], [], [], [], 0
        for px, py in _other_chips(x, y):
            ps = 2 * px + py
            for a in range(3):
                mine = srcs[a].at[pl.ds(cc * hrs[a], hrs[a])]
                ici_send.append(_remote(mine, half(a, s, cc), ssem, rsem, k, (px, py, cc)))
                ici_recv.append(_remote(mine, half(a, ps, cc), ssem, rsem, k, (px, py, cc)))
                fwd_send.append(_remote(half(a, ps, cc), half(a, ps, cc), ssem, rsem, 9 + k, sib))
                fwd_recv.append(_remote(half(a, ps, 1 - cc), half(a, ps, 1 - cc), ssem, rsem, 9 + k, sib))
                fwd_chunks.append(_row_chunks(half(a, ps, cc), half(a, ps, cc), ssem, rsem, 9 + k, sib, hrs[a]))
                k += 1
        c_send, c_recv = _all_to_all_small(c_ref, c_all, ssem, rsem, 18, x, y, cc)
        for cp in local + ici_send + c_send:
            cp.start()
        for got, chunks in zip(ici_recv, fwd_chunks):
            got.wait_recv()
            for cp in chunks:
                cp.start()
        _finish(local, ici_send + fwd_send + c_send, fwd_recv + c_recv)

    return pl.pallas_call(
        body, name="weight_gather",
        in_specs=[pl.BlockSpec(memory_space=pltpu.VMEM)] * 4,
        out_specs=(pl.BlockSpec(memory_space=pltpu.VMEM),) * 4,
        out_shape=tuple(jax.ShapeDtypeStruct((N_SHARD,) + a.shape, a.dtype) for a in arrs)
        + (jax.ShapeDtypeStruct((N_DEV,) + c.shape, c.dtype),),
        scratch_shapes=[pltpu.SemaphoreType.DMA((25,)), pltpu.SemaphoreType.DMA((25,)),
                        pltpu.SemaphoreType.DMA((4,))],
        compiler_params=pltpu.CompilerParams(vmem_limit_bytes=48 << 20))(win_b, wout_b, wada_b, c)


def _grad_reduce_call(dwin_b, dwout_b, small):
    arrs = (dwin_b, dwout_b)
    hrs = [a.shape[1] // 2 for a in arrs]
    RC = 32

    def body(din, dout, sm, gin, gout, sm_all, got_in, got_out, cp_in, cp_out, sl_in, sl_out, h_in, h_out,
             ssem, rsem, lsem):
        x, y, cc = _mesh_pos()
        s, me = 2 * x + y, 4 * x + 2 * y + cc
        sib = (x, y, 1 - cc)
        srcs, gots, cps = (din, dout), (got_in, got_out), (cp_in, cp_out)
        sls, hs, gs = (sl_in, sl_out), (h_in, h_out), (gin, gout)
        halves = [_remote(srcs[a].at[:, pl.ds((1 - cc) * hrs[a], hrs[a])], gots[a], ssem, rsem, a, sib)
                  for a in range(2)]
        sm_send, sm_recv = _all_to_all_small(sm, sm_all, ssem, rsem, 10, x, y, cc)
        sm_own = pltpu.make_async_copy(sm, sm_all.at[me], lsem.at[0])
        for cp in halves + sm_send + [sm_own]:
            cp.start()
        for cp in halves:
            cp.wait_recv()
        for a in range(2):
            for j in range(N_SHARD):
                def add(i, carry, a=a, j=j):
                    r = pl.multiple_of(i * RC, RC)
                    mine = srcs[a][j, pl.ds(pl.multiple_of(cc * hrs[a] + r, RC), RC), :].astype(F32)
                    cps[a][j, pl.ds(r, RC), :] = (mine + gots[a][j, pl.ds(r, RC), :].astype(F32)).astype(BF16)
                    return carry
                lax.fori_loop(0, hrs[a] // RC, add, 0)
        own = [pltpu.make_async_copy(cps[a].at[s], sls[a].at[s], lsem.at[1 + a]) for a in range(2)]
        sends, recvs, k = [], [], 2
        for px, py in _other_chips(x, y):
            ps = 2 * px + py
            for a in range(2):
                sends.append(_remote(cps[a].at[ps], sls[a].at[s], ssem, rsem, k, (px, py, cc)))
                recvs.append(_remote(cps[a].at[s], sls[a].at[ps], ssem, rsem, k, (px, py, cc)))
                k += 1
        for cp in own + sends:
            cp.start()
        for cp in own:
            cp.wait()
        for cp in recvs:
            cp.wait_recv()
        for a in range(2):
            def total(i, carry, a=a):
                rows = pl.ds(pl.multiple_of(i * RC, RC), RC)
                sl = sls[a]
                hs[a][rows, :] = ((sl[0, rows, :].astype(F32) + sl[1, rows, :].astype(F32))
                                  + sl[2, rows, :].astype(F32)) + sl[3, rows, :].astype(F32)
                return carry
            lax.fori_loop(0, hrs[a] // RC, total, 0)
        mine = [pltpu.make_async_copy(hs[a], gs[a].at[cc], lsem.at[3 + a]) for a in range(2)]
        back = [_remote(hs[a], gs[a].at[cc], ssem, rsem, 8 + a, sib) for a in range(2)]
        back_recv = [_remote(hs[a], gs[a].at[1 - cc], ssem, rsem, 8 + a, sib) for a in range(2)]
        for cp in mine + back:
            cp.start()
        _finish(mine + [sm_own], halves + sends + back + sm_send, back_recv + sm_recv)

    vmem = pl.BlockSpec(memory_space=pltpu.VMEM)
    half_shapes = [(N_SHARD, hrs[a], arrs[a].shape[2]) for a in range(2)]
    scratch = []
    for dt in (BF16, BF16, BF16):
        scratch += [pltpu.VMEM(half_shapes[0], dt), pltpu.VMEM(half_shapes[1], dt)]
    scratch += [pltpu.VMEM(half_shapes[0][1:], F32), pltpu.VMEM(half_shapes[1][1:], F32)]
    return pl.pallas_call(
        body, name="grad_reduce",
        in_specs=[vmem] * 3, out_specs=(vmem,) * 3,
        out_shape=(jax.ShapeDtypeStruct((2,) + half_shapes[0][1:], F32),
                   jax.ShapeDtypeStruct((2,) + half_shapes[1][1:], F32),
                   jax.ShapeDtypeStruct((N_DEV,) + small.shape, F32)),
        scratch_shapes=scratch + [pltpu.SemaphoreType.DMA((17,)), pltpu.SemaphoreType.DMA((17,)),
                                  pltpu.SemaphoreType.DMA((5,))],
        compiler_params=pltpu.CompilerParams(vmem_limit_bytes=56 << 20))(dwin_b, dwout_b, small)


def _adamw(w, g, m, v):
    m = ADAM_B1 * m + (1.0 - ADAM_B1) * g
    v = ADAM_B2 * v + (1.0 - ADAM_B2) * (g * g)
    m_hat = m / (1.0 - ADAM_B1 ** ADAM_STEP)
    v_hat = v / (1.0 - ADAM_B2 ** ADAM_STEP)
    return -ADAM_LR * (m_hat / (jnp.sqrt(v_hat) + ADAM_EPS) + ADAM_WD * w), m, v


def _adam_call(w, m, v, g, name):
    R, C = w.shape
    tr = 256

    def body(w_ref, m_ref, v_ref, g_ref, d_ref, mo_ref, vo_ref):
        d_ref[...], mo_ref[...], vo_ref[...] = _adamw(w_ref[...], g_ref[...], m_ref[...], v_ref[...])

    spec = pl.BlockSpec((tr, C), lambda i: (i, 0))
    return pl.pallas_call(
        body, name=name, grid=(R // tr,), in_specs=[spec] * 4,
        out_specs=(spec,) * 3, out_shape=(jax.ShapeDtypeStruct((R, C), F32),) * 3,
        compiler_params=_params(("arbitrary",)))(w, m, v, g)


R_GF, R_NG, R_LOSS, R_RNG, R_LGF, R_LGB, R_SHIFT, R_SCALE, R_GATE, R_SHIFT_C, R_SCALE_C, R_RNG2, R_RPB = (
    0, 1, 2, 3, 4, 5, 6, 8, 10, 12, 13, 14, 16)
W_GF, W_NG, W_CCTX, W_RNG, W_DF, W_DB, W_BADA, W_RPB = 0, 1, 2, 3, 4, 5, 6, 9


def _small_final_call(sm_all, c_t, c_ctx, wada_f, wada, m_ada, v_ada, wsm, msm, vsm, B):
    ws = wada.shape[1]
    NB = N_DEV * B

    def body(sm_ref, ct_ref, cctx_ref, wf_ref, wa_ref, ma_ref, va_ref, w_ref, m_ref, v_ref,
             g_ref, d_ref, mo_ref, vo_ref, ga_ref, da_ref, mao_ref, vao_ref, loss_ref, dmod_ref):
        x, y, _ = _mesh_pos()
        s = 2 * x + y
        tot = sm_ref[0]
        for dv in range(1, N_DEV):
            tot = tot + sm_ref[dv]
        w = w_ref[...]
        for dv in range(N_DEV):
            for b in range(B):
                r = dv * B + b
                for part, row in enumerate((R_SHIFT, R_SCALE, R_GATE)):
                    dmod_ref[r:r + 1, part * D:(part + 1) * D] = sm_ref[dv, row + b:row + b + 1, :]
        dmod_ref[NB:NB + 1, 0:D] = tot[R_SHIFT_C:R_SHIFT_C + 1, :]
        dmod_ref[NB:NB + 1, D:2 * D] = tot[R_SCALE_C:R_SCALE_C + 1, :]
        dmod_ref[NB:NB + 1, 2 * D:3 * D] = jnp.zeros((1, D), F32)
        dmod_ref[NB + 1:, :] = jnp.zeros((dmod_ref.shape[0] - NB - 1, 3 * D), F32)
        dmod = dmod_ref[...]
        cc = cctx_ref[...]
        scc = _sigmoid(cc)
        ct = ct_ref[...]
        act_t = ct * _sigmoid(ct)
        dmc = dmod[NB:NB + 1, :].astype(BF16)
        dact = jnp.zeros((1, D), F32)
        for sh in range(N_SHARD):
            dact = dact + _dot_nt(dmc[:, sh * ws:(sh + 1) * ws], wf_ref[sh])
        g = jnp.zeros((16, D), F32)
        rows = lax.broadcasted_iota(jnp.int32, (16, D), 0)

        def put(g, row, val):
            return jnp.where(rows == row, val, g)

        g = put(g, W_GF, tot[R_GF:R_GF + 1, :])
        g = put(g, W_NG, tot[R_NG:R_NG + 1, :])
        g = put(g, W_CCTX, dact * (scc * (1.0 + cc * (1.0 - scc))))
        g = put(g, W_RNG, tot[R_RNG:R_RNG + 1, :] + tot[R_RNG2:R_RNG2 + 1, :])
        g = put(g, W_DF, tot[R_LGF:R_LGF + 1, :] * (-jnp.exp(w[W_DF:W_DF + 1, :])))
        g = put(g, W_DB, tot[R_LGB:R_LGB + 1, :] * (-jnp.exp(w[W_DB:W_DB + 1, :])))
        db = jnp.sum(dmod, axis=0, keepdims=True)
        for part in range(3):
            g = put(g, W_BADA + part, db[:, part * D:(part + 1) * D])
        for part in range(4):
            g = put(g, W_RPB + part, tot[R_RPB + part:R_RPB + part + 1, :])
        g_ref[...] = g
        d_ref[...], mo_ref[...], vo_ref[...] = _adamw(w, g, m_ref[...], v_ref[...])
        loss_ref[...] = jnp.broadcast_to(
            (0.5 / D) * jnp.sum(tot[R_LOSS:R_LOSS + 1, :], axis=1, keepdims=True), (8, 128))
        for sh in range(N_SHARD):
            @pl.when(s == sh)
            def _():
                ga = jnp.dot(act_t, dmod[:, sh * ws:(sh + 1) * ws], precision=HIGHEST,
                             preferred_element_type=F32)
                ga_ref[...] = ga
                da_ref[...], mao_ref[...], vao_ref[...] = _adamw(wa_ref[...], ga, ma_ref[...], va_ref[...])

    sh_small = jax.ShapeDtypeStruct((16, D), F32)
    sh_ada = jax.ShapeDtypeStruct(wada.shape, F32)
    return pl.pallas_call(
        body, name="small_final",
        out_shape=(sh_small,) * 4 + (sh_ada,) * 4 + (jax.ShapeDtypeStruct((8, 128), F32),),
        scratch_shapes=[pltpu.VMEM((NB + 8, 3 * D), F32)],
        compiler_params=_params(vmem_mb=56))(
            sm_all, c_t, c_ctx, wada_f, wada, m_ada, v_ada, wsm, msm, vsm)


def _local_step(x, c, ctx, c_ctx, norm_g, wada_f, b_ada, win_f, na_rpb, dec_f, dec_b, ret_norm_g,
                wout_f, final_g, target):
    B, L, _ = x.shape
    LC = ctx.shape[1]
    assert B == 2
    cos2, sin2 = _rope_tables(L, LC)
    c8 = jnp.concatenate([c, c_ctx[None, :], jnp.zeros((8 - B - 1, D), F32)], axis=0)
    mod = _mod_call(c8, wada_f, b_ada)
    bias = _bias_call(na_rpb.reshape(na_rpb.shape[0], -1))
    P, h = _inproj_call(x, ctx, mod, norm_g, win_f, cos2, sin2)
    y_na = _na_fwd_call(P, bias, L, LC)
    sf, sb = _ret_states_call(P, dec_f, dec_b, L, LC)
    y_ret, o_ret = _retc_fwd_call(P, sf, sb, dec_f, dec_b, ret_norm_g, L)
    dY, dx2, dwout_p, sm_out = _out_call(y_na, y_ret, x, target, mod, final_g, wout_f.reshape(D, D))
    dnq, dng, dnk, dnv, dbias = _na_bwd_call(P, bias, dY, L, LC)
    drq, drg, drk, drv, dgn, dlg = _retc_bwd_call(P, sf, sb, dec_f, dec_b, ret_norm_g, o_ret, dY, cos2, sin2, L, LC)
    dsec = (dnq, dnk, dnv, dng, drq, drk, drv, drg)
    grad_x, sm_dh = _dh_call(dsec, win_f, x, ctx, dx2, mod, norm_g)
    dwin_p = _dw_call(dsec, h, L)
    drpb, dlg_sum = _small_reduce_call(dbias, dlg, B)
    z = jnp.zeros((1, D), F32)
    pad = lambda v: jnp.pad(v.reshape(1, -1), ((0, 0), (0, D - v.size)))
    dlg_sum = dlg_sum.reshape(4, 8, 128)
    rpb_rows = jnp.pad(drpb[:, :15, :31].reshape(-1), (0, 4 * D - drpb.shape[0] * 465)).reshape(4, D)
    small = jnp.concatenate([
        sm_out[0:1], sm_dh[0:1], sm_out[1:2], pad(dgn[0]), pad(dlg_sum[:, 0, 0]), pad(dlg_sum[:, 1, 0]),
        sm_dh[3:5], sm_dh[5:7], sm_out[2:4], sm_dh[1:2], sm_dh[2:3], pad(dgn[1]), z, rpb_rows,
        jnp.zeros((SM_ROWS - 20, D), F32)], axis=0)
    return grad_x, dwin_p, dwout_p, small


def kernel(x, c, ctx, c_ctx, norm_g, w_ada, b_ada, w_in, na_rpb, ret_decay_fwd, ret_decay_bwd, ret_norm_g, w_out, final_norm_g, loss_target, m_c_ctx, m_norm_g, m_w_ada, m_b_ada, m_w_in, m_na_rpb, m_ret_decay_fwd, m_ret_decay_bwd, m_ret_norm_g, m_w_out, m_final_norm_g, v_c_ctx, v_norm_g, v_w_ada, v_b_ada, v_w_in, v_na_rpb, v_ret_decay_fwd, v_ret_decay_bwd, v_ret_norm_g, v_w_out, v_final_norm_g):
    B = x.shape[0]
    win_f, wout_f, wada_f, c_all = _gather_call(w_in[0].astype(BF16), w_out[0].astype(BF16),
                                                w_ada[0].astype(BF16), c)
    grad_x, dwin_p, dwout_p, small = _local_step(
        x, c, ctx, c_ctx, norm_g, wada_f, b_ada, win_f, na_rpb[0], ret_decay_fwd, ret_decay_bwd,
        ret_norm_g, wout_f, final_norm_g.reshape(1, D), loss_target)
    gin, gout, sm_all = _grad_reduce_call(
        dwin_p, dwout_p.astype(BF16).reshape(N_SHARD, D // N_SHARD, D), small)
    g_win, g_wout = gin.reshape(w_in.shape[1:]), gout.reshape(w_out.shape[1:])
    d_win, nm_win, nv_win = _adam_call(w_in[0], m_w_in[0], v_w_in[0], g_win, "adam_w_in")
    d_wout, nm_wout, nv_wout = _adam_call(w_out[0], m_w_out[0], v_w_out[0], g_wout, "adam_w_out")

    def pack(gf, ng, cc, rng, df, db, bada, rpb):
        pad = lambda v: jnp.pad(v.reshape(1, -1), ((0, 0), (0, D - v.size)))
        return jnp.concatenate([
            gf.reshape(1, D), ng.reshape(1, D), cc.reshape(1, D), pad(rng), pad(df), pad(db),
            bada.reshape(3, D), jnp.pad(rpb.reshape(-1), (0, 4 * D - rpb.size)).reshape(4, D),
            jnp.zeros((3, D), F32)], axis=0)

    wsm = pack(final_norm_g, norm_g, c_ctx, ret_norm_g, ret_decay_fwd, ret_decay_bwd, b_ada, na_rpb)
    msm = pack(m_final_norm_g, m_norm_g, m_c_ctx, m_ret_norm_g, m_ret_decay_fwd, m_ret_decay_bwd, m_b_ada, m_na_rpb)
    vsm = pack(v_final_norm_g, v_norm_g, v_c_ctx, v_ret_norm_g, v_ret_decay_fwd, v_ret_decay_bwd, v_b_ada, v_na_rpb)
    c_t = jnp.concatenate([c_all.reshape(N_DEV * B, D), c_ctx.reshape(1, D), jnp.zeros((7, D), F32)], axis=0).T
    outs = _small_final_call(sm_all, c_t, c_ctx.reshape(1, D), wada_f,
                             w_ada[0], m_w_ada[0], v_w_ada[0], wsm, msm, vsm, B)
    smalls, adas, loss = outs[0:4], outs[4:8], outs[8][0, 0]

    def unpack(p):
        rw = ret_norm_g.shape[1]
        return dict(
            final_norm_g=p[W_GF], norm_g=p[W_NG:W_NG + 1], c_ctx=p[W_CCTX], ret_norm_g=p[W_RNG:W_RNG + 1, :rw],
            ret_decay_fwd=p[W_DF:W_DF + 1, :4], ret_decay_bwd=p[W_DB:W_DB + 1, :4],
            b_ada=p[W_BADA:W_BADA + 3].reshape(1, 3 * D),
            na_rpb=p[W_RPB:W_RPB + 4].reshape(-1)[:na_rpb.size].reshape(na_rpb.shape))

    res = []
    for p, ada, win_o, wout_o in zip(smalls, adas, (g_win, d_win, nm_win, nv_win),
                                     (g_wout, d_wout, nm_wout, nv_wout)):
        u = unpack(p)
        res.append([u["c_ctx"], u["norm_g"], ada[None], u["b_ada"], win_o[None], u["na_rpb"],
                    u["ret_decay_fwd"], u["ret_decay_bwd"], u["ret_norm_g"], wout_o[None], u["final_norm_g"]])
    return (loss, grad_x, *res[0], *res[1], *res[2], *res[3])
```

```python
import functools

import numpy as np
import jax
import jax.numpy as jnp
from jax import lax
from jax.experimental import pallas as pl
from jax.experimental.pallas import tpu as pltpu

F32 = jnp.float32
BF16 = jnp.bfloat16
HIGHEST = lax.Precision.HIGHEST

D = 1024
GRID_W = 64
NA_DH = 64
RET_DK = 128
ROPE_BASE = 10000.0
EPS = 1e-6
NEG = -1e30
TQ = 256
TK = 512
KW = 12 * GRID_W
N_SHARD = 4
N_DEV = 8
SM_ROWS = 24

ADAM_LR = 0.001
ADAM_B1 = 0.9
ADAM_B2 = 0.999
ADAM_EPS = 1e-08
ADAM_WD = 0.01
ADAM_STEP = 10

MESH = pl.DeviceIdType.MESH
ANY = pl.BlockSpec(memory_space=pl.ANY)


def _params(sem=None, vmem_mb=48):
    return pltpu.CompilerParams(dimension_semantics=sem, vmem_limit_bytes=vmem_mb << 20)


def _dot(a, b):
    return jnp.dot(a, b, preferred_element_type=F32)


def _dot_nt(a, b):
    return lax.dot_general(a, b, (((1,), (1,)), ((), ())), preferred_element_type=F32)


def _dot_tn(a, b):
    return lax.dot_general(a, b, (((0,), (0,)), ((), ())), preferred_element_type=F32)


def _sigmoid(x):
    return 1.0 / (1.0 + jnp.exp(-x))


def _rope_tables(L, LC):
    half = RET_DK // 2
    nf = half // 2
    t = np.arange(L)
    row = (t // GRID_W).astype(np.float32)
    col = (t % GRID_W).astype(np.float32)
    inv = (np.float32(ROPE_BASE) ** (-np.arange(nf, dtype=np.float32) / np.float32(nf))).astype(np.float32)
    ang = np.concatenate([row[:, None] * inv, col[:, None] * inv], axis=-1).astype(np.float32)
    cos, sin = np.cos(ang).astype(np.float32), np.sin(ang).astype(np.float32)
    cos2 = np.concatenate([cos, cos], axis=-1)
    sin2 = np.concatenate([-sin, sin], axis=-1)
    cos2 = np.concatenate([cos2, np.ones((LC, RET_DK), np.float32)], axis=0)
    sin2 = np.concatenate([sin2, np.zeros((LC, RET_DK), np.float32)], axis=0)
    return jnp.asarray(cos2), jnp.asarray(sin2)


def _mod_call(c8, wada_f, b_ada):
    ws = wada_f.shape[2]

    def body(c_ref, w_ref, b_ref, o_ref):
        a = c_ref[...]
        a = (a * _sigmoid(a)).astype(BF16)
        for s in range(N_SHARD):
            o_ref[:, s * ws:(s + 1) * ws] = _dot(a, w_ref[s]) + b_ref[:, s * ws:(s + 1) * ws]

    return pl.pallas_call(
        body, name="ada_mod", out_shape=jax.ShapeDtypeStruct((8, 3 * D), F32),
        compiler_params=_params())(c8, wada_f, b_ada)


def _dc_masks():
    cq = lax.broadcasted_iota(jnp.int32, (GRID_W, GRID_W), 0)
    ck = lax.broadcasted_iota(jnp.int32, (GRID_W, GRID_W), 1)
    dc = jnp.clip(ck - cq + 15, 0, 30)
    c0 = jnp.clip(cq - 8, 0, GRID_W - 16)
    col_ok = (ck >= c0) & (ck < c0 + 16)
    return dc, col_ok


def _bias_blocks():
    out = []
    for typ, delta in enumerate((4, 0, -4)):
        for rq in range(4):
            for rkk in range(12):
                dr = rkk + delta - rq - 4
                if typ == 0:
                    ok = -rq <= dr <= 7 - rq
                elif typ == 1:
                    ok = -4 <= dr <= 3
                else:
                    ok = -4 - rq <= dr <= 3 - rq
                out.append((typ, rq, rkk, dr if ok else None))
    return out


def _bias_body(r_ref, bias_ref, et_ref):
    dc, col_ok = _dc_masks()
    masks = [(dc == j).astype(F32) for j in range(31)]

    def per_h(h, carry):
        for dr in range(15):
            t = jnp.zeros((GRID_W, GRID_W), F32)
            for j in range(31):
                t = t + masks[j] * r_ref[h, dr * 31 + j]
            et_ref[dr] = jnp.where(col_ok, t, NEG)
        neg = jnp.full((GRID_W, GRID_W), NEG, F32)
        for typ, rq, rkk, dr in _bias_blocks():
            blk = neg if dr is None else et_ref[dr + 7]
            bias_ref[h, typ, rq * 64:(rq + 1) * 64, rkk * 64:(rkk + 1) * 64] = blk
        return carry

    lax.fori_loop(0, bias_ref.shape[0], per_h, 0)


def _small_reduce_body(db_ref, dlg_ref, drpb_ref, dlgo_ref, p_ref):
    dc, _ = _dc_masks()
    masks = [(dc == j).astype(F32) for j in range(31)]
    ones = jnp.ones((8, GRID_W), F32)
    p_ref[...] = jnp.zeros_like(p_ref)
    drpb_ref[...] = jnp.zeros_like(drpb_ref)

    def per_h(h, carry):
        acc = {}
        for typ, rq, rkk, dr in _bias_blocks():
            if dr is None:
                continue
            blk = db_ref[h, typ, rq * 64:(rq + 1) * 64, rkk * 64:(rkk + 1) * 64]
            acc[dr] = blk if dr not in acc else acc[dr] + blk
        for dr in range(-7, 8):
            t = acc[dr]
            for j in range(31):
                p_ref[j:j + 1, :] = jnp.sum(t * masks[j], axis=0, keepdims=True)
            red = lax.dot_general(ones, p_ref[...], (((1,), (1,)), ((), ())),
                                  precision=HIGHEST, preferred_element_type=F32)
            drpb_ref[h, dr + 7:dr + 8, :] = red[0:1, :]
        return carry

    lax.fori_loop(0, db_ref.shape[0], per_h, 0)
    x = dlg_ref[0]
    for b in range(1, dlg_ref.shape[0]):
        x = x + dlg_ref[b]
    x = x.reshape(4 * 8, x.shape[-1])
    dlgo_ref[...] = jnp.dot(x, jnp.ones((x.shape[-1], 128), F32), precision=HIGHEST,
                            preferred_element_type=F32)


def _inproj_call(x, ctx, mod, norm_g, win_f, cos2, sin2):
    B, L, _ = x.shape
    LC = ctx.shape[1]
    T = L + LC
    nl = L // TQ
    assert LC == TQ and L % TQ == 0
    kscale = RET_DK ** -0.5

    def body(x_ref, ctx_ref, mod_ref, g_ref, w_ref, cos_ref, sin_ref, p_ref, h_ref):
        b = pl.program_id(0)
        t = pl.program_id(1)
        is_lat = t < nl
        xt = jnp.where(is_lat, x_ref[...], ctx_ref[...])
        mrow = mod_ref[pl.ds(jnp.where(is_lat, b, B), 1), :]
        shift, scale = mrow[:, 0:D], mrow[:, D:2 * D]
        rstd = lax.rsqrt(jnp.mean(xt * xt, axis=-1, keepdims=True) + EPS)
        hb = ((xt * rstd * g_ref[...]) * (1.0 + scale) + shift).astype(BF16)
        h_ref[...] = hb
        cs, sn = cos_ref[...], sin_ref[...]
        for sec in range(8):
            s, half = divmod(sec, 2)
            acc = _dot(hb, w_ref[s, :, half * 512:(half + 1) * 512])
            if sec == 0:
                acc = acc * (NA_DH ** -0.5)
            if sec in (4, 5):
                for j in range(4):
                    a = acc[:, j * 128:(j + 1) * 128]
                    r = a * cs + pltpu.roll(a, 64, 1) * sn
                    if sec == 5:
                        r = r * kscale
                    p_ref[:, sec * 512 + j * 128:sec * 512 + (j + 1) * 128] = r.astype(BF16)
            else:
                p_ref[:, sec * 512:(sec + 1) * 512] = acc.astype(BF16)

    return pl.pallas_call(
        body, name="in_proj", grid=(B, T // TQ),
        in_specs=[
            pl.BlockSpec((None, TQ, D), lambda b, t: (b, jnp.minimum(t, nl - 1), 0)),
            pl.BlockSpec((None, TQ, D), lambda b, t: (b, 0, 0)),
            pl.BlockSpec((8, 3 * D), lambda b, t: (0, 0)),
            pl.BlockSpec((1, D), lambda b, t: (0, 0)),
            pl.BlockSpec((N_SHARD, D, D), lambda b, t: (0, 0, 0)),
            pl.BlockSpec((TQ, RET_DK), lambda b, t: (t, 0)),
            pl.BlockSpec((TQ, RET_DK), lambda b, t: (t, 0)),
        ],
        out_specs=(pl.BlockSpec((None, TQ, 4 * D), lambda b, t: (b, t, 0)),
                   pl.BlockSpec((None, TQ, D), lambda b, t: (b, t, 0))),
        out_shape=(jax.ShapeDtypeStruct((B, T, 4 * D), BF16), jax.ShapeDtypeStruct((B, T, D), BF16)),
        compiler_params=_params(("arbitrary", "arbitrary")))(x, ctx, mod, norm_g, win_f, cos2, sin2)


def _na_specs(L, T, rows):
    nm = rows // 4
    q_spec = pl.BlockSpec((None, TQ, 128), lambda hp, b, m: (b, m, hp))
    k_spec = pl.BlockSpec((None, T, 128), lambda hp, b, m: (b, 0, 4 + hp))
    v_spec = pl.BlockSpec((None, T, 128), lambda hp, b, m: (b, 0, 8 + hp))
    g_spec = pl.BlockSpec((None, TQ, 128), lambda hp, b, m: (b, m, 12 + hp))
    bias_spec = pl.BlockSpec((2, 3, TQ, KW), lambda hp, b, m: (hp, 0, 0, 0))
    return nm, q_spec, k_spec, v_spec, g_spec, bias_spec


def _na_tile(m, nm, rows):
    typ = jnp.where(m == 0, 0, jnp.where(m == nm - 1, 2, 1))
    start = pl.multiple_of(jnp.clip(4 * m - 4, 0, rows - 12) * GRID_W, TQ)
    return typ, start


def _na_fwd_call(P, bias, L, LC):
    B, T, _ = P.shape
    rows = L // GRID_W
    nm, q_spec, k_spec, v_spec, g_spec, bias_spec = _na_specs(L, T, rows)

    def body(q_ref, k_ref, v_ref, g_ref, bias_ref, y_ref):
        typ, start = _na_tile(pl.program_id(2), nm, rows)
        for hh in range(2):
            ln = slice(hh * NA_DH, (hh + 1) * NA_DH)
            q = q_ref[:, ln]
            kw, vw = k_ref[pl.ds(start, KW), ln], v_ref[pl.ds(start, KW), ln]
            kc, vc = k_ref[L:L + LC, ln], v_ref[L:L + LC, ln]
            s1 = _dot_nt(q, kw) + bias_ref[hh, typ]
            s2 = _dot_nt(q, kc)
            mx = jnp.maximum(jnp.max(s1, axis=-1, keepdims=True), jnp.max(s2, axis=-1, keepdims=True))
            p1, p2 = jnp.exp(s1 - mx), jnp.exp(s2 - mx)
            inv = 1.0 / (jnp.sum(p1, axis=-1, keepdims=True) + jnp.sum(p2, axis=-1, keepdims=True))
            o = (_dot(p1.astype(BF16), vw) + _dot(p2.astype(BF16), vc)) * inv
            g = g_ref[:, ln].astype(F32)
            y_ref[:, ln] = (o * (g * _sigmoid(g))).astype(BF16)

    return pl.pallas_call(
        body, name="na_fwd", grid=(4, B, nm),
        in_specs=[q_spec, k_spec, v_spec, g_spec, bias_spec],
        out_specs=pl.BlockSpec((None, TQ, 128), lambda hp, b, m: (b, m, hp)),
        out_shape=jax.ShapeDtypeStruct((B, L, 512), BF16),
        compiler_params=_params(("arbitrary",) * 3))(P, P, P, P, bias)


def _na_bwd_call(P, bias, dY, L, LC):
    B, T, _ = P.shape
    rows = L // GRID_W
    nm, q_spec, k_spec, v_spec, g_spec, bias_spec = _na_specs(L, T, rows)
    scale = NA_DH ** -0.5

    def body(q_ref, k_ref, v_ref, g_ref, bias_ref, dy_ref, dq_ref, dg_ref, dk_ref, dv_ref, db_ref):
        b, m = pl.program_id(1), pl.program_id(2)
        typ, start = _na_tile(m, nm, rows)

        @pl.when(m == 0)
        def _():
            dk_ref[...] = jnp.zeros_like(dk_ref)
            dv_ref[...] = jnp.zeros_like(dv_ref)

        @pl.when((m == 0) & (b == 0))
        def _():
            db_ref[...] = jnp.zeros_like(db_ref)

        for hh in range(2):
            ln = slice(hh * NA_DH, (hh + 1) * NA_DH)
            q = q_ref[:, ln]
            kw, vw = k_ref[pl.ds(start, KW), ln], v_ref[pl.ds(start, KW), ln]
            kc, vc = k_ref[L:L + LC, ln], v_ref[L:L + LC, ln]
            s1 = _dot_nt(q, kw) + bias_ref[hh, typ]
            s2 = _dot_nt(q, kc)
            mx = jnp.maximum(jnp.max(s1, axis=-1, keepdims=True), jnp.max(s2, axis=-1, keepdims=True))
            p1, p2 = jnp.exp(s1 - mx), jnp.exp(s2 - mx)
            inv = 1.0 / (jnp.sum(p1, axis=-1, keepdims=True) + jnp.sum(p2, axis=-1, keepdims=True))
            p1, p2 = p1 * inv, p2 * inv
            p1b, p2b = p1.astype(BF16), p2.astype(BF16)
            o = _dot(p1b, vw) + _dot(p2b, vc)
            g = g_ref[:, ln].astype(F32)
            sg = _sigmoid(g)
            dy = dy_ref[:, ln].astype(F32)
            dg_ref[:, ln] = (dy * o * (sg * (1.0 + g * (1.0 - sg)))).astype(BF16)
            do = (dy * (g * sg)).astype(BF16)
            dp1, dp2 = _dot_nt(do, vw), _dot_nt(do, vc)
            delta = jnp.sum(p1 * dp1, axis=-1, keepdims=True) + jnp.sum(p2 * dp2, axis=-1, keepdims=True)
            ds1, ds2 = p1 * (dp1 - delta), p2 * (dp2 - delta)
            db_ref[hh, typ] += ds1
            ds1b, ds2b = ds1.astype(BF16), ds2.astype(BF16)
            dq_ref[:, ln] = ((_dot(ds1b, kw) + _dot(ds2b, kc)) * scale).astype(BF16)
            dk_ref[pl.ds(start, KW), ln] += _dot_tn(ds1b, q)
            dv_ref[pl.ds(start, KW), ln] += _dot_tn(p1b, do)
            dk_ref[L:L + LC, ln] += _dot_tn(ds2b, q)
            dv_ref[L:L + LC, ln] += _dot_tn(p2b, do)

    tile = pl.BlockSpec((None, TQ, 128), lambda hp, b, m: (b, m, hp))
    kv_out = pl.BlockSpec((None, T, 128), lambda hp, b, m: (b, 0, hp))
    return pl.pallas_call(
        body, name="na_bwd", grid=(4, B, nm),
        in_specs=[q_spec, k_spec, v_spec, g_spec, bias_spec, tile],
        out_specs=(tile, tile, kv_out, kv_out, bias_spec),
        out_shape=(jax.ShapeDtypeStruct((B, L, 512), BF16), jax.ShapeDtypeStruct((B, L, 512), BF16),
                   jax.ShapeDtypeStruct((B, T, 512), F32), jax.ShapeDtypeStruct((B, T, 512), F32),
                   jax.ShapeDtypeStruct(bias.shape, F32)),
        compiler_params=_params(("arbitrary",) * 3))(P, P, P, P, bias, dY)


def _head_scalar(dec_ref, h):
    lane = lax.broadcasted_iota(jnp.int32, dec_ref.shape, 1)
    return -jnp.sum(jnp.where(lane == h, jnp.exp(dec_ref[...]), 0.0), axis=1, keepdims=True)


def _decay_lat(tpos, ks, lgf, lgb):
    spos = (ks + lax.broadcasted_iota(jnp.int32, (1, TK), 1)).astype(F32)
    dist = tpos - spos
    dm = jnp.exp(dist * jnp.where(dist > 0, lgf, -lgb)) * jnp.where(dist == 0, 2.0, 1.0)
    return dist, dm


def _decay_ctx(tpos, L, LC, lgf, lgb):
    jc = lax.broadcasted_iota(jnp.int32, (1, LC), 1).astype(F32)
    df = tpos + (float(LC) - jc)
    db = (float(L) - tpos) + jc
    return df, db, jnp.exp(lgf * df), jnp.exp(lgb * db)


def _ret_specs(T):
    q_spec = pl.BlockSpec((None, TQ, 128), lambda b, h, i: (b, i, 16 + h))
    k_spec = pl.BlockSpec((None, T, 128), lambda b, h, i: (b, 0, 20 + h))
    v_spec = pl.BlockSpec((None, T, 128), lambda b, h, i: (b, 0, 24 + h))
    g_spec = pl.BlockSpec((None, TQ, 128), lambda b, h, i: (b, i, 28 + h))
    dec_spec = pl.BlockSpec((1, 4), lambda b, h, i: (0, 0))
    gn_spec = pl.BlockSpec((1, 128), lambda b, h, i: (0, h))
    return q_spec, k_spec, v_spec, g_spec, dec_spec, gn_spec


def _ret_fwd_call(P, dec_f, dec_b, ret_norm_g, L, LC):
    B, T, _ = P.shape
    q_spec, k_spec, v_spec, g_spec, dec_spec, gn_spec = _ret_specs(T)

    def body(df_ref, db_ref, q_ref, k_ref, v_ref, g_ref, gn_ref, y_ref, o_ref):
        h, i = pl.program_id(1), pl.program_id(2)
        lgf, lgb = _head_scalar(df_ref, h), _head_scalar(db_ref, h)
        tpos = (i * TQ + lax.broadcasted_iota(jnp.int32, (TQ, 1), 0)).astype(F32)
        q = q_ref[...]

        def chunk(j, acc):
            ks = pl.multiple_of(j * TK, TK)
            kj, vj = k_ref[pl.ds(ks, TK), :], v_ref[pl.ds(ks, TK), :]
            _, dm = _decay_lat(tpos, ks, lgf, lgb)
            return acc + _dot((_dot_nt(q, kj) * dm).astype(BF16), vj)

        acc = lax.fori_loop(0, L // TK, chunk, jnp.zeros((TQ, RET_DK), F32))
        _, _, ef, eb = _decay_ctx(tpos, L, LC, lgf, lgb)
        acc = acc + _dot((_dot_nt(q, k_ref[L:L + LC, :]) * (ef + eb)).astype(BF16), v_ref[L:L + LC, :])
        o_ref[...] = acc
        rn = lax.rsqrt(jnp.mean(acc * acc, axis=-1, keepdims=True) + EPS)
        g = g_ref[...].astype(F32)
        y_ref[...] = ((acc * rn * gn_ref[...]).astype(F32) * (g * _sigmoid(g))).astype(BF16)

    tile = pl.BlockSpec((None, TQ, 128), lambda b, h, i: (b, i, h))
    return pl.pallas_call(
        body, name="ret_fwd", grid=(B, 4, L // TQ),
        in_specs=[dec_spec, dec_spec, q_spec, k_spec, v_spec, g_spec, gn_spec],
        out_specs=(tile, tile),
        out_shape=(jax.ShapeDtypeStruct((B, L, 512), BF16), jax.ShapeDtypeStruct((B, L, 512), F32)),
        compiler_params=_params(("arbitrary",) * 3))(dec_f, dec_b, P, P, P, P, ret_norm_g)


def _ret_bwd_call(P, dec_f, dec_b, ret_norm_g, o_ret, dY, cos2, sin2, L, LC):
    B, T, _ = P.shape
    ni = L // TQ
    kscale = RET_DK ** -0.5
    q_spec, k_spec, v_spec, g_spec, dec_spec, gn_spec = _ret_specs(T)

    def body(df_ref, db_ref, q_ref, k_ref, v_ref, g_ref, gn_ref, o_ref, dy_ref, cos_ref, sin_ref,
             dq_ref, dg_ref, dk_ref, dv_ref, dgn_ref, dlg_ref):
        h, i = pl.program_id(1), pl.program_id(2)
        lgf, lgb = _head_scalar(df_ref, h), _head_scalar(db_ref, h)
        tpos = (i * TQ + lax.broadcasted_iota(jnp.int32, (TQ, 1), 0)).astype(F32)

        @pl.when(i == 0)
        def _():
            dk_ref[...] = jnp.zeros_like(dk_ref)
            dv_ref[...] = jnp.zeros_like(dv_ref)
            dgn_ref[...] = jnp.zeros_like(dgn_ref)
            dlg_ref[...] = jnp.zeros_like(dlg_ref)

        q = q_ref[...]
        o = o_ref[...]
        g = g_ref[...].astype(F32)
        dy = dy_ref[...].astype(F32)
        gn = gn_ref[...]
        sg = _sigmoid(g)
        rn = lax.rsqrt(jnp.mean(o * o, axis=-1, keepdims=True) + EPS)
        nrm = o * rn
        dg_ref[...] = (dy * (nrm * gn) * (sg * (1.0 + g * (1.0 - sg)))).astype(BF16)
        dhn = dy * (g * sg)
        dgn_ref[...] += jnp.sum(dhn * nrm, axis=0, keepdims=True)
        dnrm = dhn * gn
        do = rn * (dnrm - nrm * jnp.mean(dnrm * nrm, axis=-1, keepdims=True))
        dob = do.astype(BF16)

        def chunk(j, dq):
            ks = pl.multiple_of(j * TK, TK)
            kj, vj = k_ref[pl.ds(ks, TK), :], v_ref[pl.ds(ks, TK), :]
            dist, dm = _decay_lat(tpos, ks, lgf, lgb)
            s = _dot_nt(q, kj)
            dsv = _dot_nt(dob, vj)
            dsb = (dsv * dm).astype(BF16)
            dk_ref[pl.ds(ks, TK), :] += _dot_tn(dsb, q)
            dv_ref[pl.ds(ks, TK), :] += _dot_tn((s * dm).astype(BF16), dob)
            xw = s * dsv * dm * jnp.abs(dist)
            tot = jnp.sum(xw, axis=0, keepdims=True)
            fwd = jnp.sum(jnp.where(dist > 0, xw, 0.0), axis=0, keepdims=True)
            dlg_ref[0:1, :] += fwd
            dlg_ref[1:2, :] += tot - fwd
            return dq + _dot(dsb, kj)

        dq = lax.fori_loop(0, L // TK, chunk, jnp.zeros((TQ, RET_DK), F32))
        kc, vc = k_ref[L:L + LC, :], v_ref[L:L + LC, :]
        dfc, dbc, ef, eb = _decay_ctx(tpos, L, LC, lgf, lgb)
        s = _dot_nt(q, kc)
        dsv = _dot_nt(dob, vc)
        dsb = (dsv * (ef + eb)).astype(BF16)
        dk_ref[L:L + LC, :] += _dot_tn(dsb, q)
        dv_ref[L:L + LC, :] += _dot_tn((s * (ef + eb)).astype(BF16), dob)
        a = s * dsv
        dlg_ref[0:1, 0:LC] += jnp.sum(a * ef * dfc, axis=0, keepdims=True)
        dlg_ref[1:2, 0:LC] += jnp.sum(a * eb * dbc, axis=0, keepdims=True)
        dq = dq + _dot(dsb, kc)
        cs, sn = cos_ref[pl.ds(pl.multiple_of(i * TQ, TQ), TQ), :], sin_ref[pl.ds(pl.multiple_of(i * TQ, TQ), TQ), :]
        dq_ref[...] = (dq * cs - pltpu.roll(dq, 64, 1) * sn).astype(BF16)

        @pl.when(i == ni - 1)
        def _():
            dk = dk_ref[...]
            dk_ref[...] = (dk * cos_ref[...] - pltpu.roll(dk, 64, 1) * sin_ref[...]) * kscale

    tile = pl.BlockSpec((None, TQ, 128), lambda b, h, i: (b, i, h))
    kv_out = pl.BlockSpec((None, T, 128), lambda b, h, i: (b, 0, h))
    tab = pl.BlockSpec((T, RET_DK), lambda b, h, i: (0, 0))
    return pl.pallas_call(
        body, name="ret_bwd", grid=(B, 4, ni),
        in_specs=[dec_spec, dec_spec, q_spec, k_spec, v_spec, g_spec, gn_spec, tile,
                  pl.BlockSpec((None, TQ, 128), lambda b, h, i: (b, i, 4 + h)), tab, tab],
        out_specs=(tile, tile, kv_out, kv_out,
                   pl.BlockSpec((None, 1, 128), lambda b, h, i: (b, 0, h)),
                   pl.BlockSpec((None, None, 8, TK), lambda b, h, i: (b, h, 0, 0))),
        out_shape=(jax.ShapeDtypeStruct((B, L, 512), BF16), jax.ShapeDtypeStruct((B, L, 512), BF16),
                   jax.ShapeDtypeStruct((B, T, 512), F32), jax.ShapeDtypeStruct((B, T, 512), F32),
                   jax.ShapeDtypeStruct((B, 1, 512), F32), jax.ShapeDtypeStruct((B, 4, 8, TK), F32)),
        compiler_params=_params(("arbitrary",) * 3))(
            dec_f, dec_b, P, P, P, P, ret_norm_g, o_ret, dY, cos2, sin2)


def _chunk_decay(lgf, lgb):
    tau = lax.broadcasted_iota(jnp.int32, (TQ, 1), 0).astype(F32)
    sig = lax.broadcasted_iota(jnp.int32, (1, TQ), 1).astype(F32)
    dist = tau - sig
    dm = jnp.exp(dist * jnp.where(dist > 0, lgf, -lgb)) * jnp.where(dist == 0, 2.0, 1.0)
    return tau, dist, dm


def _ret_states_call(P, dec_f, dec_b, L, LC):
    B, T, _ = P.shape
    n = L // TQ

    def body(df_ref, db_ref, k_ref, v_ref, sf_ref, sb_ref):
        h = pl.program_id(1)
        lgf, lgb = _head_scalar(df_ref, h), _head_scalar(db_ref, h)
        tau = lax.broadcasted_iota(jnp.int32, (TQ, 1), 0).astype(F32)
        jc = lax.broadcasted_iota(jnp.int32, (LC, 1), 0).astype(F32)
        wf, wb = jnp.exp(lgf * (TQ - 1.0 - tau)), jnp.exp(lgb * tau)
        gcf, gcb = jnp.exp(lgf * float(TQ)), jnp.exp(lgb * float(TQ))
        kc, vc = k_ref[L:L + LC, :].astype(F32), v_ref[L:L + LC, :]

        def chunk_state(i, w):
            ks = pl.multiple_of(i * TQ, TQ)
            return _dot_tn((k_ref[pl.ds(ks, TQ), :].astype(F32) * w).astype(BF16), v_ref[pl.ds(ks, TQ), :])

        def fwd(i, s):
            sf_ref[i] = s
            return gcf * s + chunk_state(i, wf)

        lax.fori_loop(0, n, fwd, _dot_tn((kc * jnp.exp(lgf * (LC - 1.0 - jc))).astype(BF16), vc))

        def bwd(r, s):
            i = n - 1 - r
            sb_ref[i] = s
            return gcb * s + chunk_state(i, wb)

        lax.fori_loop(0, n, bwd, _dot_tn((kc * jnp.exp(lgb * jc)).astype(BF16), vc))

    st = pl.BlockSpec((None, None, n, RET_DK, RET_DK), lambda b, h: (b, h, 0, 0, 0))
    return pl.pallas_call(
        body, name="ret_states", grid=(B, 4),
        in_specs=[pl.BlockSpec((1, 4), lambda b, h: (0, 0)), pl.BlockSpec((1, 4), lambda b, h: (0, 0)),
                  pl.BlockSpec((None, T, 128), lambda b, h: (b, 0, 20 + h)),
                  pl.BlockSpec((None, T, 128), lambda b, h: (b, 0, 24 + h))],
        out_specs=(st, st),
        out_shape=(jax.ShapeDtypeStruct((B, 4, n, RET_DK, RET_DK), F32),) * 2,
        compiler_params=_params(("arbitrary",) * 2))(dec_f, dec_b, P, P)


def _retc_fwd_call(P, sf, sb, dec_f, dec_b, ret_norm_g, L):
    B, T, _ = P.shape
    q_spec, _, _, g_spec, dec_spec, gn_spec = _ret_specs(T)
    k_spec = pl.BlockSpec((None, TQ, 128), lambda b, h, i: (b, i, 20 + h))
    v_spec = pl.BlockSpec((None, TQ, 128), lambda b, h, i: (b, i, 24 + h))
    st_spec = pl.BlockSpec((None, None, None, RET_DK, RET_DK), lambda b, h, i: (b, h, i, 0, 0))

    def body(df_ref, db_ref, q_ref, k_ref, v_ref, g_ref, gn_ref, sf_ref, sb_ref, y_ref, o_ref):
        h = pl.program_id(1)
        lgf, lgb = _head_scalar(df_ref, h), _head_scalar(db_ref, h)
        tau, _, dm = _chunk_decay(lgf, lgb)
        q = q_ref[...]
        qf = q.astype(F32)
        acc = _dot((_dot_nt(q, k_ref[...]) * dm).astype(BF16), v_ref[...])
        acc = acc + _dot((qf * jnp.exp(lgf * (tau + 1.0))).astype(BF16), sf_ref[...].astype(BF16))
        acc = acc + _dot((qf * jnp.exp(lgb * (TQ - tau))).astype(BF16), sb_ref[...].astype(BF16))
        o_ref[...] = acc
        rn = lax.rsqrt(jnp.mean(acc * acc, axis=-1, keepdims=True) + EPS)
        g = g_ref[...].astype(F32)
        y_ref[...] = ((acc * rn * gn_ref[...]) * (g * _sigmoid(g))).astype(BF16)

    tile = pl.BlockSpec((None, TQ, 128), lambda b, h, i: (b, i, h))
    return pl.pallas_call(
        body, name="ret_fwd", grid=(B, 4, L // TQ),
        in_specs=[dec_spec, dec_spec, q_spec, k_spec, v_spec, g_spec, gn_spec, st_spec, st_spec],
        out_specs=(tile, tile),
        out_shape=(jax.ShapeDtypeStruct((B, L, 512), BF16), jax.ShapeDtypeStruct((B, L, 512), F32)),
        compiler_params=_params(("arbitrary",) * 3))(dec_f, dec_b, P, P, P, P, ret_norm_g, sf, sb)


def _retc_bwd_call(P, sf, sb, dec_f, dec_b, ret_norm_g, o_ret, dY, cos2, sin2, L, LC):
    B, T, _ = P.shape
    n = L // TQ
    C = float(TQ)
    kscale = RET_DK ** -0.5
    q_spec, k_spec, v_spec, g_spec, dec_spec, gn_spec = _ret_specs(T)
    st_spec = pl.BlockSpec((None, None, n, RET_DK, RET_DK), lambda b, h, i: (b, h, 0, 0, 0))

    def body(df_ref, db_ref, q_ref, k_ref, v_ref, g_ref, gn_ref, o_ref, dy_ref, cos_ref, sin_ref, sf_ref, sb_ref,
             dq_ref, dg_ref, dk_ref, dv_ref, dgn_ref, dlg_ref, dsf_ref, dsb_ref):
        h, i = pl.program_id(1), pl.program_id(2)
        lgf, lgb = _head_scalar(df_ref, h), _head_scalar(db_ref, h)
        tau, dist, dm = _chunk_decay(lgf, lgb)

        @pl.when(i == 0)
        def _():
            dk_ref[...] = jnp.zeros_like(dk_ref)
            dv_ref[...] = jnp.zeros_like(dv_ref)
            dgn_ref[...] = jnp.zeros_like(dgn_ref)
            dlg_ref[...] = jnp.zeros_like(dlg_ref)

        def add_lg(row, x):
            cs = jnp.sum(x, axis=0, keepdims=True)
            tot = cs[:, 0:128]
            for part in range(1, x.shape[1] // 128):
                tot = tot + cs[:, part * 128:(part + 1) * 128]
            dlg_ref[row:row + 1, :] += tot

        q = q_ref[...]
        qf = q.astype(F32)
        o = o_ref[...]
        g = g_ref[...].astype(F32)
        dy = dy_ref[...].astype(F32)
        gn = gn_ref[...]
        sg = _sigmoid(g)
        rn = lax.rsqrt(jnp.mean(o * o, axis=-1, keepdims=True) + EPS)
        nrm = o * rn
        dg_ref[...] = (dy * (nrm * gn) * (sg * (1.0 + g * (1.0 - sg)))).astype(BF16)
        dhn = dy * (g * sg)
        dgn_ref[...] += jnp.sum(dhn * nrm, axis=0, keepdims=True)
        dnrm = dhn * gn
        do = rn * (dnrm - nrm * jnp.mean(dnrm * nrm, axis=-1, keepdims=True))
        dob = do.astype(BF16)
        rows = pl.ds(pl.multiple_of(i * TQ, TQ), TQ)
        ki, vi = k_ref[rows, :], v_ref[rows, :]
        s = _dot_nt(q, ki)
        dsv = _dot_nt(dob, vi)
        dsb = (dsv * dm).astype(BF16)
        dk_ref[rows, :] += _dot_tn(dsb, q)
        dv_ref[rows, :] += _dot_tn((s * dm).astype(BF16), dob)
        xw = s * dsv * dm * jnp.abs(dist)
        fpart = jnp.where(dist > 0, xw, 0.0)
        add_lg(0, fpart)
        add_lg(1, xw - fpart)
        dq = _dot(dsb, ki)
        af, ab = jnp.exp(lgf * (tau + 1.0)), jnp.exp(lgb * (C - tau))
        qa, qb = (qf * af).astype(BF16), (qf * ab).astype(BF16)
        sfi, sbi = sf_ref[i].astype(BF16), sb_ref[i].astype(BF16)
        dq = dq + af * _dot_nt(dob, sfi) + ab * _dot_nt(dob, sbi)
        dsf_ref[i] = _dot_tn(qa, dob)
        dsb_ref[i] = _dot_tn(qb, dob)
        add_lg(0, (tau + 1.0) * (_dot(qa, sfi) * do))
        add_lg(1, (C - tau) * (_dot(qb, sbi) * do))
        cs, sn = cos_ref[rows, :], sin_ref[rows, :]
        dq_ref[...] = (dq * cs - pltpu.roll(dq, 64, 1) * sn).astype(BF16)

        @pl.when(i == n - 1)
        def _():
            jc = lax.broadcasted_iota(jnp.int32, (LC, 1), 0).astype(F32)
            crow = pl.ds(L, LC)

            def through_state(rws, w, dw, gst, row):
                kk, vv = k_ref[rws, :].astype(F32), v_ref[rws, :]
                gb = gst.astype(BF16)
                vg = _dot_nt(vv, gb)
                kw = kk * w
                dk_ref[rws, :] += w * vg
                dv_ref[rws, :] += _dot(kw.astype(BF16), gb)
                add_lg(row, dw * (kw * vg))

            def scan(lg, gc, w, dw, st_ref, dst_ref, order, row):
                def step(r, gst):
                    j = order(r)
                    through_state(pl.ds(pl.multiple_of(j * TQ, TQ), TQ), w, dw, gst, row)
                    add_lg(row, (C * gc) * (gst * st_ref[j]))
                    return dst_ref[j] + gc * gst
                return lax.fori_loop(0, n, step, jnp.zeros((RET_DK, RET_DK), F32))

            gcf, gcb = jnp.exp(lgf * C), jnp.exp(lgb * C)
            g0 = scan(lgf, gcf, jnp.exp(lgf * (C - 1.0 - tau)), C - 1.0 - tau, sf_ref, dsf_ref,
                      lambda r: n - 1 - r, 0)
            through_state(crow, jnp.exp(lgf * (LC - 1.0 - jc)), LC - 1.0 - jc, g0, 0)
            g1 = scan(lgb, gcb, jnp.exp(lgb * tau), tau, sb_ref, dsb_ref, lambda r: r, 1)
            through_state(crow, jnp.exp(lgb * jc), jc, g1, 1)
            dk = dk_ref[...]
            dk_ref[...] = (dk * cos_ref[...] - pltpu.roll(dk, 64, 1) * sin_ref[...]) * kscale

    tile = pl.BlockSpec((None, TQ, 128), lambda b, h, i: (b, i, h))
    kv_out = pl.BlockSpec((None, T, 128), lambda b, h, i: (b, 0, h))
    tab = pl.BlockSpec((T, RET_DK), lambda b, h, i: (0, 0))
    return pl.pallas_call(
        body, name="ret_bwd", grid=(B, 4, n),
        in_specs=[dec_spec, dec_spec, q_spec, k_spec, v_spec, g_spec, gn_spec, tile,
                  pl.BlockSpec((None, TQ, 128), lambda b, h, i: (b, i, 4 + h)), tab, tab, st_spec, st_spec],
        out_specs=(tile, tile, kv_out, kv_out,
                   pl.BlockSpec((None, 1, 128), lambda b, h, i: (b, 0, h)),
                   pl.BlockSpec((None, None, 8, 128), lambda b, h, i: (b, h, 0, 0))),
        out_shape=(jax.ShapeDtypeStruct((B, L, 512), BF16), jax.ShapeDtypeStruct((B, L, 512), BF16),
                   jax.ShapeDtypeStruct((B, T, 512), F32), jax.ShapeDtypeStruct((B, T, 512), F32),
                   jax.ShapeDtypeStruct((B, 1, 512), F32), jax.ShapeDtypeStruct((B, 4, 8, 128), F32)),
        scratch_shapes=[pltpu.VMEM((n, RET_DK, RET_DK), F32), pltpu.VMEM((n, RET_DK, RET_DK), F32)],
        compiler_params=_params(("arbitrary",) * 3))(
            dec_f, dec_b, P, P, P, P, ret_norm_g, o_ret, dY, cos2, sin2, sf, sb)


def _out_call(y_na, y_ret, x, target, mod, final_g, wout_f):
    B, L, _ = x.shape

    def body(yn_ref, yr_ref, x_ref, t_ref, mod_ref, gf_ref, w_ref, dy_ref, dx2_ref, dw_ref, sm_ref):
        b, i = pl.program_id(0), pl.program_id(1)

        @pl.when((b == 0) & (i == 0))
        def _():
            dw_ref[...] = jnp.zeros_like(dw_ref)
            sm_ref[...] = jnp.zeros_like(sm_ref)

        gate = mod_ref[pl.ds(b, 1), 2 * D:3 * D]
        gf = gf_ref[...]
        yn, yr = yn_ref[...], yr_ref[...]
        ylat = _dot(yn, w_ref[0:512, :]) + _dot(yr, w_ref[512:1024, :])
        x2 = x_ref[...] + gate * ylat
        r = lax.rsqrt(jnp.mean(x2 * x2, axis=-1, keepdims=True) + EPS)
        xr = x2 * r
        err = xr * gf - t_ref[...]
        sm_ref[1:2, :] += jnp.sum(err * err, axis=0, keepdims=True)
        dout = err * (1.0 / D)
        sm_ref[0:1, :] += jnp.sum(dout * xr, axis=0, keepdims=True)
        gd = dout * gf
        dx2 = r * (gd - xr * jnp.mean(gd * xr, axis=-1, keepdims=True))
        dx2_ref[...] = dx2
        sm_ref[pl.ds(2 + b, 1), :] += jnp.sum(dx2 * ylat, axis=0, keepdims=True)
        dyl = (gate * dx2).astype(BF16)
        dy_ref[:, 0:512] = _dot_nt(dyl, w_ref[0:512, :]).astype(BF16)
        dy_ref[:, 512:1024] = _dot_nt(dyl, w_ref[512:1024, :]).astype(BF16)
        dw_ref[0:512, :] += _dot_tn(yn, dyl)
        dw_ref[512:1024, :] += _dot_tn(yr, dyl)

    half = pl.BlockSpec((None, TQ, 512), lambda b, i: (b, i, 0))
    full = pl.BlockSpec((None, TQ, D), lambda b, i: (b, i, 0))
    return pl.pallas_call(
        body, name="out_proj_loss", grid=(B, L // TQ),
        in_specs=[half, half, full, full,
                  pl.BlockSpec((8, 3 * D), lambda b, i: (0, 0)),
                  pl.BlockSpec((1, D), lambda b, i: (0, 0)),
                  pl.BlockSpec((D, D), lambda b, i: (0, 0))],
        out_specs=(full, full, pl.BlockSpec((D, D), lambda b, i: (0, 0)),
                   pl.BlockSpec((8, D), lambda b, i: (0, 0))),
        out_shape=(jax.ShapeDtypeStruct((B, L, D), BF16), jax.ShapeDtypeStruct((B, L, D), F32),
                   jax.ShapeDtypeStruct((D, D), F32), jax.ShapeDtypeStruct((8, D), F32)),
        compiler_params=_params(("arbitrary",) * 2))(y_na, y_ret, x, target, mod, final_g, wout_f)


def _dh_call(dsec, win_f, x, ctx, dx2, mod, norm_g, cp_in, cp_out):
    B, L, _ = x.shape
    LC = ctx.shape[1]
    nl = L // TQ

    def body(d0, d1, d2, d3, d4, d5, d6, d7, w_ref, x_ref, ctx_ref, dx2_ref, mod_ref, g_ref, cpi_ref, cpo_ref,
             gx_ref, sm_ref, sli_ref, slo_ref, ssem, rsem, lsem):
        drefs = (d0, d1, d2, d3, d4, d5, d6, d7)
        b, t = pl.program_id(0), pl.program_id(1)
        is_lat = t < nl

        @pl.when((b == 0) & (t == 0))
        def _():
            sm_ref[...] = jnp.zeros_like(sm_ref)

        def dh_of(secs):
            acc = jnp.zeros((TQ, D), F32)
            for sec in secs:
                s, half = divmod(sec, 2)
                acc = acc + _dot_nt(drefs[sec][...].astype(BF16), w_ref[s, :, half * 512:(half + 1) * 512])
            return acc

        def norm_bwd(dh, xt, mrow):
            scale = mrow[:, D:2 * D]
            g = g_ref[...]
            rstd = lax.rsqrt(jnp.mean(xt * xt, axis=-1, keepdims=True) + EPS)
            xn = xt * rstd
            dshift = jnp.sum(dh, axis=0, keepdims=True)
            dscale = jnp.sum(dh * (xn * g), axis=0, keepdims=True)
            dhn = dh * (1.0 + scale)
            sm_ref[0:1, :] += jnp.sum(dhn * xn, axis=0, keepdims=True)
            dxn = dhn * g
            dx = rstd * (dxn - xn * jnp.mean(dxn * xn, axis=-1, keepdims=True))
            return dshift, dscale, dx

        @pl.when(is_lat)
        def _():
            dshift, dscale, dx = norm_bwd(dh_of(range(8)), x_ref[...], mod_ref[pl.ds(b, 1), :])
            sm_ref[pl.ds(3 + b, 1), :] += dshift
            sm_ref[pl.ds(3 + B + b, 1), :] += dscale
            gx_ref[...] = dx2_ref[...] + dx

        @pl.when(jnp.logical_not(is_lat))
        def _():
            dshift, dscale, _ = norm_bwd(dh_of((1, 2, 5, 6)), ctx_ref[...], mod_ref[B:B + 1, :])
            sm_ref[1:2, :] += dshift
            sm_ref[2:3, :] += dscale

        mx, my, mc = _mesh_pos()
        s = 2 * mx + my
        cps, sls = (cpi_ref, cpo_ref), (sli_ref, slo_ref)
        own = [pltpu.make_async_copy(cps[a].at[s], sls[a].at[s], lsem.at[a]) for a in range(2)]
        sends, recvs, k = [], [], 0
        for px, py in _other_chips(mx, my):
            ps = 2 * px + py
            for a in range(2):
                sends.append(_remote(cps[a].at[ps], sls[a].at[s], ssem, rsem, k, (px, py, mc)))
                recvs.append(_remote(cps[a].at[s], sls[a].at[ps], ssem, rsem, k, (px, py, mc)))
                k += 1

        @pl.when((b == 0) & (t == 0))
        def _():
            for cp in own + sends:
                cp.start()

        @pl.when((b == B - 1) & (t == nl))
        def _():
            _finish(own, sends, recvs)

    lat = lambda b, t: (b, jnp.minimum(t, nl - 1), 0)
    tok = lambda b, t: (b, t, 0)
    sec_specs = [pl.BlockSpec((None, TQ, 512), lat if sec in (0, 3, 4, 7) else tok) for sec in range(8)]
    return pl.pallas_call(
        body, name="dh_norm_bwd", grid=(B, nl + 1),
        in_specs=sec_specs + [
            pl.BlockSpec((N_SHARD, D, D), lambda b, t: (0, 0, 0)),
            pl.BlockSpec((None, TQ, D), lat),
            pl.BlockSpec((None, LC, D), lambda b, t: (b, 0, 0)),
            pl.BlockSpec((None, TQ, D), lat),
            pl.BlockSpec((8, 3 * D), lambda b, t: (0, 0)),
            pl.BlockSpec((1, D), lambda b, t: (0, 0)), ANY, ANY],
        out_specs=(pl.BlockSpec((None, TQ, D), lat), pl.BlockSpec((8, D), lambda b, t: (0, 0)), ANY, ANY),
        out_shape=(jax.ShapeDtypeStruct((B, L, D), F32), jax.ShapeDtypeStruct((8, D), F32),
                   jax.ShapeDtypeStruct(cp_in.shape, cp_in.dtype), jax.ShapeDtypeStruct(cp_out.shape, cp_out.dtype)),
        scratch_shapes=[pltpu.SemaphoreType.DMA((6,)), pltpu.SemaphoreType.DMA((6,)),
                        pltpu.SemaphoreType.DMA((2,))],
        compiler_params=_params(("arbitrary",) * 2))(*dsec, win_f, x, ctx, dx2, mod, norm_g, cp_in, cp_out)


def _dw_call(dsec, h, L):
    B, T, _ = h.shape
    nl = L // TQ

    def body(d0, d1, d2, d3, d4, d5, d6, d7, h_ref, dw_ref, acc_ref):
        drefs = (d0, d1, d2, d3, d4, d5, d6, d7)
        b, t = pl.program_id(0), pl.program_id(1)

        @pl.when((b == 0) & (t == 0))
        def _():
            acc_ref[...] = jnp.zeros_like(acc_ref)

        hb = h_ref[...]

        def add(secs):
            for sec in secs:
                s, half = divmod(sec, 2)
                acc_ref[s, :, half * 512:(half + 1) * 512] += _dot_tn(hb, drefs[sec][...].astype(BF16))

        @pl.when(t < nl)
        def _():
            add(range(8))

        @pl.when(t >= nl)
        def _():
            add((1, 2, 5, 6))

        @pl.when((b == B - 1) & (t == nl))
        def _():
            dw_ref[...] = acc_ref[...].astype(BF16)

    lat = lambda b, t: (b, jnp.minimum(t, nl - 1), 0)
    tok = lambda b, t: (b, t, 0)
    sec_specs = [pl.BlockSpec((None, TQ, 512), lat if sec in (0, 3, 4, 7) else tok) for sec in range(8)]
    return pl.pallas_call(
        body, name="dw_in", grid=(B, nl + 1),
        in_specs=sec_specs + [pl.BlockSpec((None, TQ, D), tok)],
        out_specs=pl.BlockSpec((N_SHARD, D, D), lambda b, t: (0, 0, 0)),
        out_shape=jax.ShapeDtypeStruct((N_SHARD, D, D), BF16),
        scratch_shapes=[pltpu.VMEM((N_SHARD, D, D), F32)],
        compiler_params=_params(("arbitrary",) * 2, vmem_mb=56))(*dsec, h)


def _mesh_pos():
    return lax.axis_index("x"), lax.axis_index("y"), lax.axis_index("c")


def _flip(v, f):
    return 1 - v if f else v


def _remote(src, dst, ssem, rsem, k, peer):
    return pltpu.make_async_remote_copy(src_ref=src, dst_ref=dst, send_sem=ssem.at[k], recv_sem=rsem.at[k],
                                        device_id=peer, device_id_type=MESH)


def _other_chips(x, y):
    return [(_flip(x, fx), _flip(y, fy)) for fx, fy in ((1, 0), (0, 1), (1, 1))]


D2D_STREAMS = 8


def _row_chunks(src, dst, ssem, rsem, k, peer, rows, lead=None):
    step = rows // D2D_STREAMS
    out = []
    for r in range(D2D_STREAMS):
        idx = (pl.ds(r * step, step),) if lead is None else (lead, pl.ds(r * step, step))
        out.append(_remote(src.at[idx], dst.at[idx], ssem, rsem, k, peer))
    return out


def _all_to_all_small(src, dst_all, ssem, rsem, k0, x, y, cc):
    me = 4 * x + 2 * y + cc
    sends, recvs = [], []
    for f in range(1, N_DEV):
        px, py, pc = _flip(x, f & 4), _flip(y, f & 2), _flip(cc, f & 1)
        sends.append(_remote(src, dst_all.at[me], ssem, rsem, k0 + f - 1, (px, py, pc)))
        recvs.append(_remote(src, dst_all.at[4 * px + 2 * py + pc], ssem, rsem, k0 + f - 1, (px, py, pc)))
    return sends, recvs


def _finish(local, sends, recvs):
    for cp in recvs:
        cp.wait_recv()
    for cp in sends:
        cp.wait_send()
    for cp in local:
        cp.wait()


def _gather_call(win_b, wout_b, wada_b, c, rpb_flat):
    arrs = (win_b, wout_b, wada_b)
    hrs = [a.shape[0] // 2 for a in arrs]

    def body(win, wout, wada, c_ref, r_ref, win_f, wout_f, wada_f, c_all, bias_ref, et_ref, ssem, rsem, lsem):
        x, y, cc = _mesh_pos()
        s, me = 2 * x + y, 4 * x + 2 * y + cc
        sib = (x, y, 1 - cc)
        srcs, dsts = (win, wout, wada), (win_f, wout_f, wada_f)

        def half(a, shard, hc):
            return dsts[a].at[shard, pl.ds(hc * hrs[a], hrs[a])]

        local = [pltpu.make_async_copy(srcs[a], dsts[a].at[s], lsem.at[a]) for a in range(3)]
        local.append(pltpu.make_async_copy(c_ref, c_all.at[me], lsem.at[3]))
        ici_send, ici_recv, fwd_send, fwd_recv, fwd_chunks, k = [], [], [], [], [], 0
        for px, py in _other_chips(x, y):
            ps = 2 * px + py
            for a in range(3):
                mine = srcs[a].at[pl.ds(cc * hrs[a], hrs[a])]
                ici_send.append(_remote(mine, half(a, s, cc), ssem, rsem, k, (px, py, cc)))
                ici_recv.append(_remote(mine, half(a, ps, cc), ssem, rsem, k, (px, py, cc)))
                fwd_send.append(_remote(half(a, ps, cc), half(a, ps, cc), ssem, rsem, 9 + k, sib))
                fwd_recv.append(_remote(half(a, ps, 1 - cc), half(a, ps, 1 - cc), ssem, rsem, 9 + k, sib))
                fwd_chunks.append(_row_chunks(half(a, ps, cc), half(a, ps, cc), ssem, rsem, 9 + k, sib, hrs[a]))
                k += 1
        c_send, c_recv = _all_to_all_small(c_ref, c_all, ssem, rsem, 18, x, y, cc)
        for cp in local + ici_send + c_send:
            cp.start()
        _bias_body(r_ref, bias_ref, et_ref)
        for got, chunks in zip(ici_recv, fwd_chunks):
            got.wait_recv()
            for cp in chunks:
                cp.start()
        _finish(local, ici_send + fwd_send + c_send, fwd_recv + c_recv)

    return pl.pallas_call(
        body, name="weight_gather",
        in_specs=[pl.BlockSpec(memory_space=pltpu.VMEM)] * 4 + [pl.BlockSpec(memory_space=pltpu.SMEM)],
        out_specs=(pl.BlockSpec(memory_space=pltpu.VMEM),) * 5,
        out_shape=tuple(jax.ShapeDtypeStruct((N_SHARD,) + a.shape, a.dtype) for a in arrs)
        + (jax.ShapeDtypeStruct((N_DEV,) + c.shape, c.dtype),
           jax.ShapeDtypeStruct((rpb_flat.shape[0], 3, TQ, KW), F32)),
        scratch_shapes=[pltpu.VMEM((15, GRID_W, GRID_W), F32),
                        pltpu.SemaphoreType.DMA((25,)), pltpu.SemaphoreType.DMA((25,)),
                        pltpu.SemaphoreType.DMA((4,))],
        compiler_params=pltpu.CompilerParams(vmem_limit_bytes=56 << 20))(win_b, wout_b, wada_b, c, rpb_flat)


VROWS = 32


def _grad_halves_call(dwin_b, dwout_b, dbias, dlg):
    arrs = (dwin_b, dwout_b)
    hrs = [a.shape[1] // 2 for a in arrs]

    def body(din, dout, db_ref, dlg_ref, cp_in, cp_out, drpb_ref, dlgo_ref, got_in, got_out, p_ref, ssem, rsem):
        x, y, cc = _mesh_pos()
        sib = (x, y, 1 - cc)
        srcs, gots, cps = (din, dout), (got_in, got_out), (cp_in, cp_out)
        halves = [_remote(srcs[a].at[:, pl.ds((1 - cc) * hrs[a], hrs[a])], gots[a], ssem, rsem, a, sib)
                  for a in range(2)]
        for cp in halves:
            cp.start()
        _small_reduce_body(db_ref, dlg_ref, drpb_ref, dlgo_ref, p_ref)
        for cp in halves:
            cp.wait_recv()
        for a in range(2):
            for j in range(N_SHARD):
                def add(i, carry, a=a, j=j):
                    r = pl.multiple_of(i * VROWS, VROWS)
                    mine = srcs[a][j, pl.ds(pl.multiple_of(cc * hrs[a] + r, VROWS), VROWS), :].astype(F32)
                    cps[a][j, pl.ds(r, VROWS), :] = (
                        mine + gots[a][j, pl.ds(r, VROWS), :].astype(F32)).astype(BF16)
                    return carry
                lax.fori_loop(0, hrs[a] // VROWS, add, 0)
        for cp in halves:
            cp.wait_send()

    vmem = pl.BlockSpec(memory_space=pltpu.VMEM)
    half_shapes = [(N_SHARD, hrs[a], arrs[a].shape[2]) for a in range(2)]
    return pl.pallas_call(
        body, name="grad_halves",
        in_specs=[vmem] * 4, out_specs=(vmem,) * 4,
        out_shape=(jax.ShapeDtypeStruct(half_shapes[0], BF16), jax.ShapeDtypeStruct(half_shapes[1], BF16),
                   jax.ShapeDtypeStruct((dbias.shape[0], 16, 32), F32), jax.ShapeDtypeStruct((32, 128), F32)),
        scratch_shapes=[pltpu.VMEM(half_shapes[0], BF16), pltpu.VMEM(half_shapes[1], BF16),
                        pltpu.VMEM((32, GRID_W), F32),
                        pltpu.SemaphoreType.DMA((2,)), pltpu.SemaphoreType.DMA((2,))],
        compiler_params=pltpu.CompilerParams(vmem_limit_bytes=56 << 20))(dwin_b, dwout_b, dbias, dlg)


def _grad_finish_call(sl_in, sl_out, small):
    arrs = (sl_in, sl_out)

    def body(sin, sout, sm, gin, gout, sm_all, h_in, h_out, ssem, rsem, lsem):
        x, y, cc = _mesh_pos()
        me = 4 * x + 2 * y + cc
        sib = (x, y, 1 - cc)
        sls, hs, gs = (sin, sout), (h_in, h_out), (gin, gout)
        sm_send, sm_recv = _all_to_all_small(sm, sm_all, ssem, rsem, 2, x, y, cc)
        sm_own = pltpu.make_async_copy(sm, sm_all.at[me], lsem.at[0])
        for cp in sm_send + [sm_own]:
            cp.start()
        for a in range(2):
            def total(i, carry, a=a):
                rows = pl.ds(pl.multiple_of(i * VROWS, VROWS), VROWS)
                sl = sls[a]
                hs[a][rows, :] = ((sl[0, rows, :].astype(F32) + sl[1, rows, :].astype(F32))
                                  + sl[2, rows, :].astype(F32)) + sl[3, rows, :].astype(F32)
                return carry
            lax.fori_loop(0, arrs[a].shape[1] // VROWS, total, 0)
        mine = [pltpu.make_async_copy(hs[a], gs[a].at[cc], lsem.at[1 + a]) for a in range(2)]
        back = [_remote(hs[a], gs[a].at[cc], ssem, rsem, a, sib) for a in range(2)]
        back_recv = [_remote(hs[a], gs[a].at[1 - cc], ssem, rsem, a, sib) for a in range(2)]
        for cp in mine + back:
            cp.start()
        _finish(mine + [sm_own], back + sm_send, back_recv + sm_recv)

    vmem = pl.BlockSpec(memory_space=pltpu.VMEM)
    return pl.pallas_call(
        body, name="grad_finish",
        in_specs=[vmem] * 3, out_specs=(vmem,) * 3,
        out_shape=(jax.ShapeDtypeStruct((2,) + sl_in.shape[1:], F32),
                   jax.ShapeDtypeStruct((2,) + sl_out.shape[1:], F32),
                   jax.ShapeDtypeStruct((N_DEV,) + small.shape, F32)),
        scratch_shapes=[pltpu.VMEM(sl_in.shape[1:], F32), pltpu.VMEM(sl_out.shape[1:], F32),
                        pltpu.SemaphoreType.DMA((9,)), pltpu.SemaphoreType.DMA((9,)),
                        pltpu.SemaphoreType.DMA((3,))],
        compiler_params=pltpu.CompilerParams(vmem_limit_bytes=48 << 20))(sl_in, sl_out, small)


def _adamw(w, g, m, v):
    m = ADAM_B1 * m + (1.0 - ADAM_B1) * g
    v = ADAM_B2 * v + (1.0 - ADAM_B2) * (g * g)
    m_hat = m / (1.0 - ADAM_B1 ** ADAM_STEP)
    v_hat = v / (1.0 - ADAM_B2 ** ADAM_STEP)
    return -ADAM_LR * (m_hat / (jnp.sqrt(v_hat) + ADAM_EPS) + ADAM_WD * w), m, v


def _adam_call(w, m, v, g, name):
    R, C = w.shape
    tr = 256

    def body(w_ref, m_ref, v_ref, g_ref, d_ref, mo_ref, vo_ref):
        d_ref[...], mo_ref[...], vo_ref[...] = _adamw(w_ref[...], g_ref[...], m_ref[...], v_ref[...])

    spec = pl.BlockSpec((tr, C), lambda i: (i, 0))
    return pl.pallas_call(
        body, name=name, grid=(R // tr,), in_specs=[spec] * 4,
        out_specs=(spec,) * 3, out_shape=(jax.ShapeDtypeStruct((R, C), F32),) * 3,
        compiler_params=_params(("arbitrary",)))(w, m, v, g)


R_GF, R_NG, R_LOSS, R_RNG, R_LGF, R_LGB, R_SHIFT, R_SCALE, R_GATE, R_SHIFT_C, R_SCALE_C, R_RNG2, R_RPB = (
    0, 1, 2, 3, 4, 5, 6, 8, 10, 12, 13, 14, 16)
W_GF, W_NG, W_CCTX, W_RNG, W_DF, W_DB, W_BADA, W_RPB = 0, 1, 2, 3, 4, 5, 6, 9


def _small_final_call(sm_all, c_t, c_ctx, wada_f, wada, m_ada, v_ada, wsm, msm, vsm, B):
    ws = wada.shape[1]
    NB = N_DEV * B

    def body(sm_ref, ct_ref, cctx_ref, wf_ref, wa_ref, ma_ref, va_ref, w_ref, m_ref, v_ref,
             g_ref, d_ref, mo_ref, vo_ref, ga_ref, da_ref, mao_ref, vao_ref, loss_ref, dmod_ref):
        x, y, _ = _mesh_pos()
        s = 2 * x + y
        tot = sm_ref[0]
        for dv in range(1, N_DEV):
            tot = tot + sm_ref[dv]
        w = w_ref[...]
        for dv in range(N_DEV):
            for b in range(B):
                r = dv * B + b
                for part, row in enumerate((R_SHIFT, R_SCALE, R_GATE)):
                    dmod_ref[r:r + 1, part * D:(part + 1) * D] = sm_ref[dv, row + b:row + b + 1, :]
        dmod_ref[NB:NB + 1, 0:D] = tot[R_SHIFT_C:R_SHIFT_C + 1, :]
        dmod_ref[NB:NB + 1, D:2 * D] = tot[R_SCALE_C:R_SCALE_C + 1, :]
        dmod_ref[NB:NB + 1, 2 * D:3 * D] = jnp.zeros((1, D), F32)
        dmod_ref[NB + 1:, :] = jnp.zeros((dmod_ref.shape[0] - NB - 1, 3 * D), F32)
        dmod = dmod_ref[...]
        cc = cctx_ref[...]
        scc = _sigmoid(cc)
        ct = ct_ref[...]
        act_t = ct * _sigmoid(ct)
        dmc = dmod[NB:NB + 1, :].astype(BF16)
        dact = jnp.zeros((1, D), F32)
        for sh in range(N_SHARD):
            dact = dact + _dot_nt(dmc[:, sh * ws:(sh + 1) * ws], wf_ref[sh])
        g = jnp.zeros((16, D), F32)
        rows = lax.broadcasted_iota(jnp.int32, (16, D), 0)

        def put(g, row, val):
            return jnp.where(rows == row, val, g)

        g = put(g, W_GF, tot[R_GF:R_GF + 1, :])
        g = put(g, W_NG, tot[R_NG:R_NG + 1, :])
        g = put(g, W_CCTX, dact * (scc * (1.0 + cc * (1.0 - scc))))
        g = put(g, W_RNG, tot[R_RNG:R_RNG + 1, :] + tot[R_RNG2:R_RNG2 + 1, :])
        g = put(g, W_DF, tot[R_LGF:R_LGF + 1, :] * (-jnp.exp(w[W_DF:W_DF + 1, :])))
        g = put(g, W_DB, tot[R_LGB:R_LGB + 1, :] * (-jnp.exp(w[W_DB:W_DB + 1, :])))
        db = jnp.sum(dmod, axis=0, keepdims=True)
        for part in range(3):
            g = put(g, W_BADA + part, db[:, part * D:(part + 1) * D])
        for part in range(4):
            g = put(g, W_RPB + part, tot[R_RPB + part:R_RPB + part + 1, :])
        g_ref[...] = g
        d_ref[...], mo_ref[...], vo_ref[...] = _adamw(w, g, m_ref[...], v_ref[...])
        loss_ref[...] = jnp.broadcast_to(
            (0.5 / D) * jnp.sum(tot[R_LOSS:R_LOSS + 1, :], axis=1, keepdims=True), (8, 128))
        for sh in range(N_SHARD):
            @pl.when(s == sh)
            def _():
                ga = jnp.dot(act_t, dmod[:, sh * ws:(sh + 1) * ws], precision=HIGHEST,
                             preferred_element_type=F32)
                ga_ref[...] = ga
                da_ref[...], mao_ref[...], vao_ref[...] = _adamw(wa_ref[...], ga, ma_ref[...], va_ref[...])

    sh_small = jax.ShapeDtypeStruct((16, D), F32)
    sh_ada = jax.ShapeDtypeStruct(wada.shape, F32)
    return pl.pallas_call(
        body, name="small_final",
        out_shape=(sh_small,) * 4 + (sh_ada,) * 4 + (jax.ShapeDtypeStruct((8, 128), F32),),
        scratch_shapes=[pltpu.VMEM((NB + 8, 3 * D), F32)],
        compiler_params=_params(vmem_mb=56))(
            sm_all, c_t, c_ctx, wada_f, wada, m_ada, v_ada, wsm, msm, vsm)


def _local_step(x, c, ctx, c_ctx, norm_g, wada_f, b_ada, win_f, bias, dec_f, dec_b, ret_norm_g,
                wout_f, final_g, target):
    B, L, _ = x.shape
    LC = ctx.shape[1]
    assert B == 2
    cos2, sin2 = _rope_tables(L, LC)
    c8 = jnp.concatenate([c, c_ctx[None, :], jnp.zeros((8 - B - 1, D), F32)], axis=0)
    mod = _mod_call(c8, wada_f, b_ada)
    P, h = _inproj_call(x, ctx, mod, norm_g, win_f, cos2, sin2)
    y_na = _na_fwd_call(P, bias, L, LC)
    sf, sb = _ret_states_call(P, dec_f, dec_b, L, LC)
    y_ret, o_ret = _retc_fwd_call(P, sf, sb, dec_f, dec_b, ret_norm_g, L)
    dY, dx2, dwout_p, sm_out = _out_call(y_na, y_ret, x, target, mod, final_g, wout_f.reshape(D, D))
    dnq, dng, dnk, dnv, dbias = _na_bwd_call(P, bias, dY, L, LC)
    drq, drg, drk, drv, dgn, dlg = _retc_bwd_call(P, sf, sb, dec_f, dec_b, ret_norm_g, o_ret, dY, cos2, sin2, L, LC)
    dsec = (dnq, dnk, dnv, dng, drq, drk, drv, drg)
    dwin_b = _dw_call(dsec, h, L)
    cp_in, cp_out, drpb, dlg_sum = _grad_halves_call(
        dwin_b, dwout_p.astype(BF16).reshape(N_SHARD, D // N_SHARD, D), dbias, dlg)
    grad_x, sm_dh, sl_in, sl_out = _dh_call(dsec, win_f, x, ctx, dx2, mod, norm_g, cp_in, cp_out)
    z = jnp.zeros((1, D), F32)
    pad = lambda v: jnp.pad(v.reshape(1, -1), ((0, 0), (0, D - v.size)))
    dlg_sum = dlg_sum.reshape(4, 8, 128)
    rpb_rows = jnp.pad(drpb[:, :15, :31].reshape(-1), (0, 4 * D - drpb.shape[0] * 465)).reshape(4, D)
    small = jnp.concatenate([
        sm_out[0:1], sm_dh[0:1], sm_out[1:2], pad(dgn[0]), pad(dlg_sum[:, 0, 0]), pad(dlg_sum[:, 1, 0]),
        sm_dh[3:5], sm_dh[5:7], sm_out[2:4], sm_dh[1:2], sm_dh[2:3], pad(dgn[1]), z, rpb_rows,
        jnp.zeros((SM_ROWS - 20, D), F32)], axis=0)
    return grad_x, sl_in, sl_out, small


def kernel(x, c, ctx, c_ctx, norm_g, w_ada, b_ada, w_in, na_rpb, ret_decay_fwd, ret_decay_bwd, ret_norm_g, w_out, final_norm_g, loss_target, m_c_ctx, m_norm_g, m_w_ada, m_b_ada, m_w_in, m_na_rpb, m_ret_decay_fwd, m_ret_decay_bwd, m_ret_norm_g, m_w_out, m_final_norm_g, v_c_ctx, v_norm_g, v_w_ada, v_b_ada, v_w_in, v_na_rpb, v_ret_decay_fwd, v_ret_decay_bwd, v_ret_norm_g, v_w_out, v_final_norm_g):
    B = x.shape[0]
    win_f, wout_f, wada_f, c_all, bias = _gather_call(
        w_in[0].astype(BF16), w_out[0].astype(BF16), w_ada[0].astype(BF16), c,
        na_rpb[0].reshape(na_rpb.shape[1], -1))
    grad_x, sl_in, sl_out, small = _local_step(
        x, c, ctx, c_ctx, norm_g, wada_f, b_ada, win_f, bias, ret_decay_fwd, ret_decay_bwd,
        ret_norm_g, wout_f, final_norm_g.reshape(1, D), loss_target)
    gin, gout, sm_all = _grad_finish_call(sl_in, sl_out, small)
    g_win, g_wout = gin.reshape(w_in.shape[1:]), gout.reshape(w_out.shape[1:])
    d_win, nm_win, nv_win = _adam_call(w_in[0], m_w_in[0], v_w_in[0], g_win, "adam_w_in")
    d_wout, nm_wout, nv_wout = _adam_call(w_out[0], m_w_out[0], v_w_out[0], g_wout, "adam_w_out")

    def pack(gf, ng, cc, rng, df, db, bada, rpb):
        pad = lambda v: jnp.pad(v.reshape(1, -1), ((0, 0), (0, D - v.size)))
        return jnp.concatenate([
            gf.reshape(1, D), ng.reshape(1, D), cc.reshape(1, D), pad(rng), pad(df), pad(db),
            bada.reshape(3, D), jnp.pad(rpb.reshape(-1), (0, 4 * D - rpb.size)).reshape(4, D),
            jnp.zeros((3, D), F32)], axis=0)

    wsm = pack(final_norm_g, norm_g, c_ctx, ret_norm_g, ret_decay_fwd, ret_decay_bwd, b_ada, na_rpb)
    msm = pack(m_final_norm_g, m_norm_g, m_c_ctx, m_ret_norm_g, m_ret_decay_fwd, m_ret_decay_bwd, m_b_ada, m_na_rpb)
    vsm = pack(v_final_norm_g, v_norm_g, v_c_ctx, v_ret_norm_g, v_ret_decay_fwd, v_ret_decay_bwd, v_b_ada, v_na_rpb)
    c_t = jnp.concatenate([c_all.reshape(N_DEV * B, D), c_ctx.reshape(1, D), jnp.zeros((7, D), F32)], axis=0).T
    outs = _small_final_call(sm_all, c_t, c_ctx.reshape(1, D), wada_f,
                             w_ada[0], m_w_ada[0], v_w_ada[0], wsm, msm, vsm, B)
    smalls, adas, loss = outs[0:4], outs[4:8], outs[8][0, 0]

    def unpack(p):
        rw = ret_norm_g.shape[1]
        return dict(
            final_norm_g=p[W_GF], norm_g=p[W_NG:W_NG + 1], c_ctx=p[W_CCTX], ret_norm_g=p[W_RNG:W_RNG + 1, :rw],
            ret_decay_fwd=p[W_DF:W_DF + 1, :4], ret_decay_bwd=p[W_DB:W_DB + 1, :4],
            b_ada=p[W_BADA:W_BADA + 3].reshape(1, 3 * D),
            na_rpb=p[W_RPB:W_RPB + 4].reshape(-1)[:na_rpb.size].reshape(na_rpb.shape))

    res = []
    for p, ada, win_o, wout_o in zip(smalls, adas, (g_win, d_win, nm_win, nv_win),
                                     (g_wout, d_wout, nm_wout, nv_wout)):
        u = unpack(p)
        res.append([u["c_ctx"], u["norm_g"], ada[None], u["b_ada"], win_o[None], u["na_rpb"],
                    u["ret_decay_fwd"], u["ret_decay_bwd"], u["ret_norm_g"], wout_o[None], u["final_norm_g"]])
    return (loss, grad_x, *res[0], *res[1], *res[2], *res[3])
```

```python
import numpy as np
import jax
import jax.numpy as jnp
from jax import lax
from jax.experimental import pallas as pl
from jax.experimental.pallas import tpu as pltpu

F32 = jnp.float32
BF16 = jnp.bfloat16
HIGHEST = lax.Precision.HIGHEST

D = 1024
GRID_W = 64
NA_DH = 64
RET_DK = 128
ROPE_BASE = 10000.0
EPS = 1e-6
NEG = -1e30
TQ = 256
KW = 12 * GRID_W
N_SHARD = 4
N_DEV = 8
SM_ROWS = 24

ADAM_LR = 0.001
ADAM_B1 = 0.9
ADAM_B2 = 0.999
ADAM_EPS = 1e-08
ADAM_WD = 0.01
ADAM_STEP = 10

MESH = pl.DeviceIdType.MESH
ANY = pl.BlockSpec(memory_space=pl.ANY)


def _params(sem=None, vmem_mb=48):
    return pltpu.CompilerParams(dimension_semantics=sem, vmem_limit_bytes=vmem_mb << 20)


def _dot(a, b):
    return jnp.dot(a, b, preferred_element_type=F32)


def _dot_nt(a, b):
    return lax.dot_general(a, b, (((1,), (1,)), ((), ())), preferred_element_type=F32)


def _dot_tn(a, b):
    return lax.dot_general(a, b, (((0,), (0,)), ((), ())), preferred_element_type=F32)


def _sigmoid(x):
    return 1.0 / (1.0 + jnp.exp(-x))


def _rope_tables(L, LC):
    half = RET_DK // 2
    nf = half // 2
    t = np.arange(L)
    row = (t // GRID_W).astype(np.float32)
    col = (t % GRID_W).astype(np.float32)
    inv = (np.float32(ROPE_BASE) ** (-np.arange(nf, dtype=np.float32) / np.float32(nf))).astype(np.float32)
    ang = np.concatenate([row[:, None] * inv, col[:, None] * inv], axis=-1).astype(np.float32)
    cos, sin = np.cos(ang).astype(np.float32), np.sin(ang).astype(np.float32)
    cos2 = np.concatenate([cos, cos], axis=-1)
    sin2 = np.concatenate([-sin, sin], axis=-1)
    cos2 = np.concatenate([cos2, np.ones((LC, RET_DK), np.float32)], axis=0)
    sin2 = np.concatenate([sin2, np.zeros((LC, RET_DK), np.float32)], axis=0)
    return jnp.asarray(cos2), jnp.asarray(sin2)


def _mod_call(c8, wada_f, b_ada):
    ws = wada_f.shape[2]

    def body(c_ref, w_ref, b_ref, o_ref):
        a = c_ref[...]
        a = (a * _sigmoid(a)).astype(BF16)
        for s in range(N_SHARD):
            o_ref[:, s * ws:(s + 1) * ws] = _dot(a, w_ref[s]) + b_ref[:, s * ws:(s + 1) * ws]

    return pl.pallas_call(
        body, name="ada_mod", out_shape=jax.ShapeDtypeStruct((8, 3 * D), F32),
        compiler_params=_params())(c8, wada_f, b_ada)


def _dc_masks():
    cq = lax.broadcasted_iota(jnp.int32, (GRID_W, GRID_W), 0)
    ck = lax.broadcasted_iota(jnp.int32, (GRID_W, GRID_W), 1)
    dc = jnp.clip(ck - cq + 15, 0, 30)
    c0 = jnp.clip(cq - 8, 0, GRID_W - 16)
    col_ok = (ck >= c0) & (ck < c0 + 16)
    return dc, col_ok


def _bias_blocks():
    out = []
    for typ, delta in enumerate((4, 0, -4)):
        for rq in range(4):
            for rkk in range(12):
                dr = rkk + delta - rq - 4
                if typ == 0:
                    ok = -rq <= dr <= 7 - rq
                elif typ == 1:
                    ok = -4 <= dr <= 3
                else:
                    ok = -4 - rq <= dr <= 3 - rq
                out.append((typ, rq, rkk, dr if ok else None))
    return out


def _bias_body(r_ref, bias_ref, et_ref):
    dc, col_ok = _dc_masks()
    masks = [(dc == j).astype(F32) for j in range(31)]

    def per_h(h, carry):
        for dr in range(15):
            t = jnp.zeros((GRID_W, GRID_W), F32)
            for j in range(31):
                t = t + masks[j] * r_ref[h, dr * 31 + j]
            et_ref[dr] = jnp.where(col_ok, t, NEG)
        neg = jnp.full((GRID_W, GRID_W), NEG, F32)
        for typ, rq, rkk, dr in _bias_blocks():
            blk = neg if dr is None else et_ref[dr + 7]
            bias_ref[h, typ, rq * 64:(rq + 1) * 64, rkk * 64:(rkk + 1) * 64] = blk
        return carry

    lax.fori_loop(0, bias_ref.shape[0], per_h, 0)


def _small_reduce_body(db_ref, dlg_ref, drpb_ref, dlgo_ref, p_ref):
    dc, _ = _dc_masks()
    masks = [(dc == j).astype(F32) for j in range(31)]
    ones = jnp.ones((8, GRID_W), F32)
    p_ref[...] = jnp.zeros_like(p_ref)
    drpb_ref[...] = jnp.zeros_like(drpb_ref)

    def per_h(h, carry):
        acc = {}
        for typ, rq, rkk, dr in _bias_blocks():
            if dr is None:
                continue
            blk = db_ref[h, typ, rq * 64:(rq + 1) * 64, rkk * 64:(rkk + 1) * 64]
            acc[dr] = blk if dr not in acc else acc[dr] + blk
        for dr in range(-7, 8):
            t = acc[dr]
            for j in range(31):
                p_ref[j:j + 1, :] = jnp.sum(t * masks[j], axis=0, keepdims=True)
            red = lax.dot_general(ones, p_ref[...], (((1,), (1,)), ((), ())),
                                  precision=HIGHEST, preferred_element_type=F32)
            drpb_ref[h, dr + 7:dr + 8, :] = red[0:1, :]
        return carry

    lax.fori_loop(0, db_ref.shape[0], per_h, 0)
    x = dlg_ref[0]
    for b in range(1, dlg_ref.shape[0]):
        x = x + dlg_ref[b]
    x = x.reshape(4 * 8, x.shape[-1])
    dlgo_ref[...] = jnp.dot(x, jnp.ones((x.shape[-1], 128), F32), precision=HIGHEST,
                            preferred_element_type=F32)


def _inproj_call(x, ctx, mod, norm_g, win_f, cos2, sin2):
    B, L, _ = x.shape
    LC = ctx.shape[1]
    T = L + LC
    nl = L // TQ
    assert LC == TQ and L % TQ == 0
    kscale = RET_DK ** -0.5

    def body(x_ref, ctx_ref, mod_ref, g_ref, w_ref, cos_ref, sin_ref, p_ref, h_ref):
        b = pl.program_id(0)
        t = pl.program_id(1)
        is_lat = t < nl
        xt = jnp.where(is_lat, x_ref[...], ctx_ref[...])
        mrow = mod_ref[pl.ds(jnp.where(is_lat, b, B), 1), :]
        shift, scale = mrow[:, 0:D], mrow[:, D:2 * D]
        rstd = lax.rsqrt(jnp.mean(xt * xt, axis=-1, keepdims=True) + EPS)
        hb = ((xt * rstd * g_ref[...]) * (1.0 + scale) + shift).astype(BF16)
        h_ref[...] = hb
        cs, sn = cos_ref[...], sin_ref[...]
        for sec in range(8):
            s, half = divmod(sec, 2)
            acc = _dot(hb, w_ref[s, :, half * 512:(half + 1) * 512])
            if sec == 0:
                acc = acc * (NA_DH ** -0.5)
            if sec in (4, 5):
                for j in range(4):
                    a = acc[:, j * 128:(j + 1) * 128]
                    r = a * cs + pltpu.roll(a, 64, 1) * sn
                    if sec == 5:
                        r = r * kscale
                    p_ref[:, sec * 512 + j * 128:sec * 512 + (j + 1) * 128] = r.astype(BF16)
            else:
                p_ref[:, sec * 512:(sec + 1) * 512] = acc.astype(BF16)

    return pl.pallas_call(
        body, name="in_proj", grid=(B, T // TQ),
        in_specs=[
            pl.BlockSpec((None, TQ, D), lambda b, t: (b, jnp.minimum(t, nl - 1), 0)),
            pl.BlockSpec((None, TQ, D), lambda b, t: (b, 0, 0)),
            pl.BlockSpec((8, 3 * D), lambda b, t: (0, 0)),
            pl.BlockSpec((1, D), lambda b, t: (0, 0)),
            pl.BlockSpec((N_SHARD, D, D), lambda b, t: (0, 0, 0)),
            pl.BlockSpec((TQ, RET_DK), lambda b, t: (t, 0)),
            pl.BlockSpec((TQ, RET_DK), lambda b, t: (t, 0)),
        ],
        out_specs=(pl.BlockSpec((None, TQ, 4 * D), lambda b, t: (b, t, 0)),
                   pl.BlockSpec((None, TQ, D), lambda b, t: (b, t, 0))),
        out_shape=(jax.ShapeDtypeStruct((B, T, 4 * D), BF16), jax.ShapeDtypeStruct((B, T, D), BF16)),
        compiler_params=_params(("arbitrary", "arbitrary")))(x, ctx, mod, norm_g, win_f, cos2, sin2)


def _na_specs(L, T, rows):
    nm = rows // 4
    q_spec = pl.BlockSpec((None, TQ, 128), lambda hp, b, m: (b, m, hp))
    k_spec = pl.BlockSpec((None, T, 128), lambda hp, b, m: (b, 0, 4 + hp))
    v_spec = pl.BlockSpec((None, T, 128), lambda hp, b, m: (b, 0, 8 + hp))
    g_spec = pl.BlockSpec((None, TQ, 128), lambda hp, b, m: (b, m, 12 + hp))
    bias_spec = pl.BlockSpec((2, 3, TQ, KW), lambda hp, b, m: (hp, 0, 0, 0))
    return nm, q_spec, k_spec, v_spec, g_spec, bias_spec


def _na_tile(m, nm, rows):
    typ = jnp.where(m == 0, 0, jnp.where(m == nm - 1, 2, 1))
    start = pl.multiple_of(jnp.clip(4 * m - 4, 0, rows - 12) * GRID_W, TQ)
    return typ, start


def _na_fwd_call(P, bias, L, LC):
    B, T, _ = P.shape
    rows = L // GRID_W
    nm, q_spec, k_spec, v_spec, g_spec, bias_spec = _na_specs(L, T, rows)

    def body(q_ref, k_ref, v_ref, g_ref, bias_ref, y_ref, o_ref):
        typ, start = _na_tile(pl.program_id(2), nm, rows)
        for hh in range(2):
            ln = slice(hh * NA_DH, (hh + 1) * NA_DH)
            q = q_ref[:, ln]
            kw, vw = k_ref[pl.ds(start, KW), ln], v_ref[pl.ds(start, KW), ln]
            kc, vc = k_ref[L:L + LC, ln], v_ref[L:L + LC, ln]
            s1 = _dot_nt(q, kw) + bias_ref[hh, typ]
            s2 = _dot_nt(q, kc)
            mx = jnp.maximum(jnp.max(s1, axis=-1, keepdims=True), jnp.max(s2, axis=-1, keepdims=True))
            p1, p2 = jnp.exp(s1 - mx), jnp.exp(s2 - mx)
            inv = 1.0 / (jnp.sum(p1, axis=-1, keepdims=True) + jnp.sum(p2, axis=-1, keepdims=True))
            o = (_dot(p1.astype(BF16), vw) + _dot(p2.astype(BF16), vc)) * inv
            g = g_ref[:, ln].astype(F32)
            o_ref[:, ln] = o.astype(BF16)
            y_ref[:, ln] = (o * (g * _sigmoid(g))).astype(BF16)

    tile = pl.BlockSpec((None, TQ, 128), lambda hp, b, m: (b, m, hp))
    return pl.pallas_call(
        body, name="na_fwd", grid=(4, B, nm),
        in_specs=[q_spec, k_spec, v_spec, g_spec, bias_spec],
        out_specs=(tile, tile),
        out_shape=(jax.ShapeDtypeStruct((B, L, 512), BF16),) * 2,
        compiler_params=_params(("arbitrary",) * 3))(P, P, P, P, bias)


def _na_bwd_call(P, bias, dY, o_na, L, LC):
    B, T, _ = P.shape
    rows = L // GRID_W
    nm, q_spec, k_spec, v_spec, g_spec, bias_spec = _na_specs(L, T, rows)
    scale = NA_DH ** -0.5

    RB = 32

    def body(q_ref, k_ref, v_ref, g_ref, bias_ref, dy_ref, o_ref, dq_ref, dg_ref, dk_ref, dv_ref, db_ref,
             s1_ref, s2_ref, dp1_ref, dp2_ref, p1_ref, p2_ref, ds1_ref, ds2_ref):
        b, m = pl.program_id(1), pl.program_id(2)
        typ, start = _na_tile(m, nm, rows)

        @pl.when(m == 0)
        def _():
            dk_ref[...] = jnp.zeros_like(dk_ref)
            dv_ref[...] = jnp.zeros_like(dv_ref)

        @pl.when((m == 0) & (b == 0))
        def _():
            db_ref[...] = jnp.zeros_like(db_ref)

        for hh in range(2):
            ln = slice(hh * NA_DH, (hh + 1) * NA_DH)
            q = q_ref[:, ln]
            kw, vw = k_ref[pl.ds(start, KW), ln], v_ref[pl.ds(start, KW), ln]
            kc, vc = k_ref[L:L + LC, ln], v_ref[L:L + LC, ln]
            g = g_ref[:, ln].astype(F32)
            sg = _sigmoid(g)
            dy = dy_ref[:, ln].astype(F32)
            do = (dy * (g * sg)).astype(BF16)
            s1_ref[...] = _dot_nt(q, kw)
            s2_ref[...] = _dot_nt(q, kc)
            dp1_ref[...] = _dot_nt(do, vw)
            dp2_ref[...] = _dot_nt(do, vc)

            def rows_pass(r, carry, hh=hh):
                rw = pl.ds(pl.multiple_of(r * RB, RB), RB)
                a = s1_ref[rw, :] + bias_ref[hh, typ, rw, :]
                c = s2_ref[rw, :]
                mx = jnp.maximum(jnp.max(a, axis=-1, keepdims=True), jnp.max(c, axis=-1, keepdims=True))
                e1, e2 = jnp.exp(a - mx), jnp.exp(c - mx)
                inv = 1.0 / (jnp.sum(e1, axis=-1, keepdims=True) + jnp.sum(e2, axis=-1, keepdims=True))
                p1, p2 = e1 * inv, e2 * inv
                p1_ref[rw, :] = p1.astype(BF16)
                p2_ref[rw, :] = p2.astype(BF16)
                dp1, dp2 = dp1_ref[rw, :], dp2_ref[rw, :]
                delta = jnp.sum(p1 * dp1, axis=-1, keepdims=True) + jnp.sum(p2 * dp2, axis=-1, keepdims=True)
                ds1 = p1 * (dp1 - delta)
                db_ref[hh, typ, rw, :] += ds1
                ds1_ref[rw, :] = ds1.astype(BF16)
                ds2_ref[rw, :] = (p2 * (dp2 - delta)).astype(BF16)
                return carry

            lax.fori_loop(0, TQ // RB, rows_pass, 0, unroll=True)
            p1b, p2b, ds1b, ds2b = p1_ref[...], p2_ref[...], ds1_ref[...], ds2_ref[...]
            dg_ref[:, ln] = (dy * o_ref[:, ln].astype(F32) * (sg * (1.0 + g * (1.0 - sg)))).astype(BF16)
            dq_ref[:, ln] = ((_dot(ds1b, kw) + _dot(ds2b, kc)) * scale).astype(BF16)
            dk_ref[pl.ds(start, KW), ln] += _dot_tn(ds1b, q)
            dv_ref[pl.ds(start, KW), ln] += _dot_tn(p1b, do)
            dk_ref[L:L + LC, ln] += _dot_tn(ds2b, q)
            dv_ref[L:L + LC, ln] += _dot_tn(p2b, do)

    tile = pl.BlockSpec((None, TQ, 128), lambda hp, b, m: (b, m, hp))
    kv_out = pl.BlockSpec((None, T, 128), lambda hp, b, m: (b, 0, hp))
    wide, narrow = (TQ, KW), (TQ, LC)
    return pl.pallas_call(
        body, name="na_bwd", grid=(4, B, nm),
        in_specs=[q_spec, k_spec, v_spec, g_spec, bias_spec, tile, tile],
        out_specs=(tile, tile, kv_out, kv_out, bias_spec),
        out_shape=(jax.ShapeDtypeStruct((B, L, 512), BF16), jax.ShapeDtypeStruct((B, L, 512), BF16),
                   jax.ShapeDtypeStruct((B, T, 512), F32), jax.ShapeDtypeStruct((B, T, 512), F32),
                   jax.ShapeDtypeStruct(bias.shape, F32)),
        scratch_shapes=[pltpu.VMEM(wide, F32), pltpu.VMEM(narrow, F32), pltpu.VMEM(wide, F32), pltpu.VMEM(narrow, F32),
                        pltpu.VMEM(wide, BF16), pltpu.VMEM(narrow, BF16), pltpu.VMEM(wide, BF16),
                        pltpu.VMEM(narrow, BF16)],
        compiler_params=_params(("arbitrary",) * 3))(P, P, P, P, bias, dY, o_na)


def _head_scalar(dec_ref, h):
    lane = lax.broadcasted_iota(jnp.int32, dec_ref.shape, 1)
    return -jnp.sum(jnp.where(lane == h, jnp.exp(dec_ref[...]), 0.0), axis=1, keepdims=True)


def _ret_specs(T):
    q_spec = pl.BlockSpec((None, TQ, 128), lambda b, h, i: (b, i, 16 + h))
    k_spec = pl.BlockSpec((None, T, 128), lambda b, h, i: (b, 0, 20 + h))
    v_spec = pl.BlockSpec((None, T, 128), lambda b, h, i: (b, 0, 24 + h))
    g_spec = pl.BlockSpec((None, TQ, 128), lambda b, h, i: (b, i, 28 + h))
    dec_spec = pl.BlockSpec((1, 4), lambda b, h, i: (0, 0))
    gn_spec = pl.BlockSpec((1, 128), lambda b, h, i: (0, h))
    return q_spec, k_spec, v_spec, g_spec, dec_spec, gn_spec


def _chunk_decay(lgf, lgb):
    tau = lax.broadcasted_iota(jnp.int32, (TQ, 1), 0).astype(F32)
    sig = lax.broadcasted_iota(jnp.int32, (1, TQ), 1).astype(F32)
    dist = tau - sig
    dm = jnp.exp(dist * jnp.where(dist > 0, lgf, -lgb)) * jnp.where(dist == 0, 2.0, 1.0)
    return tau, dist, dm


def _ret_states_call(P, dec_f, dec_b, L, LC):
    B, T, _ = P.shape
    n = L // TQ

    def body(df_ref, db_ref, k_ref, v_ref, sf_ref, sb_ref):
        h = pl.program_id(1)
        lgf, lgb = _head_scalar(df_ref, h), _head_scalar(db_ref, h)
        tau = lax.broadcasted_iota(jnp.int32, (TQ, 1), 0).astype(F32)
        jc = lax.broadcasted_iota(jnp.int32, (LC, 1), 0).astype(F32)
        wf, wb = jnp.exp(lgf * (TQ - 1.0 - tau)), jnp.exp(lgb * tau)
        gcf, gcb = jnp.exp(lgf * float(TQ)), jnp.exp(lgb * float(TQ))
        kc, vc = k_ref[L:L + LC, :].astype(F32), v_ref[L:L + LC, :]

        def chunk_state(i, w):
            ks = pl.multiple_of(i * TQ, TQ)
            return _dot_tn((k_ref[pl.ds(ks, TQ), :].astype(F32) * w).astype(BF16), v_ref[pl.ds(ks, TQ), :])

        def fwd(i, s):
            sf_ref[i] = s
            return gcf * s + chunk_state(i, wf)

        lax.fori_loop(0, n, fwd, _dot_tn((kc * jnp.exp(lgf * (LC - 1.0 - jc))).astype(BF16), vc))

        def bwd(r, s):
            i = n - 1 - r
            sb_ref[i] = s
            return gcb * s + chunk_state(i, wb)

        lax.fori_loop(0, n, bwd, _dot_tn((kc * jnp.exp(lgb * jc)).astype(BF16), vc))

    st = pl.BlockSpec((None, None, n, RET_DK, RET_DK), lambda b, h: (b, h, 0, 0, 0))
    return pl.pallas_call(
        body, name="ret_states", grid=(B, 4),
        in_specs=[pl.BlockSpec((1, 4), lambda b, h: (0, 0)), pl.BlockSpec((1, 4), lambda b, h: (0, 0)),
                  pl.BlockSpec((None, T, 128), lambda b, h: (b, 0, 20 + h)),
                  pl.BlockSpec((None, T, 128), lambda b, h: (b, 0, 24 + h))],
        out_specs=(st, st),
        out_shape=(jax.ShapeDtypeStruct((B, 4, n, RET_DK, RET_DK), F32),) * 2,
        compiler_params=_params(("arbitrary",) * 2))(dec_f, dec_b, P, P)


def _retc_fwd_call(P, sf, sb, dec_f, dec_b, ret_norm_g, L):
    B, T, _ = P.shape
    q_spec, _, _, g_spec, dec_spec, gn_spec = _ret_specs(T)
    k_spec = pl.BlockSpec((None, TQ, 128), lambda b, h, i: (b, i, 20 + h))
    v_spec = pl.BlockSpec((None, TQ, 128), lambda b, h, i: (b, i, 24 + h))
    st_spec = pl.BlockSpec((None, None, None, RET_DK, RET_DK), lambda b, h, i: (b, h, i, 0, 0))

    def body(df_ref, db_ref, q_ref, k_ref, v_ref, g_ref, gn_ref, sf_ref, sb_ref, y_ref, o_ref):
        h = pl.program_id(1)
        lgf, lgb = _head_scalar(df_ref, h), _head_scalar(db_ref, h)
        tau, _, dm = _chunk_decay(lgf, lgb)
        q = q_ref[...]
        qf = q.astype(F32)
        acc = _dot((_dot_nt(q, k_ref[...]) * dm).astype(BF16), v_ref[...])
        acc = acc + _dot((qf * jnp.exp(lgf * (tau + 1.0))).astype(BF16), sf_ref[...].astype(BF16))
        acc = acc + _dot((qf * jnp.exp(lgb * (TQ - tau))).astype(BF16), sb_ref[...].astype(BF16))
        o_ref[...] = acc
        rn = lax.rsqrt(jnp.mean(acc * acc, axis=-1, keepdims=True) + EPS)
        g = g_ref[...].astype(F32)
        y_ref[...] = ((acc * rn * gn_ref[...]) * (g * _sigmoid(g))).astype(BF16)

    tile = pl.BlockSpec((None, TQ, 128), lambda b, h, i: (b, i, h))
    return pl.pallas_call(
        body, name="ret_fwd", grid=(B, 4, L // TQ),
        in_specs=[dec_spec, dec_spec, q_spec, k_spec, v_spec, g_spec, gn_spec, st_spec, st_spec],
        out_specs=(tile, tile),
        out_shape=(jax.ShapeDtypeStruct((B, L, 512), BF16), jax.ShapeDtypeStruct((B, L, 512), F32)),
        compiler_params=_params(("arbitrary",) * 3))(dec_f, dec_b, P, P, P, P, ret_norm_g, sf, sb)


def _retc_bwd_call(P, sf, sb, dec_f, dec_b, ret_norm_g, o_ret, dY, cos2, sin2, L, LC):
    B, T, _ = P.shape
    n = L // TQ
    C = float(TQ)
    kscale = RET_DK ** -0.5
    q_spec, k_spec, v_spec, g_spec, dec_spec, gn_spec = _ret_specs(T)
    st_spec = pl.BlockSpec((None, None, n, RET_DK, RET_DK), lambda b, h, i: (b, h, 0, 0, 0))

    def body(df_ref, db_ref, q_ref, k_ref, v_ref, g_ref, gn_ref, o_ref, dy_ref, cos_ref, sin_ref, sf_ref, sb_ref,
             dq_ref, dg_ref, dk_ref, dv_ref, dgn_ref, dlg_ref, dsf_ref, dsb_ref):
        h, i = pl.program_id(1), pl.program_id(2)
        lgf, lgb = _head_scalar(df_ref, h), _head_scalar(db_ref, h)
        tau, dist, dm = _chunk_decay(lgf, lgb)

        @pl.when(i == 0)
        def _():
            dk_ref[...] = jnp.zeros_like(dk_ref)
            dv_ref[...] = jnp.zeros_like(dv_ref)
            dgn_ref[...] = jnp.zeros_like(dgn_ref)
            dlg_ref[...] = jnp.zeros_like(dlg_ref)

        def add_lg(row, x):
            cs = jnp.sum(x, axis=0, keepdims=True)
            tot = cs[:, 0:128]
            for part in range(1, x.shape[1] // 128):
                tot = tot + cs[:, part * 128:(part + 1) * 128]
            dlg_ref[row:row + 1, :] += tot

        q = q_ref[...]
        qf = q.astype(F32)
        o = o_ref[...]
        g = g_ref[...].astype(F32)
        dy = dy_ref[...].astype(F32)
        gn = gn_ref[...]
        sg = _sigmoid(g)
        rn = lax.rsqrt(jnp.mean(o * o, axis=-1, keepdims=True) + EPS)
        nrm = o * rn
        dg_ref[...] = (dy * (nrm * gn) * (sg * (1.0 + g * (1.0 - sg)))).astype(BF16)
        dhn = dy * (g * sg)
        dgn_ref[...] += jnp.sum(dhn * nrm, axis=0, keepdims=True)
        dnrm = dhn * gn
        do = rn * (dnrm - nrm * jnp.mean(dnrm * nrm, axis=-1, keepdims=True))
        dob = do.astype(BF16)
        rows = pl.ds(pl.multiple_of(i * TQ, TQ), TQ)
        ki, vi = k_ref[rows, :], v_ref[rows, :]
        s = _dot_nt(q, ki)
        dsv = _dot_nt(dob, vi)
        dsb = (dsv * dm).astype(BF16)
        dk_ref[rows, :] += _dot_tn(dsb, q)
        dv_ref[rows, :] += _dot_tn((s * dm).astype(BF16), dob)
        xw = s * dsv * dm * jnp.abs(dist)
        fpart = jnp.where(dist > 0, xw, 0.0)
        add_lg(0, fpart)
        add_lg(1, xw - fpart)
        dq = _dot(dsb, ki)
        af, ab = jnp.exp(lgf * (tau + 1.0)), jnp.exp(lgb * (C - tau))
        qa, qb = (qf * af).astype(BF16), (qf * ab).astype(BF16)
        sfi, sbi = sf_ref[i].astype(BF16), sb_ref[i].astype(BF16)
        dq = dq + af * _dot_nt(dob, sfi) + ab * _dot_nt(dob, sbi)
        dsf_ref[i] = _dot_tn(qa, dob)
        dsb_ref[i] = _dot_tn(qb, dob)
        add_lg(0, (tau + 1.0) * (_dot(qa, sfi) * do))
        add_lg(1, (C - tau) * (_dot(qb, sbi) * do))
        cs, sn = cos_ref[rows, :], sin_ref[rows, :]
        dq_ref[...] = (dq * cs - pltpu.roll(dq, 64, 1) * sn).astype(BF16)

        @pl.when(i == n - 1)
        def _():
            jc = lax.broadcasted_iota(jnp.int32, (LC, 1), 0).astype(F32)
            crow = pl.ds(L, LC)

            def through_state(rws, w, dw, gst, row):
                kk, vv = k_ref[rws, :].astype(F32), v_ref[rws, :]
                gb = gst.astype(BF16)
                vg = _dot_nt(vv, gb)
                kw = kk * w
                dk_ref[rws, :] += w * vg
                dv_ref[rws, :] += _dot(kw.astype(BF16), gb)
                add_lg(row, dw * (kw * vg))

            def scan(lg, gc, w, dw, st_ref, dst_ref, order, row):
                def step(r, gst):
                    j = order(r)
                    through_state(pl.ds(pl.multiple_of(j * TQ, TQ), TQ), w, dw, gst, row)
                    add_lg(row, (C * gc) * (gst * st_ref[j]))
                    return dst_ref[j] + gc * gst
                return lax.fori_loop(0, n, step, jnp.zeros((RET_DK, RET_DK), F32))

            gcf, gcb = jnp.exp(lgf * C), jnp.exp(lgb * C)
            g0 = scan(lgf, gcf, jnp.exp(lgf * (C - 1.0 - tau)), C - 1.0 - tau, sf_ref, dsf_ref,
                      lambda r: n - 1 - r, 0)
            through_state(crow, jnp.exp(lgf * (LC - 1.0 - jc)), LC - 1.0 - jc, g0, 0)
            g1 = scan(lgb, gcb, jnp.exp(lgb * tau), tau, sb_ref, dsb_ref, lambda r: r, 1)
            through_state(crow, jnp.exp(lgb * jc), jc, g1, 1)
            dk = dk_ref[...]
            dk_ref[...] = (dk * cos_ref[...] - pltpu.roll(dk, 64, 1) * sin_ref[...]) * kscale

    tile = pl.BlockSpec((None, TQ, 128), lambda b, h, i: (b, i, h))
    kv_out = pl.BlockSpec((None, T, 128), lambda b, h, i: (b, 0, h))
    tab = pl.BlockSpec((T, RET_DK), lambda b, h, i: (0, 0))
    return pl.pallas_call(
        body, name="ret_bwd", grid=(B, 4, n),
        in_specs=[dec_spec, dec_spec, q_spec, k_spec, v_spec, g_spec, gn_spec, tile,
                  pl.BlockSpec((None, TQ, 128), lambda b, h, i: (b, i, 4 + h)), tab, tab, st_spec, st_spec],
        out_specs=(tile, tile, kv_out, kv_out,
                   pl.BlockSpec((None, 1, 128), lambda b, h, i: (b, 0, h)),
                   pl.BlockSpec((None, None, 8, 128), lambda b, h, i: (b, h, 0, 0))),
        out_shape=(jax.ShapeDtypeStruct((B, L, 512), BF16), jax.ShapeDtypeStruct((B, L, 512), BF16),
                   jax.ShapeDtypeStruct((B, T, 512), F32), jax.ShapeDtypeStruct((B, T, 512), F32),
                   jax.ShapeDtypeStruct((B, 1, 512), F32), jax.ShapeDtypeStruct((B, 4, 8, 128), F32)),
        scratch_shapes=[pltpu.VMEM((n, RET_DK, RET_DK), F32), pltpu.VMEM((n, RET_DK, RET_DK), F32)],
        compiler_params=_params(("arbitrary",) * 3))(
            dec_f, dec_b, P, P, P, P, ret_norm_g, o_ret, dY, cos2, sin2, sf, sb)


def _out_call(y_na, y_ret, x, target, mod, final_g, wout_f):
    B, L, _ = x.shape

    def body(yn_ref, yr_ref, x_ref, t_ref, mod_ref, gf_ref, w_ref, dy_ref, dx2_ref, dw_ref, sm_ref):
        b, i = pl.program_id(0), pl.program_id(1)

        @pl.when((b == 0) & (i == 0))
        def _():
            dw_ref[...] = jnp.zeros_like(dw_ref)
            sm_ref[...] = jnp.zeros_like(sm_ref)

        gate = mod_ref[pl.ds(b, 1), 2 * D:3 * D]
        gf = gf_ref[...]
        yn, yr = yn_ref[...], yr_ref[...]
        ylat = _dot(yn, w_ref[0:512, :]) + _dot(yr, w_ref[512:1024, :])
        x2 = x_ref[...] + gate * ylat
        r = lax.rsqrt(jnp.mean(x2 * x2, axis=-1, keepdims=True) + EPS)
        xr = x2 * r
        err = xr * gf - t_ref[...]
        sm_ref[1:2, :] += jnp.sum(err * err, axis=0, keepdims=True)
        dout = err * (1.0 / D)
        sm_ref[0:1, :] += jnp.sum(dout * xr, axis=0, keepdims=True)
        gd = dout * gf
        dx2 = r * (gd - xr * jnp.mean(gd * xr, axis=-1, keepdims=True))
        dx2_ref[...] = dx2
        sm_ref[pl.ds(2 + b, 1), :] += jnp.sum(dx2 * ylat, axis=0, keepdims=True)
        dyl = (gate * dx2).astype(BF16)
        dy_ref[:, 0:512] = _dot_nt(dyl, w_ref[0:512, :]).astype(BF16)
        dy_ref[:, 512:1024] = _dot_nt(dyl, w_ref[512:1024, :]).astype(BF16)
        dw_ref[0:512, :] += _dot_tn(yn, dyl)
        dw_ref[512:1024, :] += _dot_tn(yr, dyl)

    half = pl.BlockSpec((None, TQ, 512), lambda b, i: (b, i, 0))
    full = pl.BlockSpec((None, TQ, D), lambda b, i: (b, i, 0))
    return pl.pallas_call(
        body, name="out_proj_loss", grid=(B, L // TQ),
        in_specs=[half, half, full, full,
                  pl.BlockSpec((8, 3 * D), lambda b, i: (0, 0)),
                  pl.BlockSpec((1, D), lambda b, i: (0, 0)),
                  pl.BlockSpec((D, D), lambda b, i: (0, 0))],
        out_specs=(full, full, pl.BlockSpec((D, D), lambda b, i: (0, 0)),
                   pl.BlockSpec((8, D), lambda b, i: (0, 0))),
        out_shape=(jax.ShapeDtypeStruct((B, L, D), BF16), jax.ShapeDtypeStruct((B, L, D), F32),
                   jax.ShapeDtypeStruct((D, D), F32), jax.ShapeDtypeStruct((8, D), F32)),
        compiler_params=_params(("arbitrary",) * 2))(y_na, y_ret, x, target, mod, final_g, wout_f)


def _dh_call(dsec, win_f, x, ctx, dx2, mod, norm_g, cp_in, cp_out):
    B, L, _ = x.shape
    LC = ctx.shape[1]
    nl = L // TQ

    def body(d0, d1, d2, d3, d4, d5, d6, d7, w_ref, x_ref, ctx_ref, dx2_ref, mod_ref, g_ref, cpi_ref, cpo_ref,
             gx_ref, sm_ref, sli_ref, slo_ref, ssem, rsem, lsem):
        drefs = (d0, d1, d2, d3, d4, d5, d6, d7)
        b, t = pl.program_id(0), pl.program_id(1)
        is_lat = t < nl

        @pl.when((b == 0) & (t == 0))
        def _():
            sm_ref[...] = jnp.zeros_like(sm_ref)

        def dh_of(secs):
            acc = jnp.zeros((TQ, D), F32)
            for sec in secs:
                s, half = divmod(sec, 2)
                acc = acc + _dot_nt(drefs[sec][...].astype(BF16), w_ref[s, :, half * 512:(half + 1) * 512])
            return acc

        def norm_bwd(dh, xt, mrow):
            scale = mrow[:, D:2 * D]
            g = g_ref[...]
            rstd = lax.rsqrt(jnp.mean(xt * xt, axis=-1, keepdims=True) + EPS)
            xn = xt * rstd
            dshift = jnp.sum(dh, axis=0, keepdims=True)
            dscale = jnp.sum(dh * (xn * g), axis=0, keepdims=True)
            dhn = dh * (1.0 + scale)
            sm_ref[0:1, :] += jnp.sum(dhn * xn, axis=0, keepdims=True)
            dxn = dhn * g
            dx = rstd * (dxn - xn * jnp.mean(dxn * xn, axis=-1, keepdims=True))
            return dshift, dscale, dx

        @pl.when(is_lat)
        def _():
            dshift, dscale, dx = norm_bwd(dh_of(range(8)), x_ref[...], mod_ref[pl.ds(b, 1), :])
            sm_ref[pl.ds(3 + b, 1), :] += dshift
            sm_ref[pl.ds(3 + B + b, 1), :] += dscale
            gx_ref[...] = dx2_ref[...] + dx

        @pl.when(jnp.logical_not(is_lat))
        def _():
            dshift, dscale, _ = norm_bwd(dh_of((1, 2, 5, 6)), ctx_ref[...], mod_ref[B:B + 1, :])
            sm_ref[1:2, :] += dshift
            sm_ref[2:3, :] += dscale

        mx, my, mc = _mesh_pos()
        s = 2 * mx + my
        cps, sls = (cpi_ref, cpo_ref), (sli_ref, slo_ref)
        own = [pltpu.make_async_copy(cps[a].at[s], sls[a].at[s], lsem.at[a]) for a in range(2)]
        sends, recvs, k = [], [], 0
        for px, py in _other_chips(mx, my):
            ps = 2 * px + py
            for a in range(2):
                sends.append(_remote(cps[a].at[ps], sls[a].at[s], ssem, rsem, k, (px, py, mc)))
                recvs.append(_remote(cps[a].at[s], sls[a].at[ps], ssem, rsem, k, (px, py, mc)))
                k += 1

        @pl.when((b == 0) & (t == 0))
        def _():
            for cp in own + sends:
                cp.start()

        @pl.when((b == B - 1) & (t == nl))
        def _():
            _finish(own, sends, recvs)

    lat = lambda b, t: (b, jnp.minimum(t, nl - 1), 0)
    tok = lambda b, t: (b, t, 0)
    sec_specs = [pl.BlockSpec((None, TQ, 512), lat if sec in (0, 3, 4, 7) else tok) for sec in range(8)]
    return pl.pallas_call(
        body, name="dh_norm_bwd", grid=(B, nl + 1),
        in_specs=sec_specs + [
            pl.BlockSpec((N_SHARD, D, D), lambda b, t: (0, 0, 0)),
            pl.BlockSpec((None, TQ, D), lat),
            pl.BlockSpec((None, LC, D), lambda b, t: (b, 0, 0)),
            pl.BlockSpec((None, TQ, D), lat),
            pl.BlockSpec((8, 3 * D), lambda b, t: (0, 0)),
            pl.BlockSpec((1, D), lambda b, t: (0, 0)), ANY, ANY],
        out_specs=(pl.BlockSpec((None, TQ, D), lat), pl.BlockSpec((8, D), lambda b, t: (0, 0)), ANY, ANY),
        out_shape=(jax.ShapeDtypeStruct((B, L, D), F32), jax.ShapeDtypeStruct((8, D), F32),
                   jax.ShapeDtypeStruct(cp_in.shape, cp_in.dtype), jax.ShapeDtypeStruct(cp_out.shape, cp_out.dtype)),
        scratch_shapes=[pltpu.SemaphoreType.DMA((6,)), pltpu.SemaphoreType.DMA((6,)),
                        pltpu.SemaphoreType.DMA((2,))],
        compiler_params=_params(("arbitrary",) * 2))(*dsec, win_f, x, ctx, dx2, mod, norm_g, cp_in, cp_out)


def _dw_call(dsec, h, L):
    B, T, _ = h.shape
    nl = L // TQ

    def body(d0, d1, d2, d3, d4, d5, d6, d7, h_ref, dw_ref, acc_ref):
        drefs = (d0, d1, d2, d3, d4, d5, d6, d7)
        b, t = pl.program_id(0), pl.program_id(1)

        @pl.when((b == 0) & (t == 0))
        def _():
            acc_ref[...] = jnp.zeros_like(acc_ref)

        hb = h_ref[...]

        def add(secs):
            for sec in secs:
                s, half = divmod(sec, 2)
                acc_ref[s, :, half * 512:(half + 1) * 512] += _dot_tn(hb, drefs[sec][...].astype(BF16))

        @pl.when(t < nl)
        def _():
            add(range(8))

        @pl.when(t >= nl)
        def _():
            add((1, 2, 5, 6))

        @pl.when((b == B - 1) & (t == nl))
        def _():
            dw_ref[...] = acc_ref[...].astype(BF16)

    lat = lambda b, t: (b, jnp.minimum(t, nl - 1), 0)
    tok = lambda b, t: (b, t, 0)
    sec_specs = [pl.BlockSpec((None, TQ, 512), lat if sec in (0, 3, 4, 7) else tok) for sec in range(8)]
    return pl.pallas_call(
        body, name="dw_in", grid=(B, nl + 1),
        in_specs=sec_specs + [pl.BlockSpec((None, TQ, D), tok)],
        out_specs=pl.BlockSpec((N_SHARD, D, D), lambda b, t: (0, 0, 0)),
        out_shape=jax.ShapeDtypeStruct((N_SHARD, D, D), BF16),
        scratch_shapes=[pltpu.VMEM((N_SHARD, D, D), F32)],
        compiler_params=_params(("arbitrary",) * 2, vmem_mb=56))(*dsec, h)


def _mesh_pos():
    return lax.axis_index("x"), lax.axis_index("y"), lax.axis_index("c")


def _flip(v, f):
    return 1 - v if f else v


def _remote(src, dst, ssem, rsem, k, peer):
    return pltpu.make_async_remote_copy(src_ref=src, dst_ref=dst, send_sem=ssem.at[k], recv_sem=rsem.at[k],
                                        device_id=peer, device_id_type=MESH)


def _other_chips(x, y):
    return [(_flip(x, fx), _flip(y, fy)) for fx, fy in ((1, 0), (0, 1), (1, 1))]


D2D_STREAMS = 8


def _row_chunks(src, dst, ssem, rsem, k, peer, rows, lead=None):
    step = rows // D2D_STREAMS
    out = []
    for r in range(D2D_STREAMS):
        idx = (pl.ds(r * step, step),) if lead is None else (lead, pl.ds(r * step, step))
        out.append(_remote(src.at[idx], dst.at[idx], ssem, rsem, k, peer))
    return out


def _all_to_all_small(src, dst_all, ssem, rsem, k0, x, y, cc):
    me = 4 * x + 2 * y + cc
    sends, recvs = [], []
    for f in range(1, N_DEV):
        px, py, pc = _flip(x, f & 4), _flip(y, f & 2), _flip(cc, f & 1)
        sends.append(_remote(src, dst_all.at[me], ssem, rsem, k0 + f - 1, (px, py, pc)))
        recvs.append(_remote(src, dst_all.at[4 * px + 2 * py + pc], ssem, rsem, k0 + f - 1, (px, py, pc)))
    return sends, recvs


def _finish(local, sends, recvs):
    for cp in recvs:
        cp.wait_recv()
    for cp in sends:
        cp.wait_send()
    for cp in local:
        cp.wait()


def _gather_call(win_b, wout_b, wada_b, c, rpb_flat):
    arrs = (win_b, wout_b, wada_b)
    hrs = [a.shape[0] // 2 for a in arrs]

    def body(win, wout, wada, c_ref, r_ref, win_f, wout_f, wada_f, c_all, bias_ref, et_ref, ssem, rsem, lsem):
        x, y, cc = _mesh_pos()
        s, me = 2 * x + y, 4 * x + 2 * y + cc
        sib = (x, y, 1 - cc)
        srcs, dsts = (win, wout, wada), (win_f, wout_f, wada_f)

        def half(a, shard, hc):
            return dsts[a].at[shard, pl.ds(hc * hrs[a], hrs[a])]

        local = [pltpu.make_async_copy(srcs[a], dsts[a].at[s], lsem.at[a]) for a in range(3)]
        local.append(pltpu.make_async_copy(c_ref, c_all.at[me], lsem.at[3]))
        ici_send, ici_recv, fwd_send, fwd_recv, fwd_chunks, k = [], [], [], [], [], 0
        for px, py in _other_chips(x, y):
            ps = 2 * px + py
            for a in range(3):
                mine = srcs[a].at[pl.ds(cc * hrs[a], hrs[a])]
                ici_send.append(_remote(mine, half(a, s, cc), ssem, rsem, k, (px, py, cc)))
                ici_recv.append(_remote(mine, half(a, ps, cc), ssem, rsem, k, (px, py, cc)))
                fwd_send.append(_remote(half(a, ps, cc), half(a, ps, cc), ssem, rsem, 9 + k, sib))
                fwd_recv.append(_remote(half(a, ps, 1 - cc), half(a, ps, 1 - cc), ssem, rsem, 9 + k, sib))
                fwd_chunks.append(_row_chunks(half(a, ps, cc), half(a, ps, cc), ssem, rsem, 9 + k, sib, hrs[a]))
                k += 1
        c_send, c_recv = _all_to_all_small(c_ref, c_all, ssem, rsem, 18, x, y, cc)
        for cp in local + ici_send + c_send:
            cp.start()
        _bias_body(r_ref, bias_ref, et_ref)
        for got, chunks in zip(ici_recv, fwd_chunks):
            got.wait_recv()
            for cp in chunks:
                cp.start()
        _finish(local, ici_send + fwd_send + c_send, fwd_recv + c_recv)

    return pl.pallas_call(
        body, name="weight_gather",
        in_specs=[pl.BlockSpec(memory_space=pltpu.VMEM)] * 4 + [pl.BlockSpec(memory_space=pltpu.SMEM)],
        out_specs=(pl.BlockSpec(memory_space=pltpu.VMEM),) * 5,
        out_shape=tuple(jax.ShapeDtypeStruct((N_SHARD,) + a.shape, a.dtype) for a in arrs)
        + (jax.ShapeDtypeStruct((N_DEV,) + c.shape, c.dtype),
           jax.ShapeDtypeStruct((rpb_flat.shape[0], 3, TQ, KW), F32)),
        scratch_shapes=[pltpu.VMEM((15, GRID_W, GRID_W), F32),
                        pltpu.SemaphoreType.DMA((25,)), pltpu.SemaphoreType.DMA((25,)),
                        pltpu.SemaphoreType.DMA((4,))],
        compiler_params=pltpu.CompilerParams(vmem_limit_bytes=56 << 20))(win_b, wout_b, wada_b, c, rpb_flat)


VROWS = 32


def _grad_halves_call(dwin_b, dwout_b, dbias, dlg):
    arrs = (dwin_b, dwout_b)
    hrs = [a.shape[1] // 2 for a in arrs]

    def body(din, dout, db_ref, dlg_ref, cp_in, cp_out, drpb_ref, dlgo_ref, got_in, got_out, p_ref, ssem, rsem):
        x, y, cc = _mesh_pos()
        sib = (x, y, 1 - cc)
        srcs, gots, cps = (din, dout), (got_in, got_out), (cp_in, cp_out)
        halves = [_remote(srcs[a].at[:, pl.ds((1 - cc) * hrs[a], hrs[a])], gots[a], ssem, rsem, a, sib)
                  for a in range(2)]
        for cp in halves:
            cp.start()
        _small_reduce_body(db_ref, dlg_ref, drpb_ref, dlgo_ref, p_ref)
        for cp in halves:
            cp.wait_recv()
        for a in range(2):
            for j in range(N_SHARD):
                def add(i, carry, a=a, j=j):
                    r = pl.multiple_of(i * VROWS, VROWS)
                    mine = srcs[a][j, pl.ds(pl.multiple_of(cc * hrs[a] + r, VROWS), VROWS), :].astype(F32)
                    cps[a][j, pl.ds(r, VROWS), :] = (
                        mine + gots[a][j, pl.ds(r, VROWS), :].astype(F32)).astype(BF16)
                    return carry
                lax.fori_loop(0, hrs[a] // VROWS, add, 0)
        for cp in halves:
            cp.wait_send()

    vmem = pl.BlockSpec(memory_space=pltpu.VMEM)
    half_shapes = [(N_SHARD, hrs[a], arrs[a].shape[2]) for a in range(2)]
    return pl.pallas_call(
        body, name="grad_halves",
        in_specs=[vmem] * 4, out_specs=(vmem,) * 4,
        out_shape=(jax.ShapeDtypeStruct(half_shapes[0], BF16), jax.ShapeDtypeStruct(half_shapes[1], BF16),
                   jax.ShapeDtypeStruct((dbias.shape[0], 16, 32), F32), jax.ShapeDtypeStruct((32, 128), F32)),
        scratch_shapes=[pltpu.VMEM(half_shapes[0], BF16), pltpu.VMEM(half_shapes[1], BF16),
                        pltpu.VMEM((32, GRID_W), F32),
                        pltpu.SemaphoreType.DMA((2,)), pltpu.SemaphoreType.DMA((2,))],
        compiler_params=pltpu.CompilerParams(vmem_limit_bytes=56 << 20))(dwin_b, dwout_b, dbias, dlg)


def _grad_finish_call(sl_in, sl_out, small):
    arrs = (sl_in, sl_out)

    def body(sin, sout, sm, gin, gout, sm_all, h_in, h_out, ssem, rsem, lsem):
        x, y, cc = _mesh_pos()
        me = 4 * x + 2 * y + cc
        sib = (x, y, 1 - cc)
        sls, hs, gs = (sin, sout), (h_in, h_out), (gin, gout)
        sm_send, sm_recv = _all_to_all_small(sm, sm_all, ssem, rsem, 2, x, y, cc)
        sm_own = pltpu.make_async_copy(sm, sm_all.at[me], lsem.at[0])
        for cp in sm_send + [sm_own]:
            cp.start()
        for a in range(2):
            def total(i, carry, a=a):
                rows = pl.ds(pl.multiple_of(i * VROWS, VROWS), VROWS)
                sl = sls[a]
                hs[a][rows, :] = ((sl[0, rows, :].astype(F32) + sl[1, rows, :].astype(F32))
                                  + sl[2, rows, :].astype(F32)) + sl[3, rows, :].astype(F32)
                return carry
            lax.fori_loop(0, arrs[a].shape[1] // VROWS, total, 0)
        mine = [pltpu.make_async_copy(hs[a], gs[a].at[cc], lsem.at[1 + a]) for a in range(2)]
        back = [_remote(hs[a], gs[a].at[cc], ssem, rsem, a, sib) for a in range(2)]
        back_recv = [_remote(hs[a], gs[a].at[1 - cc], ssem, rsem, a, sib) for a in range(2)]
        for cp in mine + back:
            cp.start()
        _finish(mine + [sm_own], back + sm_send, back_recv + sm_recv)

    vmem = pl.BlockSpec(memory_space=pltpu.VMEM)
    return pl.pallas_call(
        body, name="grad_finish",
        in_specs=[vmem] * 3, out_specs=(vmem,) * 3,
        out_shape=(jax.ShapeDtypeStruct((2,) + sl_in.shape[1:], F32),
                   jax.ShapeDtypeStruct((2,) + sl_out.shape[1:], F32),
                   jax.ShapeDtypeStruct((N_DEV,) + small.shape, F32)),
        scratch_shapes=[pltpu.VMEM(sl_in.shape[1:], F32), pltpu.VMEM(sl_out.shape[1:], F32),
                        pltpu.SemaphoreType.DMA((9,)), pltpu.SemaphoreType.DMA((9,)),
                        pltpu.SemaphoreType.DMA((3,))],
        compiler_params=pltpu.CompilerParams(vmem_limit_bytes=48 << 20))(sl_in, sl_out, small)


def _adamw(w, g, m, v):
    m = ADAM_B1 * m + (1.0 - ADAM_B1) * g
    v = ADAM_B2 * v + (1.0 - ADAM_B2) * (g * g)
    m_hat = m / (1.0 - ADAM_B1 ** ADAM_STEP)
    v_hat = v / (1.0 - ADAM_B2 ** ADAM_STEP)
    return -ADAM_LR * (m_hat / (jnp.sqrt(v_hat) + ADAM_EPS) + ADAM_WD * w), m, v


def _adam_call(w, m, v, g, name):
    R, C = w.shape
    tr = 256

    def body(w_ref, m_ref, v_ref, g_ref, d_ref, mo_ref, vo_ref):
        d_ref[...], mo_ref[...], vo_ref[...] = _adamw(w_ref[...], g_ref[...], m_ref[...], v_ref[...])

    spec = pl.BlockSpec((tr, C), lambda i: (i, 0))
    return pl.pallas_call(
        body, name=name, grid=(R // tr,), in_specs=[spec] * 4,
        out_specs=(spec,) * 3, out_shape=(jax.ShapeDtypeStruct((R, C), F32),) * 3,
        compiler_params=_params(("arbitrary",)))(w, m, v, g)


R_GF, R_NG, R_LOSS, R_RNG, R_LGF, R_LGB, R_SHIFT, R_SCALE, R_GATE, R_SHIFT_C, R_SCALE_C, R_RNG2, R_RPB = (
    0, 1, 2, 3, 4, 5, 6, 8, 10, 12, 13, 14, 16)
W_GF, W_NG, W_CCTX, W_RNG, W_DF, W_DB, W_BADA, W_RPB = 0, 1, 2, 3, 4, 5, 6, 9


def _small_final_call(sm_all, c_t, c_ctx, wada_f, wada, m_ada, v_ada, wsm, msm, vsm, B):
    ws = wada.shape[1]
    NB = N_DEV * B

    def body(sm_ref, ct_ref, cctx_ref, wf_ref, wa_ref, ma_ref, va_ref, w_ref, m_ref, v_ref,
             g_ref, d_ref, mo_ref, vo_ref, ga_ref, da_ref, mao_ref, vao_ref, loss_ref, dmod_ref):
        x, y, _ = _mesh_pos()
        s = 2 * x + y
        tot = sm_ref[0]
        for dv in range(1, N_DEV):
            tot = tot + sm_ref[dv]
        w = w_ref[...]
        for dv in range(N_DEV):
            for b in range(B):
                r = dv * B + b
                for part, row in enumerate((R_SHIFT, R_SCALE, R_GATE)):
                    dmod_ref[r:r + 1, part * D:(part + 1) * D] = sm_ref[dv, row + b:row + b + 1, :]
        dmod_ref[NB:NB + 1, 0:D] = tot[R_SHIFT_C:R_SHIFT_C + 1, :]
        dmod_ref[NB:NB + 1, D:2 * D] = tot[R_SCALE_C:R_SCALE_C + 1, :]
        dmod_ref[NB:NB + 1, 2 * D:3 * D] = jnp.zeros((1, D), F32)
        dmod_ref[NB + 1:, :] = jnp.zeros((dmod_ref.shape[0] - NB - 1, 3 * D), F32)
        dmod = dmod_ref[...]
        cc = cctx_ref[...]
        scc = _sigmoid(cc)
        ct = ct_ref[...]
        act_t = ct * _sigmoid(ct)
        dmc = dmod[NB:NB + 1, :].astype(BF16)
        dact = jnp.zeros((1, D), F32)
        for sh in range(N_SHARD):
            dact = dact + _dot_nt(dmc[:, sh * ws:(sh + 1) * ws], wf_ref[sh])
        g = jnp.zeros((16, D), F32)
        rows = lax.broadcasted_iota(jnp.int32, (16, D), 0)

        def put(g, row, val):
            return jnp.where(rows == row, val, g)

        g = put(g, W_GF, tot[R_GF:R_GF + 1, :])
        g = put(g, W_NG, tot[R_NG:R_NG + 1, :])
        g = put(g, W_CCTX, dact * (scc * (1.0 + cc * (1.0 - scc))))
        g = put(g, W_RNG, tot[R_RNG:R_RNG + 1, :] + tot[R_RNG2:R_RNG2 + 1, :])
        g = put(g, W_DF, tot[R_LGF:R_LGF + 1, :] * (-jnp.exp(w[W_DF:W_DF + 1, :])))
        g = put(g, W_DB, tot[R_LGB:R_LGB + 1, :] * (-jnp.exp(w[W_DB:W_DB + 1, :])))
        db = jnp.sum(dmod, axis=0, keepdims=True)
        for part in range(3):
            g = put(g, W_BADA + part, db[:, part * D:(part + 1) * D])
        for part in range(4):
            g = put(g, W_RPB + part, tot[R_RPB + part:R_RPB + part + 1, :])
        g_ref[...] = g
        d_ref[...], mo_ref[...], vo_ref[...] = _adamw(w, g, m_ref[...], v_ref[...])
        loss_ref[...] = jnp.broadcast_to(
            (0.5 / D) * jnp.sum(tot[R_LOSS:R_LOSS + 1, :], axis=1, keepdims=True), (8, 128))
        for sh in range(N_SHARD):
            @pl.when(s == sh)
            def _():
                ga = jnp.dot(act_t, dmod[:, sh * ws:(sh + 1) * ws], precision=HIGHEST,
                             preferred_element_type=F32)
                ga_ref[...] = ga
                da_ref[...], mao_ref[...], vao_ref[...] = _adamw(wa_ref[...], ga, ma_ref[...], va_ref[...])

    sh_small = jax.ShapeDtypeStruct((16, D), F32)
    sh_ada = jax.ShapeDtypeStruct(wada.shape, F32)
    return pl.pallas_call(
        body, name="small_final",
        out_shape=(sh_small,) * 4 + (sh_ada,) * 4 + (jax.ShapeDtypeStruct((8, 128), F32),),
        scratch_shapes=[pltpu.VMEM((NB + 8, 3 * D), F32)],
        compiler_params=_params(vmem_mb=56))(
            sm_all, c_t, c_ctx, wada_f, wada, m_ada, v_ada, wsm, msm, vsm)


def _local_step(x, c, ctx, c_ctx, norm_g, wada_f, b_ada, win_f, bias, dec_f, dec_b, ret_norm_g,
                wout_f, final_g, target):
    B, L, _ = x.shape
    LC = ctx.shape[1]
    assert B == 2
    cos2, sin2 = _rope_tables(L, LC)
    c8 = jnp.concatenate([c, c_ctx[None, :], jnp.zeros((8 - B - 1, D), F32)], axis=0)
    mod = _mod_call(c8, wada_f, b_ada)
    P, h = _inproj_call(x, ctx, mod, norm_g, win_f, cos2, sin2)
    y_na, o_na = _na_fwd_call(P, bias, L, LC)
    sf, sb = _ret_states_call(P, dec_f, dec_b, L, LC)
    y_ret, o_ret = _retc_fwd_call(P, sf, sb, dec_f, dec_b, ret_norm_g, L)
    dY, dx2, dwout_p, sm_out = _out_call(y_na, y_ret, x, target, mod, final_g, wout_f.reshape(D, D))
    dnq, dng, dnk, dnv, dbias = _na_bwd_call(P, bias, dY, o_na, L, LC)
    drq, drg, drk, drv, dgn, dlg = _retc_bwd_call(P, sf, sb, dec_f, dec_b, ret_norm_g, o_ret, dY, cos2, sin2, L, LC)
    dsec = (dnq, dnk, dnv, dng, drq, drk, drv, drg)
    dwin_b = _dw_call(dsec, h, L)
    cp_in, cp_out, drpb, dlg_sum = _grad_halves_call(
        dwin_b, dwout_p.astype(BF16).reshape(N_SHARD, D // N_SHARD, D), dbias, dlg)
    grad_x, sm_dh, sl_in, sl_out = _dh_call(dsec, win_f, x, ctx, dx2, mod, norm_g, cp_in, cp_out)
    z = jnp.zeros((1, D), F32)
    pad = lambda v: jnp.pad(v.reshape(1, -1), ((0, 0), (0, D - v.size)))
    dlg_sum = dlg_sum.reshape(4, 8, 128)
    rpb_rows = jnp.pad(drpb[:, :15, :31].reshape(-1), (0, 4 * D - drpb.shape[0] * 465)).reshape(4, D)
    small = jnp.concatenate([
        sm_out[0:1], sm_dh[0:1], sm_out[1:2], pad(dgn[0]), pad(dlg_sum[:, 0, 0]), pad(dlg_sum[:, 1, 0]),
        sm_dh[3:5], sm_dh[5:7], sm_out[2:4], sm_dh[1:2], sm_dh[2:3], pad(dgn[1]), z, rpb_rows,
        jnp.zeros((SM_ROWS - 20, D), F32)], axis=0)
    return grad_x, sl_in, sl_out, small


def kernel(x, c, ctx, c_ctx, norm_g, w_ada, b_ada, w_in, na_rpb, ret_decay_fwd, ret_decay_bwd, ret_norm_g, w_out, final_norm_g, loss_target, m_c_ctx, m_norm_g, m_w_ada, m_b_ada, m_w_in, m_na_rpb, m_ret_decay_fwd, m_ret_decay_bwd, m_ret_norm_g, m_w_out, m_final_norm_g, v_c_ctx, v_norm_g, v_w_ada, v_b_ada, v_w_in, v_na_rpb, v_ret_decay_fwd, v_ret_decay_bwd, v_ret_norm_g, v_w_out, v_final_norm_g):
    B = x.shape[0]
    win_f, wout_f, wada_f, c_all, bias = _gather_call(
        w_in[0].astype(BF16), w_out[0].astype(BF16), w_ada[0].astype(BF16), c,
        na_rpb[0].reshape(na_rpb.shape[1], -1))
    grad_x, sl_in, sl_out, small = _local_step(
        x, c, ctx, c_ctx, norm_g, wada_f, b_ada, win_f, bias, ret_decay_fwd, ret_decay_bwd,
        ret_norm_g, wout_f, final_norm_g.reshape(1, D), loss_target)
    gin, gout, sm_all = _grad_finish_call(sl_in, sl_out, small)
    g_win, g_wout = gin.reshape(w_in.shape[1:]), gout.reshape(w_out.shape[1:])
    d_win, nm_win, nv_win = _adam_call(w_in[0], m_w_in[0], v_w_in[0], g_win, "adam_w_in")
    d_wout, nm_wout, nv_wout = _adam_call(w_out[0], m_w_out[0], v_w_out[0], g_wout, "adam_w_out")

    def pack(gf, ng, cc, rng, df, db, bada, rpb):
        pad = lambda v: jnp.pad(v.reshape(1, -1), ((0, 0), (0, D - v.size)))
        return jnp.concatenate([
            gf.reshape(1, D), ng.reshape(1, D), cc.reshape(1, D), pad(rng), pad(df), pad(db),
            bada.reshape(3, D), jnp.pad(rpb.reshape(-1), (0, 4 * D - rpb.size)).reshape(4, D),
            jnp.zeros((3, D), F32)], axis=0)

    wsm = pack(final_norm_g, norm_g, c_ctx, ret_norm_g, ret_decay_fwd, ret_decay_bwd, b_ada, na_rpb)
    msm = pack(m_final_norm_g, m_norm_g, m_c_ctx, m_ret_norm_g, m_ret_decay_fwd, m_ret_decay_bwd, m_b_ada, m_na_rpb)
    vsm = pack(v_final_norm_g, v_norm_g, v_c_ctx, v_ret_norm_g, v_ret_decay_fwd, v_ret_decay_bwd, v_b_ada, v_na_rpb)
    c_t = jnp.concatenate([c_all.reshape(N_DEV * B, D), c_ctx.reshape(1, D), jnp.zeros((7, D), F32)], axis=0).T
    outs = _small_final_call(sm_all, c_t, c_ctx.reshape(1, D), wada_f,
                             w_ada[0], m_w_ada[0], v_w_ada[0], wsm, msm, vsm, B)
    smalls, adas, loss = outs[0:4], outs[4:8], outs[8][0, 0]

    def unpack(p):
        rw = ret_norm_g.shape[1]
        return dict(
            final_norm_g=p[W_GF], norm_g=p[W_NG:W_NG + 1], c_ctx=p[W_CCTX], ret_norm_g=p[W_RNG:W_RNG + 1, :rw],
            ret_decay_fwd=p[W_DF:W_DF + 1, :4], ret_decay_bwd=p[W_DB:W_DB + 1, :4],
            b_ada=p[W_BADA:W_BADA + 3].reshape(1, 3 * D),
            na_rpb=p[W_RPB:W_RPB + 4].reshape(-1)[:na_rpb.size].reshape(na_rpb.shape))

    res = []
    for p, ada, win_o, wout_o in zip(smalls, adas, (g_win, d_win, nm_win, nv_win),
                                     (g_wout, d_wout, nm_wout, nv_wout)):
        u = unpack(p)
        res.append([u["c_ctx"], u["norm_g"], ada[None], u["b_ada"], win_o[None], u["na_rpb"],
                    u["ret_decay_fwd"], u["ret_decay_bwd"], u["ret_norm_g"], wout_o[None], u["final_norm_g"]])
    return (loss, grad_x, *res[0], *res[1], *res[2], *res[3])
```

```python
import numpy as np
import jax
import jax.numpy as jnp
from jax import lax
from jax.experimental import pallas as pl
from jax.experimental.pallas import tpu as pltpu

F32 = jnp.float32
BF16 = jnp.bfloat16
HIGHEST = lax.Precision.HIGHEST

D = 1024
GRID_W = 64
NA_DH = 64
RET_DK = 128
ROPE_BASE = 10000.0
EPS = 1e-6
NEG = -1e30
TQ = 256
KW = 12 * GRID_W
N_SHARD = 4
N_DEV = 8
SM_ROWS = 24

ADAM_LR = 0.001
ADAM_B1 = 0.9
ADAM_B2 = 0.999
ADAM_EPS = 1e-08
ADAM_WD = 0.01
ADAM_STEP = 10

MESH = pl.DeviceIdType.MESH
ANY = pl.BlockSpec(memory_space=pl.ANY)


def _params(sem=None, vmem_mb=48):
    return pltpu.CompilerParams(dimension_semantics=sem, vmem_limit_bytes=vmem_mb << 20)


def _dot(a, b):
    return jnp.dot(a, b, preferred_element_type=F32)


def _dot_nt(a, b):
    return lax.dot_general(a, b, (((1,), (1,)), ((), ())), preferred_element_type=F32)


def _dot_tn(a, b):
    return lax.dot_general(a, b, (((0,), (0,)), ((), ())), preferred_element_type=F32)


def _sigmoid(x):
    return 1.0 / (1.0 + jnp.exp(-x))


def _rope_tables(L, LC):
    half = RET_DK // 2
    nf = half // 2
    t = np.arange(L)
    row = (t // GRID_W).astype(np.float32)
    col = (t % GRID_W).astype(np.float32)
    inv = (np.float32(ROPE_BASE) ** (-np.arange(nf, dtype=np.float32) / np.float32(nf))).astype(np.float32)
    ang = np.concatenate([row[:, None] * inv, col[:, None] * inv], axis=-1).astype(np.float32)
    cos, sin = np.cos(ang).astype(np.float32), np.sin(ang).astype(np.float32)
    cos2 = np.concatenate([cos, cos], axis=-1)
    sin2 = np.concatenate([-sin, sin], axis=-1)
    cos2 = np.concatenate([cos2, np.ones((LC, RET_DK), np.float32)], axis=0)
    sin2 = np.concatenate([sin2, np.zeros((LC, RET_DK), np.float32)], axis=0)
    return jnp.asarray(cos2), jnp.asarray(sin2)


def _mod_call(c8, wada_f, b_ada):
    ws = wada_f.shape[2]

    def body(c_ref, w_ref, b_ref, o_ref):
        a = c_ref[...]
        a = (a * _sigmoid(a)).astype(BF16)
        for s in range(N_SHARD):
            o_ref[:, s * ws:(s + 1) * ws] = _dot(a, w_ref[s]) + b_ref[:, s * ws:(s + 1) * ws]

    return pl.pallas_call(
        body, name="ada_mod", out_shape=jax.ShapeDtypeStruct((8, 3 * D), F32),
        compiler_params=_params())(c8, wada_f, b_ada)


def _dc_masks():
    cq = lax.broadcasted_iota(jnp.int32, (GRID_W, GRID_W), 0)
    ck = lax.broadcasted_iota(jnp.int32, (GRID_W, GRID_W), 1)
    dc = jnp.clip(ck - cq + 15, 0, 30)
    c0 = jnp.clip(cq - 8, 0, GRID_W - 16)
    col_ok = (ck >= c0) & (ck < c0 + 16)
    return dc, col_ok


def _bias_blocks():
    out = []
    for typ, delta in enumerate((4, 0, -4)):
        for rq in range(4):
            for rkk in range(12):
                dr = rkk + delta - rq - 4
                if typ == 0:
                    ok = -rq <= dr <= 7 - rq
                elif typ == 1:
                    ok = -4 <= dr <= 3
                else:
                    ok = -4 - rq <= dr <= 3 - rq
                out.append((typ, rq, rkk, dr if ok else None))
    return out


def _bias_body(r_ref, bias_ref, et_ref):
    dc, col_ok = _dc_masks()
    masks = [(dc == j).astype(F32) for j in range(31)]

    def per_h(h, carry):
        for dr in range(15):
            t = jnp.zeros((GRID_W, GRID_W), F32)
            for j in range(31):
                t = t + masks[j] * r_ref[h, dr * 31 + j]
            et_ref[dr] = jnp.where(col_ok, t, NEG)
        neg = jnp.full((GRID_W, GRID_W), NEG, F32)
        for typ, rq, rkk, dr in _bias_blocks():
            blk = neg if dr is None else et_ref[dr + 7]
            bias_ref[h, typ, rq * 64:(rq + 1) * 64, rkk * 64:(rkk + 1) * 64] = blk
        return carry

    lax.fori_loop(0, bias_ref.shape[0], per_h, 0)


def _small_reduce_body(db_ref, dlg_ref, drpb_ref, dlgo_ref, p_ref):
    dc, _ = _dc_masks()
    masks = [(dc == j).astype(F32) for j in range(31)]
    ones = jnp.ones((8, GRID_W), F32)
    p_ref[...] = jnp.zeros_like(p_ref)
    drpb_ref[...] = jnp.zeros_like(drpb_ref)

    def per_h(h, carry):
        acc = {}
        for typ, rq, rkk, dr in _bias_blocks():
            if dr is None:
                continue
            blk = db_ref[h, typ, rq * 64:(rq + 1) * 64, rkk * 64:(rkk + 1) * 64]
            acc[dr] = blk if dr not in acc else acc[dr] + blk
        for dr in range(-7, 8):
            t = acc[dr]
            for j in range(31):
                p_ref[j:j + 1, :] = jnp.sum(t * masks[j], axis=0, keepdims=True)
            red = lax.dot_general(ones, p_ref[...], (((1,), (1,)), ((), ())),
                                  precision=HIGHEST, preferred_element_type=F32)
            drpb_ref[h, dr + 7:dr + 8, :] = red[0:1, :]
        return carry

    lax.fori_loop(0, db_ref.shape[0], per_h, 0)
    x = dlg_ref[0]
    for b in range(1, dlg_ref.shape[0]):
        x = x + dlg_ref[b]
    x = x.reshape(4 * 8, x.shape[-1])
    dlgo_ref[...] = jnp.dot(x, jnp.ones((x.shape[-1], 128), F32), precision=HIGHEST,
                            preferred_element_type=F32)


def _inproj_call(x, ctx, mod, norm_g, win_f, cos2, sin2):
    B, L, _ = x.shape
    LC = ctx.shape[1]
    T = L + LC
    nl = L // TQ
    assert LC == TQ and L % TQ == 0
    kscale = RET_DK ** -0.5

    def body(x_ref, ctx_ref, mod_ref, g_ref, w_ref, cos_ref, sin_ref, p_ref, h_ref):
        b = pl.program_id(0)
        t = pl.program_id(1)
        is_lat = t < nl
        xt = jnp.where(is_lat, x_ref[...], ctx_ref[...])
        mrow = mod_ref[pl.ds(jnp.where(is_lat, b, B), 1), :]
        shift, scale = mrow[:, 0:D], mrow[:, D:2 * D]
        rstd = lax.rsqrt(jnp.mean(xt * xt, axis=-1, keepdims=True) + EPS)
        hb = ((xt * rstd * g_ref[...]) * (1.0 + scale) + shift).astype(BF16)
        h_ref[...] = hb
        cs, sn = cos_ref[...], sin_ref[...]
        for sec in range(8):
            s, half = divmod(sec, 2)
            acc = _dot(hb, w_ref[s, :, half * 512:(half + 1) * 512])
            if sec == 0:
                acc = acc * (NA_DH ** -0.5)
            if sec in (4, 5):
                for j in range(4):
                    a = acc[:, j * 128:(j + 1) * 128]
                    r = a * cs + pltpu.roll(a, 64, 1) * sn
                    if sec == 5:
                        r = r * kscale
                    p_ref[:, sec * 512 + j * 128:sec * 512 + (j + 1) * 128] = r.astype(BF16)
            else:
                p_ref[:, sec * 512:(sec + 1) * 512] = acc.astype(BF16)

    return pl.pallas_call(
        body, name="in_proj", grid=(B, T // TQ),
        in_specs=[
            pl.BlockSpec((None, TQ, D), lambda b, t: (b, jnp.minimum(t, nl - 1), 0)),
            pl.BlockSpec((None, TQ, D), lambda b, t: (b, 0, 0)),
            pl.BlockSpec((8, 3 * D), lambda b, t: (0, 0)),
            pl.BlockSpec((1, D), lambda b, t: (0, 0)),
            pl.BlockSpec((N_SHARD, D, D), lambda b, t: (0, 0, 0)),
            pl.BlockSpec((TQ, RET_DK), lambda b, t: (t, 0)),
            pl.BlockSpec((TQ, RET_DK), lambda b, t: (t, 0)),
        ],
        out_specs=(pl.BlockSpec((None, TQ, 4 * D), lambda b, t: (b, t, 0)),
                   pl.BlockSpec((None, TQ, D), lambda b, t: (b, t, 0))),
        out_shape=(jax.ShapeDtypeStruct((B, T, 4 * D), BF16), jax.ShapeDtypeStruct((B, T, D), BF16)),
        compiler_params=_params(("arbitrary", "arbitrary")))(x, ctx, mod, norm_g, win_f, cos2, sin2)


def _na_specs(L, T, rows):
    nm = rows // 4
    q_spec = pl.BlockSpec((None, TQ, 128), lambda hp, b, m: (b, m, hp))
    k_spec = pl.BlockSpec((None, T, 128), lambda hp, b, m: (b, 0, 4 + hp))
    v_spec = pl.BlockSpec((None, T, 128), lambda hp, b, m: (b, 0, 8 + hp))
    g_spec = pl.BlockSpec((None, TQ, 128), lambda hp, b, m: (b, m, 12 + hp))
    bias_spec = pl.BlockSpec((2, 3, TQ, KW), lambda hp, b, m: (hp, 0, 0, 0))
    return nm, q_spec, k_spec, v_spec, g_spec, bias_spec


def _na_tile(m, nm, rows):
    typ = jnp.where(m == 0, 0, jnp.where(m == nm - 1, 2, 1))
    start = pl.multiple_of(jnp.clip(4 * m - 4, 0, rows - 12) * GRID_W, TQ)
    return typ, start


def _na_fwd_call(P, bias, L, LC):
    B, T, _ = P.shape
    rows = L // GRID_W
    nm, q_spec, k_spec, v_spec, g_spec, bias_spec = _na_specs(L, T, rows)

    def body(q_ref, k_ref, v_ref, g_ref, bias_ref, y_ref, o_ref):
        typ, start = _na_tile(pl.program_id(2), nm, rows)
        for hh in range(2):
            ln = slice(hh * NA_DH, (hh + 1) * NA_DH)
            q = q_ref[:, ln]
            kw, vw = k_ref[pl.ds(start, KW), ln], v_ref[pl.ds(start, KW), ln]
            kc, vc = k_ref[L:L + LC, ln], v_ref[L:L + LC, ln]
            s1 = _dot_nt(q, kw) + bias_ref[hh, typ]
            s2 = _dot_nt(q, kc)
            mx = jnp.maximum(jnp.max(s1, axis=-1, keepdims=True), jnp.max(s2, axis=-1, keepdims=True))
            p1, p2 = jnp.exp(s1 - mx), jnp.exp(s2 - mx)
            inv = 1.0 / (jnp.sum(p1, axis=-1, keepdims=True) + jnp.sum(p2, axis=-1, keepdims=True))
            o = (_dot(p1.astype(BF16), vw) + _dot(p2.astype(BF16), vc)) * inv
            g = g_ref[:, ln].astype(F32)
            o_ref[:, ln] = o.astype(BF16)
            y_ref[:, ln] = (o * (g * _sigmoid(g))).astype(BF16)

    tile = pl.BlockSpec((None, TQ, 128), lambda hp, b, m: (b, m, hp))
    return pl.pallas_call(
        body, name="na_fwd", grid=(4, B, nm),
        in_specs=[q_spec, k_spec, v_spec, g_spec, bias_spec],
        out_specs=(tile, tile),
        out_shape=(jax.ShapeDtypeStruct((B, L, 512), BF16),) * 2,
        compiler_params=_params(("arbitrary",) * 3))(P, P, P, P, bias)


def _na_bwd_call(P, bias, dY, o_na, L, LC):
    B, T, _ = P.shape
    rows = L // GRID_W
    nm, q_spec, k_spec, v_spec, g_spec, bias_spec = _na_specs(L, T, rows)
    scale = NA_DH ** -0.5

    RB = 32

    def body(q_ref, k_ref, v_ref, g_ref, bias_ref, dy_ref, o_ref, dq_ref, dg_ref, dk_ref, dv_ref, db_ref,
             s1_ref, s2_ref, dp1_ref, dp2_ref, p1_ref, p2_ref, ds1_ref, ds2_ref):
        b, m = pl.program_id(1), pl.program_id(2)
        typ, start = _na_tile(m, nm, rows)

        @pl.when(m == 0)
        def _():
            dk_ref[...] = jnp.zeros_like(dk_ref)
            dv_ref[...] = jnp.zeros_like(dv_ref)

        @pl.when((m == 0) & (b == 0))
        def _():
            db_ref[...] = jnp.zeros_like(db_ref)

        for hh in range(2):
            ln = slice(hh * NA_DH, (hh + 1) * NA_DH)
            q = q_ref[:, ln]
            kw, vw = k_ref[pl.ds(start, KW), ln], v_ref[pl.ds(start, KW), ln]
            kc, vc = k_ref[L:L + LC, ln], v_ref[L:L + LC, ln]
            g = g_ref[:, ln].astype(F32)
            sg = _sigmoid(g)
            dy = dy_ref[:, ln].astype(F32)
            do = (dy * (g * sg)).astype(BF16)
            s1_ref[...] = _dot_nt(q, kw)
            s2_ref[...] = _dot_nt(q, kc)
            dp1_ref[...] = _dot_nt(do, vw)
            dp2_ref[...] = _dot_nt(do, vc)

            def rows_pass(r, carry, hh=hh):
                rw = pl.ds(pl.multiple_of(r * RB, RB), RB)
                a = s1_ref[rw, :] + bias_ref[hh, typ, rw, :]
                c = s2_ref[rw, :]
                mx = jnp.maximum(jnp.max(a, axis=-1, keepdims=True), jnp.max(c, axis=-1, keepdims=True))
                e1, e2 = jnp.exp(a - mx), jnp.exp(c - mx)
                inv = 1.0 / (jnp.sum(e1, axis=-1, keepdims=True) + jnp.sum(e2, axis=-1, keepdims=True))
                p1, p2 = e1 * inv, e2 * inv
                p1_ref[rw, :] = p1.astype(BF16)
                p2_ref[rw, :] = p2.astype(BF16)
                dp1, dp2 = dp1_ref[rw, :], dp2_ref[rw, :]
                delta = jnp.sum(p1 * dp1, axis=-1, keepdims=True) + jnp.sum(p2 * dp2, axis=-1, keepdims=True)
                ds1 = p1 * (dp1 - delta)
                db_ref[hh, typ, rw, :] += ds1
                ds1_ref[rw, :] = ds1.astype(BF16)
                ds2_ref[rw, :] = (p2 * (dp2 - delta)).astype(BF16)
                return carry

            lax.fori_loop(0, TQ // RB, rows_pass, 0, unroll=True)
            p1b, p2b, ds1b, ds2b = p1_ref[...], p2_ref[...], ds1_ref[...], ds2_ref[...]
            dg_ref[:, ln] = (dy * o_ref[:, ln].astype(F32) * (sg * (1.0 + g * (1.0 - sg)))).astype(BF16)
            dq_ref[:, ln] = ((_dot(ds1b, kw) + _dot(ds2b, kc)) * scale).astype(BF16)
            dk_ref[pl.ds(start, KW), ln] += _dot_tn(ds1b, q)
            dv_ref[pl.ds(start, KW), ln] += _dot_tn(p1b, do)
            dk_ref[L:L + LC, ln] += _dot_tn(ds2b, q)
            dv_ref[L:L + LC, ln] += _dot_tn(p2b, do)

    tile = pl.BlockSpec((None, TQ, 128), lambda hp, b, m: (b, m, hp))
    kv_out = pl.BlockSpec((None, T, 128), lambda hp, b, m: (b, 0, hp))
    wide, narrow = (TQ, KW), (TQ, LC)
    return pl.pallas_call(
        body, name="na_bwd", grid=(4, B, nm),
        in_specs=[q_spec, k_spec, v_spec, g_spec, bias_spec, tile, tile],
        out_specs=(tile, tile, kv_out, kv_out, bias_spec),
        out_shape=(jax.ShapeDtypeStruct((B, L, 512), BF16), jax.ShapeDtypeStruct((B, L, 512), BF16),
                   jax.ShapeDtypeStruct((B, T, 512), F32), jax.ShapeDtypeStruct((B, T, 512), F32),
                   jax.ShapeDtypeStruct(bias.shape, F32)),
        scratch_shapes=[pltpu.VMEM(wide, F32), pltpu.VMEM(narrow, F32), pltpu.VMEM(wide, F32), pltpu.VMEM(narrow, F32),
                        pltpu.VMEM(wide, BF16), pltpu.VMEM(narrow, BF16), pltpu.VMEM(wide, BF16),
                        pltpu.VMEM(narrow, BF16)],
        compiler_params=_params(("arbitrary",) * 3))(P, P, P, P, bias, dY, o_na)


def _head_scalar(dec_ref, h):
    lane = lax.broadcasted_iota(jnp.int32, dec_ref.shape, 1)
    return -jnp.sum(jnp.where(lane == h, jnp.exp(dec_ref[...]), 0.0), axis=1, keepdims=True)


def _ret_specs(T):
    q_spec = pl.BlockSpec((None, TQ, 128), lambda b, h, i: (b, i, 16 + h))
    k_spec = pl.BlockSpec((None, T, 128), lambda b, h, i: (b, 0, 20 + h))
    v_spec = pl.BlockSpec((None, T, 128), lambda b, h, i: (b, 0, 24 + h))
    g_spec = pl.BlockSpec((None, TQ, 128), lambda b, h, i: (b, i, 28 + h))
    dec_spec = pl.BlockSpec((1, 4), lambda b, h, i: (0, 0))
    gn_spec = pl.BlockSpec((1, 128), lambda b, h, i: (0, h))
    return q_spec, k_spec, v_spec, g_spec, dec_spec, gn_spec


def _chunk_decay(lgf, lgb):
    tau = lax.broadcasted_iota(jnp.int32, (TQ, 1), 0).astype(F32)
    sig = lax.broadcasted_iota(jnp.int32, (1, TQ), 1).astype(F32)
    dist = tau - sig
    dm = jnp.exp(dist * jnp.where(dist > 0, lgf, -lgb)) * jnp.where(dist == 0, 2.0, 1.0)
    return tau, dist, dm


def _ret_states_call(P, dec_f, dec_b, L, LC):
    B, T, _ = P.shape
    n = L // TQ

    def body(df_ref, db_ref, k_ref, v_ref, sf_ref, sb_ref):
        h = pl.program_id(1)
        lgf, lgb = _head_scalar(df_ref, h), _head_scalar(db_ref, h)
        tau = lax.broadcasted_iota(jnp.int32, (TQ, 1), 0).astype(F32)
        jc = lax.broadcasted_iota(jnp.int32, (LC, 1), 0).astype(F32)
        wf, wb = jnp.exp(lgf * (TQ - 1.0 - tau)), jnp.exp(lgb * tau)
        gcf, gcb = jnp.exp(lgf * float(TQ)), jnp.exp(lgb * float(TQ))
        kc, vc = k_ref[L:L + LC, :].astype(F32), v_ref[L:L + LC, :]

        def chunk_state(i, w):
            ks = pl.multiple_of(i * TQ, TQ)
            return _dot_tn((k_ref[pl.ds(ks, TQ), :].astype(F32) * w).astype(BF16), v_ref[pl.ds(ks, TQ), :])

        def fwd(i, s):
            sf_ref[i] = s
            return gcf * s + chunk_state(i, wf)

        lax.fori_loop(0, n, fwd, _dot_tn((kc * jnp.exp(lgf * (LC - 1.0 - jc))).astype(BF16), vc), unroll=True)

        def bwd(r, s):
            i = n - 1 - r
            sb_ref[i] = s
            return gcb * s + chunk_state(i, wb)

        lax.fori_loop(0, n, bwd, _dot_tn((kc * jnp.exp(lgb * jc)).astype(BF16), vc), unroll=True)

    st = pl.BlockSpec((None, None, n, RET_DK, RET_DK), lambda b, h: (b, h, 0, 0, 0))
    return pl.pallas_call(
        body, name="ret_states", grid=(B, 4),
        in_specs=[pl.BlockSpec((1, 4), lambda b, h: (0, 0)), pl.BlockSpec((1, 4), lambda b, h: (0, 0)),
                  pl.BlockSpec((None, T, 128), lambda b, h: (b, 0, 20 + h)),
                  pl.BlockSpec((None, T, 128), lambda b, h: (b, 0, 24 + h))],
        out_specs=(st, st),
        out_shape=(jax.ShapeDtypeStruct((B, 4, n, RET_DK, RET_DK), F32),) * 2,
        compiler_params=_params(("arbitrary",) * 2))(dec_f, dec_b, P, P)


def _retc_fwd_call(P, sf, sb, dec_f, dec_b, ret_norm_g, L):
    B, T, _ = P.shape
    q_spec, _, _, g_spec, dec_spec, gn_spec = _ret_specs(T)
    k_spec = pl.BlockSpec((None, TQ, 128), lambda b, h, i: (b, i, 20 + h))
    v_spec = pl.BlockSpec((None, TQ, 128), lambda b, h, i: (b, i, 24 + h))
    st_spec = pl.BlockSpec((None, None, None, RET_DK, RET_DK), lambda b, h, i: (b, h, i, 0, 0))

    def body(df_ref, db_ref, q_ref, k_ref, v_ref, g_ref, gn_ref, sf_ref, sb_ref, y_ref, o_ref):
        h = pl.program_id(1)
        lgf, lgb = _head_scalar(df_ref, h), _head_scalar(db_ref, h)
        tau, _, dm = _chunk_decay(lgf, lgb)
        q = q_ref[...]
        qf = q.astype(F32)
        acc = _dot((_dot_nt(q, k_ref[...]) * dm).astype(BF16), v_ref[...])
        acc = acc + _dot((qf * jnp.exp(lgf * (tau + 1.0))).astype(BF16), sf_ref[...].astype(BF16))
        acc = acc + _dot((qf * jnp.exp(lgb * (TQ - tau))).astype(BF16), sb_ref[...].astype(BF16))
        o_ref[...] = acc
        rn = lax.rsqrt(jnp.mean(acc * acc, axis=-1, keepdims=True) + EPS)
        g = g_ref[...].astype(F32)
        y_ref[...] = ((acc * rn * gn_ref[...]) * (g * _sigmoid(g))).astype(BF16)

    tile = pl.BlockSpec((None, TQ, 128), lambda b, h, i: (b, i, h))
    return pl.pallas_call(
        body, name="ret_fwd", grid=(B, 4, L // TQ),
        in_specs=[dec_spec, dec_spec, q_spec, k_spec, v_spec, g_spec, gn_spec, st_spec, st_spec],
        out_specs=(tile, tile),
        out_shape=(jax.ShapeDtypeStruct((B, L, 512), BF16), jax.ShapeDtypeStruct((B, L, 512), F32)),
        compiler_params=_params(("arbitrary",) * 3))(dec_f, dec_b, P, P, P, P, ret_norm_g, sf, sb)


def _retc_bwd_call(P, sf, sb, dec_f, dec_b, ret_norm_g, o_ret, dY, cos2, sin2, L, LC):
    B, T, _ = P.shape
    n = L // TQ
    C = float(TQ)
    kscale = RET_DK ** -0.5
    q_spec, k_spec, v_spec, g_spec, dec_spec, gn_spec = _ret_specs(T)
    st_spec = pl.BlockSpec((None, None, n, RET_DK, RET_DK), lambda b, h, i: (b, h, 0, 0, 0))

    def body(df_ref, db_ref, q_ref, k_ref, v_ref, g_ref, gn_ref, o_ref, dy_ref, cos_ref, sin_ref, sf_ref, sb_ref,
             dq_ref, dg_ref, dk_ref, dv_ref, dgn_ref, dlg_ref, dsf_ref, dsb_ref):
        h, i = pl.program_id(1), pl.program_id(2)
        lgf, lgb = _head_scalar(df_ref, h), _head_scalar(db_ref, h)
        tau, dist, dm = _chunk_decay(lgf, lgb)

        @pl.when(i == 0)
        def _():
            dk_ref[...] = jnp.zeros_like(dk_ref)
            dv_ref[...] = jnp.zeros_like(dv_ref)
            dgn_ref[...] = jnp.zeros_like(dgn_ref)
            dlg_ref[...] = jnp.zeros_like(dlg_ref)

        def add_lg(row, x):
            cs = jnp.sum(x, axis=0, keepdims=True)
            tot = cs[:, 0:128]
            for part in range(1, x.shape[1] // 128):
                tot = tot + cs[:, part * 128:(part + 1) * 128]
            dlg_ref[row:row + 1, :] += tot

        q = q_ref[...]
        qf = q.astype(F32)
        o = o_ref[...]
        g = g_ref[...].astype(F32)
        dy = dy_ref[...].astype(F32)
        gn = gn_ref[...]
        sg = _sigmoid(g)
        rn = lax.rsqrt(jnp.mean(o * o, axis=-1, keepdims=True) + EPS)
        nrm = o * rn
        dg_ref[...] = (dy * (nrm * gn) * (sg * (1.0 + g * (1.0 - sg)))).astype(BF16)
        dhn = dy * (g * sg)
        dgn_ref[...] += jnp.sum(dhn * nrm, axis=0, keepdims=True)
        dnrm = dhn * gn
        do = rn * (dnrm - nrm * jnp.mean(dnrm * nrm, axis=-1, keepdims=True))
        dob = do.astype(BF16)
        rows = pl.ds(pl.multiple_of(i * TQ, TQ), TQ)
        ki, vi = k_ref[rows, :], v_ref[rows, :]
        s = _dot_nt(q, ki)
        dsv = _dot_nt(dob, vi)
        dsb = (dsv * dm).astype(BF16)
        dk_ref[rows, :] += _dot_tn(dsb, q)
        dv_ref[rows, :] += _dot_tn((s * dm).astype(BF16), dob)
        xw = s * dsv * dm * jnp.abs(dist)
        fpart = jnp.where(dist > 0, xw, 0.0)
        add_lg(0, fpart)
        add_lg(1, xw - fpart)
        dq = _dot(dsb, ki)
        af, ab = jnp.exp(lgf * (tau + 1.0)), jnp.exp(lgb * (C - tau))
        qa, qb = (qf * af).astype(BF16), (qf * ab).astype(BF16)
        sfi, sbi = sf_ref[i].astype(BF16), sb_ref[i].astype(BF16)
        dq = dq + af * _dot_nt(dob, sfi) + ab * _dot_nt(dob, sbi)
        dsf_ref[i] = _dot_tn(qa, dob)
        dsb_ref[i] = _dot_tn(qb, dob)
        add_lg(0, (tau + 1.0) * (_dot(qa, sfi) * do))
        add_lg(1, (C - tau) * (_dot(qb, sbi) * do))
        cs, sn = cos_ref[rows, :], sin_ref[rows, :]
        dq_ref[...] = (dq * cs - pltpu.roll(dq, 64, 1) * sn).astype(BF16)

        @pl.when(i == n - 1)
        def _():
            jc = lax.broadcasted_iota(jnp.int32, (LC, 1), 0).astype(F32)
            crow = pl.ds(L, LC)

            def through_state(rws, w, dw, gst, row):
                kk, vv = k_ref[rws, :].astype(F32), v_ref[rws, :]
                gb = gst.astype(BF16)
                vg = _dot_nt(vv, gb)
                kw = kk * w
                dk_ref[rws, :] += w * vg
                dv_ref[rws, :] += _dot(kw.astype(BF16), gb)
                add_lg(row, dw * (kw * vg))

            def scan(lg, gc, w, dw, st_ref, dst_ref, order, row):
                def step(r, gst):
                    j = order(r)
                    through_state(pl.ds(pl.multiple_of(j * TQ, TQ), TQ), w, dw, gst, row)
                    add_lg(row, (C * gc) * (gst * st_ref[j]))
                    return dst_ref[j] + gc * gst
                return lax.fori_loop(0, n, step, jnp.zeros((RET_DK, RET_DK), F32), unroll=True)

            gcf, gcb = jnp.exp(lgf * C), jnp.exp(lgb * C)
            g0 = scan(lgf, gcf, jnp.exp(lgf * (C - 1.0 - tau)), C - 1.0 - tau, sf_ref, dsf_ref,
                      lambda r: n - 1 - r, 0)
            through_state(crow, jnp.exp(lgf * (LC - 1.0 - jc)), LC - 1.0 - jc, g0, 0)
            g1 = scan(lgb, gcb, jnp.exp(lgb * tau), tau, sb_ref, dsb_ref, lambda r: r, 1)
            through_state(crow, jnp.exp(lgb * jc), jc, g1, 1)
            dk = dk_ref[...]
            dk_ref[...] = (dk * cos_ref[...] - pltpu.roll(dk, 64, 1) * sin_ref[...]) * kscale

    tile = pl.BlockSpec((None, TQ, 128), lambda b, h, i: (b, i, h))
    kv_out = pl.BlockSpec((None, T, 128), lambda b, h, i: (b, 0, h))
    tab = pl.BlockSpec((T, RET_DK), lambda b, h, i: (0, 0))
    return pl.pallas_call(
        body, name="ret_bwd", grid=(B, 4, n),
        in_specs=[dec_spec, dec_spec, q_spec, k_spec, v_spec, g_spec, gn_spec, tile,
                  pl.BlockSpec((None, TQ, 128), lambda b, h, i: (b, i, 4 + h)), tab, tab, st_spec, st_spec],
        out_specs=(tile, tile, kv_out, kv_out,
                   pl.BlockSpec((None, 1, 128), lambda b, h, i: (b, 0, h)),
                   pl.BlockSpec((None, None, 8, 128), lambda b, h, i: (b, h, 0, 0))),
        out_shape=(jax.ShapeDtypeStruct((B, L, 512), BF16), jax.ShapeDtypeStruct((B, L, 512), BF16),
                   jax.ShapeDtypeStruct((B, T, 512), F32), jax.ShapeDtypeStruct((B, T, 512), F32),
                   jax.ShapeDtypeStruct((B, 1, 512), F32), jax.ShapeDtypeStruct((B, 4, 8, 128), F32)),
        scratch_shapes=[pltpu.VMEM((n, RET_DK, RET_DK), F32), pltpu.VMEM((n, RET_DK, RET_DK), F32)],
        compiler_params=_params(("arbitrary",) * 3))(
            dec_f, dec_b, P, P, P, P, ret_norm_g, o_ret, dY, cos2, sin2, sf, sb)


def _out_call(y_na, y_ret, x, target, mod, final_g, wout_f):
    B, L, _ = x.shape

    def body(yn_ref, yr_ref, x_ref, t_ref, mod_ref, gf_ref, w_ref, dy_ref, dx2_ref, dw_ref, sm_ref):
        b, i = pl.program_id(0), pl.program_id(1)

        @pl.when((b == 0) & (i == 0))
        def _():
            dw_ref[...] = jnp.zeros_like(dw_ref)
            sm_ref[...] = jnp.zeros_like(sm_ref)

        gate = mod_ref[pl.ds(b, 1), 2 * D:3 * D]
        gf = gf_ref[...]
        yn, yr = yn_ref[...], yr_ref[...]
        ylat = _dot(yn, w_ref[0:512, :]) + _dot(yr, w_ref[512:1024, :])
        x2 = x_ref[...] + gate * ylat
        r = lax.rsqrt(jnp.mean(x2 * x2, axis=-1, keepdims=True) + EPS)
        xr = x2 * r
        err = xr * gf - t_ref[...]
        sm_ref[1:2, :] += jnp.sum(err * err, axis=0, keepdims=True)
        dout = err * (1.0 / D)
        sm_ref[0:1, :] += jnp.sum(dout * xr, axis=0, keepdims=True)
        gd = dout * gf
        dx2 = r * (gd - xr * jnp.mean(gd * xr, axis=-1, keepdims=True))
        dx2_ref[...] = dx2
        sm_ref[pl.ds(2 + b, 1), :] += jnp.sum(dx2 * ylat, axis=0, keepdims=True)
        dyl = (gate * dx2).astype(BF16)
        dy_ref[:, 0:512] = _dot_nt(dyl, w_ref[0:512, :]).astype(BF16)
        dy_ref[:, 512:1024] = _dot_nt(dyl, w_ref[512:1024, :]).astype(BF16)
        dw_ref[0:512, :] += _dot_tn(yn, dyl)
        dw_ref[512:1024, :] += _dot_tn(yr, dyl)

    half = pl.BlockSpec((None, TQ, 512), lambda b, i: (b, i, 0))
    full = pl.BlockSpec((None, TQ, D), lambda b, i: (b, i, 0))
    return pl.pallas_call(
        body, name="out_proj_loss", grid=(B, L // TQ),
        in_specs=[half, half, full, full,
                  pl.BlockSpec((8, 3 * D), lambda b, i: (0, 0)),
                  pl.BlockSpec((1, D), lambda b, i: (0, 0)),
                  pl.BlockSpec((D, D), lambda b, i: (0, 0))],
        out_specs=(full, full, pl.BlockSpec((D, D), lambda b, i: (0, 0)),
                   pl.BlockSpec((8, D), lambda b, i: (0, 0))),
        out_shape=(jax.ShapeDtypeStruct((B, L, D), BF16), jax.ShapeDtypeStruct((B, L, D), F32),
                   jax.ShapeDtypeStruct((D, D), F32), jax.ShapeDtypeStruct((8, D), F32)),
        compiler_params=_params(("arbitrary",) * 2))(y_na, y_ret, x, target, mod, final_g, wout_f)


def _dh_call(dsec, win_f, x, ctx, dx2, mod, norm_g, cp_in, cp_out):
    B, L, _ = x.shape
    LC = ctx.shape[1]
    nl = L // TQ

    def body(d0, d1, d2, d3, d4, d5, d6, d7, w_ref, x_ref, ctx_ref, dx2_ref, mod_ref, g_ref, cpi_ref, cpo_ref,
             gx_ref, sm_ref, sli_ref, slo_ref, ssem, rsem, lsem):
        drefs = (d0, d1, d2, d3, d4, d5, d6, d7)
        b, t = pl.program_id(0), pl.program_id(1)
        is_lat = t < nl

        @pl.when((b == 0) & (t == 0))
        def _():
            sm_ref[...] = jnp.zeros_like(sm_ref)

        def dh_of(secs):
            acc = jnp.zeros((TQ, D), F32)
            for sec in secs:
                s, half = divmod(sec, 2)
                acc = acc + _dot_nt(drefs[sec][...].astype(BF16), w_ref[s, :, half * 512:(half + 1) * 512])
            return acc

        def norm_bwd(dh, xt, mrow):
            scale = mrow[:, D:2 * D]
            g = g_ref[...]
            rstd = lax.rsqrt(jnp.mean(xt * xt, axis=-1, keepdims=True) + EPS)
            xn = xt * rstd
            dshift = jnp.sum(dh, axis=0, keepdims=True)
            dscale = jnp.sum(dh * (xn * g), axis=0, keepdims=True)
            dhn = dh * (1.0 + scale)
            sm_ref[0:1, :] += jnp.sum(dhn * xn, axis=0, keepdims=True)
            dxn = dhn * g
            dx = rstd * (dxn - xn * jnp.mean(dxn * xn, axis=-1, keepdims=True))
            return dshift, dscale, dx

        @pl.when(is_lat)
        def _():
            dshift, dscale, dx = norm_bwd(dh_of(range(8)), x_ref[...], mod_ref[pl.ds(b, 1), :])
            sm_ref[pl.ds(3 + b, 1), :] += dshift
            sm_ref[pl.ds(3 + B + b, 1), :] += dscale
            gx_ref[...] = dx2_ref[...] + dx

        @pl.when(jnp.logical_not(is_lat))
        def _():
            dshift, dscale, _ = norm_bwd(dh_of((1, 2, 5, 6)), ctx_ref[...], mod_ref[B:B + 1, :])
            sm_ref[1:2, :] += dshift
            sm_ref[2:3, :] += dscale

        mx, my, mc = _mesh_pos()
        s = 2 * mx + my
        cps, sls = (cpi_ref, cpo_ref), (sli_ref, slo_ref)
        own = [pltpu.make_async_copy(cps[a].at[s], sls[a].at[s], lsem.at[a]) for a in range(2)]
        sends, recvs, k = [], [], 0
        for px, py in _other_chips(mx, my):
            ps = 2 * px + py
            for a in range(2):
                sends.append(_remote(cps[a].at[ps], sls[a].at[s], ssem, rsem, k, (px, py, mc)))
                recvs.append(_remote(cps[a].at[s], sls[a].at[ps], ssem, rsem, k, (px, py, mc)))
                k += 1

        @pl.when((b == 0) & (t == 0))
        def _():
            for cp in own + sends:
                cp.start()

        @pl.when((b == B - 1) & (t == nl))
        def _():
            _finish(own, sends, recvs)

    lat = lambda b, t: (b, jnp.minimum(t, nl - 1), 0)
    tok = lambda b, t: (b, t, 0)
    sec_specs = [pl.BlockSpec((None, TQ, 512), lat if sec in (0, 3, 4, 7) else tok) for sec in range(8)]
    return pl.pallas_call(
        body, name="dh_norm_bwd", grid=(B, nl + 1),
        in_specs=sec_specs + [
            pl.BlockSpec((N_SHARD, D, D), lambda b, t: (0, 0, 0)),
            pl.BlockSpec((None, TQ, D), lat),
            pl.BlockSpec((None, LC, D), lambda b, t: (b, 0, 0)),
            pl.BlockSpec((None, TQ, D), lat),
            pl.BlockSpec((8, 3 * D), lambda b, t: (0, 0)),
            pl.BlockSpec((1, D), lambda b, t: (0, 0)), ANY, ANY],
        out_specs=(pl.BlockSpec((None, TQ, D), lat), pl.BlockSpec((8, D), lambda b, t: (0, 0)), ANY, ANY),
        out_shape=(jax.ShapeDtypeStruct((B, L, D), F32), jax.ShapeDtypeStruct((8, D), F32),
                   jax.ShapeDtypeStruct(cp_in.shape, cp_in.dtype), jax.ShapeDtypeStruct(cp_out.shape, cp_out.dtype)),
        scratch_shapes=[pltpu.SemaphoreType.DMA((6,)), pltpu.SemaphoreType.DMA((6,)),
                        pltpu.SemaphoreType.DMA((2,))],
        compiler_params=_params(("arbitrary",) * 2))(*dsec, win_f, x, ctx, dx2, mod, norm_g, cp_in, cp_out)


def _dw_call(dsec, h, L):
    B, T, _ = h.shape
    nl = L // TQ

    def body(d0, d1, d2, d3, d4, d5, d6, d7, h_ref, dw_ref, acc_ref):
        drefs = (d0, d1, d2, d3, d4, d5, d6, d7)
        b, t = pl.program_id(0), pl.program_id(1)

        @pl.when((b == 0) & (t == 0))
        def _():
            acc_ref[...] = jnp.zeros_like(acc_ref)

        hb = h_ref[...]

        def add(secs):
            for sec in secs:
                s, half = divmod(sec, 2)
                acc_ref[s, :, half * 512:(half + 1) * 512] += _dot_tn(hb, drefs[sec][...].astype(BF16))

        @pl.when(t < nl)
        def _():
            add(range(8))

        @pl.when(t >= nl)
        def _():
            add((1, 2, 5, 6))

        @pl.when((b == B - 1) & (t == nl))
        def _():
            dw_ref[...] = acc_ref[...].astype(BF16)

    lat = lambda b, t: (b, jnp.minimum(t, nl - 1), 0)
    tok = lambda b, t: (b, t, 0)
    sec_specs = [pl.BlockSpec((None, TQ, 512), lat if sec in (0, 3, 4, 7) else tok) for sec in range(8)]
    return pl.pallas_call(
        body, name="dw_in", grid=(B, nl + 1),
        in_specs=sec_specs + [pl.BlockSpec((None, TQ, D), tok)],
        out_specs=pl.BlockSpec((N_SHARD, D, D), lambda b, t: (0, 0, 0)),
        out_shape=jax.ShapeDtypeStruct((N_SHARD, D, D), BF16),
        scratch_shapes=[pltpu.VMEM((N_SHARD, D, D), F32)],
        compiler_params=_params(("arbitrary",) * 2, vmem_mb=56))(*dsec, h)


def _mesh_pos():
    return lax.axis_index("x"), lax.axis_index("y"), lax.axis_index("c")


def _flip(v, f):
    return 1 - v if f else v


def _remote(src, dst, ssem, rsem, k, peer):
    return pltpu.make_async_remote_copy(src_ref=src, dst_ref=dst, send_sem=ssem.at[k], recv_sem=rsem.at[k],
                                        device_id=peer, device_id_type=MESH)


def _other_chips(x, y):
    return [(_flip(x, fx), _flip(y, fy)) for fx, fy in ((1, 0), (0, 1), (1, 1))]


D2D_STREAMS = 8


def _row_chunks(src, dst, ssem, rsem, k, peer, rows, lead=None):
    step = rows // D2D_STREAMS
    out = []
    for r in range(D2D_STREAMS):
        idx = (pl.ds(r * step, step),) if lead is None else (lead, pl.ds(r * step, step))
        out.append(_remote(src.at[idx], dst.at[idx], ssem, rsem, k, peer))
    return out


def _all_to_all_small(src, dst_all, ssem, rsem, k0, x, y, cc):
    me = 4 * x + 2 * y + cc
    sends, recvs = [], []
    for f in range(1, N_DEV):
        px, py, pc = _flip(x, f & 4), _flip(y, f & 2), _flip(cc, f & 1)
        sends.append(_remote(src, dst_all.at[me], ssem, rsem, k0 + f - 1, (px, py, pc)))
        recvs.append(_remote(src, dst_all.at[4 * px + 2 * py + pc], ssem, rsem, k0 + f - 1, (px, py, pc)))
    return sends, recvs


def _finish(local, sends, recvs):
    for cp in recvs:
        cp.wait_recv()
    for cp in sends:
        cp.wait_send()
    for cp in local:
        cp.wait()


def _gather_call(win_b, wout_b, wada_b, c, rpb_flat):
    arrs = (win_b, wout_b, wada_b)
    hrs = [a.shape[0] // 2 for a in arrs]

    def body(win, wout, wada, c_ref, r_ref, win_f, wout_f, wada_f, c_all, bias_ref, et_ref, ssem, rsem, lsem):
        x, y, cc = _mesh_pos()
        s, me = 2 * x + y, 4 * x + 2 * y + cc
        sib = (x, y, 1 - cc)
        srcs, dsts = (win, wout, wada), (win_f, wout_f, wada_f)

        def half(a, shard, hc):
            return dsts[a].at[shard, pl.ds(hc * hrs[a], hrs[a])]

        local = [pltpu.make_async_copy(srcs[a], dsts[a].at[s], lsem.at[a]) for a in range(3)]
        local.append(pltpu.make_async_copy(c_ref, c_all.at[me], lsem.at[3]))
        ici_send, ici_recv, fwd_send, fwd_recv, fwd_chunks, k = [], [], [], [], [], 0
        for px, py in _other_chips(x, y):
            ps = 2 * px + py
            for a in range(3):
                mine = srcs[a].at[pl.ds(cc * hrs[a], hrs[a])]
                ici_send.append(_remote(mine, half(a, s, cc), ssem, rsem, k, (px, py, cc)))
                ici_recv.append(_remote(mine, half(a, ps, cc), ssem, rsem, k, (px, py, cc)))
                fwd_send.append(_remote(half(a, ps, cc), half(a, ps, cc), ssem, rsem, 9 + k, sib))
                fwd_recv.append(_remote(half(a, ps, 1 - cc), half(a, ps, 1 - cc), ssem, rsem, 9 + k, sib))
                fwd_chunks.append(_row_chunks(half(a, ps, cc), half(a, ps, cc), ssem, rsem, 9 + k, sib, hrs[a]))
                k += 1
        c_send, c_recv = _all_to_all_small(c_ref, c_all, ssem, rsem, 18, x, y, cc)
        for cp in local + ici_send + c_send:
            cp.start()
        _bias_body(r_ref, bias_ref, et_ref)
        for got, chunks in zip(ici_recv, fwd_chunks):
            got.wait_recv()
            for cp in chunks:
                cp.start()
        _finish(local, ici_send + fwd_send + c_send, fwd_recv + c_recv)

    return pl.pallas_call(
        body, name="weight_gather",
        in_specs=[pl.BlockSpec(memory_space=pltpu.VMEM)] * 4 + [pl.BlockSpec(memory_space=pltpu.SMEM)],
        out_specs=(pl.BlockSpec(memory_space=pltpu.VMEM),) * 5,
        out_shape=tuple(jax.ShapeDtypeStruct((N_SHARD,) + a.shape, a.dtype) for a in arrs)
        + (jax.ShapeDtypeStruct((N_DEV,) + c.shape, c.dtype),
           jax.ShapeDtypeStruct((rpb_flat.shape[0], 3, TQ, KW), F32)),
        scratch_shapes=[pltpu.VMEM((15, GRID_W, GRID_W), F32),
                        pltpu.SemaphoreType.DMA((25,)), pltpu.SemaphoreType.DMA((25,)),
                        pltpu.SemaphoreType.DMA((4,))],
        compiler_params=pltpu.CompilerParams(vmem_limit_bytes=56 << 20))(win_b, wout_b, wada_b, c, rpb_flat)


VROWS = 32


def _grad_halves_call(dwin_b, dwout_b, dbias, dlg):
    arrs = (dwin_b, dwout_b)
    hrs = [a.shape[1] // 2 for a in arrs]

    def body(din, dout, db_ref, dlg_ref, cp_in, cp_out, drpb_ref, dlgo_ref, got_in, got_out, p_ref, ssem, rsem):
        x, y, cc = _mesh_pos()
        sib = (x, y, 1 - cc)
        srcs, gots, cps = (din, dout), (got_in, got_out), (cp_in, cp_out)
        halves = [_remote(srcs[a].at[:, pl.ds((1 - cc) * hrs[a], hrs[a])], gots[a], ssem, rsem, a, sib)
                  for a in range(2)]
        for cp in halves:
            cp.start()
        _small_reduce_body(db_ref, dlg_ref, drpb_ref, dlgo_ref, p_ref)
        for cp in halves:
            cp.wait_recv()
        for a in range(2):
            for j in range(N_SHARD):
                def add(i, carry, a=a, j=j):
                    r = pl.multiple_of(i * VROWS, VROWS)
                    mine = srcs[a][j, pl.ds(pl.multiple_of(cc * hrs[a] + r, VROWS), VROWS), :].astype(F32)
                    cps[a][j, pl.ds(r, VROWS), :] = (
                        mine + gots[a][j, pl.ds(r, VROWS), :].astype(F32)).astype(BF16)
                    return carry
                lax.fori_loop(0, hrs[a] // VROWS, add, 0)
        for cp in halves:
            cp.wait_send()

    vmem = pl.BlockSpec(memory_space=pltpu.VMEM)
    half_shapes = [(N_SHARD, hrs[a], arrs[a].shape[2]) for a in range(2)]
    return pl.pallas_call(
        body, name="grad_halves",
        in_specs=[vmem] * 4, out_specs=(vmem,) * 4,
        out_shape=(jax.ShapeDtypeStruct(half_shapes[0], BF16), jax.ShapeDtypeStruct(half_shapes[1], BF16),
                   jax.ShapeDtypeStruct((dbias.shape[0], 16, 32), F32), jax.ShapeDtypeStruct((32, 128), F32)),
        scratch_shapes=[pltpu.VMEM(half_shapes[0], BF16), pltpu.VMEM(half_shapes[1], BF16),
                        pltpu.VMEM((32, GRID_W), F32),
                        pltpu.SemaphoreType.DMA((2,)), pltpu.SemaphoreType.DMA((2,))],
        compiler_params=pltpu.CompilerParams(vmem_limit_bytes=56 << 20))(dwin_b, dwout_b, dbias, dlg)


def _grad_finish_call(sl_in, sl_out, small):
    arrs = (sl_in, sl_out)

    def body(sin, sout, sm, gin, gout, sm_all, h_in, h_out, ssem, rsem, lsem):
        x, y, cc = _mesh_pos()
        me = 4 * x + 2 * y + cc
        sib = (x, y, 1 - cc)
        sls, hs, gs = (sin, sout), (h_in, h_out), (gin, gout)
        sm_send, sm_recv = _all_to_all_small(sm, sm_all, ssem, rsem, 2, x, y, cc)
        sm_own = pltpu.make_async_copy(sm, sm_all.at[me], lsem.at[0])
        for cp in sm_send + [sm_own]:
            cp.start()
        for a in range(2):
            def total(i, carry, a=a):
                rows = pl.ds(pl.multiple_of(i * VROWS, VROWS), VROWS)
                sl = sls[a]
                hs[a][rows, :] = ((sl[0, rows, :].astype(F32) + sl[1, rows, :].astype(F32))
                                  + sl[2, rows, :].astype(F32)) + sl[3, rows, :].astype(F32)
                return carry
            lax.fori_loop(0, arrs[a].shape[1] // VROWS, total, 0)
        mine = [pltpu.make_async_copy(hs[a], gs[a].at[cc], lsem.at[1 + a]) for a in range(2)]
        back = [_remote(hs[a], gs[a].at[cc], ssem, rsem, a, sib) for a in range(2)]
        back_recv = [_remote(hs[a], gs[a].at[1 - cc], ssem, rsem, a, sib) for a in range(2)]
        for cp in mine + back:
            cp.start()
        _finish(mine + [sm_own], back + sm_send, back_recv + sm_recv)

    vmem = pl.BlockSpec(memory_space=pltpu.VMEM)
    return pl.pallas_call(
        body, name="grad_finish",
        in_specs=[vmem] * 3, out_specs=(vmem,) * 3,
        out_shape=(jax.ShapeDtypeStruct((2,) + sl_in.shape[1:], F32),
                   jax.ShapeDtypeStruct((2,) + sl_out.shape[1:], F32),
                   jax.ShapeDtypeStruct((N_DEV,) + small.shape, F32)),
        scratch_shapes=[pltpu.VMEM(sl_in.shape[1:], F32), pltpu.VMEM(sl_out.shape[1:], F32),
                        pltpu.SemaphoreType.DMA((9,)), pltpu.SemaphoreType.DMA((9,)),
                        pltpu.SemaphoreType.DMA((3,))],
        compiler_params=pltpu.CompilerParams(vmem_limit_bytes=48 << 20))(sl_in, sl_out, small)


def _adamw(w, g, m, v):
    m = ADAM_B1 * m + (1.0 - ADAM_B1) * g
    v = ADAM_B2 * v + (1.0 - ADAM_B2) * (g * g)
    m_hat = m / (1.0 - ADAM_B1 ** ADAM_STEP)
    v_hat = v / (1.0 - ADAM_B2 ** ADAM_STEP)
    return -ADAM_LR * (m_hat / (jnp.sqrt(v_hat) + ADAM_EPS) + ADAM_WD * w), m, v


def _adam_call(w, m, v, g, name):
    R, C = w.shape
    tr = 256

    def body(w_ref, m_ref, v_ref, g_ref, d_ref, mo_ref, vo_ref):
        d_ref[...], mo_ref[...], vo_ref[...] = _adamw(w_ref[...], g_ref[...], m_ref[...], v_ref[...])

    spec = pl.BlockSpec((tr, C), lambda i: (i, 0))
    return pl.pallas_call(
        body, name=name, grid=(R // tr,), in_specs=[spec] * 4,
        out_specs=(spec,) * 3, out_shape=(jax.ShapeDtypeStruct((R, C), F32),) * 3,
        compiler_params=_params(("arbitrary",)))(w, m, v, g)


R_GF, R_NG, R_LOSS, R_RNG, R_LGF, R_LGB, R_SHIFT, R_SCALE, R_GATE, R_SHIFT_C, R_SCALE_C, R_RNG2, R_RPB = (
    0, 1, 2, 3, 4, 5, 6, 8, 10, 12, 13, 14, 16)
W_GF, W_NG, W_CCTX, W_RNG, W_DF, W_DB, W_BADA, W_RPB = 0, 1, 2, 3, 4, 5, 6, 9


def _small_final_call(sm_all, c_t, c_ctx, wada_f, wada, m_ada, v_ada, wsm, msm, vsm, B):
    ws = wada.shape[1]
    NB = N_DEV * B

    def body(sm_ref, ct_ref, cctx_ref, wf_ref, wa_ref, ma_ref, va_ref, w_ref, m_ref, v_ref,
             g_ref, d_ref, mo_ref, vo_ref, ga_ref, da_ref, mao_ref, vao_ref, loss_ref, dmod_ref):
        x, y, _ = _mesh_pos()
        s = 2 * x + y
        tot = sm_ref[0]
        for dv in range(1, N_DEV):
            tot = tot + sm_ref[dv]
        w = w_ref[...]
        for dv in range(N_DEV):
            for b in range(B):
                r = dv * B + b
                for part, row in enumerate((R_SHIFT, R_SCALE, R_GATE)):
                    dmod_ref[r:r + 1, part * D:(part + 1) * D] = sm_ref[dv, row + b:row + b + 1, :]
        dmod_ref[NB:NB + 1, 0:D] = tot[R_SHIFT_C:R_SHIFT_C + 1, :]
        dmod_ref[NB:NB + 1, D:2 * D] = tot[R_SCALE_C:R_SCALE_C + 1, :]
        dmod_ref[NB:NB + 1, 2 * D:3 * D] = jnp.zeros((1, D), F32)
        dmod_ref[NB + 1:, :] = jnp.zeros((dmod_ref.shape[0] - NB - 1, 3 * D), F32)
        dmod = dmod_ref[...]
        cc = cctx_ref[...]
        scc = _sigmoid(cc)
        ct = ct_ref[...]
        act_t = ct * _sigmoid(ct)
        dmc = dmod[NB:NB + 1, :].astype(BF16)
        dact = jnp.zeros((1, D), F32)
        for sh in range(N_SHARD):
            dact = dact + _dot_nt(dmc[:, sh * ws:(sh + 1) * ws], wf_ref[sh])
        g = jnp.zeros((16, D), F32)
        rows = lax.broadcasted_iota(jnp.int32, (16, D), 0)

        def put(g, row, val):
            return jnp.where(rows == row, val, g)

        g = put(g, W_GF, tot[R_GF:R_GF + 1, :])
        g = put(g, W_NG, tot[R_NG:R_NG + 1, :])
        g = put(g, W_CCTX, dact * (scc * (1.0 + cc * (1.0 - scc))))
        g = put(g, W_RNG, tot[R_RNG:R_RNG + 1, :] + tot[R_RNG2:R_RNG2 + 1, :])
        g = put(g, W_DF, tot[R_LGF:R_LGF + 1, :] * (-jnp.exp(w[W_DF:W_DF + 1, :])))
        g = put(g, W_DB, tot[R_LGB:R_LGB + 1, :] * (-jnp.exp(w[W_DB:W_DB + 1, :])))
        db = jnp.sum(dmod, axis=0, keepdims=True)
        for part in range(3):
            g = put(g, W_BADA + part, db[:, part * D:(part + 1) * D])
        for part in range(4):
            g = put(g, W_RPB + part, tot[R_RPB + part:R_RPB + part + 1, :])
        g_ref[...] = g
        d_ref[...], mo_ref[...], vo_ref[...] = _adamw(w, g, m_ref[...], v_ref[...])
        loss_ref[...] = jnp.broadcast_to(
            (0.5 / D) * jnp.sum(tot[R_LOSS:R_LOSS + 1, :], axis=1, keepdims=True), (8, 128))
        for sh in range(N_SHARD):
            @pl.when(s == sh)
            def _():
                ga = jnp.dot(act_t, dmod[:, sh * ws:(sh + 1) * ws], precision=HIGHEST,
                             preferred_element_type=F32)
                ga_ref[...] = ga
                da_ref[...], mao_ref[...], vao_ref[...] = _adamw(wa_ref[...], ga, ma_ref[...], va_ref[...])

    sh_small = jax.ShapeDtypeStruct((16, D), F32)
    sh_ada = jax.ShapeDtypeStruct(wada.shape, F32)
    return pl.pallas_call(
        body, name="small_final",
        out_shape=(sh_small,) * 4 + (sh_ada,) * 4 + (jax.ShapeDtypeStruct((8, 128), F32),),
        scratch_shapes=[pltpu.VMEM((NB + 8, 3 * D), F32)],
        compiler_params=_params(vmem_mb=56))(
            sm_all, c_t, c_ctx, wada_f, wada, m_ada, v_ada, wsm, msm, vsm)


def _local_step(x, c, ctx, c_ctx, norm_g, wada_f, b_ada, win_f, bias, dec_f, dec_b, ret_norm_g,
                wout_f, final_g, target):
    B, L, _ = x.shape
    LC = ctx.shape[1]
    assert B == 2
    cos2, sin2 = _rope_tables(L, LC)
    c8 = jnp.concatenate([c, c_ctx[None, :], jnp.zeros((8 - B - 1, D), F32)], axis=0)
    mod = _mod_call(c8, wada_f, b_ada)
    P, h = _inproj_call(x, ctx, mod, norm_g, win_f, cos2, sin2)
    y_na, o_na = _na_fwd_call(P, bias, L, LC)
    sf, sb = _ret_states_call(P, dec_f, dec_b, L, LC)
    y_ret, o_ret = _retc_fwd_call(P, sf, sb, dec_f, dec_b, ret_norm_g, L)
    dY, dx2, dwout_p, sm_out = _out_call(y_na, y_ret, x, target, mod, final_g, wout_f.reshape(D, D))
    dnq, dng, dnk, dnv, dbias = _na_bwd_call(P, bias, dY, o_na, L, LC)
    drq, drg, drk, drv, dgn, dlg = _retc_bwd_call(P, sf, sb, dec_f, dec_b, ret_norm_g, o_ret, dY, cos2, sin2, L, LC)
    dsec = (dnq, dnk, dnv, dng, drq, drk, drv, drg)
    dwin_b = _dw_call(dsec, h, L)
    cp_in, cp_out, drpb, dlg_sum = _grad_halves_call(
        dwin_b, dwout_p.astype(BF16).reshape(N_SHARD, D // N_SHARD, D), dbias, dlg)
    grad_x, sm_dh, sl_in, sl_out = _dh_call(dsec, win_f, x, ctx, dx2, mod, norm_g, cp_in, cp_out)
    z = jnp.zeros((1, D), F32)
    pad = lambda v: jnp.pad(v.reshape(1, -1), ((0, 0), (0, D - v.size)))
    dlg_sum = dlg_sum.reshape(4, 8, 128)
    rpb_rows = jnp.pad(drpb[:, :15, :31].reshape(-1), (0, 4 * D - drpb.shape[0] * 465)).reshape(4, D)
    small = jnp.concatenate([
        sm_out[0:1], sm_dh[0:1], sm_out[1:2], pad(dgn[0]), pad(dlg_sum[:, 0, 0]), pad(dlg_sum[:, 1, 0]),
        sm_dh[3:5], sm_dh[5:7], sm_out[2:4], sm_dh[1:2], sm_dh[2:3], pad(dgn[1]), z, rpb_rows,
        jnp.zeros((SM_ROWS - 20, D), F32)], axis=0)
    return grad_x, sl_in, sl_out, small


def kernel(x, c, ctx, c_ctx, norm_g, w_ada, b_ada, w_in, na_rpb, ret_decay_fwd, ret_decay_bwd, ret_norm_g, w_out, final_norm_g, loss_target, m_c_ctx, m_norm_g, m_w_ada, m_b_ada, m_w_in, m_na_rpb, m_ret_decay_fwd, m_ret_decay_bwd, m_ret_norm_g, m_w_out, m_final_norm_g, v_c_ctx, v_norm_g, v_w_ada, v_b_ada, v_w_in, v_na_rpb, v_ret_decay_fwd, v_ret_decay_bwd, v_ret_norm_g, v_w_out, v_final_norm_g):
    B = x.shape[0]
    win_f, wout_f, wada_f, c_all, bias = _gather_call(
        w_in[0].astype(BF16), w_out[0].astype(BF16), w_ada[0].astype(BF16), c,
        na_rpb[0].reshape(na_rpb.shape[1], -1))
    grad_x, sl_in, sl_out, small = _local_step(
        x, c, ctx, c_ctx, norm_g, wada_f, b_ada, win_f, bias, ret_decay_fwd, ret_decay_bwd,
        ret_norm_g, wout_f, final_norm_g.reshape(1, D), loss_target)
    gin, gout, sm_all = _grad_finish_call(sl_in, sl_out, small)
    g_win, g_wout = gin.reshape(w_in.shape[1:]), gout.reshape(w_out.shape[1:])
    d_win, nm_win, nv_win = _adam_call(w_in[0], m_w_in[0], v_w_in[0], g_win, "adam_w_in")
    d_wout, nm_wout, nv_wout = _adam_call(w_out[0], m_w_out[0], v_w_out[0], g_wout, "adam_w_out")

    def pack(gf, ng, cc, rng, df, db, bada, rpb):
        pad = lambda v: jnp.pad(v.reshape(1, -1), ((0, 0), (0, D - v.size)))
        return jnp.concatenate([
            gf.reshape(1, D), ng.reshape(1, D), cc.reshape(1, D), pad(rng), pad(df), pad(db),
            bada.reshape(3, D), jnp.pad(rpb.reshape(-1), (0, 4 * D - rpb.size)).reshape(4, D),
            jnp.zeros((3, D), F32)], axis=0)

    wsm = pack(final_norm_g, norm_g, c_ctx, ret_norm_g, ret_decay_fwd, ret_decay_bwd, b_ada, na_rpb)
    msm = pack(m_final_norm_g, m_norm_g, m_c_ctx, m_ret_norm_g, m_ret_decay_fwd, m_ret_decay_bwd, m_b_ada, m_na_rpb)
    vsm = pack(v_final_norm_g, v_norm_g, v_c_ctx, v_ret_norm_g, v_ret_decay_fwd, v_ret_decay_bwd, v_b_ada, v_na_rpb)
    c_t = jnp.concatenate([c_all.reshape(N_DEV * B, D), c_ctx.reshape(1, D), jnp.zeros((7, D), F32)], axis=0).T
    outs = _small_final_call(sm_all, c_t, c_ctx.reshape(1, D), wada_f,
                             w_ada[0], m_w_ada[0], v_w_ada[0], wsm, msm, vsm, B)
    smalls, adas, loss = outs[0:4], outs[4:8], outs[8][0, 0]

    def unpack(p):
        rw = ret_norm_g.shape[1]
        return dict(
            final_norm_g=p[W_GF], norm_g=p[W_NG:W_NG + 1], c_ctx=p[W_CCTX], ret_norm_g=p[W_RNG:W_RNG + 1, :rw],
            ret_decay_fwd=p[W_DF:W_DF + 1, :4], ret_decay_bwd=p[W_DB:W_DB + 1, :4],
            b_ada=p[W_BADA:W_BADA + 3].reshape(1, 3 * D),
            na_rpb=p[W_RPB:W_RPB + 4].reshape(-1)[:na_rpb.size].reshape(na_rpb.shape))

    res = []
    for p, ada, win_o, wout_o in zip(smalls, adas, (g_win, d_win, nm_win, nv_win),
                                     (g_wout, d_wout, nm_wout, nv_wout)):
        u = unpack(p)
        res.append([u["c_ctx"], u["norm_g"], ada[None], u["b_ada"], win_o[None], u["na_rpb"],
                    u["ret_decay_fwd"], u["ret_decay_bwd"], u["ret_norm_g"], wout_o[None], u["final_norm_g"]])
    return (loss, grad_x, *res[0], *res[1], *res[2], *res[3])
```

```python
import numpy as np
import jax
import jax.numpy as jnp
from jax import lax
from jax.experimental import pallas as pl
from jax.experimental.pallas import tpu as pltpu

F32 = jnp.float32
BF16 = jnp.bfloat16
HIGHEST = lax.Precision.HIGHEST

D = 1024
GRID_W = 64
NA_DH = 64
RET_DK = 128
ROPE_BASE = 10000.0
EPS = 1e-6
NEG = -1e30
TQ = 256
KW = 12 * GRID_W
N_SHARD = 4
N_DEV = 8
SM_ROWS = 24

ADAM_LR = 0.001
ADAM_B1 = 0.9
ADAM_B2 = 0.999
ADAM_EPS = 1e-08
ADAM_WD = 0.01
ADAM_STEP = 10

MESH = pl.DeviceIdType.MESH
ANY = pl.BlockSpec(memory_space=pl.ANY)


def _params(sem=None, vmem_mb=48):
    return pltpu.CompilerParams(dimension_semantics=sem, vmem_limit_bytes=vmem_mb << 20)


def _dot(a, b):
    return jnp.dot(a, b, preferred_element_type=F32)


def _dot_nt(a, b):
    return lax.dot_general(a, b, (((1,), (1,)), ((), ())), preferred_element_type=F32)


def _dot_tn(a, b):
    return lax.dot_general(a, b, (((0,), (0,)), ((), ())), preferred_element_type=F32)


def _sigmoid(x):
    return 1.0 / (1.0 + jnp.exp(-x))


def _rope_tables(L, LC):
    half = RET_DK // 2
    nf = half // 2
    t = np.arange(L)
    row = (t // GRID_W).astype(np.float32)
    col = (t % GRID_W).astype(np.float32)
    inv = (np.float32(ROPE_BASE) ** (-np.arange(nf, dtype=np.float32) / np.float32(nf))).astype(np.float32)
    ang = np.concatenate([row[:, None] * inv, col[:, None] * inv], axis=-1).astype(np.float32)
    cos, sin = np.cos(ang).astype(np.float32), np.sin(ang).astype(np.float32)
    cos2 = np.concatenate([cos, cos], axis=-1)
    sin2 = np.concatenate([-sin, sin], axis=-1)
    cos2 = np.concatenate([cos2, np.ones((LC, RET_DK), np.float32)], axis=0)
    sin2 = np.concatenate([sin2, np.zeros((LC, RET_DK), np.float32)], axis=0)
    return jnp.asarray(cos2), jnp.asarray(sin2)


def _mod_call(c8, wada_f, b_ada):
    ws = wada_f.shape[2]

    def body(c_ref, w_ref, b_ref, o_ref):
        a = c_ref[...]
        a = (a * _sigmoid(a)).astype(BF16)
        for s in range(N_SHARD):
            o_ref[:, s * ws:(s + 1) * ws] = _dot(a, w_ref[s]) + b_ref[:, s * ws:(s + 1) * ws]

    return pl.pallas_call(
        body, name="ada_mod", out_shape=jax.ShapeDtypeStruct((8, 3 * D), F32),
        compiler_params=_params())(c8, wada_f, b_ada)


def _dc_masks():
    cq = lax.broadcasted_iota(jnp.int32, (GRID_W, GRID_W), 0)
    ck = lax.broadcasted_iota(jnp.int32, (GRID_W, GRID_W), 1)
    dc = jnp.clip(ck - cq + 15, 0, 30)
    c0 = jnp.clip(cq - 8, 0, GRID_W - 16)
    col_ok = (ck >= c0) & (ck < c0 + 16)
    return dc, col_ok


def _bias_blocks():
    out = []
    for typ, delta in enumerate((4, 0, -4)):
        for rq in range(4):
            for rkk in range(12):
                dr = rkk + delta - rq - 4
                if typ == 0:
                    ok = -rq <= dr <= 7 - rq
                elif typ == 1:
                    ok = -4 <= dr <= 3
                else:
                    ok = -4 - rq <= dr <= 3 - rq
                out.append((typ, rq, rkk, dr if ok else None))
    return out


def _bias_body(r_ref, bias_ref, et_ref):
    dc, col_ok = _dc_masks()
    masks = [(dc == j).astype(F32) for j in range(31)]

    def per_h(h, carry):
        for dr in range(15):
            t = jnp.zeros((GRID_W, GRID_W), F32)
            for j in range(31):
                t = t + masks[j] * r_ref[h, dr * 31 + j]
            et_ref[dr] = jnp.where(col_ok, t, NEG)
        neg = jnp.full((GRID_W, GRID_W), NEG, F32)
        for typ, rq, rkk, dr in _bias_blocks():
            blk = neg if dr is None else et_ref[dr + 7]
            bias_ref[h, typ, rq * 64:(rq + 1) * 64, rkk * 64:(rkk + 1) * 64] = blk
        return carry

    lax.fori_loop(0, bias_ref.shape[0], per_h, 0)


def _small_reduce_body(db_ref, dlg_ref, drpb_ref, dlgo_ref, p_ref):
    dc, _ = _dc_masks()
    masks = [(dc == j).astype(F32) for j in range(31)]
    ones = jnp.ones((8, GRID_W), F32)
    p_ref[...] = jnp.zeros_like(p_ref)
    drpb_ref[...] = jnp.zeros_like(drpb_ref)

    def per_h(h, carry):
        acc = {}
        for typ, rq, rkk, dr in _bias_blocks():
            if dr is None:
                continue
            blk = db_ref[h, typ, rq * 64:(rq + 1) * 64, rkk * 64:(rkk + 1) * 64]
            acc[dr] = blk if dr not in acc else acc[dr] + blk
        for dr in range(-7, 8):
            t = acc[dr]
            for j in range(31):
                p_ref[j:j + 1, :] = jnp.sum(t * masks[j], axis=0, keepdims=True)
            red = lax.dot_general(ones, p_ref[...], (((1,), (1,)), ((), ())),
                                  precision=HIGHEST, preferred_element_type=F32)
            drpb_ref[h, dr + 7:dr + 8, :] = red[0:1, :]
        return carry

    lax.fori_loop(0, db_ref.shape[0], per_h, 0)
    x = dlg_ref[0]
    for b in range(1, dlg_ref.shape[0]):
        x = x + dlg_ref[b]
    x = x.reshape(4 * 8, x.shape[-1])
    dlgo_ref[...] = jnp.dot(x, jnp.ones((x.shape[-1], 128), F32), precision=HIGHEST,
                            preferred_element_type=F32)


def _inproj_gather_call(order, x, ctx, mod, norm_g, win_b, cos2, sin2):
    B, L, _ = x.shape
    LC = ctx.shape[1]
    T = L + LC
    nl, nt = L // TQ, T // TQ
    assert LC == TQ and L % TQ == 0
    kscale = RET_DK ** -0.5
    HR = D // 2

    def body(ord_ref, x_ref, ctx_ref, mod_ref, g_ref, wown_ref, cos_ref, sin_ref, p_ref, h_ref, wf_ref,
             w_all, ssem, rsem, lsem):
        j, b, t = pl.program_id(0), pl.program_id(1), pl.program_id(2)
        first = (b == 0) & (t == 0)
        mx, my, mc = _mesh_pos()
        s = 2 * mx + my
        sib = (mx, my, 1 - mc)
        own = pltpu.make_async_copy(wown_ref, w_all.at[s], lsem.at[0])
        ici_send, ici_recv, fwd_send, fwd_recv, outs = [], [], [], [], [
            pltpu.make_async_copy(w_all.at[s], wf_ref.at[s], lsem.at[1])]
        for k, (px, py) in enumerate(_other_chips(mx, my)):
            ps = 2 * px + py
            mine = wown_ref.at[pl.ds(mc * HR, HR)]
            ici_send.append(_remote(mine, w_all.at[s, pl.ds(mc * HR, HR)], ssem, rsem, k, (px, py, mc)))
            ici_recv.append(_remote(mine, w_all.at[ps, pl.ds(mc * HR, HR)], ssem, rsem, k, (px, py, mc)))
            got = w_all.at[ps, pl.ds(mc * HR, HR)]
            fwd_send.append(_remote(got, got, ssem, rsem, 3 + k, sib))
            theirs = w_all.at[ps, pl.ds((1 - mc) * HR, HR)]
            fwd_recv.append(_remote(theirs, theirs, ssem, rsem, 3 + k, sib))
            outs.append(pltpu.make_async_copy(w_all.at[ps], wf_ref.at[ps], lsem.at[2 + k]))

        @pl.when(first & (j == 0))
        def _():
            own.start()
            for cp in ici_send:
                cp.start()
            own.wait()
            outs[0].start()

        for k in range(3):
            @pl.when(first & (j == k + 1))
            def _(k=k):
                ici_recv[k].wait_recv()
                fwd_send[k].start()
                fwd_recv[k].wait_recv()
                outs[1 + k].start()

        is_lat = t < nl
        xt = jnp.where(is_lat, x_ref[...], ctx_ref[...])
        mrow = mod_ref[pl.ds(jnp.where(is_lat, b, B), 1), :]
        shift, scale = mrow[:, 0:D], mrow[:, D:2 * D]
        rstd = lax.rsqrt(jnp.mean(xt * xt, axis=-1, keepdims=True) + EPS)
        hb = ((xt * rstd * g_ref[...]) * (1.0 + scale) + shift).astype(BF16)

        @pl.when(j == 0)
        def _():
            h_ref[...] = hb

        cs, sn = cos_ref[...], sin_ref[...]
        shard = ord_ref[j]
        for sh in range(N_SHARD):
            @pl.when(shard == sh)
            def _(sh=sh):
                for half in range(2):
                    sec = 2 * sh + half
                    acc = _dot(hb, w_all[sh, :, half * 512:(half + 1) * 512])
                    if sec == 0:
                        acc = acc * (NA_DH ** -0.5)
                    if sec in (4, 5):
                        for q in range(4):
                            a = acc[:, q * 128:(q + 1) * 128]
                            r = a * cs + pltpu.roll(a, 64, 1) * sn
                            if sec == 5:
                                r = r * kscale
                            p_ref[:, half * 512 + q * 128:half * 512 + (q + 1) * 128] = r.astype(BF16)
                    else:
                        p_ref[:, half * 512:(half + 1) * 512] = acc.astype(BF16)

        @pl.when((j == N_SHARD - 1) & (b == B - 1) & (t == nt - 1))
        def _():
            _finish(outs, ici_send + fwd_send, [])

    tok = lambda j, b, t, o: (b, jnp.minimum(t, nl - 1), 0)
    grid_spec = pltpu.PrefetchScalarGridSpec(
        num_scalar_prefetch=1, grid=(N_SHARD, B, nt),
        in_specs=[
            pl.BlockSpec((None, TQ, D), tok),
            pl.BlockSpec((None, TQ, D), lambda j, b, t, o: (b, 0, 0)),
            pl.BlockSpec((8, 3 * D), lambda j, b, t, o: (0, 0)),
            pl.BlockSpec((1, D), lambda j, b, t, o: (0, 0)),
            ANY,
            pl.BlockSpec((TQ, RET_DK), lambda j, b, t, o: (t, 0)),
            pl.BlockSpec((TQ, RET_DK), lambda j, b, t, o: (t, 0)),
        ],
        out_specs=(pl.BlockSpec((None, TQ, D), lambda j, b, t, o: (b, t, o[j])),
                   pl.BlockSpec((None, TQ, D), lambda j, b, t, o: (
                       jnp.where(j == 0, b, B - 1), jnp.where(j == 0, t, nt - 1), 0)), ANY),
        scratch_shapes=[pltpu.VMEM((N_SHARD, D, D), BF16),
                        pltpu.SemaphoreType.DMA((6,)), pltpu.SemaphoreType.DMA((6,)),
                        pltpu.SemaphoreType.DMA((5,))])
    return pl.pallas_call(
        body, name="in_proj", grid_spec=grid_spec,
        out_shape=(jax.ShapeDtypeStruct((B, T, 4 * D), BF16), jax.ShapeDtypeStruct((B, T, D), BF16),
                   jax.ShapeDtypeStruct((N_SHARD, D, D), BF16)),
        compiler_params=_params(("arbitrary",) * 3))(order, x, ctx, mod, norm_g, win_b, cos2, sin2)


def _na_specs(L, T, rows):
    nm = rows // 4
    q_spec = pl.BlockSpec((None, TQ, 128), lambda hp, b, m: (b, m, hp))
    k_spec = pl.BlockSpec((None, T, 128), lambda hp, b, m: (b, 0, 4 + hp))
    v_spec = pl.BlockSpec((None, T, 128), lambda hp, b, m: (b, 0, 8 + hp))
    g_spec = pl.BlockSpec((None, TQ, 128), lambda hp, b, m: (b, m, 12 + hp))
    bias_spec = pl.BlockSpec((2, 3, TQ, KW), lambda hp, b, m: (hp, 0, 0, 0))
    return nm, q_spec, k_spec, v_spec, g_spec, bias_spec


def _na_tile(m, nm, rows):
    typ = jnp.where(m == 0, 0, jnp.where(m == nm - 1, 2, 1))
    start = pl.multiple_of(jnp.clip(4 * m - 4, 0, rows - 12) * GRID_W, TQ)
    return typ, start


def _na_fwd_call(P, bias, L, LC):
    B, T, _ = P.shape
    rows = L // GRID_W
    nm, q_spec, k_spec, v_spec, g_spec, bias_spec = _na_specs(L, T, rows)

    def body(q_ref, k_ref, v_ref, g_ref, bias_ref, y_ref, o_ref):
        typ, start = _na_tile(pl.program_id(2), nm, rows)
        for hh in range(2):
            ln = slice(hh * NA_DH, (hh + 1) * NA_DH)
            q = q_ref[:, ln]
            kw, vw = k_ref[pl.ds(start, KW), ln], v_ref[pl.ds(start, KW), ln]
            kc, vc = k_ref[L:L + LC, ln], v_ref[L:L + LC, ln]
            s1 = _dot_nt(q, kw) + bias_ref[hh, typ]
            s2 = _dot_nt(q, kc)
            mx = jnp.maximum(jnp.max(s1, axis=-1, keepdims=True), jnp.max(s2, axis=-1, keepdims=True))
            p1, p2 = jnp.exp(s1 - mx), jnp.exp(s2 - mx)
            inv = 1.0 / (jnp.sum(p1, axis=-1, keepdims=True) + jnp.sum(p2, axis=-1, keepdims=True))
            o = (_dot(p1.astype(BF16), vw) + _dot(p2.astype(BF16), vc)) * inv
            g = g_ref[:, ln].astype(F32)
            o_ref[:, ln] = o.astype(BF16)
            y_ref[:, ln] = (o * (g * _sigmoid(g))).astype(BF16)

    tile = pl.BlockSpec((None, TQ, 128), lambda hp, b, m: (b, m, hp))
    return pl.pallas_call(
        body, name="na_fwd", grid=(4, B, nm),
        in_specs=[q_spec, k_spec, v_spec, g_spec, bias_spec],
        out_specs=(tile, tile),
        out_shape=(jax.ShapeDtypeStruct((B, L, 512), BF16),) * 2,
        compiler_params=_params(("arbitrary",) * 3))(P, P, P, P, bias)


def _na_bwd_call(P, bias, dY, o_na, L, LC):
    B, T, _ = P.shape
    rows = L // GRID_W
    nm, q_spec, k_spec, v_spec, g_spec, bias_spec = _na_specs(L, T, rows)
    scale = NA_DH ** -0.5

    RB = 32

    def body(q_ref, k_ref, v_ref, g_ref, bias_ref, dy_ref, o_ref, dq_ref, dg_ref, dk_ref, dv_ref, db_ref,
             s1_ref, s2_ref, dp1_ref, dp2_ref, p1_ref, p2_ref, ds1_ref, ds2_ref):
        b, m = pl.program_id(1), pl.program_id(2)
        typ, start = _na_tile(m, nm, rows)

        @pl.when(m == 0)
        def _():
            dk_ref[...] = jnp.zeros_like(dk_ref)
            dv_ref[...] = jnp.zeros_like(dv_ref)

        @pl.when((m == 0) & (b == 0))
        def _():
            db_ref[...] = jnp.zeros_like(db_ref)

        for hh in range(2):
            ln = slice(hh * NA_DH, (hh + 1) * NA_DH)
            q = q_ref[:, ln]
            kw, vw = k_ref[pl.ds(start, KW), ln], v_ref[pl.ds(start, KW), ln]
            kc, vc = k_ref[L:L + LC, ln], v_ref[L:L + LC, ln]
            g = g_ref[:, ln].astype(F32)
            sg = _sigmoid(g)
            dy = dy_ref[:, ln].astype(F32)
            do = (dy * (g * sg)).astype(BF16)
            s1_ref[...] = _dot_nt(q, kw)
            s2_ref[...] = _dot_nt(q, kc)
            dp1_ref[...] = _dot_nt(do, vw)
            dp2_ref[...] = _dot_nt(do, vc)

            def rows_pass(r, carry, hh=hh):
                rw = pl.ds(pl.multiple_of(r * RB, RB), RB)
                a = s1_ref[rw, :] + bias_ref[hh, typ, rw, :]
                c = s2_ref[rw, :]
                mx = jnp.maximum(jnp.max(a, axis=-1, keepdims=True), jnp.max(c, axis=-1, keepdims=True))
                e1, e2 = jnp.exp(a - mx), jnp.exp(c - mx)
                inv = 1.0 / (jnp.sum(e1, axis=-1, keepdims=True) + jnp.sum(e2, axis=-1, keepdims=True))
                p1, p2 = e1 * inv, e2 * inv
                p1_ref[rw, :] = p1.astype(BF16)
                p2_ref[rw, :] = p2.astype(BF16)
                dp1, dp2 = dp1_ref[rw, :], dp2_ref[rw, :]
                delta = jnp.sum(p1 * dp1, axis=-1, keepdims=True) + jnp.sum(p2 * dp2, axis=-1, keepdims=True)
                ds1 = p1 * (dp1 - delta)
                db_ref[hh, typ, rw, :] += ds1
                ds1_ref[rw, :] = ds1.astype(BF16)
                ds2_ref[rw, :] = (p2 * (dp2 - delta)).astype(BF16)
                return carry

            lax.fori_loop(0, TQ // RB, rows_pass, 0, unroll=True)
            p1b, p2b, ds1b, ds2b = p1_ref[...], p2_ref[...], ds1_ref[...], ds2_ref[...]
            dg_ref[:, ln] = (dy * o_ref[:, ln].astype(F32) * (sg * (1.0 + g * (1.0 - sg)))).astype(BF16)
            dq_ref[:, ln] = ((_dot(ds1b, kw) + _dot(ds2b, kc)) * scale).astype(BF16)
            dk_ref[pl.ds(start, KW), ln] += _dot_tn(ds1b, q)
            dv_ref[pl.ds(start, KW), ln] += _dot_tn(p1b, do)
            dk_ref[L:L + LC, ln] += _dot_tn(ds2b, q)
            dv_ref[L:L + LC, ln] += _dot_tn(p2b, do)

    tile = pl.BlockSpec((None, TQ, 128), lambda hp, b, m: (b, m, hp))
    kv_out = pl.BlockSpec((None, T, 128), lambda hp, b, m: (b, 0, hp))
    wide, narrow = (TQ, KW), (TQ, LC)
    return pl.pallas_call(
        body, name="na_bwd", grid=(4, B, nm),
        in_specs=[q_spec, k_spec, v_spec, g_spec, bias_spec, tile, tile],
        out_specs=(tile, tile, kv_out, kv_out, bias_spec),
        out_shape=(jax.ShapeDtypeStruct((B, L, 512), BF16), jax.ShapeDtypeStruct((B, L, 512), BF16),
                   jax.ShapeDtypeStruct((B, T, 512), F32), jax.ShapeDtypeStruct((B, T, 512), F32),
                   jax.ShapeDtypeStruct(bias.shape, F32)),
        scratch_shapes=[pltpu.VMEM(wide, F32), pltpu.VMEM(narrow, F32), pltpu.VMEM(wide, F32), pltpu.VMEM(narrow, F32),
                        pltpu.VMEM(wide, BF16), pltpu.VMEM(narrow, BF16), pltpu.VMEM(wide, BF16),
                        pltpu.VMEM(narrow, BF16)],
        compiler_params=_params(("arbitrary",) * 3))(P, P, P, P, bias, dY, o_na)


def _head_scalar(dec_ref, h):
    lane = lax.broadcasted_iota(jnp.int32, dec_ref.shape, 1)
    return -jnp.sum(jnp.where(lane == h, jnp.exp(dec_ref[...]), 0.0), axis=1, keepdims=True)


def _ret_specs(T):
    q_spec = pl.BlockSpec((None, TQ, 128), lambda b, h, i: (b, i, 16 + h))
    k_spec = pl.BlockSpec((None, T, 128), lambda b, h, i: (b, 0, 20 + h))
    v_spec = pl.BlockSpec((None, T, 128), lambda b, h, i: (b, 0, 24 + h))
    g_spec = pl.BlockSpec((None, TQ, 128), lambda b, h, i: (b, i, 28 + h))
    dec_spec = pl.BlockSpec((1, 4), lambda b, h, i: (0, 0))
    gn_spec = pl.BlockSpec((1, 128), lambda b, h, i: (0, h))
    return q_spec, k_spec, v_spec, g_spec, dec_spec, gn_spec


def _chunk_decay(lgf, lgb):
    tau = lax.broadcasted_iota(jnp.int32, (TQ, 1), 0).astype(F32)
    sig = lax.broadcasted_iota(jnp.int32, (1, TQ), 1).astype(F32)
    dist = tau - sig
    dm = jnp.exp(dist * jnp.where(dist > 0, lgf, -lgb)) * jnp.where(dist == 0, 2.0, 1.0)
    return tau, dist, dm


def _ret_states_call(P, dec_f, dec_b, L, LC):
    B, T, _ = P.shape
    n = L // TQ

    def body(df_ref, db_ref, k_ref, v_ref, sf_ref, sb_ref):
        h = pl.program_id(1)
        lgf, lgb = _head_scalar(df_ref, h), _head_scalar(db_ref, h)
        tau = lax.broadcasted_iota(jnp.int32, (TQ, 1), 0).astype(F32)
        jc = lax.broadcasted_iota(jnp.int32, (LC, 1), 0).astype(F32)
        wf, wb = jnp.exp(lgf * (TQ - 1.0 - tau)), jnp.exp(lgb * tau)
        gcf, gcb = jnp.exp(lgf * float(TQ)), jnp.exp(lgb * float(TQ))
        kc, vc = k_ref[L:L + LC, :].astype(F32), v_ref[L:L + LC, :]

        def chunk_state(i, w):
            ks = pl.multiple_of(i * TQ, TQ)
            return _dot_tn((k_ref[pl.ds(ks, TQ), :].astype(F32) * w).astype(BF16), v_ref[pl.ds(ks, TQ), :])

        def fwd(i, s):
            sf_ref[i] = s
            return gcf * s + chunk_state(i, wf)

        lax.fori_loop(0, n, fwd, _dot_tn((kc * jnp.exp(lgf * (LC - 1.0 - jc))).astype(BF16), vc), unroll=True)

        def bwd(r, s):
            i = n - 1 - r
            sb_ref[i] = s
            return gcb * s + chunk_state(i, wb)

        lax.fori_loop(0, n, bwd, _dot_tn((kc * jnp.exp(lgb * jc)).astype(BF16), vc), unroll=True)

    st = pl.BlockSpec((None, None, n, RET_DK, RET_DK), lambda b, h: (b, h, 0, 0, 0))
    return pl.pallas_call(
        body, name="ret_states", grid=(B, 4),
        in_specs=[pl.BlockSpec((1, 4), lambda b, h: (0, 0)), pl.BlockSpec((1, 4), lambda b, h: (0, 0)),
                  pl.BlockSpec((None, T, 128), lambda b, h: (b, 0, 20 + h)),
                  pl.BlockSpec((None, T, 128), lambda b, h: (b, 0, 24 + h))],
        out_specs=(st, st),
        out_shape=(jax.ShapeDtypeStruct((B, 4, n, RET_DK, RET_DK), F32),) * 2,
        compiler_params=_params(("arbitrary",) * 2))(dec_f, dec_b, P, P)


def _retc_fwd_call(P, sf, sb, dec_f, dec_b, ret_norm_g, L):
    B, T, _ = P.shape
    q_spec, _, _, g_spec, dec_spec, gn_spec = _ret_specs(T)
    k_spec = pl.BlockSpec((None, TQ, 128), lambda b, h, i: (b, i, 20 + h))
    v_spec = pl.BlockSpec((None, TQ, 128), lambda b, h, i: (b, i, 24 + h))
    st_spec = pl.BlockSpec((None, None, None, RET_DK, RET_DK), lambda b, h, i: (b, h, i, 0, 0))

    def body(df_ref, db_ref, q_ref, k_ref, v_ref, g_ref, gn_ref, sf_ref, sb_ref, y_ref, o_ref):
        h = pl.program_id(1)
        lgf, lgb = _head_scalar(df_ref, h), _head_scalar(db_ref, h)
        tau, _, dm = _chunk_decay(lgf, lgb)
        q = q_ref[...]
        qf = q.astype(F32)
        acc = _dot((_dot_nt(q, k_ref[...]) * dm).astype(BF16), v_ref[...])
        acc = acc + _dot((qf * jnp.exp(lgf * (tau + 1.0))).astype(BF16), sf_ref[...].astype(BF16))
        acc = acc + _dot((qf * jnp.exp(lgb * (TQ - tau))).astype(BF16), sb_ref[...].astype(BF16))
        o_ref[...] = acc
        rn = lax.rsqrt(jnp.mean(acc * acc, axis=-1, keepdims=True) + EPS)
        g = g_ref[...].astype(F32)
        y_ref[...] = ((acc * rn * gn_ref[...]) * (g * _sigmoid(g))).astype(BF16)

    tile = pl.BlockSpec((None, TQ, 128), lambda b, h, i: (b, i, h))
    return pl.pallas_call(
        body, name="ret_fwd", grid=(B, 4, L // TQ),
        in_specs=[dec_spec, dec_spec, q_spec, k_spec, v_spec, g_spec, gn_spec, st_spec, st_spec],
        out_specs=(tile, tile),
        out_shape=(jax.ShapeDtypeStruct((B, L, 512), BF16), jax.ShapeDtypeStruct((B, L, 512), F32)),
        compiler_params=_params(("arbitrary",) * 3))(dec_f, dec_b, P, P, P, P, ret_norm_g, sf, sb)


def _retc_bwd_call(P, sf, sb, dec_f, dec_b, ret_norm_g, o_ret, dY, cos2, sin2, L, LC):
    B, T, _ = P.shape
    n = L // TQ
    C = float(TQ)
    kscale = RET_DK ** -0.5
    q_spec, k_spec, v_spec, g_spec, dec_spec, gn_spec = _ret_specs(T)
    st_spec = pl.BlockSpec((None, None, n, RET_DK, RET_DK), lambda b, h, i: (b, h, 0, 0, 0))

    def body(df_ref, db_ref, q_ref, k_ref, v_ref, g_ref, gn_ref, o_ref, dy_ref, cos_ref, sin_ref, sf_ref, sb_ref,
             dq_ref, dg_ref, dk_ref, dv_ref, dgn_ref, dlg_ref, dsf_ref, dsb_ref):
        h, i = pl.program_id(1), pl.program_id(2)
        lgf, lgb = _head_scalar(df_ref, h), _head_scalar(db_ref, h)
        tau, dist, dm = _chunk_decay(lgf, lgb)

        @pl.when(i == 0)
        def _():
            dk_ref[...] = jnp.zeros_like(dk_ref)
            dv_ref[...] = jnp.zeros_like(dv_ref)
            dgn_ref[...] = jnp.zeros_like(dgn_ref)
            dlg_ref[...] = jnp.zeros_like(dlg_ref)

        def add_lg(row, x):
            cs = jnp.sum(x, axis=0, keepdims=True)
            tot = cs[:, 0:128]
            for part in range(1, x.shape[1] // 128):
                tot = tot + cs[:, part * 128:(part + 1) * 128]
            dlg_ref[row:row + 1, :] += tot

        q = q_ref[...]
        qf = q.astype(F32)
        o = o_ref[...]
        g = g_ref[...].astype(F32)
        dy = dy_ref[...].astype(F32)
        gn = gn_ref[...]
        sg = _sigmoid(g)
        rn = lax.rsqrt(jnp.mean(o * o, axis=-1, keepdims=True) + EPS)
        nrm = o * rn
        dg_ref[...] = (dy * (nrm * gn) * (sg * (1.0 + g * (1.0 - sg)))).astype(BF16)
        dhn = dy * (g * sg)
        dgn_ref[...] += jnp.sum(dhn * nrm, axis=0, keepdims=True)
        dnrm = dhn * gn
        do = rn * (dnrm - nrm * jnp.mean(dnrm * nrm, axis=-1, keepdims=True))
        dob = do.astype(BF16)
        rows = pl.ds(pl.multiple_of(i * TQ, TQ), TQ)
        ki, vi = k_ref[rows, :], v_ref[rows, :]
        s = _dot_nt(q, ki)
        dsv = _dot_nt(dob, vi)
        dsb = (dsv * dm).astype(BF16)
        dk_ref[rows, :] += _dot_tn(dsb, q)
        dv_ref[rows, :] += _dot_tn((s * dm).astype(BF16), dob)
        xw = s * dsv * dm * jnp.abs(dist)
        fpart = jnp.where(dist > 0, xw, 0.0)
        add_lg(0, fpart)
        add_lg(1, xw - fpart)
        dq = _dot(dsb, ki)
        af, ab = jnp.exp(lgf * (tau + 1.0)), jnp.exp(lgb * (C - tau))
        qa, qb = (qf * af).astype(BF16), (qf * ab).astype(BF16)
        sfi, sbi = sf_ref[i].astype(BF16), sb_ref[i].astype(BF16)
        dq = dq + af * _dot_nt(dob, sfi) + ab * _dot_nt(dob, sbi)
        dsf_ref[i] = _dot_tn(qa, dob)
        dsb_ref[i] = _dot_tn(qb, dob)
        add_lg(0, (tau + 1.0) * (_dot(qa, sfi) * do))
        add_lg(1, (C - tau) * (_dot(qb, sbi) * do))
        cs, sn = cos_ref[rows, :], sin_ref[rows, :]
        dq_ref[...] = (dq * cs - pltpu.roll(dq, 64, 1) * sn).astype(BF16)

        @pl.when(i == n - 1)
        def _():
            jc = lax.broadcasted_iota(jnp.int32, (LC, 1), 0).astype(F32)
            crow = pl.ds(L, LC)

            def through_state(rws, w, dw, gst, row):
                kk, vv = k_ref[rws, :].astype(F32), v_ref[rws, :]
                gb = gst.astype(BF16)
                vg = _dot_nt(vv, gb)
                kw = kk * w
                dk_ref[rws, :] += w * vg
                dv_ref[rws, :] += _dot(kw.astype(BF16), gb)
                add_lg(row, dw * (kw * vg))

            def scan(lg, gc, w, dw, st_ref, dst_ref, order, row):
                def step(r, gst):
                    j = order(r)
                    through_state(pl.ds(pl.multiple_of(j * TQ, TQ), TQ), w, dw, gst, row)
                    add_lg(row, (C * gc) * (gst * st_ref[j]))
                    return dst_ref[j] + gc * gst
                return lax.fori_loop(0, n, step, jnp.zeros((RET_DK, RET_DK), F32), unroll=True)

            gcf, gcb = jnp.exp(lgf * C), jnp.exp(lgb * C)
            g0 = scan(lgf, gcf, jnp.exp(lgf * (C - 1.0 - tau)), C - 1.0 - tau, sf_ref, dsf_ref,
                      lambda r: n - 1 - r, 0)
            through_state(crow, jnp.exp(lgf * (LC - 1.0 - jc)), LC - 1.0 - jc, g0, 0)
            g1 = scan(lgb, gcb, jnp.exp(lgb * tau), tau, sb_ref, dsb_ref, lambda r: r, 1)
            through_state(crow, jnp.exp(lgb * jc), jc, g1, 1)
            dk = dk_ref[...]
            dk_ref[...] = (dk * cos_ref[...] - pltpu.roll(dk, 64, 1) * sin_ref[...]) * kscale

    tile = pl.BlockSpec((None, TQ, 128), lambda b, h, i: (b, i, h))
    kv_out = pl.BlockSpec((None, T, 128), lambda b, h, i: (b, 0, h))
    tab = pl.BlockSpec((T, RET_DK), lambda b, h, i: (0, 0))
    return pl.pallas_call(
        body, name="ret_bwd", grid=(B, 4, n),
        in_specs=[dec_spec, dec_spec, q_spec, k_spec, v_spec, g_spec, gn_spec, tile,
                  pl.BlockSpec((None, TQ, 128), lambda b, h, i: (b, i, 4 + h)), tab, tab, st_spec, st_spec],
        out_specs=(tile, tile, kv_out, kv_out,
                   pl.BlockSpec((None, 1, 128), lambda b, h, i: (b, 0, h)),
                   pl.BlockSpec((None, None, 8, 128), lambda b, h, i: (b, h, 0, 0))),
        out_shape=(jax.ShapeDtypeStruct((B, L, 512), BF16), jax.ShapeDtypeStruct((B, L, 512), BF16),
                   jax.ShapeDtypeStruct((B, T, 512), F32), jax.ShapeDtypeStruct((B, T, 512), F32),
                   jax.ShapeDtypeStruct((B, 1, 512), F32), jax.ShapeDtypeStruct((B, 4, 8, 128), F32)),
        scratch_shapes=[pltpu.VMEM((n, RET_DK, RET_DK), F32), pltpu.VMEM((n, RET_DK, RET_DK), F32)],
        compiler_params=_params(("arbitrary",) * 3))(
            dec_f, dec_b, P, P, P, P, ret_norm_g, o_ret, dY, cos2, sin2, sf, sb)


def _out_call(y_na, y_ret, x, target, mod, final_g, wout_f):
    B, L, _ = x.shape

    def body(yn_ref, yr_ref, x_ref, t_ref, mod_ref, gf_ref, w_ref, dy_ref, dx2_ref, dw_ref, sm_ref):
        b, i = pl.program_id(0), pl.program_id(1)

        @pl.when((b == 0) & (i == 0))
        def _():
            dw_ref[...] = jnp.zeros_like(dw_ref)
            sm_ref[...] = jnp.zeros_like(sm_ref)

        gate = mod_ref[pl.ds(b, 1), 2 * D:3 * D]
        gf = gf_ref[...]
        yn, yr = yn_ref[...], yr_ref[...]
        ylat = _dot(yn, w_ref[0:512, :]) + _dot(yr, w_ref[512:1024, :])
        x2 = x_ref[...] + gate * ylat
        r = lax.rsqrt(jnp.mean(x2 * x2, axis=-1, keepdims=True) + EPS)
        xr = x2 * r
        err = xr * gf - t_ref[...]
        sm_ref[1:2, :] += jnp.sum(err * err, axis=0, keepdims=True)
        dout = err * (1.0 / D)
        sm_ref[0:1, :] += jnp.sum(dout * xr, axis=0, keepdims=True)
        gd = dout * gf
        dx2 = r * (gd - xr * jnp.mean(gd * xr, axis=-1, keepdims=True))
        dx2_ref[...] = dx2
        sm_ref[pl.ds(2 + b, 1), :] += jnp.sum(dx2 * ylat, axis=0, keepdims=True)
        dyl = (gate * dx2).astype(BF16)
        dy_ref[:, 0:512] = _dot_nt(dyl, w_ref[0:512, :]).astype(BF16)
        dy_ref[:, 512:1024] = _dot_nt(dyl, w_ref[512:1024, :]).astype(BF16)
        dw_ref[0:512, :] += _dot_tn(yn, dyl)
        dw_ref[512:1024, :] += _dot_tn(yr, dyl)

    half = pl.BlockSpec((None, TQ, 512), lambda b, i: (b, i, 0))
    full = pl.BlockSpec((None, TQ, D), lambda b, i: (b, i, 0))
    return pl.pallas_call(
        body, name="out_proj_loss", grid=(B, L // TQ),
        in_specs=[half, half, full, full,
                  pl.BlockSpec((8, 3 * D), lambda b, i: (0, 0)),
                  pl.BlockSpec((1, D), lambda b, i: (0, 0)),
                  pl.BlockSpec((D, D), lambda b, i: (0, 0))],
        out_specs=(full, full, pl.BlockSpec((D, D), lambda b, i: (0, 0)),
                   pl.BlockSpec((8, D), lambda b, i: (0, 0))),
        out_shape=(jax.ShapeDtypeStruct((B, L, D), BF16), jax.ShapeDtypeStruct((B, L, D), F32),
                   jax.ShapeDtypeStruct((D, D), F32), jax.ShapeDtypeStruct((8, D), F32)),
        compiler_params=_params(("arbitrary",) * 2))(y_na, y_ret, x, target, mod, final_g, wout_f)


def _dh_call(dsec, win_f, x, ctx, dx2, mod, norm_g, cp_in, cp_out):
    B, L, _ = x.shape
    LC = ctx.shape[1]
    nl = L // TQ

    def body(d0, d1, d2, d3, d4, d5, d6, d7, w_ref, x_ref, ctx_ref, dx2_ref, mod_ref, g_ref, cpi_ref, cpo_ref,
             gx_ref, sm_ref, sli_ref, slo_ref, ssem, rsem, lsem):
        drefs = (d0, d1, d2, d3, d4, d5, d6, d7)
        b, t = pl.program_id(0), pl.program_id(1)
        is_lat = t < nl

        @pl.when((b == 0) & (t == 0))
        def _():
            sm_ref[...] = jnp.zeros_like(sm_ref)

        def dh_of(secs):
            acc = jnp.zeros((TQ, D), F32)
            for sec in secs:
                s, half = divmod(sec, 2)
                acc = acc + _dot_nt(drefs[sec][...].astype(BF16), w_ref[s, :, half * 512:(half + 1) * 512])
            return acc

        def norm_bwd(dh, xt, mrow):
            scale = mrow[:, D:2 * D]
            g = g_ref[...]
            rstd = lax.rsqrt(jnp.mean(xt * xt, axis=-1, keepdims=True) + EPS)
            xn = xt * rstd
            dshift = jnp.sum(dh, axis=0, keepdims=True)
            dscale = jnp.sum(dh * (xn * g), axis=0, keepdims=True)
            dhn = dh * (1.0 + scale)
            sm_ref[0:1, :] += jnp.sum(dhn * xn, axis=0, keepdims=True)
            dxn = dhn * g
            dx = rstd * (dxn - xn * jnp.mean(dxn * xn, axis=-1, keepdims=True))
            return dshift, dscale, dx

        @pl.when(is_lat)
        def _():
            dshift, dscale, dx = norm_bwd(dh_of(range(8)), x_ref[...], mod_ref[pl.ds(b, 1), :])
            sm_ref[pl.ds(3 + b, 1), :] += dshift
            sm_ref[pl.ds(3 + B + b, 1), :] += dscale
            gx_ref[...] = dx2_ref[...] + dx

        @pl.when(jnp.logical_not(is_lat))
        def _():
            dshift, dscale, _ = norm_bwd(dh_of((1, 2, 5, 6)), ctx_ref[...], mod_ref[B:B + 1, :])
            sm_ref[1:2, :] += dshift
            sm_ref[2:3, :] += dscale

        mx, my, mc = _mesh_pos()
        s = 2 * mx + my
        cps, sls = (cpi_ref, cpo_ref), (sli_ref, slo_ref)
        own = [pltpu.make_async_copy(cps[a].at[s], sls[a].at[s], lsem.at[a]) for a in range(2)]
        sends, recvs, k = [], [], 0
        for px, py in _other_chips(mx, my):
            ps = 2 * px + py
            for a in range(2):
                sends.append(_remote(cps[a].at[ps], sls[a].at[s], ssem, rsem, k, (px, py, mc)))
                recvs.append(_remote(cps[a].at[s], sls[a].at[ps], ssem, rsem, k, (px, py, mc)))
                k += 1

        @pl.when((b == 0) & (t == 0))
        def _():
            for cp in own + sends:
                cp.start()

        @pl.when((b == B - 1) & (t == nl))
        def _():
            _finish(own, sends, recvs)

    lat = lambda b, t: (b, jnp.minimum(t, nl - 1), 0)
    tok = lambda b, t: (b, t, 0)
    sec_specs = [pl.BlockSpec((None, TQ, 512), lat if sec in (0, 3, 4, 7) else tok) for sec in range(8)]
    return pl.pallas_call(
        body, name="dh_norm_bwd", grid=(B, nl + 1),
        in_specs=sec_specs + [
            pl.BlockSpec((N_SHARD, D, D), lambda b, t: (0, 0, 0)),
            pl.BlockSpec((None, TQ, D), lat),
            pl.BlockSpec((None, LC, D), lambda b, t: (b, 0, 0)),
            pl.BlockSpec((None, TQ, D), lat),
            pl.BlockSpec((8, 3 * D), lambda b, t: (0, 0)),
            pl.BlockSpec((1, D), lambda b, t: (0, 0)), ANY, ANY],
        out_specs=(pl.BlockSpec((None, TQ, D), lat), pl.BlockSpec((8, D), lambda b, t: (0, 0)), ANY, ANY),
        out_shape=(jax.ShapeDtypeStruct((B, L, D), F32), jax.ShapeDtypeStruct((8, D), F32),
                   jax.ShapeDtypeStruct(cp_in.shape, cp_in.dtype), jax.ShapeDtypeStruct(cp_out.shape, cp_out.dtype)),
        scratch_shapes=[pltpu.SemaphoreType.DMA((6,)), pltpu.SemaphoreType.DMA((6,)),
                        pltpu.SemaphoreType.DMA((2,))],
        compiler_params=_params(("arbitrary",) * 2))(*dsec, win_f, x, ctx, dx2, mod, norm_g, cp_in, cp_out)


def _dw_call(dsec, h, L):
    B, T, _ = h.shape
    nl = L // TQ

    def body(d0, d1, d2, d3, d4, d5, d6, d7, h_ref, dw_ref, acc_ref):
        drefs = (d0, d1, d2, d3, d4, d5, d6, d7)
        b, t = pl.program_id(0), pl.program_id(1)

        @pl.when((b == 0) & (t == 0))
        def _():
            acc_ref[...] = jnp.zeros_like(acc_ref)

        hb = h_ref[...]

        def add(secs):
            for sec in secs:
                s, half = divmod(sec, 2)
                acc_ref[s, :, half * 512:(half + 1) * 512] += _dot_tn(hb, drefs[sec][...].astype(BF16))

        @pl.when(t < nl)
        def _():
            add(range(8))

        @pl.when(t >= nl)
        def _():
            add((1, 2, 5, 6))

        @pl.when((b == B - 1) & (t == nl))
        def _():
            dw_ref[...] = acc_ref[...].astype(BF16)

    lat = lambda b, t: (b, jnp.minimum(t, nl - 1), 0)
    tok = lambda b, t: (b, t, 0)
    sec_specs = [pl.BlockSpec((None, TQ, 512), lat if sec in (0, 3, 4, 7) else tok) for sec in range(8)]
    return pl.pallas_call(
        body, name="dw_in", grid=(B, nl + 1),
        in_specs=sec_specs + [pl.BlockSpec((None, TQ, D), tok)],
        out_specs=pl.BlockSpec((N_SHARD, D, D), lambda b, t: (0, 0, 0)),
        out_shape=jax.ShapeDtypeStruct((N_SHARD, D, D), BF16),
        scratch_shapes=[pltpu.VMEM((N_SHARD, D, D), F32)],
        compiler_params=_params(("arbitrary",) * 2, vmem_mb=56))(*dsec, h)


def _mesh_pos():
    return lax.axis_index("x"), lax.axis_index("y"), lax.axis_index("c")


def _flip(v, f):
    return 1 - v if f else v


def _remote(src, dst, ssem, rsem, k, peer):
    return pltpu.make_async_remote_copy(src_ref=src, dst_ref=dst, send_sem=ssem.at[k], recv_sem=rsem.at[k],
                                        device_id=peer, device_id_type=MESH)


def _other_chips(x, y):
    return [(_flip(x, fx), _flip(y, fy)) for fx, fy in ((1, 0), (0, 1), (1, 1))]


def _all_to_all_small(src, dst_all, ssem, rsem, k0, x, y, cc):
    me = 4 * x + 2 * y + cc
    sends, recvs = [], []
    for f in range(1, N_DEV):
        px, py, pc = _flip(x, f & 4), _flip(y, f & 2), _flip(cc, f & 1)
        sends.append(_remote(src, dst_all.at[me], ssem, rsem, k0 + f - 1, (px, py, pc)))
        recvs.append(_remote(src, dst_all.at[4 * px + 2 * py + pc], ssem, rsem, k0 + f - 1, (px, py, pc)))
    return sends, recvs


def _finish(local, sends, recvs):
    for cp in recvs:
        cp.wait_recv()
    for cp in sends:
        cp.wait_send()
    for cp in local:
        cp.wait()


def _gather_call(wout_b, wada_b, c, rpb_flat):
    arrs = (wout_b, wada_b)
    na = len(arrs)
    hrs = [a.shape[0] // 2 for a in arrs]

    def body(wout, wada, c_ref, r_ref, wout_f, wada_f, c_all, bias_ref, et_ref, ssem, rsem, lsem):
        x, y, cc = _mesh_pos()
        s, me = 2 * x + y, 4 * x + 2 * y + cc
        sib = (x, y, 1 - cc)
        srcs, dsts = (wout, wada), (wout_f, wada_f)

        def half(a, shard, hc):
            return dsts[a].at[shard, pl.ds(hc * hrs[a], hrs[a])]

        local = [pltpu.make_async_copy(srcs[a], dsts[a].at[s], lsem.at[a]) for a in range(na)]
        local.append(pltpu.make_async_copy(c_ref, c_all.at[me], lsem.at[na]))
        ici_send, ici_recv, fwd_send, fwd_recv, k = [], [], [], [], 0
        for px, py in _other_chips(x, y):
            ps = 2 * px + py
            for a in range(na):
                mine = srcs[a].at[pl.ds(cc * hrs[a], hrs[a])]
                ici_send.append(_remote(mine, half(a, s, cc), ssem, rsem, k, (px, py, cc)))
                ici_recv.append(_remote(mine, half(a, ps, cc), ssem, rsem, k, (px, py, cc)))
                fwd_send.append(_remote(half(a, ps, cc), half(a, ps, cc), ssem, rsem, 3 * na + k, sib))
                fwd_recv.append(_remote(half(a, ps, 1 - cc), half(a, ps, 1 - cc), ssem, rsem, 3 * na + k, sib))
                k += 1
        c_send, c_recv = _all_to_all_small(c_ref, c_all, ssem, rsem, 6 * na, x, y, cc)
        for cp in local + ici_send + c_send:
            cp.start()
        _bias_body(r_ref, bias_ref, et_ref)
        for got, fwd in zip(ici_recv, fwd_send):
            got.wait_recv()
            fwd.start()
        _finish(local, ici_send + fwd_send + c_send, fwd_recv + c_recv)

    return pl.pallas_call(
        body, name="weight_gather",
        in_specs=[pl.BlockSpec(memory_space=pltpu.VMEM)] * 3 + [pl.BlockSpec(memory_space=pltpu.SMEM)],
        out_specs=(pl.BlockSpec(memory_space=pltpu.VMEM),) * 4,
        out_shape=tuple(jax.ShapeDtypeStruct((N_SHARD,) + a.shape, a.dtype) for a in arrs)
        + (jax.ShapeDtypeStruct((N_DEV,) + c.shape, c.dtype),
           jax.ShapeDtypeStruct((rpb_flat.shape[0], 3, TQ, KW), F32)),
        scratch_shapes=[pltpu.VMEM((15, GRID_W, GRID_W), F32),
                        pltpu.SemaphoreType.DMA((6 * na + 7,)), pltpu.SemaphoreType.DMA((6 * na + 7,)),
                        pltpu.SemaphoreType.DMA((na + 1,))],
        compiler_params=pltpu.CompilerParams(vmem_limit_bytes=56 << 20))(wout_b, wada_b, c, rpb_flat)


VROWS = 32


def _grad_halves_call(dwin_b, dwout_b, dbias, dlg):
    arrs = (dwin_b, dwout_b)
    hrs = [a.shape[1] // 2 for a in arrs]

    def body(din, dout, db_ref, dlg_ref, cp_in, cp_out, drpb_ref, dlgo_ref, got_in, got_out, p_ref, ssem, rsem):
        x, y, cc = _mesh_pos()
        sib = (x, y, 1 - cc)
        srcs, gots, cps = (din, dout), (got_in, got_out), (cp_in, cp_out)
        halves = [_remote(srcs[a].at[:, pl.ds((1 - cc) * hrs[a], hrs[a])], gots[a], ssem, rsem, a, sib)
                  for a in range(2)]
        for cp in halves:
            cp.start()
        _small_reduce_body(db_ref, dlg_ref, drpb_ref, dlgo_ref, p_ref)
        for cp in halves:
            cp.wait_recv()
        for a in range(2):
            for j in range(N_SHARD):
                def add(i, carry, a=a, j=j):
                    r = pl.multiple_of(i * VROWS, VROWS)
                    mine = srcs[a][j, pl.ds(pl.multiple_of(cc * hrs[a] + r, VROWS), VROWS), :].astype(F32)
                    cps[a][j, pl.ds(r, VROWS), :] = (
                        mine + gots[a][j, pl.ds(r, VROWS), :].astype(F32)).astype(BF16)
                    return carry
                lax.fori_loop(0, hrs[a] // VROWS, add, 0)
        for cp in halves:
            cp.wait_send()

    vmem = pl.BlockSpec(memory_space=pltpu.VMEM)
    half_shapes = [(N_SHARD, hrs[a], arrs[a].shape[2]) for a in range(2)]
    return pl.pallas_call(
        body, name="grad_halves",
        in_specs=[vmem] * 4, out_specs=(vmem,) * 4,
        out_shape=(jax.ShapeDtypeStruct(half_shapes[0], BF16), jax.ShapeDtypeStruct(half_shapes[1], BF16),
                   jax.ShapeDtypeStruct((dbias.shape[0], 16, 32), F32), jax.ShapeDtypeStruct((32, 128), F32)),
        scratch_shapes=[pltpu.VMEM(half_shapes[0], BF16), pltpu.VMEM(half_shapes[1], BF16),
                        pltpu.VMEM((32, GRID_W), F32),
                        pltpu.SemaphoreType.DMA((2,)), pltpu.SemaphoreType.DMA((2,))],
        compiler_params=pltpu.CompilerParams(vmem_limit_bytes=56 << 20))(dwin_b, dwout_b, dbias, dlg)


def _grad_finish_call(sl_in, sl_out, small):
    arrs = (sl_in, sl_out)

    def body(sin, sout, sm, gin, gout, sm_all, h_in, h_out, ssem, rsem, lsem):
        x, y, cc = _mesh_pos()
        me = 4 * x + 2 * y + cc
        sib = (x, y, 1 - cc)
        sls, hs, gs = (sin, sout), (h_in, h_out), (gin, gout)
        sm_send, sm_recv = _all_to_all_small(sm, sm_all, ssem, rsem, 2, x, y, cc)
        sm_own = pltpu.make_async_copy(sm, sm_all.at[me], lsem.at[0])
        for cp in sm_send + [sm_own]:
            cp.start()
        for a in range(2):
            def total(i, carry, a=a):
                rows = pl.ds(pl.multiple_of(i * VROWS, VROWS), VROWS)
                sl = sls[a]
                hs[a][rows, :] = ((sl[0, rows, :].astype(F32) + sl[1, rows, :].astype(F32))
                                  + sl[2, rows, :].astype(F32)) + sl[3, rows, :].astype(F32)
                return carry
            lax.fori_loop(0, arrs[a].shape[1] // VROWS, total, 0)
        mine = [pltpu.make_async_copy(hs[a], gs[a].at[cc], lsem.at[1 + a]) for a in range(2)]
        back = [_remote(hs[a], gs[a].at[cc], ssem, rsem, a, sib) for a in range(2)]
        back_recv = [_remote(hs[a], gs[a].at[1 - cc], ssem, rsem, a, sib) for a in range(2)]
        for cp in mine + back:
            cp.start()
        _finish(mine + [sm_own], back + sm_send, back_recv + sm_recv)

    vmem = pl.BlockSpec(memory_space=pltpu.VMEM)
    return pl.pallas_call(
        body, name="grad_finish",
        in_specs=[vmem] * 3, out_specs=(vmem,) * 3,
        out_shape=(jax.ShapeDtypeStruct((2,) + sl_in.shape[1:], F32),
                   jax.ShapeDtypeStruct((2,) + sl_out.shape[1:], F32),
                   jax.ShapeDtypeStruct((N_DEV,) + small.shape, F32)),
        scratch_shapes=[pltpu.VMEM(sl_in.shape[1:], F32), pltpu.VMEM(sl_out.shape[1:], F32),
                        pltpu.SemaphoreType.DMA((9,)), pltpu.SemaphoreType.DMA((9,)),
                        pltpu.SemaphoreType.DMA((3,))],
        compiler_params=pltpu.CompilerParams(vmem_limit_bytes=48 << 20))(sl_in, sl_out, small)


def _adamw(w, g, m, v):
    m = ADAM_B1 * m + (1.0 - ADAM_B1) * g
    v = ADAM_B2 * v + (1.0 - ADAM_B2) * (g * g)
    m_hat = m / (1.0 - ADAM_B1 ** ADAM_STEP)
    v_hat = v / (1.0 - ADAM_B2 ** ADAM_STEP)
    return -ADAM_LR * (m_hat / (jnp.sqrt(v_hat) + ADAM_EPS) + ADAM_WD * w), m, v


def _adam_call(w, m, v, g, name):
    R, C = w.shape
    tr = 256

    def body(w_ref, m_ref, v_ref, g_ref, d_ref, mo_ref, vo_ref):
        d_ref[...], mo_ref[...], vo_ref[...] = _adamw(w_ref[...], g_ref[...], m_ref[...], v_ref[...])

    spec = pl.BlockSpec((tr, C), lambda i: (i, 0))
    return pl.pallas_call(
        body, name=name, grid=(R // tr,), in_specs=[spec] * 4,
        out_specs=(spec,) * 3, out_shape=(jax.ShapeDtypeStruct((R, C), F32),) * 3,
        compiler_params=_params(("arbitrary",)))(w, m, v, g)


R_GF, R_NG, R_LOSS, R_RNG, R_LGF, R_LGB, R_SHIFT, R_SCALE, R_GATE, R_SHIFT_C, R_SCALE_C, R_RNG2, R_RPB = (
    0, 1, 2, 3, 4, 5, 6, 8, 10, 12, 13, 14, 16)
W_GF, W_NG, W_CCTX, W_RNG, W_DF, W_DB, W_BADA, W_RPB = 0, 1, 2, 3, 4, 5, 6, 9


def _small_final_call(sm_all, c_t, c_ctx, wada_f, wada, m_ada, v_ada, wsm, msm, vsm, B):
    ws = wada.shape[1]
    NB = N_DEV * B

    def body(sm_ref, ct_ref, cctx_ref, wf_ref, wa_ref, ma_ref, va_ref, w_ref, m_ref, v_ref,
             g_ref, d_ref, mo_ref, vo_ref, ga_ref, da_ref, mao_ref, vao_ref, loss_ref, dmod_ref):
        x, y, _ = _mesh_pos()
        s = 2 * x + y
        tot = sm_ref[0]
        for dv in range(1, N_DEV):
            tot = tot + sm_ref[dv]
        w = w_ref[...]
        for dv in range(N_DEV):
            for b in range(B):
                r = dv * B + b
                for part, row in enumerate((R_SHIFT, R_SCALE, R_GATE)):
                    dmod_ref[r:r + 1, part * D:(part + 1) * D] = sm_ref[dv, row + b:row + b + 1, :]
        dmod_ref[NB:NB + 1, 0:D] = tot[R_SHIFT_C:R_SHIFT_C + 1, :]
        dmod_ref[NB:NB + 1, D:2 * D] = tot[R_SCALE_C:R_SCALE_C + 1, :]
        dmod_ref[NB:NB + 1, 2 * D:3 * D] = jnp.zeros((1, D), F32)
        dmod_ref[NB + 1:, :] = jnp.zeros((dmod_ref.shape[0] - NB - 1, 3 * D), F32)
        dmod = dmod_ref[...]
        cc = cctx_ref[...]
        scc = _sigmoid(cc)
        ct = ct_ref[...]
        act_t = ct * _sigmoid(ct)
        dmc = dmod[NB:NB + 1, :].astype(BF16)
        dact = jnp.zeros((1, D), F32)
        for sh in range(N_SHARD):
            dact = dact + _dot_nt(dmc[:, sh * ws:(sh + 1) * ws], wf_ref[sh])
        g = jnp.zeros((16, D), F32)
        rows = lax.broadcasted_iota(jnp.int32, (16, D), 0)

        def put(g, row, val):
            return jnp.where(rows == row, val, g)

        g = put(g, W_GF, tot[R_GF:R_GF + 1, :])
        g = put(g, W_NG, tot[R_NG:R_NG + 1, :])
        g = put(g, W_CCTX, dact * (scc * (1.0 + cc * (1.0 - scc))))
        g = put(g, W_RNG, tot[R_RNG:R_RNG + 1, :] + tot[R_RNG2:R_RNG2 + 1, :])
        g = put(g, W_DF, tot[R_LGF:R_LGF + 1, :] * (-jnp.exp(w[W_DF:W_DF + 1, :])))
        g = put(g, W_DB, tot[R_LGB:R_LGB + 1, :] * (-jnp.exp(w[W_DB:W_DB + 1, :])))
        db = jnp.sum(dmod, axis=0, keepdims=True)
        for part in range(3):
            g = put(g, W_BADA + part, db[:, part * D:(part + 1) * D])
        for part in range(4):
            g = put(g, W_RPB + part, tot[R_RPB + part:R_RPB + part + 1, :])
        g_ref[...] = g
        d_ref[...], mo_ref[...], vo_ref[...] = _adamw(w, g, m_ref[...], v_ref[...])
        loss_ref[...] = jnp.broadcast_to(
            (0.5 / D) * jnp.sum(tot[R_LOSS:R_LOSS + 1, :], axis=1, keepdims=True), (8, 128))
        for sh in range(N_SHARD):
            @pl.when(s == sh)
            def _():
                ga = jnp.dot(act_t, dmod[:, sh * ws:(sh + 1) * ws], precision=HIGHEST,
                             preferred_element_type=F32)
                ga_ref[...] = ga
                da_ref[...], mao_ref[...], vao_ref[...] = _adamw(wa_ref[...], ga, ma_ref[...], va_ref[...])

    sh_small = jax.ShapeDtypeStruct((16, D), F32)
    sh_ada = jax.ShapeDtypeStruct(wada.shape, F32)
    return pl.pallas_call(
        body, name="small_final",
        out_shape=(sh_small,) * 4 + (sh_ada,) * 4 + (jax.ShapeDtypeStruct((8, 128), F32),),
        scratch_shapes=[pltpu.VMEM((NB + 8, 3 * D), F32)],
        compiler_params=_params(vmem_mb=56))(
            sm_all, c_t, c_ctx, wada_f, wada, m_ada, v_ada, wsm, msm, vsm)


def _local_step(order, x, c, ctx, c_ctx, norm_g, wada_f, b_ada, win_b, bias, dec_f, dec_b, ret_norm_g,
                wout_f, final_g, target):
    B, L, _ = x.shape
    LC = ctx.shape[1]
    assert B == 2
    cos2, sin2 = _rope_tables(L, LC)
    c8 = jnp.concatenate([c, c_ctx[None, :], jnp.zeros((8 - B - 1, D), F32)], axis=0)
    mod = _mod_call(c8, wada_f, b_ada)
    P, h, win_f = _inproj_gather_call(order, x, ctx, mod, norm_g, win_b, cos2, sin2)
    y_na, o_na = _na_fwd_call(P, bias, L, LC)
    sf, sb = _ret_states_call(P, dec_f, dec_b, L, LC)
    y_ret, o_ret = _retc_fwd_call(P, sf, sb, dec_f, dec_b, ret_norm_g, L)
    dY, dx2, dwout_p, sm_out = _out_call(y_na, y_ret, x, target, mod, final_g, wout_f.reshape(D, D))
    dnq, dng, dnk, dnv, dbias = _na_bwd_call(P, bias, dY, o_na, L, LC)
    drq, drg, drk, drv, dgn, dlg = _retc_bwd_call(P, sf, sb, dec_f, dec_b, ret_norm_g, o_ret, dY, cos2, sin2, L, LC)
    dsec = (dnq, dnk, dnv, dng, drq, drk, drv, drg)
    dwin_b = _dw_call(dsec, h, L)
    cp_in, cp_out, drpb, dlg_sum = _grad_halves_call(
        dwin_b, dwout_p.astype(BF16).reshape(N_SHARD, D // N_SHARD, D), dbias, dlg)
    grad_x, sm_dh, sl_in, sl_out = _dh_call(dsec, win_f, x, ctx, dx2, mod, norm_g, cp_in, cp_out)
    z = jnp.zeros((1, D), F32)
    pad = lambda v: jnp.pad(v.reshape(1, -1), ((0, 0), (0, D - v.size)))
    dlg_sum = dlg_sum.reshape(4, 8, 128)
    rpb_rows = jnp.pad(drpb[:, :15, :31].reshape(-1), (0, 4 * D - drpb.shape[0] * 465)).reshape(4, D)
    small = jnp.concatenate([
        sm_out[0:1], sm_dh[0:1], sm_out[1:2], pad(dgn[0]), pad(dlg_sum[:, 0, 0]), pad(dlg_sum[:, 1, 0]),
        sm_dh[3:5], sm_dh[5:7], sm_out[2:4], sm_dh[1:2], sm_dh[2:3], pad(dgn[1]), z, rpb_rows,
        jnp.zeros((SM_ROWS - 20, D), F32)], axis=0)
    return grad_x, sl_in, sl_out, small


def kernel(x, c, ctx, c_ctx, norm_g, w_ada, b_ada, w_in, na_rpb, ret_decay_fwd, ret_decay_bwd, ret_norm_g, w_out, final_norm_g, loss_target, m_c_ctx, m_norm_g, m_w_ada, m_b_ada, m_w_in, m_na_rpb, m_ret_decay_fwd, m_ret_decay_bwd, m_ret_norm_g, m_w_out, m_final_norm_g, v_c_ctx, v_norm_g, v_w_ada, v_b_ada, v_w_in, v_na_rpb, v_ret_decay_fwd, v_ret_decay_bwd, v_ret_norm_g, v_w_out, v_final_norm_g):
    B = x.shape[0]
    wout_f, wada_f, c_all, bias = _gather_call(
        w_out[0].astype(BF16), w_ada[0].astype(BF16), c, na_rpb[0].reshape(na_rpb.shape[1], -1))
    mx, my = lax.axis_index("x"), lax.axis_index("y")
    order = jnp.stack([2 * mx + my, 2 * (1 - mx) + my, 2 * mx + (1 - my),
                       2 * (1 - mx) + (1 - my)]).astype(jnp.int32)
    grad_x, sl_in, sl_out, small = _local_step(
        order, x, c, ctx, c_ctx, norm_g, wada_f, b_ada, w_in[0].astype(BF16), bias, ret_decay_fwd,
        ret_decay_bwd, ret_norm_g, wout_f, final_norm_g.reshape(1, D), loss_target)
    gin, gout, sm_all = _grad_finish_call(sl_in, sl_out, small)
    g_win, g_wout = gin.reshape(w_in.shape[1:]), gout.reshape(w_out.shape[1:])
    d_win, nm_win, nv_win = _adam_call(w_in[0], m_w_in[0], v_w_in[0], g_win, "adam_w_in")
    d_wout, nm_wout, nv_wout = _adam_call(w_out[0], m_w_out[0], v_w_out[0], g_wout, "adam_w_out")

    def pack(gf, ng, cc, rng, df, db, bada, rpb):
        pad = lambda v: jnp.pad(v.reshape(1, -1), ((0, 0), (0, D - v.size)))
        return jnp.concatenate([
            gf.reshape(1, D), ng.reshape(1, D), cc.reshape(1, D), pad(rng), pad(df), pad(db),
            bada.reshape(3, D), jnp.pad(rpb.reshape(-1), (0, 4 * D - rpb.size)).reshape(4, D),
            jnp.zeros((3, D), F32)], axis=0)

    wsm = pack(final_norm_g, norm_g, c_ctx, ret_norm_g, ret_decay_fwd, ret_decay_bwd, b_ada, na_rpb)
    msm = pack(m_final_norm_g, m_norm_g, m_c_ctx, m_ret_norm_g, m_ret_decay_fwd, m_ret_decay_bwd, m_b_ada, m_na_rpb)
    vsm = pack(v_final_norm_g, v_norm_g, v_c_ctx, v_ret_norm_g, v_ret_decay_fwd, v_ret_decay_bwd, v_b_ada, v_na_rpb)
    c_t = jnp.concatenate([c_all.reshape(N_DEV * B, D), c_ctx.reshape(1, D), jnp.zeros((7, D), F32)], axis=0).T
    outs = _small_final_call(sm_all, c_t, c_ctx.reshape(1, D), wada_f,
                             w_ada[0], m_w_ada[0], v_w_ada[0], wsm, msm, vsm, B)
    smalls, adas, loss = outs[0:4], outs[4:8], outs[8][0, 0]

    def unpack(p):
        rw = ret_norm_g.shape[1]
        return dict(
            final_norm_g=p[W_GF], norm_g=p[W_NG:W_NG + 1], c_ctx=p[W_CCTX], ret_norm_g=p[W_RNG:W_RNG + 1, :rw],
            ret_decay_fwd=p[W_DF:W_DF + 1, :4], ret_decay_bwd=p[W_DB:W_DB + 1, :4],
            b_ada=p[W_BADA:W_BADA + 3].reshape(1, 3 * D),
            na_rpb=p[W_RPB:W_RPB + 4].reshape(-1)[:na_rpb.size].reshape(na_rpb.shape))

    res = []
    for p, ada, win_o, wout_o in zip(smalls, adas, (g_win, d_win, nm_win, nv_win),
                                     (g_wout, d_wout, nm_wout, nv_wout)):
        u = unpack(p)
        res.append([u["c_ctx"], u["norm_g"], ada[None], u["b_ada"], win_o[None], u["na_rpb"],
                    u["ret_decay_fwd"], u["ret_decay_bwd"], u["ret_norm_g"], wout_o[None], u["final_norm_g"]])
    return (loss, grad_x, *res[0], *res[1], *res[2], *res[3])
```

```python
import numpy as np
import jax
import jax.numpy as jnp
from jax import lax
from jax.experimental import pallas as pl
from jax.experimental.pallas import tpu as pltpu

F32 = jnp.float32
BF16 = jnp.bfloat16
HIGHEST = lax.Precision.HIGHEST

D = 1024
GRID_W = 64
NA_DH = 64
RET_DK = 128
ROPE_BASE = 10000.0
EPS = 1e-6
NEG = -1e30
TQ = 256
KW = 12 * GRID_W
N_SHARD = 4
N_DEV = 8
SM_ROWS = 24

ADAM_LR = 0.001
ADAM_B1 = 0.9
ADAM_B2 = 0.999
ADAM_EPS = 1e-08
ADAM_WD = 0.01
ADAM_STEP = 10

MESH = pl.DeviceIdType.MESH
ANY = pl.BlockSpec(memory_space=pl.ANY)


def _params(sem=None, vmem_mb=48):
    return pltpu.CompilerParams(dimension_semantics=sem, vmem_limit_bytes=vmem_mb << 20)


def _dot(a, b):
    return jnp.dot(a, b, preferred_element_type=F32)


def _dot_nt(a, b):
    return lax.dot_general(a, b, (((1,), (1,)), ((), ())), preferred_element_type=F32)


def _dot_tn(a, b):
    return lax.dot_general(a, b, (((0,), (0,)), ((), ())), preferred_element_type=F32)


def _sigmoid(x):
    return 1.0 / (1.0 + jnp.exp(-x))


def _rope_tables(L, LC):
    half = RET_DK // 2
    nf = half // 2
    t = np.arange(L)
    row = (t // GRID_W).astype(np.float32)
    col = (t % GRID_W).astype(np.float32)
    inv = (np.float32(ROPE_BASE) ** (-np.arange(nf, dtype=np.float32) / np.float32(nf))).astype(np.float32)
    ang = np.concatenate([row[:, None] * inv, col[:, None] * inv], axis=-1).astype(np.float32)
    cos, sin = np.cos(ang).astype(np.float32), np.sin(ang).astype(np.float32)
    cos2 = np.concatenate([cos, cos], axis=-1)
    sin2 = np.concatenate([-sin, sin], axis=-1)
    cos2 = np.concatenate([cos2, np.ones((LC, RET_DK), np.float32)], axis=0)
    sin2 = np.concatenate([sin2, np.zeros((LC, RET_DK), np.float32)], axis=0)
    return jnp.asarray(cos2), jnp.asarray(sin2)


def _mod_call(c8, wada_f, b_ada):
    ws = wada_f.shape[2]

    def body(c_ref, w_ref, b_ref, o_ref):
        a = c_ref[...]
        a = (a * _sigmoid(a)).astype(BF16)
        for s in range(N_SHARD):
            o_ref[:, s * ws:(s + 1) * ws] = _dot(a, w_ref[s]) + b_ref[:, s * ws:(s + 1) * ws]

    return pl.pallas_call(
        body, name="ada_mod", out_shape=jax.ShapeDtypeStruct((8, 3 * D), F32),
        compiler_params=_params())(c8, wada_f, b_ada)


def _dc_masks():
    cq = lax.broadcasted_iota(jnp.int32, (GRID_W, GRID_W), 0)
    ck = lax.broadcasted_iota(jnp.int32, (GRID_W, GRID_W), 1)
    dc = jnp.clip(ck - cq + 15, 0, 30)
    c0 = jnp.clip(cq - 8, 0, GRID_W - 16)
    col_ok = (ck >= c0) & (ck < c0 + 16)
    return dc, col_ok


def _bias_blocks():
    out = []
    for typ, delta in enumerate((4, 0, -4)):
        for rq in range(4):
            for rkk in range(12):
                dr = rkk + delta - rq - 4
                if typ == 0:
                    ok = -rq <= dr <= 7 - rq
                elif typ == 1:
                    ok = -4 <= dr <= 3
                else:
                    ok = -4 - rq <= dr <= 3 - rq
                out.append((typ, rq, rkk, dr if ok else None))
    return out


def _bias_body(r_ref, bias_ref, et_ref):
    dc, col_ok = _dc_masks()
    masks = [(dc == j).astype(F32) for j in range(31)]

    def per_h(h, carry):
        for dr in range(15):
            t = jnp.zeros((GRID_W, GRID_W), F32)
            for j in range(31):
                t = t + masks[j] * r_ref[h, dr * 31 + j]
            et_ref[dr] = jnp.where(col_ok, t, NEG)
        neg = jnp.full((GRID_W, GRID_W), NEG, F32)
        for typ, rq, rkk, dr in _bias_blocks():
            blk = neg if dr is None else et_ref[dr + 7]
            bias_ref[h, typ, rq * 64:(rq + 1) * 64, rkk * 64:(rkk + 1) * 64] = blk
        return carry

    lax.fori_loop(0, bias_ref.shape[0], per_h, 0)


def _small_reduce_body(db_ref, dlg_ref, drpb_ref, dlgo_ref, p_ref):
    dc, _ = _dc_masks()
    masks = [(dc == j).astype(F32) for j in range(31)]
    ones = jnp.ones((8, GRID_W), F32)
    p_ref[...] = jnp.zeros_like(p_ref)
    drpb_ref[...] = jnp.zeros_like(drpb_ref)

    def per_h(h, carry):
        acc = {}
        for typ, rq, rkk, dr in _bias_blocks():
            if dr is None:
                continue
            blk = db_ref[h, typ, rq * 64:(rq + 1) * 64, rkk * 64:(rkk + 1) * 64]
            acc[dr] = blk if dr not in acc else acc[dr] + blk
        for dr in range(-7, 8):
            t = acc[dr]
            for j in range(31):
                p_ref[j:j + 1, :] = jnp.sum(t * masks[j], axis=0, keepdims=True)
            red = lax.dot_general(ones, p_ref[...], (((1,), (1,)), ((), ())),
                                  precision=HIGHEST, preferred_element_type=F32)
            drpb_ref[h, dr + 7:dr + 8, :] = red[0:1, :]
        return carry

    lax.fori_loop(0, db_ref.shape[0], per_h, 0)
    x = dlg_ref[0]
    for b in range(1, dlg_ref.shape[0]):
        x = x + dlg_ref[b]
    x = x.reshape(4 * 8, x.shape[-1])
    dlgo_ref[...] = jnp.dot(x, jnp.ones((x.shape[-1], 128), F32), precision=HIGHEST,
                            preferred_element_type=F32)


def _inproj_gather_call(order, x, ctx, mod, norm_g, win_b, cos2, sin2):
    B, L, _ = x.shape
    LC = ctx.shape[1]
    T = L + LC
    nl, nt = L // TQ, T // TQ
    assert LC == TQ and L % TQ == 0
    kscale = RET_DK ** -0.5
    HR = D // 2

    def body(ord_ref, x_ref, ctx_ref, mod_ref, g_ref, wown_ref, cos_ref, sin_ref, p_ref, h_ref, wf_ref,
             w_all, hs_ref, ssem, rsem, lsem):
        j, b, t = pl.program_id(0), pl.program_id(1), pl.program_id(2)
        first = (b == 0) & (t == 0)
        mx, my, mc = _mesh_pos()
        s = 2 * mx + my
        sib = (mx, my, 1 - mc)
        own = pltpu.make_async_copy(wown_ref, w_all.at[s], lsem.at[0])
        ici_send, ici_recv, fwd_send, fwd_recv, outs = [], [], [], [], [
            pltpu.make_async_copy(w_all.at[s], wf_ref.at[s], lsem.at[1])]
        for k, (px, py) in enumerate(_other_chips(mx, my)):
            ps = 2 * px + py
            mine = w_all.at[s, pl.ds(mc * HR, HR)]
            ici_send.append(_remote(mine, w_all.at[s, pl.ds(mc * HR, HR)], ssem, rsem, k, (px, py, mc)))
            ici_recv.append(_remote(mine, w_all.at[ps, pl.ds(mc * HR, HR)], ssem, rsem, k, (px, py, mc)))
            got = w_all.at[ps, pl.ds(mc * HR, HR)]
            fwd_send.append(_remote(got, got, ssem, rsem, 3 + k, sib))
            theirs = w_all.at[ps, pl.ds((1 - mc) * HR, HR)]
            fwd_recv.append(_remote(theirs, theirs, ssem, rsem, 3 + k, sib))
            outs.append(pltpu.make_async_copy(w_all.at[ps], wf_ref.at[ps], lsem.at[2 + k]))

        @pl.when(first & (j == 0))
        def _():
            own.start()
            own.wait()
            for cp in ici_send:
                cp.start()
            outs[0].start()

        for k in range(3):
            @pl.when(first & (j == k + 1))
            def _(k=k):
                ici_recv[k].wait_recv()
                fwd_send[k].start()
                fwd_recv[k].wait_recv()
                outs[1 + k].start()

        tile = b * nt + t

        @pl.when(j == 0)
        def _():
            is_lat = t < nl
            xt = jnp.where(is_lat, x_ref[...], ctx_ref[...])
            mrow = mod_ref[pl.ds(jnp.where(is_lat, b, B), 1), :]
            shift, scale = mrow[:, 0:D], mrow[:, D:2 * D]
            rstd = lax.rsqrt(jnp.mean(xt * xt, axis=-1, keepdims=True) + EPS)
            h0 = ((xt * rstd * g_ref[...]) * (1.0 + scale) + shift).astype(BF16)
            h_ref[...] = h0
            hs_ref[tile] = h0

        hb = hs_ref[tile]
        cs, sn = cos_ref[...], sin_ref[...]
        shard = ord_ref[j]
        for sh in range(N_SHARD):
            @pl.when(shard == sh)
            def _(sh=sh):
                for half in range(2):
                    sec = 2 * sh + half
                    acc = _dot(hb, w_all[sh, :, half * 512:(half + 1) * 512])
                    if sec == 0:
                        acc = acc * (NA_DH ** -0.5)
                    if sec in (4, 5):
                        for q in range(4):
                            a = acc[:, q * 128:(q + 1) * 128]
                            r = a * cs + pltpu.roll(a, 64, 1) * sn
                            if sec == 5:
                                r = r * kscale
                            p_ref[:, half * 512 + q * 128:half * 512 + (q + 1) * 128] = r.astype(BF16)
                    else:
                        p_ref[:, half * 512:(half + 1) * 512] = acc.astype(BF16)

        @pl.when((j == N_SHARD - 1) & (b == B - 1) & (t == nt - 1))
        def _():
            _finish(outs, ici_send + fwd_send, [])

    tok = lambda j, b, t, o: (jnp.where(j == 0, b, B - 1), jnp.where(j == 0, jnp.minimum(t, nl - 1), nl - 1), 0)
    grid_spec = pltpu.PrefetchScalarGridSpec(
        num_scalar_prefetch=1, grid=(N_SHARD, B, nt),
        in_specs=[
            pl.BlockSpec((None, TQ, D), tok),
            pl.BlockSpec((None, TQ, D), lambda j, b, t, o: (jnp.where(j == 0, b, B - 1), 0, 0)),
            pl.BlockSpec((8, 3 * D), lambda j, b, t, o: (0, 0)),
            pl.BlockSpec((1, D), lambda j, b, t, o: (0, 0)),
            ANY,
            pl.BlockSpec((TQ, RET_DK), lambda j, b, t, o: (t, 0)),
            pl.BlockSpec((TQ, RET_DK), lambda j, b, t, o: (t, 0)),
        ],
        out_specs=(pl.BlockSpec((None, TQ, D), lambda j, b, t, o: (b, t, o[j])),
                   pl.BlockSpec((None, TQ, D), lambda j, b, t, o: (
                       jnp.where(j == 0, b, B - 1), jnp.where(j == 0, t, nt - 1), 0)), ANY),
        scratch_shapes=[pltpu.VMEM((N_SHARD, D, D), BF16), pltpu.VMEM((B * nt, TQ, D), BF16),
                        pltpu.SemaphoreType.DMA((6,)), pltpu.SemaphoreType.DMA((6,)),
                        pltpu.SemaphoreType.DMA((5,))])
    return pl.pallas_call(
        body, name="in_proj", grid_spec=grid_spec,
        out_shape=(jax.ShapeDtypeStruct((B, T, 4 * D), BF16), jax.ShapeDtypeStruct((B, T, D), BF16),
                   jax.ShapeDtypeStruct((N_SHARD, D, D), BF16)),
        compiler_params=_params(("arbitrary",) * 3))(order, x, ctx, mod, norm_g, win_b, cos2, sin2)


def _na_specs(L, T, rows):
    nm = rows // 4
    q_spec = pl.BlockSpec((None, TQ, 128), lambda hp, b, m: (b, m, hp))
    k_spec = pl.BlockSpec((None, T, 128), lambda hp, b, m: (b, 0, 4 + hp))
    v_spec = pl.BlockSpec((None, T, 128), lambda hp, b, m: (b, 0, 8 + hp))
    g_spec = pl.BlockSpec((None, TQ, 128), lambda hp, b, m: (b, m, 12 + hp))
    bias_spec = pl.BlockSpec((2, 3, TQ, KW), lambda hp, b, m: (hp, 0, 0, 0))
    return nm, q_spec, k_spec, v_spec, g_spec, bias_spec


def _na_tile(m, nm, rows):
    typ = jnp.where(m == 0, 0, jnp.where(m == nm - 1, 2, 1))
    start = pl.multiple_of(jnp.clip(4 * m - 4, 0, rows - 12) * GRID_W, TQ)
    return typ, start


def _na_fwd_call(P, bias, L, LC):
    B, T, _ = P.shape
    rows = L // GRID_W
    nm, q_spec, k_spec, v_spec, g_spec, bias_spec = _na_specs(L, T, rows)

    def body(q_ref, k_ref, v_ref, g_ref, bias_ref, y_ref, o_ref):
        typ, start = _na_tile(pl.program_id(2), nm, rows)
        for hh in range(2):
            ln = slice(hh * NA_DH, (hh + 1) * NA_DH)
            q = q_ref[:, ln]
            kw, vw = k_ref[pl.ds(start, KW), ln], v_ref[pl.ds(start, KW), ln]
            kc, vc = k_ref[L:L + LC, ln], v_ref[L:L + LC, ln]
            s1 = _dot_nt(q, kw) + bias_ref[hh, typ]
            s2 = _dot_nt(q, kc)
            mx = jnp.maximum(jnp.max(s1, axis=-1, keepdims=True), jnp.max(s2, axis=-1, keepdims=True))
            p1, p2 = jnp.exp(s1 - mx), jnp.exp(s2 - mx)
            inv = 1.0 / (jnp.sum(p1, axis=-1, keepdims=True) + jnp.sum(p2, axis=-1, keepdims=True))
            o = (_dot(p1.astype(BF16), vw) + _dot(p2.astype(BF16), vc)) * inv
            g = g_ref[:, ln].astype(F32)
            o_ref[:, ln] = o.astype(BF16)
            y_ref[:, ln] = (o * (g * _sigmoid(g))).astype(BF16)

    tile = pl.BlockSpec((None, TQ, 128), lambda hp, b, m: (b, m, hp))
    return pl.pallas_call(
        body, name="na_fwd", grid=(4, B, nm),
        in_specs=[q_spec, k_spec, v_spec, g_spec, bias_spec],
        out_specs=(tile, tile),
        out_shape=(jax.ShapeDtypeStruct((B, L, 512), BF16),) * 2,
        compiler_params=_params(("arbitrary",) * 3))(P, P, P, P, bias)


def _na_bwd_call(P, bias, dY, o_na, L, LC):
    B, T, _ = P.shape
    rows = L // GRID_W
    nm, q_spec, k_spec, v_spec, g_spec, bias_spec = _na_specs(L, T, rows)
    scale = NA_DH ** -0.5

    RB = 32

    def body(q_ref, k_ref, v_ref, g_ref, bias_ref, dy_ref, o_ref, dq_ref, dg_ref, dk_ref, dv_ref, db_ref,
             s1_ref, s2_ref, dp1_ref, dp2_ref, p1_ref, p2_ref, ds1_ref, ds2_ref):
        b, m = pl.program_id(1), pl.program_id(2)
        typ, start = _na_tile(m, nm, rows)

        @pl.when(m == 0)
        def _():
            dk_ref[...] = jnp.zeros_like(dk_ref)
            dv_ref[...] = jnp.zeros_like(dv_ref)

        @pl.when((m == 0) & (b == 0))
        def _():
            db_ref[...] = jnp.zeros_like(db_ref)

        for hh in range(2):
            ln = slice(hh * NA_DH, (hh + 1) * NA_DH)
            q = q_ref[:, ln]
            kw, vw = k_ref[pl.ds(start, KW), ln], v_ref[pl.ds(start, KW), ln]
            kc, vc = k_ref[L:L + LC, ln], v_ref[L:L + LC, ln]
            g = g_ref[:, ln].astype(F32)
            sg = _sigmoid(g)
            dy = dy_ref[:, ln].astype(F32)
            do = (dy * (g * sg)).astype(BF16)
            s1_ref[...] = _dot_nt(q, kw)
            s2_ref[...] = _dot_nt(q, kc)
            dp1_ref[...] = _dot_nt(do, vw)
            dp2_ref[...] = _dot_nt(do, vc)

            def rows_pass(r, carry, hh=hh):
                rw = pl.ds(pl.multiple_of(r * RB, RB), RB)
                a = s1_ref[rw, :] + bias_ref[hh, typ, rw, :]
                c = s2_ref[rw, :]
                mx = jnp.maximum(jnp.max(a, axis=-1, keepdims=True), jnp.max(c, axis=-1, keepdims=True))
                e1, e2 = jnp.exp(a - mx), jnp.exp(c - mx)
                inv = 1.0 / (jnp.sum(e1, axis=-1, keepdims=True) + jnp.sum(e2, axis=-1, keepdims=True))
                p1, p2 = e1 * inv, e2 * inv
                p1_ref[rw, :] = p1.astype(BF16)
                p2_ref[rw, :] = p2.astype(BF16)
                dp1, dp2 = dp1_ref[rw, :], dp2_ref[rw, :]
                delta = jnp.sum(p1 * dp1, axis=-1, keepdims=True) + jnp.sum(p2 * dp2, axis=-1, keepdims=True)
                ds1 = p1 * (dp1 - delta)
                db_ref[hh, typ, rw, :] += ds1
                ds1_ref[rw, :] = ds1.astype(BF16)
                ds2_ref[rw, :] = (p2 * (dp2 - delta)).astype(BF16)
                return carry

            lax.fori_loop(0, TQ // RB, rows_pass, 0, unroll=True)
            p1b, p2b, ds1b, ds2b = p1_ref[...], p2_ref[...], ds1_ref[...], ds2_ref[...]
            dg_ref[:, ln] = (dy * o_ref[:, ln].astype(F32) * (sg * (1.0 + g * (1.0 - sg)))).astype(BF16)
            dq_ref[:, ln] = ((_dot(ds1b, kw) + _dot(ds2b, kc)) * scale).astype(BF16)
            dk_ref[pl.ds(start, KW), ln] += _dot_tn(ds1b, q)
            dv_ref[pl.ds(start, KW), ln] += _dot_tn(p1b, do)
            dk_ref[L:L + LC, ln] += _dot_tn(ds2b, q)
            dv_ref[L:L + LC, ln] += _dot_tn(p2b, do)

    tile = pl.BlockSpec((None, TQ, 128), lambda hp, b, m: (b, m, hp))
    kv_out = pl.BlockSpec((None, T, 128), lambda hp, b, m: (b, 0, hp))
    wide, narrow = (TQ, KW), (TQ, LC)
    return pl.pallas_call(
        body, name="na_bwd", grid=(4, B, nm),
        in_specs=[q_spec, k_spec, v_spec, g_spec, bias_spec, tile, tile],
        out_specs=(tile, tile, kv_out, kv_out, bias_spec),
        out_shape=(jax.ShapeDtypeStruct((B, L, 512), BF16), jax.ShapeDtypeStruct((B, L, 512), BF16),
                   jax.ShapeDtypeStruct((B, T, 512), F32), jax.ShapeDtypeStruct((B, T, 512), F32),
                   jax.ShapeDtypeStruct(bias.shape, F32)),
        scratch_shapes=[pltpu.VMEM(wide, F32), pltpu.VMEM(narrow, F32), pltpu.VMEM(wide, F32), pltpu.VMEM(narrow, F32),
                        pltpu.VMEM(wide, BF16), pltpu.VMEM(narrow, BF16), pltpu.VMEM(wide, BF16),
                        pltpu.VMEM(narrow, BF16)],
        compiler_params=_params(("arbitrary",) * 3))(P, P, P, P, bias, dY, o_na)


def _head_scalar(dec_ref, h):
    lane = lax.broadcasted_iota(jnp.int32, dec_ref.shape, 1)
    return -jnp.sum(jnp.where(lane == h, jnp.exp(dec_ref[...]), 0.0), axis=1, keepdims=True)


def _ret_specs(T):
    q_spec = pl.BlockSpec((None, TQ, 128), lambda b, h, i: (b, i, 16 + h))
    k_spec = pl.BlockSpec((None, T, 128), lambda b, h, i: (b, 0, 20 + h))
    v_spec = pl.BlockSpec((None, T, 128), lambda b, h, i: (b, 0, 24 + h))
    g_spec = pl.BlockSpec((None, TQ, 128), lambda b, h, i: (b, i, 28 + h))
    dec_spec = pl.BlockSpec((1, 4), lambda b, h, i: (0, 0))
    gn_spec = pl.BlockSpec((1, 128), lambda b, h, i: (0, h))
    return q_spec, k_spec, v_spec, g_spec, dec_spec, gn_spec


def _chunk_decay(lgf, lgb):
    tau = lax.broadcasted_iota(jnp.int32, (TQ, 1), 0).astype(F32)
    sig = lax.broadcasted_iota(jnp.int32, (1, TQ), 1).astype(F32)
    dist = tau - sig
    dm = jnp.exp(dist * jnp.where(dist > 0, lgf, -lgb)) * jnp.where(dist == 0, 2.0, 1.0)
    return tau, dist, dm


def _ret_states_call(P, dec_f, dec_b, L, LC):
    B, T, _ = P.shape
    n = L // TQ

    def body(df_ref, db_ref, k_ref, v_ref, sf_ref, sb_ref):
        h = pl.program_id(1)
        lgf, lgb = _head_scalar(df_ref, h), _head_scalar(db_ref, h)
        tau = lax.broadcasted_iota(jnp.int32, (TQ, 1), 0).astype(F32)
        jc = lax.broadcasted_iota(jnp.int32, (LC, 1), 0).astype(F32)
        wf, wb = jnp.exp(lgf * (TQ - 1.0 - tau)), jnp.exp(lgb * tau)
        gcf, gcb = jnp.exp(lgf * float(TQ)), jnp.exp(lgb * float(TQ))
        kc, vc = k_ref[L:L + LC, :].astype(F32), v_ref[L:L + LC, :]

        def chunk_state(i, w):
            ks = pl.multiple_of(i * TQ, TQ)
            return _dot_tn((k_ref[pl.ds(ks, TQ), :].astype(F32) * w).astype(BF16), v_ref[pl.ds(ks, TQ), :])

        def fwd(i, s):
            sf_ref[i] = s
            return gcf * s + chunk_state(i, wf)

        lax.fori_loop(0, n, fwd, _dot_tn((kc * jnp.exp(lgf * (LC - 1.0 - jc))).astype(BF16), vc), unroll=True)

        def bwd(r, s):
            i = n - 1 - r
            sb_ref[i] = s
            return gcb * s + chunk_state(i, wb)

        lax.fori_loop(0, n, bwd, _dot_tn((kc * jnp.exp(lgb * jc)).astype(BF16), vc), unroll=True)

    st = pl.BlockSpec((None, None, n, RET_DK, RET_DK), lambda b, h: (b, h, 0, 0, 0))
    return pl.pallas_call(
        body, name="ret_states", grid=(B, 4),
        in_specs=[pl.BlockSpec((1, 4), lambda b, h: (0, 0)), pl.BlockSpec((1, 4), lambda b, h: (0, 0)),
                  pl.BlockSpec((None, T, 128), lambda b, h: (b, 0, 20 + h)),
                  pl.BlockSpec((None, T, 128), lambda b, h: (b, 0, 24 + h))],
        out_specs=(st, st),
        out_shape=(jax.ShapeDtypeStruct((B, 4, n, RET_DK, RET_DK), F32),) * 2,
        compiler_params=_params(("arbitrary",) * 2))(dec_f, dec_b, P, P)


def _retc_fwd_call(P, sf, sb, dec_f, dec_b, ret_norm_g, L):
    B, T, _ = P.shape
    q_spec, _, _, g_spec, dec_spec, gn_spec = _ret_specs(T)
    k_spec = pl.BlockSpec((None, TQ, 128), lambda b, h, i: (b, i, 20 + h))
    v_spec = pl.BlockSpec((None, TQ, 128), lambda b, h, i: (b, i, 24 + h))
    st_spec = pl.BlockSpec((None, None, None, RET_DK, RET_DK), lambda b, h, i: (b, h, i, 0, 0))

    def body(df_ref, db_ref, q_ref, k_ref, v_ref, g_ref, gn_ref, sf_ref, sb_ref, y_ref, o_ref):
        h = pl.program_id(1)
        lgf, lgb = _head_scalar(df_ref, h), _head_scalar(db_ref, h)
        tau, _, dm = _chunk_decay(lgf, lgb)
        q = q_ref[...]
        qf = q.astype(F32)
        acc = _dot((_dot_nt(q, k_ref[...]) * dm).astype(BF16), v_ref[...])
        acc = acc + _dot((qf * jnp.exp(lgf * (tau + 1.0))).astype(BF16), sf_ref[...].astype(BF16))
        acc = acc + _dot((qf * jnp.exp(lgb * (TQ - tau))).astype(BF16), sb_ref[...].astype(BF16))
        o_ref[...] = acc
        rn = lax.rsqrt(jnp.mean(acc * acc, axis=-1, keepdims=True) + EPS)
        g = g_ref[...].astype(F32)
        y_ref[...] = ((acc * rn * gn_ref[...]) * (g * _sigmoid(g))).astype(BF16)

    tile = pl.BlockSpec((None, TQ, 128), lambda b, h, i: (b, i, h))
    return pl.pallas_call(
        body, name="ret_fwd", grid=(B, 4, L // TQ),
        in_specs=[dec_spec, dec_spec, q_spec, k_spec, v_spec, g_spec, gn_spec, st_spec, st_spec],
        out_specs=(tile, tile),
        out_shape=(jax.ShapeDtypeStruct((B, L, 512), BF16), jax.ShapeDtypeStruct((B, L, 512), F32)),
        compiler_params=_params(("arbitrary",) * 3))(dec_f, dec_b, P, P, P, P, ret_norm_g, sf, sb)


def _retc_bwd_call(P, sf, sb, dec_f, dec_b, ret_norm_g, o_ret, dY, cos2, sin2, L, LC):
    B, T, _ = P.shape
    n = L // TQ
    C = float(TQ)
    kscale = RET_DK ** -0.5
    q_spec, k_spec, v_spec, g_spec, dec_spec, gn_spec = _ret_specs(T)
    st_spec = pl.BlockSpec((None, None, n, RET_DK, RET_DK), lambda b, h, i: (b, h, 0, 0, 0))

    def body(df_ref, db_ref, q_ref, k_ref, v_ref, g_ref, gn_ref, o_ref, dy_ref, cos_ref, sin_ref, sf_ref, sb_ref,
             dq_ref, dg_ref, dk_ref, dv_ref, dgn_ref, dlg_ref, dsf_ref, dsb_ref):
        h, i = pl.program_id(1), pl.program_id(2)
        lgf, lgb = _head_scalar(df_ref, h), _head_scalar(db_ref, h)
        tau, dist, dm = _chunk_decay(lgf, lgb)

        @pl.when(i == 0)
        def _():
            dk_ref[...] = jnp.zeros_like(dk_ref)
            dv_ref[...] = jnp.zeros_like(dv_ref)
            dgn_ref[...] = jnp.zeros_like(dgn_ref)
            dlg_ref[...] = jnp.zeros_like(dlg_ref)

        def add_lg(row, x):
            cs = jnp.sum(x, axis=0, keepdims=True)
            tot = cs[:, 0:128]
            for part in range(1, x.shape[1] // 128):
                tot = tot + cs[:, part * 128:(part + 1) * 128]
            dlg_ref[row:row + 1, :] += tot

        q = q_ref[...]
        qf = q.astype(F32)
        o = o_ref[...]
        g = g_ref[...].astype(F32)
        dy = dy_ref[...].astype(F32)
        gn = gn_ref[...]
        sg = _sigmoid(g)
        rn = lax.rsqrt(jnp.mean(o * o, axis=-1, keepdims=True) + EPS)
        nrm = o * rn
        dg_ref[...] = (dy * (nrm * gn) * (sg * (1.0 + g * (1.0 - sg)))).astype(BF16)
        dhn = dy * (g * sg)
        dgn_ref[...] += jnp.sum(dhn * nrm, axis=0, keepdims=True)
        dnrm = dhn * gn
        do = rn * (dnrm - nrm * jnp.mean(dnrm * nrm, axis=-1, keepdims=True))
        dob = do.astype(BF16)
        rows = pl.ds(pl.multiple_of(i * TQ, TQ), TQ)
        ki, vi = k_ref[rows, :], v_ref[rows, :]
        s = _dot_nt(q, ki)
        dsv = _dot_nt(dob, vi)
        dsb = (dsv * dm).astype(BF16)
        dk_ref[rows, :] += _dot_tn(dsb, q)
        dv_ref[rows, :] += _dot_tn((s * dm).astype(BF16), dob)
        xw = s * dsv * dm * jnp.abs(dist)
        fpart = jnp.where(dist > 0, xw, 0.0)
        add_lg(0, fpart)
        add_lg(1, xw - fpart)
        dq = _dot(dsb, ki)
        af, ab = jnp.exp(lgf * (tau + 1.0)), jnp.exp(lgb * (C - tau))
        qa, qb = (qf * af).astype(BF16), (qf * ab).astype(BF16)
        sfi, sbi = sf_ref[i].astype(BF16), sb_ref[i].astype(BF16)
        dq = dq + af * _dot_nt(dob, sfi) + ab * _dot_nt(dob, sbi)
        dsf_ref[i] = _dot_tn(qa, dob)
        dsb_ref[i] = _dot_tn(qb, dob)
        add_lg(0, (tau + 1.0) * (_dot(qa, sfi) * do))
        add_lg(1, (C - tau) * (_dot(qb, sbi) * do))
        cs, sn = cos_ref[rows, :], sin_ref[rows, :]
        dq_ref[...] = (dq * cs - pltpu.roll(dq, 64, 1) * sn).astype(BF16)

        @pl.when(i == n - 1)
        def _():
            jc = lax.broadcasted_iota(jnp.int32, (LC, 1), 0).astype(F32)
            crow = pl.ds(L, LC)

            def through_state(rws, w, dw, gst, row):
                kk, vv = k_ref[rws, :].astype(F32), v_ref[rws, :]
                gb = gst.astype(BF16)
                vg = _dot_nt(vv, gb)
                kw = kk * w
                dk_ref[rws, :] += w * vg
                dv_ref[rws, :] += _dot(kw.astype(BF16), gb)
                add_lg(row, dw * (kw * vg))

            def scan(lg, gc, w, dw, st_ref, dst_ref, order, row):
                def step(r, gst):
                    j = order(r)
                    through_state(pl.ds(pl.multiple_of(j * TQ, TQ), TQ), w, dw, gst, row)
                    add_lg(row, (C * gc) * (gst * st_ref[j]))
                    return dst_ref[j] + gc * gst
                return lax.fori_loop(0, n, step, jnp.zeros((RET_DK, RET_DK), F32), unroll=True)

            gcf, gcb = jnp.exp(lgf * C), jnp.exp(lgb * C)
            g0 = scan(lgf, gcf, jnp.exp(lgf * (C - 1.0 - tau)), C - 1.0 - tau, sf_ref, dsf_ref,
                      lambda r: n - 1 - r, 0)
            through_state(crow, jnp.exp(lgf * (LC - 1.0 - jc)), LC - 1.0 - jc, g0, 0)
            g1 = scan(lgb, gcb, jnp.exp(lgb * tau), tau, sb_ref, dsb_ref, lambda r: r, 1)
            through_state(crow, jnp.exp(lgb * jc), jc, g1, 1)
            dk = dk_ref[...]
            dk_ref[...] = (dk * cos_ref[...] - pltpu.roll(dk, 64, 1) * sin_ref[...]) * kscale

    tile = pl.BlockSpec((None, TQ, 128), lambda b, h, i: (b, i, h))
    kv_out = pl.BlockSpec((None, T, 128), lambda b, h, i: (b, 0, h))
    tab = pl.BlockSpec((T, RET_DK), lambda b, h, i: (0, 0))
    return pl.pallas_call(
        body, name="ret_bwd", grid=(B, 4, n),
        in_specs=[dec_spec, dec_spec, q_spec, k_spec, v_spec, g_spec, gn_spec, tile,
                  pl.BlockSpec((None, TQ, 128), lambda b, h, i: (b, i, 4 + h)), tab, tab, st_spec, st_spec],
        out_specs=(tile, tile, kv_out, kv_out,
                   pl.BlockSpec((None, 1, 128), lambda b, h, i: (b, 0, h)),
                   pl.BlockSpec((None, None, 8, 128), lambda b, h, i: (b, h, 0, 0))),
        out_shape=(jax.ShapeDtypeStruct((B, L, 512), BF16), jax.ShapeDtypeStruct((B, L, 512), BF16),
                   jax.ShapeDtypeStruct((B, T, 512), F32), jax.ShapeDtypeStruct((B, T, 512), F32),
                   jax.ShapeDtypeStruct((B, 1, 512), F32), jax.ShapeDtypeStruct((B, 4, 8, 128), F32)),
        scratch_shapes=[pltpu.VMEM((n, RET_DK, RET_DK), F32), pltpu.VMEM((n, RET_DK, RET_DK), F32)],
        compiler_params=_params(("arbitrary",) * 3))(
            dec_f, dec_b, P, P, P, P, ret_norm_g, o_ret, dY, cos2, sin2, sf, sb)


def _out_call(y_na, y_ret, x, target, mod, final_g, wout_f):
    B, L, _ = x.shape

    def body(yn_ref, yr_ref, x_ref, t_ref, mod_ref, gf_ref, w_ref, dy_ref, dx2_ref, dw_ref, sm_ref):
        b, i = pl.program_id(0), pl.program_id(1)

        @pl.when((b == 0) & (i == 0))
        def _():
            dw_ref[...] = jnp.zeros_like(dw_ref)
            sm_ref[...] = jnp.zeros_like(sm_ref)

        gate = mod_ref[pl.ds(b, 1), 2 * D:3 * D]
        gf = gf_ref[...]
        yn, yr = yn_ref[...], yr_ref[...]
        ylat = _dot(yn, w_ref[0:512, :]) + _dot(yr, w_ref[512:1024, :])
        x2 = x_ref[...] + gate * ylat
        r = lax.rsqrt(jnp.mean(x2 * x2, axis=-1, keepdims=True) + EPS)
        xr = x2 * r
        err = xr * gf - t_ref[...]
        sm_ref[1:2, :] += jnp.sum(err * err, axis=0, keepdims=True)
        dout = err * (1.0 / D)
        sm_ref[0:1, :] += jnp.sum(dout * xr, axis=0, keepdims=True)
        gd = dout * gf
        dx2 = r * (gd - xr * jnp.mean(gd * xr, axis=-1, keepdims=True))
        dx2_ref[...] = dx2
        sm_ref[pl.ds(2 + b, 1), :] += jnp.sum(dx2 * ylat, axis=0, keepdims=True)
        dyl = (gate * dx2).astype(BF16)
        dy_ref[:, 0:512] = _dot_nt(dyl, w_ref[0:512, :]).astype(BF16)
        dy_ref[:, 512:1024] = _dot_nt(dyl, w_ref[512:1024, :]).astype(BF16)
        dw_ref[0:512, :] += _dot_tn(yn, dyl)
        dw_ref[512:1024, :] += _dot_tn(yr, dyl)

    half = pl.BlockSpec((None, TQ, 512), lambda b, i: (b, i, 0))
    full = pl.BlockSpec((None, TQ, D), lambda b, i: (b, i, 0))
    return pl.pallas_call(
        body, name="out_proj_loss", grid=(B, L // TQ),
        in_specs=[half, half, full, full,
                  pl.BlockSpec((8, 3 * D), lambda b, i: (0, 0)),
                  pl.BlockSpec((1, D), lambda b, i: (0, 0)),
                  pl.BlockSpec((D, D), lambda b, i: (0, 0))],
        out_specs=(full, full, pl.BlockSpec((D, D), lambda b, i: (0, 0)),
                   pl.BlockSpec((8, D), lambda b, i: (0, 0))),
        out_shape=(jax.ShapeDtypeStruct((B, L, D), BF16), jax.ShapeDtypeStruct((B, L, D), F32),
                   jax.ShapeDtypeStruct((D, D), F32), jax.ShapeDtypeStruct((8, D), F32)),
        compiler_params=_params(("arbitrary",) * 2))(y_na, y_ret, x, target, mod, final_g, wout_f)


def _dh_call(dsec, win_f, x, ctx, dx2, mod, norm_g, cp_in, cp_out):
    B, L, _ = x.shape
    LC = ctx.shape[1]
    nl = L // TQ

    def body(d0, d1, d2, d3, d4, d5, d6, d7, w_ref, x_ref, ctx_ref, dx2_ref, mod_ref, g_ref, cpi_ref, cpo_ref,
             gx_ref, sm_ref, sli_ref, slo_ref, ssem, rsem, lsem):
        drefs = (d0, d1, d2, d3, d4, d5, d6, d7)
        b, t = pl.program_id(0), pl.program_id(1)
        is_lat = t < nl

        @pl.when((b == 0) & (t == 0))
        def _():
            sm_ref[...] = jnp.zeros_like(sm_ref)

        def dh_of(secs):
            acc = jnp.zeros((TQ, D), F32)
            for sec in secs:
                s, half = divmod(sec, 2)
                acc = acc + _dot_nt(drefs[sec][...].astype(BF16), w_ref[s, :, half * 512:(half + 1) * 512])
            return acc

        def norm_bwd(dh, xt, mrow):
            scale = mrow[:, D:2 * D]
            g = g_ref[...]
            rstd = lax.rsqrt(jnp.mean(xt * xt, axis=-1, keepdims=True) + EPS)
            xn = xt * rstd
            dshift = jnp.sum(dh, axis=0, keepdims=True)
            dscale = jnp.sum(dh * (xn * g), axis=0, keepdims=True)
            dhn = dh * (1.0 + scale)
            sm_ref[0:1, :] += jnp.sum(dhn * xn, axis=0, keepdims=True)
            dxn = dhn * g
            dx = rstd * (dxn - xn * jnp.mean(dxn * xn, axis=-1, keepdims=True))
            return dshift, dscale, dx

        @pl.when(is_lat)
        def _():
            dshift, dscale, dx = norm_bwd(dh_of(range(8)), x_ref[...], mod_ref[pl.ds(b, 1), :])
            sm_ref[pl.ds(3 + b, 1), :] += dshift
            sm_ref[pl.ds(3 + B + b, 1), :] += dscale
            gx_ref[...] = dx2_ref[...] + dx

        @pl.when(jnp.logical_not(is_lat))
        def _():
            dshift, dscale, _ = norm_bwd(dh_of((1, 2, 5, 6)), ctx_ref[...], mod_ref[B:B + 1, :])
            sm_ref[1:2, :] += dshift
            sm_ref[2:3, :] += dscale

        mx, my, mc = _mesh_pos()
        s = 2 * mx + my
        cps, sls = (cpi_ref, cpo_ref), (sli_ref, slo_ref)
        own = [pltpu.make_async_copy(cps[a].at[s], sls[a].at[s], lsem.at[a]) for a in range(2)]
        sends, recvs, k = [], [], 0
        for px, py in _other_chips(mx, my):
            ps = 2 * px + py
            for a in range(2):
                sends.append(_remote(cps[a].at[ps], sls[a].at[s], ssem, rsem, k, (px, py, mc)))
                recvs.append(_remote(cps[a].at[s], sls[a].at[ps], ssem, rsem, k, (px, py, mc)))
                k += 1

        @pl.when((b == 0) & (t == 0))
        def _():
            for cp in own + sends:
                cp.start()

        @pl.when((b == B - 1) & (t == nl))
        def _():
            _finish(own, sends, recvs)

    lat = lambda b, t: (b, jnp.minimum(t, nl - 1), 0)
    tok = lambda b, t: (b, t, 0)
    sec_specs = [pl.BlockSpec((None, TQ, 512), lat if sec in (0, 3, 4, 7) else tok) for sec in range(8)]
    return pl.pallas_call(
        body, name="dh_norm_bwd", grid=(B, nl + 1),
        in_specs=sec_specs + [
            pl.BlockSpec((N_SHARD, D, D), lambda b, t: (0, 0, 0)),
            pl.BlockSpec((None, TQ, D), lat),
            pl.BlockSpec((None, LC, D), lambda b, t: (b, 0, 0)),
            pl.BlockSpec((None, TQ, D), lat),
            pl.BlockSpec((8, 3 * D), lambda b, t: (0, 0)),
            pl.BlockSpec((1, D), lambda b, t: (0, 0)), ANY, ANY],
        out_specs=(pl.BlockSpec((None, TQ, D), lat), pl.BlockSpec((8, D), lambda b, t: (0, 0)), ANY, ANY),
        out_shape=(jax.ShapeDtypeStruct((B, L, D), F32), jax.ShapeDtypeStruct((8, D), F32),
                   jax.ShapeDtypeStruct(cp_in.shape, cp_in.dtype), jax.ShapeDtypeStruct(cp_out.shape, cp_out.dtype)),
        scratch_shapes=[pltpu.SemaphoreType.DMA((6,)), pltpu.SemaphoreType.DMA((6,)),
                        pltpu.SemaphoreType.DMA((2,))],
        compiler_params=_params(("arbitrary",) * 2))(*dsec, win_f, x, ctx, dx2, mod, norm_g, cp_in, cp_out)


def _dw_call(dsec, h, L):
    B, T, _ = h.shape
    nl = L // TQ

    def body(d0, d1, d2, d3, d4, d5, d6, d7, h_ref, dw_ref, acc_ref):
        drefs = (d0, d1, d2, d3, d4, d5, d6, d7)
        b, t = pl.program_id(0), pl.program_id(1)

        @pl.when((b == 0) & (t == 0))
        def _():
            acc_ref[...] = jnp.zeros_like(acc_ref)

        hb = h_ref[...]

        def add(secs):
            for sec in secs:
                s, half = divmod(sec, 2)
                acc_ref[s, :, half * 512:(half + 1) * 512] += _dot_tn(hb, drefs[sec][...].astype(BF16))

        @pl.when(t < nl)
        def _():
            add(range(8))

        @pl.when(t >= nl)
        def _():
            add((1, 2, 5, 6))

        @pl.when((b == B - 1) & (t == nl))
        def _():
            dw_ref[...] = acc_ref[...].astype(BF16)

    lat = lambda b, t: (b, jnp.minimum(t, nl - 1), 0)
    tok = lambda b, t: (b, t, 0)
    sec_specs = [pl.BlockSpec((None, TQ, 512), lat if sec in (0, 3, 4, 7) else tok) for sec in range(8)]
    return pl.pallas_call(
        body, name="dw_in", grid=(B, nl + 1),
        in_specs=sec_specs + [pl.BlockSpec((None, TQ, D), tok)],
        out_specs=pl.BlockSpec((N_SHARD, D, D), lambda b, t: (0, 0, 0)),
        out_shape=jax.ShapeDtypeStruct((N_SHARD, D, D), BF16),
        scratch_shapes=[pltpu.VMEM((N_SHARD, D, D), F32)],
        compiler_params=_params(("arbitrary",) * 2, vmem_mb=56))(*dsec, h)


def _mesh_pos():
    return lax.axis_index("x"), lax.axis_index("y"), lax.axis_index("c")


def _flip(v, f):
    return 1 - v if f else v


def _remote(src, dst, ssem, rsem, k, peer):
    return pltpu.make_async_remote_copy(src_ref=src, dst_ref=dst, send_sem=ssem.at[k], recv_sem=rsem.at[k],
                                        device_id=peer, device_id_type=MESH)


def _other_chips(x, y):
    return [(_flip(x, fx), _flip(y, fy)) for fx, fy in ((1, 0), (0, 1), (1, 1))]


def _all_to_all_small(src, dst_all, ssem, rsem, k0, x, y, cc):
    me = 4 * x + 2 * y + cc
    sends, recvs = [], []
    for f in range(1, N_DEV):
        px, py, pc = _flip(x, f & 4), _flip(y, f & 2), _flip(cc, f & 1)
        sends.append(_remote(src, dst_all.at[me], ssem, rsem, k0 + f - 1, (px, py, pc)))
        recvs.append(_remote(src, dst_all.at[4 * px + 2 * py + pc], ssem, rsem, k0 + f - 1, (px, py, pc)))
    return sends, recvs


def _finish(local, sends, recvs):
    for cp in recvs:
        cp.wait_recv()
    for cp in sends:
        cp.wait_send()
    for cp in local:
        cp.wait()


def _gather_call(wout_b, wada_b, c, rpb_flat):
    arrs = (wout_b, wada_b)
    na = len(arrs)
    hrs = [a.shape[0] // 2 for a in arrs]

    def body(wout, wada, c_ref, r_ref, wout_f, wada_f, c_all, bias_ref, et_ref, ssem, rsem, lsem):
        x, y, cc = _mesh_pos()
        s, me = 2 * x + y, 4 * x + 2 * y + cc
        sib = (x, y, 1 - cc)
        srcs, dsts = (wout, wada), (wout_f, wada_f)

        def half(a, shard, hc):
            return dsts[a].at[shard, pl.ds(hc * hrs[a], hrs[a])]

        local = [pltpu.make_async_copy(srcs[a], dsts[a].at[s], lsem.at[a]) for a in range(na)]
        local.append(pltpu.make_async_copy(c_ref, c_all.at[me], lsem.at[na]))
        ici_send, ici_recv, fwd_send, fwd_recv, k = [], [], [], [], 0
        for px, py in _other_chips(x, y):
            ps = 2 * px + py
            for a in range(na):
                mine = srcs[a].at[pl.ds(cc * hrs[a], hrs[a])]
                ici_send.append(_remote(mine, half(a, s, cc), ssem, rsem, k, (px, py, cc)))
                ici_recv.append(_remote(mine, half(a, ps, cc), ssem, rsem, k, (px, py, cc)))
                fwd_send.append(_remote(half(a, ps, cc), half(a, ps, cc), ssem, rsem, 3 * na + k, sib))
                fwd_recv.append(_remote(half(a, ps, 1 - cc), half(a, ps, 1 - cc), ssem, rsem, 3 * na + k, sib))
                k += 1
        c_send, c_recv = _all_to_all_small(c_ref, c_all, ssem, rsem, 6 * na, x, y, cc)
        for cp in local + ici_send + c_send:
            cp.start()
        _bias_body(r_ref, bias_ref, et_ref)
        for got, fwd in zip(ici_recv, fwd_send):
            got.wait_recv()
            fwd.start()
        _finish(local, ici_send + fwd_send + c_send, fwd_recv + c_recv)

    return pl.pallas_call(
        body, name="weight_gather",
        in_specs=[pl.BlockSpec(memory_space=pltpu.VMEM)] * 3 + [pl.BlockSpec(memory_space=pltpu.SMEM)],
        out_specs=(pl.BlockSpec(memory_space=pltpu.VMEM),) * 4,
        out_shape=tuple(jax.ShapeDtypeStruct((N_SHARD,) + a.shape, a.dtype) for a in arrs)
        + (jax.ShapeDtypeStruct((N_DEV,) + c.shape, c.dtype),
           jax.ShapeDtypeStruct((rpb_flat.shape[0], 3, TQ, KW), F32)),
        scratch_shapes=[pltpu.VMEM((15, GRID_W, GRID_W), F32),
                        pltpu.SemaphoreType.DMA((6 * na + 7,)), pltpu.SemaphoreType.DMA((6 * na + 7,)),
                        pltpu.SemaphoreType.DMA((na + 1,))],
        compiler_params=pltpu.CompilerParams(vmem_limit_bytes=56 << 20))(wout_b, wada_b, c, rpb_flat)


VROWS = 32


def _grad_halves_call(dwin_b, dwout_b, dbias, dlg):
    arrs = (dwin_b, dwout_b)
    hrs = [a.shape[1] // 2 for a in arrs]

    def body(din, dout, db_ref, dlg_ref, cp_in, cp_out, drpb_ref, dlgo_ref, got_in, got_out, p_ref, ssem, rsem):
        x, y, cc = _mesh_pos()
        sib = (x, y, 1 - cc)
        srcs, gots, cps = (din, dout), (got_in, got_out), (cp_in, cp_out)
        halves = [_remote(srcs[a].at[:, pl.ds((1 - cc) * hrs[a], hrs[a])], gots[a], ssem, rsem, a, sib)
                  for a in range(2)]
        for cp in halves:
            cp.start()
        _small_reduce_body(db_ref, dlg_ref, drpb_ref, dlgo_ref, p_ref)
        for cp in halves:
            cp.wait_recv()
        for a in range(2):
            for j in range(N_SHARD):
                def add(i, carry, a=a, j=j):
                    r = pl.multiple_of(i * VROWS, VROWS)
                    mine = srcs[a][j, pl.ds(pl.multiple_of(cc * hrs[a] + r, VROWS), VROWS), :].astype(F32)
                    cps[a][j, pl.ds(r, VROWS), :] = (
                        mine + gots[a][j, pl.ds(r, VROWS), :].astype(F32)).astype(BF16)
                    return carry
                lax.fori_loop(0, hrs[a] // VROWS, add, 0)
        for cp in halves:
            cp.wait_send()

    vmem = pl.BlockSpec(memory_space=pltpu.VMEM)
    half_shapes = [(N_SHARD, hrs[a], arrs[a].shape[2]) for a in range(2)]
    return pl.pallas_call(
        body, name="grad_halves",
        in_specs=[vmem] * 4, out_specs=(vmem,) * 4,
        out_shape=(jax.ShapeDtypeStruct(half_shapes[0], BF16), jax.ShapeDtypeStruct(half_shapes[1], BF16),
                   jax.ShapeDtypeStruct((dbias.shape[0], 16, 32), F32), jax.ShapeDtypeStruct((32, 128), F32)),
        scratch_shapes=[pltpu.VMEM(half_shapes[0], BF16), pltpu.VMEM(half_shapes[1], BF16),
                        pltpu.VMEM((32, GRID_W), F32),
                        pltpu.SemaphoreType.DMA((2,)), pltpu.SemaphoreType.DMA((2,))],
        compiler_params=pltpu.CompilerParams(vmem_limit_bytes=56 << 20))(dwin_b, dwout_b, dbias, dlg)


def _grad_finish_call(sl_in, sl_out, small):
    arrs = (sl_in, sl_out)

    def body(sin, sout, sm, gin, gout, sm_all, h_in, h_out, ssem, rsem, lsem):
        x, y, cc = _mesh_pos()
        me = 4 * x + 2 * y + cc
        sib = (x, y, 1 - cc)
        sls, hs, gs = (sin, sout), (h_in, h_out), (gin, gout)
        sm_send, sm_recv = _all_to_all_small(sm, sm_all, ssem, rsem, 2, x, y, cc)
        sm_own = pltpu.make_async_copy(sm, sm_all.at[me], lsem.at[0])
        for cp in sm_send + [sm_own]:
            cp.start()
        for a in range(2):
            def total(i, carry, a=a):
                rows = pl.ds(pl.multiple_of(i * VROWS, VROWS), VROWS)
                sl = sls[a]
                hs[a][rows, :] = ((sl[0, rows, :].astype(F32) + sl[1, rows, :].astype(F32))
                                  + sl[2, rows, :].astype(F32)) + sl[3, rows, :].astype(F32)
                return carry
            lax.fori_loop(0, arrs[a].shape[1] // VROWS, total, 0)
        mine = [pltpu.make_async_copy(hs[a], gs[a].at[cc], lsem.at[1 + a]) for a in range(2)]
        back = [_remote(hs[a], gs[a].at[cc], ssem, rsem, a, sib) for a in range(2)]
        back_recv = [_remote(hs[a], gs[a].at[1 - cc], ssem, rsem, a, sib) for a in range(2)]
        for cp in mine + back:
            cp.start()
        _finish(mine + [sm_own], back + sm_send, back_recv + sm_recv)

    vmem = pl.BlockSpec(memory_space=pltpu.VMEM)
    return pl.pallas_call(
        body, name="grad_finish",
        in_specs=[vmem] * 3, out_specs=(vmem,) * 3,
        out_shape=(jax.ShapeDtypeStruct((2,) + sl_in.shape[1:], F32),
                   jax.ShapeDtypeStruct((2,) + sl_out.shape[1:], F32),
                   jax.ShapeDtypeStruct((N_DEV,) + small.shape, F32)),
        scratch_shapes=[pltpu.VMEM(sl_in.shape[1:], F32), pltpu.VMEM(sl_out.shape[1:], F32),
                        pltpu.SemaphoreType.DMA((9,)), pltpu.SemaphoreType.DMA((9,)),
                        pltpu.SemaphoreType.DMA((3,))],
        compiler_params=pltpu.CompilerParams(vmem_limit_bytes=48 << 20))(sl_in, sl_out, small)


def _adamw(w, g, m, v):
    m = ADAM_B1 * m + (1.0 - ADAM_B1) * g
    v = ADAM_B2 * v + (1.0 - ADAM_B2) * (g * g)
    m_hat = m / (1.0 - ADAM_B1 ** ADAM_STEP)
    v_hat = v / (1.0 - ADAM_B2 ** ADAM_STEP)
    return -ADAM_LR * (m_hat / (jnp.sqrt(v_hat) + ADAM_EPS) + ADAM_WD * w), m, v


def _adam_call(w, m, v, g, name):
    R, C = w.shape
    tr = 256

    def body(w_ref, m_ref, v_ref, g_ref, d_ref, mo_ref, vo_ref):
        d_ref[...], mo_ref[...], vo_ref[...] = _adamw(w_ref[...], g_ref[...], m_ref[...], v_ref[...])

    spec = pl.BlockSpec((tr, C), lambda i: (i, 0))
    return pl.pallas_call(
        body, name=name, grid=(R // tr,), in_specs=[spec] * 4,
        out_specs=(spec,) * 3, out_shape=(jax.ShapeDtypeStruct((R, C), F32),) * 3,
        compiler_params=_params(("arbitrary",)))(w, m, v, g)


R_GF, R_NG, R_LOSS, R_RNG, R_LGF, R_LGB, R_SHIFT, R_SCALE, R_GATE, R_SHIFT_C, R_SCALE_C, R_RNG2, R_RPB = (
    0, 1, 2, 3, 4, 5, 6, 8, 10, 12, 13, 14, 16)
W_GF, W_NG, W_CCTX, W_RNG, W_DF, W_DB, W_BADA, W_RPB = 0, 1, 2, 3, 4, 5, 6, 9


def _small_final_call(sm_all, c_t, c_ctx, wada_f, wada, m_ada, v_ada, wsm, msm, vsm, B):
    ws = wada.shape[1]
    NB = N_DEV * B

    def body(sm_ref, ct_ref, cctx_ref, wf_ref, wa_ref, ma_ref, va_ref, w_ref, m_ref, v_ref,
             g_ref, d_ref, mo_ref, vo_ref, ga_ref, da_ref, mao_ref, vao_ref, loss_ref, dmod_ref):
        x, y, _ = _mesh_pos()
        s = 2 * x + y
        tot = sm_ref[0]
        for dv in range(1, N_DEV):
            tot = tot + sm_ref[dv]
        w = w_ref[...]
        for dv in range(N_DEV):
            for b in range(B):
                r = dv * B + b
                for part, row in enumerate((R_SHIFT, R_SCALE, R_GATE)):
                    dmod_ref[r:r + 1, part * D:(part + 1) * D] = sm_ref[dv, row + b:row + b + 1, :]
        dmod_ref[NB:NB + 1, 0:D] = tot[R_SHIFT_C:R_SHIFT_C + 1, :]
        dmod_ref[NB:NB + 1, D:2 * D] = tot[R_SCALE_C:R_SCALE_C + 1, :]
        dmod_ref[NB:NB + 1, 2 * D:3 * D] = jnp.zeros((1, D), F32)
        dmod_ref[NB + 1:, :] = jnp.zeros((dmod_ref.shape[0] - NB - 1, 3 * D), F32)
        dmod = dmod_ref[...]
        cc = cctx_ref[...]
        scc = _sigmoid(cc)
        ct = ct_ref[...]
        act_t = ct * _sigmoid(ct)
        dmc = dmod[NB:NB + 1, :].astype(BF16)
        dact = jnp.zeros((1, D), F32)
        for sh in range(N_SHARD):
            dact = dact + _dot_nt(dmc[:, sh * ws:(sh + 1) * ws], wf_ref[sh])
        g = jnp.zeros((16, D), F32)
        rows = lax.broadcasted_iota(jnp.int32, (16, D), 0)

        def put(g, row, val):
            return jnp.where(rows == row, val, g)

        g = put(g, W_GF, tot[R_GF:R_GF + 1, :])
        g = put(g, W_NG, tot[R_NG:R_NG + 1, :])
        g = put(g, W_CCTX, dact * (scc * (1.0 + cc * (1.0 - scc))))
        g = put(g, W_RNG, tot[R_RNG:R_RNG + 1, :] + tot[R_RNG2:R_RNG2 + 1, :])
        g = put(g, W_DF, tot[R_LGF:R_LGF + 1, :] * (-jnp.exp(w[W_DF:W_DF + 1, :])))
        g = put(g, W_DB, tot[R_LGB:R_LGB + 1, :] * (-jnp.exp(w[W_DB:W_DB + 1, :])))
        db = jnp.sum(dmod, axis=0, keepdims=True)
        for part in range(3):
            g = put(g, W_BADA + part, db[:, part * D:(part + 1) * D])
        for part in range(4):
            g = put(g, W_RPB + part, tot[R_RPB + part:R_RPB + part + 1, :])
        g_ref[...] = g
        d_ref[...], mo_ref[...], vo_ref[...] = _adamw(w, g, m_ref[...], v_ref[...])
        loss_ref[...] = jnp.broadcast_to(
            (0.5 / D) * jnp.sum(tot[R_LOSS:R_LOSS + 1, :], axis=1, keepdims=True), (8, 128))
        for sh in range(N_SHARD):
            @pl.when(s == sh)
            def _():
                ga = jnp.dot(act_t, dmod[:, sh * ws:(sh + 1) * ws], precision=HIGHEST,
                             preferred_element_type=F32)
                ga_ref[...] = ga
                da_ref[...], mao_ref[...], vao_ref[...] = _adamw(wa_ref[...], ga, ma_ref[...], va_ref[...])

    sh_small = jax.ShapeDtypeStruct((16, D), F32)
    sh_ada = jax.ShapeDtypeStruct(wada.shape, F32)
    return pl.pallas_call(
        body, name="small_final",
        out_shape=(sh_small,) * 4 + (sh_ada,) * 4 + (jax.ShapeDtypeStruct((8, 128), F32),),
        scratch_shapes=[pltpu.VMEM((NB + 8, 3 * D), F32)],
        compiler_params=_params(vmem_mb=56))(
            sm_all, c_t, c_ctx, wada_f, wada, m_ada, v_ada, wsm, msm, vsm)


def _local_step(order, x, c, ctx, c_ctx, norm_g, wada_f, b_ada, win_b, bias, dec_f, dec_b, ret_norm_g,
                wout_f, final_g, target):
    B, L, _ = x.shape
    LC = ctx.shape[1]
    assert B == 2
    cos2, sin2 = _rope_tables(L, LC)
    c8 = jnp.concatenate([c, c_ctx[None, :], jnp.zeros((8 - B - 1, D), F32)], axis=0)
    mod = _mod_call(c8, wada_f, b_ada)
    P, h, win_f = _inproj_gather_call(order, x, ctx, mod, norm_g, win_b, cos2, sin2)
    y_na, o_na = _na_fwd_call(P, bias, L, LC)
    sf, sb = _ret_states_call(P, dec_f, dec_b, L, LC)
    y_ret, o_ret = _retc_fwd_call(P, sf, sb, dec_f, dec_b, ret_norm_g, L)
    dY, dx2, dwout_p, sm_out = _out_call(y_na, y_ret, x, target, mod, final_g, wout_f.reshape(D, D))
    dnq, dng, dnk, dnv, dbias = _na_bwd_call(P, bias, dY, o_na, L, LC)
    drq, drg, drk, drv, dgn, dlg = _retc_bwd_call(P, sf, sb, dec_f, dec_b, ret_norm_g, o_ret, dY, cos2, sin2, L, LC)
    dsec = (dnq, dnk, dnv, dng, drq, drk, drv, drg)
    dwin_b = _dw_call(dsec, h, L)
    cp_in, cp_out, drpb, dlg_sum = _grad_halves_call(
        dwin_b, dwout_p.astype(BF16).reshape(N_SHARD, D // N_SHARD, D), dbias, dlg)
    grad_x, sm_dh, sl_in, sl_out = _dh_call(dsec, win_f, x, ctx, dx2, mod, norm_g, cp_in, cp_out)
    z = jnp.zeros((1, D), F32)
    pad = lambda v: jnp.pad(v.reshape(1, -1), ((0, 0), (0, D - v.size)))
    dlg_sum = dlg_sum.reshape(4, 8, 128)
    rpb_rows = jnp.pad(drpb[:, :15, :31].reshape(-1), (0, 4 * D - drpb.shape[0] * 465)).reshape(4, D)
    small = jnp.concatenate([
        sm_out[0:1], sm_dh[0:1], sm_out[1:2], pad(dgn[0]), pad(dlg_sum[:, 0, 0]), pad(dlg_sum[:, 1, 0]),
        sm_dh[3:5], sm_dh[5:7], sm_out[2:4], sm_dh[1:2], sm_dh[2:3], pad(dgn[1]), z, rpb_rows,
        jnp.zeros((SM_ROWS - 20, D), F32)], axis=0)
    return grad_x, sl_in, sl_out, small


def kernel(x, c, ctx, c_ctx, norm_g, w_ada, b_ada, w_in, na_rpb, ret_decay_fwd, ret_decay_bwd, ret_norm_g, w_out, final_norm_g, loss_target, m_c_ctx, m_norm_g, m_w_ada, m_b_ada, m_w_in, m_na_rpb, m_ret_decay_fwd, m_ret_decay_bwd, m_ret_norm_g, m_w_out, m_final_norm_g, v_c_ctx, v_norm_g, v_w_ada, v_b_ada, v_w_in, v_na_rpb, v_ret_decay_fwd, v_ret_decay_bwd, v_ret_norm_g, v_w_out, v_final_norm_g):
    B = x.shape[0]
    wout_f, wada_f, c_all, bias = _gather_call(
        w_out[0].astype(BF16), w_ada[0].astype(BF16), c, na_rpb[0].reshape(na_rpb.shape[1], -1))
    mx, my = lax.axis_index("x"), lax.axis_index("y")
    order = jnp.stack([2 * mx + my, 2 * (1 - mx) + my, 2 * mx + (1 - my),
                       2 * (1 - mx) + (1 - my)]).astype(jnp.int32)
    grad_x, sl_in, sl_out, small = _local_step(
        order, x, c, ctx, c_ctx, norm_g, wada_f, b_ada, w_in[0].astype(BF16), bias, ret_decay_fwd,
        ret_decay_bwd, ret_norm_g, wout_f, final_norm_g.reshape(1, D), loss_target)
    gin, gout, sm_all = _grad_finish_call(sl_in, sl_out, small)
    g_win, g_wout = gin.reshape(w_in.shape[1:]), gout.reshape(w_out.shape[1:])
    d_win, nm_win, nv_win = _adam_call(w_in[0], m_w_in[0], v_w_in[0], g_win, "adam_w_in")
    d_wout, nm_wout, nv_wout = _adam_call(w_out[0], m_w_out[0], v_w_out[0], g_wout, "adam_w_out")

    def pack(gf, ng, cc, rng, df, db, bada, rpb):
        pad = lambda v: jnp.pad(v.reshape(1, -1), ((0, 0), (0, D - v.size)))
        return jnp.concatenate([
            gf.reshape(1, D), ng.reshape(1, D), cc.reshape(1, D), pad(rng), pad(df), pad(db),
            bada.reshape(3, D), jnp.pad(rpb.reshape(-1), (0, 4 * D - rpb.size)).reshape(4, D),
            jnp.zeros((3, D), F32)], axis=0)

    wsm = pack(final_norm_g, norm_g, c_ctx, ret_norm_g, ret_decay_fwd, ret_decay_bwd, b_ada, na_rpb)
    msm = pack(m_final_norm_g, m_norm_g, m_c_ctx, m_ret_norm_g, m_ret_decay_fwd, m_ret_decay_bwd, m_b_ada, m_na_rpb)
    vsm = pack(v_final_norm_g, v_norm_g, v_c_ctx, v_ret_norm_g, v_ret_decay_fwd, v_ret_decay_bwd, v_b_ada, v_na_rpb)
    c_t = jnp.concatenate([c_all.reshape(N_DEV * B, D), c_ctx.reshape(1, D), jnp.zeros((7, D), F32)], axis=0).T
    outs = _small_final_call(sm_all, c_t, c_ctx.reshape(1, D), wada_f,
                             w_ada[0], m_w_ada[0], v_w_ada[0], wsm, msm, vsm, B)
    smalls, adas, loss = outs[0:4], outs[4:8], outs[8][0, 0]

    def unpack(p):
        rw = ret_norm_g.shape[1]
        return dict(
            final_norm_g=p[W_GF], norm_g=p[W_NG:W_NG + 1], c_ctx=p[W_CCTX], ret_norm_g=p[W_RNG:W_RNG + 1, :rw],
            ret_decay_fwd=p[W_DF:W_DF + 1, :4], ret_decay_bwd=p[W_DB:W_DB + 1, :4],
            b_ada=p[W_BADA:W_BADA + 3].reshape(1, 3 * D),
            na_rpb=p[W_RPB:W_RPB + 4].reshape(-1)[:na_rpb.size].reshape(na_rpb.shape))

    res = []
    for p, ada, win_o, wout_o in zip(smalls, adas, (g_win, d_win, nm_win, nv_win),
                                     (g_wout, d_wout, nm_wout, nv_wout)):
        u = unpack(p)
        res.append([u["c_ctx"], u["norm_g"], ada[None], u["b_ada"], win_o[None], u["na_rpb"],
                    u["ret_decay_fwd"], u["ret_decay_bwd"], u["ret_norm_g"], wout_o[None], u["final_norm_g"]])
    return (loss, grad_x, *res[0], *res[1], *res[2], *res[3])
```

```python
import numpy as np
import jax
import jax.numpy as jnp
from jax import lax
from jax.experimental import pallas as pl
from jax.experimental.pallas import tpu as pltpu

F32 = jnp.float32
BF16 = jnp.bfloat16
HIGHEST = lax.Precision.HIGHEST

D = 1024
GRID_W = 64
NA_DH = 64
RET_DK = 128
ROPE_BASE = 10000.0
EPS = 1e-6
NEG = -1e30
TQ = 256
KW = 12 * GRID_W
N_SHARD = 4
N_DEV = 8
SM_ROWS = 24

ADAM_LR = 0.001
ADAM_B1 = 0.9
ADAM_B2 = 0.999
ADAM_EPS = 1e-08
ADAM_WD = 0.01
ADAM_STEP = 10

MESH = pl.DeviceIdType.MESH
ANY = pl.BlockSpec(memory_space=pl.ANY)


def _params(sem=None, vmem_mb=48):
    return pltpu.CompilerParams(dimension_semantics=sem, vmem_limit_bytes=vmem_mb << 20)


def _dot(a, b):
    return jnp.dot(a, b, preferred_element_type=F32)


def _dot_nt(a, b):
    return lax.dot_general(a, b, (((1,), (1,)), ((), ())), preferred_element_type=F32)


def _dot_tn(a, b):
    return lax.dot_general(a, b, (((0,), (0,)), ((), ())), preferred_element_type=F32)


def _sigmoid(x):
    return 1.0 / (1.0 + jnp.exp(-x))


def _rope_tables(L, LC):
    half = RET_DK // 2
    nf = half // 2
    t = np.arange(L)
    row = (t // GRID_W).astype(np.float32)
    col = (t % GRID_W).astype(np.float32)
    inv = (np.float32(ROPE_BASE) ** (-np.arange(nf, dtype=np.float32) / np.float32(nf))).astype(np.float32)
    ang = np.concatenate([row[:, None] * inv, col[:, None] * inv], axis=-1).astype(np.float32)
    cos, sin = np.cos(ang).astype(np.float32), np.sin(ang).astype(np.float32)
    cos2 = np.concatenate([cos, cos], axis=-1)
    sin2 = np.concatenate([-sin, sin], axis=-1)
    cos2 = np.concatenate([cos2, np.ones((LC, RET_DK), np.float32)], axis=0)
    sin2 = np.concatenate([sin2, np.zeros((LC, RET_DK), np.float32)], axis=0)
    return jnp.asarray(cos2), jnp.asarray(sin2)


def _mod_call(c8, wada_f, b_ada):
    ws = wada_f.shape[2]

    def body(c_ref, w_ref, b_ref, o_ref):
        a = c_ref[...]
        a = (a * _sigmoid(a)).astype(BF16)
        for s in range(N_SHARD):
            o_ref[:, s * ws:(s + 1) * ws] = _dot(a, w_ref[s]) + b_ref[:, s * ws:(s + 1) * ws]

    return pl.pallas_call(
        body, name="ada_mod", out_shape=jax.ShapeDtypeStruct((8, 3 * D), F32),
        compiler_params=_params())(c8, wada_f, b_ada)


def _dc_masks():
    cq = lax.broadcasted_iota(jnp.int32, (GRID_W, GRID_W), 0)
    ck = lax.broadcasted_iota(jnp.int32, (GRID_W, GRID_W), 1)
    dc = jnp.clip(ck - cq + 15, 0, 30)
    c0 = jnp.clip(cq - 8, 0, GRID_W - 16)
    col_ok = (ck >= c0) & (ck < c0 + 16)
    return dc, col_ok


def _bias_blocks():
    out = []
    for typ, delta in enumerate((4, 0, -4)):
        for rq in range(4):
            for rkk in range(12):
                dr = rkk + delta - rq - 4
                if typ == 0:
                    ok = -rq <= dr <= 7 - rq
                elif typ == 1:
                    ok = -4 <= dr <= 3
                else:
                    ok = -4 - rq <= dr <= 3 - rq
                out.append((typ, rq, rkk, dr if ok else None))
    return out


def _bias_body(r_ref, bias_ref, et_ref):
    dc, col_ok = _dc_masks()
    masks = [(dc == j).astype(F32) for j in range(31)]

    def per_h(h, carry):
        for dr in range(15):
            t = jnp.zeros((GRID_W, GRID_W), F32)
            for j in range(31):
                t = t + masks[j] * r_ref[h, dr * 31 + j]
            et_ref[dr] = jnp.where(col_ok, t, NEG)
        neg = jnp.full((GRID_W, GRID_W), NEG, F32)
        for typ, rq, rkk, dr in _bias_blocks():
            blk = neg if dr is None else et_ref[dr + 7]
            bias_ref[h, typ, rq * 64:(rq + 1) * 64, rkk * 64:(rkk + 1) * 64] = blk
        return carry

    lax.fori_loop(0, bias_ref.shape[0], per_h, 0)


def _bias_tile_sums(db_ref, hh):
    acc = {}
    for typ, rq, rkk, dr in _bias_blocks():
        if dr is None:
            continue
        blk = db_ref[hh, typ, rq * 64:(rq + 1) * 64, rkk * 64:(rkk + 1) * 64]
        acc[dr] = blk if dr not in acc else acc[dr] + blk
    return acc


def _small_reduce_body(dt_ref, dlg_ref, drpb_ref, dlgo_ref, p_ref):
    dc, _ = _dc_masks()
    masks = [(dc == j).astype(F32) for j in range(31)]
    ones = jnp.ones((8, GRID_W), F32)
    p_ref[...] = jnp.zeros_like(p_ref)
    drpb_ref[...] = jnp.zeros_like(drpb_ref)

    def per_h(h, carry):
        for dr in range(-7, 8):
            t = dt_ref[h, dr + 7]
            for j in range(31):
                p_ref[j:j + 1, :] = jnp.sum(t * masks[j], axis=0, keepdims=True)
            red = lax.dot_general(ones, p_ref[...], (((1,), (1,)), ((), ())),
                                  precision=HIGHEST, preferred_element_type=F32)
            drpb_ref[h, dr + 7:dr + 8, :] = red[0:1, :]
        return carry

    lax.fori_loop(0, dt_ref.shape[0], per_h, 0)
    x = dlg_ref[0]
    for b in range(1, dlg_ref.shape[0]):
        x = x + dlg_ref[b]
    x = x.reshape(4 * 8, x.shape[-1])
    dlgo_ref[...] = jnp.dot(x, jnp.ones((x.shape[-1], 128), F32), precision=HIGHEST,
                            preferred_element_type=F32)


def _inproj_gather_call(order, x, ctx, mod, norm_g, win_b, cos2, sin2):
    B, L, _ = x.shape
    LC = ctx.shape[1]
    T = L + LC
    nl, nt = L // TQ, T // TQ
    assert LC == TQ and L % TQ == 0
    kscale = RET_DK ** -0.5
    HR = D // 2

    def body(ord_ref, x_ref, ctx_ref, mod_ref, g_ref, wown_ref, cos_ref, sin_ref, p_ref, h_ref, wf_ref,
             w_all, hs_ref, ssem, rsem, lsem):
        j, b, t = pl.program_id(0), pl.program_id(1), pl.program_id(2)
        first = (b == 0) & (t == 0)
        mx, my, mc = _mesh_pos()
        s = 2 * mx + my
        sib = (mx, my, 1 - mc)
        own = pltpu.make_async_copy(wown_ref, w_all.at[s], lsem.at[0])
        ici_send, ici_recv, fwd_send, fwd_recv, outs = [], [], [], [], [
            pltpu.make_async_copy(w_all.at[s], wf_ref.at[s], lsem.at[1])]
        for k, (px, py) in enumerate(_other_chips(mx, my)):
            ps = 2 * px + py
            mine = w_all.at[s, pl.ds(mc * HR, HR)]
            ici_send.append(_remote(mine, w_all.at[s, pl.ds(mc * HR, HR)], ssem, rsem, k, (px, py, mc)))
            ici_recv.append(_remote(mine, w_all.at[ps, pl.ds(mc * HR, HR)], ssem, rsem, k, (px, py, mc)))
            got = w_all.at[ps, pl.ds(mc * HR, HR)]
            fwd_send.append(_remote(got, got, ssem, rsem, 3 + k, sib))
            theirs = w_all.at[ps, pl.ds((1 - mc) * HR, HR)]
            fwd_recv.append(_remote(theirs, theirs, ssem, rsem, 3 + k, sib))
            outs.append(pltpu.make_async_copy(w_all.at[ps], wf_ref.at[ps], lsem.at[2 + k]))

        @pl.when(first & (j == 0))
        def _():
            own.start()
            own.wait()
            for cp in ici_send:
                cp.start()
            outs[0].start()

        for k in range(3):
            @pl.when(first & (j == k + 1))
            def _(k=k):
                ici_recv[k].wait_recv()
                fwd_send[k].start()
                fwd_recv[k].wait_recv()
                outs[1 + k].start()

        tile = b * nt + t

        @pl.when(j == 0)
        def _():
            is_lat = t < nl
            xt = jnp.where(is_lat, x_ref[...], ctx_ref[...])
            mrow = mod_ref[pl.ds(jnp.where(is_lat, b, B), 1), :]
            shift, scale = mrow[:, 0:D], mrow[:, D:2 * D]
            rstd = lax.rsqrt(jnp.mean(xt * xt, axis=-1, keepdims=True) + EPS)
            h0 = ((xt * rstd * g_ref[...]) * (1.0 + scale) + shift).astype(BF16)
            h_ref[...] = h0
            hs_ref[tile] = h0

        hb = hs_ref[tile]
        cs, sn = cos_ref[...], sin_ref[...]
        shard = ord_ref[j]
        for sh in range(N_SHARD):
            @pl.when(shard == sh)
            def _(sh=sh):
                for half in range(2):
                    sec = 2 * sh + half
                    acc = _dot(hb, w_all[sh, :, half * 512:(half + 1) * 512])
                    if sec == 0:
                        acc = acc * (NA_DH ** -0.5)
                    if sec in (4, 5):
                        for q in range(4):
                            a = acc[:, q * 128:(q + 1) * 128]
                            r = a * cs + pltpu.roll(a, 64, 1) * sn
                            if sec == 5:
                                r = r * kscale
                            p_ref[:, half * 512 + q * 128:half * 512 + (q + 1) * 128] = r.astype(BF16)
                    else:
                        p_ref[:, half * 512:(half + 1) * 512] = acc.astype(BF16)

        @pl.when((j == N_SHARD - 1) & (b == B - 1) & (t == nt - 1))
        def _():
            _finish(outs, ici_send + fwd_send, [])

    tok = lambda j, b, t, o: (jnp.where(j == 0, b, B - 1), jnp.where(j == 0, jnp.minimum(t, nl - 1), nl - 1), 0)
    grid_spec = pltpu.PrefetchScalarGridSpec(
        num_scalar_prefetch=1, grid=(N_SHARD, B, nt),
        in_specs=[
            pl.BlockSpec((None, TQ, D), tok),
            pl.BlockSpec((None, TQ, D), lambda j, b, t, o: (jnp.where(j == 0, b, B - 1), 0, 0)),
            pl.BlockSpec((8, 3 * D), lambda j, b, t, o: (0, 0)),
            pl.BlockSpec((1, D), lambda j, b, t, o: (0, 0)),
            ANY,
            pl.BlockSpec((TQ, RET_DK), lambda j, b, t, o: (t, 0)),
            pl.BlockSpec((TQ, RET_DK), lambda j, b, t, o: (t, 0)),
        ],
        out_specs=(pl.BlockSpec((None, TQ, D), lambda j, b, t, o: (b, t, o[j])),
                   pl.BlockSpec((None, TQ, D), lambda j, b, t, o: (
                       jnp.where(j == 0, b, B - 1), jnp.where(j == 0, t, nt - 1), 0)), ANY),
        scratch_shapes=[pltpu.VMEM((N_SHARD, D, D), BF16), pltpu.VMEM((B * nt, TQ, D), BF16),
                        pltpu.SemaphoreType.DMA((6,)), pltpu.SemaphoreType.DMA((6,)),
                        pltpu.SemaphoreType.DMA((5,))])
    return pl.pallas_call(
        body, name="in_proj", grid_spec=grid_spec,
        out_shape=(jax.ShapeDtypeStruct((B, T, 4 * D), BF16), jax.ShapeDtypeStruct((B, T, D), BF16),
                   jax.ShapeDtypeStruct((N_SHARD, D, D), BF16)),
        compiler_params=_params(("arbitrary",) * 3))(order, x, ctx, mod, norm_g, win_b, cos2, sin2)


def _na_specs(L, T, rows):
    nm = rows // 4
    q_spec = pl.BlockSpec((None, TQ, 128), lambda hp, b, m: (b, m, hp))
    k_spec = pl.BlockSpec((None, T, 128), lambda hp, b, m: (b, 0, 4 + hp))
    v_spec = pl.BlockSpec((None, T, 128), lambda hp, b, m: (b, 0, 8 + hp))
    g_spec = pl.BlockSpec((None, TQ, 128), lambda hp, b, m: (b, m, 12 + hp))
    bias_spec = pl.BlockSpec((2, 3, TQ, KW), lambda hp, b, m: (hp, 0, 0, 0))
    return nm, q_spec, k_spec, v_spec, g_spec, bias_spec


def _na_tile(m, nm, rows):
    typ = jnp.where(m == 0, 0, jnp.where(m == nm - 1, 2, 1))
    start = pl.multiple_of(jnp.clip(4 * m - 4, 0, rows - 12) * GRID_W, TQ)
    return typ, start


def _na_fwd_call(P, bias, L, LC):
    B, T, _ = P.shape
    rows = L // GRID_W
    nm, q_spec, k_spec, v_spec, g_spec, bias_spec = _na_specs(L, T, rows)

    def body(q_ref, k_ref, v_ref, g_ref, bias_ref, y_ref, o_ref):
        typ, start = _na_tile(pl.program_id(2), nm, rows)
        for hh in range(2):
            ln = slice(hh * NA_DH, (hh + 1) * NA_DH)
            q = q_ref[:, ln]
            kw, vw = k_ref[pl.ds(start, KW), ln], v_ref[pl.ds(start, KW), ln]
            kc, vc = k_ref[L:L + LC, ln], v_ref[L:L + LC, ln]
            s1 = _dot_nt(q, kw) + bias_ref[hh, typ]
            s2 = _dot_nt(q, kc)
            mx = jnp.maximum(jnp.max(s1, axis=-1, keepdims=True), jnp.max(s2, axis=-1, keepdims=True))
            p1, p2 = jnp.exp(s1 - mx), jnp.exp(s2 - mx)
            inv = 1.0 / (jnp.sum(p1, axis=-1, keepdims=True) + jnp.sum(p2, axis=-1, keepdims=True))
            o = (_dot(p1.astype(BF16), vw) + _dot(p2.astype(BF16), vc)) * inv
            g = g_ref[:, ln].astype(F32)
            o_ref[:, ln] = o.astype(BF16)
            y_ref[:, ln] = (o * (g * _sigmoid(g))).astype(BF16)

    tile = pl.BlockSpec((None, TQ, 128), lambda hp, b, m: (b, m, hp))
    return pl.pallas_call(
        body, name="na_fwd", grid=(4, B, nm),
        in_specs=[q_spec, k_spec, v_spec, g_spec, bias_spec],
        out_specs=(tile, tile),
        out_shape=(jax.ShapeDtypeStruct((B, L, 512), BF16),) * 2,
        compiler_params=_params(("arbitrary",) * 3))(P, P, P, P, bias)


def _na_bwd_call(P, bias, dY, o_na, L, LC):
    B, T, _ = P.shape
    rows = L // GRID_W
    nm, q_spec, k_spec, v_spec, g_spec, bias_spec = _na_specs(L, T, rows)
    scale = NA_DH ** -0.5

    RB = 32

    def body(q_ref, k_ref, v_ref, g_ref, bias_ref, dy_ref, o_ref, dq_ref, dg_ref, dk_ref, dv_ref, dt_ref,
             db_ref, s1_ref, s2_ref, dp1_ref, dp2_ref, p1_ref, p2_ref, ds1_ref, ds2_ref):
        b, m = pl.program_id(1), pl.program_id(2)
        typ, start = _na_tile(m, nm, rows)

        @pl.when(m == 0)
        def _():
            dk_ref[...] = jnp.zeros_like(dk_ref)
            dv_ref[...] = jnp.zeros_like(dv_ref)

        @pl.when((m == 0) & (b == 0))
        def _():
            db_ref[...] = jnp.zeros_like(db_ref)

        for hh in range(2):
            ln = slice(hh * NA_DH, (hh + 1) * NA_DH)
            q = q_ref[:, ln]
            kw, vw = k_ref[pl.ds(start, KW), ln], v_ref[pl.ds(start, KW), ln]
            kc, vc = k_ref[L:L + LC, ln], v_ref[L:L + LC, ln]
            g = g_ref[:, ln].astype(F32)
            sg = _sigmoid(g)
            dy = dy_ref[:, ln].astype(F32)
            do = (dy * (g * sg)).astype(BF16)
            s1_ref[...] = _dot_nt(q, kw)
            s2_ref[...] = _dot_nt(q, kc)
            dp1_ref[...] = _dot_nt(do, vw)
            dp2_ref[...] = _dot_nt(do, vc)

            def rows_pass(r, carry, hh=hh):
                rw = pl.ds(pl.multiple_of(r * RB, RB), RB)
                a = s1_ref[rw, :] + bias_ref[hh, typ, rw, :]
                c = s2_ref[rw, :]
                mx = jnp.maximum(jnp.max(a, axis=-1, keepdims=True), jnp.max(c, axis=-1, keepdims=True))
                e1, e2 = jnp.exp(a - mx), jnp.exp(c - mx)
                inv = 1.0 / (jnp.sum(e1, axis=-1, keepdims=True) + jnp.sum(e2, axis=-1, keepdims=True))
                p1, p2 = e1 * inv, e2 * inv
                p1_ref[rw, :] = p1.astype(BF16)
                p2_ref[rw, :] = p2.astype(BF16)
                dp1, dp2 = dp1_ref[rw, :], dp2_ref[rw, :]
                delta = jnp.sum(p1 * dp1, axis=-1, keepdims=True) + jnp.sum(p2 * dp2, axis=-1, keepdims=True)
                ds1 = p1 * (dp1 - delta)
                db_ref[hh, typ, rw, :] += ds1
                ds1_ref[rw, :] = ds1.astype(BF16)
                ds2_ref[rw, :] = (p2 * (dp2 - delta)).astype(BF16)
                return carry

            lax.fori_loop(0, TQ // RB, rows_pass, 0, unroll=True)
            p1b, p2b, ds1b, ds2b = p1_ref[...], p2_ref[...], ds1_ref[...], ds2_ref[...]
            dg_ref[:, ln] = (dy * o_ref[:, ln].astype(F32) * (sg * (1.0 + g * (1.0 - sg)))).astype(BF16)
            dq_ref[:, ln] = ((_dot(ds1b, kw) + _dot(ds2b, kc)) * scale).astype(BF16)
            dk_ref[pl.ds(start, KW), ln] += _dot_tn(ds1b, q)
            dv_ref[pl.ds(start, KW), ln] += _dot_tn(p1b, do)
            dk_ref[L:L + LC, ln] += _dot_tn(ds2b, q)
            dv_ref[L:L + LC, ln] += _dot_tn(p2b, do)

        @pl.when((m == nm - 1) & (b == B - 1))
        def _():
            for hh in range(2):
                for dr, t in _bias_tile_sums(db_ref, hh).items():
                    dt_ref[hh, dr + 7] = t

    tile = pl.BlockSpec((None, TQ, 128), lambda hp, b, m: (b, m, hp))
    kv_out = pl.BlockSpec((None, T, 128), lambda hp, b, m: (b, 0, hp))
    wide, narrow = (TQ, KW), (TQ, LC)
    return pl.pallas_call(
        body, name="na_bwd", grid=(4, B, nm),
        in_specs=[q_spec, k_spec, v_spec, g_spec, bias_spec, tile, tile],
        out_specs=(tile, tile, kv_out, kv_out,
                   pl.BlockSpec((2, 15, GRID_W, GRID_W), lambda hp, b, m: (hp, 0, 0, 0))),
        out_shape=(jax.ShapeDtypeStruct((B, L, 512), BF16), jax.ShapeDtypeStruct((B, L, 512), BF16),
                   jax.ShapeDtypeStruct((B, T, 512), F32), jax.ShapeDtypeStruct((B, T, 512), F32),
                   jax.ShapeDtypeStruct((bias.shape[0], 15, GRID_W, GRID_W), F32)),
        scratch_shapes=[pltpu.VMEM((2,) + bias.shape[1:], F32),
                        pltpu.VMEM(wide, F32), pltpu.VMEM(narrow, F32), pltpu.VMEM(wide, F32), pltpu.VMEM(narrow, F32),
                        pltpu.VMEM(wide, BF16), pltpu.VMEM(narrow, BF16), pltpu.VMEM(wide, BF16),
                        pltpu.VMEM(narrow, BF16)],
        compiler_params=_params(("arbitrary",) * 3))(P, P, P, P, bias, dY, o_na)


def _head_scalar(dec_ref, h):
    lane = lax.broadcasted_iota(jnp.int32, dec_ref.shape, 1)
    return -jnp.sum(jnp.where(lane == h, jnp.exp(dec_ref[...]), 0.0), axis=1, keepdims=True)


def _ret_specs(T):
    q_spec = pl.BlockSpec((None, TQ, 128), lambda b, h, i: (b, i, 16 + h))
    k_spec = pl.BlockSpec((None, T, 128), lambda b, h, i: (b, 0, 20 + h))
    v_spec = pl.BlockSpec((None, T, 128), lambda b, h, i: (b, 0, 24 + h))
    g_spec = pl.BlockSpec((None, TQ, 128), lambda b, h, i: (b, i, 28 + h))
    dec_spec = pl.BlockSpec((1, 4), lambda b, h, i: (0, 0))
    gn_spec = pl.BlockSpec((1, 128), lambda b, h, i: (0, h))
    return q_spec, k_spec, v_spec, g_spec, dec_spec, gn_spec


def _chunk_decay(lgf, lgb):
    tau = lax.broadcasted_iota(jnp.int32, (TQ, 1), 0).astype(F32)
    sig = lax.broadcasted_iota(jnp.int32, (1, TQ), 1).astype(F32)
    dist = tau - sig
    dm = jnp.exp(dist * jnp.where(dist > 0, lgf, -lgb)) * jnp.where(dist == 0, 2.0, 1.0)
    return tau, dist, dm


def _ret_states_call(P, dec_f, dec_b, L, LC):
    B, T, _ = P.shape
    n = L // TQ

    def body(df_ref, db_ref, k_ref, v_ref, sf_ref, sb_ref):
        h = pl.program_id(1)
        lgf, lgb = _head_scalar(df_ref, h), _head_scalar(db_ref, h)
        tau = lax.broadcasted_iota(jnp.int32, (TQ, 1), 0).astype(F32)
        jc = lax.broadcasted_iota(jnp.int32, (LC, 1), 0).astype(F32)
        wf, wb = jnp.exp(lgf * (TQ - 1.0 - tau)), jnp.exp(lgb * tau)
        gcf, gcb = jnp.exp(lgf * float(TQ)), jnp.exp(lgb * float(TQ))
        kc, vc = k_ref[L:L + LC, :].astype(F32), v_ref[L:L + LC, :]

        def chunk_state(i, w):
            ks = pl.multiple_of(i * TQ, TQ)
            return _dot_tn((k_ref[pl.ds(ks, TQ), :].astype(F32) * w).astype(BF16), v_ref[pl.ds(ks, TQ), :])

        def fwd(i, s):
            sf_ref[i] = s
            return gcf * s + chunk_state(i, wf)

        lax.fori_loop(0, n, fwd, _dot_tn((kc * jnp.exp(lgf * (LC - 1.0 - jc))).astype(BF16), vc), unroll=True)

        def bwd(r, s):
            i = n - 1 - r
            sb_ref[i] = s
            return gcb * s + chunk_state(i, wb)

        lax.fori_loop(0, n, bwd, _dot_tn((kc * jnp.exp(lgb * jc)).astype(BF16), vc), unroll=True)

    st = pl.BlockSpec((None, None, n, RET_DK, RET_DK), lambda b, h: (b, h, 0, 0, 0))
    return pl.pallas_call(
        body, name="ret_states", grid=(B, 4),
        in_specs=[pl.BlockSpec((1, 4), lambda b, h: (0, 0)), pl.BlockSpec((1, 4), lambda b, h: (0, 0)),
                  pl.BlockSpec((None, T, 128), lambda b, h: (b, 0, 20 + h)),
                  pl.BlockSpec((None, T, 128), lambda b, h: (b, 0, 24 + h))],
        out_specs=(st, st),
        out_shape=(jax.ShapeDtypeStruct((B, 4, n, RET_DK, RET_DK), F32),) * 2,
        compiler_params=_params(("arbitrary",) * 2))(dec_f, dec_b, P, P)


def _retc_fwd_call(P, sf, sb, dec_f, dec_b, ret_norm_g, L):
    B, T, _ = P.shape
    q_spec, _, _, g_spec, dec_spec, gn_spec = _ret_specs(T)
    k_spec = pl.BlockSpec((None, TQ, 128), lambda b, h, i: (b, i, 20 + h))
    v_spec = pl.BlockSpec((None, TQ, 128), lambda b, h, i: (b, i, 24 + h))
    st_spec = pl.BlockSpec((None, None, None, RET_DK, RET_DK), lambda b, h, i: (b, h, i, 0, 0))

    def body(df_ref, db_ref, q_ref, k_ref, v_ref, g_ref, gn_ref, sf_ref, sb_ref, y_ref, o_ref):
        h = pl.program_id(1)
        lgf, lgb = _head_scalar(df_ref, h), _head_scalar(db_ref, h)
        tau, _, dm = _chunk_decay(lgf, lgb)
        q = q_ref[...]
        qf = q.astype(F32)
        acc = _dot((_dot_nt(q, k_ref[...]) * dm).astype(BF16), v_ref[...])
        acc = acc + _dot((qf * jnp.exp(lgf * (tau + 1.0))).astype(BF16), sf_ref[...].astype(BF16))
        acc = acc + _dot((qf * jnp.exp(lgb * (TQ - tau))).astype(BF16), sb_ref[...].astype(BF16))
        o_ref[...] = acc
        rn = lax.rsqrt(jnp.mean(acc * acc, axis=-1, keepdims=True) + EPS)
        g = g_ref[...].astype(F32)
        y_ref[...] = ((acc * rn * gn_ref[...]) * (g * _sigmoid(g))).astype(BF16)

    tile = pl.BlockSpec((None, TQ, 128), lambda b, h, i: (b, i, h))
    return pl.pallas_call(
        body, name="ret_fwd", grid=(B, 4, L // TQ),
        in_specs=[dec_spec, dec_spec, q_spec, k_spec, v_spec, g_spec, gn_spec, st_spec, st_spec],
        out_specs=(tile, tile),
        out_shape=(jax.ShapeDtypeStruct((B, L, 512), BF16), jax.ShapeDtypeStruct((B, L, 512), F32)),
        compiler_params=_params(("arbitrary",) * 3))(dec_f, dec_b, P, P, P, P, ret_norm_g, sf, sb)


def _retc_bwd_call(P, sf, sb, dec_f, dec_b, ret_norm_g, o_ret, dY, cos2, sin2, L, LC):
    B, T, _ = P.shape
    n = L // TQ
    C = float(TQ)
    kscale = RET_DK ** -0.5
    q_spec, k_spec, v_spec, g_spec, dec_spec, gn_spec = _ret_specs(T)
    st_spec = pl.BlockSpec((None, None, n, RET_DK, RET_DK), lambda b, h, i: (b, h, 0, 0, 0))

    def body(df_ref, db_ref, q_ref, k_ref, v_ref, g_ref, gn_ref, o_ref, dy_ref, cos_ref, sin_ref, sf_ref, sb_ref,
             dq_ref, dg_ref, dk_ref, dv_ref, dgn_ref, dlg_ref, dsf_ref, dsb_ref):
        h, i = pl.program_id(1), pl.program_id(2)
        lgf, lgb = _head_scalar(df_ref, h), _head_scalar(db_ref, h)
        tau, dist, dm = _chunk_decay(lgf, lgb)

        @pl.when(i == 0)
        def _():
            dk_ref[...] = jnp.zeros_like(dk_ref)
            dv_ref[...] = jnp.zeros_like(dv_ref)
            dgn_ref[...] = jnp.zeros_like(dgn_ref)
            dlg_ref[...] = jnp.zeros_like(dlg_ref)

        def add_lg(row, x):
            cs = jnp.sum(x, axis=0, keepdims=True)
            tot = cs[:, 0:128]
            for part in range(1, x.shape[1] // 128):
                tot = tot + cs[:, part * 128:(part + 1) * 128]
            dlg_ref[row:row + 1, :] += tot

        q = q_ref[...]
        qf = q.astype(F32)
        o = o_ref[...]
        g = g_ref[...].astype(F32)
        dy = dy_ref[...].astype(F32)
        gn = gn_ref[...]
        sg = _sigmoid(g)
        rn = lax.rsqrt(jnp.mean(o * o, axis=-1, keepdims=True) + EPS)
        nrm = o * rn
        dg_ref[...] = (dy * (nrm * gn) * (sg * (1.0 + g * (1.0 - sg)))).astype(BF16)
        dhn = dy * (g * sg)
        dgn_ref[...] += jnp.sum(dhn * nrm, axis=0, keepdims=True)
        dnrm = dhn * gn
        do = rn * (dnrm - nrm * jnp.mean(dnrm * nrm, axis=-1, keepdims=True))
        dob = do.astype(BF16)
        rows = pl.ds(pl.multiple_of(i * TQ, TQ), TQ)
        ki, vi = k_ref[rows, :], v_ref[rows, :]
        s = _dot_nt(q, ki)
        dsv = _dot_nt(dob, vi)
        dsb = (dsv * dm).astype(BF16)
        dk_ref[rows, :] += _dot_tn(dsb, q)
        dv_ref[rows, :] += _dot_tn((s * dm).astype(BF16), dob)
        xw = s * dsv * dm * jnp.abs(dist)
        fpart = jnp.where(dist > 0, xw, 0.0)
        add_lg(0, fpart)
        add_lg(1, xw - fpart)
        dq = _dot(dsb, ki)
        af, ab = jnp.exp(lgf * (tau + 1.0)), jnp.exp(lgb * (C - tau))
        qa, qb = (qf * af).astype(BF16), (qf * ab).astype(BF16)
        sfi, sbi = sf_ref[i].astype(BF16), sb_ref[i].astype(BF16)
        dq = dq + af * _dot_nt(dob, sfi) + ab * _dot_nt(dob, sbi)
        dsf_ref[i] = _dot_tn(qa, dob)
        dsb_ref[i] = _dot_tn(qb, dob)
        add_lg(0, (tau + 1.0) * (_dot(qa, sfi) * do))
        add_lg(1, (C - tau) * (_dot(qb, sbi) * do))
        cs, sn = cos_ref[rows, :], sin_ref[rows, :]
        dq_ref[...] = (dq * cs - pltpu.roll(dq, 64, 1) * sn).astype(BF16)

        @pl.when(i == n - 1)
        def _():
            jc = lax.broadcasted_iota(jnp.int32, (LC, 1), 0).astype(F32)
            crow = pl.ds(L, LC)

            def through_state(rws, w, dw, gst, row):
                kk, vv = k_ref[rws, :].astype(F32), v_ref[rws, :]
                gb = gst.astype(BF16)
                vg = _dot_nt(vv, gb)
                kw = kk * w
                dk_ref[rws, :] += w * vg
                dv_ref[rws, :] += _dot(kw.astype(BF16), gb)
                add_lg(row, dw * (kw * vg))

            def scan(lg, gc, w, dw, st_ref, dst_ref, order, row):
                def step(r, gst):
                    j = order(r)
                    through_state(pl.ds(pl.multiple_of(j * TQ, TQ), TQ), w, dw, gst, row)
                    add_lg(row, (C * gc) * (gst * st_ref[j]))
                    return dst_ref[j] + gc * gst
                return lax.fori_loop(0, n, step, jnp.zeros((RET_DK, RET_DK), F32), unroll=True)

            gcf, gcb = jnp.exp(lgf * C), jnp.exp(lgb * C)
            g0 = scan(lgf, gcf, jnp.exp(lgf * (C - 1.0 - tau)), C - 1.0 - tau, sf_ref, dsf_ref,
                      lambda r: n - 1 - r, 0)
            through_state(crow, jnp.exp(lgf * (LC - 1.0 - jc)), LC - 1.0 - jc, g0, 0)
            g1 = scan(lgb, gcb, jnp.exp(lgb * tau), tau, sb_ref, dsb_ref, lambda r: r, 1)
            through_state(crow, jnp.exp(lgb * jc), jc, g1, 1)
            dk = dk_ref[...]
            dk_ref[...] = (dk * cos_ref[...] - pltpu.roll(dk, 64, 1) * sin_ref[...]) * kscale

    tile = pl.BlockSpec((None, TQ, 128), lambda b, h, i: (b, i, h))
    kv_out = pl.BlockSpec((None, T, 128), lambda b, h, i: (b, 0, h))
    tab = pl.BlockSpec((T, RET_DK), lambda b, h, i: (0, 0))
    return pl.pallas_call(
        body, name="ret_bwd", grid=(B, 4, n),
        in_specs=[dec_spec, dec_spec, q_spec, k_spec, v_spec, g_spec, gn_spec, tile,
                  pl.BlockSpec((None, TQ, 128), lambda b, h, i: (b, i, 4 + h)), tab, tab, st_spec, st_spec],
        out_specs=(tile, tile, kv_out, kv_out,
                   pl.BlockSpec((None, 1, 128), lambda b, h, i: (b, 0, h)),
                   pl.BlockSpec((None, None, 8, 128), lambda b, h, i: (b, h, 0, 0))),
        out_shape=(jax.ShapeDtypeStruct((B, L, 512), BF16), jax.ShapeDtypeStruct((B, L, 512), BF16),
                   jax.ShapeDtypeStruct((B, T, 512), F32), jax.ShapeDtypeStruct((B, T, 512), F32),
                   jax.ShapeDtypeStruct((B, 1, 512), F32), jax.ShapeDtypeStruct((B, 4, 8, 128), F32)),
        scratch_shapes=[pltpu.VMEM((n, RET_DK, RET_DK), F32), pltpu.VMEM((n, RET_DK, RET_DK), F32)],
        compiler_params=_params(("arbitrary",) * 3))(
            dec_f, dec_b, P, P, P, P, ret_norm_g, o_ret, dY, cos2, sin2, sf, sb)


def _out_call(y_na, y_ret, x, target, mod, final_g, wout_f):
    B, L, _ = x.shape

    def body(yn_ref, yr_ref, x_ref, t_ref, mod_ref, gf_ref, w_ref, dy_ref, dx2_ref, dw_ref, sm_ref):
        b, i = pl.program_id(0), pl.program_id(1)

        @pl.when((b == 0) & (i == 0))
        def _():
            dw_ref[...] = jnp.zeros_like(dw_ref)
            sm_ref[...] = jnp.zeros_like(sm_ref)

        gate = mod_ref[pl.ds(b, 1), 2 * D:3 * D]
        gf = gf_ref[...]
        yn, yr = yn_ref[...], yr_ref[...]
        ylat = _dot(yn, w_ref[0:512, :]) + _dot(yr, w_ref[512:1024, :])
        x2 = x_ref[...] + gate * ylat
        r = lax.rsqrt(jnp.mean(x2 * x2, axis=-1, keepdims=True) + EPS)
        xr = x2 * r
        err = xr * gf - t_ref[...]
        sm_ref[1:2, :] += jnp.sum(err * err, axis=0, keepdims=True)
        dout = err * (1.0 / D)
        sm_ref[0:1, :] += jnp.sum(dout * xr, axis=0, keepdims=True)
        gd = dout * gf
        dx2 = r * (gd - xr * jnp.mean(gd * xr, axis=-1, keepdims=True))
        dx2_ref[...] = dx2
        sm_ref[pl.ds(2 + b, 1), :] += jnp.sum(dx2 * ylat, axis=0, keepdims=True)
        dyl = (gate * dx2).astype(BF16)
        dy_ref[:, 0:512] = _dot_nt(dyl, w_ref[0:512, :]).astype(BF16)
        dy_ref[:, 512:1024] = _dot_nt(dyl, w_ref[512:1024, :]).astype(BF16)
        dw_ref[0:512, :] += _dot_tn(yn, dyl)
        dw_ref[512:1024, :] += _dot_tn(yr, dyl)

    half = pl.BlockSpec((None, TQ, 512), lambda b, i: (b, i, 0))
    full = pl.BlockSpec((None, TQ, D), lambda b, i: (b, i, 0))
    return pl.pallas_call(
        body, name="out_proj_loss", grid=(B, L // TQ),
        in_specs=[half, half, full, full,
                  pl.BlockSpec((8, 3 * D), lambda b, i: (0, 0)),
                  pl.BlockSpec((1, D), lambda b, i: (0, 0)),
                  pl.BlockSpec((D, D), lambda b, i: (0, 0))],
        out_specs=(full, full, pl.BlockSpec((D, D), lambda b, i: (0, 0)),
                   pl.BlockSpec((8, D), lambda b, i: (0, 0))),
        out_shape=(jax.ShapeDtypeStruct((B, L, D), BF16), jax.ShapeDtypeStruct((B, L, D), F32),
                   jax.ShapeDtypeStruct((D, D), F32), jax.ShapeDtypeStruct((8, D), F32)),
        compiler_params=_params(("arbitrary",) * 2))(y_na, y_ret, x, target, mod, final_g, wout_f)


def _dh_call(dsec, win_f, x, ctx, dx2, mod, norm_g, cp_in, cp_out):
    B, L, _ = x.shape
    LC = ctx.shape[1]
    nl = L // TQ

    def body(d0, d1, d2, d3, d4, d5, d6, d7, w_ref, x_ref, ctx_ref, dx2_ref, mod_ref, g_ref, cpi_ref, cpo_ref,
             gx_ref, sm_ref, sli_ref, slo_ref, ssem, rsem, lsem):
        drefs = (d0, d1, d2, d3, d4, d5, d6, d7)
        b, t = pl.program_id(0), pl.program_id(1)
        is_lat = t < nl

        @pl.when((b == 0) & (t == 0))
        def _():
            sm_ref[...] = jnp.zeros_like(sm_ref)

        def dh_of(secs):
            acc = jnp.zeros((TQ, D), F32)
            for sec in secs:
                s, half = divmod(sec, 2)
                acc = acc + _dot_nt(drefs[sec][...].astype(BF16), w_ref[s, :, half * 512:(half + 1) * 512])
            return acc

        def norm_bwd(dh, xt, mrow):
            scale = mrow[:, D:2 * D]
            g = g_ref[...]
            rstd = lax.rsqrt(jnp.mean(xt * xt, axis=-1, keepdims=True) + EPS)
            xn = xt * rstd
            dshift = jnp.sum(dh, axis=0, keepdims=True)
            dscale = jnp.sum(dh * (xn * g), axis=0, keepdims=True)
            dhn = dh * (1.0 + scale)
            sm_ref[0:1, :] += jnp.sum(dhn * xn, axis=0, keepdims=True)
            dxn = dhn * g
            dx = rstd * (dxn - xn * jnp.mean(dxn * xn, axis=-1, keepdims=True))
            return dshift, dscale, dx

        @pl.when(is_lat)
        def _():
            dshift, dscale, dx = norm_bwd(dh_of(range(8)), x_ref[...], mod_ref[pl.ds(b, 1), :])
            sm_ref[pl.ds(3 + b, 1), :] += dshift
            sm_ref[pl.ds(3 + B + b, 1), :] += dscale
            gx_ref[...] = dx2_ref[...] + dx

        @pl.when(jnp.logical_not(is_lat))
        def _():
            dshift, dscale, _ = norm_bwd(dh_of((1, 2, 5, 6)), ctx_ref[...], mod_ref[B:B + 1, :])
            sm_ref[1:2, :] += dshift
            sm_ref[2:3, :] += dscale

        mx, my, mc = _mesh_pos()
        s = 2 * mx + my
        cps, sls = (cpi_ref, cpo_ref), (sli_ref, slo_ref)
        own = [pltpu.make_async_copy(cps[a].at[s], sls[a].at[s], lsem.at[a]) for a in range(2)]
        sends, recvs, k = [], [], 0
        for px, py in _other_chips(mx, my):
            ps = 2 * px + py
            for a in range(2):
                sends.append(_remote(cps[a].at[ps], sls[a].at[s], ssem, rsem, k, (px, py, mc)))
                recvs.append(_remote(cps[a].at[s], sls[a].at[ps], ssem, rsem, k, (px, py, mc)))
                k += 1

        @pl.when((b == 0) & (t == 0))
        def _():
            for cp in own + sends:
                cp.start()

        @pl.when((b == B - 1) & (t == nl))
        def _():
            _finish(own, sends, recvs)

    lat = lambda b, t: (b, jnp.minimum(t, nl - 1), 0)
    tok = lambda b, t: (b, t, 0)
    sec_specs = [pl.BlockSpec((None, TQ, 512), lat if sec in (0, 3, 4, 7) else tok) for sec in range(8)]
    return pl.pallas_call(
        body, name="dh_norm_bwd", grid=(B, nl + 1),
        in_specs=sec_specs + [
            pl.BlockSpec((N_SHARD, D, D), lambda b, t: (0, 0, 0)),
            pl.BlockSpec((None, TQ, D), lat),
            pl.BlockSpec((None, LC, D), lambda b, t: (b, 0, 0)),
            pl.BlockSpec((None, TQ, D), lat),
            pl.BlockSpec((8, 3 * D), lambda b, t: (0, 0)),
            pl.BlockSpec((1, D), lambda b, t: (0, 0)), ANY, ANY],
        out_specs=(pl.BlockSpec((None, TQ, D), lat), pl.BlockSpec((8, D), lambda b, t: (0, 0)), ANY, ANY),
        out_shape=(jax.ShapeDtypeStruct((B, L, D), F32), jax.ShapeDtypeStruct((8, D), F32),
                   jax.ShapeDtypeStruct(cp_in.shape, cp_in.dtype), jax.ShapeDtypeStruct(cp_out.shape, cp_out.dtype)),
        scratch_shapes=[pltpu.SemaphoreType.DMA((6,)), pltpu.SemaphoreType.DMA((6,)),
                        pltpu.SemaphoreType.DMA((2,))],
        compiler_params=_params(("arbitrary",) * 2))(*dsec, win_f, x, ctx, dx2, mod, norm_g, cp_in, cp_out)


def _dw_call(dsec, h, L):
    B, T, _ = h.shape
    nl = L // TQ

    def body(d0, d1, d2, d3, d4, d5, d6, d7, h_ref, dw_ref, acc_ref):
        drefs = (d0, d1, d2, d3, d4, d5, d6, d7)
        b, t = pl.program_id(0), pl.program_id(1)

        @pl.when((b == 0) & (t == 0))
        def _():
            acc_ref[...] = jnp.zeros_like(acc_ref)

        hb = h_ref[...]

        def add(secs):
            for sec in secs:
                s, half = divmod(sec, 2)
                acc_ref[s, :, half * 512:(half + 1) * 512] += _dot_tn(hb, drefs[sec][...].astype(BF16))

        @pl.when(t < nl)
        def _():
            add(range(8))

        @pl.when(t >= nl)
        def _():
            add((1, 2, 5, 6))

        @pl.when((b == B - 1) & (t == nl))
        def _():
            dw_ref[...] = acc_ref[...].astype(BF16)

    lat = lambda b, t: (b, jnp.minimum(t, nl - 1), 0)
    tok = lambda b, t: (b, t, 0)
    sec_specs = [pl.BlockSpec((None, TQ, 512), lat if sec in (0, 3, 4, 7) else tok) for sec in range(8)]
    return pl.pallas_call(
        body, name="dw_in", grid=(B, nl + 1),
        in_specs=sec_specs + [pl.BlockSpec((None, TQ, D), tok)],
        out_specs=pl.BlockSpec((N_SHARD, D, D), lambda b, t: (0, 0, 0)),
        out_shape=jax.ShapeDtypeStruct((N_SHARD, D, D), BF16),
        scratch_shapes=[pltpu.VMEM((N_SHARD, D, D), F32)],
        compiler_params=_params(("arbitrary",) * 2, vmem_mb=56))(*dsec, h)


def _mesh_pos():
    return lax.axis_index("x"), lax.axis_index("y"), lax.axis_index("c")


def _flip(v, f):
    return 1 - v if f else v


def _remote(src, dst, ssem, rsem, k, peer):
    return pltpu.make_async_remote_copy(src_ref=src, dst_ref=dst, send_sem=ssem.at[k], recv_sem=rsem.at[k],
                                        device_id=peer, device_id_type=MESH)


def _other_chips(x, y):
    return [(_flip(x, fx), _flip(y, fy)) for fx, fy in ((1, 0), (0, 1), (1, 1))]


def _all_to_all_small(src, dst_all, ssem, rsem, k0, x, y, cc):
    me = 4 * x + 2 * y + cc
    sends, recvs = [], []
    for f in range(1, N_DEV):
        px, py, pc = _flip(x, f & 4), _flip(y, f & 2), _flip(cc, f & 1)
        sends.append(_remote(src, dst_all.at[me], ssem, rsem, k0 + f - 1, (px, py, pc)))
        recvs.append(_remote(src, dst_all.at[4 * px + 2 * py + pc], ssem, rsem, k0 + f - 1, (px, py, pc)))
    return sends, recvs


def _finish(local, sends, recvs):
    for cp in recvs:
        cp.wait_recv()
    for cp in sends:
        cp.wait_send()
    for cp in local:
        cp.wait()


def _gather_call(wout_b, wada_b, c, rpb_flat):
    arrs = (wout_b, wada_b)
    na = len(arrs)
    hrs = [a.shape[0] // 2 for a in arrs]

    def body(wout, wada, c_ref, r_ref, wout_f, wada_f, c_all, bias_ref, et_ref, ssem, rsem, lsem):
        x, y, cc = _mesh_pos()
        s, me = 2 * x + y, 4 * x + 2 * y + cc
        sib = (x, y, 1 - cc)
        srcs, dsts = (wout, wada), (wout_f, wada_f)

        def half(a, shard, hc):
            return dsts[a].at[shard, pl.ds(hc * hrs[a], hrs[a])]

        local = [pltpu.make_async_copy(srcs[a], dsts[a].at[s], lsem.at[a]) for a in range(na)]
        local.append(pltpu.make_async_copy(c_ref, c_all.at[me], lsem.at[na]))
        ici_send, ici_recv, fwd_send, fwd_recv, k = [], [], [], [], 0
        for px, py in _other_chips(x, y):
            ps = 2 * px + py
            for a in range(na):
                mine = srcs[a].at[pl.ds(cc * hrs[a], hrs[a])]
                ici_send.append(_remote(mine, half(a, s, cc), ssem, rsem, k, (px, py, cc)))
                ici_recv.append(_remote(mine, half(a, ps, cc), ssem, rsem, k, (px, py, cc)))
                fwd_send.append(_remote(half(a, ps, cc), half(a, ps, cc), ssem, rsem, 3 * na + k, sib))
                fwd_recv.append(_remote(half(a, ps, 1 - cc), half(a, ps, 1 - cc), ssem, rsem, 3 * na + k, sib))
                k += 1
        c_send, c_recv = _all_to_all_small(c_ref, c_all, ssem, rsem, 6 * na, x, y, cc)
        for cp in local + ici_send + c_send:
            cp.start()
        _bias_body(r_ref, bias_ref, et_ref)
        for got, fwd in zip(ici_recv, fwd_send):
            got.wait_recv()
            fwd.start()
        _finish(local, ici_send + fwd_send + c_send, fwd_recv + c_recv)

    return pl.pallas_call(
        body, name="weight_gather",
        in_specs=[pl.BlockSpec(memory_space=pltpu.VMEM)] * 3 + [pl.BlockSpec(memory_space=pltpu.SMEM)],
        out_specs=(pl.BlockSpec(memory_space=pltpu.VMEM),) * 4,
        out_shape=tuple(jax.ShapeDtypeStruct((N_SHARD,) + a.shape, a.dtype) for a in arrs)
        + (jax.ShapeDtypeStruct((N_DEV,) + c.shape, c.dtype),
           jax.ShapeDtypeStruct((rpb_flat.shape[0], 3, TQ, KW), F32)),
        scratch_shapes=[pltpu.VMEM((15, GRID_W, GRID_W), F32),
                        pltpu.SemaphoreType.DMA((6 * na + 7,)), pltpu.SemaphoreType.DMA((6 * na + 7,)),
                        pltpu.SemaphoreType.DMA((na + 1,))],
        compiler_params=pltpu.CompilerParams(vmem_limit_bytes=56 << 20))(wout_b, wada_b, c, rpb_flat)


VROWS = 32


def _grad_halves_call(dwin_b, dwout_b, dbias, dlg):
    arrs = (dwin_b, dwout_b)
    hrs = [a.shape[1] // 2 for a in arrs]

    def body(din, dout, db_ref, dlg_ref, cp_in, cp_out, drpb_ref, dlgo_ref, got_in, got_out, p_ref, ssem, rsem):
        x, y, cc = _mesh_pos()
        sib = (x, y, 1 - cc)
        srcs, gots, cps = (din, dout), (got_in, got_out), (cp_in, cp_out)
        halves = [_remote(srcs[a].at[:, pl.ds((1 - cc) * hrs[a], hrs[a])], gots[a], ssem, rsem, a, sib)
                  for a in range(2)]
        for cp in halves:
            cp.start()
        _small_reduce_body(db_ref, dlg_ref, drpb_ref, dlgo_ref, p_ref)
        for cp in halves:
            cp.wait_recv()
        for a in range(2):
            for j in range(N_SHARD):
                def add(i, carry, a=a, j=j):
                    r = pl.multiple_of(i * VROWS, VROWS)
                    mine = srcs[a][j, pl.ds(pl.multiple_of(cc * hrs[a] + r, VROWS), VROWS), :].astype(F32)
                    cps[a][j, pl.ds(r, VROWS), :] = (
                        mine + gots[a][j, pl.ds(r, VROWS), :].astype(F32)).astype(BF16)
                    return carry
                lax.fori_loop(0, hrs[a] // VROWS, add, 0)
        for cp in halves:
            cp.wait_send()

    vmem = pl.BlockSpec(memory_space=pltpu.VMEM)
    half_shapes = [(N_SHARD, hrs[a], arrs[a].shape[2]) for a in range(2)]
    return pl.pallas_call(
        body, name="grad_halves",
        in_specs=[vmem] * 4, out_specs=(vmem,) * 4,
        out_shape=(jax.ShapeDtypeStruct(half_shapes[0], BF16), jax.ShapeDtypeStruct(half_shapes[1], BF16),
                   jax.ShapeDtypeStruct((dbias.shape[0], 16, 32), F32), jax.ShapeDtypeStruct((32, 128), F32)),
        scratch_shapes=[pltpu.VMEM(half_shapes[0], BF16), pltpu.VMEM(half_shapes[1], BF16),
                        pltpu.VMEM((32, GRID_W), F32),
                        pltpu.SemaphoreType.DMA((2,)), pltpu.SemaphoreType.DMA((2,))],
        compiler_params=pltpu.CompilerParams(vmem_limit_bytes=56 << 20))(dwin_b, dwout_b, dbias, dlg)


def _grad_finish_call(sl_in, sl_out, small):
    arrs = (sl_in, sl_out)

    def body(sin, sout, sm, gin, gout, sm_all, h_in, h_out, ssem, rsem, lsem):
        x, y, cc = _mesh_pos()
        me = 4 * x + 2 * y + cc
        sib = (x, y, 1 - cc)
        sls, hs, gs = (sin, sout), (h_in, h_out), (gin, gout)
        sm_send, sm_recv = _all_to_all_small(sm, sm_all, ssem, rsem, 2, x, y, cc)
        sm_own = pltpu.make_async_copy(sm, sm_all.at[me], lsem.at[0])
        for cp in sm_send + [sm_own]:
            cp.start()
        for a in range(2):
            def total(i, carry, a=a):
                rows = pl.ds(pl.multiple_of(i * VROWS, VROWS), VROWS)
                sl = sls[a]
                hs[a][rows, :] = ((sl[0, rows, :].astype(F32) + sl[1, rows, :].astype(F32))
                                  + sl[2, rows, :].astype(F32)) + sl[3, rows, :].astype(F32)
                return carry
            lax.fori_loop(0, arrs[a].shape[1] // VROWS, total, 0)
        mine = [pltpu.make_async_copy(hs[a], gs[a].at[cc], lsem.at[1 + a]) for a in range(2)]
        back = [_remote(hs[a], gs[a].at[cc], ssem, rsem, a, sib) for a in range(2)]
        back_recv = [_remote(hs[a], gs[a].at[1 - cc], ssem, rsem, a, sib) for a in range(2)]
        for cp in mine + back:
            cp.start()
        _finish(mine + [sm_own], back + sm_send, back_recv + sm_recv)

    vmem = pl.BlockSpec(memory_space=pltpu.VMEM)
    return pl.pallas_call(
        body, name="grad_finish",
        in_specs=[vmem] * 3, out_specs=(vmem,) * 3,
        out_shape=(jax.ShapeDtypeStruct((2,) + sl_in.shape[1:], F32),
                   jax.ShapeDtypeStruct((2,) + sl_out.shape[1:], F32),
                   jax.ShapeDtypeStruct((N_DEV,) + small.shape, F32)),
        scratch_shapes=[pltpu.VMEM(sl_in.shape[1:], F32), pltpu.VMEM(sl_out.shape[1:], F32),
                        pltpu.SemaphoreType.DMA((9,)), pltpu.SemaphoreType.DMA((9,)),
                        pltpu.SemaphoreType.DMA((3,))],
        compiler_params=pltpu.CompilerParams(vmem_limit_bytes=48 << 20))(sl_in, sl_out, small)


def _adamw(w, g, m, v):
    m = ADAM_B1 * m + (1.0 - ADAM_B1) * g
    v = ADAM_B2 * v + (1.0 - ADAM_B2) * (g * g)
    m_hat = m / (1.0 - ADAM_B1 ** ADAM_STEP)
    v_hat = v / (1.0 - ADAM_B2 ** ADAM_STEP)
    return -ADAM_LR * (m_hat / (jnp.sqrt(v_hat) + ADAM_EPS) + ADAM_WD * w), m, v


def _adam_call(w, m, v, g, name):
    R, C = w.shape
    tr = 256

    def body(w_ref, m_ref, v_ref, g_ref, d_ref, mo_ref, vo_ref):
        d_ref[...], mo_ref[...], vo_ref[...] = _adamw(w_ref[...], g_ref[...], m_ref[...], v_ref[...])

    spec = pl.BlockSpec((tr, C), lambda i: (i, 0))
    return pl.pallas_call(
        body, name=name, grid=(R // tr,), in_specs=[spec] * 4,
        out_specs=(spec,) * 3, out_shape=(jax.ShapeDtypeStruct((R, C), F32),) * 3,
        compiler_params=_params(("arbitrary",)))(w, m, v, g)


R_GF, R_NG, R_LOSS, R_RNG, R_LGF, R_LGB, R_SHIFT, R_SCALE, R_GATE, R_SHIFT_C, R_SCALE_C, R_RNG2, R_RPB = (
    0, 1, 2, 3, 4, 5, 6, 8, 10, 12, 13, 14, 16)
W_GF, W_NG, W_CCTX, W_RNG, W_DF, W_DB, W_BADA, W_RPB = 0, 1, 2, 3, 4, 5, 6, 9


def _small_final_call(sm_all, c_t, c_ctx, wada_f, wada, m_ada, v_ada, wsm, msm, vsm, B):
    ws = wada.shape[1]
    NB = N_DEV * B

    def body(sm_ref, ct_ref, cctx_ref, wf_ref, wa_ref, ma_ref, va_ref, w_ref, m_ref, v_ref,
             g_ref, d_ref, mo_ref, vo_ref, ga_ref, da_ref, mao_ref, vao_ref, loss_ref, dmod_ref):
        x, y, _ = _mesh_pos()
        s = 2 * x + y
        tot = sm_ref[0]
        for dv in range(1, N_DEV):
            tot = tot + sm_ref[dv]
        w = w_ref[...]
        for dv in range(N_DEV):
            for b in range(B):
                r = dv * B + b
                for part, row in enumerate((R_SHIFT, R_SCALE, R_GATE)):
                    dmod_ref[r:r + 1, part * D:(part + 1) * D] = sm_ref[dv, row + b:row + b + 1, :]
        dmod_ref[NB:NB + 1, 0:D] = tot[R_SHIFT_C:R_SHIFT_C + 1, :]
        dmod_ref[NB:NB + 1, D:2 * D] = tot[R_SCALE_C:R_SCALE_C + 1, :]
        dmod_ref[NB:NB + 1, 2 * D:3 * D] = jnp.zeros((1, D), F32)
        dmod_ref[NB + 1:, :] = jnp.zeros((dmod_ref.shape[0] - NB - 1, 3 * D), F32)
        dmod = dmod_ref[...]
        cc = cctx_ref[...]
        scc = _sigmoid(cc)
        ct = ct_ref[...]
        act_t = ct * _sigmoid(ct)
        dmc = dmod[NB:NB + 1, :].astype(BF16)
        dact = jnp.zeros((1, D), F32)
        for sh in range(N_SHARD):
            dact = dact + _dot_nt(dmc[:, sh * ws:(sh + 1) * ws], wf_ref[sh])
        g = jnp.zeros((16, D), F32)
        rows = lax.broadcasted_iota(jnp.int32, (16, D), 0)

        def put(g, row, val):
            return jnp.where(rows == row, val, g)

        g = put(g, W_GF, tot[R_GF:R_GF + 1, :])
        g = put(g, W_NG, tot[R_NG:R_NG + 1, :])
        g = put(g, W_CCTX, dact * (scc * (1.0 + cc * (1.0 - scc))))
        g = put(g, W_RNG, tot[R_RNG:R_RNG + 1, :] + tot[R_RNG2:R_RNG2 + 1, :])
        g = put(g, W_DF, tot[R_LGF:R_LGF + 1, :] * (-jnp.exp(w[W_DF:W_DF + 1, :])))
        g = put(g, W_DB, tot[R_LGB:R_LGB + 1, :] * (-jnp.exp(w[W_DB:W_DB + 1, :])))
        db = jnp.sum(dmod, axis=0, keepdims=True)
        for part in range(3):
            g = put(g, W_BADA + part, db[:, part * D:(part + 1) * D])
        for part in range(4):
            g = put(g, W_RPB + part, tot[R_RPB + part:R_RPB + part + 1, :])
        g_ref[...] = g
        d_ref[...], mo_ref[...], vo_ref[...] = _adamw(w, g, m_ref[...], v_ref[...])
        loss_ref[...] = jnp.broadcast_to(
            (0.5 / D) * jnp.sum(tot[R_LOSS:R_LOSS + 1, :], axis=1, keepdims=True), (8, 128))
        for sh in range(N_SHARD):
            @pl.when(s == sh)
            def _():
                ga = jnp.dot(act_t, dmod[:, sh * ws:(sh + 1) * ws], precision=HIGHEST,
                             preferred_element_type=F32)
                ga_ref[...] = ga
                da_ref[...], mao_ref[...], vao_ref[...] = _adamw(wa_ref[...], ga, ma_ref[...], va_ref[...])

    sh_small = jax.ShapeDtypeStruct((16, D), F32)
    sh_ada = jax.ShapeDtypeStruct(wada.shape, F32)
    return pl.pallas_call(
        body, name="small_final",
        out_shape=(sh_small,) * 4 + (sh_ada,) * 4 + (jax.ShapeDtypeStruct((8, 128), F32),),
        scratch_shapes=[pltpu.VMEM((NB + 8, 3 * D), F32)],
        compiler_params=_params(vmem_mb=56))(
            sm_all, c_t, c_ctx, wada_f, wada, m_ada, v_ada, wsm, msm, vsm)


def _local_step(order, x, c, ctx, c_ctx, norm_g, wada_f, b_ada, win_b, bias, dec_f, dec_b, ret_norm_g,
                wout_f, final_g, target):
    B, L, _ = x.shape
    LC = ctx.shape[1]
    assert B == 2
    cos2, sin2 = _rope_tables(L, LC)
    c8 = jnp.concatenate([c, c_ctx[None, :], jnp.zeros((8 - B - 1, D), F32)], axis=0)
    mod = _mod_call(c8, wada_f, b_ada)
    P, h, win_f = _inproj_gather_call(order, x, ctx, mod, norm_g, win_b, cos2, sin2)
    y_na, o_na = _na_fwd_call(P, bias, L, LC)
    sf, sb = _ret_states_call(P, dec_f, dec_b, L, LC)
    y_ret, o_ret = _retc_fwd_call(P, sf, sb, dec_f, dec_b, ret_norm_g, L)
    dY, dx2, dwout_p, sm_out = _out_call(y_na, y_ret, x, target, mod, final_g, wout_f.reshape(D, D))
    dnq, dng, dnk, dnv, dbias = _na_bwd_call(P, bias, dY, o_na, L, LC)
    drq, drg, drk, drv, dgn, dlg = _retc_bwd_call(P, sf, sb, dec_f, dec_b, ret_norm_g, o_ret, dY, cos2, sin2, L, LC)
    dsec = (dnq, dnk, dnv, dng, drq, drk, drv, drg)
    dwin_b = _dw_call(dsec, h, L)
    cp_in, cp_out, drpb, dlg_sum = _grad_halves_call(
        dwin_b, dwout_p.astype(BF16).reshape(N_SHARD, D // N_SHARD, D), dbias, dlg)
    grad_x, sm_dh, sl_in, sl_out = _dh_call(dsec, win_f, x, ctx, dx2, mod, norm_g, cp_in, cp_out)
    z = jnp.zeros((1, D), F32)
    pad = lambda v: jnp.pad(v.reshape(1, -1), ((0, 0), (0, D - v.size)))
    dlg_sum = dlg_sum.reshape(4, 8, 128)
    rpb_rows = jnp.pad(drpb[:, :15, :31].reshape(-1), (0, 4 * D - drpb.shape[0] * 465)).reshape(4, D)
    small = jnp.concatenate([
        sm_out[0:1], sm_dh[0:1], sm_out[1:2], pad(dgn[0]), pad(dlg_sum[:, 0, 0]), pad(dlg_sum[:, 1, 0]),
        sm_dh[3:5], sm_dh[5:7], sm_out[2:4], sm_dh[1:2], sm_dh[2:3], pad(dgn[1]), z, rpb_rows,
        jnp.zeros((SM_ROWS - 20, D), F32)], axis=0)
    return grad_x, sl_in, sl_out, small


def kernel(x, c, ctx, c_ctx, norm_g, w_ada, b_ada, w_in, na_rpb, ret_decay_fwd, ret_decay_bwd, ret_norm_g, w_out, final_norm_g, loss_target, m_c_ctx, m_norm_g, m_w_ada, m_b_ada, m_w_in, m_na_rpb, m_ret_decay_fwd, m_ret_decay_bwd, m_ret_norm_g, m_w_out, m_final_norm_g, v_c_ctx, v_norm_g, v_w_ada, v_b_ada, v_w_in, v_na_rpb, v_ret_decay_fwd, v_ret_decay_bwd, v_ret_norm_g, v_w_out, v_final_norm_g):
    B = x.shape[0]
    wout_f, wada_f, c_all, bias = _gather_call(
        w_out[0].astype(BF16), w_ada[0].astype(BF16), c, na_rpb[0].reshape(na_rpb.shape[1], -1))
    mx, my = lax.axis_index("x"), lax.axis_index("y")
    order = jnp.stack([2 * mx + my, 2 * (1 - mx) + my, 2 * mx + (1 - my),
                       2 * (1 - mx) + (1 - my)]).astype(jnp.int32)
    grad_x, sl_in, sl_out, small = _local_step(
        order, x, c, ctx, c_ctx, norm_g, wada_f, b_ada, w_in[0].astype(BF16), bias, ret_decay_fwd,
        ret_decay_bwd, ret_norm_g, wout_f, final_norm_g.reshape(1, D), loss_target)
    gin, gout, sm_all = _grad_finish_call(sl_in, sl_out, small)
    g_win, g_wout = gin.reshape(w_in.shape[1:]), gout.reshape(w_out.shape[1:])
    d_win, nm_win, nv_win = _adam_call(w_in[0], m_w_in[0], v_w_in[0], g_win, "adam_w_in")
    d_wout, nm_wout, nv_wout = _adam_call(w_out[0], m_w_out[0], v_w_out[0], g_wout, "adam_w_out")

    def pack(gf, ng, cc, rng, df, db, bada, rpb):
        pad = lambda v: jnp.pad(v.reshape(1, -1), ((0, 0), (0, D - v.size)))
        return jnp.concatenate([
            gf.reshape(1, D), ng.reshape(1, D), cc.reshape(1, D), pad(rng), pad(df), pad(db),
            bada.reshape(3, D), jnp.pad(rpb.reshape(-1), (0, 4 * D - rpb.size)).reshape(4, D),
            jnp.zeros((3, D), F32)], axis=0)

    wsm = pack(final_norm_g, norm_g, c_ctx, ret_norm_g, ret_decay_fwd, ret_decay_bwd, b_ada, na_rpb)
    msm = pack(m_final_norm_g, m_norm_g, m_c_ctx, m_ret_norm_g, m_ret_decay_fwd, m_ret_decay_bwd, m_b_ada, m_na_rpb)
    vsm = pack(v_final_norm_g, v_norm_g, v_c_ctx, v_ret_norm_g, v_ret_decay_fwd, v_ret_decay_bwd, v_b_ada, v_na_rpb)
    c_t = jnp.concatenate([c_all.reshape(N_DEV * B, D), c_ctx.reshape(1, D), jnp.zeros((7, D), F32)], axis=0).T
    outs = _small_final_call(sm_all, c_t, c_ctx.reshape(1, D), wada_f,
                             w_ada[0], m_w_ada[0], v_w_ada[0], wsm, msm, vsm, B)
    smalls, adas, loss = outs[0:4], outs[4:8], outs[8][0, 0]

    def unpack(p):
        rw = ret_norm_g.shape[1]
        return dict(
            final_norm_g=p[W_GF], norm_g=p[W_NG:W_NG + 1], c_ctx=p[W_CCTX], ret_norm_g=p[W_RNG:W_RNG + 1, :rw],
            ret_decay_fwd=p[W_DF:W_DF + 1, :4], ret_decay_bwd=p[W_DB:W_DB + 1, :4],
            b_ada=p[W_BADA:W_BADA + 3].reshape(1, 3 * D),
            na_rpb=p[W_RPB:W_RPB + 4].reshape(-1)[:na_rpb.size].reshape(na_rpb.shape))

    res = []
    for p, ada, win_o, wout_o in zip(smalls, adas, (g_win, d_win, nm_win, nv_win),
                                     (g_wout, d_wout, nm_wout, nv_wout)):
        u = unpack(p)
        res.append([u["c_ctx"], u["norm_g"], ada[None], u["b_ada"], win_o[None], u["na_rpb"],
                    u["ret_decay_fwd"], u["ret_decay_bwd"], u["ret_norm_g"], wout_o[None], u["final_norm_g"]])
    return (loss, grad_x, *res[0], *res[1], *res[2], *res[3])
```

```python
import numpy as np
import jax
import jax.numpy as jnp
from jax import lax
from jax.experimental import pallas as pl
from jax.experimental.pallas import tpu as pltpu

F32 = jnp.float32
BF16 = jnp.bfloat16
HIGHEST = lax.Precision.HIGHEST

D = 1024
GRID_W = 64
NA_DH = 64
RET_DK = 128
ROPE_BASE = 10000.0
EPS = 1e-6
NEG = -1e30
TQ = 256
KW = 12 * GRID_W
N_SHARD = 4
N_DEV = 8
SM_ROWS = 24

ADAM_LR = 0.001
ADAM_B1 = 0.9
ADAM_B2 = 0.999
ADAM_EPS = 1e-08
ADAM_WD = 0.01
ADAM_STEP = 10

MESH = pl.DeviceIdType.MESH
ANY = pl.BlockSpec(memory_space=pl.ANY)


def _params(sem=None, vmem_mb=48):
    return pltpu.CompilerParams(dimension_semantics=sem, vmem_limit_bytes=vmem_mb << 20)


def _dot(a, b):
    return jnp.dot(a, b, preferred_element_type=F32)


def _dot_nt(a, b):
    return lax.dot_general(a, b, (((1,), (1,)), ((), ())), preferred_element_type=F32)


def _dot_tn(a, b):
    return lax.dot_general(a, b, (((0,), (0,)), ((), ())), preferred_element_type=F32)


def _sigmoid(x):
    return 1.0 / (1.0 + jnp.exp(-x))


def _rope_tables(L, LC):
    half = RET_DK // 2
    nf = half // 2
    t = np.arange(L)
    row = (t // GRID_W).astype(np.float32)
    col = (t % GRID_W).astype(np.float32)
    inv = (np.float32(ROPE_BASE) ** (-np.arange(nf, dtype=np.float32) / np.float32(nf))).astype(np.float32)
    ang = np.concatenate([row[:, None] * inv, col[:, None] * inv], axis=-1).astype(np.float32)
    cos, sin = np.cos(ang).astype(np.float32), np.sin(ang).astype(np.float32)
    cos2 = np.concatenate([cos, cos], axis=-1)
    sin2 = np.concatenate([-sin, sin], axis=-1)
    cos2 = np.concatenate([cos2, np.ones((LC, RET_DK), np.float32)], axis=0)
    sin2 = np.concatenate([sin2, np.zeros((LC, RET_DK), np.float32)], axis=0)
    return jnp.asarray(cos2), jnp.asarray(sin2)


def _mod_call(c8, wada_f, b_ada):
    ws = wada_f.shape[2]

    def body(c_ref, w_ref, b_ref, o_ref):
        a = c_ref[...]
        a = (a * _sigmoid(a)).astype(BF16)
        for s in range(N_SHARD):
            o_ref[:, s * ws:(s + 1) * ws] = _dot(a, w_ref[s]) + b_ref[:, s * ws:(s + 1) * ws]

    return pl.pallas_call(
        body, name="ada_mod", out_shape=jax.ShapeDtypeStruct((8, 3 * D), F32),
        compiler_params=_params())(c8, wada_f, b_ada)


def _dc_masks():
    cq = lax.broadcasted_iota(jnp.int32, (GRID_W, GRID_W), 0)
    ck = lax.broadcasted_iota(jnp.int32, (GRID_W, GRID_W), 1)
    dc = jnp.clip(ck - cq + 15, 0, 30)
    c0 = jnp.clip(cq - 8, 0, GRID_W - 16)
    col_ok = (ck >= c0) & (ck < c0 + 16)
    return dc, col_ok


def _bias_blocks():
    out = []
    for typ, delta in enumerate((4, 0, -4)):
        for rq in range(4):
            for rkk in range(12):
                dr = rkk + delta - rq - 4
                if typ == 0:
                    ok = -rq <= dr <= 7 - rq
                elif typ == 1:
                    ok = -4 <= dr <= 3
                else:
                    ok = -4 - rq <= dr <= 3 - rq
                out.append((typ, rq, rkk, dr if ok else None))
    return out


def _bias_body(r_ref, bias_ref, et_ref):
    dc, col_ok = _dc_masks()
    masks = [(dc == j).astype(F32) for j in range(31)]

    def per_h(h, carry):
        for dr in range(15):
            t = jnp.zeros((GRID_W, GRID_W), F32)
            for j in range(31):
                t = t + masks[j] * r_ref[h, dr * 31 + j]
            et_ref[dr] = jnp.where(col_ok, t, NEG)
        neg = jnp.full((GRID_W, GRID_W), NEG, F32)
        for typ, rq, rkk, dr in _bias_blocks():
            blk = neg if dr is None else et_ref[dr + 7]
            bias_ref[h, typ, rq * 64:(rq + 1) * 64, rkk * 64:(rkk + 1) * 64] = blk
        return carry

    lax.fori_loop(0, bias_ref.shape[0], per_h, 0)


def _bias_tile_sums(db_ref, hh):
    acc = {}
    for typ, rq, rkk, dr in _bias_blocks():
        if dr is None:
            continue
        blk = db_ref[hh, typ, rq * 64:(rq + 1) * 64, rkk * 64:(rkk + 1) * 64]
        acc[dr] = blk if dr not in acc else acc[dr] + blk
    return acc


def _small_reduce_body(dt_ref, dlg_ref, drpb_ref, dlgo_ref, p_ref):
    dc, _ = _dc_masks()
    masks = [(dc == j).astype(F32) for j in range(31)]
    ones = jnp.ones((8, GRID_W), F32)
    p_ref[...] = jnp.zeros_like(p_ref)
    drpb_ref[...] = jnp.zeros_like(drpb_ref)

    def per_h(h, carry):
        for dr in range(-7, 8):
            t = dt_ref[h, dr + 7]
            for j in range(31):
                p_ref[j:j + 1, :] = jnp.sum(t * masks[j], axis=0, keepdims=True)
            red = lax.dot_general(ones, p_ref[...], (((1,), (1,)), ((), ())),
                                  precision=HIGHEST, preferred_element_type=F32)
            drpb_ref[h, dr + 7:dr + 8, :] = red[0:1, :]
        return carry

    lax.fori_loop(0, dt_ref.shape[0], per_h, 0)
    x = dlg_ref[0]
    for b in range(1, dlg_ref.shape[0]):
        x = x + dlg_ref[b]
    x = x.reshape(4 * 8, x.shape[-1])
    dlgo_ref[...] = jnp.dot(x, jnp.ones((x.shape[-1], 128), F32), precision=HIGHEST,
                            preferred_element_type=F32)


def _inproj_gather_call(order, x, ctx, mod, norm_g, win_b, cos2, sin2):
    B, L, _ = x.shape
    LC = ctx.shape[1]
    T = L + LC
    nl, nt = L // TQ, T // TQ
    assert LC == TQ and L % TQ == 0
    kscale = RET_DK ** -0.5
    HR = D // 2

    def body(ord_ref, x_ref, ctx_ref, mod_ref, g_ref, wown_ref, cos_ref, sin_ref, p_ref, h_ref, wf_ref,
             w_all, hs_ref, ssem, rsem, lsem):
        j, b, t = pl.program_id(0), pl.program_id(1), pl.program_id(2)
        first = (b == 0) & (t == 0)
        mx, my, mc = _mesh_pos()
        s = 2 * mx + my
        sib = (mx, my, 1 - mc)
        own = pltpu.make_async_copy(wown_ref, w_all.at[s], lsem.at[0])
        ici_send, ici_recv, fwd_send, fwd_recv, outs = [], [], [], [], [
            pltpu.make_async_copy(w_all.at[s], wf_ref.at[s], lsem.at[1])]
        for k, (px, py) in enumerate(_other_chips(mx, my)):
            ps = 2 * px + py
            mine = w_all.at[s, pl.ds(mc * HR, HR)]
            ici_send.append(_remote(mine, w_all.at[s, pl.ds(mc * HR, HR)], ssem, rsem, k, (px, py, mc)))
            ici_recv.append(_remote(mine, w_all.at[ps, pl.ds(mc * HR, HR)], ssem, rsem, k, (px, py, mc)))
            got = w_all.at[ps, pl.ds(mc * HR, HR)]
            fwd_send.append(_remote(got, got, ssem, rsem, 3 + k, sib))
            theirs = w_all.at[ps, pl.ds((1 - mc) * HR, HR)]
            fwd_recv.append(_remote(theirs, theirs, ssem, rsem, 3 + k, sib))
            outs.append(pltpu.make_async_copy(w_all.at[ps], wf_ref.at[ps], lsem.at[2 + k]))

        @pl.when(first & (j == 0))
        def _():
            own.start()
            own.wait()
            for cp in ici_send:
                cp.start()
            outs[0].start()

        for k in range(3):
            @pl.when(first & (j == k + 1))
            def _(k=k):
                ici_recv[k].wait_recv()
                fwd_send[k].start()
                fwd_recv[k].wait_recv()
                outs[1 + k].start()

        tile = b * nt + t

        @pl.when(j == 0)
        def _():
            is_lat = t < nl
            xt = jnp.where(is_lat, x_ref[...], ctx_ref[...])
            mrow = mod_ref[pl.ds(jnp.where(is_lat, b, B), 1), :]
            shift, scale = mrow[:, 0:D], mrow[:, D:2 * D]
            rstd = lax.rsqrt(jnp.mean(xt * xt, axis=-1, keepdims=True) + EPS)
            h0 = ((xt * rstd * g_ref[...]) * (1.0 + scale) + shift).astype(BF16)
            h_ref[...] = h0
            hs_ref[tile] = h0

        hb = hs_ref[tile]
        cs, sn = cos_ref[...], sin_ref[...]
        shard = ord_ref[j]
        for sh in range(N_SHARD):
            @pl.when(shard == sh)
            def _(sh=sh):
                for half in range(2):
                    sec = 2 * sh + half
                    acc = _dot(hb, w_all[sh, :, half * 512:(half + 1) * 512])
                    if sec == 0:
                        acc = acc * (NA_DH ** -0.5)
                    if sec in (4, 5):
                        for q in range(4):
                            a = acc[:, q * 128:(q + 1) * 128]
                            r = a * cs + pltpu.roll(a, 64, 1) * sn
                            if sec == 5:
                                r = r * kscale
                            p_ref[:, half * 512 + q * 128:half * 512 + (q + 1) * 128] = r.astype(BF16)
                    else:
                        p_ref[:, half * 512:(half + 1) * 512] = acc.astype(BF16)

        @pl.when((j == N_SHARD - 1) & (b == B - 1) & (t == nt - 1))
        def _():
            _finish(outs, ici_send + fwd_send, [])

    tok = lambda j, b, t, o: (jnp.where(j == 0, b, B - 1), jnp.where(j == 0, jnp.minimum(t, nl - 1), nl - 1), 0)
    grid_spec = pltpu.PrefetchScalarGridSpec(
        num_scalar_prefetch=1, grid=(N_SHARD, B, nt),
        in_specs=[
            pl.BlockSpec((None, TQ, D), tok),
            pl.BlockSpec((None, TQ, D), lambda j, b, t, o: (jnp.where(j == 0, b, B - 1), 0, 0)),
            pl.BlockSpec((8, 3 * D), lambda j, b, t, o: (0, 0)),
            pl.BlockSpec((1, D), lambda j, b, t, o: (0, 0)),
            ANY,
            pl.BlockSpec((TQ, RET_DK), lambda j, b, t, o: (t, 0)),
            pl.BlockSpec((TQ, RET_DK), lambda j, b, t, o: (t, 0)),
        ],
        out_specs=(pl.BlockSpec((None, TQ, D), lambda j, b, t, o: (b, t, o[j])),
                   pl.BlockSpec((None, TQ, D), lambda j, b, t, o: (
                       jnp.where(j == 0, b, B - 1), jnp.where(j == 0, t, nt - 1), 0)), ANY),
        scratch_shapes=[pltpu.VMEM((N_SHARD, D, D), BF16), pltpu.VMEM((B * nt, TQ, D), BF16),
                        pltpu.SemaphoreType.DMA((6,)), pltpu.SemaphoreType.DMA((6,)),
                        pltpu.SemaphoreType.DMA((5,))])
    return pl.pallas_call(
        body, name="in_proj", grid_spec=grid_spec,
        out_shape=(jax.ShapeDtypeStruct((B, T, 4 * D), BF16), jax.ShapeDtypeStruct((B, T, D), BF16),
                   jax.ShapeDtypeStruct((N_SHARD, D, D), BF16)),
        compiler_params=_params(("arbitrary",) * 3))(order, x, ctx, mod, norm_g, win_b, cos2, sin2)


def _na_specs(L, T, rows):
    nm = rows // 4
    q_spec = pl.BlockSpec((None, TQ, 128), lambda hp, b, m: (b, m, hp))
    k_spec = pl.BlockSpec((None, T, 128), lambda hp, b, m: (b, 0, 4 + hp))
    v_spec = pl.BlockSpec((None, T, 128), lambda hp, b, m: (b, 0, 8 + hp))
    g_spec = pl.BlockSpec((None, TQ, 128), lambda hp, b, m: (b, m, 12 + hp))
    bias_spec = pl.BlockSpec((2, 3, TQ, KW), lambda hp, b, m: (hp, 0, 0, 0))
    return nm, q_spec, k_spec, v_spec, g_spec, bias_spec


def _na_tile(m, nm, rows):
    typ = jnp.where(m == 0, 0, jnp.where(m == nm - 1, 2, 1))
    start = pl.multiple_of(jnp.clip(4 * m - 4, 0, rows - 12) * GRID_W, TQ)
    return typ, start


def _na_fwd_call(P, bias, L, LC):
    B, T, _ = P.shape
    rows = L // GRID_W
    nm, q_spec, k_spec, v_spec, g_spec, bias_spec = _na_specs(L, T, rows)

    def body(q_ref, k_ref, v_ref, g_ref, bias_ref, y_ref, o_ref):
        typ, start = _na_tile(pl.program_id(2), nm, rows)
        for hh in range(2):
            ln = slice(hh * NA_DH, (hh + 1) * NA_DH)
            q = q_ref[:, ln]
            kw, vw = k_ref[pl.ds(start, KW), ln], v_ref[pl.ds(start, KW), ln]
            kc, vc = k_ref[L:L + LC, ln], v_ref[L:L + LC, ln]
            s1 = _dot_nt(q, kw) + bias_ref[hh, typ]
            s2 = _dot_nt(q, kc)
            mx = jnp.maximum(jnp.max(s1, axis=-1, keepdims=True), jnp.max(s2, axis=-1, keepdims=True))
            p1, p2 = jnp.exp(s1 - mx), jnp.exp(s2 - mx)
            inv = 1.0 / (jnp.sum(p1, axis=-1, keepdims=True) + jnp.sum(p2, axis=-1, keepdims=True))
            o = (_dot(p1.astype(BF16), vw) + _dot(p2.astype(BF16), vc)) * inv
            g = g_ref[:, ln].astype(F32)
            o_ref[:, ln] = o.astype(BF16)
            y_ref[:, ln] = (o * (g * _sigmoid(g))).astype(BF16)

    tile = pl.BlockSpec((None, TQ, 128), lambda hp, b, m: (b, m, hp))
    return pl.pallas_call(
        body, name="na_fwd", grid=(4, B, nm),
        in_specs=[q_spec, k_spec, v_spec, g_spec, bias_spec],
        out_specs=(tile, tile),
        out_shape=(jax.ShapeDtypeStruct((B, L, 512), BF16),) * 2,
        compiler_params=_params(("arbitrary",) * 3))(P, P, P, P, bias)


def _na_bwd_call(P, bias, dY, o_na, L, LC):
    B, T, _ = P.shape
    rows = L // GRID_W
    nm, q_spec, k_spec, v_spec, g_spec, bias_spec = _na_specs(L, T, rows)
    scale = NA_DH ** -0.5

    RB = 32

    def body(q_ref, k_ref, v_ref, g_ref, bias_ref, dy_ref, o_ref, dq_ref, dg_ref, dk_ref, dv_ref, dt_ref,
             db_ref, s1_ref, s2_ref, dp1_ref, dp2_ref, p1_ref, p2_ref, ds1_ref, ds2_ref):
        b, m = pl.program_id(1), pl.program_id(2)
        typ, start = _na_tile(m, nm, rows)

        @pl.when(m == 0)
        def _():
            dk_ref[...] = jnp.zeros_like(dk_ref)
            dv_ref[...] = jnp.zeros_like(dv_ref)

        @pl.when((m == 0) & (b == 0))
        def _():
            db_ref[...] = jnp.zeros_like(db_ref)

        for hh in range(2):
            ln = slice(hh * NA_DH, (hh + 1) * NA_DH)
            q = q_ref[:, ln]
            kw, vw = k_ref[pl.ds(start, KW), ln], v_ref[pl.ds(start, KW), ln]
            kc, vc = k_ref[L:L + LC, ln], v_ref[L:L + LC, ln]
            g = g_ref[:, ln].astype(F32)
            sg = _sigmoid(g)
            dy = dy_ref[:, ln].astype(F32)
            do = (dy * (g * sg)).astype(BF16)
            s1_ref[...] = _dot_nt(q, kw)
            s2_ref[...] = _dot_nt(q, kc)
            dp1_ref[...] = _dot_nt(do, vw)
            dp2_ref[...] = _dot_nt(do, vc)

            def rows_pass(r, carry, hh=hh):
                rw = pl.ds(pl.multiple_of(r * RB, RB), RB)
                a = s1_ref[rw, :] + bias_ref[hh, typ, rw, :]
                c = s2_ref[rw, :]
                mx = jnp.maximum(jnp.max(a, axis=-1, keepdims=True), jnp.max(c, axis=-1, keepdims=True))
                e1, e2 = jnp.exp(a - mx), jnp.exp(c - mx)
                inv = 1.0 / (jnp.sum(e1, axis=-1, keepdims=True) + jnp.sum(e2, axis=-1, keepdims=True))
                p1, p2 = e1 * inv, e2 * inv
                p1_ref[rw, :] = p1.astype(BF16)
                p2_ref[rw, :] = p2.astype(BF16)
                dp1, dp2 = dp1_ref[rw, :], dp2_ref[rw, :]
                delta = jnp.sum(p1 * dp1, axis=-1, keepdims=True) + jnp.sum(p2 * dp2, axis=-1, keepdims=True)
                ds1 = p1 * (dp1 - delta)
                db_ref[hh, typ, rw, :] += ds1
                ds1_ref[rw, :] = ds1.astype(BF16)
                ds2_ref[rw, :] = (p2 * (dp2 - delta)).astype(BF16)
                return carry

            lax.fori_loop(0, TQ // RB, rows_pass, 0, unroll=True)
            p1b, p2b, ds1b, ds2b = p1_ref[...], p2_ref[...], ds1_ref[...], ds2_ref[...]
            dg_ref[:, ln] = (dy * o_ref[:, ln].astype(F32) * (sg * (1.0 + g * (1.0 - sg)))).astype(BF16)
            dq_ref[:, ln] = ((_dot(ds1b, kw) + _dot(ds2b, kc)) * scale).astype(BF16)
            dk_ref[pl.ds(start, KW), ln] += _dot_tn(ds1b, q)
            dv_ref[pl.ds(start, KW), ln] += _dot_tn(p1b, do)
            dk_ref[L:L + LC, ln] += _dot_tn(ds2b, q)
            dv_ref[L:L + LC, ln] += _dot_tn(p2b, do)

        @pl.when((m == nm - 1) & (b == B - 1))
        def _():
            for hh in range(2):
                for dr, t in _bias_tile_sums(db_ref, hh).items():
                    dt_ref[hh, dr + 7] = t

    tile = pl.BlockSpec((None, TQ, 128), lambda hp, b, m: (b, m, hp))
    kv_out = pl.BlockSpec((None, T, 128), lambda hp, b, m: (b, 0, hp))
    wide, narrow = (TQ, KW), (TQ, LC)
    return pl.pallas_call(
        body, name="na_bwd", grid=(4, B, nm),
        in_specs=[q_spec, k_spec, v_spec, g_spec, bias_spec, tile, tile],
        out_specs=(tile, tile, kv_out, kv_out,
                   pl.BlockSpec((2, 15, GRID_W, GRID_W), lambda hp, b, m: (hp, 0, 0, 0))),
        out_shape=(jax.ShapeDtypeStruct((B, L, 512), BF16), jax.ShapeDtypeStruct((B, L, 512), BF16),
                   jax.ShapeDtypeStruct((B, T, 512), F32), jax.ShapeDtypeStruct((B, T, 512), F32),
                   jax.ShapeDtypeStruct((bias.shape[0], 15, GRID_W, GRID_W), F32)),
        scratch_shapes=[pltpu.VMEM((2,) + bias.shape[1:], F32),
                        pltpu.VMEM(wide, F32), pltpu.VMEM(narrow, F32), pltpu.VMEM(wide, F32), pltpu.VMEM(narrow, F32),
                        pltpu.VMEM(wide, BF16), pltpu.VMEM(narrow, BF16), pltpu.VMEM(wide, BF16),
                        pltpu.VMEM(narrow, BF16)],
        compiler_params=_params(("arbitrary",) * 3))(P, P, P, P, bias, dY, o_na)


def _head_scalar(dec_ref, h):
    lane = lax.broadcasted_iota(jnp.int32, dec_ref.shape, 1)
    return -jnp.sum(jnp.where(lane == h, jnp.exp(dec_ref[...]), 0.0), axis=1, keepdims=True)


def _ret_specs(T):
    q_spec = pl.BlockSpec((None, TQ, 128), lambda b, h, i: (b, i, 16 + h))
    k_spec = pl.BlockSpec((None, T, 128), lambda b, h, i: (b, 0, 20 + h))
    v_spec = pl.BlockSpec((None, T, 128), lambda b, h, i: (b, 0, 24 + h))
    g_spec = pl.BlockSpec((None, TQ, 128), lambda b, h, i: (b, i, 28 + h))
    dec_spec = pl.BlockSpec((1, 4), lambda b, h, i: (0, 0))
    gn_spec = pl.BlockSpec((1, 128), lambda b, h, i: (0, h))
    return q_spec, k_spec, v_spec, g_spec, dec_spec, gn_spec


def _chunk_decay(lgf, lgb):
    tau = lax.broadcasted_iota(jnp.int32, (TQ, 1), 0).astype(F32)
    sig = lax.broadcasted_iota(jnp.int32, (1, TQ), 1).astype(F32)
    dist = tau - sig
    dm = jnp.exp(dist * jnp.where(dist > 0, lgf, -lgb)) * jnp.where(dist == 0, 2.0, 1.0)
    return tau, dist, dm


def _ret_states_call(P, dec_f, dec_b, L, LC):
    B, T, _ = P.shape
    n = L // TQ

    def body(df_ref, db_ref, k_ref, v_ref, sf_ref, sb_ref):
        h = pl.program_id(1)
        lgf, lgb = _head_scalar(df_ref, h), _head_scalar(db_ref, h)
        tau = lax.broadcasted_iota(jnp.int32, (TQ, 1), 0).astype(F32)
        jc = lax.broadcasted_iota(jnp.int32, (LC, 1), 0).astype(F32)
        wf, wb = jnp.exp(lgf * (TQ - 1.0 - tau)), jnp.exp(lgb * tau)
        gcf, gcb = jnp.exp(lgf * float(TQ)), jnp.exp(lgb * float(TQ))
        kc, vc = k_ref[L:L + LC, :].astype(F32), v_ref[L:L + LC, :]

        def chunk_state(i, w):
            ks = pl.multiple_of(i * TQ, TQ)
            return _dot_tn((k_ref[pl.ds(ks, TQ), :].astype(F32) * w).astype(BF16), v_ref[pl.ds(ks, TQ), :])

        def fwd(i, s):
            sf_ref[i] = s
            return gcf * s + chunk_state(i, wf)

        lax.fori_loop(0, n, fwd, _dot_tn((kc * jnp.exp(lgf * (LC - 1.0 - jc))).astype(BF16), vc), unroll=True)

        def bwd(r, s):
            i = n - 1 - r
            sb_ref[i] = s
            return gcb * s + chunk_state(i, wb)

        lax.fori_loop(0, n, bwd, _dot_tn((kc * jnp.exp(lgb * jc)).astype(BF16), vc), unroll=True)

    st = pl.BlockSpec((None, None, n, RET_DK, RET_DK), lambda b, h: (b, h, 0, 0, 0))
    return pl.pallas_call(
        body, name="ret_states", grid=(B, 4),
        in_specs=[pl.BlockSpec((1, 4), lambda b, h: (0, 0)), pl.BlockSpec((1, 4), lambda b, h: (0, 0)),
                  pl.BlockSpec((None, T, 128), lambda b, h: (b, 0, 20 + h)),
                  pl.BlockSpec((None, T, 128), lambda b, h: (b, 0, 24 + h))],
        out_specs=(st, st),
        out_shape=(jax.ShapeDtypeStruct((B, 4, n, RET_DK, RET_DK), F32),) * 2,
        compiler_params=_params(("arbitrary",) * 2))(dec_f, dec_b, P, P)


def _retc_fwd_call(P, sf, sb, dec_f, dec_b, ret_norm_g, L):
    B, T, _ = P.shape
    sec = lambda k: pl.BlockSpec((None, TQ, 512), lambda b, i: (b, i, k))
    dec_spec = pl.BlockSpec((1, 4), lambda b, i: (0, 0))
    st_spec = pl.BlockSpec((None, 4, None, RET_DK, RET_DK), lambda b, i: (b, 0, i, 0, 0))

    def body(df_ref, db_ref, q_ref, k_ref, v_ref, g_ref, gn_ref, sf_ref, sb_ref, y_ref, o_ref):
        for h in range(4):
            ln = slice(h * RET_DK, (h + 1) * RET_DK)
            lgf, lgb = _head_scalar(df_ref, h), _head_scalar(db_ref, h)
            tau, _, dm = _chunk_decay(lgf, lgb)
            q = q_ref[:, ln]
            qf = q.astype(F32)
            acc = _dot((_dot_nt(q, k_ref[:, ln]) * dm).astype(BF16), v_ref[:, ln])
            acc = acc + _dot((qf * jnp.exp(lgf * (tau + 1.0))).astype(BF16), sf_ref[h].astype(BF16))
            acc = acc + _dot((qf * jnp.exp(lgb * (TQ - tau))).astype(BF16), sb_ref[h].astype(BF16))
            o_ref[:, ln] = acc
            rn = lax.rsqrt(jnp.mean(acc * acc, axis=-1, keepdims=True) + EPS)
            g = g_ref[:, ln].astype(F32)
            y_ref[:, ln] = ((acc * rn * gn_ref[:, ln]) * (g * _sigmoid(g))).astype(BF16)

    tile = pl.BlockSpec((None, TQ, 512), lambda b, i: (b, i, 0))
    return pl.pallas_call(
        body, name="ret_fwd", grid=(B, L // TQ),
        in_specs=[dec_spec, dec_spec, sec(4), sec(5), sec(6), sec(7),
                  pl.BlockSpec((1, 512), lambda b, i: (0, 0)), st_spec, st_spec],
        out_specs=(tile, tile),
        out_shape=(jax.ShapeDtypeStruct((B, L, 512), BF16), jax.ShapeDtypeStruct((B, L, 512), F32)),
        compiler_params=_params(("arbitrary",) * 2))(dec_f, dec_b, P, P, P, P, ret_norm_g, sf, sb)


def _retc_bwd_call(P, sf, sb, dec_f, dec_b, ret_norm_g, o_ret, dY, cos2, sin2, L, LC):
    B, T, _ = P.shape
    n = L // TQ
    C = float(TQ)
    kscale = RET_DK ** -0.5
    st_spec = pl.BlockSpec((None, 4, n, RET_DK, RET_DK), lambda b, i: (b, 0, 0, 0, 0))

    def body(df_ref, db_ref, q_ref, k_ref, v_ref, g_ref, gn_ref, o_ref, dy_ref, cos_ref, sin_ref, sf_ref, sb_ref,
             dq_ref, dg_ref, dk_ref, dv_ref, dgn_ref, dlg_ref, dsf_ref, dsb_ref):
        i = pl.program_id(1)

        @pl.when(i == 0)
        def _():
            dk_ref[...] = jnp.zeros_like(dk_ref)
            dv_ref[...] = jnp.zeros_like(dv_ref)
            dgn_ref[...] = jnp.zeros_like(dgn_ref)
            dlg_ref[...] = jnp.zeros_like(dlg_ref)

        rows = pl.ds(pl.multiple_of(i * TQ, TQ), TQ)
        cs, sn = cos_ref[rows, :], sin_ref[rows, :]

        def one_head(h):
            ln = slice(h * RET_DK, (h + 1) * RET_DK)
            lgf, lgb = _head_scalar(df_ref, h), _head_scalar(db_ref, h)
            tau, dist, dm = _chunk_decay(lgf, lgb)

            def add_lg(row, x):
                csum = jnp.sum(x, axis=0, keepdims=True)
                tot = csum[:, 0:128]
                for part in range(1, x.shape[1] // 128):
                    tot = tot + csum[:, part * 128:(part + 1) * 128]
                dlg_ref[h, row:row + 1, :] += tot

            q = q_ref[:, ln]
            qf = q.astype(F32)
            o = o_ref[:, ln]
            g = g_ref[:, ln].astype(F32)
            dy = dy_ref[:, ln].astype(F32)
            gn = gn_ref[:, ln]
            sg = _sigmoid(g)
            rn = lax.rsqrt(jnp.mean(o * o, axis=-1, keepdims=True) + EPS)
            nrm = o * rn
            dg_ref[:, ln] = (dy * (nrm * gn) * (sg * (1.0 + g * (1.0 - sg)))).astype(BF16)
            dhn = dy * (g * sg)
            dgn_ref[:, ln] += jnp.sum(dhn * nrm, axis=0, keepdims=True)
            dnrm = dhn * gn
            do = rn * (dnrm - nrm * jnp.mean(dnrm * nrm, axis=-1, keepdims=True))
            dob = do.astype(BF16)
            ki, vi = k_ref[rows, ln], v_ref[rows, ln]
            s = _dot_nt(q, ki)
            dsv = _dot_nt(dob, vi)
            dsb = (dsv * dm).astype(BF16)
            dk_ref[rows, ln] += _dot_tn(dsb, q)
            dv_ref[rows, ln] += _dot_tn((s * dm).astype(BF16), dob)
            xw = s * dsv * dm * jnp.abs(dist)
            fpart = jnp.where(dist > 0, xw, 0.0)
            add_lg(0, fpart)
            add_lg(1, xw - fpart)
            dq = _dot(dsb, ki)
            af, ab = jnp.exp(lgf * (tau + 1.0)), jnp.exp(lgb * (C - tau))
            qa, qb = (qf * af).astype(BF16), (qf * ab).astype(BF16)
            sfi, sbi = sf_ref[h, i].astype(BF16), sb_ref[h, i].astype(BF16)
            dq = dq + af * _dot_nt(dob, sfi) + ab * _dot_nt(dob, sbi)
            dsf_ref[h, i] = _dot_tn(qa, dob)
            dsb_ref[h, i] = _dot_tn(qb, dob)
            add_lg(0, (tau + 1.0) * (_dot(qa, sfi) * do))
            add_lg(1, (C - tau) * (_dot(qb, sbi) * do))
            dq_ref[:, ln] = (dq * cs - pltpu.roll(dq, 64, 1) * sn).astype(BF16)

            @pl.when(i == n - 1)
            def _():
                jc = lax.broadcasted_iota(jnp.int32, (LC, 1), 0).astype(F32)
                crow = pl.ds(L, LC)

                def through_state(rws, w, dw, gst, row):
                    kk, vv = k_ref[rws, ln].astype(F32), v_ref[rws, ln]
                    gb = gst.astype(BF16)
                    vg = _dot_nt(vv, gb)
                    kw = kk * w
                    dk_ref[rws, ln] += w * vg
                    dv_ref[rws, ln] += _dot(kw.astype(BF16), gb)
                    add_lg(row, dw * (kw * vg))

                def scan(gc, w, dw, st_ref, dst_ref, order, row):
                    def step(r, gst):
                        j = order(r)
                        through_state(pl.ds(pl.multiple_of(j * TQ, TQ), TQ), w, dw, gst, row)
                        add_lg(row, (C * gc) * (gst * st_ref[h, j]))
                        return dst_ref[h, j] + gc * gst
                    return lax.fori_loop(0, n, step, jnp.zeros((RET_DK, RET_DK), F32), unroll=True)

                gcf, gcb = jnp.exp(lgf * C), jnp.exp(lgb * C)
                g0 = scan(gcf, jnp.exp(lgf * (C - 1.0 - tau)), C - 1.0 - tau, sf_ref, dsf_ref,
                          lambda r: n - 1 - r, 0)
                through_state(crow, jnp.exp(lgf * (LC - 1.0 - jc)), LC - 1.0 - jc, g0, 0)
                g1 = scan(gcb, jnp.exp(lgb * tau), tau, sb_ref, dsb_ref, lambda r: r, 1)
                through_state(crow, jnp.exp(lgb * jc), jc, g1, 1)
                dk = dk_ref[:, ln]
                dk_ref[:, ln] = (dk * cos_ref[...] - pltpu.roll(dk, 64, 1) * sin_ref[...]) * kscale

        for h in range(4):
            one_head(h)

    sec = lambda k: pl.BlockSpec((None, TQ, 512), lambda b, i: (b, i, k))
    full = lambda k: pl.BlockSpec((None, T, 512), lambda b, i: (b, 0, k))
    dec_spec = pl.BlockSpec((1, 4), lambda b, i: (0, 0))
    tab = pl.BlockSpec((T, RET_DK), lambda b, i: (0, 0))
    return pl.pallas_call(
        body, name="ret_bwd", grid=(B, n),
        in_specs=[dec_spec, dec_spec, sec(4), full(5), full(6), sec(7),
                  pl.BlockSpec((1, 512), lambda b, i: (0, 0)), sec(0), sec(1), tab, tab, st_spec, st_spec],
        out_specs=(sec(0), sec(0), full(0), full(0),
                   pl.BlockSpec((None, 1, 512), lambda b, i: (b, 0, 0)),
                   pl.BlockSpec((None, 4, 8, 128), lambda b, i: (b, 0, 0, 0))),
        out_shape=(jax.ShapeDtypeStruct((B, L, 512), BF16), jax.ShapeDtypeStruct((B, L, 512), BF16),
                   jax.ShapeDtypeStruct((B, T, 512), F32), jax.ShapeDtypeStruct((B, T, 512), F32),
                   jax.ShapeDtypeStruct((B, 1, 512), F32), jax.ShapeDtypeStruct((B, 4, 8, 128), F32)),
        scratch_shapes=[pltpu.VMEM((4, n, RET_DK, RET_DK), F32), pltpu.VMEM((4, n, RET_DK, RET_DK), F32)],
        compiler_params=_params(("arbitrary",) * 2, vmem_mb=56))(
            dec_f, dec_b, P, P, P, P, ret_norm_g, o_ret, dY, cos2, sin2, sf, sb)


def _out_call(y_na, y_ret, x, target, mod, final_g, wout_f):
    B, L, _ = x.shape

    def body(yn_ref, yr_ref, x_ref, t_ref, mod_ref, gf_ref, w_ref, dy_ref, dx2_ref, dw_ref, sm_ref):
        b, i = pl.program_id(0), pl.program_id(1)

        @pl.when((b == 0) & (i == 0))
        def _():
            dw_ref[...] = jnp.zeros_like(dw_ref)
            sm_ref[...] = jnp.zeros_like(sm_ref)

        gate = mod_ref[pl.ds(b, 1), 2 * D:3 * D]
        gf = gf_ref[...]
        yn, yr = yn_ref[...], yr_ref[...]
        ylat = _dot(yn, w_ref[0:512, :]) + _dot(yr, w_ref[512:1024, :])
        x2 = x_ref[...] + gate * ylat
        r = lax.rsqrt(jnp.mean(x2 * x2, axis=-1, keepdims=True) + EPS)
        xr = x2 * r
        err = xr * gf - t_ref[...]
        sm_ref[1:2, :] += jnp.sum(err * err, axis=0, keepdims=True)
        dout = err * (1.0 / D)
        sm_ref[0:1, :] += jnp.sum(dout * xr, axis=0, keepdims=True)
        gd = dout * gf
        dx2 = r * (gd - xr * jnp.mean(gd * xr, axis=-1, keepdims=True))
        dx2_ref[...] = dx2
        sm_ref[pl.ds(2 + b, 1), :] += jnp.sum(dx2 * ylat, axis=0, keepdims=True)
        dyl = (gate * dx2).astype(BF16)
        dy_ref[:, 0:512] = _dot_nt(dyl, w_ref[0:512, :]).astype(BF16)
        dy_ref[:, 512:1024] = _dot_nt(dyl, w_ref[512:1024, :]).astype(BF16)
        dw_ref[0:512, :] += _dot_tn(yn, dyl)
        dw_ref[512:1024, :] += _dot_tn(yr, dyl)

    half = pl.BlockSpec((None, TQ, 512), lambda b, i: (b, i, 0))
    full = pl.BlockSpec((None, TQ, D), lambda b, i: (b, i, 0))
    return pl.pallas_call(
        body, name="out_proj_loss", grid=(B, L // TQ),
        in_specs=[half, half, full, full,
                  pl.BlockSpec((8, 3 * D), lambda b, i: (0, 0)),
                  pl.BlockSpec((1, D), lambda b, i: (0, 0)),
                  pl.BlockSpec((D, D), lambda b, i: (0, 0))],
        out_specs=(full, full, pl.BlockSpec((D, D), lambda b, i: (0, 0)),
                   pl.BlockSpec((8, D), lambda b, i: (0, 0))),
        out_shape=(jax.ShapeDtypeStruct((B, L, D), BF16), jax.ShapeDtypeStruct((B, L, D), F32),
                   jax.ShapeDtypeStruct((D, D), F32), jax.ShapeDtypeStruct((8, D), F32)),
        compiler_params=_params(("arbitrary",) * 2))(y_na, y_ret, x, target, mod, final_g, wout_f)


def _dh_call(dsec, win_f, x, ctx, dx2, mod, norm_g, cp_in, cp_out):
    B, L, _ = x.shape
    LC = ctx.shape[1]
    nl = L // TQ

    def body(d0, d1, d2, d3, d4, d5, d6, d7, w_ref, x_ref, ctx_ref, dx2_ref, mod_ref, g_ref, cpi_ref, cpo_ref,
             gx_ref, sm_ref, sli_ref, slo_ref, ssem, rsem, lsem):
        drefs = (d0, d1, d2, d3, d4, d5, d6, d7)
        b, t = pl.program_id(0), pl.program_id(1)
        is_lat = t < nl

        @pl.when((b == 0) & (t == 0))
        def _():
            sm_ref[...] = jnp.zeros_like(sm_ref)

        def dh_of(secs):
            acc = jnp.zeros((TQ, D), F32)
            for sec in secs:
                s, half = divmod(sec, 2)
                acc = acc + _dot_nt(drefs[sec][...].astype(BF16), w_ref[s, :, half * 512:(half + 1) * 512])
            return acc

        def norm_bwd(dh, xt, mrow):
            scale = mrow[:, D:2 * D]
            g = g_ref[...]
            rstd = lax.rsqrt(jnp.mean(xt * xt, axis=-1, keepdims=True) + EPS)
            xn = xt * rstd
            dshift = jnp.sum(dh, axis=0, keepdims=True)
            dscale = jnp.sum(dh * (xn * g), axis=0, keepdims=True)
            dhn = dh * (1.0 + scale)
            sm_ref[0:1, :] += jnp.sum(dhn * xn, axis=0, keepdims=True)
            dxn = dhn * g
            dx = rstd * (dxn - xn * jnp.mean(dxn * xn, axis=-1, keepdims=True))
            return dshift, dscale, dx

        @pl.when(is_lat)
        def _():
            dshift, dscale, dx = norm_bwd(dh_of(range(8)), x_ref[...], mod_ref[pl.ds(b, 1), :])
            sm_ref[pl.ds(3 + b, 1), :] += dshift
            sm_ref[pl.ds(3 + B + b, 1), :] += dscale
            gx_ref[...] = dx2_ref[...] + dx

        @pl.when(jnp.logical_not(is_lat))
        def _():
            dshift, dscale, _ = norm_bwd(dh_of((1, 2, 5, 6)), ctx_ref[...], mod_ref[B:B + 1, :])
            sm_ref[1:2, :] += dshift
            sm_ref[2:3, :] += dscale

        mx, my, mc = _mesh_pos()
        s = 2 * mx + my
        cps, sls = (cpi_ref, cpo_ref), (sli_ref, slo_ref)
        own = [pltpu.make_async_copy(cps[a].at[s], sls[a].at[s], lsem.at[a]) for a in range(2)]
        sends, recvs, k = [], [], 0
        for px, py in _other_chips(mx, my):
            ps = 2 * px + py
            for a in range(2):
                sends.append(_remote(cps[a].at[ps], sls[a].at[s], ssem, rsem, k, (px, py, mc)))
                recvs.append(_remote(cps[a].at[s], sls[a].at[ps], ssem, rsem, k, (px, py, mc)))
                k += 1

        @pl.when((b == 0) & (t == 0))
        def _():
            for cp in own + sends:
                cp.start()

        @pl.when((b == B - 1) & (t == nl))
        def _():
            _finish(own, sends, recvs)

    lat = lambda b, t: (b, jnp.minimum(t, nl - 1), 0)
    tok = lambda b, t: (b, t, 0)
    sec_specs = [pl.BlockSpec((None, TQ, 512), lat if sec in (0, 3, 4, 7) else tok) for sec in range(8)]
    return pl.pallas_call(
        body, name="dh_norm_bwd", grid=(B, nl + 1),
        in_specs=sec_specs + [
            pl.BlockSpec((N_SHARD, D, D), lambda b, t: (0, 0, 0)),
            pl.BlockSpec((None, TQ, D), lat),
            pl.BlockSpec((None, LC, D), lambda b, t: (b, 0, 0)),
            pl.BlockSpec((None, TQ, D), lat),
            pl.BlockSpec((8, 3 * D), lambda b, t: (0, 0)),
            pl.BlockSpec((1, D), lambda b, t: (0, 0)), ANY, ANY],
        out_specs=(pl.BlockSpec((None, TQ, D), lat), pl.BlockSpec((8, D), lambda b, t: (0, 0)), ANY, ANY),
        out_shape=(jax.ShapeDtypeStruct((B, L, D), F32), jax.ShapeDtypeStruct((8, D), F32),
                   jax.ShapeDtypeStruct(cp_in.shape, cp_in.dtype), jax.ShapeDtypeStruct(cp_out.shape, cp_out.dtype)),
        scratch_shapes=[pltpu.SemaphoreType.DMA((6,)), pltpu.SemaphoreType.DMA((6,)),
                        pltpu.SemaphoreType.DMA((2,))],
        compiler_params=_params(("arbitrary",) * 2))(*dsec, win_f, x, ctx, dx2, mod, norm_g, cp_in, cp_out)


def _dw_call(dsec, h, L):
    B, T, _ = h.shape
    nl = L // TQ

    def body(d0, d1, d2, d3, d4, d5, d6, d7, h_ref, dw_ref, acc_ref):
        drefs = (d0, d1, d2, d3, d4, d5, d6, d7)
        b, t = pl.program_id(0), pl.program_id(1)

        @pl.when((b == 0) & (t == 0))
        def _():
            acc_ref[...] = jnp.zeros_like(acc_ref)

        hb = h_ref[...]

        def add(secs):
            for sec in secs:
                s, half = divmod(sec, 2)
                acc_ref[s, :, half * 512:(half + 1) * 512] += _dot_tn(hb, drefs[sec][...].astype(BF16))

        @pl.when(t < nl)
        def _():
            add(range(8))

        @pl.when(t >= nl)
        def _():
            add((1, 2, 5, 6))

        @pl.when((b == B - 1) & (t == nl))
        def _():
            dw_ref[...] = acc_ref[...].astype(BF16)

    lat = lambda b, t: (b, jnp.minimum(t, nl - 1), 0)
    tok = lambda b, t: (b, t, 0)
    sec_specs = [pl.BlockSpec((None, TQ, 512), lat if sec in (0, 3, 4, 7) else tok) for sec in range(8)]
    return pl.pallas_call(
        body, name="dw_in", grid=(B, nl + 1),
        in_specs=sec_specs + [pl.BlockSpec((None, TQ, D), tok)],
        out_specs=pl.BlockSpec((N_SHARD, D, D), lambda b, t: (0, 0, 0)),
        out_shape=jax.ShapeDtypeStruct((N_SHARD, D, D), BF16),
        scratch_shapes=[pltpu.VMEM((N_SHARD, D, D), F32)],
        compiler_params=_params(("arbitrary",) * 2, vmem_mb=56))(*dsec, h)


def _mesh_pos():
    return lax.axis_index("x"), lax.axis_index("y"), lax.axis_index("c")


def _flip(v, f):
    return 1 - v if f else v


def _remote(src, dst, ssem, rsem, k, peer):
    return pltpu.make_async_remote_copy(src_ref=src, dst_ref=dst, send_sem=ssem.at[k], recv_sem=rsem.at[k],
                                        device_id=peer, device_id_type=MESH)


def _other_chips(x, y):
    return [(_flip(x, fx), _flip(y, fy)) for fx, fy in ((1, 0), (0, 1), (1, 1))]


def _all_to_all_small(src, dst_all, ssem, rsem, k0, x, y, cc):
    me = 4 * x + 2 * y + cc
    sends, recvs = [], []
    for f in range(1, N_DEV):
        px, py, pc = _flip(x, f & 4), _flip(y, f & 2), _flip(cc, f & 1)
        sends.append(_remote(src, dst_all.at[me], ssem, rsem, k0 + f - 1, (px, py, pc)))
        recvs.append(_remote(src, dst_all.at[4 * px + 2 * py + pc], ssem, rsem, k0 + f - 1, (px, py, pc)))
    return sends, recvs


def _finish(local, sends, recvs):
    for cp in recvs:
        cp.wait_recv()
    for cp in sends:
        cp.wait_send()
    for cp in local:
        cp.wait()


def _gather_call(wout_b, wada_b, c, rpb_flat):
    arrs = (wout_b, wada_b)
    na = len(arrs)
    hrs = [a.shape[0] // 2 for a in arrs]

    def body(wout, wada, c_ref, r_ref, wout_f, wada_f, c_all, bias_ref, et_ref, ssem, rsem, lsem):
        x, y, cc = _mesh_pos()
        s, me = 2 * x + y, 4 * x + 2 * y + cc
        sib = (x, y, 1 - cc)
        srcs, dsts = (wout, wada), (wout_f, wada_f)

        def half(a, shard, hc):
            return dsts[a].at[shard, pl.ds(hc * hrs[a], hrs[a])]

        local = [pltpu.make_async_copy(srcs[a], dsts[a].at[s], lsem.at[a]) for a in range(na)]
        local.append(pltpu.make_async_copy(c_ref, c_all.at[me], lsem.at[na]))
        ici_send, ici_recv, fwd_send, fwd_recv, k = [], [], [], [], 0
        for px, py in _other_chips(x, y):
            ps = 2 * px + py
            for a in range(na):
                mine = srcs[a].at[pl.ds(cc * hrs[a], hrs[a])]
                ici_send.append(_remote(mine, half(a, s, cc), ssem, rsem, k, (px, py, cc)))
                ici_recv.append(_remote(mine, half(a, ps, cc), ssem, rsem, k, (px, py, cc)))
                fwd_send.append(_remote(half(a, ps, cc), half(a, ps, cc), ssem, rsem, 3 * na + k, sib))
                fwd_recv.append(_remote(half(a, ps, 1 - cc), half(a, ps, 1 - cc), ssem, rsem, 3 * na + k, sib))
                k += 1
        c_send, c_recv = _all_to_all_small(c_ref, c_all, ssem, rsem, 6 * na, x, y, cc)
        for cp in local + ici_send + c_send:
            cp.start()
        _bias_body(r_ref, bias_ref, et_ref)
        for got, fwd in zip(ici_recv, fwd_send):
            got.wait_recv()
            fwd.start()
        _finish(local, ici_send + fwd_send + c_send, fwd_recv + c_recv)

    return pl.pallas_call(
        body, name="weight_gather",
        in_specs=[pl.BlockSpec(memory_space=pltpu.VMEM)] * 3 + [pl.BlockSpec(memory_space=pltpu.SMEM)],
        out_specs=(pl.BlockSpec(memory_space=pltpu.VMEM),) * 4,
        out_shape=tuple(jax.ShapeDtypeStruct((N_SHARD,) + a.shape, a.dtype) for a in arrs)
        + (jax.ShapeDtypeStruct((N_DEV,) + c.shape, c.dtype),
           jax.ShapeDtypeStruct((rpb_flat.shape[0], 3, TQ, KW), F32)),
        scratch_shapes=[pltpu.VMEM((15, GRID_W, GRID_W), F32),
                        pltpu.SemaphoreType.DMA((6 * na + 7,)), pltpu.SemaphoreType.DMA((6 * na + 7,)),
                        pltpu.SemaphoreType.DMA((na + 1,))],
        compiler_params=pltpu.CompilerParams(vmem_limit_bytes=56 << 20))(wout_b, wada_b, c, rpb_flat)


VROWS = 32


def _grad_halves_call(dwin_b, dwout_b, dbias, dlg):
    arrs = (dwin_b, dwout_b)
    hrs = [a.shape[1] // 2 for a in arrs]

    def body(din, dout, db_ref, dlg_ref, cp_in, cp_out, drpb_ref, dlgo_ref, got_in, got_out, p_ref, ssem, rsem):
        x, y, cc = _mesh_pos()
        sib = (x, y, 1 - cc)
        srcs, gots, cps = (din, dout), (got_in, got_out), (cp_in, cp_out)
        halves = [_remote(srcs[a].at[:, pl.ds((1 - cc) * hrs[a], hrs[a])], gots[a], ssem, rsem, a, sib)
                  for a in range(2)]
        for cp in halves:
            cp.start()
        _small_reduce_body(db_ref, dlg_ref, drpb_ref, dlgo_ref, p_ref)
        for cp in halves:
            cp.wait_recv()
        for a in range(2):
            for j in range(N_SHARD):
                def add(i, carry, a=a, j=j):
                    r = pl.multiple_of(i * VROWS, VROWS)
                    mine = srcs[a][j, pl.ds(pl.multiple_of(cc * hrs[a] + r, VROWS), VROWS), :].astype(F32)
                    cps[a][j, pl.ds(r, VROWS), :] = (
                        mine + gots[a][j, pl.ds(r, VROWS), :].astype(F32)).astype(BF16)
                    return carry
                lax.fori_loop(0, hrs[a] // VROWS, add, 0)
        for cp in halves:
            cp.wait_send()

    vmem = pl.BlockSpec(memory_space=pltpu.VMEM)
    half_shapes = [(N_SHARD, hrs[a], arrs[a].shape[2]) for a in range(2)]
    return pl.pallas_call(
        body, name="grad_halves",
        in_specs=[vmem] * 4, out_specs=(vmem,) * 4,
        out_shape=(jax.ShapeDtypeStruct(half_shapes[0], BF16), jax.ShapeDtypeStruct(half_shapes[1], BF16),
                   jax.ShapeDtypeStruct((dbias.shape[0], 16, 32), F32), jax.ShapeDtypeStruct((32, 128), F32)),
        scratch_shapes=[pltpu.VMEM(half_shapes[0], BF16), pltpu.VMEM(half_shapes[1], BF16),
                        pltpu.VMEM((32, GRID_W), F32),
                        pltpu.SemaphoreType.DMA((2,)), pltpu.SemaphoreType.DMA((2,))],
        compiler_params=pltpu.CompilerParams(vmem_limit_bytes=56 << 20))(dwin_b, dwout_b, dbias, dlg)


def _grad_finish_call(sl_in, sl_out, small):
    arrs = (sl_in, sl_out)

    def body(sin, sout, sm, gin, gout, sm_all, h_in, h_out, ssem, rsem, lsem):
        x, y, cc = _mesh_pos()
        me = 4 * x + 2 * y + cc
        sib = (x, y, 1 - cc)
        sls, hs, gs = (sin, sout), (h_in, h_out), (gin, gout)
        sm_send, sm_recv = _all_to_all_small(sm, sm_all, ssem, rsem, 2, x, y, cc)
        sm_own = pltpu.make_async_copy(sm, sm_all.at[me], lsem.at[0])
        for cp in sm_send + [sm_own]:
            cp.start()
        for a in range(2):
            def total(i, carry, a=a):
                rows = pl.ds(pl.multiple_of(i * VROWS, VROWS), VROWS)
                sl = sls[a]
                hs[a][rows, :] = ((sl[0, rows, :].astype(F32) + sl[1, rows, :].astype(F32))
                                  + sl[2, rows, :].astype(F32)) + sl[3, rows, :].astype(F32)
                return carry
            lax.fori_loop(0, arrs[a].shape[1] // VROWS, total, 0)
        mine = [pltpu.make_async_copy(hs[a], gs[a].at[cc], lsem.at[1 + a]) for a in range(2)]
        back = [_remote(hs[a], gs[a].at[cc], ssem, rsem, a, sib) for a in range(2)]
        back_recv = [_remote(hs[a], gs[a].at[1 - cc], ssem, rsem, a, sib) for a in range(2)]
        for cp in mine + back:
            cp.start()
        _finish(mine + [sm_own], back + sm_send, back_recv + sm_recv)

    vmem = pl.BlockSpec(memory_space=pltpu.VMEM)
    return pl.pallas_call(
        body, name="grad_finish",
        in_specs=[vmem] * 3, out_specs=(vmem,) * 3,
        out_shape=(jax.ShapeDtypeStruct((2,) + sl_in.shape[1:], F32),
                   jax.ShapeDtypeStruct((2,) + sl_out.shape[1:], F32),
                   jax.ShapeDtypeStruct((N_DEV,) + small.shape, F32)),
        scratch_shapes=[pltpu.VMEM(sl_in.shape[1:], F32), pltpu.VMEM(sl_out.shape[1:], F32),
                        pltpu.SemaphoreType.DMA((9,)), pltpu.SemaphoreType.DMA((9,)),
                        pltpu.SemaphoreType.DMA((3,))],
        compiler_params=pltpu.CompilerParams(vmem_limit_bytes=48 << 20))(sl_in, sl_out, small)


def _adamw(w, g, m, v):
    m = ADAM_B1 * m + (1.0 - ADAM_B1) * g
    v = ADAM_B2 * v + (1.0 - ADAM_B2) * (g * g)
    m_hat = m / (1.0 - ADAM_B1 ** ADAM_STEP)
    v_hat = v / (1.0 - ADAM_B2 ** ADAM_STEP)
    return -ADAM_LR * (m_hat / (jnp.sqrt(v_hat) + ADAM_EPS) + ADAM_WD * w), m, v


def _adam_call(w, m, v, g, name):
    R, C = w.shape
    tr = 256

    def body(w_ref, m_ref, v_ref, g_ref, d_ref, mo_ref, vo_ref):
        d_ref[...], mo_ref[...], vo_ref[...] = _adamw(w_ref[...], g_ref[...], m_ref[...], v_ref[...])

    spec = pl.BlockSpec((tr, C), lambda i: (i, 0))
    return pl.pallas_call(
        body, name=name, grid=(R // tr,), in_specs=[spec] * 4,
        out_specs=(spec,) * 3, out_shape=(jax.ShapeDtypeStruct((R, C), F32),) * 3,
        compiler_params=_params(("arbitrary",)))(w, m, v, g)


R_GF, R_NG, R_LOSS, R_RNG, R_LGF, R_LGB, R_SHIFT, R_SCALE, R_GATE, R_SHIFT_C, R_SCALE_C, R_RNG2, R_RPB = (
    0, 1, 2, 3, 4, 5, 6, 8, 10, 12, 13, 14, 16)
W_GF, W_NG, W_CCTX, W_RNG, W_DF, W_DB, W_BADA, W_RPB = 0, 1, 2, 3, 4, 5, 6, 9


def _small_final_call(sm_all, c_t, c_ctx, wada_f, wada, m_ada, v_ada, wsm, msm, vsm, B):
    ws = wada.shape[1]
    NB = N_DEV * B

    def body(sm_ref, ct_ref, cctx_ref, wf_ref, wa_ref, ma_ref, va_ref, w_ref, m_ref, v_ref,
             g_ref, d_ref, mo_ref, vo_ref, ga_ref, da_ref, mao_ref, vao_ref, loss_ref, dmod_ref):
        x, y, _ = _mesh_pos()
        s = 2 * x + y
        tot = sm_ref[0]
        for dv in range(1, N_DEV):
            tot = tot + sm_ref[dv]
        w = w_ref[...]
        for dv in range(N_DEV):
            for b in range(B):
                r = dv * B + b
                for part, row in enumerate((R_SHIFT, R_SCALE, R_GATE)):
                    dmod_ref[r:r + 1, part * D:(part + 1) * D] = sm_ref[dv, row + b:row + b + 1, :]
        dmod_ref[NB:NB + 1, 0:D] = tot[R_SHIFT_C:R_SHIFT_C + 1, :]
        dmod_ref[NB:NB + 1, D:2 * D] = tot[R_SCALE_C:R_SCALE_C + 1, :]
        dmod_ref[NB:NB + 1, 2 * D:3 * D] = jnp.zeros((1, D), F32)
        dmod_ref[NB + 1:, :] = jnp.zeros((dmod_ref.shape[0] - NB - 1, 3 * D), F32)
        dmod = dmod_ref[...]
        cc = cctx_ref[...]
        scc = _sigmoid(cc)
        ct = ct_ref[...]
        act_t = ct * _sigmoid(ct)
        dmc = dmod[NB:NB + 1, :].astype(BF16)
        dact = jnp.zeros((1, D), F32)
        for sh in range(N_SHARD):
            dact = dact + _dot_nt(dmc[:, sh * ws:(sh + 1) * ws], wf_ref[sh])
        g = jnp.zeros((16, D), F32)
        rows = lax.broadcasted_iota(jnp.int32, (16, D), 0)

        def put(g, row, val):
            return jnp.where(rows == row, val, g)

        g = put(g, W_GF, tot[R_GF:R_GF + 1, :])
        g = put(g, W_NG, tot[R_NG:R_NG + 1, :])
        g = put(g, W_CCTX, dact * (scc * (1.0 + cc * (1.0 - scc))))
        g = put(g, W_RNG, tot[R_RNG:R_RNG + 1, :] + tot[R_RNG2:R_RNG2 + 1, :])
        g = put(g, W_DF, tot[R_LGF:R_LGF + 1, :] * (-jnp.exp(w[W_DF:W_DF + 1, :])))
        g = put(g, W_DB, tot[R_LGB:R_LGB + 1, :] * (-jnp.exp(w[W_DB:W_DB + 1, :])))
        db = jnp.sum(dmod, axis=0, keepdims=True)
        for part in range(3):
            g = put(g, W_BADA + part, db[:, part * D:(part + 1) * D])
        for part in range(4):
            g = put(g, W_RPB + part, tot[R_RPB + part:R_RPB + part + 1, :])
        g_ref[...] = g
        d_ref[...], mo_ref[...], vo_ref[...] = _adamw(w, g, m_ref[...], v_ref[...])
        loss_ref[...] = jnp.broadcast_to(
            (0.5 / D) * jnp.sum(tot[R_LOSS:R_LOSS + 1, :], axis=1, keepdims=True), (8, 128))
        for sh in range(N_SHARD):
            @pl.when(s == sh)
            def _():
                ga = jnp.dot(act_t, dmod[:, sh * ws:(sh + 1) * ws], precision=HIGHEST,
                             preferred_element_type=F32)
                ga_ref[...] = ga
                da_ref[...], mao_ref[...], vao_ref[...] = _adamw(wa_ref[...], ga, ma_ref[...], va_ref[...])

    sh_small = jax.ShapeDtypeStruct((16, D), F32)
    sh_ada = jax.ShapeDtypeStruct(wada.shape, F32)
    return pl.pallas_call(
        body, name="small_final",
        out_shape=(sh_small,) * 4 + (sh_ada,) * 4 + (jax.ShapeDtypeStruct((8, 128), F32),),
        scratch_shapes=[pltpu.VMEM((NB + 8, 3 * D), F32)],
        compiler_params=_params(vmem_mb=56))(
            sm_all, c_t, c_ctx, wada_f, wada, m_ada, v_ada, wsm, msm, vsm)


def _local_step(order, x, c, ctx, c_ctx, norm_g, wada_f, b_ada, win_b, bias, dec_f, dec_b, ret_norm_g,
                wout_f, final_g, target):
    B, L, _ = x.shape
    LC = ctx.shape[1]
    assert B == 2
    cos2, sin2 = _rope_tables(L, LC)
    c8 = jnp.concatenate([c, c_ctx[None, :], jnp.zeros((8 - B - 1, D), F32)], axis=0)
    mod = _mod_call(c8, wada_f, b_ada)
    P, h, win_f = _inproj_gather_call(order, x, ctx, mod, norm_g, win_b, cos2, sin2)
    y_na, o_na = _na_fwd_call(P, bias, L, LC)
    sf, sb = _ret_states_call(P, dec_f, dec_b, L, LC)
    y_ret, o_ret = _retc_fwd_call(P, sf, sb, dec_f, dec_b, ret_norm_g, L)
    dY, dx2, dwout_p, sm_out = _out_call(y_na, y_ret, x, target, mod, final_g, wout_f.reshape(D, D))
    dnq, dng, dnk, dnv, dbias = _na_bwd_call(P, bias, dY, o_na, L, LC)
    drq, drg, drk, drv, dgn, dlg = _retc_bwd_call(P, sf, sb, dec_f, dec_b, ret_norm_g, o_ret, dY, cos2, sin2, L, LC)
    dsec = (dnq, dnk, dnv, dng, drq, drk, drv, drg)
    dwin_b = _dw_call(dsec, h, L)
    cp_in, cp_out, drpb, dlg_sum = _grad_halves_call(
        dwin_b, dwout_p.astype(BF16).reshape(N_SHARD, D // N_SHARD, D), dbias, dlg)
    grad_x, sm_dh, sl_in, sl_out = _dh_call(dsec, win_f, x, ctx, dx2, mod, norm_g, cp_in, cp_out)
    z = jnp.zeros((1, D), F32)
    pad = lambda v: jnp.pad(v.reshape(1, -1), ((0, 0), (0, D - v.size)))
    dlg_sum = dlg_sum.reshape(4, 8, 128)
    rpb_rows = jnp.pad(drpb[:, :15, :31].reshape(-1), (0, 4 * D - drpb.shape[0] * 465)).reshape(4, D)
    small = jnp.concatenate([
        sm_out[0:1], sm_dh[0:1], sm_out[1:2], pad(dgn[0]), pad(dlg_sum[:, 0, 0]), pad(dlg_sum[:, 1, 0]),
        sm_dh[3:5], sm_dh[5:7], sm_out[2:4], sm_dh[1:2], sm_dh[2:3], pad(dgn[1]), z, rpb_rows,
        jnp.zeros((SM_ROWS - 20, D), F32)], axis=0)
    return grad_x, sl_in, sl_out, small


def kernel(x, c, ctx, c_ctx, norm_g, w_ada, b_ada, w_in, na_rpb, ret_decay_fwd, ret_decay_bwd, ret_norm_g, w_out, final_norm_g, loss_target, m_c_ctx, m_norm_g, m_w_ada, m_b_ada, m_w_in, m_na_rpb, m_ret_decay_fwd, m_ret_decay_bwd, m_ret_norm_g, m_w_out, m_final_norm_g, v_c_ctx, v_norm_g, v_w_ada, v_b_ada, v_w_in, v_na_rpb, v_ret_decay_fwd, v_ret_decay_bwd, v_ret_norm_g, v_w_out, v_final_norm_g):
    B = x.shape[0]
    wout_f, wada_f, c_all, bias = _gather_call(
        w_out[0].astype(BF16), w_ada[0].astype(BF16), c, na_rpb[0].reshape(na_rpb.shape[1], -1))
    mx, my = lax.axis_index("x"), lax.axis_index("y")
    order = jnp.stack([2 * mx + my, 2 * (1 - mx) + my, 2 * mx + (1 - my),
                       2 * (1 - mx) + (1 - my)]).astype(jnp.int32)
    grad_x, sl_in, sl_out, small = _local_step(
        order, x, c, ctx, c_ctx, norm_g, wada_f, b_ada, w_in[0].astype(BF16), bias, ret_decay_fwd,
        ret_decay_bwd, ret_norm_g, wout_f, final_norm_g.reshape(1, D), loss_target)
    gin, gout, sm_all = _grad_finish_call(sl_in, sl_out, small)
    g_win, g_wout = gin.reshape(w_in.shape[1:]), gout.reshape(w_out.shape[1:])
    d_win, nm_win, nv_win = _adam_call(w_in[0], m_w_in[0], v_w_in[0], g_win, "adam_w_in")
    d_wout, nm_wout, nv_wout = _adam_call(w_out[0], m_w_out[0], v_w_out[0], g_wout, "adam_w_out")

    def pack(gf, ng, cc, rng, df, db, bada, rpb):
        pad = lambda v: jnp.pad(v.reshape(1, -1), ((0, 0), (0, D - v.size)))
        return jnp.concatenate([
            gf.reshape(1, D), ng.reshape(1, D), cc.reshape(1, D), pad(rng), pad(df), pad(db),
            bada.reshape(3, D), jnp.pad(rpb.reshape(-1), (0, 4 * D - rpb.size)).reshape(4, D),
            jnp.zeros((3, D), F32)], axis=0)

    wsm = pack(final_norm_g, norm_g, c_ctx, ret_norm_g, ret_decay_fwd, ret_decay_bwd, b_ada, na_rpb)
    msm = pack(m_final_norm_g, m_norm_g, m_c_ctx, m_ret_norm_g, m_ret_decay_fwd, m_ret_decay_bwd, m_b_ada, m_na_rpb)
    vsm = pack(v_final_norm_g, v_norm_g, v_c_ctx, v_ret_norm_g, v_ret_decay_fwd, v_ret_decay_bwd, v_b_ada, v_na_rpb)
    c_t = jnp.concatenate([c_all.reshape(N_DEV * B, D), c_ctx.reshape(1, D), jnp.zeros((7, D), F32)], axis=0).T
    outs = _small_final_call(sm_all, c_t, c_ctx.reshape(1, D), wada_f,
                             w_ada[0], m_w_ada[0], v_w_ada[0], wsm, msm, vsm, B)
    smalls, adas, loss = outs[0:4], outs[4:8], outs[8][0, 0]

    def unpack(p):
        rw = ret_norm_g.shape[1]
        return dict(
            final_norm_g=p[W_GF], norm_g=p[W_NG:W_NG + 1], c_ctx=p[W_CCTX], ret_norm_g=p[W_RNG:W_RNG + 1, :rw],
            ret_decay_fwd=p[W_DF:W_DF + 1, :4], ret_decay_bwd=p[W_DB:W_DB + 1, :4],
            b_ada=p[W_BADA:W_BADA + 3].reshape(1, 3 * D),
            na_rpb=p[W_RPB:W_RPB + 4].reshape(-1)[:na_rpb.size].reshape(na_rpb.shape))

    res = []
    for p, ada, win_o, wout_o in zip(smalls, adas, (g_win, d_win, nm_win, nv_win),
                                     (g_wout, d_wout, nm_wout, nv_wout)):
        u = unpack(p)
        res.append([u["c_ctx"], u["norm_g"], ada[None], u["b_ada"], win_o[None], u["na_rpb"],
                    u["ret_decay_fwd"], u["ret_decay_bwd"], u["ret_norm_g"], wout_o[None], u["final_norm_g"]])
    return (loss, grad_x, *res[0], *res[1], *res[2], *res[3])
```

```python
import numpy as np
import jax
import jax.numpy as jnp
from jax import lax
from jax.experimental import pallas as pl
from jax.experimental.pallas import tpu as pltpu

F32 = jnp.float32
BF16 = jnp.bfloat16
HIGHEST = lax.Precision.HIGHEST

D = 1024
GRID_W = 64
NA_DH = 64
RET_DK = 128
ROPE_BASE = 10000.0
EPS = 1e-6
NEG = -1e30
TQ = 256
KW = 12 * GRID_W
N_SHARD = 4
N_DEV = 8
SM_ROWS = 24

ADAM_LR = 0.001
ADAM_B1 = 0.9
ADAM_B2 = 0.999
ADAM_EPS = 1e-08
ADAM_WD = 0.01
ADAM_STEP = 10

MESH = pl.DeviceIdType.MESH
ANY = pl.BlockSpec(memory_space=pl.ANY)


def _params(sem=None, vmem_mb=48):
    return pltpu.CompilerParams(dimension_semantics=sem, vmem_limit_bytes=vmem_mb << 20)


def _dot(a, b):
    return jnp.dot(a, b, preferred_element_type=F32)


def _dot_nt(a, b):
    return lax.dot_general(a, b, (((1,), (1,)), ((), ())), preferred_element_type=F32)


def _dot_tn(a, b):
    return lax.dot_general(a, b, (((0,), (0,)), ((), ())), preferred_element_type=F32)


def _sigmoid(x):
    return 1.0 / (1.0 + jnp.exp(-x))


def _rope_tables(L, LC):
    half = RET_DK // 2
    nf = half // 2
    t = np.arange(L)
    row = (t // GRID_W).astype(np.float32)
    col = (t % GRID_W).astype(np.float32)
    inv = (np.float32(ROPE_BASE) ** (-np.arange(nf, dtype=np.float32) / np.float32(nf))).astype(np.float32)
    ang = np.concatenate([row[:, None] * inv, col[:, None] * inv], axis=-1).astype(np.float32)
    cos, sin = np.cos(ang).astype(np.float32), np.sin(ang).astype(np.float32)
    cos2 = np.concatenate([cos, cos], axis=-1)
    sin2 = np.concatenate([-sin, sin], axis=-1)
    cos2 = np.concatenate([cos2, np.ones((LC, RET_DK), np.float32)], axis=0)
    sin2 = np.concatenate([sin2, np.zeros((LC, RET_DK), np.float32)], axis=0)
    return jnp.asarray(cos2), jnp.asarray(sin2)


def _mod_call(c8, wada_f, b_ada):
    ws = wada_f.shape[2]

    def body(c_ref, w_ref, b_ref, o_ref):
        a = c_ref[...]
        a = (a * _sigmoid(a)).astype(BF16)
        for s in range(N_SHARD):
            o_ref[:, s * ws:(s + 1) * ws] = _dot(a, w_ref[s]) + b_ref[:, s * ws:(s + 1) * ws]

    return pl.pallas_call(
        body, name="ada_mod", out_shape=jax.ShapeDtypeStruct((8, 3 * D), F32),
        compiler_params=_params())(c8, wada_f, b_ada)


def _dc_masks():
    cq = lax.broadcasted_iota(jnp.int32, (GRID_W, GRID_W), 0)
    ck = lax.broadcasted_iota(jnp.int32, (GRID_W, GRID_W), 1)
    dc = jnp.clip(ck - cq + 15, 0, 30)
    c0 = jnp.clip(cq - 8, 0, GRID_W - 16)
    col_ok = (ck >= c0) & (ck < c0 + 16)
    return dc, col_ok


def _bias_blocks():
    out = []
    for typ, delta in enumerate((4, 0, -4)):
        for rq in range(4):
            for rkk in range(12):
                dr = rkk + delta - rq - 4
                if typ == 0:
                    ok = -rq <= dr <= 7 - rq
                elif typ == 1:
                    ok = -4 <= dr <= 3
                else:
                    ok = -4 - rq <= dr <= 3 - rq
                out.append((typ, rq, rkk, dr if ok else None))
    return out


def _bias_body(r_ref, bias_ref, et_ref):
    dc, col_ok = _dc_masks()
    masks = [(dc == j).astype(F32) for j in range(31)]

    def per_h(h, carry):
        for dr in range(15):
            t = jnp.zeros((GRID_W, GRID_W), F32)
            for j in range(31):
                t = t + masks[j] * r_ref[h, dr * 31 + j]
            et_ref[dr] = jnp.where(col_ok, t, NEG)
        neg = jnp.full((GRID_W, GRID_W), NEG, F32)
        for typ, rq, rkk, dr in _bias_blocks():
            blk = neg if dr is None else et_ref[dr + 7]
            bias_ref[h, typ, rq * 64:(rq + 1) * 64, rkk * 64:(rkk + 1) * 64] = blk
        return carry

    lax.fori_loop(0, bias_ref.shape[0], per_h, 0)


def _bias_tile_sums(db_ref, hh):
    acc = {}
    for typ, rq, rkk, dr in _bias_blocks():
        if dr is None:
            continue
        blk = db_ref[hh, typ, rq * 64:(rq + 1) * 64, rkk * 64:(rkk + 1) * 64]
        acc[dr] = blk if dr not in acc else acc[dr] + blk
    return acc


def _small_reduce_body(dt_ref, dlg_ref, drpb_ref, dlgo_ref, p_ref):
    dc, _ = _dc_masks()
    masks = [(dc == j).astype(F32) for j in range(31)]
    ones = jnp.ones((8, GRID_W), F32)
    p_ref[...] = jnp.zeros_like(p_ref)
    drpb_ref[...] = jnp.zeros_like(drpb_ref)

    def per_h(h, carry):
        for dr in range(-7, 8):
            t = dt_ref[h, dr + 7]
            for j in range(31):
                p_ref[j:j + 1, :] = jnp.sum(t * masks[j], axis=0, keepdims=True)
            red = lax.dot_general(ones, p_ref[...], (((1,), (1,)), ((), ())),
                                  precision=HIGHEST, preferred_element_type=F32)
            drpb_ref[h, dr + 7:dr + 8, :] = red[0:1, :]
        return carry

    lax.fori_loop(0, dt_ref.shape[0], per_h, 0)
    x = dlg_ref[0]
    for b in range(1, dlg_ref.shape[0]):
        x = x + dlg_ref[b]
    x = x.reshape(4 * 8, x.shape[-1])
    dlgo_ref[...] = jnp.dot(x, jnp.ones((x.shape[-1], 128), F32), precision=HIGHEST,
                            preferred_element_type=F32)


def _inproj_gather_call(order, x, ctx, mod, norm_g, win_b, cos2, sin2):
    B, L, _ = x.shape
    LC = ctx.shape[1]
    T = L + LC
    nl, nt = L // TQ, T // TQ
    assert LC == TQ and L % TQ == 0
    kscale = RET_DK ** -0.5
    HR = D // 2

    def body(ord_ref, x_ref, ctx_ref, mod_ref, g_ref, wown_ref, cos_ref, sin_ref, p_ref, h_ref, wf_ref,
             w_all, hs_ref, ssem, rsem, lsem):
        j, b, t = pl.program_id(0), pl.program_id(1), pl.program_id(2)
        first = (b == 0) & (t == 0)
        mx, my, mc = _mesh_pos()
        s = 2 * mx + my
        sib = (mx, my, 1 - mc)
        own = pltpu.make_async_copy(wown_ref, w_all.at[s], lsem.at[0])
        ici_send, ici_recv, fwd_send, fwd_recv, outs = [], [], [], [], [
            pltpu.make_async_copy(w_all.at[s], wf_ref.at[s], lsem.at[1])]
        for k, (px, py) in enumerate(_other_chips(mx, my)):
            ps = 2 * px + py
            mine = w_all.at[s, pl.ds(mc * HR, HR)]
            ici_send.append(_remote(mine, w_all.at[s, pl.ds(mc * HR, HR)], ssem, rsem, k, (px, py, mc)))
            ici_recv.append(_remote(mine, w_all.at[ps, pl.ds(mc * HR, HR)], ssem, rsem, k, (px, py, mc)))
            got = w_all.at[ps, pl.ds(mc * HR, HR)]
            fwd_send.append(_remote(got, got, ssem, rsem, 3 + k, sib))
            theirs = w_all.at[ps, pl.ds((1 - mc) * HR, HR)]
            fwd_recv.append(_remote(theirs, theirs, ssem, rsem, 3 + k, sib))
            outs.append(pltpu.make_async_copy(w_all.at[ps], wf_ref.at[ps], lsem.at[2 + k]))

        @pl.when(first & (j == 0))
        def _():
            own.start()
            own.wait()
            for cp in ici_send:
                cp.start()
            outs[0].start()

        for k in range(3):
            @pl.when(first & (j == k + 1))
            def _(k=k):
                ici_recv[k].wait_recv()
                fwd_send[k].start()
                fwd_recv[k].wait_recv()
                outs[1 + k].start()

        tile = b * nt + t

        @pl.when(j == 0)
        def _():
            is_lat = t < nl
            xt = jnp.where(is_lat, x_ref[...], ctx_ref[...])
            mrow = mod_ref[pl.ds(jnp.where(is_lat, b, B), 1), :]
            shift, scale = mrow[:, 0:D], mrow[:, D:2 * D]
            rstd = lax.rsqrt(jnp.mean(xt * xt, axis=-1, keepdims=True) + EPS)
            h0 = ((xt * rstd * g_ref[...]) * (1.0 + scale) + shift).astype(BF16)
            h_ref[...] = h0
            hs_ref[tile] = h0

        hb = hs_ref[tile]
        cs, sn = cos_ref[...], sin_ref[...]
        shard = ord_ref[j]
        for sh in range(N_SHARD):
            @pl.when(shard == sh)
            def _(sh=sh):
                for half in range(2):
                    sec = 2 * sh + half
                    acc = _dot(hb, w_all[sh, :, half * 512:(half + 1) * 512])
                    if sec == 0:
                        acc = acc * (NA_DH ** -0.5)
                    if sec in (4, 5):
                        for q in range(4):
                            a = acc[:, q * 128:(q + 1) * 128]
                            r = a * cs + pltpu.roll(a, 64, 1) * sn
                            if sec == 5:
                                r = r * kscale
                            p_ref[:, half * 512 + q * 128:half * 512 + (q + 1) * 128] = r.astype(BF16)
                    else:
                        p_ref[:, half * 512:(half + 1) * 512] = acc.astype(BF16)

        @pl.when((j == N_SHARD - 1) & (b == B - 1) & (t == nt - 1))
        def _():
            _finish(outs, ici_send + fwd_send, [])

    tok = lambda j, b, t, o: (jnp.where(j == 0, b, B - 1), jnp.where(j == 0, jnp.minimum(t, nl - 1), nl - 1), 0)
    grid_spec = pltpu.PrefetchScalarGridSpec(
        num_scalar_prefetch=1, grid=(N_SHARD, B, nt),
        in_specs=[
            pl.BlockSpec((None, TQ, D), tok),
            pl.BlockSpec((None, TQ, D), lambda j, b, t, o: (jnp.where(j == 0, b, B - 1), 0, 0)),
            pl.BlockSpec((8, 3 * D), lambda j, b, t, o: (0, 0)),
            pl.BlockSpec((1, D), lambda j, b, t, o: (0, 0)),
            ANY,
            pl.BlockSpec((TQ, RET_DK), lambda j, b, t, o: (t, 0)),
            pl.BlockSpec((TQ, RET_DK), lambda j, b, t, o: (t, 0)),
        ],
        out_specs=(pl.BlockSpec((None, TQ, D), lambda j, b, t, o: (b, t, o[j])),
                   pl.BlockSpec((None, TQ, D), lambda j, b, t, o: (
                       jnp.where(j == 0, b, B - 1), jnp.where(j == 0, t, nt - 1), 0)), ANY),
        scratch_shapes=[pltpu.VMEM((N_SHARD, D, D), BF16), pltpu.VMEM((B * nt, TQ, D), BF16),
                        pltpu.SemaphoreType.DMA((6,)), pltpu.SemaphoreType.DMA((6,)),
                        pltpu.SemaphoreType.DMA((5,))])
    return pl.pallas_call(
        body, name="in_proj", grid_spec=grid_spec,
        out_shape=(jax.ShapeDtypeStruct((B, T, 4 * D), BF16), jax.ShapeDtypeStruct((B, T, D), BF16),
                   jax.ShapeDtypeStruct((N_SHARD, D, D), BF16)),
        compiler_params=_params(("arbitrary",) * 3))(order, x, ctx, mod, norm_g, win_b, cos2, sin2)


def _na_specs(L, T, rows):
    nm = rows // 4
    q_spec = pl.BlockSpec((None, TQ, 128), lambda hp, b, m: (b, m, hp))
    k_spec = pl.BlockSpec((None, T, 128), lambda hp, b, m: (b, 0, 4 + hp))
    v_spec = pl.BlockSpec((None, T, 128), lambda hp, b, m: (b, 0, 8 + hp))
    g_spec = pl.BlockSpec((None, TQ, 128), lambda hp, b, m: (b, m, 12 + hp))
    bias_spec = pl.BlockSpec((2, 3, TQ, KW), lambda hp, b, m: (hp, 0, 0, 0))
    return nm, q_spec, k_spec, v_spec, g_spec, bias_spec


def _na_tile(m, nm, rows):
    typ = jnp.where(m == 0, 0, jnp.where(m == nm - 1, 2, 1))
    start = pl.multiple_of(jnp.clip(4 * m - 4, 0, rows - 12) * GRID_W, TQ)
    return typ, start


def _na_fwd_call(P, bias, L, LC):
    B, T, _ = P.shape
    rows = L // GRID_W
    nm, q_spec, k_spec, v_spec, g_spec, bias_spec = _na_specs(L, T, rows)

    def body(q_ref, k_ref, v_ref, g_ref, bias_ref, y_ref, o_ref):
        typ, start = _na_tile(pl.program_id(2), nm, rows)
        for hh in range(2):
            ln = slice(hh * NA_DH, (hh + 1) * NA_DH)
            q = q_ref[:, ln]
            kw, vw = k_ref[pl.ds(start, KW), ln], v_ref[pl.ds(start, KW), ln]
            kc, vc = k_ref[L:L + LC, ln], v_ref[L:L + LC, ln]
            s1 = _dot_nt(q, kw) + bias_ref[hh, typ]
            s2 = _dot_nt(q, kc)
            mx = jnp.maximum(jnp.max(s1, axis=-1, keepdims=True), jnp.max(s2, axis=-1, keepdims=True))
            p1, p2 = jnp.exp(s1 - mx), jnp.exp(s2 - mx)
            inv = 1.0 / (jnp.sum(p1, axis=-1, keepdims=True) + jnp.sum(p2, axis=-1, keepdims=True))
            o = (_dot(p1.astype(BF16), vw) + _dot(p2.astype(BF16), vc)) * inv
            g = g_ref[:, ln].astype(F32)
            o_ref[:, ln] = o.astype(BF16)
            y_ref[:, ln] = (o * (g * _sigmoid(g))).astype(BF16)

    tile = pl.BlockSpec((None, TQ, 128), lambda hp, b, m: (b, m, hp))
    return pl.pallas_call(
        body, name="na_fwd", grid=(4, B, nm),
        in_specs=[q_spec, k_spec, v_spec, g_spec, bias_spec],
        out_specs=(tile, tile),
        out_shape=(jax.ShapeDtypeStruct((B, L, 512), BF16),) * 2,
        compiler_params=_params(("arbitrary",) * 3))(P, P, P, P, bias)


def _na_bwd_call(P, bias, dY, o_na, L, LC):
    B, T, _ = P.shape
    rows = L // GRID_W
    nm, q_spec, k_spec, v_spec, g_spec, bias_spec = _na_specs(L, T, rows)
    scale = NA_DH ** -0.5

    RB = 32

    def body(q_ref, k_ref, v_ref, g_ref, bias_ref, dy_ref, o_ref, dq_ref, dg_ref, dk_ref, dv_ref, dt_ref,
             db_ref, s1_ref, s2_ref, dp1_ref, dp2_ref, p1_ref, p2_ref, ds1_ref, ds2_ref, dkt_ref, dvt_ref):
        b, m = pl.program_id(1), pl.program_id(2)
        typ, start = _na_tile(m, nm, rows)

        @pl.when(m == 0)
        def _():
            dkt_ref[...] = jnp.zeros_like(dkt_ref)
            dvt_ref[...] = jnp.zeros_like(dvt_ref)

        @pl.when((m == 0) & (b == 0))
        def _():
            db_ref[...] = jnp.zeros_like(db_ref)

        for hh in range(2):
            ln = slice(hh * NA_DH, (hh + 1) * NA_DH)
            q = q_ref[:, ln]
            kw, vw = k_ref[pl.ds(start, KW), ln], v_ref[pl.ds(start, KW), ln]
            kc, vc = k_ref[L:L + LC, ln], v_ref[L:L + LC, ln]
            g = g_ref[:, ln].astype(F32)
            sg = _sigmoid(g)
            dy = dy_ref[:, ln].astype(F32)
            do = (dy * (g * sg)).astype(BF16)
            s1_ref[hh] = _dot_nt(q, kw)
            s2_ref[hh] = _dot_nt(q, kc)
            dp1_ref[hh] = _dot_nt(do, vw)
            dp2_ref[hh] = _dot_nt(do, vc)

            def rows_pass(r, carry, hh=hh):
                rw = pl.ds(pl.multiple_of(r * RB, RB), RB)
                a = s1_ref[hh, rw, :] + bias_ref[hh, typ, rw, :]
                c = s2_ref[hh, rw, :]
                mx = jnp.maximum(jnp.max(a, axis=-1, keepdims=True), jnp.max(c, axis=-1, keepdims=True))
                e1, e2 = jnp.exp(a - mx), jnp.exp(c - mx)
                inv = 1.0 / (jnp.sum(e1, axis=-1, keepdims=True) + jnp.sum(e2, axis=-1, keepdims=True))
                p1, p2 = e1 * inv, e2 * inv
                p1_ref[hh, rw, :] = p1.astype(BF16)
                p2_ref[hh, rw, :] = p2.astype(BF16)
                dp1, dp2 = dp1_ref[hh, rw, :], dp2_ref[hh, rw, :]
                delta = jnp.sum(p1 * dp1, axis=-1, keepdims=True) + jnp.sum(p2 * dp2, axis=-1, keepdims=True)
                ds1 = p1 * (dp1 - delta)
                db_ref[hh, typ, rw, :] += ds1
                ds1_ref[hh, rw, :] = ds1.astype(BF16)
                ds2_ref[hh, rw, :] = (p2 * (dp2 - delta)).astype(BF16)
                return carry

            lax.fori_loop(0, TQ // RB, rows_pass, 0, unroll=True)
            p1b, p2b, ds1b, ds2b = p1_ref[hh], p2_ref[hh], ds1_ref[hh], ds2_ref[hh]
            dg_ref[:, ln] = (dy * o_ref[:, ln].astype(F32) * (sg * (1.0 + g * (1.0 - sg)))).astype(BF16)
            dq_ref[:, ln] = ((_dot(ds1b, kw) + _dot(ds2b, kc)) * scale).astype(BF16)
            dkt_ref[ln, pl.ds(start, KW)] += _dot_tn(q, ds1b)
            dvt_ref[ln, pl.ds(start, KW)] += _dot_tn(do, p1b)
            dkt_ref[ln, L:L + LC] += _dot_tn(q, ds2b)
            dvt_ref[ln, L:L + LC] += _dot_tn(do, p2b)

        @pl.when(m == nm - 1)
        def _():
            dk_ref[...] = dkt_ref[...].T
            dv_ref[...] = dvt_ref[...].T

        @pl.when((m == nm - 1) & (b == B - 1))
        def _():
            for hh in range(2):
                for dr, t in _bias_tile_sums(db_ref, hh).items():
                    dt_ref[hh, dr + 7] = t

    tile = pl.BlockSpec((None, TQ, 128), lambda hp, b, m: (b, m, hp))
    kv_out = pl.BlockSpec((None, T, 128), lambda hp, b, m: (b, 0, hp))
    wide, narrow = (2, TQ, KW), (2, TQ, LC)
    return pl.pallas_call(
        body, name="na_bwd", grid=(4, B, nm),
        in_specs=[q_spec, k_spec, v_spec, g_spec, bias_spec, tile, tile],
        out_specs=(tile, tile, kv_out, kv_out,
                   pl.BlockSpec((2, 15, GRID_W, GRID_W), lambda hp, b, m: (hp, 0, 0, 0))),
        out_shape=(jax.ShapeDtypeStruct((B, L, 512), BF16), jax.ShapeDtypeStruct((B, L, 512), BF16),
                   jax.ShapeDtypeStruct((B, T, 512), F32), jax.ShapeDtypeStruct((B, T, 512), F32),
                   jax.ShapeDtypeStruct((bias.shape[0], 15, GRID_W, GRID_W), F32)),
        scratch_shapes=[pltpu.VMEM((2,) + bias.shape[1:], F32),
                        pltpu.VMEM(wide, F32), pltpu.VMEM(narrow, F32), pltpu.VMEM(wide, F32), pltpu.VMEM(narrow, F32),
                        pltpu.VMEM(wide, BF16), pltpu.VMEM(narrow, BF16), pltpu.VMEM(wide, BF16),
                        pltpu.VMEM(narrow, BF16), pltpu.VMEM((128, T), F32), pltpu.VMEM((128, T), F32)],
        compiler_params=_params(("arbitrary",) * 3))(P, P, P, P, bias, dY, o_na)


def _head_scalar(dec_ref, h):
    lane = lax.broadcasted_iota(jnp.int32, dec_ref.shape, 1)
    return -jnp.sum(jnp.where(lane == h, jnp.exp(dec_ref[...]), 0.0), axis=1, keepdims=True)


def _ret_specs(T):
    q_spec = pl.BlockSpec((None, TQ, 128), lambda b, h, i: (b, i, 16 + h))
    k_spec = pl.BlockSpec((None, T, 128), lambda b, h, i: (b, 0, 20 + h))
    v_spec = pl.BlockSpec((None, T, 128), lambda b, h, i: (b, 0, 24 + h))
    g_spec = pl.BlockSpec((None, TQ, 128), lambda b, h, i: (b, i, 28 + h))
    dec_spec = pl.BlockSpec((1, 4), lambda b, h, i: (0, 0))
    gn_spec = pl.BlockSpec((1, 128), lambda b, h, i: (0, h))
    return q_spec, k_spec, v_spec, g_spec, dec_spec, gn_spec


def _chunk_decay(lgf, lgb):
    tau = lax.broadcasted_iota(jnp.int32, (TQ, 1), 0).astype(F32)
    sig = lax.broadcasted_iota(jnp.int32, (1, TQ), 1).astype(F32)
    dist = tau - sig
    dm = jnp.exp(dist * jnp.where(dist > 0, lgf, -lgb)) * jnp.where(dist == 0, 2.0, 1.0)
    return tau, dist, dm


def _ret_states_call(P, dec_f, dec_b, L, LC):
    B, T, _ = P.shape
    n = L // TQ

    def body(df_ref, db_ref, k_ref, v_ref, sf_ref, sb_ref):
        h = pl.program_id(1)
        lgf, lgb = _head_scalar(df_ref, h), _head_scalar(db_ref, h)
        tau = lax.broadcasted_iota(jnp.int32, (TQ, 1), 0).astype(F32)
        jc = lax.broadcasted_iota(jnp.int32, (LC, 1), 0).astype(F32)
        wf, wb = jnp.exp(lgf * (TQ - 1.0 - tau)), jnp.exp(lgb * tau)
        gcf, gcb = jnp.exp(lgf * float(TQ)), jnp.exp(lgb * float(TQ))
        kc, vc = k_ref[L:L + LC, :].astype(F32), v_ref[L:L + LC, :]

        def chunk_state(i, w):
            ks = pl.multiple_of(i * TQ, TQ)
            return _dot_tn((k_ref[pl.ds(ks, TQ), :].astype(F32) * w).astype(BF16), v_ref[pl.ds(ks, TQ), :])

        def fwd(i, s):
            sf_ref[i] = s
            return gcf * s + chunk_state(i, wf)

        lax.fori_loop(0, n, fwd, _dot_tn((kc * jnp.exp(lgf * (LC - 1.0 - jc))).astype(BF16), vc), unroll=True)

        def bwd(r, s):
            i = n - 1 - r
            sb_ref[i] = s
            return gcb * s + chunk_state(i, wb)

        lax.fori_loop(0, n, bwd, _dot_tn((kc * jnp.exp(lgb * jc)).astype(BF16), vc), unroll=True)

    st = pl.BlockSpec((None, None, n, RET_DK, RET_DK), lambda b, h: (b, h, 0, 0, 0))
    return pl.pallas_call(
        body, name="ret_states", grid=(B, 4),
        in_specs=[pl.BlockSpec((1, 4), lambda b, h: (0, 0)), pl.BlockSpec((1, 4), lambda b, h: (0, 0)),
                  pl.BlockSpec((None, T, 128), lambda b, h: (b, 0, 20 + h)),
                  pl.BlockSpec((None, T, 128), lambda b, h: (b, 0, 24 + h))],
        out_specs=(st, st),
        out_shape=(jax.ShapeDtypeStruct((B, 4, n, RET_DK, RET_DK), F32),) * 2,
        compiler_params=_params(("arbitrary",) * 2))(dec_f, dec_b, P, P)


def _retc_fwd_call(P, sf, sb, dec_f, dec_b, ret_norm_g, L):
    B, T, _ = P.shape
    sec = lambda k: pl.BlockSpec((None, TQ, 512), lambda b, i: (b, i, k))
    dec_spec = pl.BlockSpec((1, 4), lambda b, i: (0, 0))
    st_spec = pl.BlockSpec((None, 4, None, RET_DK, RET_DK), lambda b, i: (b, 0, i, 0, 0))

    def body(df_ref, db_ref, q_ref, k_ref, v_ref, g_ref, gn_ref, sf_ref, sb_ref, y_ref, o_ref):
        for h in range(4):
            ln = slice(h * RET_DK, (h + 1) * RET_DK)
            lgf, lgb = _head_scalar(df_ref, h), _head_scalar(db_ref, h)
            tau, _, dm = _chunk_decay(lgf, lgb)
            q = q_ref[:, ln]
            qf = q.astype(F32)
            acc = _dot((_dot_nt(q, k_ref[:, ln]) * dm).astype(BF16), v_ref[:, ln])
            acc = acc + _dot((qf * jnp.exp(lgf * (tau + 1.0))).astype(BF16), sf_ref[h].astype(BF16))
            acc = acc + _dot((qf * jnp.exp(lgb * (TQ - tau))).astype(BF16), sb_ref[h].astype(BF16))
            o_ref[:, ln] = acc
            rn = lax.rsqrt(jnp.mean(acc * acc, axis=-1, keepdims=True) + EPS)
            g = g_ref[:, ln].astype(F32)
            y_ref[:, ln] = ((acc * rn * gn_ref[:, ln]) * (g * _sigmoid(g))).astype(BF16)

    tile = pl.BlockSpec((None, TQ, 512), lambda b, i: (b, i, 0))
    return pl.pallas_call(
        body, name="ret_fwd", grid=(B, L // TQ),
        in_specs=[dec_spec, dec_spec, sec(4), sec(5), sec(6), sec(7),
                  pl.BlockSpec((1, 512), lambda b, i: (0, 0)), st_spec, st_spec],
        out_specs=(tile, tile),
        out_shape=(jax.ShapeDtypeStruct((B, L, 512), BF16), jax.ShapeDtypeStruct((B, L, 512), F32)),
        compiler_params=_params(("arbitrary",) * 2))(dec_f, dec_b, P, P, P, P, ret_norm_g, sf, sb)


def _retc_bwd_call(P, sf, sb, dec_f, dec_b, ret_norm_g, o_ret, dY, cos2, sin2, L, LC):
    B, T, _ = P.shape
    n = L // TQ
    C = float(TQ)
    kscale = RET_DK ** -0.5
    st_spec = pl.BlockSpec((None, 4, n, RET_DK, RET_DK), lambda b, i: (b, 0, 0, 0, 0))

    def body(df_ref, db_ref, q_ref, k_ref, v_ref, g_ref, gn_ref, o_ref, dy_ref, cos_ref, sin_ref, sf_ref, sb_ref,
             dq_ref, dg_ref, dk_ref, dv_ref, dgn_ref, dlg_ref, dsf_ref, dsb_ref):
        i = pl.program_id(1)

        @pl.when(i == 0)
        def _():
            dk_ref[...] = jnp.zeros_like(dk_ref)
            dv_ref[...] = jnp.zeros_like(dv_ref)
            dgn_ref[...] = jnp.zeros_like(dgn_ref)
            dlg_ref[...] = jnp.zeros_like(dlg_ref)

        rows = pl.ds(pl.multiple_of(i * TQ, TQ), TQ)
        cs, sn = cos_ref[rows, :], sin_ref[rows, :]

        def one_head(h):
            ln = slice(h * RET_DK, (h + 1) * RET_DK)
            lgf, lgb = _head_scalar(df_ref, h), _head_scalar(db_ref, h)
            tau, dist, dm = _chunk_decay(lgf, lgb)

            def add_lg(row, x):
                csum = jnp.sum(x, axis=0, keepdims=True)
                tot = csum[:, 0:128]
                for part in range(1, x.shape[1] // 128):
                    tot = tot + csum[:, part * 128:(part + 1) * 128]
                dlg_ref[h, row:row + 1, :] += tot

            q = q_ref[:, ln]
            qf = q.astype(F32)
            o = o_ref[:, ln]
            g = g_ref[:, ln].astype(F32)
            dy = dy_ref[:, ln].astype(F32)
            gn = gn_ref[:, ln]
            sg = _sigmoid(g)
            rn = lax.rsqrt(jnp.mean(o * o, axis=-1, keepdims=True) + EPS)
            nrm = o * rn
            dg_ref[:, ln] = (dy * (nrm * gn) * (sg * (1.0 + g * (1.0 - sg)))).astype(BF16)
            dhn = dy * (g * sg)
            dgn_ref[:, ln] += jnp.sum(dhn * nrm, axis=0, keepdims=True)
            dnrm = dhn * gn
            do = rn * (dnrm - nrm * jnp.mean(dnrm * nrm, axis=-1, keepdims=True))
            dob = do.astype(BF16)
            ki, vi = k_ref[rows, ln], v_ref[rows, ln]
            s = _dot_nt(q, ki)
            dsv = _dot_nt(dob, vi)
            dsb = (dsv * dm).astype(BF16)
            dk_ref[rows, ln] += _dot_tn(dsb, q)
            dv_ref[rows, ln] += _dot_tn((s * dm).astype(BF16), dob)
            xw = s * dsv * dm * jnp.abs(dist)
            fpart = jnp.where(dist > 0, xw, 0.0)
            add_lg(0, fpart)
            add_lg(1, xw - fpart)
            dq = _dot(dsb, ki)
            af, ab = jnp.exp(lgf * (tau + 1.0)), jnp.exp(lgb * (C - tau))
            qa, qb = (qf * af).astype(BF16), (qf * ab).astype(BF16)
            sfi, sbi = sf_ref[h, i].astype(BF16), sb_ref[h, i].astype(BF16)
            dq = dq + af * _dot_nt(dob, sfi) + ab * _dot_nt(dob, sbi)
            dsf_ref[h, i] = _dot_tn(qa, dob)
            dsb_ref[h, i] = _dot_tn(qb, dob)
            add_lg(0, (tau + 1.0) * (_dot(qa, sfi) * do))
            add_lg(1, (C - tau) * (_dot(qb, sbi) * do))
            dq_ref[:, ln] = (dq * cs - pltpu.roll(dq, 64, 1) * sn).astype(BF16)

            @pl.when(i == n - 1)
            def _():
                jc = lax.broadcasted_iota(jnp.int32, (LC, 1), 0).astype(F32)
                crow = pl.ds(L, LC)

                def through_state(rws, w, dw, gst, row):
                    kk, vv = k_ref[rws, ln].astype(F32), v_ref[rws, ln]
                    gb = gst.astype(BF16)
                    vg = _dot_nt(vv, gb)
                    kw = kk * w
                    dk_ref[rws, ln] += w * vg
                    dv_ref[rws, ln] += _dot(kw.astype(BF16), gb)
                    add_lg(row, dw * (kw * vg))

                def scan(gc, w, dw, st_ref, dst_ref, order, row):
                    def step(r, gst):
                        j = order(r)
                        through_state(pl.ds(pl.multiple_of(j * TQ, TQ), TQ), w, dw, gst, row)
                        add_lg(row, (C * gc) * (gst * st_ref[h, j]))
                        return dst_ref[h, j] + gc * gst
                    return lax.fori_loop(0, n, step, jnp.zeros((RET_DK, RET_DK), F32), unroll=True)

                gcf, gcb = jnp.exp(lgf * C), jnp.exp(lgb * C)
                g0 = scan(gcf, jnp.exp(lgf * (C - 1.0 - tau)), C - 1.0 - tau, sf_ref, dsf_ref,
                          lambda r: n - 1 - r, 0)
                through_state(crow, jnp.exp(lgf * (LC - 1.0 - jc)), LC - 1.0 - jc, g0, 0)
                g1 = scan(gcb, jnp.exp(lgb * tau), tau, sb_ref, dsb_ref, lambda r: r, 1)
                through_state(crow, jnp.exp(lgb * jc), jc, g1, 1)
                dk = dk_ref[:, ln]
                dk_ref[:, ln] = (dk * cos_ref[...] - pltpu.roll(dk, 64, 1) * sin_ref[...]) * kscale

        for h in range(4):
            one_head(h)

    sec = lambda k: pl.BlockSpec((None, TQ, 512), lambda b, i: (b, i, k))
    full = lambda k: pl.BlockSpec((None, T, 512), lambda b, i: (b, 0, k))
    dec_spec = pl.BlockSpec((1, 4), lambda b, i: (0, 0))
    tab = pl.BlockSpec((T, RET_DK), lambda b, i: (0, 0))
    return pl.pallas_call(
        body, name="ret_bwd", grid=(B, n),
        in_specs=[dec_spec, dec_spec, sec(4), full(5), full(6), sec(7),
                  pl.BlockSpec((1, 512), lambda b, i: (0, 0)), sec(0), sec(1), tab, tab, st_spec, st_spec],
        out_specs=(sec(0), sec(0), full(0), full(0),
                   pl.BlockSpec((None, 1, 512), lambda b, i: (b, 0, 0)),
                   pl.BlockSpec((None, 4, 8, 128), lambda b, i: (b, 0, 0, 0))),
        out_shape=(jax.ShapeDtypeStruct((B, L, 512), BF16), jax.ShapeDtypeStruct((B, L, 512), BF16),
                   jax.ShapeDtypeStruct((B, T, 512), F32), jax.ShapeDtypeStruct((B, T, 512), F32),
                   jax.ShapeDtypeStruct((B, 1, 512), F32), jax.ShapeDtypeStruct((B, 4, 8, 128), F32)),
        scratch_shapes=[pltpu.VMEM((4, n, RET_DK, RET_DK), F32), pltpu.VMEM((4, n, RET_DK, RET_DK), F32)],
        compiler_params=_params(("arbitrary",) * 2, vmem_mb=56))(
            dec_f, dec_b, P, P, P, P, ret_norm_g, o_ret, dY, cos2, sin2, sf, sb)


def _out_call(y_na, y_ret, x, target, mod, final_g, wout_f):
    B, L, _ = x.shape

    def body(yn_ref, yr_ref, x_ref, t_ref, mod_ref, gf_ref, w_ref, dy_ref, dx2_ref, dw_ref, sm_ref):
        b, i = pl.program_id(0), pl.program_id(1)

        @pl.when((b == 0) & (i == 0))
        def _():
            dw_ref[...] = jnp.zeros_like(dw_ref)
            sm_ref[...] = jnp.zeros_like(sm_ref)

        gate = mod_ref[pl.ds(b, 1), 2 * D:3 * D]
        gf = gf_ref[...]
        yn, yr = yn_ref[...], yr_ref[...]
        ylat = _dot(yn, w_ref[0:512, :]) + _dot(yr, w_ref[512:1024, :])
        x2 = x_ref[...] + gate * ylat
        r = lax.rsqrt(jnp.mean(x2 * x2, axis=-1, keepdims=True) + EPS)
        xr = x2 * r
        err = xr * gf - t_ref[...]
        sm_ref[1:2, :] += jnp.sum(err * err, axis=0, keepdims=True)
        dout = err * (1.0 / D)
        sm_ref[0:1, :] += jnp.sum(dout * xr, axis=0, keepdims=True)
        gd = dout * gf
        dx2 = r * (gd - xr * jnp.mean(gd * xr, axis=-1, keepdims=True))
        dx2_ref[...] = dx2
        sm_ref[pl.ds(2 + b, 1), :] += jnp.sum(dx2 * ylat, axis=0, keepdims=True)
        dyl = (gate * dx2).astype(BF16)
        dy_ref[:, 0:512] = _dot_nt(dyl, w_ref[0:512, :]).astype(BF16)
        dy_ref[:, 512:1024] = _dot_nt(dyl, w_ref[512:1024, :]).astype(BF16)
        dw_ref[0:512, :] += _dot_tn(yn, dyl)
        dw_ref[512:1024, :] += _dot_tn(yr, dyl)

    half = pl.BlockSpec((None, TQ, 512), lambda b, i: (b, i, 0))
    full = pl.BlockSpec((None, TQ, D), lambda b, i: (b, i, 0))
    return pl.pallas_call(
        body, name="out_proj_loss", grid=(B, L // TQ),
        in_specs=[half, half, full, full,
                  pl.BlockSpec((8, 3 * D), lambda b, i: (0, 0)),
                  pl.BlockSpec((1, D), lambda b, i: (0, 0)),
                  pl.BlockSpec((D, D), lambda b, i: (0, 0))],
        out_specs=(full, full, pl.BlockSpec((D, D), lambda b, i: (0, 0)),
                   pl.BlockSpec((8, D), lambda b, i: (0, 0))),
        out_shape=(jax.ShapeDtypeStruct((B, L, D), BF16), jax.ShapeDtypeStruct((B, L, D), F32),
                   jax.ShapeDtypeStruct((D, D), F32), jax.ShapeDtypeStruct((8, D), F32)),
        compiler_params=_params(("arbitrary",) * 2))(y_na, y_ret, x, target, mod, final_g, wout_f)


def _dh_call(dsec, win_f, x, ctx, dx2, mod, norm_g, cp_in, cp_out):
    B, L, _ = x.shape
    LC = ctx.shape[1]
    nl = L // TQ

    def body(d0, d1, d2, d3, d4, d5, d6, d7, w_ref, x_ref, ctx_ref, dx2_ref, mod_ref, g_ref, cpi_ref, cpo_ref,
             gx_ref, sm_ref, sli_ref, slo_ref, ssem, rsem, lsem):
        drefs = (d0, d1, d2, d3, d4, d5, d6, d7)
        b, t = pl.program_id(0), pl.program_id(1)
        is_lat = t < nl

        @pl.when((b == 0) & (t == 0))
        def _():
            sm_ref[...] = jnp.zeros_like(sm_ref)

        def dh_of(secs):
            acc = jnp.zeros((TQ, D), F32)
            for sec in secs:
                s, half = divmod(sec, 2)
                acc = acc + _dot_nt(drefs[sec][...].astype(BF16), w_ref[s, :, half * 512:(half + 1) * 512])
            return acc

        def norm_bwd(dh, xt, mrow):
            scale = mrow[:, D:2 * D]
            g = g_ref[...]
            rstd = lax.rsqrt(jnp.mean(xt * xt, axis=-1, keepdims=True) + EPS)
            xn = xt * rstd
            dshift = jnp.sum(dh, axis=0, keepdims=True)
            dscale = jnp.sum(dh * (xn * g), axis=0, keepdims=True)
            dhn = dh * (1.0 + scale)
            sm_ref[0:1, :] += jnp.sum(dhn * xn, axis=0, keepdims=True)
            dxn = dhn * g
            dx = rstd * (dxn - xn * jnp.mean(dxn * xn, axis=-1, keepdims=True))
            return dshift, dscale, dx

        @pl.when(is_lat)
        def _():
            dshift, dscale, dx = norm_bwd(dh_of(range(8)), x_ref[...], mod_ref[pl.ds(b, 1), :])
            sm_ref[pl.ds(3 + b, 1), :] += dshift
            sm_ref[pl.ds(3 + B + b, 1), :] += dscale
            gx_ref[...] = dx2_ref[...] + dx

        @pl.when(jnp.logical_not(is_lat))
        def _():
            dshift, dscale, _ = norm_bwd(dh_of((1, 2, 5, 6)), ctx_ref[...], mod_ref[B:B + 1, :])
            sm_ref[1:2, :] += dshift
            sm_ref[2:3, :] += dscale

        mx, my, mc = _mesh_pos()
        s = 2 * mx + my
        cps, sls = (cpi_ref, cpo_ref), (sli_ref, slo_ref)
        own = [pltpu.make_async_copy(cps[a].at[s], sls[a].at[s], lsem.at[a]) for a in range(2)]
        sends, recvs, k = [], [], 0
        for px, py in _other_chips(mx, my):
            ps = 2 * px + py
            for a in range(2):
                sends.append(_remote(cps[a].at[ps], sls[a].at[s], ssem, rsem, k, (px, py, mc)))
                recvs.append(_remote(cps[a].at[s], sls[a].at[ps], ssem, rsem, k, (px, py, mc)))
                k += 1

        @pl.when((b == 0) & (t == 0))
        def _():
            for cp in own + sends:
                cp.start()

        @pl.when((b == B - 1) & (t == nl))
        def _():
            _finish(own, sends, recvs)

    lat = lambda b, t: (b, jnp.minimum(t, nl - 1), 0)
    tok = lambda b, t: (b, t, 0)
    sec_specs = [pl.BlockSpec((None, TQ, 512), lat if sec in (0, 3, 4, 7) else tok) for sec in range(8)]
    return pl.pallas_call(
        body, name="dh_norm_bwd", grid=(B, nl + 1),
        in_specs=sec_specs + [
            pl.BlockSpec((N_SHARD, D, D), lambda b, t: (0, 0, 0)),
            pl.BlockSpec((None, TQ, D), lat),
            pl.BlockSpec((None, LC, D), lambda b, t: (b, 0, 0)),
            pl.BlockSpec((None, TQ, D), lat),
            pl.BlockSpec((8, 3 * D), lambda b, t: (0, 0)),
            pl.BlockSpec((1, D), lambda b, t: (0, 0)), ANY, ANY],
        out_specs=(pl.BlockSpec((None, TQ, D), lat), pl.BlockSpec((8, D), lambda b, t: (0, 0)), ANY, ANY),
        out_shape=(jax.ShapeDtypeStruct((B, L, D), F32), jax.ShapeDtypeStruct((8, D), F32),
                   jax.ShapeDtypeStruct(cp_in.shape, cp_in.dtype), jax.ShapeDtypeStruct(cp_out.shape, cp_out.dtype)),
        scratch_shapes=[pltpu.SemaphoreType.DMA((6,)), pltpu.SemaphoreType.DMA((6,)),
                        pltpu.SemaphoreType.DMA((2,))],
        compiler_params=_params(("arbitrary",) * 2))(*dsec, win_f, x, ctx, dx2, mod, norm_g, cp_in, cp_out)


def _dw_call(dsec, h, L):
    B, T, _ = h.shape
    nl = L // TQ

    def body(d0, d1, d2, d3, d4, d5, d6, d7, h_ref, dw_ref, acc_ref):
        drefs = (d0, d1, d2, d3, d4, d5, d6, d7)
        b, t = pl.program_id(0), pl.program_id(1)

        @pl.when((b == 0) & (t == 0))
        def _():
            acc_ref[...] = jnp.zeros_like(acc_ref)

        hb = h_ref[...]

        def add(secs):
            for sec in secs:
                s, half = divmod(sec, 2)
                acc_ref[s, :, half * 512:(half + 1) * 512] += _dot_tn(hb, drefs[sec][...].astype(BF16))

        @pl.when(t < nl)
        def _():
            add(range(8))

        @pl.when(t >= nl)
        def _():
            add((1, 2, 5, 6))

        @pl.when((b == B - 1) & (t == nl))
        def _():
            dw_ref[...] = acc_ref[...].astype(BF16)

    lat = lambda b, t: (b, jnp.minimum(t, nl - 1), 0)
    tok = lambda b, t: (b, t, 0)
    sec_specs = [pl.BlockSpec((None, TQ, 512), lat if sec in (0, 3, 4, 7) else tok) for sec in range(8)]
    return pl.pallas_call(
        body, name="dw_in", grid=(B, nl + 1),
        in_specs=sec_specs + [pl.BlockSpec((None, TQ, D), tok)],
        out_specs=pl.BlockSpec((N_SHARD, D, D), lambda b, t: (0, 0, 0)),
        out_shape=jax.ShapeDtypeStruct((N_SHARD, D, D), BF16),
        scratch_shapes=[pltpu.VMEM((N_SHARD, D, D), F32)],
        compiler_params=_params(("arbitrary",) * 2, vmem_mb=56))(*dsec, h)


def _mesh_pos():
    return lax.axis_index("x"), lax.axis_index("y"), lax.axis_index("c")


def _flip(v, f):
    return 1 - v if f else v


def _remote(src, dst, ssem, rsem, k, peer):
    return pltpu.make_async_remote_copy(src_ref=src, dst_ref=dst, send_sem=ssem.at[k], recv_sem=rsem.at[k],
                                        device_id=peer, device_id_type=MESH)


def _other_chips(x, y):
    return [(_flip(x, fx), _flip(y, fy)) for fx, fy in ((1, 0), (0, 1), (1, 1))]


def _all_to_all_small(src, dst_all, ssem, rsem, k0, x, y, cc):
    me = 4 * x + 2 * y + cc
    sends, recvs = [], []
    for f in range(1, N_DEV):
        px, py, pc = _flip(x, f & 4), _flip(y, f & 2), _flip(cc, f & 1)
        sends.append(_remote(src, dst_all.at[me], ssem, rsem, k0 + f - 1, (px, py, pc)))
        recvs.append(_remote(src, dst_all.at[4 * px + 2 * py + pc], ssem, rsem, k0 + f - 1, (px, py, pc)))
    return sends, recvs


def _finish(local, sends, recvs):
    for cp in recvs:
        cp.wait_recv()
    for cp in sends:
        cp.wait_send()
    for cp in local:
        cp.wait()


def _gather_call(wout_b, wada_b, c, rpb_flat):
    arrs = (wout_b, wada_b)
    na = len(arrs)
    hrs = [a.shape[0] // 2 for a in arrs]

    def body(wout, wada, c_ref, r_ref, wout_f, wada_f, c_all, bias_ref, et_ref, ssem, rsem, lsem):
        x, y, cc = _mesh_pos()
        s, me = 2 * x + y, 4 * x + 2 * y + cc
        sib = (x, y, 1 - cc)
        srcs, dsts = (wout, wada), (wout_f, wada_f)

        def half(a, shard, hc):
            return dsts[a].at[shard, pl.ds(hc * hrs[a], hrs[a])]

        local = [pltpu.make_async_copy(srcs[a], dsts[a].at[s], lsem.at[a]) for a in range(na)]
        local.append(pltpu.make_async_copy(c_ref, c_all.at[me], lsem.at[na]))
        ici_send, ici_recv, fwd_send, fwd_recv, k = [], [], [], [], 0
        for px, py in _other_chips(x, y):
            ps = 2 * px + py
            for a in range(na):
                mine = srcs[a].at[pl.ds(cc * hrs[a], hrs[a])]
                ici_send.append(_remote(mine, half(a, s, cc), ssem, rsem, k, (px, py, cc)))
                ici_recv.append(_remote(mine, half(a, ps, cc), ssem, rsem, k, (px, py, cc)))
                fwd_send.append(_remote(half(a, ps, cc), half(a, ps, cc), ssem, rsem, 3 * na + k, sib))
                fwd_recv.append(_remote(half(a, ps, 1 - cc), half(a, ps, 1 - cc), ssem, rsem, 3 * na + k, sib))
                k += 1
        c_send, c_recv = _all_to_all_small(c_ref, c_all, ssem, rsem, 6 * na, x, y, cc)
        for cp in local + ici_send + c_send:
            cp.start()
        _bias_body(r_ref, bias_ref, et_ref)
        for got, fwd in zip(ici_recv, fwd_send):
            got.wait_recv()
            fwd.start()
        _finish(local, ici_send + fwd_send + c_send, fwd_recv + c_recv)

    return pl.pallas_call(
        body, name="weight_gather",
        in_specs=[pl.BlockSpec(memory_space=pltpu.VMEM)] * 3 + [pl.BlockSpec(memory_space=pltpu.SMEM)],
        out_specs=(pl.BlockSpec(memory_space=pltpu.VMEM),) * 4,
        out_shape=tuple(jax.ShapeDtypeStruct((N_SHARD,) + a.shape, a.dtype) for a in arrs)
        + (jax.ShapeDtypeStruct((N_DEV,) + c.shape, c.dtype),
           jax.ShapeDtypeStruct((rpb_flat.shape[0], 3, TQ, KW), F32)),
        scratch_shapes=[pltpu.VMEM((15, GRID_W, GRID_W), F32),
                        pltpu.SemaphoreType.DMA((6 * na + 7,)), pltpu.SemaphoreType.DMA((6 * na + 7,)),
                        pltpu.SemaphoreType.DMA((na + 1,))],
        compiler_params=pltpu.CompilerParams(vmem_limit_bytes=56 << 20))(wout_b, wada_b, c, rpb_flat)


VROWS = 32


def _grad_halves_call(dwin_b, dwout_b, dbias, dlg):
    arrs = (dwin_b, dwout_b)
    hrs = [a.shape[1] // 2 for a in arrs]

    def body(din, dout, db_ref, dlg_ref, cp_in, cp_out, drpb_ref, dlgo_ref, got_in, got_out, p_ref, ssem, rsem):
        x, y, cc = _mesh_pos()
        sib = (x, y, 1 - cc)
        srcs, gots, cps = (din, dout), (got_in, got_out), (cp_in, cp_out)
        halves = [_remote(srcs[a].at[:, pl.ds((1 - cc) * hrs[a], hrs[a])], gots[a], ssem, rsem, a, sib)
                  for a in range(2)]
        for cp in halves:
            cp.start()
        _small_reduce_body(db_ref, dlg_ref, drpb_ref, dlgo_ref, p_ref)
        for cp in halves:
            cp.wait_recv()
        for a in range(2):
            for j in range(N_SHARD):
                def add(i, carry, a=a, j=j):
                    r = pl.multiple_of(i * VROWS, VROWS)
                    mine = srcs[a][j, pl.ds(pl.multiple_of(cc * hrs[a] + r, VROWS), VROWS), :].astype(F32)
                    cps[a][j, pl.ds(r, VROWS), :] = (
                        mine + gots[a][j, pl.ds(r, VROWS), :].astype(F32)).astype(BF16)
                    return carry
                lax.fori_loop(0, hrs[a] // VROWS, add, 0)
        for cp in halves:
            cp.wait_send()

    vmem = pl.BlockSpec(memory_space=pltpu.VMEM)
    half_shapes = [(N_SHARD, hrs[a], arrs[a].shape[2]) for a in range(2)]
    return pl.pallas_call(
        body, name="grad_halves",
        in_specs=[vmem] * 4, out_specs=(vmem,) * 4,
        out_shape=(jax.ShapeDtypeStruct(half_shapes[0], BF16), jax.ShapeDtypeStruct(half_shapes[1], BF16),
                   jax.ShapeDtypeStruct((dbias.shape[0], 16, 32), F32), jax.ShapeDtypeStruct((32, 128), F32)),
        scratch_shapes=[pltpu.VMEM(half_shapes[0], BF16), pltpu.VMEM(half_shapes[1], BF16),
                        pltpu.VMEM((32, GRID_W), F32),
                        pltpu.SemaphoreType.DMA((2,)), pltpu.SemaphoreType.DMA((2,))],
        compiler_params=pltpu.CompilerParams(vmem_limit_bytes=56 << 20))(dwin_b, dwout_b, dbias, dlg)


def _grad_finish_call(sl_in, sl_out, small):
    arrs = (sl_in, sl_out)

    def body(sin, sout, sm, gin, gout, sm_all, h_in, h_out, ssem, rsem, lsem):
        x, y, cc = _mesh_pos()
        me = 4 * x + 2 * y + cc
        sib = (x, y, 1 - cc)
        sls, hs, gs = (sin, sout), (h_in, h_out), (gin, gout)
        sm_send, sm_recv = _all_to_all_small(sm, sm_all, ssem, rsem, 2, x, y, cc)
        sm_own = pltpu.make_async_copy(sm, sm_all.at[me], lsem.at[0])
        for cp in sm_send + [sm_own]:
            cp.start()
        for a in range(2):
            def total(i, carry, a=a):
                rows = pl.ds(pl.multiple_of(i * VROWS, VROWS), VROWS)
                sl = sls[a]
                hs[a][rows, :] = ((sl[0, rows, :].astype(F32) + sl[1, rows, :].astype(F32))
                                  + sl[2, rows, :].astype(F32)) + sl[3, rows, :].astype(F32)
                return carry
            lax.fori_loop(0, arrs[a].shape[1] // VROWS, total, 0)
        mine = [pltpu.make_async_copy(hs[a], gs[a].at[cc], lsem.at[1 + a]) for a in range(2)]
        back = [_remote(hs[a], gs[a].at[cc], ssem, rsem, a, sib) for a in range(2)]
        back_recv = [_remote(hs[a], gs[a].at[1 - cc], ssem, rsem, a, sib) for a in range(2)]
        for cp in mine + back:
            cp.start()
        _finish(mine + [sm_own], back + sm_send, back_recv + sm_recv)

    vmem = pl.BlockSpec(memory_space=pltpu.VMEM)
    return pl.pallas_call(
        body, name="grad_finish",
        in_specs=[vmem] * 3, out_specs=(vmem,) * 3,
        out_shape=(jax.ShapeDtypeStruct((2,) + sl_in.shape[1:], F32),
                   jax.ShapeDtypeStruct((2,) + sl_out.shape[1:], F32),
                   jax.ShapeDtypeStruct((N_DEV,) + small.shape, F32)),
        scratch_shapes=[pltpu.VMEM(sl_in.shape[1:], F32), pltpu.VMEM(sl_out.shape[1:], F32),
                        pltpu.SemaphoreType.DMA((9,)), pltpu.SemaphoreType.DMA((9,)),
                        pltpu.SemaphoreType.DMA((3,))],
        compiler_params=pltpu.CompilerParams(vmem_limit_bytes=48 << 20))(sl_in, sl_out, small)


def _adamw(w, g, m, v):
    m = ADAM_B1 * m + (1.0 - ADAM_B1) * g
    v = ADAM_B2 * v + (1.0 - ADAM_B2) * (g * g)
    m_hat = m / (1.0 - ADAM_B1 ** ADAM_STEP)
    v_hat = v / (1.0 - ADAM_B2 ** ADAM_STEP)
    return -ADAM_LR * (m_hat / (jnp.sqrt(v_hat) + ADAM_EPS) + ADAM_WD * w), m, v


def _adam_call(w, m, v, g, name):
    R, C = w.shape
    tr = 256

    def body(w_ref, m_ref, v_ref, g_ref, d_ref, mo_ref, vo_ref):
        d_ref[...], mo_ref[...], vo_ref[...] = _adamw(w_ref[...], g_ref[...], m_ref[...], v_ref[...])

    spec = pl.BlockSpec((tr, C), lambda i: (i, 0))
    return pl.pallas_call(
        body, name=name, grid=(R // tr,), in_specs=[spec] * 4,
        out_specs=(spec,) * 3, out_shape=(jax.ShapeDtypeStruct((R, C), F32),) * 3,
        compiler_params=_params(("arbitrary",)))(w, m, v, g)


R_GF, R_NG, R_LOSS, R_RNG, R_LGF, R_LGB, R_SHIFT, R_SCALE, R_GATE, R_SHIFT_C, R_SCALE_C, R_RNG2, R_RPB = (
    0, 1, 2, 3, 4, 5, 6, 8, 10, 12, 13, 14, 16)
W_GF, W_NG, W_CCTX, W_RNG, W_DF, W_DB, W_BADA, W_RPB = 0, 1, 2, 3, 4, 5, 6, 9


def _small_final_call(sm_all, c_t, c_ctx, wada_f, wada, m_ada, v_ada, wsm, msm, vsm, B):
    ws = wada.shape[1]
    NB = N_DEV * B

    def body(sm_ref, ct_ref, cctx_ref, wf_ref, wa_ref, ma_ref, va_ref, w_ref, m_ref, v_ref,
             g_ref, d_ref, mo_ref, vo_ref, ga_ref, da_ref, mao_ref, vao_ref, loss_ref, dmod_ref):
        x, y, _ = _mesh_pos()
        s = 2 * x + y
        tot = sm_ref[0]
        for dv in range(1, N_DEV):
            tot = tot + sm_ref[dv]
        w = w_ref[...]
        for dv in range(N_DEV):
            for b in range(B):
                r = dv * B + b
                for part, row in enumerate((R_SHIFT, R_SCALE, R_GATE)):
                    dmod_ref[r:r + 1, part * D:(part + 1) * D] = sm_ref[dv, row + b:row + b + 1, :]
        dmod_ref[NB:NB + 1, 0:D] = tot[R_SHIFT_C:R_SHIFT_C + 1, :]
        dmod_ref[NB:NB + 1, D:2 * D] = tot[R_SCALE_C:R_SCALE_C + 1, :]
        dmod_ref[NB:NB + 1, 2 * D:3 * D] = jnp.zeros((1, D), F32)
        dmod_ref[NB + 1:, :] = jnp.zeros((dmod_ref.shape[0] - NB - 1, 3 * D), F32)
        dmod = dmod_ref[...]
        cc = cctx_ref[...]
        scc = _sigmoid(cc)
        ct = ct_ref[...]
        act_t = ct * _sigmoid(ct)
        dmc = dmod[NB:NB + 1, :].astype(BF16)
        dact = jnp.zeros((1, D), F32)
        for sh in range(N_SHARD):
            dact = dact + _dot_nt(dmc[:, sh * ws:(sh + 1) * ws], wf_ref[sh])
        g = jnp.zeros((16, D), F32)
        rows = lax.broadcasted_iota(jnp.int32, (16, D), 0)

        def put(g, row, val):
            return jnp.where(rows == row, val, g)

        g = put(g, W_GF, tot[R_GF:R_GF + 1, :])
        g = put(g, W_NG, tot[R_NG:R_NG + 1, :])
        g = put(g, W_CCTX, dact * (scc * (1.0 + cc * (1.0 - scc))))
        g = put(g, W_RNG, tot[R_RNG:R_RNG + 1, :] + tot[R_RNG2:R_RNG2 + 1, :])
        g = put(g, W_DF, tot[R_LGF:R_LGF + 1, :] * (-jnp.exp(w[W_DF:W_DF + 1, :])))
        g = put(g, W_DB, tot[R_LGB:R_LGB + 1, :] * (-jnp.exp(w[W_DB:W_DB + 1, :])))
        db = jnp.sum(dmod, axis=0, keepdims=True)
        for part in range(3):
            g = put(g, W_BADA + part, db[:, part * D:(part + 1) * D])
        for part in range(4):
            g = put(g, W_RPB + part, tot[R_RPB + part:R_RPB + part + 1, :])
        g_ref[...] = g
        d_ref[...], mo_ref[...], vo_ref[...] = _adamw(w, g, m_ref[...], v_ref[...])
        loss_ref[...] = jnp.broadcast_to(
            (0.5 / D) * jnp.sum(tot[R_LOSS:R_LOSS + 1, :], axis=1, keepdims=True), (8, 128))
        for sh in range(N_SHARD):
            @pl.when(s == sh)
            def _():
                ga = jnp.dot(act_t, dmod[:, sh * ws:(sh + 1) * ws], precision=HIGHEST,
                             preferred_element_type=F32)
                ga_ref[...] = ga
                da_ref[...], mao_ref[...], vao_ref[...] = _adamw(wa_ref[...], ga, ma_ref[...], va_ref[...])

    sh_small = jax.ShapeDtypeStruct((16, D), F32)
    sh_ada = jax.ShapeDtypeStruct(wada.shape, F32)
    return pl.pallas_call(
        body, name="small_final",
        out_shape=(sh_small,) * 4 + (sh_ada,) * 4 + (jax.ShapeDtypeStruct((8, 128), F32),),
        scratch_shapes=[pltpu.VMEM((NB + 8, 3 * D), F32)],
        compiler_params=_params(vmem_mb=56))(
            sm_all, c_t, c_ctx, wada_f, wada, m_ada, v_ada, wsm, msm, vsm)


def _local_step(order, x, c, ctx, c_ctx, norm_g, wada_f, b_ada, win_b, bias, dec_f, dec_b, ret_norm_g,
                wout_f, final_g, target):
    B, L, _ = x.shape
    LC = ctx.shape[1]
    assert B == 2
    cos2, sin2 = _rope_tables(L, LC)
    c8 = jnp.concatenate([c, c_ctx[None, :], jnp.zeros((8 - B - 1, D), F32)], axis=0)
    mod = _mod_call(c8, wada_f, b_ada)
    P, h, win_f = _inproj_gather_call(order, x, ctx, mod, norm_g, win_b, cos2, sin2)
    y_na, o_na = _na_fwd_call(P, bias, L, LC)
    sf, sb = _ret_states_call(P, dec_f, dec_b, L, LC)
    y_ret, o_ret = _retc_fwd_call(P, sf, sb, dec_f, dec_b, ret_norm_g, L)
    dY, dx2, dwout_p, sm_out = _out_call(y_na, y_ret, x, target, mod, final_g, wout_f.reshape(D, D))
    dnq, dng, dnk, dnv, dbias = _na_bwd_call(P, bias, dY, o_na, L, LC)
    drq, drg, drk, drv, dgn, dlg = _retc_bwd_call(P, sf, sb, dec_f, dec_b, ret_norm_g, o_ret, dY, cos2, sin2, L, LC)
    dsec = (dnq, dnk, dnv, dng, drq, drk, drv, drg)
    dwin_b = _dw_call(dsec, h, L)
    cp_in, cp_out, drpb, dlg_sum = _grad_halves_call(
        dwin_b, dwout_p.astype(BF16).reshape(N_SHARD, D // N_SHARD, D), dbias, dlg)
    grad_x, sm_dh, sl_in, sl_out = _dh_call(dsec, win_f, x, ctx, dx2, mod, norm_g, cp_in, cp_out)
    z = jnp.zeros((1, D), F32)
    pad = lambda v: jnp.pad(v.reshape(1, -1), ((0, 0), (0, D - v.size)))
    dlg_sum = dlg_sum.reshape(4, 8, 128)
    rpb_rows = jnp.pad(drpb[:, :15, :31].reshape(-1), (0, 4 * D - drpb.shape[0] * 465)).reshape(4, D)
    small = jnp.concatenate([
        sm_out[0:1], sm_dh[0:1], sm_out[1:2], pad(dgn[0]), pad(dlg_sum[:, 0, 0]), pad(dlg_sum[:, 1, 0]),
        sm_dh[3:5], sm_dh[5:7], sm_out[2:4], sm_dh[1:2], sm_dh[2:3], pad(dgn[1]), z, rpb_rows,
        jnp.zeros((SM_ROWS - 20, D), F32)], axis=0)
    return grad_x, sl_in, sl_out, small


def kernel(x, c, ctx, c_ctx, norm_g, w_ada, b_ada, w_in, na_rpb, ret_decay_fwd, ret_decay_bwd, ret_norm_g, w_out, final_norm_g, loss_target, m_c_ctx, m_norm_g, m_w_ada, m_b_ada, m_w_in, m_na_rpb, m_ret_decay_fwd, m_ret_decay_bwd, m_ret_norm_g, m_w_out, m_final_norm_g, v_c_ctx, v_norm_g, v_w_ada, v_b_ada, v_w_in, v_na_rpb, v_ret_decay_fwd, v_ret_decay_bwd, v_ret_norm_g, v_w_out, v_final_norm_g):
    B = x.shape[0]
    wout_f, wada_f, c_all, bias = _gather_call(
        w_out[0].astype(BF16), w_ada[0].astype(BF16), c, na_rpb[0].reshape(na_rpb.shape[1], -1))
    mx, my = lax.axis_index("x"), lax.axis_index("y")
    order = jnp.stack([2 * mx + my, 2 * (1 - mx) + my, 2 * mx + (1 - my),
                       2 * (1 - mx) + (1 - my)]).astype(jnp.int32)
    grad_x, sl_in, sl_out, small = _local_step(
        order, x, c, ctx, c_ctx, norm_g, wada_f, b_ada, w_in[0].astype(BF16), bias, ret_decay_fwd,
        ret_decay_bwd, ret_norm_g, wout_f, final_norm_g.reshape(1, D), loss_target)
    gin, gout, sm_all = _grad_finish_call(sl_in, sl_out, small)
    g_win, g_wout = gin.reshape(w_in.shape[1:]), gout.reshape(w_out.shape[1:])
    d_win, nm_win, nv_win = _adam_call(w_in[0], m_w_in[0], v_w_in[0], g_win, "adam_w_in")
    d_wout, nm_wout, nv_wout = _adam_call(w_out[0], m_w_out[0], v_w_out[0], g_wout, "adam_w_out")

    def pack(gf, ng, cc, rng, df, db, bada, rpb):
        pad = lambda v: jnp.pad(v.reshape(1, -1), ((0, 0), (0, D - v.size)))
        return jnp.concatenate([
            gf.reshape(1, D), ng.reshape(1, D), cc.reshape(1, D), pad(rng), pad(df), pad(db),
            bada.reshape(3, D), jnp.pad(rpb.reshape(-1), (0, 4 * D - rpb.size)).reshape(4, D),
            jnp.zeros((3, D), F32)], axis=0)

    wsm = pack(final_norm_g, norm_g, c_ctx, ret_norm_g, ret_decay_fwd, ret_decay_bwd, b_ada, na_rpb)
    msm = pack(m_final_norm_g, m_norm_g, m_c_ctx, m_ret_norm_g, m_ret_decay_fwd, m_ret_decay_bwd, m_b_ada, m_na_rpb)
    vsm = pack(v_final_norm_g, v_norm_g, v_c_ctx, v_ret_norm_g, v_ret_decay_fwd, v_ret_decay_bwd, v_b_ada, v_na_rpb)
    c_t = jnp.concatenate([c_all.reshape(N_DEV * B, D), c_ctx.reshape(1, D), jnp.zeros((7, D), F32)], axis=0).T
    outs = _small_final_call(sm_all, c_t, c_ctx.reshape(1, D), wada_f,
                             w_ada[0], m_w_ada[0], v_w_ada[0], wsm, msm, vsm, B)
    smalls, adas, loss = outs[0:4], outs[4:8], outs[8][0, 0]

    def unpack(p):
        rw = ret_norm_g.shape[1]
        return dict(
            final_norm_g=p[W_GF], norm_g=p[W_NG:W_NG + 1], c_ctx=p[W_CCTX], ret_norm_g=p[W_RNG:W_RNG + 1, :rw],
            ret_decay_fwd=p[W_DF:W_DF + 1, :4], ret_decay_bwd=p[W_DB:W_DB + 1, :4],
            b_ada=p[W_BADA:W_BADA + 3].reshape(1, 3 * D),
            na_rpb=p[W_RPB:W_RPB + 4].reshape(-1)[:na_rpb.size].reshape(na_rpb.shape))

    res = []
    for p, ada, win_o, wout_o in zip(smalls, adas, (g_win, d_win, nm_win, nv_win),
                                     (g_wout, d_wout, nm_wout, nv_wout)):
        u = unpack(p)
        res.append([u["c_ctx"], u["norm_g"], ada[None], u["b_ada"], win_o[None], u["na_rpb"],
                    u["ret_decay_fwd"], u["ret_decay_bwd"], u["ret_norm_g"], wout_o[None], u["final_norm_g"]])
    return (loss, grad_x, *res[0], *res[1], *res[2], *res[3])
```

```python
import numpy as np
import jax
import jax.numpy as jnp
from jax import lax
from jax.experimental import pallas as pl
from jax.experimental.pallas import tpu as pltpu

F32 = jnp.float32
BF16 = jnp.bfloat16
HIGHEST = lax.Precision.HIGHEST

D = 1024
GRID_W = 64
NA_DH = 64
RET_DK = 128
ROPE_BASE = 10000.0
EPS = 1e-6
NEG = -1e30
TQ = 256
KW = 12 * GRID_W
N_SHARD = 4
N_DEV = 8
SM_ROWS = 24

ADAM_LR = 0.001
ADAM_B1 = 0.9
ADAM_B2 = 0.999
ADAM_EPS = 1e-08
ADAM_WD = 0.01
ADAM_STEP = 10

MESH = pl.DeviceIdType.MESH
ANY = pl.BlockSpec(memory_space=pl.ANY)


def _params(sem=None, vmem_mb=48):
    return pltpu.CompilerParams(dimension_semantics=sem, vmem_limit_bytes=vmem_mb << 20)


def _dot(a, b):
    return jnp.dot(a, b, preferred_element_type=F32)


def _dot_nt(a, b):
    return lax.dot_general(a, b, (((1,), (1,)), ((), ())), preferred_element_type=F32)


def _dot_tn(a, b):
    return lax.dot_general(a, b, (((0,), (0,)), ((), ())), preferred_element_type=F32)


def _sigmoid(x):
    return 1.0 / (1.0 + jnp.exp(-x))


def _rope_tables(L, LC):
    half = RET_DK // 2
    nf = half // 2
    t = np.arange(L)
    row = (t // GRID_W).astype(np.float32)
    col = (t % GRID_W).astype(np.float32)
    inv = (np.float32(ROPE_BASE) ** (-np.arange(nf, dtype=np.float32) / np.float32(nf))).astype(np.float32)
    ang = np.concatenate([row[:, None] * inv, col[:, None] * inv], axis=-1).astype(np.float32)
    cos, sin = np.cos(ang).astype(np.float32), np.sin(ang).astype(np.float32)
    cos2 = np.concatenate([cos, cos], axis=-1)
    sin2 = np.concatenate([-sin, sin], axis=-1)
    cos2 = np.concatenate([cos2, np.ones((LC, RET_DK), np.float32)], axis=0)
    sin2 = np.concatenate([sin2, np.zeros((LC, RET_DK), np.float32)], axis=0)
    return jnp.asarray(cos2), jnp.asarray(sin2)


def _mod_call(c8, wada_f, b_ada):
    ws = wada_f.shape[2]

    def body(c_ref, w_ref, b_ref, o_ref):
        a = c_ref[...]
        a = (a * _sigmoid(a)).astype(BF16)
        for s in range(N_SHARD):
            o_ref[:, s * ws:(s + 1) * ws] = _dot(a, w_ref[s]) + b_ref[:, s * ws:(s + 1) * ws]

    return pl.pallas_call(
        body, name="ada_mod", out_shape=jax.ShapeDtypeStruct((8, 3 * D), F32),
        compiler_params=_params())(c8, wada_f, b_ada)


def _dc_masks():
    cq = lax.broadcasted_iota(jnp.int32, (GRID_W, GRID_W), 0)
    ck = lax.broadcasted_iota(jnp.int32, (GRID_W, GRID_W), 1)
    dc = jnp.clip(ck - cq + 15, 0, 30)
    c0 = jnp.clip(cq - 8, 0, GRID_W - 16)
    col_ok = (ck >= c0) & (ck < c0 + 16)
    return dc, col_ok


def _bias_blocks():
    out = []
    for typ, delta in enumerate((4, 0, -4)):
        for rq in range(4):
            for rkk in range(12):
                dr = rkk + delta - rq - 4
                if typ == 0:
                    ok = -rq <= dr <= 7 - rq
                elif typ == 1:
                    ok = -4 <= dr <= 3
                else:
                    ok = -4 - rq <= dr <= 3 - rq
                out.append((typ, rq, rkk, dr if ok else None))
    return out


def _bias_body(r_ref, bias_ref, et_ref):
    dc, col_ok = _dc_masks()
    masks = [(dc == j).astype(F32) for j in range(31)]

    def per_h(h, carry):
        for dr in range(15):
            t = jnp.zeros((GRID_W, GRID_W), F32)
            for j in range(31):
                t = t + masks[j] * r_ref[h, dr * 31 + j]
            et_ref[dr] = jnp.where(col_ok, t, NEG)
        neg = jnp.full((GRID_W, GRID_W), NEG, F32)
        for typ, rq, rkk, dr in _bias_blocks():
            blk = neg if dr is None else et_ref[dr + 7]
            bias_ref[h, typ, rq * 64:(rq + 1) * 64, rkk * 64:(rkk + 1) * 64] = blk
        return carry

    lax.fori_loop(0, bias_ref.shape[0], per_h, 0)


def _bias_tile_sums(db_ref, hh):
    acc = {}
    for typ, rq, rkk, dr in _bias_blocks():
        if dr is None:
            continue
        blk = db_ref[hh, typ, rq * 64:(rq + 1) * 64, rkk * 64:(rkk + 1) * 64]
        acc[dr] = blk if dr not in acc else acc[dr] + blk
    return acc


def _small_reduce_body(dt_ref, dlg_ref, drpb_ref, dlgo_ref, p_ref):
    dc, _ = _dc_masks()
    masks = [(dc == j).astype(F32) for j in range(31)]
    ones = jnp.ones((8, GRID_W), F32)
    p_ref[...] = jnp.zeros_like(p_ref)
    drpb_ref[...] = jnp.zeros_like(drpb_ref)

    def per_h(h, carry):
        for dr in range(-7, 8):
            t = dt_ref[h, dr + 7]
            for j in range(31):
                p_ref[j:j + 1, :] = jnp.sum(t * masks[j], axis=0, keepdims=True)
            red = lax.dot_general(ones, p_ref[...], (((1,), (1,)), ((), ())),
                                  precision=HIGHEST, preferred_element_type=F32)
            drpb_ref[h, dr + 7:dr + 8, :] = red[0:1, :]
        return carry

    lax.fori_loop(0, dt_ref.shape[0], per_h, 0)
    x = dlg_ref[0]
    for b in range(1, dlg_ref.shape[0]):
        x = x + dlg_ref[b]
    x = x.reshape(4 * 8, x.shape[-1])
    dlgo_ref[...] = jnp.dot(x, jnp.ones((x.shape[-1], 128), F32), precision=HIGHEST,
                            preferred_element_type=F32)


def _inproj_gather_call(order, x, ctx, mod, norm_g, win_b, cos2, sin2):
    B, L, _ = x.shape
    LC = ctx.shape[1]
    T = L + LC
    nl, nt = L // TQ, T // TQ
    assert LC == TQ and L % TQ == 0
    kscale = RET_DK ** -0.5
    HR = D // 2

    def body(ord_ref, x_ref, ctx_ref, mod_ref, g_ref, wown_ref, cos_ref, sin_ref, p_ref, h_ref, wf_ref,
             w_all, hs_ref, ssem, rsem, lsem):
        j, b, t = pl.program_id(0), pl.program_id(1), pl.program_id(2)
        first = (b == 0) & (t == 0)
        mx, my, mc = _mesh_pos()
        s = 2 * mx + my
        sib = (mx, my, 1 - mc)
        own = pltpu.make_async_copy(wown_ref, w_all.at[s], lsem.at[0])
        ici_send, ici_recv, fwd_send, fwd_recv, outs = [], [], [], [], [
            pltpu.make_async_copy(w_all.at[s], wf_ref.at[s], lsem.at[1])]
        for k, (px, py) in enumerate(_other_chips(mx, my)):
            ps = 2 * px + py
            mine = w_all.at[s, pl.ds(mc * HR, HR)]
            ici_send.append(_remote(mine, w_all.at[s, pl.ds(mc * HR, HR)], ssem, rsem, k, (px, py, mc)))
            ici_recv.append(_remote(mine, w_all.at[ps, pl.ds(mc * HR, HR)], ssem, rsem, k, (px, py, mc)))
            got = w_all.at[ps, pl.ds(mc * HR, HR)]
            fwd_send.append(_remote(got, got, ssem, rsem, 3 + k, sib))
            theirs = w_all.at[ps, pl.ds((1 - mc) * HR, HR)]
            fwd_recv.append(_remote(theirs, theirs, ssem, rsem, 3 + k, sib))
            outs.append(pltpu.make_async_copy(w_all.at[ps], wf_ref.at[ps], lsem.at[2 + k]))

        @pl.when(first & (j == 0))
        def _():
            own.start()
            own.wait()
            for cp in ici_send:
                cp.start()
            outs[0].start()

        for k in range(3):
            @pl.when(first & (j == k + 1))
            def _(k=k):
                ici_recv[k].wait_recv()
                fwd_send[k].start()
                fwd_recv[k].wait_recv()
                outs[1 + k].start()

        tile = b * nt + t

        @pl.when(j == 0)
        def _():
            is_lat = t < nl
            xt = jnp.where(is_lat, x_ref[...], ctx_ref[...])
            mrow = mod_ref[pl.ds(jnp.where(is_lat, b, B), 1), :]
            shift, scale = mrow[:, 0:D], mrow[:, D:2 * D]
            rstd = lax.rsqrt(jnp.mean(xt * xt, axis=-1, keepdims=True) + EPS)
            h0 = ((xt * rstd * g_ref[...]) * (1.0 + scale) + shift).astype(BF16)
            h_ref[...] = h0
            hs_ref[tile] = h0

        hb = hs_ref[tile]
        cs, sn = cos_ref[...], sin_ref[...]
        shard = ord_ref[j]
        for sh in range(N_SHARD):
            @pl.when(shard == sh)
            def _(sh=sh):
                for half in range(2):
                    sec = 2 * sh + half
                    acc = _dot(hb, w_all[sh, :, half * 512:(half + 1) * 512])
                    if sec == 0:
                        acc = acc * (NA_DH ** -0.5)
                    if sec in (4, 5):
                        for q in range(4):
                            a = acc[:, q * 128:(q + 1) * 128]
                            r = a * cs + pltpu.roll(a, 64, 1) * sn
                            if sec == 5:
                                r = r * kscale
                            p_ref[:, half * 512 + q * 128:half * 512 + (q + 1) * 128] = r.astype(BF16)
                    else:
                        p_ref[:, half * 512:(half + 1) * 512] = acc.astype(BF16)

        @pl.when((j == N_SHARD - 1) & (b == B - 1) & (t == nt - 1))
        def _():
            _finish(outs, ici_send + fwd_send, [])

    tok = lambda j, b, t, o: (jnp.where(j == 0, b, B - 1), jnp.where(j == 0, jnp.minimum(t, nl - 1), nl - 1), 0)
    grid_spec = pltpu.PrefetchScalarGridSpec(
        num_scalar_prefetch=1, grid=(N_SHARD, B, nt),
        in_specs=[
            pl.BlockSpec((None, TQ, D), tok),
            pl.BlockSpec((None, TQ, D), lambda j, b, t, o: (jnp.where(j == 0, b, B - 1), 0, 0)),
            pl.BlockSpec((8, 3 * D), lambda j, b, t, o: (0, 0)),
            pl.BlockSpec((1, D), lambda j, b, t, o: (0, 0)),
            ANY,
            pl.BlockSpec((TQ, RET_DK), lambda j, b, t, o: (t, 0)),
            pl.BlockSpec((TQ, RET_DK), lambda j, b, t, o: (t, 0)),
        ],
        out_specs=(pl.BlockSpec((None, TQ, D), lambda j, b, t, o: (b, t, o[j])),
                   pl.BlockSpec((None, TQ, D), lambda j, b, t, o: (
                       jnp.where(j == 0, b, B - 1), jnp.where(j == 0, t, nt - 1), 0)), ANY),
        scratch_shapes=[pltpu.VMEM((N_SHARD, D, D), BF16), pltpu.VMEM((B * nt, TQ, D), BF16),
                        pltpu.SemaphoreType.DMA((6,)), pltpu.SemaphoreType.DMA((6,)),
                        pltpu.SemaphoreType.DMA((5,))])
    return pl.pallas_call(
        body, name="in_proj", grid_spec=grid_spec,
        out_shape=(jax.ShapeDtypeStruct((B, T, 4 * D), BF16), jax.ShapeDtypeStruct((B, T, D), BF16),
                   jax.ShapeDtypeStruct((N_SHARD, D, D), BF16)),
        compiler_params=_params(("arbitrary",) * 3))(order, x, ctx, mod, norm_g, win_b, cos2, sin2)


def _na_specs(L, T, rows, nh=2):
    nm = rows // 4
    w = nh * NA_DH
    per = 512 // w
    q_spec = pl.BlockSpec((None, TQ, w), lambda hp, b, m: (b, m, hp))
    k_spec = pl.BlockSpec((None, T, w), lambda hp, b, m: (b, 0, per + hp))
    v_spec = pl.BlockSpec((None, T, w), lambda hp, b, m: (b, 0, 2 * per + hp))
    g_spec = pl.BlockSpec((None, TQ, w), lambda hp, b, m: (b, m, 3 * per + hp))
    bias_spec = pl.BlockSpec((nh, 3, TQ, KW), lambda hp, b, m: (hp, 0, 0, 0))
    return nm, q_spec, k_spec, v_spec, g_spec, bias_spec


def _na_tile(m, nm, rows):
    typ = jnp.where(m == 0, 0, jnp.where(m == nm - 1, 2, 1))
    start = pl.multiple_of(jnp.clip(4 * m - 4, 0, rows - 12) * GRID_W, TQ)
    return typ, start


def _na_fwd_call(P, bias, L, LC):
    B, T, _ = P.shape
    rows = L // GRID_W
    NH = 4
    nm, q_spec, k_spec, v_spec, g_spec, bias_spec = _na_specs(L, T, rows, NH)

    def body(q_ref, k_ref, v_ref, g_ref, bias_ref, y_ref, o_ref):
        typ, start = _na_tile(pl.program_id(2), nm, rows)
        for hh in range(NH):
            ln = slice(hh * NA_DH, (hh + 1) * NA_DH)
            q = q_ref[:, ln]
            kw, vw = k_ref[pl.ds(start, KW), ln], v_ref[pl.ds(start, KW), ln]
            kc, vc = k_ref[L:L + LC, ln], v_ref[L:L + LC, ln]
            s1 = _dot_nt(q, kw) + bias_ref[hh, typ]
            s2 = _dot_nt(q, kc)
            mx = jnp.maximum(jnp.max(s1, axis=-1, keepdims=True), jnp.max(s2, axis=-1, keepdims=True))
            p1, p2 = jnp.exp(s1 - mx), jnp.exp(s2 - mx)
            inv = 1.0 / (jnp.sum(p1, axis=-1, keepdims=True) + jnp.sum(p2, axis=-1, keepdims=True))
            o = (_dot(p1.astype(BF16), vw) + _dot(p2.astype(BF16), vc)) * inv
            g = g_ref[:, ln].astype(F32)
            o_ref[:, ln] = o.astype(BF16)
            y_ref[:, ln] = (o * (g * _sigmoid(g))).astype(BF16)

    tile = pl.BlockSpec((None, TQ, NH * NA_DH), lambda hp, b, m: (b, m, hp))
    return pl.pallas_call(
        body, name="na_fwd", grid=(8 // NH, B, nm),
        in_specs=[q_spec, k_spec, v_spec, g_spec, bias_spec],
        out_specs=(tile, tile),
        out_shape=(jax.ShapeDtypeStruct((B, L, 512), BF16),) * 2,
        compiler_params=_params(("arbitrary",) * 3))(P, P, P, P, bias)


def _na_bwd_call(P, bias, dY, o_na, L, LC):
    B, T, _ = P.shape
    rows = L // GRID_W
    NH = 4
    W = NH * NA_DH
    nm, q_spec, k_spec, v_spec, g_spec, bias_spec = _na_specs(L, T, rows, NH)
    scale = NA_DH ** -0.5

    RB = 32

    def body(q_ref, k_ref, v_ref, g_ref, bias_ref, dy_ref, o_ref, dq_ref, dg_ref, dk_ref, dv_ref, dt_ref,
             db_ref, s1_ref, s2_ref, dp1_ref, dp2_ref, p1_ref, p2_ref, ds1_ref, ds2_ref, dkt_ref, dvt_ref):
        b, m = pl.program_id(1), pl.program_id(2)
        typ, start = _na_tile(m, nm, rows)

        @pl.when(m == 0)
        def _():
            dkt_ref[...] = jnp.zeros_like(dkt_ref)
            dvt_ref[...] = jnp.zeros_like(dvt_ref)

        @pl.when((m == 0) & (b == 0))
        def _():
            db_ref[...] = jnp.zeros_like(db_ref)

        for hh in range(NH):
            ln = slice(hh * NA_DH, (hh + 1) * NA_DH)
            q = q_ref[:, ln]
            kw, vw = k_ref[pl.ds(start, KW), ln], v_ref[pl.ds(start, KW), ln]
            kc, vc = k_ref[L:L + LC, ln], v_ref[L:L + LC, ln]
            g = g_ref[:, ln].astype(F32)
            sg = _sigmoid(g)
            dy = dy_ref[:, ln].astype(F32)
            do = (dy * (g * sg)).astype(BF16)
            s1_ref[hh] = _dot_nt(q, kw)
            s2_ref[hh] = _dot_nt(q, kc)
            dp1_ref[hh] = _dot_nt(do, vw)
            dp2_ref[hh] = _dot_nt(do, vc)

            def rows_pass(r, carry, hh=hh):
                rw = pl.ds(pl.multiple_of(r * RB, RB), RB)
                a = s1_ref[hh, rw, :] + bias_ref[hh, typ, rw, :]
                c = s2_ref[hh, rw, :]
                mx = jnp.maximum(jnp.max(a, axis=-1, keepdims=True), jnp.max(c, axis=-1, keepdims=True))
                e1, e2 = jnp.exp(a - mx), jnp.exp(c - mx)
                inv = 1.0 / (jnp.sum(e1, axis=-1, keepdims=True) + jnp.sum(e2, axis=-1, keepdims=True))
                p1, p2 = e1 * inv, e2 * inv
                p1_ref[hh, rw, :] = p1.astype(BF16)
                p2_ref[hh, rw, :] = p2.astype(BF16)
                dp1, dp2 = dp1_ref[hh, rw, :], dp2_ref[hh, rw, :]
                delta = jnp.sum(p1 * dp1, axis=-1, keepdims=True) + jnp.sum(p2 * dp2, axis=-1, keepdims=True)
                ds1 = p1 * (dp1 - delta)
                db_ref[hh, typ, rw, :] += ds1
                ds1_ref[hh, rw, :] = ds1.astype(BF16)
                ds2_ref[hh, rw, :] = (p2 * (dp2 - delta)).astype(BF16)
                return carry

            lax.fori_loop(0, TQ // RB, rows_pass, 0, unroll=True)
            p1b, p2b, ds1b, ds2b = p1_ref[hh], p2_ref[hh], ds1_ref[hh], ds2_ref[hh]
            dg_ref[:, ln] = (dy * o_ref[:, ln].astype(F32) * (sg * (1.0 + g * (1.0 - sg)))).astype(BF16)
            dq_ref[:, ln] = ((_dot(ds1b, kw) + _dot(ds2b, kc)) * scale).astype(BF16)
            dkt_ref[ln, pl.ds(start, KW)] += _dot_tn(q, ds1b)
            dvt_ref[ln, pl.ds(start, KW)] += _dot_tn(do, p1b)
            dkt_ref[ln, L:L + LC] += _dot_tn(q, ds2b)
            dvt_ref[ln, L:L + LC] += _dot_tn(do, p2b)

        @pl.when(m == nm - 1)
        def _():
            dk_ref[...] = dkt_ref[...].T
            dv_ref[...] = dvt_ref[...].T

        @pl.when((m == nm - 1) & (b == B - 1))
        def _():
            for hh in range(NH):
                for dr, t in _bias_tile_sums(db_ref, hh).items():
                    dt_ref[hh, dr + 7] = t

    tile = pl.BlockSpec((None, TQ, W), lambda hp, b, m: (b, m, hp))
    kv_out = pl.BlockSpec((None, T, W), lambda hp, b, m: (b, 0, hp))
    wide, narrow = (NH, TQ, KW), (NH, TQ, LC)
    return pl.pallas_call(
        body, name="na_bwd", grid=(8 // NH, B, nm),
        in_specs=[q_spec, k_spec, v_spec, g_spec, bias_spec, tile, tile],
        out_specs=(tile, tile, kv_out, kv_out,
                   pl.BlockSpec((NH, 15, GRID_W, GRID_W), lambda hp, b, m: (hp, 0, 0, 0))),
        out_shape=(jax.ShapeDtypeStruct((B, L, 512), BF16), jax.ShapeDtypeStruct((B, L, 512), BF16),
                   jax.ShapeDtypeStruct((B, T, 512), F32), jax.ShapeDtypeStruct((B, T, 512), F32),
                   jax.ShapeDtypeStruct((bias.shape[0], 15, GRID_W, GRID_W), F32)),
        scratch_shapes=[pltpu.VMEM((NH,) + bias.shape[1:], F32),
                        pltpu.VMEM(wide, F32), pltpu.VMEM(narrow, F32), pltpu.VMEM(wide, F32), pltpu.VMEM(narrow, F32),
                        pltpu.VMEM(wide, BF16), pltpu.VMEM(narrow, BF16), pltpu.VMEM(wide, BF16),
                        pltpu.VMEM(narrow, BF16), pltpu.VMEM((W, T), F32), pltpu.VMEM((W, T), F32)],
        compiler_params=_params(("arbitrary",) * 3, vmem_mb=60))(P, P, P, P, bias, dY, o_na)


def _head_scalar(dec_ref, h):
    lane = lax.broadcasted_iota(jnp.int32, dec_ref.shape, 1)
    return -jnp.sum(jnp.where(lane == h, jnp.exp(dec_ref[...]), 0.0), axis=1, keepdims=True)


def _ret_specs(T):
    q_spec = pl.BlockSpec((None, TQ, 128), lambda b, h, i: (b, i, 16 + h))
    k_spec = pl.BlockSpec((None, T, 128), lambda b, h, i: (b, 0, 20 + h))
    v_spec = pl.BlockSpec((None, T, 128), lambda b, h, i: (b, 0, 24 + h))
    g_spec = pl.BlockSpec((None, TQ, 128), lambda b, h, i: (b, i, 28 + h))
    dec_spec = pl.BlockSpec((1, 4), lambda b, h, i: (0, 0))
    gn_spec = pl.BlockSpec((1, 128), lambda b, h, i: (0, h))
    return q_spec, k_spec, v_spec, g_spec, dec_spec, gn_spec


def _chunk_decay(lgf, lgb):
    tau = lax.broadcasted_iota(jnp.int32, (TQ, 1), 0).astype(F32)
    sig = lax.broadcasted_iota(jnp.int32, (1, TQ), 1).astype(F32)
    dist = tau - sig
    dm = jnp.exp(dist * jnp.where(dist > 0, lgf, -lgb)) * jnp.where(dist == 0, 2.0, 1.0)
    return tau, dist, dm


def _ret_states_call(P, dec_f, dec_b, L, LC):
    B, T, _ = P.shape
    n = L // TQ

    def body(df_ref, db_ref, k_ref, v_ref, sf_ref, sb_ref):
        h = pl.program_id(1)
        lgf, lgb = _head_scalar(df_ref, h), _head_scalar(db_ref, h)
        tau = lax.broadcasted_iota(jnp.int32, (TQ, 1), 0).astype(F32)
        jc = lax.broadcasted_iota(jnp.int32, (LC, 1), 0).astype(F32)
        wf, wb = jnp.exp(lgf * (TQ - 1.0 - tau)), jnp.exp(lgb * tau)
        gcf, gcb = jnp.exp(lgf * float(TQ)), jnp.exp(lgb * float(TQ))
        kc, vc = k_ref[L:L + LC, :].astype(F32), v_ref[L:L + LC, :]

        def chunk_state(i, w):
            ks = pl.multiple_of(i * TQ, TQ)
            return _dot_tn((k_ref[pl.ds(ks, TQ), :].astype(F32) * w).astype(BF16), v_ref[pl.ds(ks, TQ), :])

        def fwd(i, s):
            sf_ref[i] = s
            return gcf * s + chunk_state(i, wf)

        lax.fori_loop(0, n, fwd, _dot_tn((kc * jnp.exp(lgf * (LC - 1.0 - jc))).astype(BF16), vc), unroll=True)

        def bwd(r, s):
            i = n - 1 - r
            sb_ref[i] = s
            return gcb * s + chunk_state(i, wb)

        lax.fori_loop(0, n, bwd, _dot_tn((kc * jnp.exp(lgb * jc)).astype(BF16), vc), unroll=True)

    st = pl.BlockSpec((None, None, n, RET_DK, RET_DK), lambda b, h: (b, h, 0, 0, 0))
    return pl.pallas_call(
        body, name="ret_states", grid=(B, 4),
        in_specs=[pl.BlockSpec((1, 4), lambda b, h: (0, 0)), pl.BlockSpec((1, 4), lambda b, h: (0, 0)),
                  pl.BlockSpec((None, T, 128), lambda b, h: (b, 0, 20 + h)),
                  pl.BlockSpec((None, T, 128), lambda b, h: (b, 0, 24 + h))],
        out_specs=(st, st),
        out_shape=(jax.ShapeDtypeStruct((B, 4, n, RET_DK, RET_DK), F32),) * 2,
        compiler_params=_params(("arbitrary",) * 2))(dec_f, dec_b, P, P)


def _retc_fwd_call(P, sf, sb, dec_f, dec_b, ret_norm_g, L):
    B, T, _ = P.shape
    sec = lambda k: pl.BlockSpec((None, TQ, 512), lambda b, i: (b, i, k))
    dec_spec = pl.BlockSpec((1, 4), lambda b, i: (0, 0))
    st_spec = pl.BlockSpec((None, 4, None, RET_DK, RET_DK), lambda b, i: (b, 0, i, 0, 0))

    def body(df_ref, db_ref, q_ref, k_ref, v_ref, g_ref, gn_ref, sf_ref, sb_ref, y_ref, o_ref):
        for h in range(4):
            ln = slice(h * RET_DK, (h + 1) * RET_DK)
            lgf, lgb = _head_scalar(df_ref, h), _head_scalar(db_ref, h)
            tau, _, dm = _chunk_decay(lgf, lgb)
            q = q_ref[:, ln]
            qf = q.astype(F32)
            acc = _dot((_dot_nt(q, k_ref[:, ln]) * dm).astype(BF16), v_ref[:, ln])
            acc = acc + _dot((qf * jnp.exp(lgf * (tau + 1.0))).astype(BF16), sf_ref[h].astype(BF16))
            acc = acc + _dot((qf * jnp.exp(lgb * (TQ - tau))).astype(BF16), sb_ref[h].astype(BF16))
            o_ref[:, ln] = acc
            rn = lax.rsqrt(jnp.mean(acc * acc, axis=-1, keepdims=True) + EPS)
            g = g_ref[:, ln].astype(F32)
            y_ref[:, ln] = ((acc * rn * gn_ref[:, ln]) * (g * _sigmoid(g))).astype(BF16)

    tile = pl.BlockSpec((None, TQ, 512), lambda b, i: (b, i, 0))
    return pl.pallas_call(
        body, name="ret_fwd", grid=(B, L // TQ),
        in_specs=[dec_spec, dec_spec, sec(4), sec(5), sec(6), sec(7),
                  pl.BlockSpec((1, 512), lambda b, i: (0, 0)), st_spec, st_spec],
        out_specs=(tile, tile),
        out_shape=(jax.ShapeDtypeStruct((B, L, 512), BF16), jax.ShapeDtypeStruct((B, L, 512), F32)),
        compiler_params=_params(("arbitrary",) * 2))(dec_f, dec_b, P, P, P, P, ret_norm_g, sf, sb)


def _retc_bwd_call(P, sf, sb, dec_f, dec_b, ret_norm_g, o_ret, dY, cos2, sin2, L, LC):
    B, T, _ = P.shape
    n = L // TQ
    C = float(TQ)
    kscale = RET_DK ** -0.5
    st_spec = pl.BlockSpec((None, 4, n, RET_DK, RET_DK), lambda b, i: (b, 0, 0, 0, 0))

    def body(df_ref, db_ref, q_ref, k_ref, v_ref, g_ref, gn_ref, o_ref, dy_ref, cos_ref, sin_ref, sf_ref, sb_ref,
             dq_ref, dg_ref, dk_ref, dv_ref, dgn_ref, dlg_ref, dsf_ref, dsb_ref):
        i = pl.program_id(1)

        @pl.when(i == 0)
        def _():
            dk_ref[...] = jnp.zeros_like(dk_ref)
            dv_ref[...] = jnp.zeros_like(dv_ref)
            dgn_ref[...] = jnp.zeros_like(dgn_ref)
            dlg_ref[...] = jnp.zeros_like(dlg_ref)

        rows = pl.ds(pl.multiple_of(i * TQ, TQ), TQ)
        cs, sn = cos_ref[rows, :], sin_ref[rows, :]

        def one_head(h):
            ln = slice(h * RET_DK, (h + 1) * RET_DK)
            lgf, lgb = _head_scalar(df_ref, h), _head_scalar(db_ref, h)
            tau, dist, dm = _chunk_decay(lgf, lgb)

            def add_lg(row, x):
                csum = jnp.sum(x, axis=0, keepdims=True)
                tot = csum[:, 0:128]
                for part in range(1, x.shape[1] // 128):
                    tot = tot + csum[:, part * 128:(part + 1) * 128]
                dlg_ref[h, row:row + 1, :] += tot

            q = q_ref[:, ln]
            qf = q.astype(F32)
            o = o_ref[:, ln]
            g = g_ref[:, ln].astype(F32)
            dy = dy_ref[:, ln].astype(F32)
            gn = gn_ref[:, ln]
            sg = _sigmoid(g)
            rn = lax.rsqrt(jnp.mean(o * o, axis=-1, keepdims=True) + EPS)
            nrm = o * rn
            dg_ref[:, ln] = (dy * (nrm * gn) * (sg * (1.0 + g * (1.0 - sg)))).astype(BF16)
            dhn = dy * (g * sg)
            dgn_ref[:, ln] += jnp.sum(dhn * nrm, axis=0, keepdims=True)
            dnrm = dhn * gn
            do = rn * (dnrm - nrm * jnp.mean(dnrm * nrm, axis=-1, keepdims=True))
            dob = do.astype(BF16)
            ki, vi = k_ref[rows, ln], v_ref[rows, ln]
            s = _dot_nt(q, ki)
            dsv = _dot_nt(dob, vi)
            dsb = (dsv * dm).astype(BF16)
            dk_ref[rows, ln] += _dot_tn(dsb, q)
            dv_ref[rows, ln] += _dot_tn((s * dm).astype(BF16), dob)
            xw = s * dsv * dm * jnp.abs(dist)
            fpart = jnp.where(dist > 0, xw, 0.0)
            add_lg(0, fpart)
            add_lg(1, xw - fpart)
            dq = _dot(dsb, ki)
            af, ab = jnp.exp(lgf * (tau + 1.0)), jnp.exp(lgb * (C - tau))
            qa, qb = (qf * af).astype(BF16), (qf * ab).astype(BF16)
            sfi, sbi = sf_ref[h, i].astype(BF16), sb_ref[h, i].astype(BF16)
            dq = dq + af * _dot_nt(dob, sfi) + ab * _dot_nt(dob, sbi)
            dsf_ref[h, i] = _dot_tn(qa, dob)
            dsb_ref[h, i] = _dot_tn(qb, dob)
            add_lg(0, (tau + 1.0) * (_dot(qa, sfi) * do))
            add_lg(1, (C - tau) * (_dot(qb, sbi) * do))
            dq_ref[:, ln] = (dq * cs - pltpu.roll(dq, 64, 1) * sn).astype(BF16)

            @pl.when(i == n - 1)
            def _():
                jc = lax.broadcasted_iota(jnp.int32, (LC, 1), 0).astype(F32)
                crow = pl.ds(L, LC)

                def through_state(rws, w, dw, gst, row):
                    kk, vv = k_ref[rws, ln].astype(F32), v_ref[rws, ln]
                    gb = gst.astype(BF16)
                    vg = _dot_nt(vv, gb)
                    kw = kk * w
                    dk_ref[rws, ln] += w * vg
                    dv_ref[rws, ln] += _dot(kw.astype(BF16), gb)
                    add_lg(row, dw * (kw * vg))

                def scan(gc, w, dw, st_ref, dst_ref, order, row):
                    def step(r, gst):
                        j = order(r)
                        through_state(pl.ds(pl.multiple_of(j * TQ, TQ), TQ), w, dw, gst, row)
                        add_lg(row, (C * gc) * (gst * st_ref[h, j]))
                        return dst_ref[h, j] + gc * gst
                    return lax.fori_loop(0, n, step, jnp.zeros((RET_DK, RET_DK), F32), unroll=True)

                gcf, gcb = jnp.exp(lgf * C), jnp.exp(lgb * C)
                g0 = scan(gcf, jnp.exp(lgf * (C - 1.0 - tau)), C - 1.0 - tau, sf_ref, dsf_ref,
                          lambda r: n - 1 - r, 0)
                through_state(crow, jnp.exp(lgf * (LC - 1.0 - jc)), LC - 1.0 - jc, g0, 0)
                g1 = scan(gcb, jnp.exp(lgb * tau), tau, sb_ref, dsb_ref, lambda r: r, 1)
                through_state(crow, jnp.exp(lgb * jc), jc, g1, 1)
                dk = dk_ref[:, ln]
                dk_ref[:, ln] = (dk * cos_ref[...] - pltpu.roll(dk, 64, 1) * sin_ref[...]) * kscale

        for h in range(4):
            one_head(h)

    sec = lambda k: pl.BlockSpec((None, TQ, 512), lambda b, i: (b, i, k))
    full = lambda k: pl.BlockSpec((None, T, 512), lambda b, i: (b, 0, k))
    dec_spec = pl.BlockSpec((1, 4), lambda b, i: (0, 0))
    tab = pl.BlockSpec((T, RET_DK), lambda b, i: (0, 0))
    return pl.pallas_call(
        body, name="ret_bwd", grid=(B, n),
        in_specs=[dec_spec, dec_spec, sec(4), full(5), full(6), sec(7),
                  pl.BlockSpec((1, 512), lambda b, i: (0, 0)), sec(0), sec(1), tab, tab, st_spec, st_spec],
        out_specs=(sec(0), sec(0), full(0), full(0),
                   pl.BlockSpec((None, 1, 512), lambda b, i: (b, 0, 0)),
                   pl.BlockSpec((None, 4, 8, 128), lambda b, i: (b, 0, 0, 0))),
        out_shape=(jax.ShapeDtypeStruct((B, L, 512), BF16), jax.ShapeDtypeStruct((B, L, 512), BF16),
                   jax.ShapeDtypeStruct((B, T, 512), F32), jax.ShapeDtypeStruct((B, T, 512), F32),
                   jax.ShapeDtypeStruct((B, 1, 512), F32), jax.ShapeDtypeStruct((B, 4, 8, 128), F32)),
        scratch_shapes=[pltpu.VMEM((4, n, RET_DK, RET_DK), F32), pltpu.VMEM((4, n, RET_DK, RET_DK), F32)],
        compiler_params=_params(("arbitrary",) * 2, vmem_mb=56))(
            dec_f, dec_b, P, P, P, P, ret_norm_g, o_ret, dY, cos2, sin2, sf, sb)


def _out_call(y_na, y_ret, x, target, mod, final_g, wout_f):
    B, L, _ = x.shape

    def body(yn_ref, yr_ref, x_ref, t_ref, mod_ref, gf_ref, w_ref, dy_ref, dx2_ref, dw_ref, sm_ref):
        b, i = pl.program_id(0), pl.program_id(1)

        @pl.when((b == 0) & (i == 0))
        def _():
            dw_ref[...] = jnp.zeros_like(dw_ref)
            sm_ref[...] = jnp.zeros_like(sm_ref)

        gate = mod_ref[pl.ds(b, 1), 2 * D:3 * D]
        gf = gf_ref[...]
        yn, yr = yn_ref[...], yr_ref[...]
        ylat = _dot(yn, w_ref[0:512, :]) + _dot(yr, w_ref[512:1024, :])
        x2 = x_ref[...] + gate * ylat
        r = lax.rsqrt(jnp.mean(x2 * x2, axis=-1, keepdims=True) + EPS)
        xr = x2 * r
        err = xr * gf - t_ref[...]
        sm_ref[1:2, :] += jnp.sum(err * err, axis=0, keepdims=True)
        dout = err * (1.0 / D)
        sm_ref[0:1, :] += jnp.sum(dout * xr, axis=0, keepdims=True)
        gd = dout * gf
        dx2 = r * (gd - xr * jnp.mean(gd * xr, axis=-1, keepdims=True))
        dx2_ref[...] = dx2
        sm_ref[pl.ds(2 + b, 1), :] += jnp.sum(dx2 * ylat, axis=0, keepdims=True)
        dyl = (gate * dx2).astype(BF16)
        dy_ref[:, 0:512] = _dot_nt(dyl, w_ref[0:512, :]).astype(BF16)
        dy_ref[:, 512:1024] = _dot_nt(dyl, w_ref[512:1024, :]).astype(BF16)
        dw_ref[0:512, :] += _dot_tn(yn, dyl)
        dw_ref[512:1024, :] += _dot_tn(yr, dyl)

    half = pl.BlockSpec((None, TQ, 512), lambda b, i: (b, i, 0))
    full = pl.BlockSpec((None, TQ, D), lambda b, i: (b, i, 0))
    return pl.pallas_call(
        body, name="out_proj_loss", grid=(B, L // TQ),
        in_specs=[half, half, full, full,
                  pl.BlockSpec((8, 3 * D), lambda b, i: (0, 0)),
                  pl.BlockSpec((1, D), lambda b, i: (0, 0)),
                  pl.BlockSpec((D, D), lambda b, i: (0, 0))],
        out_specs=(full, full, pl.BlockSpec((D, D), lambda b, i: (0, 0)),
                   pl.BlockSpec((8, D), lambda b, i: (0, 0))),
        out_shape=(jax.ShapeDtypeStruct((B, L, D), BF16), jax.ShapeDtypeStruct((B, L, D), F32),
                   jax.ShapeDtypeStruct((D, D), F32), jax.ShapeDtypeStruct((8, D), F32)),
        compiler_params=_params(("arbitrary",) * 2))(y_na, y_ret, x, target, mod, final_g, wout_f)


def _dh_call(dsec, win_f, x, ctx, dx2, mod, norm_g, cp_in, cp_out):
    B, L, _ = x.shape
    LC = ctx.shape[1]
    nl = L // TQ

    def body(d0, d1, d2, d3, d4, d5, d6, d7, w_ref, x_ref, ctx_ref, dx2_ref, mod_ref, g_ref, cpi_ref, cpo_ref,
             gx_ref, sm_ref, sli_ref, slo_ref, ssem, rsem, lsem):
        drefs = (d0, d1, d2, d3, d4, d5, d6, d7)
        b, t = pl.program_id(0), pl.program_id(1)
        is_lat = t < nl

        @pl.when((b == 0) & (t == 0))
        def _():
            sm_ref[...] = jnp.zeros_like(sm_ref)

        def dh_of(secs):
            acc = jnp.zeros((TQ, D), F32)
            for sec in secs:
                s, half = divmod(sec, 2)
                acc = acc + _dot_nt(drefs[sec][...].astype(BF16), w_ref[s, :, half * 512:(half + 1) * 512])
            return acc

        def norm_bwd(dh, xt, mrow):
            scale = mrow[:, D:2 * D]
            g = g_ref[...]
            rstd = lax.rsqrt(jnp.mean(xt * xt, axis=-1, keepdims=True) + EPS)
            xn = xt * rstd
            dshift = jnp.sum(dh, axis=0, keepdims=True)
            dscale = jnp.sum(dh * (xn * g), axis=0, keepdims=True)
            dhn = dh * (1.0 + scale)
            sm_ref[0:1, :] += jnp.sum(dhn * xn, axis=0, keepdims=True)
            dxn = dhn * g
            dx = rstd * (dxn - xn * jnp.mean(dxn * xn, axis=-1, keepdims=True))
            return dshift, dscale, dx

        @pl.when(is_lat)
        def _():
            dshift, dscale, dx = norm_bwd(dh_of(range(8)), x_ref[...], mod_ref[pl.ds(b, 1), :])
            sm_ref[pl.ds(3 + b, 1), :] += dshift
            sm_ref[pl.ds(3 + B + b, 1), :] += dscale
            gx_ref[...] = dx2_ref[...] + dx

        @pl.when(jnp.logical_not(is_lat))
        def _():
            dshift, dscale, _ = norm_bwd(dh_of((1, 2, 5, 6)), ctx_ref[...], mod_ref[B:B + 1, :])
            sm_ref[1:2, :] += dshift
            sm_ref[2:3, :] += dscale

        mx, my, mc = _mesh_pos()
        s = 2 * mx + my
        cps, sls = (cpi_ref, cpo_ref), (sli_ref, slo_ref)
        own = [pltpu.make_async_copy(cps[a].at[s], sls[a].at[s], lsem.at[a]) for a in range(2)]
        sends, recvs, k = [], [], 0
        for px, py in _other_chips(mx, my):
            ps = 2 * px + py
            for a in range(2):
                sends.append(_remote(cps[a].at[ps], sls[a].at[s], ssem, rsem, k, (px, py, mc)))
                recvs.append(_remote(cps[a].at[s], sls[a].at[ps], ssem, rsem, k, (px, py, mc)))
                k += 1

        @pl.when((b == 0) & (t == 0))
        def _():
            for cp in own + sends:
                cp.start()

        @pl.when((b == B - 1) & (t == nl))
        def _():
            _finish(own, sends, recvs)

    lat = lambda b, t: (b, jnp.minimum(t, nl - 1), 0)
    tok = lambda b, t: (b, t, 0)
    sec_specs = [pl.BlockSpec((None, TQ, 512), lat if sec in (0, 3, 4, 7) else tok) for sec in range(8)]
    return pl.pallas_call(
        body, name="dh_norm_bwd", grid=(B, nl + 1),
        in_specs=sec_specs + [
            pl.BlockSpec((N_SHARD, D, D), lambda b, t: (0, 0, 0)),
            pl.BlockSpec((None, TQ, D), lat),
            pl.BlockSpec((None, LC, D), lambda b, t: (b, 0, 0)),
            pl.BlockSpec((None, TQ, D), lat),
            pl.BlockSpec((8, 3 * D), lambda b, t: (0, 0)),
            pl.BlockSpec((1, D), lambda b, t: (0, 0)), ANY, ANY],
        out_specs=(pl.BlockSpec((None, TQ, D), lat), pl.BlockSpec((8, D), lambda b, t: (0, 0)), ANY, ANY),
        out_shape=(jax.ShapeDtypeStruct((B, L, D), F32), jax.ShapeDtypeStruct((8, D), F32),
                   jax.ShapeDtypeStruct(cp_in.shape, cp_in.dtype), jax.ShapeDtypeStruct(cp_out.shape, cp_out.dtype)),
        scratch_shapes=[pltpu.SemaphoreType.DMA((6,)), pltpu.SemaphoreType.DMA((6,)),
                        pltpu.SemaphoreType.DMA((2,))],
        compiler_params=_params(("arbitrary",) * 2))(*dsec, win_f, x, ctx, dx2, mod, norm_g, cp_in, cp_out)


def _dw_call(dsec, h, L):
    B, T, _ = h.shape
    nl = L // TQ

    def body(d0, d1, d2, d3, d4, d5, d6, d7, h_ref, dw_ref, acc_ref):
        drefs = (d0, d1, d2, d3, d4, d5, d6, d7)
        b, t = pl.program_id(0), pl.program_id(1)

        @pl.when((b == 0) & (t == 0))
        def _():
            acc_ref[...] = jnp.zeros_like(acc_ref)

        hb = h_ref[...]

        def add(secs):
            for sec in secs:
                s, half = divmod(sec, 2)
                acc_ref[s, :, half * 512:(half + 1) * 512] += _dot_tn(hb, drefs[sec][...].astype(BF16))

        @pl.when(t < nl)
        def _():
            add(range(8))

        @pl.when(t >= nl)
        def _():
            add((1, 2, 5, 6))

        @pl.when((b == B - 1) & (t == nl))
        def _():
            dw_ref[...] = acc_ref[...].astype(BF16)

    lat = lambda b, t: (b, jnp.minimum(t, nl - 1), 0)
    tok = lambda b, t: (b, t, 0)
    sec_specs = [pl.BlockSpec((None, TQ, 512), lat if sec in (0, 3, 4, 7) else tok) for sec in range(8)]
    return pl.pallas_call(
        body, name="dw_in", grid=(B, nl + 1),
        in_specs=sec_specs + [pl.BlockSpec((None, TQ, D), tok)],
        out_specs=pl.BlockSpec((N_SHARD, D, D), lambda b, t: (0, 0, 0)),
        out_shape=jax.ShapeDtypeStruct((N_SHARD, D, D), BF16),
        scratch_shapes=[pltpu.VMEM((N_SHARD, D, D), F32)],
        compiler_params=_params(("arbitrary",) * 2, vmem_mb=56))(*dsec, h)


def _mesh_pos():
    return lax.axis_index("x"), lax.axis_index("y"), lax.axis_index("c")


def _flip(v, f):
    return 1 - v if f else v


def _remote(src, dst, ssem, rsem, k, peer):
    return pltpu.make_async_remote_copy(src_ref=src, dst_ref=dst, send_sem=ssem.at[k], recv_sem=rsem.at[k],
                                        device_id=peer, device_id_type=MESH)


def _other_chips(x, y):
    return [(_flip(x, fx), _flip(y, fy)) for fx, fy in ((1, 0), (0, 1), (1, 1))]


def _all_to_all_small(src, dst_all, ssem, rsem, k0, x, y, cc):
    me = 4 * x + 2 * y + cc
    sends, recvs = [], []
    for f in range(1, N_DEV):
        px, py, pc = _flip(x, f & 4), _flip(y, f & 2), _flip(cc, f & 1)
        sends.append(_remote(src, dst_all.at[me], ssem, rsem, k0 + f - 1, (px, py, pc)))
        recvs.append(_remote(src, dst_all.at[4 * px + 2 * py + pc], ssem, rsem, k0 + f - 1, (px, py, pc)))
    return sends, recvs


def _finish(local, sends, recvs):
    for cp in recvs:
        cp.wait_recv()
    for cp in sends:
        cp.wait_send()
    for cp in local:
        cp.wait()


def _gather_call(wout_b, wada_b, c, rpb_flat):
    arrs = (wout_b, wada_b)
    na = len(arrs)
    hrs = [a.shape[0] // 2 for a in arrs]

    def body(wout, wada, c_ref, r_ref, wout_f, wada_f, c_all, bias_ref, et_ref, ssem, rsem, lsem):
        x, y, cc = _mesh_pos()
        s, me = 2 * x + y, 4 * x + 2 * y + cc
        sib = (x, y, 1 - cc)
        srcs, dsts = (wout, wada), (wout_f, wada_f)

        def half(a, shard, hc):
            return dsts[a].at[shard, pl.ds(hc * hrs[a], hrs[a])]

        local = [pltpu.make_async_copy(srcs[a], dsts[a].at[s], lsem.at[a]) for a in range(na)]
        local.append(pltpu.make_async_copy(c_ref, c_all.at[me], lsem.at[na]))
        ici_send, ici_recv, fwd_send, fwd_recv, k = [], [], [], [], 0
        for px, py in _other_chips(x, y):
            ps = 2 * px + py
            for a in range(na):
                mine = srcs[a].at[pl.ds(cc * hrs[a], hrs[a])]
                ici_send.append(_remote(mine, half(a, s, cc), ssem, rsem, k, (px, py, cc)))
                ici_recv.append(_remote(mine, half(a, ps, cc), ssem, rsem, k, (px, py, cc)))
                fwd_send.append(_remote(half(a, ps, cc), half(a, ps, cc), ssem, rsem, 3 * na + k, sib))
                fwd_recv.append(_remote(half(a, ps, 1 - cc), half(a, ps, 1 - cc), ssem, rsem, 3 * na + k, sib))
                k += 1
        c_send, c_recv = _all_to_all_small(c_ref, c_all, ssem, rsem, 6 * na, x, y, cc)
        for cp in local + ici_send + c_send:
            cp.start()
        _bias_body(r_ref, bias_ref, et_ref)
        for got, fwd in zip(ici_recv, fwd_send):
            got.wait_recv()
            fwd.start()
        _finish(local, ici_send + fwd_send + c_send, fwd_recv + c_recv)

    return pl.pallas_call(
        body, name="weight_gather",
        in_specs=[pl.BlockSpec(memory_space=pltpu.VMEM)] * 3 + [pl.BlockSpec(memory_space=pltpu.SMEM)],
        out_specs=(pl.BlockSpec(memory_space=pltpu.VMEM),) * 4,
        out_shape=tuple(jax.ShapeDtypeStruct((N_SHARD,) + a.shape, a.dtype) for a in arrs)
        + (jax.ShapeDtypeStruct((N_DEV,) + c.shape, c.dtype),
           jax.ShapeDtypeStruct((rpb_flat.shape[0], 3, TQ, KW), F32)),
        scratch_shapes=[pltpu.VMEM((15, GRID_W, GRID_W), F32),
                        pltpu.SemaphoreType.DMA((6 * na + 7,)), pltpu.SemaphoreType.DMA((6 * na + 7,)),
                        pltpu.SemaphoreType.DMA((na + 1,))],
        compiler_params=pltpu.CompilerParams(vmem_limit_bytes=56 << 20))(wout_b, wada_b, c, rpb_flat)


VROWS = 32


def _grad_halves_call(dwin_b, dwout_b, dbias, dlg):
    arrs = (dwin_b, dwout_b)
    hrs = [a.shape[1] // 2 for a in arrs]

    def body(din, dout, db_ref, dlg_ref, cp_in, cp_out, drpb_ref, dlgo_ref, got_in, got_out, p_ref, ssem, rsem):
        x, y, cc = _mesh_pos()
        sib = (x, y, 1 - cc)
        srcs, gots, cps = (din, dout), (got_in, got_out), (cp_in, cp_out)
        halves = [_remote(srcs[a].at[:, pl.ds((1 - cc) * hrs[a], hrs[a])], gots[a], ssem, rsem, a, sib)
                  for a in range(2)]
        for cp in halves:
            cp.start()
        _small_reduce_body(db_ref, dlg_ref, drpb_ref, dlgo_ref, p_ref)
        for cp in halves:
            cp.wait_recv()
        for a in range(2):
            for j in range(N_SHARD):
                def add(i, carry, a=a, j=j):
                    r = pl.multiple_of(i * VROWS, VROWS)
                    mine = srcs[a][j, pl.ds(pl.multiple_of(cc * hrs[a] + r, VROWS), VROWS), :].astype(F32)
                    cps[a][j, pl.ds(r, VROWS), :] = (
                        mine + gots[a][j, pl.ds(r, VROWS), :].astype(F32)).astype(BF16)
                    return carry
                lax.fori_loop(0, hrs[a] // VROWS, add, 0)
        for cp in halves:
            cp.wait_send()

    vmem = pl.BlockSpec(memory_space=pltpu.VMEM)
    half_shapes = [(N_SHARD, hrs[a], arrs[a].shape[2]) for a in range(2)]
    return pl.pallas_call(
        body, name="grad_halves",
        in_specs=[vmem] * 4, out_specs=(vmem,) * 4,
        out_shape=(jax.ShapeDtypeStruct(half_shapes[0], BF16), jax.ShapeDtypeStruct(half_shapes[1], BF16),
                   jax.ShapeDtypeStruct((dbias.shape[0], 16, 32), F32), jax.ShapeDtypeStruct((32, 128), F32)),
        scratch_shapes=[pltpu.VMEM(half_shapes[0], BF16), pltpu.VMEM(half_shapes[1], BF16),
                        pltpu.VMEM((32, GRID_W), F32),
                        pltpu.SemaphoreType.DMA((2,)), pltpu.SemaphoreType.DMA((2,))],
        compiler_params=pltpu.CompilerParams(vmem_limit_bytes=56 << 20))(dwin_b, dwout_b, dbias, dlg)


def _grad_finish_call(sl_in, sl_out, small):
    arrs = (sl_in, sl_out)

    def body(sin, sout, sm, gin, gout, sm_all, h_in, h_out, ssem, rsem, lsem):
        x, y, cc = _mesh_pos()
        me = 4 * x + 2 * y + cc
        sib = (x, y, 1 - cc)
        sls, hs, gs = (sin, sout), (h_in, h_out), (gin, gout)
        sm_send, sm_recv = _all_to_all_small(sm, sm_all, ssem, rsem, 2, x, y, cc)
        sm_own = pltpu.make_async_copy(sm, sm_all.at[me], lsem.at[0])
        for cp in sm_send + [sm_own]:
            cp.start()
        for a in range(2):
            def total(i, carry, a=a):
                rows = pl.ds(pl.multiple_of(i * VROWS, VROWS), VROWS)
                sl = sls[a]
                hs[a][rows, :] = ((sl[0, rows, :].astype(F32) + sl[1, rows, :].astype(F32))
                                  + sl[2, rows, :].astype(F32)) + sl[3, rows, :].astype(F32)
                return carry
            lax.fori_loop(0, arrs[a].shape[1] // VROWS, total, 0)
        mine = [pltpu.make_async_copy(hs[a], gs[a].at[cc], lsem.at[1 + a]) for a in range(2)]
        back = [_remote(hs[a], gs[a].at[cc], ssem, rsem, a, sib) for a in range(2)]
        back_recv = [_remote(hs[a], gs[a].at[1 - cc], ssem, rsem, a, sib) for a in range(2)]
        for cp in mine + back:
            cp.start()
        _finish(mine + [sm_own], back + sm_send, back_recv + sm_recv)

    vmem = pl.BlockSpec(memory_space=pltpu.VMEM)
    return pl.pallas_call(
        body, name="grad_finish",
        in_specs=[vmem] * 3, out_specs=(vmem,) * 3,
        out_shape=(jax.ShapeDtypeStruct((2,) + sl_in.shape[1:], F32),
                   jax.ShapeDtypeStruct((2,) + sl_out.shape[1:], F32),
                   jax.ShapeDtypeStruct((N_DEV,) + small.shape, F32)),
        scratch_shapes=[pltpu.VMEM(sl_in.shape[1:], F32), pltpu.VMEM(sl_out.shape[1:], F32),
                        pltpu.SemaphoreType.DMA((9,)), pltpu.SemaphoreType.DMA((9,)),
                        pltpu.SemaphoreType.DMA((3,))],
        compiler_params=pltpu.CompilerParams(vmem_limit_bytes=48 << 20))(sl_in, sl_out, small)


def _adamw(w, g, m, v):
    m = ADAM_B1 * m + (1.0 - ADAM_B1) * g
    v = ADAM_B2 * v + (1.0 - ADAM_B2) * (g * g)
    m_hat = m / (1.0 - ADAM_B1 ** ADAM_STEP)
    v_hat = v / (1.0 - ADAM_B2 ** ADAM_STEP)
    return -ADAM_LR * (m_hat / (jnp.sqrt(v_hat) + ADAM_EPS) + ADAM_WD * w), m, v


def _adam_call(w, m, v, g, name):
    R, C = w.shape
    tr = 256

    def body(w_ref, m_ref, v_ref, g_ref, d_ref, mo_ref, vo_ref):
        d_ref[...], mo_ref[...], vo_ref[...] = _adamw(w_ref[...], g_ref[...], m_ref[...], v_ref[...])

    spec = pl.BlockSpec((tr, C), lambda i: (i, 0))
    return pl.pallas_call(
        body, name=name, grid=(R // tr,), in_specs=[spec] * 4,
        out_specs=(spec,) * 3, out_shape=(jax.ShapeDtypeStruct((R, C), F32),) * 3,
        compiler_params=_params(("arbitrary",)))(w, m, v, g)


R_GF, R_NG, R_LOSS, R_RNG, R_LGF, R_LGB, R_SHIFT, R_SCALE, R_GATE, R_SHIFT_C, R_SCALE_C, R_RNG2, R_RPB = (
    0, 1, 2, 3, 4, 5, 6, 8, 10, 12, 13, 14, 16)
W_GF, W_NG, W_CCTX, W_RNG, W_DF, W_DB, W_BADA, W_RPB = 0, 1, 2, 3, 4, 5, 6, 9


def _small_final_call(sm_all, c_t, c_ctx, wada_f, wada, m_ada, v_ada, wsm, msm, vsm, B):
    ws = wada.shape[1]
    NB = N_DEV * B

    def body(sm_ref, ct_ref, cctx_ref, wf_ref, wa_ref, ma_ref, va_ref, w_ref, m_ref, v_ref,
             g_ref, d_ref, mo_ref, vo_ref, ga_ref, da_ref, mao_ref, vao_ref, loss_ref, dmod_ref):
        x, y, _ = _mesh_pos()
        s = 2 * x + y
        tot = sm_ref[0]
        for dv in range(1, N_DEV):
            tot = tot + sm_ref[dv]
        w = w_ref[...]
        for dv in range(N_DEV):
            for b in range(B):
                r = dv * B + b
                for part, row in enumerate((R_SHIFT, R_SCALE, R_GATE)):
                    dmod_ref[r:r + 1, part * D:(part + 1) * D] = sm_ref[dv, row + b:row + b + 1, :]
        dmod_ref[NB:NB + 1, 0:D] = tot[R_SHIFT_C:R_SHIFT_C + 1, :]
        dmod_ref[NB:NB + 1, D:2 * D] = tot[R_SCALE_C:R_SCALE_C + 1, :]
        dmod_ref[NB:NB + 1, 2 * D:3 * D] = jnp.zeros((1, D), F32)
        dmod_ref[NB + 1:, :] = jnp.zeros((dmod_ref.shape[0] - NB - 1, 3 * D), F32)
        dmod = dmod_ref[...]
        cc = cctx_ref[...]
        scc = _sigmoid(cc)
        ct = ct_ref[...]
        act_t = ct * _sigmoid(ct)
        dmc = dmod[NB:NB + 1, :].astype(BF16)
        dact = jnp.zeros((1, D), F32)
        for sh in range(N_SHARD):
            dact = dact + _dot_nt(dmc[:, sh * ws:(sh + 1) * ws], wf_ref[sh])
        g = jnp.zeros((16, D), F32)
        rows = lax.broadcasted_iota(jnp.int32, (16, D), 0)

        def put(g, row, val):
            return jnp.where(rows == row, val, g)

        g = put(g, W_GF, tot[R_GF:R_GF + 1, :])
        g = put(g, W_NG, tot[R_NG:R_NG + 1, :])
        g = put(g, W_CCTX, dact * (scc * (1.0 + cc * (1.0 - scc))))
        g = put(g, W_RNG, tot[R_RNG:R_RNG + 1, :] + tot[R_RNG2:R_RNG2 + 1, :])
        g = put(g, W_DF, tot[R_LGF:R_LGF + 1, :] * (-jnp.exp(w[W_DF:W_DF + 1, :])))
        g = put(g, W_DB, tot[R_LGB:R_LGB + 1, :] * (-jnp.exp(w[W_DB:W_DB + 1, :])))
        db = jnp.sum(dmod, axis=0, keepdims=True)
        for part in range(3):
            g = put(g, W_BADA + part, db[:, part * D:(part + 1) * D])
        for part in range(4):
            g = put(g, W_RPB + part, tot[R_RPB + part:R_RPB + part + 1, :])
        g_ref[...] = g
        d_ref[...], mo_ref[...], vo_ref[...] = _adamw(w, g, m_ref[...], v_ref[...])
        loss_ref[...] = jnp.broadcast_to(
            (0.5 / D) * jnp.sum(tot[R_LOSS:R_LOSS + 1, :], axis=1, keepdims=True), (8, 128))
        for sh in range(N_SHARD):
            @pl.when(s == sh)
            def _():
                ga = jnp.dot(act_t, dmod[:, sh * ws:(sh + 1) * ws], precision=HIGHEST,
                             preferred_element_type=F32)
                ga_ref[...] = ga
                da_ref[...], mao_ref[...], vao_ref[...] = _adamw(wa_ref[...], ga, ma_ref[...], va_ref[...])

    sh_small = jax.ShapeDtypeStruct((16, D), F32)
    sh_ada = jax.ShapeDtypeStruct(wada.shape, F32)
    return pl.pallas_call(
        body, name="small_final",
        out_shape=(sh_small,) * 4 + (sh_ada,) * 4 + (jax.ShapeDtypeStruct((8, 128), F32),),
        scratch_shapes=[pltpu.VMEM((NB + 8, 3 * D), F32)],
        compiler_params=_params(vmem_mb=56))(
            sm_all, c_t, c_ctx, wada_f, wada, m_ada, v_ada, wsm, msm, vsm)


def _local_step(order, x, c, ctx, c_ctx, norm_g, wada_f, b_ada, win_b, bias, dec_f, dec_b, ret_norm_g,
                wout_f, final_g, target):
    B, L, _ = x.shape
    LC = ctx.shape[1]
    assert B == 2
    cos2, sin2 = _rope_tables(L, LC)
    c8 = jnp.concatenate([c, c_ctx[None, :], jnp.zeros((8 - B - 1, D), F32)], axis=0)
    mod = _mod_call(c8, wada_f, b_ada)
    P, h, win_f = _inproj_gather_call(order, x, ctx, mod, norm_g, win_b, cos2, sin2)
    y_na, o_na = _na_fwd_call(P, bias, L, LC)
    sf, sb = _ret_states_call(P, dec_f, dec_b, L, LC)
    y_ret, o_ret = _retc_fwd_call(P, sf, sb, dec_f, dec_b, ret_norm_g, L)
    dY, dx2, dwout_p, sm_out = _out_call(y_na, y_ret, x, target, mod, final_g, wout_f.reshape(D, D))
    dnq, dng, dnk, dnv, dbias = _na_bwd_call(P, bias, dY, o_na, L, LC)
    drq, drg, drk, drv, dgn, dlg = _retc_bwd_call(P, sf, sb, dec_f, dec_b, ret_norm_g, o_ret, dY, cos2, sin2, L, LC)
    dsec = (dnq, dnk, dnv, dng, drq, drk, drv, drg)
    dwin_b = _dw_call(dsec, h, L)
    cp_in, cp_out, drpb, dlg_sum = _grad_halves_call(
        dwin_b, dwout_p.astype(BF16).reshape(N_SHARD, D // N_SHARD, D), dbias, dlg)
    grad_x, sm_dh, sl_in, sl_out = _dh_call(dsec, win_f, x, ctx, dx2, mod, norm_g, cp_in, cp_out)
    z = jnp.zeros((1, D), F32)
    pad = lambda v: jnp.pad(v.reshape(1, -1), ((0, 0), (0, D - v.size)))
    dlg_sum = dlg_sum.reshape(4, 8, 128)
    rpb_rows = jnp.pad(drpb[:, :15, :31].reshape(-1), (0, 4 * D - drpb.shape[0] * 465)).reshape(4, D)
    small = jnp.concatenate([
        sm_out[0:1], sm_dh[0:1], sm_out[1:2], pad(dgn[0]), pad(dlg_sum[:, 0, 0]), pad(dlg_sum[:, 1, 0]),
        sm_dh[3:5], sm_dh[5:7], sm_out[2:4], sm_dh[1:2], sm_dh[2:3], pad(dgn[1]), z, rpb_rows,
        jnp.zeros((SM_ROWS - 20, D), F32)], axis=0)
    return grad_x, sl_in, sl_out, small


def kernel(x, c, ctx, c_ctx, norm_g, w_ada, b_ada, w_in, na_rpb, ret_decay_fwd, ret_decay_bwd, ret_norm_g, w_out, final_norm_g, loss_target, m_c_ctx, m_norm_g, m_w_ada, m_b_ada, m_w_in, m_na_rpb, m_ret_decay_fwd, m_ret_decay_bwd, m_ret_norm_g, m_w_out, m_final_norm_g, v_c_ctx, v_norm_g, v_w_ada, v_b_ada, v_w_in, v_na_rpb, v_ret_decay_fwd, v_ret_decay_bwd, v_ret_norm_g, v_w_out, v_final_norm_g):
    B = x.shape[0]
    wout_f, wada_f, c_all, bias = _gather_call(
        w_out[0].astype(BF16), w_ada[0].astype(BF16), c, na_rpb[0].reshape(na_rpb.shape[1], -1))
    mx, my = lax.axis_index("x"), lax.axis_index("y")
    order = jnp.stack([2 * mx + my, 2 * (1 - mx) + my, 2 * mx + (1 - my),
                       2 * (1 - mx) + (1 - my)]).astype(jnp.int32)
    grad_x, sl_in, sl_out, small = _local_step(
        order, x, c, ctx, c_ctx, norm_g, wada_f, b_ada, w_in[0].astype(BF16), bias, ret_decay_fwd,
        ret_decay_bwd, ret_norm_g, wout_f, final_norm_g.reshape(1, D), loss_target)
    gin, gout, sm_all = _grad_finish_call(sl_in, sl_out, small)
    g_win, g_wout = gin.reshape(w_in.shape[1:]), gout.reshape(w_out.shape[1:])
    d_win, nm_win, nv_win = _adam_call(w_in[0], m_w_in[0], v_w_in[0], g_win, "adam_w_in")
    d_wout, nm_wout, nv_wout = _adam_call(w_out[0], m_w_out[0], v_w_out[0], g_wout, "adam_w_out")

    def pack(gf, ng, cc, rng, df, db, bada, rpb):
        pad = lambda v: jnp.pad(v.reshape(1, -1), ((0, 0), (0, D - v.size)))
        return jnp.concatenate([
            gf.reshape(1, D), ng.reshape(1, D), cc.reshape(1, D), pad(rng), pad(df), pad(db),
            bada.reshape(3, D), jnp.pad(rpb.reshape(-1), (0, 4 * D - rpb.size)).reshape(4, D),
            jnp.zeros((3, D), F32)], axis=0)

    wsm = pack(final_norm_g, norm_g, c_ctx, ret_norm_g, ret_decay_fwd, ret_decay_bwd, b_ada, na_rpb)
    msm = pack(m_final_norm_g, m_norm_g, m_c_ctx, m_ret_norm_g, m_ret_decay_fwd, m_ret_decay_bwd, m_b_ada, m_na_rpb)
    vsm = pack(v_final_norm_g, v_norm_g, v_c_ctx, v_ret_norm_g, v_ret_decay_fwd, v_ret_decay_bwd, v_b_ada, v_na_rpb)
    c_t = jnp.concatenate([c_all.reshape(N_DEV * B, D), c_ctx.reshape(1, D), jnp.zeros((7, D), F32)], axis=0).T
    outs = _small_final_call(sm_all, c_t, c_ctx.reshape(1, D), wada_f,
                             w_ada[0], m_w_ada[0], v_w_ada[0], wsm, msm, vsm, B)
    smalls, adas, loss = outs[0:4], outs[4:8], outs[8][0, 0]

    def unpack(p):
        rw = ret_norm_g.shape[1]
        return dict(
            final_norm_g=p[W_GF], norm_g=p[W_NG:W_NG + 1], c_ctx=p[W_CCTX], ret_norm_g=p[W_RNG:W_RNG + 1, :rw],
            ret_decay_fwd=p[W_DF:W_DF + 1, :4], ret_decay_bwd=p[W_DB:W_DB + 1, :4],
            b_ada=p[W_BADA:W_BADA + 3].reshape(1, 3 * D),
            na_rpb=p[W_RPB:W_RPB + 4].reshape(-1)[:na_rpb.size].reshape(na_rpb.shape))

    res = []
    for p, ada, win_o, wout_o in zip(smalls, adas, (g_win, d_win, nm_win, nv_win),
                                     (g_wout, d_wout, nm_wout, nv_wout)):
        u = unpack(p)
        res.append([u["c_ctx"], u["norm_g"], ada[None], u["b_ada"], win_o[None], u["na_rpb"],
                    u["ret_decay_fwd"], u["ret_decay_bwd"], u["ret_norm_g"], wout_o[None], u["final_norm_g"]])
    return (loss, grad_x, *res[0], *res[1], *res[2], *res[3])
```

```python
import numpy as np
import jax
import jax.numpy as jnp
from jax import lax
from jax.experimental import pallas as pl
from jax.experimental.pallas import tpu as pltpu

F32 = jnp.float32
BF16 = jnp.bfloat16
HIGHEST = lax.Precision.HIGHEST

D = 1024
GRID_W = 64
NA_DH = 64
RET_DK = 128
ROPE_BASE = 10000.0
EPS = 1e-6
NEG = -1e30
TQ = 256
KW = 12 * GRID_W
N_SHARD = 4
N_DEV = 8
SM_ROWS = 24

ADAM_LR = 0.001
ADAM_B1 = 0.9
ADAM_B2 = 0.999
ADAM_EPS = 1e-08
ADAM_WD = 0.01
ADAM_STEP = 10

MESH = pl.DeviceIdType.MESH
ANY = pl.BlockSpec(memory_space=pl.ANY)


def _params(sem=None, vmem_mb=48):
    return pltpu.CompilerParams(dimension_semantics=sem, vmem_limit_bytes=vmem_mb << 20)


def _dot(a, b):
    return jnp.dot(a, b, preferred_element_type=F32)


def _dot_nt(a, b):
    return lax.dot_general(a, b, (((1,), (1,)), ((), ())), preferred_element_type=F32)


def _dot_tn(a, b):
    return lax.dot_general(a, b, (((0,), (0,)), ((), ())), preferred_element_type=F32)


def _sigmoid(x):
    return 1.0 / (1.0 + jnp.exp(-x))


def _rope_tables(L, LC):
    half = RET_DK // 2
    nf = half // 2
    t = np.arange(L)
    row = (t // GRID_W).astype(np.float32)
    col = (t % GRID_W).astype(np.float32)
    inv = (np.float32(ROPE_BASE) ** (-np.arange(nf, dtype=np.float32) / np.float32(nf))).astype(np.float32)
    ang = np.concatenate([row[:, None] * inv, col[:, None] * inv], axis=-1).astype(np.float32)
    cos, sin = np.cos(ang).astype(np.float32), np.sin(ang).astype(np.float32)
    cos2 = np.concatenate([cos, cos], axis=-1)
    sin2 = np.concatenate([-sin, sin], axis=-1)
    cos2 = np.concatenate([cos2, np.ones((LC, RET_DK), np.float32)], axis=0)
    sin2 = np.concatenate([sin2, np.zeros((LC, RET_DK), np.float32)], axis=0)
    return jnp.asarray(cos2), jnp.asarray(sin2)


def _mod_call(c8, wada_f, b_ada):
    ws = wada_f.shape[2]

    def body(c_ref, w_ref, b_ref, o_ref):
        a = c_ref[...]
        a = (a * _sigmoid(a)).astype(BF16)
        for s in range(N_SHARD):
            o_ref[:, s * ws:(s + 1) * ws] = _dot(a, w_ref[s]) + b_ref[:, s * ws:(s + 1) * ws]

    return pl.pallas_call(
        body, name="ada_mod", out_shape=jax.ShapeDtypeStruct((8, 3 * D), F32),
        compiler_params=_params())(c8, wada_f, b_ada)


def _dc_masks():
    cq = lax.broadcasted_iota(jnp.int32, (GRID_W, GRID_W), 0)
    ck = lax.broadcasted_iota(jnp.int32, (GRID_W, GRID_W), 1)
    dc = jnp.clip(ck - cq + 15, 0, 30)
    c0 = jnp.clip(cq - 8, 0, GRID_W - 16)
    col_ok = (ck >= c0) & (ck < c0 + 16)
    return dc, col_ok


def _bias_blocks():
    out = []
    for typ, delta in enumerate((4, 0, -4)):
        for rq in range(4):
            for rkk in range(12):
                dr = rkk + delta - rq - 4
                if typ == 0:
                    ok = -rq <= dr <= 7 - rq
                elif typ == 1:
                    ok = -4 <= dr <= 3
                else:
                    ok = -4 - rq <= dr <= 3 - rq
                out.append((typ, rq, rkk, dr if ok else None))
    return out


def _bias_body(r_ref, bias_ref, et_ref):
    dc, col_ok = _dc_masks()
    masks = [(dc == j).astype(F32) for j in range(31)]

    def per_h(h, carry):
        for dr in range(15):
            t = jnp.zeros((GRID_W, GRID_W), F32)
            for j in range(31):
                t = t + masks[j] * r_ref[h, dr * 31 + j]
            et_ref[dr] = jnp.where(col_ok, t, NEG)
        neg = jnp.full((GRID_W, GRID_W), NEG, F32)
        for typ, rq, rkk, dr in _bias_blocks():
            blk = neg if dr is None else et_ref[dr + 7]
            bias_ref[h, typ, rq * 64:(rq + 1) * 64, rkk * 64:(rkk + 1) * 64] = blk
        return carry

    lax.fori_loop(0, bias_ref.shape[0], per_h, 0)


def _bias_tile_sums(db_ref, hh):
    acc = {}
    for typ, rq, rkk, dr in _bias_blocks():
        if dr is None:
            continue
        blk = db_ref[hh, typ, rq * 64:(rq + 1) * 64, rkk * 64:(rkk + 1) * 64]
        acc[dr] = blk if dr not in acc else acc[dr] + blk
    return acc


def _small_reduce_body(dt_ref, dlg_ref, drpb_ref, dlgo_ref, p_ref):
    dc, _ = _dc_masks()
    masks = [(dc == j).astype(F32) for j in range(31)]
    ones = jnp.ones((8, GRID_W), F32)
    p_ref[...] = jnp.zeros_like(p_ref)
    drpb_ref[...] = jnp.zeros_like(drpb_ref)

    def per_h(h, carry):
        for dr in range(-7, 8):
            t = dt_ref[h, dr + 7]
            for j in range(31):
                p_ref[j:j + 1, :] = jnp.sum(t * masks[j], axis=0, keepdims=True)
            red = lax.dot_general(ones, p_ref[...], (((1,), (1,)), ((), ())),
                                  precision=HIGHEST, preferred_element_type=F32)
            drpb_ref[h, dr + 7:dr + 8, :] = red[0:1, :]
        return carry

    lax.fori_loop(0, dt_ref.shape[0], per_h, 0)
    x = dlg_ref[0]
    for b in range(1, dlg_ref.shape[0]):
        x = x + dlg_ref[b]
    x = x.reshape(4 * 8, x.shape[-1])
    dlgo_ref[...] = jnp.dot(x, jnp.ones((x.shape[-1], 128), F32), precision=HIGHEST,
                            preferred_element_type=F32)


def _inproj_gather_call(order, x, ctx, mod, norm_g, win_b, cos2, sin2):
    B, L, _ = x.shape
    LC = ctx.shape[1]
    T = L + LC
    nl, nt = L // TQ, T // TQ
    assert LC == TQ and L % TQ == 0
    kscale = RET_DK ** -0.5
    HR = D // 2

    def body(ord_ref, x_ref, ctx_ref, mod_ref, g_ref, wown_ref, cos_ref, sin_ref, p_ref, h_ref, wf_ref,
             w_all, hs_ref, ssem, rsem, lsem):
        j, b, t = pl.program_id(0), pl.program_id(1), pl.program_id(2)
        first = (b == 0) & (t == 0)
        mx, my, mc = _mesh_pos()
        s = 2 * mx + my
        sib = (mx, my, 1 - mc)
        own = pltpu.make_async_copy(wown_ref, w_all.at[s], lsem.at[0])
        ici_send, ici_recv, fwd_send, fwd_recv, outs = [], [], [], [], [
            pltpu.make_async_copy(w_all.at[s], wf_ref.at[s], lsem.at[1])]
        for k, (px, py) in enumerate(_other_chips(mx, my)):
            ps = 2 * px + py
            mine = w_all.at[s, pl.ds(mc * HR, HR)]
            ici_send.append(_remote(mine, w_all.at[s, pl.ds(mc * HR, HR)], ssem, rsem, k, (px, py, mc)))
            ici_recv.append(_remote(mine, w_all.at[ps, pl.ds(mc * HR, HR)], ssem, rsem, k, (px, py, mc)))
            got = w_all.at[ps, pl.ds(mc * HR, HR)]
            fwd_send.append(_remote(got, got, ssem, rsem, 3 + k, sib))
            theirs = w_all.at[ps, pl.ds((1 - mc) * HR, HR)]
            fwd_recv.append(_remote(theirs, theirs, ssem, rsem, 3 + k, sib))
            outs.append(pltpu.make_async_copy(w_all.at[ps], wf_ref.at[ps], lsem.at[2 + k]))

        @pl.when(first & (j == 0))
        def _():
            own.start()
            own.wait()
            for cp in ici_send:
                cp.start()
            outs[0].start()

        for k in range(3):
            @pl.when(first & (j == k + 1))
            def _(k=k):
                ici_recv[k].wait_recv()
                fwd_send[k].start()
                fwd_recv[k].wait_recv()
                outs[1 + k].start()

        tile = b * nt + t

        @pl.when(j == 0)
        def _():
            is_lat = t < nl
            xt = jnp.where(is_lat, x_ref[...], ctx_ref[...])
            mrow = mod_ref[pl.ds(jnp.where(is_lat, b, B), 1), :]
            shift, scale = mrow[:, 0:D], mrow[:, D:2 * D]
            rstd = lax.rsqrt(jnp.mean(xt * xt, axis=-1, keepdims=True) + EPS)
            h0 = ((xt * rstd * g_ref[...]) * (1.0 + scale) + shift).astype(BF16)
            h_ref[...] = h0
            hs_ref[tile] = h0

        hb = hs_ref[tile]
        cs, sn = cos_ref[...], sin_ref[...]
        shard = ord_ref[j]
        for sh in range(N_SHARD):
            @pl.when(shard == sh)
            def _(sh=sh):
                for half in range(2):
                    sec = 2 * sh + half
                    acc = _dot(hb, w_all[sh, :, half * 512:(half + 1) * 512])
                    if sec == 0:
                        acc = acc * (NA_DH ** -0.5)
                    if sec in (4, 5):
                        for q in range(4):
                            a = acc[:, q * 128:(q + 1) * 128]
                            r = a * cs + pltpu.roll(a, 64, 1) * sn
                            if sec == 5:
                                r = r * kscale
                            p_ref[:, half * 512 + q * 128:half * 512 + (q + 1) * 128] = r.astype(BF16)
                    else:
                        p_ref[:, half * 512:(half + 1) * 512] = acc.astype(BF16)

        @pl.when((j == N_SHARD - 1) & (b == B - 1) & (t == nt - 1))
        def _():
            _finish(outs, ici_send + fwd_send, [])

    tok = lambda j, b, t, o: (jnp.where(j == 0, b, B - 1), jnp.where(j == 0, jnp.minimum(t, nl - 1), nl - 1), 0)
    grid_spec = pltpu.PrefetchScalarGridSpec(
        num_scalar_prefetch=1, grid=(N_SHARD, B, nt),
        in_specs=[
            pl.BlockSpec((None, TQ, D), tok),
            pl.BlockSpec((None, TQ, D), lambda j, b, t, o: (jnp.where(j == 0, b, B - 1), 0, 0)),
            pl.BlockSpec((8, 3 * D), lambda j, b, t, o: (0, 0)),
            pl.BlockSpec((1, D), lambda j, b, t, o: (0, 0)),
            ANY,
            pl.BlockSpec((TQ, RET_DK), lambda j, b, t, o: (t, 0)),
            pl.BlockSpec((TQ, RET_DK), lambda j, b, t, o: (t, 0)),
        ],
        out_specs=(pl.BlockSpec((None, TQ, D), lambda j, b, t, o: (b, t, o[j])),
                   pl.BlockSpec((None, TQ, D), lambda j, b, t, o: (
                       jnp.where(j == 0, b, B - 1), jnp.where(j == 0, t, nt - 1), 0)), ANY),
        scratch_shapes=[pltpu.VMEM((N_SHARD, D, D), BF16), pltpu.VMEM((B * nt, TQ, D), BF16),
                        pltpu.SemaphoreType.DMA((6,)), pltpu.SemaphoreType.DMA((6,)),
                        pltpu.SemaphoreType.DMA((5,))])
    return pl.pallas_call(
        body, name="in_proj", grid_spec=grid_spec,
        out_shape=(jax.ShapeDtypeStruct((B, T, 4 * D), BF16), jax.ShapeDtypeStruct((B, T, D), BF16),
                   jax.ShapeDtypeStruct((N_SHARD, D, D), BF16)),
        compiler_params=_params(("arbitrary",) * 3))(order, x, ctx, mod, norm_g, win_b, cos2, sin2)


def _na_specs(L, T, rows, nh=2):
    nm = rows // 4
    w = nh * NA_DH
    per = 512 // w
    q_spec = pl.BlockSpec((None, TQ, w), lambda hp, b, m: (b, m, hp))
    k_spec = pl.BlockSpec((None, T, w), lambda hp, b, m: (b, 0, per + hp))
    v_spec = pl.BlockSpec((None, T, w), lambda hp, b, m: (b, 0, 2 * per + hp))
    g_spec = pl.BlockSpec((None, TQ, w), lambda hp, b, m: (b, m, 3 * per + hp))
    bias_spec = pl.BlockSpec((nh, 3, TQ, KW), lambda hp, b, m: (hp, 0, 0, 0))
    return nm, q_spec, k_spec, v_spec, g_spec, bias_spec


def _na_tile(m, nm, rows):
    typ = jnp.where(m == 0, 0, jnp.where(m == nm - 1, 2, 1))
    start = pl.multiple_of(jnp.clip(4 * m - 4, 0, rows - 12) * GRID_W, TQ)
    return typ, start


def _na_fwd_call(P, bias, L, LC):
    B, T, _ = P.shape
    rows = L // GRID_W
    NH = 4
    nm, q_spec, k_spec, v_spec, g_spec, bias_spec = _na_specs(L, T, rows, NH)

    def body(q_ref, k_ref, v_ref, g_ref, bias_ref, y_ref, o_ref):
        typ, start = _na_tile(pl.program_id(2), nm, rows)
        for hh in range(NH):
            ln = slice(hh * NA_DH, (hh + 1) * NA_DH)
            q = q_ref[:, ln]
            kw, vw = k_ref[pl.ds(start, KW), ln], v_ref[pl.ds(start, KW), ln]
            kc, vc = k_ref[L:L + LC, ln], v_ref[L:L + LC, ln]
            s1 = _dot_nt(q, kw) + bias_ref[hh, typ]
            s2 = _dot_nt(q, kc)
            mx = jnp.maximum(jnp.max(s1, axis=-1, keepdims=True), jnp.max(s2, axis=-1, keepdims=True))
            p1, p2 = jnp.exp(s1 - mx), jnp.exp(s2 - mx)
            inv = 1.0 / (jnp.sum(p1, axis=-1, keepdims=True) + jnp.sum(p2, axis=-1, keepdims=True))
            o = (_dot(p1.astype(BF16), vw) + _dot(p2.astype(BF16), vc)) * inv
            g = g_ref[:, ln].astype(F32)
            o_ref[:, ln] = o.astype(BF16)
            y_ref[:, ln] = (o * (g * _sigmoid(g))).astype(BF16)

    tile = pl.BlockSpec((None, TQ, NH * NA_DH), lambda hp, b, m: (b, m, hp))
    return pl.pallas_call(
        body, name="na_fwd", grid=(8 // NH, B, nm),
        in_specs=[q_spec, k_spec, v_spec, g_spec, bias_spec],
        out_specs=(tile, tile),
        out_shape=(jax.ShapeDtypeStruct((B, L, 512), BF16),) * 2,
        compiler_params=_params(("arbitrary",) * 3))(P, P, P, P, bias)


def _na_bwd_call(P, bias, dY, o_na, L, LC):
    B, T, _ = P.shape
    rows = L // GRID_W
    NH = 4
    W = NH * NA_DH
    nm, q_spec, k_spec, v_spec, g_spec, bias_spec = _na_specs(L, T, rows, NH)
    scale = NA_DH ** -0.5

    RB = 32

    def body(q_ref, k_ref, v_ref, g_ref, bias_ref, dy_ref, o_ref, dq_ref, dg_ref, dk_ref, dv_ref, dt_ref,
             db_ref, s1_ref, s2_ref, dp1_ref, dp2_ref, p1_ref, p2_ref, ds1_ref, ds2_ref, dkt_ref, dvt_ref):
        b, m = pl.program_id(1), pl.program_id(2)
        typ, start = _na_tile(m, nm, rows)

        @pl.when(m == 0)
        def _():
            dkt_ref[...] = jnp.zeros_like(dkt_ref)
            dvt_ref[...] = jnp.zeros_like(dvt_ref)

        @pl.when((m == 0) & (b == 0))
        def _():
            db_ref[...] = jnp.zeros_like(db_ref)

        for hh in range(NH):
            ln = slice(hh * NA_DH, (hh + 1) * NA_DH)
            q = q_ref[:, ln]
            kw, vw = k_ref[pl.ds(start, KW), ln], v_ref[pl.ds(start, KW), ln]
            kc, vc = k_ref[L:L + LC, ln], v_ref[L:L + LC, ln]
            g = g_ref[:, ln].astype(F32)
            sg = _sigmoid(g)
            dy = dy_ref[:, ln].astype(F32)
            do = (dy * (g * sg)).astype(BF16)
            s1_ref[hh] = _dot_nt(q, kw)
            s2_ref[hh] = _dot_nt(q, kc)
            dp1_ref[hh] = _dot_nt(do, vw)
            dp2_ref[hh] = _dot_nt(do, vc)

            def rows_pass(r, carry, hh=hh):
                rw = pl.ds(pl.multiple_of(r * RB, RB), RB)
                a = s1_ref[hh, rw, :] + bias_ref[hh, typ, rw, :]
                c = s2_ref[hh, rw, :]
                mx = jnp.maximum(jnp.max(a, axis=-1, keepdims=True), jnp.max(c, axis=-1, keepdims=True))
                e1, e2 = jnp.exp(a - mx), jnp.exp(c - mx)
                inv = 1.0 / (jnp.sum(e1, axis=-1, keepdims=True) + jnp.sum(e2, axis=-1, keepdims=True))
                p1, p2 = e1 * inv, e2 * inv
                p1_ref[hh, rw, :] = p1.astype(BF16)
                p2_ref[hh, rw, :] = p2.astype(BF16)
                dp1, dp2 = dp1_ref[hh, rw, :], dp2_ref[hh, rw, :]
                delta = jnp.sum(p1 * dp1, axis=-1, keepdims=True) + jnp.sum(p2 * dp2, axis=-1, keepdims=True)
                ds1 = p1 * (dp1 - delta)
                db_ref[hh, typ, rw, :] += ds1
                ds1_ref[hh, rw, :] = ds1.astype(BF16)
                ds2_ref[hh, rw, :] = (p2 * (dp2 - delta)).astype(BF16)
                return carry

            lax.fori_loop(0, TQ // RB, rows_pass, 0, unroll=True)
            p1b, p2b, ds1b, ds2b = p1_ref[hh], p2_ref[hh], ds1_ref[hh], ds2_ref[hh]
            dg_ref[:, ln] = (dy * o_ref[:, ln].astype(F32) * (sg * (1.0 + g * (1.0 - sg)))).astype(BF16)
            dq_ref[:, ln] = ((_dot(ds1b, kw) + _dot(ds2b, kc)) * scale).astype(BF16)
            dkt_ref[ln, pl.ds(start, KW)] += _dot_tn(q, ds1b)
            dvt_ref[ln, pl.ds(start, KW)] += _dot_tn(do, p1b)
            dkt_ref[ln, L:L + LC] += _dot_tn(q, ds2b)
            dvt_ref[ln, L:L + LC] += _dot_tn(do, p2b)

        @pl.when(m == nm - 1)
        def _():
            dk_ref[...] = dkt_ref[...].T
            dv_ref[...] = dvt_ref[...].T

        @pl.when((m == nm - 1) & (b == B - 1))
        def _():
            for hh in range(NH):
                for dr, t in _bias_tile_sums(db_ref, hh).items():
                    dt_ref[hh, dr + 7] = t

    tile = pl.BlockSpec((None, TQ, W), lambda hp, b, m: (b, m, hp))
    kv_out = pl.BlockSpec((None, T, W), lambda hp, b, m: (b, 0, hp))
    wide, narrow = (NH, TQ, KW), (NH, TQ, LC)
    return pl.pallas_call(
        body, name="na_bwd", grid=(8 // NH, B, nm),
        in_specs=[q_spec, k_spec, v_spec, g_spec, bias_spec, tile, tile],
        out_specs=(tile, tile, kv_out, kv_out,
                   pl.BlockSpec((NH, 15, GRID_W, GRID_W), lambda hp, b, m: (hp, 0, 0, 0))),
        out_shape=(jax.ShapeDtypeStruct((B, L, 512), BF16), jax.ShapeDtypeStruct((B, L, 512), BF16),
                   jax.ShapeDtypeStruct((B, T, 512), F32), jax.ShapeDtypeStruct((B, T, 512), F32),
                   jax.ShapeDtypeStruct((bias.shape[0], 15, GRID_W, GRID_W), F32)),
        scratch_shapes=[pltpu.VMEM((NH,) + bias.shape[1:], F32),
                        pltpu.VMEM(wide, F32), pltpu.VMEM(narrow, F32), pltpu.VMEM(wide, F32), pltpu.VMEM(narrow, F32),
                        pltpu.VMEM(wide, BF16), pltpu.VMEM(narrow, BF16), pltpu.VMEM(wide, BF16),
                        pltpu.VMEM(narrow, BF16), pltpu.VMEM((W, T), F32), pltpu.VMEM((W, T), F32)],
        compiler_params=_params(("arbitrary",) * 3, vmem_mb=60))(P, P, P, P, bias, dY, o_na)


def _head_scalar(dec_ref, h):
    lane = lax.broadcasted_iota(jnp.int32, dec_ref.shape, 1)
    return -jnp.sum(jnp.where(lane == h, jnp.exp(dec_ref[...]), 0.0), axis=1, keepdims=True)


def _chunk_decay(lgf, lgb):
    tau = lax.broadcasted_iota(jnp.int32, (TQ, 1), 0).astype(F32)
    sig = lax.broadcasted_iota(jnp.int32, (1, TQ), 1).astype(F32)
    dist = tau - sig
    dm = jnp.exp(dist * jnp.where(dist > 0, lgf, -lgb)) * jnp.where(dist == 0, 2.0, 1.0)
    return tau, dist, dm


def _ret_states_call(P, dec_f, dec_b, L, LC):
    B, T, _ = P.shape
    n = L // TQ

    def body(df_ref, db_ref, k_ref, v_ref, sf_ref, sb_ref):
        h = pl.program_id(1)
        lgf, lgb = _head_scalar(df_ref, h), _head_scalar(db_ref, h)
        tau = lax.broadcasted_iota(jnp.int32, (TQ, 1), 0).astype(F32)
        jc = lax.broadcasted_iota(jnp.int32, (LC, 1), 0).astype(F32)
        wf, wb = jnp.exp(lgf * (TQ - 1.0 - tau)), jnp.exp(lgb * tau)
        gcf, gcb = jnp.exp(lgf * float(TQ)), jnp.exp(lgb * float(TQ))
        kc, vc = k_ref[L:L + LC, :].astype(F32), v_ref[L:L + LC, :]

        def chunk_state(i, w):
            ks = pl.multiple_of(i * TQ, TQ)
            return _dot_tn((k_ref[pl.ds(ks, TQ), :].astype(F32) * w).astype(BF16), v_ref[pl.ds(ks, TQ), :])

        def fwd(i, s):
            sf_ref[i] = s
            return gcf * s + chunk_state(i, wf)

        lax.fori_loop(0, n, fwd, _dot_tn((kc * jnp.exp(lgf * (LC - 1.0 - jc))).astype(BF16), vc), unroll=True)

        def bwd(r, s):
            i = n - 1 - r
            sb_ref[i] = s
            return gcb * s + chunk_state(i, wb)

        lax.fori_loop(0, n, bwd, _dot_tn((kc * jnp.exp(lgb * jc)).astype(BF16), vc), unroll=True)

    st = pl.BlockSpec((None, None, n, RET_DK, RET_DK), lambda b, h: (b, h, 0, 0, 0))
    return pl.pallas_call(
        body, name="ret_states", grid=(B, 4),
        in_specs=[pl.BlockSpec((1, 4), lambda b, h: (0, 0)), pl.BlockSpec((1, 4), lambda b, h: (0, 0)),
                  pl.BlockSpec((None, T, 128), lambda b, h: (b, 0, 20 + h)),
                  pl.BlockSpec((None, T, 128), lambda b, h: (b, 0, 24 + h))],
        out_specs=(st, st),
        out_shape=(jax.ShapeDtypeStruct((B, 4, n, RET_DK, RET_DK), F32),) * 2,
        compiler_params=_params(("arbitrary",) * 2))(dec_f, dec_b, P, P)


def _retc_fwd_call(P, sf, sb, dec_f, dec_b, ret_norm_g, L):
    B, T, _ = P.shape
    sec = lambda k: pl.BlockSpec((None, TQ, 512), lambda b, i: (b, i, k))
    dec_spec = pl.BlockSpec((1, 4), lambda b, i: (0, 0))
    st_spec = pl.BlockSpec((None, 4, None, RET_DK, RET_DK), lambda b, i: (b, 0, i, 0, 0))

    def body(df_ref, db_ref, q_ref, k_ref, v_ref, g_ref, gn_ref, sf_ref, sb_ref, y_ref, o_ref):
        for h in range(4):
            ln = slice(h * RET_DK, (h + 1) * RET_DK)
            lgf, lgb = _head_scalar(df_ref, h), _head_scalar(db_ref, h)
            tau, _, dm = _chunk_decay(lgf, lgb)
            q = q_ref[:, ln]
            qf = q.astype(F32)
            acc = _dot((_dot_nt(q, k_ref[:, ln]) * dm).astype(BF16), v_ref[:, ln])
            acc = acc + _dot((qf * jnp.exp(lgf * (tau + 1.0))).astype(BF16), sf_ref[h].astype(BF16))
            acc = acc + _dot((qf * jnp.exp(lgb * (TQ - tau))).astype(BF16), sb_ref[h].astype(BF16))
            o_ref[:, ln] = acc
            rn = lax.rsqrt(jnp.mean(acc * acc, axis=-1, keepdims=True) + EPS)
            g = g_ref[:, ln].astype(F32)
            y_ref[:, ln] = ((acc * rn * gn_ref[:, ln]) * (g * _sigmoid(g))).astype(BF16)

    tile = pl.BlockSpec((None, TQ, 512), lambda b, i: (b, i, 0))
    return pl.pallas_call(
        body, name="ret_fwd", grid=(B, L // TQ),
        in_specs=[dec_spec, dec_spec, sec(4), sec(5), sec(6), sec(7),
                  pl.BlockSpec((1, 512), lambda b, i: (0, 0)), st_spec, st_spec],
        out_specs=(tile, tile),
        out_shape=(jax.ShapeDtypeStruct((B, L, 512), BF16), jax.ShapeDtypeStruct((B, L, 512), F32)),
        compiler_params=_params(("arbitrary",) * 2))(dec_f, dec_b, P, P, P, P, ret_norm_g, sf, sb)


def _retc_bwd_call(P, sf, sb, dec_f, dec_b, ret_norm_g, o_ret, dY, cos2, sin2, L, LC):
    B, T, _ = P.shape
    n = L // TQ
    C = float(TQ)
    kscale = RET_DK ** -0.5
    st_spec = pl.BlockSpec((None, 4, n, RET_DK, RET_DK), lambda b, i: (b, 0, 0, 0, 0))

    def body(df_ref, db_ref, q_ref, k_ref, v_ref, g_ref, gn_ref, o_ref, dy_ref, cos_ref, sin_ref, sf_ref, sb_ref,
             dq_ref, dg_ref, dk_ref, dv_ref, dgn_ref, dlg_ref, dsf_ref, dsb_ref):
        i = pl.program_id(1)

        @pl.when(i == 0)
        def _():
            dk_ref[...] = jnp.zeros_like(dk_ref)
            dv_ref[...] = jnp.zeros_like(dv_ref)
            dgn_ref[...] = jnp.zeros_like(dgn_ref)
            dlg_ref[...] = jnp.zeros_like(dlg_ref)

        rows = pl.ds(pl.multiple_of(i * TQ, TQ), TQ)
        cs, sn = cos_ref[rows, :], sin_ref[rows, :]

        def one_head(h):
            ln = slice(h * RET_DK, (h + 1) * RET_DK)
            lgf, lgb = _head_scalar(df_ref, h), _head_scalar(db_ref, h)
            tau, dist, dm = _chunk_decay(lgf, lgb)

            def add_lg(row, x):
                csum = jnp.sum(x, axis=0, keepdims=True)
                tot = csum[:, 0:128]
                for part in range(1, x.shape[1] // 128):
                    tot = tot + csum[:, part * 128:(part + 1) * 128]
                dlg_ref[h, row:row + 1, :] += tot

            q = q_ref[:, ln]
            qf = q.astype(F32)
            o = o_ref[:, ln]
            g = g_ref[:, ln].astype(F32)
            dy = dy_ref[:, ln].astype(F32)
            gn = gn_ref[:, ln]
            sg = _sigmoid(g)
            rn = lax.rsqrt(jnp.mean(o * o, axis=-1, keepdims=True) + EPS)
            nrm = o * rn
            dg_ref[:, ln] = (dy * (nrm * gn) * (sg * (1.0 + g * (1.0 - sg)))).astype(BF16)
            dhn = dy * (g * sg)
            dgn_ref[:, ln] += jnp.sum(dhn * nrm, axis=0, keepdims=True)
            dnrm = dhn * gn
            do = rn * (dnrm - nrm * jnp.mean(dnrm * nrm, axis=-1, keepdims=True))
            dob = do.astype(BF16)
            ki, vi = k_ref[rows, ln], v_ref[rows, ln]
            s = _dot_nt(q, ki)
            dsv = _dot_nt(dob, vi)
            dsb = (dsv * dm).astype(BF16)
            dk_ref[rows, ln] += _dot_tn(dsb, q)
            dv_ref[rows, ln] += _dot_tn((s * dm).astype(BF16), dob)
            xw = s * dsv * dm * jnp.abs(dist)
            fpart = jnp.where(dist > 0, xw, 0.0)
            add_lg(0, fpart)
            add_lg(1, xw - fpart)
            dq = _dot(dsb, ki)
            af, ab = jnp.exp(lgf * (tau + 1.0)), jnp.exp(lgb * (C - tau))
            qa, qb = (qf * af).astype(BF16), (qf * ab).astype(BF16)
            sfi, sbi = sf_ref[h, i].astype(BF16), sb_ref[h, i].astype(BF16)
            dq = dq + af * _dot_nt(dob, sfi) + ab * _dot_nt(dob, sbi)
            dsf_ref[h, i] = _dot_tn(qa, dob)
            dsb_ref[h, i] = _dot_tn(qb, dob)
            add_lg(0, (tau + 1.0) * (_dot(qa, sfi) * do))
            add_lg(1, (C - tau) * (_dot(qb, sbi) * do))
            dq_ref[:, ln] = (dq * cs - pltpu.roll(dq, 64, 1) * sn).astype(BF16)

            @pl.when(i == n - 1)
            def _():
                jc = lax.broadcasted_iota(jnp.int32, (LC, 1), 0).astype(F32)
                crow = pl.ds(L, LC)

                def through_state(rws, w, dw, gst, row):
                    kk, vv = k_ref[rws, ln].astype(F32), v_ref[rws, ln]
                    gb = gst.astype(BF16)
                    vg = _dot_nt(vv, gb)
                    kw = kk * w
                    dk_ref[rws, ln] += w * vg
                    dv_ref[rws, ln] += _dot(kw.astype(BF16), gb)
                    add_lg(row, dw * (kw * vg))

                def scan(gc, w, dw, st_ref, dst_ref, order, row):
                    def step(r, gst):
                        j = order(r)
                        through_state(pl.ds(pl.multiple_of(j * TQ, TQ), TQ), w, dw, gst, row)
                        add_lg(row, (C * gc) * (gst * st_ref[h, j]))
                        return dst_ref[h, j] + gc * gst
                    return lax.fori_loop(0, n, step, jnp.zeros((RET_DK, RET_DK), F32), unroll=True)

                gcf, gcb = jnp.exp(lgf * C), jnp.exp(lgb * C)
                g0 = scan(gcf, jnp.exp(lgf * (C - 1.0 - tau)), C - 1.0 - tau, sf_ref, dsf_ref,
                          lambda r: n - 1 - r, 0)
                through_state(crow, jnp.exp(lgf * (LC - 1.0 - jc)), LC - 1.0 - jc, g0, 0)
                g1 = scan(gcb, jnp.exp(lgb * tau), tau, sb_ref, dsb_ref, lambda r: r, 1)
                through_state(crow, jnp.exp(lgb * jc), jc, g1, 1)
                dk = dk_ref[:, ln]
                dk_ref[:, ln] = (dk * cos_ref[...] - pltpu.roll(dk, 64, 1) * sin_ref[...]) * kscale

        for h in range(4):
            one_head(h)

    sec = lambda k: pl.BlockSpec((None, TQ, 512), lambda b, i: (b, i, k))
    full = lambda k: pl.BlockSpec((None, T, 512), lambda b, i: (b, 0, k))
    dec_spec = pl.BlockSpec((1, 4), lambda b, i: (0, 0))
    tab = pl.BlockSpec((T, RET_DK), lambda b, i: (0, 0))
    return pl.pallas_call(
        body, name="ret_bwd", grid=(B, n),
        in_specs=[dec_spec, dec_spec, sec(4), full(5), full(6), sec(7),
                  pl.BlockSpec((1, 512), lambda b, i: (0, 0)), sec(0), sec(1), tab, tab, st_spec, st_spec],
        out_specs=(sec(0), sec(0), full(0), full(0),
                   pl.BlockSpec((None, 1, 512), lambda b, i: (b, 0, 0)),
                   pl.BlockSpec((None, 4, 8, 128), lambda b, i: (b, 0, 0, 0))),
        out_shape=(jax.ShapeDtypeStruct((B, L, 512), BF16), jax.ShapeDtypeStruct((B, L, 512), BF16),
                   jax.ShapeDtypeStruct((B, T, 512), F32), jax.ShapeDtypeStruct((B, T, 512), F32),
                   jax.ShapeDtypeStruct((B, 1, 512), F32), jax.ShapeDtypeStruct((B, 4, 8, 128), F32)),
        scratch_shapes=[pltpu.VMEM((4, n, RET_DK, RET_DK), F32), pltpu.VMEM((4, n, RET_DK, RET_DK), F32)],
        compiler_params=_params(("arbitrary",) * 2, vmem_mb=56))(
            dec_f, dec_b, P, P, P, P, ret_norm_g, o_ret, dY, cos2, sin2, sf, sb)


def _out_call(y_na, y_ret, x, target, mod, final_g, wout_f):
    B, L, _ = x.shape

    def body(yn_ref, yr_ref, x_ref, t_ref, mod_ref, gf_ref, w_ref, dy_ref, dx2_ref, dw_ref, sm_ref):
        b, i = pl.program_id(0), pl.program_id(1)

        @pl.when((b == 0) & (i == 0))
        def _():
            dw_ref[...] = jnp.zeros_like(dw_ref)
            sm_ref[...] = jnp.zeros_like(sm_ref)

        gate = mod_ref[pl.ds(b, 1), 2 * D:3 * D]
        gf = gf_ref[...]
        yn, yr = yn_ref[...], yr_ref[...]
        ylat = _dot(yn, w_ref[0:512, :]) + _dot(yr, w_ref[512:1024, :])
        x2 = x_ref[...] + gate * ylat
        r = lax.rsqrt(jnp.mean(x2 * x2, axis=-1, keepdims=True) + EPS)
        xr = x2 * r
        err = xr * gf - t_ref[...]
        sm_ref[1:2, :] += jnp.sum(err * err, axis=0, keepdims=True)
        dout = err * (1.0 / D)
        sm_ref[0:1, :] += jnp.sum(dout * xr, axis=0, keepdims=True)
        gd = dout * gf
        dx2 = r * (gd - xr * jnp.mean(gd * xr, axis=-1, keepdims=True))
        dx2_ref[...] = dx2
        sm_ref[pl.ds(2 + b, 1), :] += jnp.sum(dx2 * ylat, axis=0, keepdims=True)
        dyl = (gate * dx2).astype(BF16)
        dy_ref[:, 0:512] = _dot_nt(dyl, w_ref[0:512, :]).astype(BF16)
        dy_ref[:, 512:1024] = _dot_nt(dyl, w_ref[512:1024, :]).astype(BF16)
        dw_ref[0:512, :] += _dot_tn(yn, dyl)
        dw_ref[512:1024, :] += _dot_tn(yr, dyl)

    half = pl.BlockSpec((None, TQ, 512), lambda b, i: (b, i, 0))
    full = pl.BlockSpec((None, TQ, D), lambda b, i: (b, i, 0))
    return pl.pallas_call(
        body, name="out_proj_loss", grid=(B, L // TQ),
        in_specs=[half, half, full, full,
                  pl.BlockSpec((8, 3 * D), lambda b, i: (0, 0)),
                  pl.BlockSpec((1, D), lambda b, i: (0, 0)),
                  pl.BlockSpec((D, D), lambda b, i: (0, 0))],
        out_specs=(full, full, pl.BlockSpec((D, D), lambda b, i: (0, 0)),
                   pl.BlockSpec((8, D), lambda b, i: (0, 0))),
        out_shape=(jax.ShapeDtypeStruct((B, L, D), BF16), jax.ShapeDtypeStruct((B, L, D), F32),
                   jax.ShapeDtypeStruct((D, D), F32), jax.ShapeDtypeStruct((8, D), F32)),
        compiler_params=_params(("arbitrary",) * 2))(y_na, y_ret, x, target, mod, final_g, wout_f)


def _dh_call(dsec, win_f, x, ctx, dx2, mod, norm_g, cp_in, cp_out):
    B, L, _ = x.shape
    LC = ctx.shape[1]
    nl = L // TQ

    def body(d0, d1, d2, d3, d4, d5, d6, d7, w_ref, x_ref, ctx_ref, dx2_ref, mod_ref, g_ref, cpi_ref, cpo_ref,
             gx_ref, sm_ref, sli_ref, slo_ref, ssem, rsem, lsem):
        drefs = (d0, d1, d2, d3, d4, d5, d6, d7)
        b, t = pl.program_id(0), pl.program_id(1)
        is_lat = t < nl

        @pl.when((b == 0) & (t == 0))
        def _():
            sm_ref[...] = jnp.zeros_like(sm_ref)

        def dh_of(secs):
            acc = jnp.zeros((TQ, D), F32)
            for sec in secs:
                s, half = divmod(sec, 2)
                acc = acc + _dot_nt(drefs[sec][...].astype(BF16), w_ref[s, :, half * 512:(half + 1) * 512])
            return acc

        def norm_bwd(dh, xt, mrow):
            scale = mrow[:, D:2 * D]
            g = g_ref[...]
            rstd = lax.rsqrt(jnp.mean(xt * xt, axis=-1, keepdims=True) + EPS)
            xn = xt * rstd
            dshift = jnp.sum(dh, axis=0, keepdims=True)
            dscale = jnp.sum(dh * (xn * g), axis=0, keepdims=True)
            dhn = dh * (1.0 + scale)
            sm_ref[0:1, :] += jnp.sum(dhn * xn, axis=0, keepdims=True)
            dxn = dhn * g
            dx = rstd * (dxn - xn * jnp.mean(dxn * xn, axis=-1, keepdims=True))
            return dshift, dscale, dx

        @pl.when(is_lat)
        def _():
            dshift, dscale, dx = norm_bwd(dh_of(range(8)), x_ref[...], mod_ref[pl.ds(b, 1), :])
            sm_ref[pl.ds(3 + b, 1), :] += dshift
            sm_ref[pl.ds(3 + B + b, 1), :] += dscale
            gx_ref[...] = dx2_ref[...] + dx

        @pl.when(jnp.logical_not(is_lat))
        def _():
            dshift, dscale, _ = norm_bwd(dh_of((1, 2, 5, 6)), ctx_ref[...], mod_ref[B:B + 1, :])
            sm_ref[1:2, :] += dshift
            sm_ref[2:3, :] += dscale

        mx, my, mc = _mesh_pos()
        s = 2 * mx + my
        cps, sls = (cpi_ref, cpo_ref), (sli_ref, slo_ref)
        own = [pltpu.make_async_copy(cps[a].at[s], sls[a].at[s], lsem.at[a]) for a in range(2)]
        sends, recvs, k = [], [], 0
        for px, py in _other_chips(mx, my):
            ps = 2 * px + py
            for a in range(2):
                sends.append(_remote(cps[a].at[ps], sls[a].at[s], ssem, rsem, k, (px, py, mc)))
                recvs.append(_remote(cps[a].at[s], sls[a].at[ps], ssem, rsem, k, (px, py, mc)))
                k += 1

        @pl.when((b == 0) & (t == 0))
        def _():
            for cp in own + sends:
                cp.start()

        @pl.when((b == B - 1) & (t == nl))
        def _():
            _finish(own, sends, recvs)

    lat = lambda b, t: (b, jnp.minimum(t, nl - 1), 0)
    tok = lambda b, t: (b, t, 0)
    sec_specs = [pl.BlockSpec((None, TQ, 512), lat if sec in (0, 3, 4, 7) else tok) for sec in range(8)]
    return pl.pallas_call(
        body, name="dh_norm_bwd", grid=(B, nl + 1),
        in_specs=sec_specs + [
            pl.BlockSpec((N_SHARD, D, D), lambda b, t: (0, 0, 0)),
            pl.BlockSpec((None, TQ, D), lat),
            pl.BlockSpec((None, LC, D), lambda b, t: (b, 0, 0)),
            pl.BlockSpec((None, TQ, D), lat),
            pl.BlockSpec((8, 3 * D), lambda b, t: (0, 0)),
            pl.BlockSpec((1, D), lambda b, t: (0, 0)), ANY, ANY],
        out_specs=(pl.BlockSpec((None, TQ, D), lat), pl.BlockSpec((8, D), lambda b, t: (0, 0)), ANY, ANY),
        out_shape=(jax.ShapeDtypeStruct((B, L, D), F32), jax.ShapeDtypeStruct((8, D), F32),
                   jax.ShapeDtypeStruct(cp_in.shape, cp_in.dtype), jax.ShapeDtypeStruct(cp_out.shape, cp_out.dtype)),
        scratch_shapes=[pltpu.SemaphoreType.DMA((6,)), pltpu.SemaphoreType.DMA((6,)),
                        pltpu.SemaphoreType.DMA((2,))],
        compiler_params=_params(("arbitrary",) * 2))(*dsec, win_f, x, ctx, dx2, mod, norm_g, cp_in, cp_out)


def _dw_call(dsec, h, L):
    B, T, _ = h.shape
    nl = L // TQ

    def body(d0, d1, d2, d3, d4, d5, d6, d7, h_ref, dw_ref, acc_ref):
        drefs = (d0, d1, d2, d3, d4, d5, d6, d7)
        b, t = pl.program_id(0), pl.program_id(1)

        @pl.when((b == 0) & (t == 0))
        def _():
            acc_ref[...] = jnp.zeros_like(acc_ref)

        hb = h_ref[...]

        def add(secs):
            for sec in secs:
                s, half = divmod(sec, 2)
                acc_ref[s, :, half * 512:(half + 1) * 512] += _dot_tn(hb, drefs[sec][...].astype(BF16))

        @pl.when(t < nl)
        def _():
            add(range(8))

        @pl.when(t >= nl)
        def _():
            add((1, 2, 5, 6))

        @pl.when((b == B - 1) & (t == nl))
        def _():
            dw_ref[...] = acc_ref[...].astype(BF16)

    lat = lambda b, t: (b, jnp.minimum(t, nl - 1), 0)
    tok = lambda b, t: (b, t, 0)
    sec_specs = [pl.BlockSpec((None, TQ, 512), lat if sec in (0, 3, 4, 7) else tok) for sec in range(8)]
    return pl.pallas_call(
        body, name="dw_in", grid=(B, nl + 1),
        in_specs=sec_specs + [pl.BlockSpec((None, TQ, D), tok)],
        out_specs=pl.BlockSpec((N_SHARD, D, D), lambda b, t: (0, 0, 0)),
        out_shape=jax.ShapeDtypeStruct((N_SHARD, D, D), BF16),
        scratch_shapes=[pltpu.VMEM((N_SHARD, D, D), F32)],
        compiler_params=_params(("arbitrary",) * 2, vmem_mb=56))(*dsec, h)


def _mesh_pos():
    return lax.axis_index("x"), lax.axis_index("y"), lax.axis_index("c")


def _flip(v, f):
    return 1 - v if f else v


def _remote(src, dst, ssem, rsem, k, peer):
    return pltpu.make_async_remote_copy(src_ref=src, dst_ref=dst, send_sem=ssem.at[k], recv_sem=rsem.at[k],
                                        device_id=peer, device_id_type=MESH)


def _other_chips(x, y):
    return [(_flip(x, fx), _flip(y, fy)) for fx, fy in ((1, 0), (0, 1), (1, 1))]


def _all_to_all_small(src, dst_all, ssem, rsem, k0, x, y, cc):
    me = 4 * x + 2 * y + cc
    sends, recvs = [], []
    for f in range(1, N_DEV):
        px, py, pc = _flip(x, f & 4), _flip(y, f & 2), _flip(cc, f & 1)
        sends.append(_remote(src, dst_all.at[me], ssem, rsem, k0 + f - 1, (px, py, pc)))
        recvs.append(_remote(src, dst_all.at[4 * px + 2 * py + pc], ssem, rsem, k0 + f - 1, (px, py, pc)))
    return sends, recvs


def _finish(local, sends, recvs):
    for cp in recvs:
        cp.wait_recv()
    for cp in sends:
        cp.wait_send()
    for cp in local:
        cp.wait()


def _gather_call(wout_b, wada_b, c, rpb_flat):
    arrs = (wout_b, wada_b)
    na = len(arrs)
    hrs = [a.shape[0] // 2 for a in arrs]

    def body(wout, wada, c_ref, r_ref, wout_f, wada_f, c_all, bias_ref, et_ref, ssem, rsem, lsem):
        x, y, cc = _mesh_pos()
        s, me = 2 * x + y, 4 * x + 2 * y + cc
        sib = (x, y, 1 - cc)
        srcs, dsts = (wout, wada), (wout_f, wada_f)

        def half(a, shard, hc):
            return dsts[a].at[shard, pl.ds(hc * hrs[a], hrs[a])]

        local = [pltpu.make_async_copy(srcs[a], dsts[a].at[s], lsem.at[a]) for a in range(na)]
        local.append(pltpu.make_async_copy(c_ref, c_all.at[me], lsem.at[na]))
        ici_send, ici_recv, fwd_send, fwd_recv, k = [], [], [], [], 0
        for px, py in _other_chips(x, y):
            ps = 2 * px + py
            for a in range(na):
                mine = srcs[a].at[pl.ds(cc * hrs[a], hrs[a])]
                ici_send.append(_remote(mine, half(a, s, cc), ssem, rsem, k, (px, py, cc)))
                ici_recv.append(_remote(mine, half(a, ps, cc), ssem, rsem, k, (px, py, cc)))
                fwd_send.append(_remote(half(a, ps, cc), half(a, ps, cc), ssem, rsem, 3 * na + k, sib))
                fwd_recv.append(_remote(half(a, ps, 1 - cc), half(a, ps, 1 - cc), ssem, rsem, 3 * na + k, sib))
                k += 1
        c_send, c_recv = _all_to_all_small(c_ref, c_all, ssem, rsem, 6 * na, x, y, cc)
        for cp in local + ici_send + c_send:
            cp.start()
        _bias_body(r_ref, bias_ref, et_ref)
        for got, fwd in zip(ici_recv, fwd_send):
            got.wait_recv()
            fwd.start()
        _finish(local, ici_send + fwd_send + c_send, fwd_recv + c_recv)

    return pl.pallas_call(
        body, name="weight_gather",
        in_specs=[pl.BlockSpec(memory_space=pltpu.VMEM)] * 3 + [pl.BlockSpec(memory_space=pltpu.SMEM)],
        out_specs=(pl.BlockSpec(memory_space=pltpu.VMEM),) * 4,
        out_shape=tuple(jax.ShapeDtypeStruct((N_SHARD,) + a.shape, a.dtype) for a in arrs)
        + (jax.ShapeDtypeStruct((N_DEV,) + c.shape, c.dtype),
           jax.ShapeDtypeStruct((rpb_flat.shape[0], 3, TQ, KW), F32)),
        scratch_shapes=[pltpu.VMEM((15, GRID_W, GRID_W), F32),
                        pltpu.SemaphoreType.DMA((6 * na + 7,)), pltpu.SemaphoreType.DMA((6 * na + 7,)),
                        pltpu.SemaphoreType.DMA((na + 1,))],
        compiler_params=pltpu.CompilerParams(vmem_limit_bytes=56 << 20))(wout_b, wada_b, c, rpb_flat)


VROWS = 32


def _grad_halves_call(dwin_b, dwout_b, dbias, dlg):
    arrs = (dwin_b, dwout_b)
    hrs = [a.shape[1] // 2 for a in arrs]

    def body(din, dout, db_ref, dlg_ref, cp_in, cp_out, drpb_ref, dlgo_ref, got_in, got_out, p_ref, ssem, rsem):
        x, y, cc = _mesh_pos()
        sib = (x, y, 1 - cc)
        srcs, gots, cps = (din, dout), (got_in, got_out), (cp_in, cp_out)
        halves = [_remote(srcs[a].at[:, pl.ds((1 - cc) * hrs[a], hrs[a])], gots[a], ssem, rsem, a, sib)
                  for a in range(2)]
        for cp in halves:
            cp.start()
        _small_reduce_body(db_ref, dlg_ref, drpb_ref, dlgo_ref, p_ref)
        for cp in halves:
            cp.wait_recv()
        for a in range(2):
            for j in range(N_SHARD):
                def add(i, carry, a=a, j=j):
                    r = pl.multiple_of(i * VROWS, VROWS)
                    mine = srcs[a][j, pl.ds(pl.multiple_of(cc * hrs[a] + r, VROWS), VROWS), :].astype(F32)
                    cps[a][j, pl.ds(r, VROWS), :] = (
                        mine + gots[a][j, pl.ds(r, VROWS), :].astype(F32)).astype(BF16)
                    return carry
                lax.fori_loop(0, hrs[a] // VROWS, add, 0)
        for cp in halves:
            cp.wait_send()

    vmem = pl.BlockSpec(memory_space=pltpu.VMEM)
    half_shapes = [(N_SHARD, hrs[a], arrs[a].shape[2]) for a in range(2)]
    return pl.pallas_call(
        body, name="grad_halves",
        in_specs=[vmem] * 4, out_specs=(vmem,) * 4,
        out_shape=(jax.ShapeDtypeStruct(half_shapes[0], BF16), jax.ShapeDtypeStruct(half_shapes[1], BF16),
                   jax.ShapeDtypeStruct((dbias.shape[0], 16, 32), F32), jax.ShapeDtypeStruct((32, 128), F32)),
        scratch_shapes=[pltpu.VMEM(half_shapes[0], BF16), pltpu.VMEM(half_shapes[1], BF16),
                        pltpu.VMEM((32, GRID_W), F32),
                        pltpu.SemaphoreType.DMA((2,)), pltpu.SemaphoreType.DMA((2,))],
        compiler_params=pltpu.CompilerParams(vmem_limit_bytes=56 << 20))(dwin_b, dwout_b, dbias, dlg)


def _grad_finish_call(sl_in, sl_out, small):
    arrs = (sl_in, sl_out)

    def body(sin, sout, sm, gin, gout, sm_all, h_in, h_out, ssem, rsem, lsem):
        x, y, cc = _mesh_pos()
        me = 4 * x + 2 * y + cc
        sib = (x, y, 1 - cc)
        sls, hs, gs = (sin, sout), (h_in, h_out), (gin, gout)
        sm_send, sm_recv = _all_to_all_small(sm, sm_all, ssem, rsem, 2, x, y, cc)
        sm_own = pltpu.make_async_copy(sm, sm_all.at[me], lsem.at[0])
        for cp in sm_send + [sm_own]:
            cp.start()
        for a in range(2):
            def total(i, carry, a=a):
                rows = pl.ds(pl.multiple_of(i * VROWS, VROWS), VROWS)
                sl = sls[a]
                hs[a][rows, :] = ((sl[0, rows, :].astype(F32) + sl[1, rows, :].astype(F32))
                                  + sl[2, rows, :].astype(F32)) + sl[3, rows, :].astype(F32)
                return carry
            lax.fori_loop(0, arrs[a].shape[1] // VROWS, total, 0)
        mine = [pltpu.make_async_copy(hs[a], gs[a].at[cc], lsem.at[1 + a]) for a in range(2)]
        back = [_remote(hs[a], gs[a].at[cc], ssem, rsem, a, sib) for a in range(2)]
        back_recv = [_remote(hs[a], gs[a].at[1 - cc], ssem, rsem, a, sib) for a in range(2)]
        for cp in mine + back:
            cp.start()
        _finish(mine + [sm_own], back + sm_send, back_recv + sm_recv)

    vmem = pl.BlockSpec(memory_space=pltpu.VMEM)
    return pl.pallas_call(
        body, name="grad_finish",
        in_specs=[vmem] * 3, out_specs=(vmem,) * 3,
        out_shape=(jax.ShapeDtypeStruct((2,) + sl_in.shape[1:], F32),
                   jax.ShapeDtypeStruct((2,) + sl_out.shape[1:], F32),
                   jax.ShapeDtypeStruct((N_DEV,) + small.shape, F32)),
        scratch_shapes=[pltpu.VMEM(sl_in.shape[1:], F32), pltpu.VMEM(sl_out.shape[1:], F32),
                        pltpu.SemaphoreType.DMA((9,)), pltpu.SemaphoreType.DMA((9,)),
                        pltpu.SemaphoreType.DMA((3,))],
        compiler_params=pltpu.CompilerParams(vmem_limit_bytes=48 << 20))(sl_in, sl_out, small)


def _adamw(w, g, m, v):
    m = ADAM_B1 * m + (1.0 - ADAM_B1) * g
    v = ADAM_B2 * v + (1.0 - ADAM_B2) * (g * g)
    m_hat = m / (1.0 - ADAM_B1 ** ADAM_STEP)
    v_hat = v / (1.0 - ADAM_B2 ** ADAM_STEP)
    return -ADAM_LR * (m_hat / (jnp.sqrt(v_hat) + ADAM_EPS) + ADAM_WD * w), m, v


def _adam_call(w, m, v, g, name):
    R, C = w.shape
    tr = 256

    def body(w_ref, m_ref, v_ref, g_ref, go_ref, d_ref, mo_ref, vo_ref):
        g = g_ref[...]
        go_ref[...] = g
        d_ref[...], mo_ref[...], vo_ref[...] = _adamw(w_ref[...], g, m_ref[...], v_ref[...])

    spec = pl.BlockSpec((tr, C), lambda i: (i, 0))
    return pl.pallas_call(
        body, name=name, grid=(R // tr,), in_specs=[spec] * 4,
        out_specs=(spec,) * 4, out_shape=(jax.ShapeDtypeStruct((R, C), F32),) * 4,
        compiler_params=_params(("arbitrary",)))(w, m, v, g)


R_GF, R_NG, R_LOSS, R_RNG, R_LGF, R_LGB, R_SHIFT, R_SCALE, R_GATE, R_SHIFT_C, R_SCALE_C, R_RNG2, R_RPB = (
    0, 1, 2, 3, 4, 5, 6, 8, 10, 12, 13, 14, 16)
W_GF, W_NG, W_CCTX, W_RNG, W_DF, W_DB, W_BADA, W_RPB = 0, 1, 2, 3, 4, 5, 6, 9


SMALL = (("final_norm_g", W_GF, 1, D), ("norm_g", W_NG, 1, D), ("c_ctx", W_CCTX, 1, D),
         ("ret_norm_g", W_RNG, 1, 512), ("ret_decay_fwd", W_DF, 1, 4), ("ret_decay_bwd", W_DB, 1, 4),
         ("b_ada", W_BADA, 3, D), ("na_rpb", W_RPB, 4, D))
N_SMALL = len(SMALL)


def _small_final_call(sm_all, c_t, wada_f, wada, m_ada, v_ada, small_w, small_m, small_v, B):
    ws = wada.shape[1]
    NB = N_DEV * B

    def body(*refs):
        sm_ref, ct_ref, wf_ref, wa_ref, ma_ref, va_ref = refs[:6]
        ins = refs[6:6 + 3 * N_SMALL]
        outs = refs[6 + 3 * N_SMALL:6 + 7 * N_SMALL]
        ga_ref, da_ref, mao_ref, vao_ref, loss_ref, dmod_ref, pk_ref = refs[6 + 7 * N_SMALL:]
        x, y, _ = _mesh_pos()
        s = 2 * x + y
        tot = sm_ref[0]
        for dv in range(1, N_DEV):
            tot = tot + sm_ref[dv]
        pk_ref[...] = jnp.zeros_like(pk_ref)
        for kind in range(3):
            for i, (_, row, nrow, width) in enumerate(SMALL):
                ref = ins[kind * N_SMALL + i]
                if nrow == 3:
                    for part in range(3):
                        pk_ref[kind, row + part:row + part + 1, :] = ref[:, part * D:(part + 1) * D]
                else:
                    pk_ref[kind, row:row + nrow, 0:width] = ref[...]
        w = pk_ref[0]
        cctx_ref = ins[2]
        for dv in range(N_DEV):
            for b in range(B):
                r = dv * B + b
                for part, row in enumerate((R_SHIFT, R_SCALE, R_GATE)):
                    dmod_ref[r:r + 1, part * D:(part + 1) * D] = sm_ref[dv, row + b:row + b + 1, :]
        dmod_ref[NB:NB + 1, 0:D] = tot[R_SHIFT_C:R_SHIFT_C + 1, :]
        dmod_ref[NB:NB + 1, D:2 * D] = tot[R_SCALE_C:R_SCALE_C + 1, :]
        dmod_ref[NB:NB + 1, 2 * D:3 * D] = jnp.zeros((1, D), F32)
        dmod_ref[NB + 1:, :] = jnp.zeros((dmod_ref.shape[0] - NB - 1, 3 * D), F32)
        dmod = dmod_ref[...]
        cc = cctx_ref[...]
        scc = _sigmoid(cc)
        ct = ct_ref[...]
        act_t = ct * _sigmoid(ct)
        dmc = dmod[NB:NB + 1, :].astype(BF16)
        dact = jnp.zeros((1, D), F32)
        for sh in range(N_SHARD):
            dact = dact + _dot_nt(dmc[:, sh * ws:(sh + 1) * ws], wf_ref[sh])
        g = jnp.zeros((16, D), F32)
        rows = lax.broadcasted_iota(jnp.int32, (16, D), 0)

        def put(g, row, val):
            return jnp.where(rows == row, val, g)

        g = put(g, W_GF, tot[R_GF:R_GF + 1, :])
        g = put(g, W_NG, tot[R_NG:R_NG + 1, :])
        g = put(g, W_CCTX, dact * (scc * (1.0 + cc * (1.0 - scc))))
        g = put(g, W_RNG, tot[R_RNG:R_RNG + 1, :] + tot[R_RNG2:R_RNG2 + 1, :])
        g = put(g, W_DF, tot[R_LGF:R_LGF + 1, :] * (-jnp.exp(w[W_DF:W_DF + 1, :])))
        g = put(g, W_DB, tot[R_LGB:R_LGB + 1, :] * (-jnp.exp(w[W_DB:W_DB + 1, :])))
        db = jnp.sum(dmod, axis=0, keepdims=True)
        for part in range(3):
            g = put(g, W_BADA + part, db[:, part * D:(part + 1) * D])
        for part in range(4):
            g = put(g, W_RPB + part, tot[R_RPB + part:R_RPB + part + 1, :])
        for kind, val in enumerate((g,) + _adamw(w, g, pk_ref[1], pk_ref[2])):
            for i, (_, row, nrow, width) in enumerate(SMALL):
                out = outs[kind * N_SMALL + i]
                if nrow == 3:
                    for part in range(3):
                        out[:, part * D:(part + 1) * D] = val[row + part:row + part + 1, :]
                else:
                    out[...] = val[row:row + nrow, 0:width]
        loss_ref[...] = jnp.broadcast_to(
            (0.5 / D) * jnp.sum(tot[R_LOSS:R_LOSS + 1, :], axis=1, keepdims=True), (8, 128))
        for sh in range(N_SHARD):
            @pl.when(s == sh)
            def _():
                ga = jnp.dot(act_t, dmod[:, sh * ws:(sh + 1) * ws], precision=HIGHEST,
                             preferred_element_type=F32)
                ga_ref[...] = ga
                da_ref[...], mao_ref[...], vao_ref[...] = _adamw(wa_ref[...], ga, ma_ref[...], va_ref[...])

    sh_small = tuple(jax.ShapeDtypeStruct(a.shape, F32) for a in small_w)
    sh_ada = jax.ShapeDtypeStruct(wada.shape, F32)
    res = pl.pallas_call(
        body, name="small_final",
        out_shape=sh_small * 4 + (sh_ada,) * 4 + (jax.ShapeDtypeStruct((8, 128), F32),),
        scratch_shapes=[pltpu.VMEM((NB + 8, 3 * D), F32), pltpu.VMEM((3, 16, D), F32)],
        compiler_params=_params(vmem_mb=56))(
            sm_all, c_t, wada_f, wada, m_ada, v_ada, *small_w, *small_m, *small_v)
    smalls = [res[k * N_SMALL:(k + 1) * N_SMALL] for k in range(4)]
    return smalls, res[4 * N_SMALL:4 * N_SMALL + 4], res[4 * N_SMALL + 4]


def _local_step(order, x, c, ctx, c_ctx, norm_g, wada_f, b_ada, win_b, bias, dec_f, dec_b, ret_norm_g,
                wout_f, final_g, target):
    B, L, _ = x.shape
    LC = ctx.shape[1]
    assert B == 2
    cos2, sin2 = _rope_tables(L, LC)
    c8 = jnp.concatenate([c, c_ctx[None, :], jnp.zeros((8 - B - 1, D), F32)], axis=0)
    mod = _mod_call(c8, wada_f, b_ada)
    P, h, win_f = _inproj_gather_call(order, x, ctx, mod, norm_g, win_b, cos2, sin2)
    y_na, o_na = _na_fwd_call(P, bias, L, LC)
    sf, sb = _ret_states_call(P, dec_f, dec_b, L, LC)
    y_ret, o_ret = _retc_fwd_call(P, sf, sb, dec_f, dec_b, ret_norm_g, L)
    dY, dx2, dwout_p, sm_out = _out_call(y_na, y_ret, x, target, mod, final_g, wout_f.reshape(D, D))
    dnq, dng, dnk, dnv, dbias = _na_bwd_call(P, bias, dY, o_na, L, LC)
    drq, drg, drk, drv, dgn, dlg = _retc_bwd_call(P, sf, sb, dec_f, dec_b, ret_norm_g, o_ret, dY, cos2, sin2, L, LC)
    dsec = (dnq, dnk, dnv, dng, drq, drk, drv, drg)
    dwin_b = _dw_call(dsec, h, L)
    cp_in, cp_out, drpb, dlg_sum = _grad_halves_call(
        dwin_b, dwout_p.astype(BF16).reshape(N_SHARD, D // N_SHARD, D), dbias, dlg)
    grad_x, sm_dh, sl_in, sl_out = _dh_call(dsec, win_f, x, ctx, dx2, mod, norm_g, cp_in, cp_out)
    z = jnp.zeros((1, D), F32)
    pad = lambda v: jnp.pad(v.reshape(1, -1), ((0, 0), (0, D - v.size)))
    dlg_sum = dlg_sum.reshape(4, 8, 128)
    rpb_rows = jnp.pad(drpb[:, :15, :31].reshape(-1), (0, 4 * D - drpb.shape[0] * 465)).reshape(4, D)
    small = jnp.concatenate([
        sm_out[0:1], sm_dh[0:1], sm_out[1:2], pad(dgn[0]), pad(dlg_sum[:, 0, 0]), pad(dlg_sum[:, 1, 0]),
        sm_dh[3:5], sm_dh[5:7], sm_out[2:4], sm_dh[1:2], sm_dh[2:3], pad(dgn[1]), z, rpb_rows,
        jnp.zeros((SM_ROWS - 20, D), F32)], axis=0)
    return grad_x, sl_in, sl_out, small


def kernel(x, c, ctx, c_ctx, norm_g, w_ada, b_ada, w_in, na_rpb, ret_decay_fwd, ret_decay_bwd, ret_norm_g, w_out, final_norm_g, loss_target, m_c_ctx, m_norm_g, m_w_ada, m_b_ada, m_w_in, m_na_rpb, m_ret_decay_fwd, m_ret_decay_bwd, m_ret_norm_g, m_w_out, m_final_norm_g, v_c_ctx, v_norm_g, v_w_ada, v_b_ada, v_w_in, v_na_rpb, v_ret_decay_fwd, v_ret_decay_bwd, v_ret_norm_g, v_w_out, v_final_norm_g):
    B = x.shape[0]
    wout_f, wada_f, c_all, bias = _gather_call(
        w_out[0].astype(BF16), w_ada[0].astype(BF16), c, na_rpb[0].reshape(na_rpb.shape[1], -1))
    mx, my = lax.axis_index("x"), lax.axis_index("y")
    order = jnp.stack([2 * mx + my, 2 * (1 - mx) + my, 2 * mx + (1 - my),
                       2 * (1 - mx) + (1 - my)]).astype(jnp.int32)
    grad_x, sl_in, sl_out, small = _local_step(
        order, x, c, ctx, c_ctx, norm_g, wada_f, b_ada, w_in[0].astype(BF16), bias, ret_decay_fwd,
        ret_decay_bwd, ret_norm_g, wout_f, final_norm_g.reshape(1, D), loss_target)
    gin, gout, sm_all = _grad_finish_call(sl_in, sl_out, small)
    g_win, d_win, nm_win, nv_win = _adam_call(
        w_in[0], m_w_in[0], v_w_in[0], gin.reshape(w_in.shape[1:]), "adam_w_in")
    g_wout, d_wout, nm_wout, nv_wout = _adam_call(
        w_out[0], m_w_out[0], v_w_out[0], gout.reshape(w_out.shape[1:]), "adam_w_out")

    def small_inputs(gf, ng, cc, rng, df, db, bada, rpb):
        return (gf.reshape(1, D), ng, cc.reshape(1, D), rng, df, db, bada,
                jnp.pad(rpb.reshape(-1), (0, 4 * D - rpb.size)).reshape(4, D))

    c_t = jnp.concatenate([c_all.reshape(N_DEV * B, D), c_ctx.reshape(1, D), jnp.zeros((7, D), F32)], axis=0).T
    smalls, adas, loss = _small_final_call(
        sm_all, c_t, wada_f, w_ada[0], m_w_ada[0], v_w_ada[0],
        small_inputs(final_norm_g, norm_g, c_ctx, ret_norm_g, ret_decay_fwd, ret_decay_bwd, b_ada, na_rpb),
        small_inputs(m_final_norm_g, m_norm_g, m_c_ctx, m_ret_norm_g, m_ret_decay_fwd, m_ret_decay_bwd, m_b_ada,
                     m_na_rpb),
        small_inputs(v_final_norm_g, v_norm_g, v_c_ctx, v_ret_norm_g, v_ret_decay_fwd, v_ret_decay_bwd, v_b_ada,
                     v_na_rpb), B)
    res = []
    for p, ada, win_o, wout_o in zip(smalls, adas, (g_win, d_win, nm_win, nv_win),
                                     (g_wout, d_wout, nm_wout, nv_wout)):
        gf, ng, cc, rng, df, db, bada, rpb = p
        res.append([cc.reshape(D), ng, ada[None], bada, win_o[None],
                    rpb.reshape(-1)[:na_rpb.size].reshape(na_rpb.shape), df, db, rng, wout_o[None], gf.reshape(D)])
    return (loss[0, 0], grad_x, *res[0], *res[1], *res[2], *res[3])
```

```python
import numpy as np
import jax
import jax.numpy as jnp
from jax import lax
from jax.experimental import pallas as pl
from jax.experimental.pallas import tpu as pltpu

F32 = jnp.float32
BF16 = jnp.bfloat16
HIGHEST = lax.Precision.HIGHEST

D = 1024
GRID_W = 64
NA_DH = 64
RET_DK = 128
ROPE_BASE = 10000.0
EPS = 1e-6
NEG = -1e30
TQ = 256
KW = 12 * GRID_W
N_SHARD = 4
N_DEV = 8
SM_ROWS = 24

ADAM_LR = 0.001
ADAM_B1 = 0.9
ADAM_B2 = 0.999
ADAM_EPS = 1e-08
ADAM_WD = 0.01
ADAM_STEP = 10

MESH = pl.DeviceIdType.MESH
ANY = pl.BlockSpec(memory_space=pl.ANY)


def _params(sem=None, vmem_mb=48):
    return pltpu.CompilerParams(dimension_semantics=sem, vmem_limit_bytes=vmem_mb << 20)


def _dot(a, b):
    return jnp.dot(a, b, preferred_element_type=F32)


def _dot_nt(a, b):
    return lax.dot_general(a, b, (((1,), (1,)), ((), ())), preferred_element_type=F32)


def _dot_tn(a, b):
    return lax.dot_general(a, b, (((0,), (0,)), ((), ())), preferred_element_type=F32)


def _sigmoid(x):
    return 1.0 / (1.0 + jnp.exp(-x))


def _rope_tables(L, LC):
    half = RET_DK // 2
    nf = half // 2
    t = np.arange(L)
    row = (t // GRID_W).astype(np.float32)
    col = (t % GRID_W).astype(np.float32)
    inv = (np.float32(ROPE_BASE) ** (-np.arange(nf, dtype=np.float32) / np.float32(nf))).astype(np.float32)
    ang = np.concatenate([row[:, None] * inv, col[:, None] * inv], axis=-1).astype(np.float32)
    cos, sin = np.cos(ang).astype(np.float32), np.sin(ang).astype(np.float32)
    cos2 = np.concatenate([cos, cos], axis=-1)
    sin2 = np.concatenate([-sin, sin], axis=-1)
    cos2 = np.concatenate([cos2, np.ones((LC, RET_DK), np.float32)], axis=0)
    sin2 = np.concatenate([sin2, np.zeros((LC, RET_DK), np.float32)], axis=0)
    return jnp.asarray(cos2), jnp.asarray(sin2)


def _mod_call(c8, wada_f, b_ada):
    ws = wada_f.shape[2]

    def body(c_ref, w_ref, b_ref, o_ref):
        a = c_ref[...]
        a = (a * _sigmoid(a)).astype(BF16)
        for s in range(N_SHARD):
            o_ref[:, s * ws:(s + 1) * ws] = _dot(a, w_ref[s]) + b_ref[:, s * ws:(s + 1) * ws]

    return pl.pallas_call(
        body, name="ada_mod", out_shape=jax.ShapeDtypeStruct((8, 3 * D), F32),
        compiler_params=_params())(c8, wada_f, b_ada)


def _dc_masks():
    cq = lax.broadcasted_iota(jnp.int32, (GRID_W, GRID_W), 0)
    ck = lax.broadcasted_iota(jnp.int32, (GRID_W, GRID_W), 1)
    dc = jnp.clip(ck - cq + 15, 0, 30)
    c0 = jnp.clip(cq - 8, 0, GRID_W - 16)
    col_ok = (ck >= c0) & (ck < c0 + 16)
    return dc, col_ok


def _bias_blocks():
    out = []
    for typ, delta in enumerate((4, 0, -4)):
        for rq in range(4):
            for rkk in range(12):
                dr = rkk + delta - rq - 4
                if typ == 0:
                    ok = -rq <= dr <= 7 - rq
                elif typ == 1:
                    ok = -4 <= dr <= 3
                else:
                    ok = -4 - rq <= dr <= 3 - rq
                out.append((typ, rq, rkk, dr if ok else None))
    return out


def _bias_body(r_ref, bias_ref, et_ref):
    dc, col_ok = _dc_masks()
    masks = [(dc == j).astype(F32) for j in range(31)]

    def per_h(h, carry):
        for dr in range(15):
            t = jnp.zeros((GRID_W, GRID_W), F32)
            for j in range(31):
                t = t + masks[j] * r_ref[h, dr * 31 + j]
            et_ref[dr] = jnp.where(col_ok, t, NEG)
        neg = jnp.full((GRID_W, GRID_W), NEG, F32)
        for typ, rq, rkk, dr in _bias_blocks():
            blk = neg if dr is None else et_ref[dr + 7]
            bias_ref[h, typ, rq * 64:(rq + 1) * 64, rkk * 64:(rkk + 1) * 64] = blk
        return carry

    lax.fori_loop(0, bias_ref.shape[0], per_h, 0)


def _bias_tile_sums(db_ref, hh):
    acc = {}
    for typ, rq, rkk, dr in _bias_blocks():
        if dr is None:
            continue
        blk = db_ref[hh, typ, rq * 64:(rq + 1) * 64, rkk * 64:(rkk + 1) * 64]
        acc[dr] = blk if dr not in acc else acc[dr] + blk
    return acc


def _small_reduce_body(dt_ref, dlg_ref, drpb_ref, dlgo_ref, p_ref):
    dc, _ = _dc_masks()
    masks = [(dc == j).astype(F32) for j in range(31)]
    ones = jnp.ones((8, GRID_W), F32)
    p_ref[...] = jnp.zeros_like(p_ref)
    drpb_ref[...] = jnp.zeros_like(drpb_ref)

    def per_h(h, carry):
        for dr in range(-7, 8):
            t = dt_ref[h, dr + 7]
            for j in range(31):
                p_ref[j:j + 1, :] = jnp.sum(t * masks[j], axis=0, keepdims=True)
            red = lax.dot_general(ones, p_ref[...], (((1,), (1,)), ((), ())),
                                  precision=HIGHEST, preferred_element_type=F32)
            drpb_ref[h, dr + 7:dr + 8, :] = red[0:1, :]
        return carry

    lax.fori_loop(0, dt_ref.shape[0], per_h, 0)
    x = dlg_ref[0]
    for b in range(1, dlg_ref.shape[0]):
        x = x + dlg_ref[b]
    x = x.reshape(4 * 8, x.shape[-1])
    dlgo_ref[...] = jnp.dot(x, jnp.ones((x.shape[-1], 128), F32), precision=HIGHEST,
                            preferred_element_type=F32)


def _inproj_gather_call(order, x, ctx, mod, norm_g, win_b, cos2, sin2):
    B, L, _ = x.shape
    LC = ctx.shape[1]
    T = L + LC
    nl, nt = L // TQ, T // TQ
    assert LC == TQ and L % TQ == 0
    kscale = RET_DK ** -0.5
    HR = D // 2

    def body(ord_ref, x_ref, ctx_ref, mod_ref, g_ref, wown_ref, cos_ref, sin_ref, p_ref, h_ref, wf_ref,
             w_all, hs_ref, ssem, rsem, lsem):
        j, b, t = pl.program_id(0), pl.program_id(1), pl.program_id(2)
        first = (b == 0) & (t == 0)
        mx, my, mc = _mesh_pos()
        s = 2 * mx + my
        sib = (mx, my, 1 - mc)
        own = pltpu.make_async_copy(wown_ref, w_all.at[s], lsem.at[0])
        ici_send, ici_recv, fwd_send, fwd_recv, outs = [], [], [], [], [
            pltpu.make_async_copy(w_all.at[s], wf_ref.at[s], lsem.at[1])]
        for k, (px, py) in enumerate(_other_chips(mx, my)):
            ps = 2 * px + py
            mine = w_all.at[s, pl.ds(mc * HR, HR)]
            ici_send.append(_remote(mine, w_all.at[s, pl.ds(mc * HR, HR)], ssem, rsem, k, (px, py, mc)))
            ici_recv.append(_remote(mine, w_all.at[ps, pl.ds(mc * HR, HR)], ssem, rsem, k, (px, py, mc)))
            got = w_all.at[ps, pl.ds(mc * HR, HR)]
            fwd_send.append(_remote(got, got, ssem, rsem, 3 + k, sib))
            theirs = w_all.at[ps, pl.ds((1 - mc) * HR, HR)]
            fwd_recv.append(_remote(theirs, theirs, ssem, rsem, 3 + k, sib))
            outs.append(pltpu.make_async_copy(w_all.at[ps], wf_ref.at[ps], lsem.at[2 + k]))

        @pl.when(first & (j == 0))
        def _():
            own.start()
            own.wait()
            ici_send[0].start()
            ici_send[1].start()
            outs[0].start()

        for k in range(3):
            @pl.when(first & (j == k + 1))
            def _(k=k):
                ici_recv[k].wait_recv()
                if k == 0:
                    ici_send[2].start()
                fwd_send[k].start()
                fwd_recv[k].wait_recv()
                outs[1 + k].start()

        tile = b * nt + t

        @pl.when(j == 0)
        def _():
            is_lat = t < nl
            xt = jnp.where(is_lat, x_ref[...], ctx_ref[...])
            mrow = mod_ref[pl.ds(jnp.where(is_lat, b, B), 1), :]
            shift, scale = mrow[:, 0:D], mrow[:, D:2 * D]
            rstd = lax.rsqrt(jnp.mean(xt * xt, axis=-1, keepdims=True) + EPS)
            h0 = ((xt * rstd * g_ref[...]) * (1.0 + scale) + shift).astype(BF16)
            h_ref[...] = h0
            hs_ref[tile] = h0

        hb = hs_ref[tile]
        cs, sn = cos_ref[...], sin_ref[...]
        shard = ord_ref[j]
        for sh in range(N_SHARD):
            @pl.when(shard == sh)
            def _(sh=sh):
                for half in range(2):
                    sec = 2 * sh + half
                    acc = _dot(hb, w_all[sh, :, half * 512:(half + 1) * 512])
                    if sec == 0:
                        acc = acc * (NA_DH ** -0.5)
                    if sec in (4, 5):
                        for q in range(4):
                            a = acc[:, q * 128:(q + 1) * 128]
                            r = a * cs + pltpu.roll(a, 64, 1) * sn
                            if sec == 5:
                                r = r * kscale
                            p_ref[:, half * 512 + q * 128:half * 512 + (q + 1) * 128] = r.astype(BF16)
                    else:
                        p_ref[:, half * 512:(half + 1) * 512] = acc.astype(BF16)

        @pl.when((j == N_SHARD - 1) & (b == B - 1) & (t == nt - 1))
        def _():
            _finish(outs, ici_send + fwd_send, [])

    tok = lambda j, b, t, o: (jnp.where(j == 0, b, B - 1), jnp.where(j == 0, jnp.minimum(t, nl - 1), nl - 1), 0)
    grid_spec = pltpu.PrefetchScalarGridSpec(
        num_scalar_prefetch=1, grid=(N_SHARD, B, nt),
        in_specs=[
            pl.BlockSpec((None, TQ, D), tok),
            pl.BlockSpec((None, TQ, D), lambda j, b, t, o: (jnp.where(j == 0, b, B - 1), 0, 0)),
            pl.BlockSpec((8, 3 * D), lambda j, b, t, o: (0, 0)),
            pl.BlockSpec((1, D), lambda j, b, t, o: (0, 0)),
            ANY,
            pl.BlockSpec((TQ, RET_DK), lambda j, b, t, o: (t, 0)),
            pl.BlockSpec((TQ, RET_DK), lambda j, b, t, o: (t, 0)),
        ],
        out_specs=(pl.BlockSpec((None, TQ, D), lambda j, b, t, o: (b, t, o[j])),
                   pl.BlockSpec((None, TQ, D), lambda j, b, t, o: (
                       jnp.where(j == 0, b, B - 1), jnp.where(j == 0, t, nt - 1), 0)), ANY),
        scratch_shapes=[pltpu.VMEM((N_SHARD, D, D), BF16), pltpu.VMEM((B * nt, TQ, D), BF16),
                        pltpu.SemaphoreType.DMA((6,)), pltpu.SemaphoreType.DMA((6,)),
                        pltpu.SemaphoreType.DMA((5,))])
    return pl.pallas_call(
        body, name="in_proj", grid_spec=grid_spec,
        out_shape=(jax.ShapeDtypeStruct((B, T, 4 * D), BF16), jax.ShapeDtypeStruct((B, T, D), BF16),
                   jax.ShapeDtypeStruct((N_SHARD, D, D), BF16)),
        compiler_params=_params(("arbitrary",) * 3))(order, x, ctx, mod, norm_g, win_b, cos2, sin2)


def _na_specs(L, T, rows, nh=2):
    nm = rows // 4
    w = nh * NA_DH
    per = 512 // w
    q_spec = pl.BlockSpec((None, TQ, w), lambda hp, b, m: (b, m, hp))
    k_spec = pl.BlockSpec((None, T, w), lambda hp, b, m: (b, 0, per + hp))
    v_spec = pl.BlockSpec((None, T, w), lambda hp, b, m: (b, 0, 2 * per + hp))
    g_spec = pl.BlockSpec((None, TQ, w), lambda hp, b, m: (b, m, 3 * per + hp))
    bias_spec = pl.BlockSpec((nh, 3, TQ, KW), lambda hp, b, m: (hp, 0, 0, 0))
    return nm, q_spec, k_spec, v_spec, g_spec, bias_spec


def _na_tile(m, nm, rows):
    typ = jnp.where(m == 0, 0, jnp.where(m == nm - 1, 2, 1))
    start = pl.multiple_of(jnp.clip(4 * m - 4, 0, rows - 12) * GRID_W, TQ)
    return typ, start


def _na_fwd_call(P, bias, L, LC):
    B, T, _ = P.shape
    rows = L // GRID_W
    NH = 4
    nm, q_spec, k_spec, v_spec, g_spec, bias_spec = _na_specs(L, T, rows, NH)

    def body(q_ref, k_ref, v_ref, g_ref, bias_ref, y_ref, o_ref):
        typ, start = _na_tile(pl.program_id(2), nm, rows)
        for hh in range(NH):
            ln = slice(hh * NA_DH, (hh + 1) * NA_DH)
            q = q_ref[:, ln]
            kw, vw = k_ref[pl.ds(start, KW), ln], v_ref[pl.ds(start, KW), ln]
            kc, vc = k_ref[L:L + LC, ln], v_ref[L:L + LC, ln]
            s1 = _dot_nt(q, kw) + bias_ref[hh, typ]
            s2 = _dot_nt(q, kc)
            mx = jnp.maximum(jnp.max(s1, axis=-1, keepdims=True), jnp.max(s2, axis=-1, keepdims=True))
            p1, p2 = jnp.exp(s1 - mx), jnp.exp(s2 - mx)
            inv = 1.0 / (jnp.sum(p1, axis=-1, keepdims=True) + jnp.sum(p2, axis=-1, keepdims=True))
            o = (_dot(p1.astype(BF16), vw) + _dot(p2.astype(BF16), vc)) * inv
            g = g_ref[:, ln].astype(F32)
            o_ref[:, ln] = o.astype(BF16)
            y_ref[:, ln] = (o * (g * _sigmoid(g))).astype(BF16)

    tile = pl.BlockSpec((None, TQ, NH * NA_DH), lambda hp, b, m: (b, m, hp))
    return pl.pallas_call(
        body, name="na_fwd", grid=(8 // NH, B, nm),
        in_specs=[q_spec, k_spec, v_spec, g_spec, bias_spec],
        out_specs=(tile, tile),
        out_shape=(jax.ShapeDtypeStruct((B, L, 512), BF16),) * 2,
        compiler_params=_params(("arbitrary",) * 3))(P, P, P, P, bias)


def _na_bwd_call(P, bias, dY, o_na, L, LC):
    B, T, _ = P.shape
    rows = L // GRID_W
    NH = 4
    W = NH * NA_DH
    nm, q_spec, k_spec, v_spec, g_spec, bias_spec = _na_specs(L, T, rows, NH)
    scale = NA_DH ** -0.5

    RB = 32

    def body(q_ref, k_ref, v_ref, g_ref, bias_ref, dy_ref, o_ref, dq_ref, dg_ref, dk_ref, dv_ref, dt_ref,
             db_ref, s1_ref, s2_ref, dp1_ref, dp2_ref, p1_ref, p2_ref, ds1_ref, ds2_ref, dkt_ref, dvt_ref):
        b, m = pl.program_id(1), pl.program_id(2)
        typ, start = _na_tile(m, nm, rows)

        @pl.when(m == 0)
        def _():
            dkt_ref[...] = jnp.zeros_like(dkt_ref)
            dvt_ref[...] = jnp.zeros_like(dvt_ref)

        @pl.when((m == 0) & (b == 0))
        def _():
            db_ref[...] = jnp.zeros_like(db_ref)

        for hh in range(NH):
            ln = slice(hh * NA_DH, (hh + 1) * NA_DH)
            q = q_ref[:, ln]
            kw, vw = k_ref[pl.ds(start, KW), ln], v_ref[pl.ds(start, KW), ln]
            kc, vc = k_ref[L:L + LC, ln], v_ref[L:L + LC, ln]
            g = g_ref[:, ln].astype(F32)
            sg = _sigmoid(g)
            dy = dy_ref[:, ln].astype(F32)
            do = (dy * (g * sg)).astype(BF16)
            s1_ref[hh] = _dot_nt(q, kw)
            s2_ref[hh] = _dot_nt(q, kc)
            dp1_ref[hh] = _dot_nt(do, vw)
            dp2_ref[hh] = _dot_nt(do, vc)

            def rows_pass(r, carry, hh=hh):
                rw = pl.ds(pl.multiple_of(r * RB, RB), RB)
                a = s1_ref[hh, rw, :] + bias_ref[hh, typ, rw, :]
                c = s2_ref[hh, rw, :]
                mx = jnp.maximum(jnp.max(a, axis=-1, keepdims=True), jnp.max(c, axis=-1, keepdims=True))
                e1, e2 = jnp.exp(a - mx), jnp.exp(c - mx)
                inv = 1.0 / (jnp.sum(e1, axis=-1, keepdims=True) + jnp.sum(e2, axis=-1, keepdims=True))
                p1, p2 = e1 * inv, e2 * inv
                p1_ref[hh, rw, :] = p1.astype(BF16)
                p2_ref[hh, rw, :] = p2.astype(BF16)
                dp1, dp2 = dp1_ref[hh, rw, :], dp2_ref[hh, rw, :]
                delta = jnp.sum(p1 * dp1, axis=-1, keepdims=True) + jnp.sum(p2 * dp2, axis=-1, keepdims=True)
                ds1 = p1 * (dp1 - delta)
                db_ref[hh, typ, rw, :] += ds1
                ds1_ref[hh, rw, :] = ds1.astype(BF16)
                ds2_ref[hh, rw, :] = (p2 * (dp2 - delta)).astype(BF16)
                return carry

            lax.fori_loop(0, TQ // RB, rows_pass, 0, unroll=True)
            p1b, p2b, ds1b, ds2b = p1_ref[hh], p2_ref[hh], ds1_ref[hh], ds2_ref[hh]
            dg_ref[:, ln] = (dy * o_ref[:, ln].astype(F32) * (sg * (1.0 + g * (1.0 - sg)))).astype(BF16)
            dq_ref[:, ln] = ((_dot(ds1b, kw) + _dot(ds2b, kc)) * scale).astype(BF16)
            dkt_ref[ln, pl.ds(start, KW)] += _dot_tn(q, ds1b)
            dvt_ref[ln, pl.ds(start, KW)] += _dot_tn(do, p1b)
            dkt_ref[ln, L:L + LC] += _dot_tn(q, ds2b)
            dvt_ref[ln, L:L + LC] += _dot_tn(do, p2b)

        @pl.when(m == nm - 1)
        def _():
            dk_ref[...] = dkt_ref[...].T
            dv_ref[...] = dvt_ref[...].T

        @pl.when((m == nm - 1) & (b == B - 1))
        def _():
            for hh in range(NH):
                for dr, t in _bias_tile_sums(db_ref, hh).items():
                    dt_ref[hh, dr + 7] = t

    tile = pl.BlockSpec((None, TQ, W), lambda hp, b, m: (b, m, hp))
    kv_out = pl.BlockSpec((None, T, W), lambda hp, b, m: (b, 0, hp))
    wide, narrow = (NH, TQ, KW), (NH, TQ, LC)
    return pl.pallas_call(
        body, name="na_bwd", grid=(8 // NH, B, nm),
        in_specs=[q_spec, k_spec, v_spec, g_spec, bias_spec, tile, tile],
        out_specs=(tile, tile, kv_out, kv_out,
                   pl.BlockSpec((NH, 15, GRID_W, GRID_W), lambda hp, b, m: (hp, 0, 0, 0))),
        out_shape=(jax.ShapeDtypeStruct((B, L, 512), BF16), jax.ShapeDtypeStruct((B, L, 512), BF16),
                   jax.ShapeDtypeStruct((B, T, 512), F32), jax.ShapeDtypeStruct((B, T, 512), F32),
                   jax.ShapeDtypeStruct((bias.shape[0], 15, GRID_W, GRID_W), F32)),
        scratch_shapes=[pltpu.VMEM((NH,) + bias.shape[1:], F32),
                        pltpu.VMEM(wide, F32), pltpu.VMEM(narrow, F32), pltpu.VMEM(wide, F32), pltpu.VMEM(narrow, F32),
                        pltpu.VMEM(wide, BF16), pltpu.VMEM(narrow, BF16), pltpu.VMEM(wide, BF16),
                        pltpu.VMEM(narrow, BF16), pltpu.VMEM((W, T), F32), pltpu.VMEM((W, T), F32)],
        compiler_params=_params(("arbitrary",) * 3, vmem_mb=60))(P, P, P, P, bias, dY, o_na)


def _head_scalar(dec_ref, h):
    lane = lax.broadcasted_iota(jnp.int32, dec_ref.shape, 1)
    return -jnp.sum(jnp.where(lane == h, jnp.exp(dec_ref[...]), 0.0), axis=1, keepdims=True)


def _chunk_decay(lgf, lgb):
    tau = lax.broadcasted_iota(jnp.int32, (TQ, 1), 0).astype(F32)
    sig = lax.broadcasted_iota(jnp.int32, (1, TQ), 1).astype(F32)
    dist = tau - sig
    dm = jnp.exp(dist * jnp.where(dist > 0, lgf, -lgb)) * jnp.where(dist == 0, 2.0, 1.0)
    return tau, dist, dm


def _ret_states_call(P, dec_f, dec_b, L, LC):
    B, T, _ = P.shape
    n = L // TQ

    def body(df_ref, db_ref, k_ref, v_ref, sf_ref, sb_ref):
        h = pl.program_id(1)
        lgf, lgb = _head_scalar(df_ref, h), _head_scalar(db_ref, h)
        tau = lax.broadcasted_iota(jnp.int32, (TQ, 1), 0).astype(F32)
        jc = lax.broadcasted_iota(jnp.int32, (LC, 1), 0).astype(F32)
        wf, wb = jnp.exp(lgf * (TQ - 1.0 - tau)), jnp.exp(lgb * tau)
        gcf, gcb = jnp.exp(lgf * float(TQ)), jnp.exp(lgb * float(TQ))
        kc, vc = k_ref[L:L + LC, :].astype(F32), v_ref[L:L + LC, :]

        def chunk_state(i, w):
            ks = pl.multiple_of(i * TQ, TQ)
            return _dot_tn((k_ref[pl.ds(ks, TQ), :].astype(F32) * w).astype(BF16), v_ref[pl.ds(ks, TQ), :])

        def fwd(i, s):
            sf_ref[i] = s
            return gcf * s + chunk_state(i, wf)

        lax.fori_loop(0, n, fwd, _dot_tn((kc * jnp.exp(lgf * (LC - 1.0 - jc))).astype(BF16), vc), unroll=True)

        def bwd(r, s):
            i = n - 1 - r
            sb_ref[i] = s
            return gcb * s + chunk_state(i, wb)

        lax.fori_loop(0, n, bwd, _dot_tn((kc * jnp.exp(lgb * jc)).astype(BF16), vc), unroll=True)

    st = pl.BlockSpec((None, None, n, RET_DK, RET_DK), lambda b, h: (b, h, 0, 0, 0))
    return pl.pallas_call(
        body, name="ret_states", grid=(B, 4),
        in_specs=[pl.BlockSpec((1, 4), lambda b, h: (0, 0)), pl.BlockSpec((1, 4), lambda b, h: (0, 0)),
                  pl.BlockSpec((None, T, 128), lambda b, h: (b, 0, 20 + h)),
                  pl.BlockSpec((None, T, 128), lambda b, h: (b, 0, 24 + h))],
        out_specs=(st, st),
        out_shape=(jax.ShapeDtypeStruct((B, 4, n, RET_DK, RET_DK), F32),) * 2,
        compiler_params=_params(("arbitrary",) * 2))(dec_f, dec_b, P, P)


def _retc_fwd_call(P, sf, sb, dec_f, dec_b, ret_norm_g, L):
    B, T, _ = P.shape
    sec = lambda k: pl.BlockSpec((None, TQ, 512), lambda b, i: (b, i, k))
    dec_spec = pl.BlockSpec((1, 4), lambda b, i: (0, 0))
    st_spec = pl.BlockSpec((None, 4, None, RET_DK, RET_DK), lambda b, i: (b, 0, i, 0, 0))

    def body(df_ref, db_ref, q_ref, k_ref, v_ref, g_ref, gn_ref, sf_ref, sb_ref, y_ref, o_ref):
        for h in range(4):
            ln = slice(h * RET_DK, (h + 1) * RET_DK)
            lgf, lgb = _head_scalar(df_ref, h), _head_scalar(db_ref, h)
            tau, _, dm = _chunk_decay(lgf, lgb)
            q = q_ref[:, ln]
            qf = q.astype(F32)
            acc = _dot((_dot_nt(q, k_ref[:, ln]) * dm).astype(BF16), v_ref[:, ln])
            acc = acc + _dot((qf * jnp.exp(lgf * (tau + 1.0))).astype(BF16), sf_ref[h].astype(BF16))
            acc = acc + _dot((qf * jnp.exp(lgb * (TQ - tau))).astype(BF16), sb_ref[h].astype(BF16))
            o_ref[:, ln] = acc
            rn = lax.rsqrt(jnp.mean(acc * acc, axis=-1, keepdims=True) + EPS)
            g = g_ref[:, ln].astype(F32)
            y_ref[:, ln] = ((acc * rn * gn_ref[:, ln]) * (g * _sigmoid(g))).astype(BF16)

    tile = pl.BlockSpec((None, TQ, 512), lambda b, i: (b, i, 0))
    return pl.pallas_call(
        body, name="ret_fwd", grid=(B, L // TQ),
        in_specs=[dec_spec, dec_spec, sec(4), sec(5), sec(6), sec(7),
                  pl.BlockSpec((1, 512), lambda b, i: (0, 0)), st_spec, st_spec],
        out_specs=(tile, tile),
        out_shape=(jax.ShapeDtypeStruct((B, L, 512), BF16), jax.ShapeDtypeStruct((B, L, 512), F32)),
        compiler_params=_params(("arbitrary",) * 2))(dec_f, dec_b, P, P, P, P, ret_norm_g, sf, sb)


def _retc_bwd_call(P, sf, sb, dec_f, dec_b, ret_norm_g, o_ret, dY, cos2, sin2, L, LC):
    B, T, _ = P.shape
    n = L // TQ
    C = float(TQ)
    kscale = RET_DK ** -0.5
    st_spec = pl.BlockSpec((None, 4, n, RET_DK, RET_DK), lambda b, i: (b, 0, 0, 0, 0))

    def body(df_ref, db_ref, q_ref, k_ref, v_ref, g_ref, gn_ref, o_ref, dy_ref, cos_ref, sin_ref, sf_ref, sb_ref,
             dq_ref, dg_ref, dk_ref, dv_ref, dgn_ref, dlg_ref, dsf_ref, dsb_ref):
        i = pl.program_id(1)

        @pl.when(i == 0)
        def _():
            dk_ref[...] = jnp.zeros_like(dk_ref)
            dv_ref[...] = jnp.zeros_like(dv_ref)
            dgn_ref[...] = jnp.zeros_like(dgn_ref)
            dlg_ref[...] = jnp.zeros_like(dlg_ref)

        rows = pl.ds(pl.multiple_of(i * TQ, TQ), TQ)
        cs, sn = cos_ref[rows, :], sin_ref[rows, :]

        def one_head(h):
            ln = slice(h * RET_DK, (h + 1) * RET_DK)
            lgf, lgb = _head_scalar(df_ref, h), _head_scalar(db_ref, h)
            tau, dist, dm = _chunk_decay(lgf, lgb)

            def add_lg(row, x):
                csum = jnp.sum(x, axis=0, keepdims=True)
                tot = csum[:, 0:128]
                for part in range(1, x.shape[1] // 128):
                    tot = tot + csum[:, part * 128:(part + 1) * 128]
                dlg_ref[h, row:row + 1, :] += tot

            q = q_ref[:, ln]
            qf = q.astype(F32)
            o = o_ref[:, ln]
            g = g_ref[:, ln].astype(F32)
            dy = dy_ref[:, ln].astype(F32)
            gn = gn_ref[:, ln]
            sg = _sigmoid(g)
            rn = lax.rsqrt(jnp.mean(o * o, axis=-1, keepdims=True) + EPS)
            nrm = o * rn
            dg_ref[:, ln] = (dy * (nrm * gn) * (sg * (1.0 + g * (1.0 - sg)))).astype(BF16)
            dhn = dy * (g * sg)
            dgn_ref[:, ln] += jnp.sum(dhn * nrm, axis=0, keepdims=True)
            dnrm = dhn * gn
            do = rn * (dnrm - nrm * jnp.mean(dnrm * nrm, axis=-1, keepdims=True))
            dob = do.astype(BF16)
            ki, vi = k_ref[rows, ln], v_ref[rows, ln]
            s = _dot_nt(q, ki)
            dsv = _dot_nt(dob, vi)
            dsb = (dsv * dm).astype(BF16)
            dk_ref[rows, ln] += _dot_tn(dsb, q)
            dv_ref[rows, ln] += _dot_tn((s * dm).astype(BF16), dob)
            xw = s * dsv * dm * jnp.abs(dist)
            fpart = jnp.where(dist > 0, xw, 0.0)
            add_lg(0, fpart)
            add_lg(1, xw - fpart)
            dq = _dot(dsb, ki)
            af, ab = jnp.exp(lgf * (tau + 1.0)), jnp.exp(lgb * (C - tau))
            qa, qb = (qf * af).astype(BF16), (qf * ab).astype(BF16)
            sfi, sbi = sf_ref[h, i].astype(BF16), sb_ref[h, i].astype(BF16)
            dq = dq + af * _dot_nt(dob, sfi) + ab * _dot_nt(dob, sbi)
            dsf_ref[h, i] = _dot_tn(qa, dob)
            dsb_ref[h, i] = _dot_tn(qb, dob)
            add_lg(0, (tau + 1.0) * (_dot(qa, sfi) * do))
            add_lg(1, (C - tau) * (_dot(qb, sbi) * do))
            dq_ref[:, ln] = (dq * cs - pltpu.roll(dq, 64, 1) * sn).astype(BF16)

            @pl.when(i == n - 1)
            def _():
                jc = lax.broadcasted_iota(jnp.int32, (LC, 1), 0).astype(F32)
                crow = pl.ds(L, LC)

                def through_state(rws, w, dw, gst, row):
                    kk, vv = k_ref[rws, ln].astype(F32), v_ref[rws, ln]
                    gb = gst.astype(BF16)
                    vg = _dot_nt(vv, gb)
                    kw = kk * w
                    dk_ref[rws, ln] += w * vg
                    dv_ref[rws, ln] += _dot(kw.astype(BF16), gb)
                    add_lg(row, dw * (kw * vg))

                def scan(gc, w, dw, st_ref, dst_ref, order, row):
                    def step(r, gst):
                        j = order(r)
                        through_state(pl.ds(pl.multiple_of(j * TQ, TQ), TQ), w, dw, gst, row)
                        add_lg(row, (C * gc) * (gst * st_ref[h, j]))
                        return dst_ref[h, j] + gc * gst
                    return lax.fori_loop(0, n, step, jnp.zeros((RET_DK, RET_DK), F32), unroll=True)

                gcf, gcb = jnp.exp(lgf * C), jnp.exp(lgb * C)
                g0 = scan(gcf, jnp.exp(lgf * (C - 1.0 - tau)), C - 1.0 - tau, sf_ref, dsf_ref,
                          lambda r: n - 1 - r, 0)
                through_state(crow, jnp.exp(lgf * (LC - 1.0 - jc)), LC - 1.0 - jc, g0, 0)
                g1 = scan(gcb, jnp.exp(lgb * tau), tau, sb_ref, dsb_ref, lambda r: r, 1)
                through_state(crow, jnp.exp(lgb * jc), jc, g1, 1)
                dk = dk_ref[:, ln]
                dk_ref[:, ln] = (dk * cos_ref[...] - pltpu.roll(dk, 64, 1) * sin_ref[...]) * kscale

        for h in range(4):
            one_head(h)

    sec = lambda k: pl.BlockSpec((None, TQ, 512), lambda b, i: (b, i, k))
    full = lambda k: pl.BlockSpec((None, T, 512), lambda b, i: (b, 0, k))
    dec_spec = pl.BlockSpec((1, 4), lambda b, i: (0, 0))
    tab = pl.BlockSpec((T, RET_DK), lambda b, i: (0, 0))
    return pl.pallas_call(
        body, name="ret_bwd", grid=(B, n),
        in_specs=[dec_spec, dec_spec, sec(4), full(5), full(6), sec(7),
                  pl.BlockSpec((1, 512), lambda b, i: (0, 0)), sec(0), sec(1), tab, tab, st_spec, st_spec],
        out_specs=(sec(0), sec(0), full(0), full(0),
                   pl.BlockSpec((None, 1, 512), lambda b, i: (b, 0, 0)),
                   pl.BlockSpec((None, 4, 8, 128), lambda b, i: (b, 0, 0, 0))),
        out_shape=(jax.ShapeDtypeStruct((B, L, 512), BF16), jax.ShapeDtypeStruct((B, L, 512), BF16),
                   jax.ShapeDtypeStruct((B, T, 512), F32), jax.ShapeDtypeStruct((B, T, 512), F32),
                   jax.ShapeDtypeStruct((B, 1, 512), F32), jax.ShapeDtypeStruct((B, 4, 8, 128), F32)),
        scratch_shapes=[pltpu.VMEM((4, n, RET_DK, RET_DK), F32), pltpu.VMEM((4, n, RET_DK, RET_DK), F32)],
        compiler_params=_params(("arbitrary",) * 2, vmem_mb=56))(
            dec_f, dec_b, P, P, P, P, ret_norm_g, o_ret, dY, cos2, sin2, sf, sb)


def _out_call(y_na, y_ret, x, target, mod, final_g, wout_f):
    B, L, _ = x.shape

    def body(yn_ref, yr_ref, x_ref, t_ref, mod_ref, gf_ref, w_ref, dy_ref, dx2_ref, dw_ref, sm_ref):
        b, i = pl.program_id(0), pl.program_id(1)

        @pl.when((b == 0) & (i == 0))
        def _():
            dw_ref[...] = jnp.zeros_like(dw_ref)
            sm_ref[...] = jnp.zeros_like(sm_ref)

        gate = mod_ref[pl.ds(b, 1), 2 * D:3 * D]
        gf = gf_ref[...]
        yn, yr = yn_ref[...], yr_ref[...]
        ylat = _dot(yn, w_ref[0:512, :]) + _dot(yr, w_ref[512:1024, :])
        x2 = x_ref[...] + gate * ylat
        r = lax.rsqrt(jnp.mean(x2 * x2, axis=-1, keepdims=True) + EPS)
        xr = x2 * r
        err = xr * gf - t_ref[...]
        sm_ref[1:2, :] += jnp.sum(err * err, axis=0, keepdims=True)
        dout = err * (1.0 / D)
        sm_ref[0:1, :] += jnp.sum(dout * xr, axis=0, keepdims=True)
        gd = dout * gf
        dx2 = r * (gd - xr * jnp.mean(gd * xr, axis=-1, keepdims=True))
        dx2_ref[...] = dx2
        sm_ref[pl.ds(2 + b, 1), :] += jnp.sum(dx2 * ylat, axis=0, keepdims=True)
        dyl = (gate * dx2).astype(BF16)
        dy_ref[:, 0:512] = _dot_nt(dyl, w_ref[0:512, :]).astype(BF16)
        dy_ref[:, 512:1024] = _dot_nt(dyl, w_ref[512:1024, :]).astype(BF16)
        dw_ref[0:512, :] += _dot_tn(yn, dyl)
        dw_ref[512:1024, :] += _dot_tn(yr, dyl)

    half = pl.BlockSpec((None, TQ, 512), lambda b, i: (b, i, 0))
    full = pl.BlockSpec((None, TQ, D), lambda b, i: (b, i, 0))
    return pl.pallas_call(
        body, name="out_proj_loss", grid=(B, L // TQ),
        in_specs=[half, half, full, full,
                  pl.BlockSpec((8, 3 * D), lambda b, i: (0, 0)),
                  pl.BlockSpec((1, D), lambda b, i: (0, 0)),
                  pl.BlockSpec((D, D), lambda b, i: (0, 0))],
        out_specs=(full, full, pl.BlockSpec((D, D), lambda b, i: (0, 0)),
                   pl.BlockSpec((8, D), lambda b, i: (0, 0))),
        out_shape=(jax.ShapeDtypeStruct((B, L, D), BF16), jax.ShapeDtypeStruct((B, L, D), F32),
                   jax.ShapeDtypeStruct((D, D), F32), jax.ShapeDtypeStruct((8, D), F32)),
        compiler_params=_params(("arbitrary",) * 2))(y_na, y_ret, x, target, mod, final_g, wout_f)


def _dh_call(dsec, win_f, x, ctx, dx2, mod, norm_g, cp_in, cp_out):
    B, L, _ = x.shape
    LC = ctx.shape[1]
    nl = L // TQ

    def body(d0, d1, d2, d3, d4, d5, d6, d7, w_ref, x_ref, ctx_ref, dx2_ref, mod_ref, g_ref, cpi_ref, cpo_ref,
             gx_ref, sm_ref, sli_ref, slo_ref, ssem, rsem, lsem):
        drefs = (d0, d1, d2, d3, d4, d5, d6, d7)
        b, t = pl.program_id(0), pl.program_id(1)
        is_lat = t < nl

        @pl.when((b == 0) & (t == 0))
        def _():
            sm_ref[...] = jnp.zeros_like(sm_ref)

        def dh_of(secs):
            acc = jnp.zeros((TQ, D), F32)
            for sec in secs:
                s, half = divmod(sec, 2)
                acc = acc + _dot_nt(drefs[sec][...].astype(BF16), w_ref[s, :, half * 512:(half + 1) * 512])
            return acc

        def norm_bwd(dh, xt, mrow):
            scale = mrow[:, D:2 * D]
            g = g_ref[...]
            rstd = lax.rsqrt(jnp.mean(xt * xt, axis=-1, keepdims=True) + EPS)
            xn = xt * rstd
            dshift = jnp.sum(dh, axis=0, keepdims=True)
            dscale = jnp.sum(dh * (xn * g), axis=0, keepdims=True)
            dhn = dh * (1.0 + scale)
            sm_ref[0:1, :] += jnp.sum(dhn * xn, axis=0, keepdims=True)
            dxn = dhn * g
            dx = rstd * (dxn - xn * jnp.mean(dxn * xn, axis=-1, keepdims=True))
            return dshift, dscale, dx

        @pl.when(is_lat)
        def _():
            dshift, dscale, dx = norm_bwd(dh_of(range(8)), x_ref[...], mod_ref[pl.ds(b, 1), :])
            sm_ref[pl.ds(3 + b, 1), :] += dshift
            sm_ref[pl.ds(3 + B + b, 1), :] += dscale
            gx_ref[...] = dx2_ref[...] + dx

        @pl.when(jnp.logical_not(is_lat))
        def _():
            dshift, dscale, _ = norm_bwd(dh_of((1, 2, 5, 6)), ctx_ref[...], mod_ref[B:B + 1, :])
            sm_ref[1:2, :] += dshift
            sm_ref[2:3, :] += dscale

        mx, my, mc = _mesh_pos()
        s = 2 * mx + my
        cps, sls = (cpi_ref, cpo_ref), (sli_ref, slo_ref)
        own = [pltpu.make_async_copy(cps[a].at[s], sls[a].at[s], lsem.at[a]) for a in range(2)]
        sends, recvs, k = [], [], 0
        for px, py in _other_chips(mx, my):
            ps = 2 * px + py
            for a in range(2):
                sends.append(_remote(cps[a].at[ps], sls[a].at[s], ssem, rsem, k, (px, py, mc)))
                recvs.append(_remote(cps[a].at[s], sls[a].at[ps], ssem, rsem, k, (px, py, mc)))
                k += 1

        @pl.when((b == 0) & (t == 0))
        def _():
            for cp in own + sends:
                cp.start()

        @pl.when((b == B - 1) & (t == nl))
        def _():
            _finish(own, sends, recvs)

    lat = lambda b, t: (b, jnp.minimum(t, nl - 1), 0)
    tok = lambda b, t: (b, t, 0)
    sec_specs = [pl.BlockSpec((None, TQ, 512), lat if sec in (0, 3, 4, 7) else tok) for sec in range(8)]
    return pl.pallas_call(
        body, name="dh_norm_bwd", grid=(B, nl + 1),
        in_specs=sec_specs + [
            pl.BlockSpec((N_SHARD, D, D), lambda b, t: (0, 0, 0)),
            pl.BlockSpec((None, TQ, D), lat),
            pl.BlockSpec((None, LC, D), lambda b, t: (b, 0, 0)),
            pl.BlockSpec((None, TQ, D), lat),
            pl.BlockSpec((8, 3 * D), lambda b, t: (0, 0)),
            pl.BlockSpec((1, D), lambda b, t: (0, 0)), ANY, ANY],
        out_specs=(pl.BlockSpec((None, TQ, D), lat), pl.BlockSpec((8, D), lambda b, t: (0, 0)), ANY, ANY),
        out_shape=(jax.ShapeDtypeStruct((B, L, D), F32), jax.ShapeDtypeStruct((8, D), F32),
                   jax.ShapeDtypeStruct(cp_in.shape, cp_in.dtype), jax.ShapeDtypeStruct(cp_out.shape, cp_out.dtype)),
        scratch_shapes=[pltpu.SemaphoreType.DMA((6,)), pltpu.SemaphoreType.DMA((6,)),
                        pltpu.SemaphoreType.DMA((2,))],
        compiler_params=_params(("arbitrary",) * 2))(*dsec, win_f, x, ctx, dx2, mod, norm_g, cp_in, cp_out)


def _dw_call(dsec, h, L):
    B, T, _ = h.shape
    nl = L // TQ

    def body(d0, d1, d2, d3, d4, d5, d6, d7, h_ref, dw_ref, acc_ref):
        drefs = (d0, d1, d2, d3, d4, d5, d6, d7)
        b, t = pl.program_id(0), pl.program_id(1)

        @pl.when((b == 0) & (t == 0))
        def _():
            acc_ref[...] = jnp.zeros_like(acc_ref)

        hb = h_ref[...]

        def add(secs):
            for sec in secs:
                s, half = divmod(sec, 2)
                acc_ref[s, :, half * 512:(half + 1) * 512] += _dot_tn(hb, drefs[sec][...].astype(BF16))

        @pl.when(t < nl)
        def _():
            add(range(8))

        @pl.when(t >= nl)
        def _():
            add((1, 2, 5, 6))

        @pl.when((b == B - 1) & (t == nl))
        def _():
            dw_ref[...] = acc_ref[...].astype(BF16)

    lat = lambda b, t: (b, jnp.minimum(t, nl - 1), 0)
    tok = lambda b, t: (b, t, 0)
    sec_specs = [pl.BlockSpec((None, TQ, 512), lat if sec in (0, 3, 4, 7) else tok) for sec in range(8)]
    return pl.pallas_call(
        body, name="dw_in", grid=(B, nl + 1),
        in_specs=sec_specs + [pl.BlockSpec((None, TQ, D), tok)],
        out_specs=pl.BlockSpec((N_SHARD, D, D), lambda b, t: (0, 0, 0)),
        out_shape=jax.ShapeDtypeStruct((N_SHARD, D, D), BF16),
        scratch_shapes=[pltpu.VMEM((N_SHARD, D, D), F32)],
        compiler_params=_params(("arbitrary",) * 2, vmem_mb=56))(*dsec, h)


def _mesh_pos():
    return lax.axis_index("x"), lax.axis_index("y"), lax.axis_index("c")


def _flip(v, f):
    return 1 - v if f else v


def _remote(src, dst, ssem, rsem, k, peer):
    return pltpu.make_async_remote_copy(src_ref=src, dst_ref=dst, send_sem=ssem.at[k], recv_sem=rsem.at[k],
                                        device_id=peer, device_id_type=MESH)


def _other_chips(x, y):
    return [(_flip(x, fx), _flip(y, fy)) for fx, fy in ((1, 0), (0, 1), (1, 1))]


def _all_to_all_small(src, dst_all, ssem, rsem, k0, x, y, cc):
    me = 4 * x + 2 * y + cc
    sends, recvs = [], []
    for f in range(1, N_DEV):
        px, py, pc = _flip(x, f & 4), _flip(y, f & 2), _flip(cc, f & 1)
        sends.append(_remote(src, dst_all.at[me], ssem, rsem, k0 + f - 1, (px, py, pc)))
        recvs.append(_remote(src, dst_all.at[4 * px + 2 * py + pc], ssem, rsem, k0 + f - 1, (px, py, pc)))
    return sends, recvs


def _finish(local, sends, recvs):
    for cp in recvs:
        cp.wait_recv()
    for cp in sends:
        cp.wait_send()
    for cp in local:
        cp.wait()


def _gather_call(wout_b, wada_b, c, rpb_flat):
    arrs = (wout_b, wada_b)
    na = len(arrs)
    hrs = [a.shape[0] // 2 for a in arrs]

    def body(wout, wada, c_ref, r_ref, wout_f, wada_f, c_all, bias_ref, et_ref, ssem, rsem, lsem):
        x, y, cc = _mesh_pos()
        s, me = 2 * x + y, 4 * x + 2 * y + cc
        sib = (x, y, 1 - cc)
        srcs, dsts = (wout, wada), (wout_f, wada_f)

        def half(a, shard, hc):
            return dsts[a].at[shard, pl.ds(hc * hrs[a], hrs[a])]

        local = [pltpu.make_async_copy(srcs[a], dsts[a].at[s], lsem.at[a]) for a in range(na)]
        local.append(pltpu.make_async_copy(c_ref, c_all.at[me], lsem.at[na]))
        ici_send, ici_recv, fwd_send, fwd_recv, k = [], [], [], [], 0
        for px, py in _other_chips(x, y):
            ps = 2 * px + py
            for a in range(na):
                mine = srcs[a].at[pl.ds(cc * hrs[a], hrs[a])]
                ici_send.append(_remote(mine, half(a, s, cc), ssem, rsem, k, (px, py, cc)))
                ici_recv.append(_remote(mine, half(a, ps, cc), ssem, rsem, k, (px, py, cc)))
                fwd_send.append(_remote(half(a, ps, cc), half(a, ps, cc), ssem, rsem, 3 * na + k, sib))
                fwd_recv.append(_remote(half(a, ps, 1 - cc), half(a, ps, 1 - cc), ssem, rsem, 3 * na + k, sib))
                k += 1
        c_send, c_recv = _all_to_all_small(c_ref, c_all, ssem, rsem, 6 * na, x, y, cc)
        for cp in local + ici_send + c_send:
            cp.start()
        _bias_body(r_ref, bias_ref, et_ref)
        for got, fwd in zip(ici_recv, fwd_send):
            got.wait_recv()
            fwd.start()
        _finish(local, ici_send + fwd_send + c_send, fwd_recv + c_recv)

    return pl.pallas_call(
        body, name="weight_gather",
        in_specs=[pl.BlockSpec(memory_space=pltpu.VMEM)] * 3 + [pl.BlockSpec(memory_space=pltpu.SMEM)],
        out_specs=(pl.BlockSpec(memory_space=pltpu.VMEM),) * 4,
        out_shape=tuple(jax.ShapeDtypeStruct((N_SHARD,) + a.shape, a.dtype) for a in arrs)
        + (jax.ShapeDtypeStruct((N_DEV,) + c.shape, c.dtype),
           jax.ShapeDtypeStruct((rpb_flat.shape[0], 3, TQ, KW), F32)),
        scratch_shapes=[pltpu.VMEM((15, GRID_W, GRID_W), F32),
                        pltpu.SemaphoreType.DMA((6 * na + 7,)), pltpu.SemaphoreType.DMA((6 * na + 7,)),
                        pltpu.SemaphoreType.DMA((na + 1,))],
        compiler_params=pltpu.CompilerParams(vmem_limit_bytes=56 << 20))(wout_b, wada_b, c, rpb_flat)


VROWS = 32


def _grad_halves_call(dwin_b, dwout_b, dbias, dlg):
    arrs = (dwin_b, dwout_b)
    hrs = [a.shape[1] // 2 for a in arrs]

    def body(din, dout, db_ref, dlg_ref, cp_in, cp_out, drpb_ref, dlgo_ref, got_in, got_out, p_ref, ssem, rsem):
        x, y, cc = _mesh_pos()
        sib = (x, y, 1 - cc)
        srcs, gots, cps = (din, dout), (got_in, got_out), (cp_in, cp_out)
        halves = [_remote(srcs[a].at[:, pl.ds((1 - cc) * hrs[a], hrs[a])], gots[a], ssem, rsem, a, sib)
                  for a in range(2)]
        for cp in halves:
            cp.start()
        _small_reduce_body(db_ref, dlg_ref, drpb_ref, dlgo_ref, p_ref)
        for cp in halves:
            cp.wait_recv()
        for a in range(2):
            for j in range(N_SHARD):
                def add(i, carry, a=a, j=j):
                    r = pl.multiple_of(i * VROWS, VROWS)
                    mine = srcs[a][j, pl.ds(pl.multiple_of(cc * hrs[a] + r, VROWS), VROWS), :].astype(F32)
                    cps[a][j, pl.ds(r, VROWS), :] = (
                        mine + gots[a][j, pl.ds(r, VROWS), :].astype(F32)).astype(BF16)
                    return carry
                lax.fori_loop(0, hrs[a] // VROWS, add, 0)
        for cp in halves:
            cp.wait_send()

    vmem = pl.BlockSpec(memory_space=pltpu.VMEM)
    half_shapes = [(N_SHARD, hrs[a], arrs[a].shape[2]) for a in range(2)]
    return pl.pallas_call(
        body, name="grad_halves",
        in_specs=[vmem] * 4, out_specs=(vmem,) * 4,
        out_shape=(jax.ShapeDtypeStruct(half_shapes[0], BF16), jax.ShapeDtypeStruct(half_shapes[1], BF16),
                   jax.ShapeDtypeStruct((dbias.shape[0], 16, 32), F32), jax.ShapeDtypeStruct((32, 128), F32)),
        scratch_shapes=[pltpu.VMEM(half_shapes[0], BF16), pltpu.VMEM(half_shapes[1], BF16),
                        pltpu.VMEM((32, GRID_W), F32),
                        pltpu.SemaphoreType.DMA((2,)), pltpu.SemaphoreType.DMA((2,))],
        compiler_params=pltpu.CompilerParams(vmem_limit_bytes=56 << 20))(dwin_b, dwout_b, dbias, dlg)


def _grad_finish_call(sl_in, sl_out, small):
    arrs = (sl_in, sl_out)

    def body(sin, sout, sm, gin, gout, sm_all, h_in, h_out, ssem, rsem, lsem):
        x, y, cc = _mesh_pos()
        me = 4 * x + 2 * y + cc
        sib = (x, y, 1 - cc)
        sls, hs, gs = (sin, sout), (h_in, h_out), (gin, gout)
        sm_send, sm_recv = _all_to_all_small(sm, sm_all, ssem, rsem, 2, x, y, cc)
        sm_own = pltpu.make_async_copy(sm, sm_all.at[me], lsem.at[0])
        for cp in sm_send + [sm_own]:
            cp.start()
        for a in range(2):
            def total(i, carry, a=a):
                rows = pl.ds(pl.multiple_of(i * VROWS, VROWS), VROWS)
                sl = sls[a]
                hs[a][rows, :] = ((sl[0, rows, :].astype(F32) + sl[1, rows, :].astype(F32))
                                  + sl[2, rows, :].astype(F32)) + sl[3, rows, :].astype(F32)
                return carry
            lax.fori_loop(0, arrs[a].shape[1] // VROWS, total, 0)
        mine = [pltpu.make_async_copy(hs[a], gs[a].at[cc], lsem.at[1 + a]) for a in range(2)]
        back = [_remote(hs[a], gs[a].at[cc], ssem, rsem, a, sib) for a in range(2)]
        back_recv = [_remote(hs[a], gs[a].at[1 - cc], ssem, rsem, a, sib) for a in range(2)]
        for cp in mine + back:
            cp.start()
        _finish(mine + [sm_own], back + sm_send, back_recv + sm_recv)

    vmem = pl.BlockSpec(memory_space=pltpu.VMEM)
    return pl.pallas_call(
        body, name="grad_finish",
        in_specs=[vmem] * 3, out_specs=(vmem,) * 3,
        out_shape=(jax.ShapeDtypeStruct((2,) + sl_in.shape[1:], F32),
                   jax.ShapeDtypeStruct((2,) + sl_out.shape[1:], F32),
                   jax.ShapeDtypeStruct((N_DEV,) + small.shape, F32)),
        scratch_shapes=[pltpu.VMEM(sl_in.shape[1:], F32), pltpu.VMEM(sl_out.shape[1:], F32),
                        pltpu.SemaphoreType.DMA((9,)), pltpu.SemaphoreType.DMA((9,)),
                        pltpu.SemaphoreType.DMA((3,))],
        compiler_params=pltpu.CompilerParams(vmem_limit_bytes=48 << 20))(sl_in, sl_out, small)


def _adamw(w, g, m, v):
    m = ADAM_B1 * m + (1.0 - ADAM_B1) * g
    v = ADAM_B2 * v + (1.0 - ADAM_B2) * (g * g)
    m_hat = m / (1.0 - ADAM_B1 ** ADAM_STEP)
    v_hat = v / (1.0 - ADAM_B2 ** ADAM_STEP)
    return -ADAM_LR * (m_hat / (jnp.sqrt(v_hat) + ADAM_EPS) + ADAM_WD * w), m, v


def _adam_call(w, m, v, g, name):
    R, C = w.shape
    tr = 256

    def body(w_ref, m_ref, v_ref, g_ref, go_ref, d_ref, mo_ref, vo_ref):
        g = g_ref[...]
        go_ref[...] = g
        d_ref[...], mo_ref[...], vo_ref[...] = _adamw(w_ref[...], g, m_ref[...], v_ref[...])

    spec = pl.BlockSpec((tr, C), lambda i: (i, 0))
    return pl.pallas_call(
        body, name=name, grid=(R // tr,), in_specs=[spec] * 4,
        out_specs=(spec,) * 4, out_shape=(jax.ShapeDtypeStruct((R, C), F32),) * 4,
        compiler_params=_params(("arbitrary",)))(w, m, v, g)


R_GF, R_NG, R_LOSS, R_RNG, R_LGF, R_LGB, R_SHIFT, R_SCALE, R_GATE, R_SHIFT_C, R_SCALE_C, R_RNG2, R_RPB = (
    0, 1, 2, 3, 4, 5, 6, 8, 10, 12, 13, 14, 16)
W_GF, W_NG, W_CCTX, W_RNG, W_DF, W_DB, W_BADA, W_RPB = 0, 1, 2, 3, 4, 5, 6, 9


SMALL = (("final_norm_g", W_GF, 1, D), ("norm_g", W_NG, 1, D), ("c_ctx", W_CCTX, 1, D),
         ("ret_norm_g", W_RNG, 1, 512), ("ret_decay_fwd", W_DF, 1, 4), ("ret_decay_bwd", W_DB, 1, 4),
         ("b_ada", W_BADA, 3, D), ("na_rpb", W_RPB, 4, D))
N_SMALL = len(SMALL)


def _small_final_call(sm_all, c_t, wada_f, wada, m_ada, v_ada, small_w, small_m, small_v, B):
    ws = wada.shape[1]
    NB = N_DEV * B

    def body(*refs):
        sm_ref, ct_ref, wf_ref, wa_ref, ma_ref, va_ref = refs[:6]
        ins = refs[6:6 + 3 * N_SMALL]
        outs = refs[6 + 3 * N_SMALL:6 + 7 * N_SMALL]
        ga_ref, da_ref, mao_ref, vao_ref, loss_ref, dmod_ref, pk_ref = refs[6 + 7 * N_SMALL:]
        x, y, _ = _mesh_pos()
        s = 2 * x + y
        tot = sm_ref[0]
        for dv in range(1, N_DEV):
            tot = tot + sm_ref[dv]
        pk_ref[...] = jnp.zeros_like(pk_ref)
        for kind in range(3):
            for i, (_, row, nrow, width) in enumerate(SMALL):
                ref = ins[kind * N_SMALL + i]
                if nrow == 3:
                    for part in range(3):
                        pk_ref[kind, row + part:row + part + 1, :] = ref[:, part * D:(part + 1) * D]
                else:
                    pk_ref[kind, row:row + nrow, 0:width] = ref[...]
        w = pk_ref[0]
        cctx_ref = ins[2]
        for dv in range(N_DEV):
            for b in range(B):
                r = dv * B + b
                for part, row in enumerate((R_SHIFT, R_SCALE, R_GATE)):
                    dmod_ref[r:r + 1, part * D:(part + 1) * D] = sm_ref[dv, row + b:row + b + 1, :]
        dmod_ref[NB:NB + 1, 0:D] = tot[R_SHIFT_C:R_SHIFT_C + 1, :]
        dmod_ref[NB:NB + 1, D:2 * D] = tot[R_SCALE_C:R_SCALE_C + 1, :]
        dmod_ref[NB:NB + 1, 2 * D:3 * D] = jnp.zeros((1, D), F32)
        dmod_ref[NB + 1:, :] = jnp.zeros((dmod_ref.shape[0] - NB - 1, 3 * D), F32)
        dmod = dmod_ref[...]
        cc = cctx_ref[...]
        scc = _sigmoid(cc)
        ct = ct_ref[...]
        act_t = ct * _sigmoid(ct)
        dmc = dmod[NB:NB + 1, :].astype(BF16)
        dact = jnp.zeros((1, D), F32)
        for sh in range(N_SHARD):
            dact = dact + _dot_nt(dmc[:, sh * ws:(sh + 1) * ws], wf_ref[sh])
        g = jnp.zeros((16, D), F32)
        rows = lax.broadcasted_iota(jnp.int32, (16, D), 0)

        def put(g, row, val):
            return jnp.where(rows == row, val, g)

        g = put(g, W_GF, tot[R_GF:R_GF + 1, :])
        g = put(g, W_NG, tot[R_NG:R_NG + 1, :])
        g = put(g, W_CCTX, dact * (scc * (1.0 + cc * (1.0 - scc))))
        g = put(g, W_RNG, tot[R_RNG:R_RNG + 1, :] + tot[R_RNG2:R_RNG2 + 1, :])
        g = put(g, W_DF, tot[R_LGF:R_LGF + 1, :] * (-jnp.exp(w[W_DF:W_DF + 1, :])))
        g = put(g, W_DB, tot[R_LGB:R_LGB + 1, :] * (-jnp.exp(w[W_DB:W_DB + 1, :])))
        db = jnp.sum(dmod, axis=0, keepdims=True)
        for part in range(3):
            g = put(g, W_BADA + part, db[:, part * D:(part + 1) * D])
        for part in range(4):
            g = put(g, W_RPB + part, tot[R_RPB + part:R_RPB + part + 1, :])
        for kind, val in enumerate((g,) + _adamw(w, g, pk_ref[1], pk_ref[2])):
            for i, (_, row, nrow, width) in enumerate(SMALL):
                out = outs[kind * N_SMALL + i]
                if nrow == 3:
                    for part in range(3):
                        out[:, part * D:(part + 1) * D] = val[row + part:row + part + 1, :]
                else:
                    out[...] = val[row:row + nrow, 0:width]
        loss_ref[...] = jnp.broadcast_to(
            (0.5 / D) * jnp.sum(tot[R_LOSS:R_LOSS + 1, :], axis=1, keepdims=True), (8, 128))
        for sh in range(N_SHARD):
            @pl.when(s == sh)
            def _():
                ga = jnp.dot(act_t, dmod[:, sh * ws:(sh + 1) * ws], precision=HIGHEST,
                             preferred_element_type=F32)
                ga_ref[...] = ga
                da_ref[...], mao_ref[...], vao_ref[...] = _adamw(wa_ref[...], ga, ma_ref[...], va_ref[...])

    sh_small = tuple(jax.ShapeDtypeStruct(a.shape, F32) for a in small_w)
    sh_ada = jax.ShapeDtypeStruct(wada.shape, F32)
    res = pl.pallas_call(
        body, name="small_final",
        out_shape=sh_small * 4 + (sh_ada,) * 4 + (jax.ShapeDtypeStruct((8, 128), F32),),
        scratch_shapes=[pltpu.VMEM((NB + 8, 3 * D), F32), pltpu.VMEM((3, 16, D), F32)],
        compiler_params=_params(vmem_mb=56))(
            sm_all, c_t, wada_f, wada, m_ada, v_ada, *small_w, *small_m, *small_v)
    smalls = [res[k * N_SMALL:(k + 1) * N_SMALL] for k in range(4)]
    return smalls, res[4 * N_SMALL:4 * N_SMALL + 4], res[4 * N_SMALL + 4]


def _local_step(order, x, c, ctx, c_ctx, norm_g, wada_f, b_ada, win_b, bias, dec_f, dec_b, ret_norm_g,
                wout_f, final_g, target):
    B, L, _ = x.shape
    LC = ctx.shape[1]
    assert B == 2
    cos2, sin2 = _rope_tables(L, LC)
    c8 = jnp.concatenate([c, c_ctx[None, :], jnp.zeros((8 - B - 1, D), F32)], axis=0)
    mod = _mod_call(c8, wada_f, b_ada)
    P, h, win_f = _inproj_gather_call(order, x, ctx, mod, norm_g, win_b, cos2, sin2)
    y_na, o_na = _na_fwd_call(P, bias, L, LC)
    sf, sb = _ret_states_call(P, dec_f, dec_b, L, LC)
    y_ret, o_ret = _retc_fwd_call(P, sf, sb, dec_f, dec_b, ret_norm_g, L)
    dY, dx2, dwout_p, sm_out = _out_call(y_na, y_ret, x, target, mod, final_g, wout_f.reshape(D, D))
    dnq, dng, dnk, dnv, dbias = _na_bwd_call(P, bias, dY, o_na, L, LC)
    drq, drg, drk, drv, dgn, dlg = _retc_bwd_call(P, sf, sb, dec_f, dec_b, ret_norm_g, o_ret, dY, cos2, sin2, L, LC)
    dsec = (dnq, dnk, dnv, dng, drq, drk, drv, drg)
    dwin_b = _dw_call(dsec, h, L)
    cp_in, cp_out, drpb, dlg_sum = _grad_halves_call(
        dwin_b, dwout_p.astype(BF16).reshape(N_SHARD, D // N_SHARD, D), dbias, dlg)
    grad_x, sm_dh, sl_in, sl_out = _dh_call(dsec, win_f, x, ctx, dx2, mod, norm_g, cp_in, cp_out)
    z = jnp.zeros((1, D), F32)
    pad = lambda v: jnp.pad(v.reshape(1, -1), ((0, 0), (0, D - v.size)))
    dlg_sum = dlg_sum.reshape(4, 8, 128)
    rpb_rows = jnp.pad(drpb[:, :15, :31].reshape(-1), (0, 4 * D - drpb.shape[0] * 465)).reshape(4, D)
    small = jnp.concatenate([
        sm_out[0:1], sm_dh[0:1], sm_out[1:2], pad(dgn[0]), pad(dlg_sum[:, 0, 0]), pad(dlg_sum[:, 1, 0]),
        sm_dh[3:5], sm_dh[5:7], sm_out[2:4], sm_dh[1:2], sm_dh[2:3], pad(dgn[1]), z, rpb_rows,
        jnp.zeros((SM_ROWS - 20, D), F32)], axis=0)
    return grad_x, sl_in, sl_out, small


def kernel(x, c, ctx, c_ctx, norm_g, w_ada, b_ada, w_in, na_rpb, ret_decay_fwd, ret_decay_bwd, ret_norm_g, w_out, final_norm_g, loss_target, m_c_ctx, m_norm_g, m_w_ada, m_b_ada, m_w_in, m_na_rpb, m_ret_decay_fwd, m_ret_decay_bwd, m_ret_norm_g, m_w_out, m_final_norm_g, v_c_ctx, v_norm_g, v_w_ada, v_b_ada, v_w_in, v_na_rpb, v_ret_decay_fwd, v_ret_decay_bwd, v_ret_norm_g, v_w_out, v_final_norm_g):
    B = x.shape[0]
    wout_f, wada_f, c_all, bias = _gather_call(
        w_out[0].astype(BF16), w_ada[0].astype(BF16), c, na_rpb[0].reshape(na_rpb.shape[1], -1))
    mx, my = lax.axis_index("x"), lax.axis_index("y")
    order = jnp.stack([2 * mx + my, 2 * (1 - mx) + my, 2 * mx + (1 - my),
                       2 * (1 - mx) + (1 - my)]).astype(jnp.int32)
    grad_x, sl_in, sl_out, small = _local_step(
        order, x, c, ctx, c_ctx, norm_g, wada_f, b_ada, w_in[0].astype(BF16), bias, ret_decay_fwd,
        ret_decay_bwd, ret_norm_g, wout_f, final_norm_g.reshape(1, D), loss_target)
    gin, gout, sm_all = _grad_finish_call(sl_in, sl_out, small)
    g_win, d_win, nm_win, nv_win = _adam_call(
        w_in[0], m_w_in[0], v_w_in[0], gin.reshape(w_in.shape[1:]), "adam_w_in")
    g_wout, d_wout, nm_wout, nv_wout = _adam_call(
        w_out[0], m_w_out[0], v_w_out[0], gout.reshape(w_out.shape[1:]), "adam_w_out")

    def small_inputs(gf, ng, cc, rng, df, db, bada, rpb):
        return (gf.reshape(1, D), ng, cc.reshape(1, D), rng, df, db, bada,
                jnp.pad(rpb.reshape(-1), (0, 4 * D - rpb.size)).reshape(4, D))

    c_t = jnp.concatenate([c_all.reshape(N_DEV * B, D), c_ctx.reshape(1, D), jnp.zeros((7, D), F32)], axis=0).T
    smalls, adas, loss = _small_final_call(
        sm_all, c_t, wada_f, w_ada[0], m_w_ada[0], v_w_ada[0],
        small_inputs(final_norm_g, norm_g, c_ctx, ret_norm_g, ret_decay_fwd, ret_decay_bwd, b_ada, na_rpb),
        small_inputs(m_final_norm_g, m_norm_g, m_c_ctx, m_ret_norm_g, m_ret_decay_fwd, m_ret_decay_bwd, m_b_ada,
                     m_na_rpb),
        small_inputs(v_final_norm_g, v_norm_g, v_c_ctx, v_ret_norm_g, v_ret_decay_fwd, v_ret_decay_bwd, v_b_ada,
                     v_na_rpb), B)
    res = []
    for p, ada, win_o, wout_o in zip(smalls, adas, (g_win, d_win, nm_win, nv_win),
                                     (g_wout, d_wout, nm_wout, nv_wout)):
        gf, ng, cc, rng, df, db, bada, rpb = p
        res.append([cc.reshape(D), ng, ada[None], bada, win_o[None],
                    rpb.reshape(-1)[:na_rpb.size].reshape(na_rpb.shape), df, db, rng, wout_o[None], gf.reshape(D)])
    return (loss[0, 0], grad_x, *res[0], *res[1], *res[2], *res[3])
```

```python
import numpy as np
import jax
import jax.numpy as jnp
from jax import lax
from jax.experimental import pallas as pl
from jax.experimental.pallas import tpu as pltpu

F32 = jnp.float32
BF16 = jnp.bfloat16
HIGHEST = lax.Precision.HIGHEST

D = 1024
GRID_W = 64
NA_DH = 64
RET_DK = 128
ROPE_BASE = 10000.0
EPS = 1e-6
NEG = -1e30
TQ = 256
KW = 12 * GRID_W
N_SHARD = 4
N_DEV = 8
SM_ROWS = 24

ADAM_LR = 0.001
ADAM_B1 = 0.9
ADAM_B2 = 0.999
ADAM_EPS = 1e-08
ADAM_WD = 0.01
ADAM_STEP = 10

MESH = pl.DeviceIdType.MESH
ANY = pl.BlockSpec(memory_space=pl.ANY)


def _params(sem=None, vmem_mb=48):
    return pltpu.CompilerParams(dimension_semantics=sem, vmem_limit_bytes=vmem_mb << 20)


def _dot(a, b):
    return jnp.dot(a, b, preferred_element_type=F32)


def _dot_nt(a, b):
    return lax.dot_general(a, b, (((1,), (1,)), ((), ())), preferred_element_type=F32)


def _dot_tn(a, b):
    return lax.dot_general(a, b, (((0,), (0,)), ((), ())), preferred_element_type=F32)


def _sigmoid(x):
    return 1.0 / (1.0 + jnp.exp(-x))


def _rope_tables(L, LC):
    half = RET_DK // 2
    nf = half // 2
    t = np.arange(L)
    row = (t // GRID_W).astype(np.float32)
    col = (t % GRID_W).astype(np.float32)
    inv = (np.float32(ROPE_BASE) ** (-np.arange(nf, dtype=np.float32) / np.float32(nf))).astype(np.float32)
    ang = np.concatenate([row[:, None] * inv, col[:, None] * inv], axis=-1).astype(np.float32)
    cos, sin = np.cos(ang).astype(np.float32), np.sin(ang).astype(np.float32)
    cos2 = np.concatenate([cos, cos], axis=-1)
    sin2 = np.concatenate([-sin, sin], axis=-1)
    cos2 = np.concatenate([cos2, np.ones((LC, RET_DK), np.float32)], axis=0)
    sin2 = np.concatenate([sin2, np.zeros((LC, RET_DK), np.float32)], axis=0)
    return jnp.asarray(cos2), jnp.asarray(sin2)


def _mod_call(c8, wada_f, b_ada):
    ws = wada_f.shape[2]

    def body(c_ref, w_ref, b_ref, o_ref):
        a = c_ref[...]
        a = (a * _sigmoid(a)).astype(BF16)
        for s in range(N_SHARD):
            o_ref[:, s * ws:(s + 1) * ws] = _dot(a, w_ref[s]) + b_ref[:, s * ws:(s + 1) * ws]

    return pl.pallas_call(
        body, name="ada_mod", out_shape=jax.ShapeDtypeStruct((8, 3 * D), F32),
        compiler_params=_params())(c8, wada_f, b_ada)


def _dc_masks():
    cq = lax.broadcasted_iota(jnp.int32, (GRID_W, GRID_W), 0)
    ck = lax.broadcasted_iota(jnp.int32, (GRID_W, GRID_W), 1)
    dc = jnp.clip(ck - cq + 15, 0, 30)
    c0 = jnp.clip(cq - 8, 0, GRID_W - 16)
    col_ok = (ck >= c0) & (ck < c0 + 16)
    return dc, col_ok


def _bias_blocks():
    out = []
    for typ, delta in enumerate((4, 0, -4)):
        for rq in range(4):
            for rkk in range(12):
                dr = rkk + delta - rq - 4
                if typ == 0:
                    ok = -rq <= dr <= 7 - rq
                elif typ == 1:
                    ok = -4 <= dr <= 3
                else:
                    ok = -4 - rq <= dr <= 3 - rq
                out.append((typ, rq, rkk, dr if ok else None))
    return out


def _bias_body(r_ref, bias_ref, et_ref):
    dc, col_ok = _dc_masks()
    masks = [(dc == j).astype(F32) for j in range(31)]

    def per_h(h, carry):
        for dr in range(15):
            t = jnp.zeros((GRID_W, GRID_W), F32)
            for j in range(31):
                t = t + masks[j] * r_ref[h, dr * 31 + j]
            et_ref[dr] = jnp.where(col_ok, t, NEG)
        neg = jnp.full((GRID_W, GRID_W), NEG, F32)
        for typ, rq, rkk, dr in _bias_blocks():
            blk = neg if dr is None else et_ref[dr + 7]
            bias_ref[h, typ, rq * 64:(rq + 1) * 64, rkk * 64:(rkk + 1) * 64] = blk
        return carry

    lax.fori_loop(0, bias_ref.shape[0], per_h, 0)


def _bias_tile_sums(db_ref, hh):
    acc = {}
    for typ, rq, rkk, dr in _bias_blocks():
        if dr is None:
            continue
        blk = db_ref[hh, typ, rq * 64:(rq + 1) * 64, rkk * 64:(rkk + 1) * 64]
        acc[dr] = blk if dr not in acc else acc[dr] + blk
    return acc


def _small_reduce_body(dt_ref, dlg_ref, drpb_ref, dlgo_ref, p_ref):
    dc, _ = _dc_masks()
    masks = [(dc == j).astype(F32) for j in range(31)]
    ones = jnp.ones((8, GRID_W), F32)
    p_ref[...] = jnp.zeros_like(p_ref)
    drpb_ref[...] = jnp.zeros_like(drpb_ref)

    def per_h(h, carry):
        for dr in range(-7, 8):
            t = dt_ref[h, dr + 7]
            for j in range(31):
                p_ref[j:j + 1, :] = jnp.sum(t * masks[j], axis=0, keepdims=True)
            red = lax.dot_general(ones, p_ref[...], (((1,), (1,)), ((), ())),
                                  precision=HIGHEST, preferred_element_type=F32)
            drpb_ref[h, dr + 7:dr + 8, :] = red[0:1, :]
        return carry

    lax.fori_loop(0, dt_ref.shape[0], per_h, 0)
    x = dlg_ref[0]
    for b in range(1, dlg_ref.shape[0]):
        x = x + dlg_ref[b]
    x = x.reshape(4 * 8, x.shape[-1])
    dlgo_ref[...] = jnp.dot(x, jnp.ones((x.shape[-1], 128), F32), precision=HIGHEST,
                            preferred_element_type=F32)


def _inproj_gather_call(order, x, ctx, mod, norm_g, win_b, cos2, sin2):
    B, L, _ = x.shape
    LC = ctx.shape[1]
    T = L + LC
    TI = 2 * TQ
    nl = L // TI
    nt = nl + 1
    assert LC == TQ and L % TI == 0
    kscale = RET_DK ** -0.5
    HR = D // 2
    pad_rows = nt * TI - T
    cos2 = jnp.pad(cos2, ((0, pad_rows), (0, 0)))
    sin2 = jnp.pad(sin2, ((0, pad_rows), (0, 0)))

    def body(ord_ref, x_ref, ctx_ref, mod_ref, g_ref, wown_ref, cos_ref, sin_ref, p_ref, h_ref, wf_ref,
             w_all, hs_ref, ssem, rsem, lsem):
        j, b, t = pl.program_id(0), pl.program_id(1), pl.program_id(2)
        first = (b == 0) & (t == 0)
        mx, my, mc = _mesh_pos()
        s = 2 * mx + my
        sib = (mx, my, 1 - mc)
        own = pltpu.make_async_copy(wown_ref, w_all.at[s], lsem.at[0])
        ici_send, ici_recv, fwd_send, fwd_recv, outs = [], [], [], [], [
            pltpu.make_async_copy(w_all.at[s], wf_ref.at[s], lsem.at[1])]
        for k, (px, py) in enumerate(_other_chips(mx, my)):
            ps = 2 * px + py
            mine = w_all.at[s, pl.ds(mc * HR, HR)]
            ici_send.append(_remote(mine, w_all.at[s, pl.ds(mc * HR, HR)], ssem, rsem, k, (px, py, mc)))
            ici_recv.append(_remote(mine, w_all.at[ps, pl.ds(mc * HR, HR)], ssem, rsem, k, (px, py, mc)))
            got = w_all.at[ps, pl.ds(mc * HR, HR)]
            fwd_send.append(_remote(got, got, ssem, rsem, 3 + k, sib))
            theirs = w_all.at[ps, pl.ds((1 - mc) * HR, HR)]
            fwd_recv.append(_remote(theirs, theirs, ssem, rsem, 3 + k, sib))
            outs.append(pltpu.make_async_copy(w_all.at[ps], wf_ref.at[ps], lsem.at[2 + k]))

        @pl.when(first & (j == 0))
        def _():
            own.start()
            own.wait()
            ici_send[0].start()
            ici_send[1].start()
            outs[0].start()

        for k in range(3):
            @pl.when(first & (j == k + 1))
            def _(k=k):
                ici_recv[k].wait_recv()
                if k == 0:
                    ici_send[2].start()
                fwd_send[k].start()
                fwd_recv[k].wait_recv()
                outs[1 + k].start()

        tile = b * nt + t

        @pl.when(j == 0)
        def _():
            is_lat = t < nl
            ctx_tile = jnp.concatenate([ctx_ref[...], jnp.zeros((TI - LC, D), F32)], axis=0)
            xt = jnp.where(is_lat, x_ref[...], ctx_tile)
            mrow = mod_ref[pl.ds(jnp.where(is_lat, b, B), 1), :]
            shift, scale = mrow[:, 0:D], mrow[:, D:2 * D]
            rstd = lax.rsqrt(jnp.mean(xt * xt, axis=-1, keepdims=True) + EPS)
            h0 = ((xt * rstd * g_ref[...]) * (1.0 + scale) + shift).astype(BF16)
            h_ref[...] = h0
            hs_ref[tile] = h0

        hb = hs_ref[tile]
        cs, sn = cos_ref[...], sin_ref[...]
        shard = ord_ref[j]
        for sh in range(N_SHARD):
            @pl.when(shard == sh)
            def _(sh=sh):
                for half in range(2):
                    sec = 2 * sh + half
                    acc = _dot(hb, w_all[sh, :, half * 512:(half + 1) * 512])
                    if sec == 0:
                        acc = acc * (NA_DH ** -0.5)
                    if sec in (4, 5):
                        for q in range(4):
                            a = acc[:, q * 128:(q + 1) * 128]
                            r = a * cs + pltpu.roll(a, 64, 1) * sn
                            if sec == 5:
                                r = r * kscale
                            p_ref[:, half * 512 + q * 128:half * 512 + (q + 1) * 128] = r.astype(BF16)
                    else:
                        p_ref[:, half * 512:(half + 1) * 512] = acc.astype(BF16)

        @pl.when((j == N_SHARD - 1) & (b == B - 1) & (t == nt - 1))
        def _():
            _finish(outs, ici_send + fwd_send, [])

    tok = lambda j, b, t, o: (jnp.where(j == 0, b, B - 1), jnp.where(j == 0, jnp.minimum(t, nl - 1), nl - 1), 0)
    grid_spec = pltpu.PrefetchScalarGridSpec(
        num_scalar_prefetch=1, grid=(N_SHARD, B, nt),
        in_specs=[
            pl.BlockSpec((None, TI, D), tok),
            pl.BlockSpec((None, LC, D), lambda j, b, t, o: (jnp.where(j == 0, b, B - 1), 0, 0)),
            pl.BlockSpec((8, 3 * D), lambda j, b, t, o: (0, 0)),
            pl.BlockSpec((1, D), lambda j, b, t, o: (0, 0)),
            ANY,
            pl.BlockSpec((TI, RET_DK), lambda j, b, t, o: (t, 0)),
            pl.BlockSpec((TI, RET_DK), lambda j, b, t, o: (t, 0)),
        ],
        out_specs=(pl.BlockSpec((None, TI, D), lambda j, b, t, o: (b, t, o[j])),
                   pl.BlockSpec((None, TI, D), lambda j, b, t, o: (
                       jnp.where(j == 0, b, B - 1), jnp.where(j == 0, t, nt - 1), 0)), ANY),
        scratch_shapes=[pltpu.VMEM((N_SHARD, D, D), BF16), pltpu.VMEM((B * nt, TI, D), BF16),
                        pltpu.SemaphoreType.DMA((6,)), pltpu.SemaphoreType.DMA((6,)),
                        pltpu.SemaphoreType.DMA((5,))])
    return pl.pallas_call(
        body, name="in_proj", grid_spec=grid_spec,
        out_shape=(jax.ShapeDtypeStruct((B, T, 4 * D), BF16), jax.ShapeDtypeStruct((B, T, D), BF16),
                   jax.ShapeDtypeStruct((N_SHARD, D, D), BF16)),
        compiler_params=_params(("arbitrary",) * 3, vmem_mb=56))(order, x, ctx, mod, norm_g, win_b, cos2, sin2)


def _na_specs(L, T, rows, nh=2):
    nm = rows // 4
    w = nh * NA_DH
    per = 512 // w
    q_spec = pl.BlockSpec((None, TQ, w), lambda hp, b, m: (b, m, hp))
    k_spec = pl.BlockSpec((None, T, w), lambda hp, b, m: (b, 0, per + hp))
    v_spec = pl.BlockSpec((None, T, w), lambda hp, b, m: (b, 0, 2 * per + hp))
    g_spec = pl.BlockSpec((None, TQ, w), lambda hp, b, m: (b, m, 3 * per + hp))
    bias_spec = pl.BlockSpec((nh, 3, TQ, KW), lambda hp, b, m: (hp, 0, 0, 0))
    return nm, q_spec, k_spec, v_spec, g_spec, bias_spec


def _na_tile(m, nm, rows):
    typ = jnp.where(m == 0, 0, jnp.where(m == nm - 1, 2, 1))
    start = pl.multiple_of(jnp.clip(4 * m - 4, 0, rows - 12) * GRID_W, TQ)
    return typ, start


def _na_fwd_call(P, bias, L, LC):
    B, T, _ = P.shape
    rows = L // GRID_W
    NH = 4
    nm, q_spec, k_spec, v_spec, g_spec, bias_spec = _na_specs(L, T, rows, NH)

    def body(q_ref, k_ref, v_ref, g_ref, bias_ref, y_ref, o_ref):
        typ, start = _na_tile(pl.program_id(2), nm, rows)
        for hh in range(NH):
            ln = slice(hh * NA_DH, (hh + 1) * NA_DH)
            q = q_ref[:, ln]
            kw, vw = k_ref[pl.ds(start, KW), ln], v_ref[pl.ds(start, KW), ln]
            kc, vc = k_ref[L:L + LC, ln], v_ref[L:L + LC, ln]
            s1 = _dot_nt(q, kw) + bias_ref[hh, typ]
            s2 = _dot_nt(q, kc)
            mx = jnp.maximum(jnp.max(s1, axis=-1, keepdims=True), jnp.max(s2, axis=-1, keepdims=True))
            p1, p2 = jnp.exp(s1 - mx), jnp.exp(s2 - mx)
            inv = 1.0 / (jnp.sum(p1, axis=-1, keepdims=True) + jnp.sum(p2, axis=-1, keepdims=True))
            o = (_dot(p1.astype(BF16), vw) + _dot(p2.astype(BF16), vc)) * inv
            g = g_ref[:, ln].astype(F32)
            o_ref[:, ln] = o.astype(BF16)
            y_ref[:, ln] = (o * (g * _sigmoid(g))).astype(BF16)

    tile = pl.BlockSpec((None, TQ, NH * NA_DH), lambda hp, b, m: (b, m, hp))
    return pl.pallas_call(
        body, name="na_fwd", grid=(8 // NH, B, nm),
        in_specs=[q_spec, k_spec, v_spec, g_spec, bias_spec],
        out_specs=(tile, tile),
        out_shape=(jax.ShapeDtypeStruct((B, L, 512), BF16),) * 2,
        compiler_params=_params(("arbitrary",) * 3))(P, P, P, P, bias)


def _na_bwd_call(P, bias, dY, o_na, L, LC):
    B, T, _ = P.shape
    rows = L // GRID_W
    NH = 4
    W = NH * NA_DH
    nm, q_spec, k_spec, v_spec, g_spec, bias_spec = _na_specs(L, T, rows, NH)
    scale = NA_DH ** -0.5

    RB = 32

    def body(q_ref, k_ref, v_ref, g_ref, bias_ref, dy_ref, o_ref, dq_ref, dg_ref, dk_ref, dv_ref, dt_ref,
             db_ref, s1_ref, s2_ref, dp1_ref, dp2_ref, p1_ref, p2_ref, ds1_ref, ds2_ref, dkt_ref, dvt_ref):
        b, m = pl.program_id(1), pl.program_id(2)
        typ, start = _na_tile(m, nm, rows)

        @pl.when(m == 0)
        def _():
            dkt_ref[...] = jnp.zeros_like(dkt_ref)
            dvt_ref[...] = jnp.zeros_like(dvt_ref)

        @pl.when((m == 0) & (b == 0))
        def _():
            db_ref[...] = jnp.zeros_like(db_ref)

        for hh in range(NH):
            ln = slice(hh * NA_DH, (hh + 1) * NA_DH)
            q = q_ref[:, ln]
            kw, vw = k_ref[pl.ds(start, KW), ln], v_ref[pl.ds(start, KW), ln]
            kc, vc = k_ref[L:L + LC, ln], v_ref[L:L + LC, ln]
            g = g_ref[:, ln].astype(F32)
            sg = _sigmoid(g)
            dy = dy_ref[:, ln].astype(F32)
            do = (dy * (g * sg)).astype(BF16)
            s1_ref[hh] = _dot_nt(q, kw)
            s2_ref[hh] = _dot_nt(q, kc)
            dp1_ref[hh] = _dot_nt(do, vw)
            dp2_ref[hh] = _dot_nt(do, vc)

            def rows_pass(r, carry, hh=hh):
                rw = pl.ds(pl.multiple_of(r * RB, RB), RB)
                a = s1_ref[hh, rw, :] + bias_ref[hh, typ, rw, :]
                c = s2_ref[hh, rw, :]
                mx = jnp.maximum(jnp.max(a, axis=-1, keepdims=True), jnp.max(c, axis=-1, keepdims=True))
                e1, e2 = jnp.exp(a - mx), jnp.exp(c - mx)
                inv = 1.0 / (jnp.sum(e1, axis=-1, keepdims=True) + jnp.sum(e2, axis=-1, keepdims=True))
                p1, p2 = e1 * inv, e2 * inv
                p1_ref[hh, rw, :] = p1.astype(BF16)
                p2_ref[hh, rw, :] = p2.astype(BF16)
                dp1, dp2 = dp1_ref[hh, rw, :], dp2_ref[hh, rw, :]
                delta = jnp.sum(p1 * dp1, axis=-1, keepdims=True) + jnp.sum(p2 * dp2, axis=-1, keepdims=True)
                ds1 = p1 * (dp1 - delta)
                db_ref[hh, typ, rw, :] += ds1
                ds1_ref[hh, rw, :] = ds1.astype(BF16)
                ds2_ref[hh, rw, :] = (p2 * (dp2 - delta)).astype(BF16)
                return carry

            lax.fori_loop(0, TQ // RB, rows_pass, 0, unroll=True)
            p1b, p2b, ds1b, ds2b = p1_ref[hh], p2_ref[hh], ds1_ref[hh], ds2_ref[hh]
            dg_ref[:, ln] = (dy * o_ref[:, ln].astype(F32) * (sg * (1.0 + g * (1.0 - sg)))).astype(BF16)
            dq_ref[:, ln] = ((_dot(ds1b, kw) + _dot(ds2b, kc)) * scale).astype(BF16)
            dkt_ref[ln, pl.ds(start, KW)] += _dot_tn(q, ds1b)
            dvt_ref[ln, pl.ds(start, KW)] += _dot_tn(do, p1b)
            dkt_ref[ln, L:L + LC] += _dot_tn(q, ds2b)
            dvt_ref[ln, L:L + LC] += _dot_tn(do, p2b)

        @pl.when(m == nm - 1)
        def _():
            dk_ref[...] = dkt_ref[...].T
            dv_ref[...] = dvt_ref[...].T

        @pl.when((m == nm - 1) & (b == B - 1))
        def _():
            for hh in range(NH):
                for dr, t in _bias_tile_sums(db_ref, hh).items():
                    dt_ref[hh, dr + 7] = t

    tile = pl.BlockSpec((None, TQ, W), lambda hp, b, m: (b, m, hp))
    kv_out = pl.BlockSpec((None, T, W), lambda hp, b, m: (b, 0, hp))
    wide, narrow = (NH, TQ, KW), (NH, TQ, LC)
    return pl.pallas_call(
        body, name="na_bwd", grid=(8 // NH, B, nm),
        in_specs=[q_spec, k_spec, v_spec, g_spec, bias_spec, tile, tile],
        out_specs=(tile, tile, kv_out, kv_out,
                   pl.BlockSpec((NH, 15, GRID_W, GRID_W), lambda hp, b, m: (hp, 0, 0, 0))),
        out_shape=(jax.ShapeDtypeStruct((B, L, 512), BF16), jax.ShapeDtypeStruct((B, L, 512), BF16),
                   jax.ShapeDtypeStruct((B, T, 512), F32), jax.ShapeDtypeStruct((B, T, 512), F32),
                   jax.ShapeDtypeStruct((bias.shape[0], 15, GRID_W, GRID_W), F32)),
        scratch_shapes=[pltpu.VMEM((NH,) + bias.shape[1:], F32),
                        pltpu.VMEM(wide, F32), pltpu.VMEM(narrow, F32), pltpu.VMEM(wide, F32), pltpu.VMEM(narrow, F32),
                        pltpu.VMEM(wide, BF16), pltpu.VMEM(narrow, BF16), pltpu.VMEM(wide, BF16),
                        pltpu.VMEM(narrow, BF16), pltpu.VMEM((W, T), F32), pltpu.VMEM((W, T), F32)],
        compiler_params=_params(("arbitrary",) * 3, vmem_mb=60))(P, P, P, P, bias, dY, o_na)


def _head_scalar(dec_ref, h):
    lane = lax.broadcasted_iota(jnp.int32, dec_ref.shape, 1)
    return -jnp.sum(jnp.where(lane == h, jnp.exp(dec_ref[...]), 0.0), axis=1, keepdims=True)


def _chunk_decay(lgf, lgb):
    tau = lax.broadcasted_iota(jnp.int32, (TQ, 1), 0).astype(F32)
    sig = lax.broadcasted_iota(jnp.int32, (1, TQ), 1).astype(F32)
    dist = tau - sig
    dm = jnp.exp(dist * jnp.where(dist > 0, lgf, -lgb)) * jnp.where(dist == 0, 2.0, 1.0)
    return tau, dist, dm


def _ret_states_call(P, dec_f, dec_b, L, LC):
    B, T, _ = P.shape
    n = L // TQ

    def body(df_ref, db_ref, k_ref, v_ref, sf_ref, sb_ref):
        h = pl.program_id(1)
        lgf, lgb = _head_scalar(df_ref, h), _head_scalar(db_ref, h)
        tau = lax.broadcasted_iota(jnp.int32, (TQ, 1), 0).astype(F32)
        jc = lax.broadcasted_iota(jnp.int32, (LC, 1), 0).astype(F32)
        wf, wb = jnp.exp(lgf * (TQ - 1.0 - tau)), jnp.exp(lgb * tau)
        gcf, gcb = jnp.exp(lgf * float(TQ)), jnp.exp(lgb * float(TQ))
        kc, vc = k_ref[L:L + LC, :].astype(F32), v_ref[L:L + LC, :]

        def chunk_state(i, w):
            ks = pl.multiple_of(i * TQ, TQ)
            return _dot_tn((k_ref[pl.ds(ks, TQ), :].astype(F32) * w).astype(BF16), v_ref[pl.ds(ks, TQ), :])

        def fwd(i, s):
            sf_ref[i] = s
            return gcf * s + chunk_state(i, wf)

        lax.fori_loop(0, n, fwd, _dot_tn((kc * jnp.exp(lgf * (LC - 1.0 - jc))).astype(BF16), vc), unroll=True)

        def bwd(r, s):
            i = n - 1 - r
            sb_ref[i] = s
            return gcb * s + chunk_state(i, wb)

        lax.fori_loop(0, n, bwd, _dot_tn((kc * jnp.exp(lgb * jc)).astype(BF16), vc), unroll=True)

    st = pl.BlockSpec((None, None, n, RET_DK, RET_DK), lambda b, h: (b, h, 0, 0, 0))
    return pl.pallas_call(
        body, name="ret_states", grid=(B, 4),
        in_specs=[pl.BlockSpec((1, 4), lambda b, h: (0, 0)), pl.BlockSpec((1, 4), lambda b, h: (0, 0)),
                  pl.BlockSpec((None, T, 128), lambda b, h: (b, 0, 20 + h)),
                  pl.BlockSpec((None, T, 128), lambda b, h: (b, 0, 24 + h))],
        out_specs=(st, st),
        out_shape=(jax.ShapeDtypeStruct((B, 4, n, RET_DK, RET_DK), F32),) * 2,
        compiler_params=_params(("arbitrary",) * 2))(dec_f, dec_b, P, P)


def _retc_fwd_call(P, sf, sb, dec_f, dec_b, ret_norm_g, L):
    B, T, _ = P.shape
    sec = lambda k: pl.BlockSpec((None, TQ, 512), lambda b, i: (b, i, k))
    dec_spec = pl.BlockSpec((1, 4), lambda b, i: (0, 0))
    st_spec = pl.BlockSpec((None, 4, None, RET_DK, RET_DK), lambda b, i: (b, 0, i, 0, 0))

    def body(df_ref, db_ref, q_ref, k_ref, v_ref, g_ref, gn_ref, sf_ref, sb_ref, y_ref, o_ref):
        for h in range(4):
            ln = slice(h * RET_DK, (h + 1) * RET_DK)
            lgf, lgb = _head_scalar(df_ref, h), _head_scalar(db_ref, h)
            tau, _, dm = _chunk_decay(lgf, lgb)
            q = q_ref[:, ln]
            qf = q.astype(F32)
            acc = _dot((_dot_nt(q, k_ref[:, ln]) * dm).astype(BF16), v_ref[:, ln])
            acc = acc + _dot((qf * jnp.exp(lgf * (tau + 1.0))).astype(BF16), sf_ref[h].astype(BF16))
            acc = acc + _dot((qf * jnp.exp(lgb * (TQ - tau))).astype(BF16), sb_ref[h].astype(BF16))
            o_ref[:, ln] = acc
            rn = lax.rsqrt(jnp.mean(acc * acc, axis=-1, keepdims=True) + EPS)
            g = g_ref[:, ln].astype(F32)
            y_ref[:, ln] = ((acc * rn * gn_ref[:, ln]) * (g * _sigmoid(g))).astype(BF16)

    tile = pl.BlockSpec((None, TQ, 512), lambda b, i: (b, i, 0))
    return pl.pallas_call(
        body, name="ret_fwd", grid=(B, L // TQ),
        in_specs=[dec_spec, dec_spec, sec(4), sec(5), sec(6), sec(7),
                  pl.BlockSpec((1, 512), lambda b, i: (0, 0)), st_spec, st_spec],
        out_specs=(tile, tile),
        out_shape=(jax.ShapeDtypeStruct((B, L, 512), BF16), jax.ShapeDtypeStruct((B, L, 512), F32)),
        compiler_params=_params(("arbitrary",) * 2))(dec_f, dec_b, P, P, P, P, ret_norm_g, sf, sb)


def _retc_bwd_call(P, sf, sb, dec_f, dec_b, ret_norm_g, o_ret, dY, cos2, sin2, L, LC):
    B, T, _ = P.shape
    n = L // TQ
    C = float(TQ)
    kscale = RET_DK ** -0.5
    st_spec = pl.BlockSpec((None, 4, n, RET_DK, RET_DK), lambda b, i: (b, 0, 0, 0, 0))

    def body(df_ref, db_ref, q_ref, k_ref, v_ref, g_ref, gn_ref, o_ref, dy_ref, cos_ref, sin_ref, sf_ref, sb_ref,
             dq_ref, dg_ref, dk_ref, dv_ref, dgn_ref, dlg_ref, dsf_ref, dsb_ref):
        i = pl.program_id(1)

        @pl.when(i == 0)
        def _():
            dk_ref[...] = jnp.zeros_like(dk_ref)
            dv_ref[...] = jnp.zeros_like(dv_ref)
            dgn_ref[...] = jnp.zeros_like(dgn_ref)
            dlg_ref[...] = jnp.zeros_like(dlg_ref)

        rows = pl.ds(pl.multiple_of(i * TQ, TQ), TQ)
        cs, sn = cos_ref[rows, :], sin_ref[rows, :]

        def one_head(h):
            ln = slice(h * RET_DK, (h + 1) * RET_DK)
            lgf, lgb = _head_scalar(df_ref, h), _head_scalar(db_ref, h)
            tau, dist, dm = _chunk_decay(lgf, lgb)

            def add_lg(row, x):
                csum = jnp.sum(x, axis=0, keepdims=True)
                tot = csum[:, 0:128]
                for part in range(1, x.shape[1] // 128):
                    tot = tot + csum[:, part * 128:(part + 1) * 128]
                dlg_ref[h, row:row + 1, :] += tot

            q = q_ref[:, ln]
            qf = q.astype(F32)
            o = o_ref[:, ln]
            g = g_ref[:, ln].astype(F32)
            dy = dy_ref[:, ln].astype(F32)
            gn = gn_ref[:, ln]
            sg = _sigmoid(g)
            rn = lax.rsqrt(jnp.mean(o * o, axis=-1, keepdims=True) + EPS)
            nrm = o * rn
            dg_ref[:, ln] = (dy * (nrm * gn) * (sg * (1.0 + g * (1.0 - sg)))).astype(BF16)
            dhn = dy * (g * sg)
            dgn_ref[:, ln] += jnp.sum(dhn * nrm, axis=0, keepdims=True)
            dnrm = dhn * gn
            do = rn * (dnrm - nrm * jnp.mean(dnrm * nrm, axis=-1, keepdims=True))
            dob = do.astype(BF16)
            ki, vi = k_ref[rows, ln], v_ref[rows, ln]
            s = _dot_nt(q, ki)
            dsv = _dot_nt(dob, vi)
            dsb = (dsv * dm).astype(BF16)
            dk_ref[rows, ln] += _dot_tn(dsb, q)
            dv_ref[rows, ln] += _dot_tn((s * dm).astype(BF16), dob)
            xw = s * dsv * dm * jnp.abs(dist)
            fpart = jnp.where(dist > 0, xw, 0.0)
            add_lg(0, fpart)
            add_lg(1, xw - fpart)
            dq = _dot(dsb, ki)
            af, ab = jnp.exp(lgf * (tau + 1.0)), jnp.exp(lgb * (C - tau))
            qa, qb = (qf * af).astype(BF16), (qf * ab).astype(BF16)
            sfi, sbi = sf_ref[h, i].astype(BF16), sb_ref[h, i].astype(BF16)
            dq = dq + af * _dot_nt(dob, sfi) + ab * _dot_nt(dob, sbi)
            dsf_ref[h, i] = _dot_tn(qa, dob)
            dsb_ref[h, i] = _dot_tn(qb, dob)
            add_lg(0, (tau + 1.0) * (_dot(qa, sfi) * do))
            add_lg(1, (C - tau) * (_dot(qb, sbi) * do))
            dq_ref[:, ln] = (dq * cs - pltpu.roll(dq, 64, 1) * sn).astype(BF16)

            @pl.when(i == n - 1)
            def _():
                jc = lax.broadcasted_iota(jnp.int32, (LC, 1), 0).astype(F32)
                crow = pl.ds(L, LC)

                def through_state(rws, w, dw, gst, row):
                    kk, vv = k_ref[rws, ln].astype(F32), v_ref[rws, ln]
                    gb = gst.astype(BF16)
                    vg = _dot_nt(vv, gb)
                    kw = kk * w
                    dk_ref[rws, ln] += w * vg
                    dv_ref[rws, ln] += _dot(kw.astype(BF16), gb)
                    add_lg(row, dw * (kw * vg))

                def scan(gc, w, dw, st_ref, dst_ref, order, row):
                    def step(r, gst):
                        j = order(r)
                        through_state(pl.ds(pl.multiple_of(j * TQ, TQ), TQ), w, dw, gst, row)
                        add_lg(row, (C * gc) * (gst * st_ref[h, j]))
                        return dst_ref[h, j] + gc * gst
                    return lax.fori_loop(0, n, step, jnp.zeros((RET_DK, RET_DK), F32), unroll=True)

                gcf, gcb = jnp.exp(lgf * C), jnp.exp(lgb * C)
                g0 = scan(gcf, jnp.exp(lgf * (C - 1.0 - tau)), C - 1.0 - tau, sf_ref, dsf_ref,
                          lambda r: n - 1 - r, 0)
                through_state(crow, jnp.exp(lgf * (LC - 1.0 - jc)), LC - 1.0 - jc, g0, 0)
                g1 = scan(gcb, jnp.exp(lgb * tau), tau, sb_ref, dsb_ref, lambda r: r, 1)
                through_state(crow, jnp.exp(lgb * jc), jc, g1, 1)
                dk = dk_ref[:, ln]
                dk_ref[:, ln] = (dk * cos_ref[...] - pltpu.roll(dk, 64, 1) * sin_ref[...]) * kscale

        for h in range(4):
            one_head(h)

    sec = lambda k: pl.BlockSpec((None, TQ, 512), lambda b, i: (b, i, k))
    full = lambda k: pl.BlockSpec((None, T, 512), lambda b, i: (b, 0, k))
    dec_spec = pl.BlockSpec((1, 4), lambda b, i: (0, 0))
    tab = pl.BlockSpec((T, RET_DK), lambda b, i: (0, 0))
    return pl.pallas_call(
        body, name="ret_bwd", grid=(B, n),
        in_specs=[dec_spec, dec_spec, sec(4), full(5), full(6), sec(7),
                  pl.BlockSpec((1, 512), lambda b, i: (0, 0)), sec(0), sec(1), tab, tab, st_spec, st_spec],
        out_specs=(sec(0), sec(0), full(0), full(0),
                   pl.BlockSpec((None, 1, 512), lambda b, i: (b, 0, 0)),
                   pl.BlockSpec((None, 4, 8, 128), lambda b, i: (b, 0, 0, 0))),
        out_shape=(jax.ShapeDtypeStruct((B, L, 512), BF16), jax.ShapeDtypeStruct((B, L, 512), BF16),
                   jax.ShapeDtypeStruct((B, T, 512), F32), jax.ShapeDtypeStruct((B, T, 512), F32),
                   jax.ShapeDtypeStruct((B, 1, 512), F32), jax.ShapeDtypeStruct((B, 4, 8, 128), F32)),
        scratch_shapes=[pltpu.VMEM((4, n, RET_DK, RET_DK), F32), pltpu.VMEM((4, n, RET_DK, RET_DK), F32)],
        compiler_params=_params(("arbitrary",) * 2, vmem_mb=56))(
            dec_f, dec_b, P, P, P, P, ret_norm_g, o_ret, dY, cos2, sin2, sf, sb)


def _out_call(y_na, y_ret, x, target, mod, final_g, wout_f):
    B, L, _ = x.shape

    def body(yn_ref, yr_ref, x_ref, t_ref, mod_ref, gf_ref, w_ref, dy_ref, dx2_ref, dw_ref, sm_ref):
        b, i = pl.program_id(0), pl.program_id(1)

        @pl.when((b == 0) & (i == 0))
        def _():
            dw_ref[...] = jnp.zeros_like(dw_ref)
            sm_ref[...] = jnp.zeros_like(sm_ref)

        gate = mod_ref[pl.ds(b, 1), 2 * D:3 * D]
        gf = gf_ref[...]
        yn, yr = yn_ref[...], yr_ref[...]
        ylat = _dot(yn, w_ref[0:512, :]) + _dot(yr, w_ref[512:1024, :])
        x2 = x_ref[...] + gate * ylat
        r = lax.rsqrt(jnp.mean(x2 * x2, axis=-1, keepdims=True) + EPS)
        xr = x2 * r
        err = xr * gf - t_ref[...]
        sm_ref[1:2, :] += jnp.sum(err * err, axis=0, keepdims=True)
        dout = err * (1.0 / D)
        sm_ref[0:1, :] += jnp.sum(dout * xr, axis=0, keepdims=True)
        gd = dout * gf
        dx2 = r * (gd - xr * jnp.mean(gd * xr, axis=-1, keepdims=True))
        dx2_ref[...] = dx2
        sm_ref[pl.ds(2 + b, 1), :] += jnp.sum(dx2 * ylat, axis=0, keepdims=True)
        dyl = (gate * dx2).astype(BF16)
        dy_ref[:, 0:512] = _dot_nt(dyl, w_ref[0:512, :]).astype(BF16)
        dy_ref[:, 512:1024] = _dot_nt(dyl, w_ref[512:1024, :]).astype(BF16)
        dw_ref[0:512, :] += _dot_tn(yn, dyl)
        dw_ref[512:1024, :] += _dot_tn(yr, dyl)

    half = pl.BlockSpec((None, TQ, 512), lambda b, i: (b, i, 0))
    full = pl.BlockSpec((None, TQ, D), lambda b, i: (b, i, 0))
    return pl.pallas_call(
        body, name="out_proj_loss", grid=(B, L // TQ),
        in_specs=[half, half, full, full,
                  pl.BlockSpec((8, 3 * D), lambda b, i: (0, 0)),
                  pl.BlockSpec((1, D), lambda b, i: (0, 0)),
                  pl.BlockSpec((D, D), lambda b, i: (0, 0))],
        out_specs=(full, full, pl.BlockSpec((D, D), lambda b, i: (0, 0)),
                   pl.BlockSpec((8, D), lambda b, i: (0, 0))),
        out_shape=(jax.ShapeDtypeStruct((B, L, D), BF16), jax.ShapeDtypeStruct((B, L, D), F32),
                   jax.ShapeDtypeStruct((D, D), F32), jax.ShapeDtypeStruct((8, D), F32)),
        compiler_params=_params(("arbitrary",) * 2))(y_na, y_ret, x, target, mod, final_g, wout_f)


def _dh_call(dsec, win_f, x, ctx, dx2, mod, norm_g, cp_in, cp_out):
    B, L, _ = x.shape
    LC = ctx.shape[1]
    nl = L // TQ

    def body(d0, d1, d2, d3, d4, d5, d6, d7, w_ref, x_ref, ctx_ref, dx2_ref, mod_ref, g_ref, cpi_ref, cpo_ref,
             gx_ref, sm_ref, sli_ref, slo_ref, ssem, rsem, lsem):
        drefs = (d0, d1, d2, d3, d4, d5, d6, d7)
        b, t = pl.program_id(0), pl.program_id(1)
        is_lat = t < nl

        @pl.when((b == 0) & (t == 0))
        def _():
            sm_ref[...] = jnp.zeros_like(sm_ref)

        def dh_of(secs):
            acc = jnp.zeros((TQ, D), F32)
            for sec in secs:
                s, half = divmod(sec, 2)
                acc = acc + _dot_nt(drefs[sec][...].astype(BF16), w_ref[s, :, half * 512:(half + 1) * 512])
            return acc

        def norm_bwd(dh, xt, mrow):
            scale = mrow[:, D:2 * D]
            g = g_ref[...]
            rstd = lax.rsqrt(jnp.mean(xt * xt, axis=-1, keepdims=True) + EPS)
            xn = xt * rstd
            dshift = jnp.sum(dh, axis=0, keepdims=True)
            dscale = jnp.sum(dh * (xn * g), axis=0, keepdims=True)
            dhn = dh * (1.0 + scale)
            sm_ref[0:1, :] += jnp.sum(dhn * xn, axis=0, keepdims=True)
            dxn = dhn * g
            dx = rstd * (dxn - xn * jnp.mean(dxn * xn, axis=-1, keepdims=True))
            return dshift, dscale, dx

        @pl.when(is_lat)
        def _():
            dshift, dscale, dx = norm_bwd(dh_of(range(8)), x_ref[...], mod_ref[pl.ds(b, 1), :])
            sm_ref[pl.ds(3 + b, 1), :] += dshift
            sm_ref[pl.ds(3 + B + b, 1), :] += dscale
            gx_ref[...] = dx2_ref[...] + dx

        @pl.when(jnp.logical_not(is_lat))
        def _():
            dshift, dscale, _ = norm_bwd(dh_of((1, 2, 5, 6)), ctx_ref[...], mod_ref[B:B + 1, :])
            sm_ref[1:2, :] += dshift
            sm_ref[2:3, :] += dscale

        mx, my, mc = _mesh_pos()
        s = 2 * mx + my
        cps, sls = (cpi_ref, cpo_ref), (sli_ref, slo_ref)
        own = [pltpu.make_async_copy(cps[a].at[s], sls[a].at[s], lsem.at[a]) for a in range(2)]
        sends, recvs, k = [], [], 0
        for px, py in _other_chips(mx, my):
            ps = 2 * px + py
            for a in range(2):
                sends.append(_remote(cps[a].at[ps], sls[a].at[s], ssem, rsem, k, (px, py, mc)))
                recvs.append(_remote(cps[a].at[s], sls[a].at[ps], ssem, rsem, k, (px, py, mc)))
                k += 1

        @pl.when((b == 0) & (t == 0))
        def _():
            for cp in own + sends:
                cp.start()

        @pl.when((b == B - 1) & (t == nl))
        def _():
            _finish(own, sends, recvs)

    lat = lambda b, t: (b, jnp.minimum(t, nl - 1), 0)
    tok = lambda b, t: (b, t, 0)
    sec_specs = [pl.BlockSpec((None, TQ, 512), lat if sec in (0, 3, 4, 7) else tok) for sec in range(8)]
    return pl.pallas_call(
        body, name="dh_norm_bwd", grid=(B, nl + 1),
        in_specs=sec_specs + [
            pl.BlockSpec((N_SHARD, D, D), lambda b, t: (0, 0, 0)),
            pl.BlockSpec((None, TQ, D), lat),
            pl.BlockSpec((None, LC, D), lambda b, t: (b, 0, 0)),
            pl.BlockSpec((None, TQ, D), lat),
            pl.BlockSpec((8, 3 * D), lambda b, t: (0, 0)),
            pl.BlockSpec((1, D), lambda b, t: (0, 0)), ANY, ANY],
        out_specs=(pl.BlockSpec((None, TQ, D), lat), pl.BlockSpec((8, D), lambda b, t: (0, 0)), ANY, ANY),
        out_shape=(jax.ShapeDtypeStruct((B, L, D), F32), jax.ShapeDtypeStruct((8, D), F32),
                   jax.ShapeDtypeStruct(cp_in.shape, cp_in.dtype), jax.ShapeDtypeStruct(cp_out.shape, cp_out.dtype)),
        scratch_shapes=[pltpu.SemaphoreType.DMA((6,)), pltpu.SemaphoreType.DMA((6,)),
                        pltpu.SemaphoreType.DMA((2,))],
        compiler_params=_params(("arbitrary",) * 2))(*dsec, win_f, x, ctx, dx2, mod, norm_g, cp_in, cp_out)


def _dw_call(dsec, h, L):
    B, T, _ = h.shape
    nl = L // TQ

    def body(d0, d1, d2, d3, d4, d5, d6, d7, h_ref, dw_ref, acc_ref):
        drefs = (d0, d1, d2, d3, d4, d5, d6, d7)
        b, t = pl.program_id(0), pl.program_id(1)

        @pl.when((b == 0) & (t == 0))
        def _():
            acc_ref[...] = jnp.zeros_like(acc_ref)

        hb = h_ref[...]

        def add(secs):
            for sec in secs:
                s, half = divmod(sec, 2)
                acc_ref[s, :, half * 512:(half + 1) * 512] += _dot_tn(hb, drefs[sec][...].astype(BF16))

        @pl.when(t < nl)
        def _():
            add(range(8))

        @pl.when(t >= nl)
        def _():
            add((1, 2, 5, 6))

        @pl.when((b == B - 1) & (t == nl))
        def _():
            dw_ref[...] = acc_ref[...].astype(BF16)

    lat = lambda b, t: (b, jnp.minimum(t, nl - 1), 0)
    tok = lambda b, t: (b, t, 0)
    sec_specs = [pl.BlockSpec((None, TQ, 512), lat if sec in (0, 3, 4, 7) else tok) for sec in range(8)]
    return pl.pallas_call(
        body, name="dw_in", grid=(B, nl + 1),
        in_specs=sec_specs + [pl.BlockSpec((None, TQ, D), tok)],
        out_specs=pl.BlockSpec((N_SHARD, D, D), lambda b, t: (0, 0, 0)),
        out_shape=jax.ShapeDtypeStruct((N_SHARD, D, D), BF16),
        scratch_shapes=[pltpu.VMEM((N_SHARD, D, D), F32)],
        compiler_params=_params(("arbitrary",) * 2, vmem_mb=56))(*dsec, h)


def _mesh_pos():
    return lax.axis_index("x"), lax.axis_index("y"), lax.axis_index("c")


def _flip(v, f):
    return 1 - v if f else v


def _remote(src, dst, ssem, rsem, k, peer):
    return pltpu.make_async_remote_copy(src_ref=src, dst_ref=dst, send_sem=ssem.at[k], recv_sem=rsem.at[k],
                                        device_id=peer, device_id_type=MESH)


def _other_chips(x, y):
    return [(_flip(x, fx), _flip(y, fy)) for fx, fy in ((1, 0), (0, 1), (1, 1))]


def _all_to_all_small(src, dst_all, ssem, rsem, k0, x, y, cc):
    me = 4 * x + 2 * y + cc
    sends, recvs = [], []
    for f in range(1, N_DEV):
        px, py, pc = _flip(x, f & 4), _flip(y, f & 2), _flip(cc, f & 1)
        sends.append(_remote(src, dst_all.at[me], ssem, rsem, k0 + f - 1, (px, py, pc)))
        recvs.append(_remote(src, dst_all.at[4 * px + 2 * py + pc], ssem, rsem, k0 + f - 1, (px, py, pc)))
    return sends, recvs


def _finish(local, sends, recvs):
    for cp in recvs:
        cp.wait_recv()
    for cp in sends:
        cp.wait_send()
    for cp in local:
        cp.wait()


def _gather_call(wout_b, wada_b, c, rpb_flat):
    arrs = (wout_b, wada_b)
    na = len(arrs)
    hrs = [a.shape[0] // 2 for a in arrs]

    def body(wout, wada, c_ref, r_ref, wout_f, wada_f, c_all, bias_ref, et_ref, ssem, rsem, lsem):
        x, y, cc = _mesh_pos()
        s, me = 2 * x + y, 4 * x + 2 * y + cc
        sib = (x, y, 1 - cc)
        srcs, dsts = (wout, wada), (wout_f, wada_f)

        def half(a, shard, hc):
            return dsts[a].at[shard, pl.ds(hc * hrs[a], hrs[a])]

        local = [pltpu.make_async_copy(srcs[a], dsts[a].at[s], lsem.at[a]) for a in range(na)]
        local.append(pltpu.make_async_copy(c_ref, c_all.at[me], lsem.at[na]))
        ici_send, ici_recv, fwd_send, fwd_recv, k = [], [], [], [], 0
        for px, py in _other_chips(x, y):
            ps = 2 * px + py
            for a in range(na):
                mine = srcs[a].at[pl.ds(cc * hrs[a], hrs[a])]
                ici_send.append(_remote(mine, half(a, s, cc), ssem, rsem, k, (px, py, cc)))
                ici_recv.append(_remote(mine, half(a, ps, cc), ssem, rsem, k, (px, py, cc)))
                fwd_send.append(_remote(half(a, ps, cc), half(a, ps, cc), ssem, rsem, 3 * na + k, sib))
                fwd_recv.append(_remote(half(a, ps, 1 - cc), half(a, ps, 1 - cc), ssem, rsem, 3 * na + k, sib))
                k += 1
        c_send, c_recv = _all_to_all_small(c_ref, c_all, ssem, rsem, 6 * na, x, y, cc)
        for cp in local + ici_send + c_send:
            cp.start()
        _bias_body(r_ref, bias_ref, et_ref)
        for got, fwd in zip(ici_recv, fwd_send):
            got.wait_recv()
            fwd.start()
        _finish(local, ici_send + fwd_send + c_send, fwd_recv + c_recv)

    return pl.pallas_call(
        body, name="weight_gather",
        in_specs=[pl.BlockSpec(memory_space=pltpu.VMEM)] * 3 + [pl.BlockSpec(memory_space=pltpu.SMEM)],
        out_specs=(pl.BlockSpec(memory_space=pltpu.VMEM),) * 4,
        out_shape=tuple(jax.ShapeDtypeStruct((N_SHARD,) + a.shape, a.dtype) for a in arrs)
        + (jax.ShapeDtypeStruct((N_DEV,) + c.shape, c.dtype),
           jax.ShapeDtypeStruct((rpb_flat.shape[0], 3, TQ, KW), F32)),
        scratch_shapes=[pltpu.VMEM((15, GRID_W, GRID_W), F32),
                        pltpu.SemaphoreType.DMA((6 * na + 7,)), pltpu.SemaphoreType.DMA((6 * na + 7,)),
                        pltpu.SemaphoreType.DMA((na + 1,))],
        compiler_params=pltpu.CompilerParams(vmem_limit_bytes=56 << 20))(wout_b, wada_b, c, rpb_flat)


VROWS = 32


def _grad_halves_call(dwin_b, dwout_b, dbias, dlg):
    arrs = (dwin_b, dwout_b)
    hrs = [a.shape[1] // 2 for a in arrs]

    def body(din, dout, db_ref, dlg_ref, cp_in, cp_out, drpb_ref, dlgo_ref, got_in, got_out, p_ref, ssem, rsem):
        x, y, cc = _mesh_pos()
        sib = (x, y, 1 - cc)
        srcs, gots, cps = (din, dout), (got_in, got_out), (cp_in, cp_out)
        halves = [_remote(srcs[a].at[:, pl.ds((1 - cc) * hrs[a], hrs[a])], gots[a], ssem, rsem, a, sib)
                  for a in range(2)]
        for cp in halves:
            cp.start()
        _small_reduce_body(db_ref, dlg_ref, drpb_ref, dlgo_ref, p_ref)
        for cp in halves:
            cp.wait_recv()
        for a in range(2):
            for j in range(N_SHARD):
                def add(i, carry, a=a, j=j):
                    r = pl.multiple_of(i * VROWS, VROWS)
                    mine = srcs[a][j, pl.ds(pl.multiple_of(cc * hrs[a] + r, VROWS), VROWS), :].astype(F32)
                    cps[a][j, pl.ds(r, VROWS), :] = (
                        mine + gots[a][j, pl.ds(r, VROWS), :].astype(F32)).astype(BF16)
                    return carry
                lax.fori_loop(0, hrs[a] // VROWS, add, 0)
        for cp in halves:
            cp.wait_send()

    vmem = pl.BlockSpec(memory_space=pltpu.VMEM)
    half_shapes = [(N_SHARD, hrs[a], arrs[a].shape[2]) for a in range(2)]
    return pl.pallas_call(
        body, name="grad_halves",
        in_specs=[vmem] * 4, out_specs=(vmem,) * 4,
        out_shape=(jax.ShapeDtypeStruct(half_shapes[0], BF16), jax.ShapeDtypeStruct(half_shapes[1], BF16),
                   jax.ShapeDtypeStruct((dbias.shape[0], 16, 32), F32), jax.ShapeDtypeStruct((32, 128), F32)),
        scratch_shapes=[pltpu.VMEM(half_shapes[0], BF16), pltpu.VMEM(half_shapes[1], BF16),
                        pltpu.VMEM((32, GRID_W), F32),
                        pltpu.SemaphoreType.DMA((2,)), pltpu.SemaphoreType.DMA((2,))],
        compiler_params=pltpu.CompilerParams(vmem_limit_bytes=56 << 20))(dwin_b, dwout_b, dbias, dlg)


def _grad_finish_call(sl_in, sl_out, small):
    arrs = (sl_in, sl_out)

    def body(sin, sout, sm, gin, gout, sm_all, h_in, h_out, ssem, rsem, lsem):
        x, y, cc = _mesh_pos()
        me = 4 * x + 2 * y + cc
        sib = (x, y, 1 - cc)
        sls, hs, gs = (sin, sout), (h_in, h_out), (gin, gout)
        sm_send, sm_recv = _all_to_all_small(sm, sm_all, ssem, rsem, 2, x, y, cc)
        sm_own = pltpu.make_async_copy(sm, sm_all.at[me], lsem.at[0])
        for cp in sm_send + [sm_own]:
            cp.start()
        for a in range(2):
            def total(i, carry, a=a):
                rows = pl.ds(pl.multiple_of(i * VROWS, VROWS), VROWS)
                sl = sls[a]
                hs[a][rows, :] = ((sl[0, rows, :].astype(F32) + sl[1, rows, :].astype(F32))
                                  + sl[2, rows, :].astype(F32)) + sl[3, rows, :].astype(F32)
                return carry
            lax.fori_loop(0, arrs[a].shape[1] // VROWS, total, 0)
        mine = [pltpu.make_async_copy(hs[a], gs[a].at[cc], lsem.at[1 + a]) for a in range(2)]
        back = [_remote(hs[a], gs[a].at[cc], ssem, rsem, a, sib) for a in range(2)]
        back_recv = [_remote(hs[a], gs[a].at[1 - cc], ssem, rsem, a, sib) for a in range(2)]
        for cp in mine + back:
            cp.start()
        _finish(mine + [sm_own], back + sm_send, back_recv + sm_recv)

    vmem = pl.BlockSpec(memory_space=pltpu.VMEM)
    return pl.pallas_call(
        body, name="grad_finish",
        in_specs=[vmem] * 3, out_specs=(vmem,) * 3,
        out_shape=(jax.ShapeDtypeStruct((2,) + sl_in.shape[1:], F32),
                   jax.ShapeDtypeStruct((2,) + sl_out.shape[1:], F32),
                   jax.ShapeDtypeStruct((N_DEV,) + small.shape, F32)),
        scratch_shapes=[pltpu.VMEM(sl_in.shape[1:], F32), pltpu.VMEM(sl_out.shape[1:], F32),
                        pltpu.SemaphoreType.DMA((9,)), pltpu.SemaphoreType.DMA((9,)),
                        pltpu.SemaphoreType.DMA((3,))],
        compiler_params=pltpu.CompilerParams(vmem_limit_bytes=48 << 20))(sl_in, sl_out, small)


def _adamw(w, g, m, v):
    m = ADAM_B1 * m + (1.0 - ADAM_B1) * g
    v = ADAM_B2 * v + (1.0 - ADAM_B2) * (g * g)
    m_hat = m / (1.0 - ADAM_B1 ** ADAM_STEP)
    v_hat = v / (1.0 - ADAM_B2 ** ADAM_STEP)
    return -ADAM_LR * (m_hat / (jnp.sqrt(v_hat) + ADAM_EPS) + ADAM_WD * w), m, v


def _adam_call(w, m, v, g, name):
    R, C = w.shape
    tr = 256

    def body(w_ref, m_ref, v_ref, g_ref, go_ref, d_ref, mo_ref, vo_ref):
        g = g_ref[...]
        go_ref[...] = g
        d_ref[...], mo_ref[...], vo_ref[...] = _adamw(w_ref[...], g, m_ref[...], v_ref[...])

    spec = pl.BlockSpec((tr, C), lambda i: (i, 0))
    return pl.pallas_call(
        body, name=name, grid=(R // tr,), in_specs=[spec] * 4,
        out_specs=(spec,) * 4, out_shape=(jax.ShapeDtypeStruct((R, C), F32),) * 4,
        compiler_params=_params(("arbitrary",)))(w, m, v, g)


R_GF, R_NG, R_LOSS, R_RNG, R_LGF, R_LGB, R_SHIFT, R_SCALE, R_GATE, R_SHIFT_C, R_SCALE_C, R_RNG2, R_RPB = (
    0, 1, 2, 3, 4, 5, 6, 8, 10, 12, 13, 14, 16)
W_GF, W_NG, W_CCTX, W_RNG, W_DF, W_DB, W_BADA, W_RPB = 0, 1, 2, 3, 4, 5, 6, 9


SMALL = (("final_norm_g", W_GF, 1, D), ("norm_g", W_NG, 1, D), ("c_ctx", W_CCTX, 1, D),
         ("ret_norm_g", W_RNG, 1, 512), ("ret_decay_fwd", W_DF, 1, 4), ("ret_decay_bwd", W_DB, 1, 4),
         ("b_ada", W_BADA, 3, D), ("na_rpb", W_RPB, 4, D))
N_SMALL = len(SMALL)


def _small_final_call(sm_all, c_t, wada_f, wada, m_ada, v_ada, small_w, small_m, small_v, B):
    ws = wada.shape[1]
    NB = N_DEV * B

    def body(*refs):
        sm_ref, ct_ref, wf_ref, wa_ref, ma_ref, va_ref = refs[:6]
        ins = refs[6:6 + 3 * N_SMALL]
        outs = refs[6 + 3 * N_SMALL:6 + 7 * N_SMALL]
        ga_ref, da_ref, mao_ref, vao_ref, loss_ref, dmod_ref, pk_ref = refs[6 + 7 * N_SMALL:]
        x, y, _ = _mesh_pos()
        s = 2 * x + y
        tot = sm_ref[0]
        for dv in range(1, N_DEV):
            tot = tot + sm_ref[dv]
        pk_ref[...] = jnp.zeros_like(pk_ref)
        for kind in range(3):
            for i, (_, row, nrow, width) in enumerate(SMALL):
                ref = ins[kind * N_SMALL + i]
                if nrow == 3:
                    for part in range(3):
                        pk_ref[kind, row + part:row + part + 1, :] = ref[:, part * D:(part + 1) * D]
                else:
                    pk_ref[kind, row:row + nrow, 0:width] = ref[...]
        w = pk_ref[0]
        cctx_ref = ins[2]
        for dv in range(N_DEV):
            for b in range(B):
                r = dv * B + b
                for part, row in enumerate((R_SHIFT, R_SCALE, R_GATE)):
                    dmod_ref[r:r + 1, part * D:(part + 1) * D] = sm_ref[dv, row + b:row + b + 1, :]
        dmod_ref[NB:NB + 1, 0:D] = tot[R_SHIFT_C:R_SHIFT_C + 1, :]
        dmod_ref[NB:NB + 1, D:2 * D] = tot[R_SCALE_C:R_SCALE_C + 1, :]
        dmod_ref[NB:NB + 1, 2 * D:3 * D] = jnp.zeros((1, D), F32)
        dmod_ref[NB + 1:, :] = jnp.zeros((dmod_ref.shape[0] - NB - 1, 3 * D), F32)
        dmod = dmod_ref[...]
        cc = cctx_ref[...]
        scc = _sigmoid(cc)
        ct = ct_ref[...]
        act_t = ct * _sigmoid(ct)
        dmc = dmod[NB:NB + 1, :].astype(BF16)
        dact = jnp.zeros((1, D), F32)
        for sh in range(N_SHARD):
            dact = dact + _dot_nt(dmc[:, sh * ws:(sh + 1) * ws], wf_ref[sh])
        g = jnp.zeros((16, D), F32)
        rows = lax.broadcasted_iota(jnp.int32, (16, D), 0)

        def put(g, row, val):
            return jnp.where(rows == row, val, g)

        g = put(g, W_GF, tot[R_GF:R_GF + 1, :])
        g = put(g, W_NG, tot[R_NG:R_NG + 1, :])
        g = put(g, W_CCTX, dact * (scc * (1.0 + cc * (1.0 - scc))))
        g = put(g, W_RNG, tot[R_RNG:R_RNG + 1, :] + tot[R_RNG2:R_RNG2 + 1, :])
        g = put(g, W_DF, tot[R_LGF:R_LGF + 1, :] * (-jnp.exp(w[W_DF:W_DF + 1, :])))
        g = put(g, W_DB, tot[R_LGB:R_LGB + 1, :] * (-jnp.exp(w[W_DB:W_DB + 1, :])))
        db = jnp.sum(dmod, axis=0, keepdims=True)
        for part in range(3):
            g = put(g, W_BADA + part, db[:, part * D:(part + 1) * D])
        for part in range(4):
            g = put(g, W_RPB + part, tot[R_RPB + part:R_RPB + part + 1, :])
        for kind, val in enumerate((g,) + _adamw(w, g, pk_ref[1], pk_ref[2])):
            for i, (_, row, nrow, width) in enumerate(SMALL):
                out = outs[kind * N_SMALL + i]
                if nrow == 3:
                    for part in range(3):
                        out[:, part * D:(part + 1) * D] = val[row + part:row + part + 1, :]
                else:
                    out[...] = val[row:row + nrow, 0:width]
        loss_ref[...] = jnp.broadcast_to(
            (0.5 / D) * jnp.sum(tot[R_LOSS:R_LOSS + 1, :], axis=1, keepdims=True), (8, 128))
        for sh in range(N_SHARD):
            @pl.when(s == sh)
            def _():
                ga = jnp.dot(act_t, dmod[:, sh * ws:(sh + 1) * ws], precision=HIGHEST,
                             preferred_element_type=F32)
                ga_ref[...] = ga
                da_ref[...], mao_ref[...], vao_ref[...] = _adamw(wa_ref[...], ga, ma_ref[...], va_ref[...])

    sh_small = tuple(jax.ShapeDtypeStruct(a.shape, F32) for a in small_w)
    sh_ada = jax.ShapeDtypeStruct(wada.shape, F32)
    res = pl.pallas_call(
        body, name="small_final",
        out_shape=sh_small * 4 + (sh_ada,) * 4 + (jax.ShapeDtypeStruct((8, 128), F32),),
        scratch_shapes=[pltpu.VMEM((NB + 8, 3 * D), F32), pltpu.VMEM((3, 16, D), F32)],
        compiler_params=_params(vmem_mb=56))(
            sm_all, c_t, wada_f, wada, m_ada, v_ada, *small_w, *small_m, *small_v)
    smalls = [res[k * N_SMALL:(k + 1) * N_SMALL] for k in range(4)]
    return smalls, res[4 * N_SMALL:4 * N_SMALL + 4], res[4 * N_SMALL + 4]


def _local_step(order, x, c, ctx, c_ctx, norm_g, wada_f, b_ada, win_b, bias, dec_f, dec_b, ret_norm_g,
                wout_f, final_g, target):
    B, L, _ = x.shape
    LC = ctx.shape[1]
    assert B == 2
    cos2, sin2 = _rope_tables(L, LC)
    c8 = jnp.concatenate([c, c_ctx[None, :], jnp.zeros((8 - B - 1, D), F32)], axis=0)
    mod = _mod_call(c8, wada_f, b_ada)
    P, h, win_f = _inproj_gather_call(order, x, ctx, mod, norm_g, win_b, cos2, sin2)
    y_na, o_na = _na_fwd_call(P, bias, L, LC)
    sf, sb = _ret_states_call(P, dec_f, dec_b, L, LC)
    y_ret, o_ret = _retc_fwd_call(P, sf, sb, dec_f, dec_b, ret_norm_g, L)
    dY, dx2, dwout_p, sm_out = _out_call(y_na, y_ret, x, target, mod, final_g, wout_f.reshape(D, D))
    dnq, dng, dnk, dnv, dbias = _na_bwd_call(P, bias, dY, o_na, L, LC)
    drq, drg, drk, drv, dgn, dlg = _retc_bwd_call(P, sf, sb, dec_f, dec_b, ret_norm_g, o_ret, dY, cos2, sin2, L, LC)
    dsec = (dnq, dnk, dnv, dng, drq, drk, drv, drg)
    dwin_b = _dw_call(dsec, h, L)
    cp_in, cp_out, drpb, dlg_sum = _grad_halves_call(
        dwin_b, dwout_p.astype(BF16).reshape(N_SHARD, D // N_SHARD, D), dbias, dlg)
    grad_x, sm_dh, sl_in, sl_out = _dh_call(dsec, win_f, x, ctx, dx2, mod, norm_g, cp_in, cp_out)
    z = jnp.zeros((1, D), F32)
    pad = lambda v: jnp.pad(v.reshape(1, -1), ((0, 0), (0, D - v.size)))
    dlg_sum = dlg_sum.reshape(4, 8, 128)
    rpb_rows = jnp.pad(drpb[:, :15, :31].reshape(-1), (0, 4 * D - drpb.shape[0] * 465)).reshape(4, D)
    small = jnp.concatenate([
        sm_out[0:1], sm_dh[0:1], sm_out[1:2], pad(dgn[0]), pad(dlg_sum[:, 0, 0]), pad(dlg_sum[:, 1, 0]),
        sm_dh[3:5], sm_dh[5:7], sm_out[2:4], sm_dh[1:2], sm_dh[2:3], pad(dgn[1]), z, rpb_rows,
        jnp.zeros((SM_ROWS - 20, D), F32)], axis=0)
    return grad_x, sl_in, sl_out, small


def kernel(x, c, ctx, c_ctx, norm_g, w_ada, b_ada, w_in, na_rpb, ret_decay_fwd, ret_decay_bwd, ret_norm_g, w_out, final_norm_g, loss_target, m_c_ctx, m_norm_g, m_w_ada, m_b_ada, m_w_in, m_na_rpb, m_ret_decay_fwd, m_ret_decay_bwd, m_ret_norm_g, m_w_out, m_final_norm_g, v_c_ctx, v_norm_g, v_w_ada, v_b_ada, v_w_in, v_na_rpb, v_ret_decay_fwd, v_ret_decay_bwd, v_ret_norm_g, v_w_out, v_final_norm_g):
    B = x.shape[0]
    wout_f, wada_f, c_all, bias = _gather_call(
        w_out[0].astype(BF16), w_ada[0].astype(BF16), c, na_rpb[0].reshape(na_rpb.shape[1], -1))
    mx, my = lax.axis_index("x"), lax.axis_index("y")
    order = jnp.stack([2 * mx + my, 2 * (1 - mx) + my, 2 * mx + (1 - my),
                       2 * (1 - mx) + (1 - my)]).astype(jnp.int32)
    grad_x, sl_in, sl_out, small = _local_step(
        order, x, c, ctx, c_ctx, norm_g, wada_f, b_ada, w_in[0].astype(BF16), bias, ret_decay_fwd,
        ret_decay_bwd, ret_norm_g, wout_f, final_norm_g.reshape(1, D), loss_target)
    gin, gout, sm_all = _grad_finish_call(sl_in, sl_out, small)
    g_win, d_win, nm_win, nv_win = _adam_call(
        w_in[0], m_w_in[0], v_w_in[0], gin.reshape(w_in.shape[1:]), "adam_w_in")
    g_wout, d_wout, nm_wout, nv_wout = _adam_call(
        w_out[0], m_w_out[0], v_w_out[0], gout.reshape(w_out.shape[1:]), "adam_w_out")

    def small_inputs(gf, ng, cc, rng, df, db, bada, rpb):
        return (gf.reshape(1, D), ng, cc.reshape(1, D), rng, df, db, bada,
                jnp.pad(rpb.reshape(-1), (0, 4 * D - rpb.size)).reshape(4, D))

    c_t = jnp.concatenate([c_all.reshape(N_DEV * B, D), c_ctx.reshape(1, D), jnp.zeros((7, D), F32)], axis=0).T
    smalls, adas, loss = _small_final_call(
        sm_all, c_t, wada_f, w_ada[0], m_w_ada[0], v_w_ada[0],
        small_inputs(final_norm_g, norm_g, c_ctx, ret_norm_g, ret_decay_fwd, ret_decay_bwd, b_ada, na_rpb),
        small_inputs(m_final_norm_g, m_norm_g, m_c_ctx, m_ret_norm_g, m_ret_decay_fwd, m_ret_decay_bwd, m_b_ada,
                     m_na_rpb),
        small_inputs(v_final_norm_g, v_norm_g, v_c_ctx, v_ret_norm_g, v_ret_decay_fwd, v_ret_decay_bwd, v_b_ada,
                     v_na_rpb), B)
    res = []
    for p, ada, win_o, wout_o in zip(smalls, adas, (g_win, d_win, nm_win, nv_win),
                                     (g_wout, d_wout, nm_wout, nv_wout)):
        gf, ng, cc, rng, df, db, bada, rpb = p
        res.append([cc.reshape(D), ng, ada[None], bada, win_o[None],
                    rpb.reshape(-1)[:na_rpb.size].reshape(na_rpb.shape), df, db, rng, wout_o[None], gf.reshape(D)])
    return (loss[0, 0], grad_x, *res[0], *res[1], *res[2], *res[3])
```

```python
import numpy as np
import jax
import jax.numpy as jnp
from jax import lax
from jax.experimental import pallas as pl
from jax.experimental.pallas import tpu as pltpu

F32 = jnp.float32
BF16 = jnp.bfloat16
HIGHEST = lax.Precision.HIGHEST

D = 1024
GRID_W = 64
NA_DH = 64
RET_DK = 128
ROPE_BASE = 10000.0
EPS = 1e-6
NEG = -1e30
TQ = 256
KW = 12 * GRID_W
N_SHARD = 4
N_DEV = 8
SM_ROWS = 24

ADAM_LR = 0.001
ADAM_B1 = 0.9
ADAM_B2 = 0.999
ADAM_EPS = 1e-08
ADAM_WD = 0.01
ADAM_STEP = 10

MESH = pl.DeviceIdType.MESH
ANY = pl.BlockSpec(memory_space=pl.ANY)


def _params(sem=None, vmem_mb=48):
    return pltpu.CompilerParams(dimension_semantics=sem, vmem_limit_bytes=vmem_mb << 20)


def _dot(a, b):
    return jnp.dot(a, b, preferred_element_type=F32)


def _dot_nt(a, b):
    return lax.dot_general(a, b, (((1,), (1,)), ((), ())), preferred_element_type=F32)


def _dot_tn(a, b):
    return lax.dot_general(a, b, (((0,), (0,)), ((), ())), preferred_element_type=F32)


def _sigmoid(x):
    return 1.0 / (1.0 + jnp.exp(-x))


def _rope_tables(L, LC):
    half = RET_DK // 2
    nf = half // 2
    t = np.arange(L)
    row = (t // GRID_W).astype(np.float32)
    col = (t % GRID_W).astype(np.float32)
    inv = (np.float32(ROPE_BASE) ** (-np.arange(nf, dtype=np.float32) / np.float32(nf))).astype(np.float32)
    ang = np.concatenate([row[:, None] * inv, col[:, None] * inv], axis=-1).astype(np.float32)
    cos, sin = np.cos(ang).astype(np.float32), np.sin(ang).astype(np.float32)
    cos2 = np.concatenate([cos, cos], axis=-1)
    sin2 = np.concatenate([-sin, sin], axis=-1)
    cos2 = np.concatenate([cos2, np.ones((LC, RET_DK), np.float32)], axis=0)
    sin2 = np.concatenate([sin2, np.zeros((LC, RET_DK), np.float32)], axis=0)
    return jnp.asarray(cos2), jnp.asarray(sin2)


def _mod_call(c8, wada_f, b_ada):
    ws = wada_f.shape[2]

    def body(c_ref, w_ref, b_ref, o_ref):
        a = c_ref[...]
        a = (a * _sigmoid(a)).astype(BF16)
        for s in range(N_SHARD):
            o_ref[:, s * ws:(s + 1) * ws] = _dot(a, w_ref[s]) + b_ref[:, s * ws:(s + 1) * ws]

    return pl.pallas_call(
        body, name="ada_mod", out_shape=jax.ShapeDtypeStruct((8, 3 * D), F32),
        compiler_params=_params())(c8, wada_f, b_ada)


def _dc_masks():
    cq = lax.broadcasted_iota(jnp.int32, (GRID_W, GRID_W), 0)
    ck = lax.broadcasted_iota(jnp.int32, (GRID_W, GRID_W), 1)
    dc = jnp.clip(ck - cq + 15, 0, 30)
    c0 = jnp.clip(cq - 8, 0, GRID_W - 16)
    col_ok = (ck >= c0) & (ck < c0 + 16)
    return dc, col_ok


def _bias_blocks():
    out = []
    for typ, delta in enumerate((4, 0, -4)):
        for rq in range(4):
            for rkk in range(12):
                dr = rkk + delta - rq - 4
                if typ == 0:
                    ok = -rq <= dr <= 7 - rq
                elif typ == 1:
                    ok = -4 <= dr <= 3
                else:
                    ok = -4 - rq <= dr <= 3 - rq
                out.append((typ, rq, rkk, dr if ok else None))
    return out


def _bias_body(r_ref, bias_ref, et_ref):
    dc, col_ok = _dc_masks()
    masks = [(dc == j).astype(F32) for j in range(31)]

    def per_h(h, carry):
        for dr in range(15):
            t = jnp.zeros((GRID_W, GRID_W), F32)
            for j in range(31):
                t = t + masks[j] * r_ref[h, dr * 31 + j]
            et_ref[dr] = jnp.where(col_ok, t, NEG)
        neg = jnp.full((GRID_W, GRID_W), NEG, F32)
        for typ, rq, rkk, dr in _bias_blocks():
            blk = neg if dr is None else et_ref[dr + 7]
            bias_ref[h, typ, rq * 64:(rq + 1) * 64, rkk * 64:(rkk + 1) * 64] = blk
        return carry

    lax.fori_loop(0, bias_ref.shape[0], per_h, 0)


def _bias_tile_sums(db_ref, hh):
    acc = {}
    for typ, rq, rkk, dr in _bias_blocks():
        if dr is None:
            continue
        blk = db_ref[hh, typ, rq * 64:(rq + 1) * 64, rkk * 64:(rkk + 1) * 64]
        acc[dr] = blk if dr not in acc else acc[dr] + blk
    return acc


def _small_reduce_body(dt_ref, dlg_ref, drpb_ref, dlgo_ref, p_ref):
    dc, _ = _dc_masks()
    masks = [(dc == j).astype(F32) for j in range(31)]
    ones = jnp.ones((8, GRID_W), F32)
    p_ref[...] = jnp.zeros_like(p_ref)
    drpb_ref[...] = jnp.zeros_like(drpb_ref)

    def per_h(h, carry):
        for dr in range(-7, 8):
            t = dt_ref[h, dr + 7]
            for j in range(31):
                p_ref[j:j + 1, :] = jnp.sum(t * masks[j], axis=0, keepdims=True)
            red = lax.dot_general(ones, p_ref[...], (((1,), (1,)), ((), ())),
                                  precision=HIGHEST, preferred_element_type=F32)
            drpb_ref[h, dr + 7:dr + 8, :] = red[0:1, :]
        return carry

    lax.fori_loop(0, dt_ref.shape[0], per_h, 0)
    x = dlg_ref[0]
    for b in range(1, dlg_ref.shape[0]):
        x = x + dlg_ref[b]
    x = x.reshape(4 * 8, x.shape[-1])
    dlgo_ref[...] = jnp.dot(x, jnp.ones((x.shape[-1], 128), F32), precision=HIGHEST,
                            preferred_element_type=F32)


def _inproj_gather_call(order, x, ctx, mod, norm_g, win_b, cos2, sin2):
    B, L, _ = x.shape
    LC = ctx.shape[1]
    T = L + LC
    TI = 2 * TQ
    nl = L // TI
    nt = nl + 1
    assert LC == TQ and L % TI == 0
    kscale = RET_DK ** -0.5
    HR = D // 2
    pad_rows = nt * TI - T
    cos2 = jnp.pad(cos2, ((0, pad_rows), (0, 0)))
    sin2 = jnp.pad(sin2, ((0, pad_rows), (0, 0)))

    def body(ord_ref, x_ref, ctx_ref, mod_ref, g_ref, wown_ref, cos_ref, sin_ref, p_ref, h_ref, wf_ref,
             w_all, hs_ref, ssem, rsem, lsem):
        j, b, t = pl.program_id(0), pl.program_id(1), pl.program_id(2)
        first = (b == 0) & (t == 0)
        mx, my, mc = _mesh_pos()
        s = 2 * mx + my
        sib = (mx, my, 1 - mc)
        own = pltpu.make_async_copy(wown_ref, w_all.at[s], lsem.at[0])
        ici_send, ici_recv, fwd_send, fwd_recv, outs = [], [], [], [], [
            pltpu.make_async_copy(w_all.at[s], wf_ref.at[s], lsem.at[1])]
        for k, (px, py) in enumerate(_other_chips(mx, my)):
            ps = 2 * px + py
            mine = w_all.at[s, pl.ds(mc * HR, HR)]
            ici_send.append(_remote(mine, w_all.at[s, pl.ds(mc * HR, HR)], ssem, rsem, k, (px, py, mc)))
            ici_recv.append(_remote(mine, w_all.at[ps, pl.ds(mc * HR, HR)], ssem, rsem, k, (px, py, mc)))
            got = w_all.at[ps, pl.ds(mc * HR, HR)]
            fwd_send.append(_remote(got, got, ssem, rsem, 3 + k, sib))
            theirs = w_all.at[ps, pl.ds((1 - mc) * HR, HR)]
            fwd_recv.append(_remote(theirs, theirs, ssem, rsem, 3 + k, sib))
            outs.append(pltpu.make_async_copy(w_all.at[ps], wf_ref.at[ps], lsem.at[2 + k]))

        @pl.when(first & (j == 0))
        def _():
            own.start()
            own.wait()
            ici_send[0].start()
            ici_send[1].start()
            outs[0].start()

        for k in range(3):
            @pl.when(first & (j == k + 1))
            def _(k=k):
                ici_recv[k].wait_recv()
                if k == 0:
                    ici_send[2].start()
                fwd_send[k].start()
                fwd_recv[k].wait_recv()
                outs[1 + k].start()

        tile = b * nt + t

        @pl.when(j == 0)
        def _():
            is_lat = t < nl
            ctx_tile = jnp.concatenate([ctx_ref[...], jnp.zeros((TI - LC, D), F32)], axis=0)
            xt = jnp.where(is_lat, x_ref[...], ctx_tile)
            mrow = mod_ref[pl.ds(jnp.where(is_lat, b, B), 1), :]
            shift, scale = mrow[:, 0:D], mrow[:, D:2 * D]
            rstd = lax.rsqrt(jnp.mean(xt * xt, axis=-1, keepdims=True) + EPS)
            h0 = ((xt * rstd * g_ref[...]) * (1.0 + scale) + shift).astype(BF16)
            h_ref[...] = h0
            hs_ref[tile] = h0

        hb = hs_ref[tile]
        cs, sn = cos_ref[...], sin_ref[...]
        shard = ord_ref[j]
        for sh in range(N_SHARD):
            @pl.when(shard == sh)
            def _(sh=sh):
                for half in range(2):
                    sec = 2 * sh + half
                    acc = _dot(hb, w_all[sh, :, half * 512:(half + 1) * 512])
                    if sec == 0:
                        acc = acc * (NA_DH ** -0.5)
                    if sec in (4, 5):
                        for q in range(4):
                            a = acc[:, q * 128:(q + 1) * 128]
                            r = a * cs + pltpu.roll(a, 64, 1) * sn
                            if sec == 5:
                                r = r * kscale
                            p_ref[:, half * 512 + q * 128:half * 512 + (q + 1) * 128] = r.astype(BF16)
                    else:
                        p_ref[:, half * 512:(half + 1) * 512] = acc.astype(BF16)

        @pl.when((j == N_SHARD - 1) & (b == B - 1) & (t == nt - 1))
        def _():
            _finish(outs, ici_send + fwd_send, [])

    tok = lambda j, b, t, o: (jnp.where(j == 0, b, B - 1), jnp.where(j == 0, jnp.minimum(t, nl - 1), nl - 1), 0)
    grid_spec = pltpu.PrefetchScalarGridSpec(
        num_scalar_prefetch=1, grid=(N_SHARD, B, nt),
        in_specs=[
            pl.BlockSpec((None, TI, D), tok),
            pl.BlockSpec((None, LC, D), lambda j, b, t, o: (jnp.where(j == 0, b, B - 1), 0, 0)),
            pl.BlockSpec((8, 3 * D), lambda j, b, t, o: (0, 0)),
            pl.BlockSpec((1, D), lambda j, b, t, o: (0, 0)),
            ANY,
            pl.BlockSpec((TI, RET_DK), lambda j, b, t, o: (t, 0)),
            pl.BlockSpec((TI, RET_DK), lambda j, b, t, o: (t, 0)),
        ],
        out_specs=(pl.BlockSpec((None, TI, D), lambda j, b, t, o: (b, t, o[j])),
                   pl.BlockSpec((None, TI, D), lambda j, b, t, o: (
                       jnp.where(j == 0, b, B - 1), jnp.where(j == 0, t, nt - 1), 0)), ANY),
        scratch_shapes=[pltpu.VMEM((N_SHARD, D, D), BF16), pltpu.VMEM((B * nt, TI, D), BF16),
                        pltpu.SemaphoreType.DMA((6,)), pltpu.SemaphoreType.DMA((6,)),
                        pltpu.SemaphoreType.DMA((5,))])
    return pl.pallas_call(
        body, name="in_proj", grid_spec=grid_spec,
        out_shape=(jax.ShapeDtypeStruct((B, T, 4 * D), BF16), jax.ShapeDtypeStruct((B, T, D), BF16),
                   jax.ShapeDtypeStruct((N_SHARD, D, D), BF16)),
        compiler_params=_params(("arbitrary",) * 3, vmem_mb=56))(order, x, ctx, mod, norm_g, win_b, cos2, sin2)


def _na_specs(L, T, rows, nh=2):
    nm = rows // 4
    w = nh * NA_DH
    per = 512 // w
    q_spec = pl.BlockSpec((None, TQ, w), lambda hp, b, m: (b, m, hp))
    k_spec = pl.BlockSpec((None, T, w), lambda hp, b, m: (b, 0, per + hp))
    v_spec = pl.BlockSpec((None, T, w), lambda hp, b, m: (b, 0, 2 * per + hp))
    g_spec = pl.BlockSpec((None, TQ, w), lambda hp, b, m: (b, m, 3 * per + hp))
    bias_spec = pl.BlockSpec((nh, 3, TQ, KW), lambda hp, b, m: (hp, 0, 0, 0))
    return nm, q_spec, k_spec, v_spec, g_spec, bias_spec


def _na_tile(m, nm, rows):
    typ = jnp.where(m == 0, 0, jnp.where(m == nm - 1, 2, 1))
    start = pl.multiple_of(jnp.clip(4 * m - 4, 0, rows - 12) * GRID_W, TQ)
    return typ, start


def _na_fwd_call(P, bias, L, LC):
    B, T, _ = P.shape
    rows = L // GRID_W
    NH = 4
    nm, q_spec, k_spec, v_spec, g_spec, bias_spec = _na_specs(L, T, rows, NH)

    def body(q_ref, k_ref, v_ref, g_ref, bias_ref, y_ref, o_ref):
        typ, start = _na_tile(pl.program_id(2), nm, rows)
        for hh in range(NH):
            ln = slice(hh * NA_DH, (hh + 1) * NA_DH)
            q = q_ref[:, ln]
            kw, vw = k_ref[pl.ds(start, KW), ln], v_ref[pl.ds(start, KW), ln]
            kc, vc = k_ref[L:L + LC, ln], v_ref[L:L + LC, ln]
            s1 = _dot_nt(q, kw) + bias_ref[hh, typ]
            s2 = _dot_nt(q, kc)
            mx = jnp.maximum(jnp.max(s1, axis=-1, keepdims=True), jnp.max(s2, axis=-1, keepdims=True))
            p1, p2 = jnp.exp(s1 - mx), jnp.exp(s2 - mx)
            inv = 1.0 / (jnp.sum(p1, axis=-1, keepdims=True) + jnp.sum(p2, axis=-1, keepdims=True))
            o = (_dot(p1.astype(BF16), vw) + _dot(p2.astype(BF16), vc)) * inv
            g = g_ref[:, ln].astype(F32)
            o_ref[:, ln] = o.astype(BF16)
            y_ref[:, ln] = (o * (g * _sigmoid(g))).astype(BF16)

    tile = pl.BlockSpec((None, TQ, NH * NA_DH), lambda hp, b, m: (b, m, hp))
    return pl.pallas_call(
        body, name="na_fwd", grid=(8 // NH, B, nm),
        in_specs=[q_spec, k_spec, v_spec, g_spec, bias_spec],
        out_specs=(tile, tile),
        out_shape=(jax.ShapeDtypeStruct((B, L, 512), BF16),) * 2,
        compiler_params=_params(("arbitrary",) * 3))(P, P, P, P, bias)


def _na_bwd_call(P, bias, dY, o_na, L, LC):
    B, T, _ = P.shape
    rows = L // GRID_W
    NH = 4
    W = NH * NA_DH
    nm, q_spec, k_spec, v_spec, g_spec, bias_spec = _na_specs(L, T, rows, NH)
    scale = NA_DH ** -0.5

    RB = 32

    def body(q_ref, k_ref, v_ref, g_ref, bias_ref, dy_ref, o_ref, dq_ref, dg_ref, dk_ref, dv_ref, dt_ref,
             db_ref, s1_ref, s2_ref, dp1_ref, dp2_ref, p1_ref, p2_ref, ds1_ref, ds2_ref, dkt_ref, dvt_ref):
        b, m = pl.program_id(1), pl.program_id(2)
        typ, start = _na_tile(m, nm, rows)

        @pl.when(m == 0)
        def _():
            dkt_ref[...] = jnp.zeros_like(dkt_ref)
            dvt_ref[...] = jnp.zeros_like(dvt_ref)

        @pl.when((m == 0) & (b == 0))
        def _():
            db_ref[...] = jnp.zeros_like(db_ref)

        for hh in range(NH):
            ln = slice(hh * NA_DH, (hh + 1) * NA_DH)
            q = q_ref[:, ln]
            kw, vw = k_ref[pl.ds(start, KW), ln], v_ref[pl.ds(start, KW), ln]
            kc, vc = k_ref[L:L + LC, ln], v_ref[L:L + LC, ln]
            g = g_ref[:, ln].astype(F32)
            sg = _sigmoid(g)
            dy = dy_ref[:, ln].astype(F32)
            do = (dy * (g * sg)).astype(BF16)
            s1_ref[hh] = _dot_nt(q, kw)
            s2_ref[hh] = _dot_nt(q, kc)
            dp1_ref[hh] = _dot_nt(do, vw)
            dp2_ref[hh] = _dot_nt(do, vc)

            def rows_pass(r, carry, hh=hh):
                rw = pl.ds(pl.multiple_of(r * RB, RB), RB)
                a = s1_ref[hh, rw, :] + bias_ref[hh, typ, rw, :]
                c = s2_ref[hh, rw, :]
                mx = jnp.maximum(jnp.max(a, axis=-1, keepdims=True), jnp.max(c, axis=-1, keepdims=True))
                e1, e2 = jnp.exp(a - mx), jnp.exp(c - mx)
                inv = 1.0 / (jnp.sum(e1, axis=-1, keepdims=True) + jnp.sum(e2, axis=-1, keepdims=True))
                p1, p2 = e1 * inv, e2 * inv
                p1_ref[hh, rw, :] = p1.astype(BF16)
                p2_ref[hh, rw, :] = p2.astype(BF16)
                dp1, dp2 = dp1_ref[hh, rw, :], dp2_ref[hh, rw, :]
                delta = jnp.sum(p1 * dp1, axis=-1, keepdims=True) + jnp.sum(p2 * dp2, axis=-1, keepdims=True)
                ds1 = p1 * (dp1 - delta)
                db_ref[hh, typ, rw, :] += ds1
                ds1_ref[hh, rw, :] = ds1.astype(BF16)
                ds2_ref[hh, rw, :] = (p2 * (dp2 - delta)).astype(BF16)
                return carry

            lax.fori_loop(0, TQ // RB, rows_pass, 0, unroll=True)
            p1b, p2b, ds1b, ds2b = p1_ref[hh], p2_ref[hh], ds1_ref[hh], ds2_ref[hh]
            dg_ref[:, ln] = (dy * o_ref[:, ln].astype(F32) * (sg * (1.0 + g * (1.0 - sg)))).astype(BF16)
            dq_ref[:, ln] = ((_dot(ds1b, kw) + _dot(ds2b, kc)) * scale).astype(BF16)
            dkt_ref[ln, pl.ds(start, KW)] += _dot_tn(q, ds1b)
            dvt_ref[ln, pl.ds(start, KW)] += _dot_tn(do, p1b)
            dkt_ref[ln, L:L + LC] += _dot_tn(q, ds2b)
            dvt_ref[ln, L:L + LC] += _dot_tn(do, p2b)

        @pl.when(m == nm - 1)
        def _():
            dk_ref[...] = dkt_ref[...].T
            dv_ref[...] = dvt_ref[...].T

        @pl.when((m == nm - 1) & (b == B - 1))
        def _():
            for hh in range(NH):
                for dr, t in _bias_tile_sums(db_ref, hh).items():
                    dt_ref[hh, dr + 7] = t

    tile = pl.BlockSpec((None, TQ, W), lambda hp, b, m: (b, m, hp))
    kv_out = pl.BlockSpec((None, T, W), lambda hp, b, m: (b, 0, hp))
    wide, narrow = (NH, TQ, KW), (NH, TQ, LC)
    return pl.pallas_call(
        body, name="na_bwd", grid=(8 // NH, B, nm),
        in_specs=[q_spec, k_spec, v_spec, g_spec, bias_spec, tile, tile],
        out_specs=(tile, tile, kv_out, kv_out,
                   pl.BlockSpec((NH, 15, GRID_W, GRID_W), lambda hp, b, m: (hp, 0, 0, 0))),
        out_shape=(jax.ShapeDtypeStruct((B, L, 512), BF16), jax.ShapeDtypeStruct((B, L, 512), BF16),
                   jax.ShapeDtypeStruct((B, T, 512), F32), jax.ShapeDtypeStruct((B, T, 512), F32),
                   jax.ShapeDtypeStruct((bias.shape[0], 15, GRID_W, GRID_W), F32)),
        scratch_shapes=[pltpu.VMEM((NH,) + bias.shape[1:], F32),
                        pltpu.VMEM(wide, F32), pltpu.VMEM(narrow, F32), pltpu.VMEM(wide, F32), pltpu.VMEM(narrow, F32),
                        pltpu.VMEM(wide, BF16), pltpu.VMEM(narrow, BF16), pltpu.VMEM(wide, BF16),
                        pltpu.VMEM(narrow, BF16), pltpu.VMEM((W, T), F32), pltpu.VMEM((W, T), F32)],
        compiler_params=_params(("arbitrary",) * 3, vmem_mb=60))(P, P, P, P, bias, dY, o_na)


def _head_scalar(dec_ref, h):
    lane = lax.broadcasted_iota(jnp.int32, dec_ref.shape, 1)
    return -jnp.sum(jnp.where(lane == h, jnp.exp(dec_ref[...]), 0.0), axis=1, keepdims=True)


def _chunk_decay(lgf, lgb):
    tau = lax.broadcasted_iota(jnp.int32, (TQ, 1), 0).astype(F32)
    sig = lax.broadcasted_iota(jnp.int32, (1, TQ), 1).astype(F32)
    dist = tau - sig
    dm = jnp.exp(dist * jnp.where(dist > 0, lgf, -lgb)) * jnp.where(dist == 0, 2.0, 1.0)
    return tau, dist, dm


def _ret_states_call(P, dec_f, dec_b, L, LC):
    B, T, _ = P.shape
    n = L // TQ

    def body(df_ref, db_ref, k_ref, v_ref, sf_ref, sb_ref):
        h = pl.program_id(1)
        lgf, lgb = _head_scalar(df_ref, h), _head_scalar(db_ref, h)
        tau = lax.broadcasted_iota(jnp.int32, (TQ, 1), 0).astype(F32)
        jc = lax.broadcasted_iota(jnp.int32, (LC, 1), 0).astype(F32)
        wf, wb = jnp.exp(lgf * (TQ - 1.0 - tau)), jnp.exp(lgb * tau)
        gcf, gcb = jnp.exp(lgf * float(TQ)), jnp.exp(lgb * float(TQ))
        kc, vc = k_ref[L:L + LC, :].astype(F32), v_ref[L:L + LC, :]

        def chunk_state(i, w):
            ks = pl.multiple_of(i * TQ, TQ)
            return _dot_tn((k_ref[pl.ds(ks, TQ), :].astype(F32) * w).astype(BF16), v_ref[pl.ds(ks, TQ), :])

        def fwd(i, s):
            sf_ref[i] = s
            return gcf * s + chunk_state(i, wf)

        lax.fori_loop(0, n, fwd, _dot_tn((kc * jnp.exp(lgf * (LC - 1.0 - jc))).astype(BF16), vc), unroll=True)

        def bwd(r, s):
            i = n - 1 - r
            sb_ref[i] = s
            return gcb * s + chunk_state(i, wb)

        lax.fori_loop(0, n, bwd, _dot_tn((kc * jnp.exp(lgb * jc)).astype(BF16), vc), unroll=True)

    st = pl.BlockSpec((None, None, n, RET_DK, RET_DK), lambda b, h: (b, h, 0, 0, 0))
    return pl.pallas_call(
        body, name="ret_states", grid=(B, 4),
        in_specs=[pl.BlockSpec((1, 4), lambda b, h: (0, 0)), pl.BlockSpec((1, 4), lambda b, h: (0, 0)),
                  pl.BlockSpec((None, T, 128), lambda b, h: (b, 0, 20 + h)),
                  pl.BlockSpec((None, T, 128), lambda b, h: (b, 0, 24 + h))],
        out_specs=(st, st),
        out_shape=(jax.ShapeDtypeStruct((B, 4, n, RET_DK, RET_DK), F32),) * 2,
        compiler_params=_params(("arbitrary",) * 2))(dec_f, dec_b, P, P)


def _retc_fwd_call(P, sf, sb, dec_f, dec_b, ret_norm_g, L):
    B, T, _ = P.shape
    sec = lambda k: pl.BlockSpec((None, TQ, 512), lambda b, i: (b, i, k))
    dec_spec = pl.BlockSpec((1, 4), lambda b, i: (0, 0))
    st_spec = pl.BlockSpec((None, 4, None, RET_DK, RET_DK), lambda b, i: (b, 0, i, 0, 0))

    def body(df_ref, db_ref, q_ref, k_ref, v_ref, g_ref, gn_ref, sf_ref, sb_ref, y_ref, o_ref):
        for h in range(4):
            ln = slice(h * RET_DK, (h + 1) * RET_DK)
            lgf, lgb = _head_scalar(df_ref, h), _head_scalar(db_ref, h)
            tau, _, dm = _chunk_decay(lgf, lgb)
            q = q_ref[:, ln]
            qf = q.astype(F32)
            acc = _dot((_dot_nt(q, k_ref[:, ln]) * dm).astype(BF16), v_ref[:, ln])
            acc = acc + _dot((qf * jnp.exp(lgf * (tau + 1.0))).astype(BF16), sf_ref[h].astype(BF16))
            acc = acc + _dot((qf * jnp.exp(lgb * (TQ - tau))).astype(BF16), sb_ref[h].astype(BF16))
            o_ref[:, ln] = acc
            rn = lax.rsqrt(jnp.mean(acc * acc, axis=-1, keepdims=True) + EPS)
            g = g_ref[:, ln].astype(F32)
            y_ref[:, ln] = ((acc * rn * gn_ref[:, ln]) * (g * _sigmoid(g))).astype(BF16)

    tile = pl.BlockSpec((None, TQ, 512), lambda b, i: (b, i, 0))
    return pl.pallas_call(
        body, name="ret_fwd", grid=(B, L // TQ),
        in_specs=[dec_spec, dec_spec, sec(4), sec(5), sec(6), sec(7),
                  pl.BlockSpec((1, 512), lambda b, i: (0, 0)), st_spec, st_spec],
        out_specs=(tile, tile),
        out_shape=(jax.ShapeDtypeStruct((B, L, 512), BF16), jax.ShapeDtypeStruct((B, L, 512), F32)),
        compiler_params=_params(("arbitrary",) * 2))(dec_f, dec_b, P, P, P, P, ret_norm_g, sf, sb)


def _retc_bwd_call(P, sf, sb, dec_f, dec_b, ret_norm_g, o_ret, dY, cos2, sin2, L, LC):
    B, T, _ = P.shape
    n = L // TQ
    C = float(TQ)
    kscale = RET_DK ** -0.5
    st_spec = pl.BlockSpec((None, 4, n, RET_DK, RET_DK), lambda b, i: (b, 0, 0, 0, 0))

    def body(df_ref, db_ref, q_ref, k_ref, v_ref, g_ref, gn_ref, o_ref, dy_ref, cos_ref, sin_ref, sf_ref, sb_ref,
             dq_ref, dg_ref, dk_ref, dv_ref, dgn_ref, dlg_ref, dsf_ref, dsb_ref):
        i = pl.program_id(1)

        @pl.when(i == 0)
        def _():
            dk_ref[...] = jnp.zeros_like(dk_ref)
            dv_ref[...] = jnp.zeros_like(dv_ref)
            dgn_ref[...] = jnp.zeros_like(dgn_ref)
            dlg_ref[...] = jnp.zeros_like(dlg_ref)

        rows = pl.ds(pl.multiple_of(i * TQ, TQ), TQ)
        cs, sn = cos_ref[rows, :], sin_ref[rows, :]

        def one_head(h):
            ln = slice(h * RET_DK, (h + 1) * RET_DK)
            lgf, lgb = _head_scalar(df_ref, h), _head_scalar(db_ref, h)
            tau, dist, dm = _chunk_decay(lgf, lgb)

            def add_lg(row, x):
                csum = jnp.sum(x, axis=0, keepdims=True)
                tot = csum[:, 0:128]
                for part in range(1, x.shape[1] // 128):
                    tot = tot + csum[:, part * 128:(part + 1) * 128]
                dlg_ref[h, row:row + 1, :] += tot

            q = q_ref[:, ln]
            qf = q.astype(F32)
            o = o_ref[:, ln]
            g = g_ref[:, ln].astype(F32)
            dy = dy_ref[:, ln].astype(F32)
            gn = gn_ref[:, ln]
            sg = _sigmoid(g)
            rn = lax.rsqrt(jnp.mean(o * o, axis=-1, keepdims=True) + EPS)
            nrm = o * rn
            dg_ref[:, ln] = (dy * (nrm * gn) * (sg * (1.0 + g * (1.0 - sg)))).astype(BF16)
            dhn = dy * (g * sg)
            dgn_ref[:, ln] += jnp.sum(dhn * nrm, axis=0, keepdims=True)
            dnrm = dhn * gn
            do = rn * (dnrm - nrm * jnp.mean(dnrm * nrm, axis=-1, keepdims=True))
            dob = do.astype(BF16)
            ki, vi = k_ref[rows, ln], v_ref[rows, ln]
            s = _dot_nt(q, ki)
            dsv = _dot_nt(dob, vi)
            dsb = (dsv * dm).astype(BF16)
            dk_ref[rows, ln] += _dot_tn(dsb, q)
            dv_ref[rows, ln] += _dot_tn((s * dm).astype(BF16), dob)
            xw = s * dsv * dm * jnp.abs(dist)
            fpart = jnp.where(dist > 0, xw, 0.0)
            add_lg(0, fpart)
            add_lg(1, xw - fpart)
            dq = _dot(dsb, ki)
            af, ab = jnp.exp(lgf * (tau + 1.0)), jnp.exp(lgb * (C - tau))
            qa, qb = (qf * af).astype(BF16), (qf * ab).astype(BF16)
            sfi, sbi = sf_ref[h, i].astype(BF16), sb_ref[h, i].astype(BF16)
            dq = dq + af * _dot_nt(dob, sfi) + ab * _dot_nt(dob, sbi)
            dsf_ref[h, i] = _dot_tn(qa, dob)
            dsb_ref[h, i] = _dot_tn(qb, dob)
            add_lg(0, (tau + 1.0) * (_dot(qa, sfi) * do))
            add_lg(1, (C - tau) * (_dot(qb, sbi) * do))
            dq_ref[:, ln] = (dq * cs - pltpu.roll(dq, 64, 1) * sn).astype(BF16)

            @pl.when(i == n - 1)
            def _():
                jc = lax.broadcasted_iota(jnp.int32, (LC, 1), 0).astype(F32)
                crow = pl.ds(L, LC)

                def through_state(rws, w, dw, gst, row):
                    kk, vv = k_ref[rws, ln].astype(F32), v_ref[rws, ln]
                    gb = gst.astype(BF16)
                    vg = _dot_nt(vv, gb)
                    kw = kk * w
                    dk_ref[rws, ln] += w * vg
                    dv_ref[rws, ln] += _dot(kw.astype(BF16), gb)
                    add_lg(row, dw * (kw * vg))

                def scan(gc, w, dw, st_ref, dst_ref, order, row):
                    def step(r, gst):
                        j = order(r)
                        through_state(pl.ds(pl.multiple_of(j * TQ, TQ), TQ), w, dw, gst, row)
                        add_lg(row, (C * gc) * (gst * st_ref[h, j]))
                        return dst_ref[h, j] + gc * gst
                    return lax.fori_loop(0, n, step, jnp.zeros((RET_DK, RET_DK), F32), unroll=True)

                gcf, gcb = jnp.exp(lgf * C), jnp.exp(lgb * C)
                g0 = scan(gcf, jnp.exp(lgf * (C - 1.0 - tau)), C - 1.0 - tau, sf_ref, dsf_ref,
                          lambda r: n - 1 - r, 0)
                through_state(crow, jnp.exp(lgf * (LC - 1.0 - jc)), LC - 1.0 - jc, g0, 0)
                g1 = scan(gcb, jnp.exp(lgb * tau), tau, sb_ref, dsb_ref, lambda r: r, 1)
                through_state(crow, jnp.exp(lgb * jc), jc, g1, 1)
                dk = dk_ref[:, ln]
                dk_ref[:, ln] = (dk * cos_ref[...] - pltpu.roll(dk, 64, 1) * sin_ref[...]) * kscale

        for h in range(4):
            one_head(h)

    sec = lambda k: pl.BlockSpec((None, TQ, 512), lambda b, i: (b, i, k))
    full = lambda k: pl.BlockSpec((None, T, 512), lambda b, i: (b, 0, k))
    dec_spec = pl.BlockSpec((1, 4), lambda b, i: (0, 0))
    tab = pl.BlockSpec((T, RET_DK), lambda b, i: (0, 0))
    return pl.pallas_call(
        body, name="ret_bwd", grid=(B, n),
        in_specs=[dec_spec, dec_spec, sec(4), full(5), full(6), sec(7),
                  pl.BlockSpec((1, 512), lambda b, i: (0, 0)), sec(0), sec(1), tab, tab, st_spec, st_spec],
        out_specs=(sec(0), sec(0), full(0), full(0),
                   pl.BlockSpec((None, 1, 512), lambda b, i: (b, 0, 0)),
                   pl.BlockSpec((None, 4, 8, 128), lambda b, i: (b, 0, 0, 0))),
        out_shape=(jax.ShapeDtypeStruct((B, L, 512), BF16), jax.ShapeDtypeStruct((B, L, 512), BF16),
                   jax.ShapeDtypeStruct((B, T, 512), F32), jax.ShapeDtypeStruct((B, T, 512), F32),
                   jax.ShapeDtypeStruct((B, 1, 512), F32), jax.ShapeDtypeStruct((B, 4, 8, 128), F32)),
        scratch_shapes=[pltpu.VMEM((4, n, RET_DK, RET_DK), F32), pltpu.VMEM((4, n, RET_DK, RET_DK), F32)],
        compiler_params=_params(("arbitrary",) * 2, vmem_mb=56))(
            dec_f, dec_b, P, P, P, P, ret_norm_g, o_ret, dY, cos2, sin2, sf, sb)


def _out_call(y_na, y_ret, x, target, mod, final_g, wout_f):
    B, L, _ = x.shape
    TO = 2 * TQ

    def body(yn_ref, yr_ref, x_ref, t_ref, mod_ref, gf_ref, w_ref, dy_ref, dx2_ref, dwb_ref, sm_ref, dw_ref):
        b, i = pl.program_id(0), pl.program_id(1)

        @pl.when((b == 0) & (i == 0))
        def _():
            dw_ref[...] = jnp.zeros_like(dw_ref)
            sm_ref[...] = jnp.zeros_like(sm_ref)

        gate = mod_ref[pl.ds(b, 1), 2 * D:3 * D]
        gf = gf_ref[...]
        yn, yr = yn_ref[...], yr_ref[...]
        ylat = _dot(yn, w_ref[0:512, :]) + _dot(yr, w_ref[512:1024, :])
        x2 = x_ref[...] + gate * ylat
        r = lax.rsqrt(jnp.mean(x2 * x2, axis=-1, keepdims=True) + EPS)
        xr = x2 * r
        err = xr * gf - t_ref[...]
        sm_ref[1:2, :] += jnp.sum(err * err, axis=0, keepdims=True)
        dout = err * (1.0 / D)
        sm_ref[0:1, :] += jnp.sum(dout * xr, axis=0, keepdims=True)
        gd = dout * gf
        dx2 = r * (gd - xr * jnp.mean(gd * xr, axis=-1, keepdims=True))
        dx2_ref[...] = dx2
        sm_ref[pl.ds(2 + b, 1), :] += jnp.sum(dx2 * ylat, axis=0, keepdims=True)
        dyl = (gate * dx2).astype(BF16)
        dy_ref[:, 0:512] = _dot_nt(dyl, w_ref[0:512, :]).astype(BF16)
        dy_ref[:, 512:1024] = _dot_nt(dyl, w_ref[512:1024, :]).astype(BF16)
        dw_ref[0:512, :] += _dot_tn(yn, dyl)
        dw_ref[512:1024, :] += _dot_tn(yr, dyl)

        @pl.when((b == B - 1) & (i == L // TO - 1))
        def _():
            dwb_ref[...] = dw_ref[...].astype(BF16)

    half = pl.BlockSpec((None, TO, 512), lambda b, i: (b, i, 0))
    full = pl.BlockSpec((None, TO, D), lambda b, i: (b, i, 0))
    return pl.pallas_call(
        body, name="out_proj_loss", grid=(B, L // TO),
        in_specs=[half, half, full, full,
                  pl.BlockSpec((8, 3 * D), lambda b, i: (0, 0)),
                  pl.BlockSpec((1, D), lambda b, i: (0, 0)),
                  pl.BlockSpec((D, D), lambda b, i: (0, 0))],
        out_specs=(full, full, pl.BlockSpec((D, D), lambda b, i: (0, 0)),
                   pl.BlockSpec((8, D), lambda b, i: (0, 0))),
        out_shape=(jax.ShapeDtypeStruct((B, L, D), BF16), jax.ShapeDtypeStruct((B, L, D), F32),
                   jax.ShapeDtypeStruct((D, D), BF16), jax.ShapeDtypeStruct((8, D), F32)),
        scratch_shapes=[pltpu.VMEM((D, D), F32)],
        compiler_params=_params(("arbitrary",) * 2))(y_na, y_ret, x, target, mod, final_g, wout_f)


def _dh_call(dsec, win_f, x, ctx, dx2, mod, norm_g, cp_in, cp_out):
    B, L, _ = x.shape
    LC = ctx.shape[1]
    nl = L // TQ

    def body(d0, d1, d2, d3, d4, d5, d6, d7, w_ref, x_ref, ctx_ref, dx2_ref, mod_ref, g_ref, cpi_ref, cpo_ref,
             gx_ref, sm_ref, sli_ref, slo_ref, ssem, rsem, lsem):
        drefs = (d0, d1, d2, d3, d4, d5, d6, d7)
        b, t = pl.program_id(0), pl.program_id(1)
        is_lat = t < nl

        @pl.when((b == 0) & (t == 0))
        def _():
            sm_ref[...] = jnp.zeros_like(sm_ref)

        def dh_of(secs):
            acc = jnp.zeros((TQ, D), F32)
            for sec in secs:
                s, half = divmod(sec, 2)
                acc = acc + _dot_nt(drefs[sec][...].astype(BF16), w_ref[s, :, half * 512:(half + 1) * 512])
            return acc

        def norm_bwd(dh, xt, mrow):
            scale = mrow[:, D:2 * D]
            g = g_ref[...]
            rstd = lax.rsqrt(jnp.mean(xt * xt, axis=-1, keepdims=True) + EPS)
            xn = xt * rstd
            dshift = jnp.sum(dh, axis=0, keepdims=True)
            dscale = jnp.sum(dh * (xn * g), axis=0, keepdims=True)
            dhn = dh * (1.0 + scale)
            sm_ref[0:1, :] += jnp.sum(dhn * xn, axis=0, keepdims=True)
            dxn = dhn * g
            dx = rstd * (dxn - xn * jnp.mean(dxn * xn, axis=-1, keepdims=True))
            return dshift, dscale, dx

        @pl.when(is_lat)
        def _():
            dshift, dscale, dx = norm_bwd(dh_of(range(8)), x_ref[...], mod_ref[pl.ds(b, 1), :])
            sm_ref[pl.ds(3 + b, 1), :] += dshift
            sm_ref[pl.ds(3 + B + b, 1), :] += dscale
            gx_ref[...] = dx2_ref[...] + dx

        @pl.when(jnp.logical_not(is_lat))
        def _():
            dshift, dscale, _ = norm_bwd(dh_of((1, 2, 5, 6)), ctx_ref[...], mod_ref[B:B + 1, :])
            sm_ref[1:2, :] += dshift
            sm_ref[2:3, :] += dscale

        mx, my, mc = _mesh_pos()
        s = 2 * mx + my
        cps, sls = (cpi_ref, cpo_ref), (sli_ref, slo_ref)
        own = [pltpu.make_async_copy(cps[a].at[s], sls[a].at[s], lsem.at[a]) for a in range(2)]
        sends, recvs, k = [], [], 0
        for px, py in _other_chips(mx, my):
            ps = 2 * px + py
            for a in range(2):
                sends.append(_remote(cps[a].at[ps], sls[a].at[s], ssem, rsem, k, (px, py, mc)))
                recvs.append(_remote(cps[a].at[s], sls[a].at[ps], ssem, rsem, k, (px, py, mc)))
                k += 1

        @pl.when((b == 0) & (t == 0))
        def _():
            for cp in own + sends:
                cp.start()

        @pl.when((b == B - 1) & (t == nl))
        def _():
            _finish(own, sends, recvs)

    lat = lambda b, t: (b, jnp.minimum(t, nl - 1), 0)
    tok = lambda b, t: (b, t, 0)
    sec_specs = [pl.BlockSpec((None, TQ, 512), lat if sec in (0, 3, 4, 7) else tok) for sec in range(8)]
    return pl.pallas_call(
        body, name="dh_norm_bwd", grid=(B, nl + 1),
        in_specs=sec_specs + [
            pl.BlockSpec((N_SHARD, D, D), lambda b, t: (0, 0, 0)),
            pl.BlockSpec((None, TQ, D), lat),
            pl.BlockSpec((None, LC, D), lambda b, t: (b, 0, 0)),
            pl.BlockSpec((None, TQ, D), lat),
            pl.BlockSpec((8, 3 * D), lambda b, t: (0, 0)),
            pl.BlockSpec((1, D), lambda b, t: (0, 0)), ANY, ANY],
        out_specs=(pl.BlockSpec((None, TQ, D), lat), pl.BlockSpec((8, D), lambda b, t: (0, 0)), ANY, ANY),
        out_shape=(jax.ShapeDtypeStruct((B, L, D), F32), jax.ShapeDtypeStruct((8, D), F32),
                   jax.ShapeDtypeStruct(cp_in.shape, cp_in.dtype), jax.ShapeDtypeStruct(cp_out.shape, cp_out.dtype)),
        scratch_shapes=[pltpu.SemaphoreType.DMA((6,)), pltpu.SemaphoreType.DMA((6,)),
                        pltpu.SemaphoreType.DMA((2,))],
        compiler_params=_params(("arbitrary",) * 2))(*dsec, win_f, x, ctx, dx2, mod, norm_g, cp_in, cp_out)


def _dw_call(dsec, h, L):
    B, T, _ = h.shape
    TW = 2 * TQ
    nl = L // TW
    KV = (1, 2, 5, 6)

    def body(d0, d1, d2, d3, d4, d5, d6, d7, c1, c2, c5, c6, h_ref, hc_ref, dw_ref, acc_ref):
        drefs = (d0, d1, d2, d3, d4, d5, d6, d7)
        crefs = dict(zip(KV, (c1, c2, c5, c6)))
        b, t = pl.program_id(0), pl.program_id(1)

        @pl.when((b == 0) & (t == 0))
        def _():
            acc_ref[...] = jnp.zeros_like(acc_ref)

        def add(hb, refs, secs):
            for sec in secs:
                s, half = divmod(sec, 2)
                acc_ref[s, :, half * 512:(half + 1) * 512] += _dot_tn(hb, refs[sec][...].astype(BF16))

        @pl.when(t < nl)
        def _():
            add(h_ref[...], drefs, range(8))

        @pl.when(t == nl)
        def _():
            add(hc_ref[...], crefs, KV)

        @pl.when((b == B - 1) & (t == nl))
        def _():
            dw_ref[...] = acc_ref[...].astype(BF16)

    lat = lambda b, t: (b, jnp.minimum(t, nl - 1), 0)
    ctx = lambda b, t: (b, L // TQ, 0)
    return pl.pallas_call(
        body, name="dw_in", grid=(B, nl + 1),
        in_specs=[pl.BlockSpec((None, TW, 512), lat)] * 8 + [pl.BlockSpec((None, TQ, 512), ctx)] * 4
        + [pl.BlockSpec((None, TW, D), lat), pl.BlockSpec((None, TQ, D), ctx)],
        out_specs=pl.BlockSpec((N_SHARD, D, D), lambda b, t: (0, 0, 0)),
        out_shape=jax.ShapeDtypeStruct((N_SHARD, D, D), BF16),
        scratch_shapes=[pltpu.VMEM((N_SHARD, D, D), F32)],
        compiler_params=_params(("arbitrary",) * 2, vmem_mb=60))(*dsec, *[dsec[k] for k in KV], h, h)


def _mesh_pos():
    return lax.axis_index("x"), lax.axis_index("y"), lax.axis_index("c")


def _flip(v, f):
    return 1 - v if f else v


def _remote(src, dst, ssem, rsem, k, peer):
    return pltpu.make_async_remote_copy(src_ref=src, dst_ref=dst, send_sem=ssem.at[k], recv_sem=rsem.at[k],
                                        device_id=peer, device_id_type=MESH)


def _other_chips(x, y):
    return [(_flip(x, fx), _flip(y, fy)) for fx, fy in ((1, 0), (0, 1), (1, 1))]


def _all_to_all_small(src, dst_all, ssem, rsem, k0, x, y, cc):
    me = 4 * x + 2 * y + cc
    sends, recvs = [], []
    for f in range(1, N_DEV):
        px, py, pc = _flip(x, f & 4), _flip(y, f & 2), _flip(cc, f & 1)
        sends.append(_remote(src, dst_all.at[me], ssem, rsem, k0 + f - 1, (px, py, pc)))
        recvs.append(_remote(src, dst_all.at[4 * px + 2 * py + pc], ssem, rsem, k0 + f - 1, (px, py, pc)))
    return sends, recvs


def _finish(local, sends, recvs):
    for cp in recvs:
        cp.wait_recv()
    for cp in sends:
        cp.wait_send()
    for cp in local:
        cp.wait()


def _gather_call(wout_b, wada_b, c, rpb_flat):
    arrs = (wout_b, wada_b)
    na = len(arrs)
    hrs = [a.shape[0] // 2 for a in arrs]

    def body(wout, wada, c_ref, r_ref, wout_f, wada_f, c_all, bias_ref, et_ref, ssem, rsem, lsem):
        x, y, cc = _mesh_pos()
        s, me = 2 * x + y, 4 * x + 2 * y + cc
        sib = (x, y, 1 - cc)
        srcs, dsts = (wout, wada), (wout_f, wada_f)

        def half(a, shard, hc):
            return dsts[a].at[shard, pl.ds(hc * hrs[a], hrs[a])]

        local = [pltpu.make_async_copy(srcs[a], dsts[a].at[s], lsem.at[a]) for a in range(na)]
        local.append(pltpu.make_async_copy(c_ref, c_all.at[me], lsem.at[na]))
        ici_send, ici_recv, fwd_send, fwd_recv, k = [], [], [], [], 0
        for px, py in _other_chips(x, y):
            ps = 2 * px + py
            for a in range(na):
                mine = srcs[a].at[pl.ds(cc * hrs[a], hrs[a])]
                ici_send.append(_remote(mine, half(a, s, cc), ssem, rsem, k, (px, py, cc)))
                ici_recv.append(_remote(mine, half(a, ps, cc), ssem, rsem, k, (px, py, cc)))
                fwd_send.append(_remote(half(a, ps, cc), half(a, ps, cc), ssem, rsem, 3 * na + k, sib))
                fwd_recv.append(_remote(half(a, ps, 1 - cc), half(a, ps, 1 - cc), ssem, rsem, 3 * na + k, sib))
                k += 1
        c_send, c_recv = _all_to_all_small(c_ref, c_all, ssem, rsem, 6 * na, x, y, cc)
        for cp in local + ici_send + c_send:
            cp.start()
        _bias_body(r_ref, bias_ref, et_ref)
        for got, fwd in zip(ici_recv, fwd_send):
            got.wait_recv()
            fwd.start()
        _finish(local, ici_send + fwd_send + c_send, fwd_recv + c_recv)

    return pl.pallas_call(
        body, name="weight_gather",
        in_specs=[pl.BlockSpec(memory_space=pltpu.VMEM)] * 3 + [pl.BlockSpec(memory_space=pltpu.SMEM)],
        out_specs=(pl.BlockSpec(memory_space=pltpu.VMEM),) * 4,
        out_shape=tuple(jax.ShapeDtypeStruct((N_SHARD,) + a.shape, a.dtype) for a in arrs)
        + (jax.ShapeDtypeStruct((N_DEV,) + c.shape, c.dtype),
           jax.ShapeDtypeStruct((rpb_flat.shape[0], 3, TQ, KW), F32)),
        scratch_shapes=[pltpu.VMEM((15, GRID_W, GRID_W), F32),
                        pltpu.SemaphoreType.DMA((6 * na + 7,)), pltpu.SemaphoreType.DMA((6 * na + 7,)),
                        pltpu.SemaphoreType.DMA((na + 1,))],
        compiler_params=pltpu.CompilerParams(vmem_limit_bytes=56 << 20))(wout_b, wada_b, c, rpb_flat)


VROWS = 32


def _grad_halves_call(dwin_b, dwout_b, dbias, dlg):
    arrs = (dwin_b, dwout_b)
    hrs = [a.shape[1] // 2 for a in arrs]

    def body(din, dout, db_ref, dlg_ref, cp_in, cp_out, drpb_ref, dlgo_ref, got_in, got_out, p_ref, ssem, rsem):
        x, y, cc = _mesh_pos()
        sib = (x, y, 1 - cc)
        srcs, gots, cps = (din, dout), (got_in, got_out), (cp_in, cp_out)
        halves = [_remote(srcs[a].at[:, pl.ds((1 - cc) * hrs[a], hrs[a])], gots[a], ssem, rsem, a, sib)
                  for a in range(2)]
        for cp in halves:
            cp.start()
        _small_reduce_body(db_ref, dlg_ref, drpb_ref, dlgo_ref, p_ref)
        for cp in halves:
            cp.wait_recv()
        for a in range(2):
            for j in range(N_SHARD):
                def add(i, carry, a=a, j=j):
                    r = pl.multiple_of(i * VROWS, VROWS)
                    mine = srcs[a][j, pl.ds(pl.multiple_of(cc * hrs[a] + r, VROWS), VROWS), :].astype(F32)
                    cps[a][j, pl.ds(r, VROWS), :] = (
                        mine + gots[a][j, pl.ds(r, VROWS), :].astype(F32)).astype(BF16)
                    return carry
                lax.fori_loop(0, hrs[a] // VROWS, add, 0)
        for cp in halves:
            cp.wait_send()

    vmem = pl.BlockSpec(memory_space=pltpu.VMEM)
    half_shapes = [(N_SHARD, hrs[a], arrs[a].shape[2]) for a in range(2)]
    return pl.pallas_call(
        body, name="grad_halves",
        in_specs=[vmem] * 4, out_specs=(vmem,) * 4,
        out_shape=(jax.ShapeDtypeStruct(half_shapes[0], BF16), jax.ShapeDtypeStruct(half_shapes[1], BF16),
                   jax.ShapeDtypeStruct((dbias.shape[0], 16, 32), F32), jax.ShapeDtypeStruct((32, 128), F32)),
        scratch_shapes=[pltpu.VMEM(half_shapes[0], BF16), pltpu.VMEM(half_shapes[1], BF16),
                        pltpu.VMEM((32, GRID_W), F32),
                        pltpu.SemaphoreType.DMA((2,)), pltpu.SemaphoreType.DMA((2,))],
        compiler_params=pltpu.CompilerParams(vmem_limit_bytes=56 << 20))(dwin_b, dwout_b, dbias, dlg)


def _grad_finish_call(sl_in, sl_out, small):
    arrs = (sl_in, sl_out)

    def body(sin, sout, sm, gin, gout, sm_all, h_in, h_out, ssem, rsem, lsem):
        x, y, cc = _mesh_pos()
        me = 4 * x + 2 * y + cc
        sib = (x, y, 1 - cc)
        sls, hs, gs = (sin, sout), (h_in, h_out), (gin, gout)
        sm_send, sm_recv = _all_to_all_small(sm, sm_all, ssem, rsem, 2, x, y, cc)
        sm_own = pltpu.make_async_copy(sm, sm_all.at[me], lsem.at[0])
        for cp in sm_send + [sm_own]:
            cp.start()
        for a in range(2):
            def total(i, carry, a=a):
                rows = pl.ds(pl.multiple_of(i * VROWS, VROWS), VROWS)
                sl = sls[a]
                hs[a][rows, :] = ((sl[0, rows, :].astype(F32) + sl[1, rows, :].astype(F32))
                                  + sl[2, rows, :].astype(F32)) + sl[3, rows, :].astype(F32)
                return carry
            lax.fori_loop(0, arrs[a].shape[1] // VROWS, total, 0)
        mine = [pltpu.make_async_copy(hs[a], gs[a].at[cc], lsem.at[1 + a]) for a in range(2)]
        back = [_remote(hs[a], gs[a].at[cc], ssem, rsem, a, sib) for a in range(2)]
        back_recv = [_remote(hs[a], gs[a].at[1 - cc], ssem, rsem, a, sib) for a in range(2)]
        for cp in mine + back:
            cp.start()
        _finish(mine + [sm_own], back + sm_send, back_recv + sm_recv)

    vmem = pl.BlockSpec(memory_space=pltpu.VMEM)
    return pl.pallas_call(
        body, name="grad_finish",
        in_specs=[vmem] * 3, out_specs=(vmem,) * 3,
        out_shape=(jax.ShapeDtypeStruct((2,) + sl_in.shape[1:], F32),
                   jax.ShapeDtypeStruct((2,) + sl_out.shape[1:], F32),
                   jax.ShapeDtypeStruct((N_DEV,) + small.shape, F32)),
        scratch_shapes=[pltpu.VMEM(sl_in.shape[1:], F32), pltpu.VMEM(sl_out.shape[1:], F32),
                        pltpu.SemaphoreType.DMA((9,)), pltpu.SemaphoreType.DMA((9,)),
                        pltpu.SemaphoreType.DMA((3,))],
        compiler_params=pltpu.CompilerParams(vmem_limit_bytes=48 << 20))(sl_in, sl_out, small)


def _adamw(w, g, m, v):
    m = ADAM_B1 * m + (1.0 - ADAM_B1) * g
    v = ADAM_B2 * v + (1.0 - ADAM_B2) * (g * g)
    m_hat = m / (1.0 - ADAM_B1 ** ADAM_STEP)
    v_hat = v / (1.0 - ADAM_B2 ** ADAM_STEP)
    return -ADAM_LR * (m_hat / (jnp.sqrt(v_hat) + ADAM_EPS) + ADAM_WD * w), m, v


def _adam_call(w, m, v, g, name):
    R, C = w.shape
    tr = 256

    def body(w_ref, m_ref, v_ref, g_ref, go_ref, d_ref, mo_ref, vo_ref):
        g = g_ref[...]
        go_ref[...] = g
        d_ref[...], mo_ref[...], vo_ref[...] = _adamw(w_ref[...], g, m_ref[...], v_ref[...])

    spec = pl.BlockSpec((tr, C), lambda i: (i, 0))
    return pl.pallas_call(
        body, name=name, grid=(R // tr,), in_specs=[spec] * 4,
        out_specs=(spec,) * 4, out_shape=(jax.ShapeDtypeStruct((R, C), F32),) * 4,
        compiler_params=_params(("arbitrary",)))(w, m, v, g)


R_GF, R_NG, R_LOSS, R_RNG, R_LGF, R_LGB, R_SHIFT, R_SCALE, R_GATE, R_SHIFT_C, R_SCALE_C, R_RNG2, R_RPB = (
    0, 1, 2, 3, 4, 5, 6, 8, 10, 12, 13, 14, 16)
W_GF, W_NG, W_CCTX, W_RNG, W_DF, W_DB, W_BADA, W_RPB = 0, 1, 2, 3, 4, 5, 6, 9


SMALL = (("final_norm_g", W_GF, 1, D), ("norm_g", W_NG, 1, D), ("c_ctx", W_CCTX, 1, D),
         ("ret_norm_g", W_RNG, 1, 512), ("ret_decay_fwd", W_DF, 1, 4), ("ret_decay_bwd", W_DB, 1, 4),
         ("b_ada", W_BADA, 3, D), ("na_rpb", W_RPB, 4, D))
N_SMALL = len(SMALL)


def _small_final_call(sm_all, c_t, wada_f, wada, m_ada, v_ada, small_w, small_m, small_v, B):
    ws = wada.shape[1]
    NB = N_DEV * B

    def body(*refs):
        sm_ref, ct_ref, wf_ref, wa_ref, ma_ref, va_ref = refs[:6]
        ins = refs[6:6 + 3 * N_SMALL]
        outs = refs[6 + 3 * N_SMALL:6 + 7 * N_SMALL]
        ga_ref, da_ref, mao_ref, vao_ref, loss_ref, dmod_ref, pk_ref = refs[6 + 7 * N_SMALL:]
        x, y, _ = _mesh_pos()
        s = 2 * x + y
        tot = sm_ref[0]
        for dv in range(1, N_DEV):
            tot = tot + sm_ref[dv]
        pk_ref[...] = jnp.zeros_like(pk_ref)
        for kind in range(3):
            for i, (_, row, nrow, width) in enumerate(SMALL):
                ref = ins[kind * N_SMALL + i]
                if nrow == 3:
                    for part in range(3):
                        pk_ref[kind, row + part:row + part + 1, :] = ref[:, part * D:(part + 1) * D]
                else:
                    pk_ref[kind, row:row + nrow, 0:width] = ref[...]
        w = pk_ref[0]
        cctx_ref = ins[2]
        for dv in range(N_DEV):
            for b in range(B):
                r = dv * B + b
                for part, row in enumerate((R_SHIFT, R_SCALE, R_GATE)):
                    dmod_ref[r:r + 1, part * D:(part + 1) * D] = sm_ref[dv, row + b:row + b + 1, :]
        dmod_ref[NB:NB + 1, 0:D] = tot[R_SHIFT_C:R_SHIFT_C + 1, :]
        dmod_ref[NB:NB + 1, D:2 * D] = tot[R_SCALE_C:R_SCALE_C + 1, :]
        dmod_ref[NB:NB + 1, 2 * D:3 * D] = jnp.zeros((1, D), F32)
        dmod_ref[NB + 1:, :] = jnp.zeros((dmod_ref.shape[0] - NB - 1, 3 * D), F32)
        dmod = dmod_ref[...]
        cc = cctx_ref[...]
        scc = _sigmoid(cc)
        ct = ct_ref[...]
        act_t = ct * _sigmoid(ct)
        dmc = dmod[NB:NB + 1, :].astype(BF16)
        dact = jnp.zeros((1, D), F32)
        for sh in range(N_SHARD):
            dact = dact + _dot_nt(dmc[:, sh * ws:(sh + 1) * ws], wf_ref[sh])
        g = jnp.zeros((16, D), F32)
        rows = lax.broadcasted_iota(jnp.int32, (16, D), 0)

        def put(g, row, val):
            return jnp.where(rows == row, val, g)

        g = put(g, W_GF, tot[R_GF:R_GF + 1, :])
        g = put(g, W_NG, tot[R_NG:R_NG + 1, :])
        g = put(g, W_CCTX, dact * (scc * (1.0 + cc * (1.0 - scc))))
        g = put(g, W_RNG, tot[R_RNG:R_RNG + 1, :] + tot[R_RNG2:R_RNG2 + 1, :])
        g = put(g, W_DF, tot[R_LGF:R_LGF + 1, :] * (-jnp.exp(w[W_DF:W_DF + 1, :])))
        g = put(g, W_DB, tot[R_LGB:R_LGB + 1, :] * (-jnp.exp(w[W_DB:W_DB + 1, :])))
        db = jnp.sum(dmod, axis=0, keepdims=True)
        for part in range(3):
            g = put(g, W_BADA + part, db[:, part * D:(part + 1) * D])
        for part in range(4):
            g = put(g, W_RPB + part, tot[R_RPB + part:R_RPB + part + 1, :])
        for kind, val in enumerate((g,) + _adamw(w, g, pk_ref[1], pk_ref[2])):
            for i, (_, row, nrow, width) in enumerate(SMALL):
                out = outs[kind * N_SMALL + i]
                if nrow == 3:
                    for part in range(3):
                        out[:, part * D:(part + 1) * D] = val[row + part:row + part + 1, :]
                else:
                    out[...] = val[row:row + nrow, 0:width]
        loss_ref[...] = jnp.broadcast_to(
            (0.5 / D) * jnp.sum(tot[R_LOSS:R_LOSS + 1, :], axis=1, keepdims=True), (8, 128))
        for sh in range(N_SHARD):
            @pl.when(s == sh)
            def _():
                ga = jnp.dot(act_t, dmod[:, sh * ws:(sh + 1) * ws], precision=HIGHEST,
                             preferred_element_type=F32)
                ga_ref[...] = ga
                da_ref[...], mao_ref[...], vao_ref[...] = _adamw(wa_ref[...], ga, ma_ref[...], va_ref[...])

    sh_small = tuple(jax.ShapeDtypeStruct(a.shape, F32) for a in small_w)
    sh_ada = jax.ShapeDtypeStruct(wada.shape, F32)
    res = pl.pallas_call(
        body, name="small_final",
        out_shape=sh_small * 4 + (sh_ada,) * 4 + (jax.ShapeDtypeStruct((8, 128), F32),),
        scratch_shapes=[pltpu.VMEM((NB + 8, 3 * D), F32), pltpu.VMEM((3, 16, D), F32)],
        compiler_params=_params(vmem_mb=56))(
            sm_all, c_t, wada_f, wada, m_ada, v_ada, *small_w, *small_m, *small_v)
    smalls = [res[k * N_SMALL:(k + 1) * N_SMALL] for k in range(4)]
    return smalls, res[4 * N_SMALL:4 * N_SMALL + 4], res[4 * N_SMALL + 4]


def _local_step(order, x, c, ctx, c_ctx, norm_g, wada_f, b_ada, win_b, bias, dec_f, dec_b, ret_norm_g,
                wout_f, final_g, target):
    B, L, _ = x.shape
    LC = ctx.shape[1]
    assert B == 2
    cos2, sin2 = _rope_tables(L, LC)
    c8 = jnp.concatenate([c, c_ctx[None, :], jnp.zeros((8 - B - 1, D), F32)], axis=0)
    mod = _mod_call(c8, wada_f, b_ada)
    P, h, win_f = _inproj_gather_call(order, x, ctx, mod, norm_g, win_b, cos2, sin2)
    y_na, o_na = _na_fwd_call(P, bias, L, LC)
    sf, sb = _ret_states_call(P, dec_f, dec_b, L, LC)
    y_ret, o_ret = _retc_fwd_call(P, sf, sb, dec_f, dec_b, ret_norm_g, L)
    dY, dx2, dwout_p, sm_out = _out_call(y_na, y_ret, x, target, mod, final_g, wout_f.reshape(D, D))
    dnq, dng, dnk, dnv, dbias = _na_bwd_call(P, bias, dY, o_na, L, LC)
    drq, drg, drk, drv, dgn, dlg = _retc_bwd_call(P, sf, sb, dec_f, dec_b, ret_norm_g, o_ret, dY, cos2, sin2, L, LC)
    dsec = (dnq, dnk, dnv, dng, drq, drk, drv, drg)
    dwin_b = _dw_call(dsec, h, L)
    cp_in, cp_out, drpb, dlg_sum = _grad_halves_call(
        dwin_b, dwout_p.reshape(N_SHARD, D // N_SHARD, D), dbias, dlg)
    grad_x, sm_dh, sl_in, sl_out = _dh_call(dsec, win_f, x, ctx, dx2, mod, norm_g, cp_in, cp_out)
    z = jnp.zeros((1, D), F32)
    pad = lambda v: jnp.pad(v.reshape(1, -1), ((0, 0), (0, D - v.size)))
    dlg_sum = dlg_sum.reshape(4, 8, 128)
    rpb_rows = jnp.pad(drpb[:, :15, :31].reshape(-1), (0, 4 * D - drpb.shape[0] * 465)).reshape(4, D)
    small = jnp.concatenate([
        sm_out[0:1], sm_dh[0:1], sm_out[1:2], pad(dgn[0]), pad(dlg_sum[:, 0, 0]), pad(dlg_sum[:, 1, 0]),
        sm_dh[3:5], sm_dh[5:7], sm_out[2:4], sm_dh[1:2], sm_dh[2:3], pad(dgn[1]), z, rpb_rows,
        jnp.zeros((SM_ROWS - 20, D), F32)], axis=0)
    return grad_x, sl_in, sl_out, small


def kernel(x, c, ctx, c_ctx, norm_g, w_ada, b_ada, w_in, na_rpb, ret_decay_fwd, ret_decay_bwd, ret_norm_g, w_out, final_norm_g, loss_target, m_c_ctx, m_norm_g, m_w_ada, m_b_ada, m_w_in, m_na_rpb, m_ret_decay_fwd, m_ret_decay_bwd, m_ret_norm_g, m_w_out, m_final_norm_g, v_c_ctx, v_norm_g, v_w_ada, v_b_ada, v_w_in, v_na_rpb, v_ret_decay_fwd, v_ret_decay_bwd, v_ret_norm_g, v_w_out, v_final_norm_g):
    B = x.shape[0]
    wout_f, wada_f, c_all, bias = _gather_call(
        w_out[0].astype(BF16), w_ada[0].astype(BF16), c, na_rpb[0].reshape(na_rpb.shape[1], -1))
    mx, my = lax.axis_index("x"), lax.axis_index("y")
    order = jnp.stack([2 * mx + my, 2 * (1 - mx) + my, 2 * mx + (1 - my),
                       2 * (1 - mx) + (1 - my)]).astype(jnp.int32)
    grad_x, sl_in, sl_out, small = _local_step(
        order, x, c, ctx, c_ctx, norm_g, wada_f, b_ada, w_in[0].astype(BF16), bias, ret_decay_fwd,
        ret_decay_bwd, ret_norm_g, wout_f, final_norm_g.reshape(1, D), loss_target)
    gin, gout, sm_all = _grad_finish_call(sl_in, sl_out, small)
    g_win, d_win, nm_win, nv_win = _adam_call(
        w_in[0], m_w_in[0], v_w_in[0], gin.reshape(w_in.shape[1:]), "adam_w_in")
    g_wout, d_wout, nm_wout, nv_wout = _adam_call(
        w_out[0], m_w_out[0], v_w_out[0], gout.reshape(w_out.shape[1:]), "adam_w_out")

    def small_inputs(gf, ng, cc, rng, df, db, bada, rpb):
        return (gf.reshape(1, D), ng, cc.reshape(1, D), rng, df, db, bada,
                jnp.pad(rpb.reshape(-1), (0, 4 * D - rpb.size)).reshape(4, D))

    c_t = jnp.concatenate([c_all.reshape(N_DEV * B, D), c_ctx.reshape(1, D), jnp.zeros((7, D), F32)], axis=0).T
    smalls, adas, loss = _small_final_call(
        sm_all, c_t, wada_f, w_ada[0], m_w_ada[0], v_w_ada[0],
        small_inputs(final_norm_g, norm_g, c_ctx, ret_norm_g, ret_decay_fwd, ret_decay_bwd, b_ada, na_rpb),
        small_inputs(m_final_norm_g, m_norm_g, m_c_ctx, m_ret_norm_g, m_ret_decay_fwd, m_ret_decay_bwd, m_b_ada,
                     m_na_rpb),
        small_inputs(v_final_norm_g, v_norm_g, v_c_ctx, v_ret_norm_g, v_ret_decay_fwd, v_ret_decay_bwd, v_b_ada,
                     v_na_rpb), B)
    res = []
    for p, ada, win_o, wout_o in zip(smalls, adas, (g_win, d_win, nm_win, nv_win),
                                     (g_wout, d_wout, nm_wout, nv_wout)):
        gf, ng, cc, rng, df, db, bada, rpb = p
        res.append([cc.reshape(D), ng, ada[None], bada, win_o[None],
                    rpb.reshape(-1)[:na_rpb.size].reshape(na_rpb.shape), df, db, rng, wout_o[None], gf.reshape(D)])
    return (loss[0, 0], grad_x, *res[0], *res[1], *res[2], *res[3])
```

```python
import numpy as np
import jax
import jax.numpy as jnp
from jax import lax
from jax.experimental import pallas as pl
from jax.experimental.pallas import tpu as pltpu

F32 = jnp.float32
BF16 = jnp.bfloat16
HIGHEST = lax.Precision.HIGHEST

D = 1024
GRID_W = 64
NA_DH = 64
RET_DK = 128
ROPE_BASE = 10000.0
EPS = 1e-6
NEG = -1e30
TQ = 256
KW = 12 * GRID_W
N_SHARD = 4
N_DEV = 8
SM_ROWS = 24

ADAM_LR = 0.001
ADAM_B1 = 0.9
ADAM_B2 = 0.999
ADAM_EPS = 1e-08
ADAM_WD = 0.01
ADAM_STEP = 10

MESH = pl.DeviceIdType.MESH
ANY = pl.BlockSpec(memory_space=pl.ANY)


def _params(sem=None, vmem_mb=48):
    return pltpu.CompilerParams(dimension_semantics=sem, vmem_limit_bytes=vmem_mb << 20)


def _dot(a, b):
    return jnp.dot(a, b, preferred_element_type=F32)


def _dot_nt(a, b):
    return lax.dot_general(a, b, (((1,), (1,)), ((), ())), preferred_element_type=F32)


def _dot_tn(a, b):
    return lax.dot_general(a, b, (((0,), (0,)), ((), ())), preferred_element_type=F32)


def _sigmoid(x):
    return 1.0 / (1.0 + jnp.exp(-x))


def _rope_tables(L, LC):
    half = RET_DK // 2
    nf = half // 2
    t = np.arange(L)
    row = (t // GRID_W).astype(np.float32)
    col = (t % GRID_W).astype(np.float32)
    inv = (np.float32(ROPE_BASE) ** (-np.arange(nf, dtype=np.float32) / np.float32(nf))).astype(np.float32)
    ang = np.concatenate([row[:, None] * inv, col[:, None] * inv], axis=-1).astype(np.float32)
    cos, sin = np.cos(ang).astype(np.float32), np.sin(ang).astype(np.float32)
    cos2 = np.concatenate([cos, cos], axis=-1)
    sin2 = np.concatenate([-sin, sin], axis=-1)
    cos2 = np.concatenate([cos2, np.ones((LC, RET_DK), np.float32)], axis=0)
    sin2 = np.concatenate([sin2, np.zeros((LC, RET_DK), np.float32)], axis=0)
    return jnp.asarray(cos2), jnp.asarray(sin2)


def _mod_call(c8, wada_f, b_ada):
    ws = wada_f.shape[2]

    def body(c_ref, w_ref, b_ref, o_ref):
        a = c_ref[...]
        a = (a * _sigmoid(a)).astype(BF16)
        for s in range(N_SHARD):
            o_ref[:, s * ws:(s + 1) * ws] = _dot(a, w_ref[s]) + b_ref[:, s * ws:(s + 1) * ws]

    return pl.pallas_call(
        body, name="ada_mod", out_shape=jax.ShapeDtypeStruct((8, 3 * D), F32),
        compiler_params=_params())(c8, wada_f, b_ada)


def _dc_masks():
    cq = lax.broadcasted_iota(jnp.int32, (GRID_W, GRID_W), 0)
    ck = lax.broadcasted_iota(jnp.int32, (GRID_W, GRID_W), 1)
    dc = jnp.clip(ck - cq + 15, 0, 30)
    c0 = jnp.clip(cq - 8, 0, GRID_W - 16)
    col_ok = (ck >= c0) & (ck < c0 + 16)
    return dc, col_ok


def _bias_blocks():
    out = []
    for typ, delta in enumerate((4, 0, -4)):
        for rq in range(4):
            for rkk in range(12):
                dr = rkk + delta - rq - 4
                if typ == 0:
                    ok = -rq <= dr <= 7 - rq
                elif typ == 1:
                    ok = -4 <= dr <= 3
                else:
                    ok = -4 - rq <= dr <= 3 - rq
                out.append((typ, rq, rkk, dr if ok else None))
    return out


def _bias_body(r_ref, bias_ref, et_ref, out_ref, sem):
    dc, col_ok = _dc_masks()
    masks = [(dc == j).astype(F32) for j in range(31)]
    nh = bias_ref.shape[0]

    def per_h(h, carry):
        for dr in range(15):
            t = jnp.zeros((GRID_W, GRID_W), F32)
            for j in range(31):
                t = t + masks[j] * r_ref[h, dr * 31 + j]
            et_ref[dr] = jnp.where(col_ok, t, NEG)
        neg = jnp.full((GRID_W, GRID_W), NEG, F32)
        for typ, rq, rkk, dr in _bias_blocks():
            blk = neg if dr is None else et_ref[dr + 7]
            bias_ref[h, typ, rq * 64:(rq + 1) * 64, rkk * 64:(rkk + 1) * 64] = blk
        pltpu.make_async_copy(bias_ref.at[h], out_ref.at[h], sem).start()
        return carry

    lax.fori_loop(0, nh, per_h, 0)
    return [pltpu.make_async_copy(bias_ref.at[h], out_ref.at[h], sem) for h in range(nh)]


def _bias_tile_sums(db_ref, hh):
    acc = {}
    for typ, rq, rkk, dr in _bias_blocks():
        if dr is None:
            continue
        blk = db_ref[hh, typ, rq * 64:(rq + 1) * 64, rkk * 64:(rkk + 1) * 64]
        acc[dr] = blk if dr not in acc else acc[dr] + blk
    return acc


def _small_reduce_body(dt_ref, dlg_ref, drpb_ref, dlgo_ref, p_ref):
    dc, _ = _dc_masks()
    masks = [(dc == j).astype(F32) for j in range(31)]
    ones = jnp.ones((8, GRID_W), F32)
    p_ref[...] = jnp.zeros_like(p_ref)
    drpb_ref[...] = jnp.zeros_like(drpb_ref)

    def per_h(h, carry):
        for dr in range(-7, 8):
            t = dt_ref[h, dr + 7]
            for j in range(31):
                p_ref[j:j + 1, :] = jnp.sum(t * masks[j], axis=0, keepdims=True)
            red = lax.dot_general(ones, p_ref[...], (((1,), (1,)), ((), ())),
                                  precision=HIGHEST, preferred_element_type=F32)
            drpb_ref[h, dr + 7:dr + 8, :] = red[0:1, :]
        return carry

    lax.fori_loop(0, dt_ref.shape[0], per_h, 0)
    x = dlg_ref[0]
    for b in range(1, dlg_ref.shape[0]):
        x = x + dlg_ref[b]
    x = x.reshape(4 * 8, x.shape[-1])
    dlgo_ref[...] = jnp.dot(x, jnp.ones((x.shape[-1], 128), F32), precision=HIGHEST,
                            preferred_element_type=F32)


def _inproj_gather_call(order, x, ctx, mod, norm_g, win_b, cos2, sin2):
    B, L, _ = x.shape
    LC = ctx.shape[1]
    T = L + LC
    TI = 2 * TQ
    nl = L // TI
    nt = nl + 1
    assert LC == TQ and L % TI == 0
    kscale = RET_DK ** -0.5
    HR = D // 2
    pad_rows = nt * TI - T
    cos2 = jnp.pad(cos2, ((0, pad_rows), (0, 0)))
    sin2 = jnp.pad(sin2, ((0, pad_rows), (0, 0)))

    def body(ord_ref, x_ref, ctx_ref, mod_ref, g_ref, wown_ref, cos_ref, sin_ref, p_ref, h_ref, wf_ref,
             w_all, hs_ref, ssem, rsem, lsem):
        j, b, t = pl.program_id(0), pl.program_id(1), pl.program_id(2)
        first = (b == 0) & (t == 0)
        mx, my, mc = _mesh_pos()
        s = 2 * mx + my
        sib = (mx, my, 1 - mc)
        own = pltpu.make_async_copy(wown_ref, w_all.at[s], lsem.at[0])
        ici_send, ici_recv, fwd_send, fwd_recv, outs = [], [], [], [], [
            pltpu.make_async_copy(w_all.at[s], wf_ref.at[s], lsem.at[1])]
        for k, (px, py) in enumerate(_other_chips(mx, my)):
            ps = 2 * px + py
            mine = w_all.at[s, pl.ds(mc * HR, HR)]
            ici_send.append(_remote(mine, w_all.at[s, pl.ds(mc * HR, HR)], ssem, rsem, k, (px, py, mc)))
            ici_recv.append(_remote(mine, w_all.at[ps, pl.ds(mc * HR, HR)], ssem, rsem, k, (px, py, mc)))
            got = w_all.at[ps, pl.ds(mc * HR, HR)]
            fwd_send.append(_remote(got, got, ssem, rsem, 3 + k, sib))
            theirs = w_all.at[ps, pl.ds((1 - mc) * HR, HR)]
            fwd_recv.append(_remote(theirs, theirs, ssem, rsem, 3 + k, sib))
            outs.append(pltpu.make_async_copy(w_all.at[ps], wf_ref.at[ps], lsem.at[2 + k]))

        @pl.when(first & (j == 0))
        def _():
            own.start()
            own.wait()
            ici_send[0].start()
            ici_send[1].start()
            outs[0].start()

        for k in range(3):
            @pl.when(first & (j == k + 1))
            def _(k=k):
                ici_recv[k].wait_recv()
                if k == 0:
                    ici_send[2].start()
                fwd_send[k].start()
                fwd_recv[k].wait_recv()
                outs[1 + k].start()

        tile = b * nt + t

        @pl.when(j == 0)
        def _():
            is_lat = t < nl
            ctx_tile = jnp.concatenate([ctx_ref[...], jnp.zeros((TI - LC, D), F32)], axis=0)
            xt = jnp.where(is_lat, x_ref[...], ctx_tile)
            mrow = mod_ref[pl.ds(jnp.where(is_lat, b, B), 1), :]
            shift, scale = mrow[:, 0:D], mrow[:, D:2 * D]
            rstd = lax.rsqrt(jnp.mean(xt * xt, axis=-1, keepdims=True) + EPS)
            h0 = ((xt * rstd * g_ref[...]) * (1.0 + scale) + shift).astype(BF16)
            h_ref[...] = h0
            hs_ref[tile] = h0

        hb = hs_ref[tile]
        cs, sn = cos_ref[...], sin_ref[...]
        shard = ord_ref[j]
        for sh in range(N_SHARD):
            @pl.when(shard == sh)
            def _(sh=sh):
                for half in range(2):
                    sec = 2 * sh + half
                    acc = _dot(hb, w_all[sh, :, half * 512:(half + 1) * 512])
                    if sec == 0:
                        acc = acc * (NA_DH ** -0.5)
                    if sec in (4, 5):
                        for q in range(4):
                            a = acc[:, q * 128:(q + 1) * 128]
                            r = a * cs + pltpu.roll(a, 64, 1) * sn
                            if sec == 5:
                                r = r * kscale
                            p_ref[:, half * 512 + q * 128:half * 512 + (q + 1) * 128] = r.astype(BF16)
                    else:
                        p_ref[:, half * 512:(half + 1) * 512] = acc.astype(BF16)

        @pl.when((j == N_SHARD - 1) & (b == B - 1) & (t == nt - 1))
        def _():
            _finish(outs, ici_send + fwd_send, [])

    tok = lambda j, b, t, o: (jnp.where(j == 0, b, B - 1), jnp.where(j == 0, jnp.minimum(t, nl - 1), nl - 1), 0)
    grid_spec = pltpu.PrefetchScalarGridSpec(
        num_scalar_prefetch=1, grid=(N_SHARD, B, nt),
        in_specs=[
            pl.BlockSpec((None, TI, D), tok),
            pl.BlockSpec((None, LC, D), lambda j, b, t, o: (jnp.where(j == 0, b, B - 1), 0, 0)),
            pl.BlockSpec((8, 3 * D), lambda j, b, t, o: (0, 0)),
            pl.BlockSpec((1, D), lambda j, b, t, o: (0, 0)),
            ANY,
            pl.BlockSpec((TI, RET_DK), lambda j, b, t, o: (t, 0)),
            pl.BlockSpec((TI, RET_DK), lambda j, b, t, o: (t, 0)),
        ],
        out_specs=(pl.BlockSpec((None, TI, D), lambda j, b, t, o: (b, t, o[j])),
                   pl.BlockSpec((None, TI, D), lambda j, b, t, o: (
                       jnp.where(j == 0, b, B - 1), jnp.where(j == 0, t, nt - 1), 0)), ANY),
        scratch_shapes=[pltpu.VMEM((N_SHARD, D, D), BF16), pltpu.VMEM((B * nt, TI, D), BF16),
                        pltpu.SemaphoreType.DMA((6,)), pltpu.SemaphoreType.DMA((6,)),
                        pltpu.SemaphoreType.DMA((5,))])
    return pl.pallas_call(
        body, name="in_proj", grid_spec=grid_spec,
        out_shape=(jax.ShapeDtypeStruct((B, T, 4 * D), BF16), jax.ShapeDtypeStruct((B, T, D), BF16),
                   jax.ShapeDtypeStruct((N_SHARD, D, D), BF16)),
        compiler_params=_params(("arbitrary",) * 3, vmem_mb=56))(order, x, ctx, mod, norm_g, win_b, cos2, sin2)


def _na_specs(L, T, rows, nh=2):
    nm = rows // 4
    w = nh * NA_DH
    per = 512 // w
    q_spec = pl.BlockSpec((None, TQ, w), lambda hp, b, m: (b, m, hp))
    k_spec = pl.BlockSpec((None, T, w), lambda hp, b, m: (b, 0, per + hp))
    v_spec = pl.BlockSpec((None, T, w), lambda hp, b, m: (b, 0, 2 * per + hp))
    g_spec = pl.BlockSpec((None, TQ, w), lambda hp, b, m: (b, m, 3 * per + hp))
    bias_spec = pl.BlockSpec((nh, 3, TQ, KW), lambda hp, b, m: (hp, 0, 0, 0))
    return nm, q_spec, k_spec, v_spec, g_spec, bias_spec


def _na_tile(m, nm, rows):
    typ = jnp.where(m == 0, 0, jnp.where(m == nm - 1, 2, 1))
    start = pl.multiple_of(jnp.clip(4 * m - 4, 0, rows - 12) * GRID_W, TQ)
    return typ, start


def _na_fwd_call(P, bias, L, LC):
    B, T, _ = P.shape
    rows = L // GRID_W
    NH = 4
    nm, q_spec, k_spec, v_spec, g_spec, bias_spec = _na_specs(L, T, rows, NH)

    def body(q_ref, k_ref, v_ref, g_ref, bias_ref, y_ref, o_ref):
        typ, start = _na_tile(pl.program_id(2), nm, rows)
        for hh in range(NH):
            ln = slice(hh * NA_DH, (hh + 1) * NA_DH)
            q = q_ref[:, ln]
            kw, vw = k_ref[pl.ds(start, KW), ln], v_ref[pl.ds(start, KW), ln]
            kc, vc = k_ref[L:L + LC, ln], v_ref[L:L + LC, ln]
            s1 = _dot_nt(q, kw) + bias_ref[hh, typ]
            s2 = _dot_nt(q, kc)
            mx = jnp.maximum(jnp.max(s1, axis=-1, keepdims=True), jnp.max(s2, axis=-1, keepdims=True))
            p1, p2 = jnp.exp(s1 - mx), jnp.exp(s2 - mx)
            inv = 1.0 / (jnp.sum(p1, axis=-1, keepdims=True) + jnp.sum(p2, axis=-1, keepdims=True))
            o = (_dot(p1.astype(BF16), vw) + _dot(p2.astype(BF16), vc)) * inv
            g = g_ref[:, ln].astype(F32)
            o_ref[:, ln] = o.astype(BF16)
            y_ref[:, ln] = (o * (g * _sigmoid(g))).astype(BF16)

    tile = pl.BlockSpec((None, TQ, NH * NA_DH), lambda hp, b, m: (b, m, hp))
    return pl.pallas_call(
        body, name="na_fwd", grid=(8 // NH, B, nm),
        in_specs=[q_spec, k_spec, v_spec, g_spec, bias_spec],
        out_specs=(tile, tile),
        out_shape=(jax.ShapeDtypeStruct((B, L, 512), BF16),) * 2,
        compiler_params=_params(("arbitrary",) * 3))(P, P, P, P, bias)


def _na_bwd_call(P, bias, dY, o_na, L, LC):
    B, T, _ = P.shape
    rows = L // GRID_W
    NH = 4
    W = NH * NA_DH
    nm, q_spec, k_spec, v_spec, g_spec, bias_spec = _na_specs(L, T, rows, NH)
    scale = NA_DH ** -0.5

    RB = 32

    def body(q_ref, k_ref, v_ref, g_ref, bias_ref, dy_ref, o_ref, dq_ref, dg_ref, dk_ref, dv_ref, dt_ref,
             db_ref, s1_ref, s2_ref, dp1_ref, dp2_ref, p1_ref, p2_ref, ds1_ref, ds2_ref, dkt_ref, dvt_ref):
        b, m = pl.program_id(1), pl.program_id(2)
        typ, start = _na_tile(m, nm, rows)

        @pl.when(m == 0)
        def _():
            dkt_ref[...] = jnp.zeros_like(dkt_ref)
            dvt_ref[...] = jnp.zeros_like(dvt_ref)

        @pl.when((m == 0) & (b == 0))
        def _():
            db_ref[...] = jnp.zeros_like(db_ref)

        for hh in range(NH):
            ln = slice(hh * NA_DH, (hh + 1) * NA_DH)
            q = q_ref[:, ln]
            kw, vw = k_ref[pl.ds(start, KW), ln], v_ref[pl.ds(start, KW), ln]
            kc, vc = k_ref[L:L + LC, ln], v_ref[L:L + LC, ln]
            g = g_ref[:, ln].astype(F32)
            sg = _sigmoid(g)
            dy = dy_ref[:, ln].astype(F32)
            do = (dy * (g * sg)).astype(BF16)
            s1_ref[hh] = _dot_nt(q, kw)
            s2_ref[hh] = _dot_nt(q, kc)
            dp1_ref[hh] = _dot_nt(do, vw)
            dp2_ref[hh] = _dot_nt(do, vc)

            def rows_pass(r, carry, hh=hh):
                rw = pl.ds(pl.multiple_of(r * RB, RB), RB)
                a = s1_ref[hh, rw, :] + bias_ref[hh, typ, rw, :]
                c = s2_ref[hh, rw, :]
                mx = jnp.maximum(jnp.max(a, axis=-1, keepdims=True), jnp.max(c, axis=-1, keepdims=True))
                e1, e2 = jnp.exp(a - mx), jnp.exp(c - mx)
                inv = 1.0 / (jnp.sum(e1, axis=-1, keepdims=True) + jnp.sum(e2, axis=-1, keepdims=True))
                p1, p2 = e1 * inv, e2 * inv
                p1_ref[hh, rw, :] = p1.astype(BF16)
                p2_ref[hh, rw, :] = p2.astype(BF16)
                dp1, dp2 = dp1_ref[hh, rw, :], dp2_ref[hh, rw, :]
                delta = jnp.sum(p1 * dp1, axis=-1, keepdims=True) + jnp.sum(p2 * dp2, axis=-1, keepdims=True)
                ds1 = p1 * (dp1 - delta)
                db_ref[hh, typ, rw, :] += ds1
                ds1_ref[hh, rw, :] = ds1.astype(BF16)
                ds2_ref[hh, rw, :] = (p2 * (dp2 - delta)).astype(BF16)
                return carry

            lax.fori_loop(0, TQ // RB, rows_pass, 0, unroll=True)
            p1b, p2b, ds1b, ds2b = p1_ref[hh], p2_ref[hh], ds1_ref[hh], ds2_ref[hh]
            dg_ref[:, ln] = (dy * o_ref[:, ln].astype(F32) * (sg * (1.0 + g * (1.0 - sg)))).astype(BF16)
            dq_ref[:, ln] = ((_dot(ds1b, kw) + _dot(ds2b, kc)) * scale).astype(BF16)
            dkt_ref[ln, pl.ds(start, KW)] += _dot_tn(q, ds1b)
            dvt_ref[ln, pl.ds(start, KW)] += _dot_tn(do, p1b)
            dkt_ref[ln, L:L + LC] += _dot_tn(q, ds2b)
            dvt_ref[ln, L:L + LC] += _dot_tn(do, p2b)

        @pl.when(m == nm - 1)
        def _():
            dk_ref[...] = dkt_ref[...].T
            dv_ref[...] = dvt_ref[...].T

        @pl.when((m == nm - 1) & (b == B - 1))
        def _():
            for hh in range(NH):
                for dr, t in _bias_tile_sums(db_ref, hh).items():
                    dt_ref[hh, dr + 7] = t

    tile = pl.BlockSpec((None, TQ, W), lambda hp, b, m: (b, m, hp))
    kv_out = pl.BlockSpec((None, T, W), lambda hp, b, m: (b, 0, hp))
    wide, narrow = (NH, TQ, KW), (NH, TQ, LC)
    return pl.pallas_call(
        body, name="na_bwd", grid=(8 // NH, B, nm),
        in_specs=[q_spec, k_spec, v_spec, g_spec, bias_spec, tile, tile],
        out_specs=(tile, tile, kv_out, kv_out,
                   pl.BlockSpec((NH, 15, GRID_W, GRID_W), lambda hp, b, m: (hp, 0, 0, 0))),
        out_shape=(jax.ShapeDtypeStruct((B, L, 512), BF16), jax.ShapeDtypeStruct((B, L, 512), BF16),
                   jax.ShapeDtypeStruct((B, T, 512), F32), jax.ShapeDtypeStruct((B, T, 512), F32),
                   jax.ShapeDtypeStruct((bias.shape[0], 15, GRID_W, GRID_W), F32)),
        scratch_shapes=[pltpu.VMEM((NH,) + bias.shape[1:], F32),
                        pltpu.VMEM(wide, F32), pltpu.VMEM(narrow, F32), pltpu.VMEM(wide, F32), pltpu.VMEM(narrow, F32),
                        pltpu.VMEM(wide, BF16), pltpu.VMEM(narrow, BF16), pltpu.VMEM(wide, BF16),
                        pltpu.VMEM(narrow, BF16), pltpu.VMEM((W, T), F32), pltpu.VMEM((W, T), F32)],
        compiler_params=_params(("arbitrary",) * 3, vmem_mb=60))(P, P, P, P, bias, dY, o_na)


def _head_scalar(dec_ref, h):
    lane = lax.broadcasted_iota(jnp.int32, dec_ref.shape, 1)
    return -jnp.sum(jnp.where(lane == h, jnp.exp(dec_ref[...]), 0.0), axis=1, keepdims=True)


def _chunk_decay(lgf, lgb):
    tau = lax.broadcasted_iota(jnp.int32, (TQ, 1), 0).astype(F32)
    sig = lax.broadcasted_iota(jnp.int32, (1, TQ), 1).astype(F32)
    dist = tau - sig
    dm = jnp.exp(dist * jnp.where(dist > 0, lgf, -lgb)) * jnp.where(dist == 0, 2.0, 1.0)
    return tau, dist, dm


def _ret_states_call(P, dec_f, dec_b, L, LC):
    B, T, _ = P.shape
    n = L // TQ

    def body(df_ref, db_ref, k_ref, v_ref, sf_ref, sb_ref):
        h = pl.program_id(1)
        lgf, lgb = _head_scalar(df_ref, h), _head_scalar(db_ref, h)
        tau = lax.broadcasted_iota(jnp.int32, (TQ, 1), 0).astype(F32)
        jc = lax.broadcasted_iota(jnp.int32, (LC, 1), 0).astype(F32)
        wf, wb = jnp.exp(lgf * (TQ - 1.0 - tau)), jnp.exp(lgb * tau)
        gcf, gcb = jnp.exp(lgf * float(TQ)), jnp.exp(lgb * float(TQ))
        kc, vc = k_ref[L:L + LC, :].astype(F32), v_ref[L:L + LC, :]

        def chunk_state(i, w):
            ks = pl.multiple_of(i * TQ, TQ)
            return _dot_tn((k_ref[pl.ds(ks, TQ), :].astype(F32) * w).astype(BF16), v_ref[pl.ds(ks, TQ), :])

        def fwd(i, s):
            sf_ref[i] = s
            return gcf * s + chunk_state(i, wf)

        lax.fori_loop(0, n, fwd, _dot_tn((kc * jnp.exp(lgf * (LC - 1.0 - jc))).astype(BF16), vc), unroll=True)

        def bwd(r, s):
            i = n - 1 - r
            sb_ref[i] = s
            return gcb * s + chunk_state(i, wb)

        lax.fori_loop(0, n, bwd, _dot_tn((kc * jnp.exp(lgb * jc)).astype(BF16), vc), unroll=True)

    st = pl.BlockSpec((None, None, n, RET_DK, RET_DK), lambda b, h: (b, h, 0, 0, 0))
    return pl.pallas_call(
        body, name="ret_states", grid=(B, 4),
        in_specs=[pl.BlockSpec((1, 4), lambda b, h: (0, 0)), pl.BlockSpec((1, 4), lambda b, h: (0, 0)),
                  pl.BlockSpec((None, T, 128), lambda b, h: (b, 0, 20 + h)),
                  pl.BlockSpec((None, T, 128), lambda b, h: (b, 0, 24 + h))],
        out_specs=(st, st),
        out_shape=(jax.ShapeDtypeStruct((B, 4, n, RET_DK, RET_DK), F32),) * 2,
        compiler_params=_params(("arbitrary",) * 2))(dec_f, dec_b, P, P)


def _retc_fwd_call(P, sf, sb, dec_f, dec_b, ret_norm_g, L):
    B, T, _ = P.shape
    sec = lambda k: pl.BlockSpec((None, TQ, 512), lambda b, i: (b, i, k))
    dec_spec = pl.BlockSpec((1, 4), lambda b, i: (0, 0))
    st_spec = pl.BlockSpec((None, 4, None, RET_DK, RET_DK), lambda b, i: (b, 0, i, 0, 0))

    def body(df_ref, db_ref, q_ref, k_ref, v_ref, g_ref, gn_ref, sf_ref, sb_ref, y_ref, o_ref):
        for h in range(4):
            ln = slice(h * RET_DK, (h + 1) * RET_DK)
            lgf, lgb = _head_scalar(df_ref, h), _head_scalar(db_ref, h)
            tau, _, dm = _chunk_decay(lgf, lgb)
            q = q_ref[:, ln]
            qf = q.astype(F32)
            acc = _dot((_dot_nt(q, k_ref[:, ln]) * dm).astype(BF16), v_ref[:, ln])
            acc = acc + _dot((qf * jnp.exp(lgf * (tau + 1.0))).astype(BF16), sf_ref[h].astype(BF16))
            acc = acc + _dot((qf * jnp.exp(lgb * (TQ - tau))).astype(BF16), sb_ref[h].astype(BF16))
            o_ref[:, ln] = acc
            rn = lax.rsqrt(jnp.mean(acc * acc, axis=-1, keepdims=True) + EPS)
            g = g_ref[:, ln].astype(F32)
            y_ref[:, ln] = ((acc * rn * gn_ref[:, ln]) * (g * _sigmoid(g))).astype(BF16)

    tile = pl.BlockSpec((None, TQ, 512), lambda b, i: (b, i, 0))
    return pl.pallas_call(
        body, name="ret_fwd", grid=(B, L // TQ),
        in_specs=[dec_spec, dec_spec, sec(4), sec(5), sec(6), sec(7),
                  pl.BlockSpec((1, 512), lambda b, i: (0, 0)), st_spec, st_spec],
        out_specs=(tile, tile),
        out_shape=(jax.ShapeDtypeStruct((B, L, 512), BF16), jax.ShapeDtypeStruct((B, L, 512), F32)),
        compiler_params=_params(("arbitrary",) * 2))(dec_f, dec_b, P, P, P, P, ret_norm_g, sf, sb)


def _retc_bwd_call(P, sf, sb, dec_f, dec_b, ret_norm_g, o_ret, dY, cos2, sin2, L, LC):
    B, T, _ = P.shape
    n = L // TQ
    C = float(TQ)
    kscale = RET_DK ** -0.5
    st_spec = pl.BlockSpec((None, 4, n, RET_DK, RET_DK), lambda b, i: (b, 0, 0, 0, 0))

    def body(df_ref, db_ref, q_ref, k_ref, v_ref, g_ref, gn_ref, o_ref, dy_ref, cos_ref, sin_ref, sf_ref, sb_ref,
             dq_ref, dg_ref, dk_ref, dv_ref, dgn_ref, dlg_ref, dsf_ref, dsb_ref):
        i = pl.program_id(1)

        @pl.when(i == 0)
        def _():
            dk_ref[...] = jnp.zeros_like(dk_ref)
            dv_ref[...] = jnp.zeros_like(dv_ref)
            dgn_ref[...] = jnp.zeros_like(dgn_ref)
            dlg_ref[...] = jnp.zeros_like(dlg_ref)

        rows = pl.ds(pl.multiple_of(i * TQ, TQ), TQ)
        cs, sn = cos_ref[rows, :], sin_ref[rows, :]

        def one_head(h):
            ln = slice(h * RET_DK, (h + 1) * RET_DK)
            lgf, lgb = _head_scalar(df_ref, h), _head_scalar(db_ref, h)
            tau, dist, dm = _chunk_decay(lgf, lgb)

            def add_lg(row, x):
                csum = jnp.sum(x, axis=0, keepdims=True)
                tot = csum[:, 0:128]
                for part in range(1, x.shape[1] // 128):
                    tot = tot + csum[:, part * 128:(part + 1) * 128]
                dlg_ref[h, row:row + 1, :] += tot

            q = q_ref[:, ln]
            qf = q.astype(F32)
            o = o_ref[:, ln]
            g = g_ref[:, ln].astype(F32)
            dy = dy_ref[:, ln].astype(F32)
            gn = gn_ref[:, ln]
            sg = _sigmoid(g)
            rn = lax.rsqrt(jnp.mean(o * o, axis=-1, keepdims=True) + EPS)
            nrm = o * rn
            dg_ref[:, ln] = (dy * (nrm * gn) * (sg * (1.0 + g * (1.0 - sg)))).astype(BF16)
            dhn = dy * (g * sg)
            dgn_ref[:, ln] += jnp.sum(dhn * nrm, axis=0, keepdims=True)
            dnrm = dhn * gn
            do = rn * (dnrm - nrm * jnp.mean(dnrm * nrm, axis=-1, keepdims=True))
            dob = do.astype(BF16)
            ki, vi = k_ref[rows, ln], v_ref[rows, ln]
            s = _dot_nt(q, ki)
            dsv = _dot_nt(dob, vi)
            dsb = (dsv * dm).astype(BF16)
            dk_ref[rows, ln] += _dot_tn(dsb, q)
            dv_ref[rows, ln] += _dot_tn((s * dm).astype(BF16), dob)
            xw = s * dsv * dm * jnp.abs(dist)
            fpart = jnp.where(dist > 0, xw, 0.0)
            add_lg(0, fpart)
            add_lg(1, xw - fpart)
            dq = _dot(dsb, ki)
            af, ab = jnp.exp(lgf * (tau + 1.0)), jnp.exp(lgb * (C - tau))
            qa, qb = (qf * af).astype(BF16), (qf * ab).astype(BF16)
            sfi, sbi = sf_ref[h, i].astype(BF16), sb_ref[h, i].astype(BF16)
            dq = dq + af * _dot_nt(dob, sfi) + ab * _dot_nt(dob, sbi)
            dsf_ref[h, i] = _dot_tn(qa, dob)
            dsb_ref[h, i] = _dot_tn(qb, dob)
            add_lg(0, (tau + 1.0) * (_dot(qa, sfi) * do))
            add_lg(1, (C - tau) * (_dot(qb, sbi) * do))
            dq_ref[:, ln] = (dq * cs - pltpu.roll(dq, 64, 1) * sn).astype(BF16)

            @pl.when(i == n - 1)
            def _():
                jc = lax.broadcasted_iota(jnp.int32, (LC, 1), 0).astype(F32)
                crow = pl.ds(L, LC)

                def through_state(rws, w, dw, gst, row):
                    kk, vv = k_ref[rws, ln].astype(F32), v_ref[rws, ln]
                    gb = gst.astype(BF16)
                    vg = _dot_nt(vv, gb)
                    kw = kk * w
                    dk_ref[rws, ln] += w * vg
                    dv_ref[rws, ln] += _dot(kw.astype(BF16), gb)
                    add_lg(row, dw * (kw * vg))

                def scan(gc, w, dw, st_ref, dst_ref, order, row):
                    def step(r, gst):
                        j = order(r)
                        through_state(pl.ds(pl.multiple_of(j * TQ, TQ), TQ), w, dw, gst, row)
                        add_lg(row, (C * gc) * (gst * st_ref[h, j]))
                        return dst_ref[h, j] + gc * gst
                    return lax.fori_loop(0, n, step, jnp.zeros((RET_DK, RET_DK), F32), unroll=True)

                gcf, gcb = jnp.exp(lgf * C), jnp.exp(lgb * C)
                g0 = scan(gcf, jnp.exp(lgf * (C - 1.0 - tau)), C - 1.0 - tau, sf_ref, dsf_ref,
                          lambda r: n - 1 - r, 0)
                through_state(crow, jnp.exp(lgf * (LC - 1.0 - jc)), LC - 1.0 - jc, g0, 0)
                g1 = scan(gcb, jnp.exp(lgb * tau), tau, sb_ref, dsb_ref, lambda r: r, 1)
                through_state(crow, jnp.exp(lgb * jc), jc, g1, 1)
                dk = dk_ref[:, ln]
                dk_ref[:, ln] = (dk * cos_ref[...] - pltpu.roll(dk, 64, 1) * sin_ref[...]) * kscale

        for h in range(4):
            one_head(h)

    sec = lambda k: pl.BlockSpec((None, TQ, 512), lambda b, i: (b, i, k))
    full = lambda k: pl.BlockSpec((None, T, 512), lambda b, i: (b, 0, k))
    dec_spec = pl.BlockSpec((1, 4), lambda b, i: (0, 0))
    tab = pl.BlockSpec((T, RET_DK), lambda b, i: (0, 0))
    return pl.pallas_call(
        body, name="ret_bwd", grid=(B, n),
        in_specs=[dec_spec, dec_spec, sec(4), full(5), full(6), sec(7),
                  pl.BlockSpec((1, 512), lambda b, i: (0, 0)), sec(0), sec(1), tab, tab, st_spec, st_spec],
        out_specs=(sec(0), sec(0), full(0), full(0),
                   pl.BlockSpec((None, 1, 512), lambda b, i: (b, 0, 0)),
                   pl.BlockSpec((None, 4, 8, 128), lambda b, i: (b, 0, 0, 0))),
        out_shape=(jax.ShapeDtypeStruct((B, L, 512), BF16), jax.ShapeDtypeStruct((B, L, 512), BF16),
                   jax.ShapeDtypeStruct((B, T, 512), F32), jax.ShapeDtypeStruct((B, T, 512), F32),
                   jax.ShapeDtypeStruct((B, 1, 512), F32), jax.ShapeDtypeStruct((B, 4, 8, 128), F32)),
        scratch_shapes=[pltpu.VMEM((4, n, RET_DK, RET_DK), F32), pltpu.VMEM((4, n, RET_DK, RET_DK), F32)],
        compiler_params=_params(("arbitrary",) * 2, vmem_mb=56))(
            dec_f, dec_b, P, P, P, P, ret_norm_g, o_ret, dY, cos2, sin2, sf, sb)


def _out_call(y_na, y_ret, x, target, mod, final_g, wout_f):
    B, L, _ = x.shape
    TO = 2 * TQ

    def body(yn_ref, yr_ref, x_ref, t_ref, mod_ref, gf_ref, w_ref, dy_ref, dx2_ref, dwb_ref, sm_ref, dw_ref):
        b, i = pl.program_id(0), pl.program_id(1)

        @pl.when((b == 0) & (i == 0))
        def _():
            dw_ref[...] = jnp.zeros_like(dw_ref)
            sm_ref[...] = jnp.zeros_like(sm_ref)

        gate = mod_ref[pl.ds(b, 1), 2 * D:3 * D]
        gf = gf_ref[...]
        yn, yr = yn_ref[...], yr_ref[...]
        ylat = _dot(yn, w_ref[0:512, :]) + _dot(yr, w_ref[512:1024, :])
        x2 = x_ref[...] + gate * ylat
        r = lax.rsqrt(jnp.mean(x2 * x2, axis=-1, keepdims=True) + EPS)
        xr = x2 * r
        err = xr * gf - t_ref[...]
        sm_ref[1:2, :] += jnp.sum(err * err, axis=0, keepdims=True)
        dout = err * (1.0 / D)
        sm_ref[0:1, :] += jnp.sum(dout * xr, axis=0, keepdims=True)
        gd = dout * gf
        dx2 = r * (gd - xr * jnp.mean(gd * xr, axis=-1, keepdims=True))
        dx2_ref[...] = dx2
        sm_ref[pl.ds(2 + b, 1), :] += jnp.sum(dx2 * ylat, axis=0, keepdims=True)
        dyl = (gate * dx2).astype(BF16)
        dy_ref[:, 0:512] = _dot_nt(dyl, w_ref[0:512, :]).astype(BF16)
        dy_ref[:, 512:1024] = _dot_nt(dyl, w_ref[512:1024, :]).astype(BF16)
        dw_ref[0:512, :] += _dot_tn(yn, dyl)
        dw_ref[512:1024, :] += _dot_tn(yr, dyl)

        @pl.when((b == B - 1) & (i == L // TO - 1))
        def _():
            dwb_ref[...] = dw_ref[...].astype(BF16)

    half = pl.BlockSpec((None, TO, 512), lambda b, i: (b, i, 0))
    full = pl.BlockSpec((None, TO, D), lambda b, i: (b, i, 0))
    return pl.pallas_call(
        body, name="out_proj_loss", grid=(B, L // TO),
        in_specs=[half, half, full, full,
                  pl.BlockSpec((8, 3 * D), lambda b, i: (0, 0)),
                  pl.BlockSpec((1, D), lambda b, i: (0, 0)),
                  pl.BlockSpec((D, D), lambda b, i: (0, 0))],
        out_specs=(full, full, pl.BlockSpec((D, D), lambda b, i: (0, 0)),
                   pl.BlockSpec((8, D), lambda b, i: (0, 0))),
        out_shape=(jax.ShapeDtypeStruct((B, L, D), BF16), jax.ShapeDtypeStruct((B, L, D), F32),
                   jax.ShapeDtypeStruct((D, D), BF16), jax.ShapeDtypeStruct((8, D), F32)),
        scratch_shapes=[pltpu.VMEM((D, D), F32)],
        compiler_params=_params(("arbitrary",) * 2))(y_na, y_ret, x, target, mod, final_g, wout_f)


def _dh_call(dsec, win_f, x, ctx, dx2, mod, norm_g, cp_in, cp_out):
    B, L, _ = x.shape
    LC = ctx.shape[1]
    nl = L // TQ

    def body(d0, d1, d2, d3, d4, d5, d6, d7, w_ref, x_ref, ctx_ref, dx2_ref, mod_ref, g_ref, cpi_ref, cpo_ref,
             gx_ref, sm_ref, sli_ref, slo_ref, ssem, rsem, lsem):
        drefs = (d0, d1, d2, d3, d4, d5, d6, d7)
        b, t = pl.program_id(0), pl.program_id(1)
        is_lat = t < nl

        @pl.when((b == 0) & (t == 0))
        def _():
            sm_ref[...] = jnp.zeros_like(sm_ref)

        def dh_of(secs):
            acc = jnp.zeros((TQ, D), F32)
            for sec in secs:
                s, half = divmod(sec, 2)
                acc = acc + _dot_nt(drefs[sec][...].astype(BF16), w_ref[s, :, half * 512:(half + 1) * 512])
            return acc

        def norm_bwd(dh, xt, mrow):
            scale = mrow[:, D:2 * D]
            g = g_ref[...]
            rstd = lax.rsqrt(jnp.mean(xt * xt, axis=-1, keepdims=True) + EPS)
            xn = xt * rstd
            dshift = jnp.sum(dh, axis=0, keepdims=True)
            dscale = jnp.sum(dh * (xn * g), axis=0, keepdims=True)
            dhn = dh * (1.0 + scale)
            sm_ref[0:1, :] += jnp.sum(dhn * xn, axis=0, keepdims=True)
            dxn = dhn * g
            dx = rstd * (dxn - xn * jnp.mean(dxn * xn, axis=-1, keepdims=True))
            return dshift, dscale, dx

        @pl.when(is_lat)
        def _():
            dshift, dscale, dx = norm_bwd(dh_of(range(8)), x_ref[...], mod_ref[pl.ds(b, 1), :])
            sm_ref[pl.ds(3 + b, 1), :] += dshift
            sm_ref[pl.ds(3 + B + b, 1), :] += dscale
            gx_ref[...] = dx2_ref[...] + dx

        @pl.when(jnp.logical_not(is_lat))
        def _():
            dshift, dscale, _ = norm_bwd(dh_of((1, 2, 5, 6)), ctx_ref[...], mod_ref[B:B + 1, :])
            sm_ref[1:2, :] += dshift
            sm_ref[2:3, :] += dscale

        mx, my, mc = _mesh_pos()
        s = 2 * mx + my
        cps, sls = (cpi_ref, cpo_ref), (sli_ref, slo_ref)
        own = [pltpu.make_async_copy(cps[a].at[s], sls[a].at[s], lsem.at[a]) for a in range(2)]
        sends, recvs, k = [], [], 0
        for px, py in _other_chips(mx, my):
            ps = 2 * px + py
            for a in range(2):
                sends.append(_remote(cps[a].at[ps], sls[a].at[s], ssem, rsem, k, (px, py, mc)))
                recvs.append(_remote(cps[a].at[s], sls[a].at[ps], ssem, rsem, k, (px, py, mc)))
                k += 1

        @pl.when((b == 0) & (t == 0))
        def _():
            for cp in own + sends:
                cp.start()

        @pl.when((b == B - 1) & (t == nl))
        def _():
            _finish(own, sends, recvs)

    lat = lambda b, t: (b, jnp.minimum(t, nl - 1), 0)
    tok = lambda b, t: (b, t, 0)
    sec_specs = [pl.BlockSpec((None, TQ, 512), lat if sec in (0, 3, 4, 7) else tok) for sec in range(8)]
    return pl.pallas_call(
        body, name="dh_norm_bwd", grid=(B, nl + 1),
        in_specs=sec_specs + [
            pl.BlockSpec((N_SHARD, D, D), lambda b, t: (0, 0, 0)),
            pl.BlockSpec((None, TQ, D), lat),
            pl.BlockSpec((None, LC, D), lambda b, t: (b, 0, 0)),
            pl.BlockSpec((None, TQ, D), lat),
            pl.BlockSpec((8, 3 * D), lambda b, t: (0, 0)),
            pl.BlockSpec((1, D), lambda b, t: (0, 0)), ANY, ANY],
        out_specs=(pl.BlockSpec((None, TQ, D), lat), pl.BlockSpec((8, D), lambda b, t: (0, 0)), ANY, ANY),
        out_shape=(jax.ShapeDtypeStruct((B, L, D), F32), jax.ShapeDtypeStruct((8, D), F32),
                   jax.ShapeDtypeStruct(cp_in.shape, cp_in.dtype), jax.ShapeDtypeStruct(cp_out.shape, cp_out.dtype)),
        scratch_shapes=[pltpu.SemaphoreType.DMA((6,)), pltpu.SemaphoreType.DMA((6,)),
                        pltpu.SemaphoreType.DMA((2,))],
        compiler_params=_params(("arbitrary",) * 2))(*dsec, win_f, x, ctx, dx2, mod, norm_g, cp_in, cp_out)


def _dw_call(dsec, h, L):
    B, T, _ = h.shape
    TW = 2 * TQ
    nl = L // TW
    KV = (1, 2, 5, 6)

    def body(d0, d1, d2, d3, d4, d5, d6, d7, c1, c2, c5, c6, h_ref, hc_ref, dw_ref, acc_ref):
        drefs = (d0, d1, d2, d3, d4, d5, d6, d7)
        crefs = dict(zip(KV, (c1, c2, c5, c6)))
        b, t = pl.program_id(0), pl.program_id(1)

        @pl.when((b == 0) & (t == 0))
        def _():
            acc_ref[...] = jnp.zeros_like(acc_ref)

        def add(hb, refs, secs):
            for sec in secs:
                s, half = divmod(sec, 2)
                acc_ref[s, :, half * 512:(half + 1) * 512] += _dot_tn(hb, refs[sec][...].astype(BF16))

        @pl.when(t < nl)
        def _():
            add(h_ref[...], drefs, range(8))

        @pl.when(t == nl)
        def _():
            add(hc_ref[...], crefs, KV)

        @pl.when((b == B - 1) & (t == nl))
        def _():
            dw_ref[...] = acc_ref[...].astype(BF16)

    lat = lambda b, t: (b, jnp.minimum(t, nl - 1), 0)
    ctx = lambda b, t: (b, L // TQ, 0)
    return pl.pallas_call(
        body, name="dw_in", grid=(B, nl + 1),
        in_specs=[pl.BlockSpec((None, TW, 512), lat)] * 8 + [pl.BlockSpec((None, TQ, 512), ctx)] * 4
        + [pl.BlockSpec((None, TW, D), lat), pl.BlockSpec((None, TQ, D), ctx)],
        out_specs=pl.BlockSpec((N_SHARD, D, D), lambda b, t: (0, 0, 0)),
        out_shape=jax.ShapeDtypeStruct((N_SHARD, D, D), BF16),
        scratch_shapes=[pltpu.VMEM((N_SHARD, D, D), F32)],
        compiler_params=_params(("arbitrary",) * 2, vmem_mb=60))(*dsec, *[dsec[k] for k in KV], h, h)


def _mesh_pos():
    return lax.axis_index("x"), lax.axis_index("y"), lax.axis_index("c")


def _flip(v, f):
    return 1 - v if f else v


def _remote(src, dst, ssem, rsem, k, peer):
    return pltpu.make_async_remote_copy(src_ref=src, dst_ref=dst, send_sem=ssem.at[k], recv_sem=rsem.at[k],
                                        device_id=peer, device_id_type=MESH)


def _other_chips(x, y):
    return [(_flip(x, fx), _flip(y, fy)) for fx, fy in ((1, 0), (0, 1), (1, 1))]


def _all_to_all_small(src, dst_all, ssem, rsem, k0, x, y, cc):
    me = 4 * x + 2 * y + cc
    sends, recvs = [], []
    for f in range(1, N_DEV):
        px, py, pc = _flip(x, f & 4), _flip(y, f & 2), _flip(cc, f & 1)
        sends.append(_remote(src, dst_all.at[me], ssem, rsem, k0 + f - 1, (px, py, pc)))
        recvs.append(_remote(src, dst_all.at[4 * px + 2 * py + pc], ssem, rsem, k0 + f - 1, (px, py, pc)))
    return sends, recvs


def _finish(local, sends, recvs):
    for cp in recvs:
        cp.wait_recv()
    for cp in sends:
        cp.wait_send()
    for cp in local:
        cp.wait()


def _gather_call(wout_b, wada_b, c, rpb_flat):
    arrs = (wout_b, wada_b)
    na = len(arrs)
    hrs = [a.shape[0] // 2 for a in arrs]

    def body(wout, wada, c_ref, r_ref, wout_f, wada_f, c_all, bias_out, bias_ref, et_ref, ssem, rsem, lsem):
        x, y, cc = _mesh_pos()
        s, me = 2 * x + y, 4 * x + 2 * y + cc
        sib = (x, y, 1 - cc)
        srcs, dsts = (wout, wada), (wout_f, wada_f)

        def half(a, shard, hc):
            return dsts[a].at[shard, pl.ds(hc * hrs[a], hrs[a])]

        local = [pltpu.make_async_copy(srcs[a], dsts[a].at[s], lsem.at[a]) for a in range(na)]
        local.append(pltpu.make_async_copy(c_ref, c_all.at[me], lsem.at[na]))
        ici_send, ici_recv, fwd_send, fwd_recv, k = [], [], [], [], 0
        for px, py in _other_chips(x, y):
            ps = 2 * px + py
            for a in range(na):
                mine = srcs[a].at[pl.ds(cc * hrs[a], hrs[a])]
                ici_send.append(_remote(mine, half(a, s, cc), ssem, rsem, k, (px, py, cc)))
                ici_recv.append(_remote(mine, half(a, ps, cc), ssem, rsem, k, (px, py, cc)))
                fwd_send.append(_remote(half(a, ps, cc), half(a, ps, cc), ssem, rsem, 3 * na + k, sib))
                fwd_recv.append(_remote(half(a, ps, 1 - cc), half(a, ps, 1 - cc), ssem, rsem, 3 * na + k, sib))
                k += 1
        c_send, c_recv = _all_to_all_small(c_ref, c_all, ssem, rsem, 6 * na, x, y, cc)
        for cp in local + ici_send + c_send:
            cp.start()
        bias_out_copies = _bias_body(r_ref, bias_ref, et_ref, bias_out, lsem.at[na + 1])
        for got, fwd in zip(ici_recv, fwd_send):
            got.wait_recv()
            fwd.start()
        _finish(local + bias_out_copies, ici_send + fwd_send + c_send, fwd_recv + c_recv)

    bias_shape = (rpb_flat.shape[0], 3, TQ, KW)
    return pl.pallas_call(
        body, name="weight_gather",
        in_specs=[pl.BlockSpec(memory_space=pltpu.VMEM)] * 3 + [pl.BlockSpec(memory_space=pltpu.SMEM)],
        out_specs=(pl.BlockSpec(memory_space=pltpu.VMEM),) * 3 + (ANY,),
        out_shape=tuple(jax.ShapeDtypeStruct((N_SHARD,) + a.shape, a.dtype) for a in arrs)
        + (jax.ShapeDtypeStruct((N_DEV,) + c.shape, c.dtype), jax.ShapeDtypeStruct(bias_shape, F32)),
        scratch_shapes=[pltpu.VMEM(bias_shape, F32), pltpu.VMEM((15, GRID_W, GRID_W), F32),
                        pltpu.SemaphoreType.DMA((6 * na + 7,)), pltpu.SemaphoreType.DMA((6 * na + 7,)),
                        pltpu.SemaphoreType.DMA((na + 2,))],
        compiler_params=pltpu.CompilerParams(vmem_limit_bytes=56 << 20))(wout_b, wada_b, c, rpb_flat)


VROWS = 32


def _grad_halves_call(dwin_b, dwout_b, dbias, dlg):
    arrs = (dwin_b, dwout_b)
    hrs = [a.shape[1] // 2 for a in arrs]

    def body(din, dout, db_ref, dlg_ref, cp_in, cp_out, drpb_ref, dlgo_ref, got_in, got_out, p_ref, ssem, rsem):
        x, y, cc = _mesh_pos()
        sib = (x, y, 1 - cc)
        srcs, gots, cps = (din, dout), (got_in, got_out), (cp_in, cp_out)
        halves = [_remote(srcs[a].at[:, pl.ds((1 - cc) * hrs[a], hrs[a])], gots[a], ssem, rsem, a, sib)
                  for a in range(2)]
        for cp in halves:
            cp.start()
        _small_reduce_body(db_ref, dlg_ref, drpb_ref, dlgo_ref, p_ref)
        for cp in halves:
            cp.wait_recv()
        for a in range(2):
            for j in range(N_SHARD):
                def add(i, carry, a=a, j=j):
                    r = pl.multiple_of(i * VROWS, VROWS)
                    mine = srcs[a][j, pl.ds(pl.multiple_of(cc * hrs[a] + r, VROWS), VROWS), :].astype(F32)
                    cps[a][j, pl.ds(r, VROWS), :] = (
                        mine + gots[a][j, pl.ds(r, VROWS), :].astype(F32)).astype(BF16)
                    return carry
                lax.fori_loop(0, hrs[a] // VROWS, add, 0)
        for cp in halves:
            cp.wait_send()

    vmem = pl.BlockSpec(memory_space=pltpu.VMEM)
    half_shapes = [(N_SHARD, hrs[a], arrs[a].shape[2]) for a in range(2)]
    return pl.pallas_call(
        body, name="grad_halves",
        in_specs=[vmem] * 4, out_specs=(vmem,) * 4,
        out_shape=(jax.ShapeDtypeStruct(half_shapes[0], BF16), jax.ShapeDtypeStruct(half_shapes[1], BF16),
                   jax.ShapeDtypeStruct((dbias.shape[0], 16, 32), F32), jax.ShapeDtypeStruct((32, 128), F32)),
        scratch_shapes=[pltpu.VMEM(half_shapes[0], BF16), pltpu.VMEM(half_shapes[1], BF16),
                        pltpu.VMEM((32, GRID_W), F32),
                        pltpu.SemaphoreType.DMA((2,)), pltpu.SemaphoreType.DMA((2,))],
        compiler_params=pltpu.CompilerParams(vmem_limit_bytes=56 << 20))(dwin_b, dwout_b, dbias, dlg)


def _grad_finish_call(sl_in, sl_out, small):
    arrs = (sl_in, sl_out)

    def body(sin, sout, sm, gin, gout, sm_all, h_in, h_out, ssem, rsem, lsem):
        x, y, cc = _mesh_pos()
        me = 4 * x + 2 * y + cc
        sib = (x, y, 1 - cc)
        sls, hs, gs = (sin, sout), (h_in, h_out), (gin, gout)
        sm_send, sm_recv = _all_to_all_small(sm, sm_all, ssem, rsem, 2, x, y, cc)
        sm_own = pltpu.make_async_copy(sm, sm_all.at[me], lsem.at[0])
        for cp in sm_send + [sm_own]:
            cp.start()
        for a in range(2):
            def total(i, carry, a=a):
                rows = pl.ds(pl.multiple_of(i * VROWS, VROWS), VROWS)
                sl = sls[a]
                hs[a][rows, :] = ((sl[0, rows, :].astype(F32) + sl[1, rows, :].astype(F32))
                                  + sl[2, rows, :].astype(F32)) + sl[3, rows, :].astype(F32)
                return carry
            lax.fori_loop(0, arrs[a].shape[1] // VROWS, total, 0)
        mine = [pltpu.make_async_copy(hs[a], gs[a].at[cc], lsem.at[1 + a]) for a in range(2)]
        back = [_remote(hs[a], gs[a].at[cc], ssem, rsem, a, sib) for a in range(2)]
        back_recv = [_remote(hs[a], gs[a].at[1 - cc], ssem, rsem, a, sib) for a in range(2)]
        for cp in mine + back:
            cp.start()
        _finish(mine + [sm_own], back + sm_send, back_recv + sm_recv)

    vmem = pl.BlockSpec(memory_space=pltpu.VMEM)
    return pl.pallas_call(
        body, name="grad_finish",
        in_specs=[vmem] * 3, out_specs=(vmem,) * 3,
        out_shape=(jax.ShapeDtypeStruct((2,) + sl_in.shape[1:], F32),
                   jax.ShapeDtypeStruct((2,) + sl_out.shape[1:], F32),
                   jax.ShapeDtypeStruct((N_DEV,) + small.shape, F32)),
        scratch_shapes=[pltpu.VMEM(sl_in.shape[1:], F32), pltpu.VMEM(sl_out.shape[1:], F32),
                        pltpu.SemaphoreType.DMA((9,)), pltpu.SemaphoreType.DMA((9,)),
                        pltpu.SemaphoreType.DMA((3,))],
        compiler_params=pltpu.CompilerParams(vmem_limit_bytes=48 << 20))(sl_in, sl_out, small)


def _adamw(w, g, m, v):
    m = ADAM_B1 * m + (1.0 - ADAM_B1) * g
    v = ADAM_B2 * v + (1.0 - ADAM_B2) * (g * g)
    m_hat = m / (1.0 - ADAM_B1 ** ADAM_STEP)
    v_hat = v / (1.0 - ADAM_B2 ** ADAM_STEP)
    return -ADAM_LR * (m_hat / (jnp.sqrt(v_hat) + ADAM_EPS) + ADAM_WD * w), m, v


def _adam_call(w, m, v, g, name):
    R, C = w.shape
    tr = 256

    def body(w_ref, m_ref, v_ref, g_ref, go_ref, d_ref, mo_ref, vo_ref):
        g = g_ref[...]
        go_ref[...] = g
        d_ref[...], mo_ref[...], vo_ref[...] = _adamw(w_ref[...], g, m_ref[...], v_ref[...])

    spec = pl.BlockSpec((tr, C), lambda i: (i, 0))
    return pl.pallas_call(
        body, name=name, grid=(R // tr,), in_specs=[spec] * 4,
        out_specs=(spec,) * 4, out_shape=(jax.ShapeDtypeStruct((R, C), F32),) * 4,
        compiler_params=_params(("arbitrary",)))(w, m, v, g)


R_GF, R_NG, R_LOSS, R_RNG, R_LGF, R_LGB, R_SHIFT, R_SCALE, R_GATE, R_SHIFT_C, R_SCALE_C, R_RNG2, R_RPB = (
    0, 1, 2, 3, 4, 5, 6, 8, 10, 12, 13, 14, 16)
W_GF, W_NG, W_CCTX, W_RNG, W_DF, W_DB, W_BADA, W_RPB = 0, 1, 2, 3, 4, 5, 6, 9


SMALL = (("final_norm_g", W_GF, 1, D), ("norm_g", W_NG, 1, D), ("c_ctx", W_CCTX, 1, D),
         ("ret_norm_g", W_RNG, 1, 512), ("ret_decay_fwd", W_DF, 1, 4), ("ret_decay_bwd", W_DB, 1, 4),
         ("b_ada", W_BADA, 3, D), ("na_rpb", W_RPB, 4, D))
N_SMALL = len(SMALL)


def _small_final_call(sm_all, c_t, wada_f, wada, m_ada, v_ada, small_w, small_m, small_v, B):
    ws = wada.shape[1]
    NB = N_DEV * B

    def body(*refs):
        sm_ref, ct_ref, wf_ref, wa_ref, ma_ref, va_ref = refs[:6]
        ins = refs[6:6 + 3 * N_SMALL]
        outs = refs[6 + 3 * N_SMALL:6 + 7 * N_SMALL]
        ga_ref, da_ref, mao_ref, vao_ref, loss_ref, dmod_ref, pk_ref = refs[6 + 7 * N_SMALL:]
        x, y, _ = _mesh_pos()
        s = 2 * x + y
        tot = sm_ref[0]
        for dv in range(1, N_DEV):
            tot = tot + sm_ref[dv]
        pk_ref[...] = jnp.zeros_like(pk_ref)
        for kind in range(3):
            for i, (_, row, nrow, width) in enumerate(SMALL):
                ref = ins[kind * N_SMALL + i]
                if nrow == 3:
                    for part in range(3):
                        pk_ref[kind, row + part:row + part + 1, :] = ref[:, part * D:(part + 1) * D]
                else:
                    pk_ref[kind, row:row + nrow, 0:width] = ref[...]
        w = pk_ref[0]
        cctx_ref = ins[2]
        for dv in range(N_DEV):
            for b in range(B):
                r = dv * B + b
                for part, row in enumerate((R_SHIFT, R_SCALE, R_GATE)):
                    dmod_ref[r:r + 1, part * D:(part + 1) * D] = sm_ref[dv, row + b:row + b + 1, :]
        dmod_ref[NB:NB + 1, 0:D] = tot[R_SHIFT_C:R_SHIFT_C + 1, :]
        dmod_ref[NB:NB + 1, D:2 * D] = tot[R_SCALE_C:R_SCALE_C + 1, :]
        dmod_ref[NB:NB + 1, 2 * D:3 * D] = jnp.zeros((1, D), F32)
        dmod_ref[NB + 1:, :] = jnp.zeros((dmod_ref.shape[0] - NB - 1, 3 * D), F32)
        dmod = dmod_ref[...]
        cc = cctx_ref[...]
        scc = _sigmoid(cc)
        ct = ct_ref[...]
        act_t = ct * _sigmoid(ct)
        dmc = dmod[NB:NB + 1, :].astype(BF16)
        dact = jnp.zeros((1, D), F32)
        for sh in range(N_SHARD):
            dact = dact + _dot_nt(dmc[:, sh * ws:(sh + 1) * ws], wf_ref[sh])
        g = jnp.zeros((16, D), F32)
        rows = lax.broadcasted_iota(jnp.int32, (16, D), 0)

        def put(g, row, val):
            return jnp.where(rows == row, val, g)

        g = put(g, W_GF, tot[R_GF:R_GF + 1, :])
        g = put(g, W_NG, tot[R_NG:R_NG + 1, :])
        g = put(g, W_CCTX, dact * (scc * (1.0 + cc * (1.0 - scc))))
        g = put(g, W_RNG, tot[R_RNG:R_RNG + 1, :] + tot[R_RNG2:R_RNG2 + 1, :])
        g = put(g, W_DF, tot[R_LGF:R_LGF + 1, :] * (-jnp.exp(w[W_DF:W_DF + 1, :])))
        g = put(g, W_DB, tot[R_LGB:R_LGB + 1, :] * (-jnp.exp(w[W_DB:W_DB + 1, :])))
        db = jnp.sum(dmod, axis=0, keepdims=True)
        for part in range(3):
            g = put(g, W_BADA + part, db[:, part * D:(part + 1) * D])
        for part in range(4):
            g = put(g, W_RPB + part, tot[R_RPB + part:R_RPB + part + 1, :])
        for kind, val in enumerate((g,) + _adamw(w, g, pk_ref[1], pk_ref[2])):
            for i, (_, row, nrow, width) in enumerate(SMALL):
                out = outs[kind * N_SMALL + i]
                if nrow == 3:
                    for part in range(3):
                        out[:, part * D:(part + 1) * D] = val[row + part:row + part + 1, :]
                else:
                    out[...] = val[row:row + nrow, 0:width]
        loss_ref[...] = jnp.broadcast_to(
            (0.5 / D) * jnp.sum(tot[R_LOSS:R_LOSS + 1, :], axis=1, keepdims=True), (8, 128))
        for sh in range(N_SHARD):
            @pl.when(s == sh)
            def _():
                ga = jnp.dot(act_t, dmod[:, sh * ws:(sh + 1) * ws], precision=HIGHEST,
                             preferred_element_type=F32)
                ga_ref[...] = ga
                da_ref[...], mao_ref[...], vao_ref[...] = _adamw(wa_ref[...], ga, ma_ref[...], va_ref[...])

    sh_small = tuple(jax.ShapeDtypeStruct(a.shape, F32) for a in small_w)
    sh_ada = jax.ShapeDtypeStruct(wada.shape, F32)
    res = pl.pallas_call(
        body, name="small_final",
        out_shape=sh_small * 4 + (sh_ada,) * 4 + (jax.ShapeDtypeStruct((8, 128), F32),),
        scratch_shapes=[pltpu.VMEM((NB + 8, 3 * D), F32), pltpu.VMEM((3, 16, D), F32)],
        compiler_params=_params(vmem_mb=56))(
            sm_all, c_t, wada_f, wada, m_ada, v_ada, *small_w, *small_m, *small_v)
    smalls = [res[k * N_SMALL:(k + 1) * N_SMALL] for k in range(4)]
    return smalls, res[4 * N_SMALL:4 * N_SMALL + 4], res[4 * N_SMALL + 4]


def _local_step(order, x, c, ctx, c_ctx, norm_g, wada_f, b_ada, win_b, bias, dec_f, dec_b, ret_norm_g,
                wout_f, final_g, target):
    B, L, _ = x.shape
    LC = ctx.shape[1]
    assert B == 2
    cos2, sin2 = _rope_tables(L, LC)
    c8 = jnp.concatenate([c, c_ctx[None, :], jnp.zeros((8 - B - 1, D), F32)], axis=0)
    mod = _mod_call(c8, wada_f, b_ada)
    P, h, win_f = _inproj_gather_call(order, x, ctx, mod, norm_g, win_b, cos2, sin2)
    y_na, o_na = _na_fwd_call(P, bias, L, LC)
    sf, sb = _ret_states_call(P, dec_f, dec_b, L, LC)
    y_ret, o_ret = _retc_fwd_call(P, sf, sb, dec_f, dec_b, ret_norm_g, L)
    dY, dx2, dwout_p, sm_out = _out_call(y_na, y_ret, x, target, mod, final_g, wout_f.reshape(D, D))
    dnq, dng, dnk, dnv, dbias = _na_bwd_call(P, bias, dY, o_na, L, LC)
    drq, drg, drk, drv, dgn, dlg = _retc_bwd_call(P, sf, sb, dec_f, dec_b, ret_norm_g, o_ret, dY, cos2, sin2, L, LC)
    dsec = (dnq, dnk, dnv, dng, drq, drk, drv, drg)
    dwin_b = _dw_call(dsec, h, L)
    cp_in, cp_out, drpb, dlg_sum = _grad_halves_call(
        dwin_b, dwout_p.reshape(N_SHARD, D // N_SHARD, D), dbias, dlg)
    grad_x, sm_dh, sl_in, sl_out = _dh_call(dsec, win_f, x, ctx, dx2, mod, norm_g, cp_in, cp_out)
    z = jnp.zeros((1, D), F32)
    pad = lambda v: jnp.pad(v.reshape(1, -1), ((0, 0), (0, D - v.size)))
    dlg_sum = dlg_sum.reshape(4, 8, 128)
    rpb_rows = jnp.pad(drpb[:, :15, :31].reshape(-1), (0, 4 * D - drpb.shape[0] * 465)).reshape(4, D)
    small = jnp.concatenate([
        sm_out[0:1], sm_dh[0:1], sm_out[1:2], pad(dgn[0]), pad(dlg_sum[:, 0, 0]), pad(dlg_sum[:, 1, 0]),
        sm_dh[3:5], sm_dh[5:7], sm_out[2:4], sm_dh[1:2], sm_dh[2:3], pad(dgn[1]), z, rpb_rows,
        jnp.zeros((SM_ROWS - 20, D), F32)], axis=0)
    return grad_x, sl_in, sl_out, small


def kernel(x, c, ctx, c_ctx, norm_g, w_ada, b_ada, w_in, na_rpb, ret_decay_fwd, ret_decay_bwd, ret_norm_g, w_out, final_norm_g, loss_target, m_c_ctx, m_norm_g, m_w_ada, m_b_ada, m_w_in, m_na_rpb, m_ret_decay_fwd, m_ret_decay_bwd, m_ret_norm_g, m_w_out, m_final_norm_g, v_c_ctx, v_norm_g, v_w_ada, v_b_ada, v_w_in, v_na_rpb, v_ret_decay_fwd, v_ret_decay_bwd, v_ret_norm_g, v_w_out, v_final_norm_g):
    B = x.shape[0]
    wout_f, wada_f, c_all, bias = _gather_call(
        w_out[0].astype(BF16), w_ada[0].astype(BF16), c, na_rpb[0].reshape(na_rpb.shape[1], -1))
    mx, my = lax.axis_index("x"), lax.axis_index("y")
    order = jnp.stack([2 * mx + my, 2 * (1 - mx) + my, 2 * mx + (1 - my),
                       2 * (1 - mx) + (1 - my)]).astype(jnp.int32)
    grad_x, sl_in, sl_out, small = _local_step(
        order, x, c, ctx, c_ctx, norm_g, wada_f, b_ada, w_in[0].astype(BF16), bias, ret_decay_fwd,
        ret_decay_bwd, ret_norm_g, wout_f, final_norm_g.reshape(1, D), loss_target)
    gin, gout, sm_all = _grad_finish_call(sl_in, sl_out, small)
    g_win, d_win, nm_win, nv_win = _adam_call(
        w_in[0], m_w_in[0], v_w_in[0], gin.reshape(w_in.shape[1:]), "adam_w_in")
    g_wout, d_wout, nm_wout, nv_wout = _adam_call(
        w_out[0], m_w_out[0], v_w_out[0], gout.reshape(w_out.shape[1:]), "adam_w_out")

    def small_inputs(gf, ng, cc, rng, df, db, bada, rpb):
        return (gf.reshape(1, D), ng, cc.reshape(1, D), rng, df, db, bada,
                jnp.pad(rpb.reshape(-1), (0, 4 * D - rpb.size)).reshape(4, D))

    c_t = jnp.concatenate([c_all.reshape(N_DEV * B, D), c_ctx.reshape(1, D), jnp.zeros((7, D), F32)], axis=0).T
    smalls, adas, loss = _small_final_call(
        sm_all, c_t, wada_f, w_ada[0], m_w_ada[0], v_w_ada[0],
        small_inputs(final_norm_g, norm_g, c_ctx, ret_norm_g, ret_decay_fwd, ret_decay_bwd, b_ada, na_rpb),
        small_inputs(m_final_norm_g, m_norm_g, m_c_ctx, m_ret_norm_g, m_ret_decay_fwd, m_ret_decay_bwd, m_b_ada,
                     m_na_rpb),
        small_inputs(v_final_norm_g, v_norm_g, v_c_ctx, v_ret_norm_g, v_ret_decay_fwd, v_ret_decay_bwd, v_b_ada,
                     v_na_rpb), B)
    res = []
    for p, ada, win_o, wout_o in zip(smalls, adas, (g_win, d_win, nm_win, nv_win),
                                     (g_wout, d_wout, nm_wout, nv_wout)):
        gf, ng, cc, rng, df, db, bada, rpb = p
        res.append([cc.reshape(D), ng, ada[None], bada, win_o[None],
                    rpb.reshape(-1)[:na_rpb.size].reshape(na_rpb.shape), df, db, rng, wout_o[None], gf.reshape(D)])
    return (loss[0, 0], grad_x, *res[0], *res[1], *res[2], *res[3])
```

```python
import numpy as np
import jax
import jax.numpy as jnp
from jax import lax
from jax.experimental import pallas as pl
from jax.experimental.pallas import tpu as pltpu

F32 = jnp.float32
BF16 = jnp.bfloat16
HIGHEST = lax.Precision.HIGHEST

D = 1024
GRID_W = 64
NA_DH = 64
RET_DK = 128
ROPE_BASE = 10000.0
EPS = 1e-6
NEG = -1e30
TQ = 256
KW = 12 * GRID_W
N_SHARD = 4
N_DEV = 8
SM_ROWS = 24

ADAM_LR = 0.001
ADAM_B1 = 0.9
ADAM_B2 = 0.999
ADAM_EPS = 1e-08
ADAM_WD = 0.01
ADAM_STEP = 10

MESH = pl.DeviceIdType.MESH
ANY = pl.BlockSpec(memory_space=pl.ANY)


def _params(sem=None, vmem_mb=48):
    return pltpu.CompilerParams(dimension_semantics=sem, vmem_limit_bytes=vmem_mb << 20)


def _dot(a, b):
    return jnp.dot(a, b, preferred_element_type=F32)


def _dot_nt(a, b):
    return lax.dot_general(a, b, (((1,), (1,)), ((), ())), preferred_element_type=F32)


def _dot_tn(a, b):
    return lax.dot_general(a, b, (((0,), (0,)), ((), ())), preferred_element_type=F32)


def _sigmoid(x):
    return 1.0 / (1.0 + jnp.exp(-x))


def _rope_tables(L, LC):
    half = RET_DK // 2
    nf = half // 2
    t = np.arange(L)
    row = (t // GRID_W).astype(np.float32)
    col = (t % GRID_W).astype(np.float32)
    inv = (np.float32(ROPE_BASE) ** (-np.arange(nf, dtype=np.float32) / np.float32(nf))).astype(np.float32)
    ang = np.concatenate([row[:, None] * inv, col[:, None] * inv], axis=-1).astype(np.float32)
    cos, sin = np.cos(ang).astype(np.float32), np.sin(ang).astype(np.float32)
    cos2 = np.concatenate([cos, cos], axis=-1)
    sin2 = np.concatenate([-sin, sin], axis=-1)
    cos2 = np.concatenate([cos2, np.ones((LC, RET_DK), np.float32)], axis=0)
    sin2 = np.concatenate([sin2, np.zeros((LC, RET_DK), np.float32)], axis=0)
    return jnp.asarray(cos2), jnp.asarray(sin2)


def _mod_call(c8, wada_f, b_ada):
    ws = wada_f.shape[2]

    def body(c_ref, w_ref, b_ref, o_ref):
        a = c_ref[...]
        a = (a * _sigmoid(a)).astype(BF16)
        for s in range(N_SHARD):
            o_ref[:, s * ws:(s + 1) * ws] = _dot(a, w_ref[s]) + b_ref[:, s * ws:(s + 1) * ws]

    return pl.pallas_call(
        body, name="ada_mod", out_shape=jax.ShapeDtypeStruct((8, 3 * D), F32),
        compiler_params=_params())(c8, wada_f, b_ada)


def _dc_masks():
    cq = lax.broadcasted_iota(jnp.int32, (GRID_W, GRID_W), 0)
    ck = lax.broadcasted_iota(jnp.int32, (GRID_W, GRID_W), 1)
    dc = jnp.clip(ck - cq + 15, 0, 30)
    c0 = jnp.clip(cq - 8, 0, GRID_W - 16)
    col_ok = (ck >= c0) & (ck < c0 + 16)
    return dc, col_ok


def _bias_blocks():
    out = []
    for typ, delta in enumerate((4, 0, -4)):
        for rq in range(4):
            for rkk in range(12):
                dr = rkk + delta - rq - 4
                if typ == 0:
                    ok = -rq <= dr <= 7 - rq
                elif typ == 1:
                    ok = -4 <= dr <= 3
                else:
                    ok = -4 - rq <= dr <= 3 - rq
                out.append((typ, rq, rkk, dr if ok else None))
    return out


def _bias_body(r_ref, bias_ref, et_ref, out_ref, sem):
    dc, col_ok = _dc_masks()
    masks = [(dc == j).astype(F32) for j in range(31)]
    nh = bias_ref.shape[0]

    def per_h(h, carry):
        for dr in range(15):
            t = jnp.zeros((GRID_W, GRID_W), F32)
            for j in range(31):
                t = t + masks[j] * r_ref[h, dr * 31 + j]
            et_ref[dr] = jnp.where(col_ok, t, NEG)
        neg = jnp.full((GRID_W, GRID_W), NEG, F32)
        for typ, rq, rkk, dr in _bias_blocks():
            blk = neg if dr is None else et_ref[dr + 7]
            bias_ref[h, typ, rq * 64:(rq + 1) * 64, rkk * 64:(rkk + 1) * 64] = blk
        pltpu.make_async_copy(bias_ref.at[h], out_ref.at[h], sem).start()
        return carry

    lax.fori_loop(0, nh, per_h, 0)
    return [pltpu.make_async_copy(bias_ref.at[h], out_ref.at[h], sem) for h in range(nh)]


def _bias_tile_sums(db_ref, hh):
    acc = {}
    for typ, rq, rkk, dr in _bias_blocks():
        if dr is None:
            continue
        blk = db_ref[hh, typ, rq * 64:(rq + 1) * 64, rkk * 64:(rkk + 1) * 64]
        acc[dr] = blk if dr not in acc else acc[dr] + blk
    return acc


def _small_reduce_body(dt_ref, dlg_ref, drpb_ref, dlgo_ref, p_ref):
    dc, _ = _dc_masks()
    masks = [(dc == j).astype(F32) for j in range(31)]
    ones = jnp.ones((8, GRID_W), F32)
    p_ref[...] = jnp.zeros_like(p_ref)
    drpb_ref[...] = jnp.zeros_like(drpb_ref)

    def per_h(h, carry):
        for dr in range(-7, 8):
            t = dt_ref[h, dr + 7]
            for j in range(31):
                p_ref[j:j + 1, :] = jnp.sum(t * masks[j], axis=0, keepdims=True)
            red = lax.dot_general(ones, p_ref[...], (((1,), (1,)), ((), ())),
                                  precision=HIGHEST, preferred_element_type=F32)
            drpb_ref[h, dr + 7:dr + 8, :] = red[0:1, :]
        return carry

    lax.fori_loop(0, dt_ref.shape[0], per_h, 0)
    x = dlg_ref[0]
    for b in range(1, dlg_ref.shape[0]):
        x = x + dlg_ref[b]
    x = x.reshape(4 * 8, x.shape[-1])
    dlgo_ref[...] = jnp.dot(x, jnp.ones((x.shape[-1], 128), F32), precision=HIGHEST,
                            preferred_element_type=F32)


def _inproj_gather_call(order, x, ctx, mod, norm_g, win_b, wout_b, cos2, sin2):
    B, L, _ = x.shape
    LC = ctx.shape[1]
    T = L + LC
    TI = 2 * TQ
    nl = L // TI
    nt = nl + 1
    assert LC == TQ and L % TI == 0
    kscale = RET_DK ** -0.5
    HR = D // 2
    pad_rows = nt * TI - T
    cos2 = jnp.pad(cos2, ((0, pad_rows), (0, 0)))
    sin2 = jnp.pad(sin2, ((0, pad_rows), (0, 0)))

    def body(ord_ref, x_ref, ctx_ref, mod_ref, g_ref, wown_ref, woown_ref, cos_ref, sin_ref,
             p_ref, h_ref, wf_ref, wof_ref, w_all, wo_all, hs_ref, ssem, rsem, lsem):
        j, b, t = pl.program_id(0), pl.program_id(1), pl.program_id(2)
        first = (b == 0) & (t == 0)
        mx, my, mc = _mesh_pos()
        s = 2 * mx + my
        sib = (mx, my, 1 - mc)

        def gather_copies(own_ref, all_ref, out_ref, hr, k0, l0):
            own = pltpu.make_async_copy(own_ref, all_ref.at[s], lsem.at[l0])
            send, recv, fsend, frecv = [], [], [], []
            outs = [pltpu.make_async_copy(all_ref.at[s], out_ref.at[s], lsem.at[l0 + 1])]
            for k, (px, py) in enumerate(_other_chips(mx, my)):
                ps = 2 * px + py
                mine = all_ref.at[s, pl.ds(mc * hr, hr)]
                send.append(_remote(mine, mine, ssem, rsem, k0 + k, (px, py, mc)))
                got = all_ref.at[ps, pl.ds(mc * hr, hr)]
                recv.append(_remote(mine, got, ssem, rsem, k0 + k, (px, py, mc)))
                fsend.append(_remote(got, got, ssem, rsem, k0 + 3 + k, sib))
                theirs = all_ref.at[ps, pl.ds((1 - mc) * hr, hr)]
                frecv.append(_remote(theirs, theirs, ssem, rsem, k0 + 3 + k, sib))
                outs.append(pltpu.make_async_copy(all_ref.at[ps], out_ref.at[ps], lsem.at[l0 + 2 + k]))
            return own, send, recv, fsend, frecv, outs

        own, ici_send, ici_recv, fwd_send, fwd_recv, outs = gather_copies(wown_ref, w_all, wf_ref, HR, 0, 0)
        oown, o_send, o_recv, o_fsend, o_frecv, o_outs = gather_copies(
            woown_ref, wo_all, wof_ref, woown_ref.shape[0] // 2, 6, 5)

        @pl.when(first & (j == 0))
        def _():
            own.start()
            oown.start()
            own.wait()
            ici_send[0].start()
            ici_send[1].start()
            outs[0].start()
            oown.wait()

        for k in range(3):
            @pl.when(first & (j == k + 1))
            def _(k=k):
                ici_recv[k].wait_recv()
                if k == 0:
                    ici_send[2].start()
                fwd_send[k].start()
                fwd_recv[k].wait_recv()
                outs[1 + k].start()
                if k == 1:
                    for cp in o_send:
                        cp.start()
                if k == 2:
                    for got, fwd in zip(o_recv, o_fsend):
                        got.wait_recv()
                        fwd.start()

        tile = b * nt + t

        @pl.when(j == 0)
        def _():
            is_lat = t < nl
            ctx_tile = jnp.concatenate([ctx_ref[...], jnp.zeros((TI - LC, D), F32)], axis=0)
            xt = jnp.where(is_lat, x_ref[...], ctx_tile)
            mrow = mod_ref[pl.ds(jnp.where(is_lat, b, B), 1), :]
            shift, scale = mrow[:, 0:D], mrow[:, D:2 * D]
            rstd = lax.rsqrt(jnp.mean(xt * xt, axis=-1, keepdims=True) + EPS)
            h0 = ((xt * rstd * g_ref[...]) * (1.0 + scale) + shift).astype(BF16)
            h_ref[...] = h0
            hs_ref[tile] = h0

        hb = hs_ref[tile]
        cs, sn = cos_ref[...], sin_ref[...]
        shard = ord_ref[j]
        for sh in range(N_SHARD):
            @pl.when(shard == sh)
            def _(sh=sh):
                for half in range(2):
                    sec = 2 * sh + half
                    acc = _dot(hb, w_all[sh, :, half * 512:(half + 1) * 512])
                    if sec == 0:
                        acc = acc * (NA_DH ** -0.5)
                    if sec in (4, 5):
                        for q in range(4):
                            a = acc[:, q * 128:(q + 1) * 128]
                            r = a * cs + pltpu.roll(a, 64, 1) * sn
                            if sec == 5:
                                r = r * kscale
                            p_ref[:, half * 512 + q * 128:half * 512 + (q + 1) * 128] = r.astype(BF16)
                    else:
                        p_ref[:, half * 512:(half + 1) * 512] = acc.astype(BF16)

        @pl.when((j == N_SHARD - 1) & (b == B - 1) & (t == nt - 1))
        def _():
            for cp in o_frecv:
                cp.wait_recv()
            for cp in o_outs:
                cp.start()
            _finish(outs + o_outs, ici_send + fwd_send + o_send + o_fsend, [])

    tok = lambda j, b, t, o: (jnp.where(j == 0, b, B - 1), jnp.where(j == 0, jnp.minimum(t, nl - 1), nl - 1), 0)
    grid_spec = pltpu.PrefetchScalarGridSpec(
        num_scalar_prefetch=1, grid=(N_SHARD, B, nt),
        in_specs=[
            pl.BlockSpec((None, TI, D), tok),
            pl.BlockSpec((None, LC, D), lambda j, b, t, o: (jnp.where(j == 0, b, B - 1), 0, 0)),
            pl.BlockSpec((8, 3 * D), lambda j, b, t, o: (0, 0)),
            pl.BlockSpec((1, D), lambda j, b, t, o: (0, 0)),
            ANY, ANY,
            pl.BlockSpec((TI, RET_DK), lambda j, b, t, o: (t, 0)),
            pl.BlockSpec((TI, RET_DK), lambda j, b, t, o: (t, 0)),
        ],
        out_specs=(pl.BlockSpec((None, TI, D), lambda j, b, t, o: (b, t, o[j])),
                   pl.BlockSpec((None, TI, D), lambda j, b, t, o: (
                       jnp.where(j == 0, b, B - 1), jnp.where(j == 0, t, nt - 1), 0)), ANY, ANY),
        scratch_shapes=[pltpu.VMEM((N_SHARD, D, D), BF16), pltpu.VMEM((N_SHARD,) + wout_b.shape, BF16),
                        pltpu.VMEM((B * nt, TI, D), BF16),
                        pltpu.SemaphoreType.DMA((12,)), pltpu.SemaphoreType.DMA((12,)),
                        pltpu.SemaphoreType.DMA((10,))])
    return pl.pallas_call(
        body, name="in_proj", grid_spec=grid_spec,
        out_shape=(jax.ShapeDtypeStruct((B, T, 4 * D), BF16), jax.ShapeDtypeStruct((B, T, D), BF16),
                   jax.ShapeDtypeStruct((N_SHARD, D, D), BF16),
                   jax.ShapeDtypeStruct((N_SHARD,) + wout_b.shape, BF16)),
        compiler_params=_params(("arbitrary",) * 3, vmem_mb=56))(
            order, x, ctx, mod, norm_g, win_b, wout_b, cos2, sin2)


def _na_specs(L, T, rows, nh=2):
    nm = rows // 4
    w = nh * NA_DH
    per = 512 // w
    q_spec = pl.BlockSpec((None, TQ, w), lambda hp, b, m: (b, m, hp))
    k_spec = pl.BlockSpec((None, T, w), lambda hp, b, m: (b, 0, per + hp))
    v_spec = pl.BlockSpec((None, T, w), lambda hp, b, m: (b, 0, 2 * per + hp))
    g_spec = pl.BlockSpec((None, TQ, w), lambda hp, b, m: (b, m, 3 * per + hp))
    bias_spec = pl.BlockSpec((nh, 3, TQ, KW), lambda hp, b, m: (hp, 0, 0, 0))
    return nm, q_spec, k_spec, v_spec, g_spec, bias_spec


def _na_tile(m, nm, rows):
    typ = jnp.where(m == 0, 0, jnp.where(m == nm - 1, 2, 1))
    start = pl.multiple_of(jnp.clip(4 * m - 4, 0, rows - 12) * GRID_W, TQ)
    return typ, start


def _na_fwd_call(P, bias, L, LC):
    B, T, _ = P.shape
    rows = L // GRID_W
    NH = 4
    nm, q_spec, k_spec, v_spec, g_spec, bias_spec = _na_specs(L, T, rows, NH)

    def body(q_ref, k_ref, v_ref, g_ref, bias_ref, y_ref, o_ref):
        typ, start = _na_tile(pl.program_id(2), nm, rows)
        for hh in range(NH):
            ln = slice(hh * NA_DH, (hh + 1) * NA_DH)
            q = q_ref[:, ln]
            kw, vw = k_ref[pl.ds(start, KW), ln], v_ref[pl.ds(start, KW), ln]
            kc, vc = k_ref[L:L + LC, ln], v_ref[L:L + LC, ln]
            s1 = _dot_nt(q, kw) + bias_ref[hh, typ]
            s2 = _dot_nt(q, kc)
            mx = jnp.maximum(jnp.max(s1, axis=-1, keepdims=True), jnp.max(s2, axis=-1, keepdims=True))
            p1, p2 = jnp.exp(s1 - mx), jnp.exp(s2 - mx)
            inv = 1.0 / (jnp.sum(p1, axis=-1, keepdims=True) + jnp.sum(p2, axis=-1, keepdims=True))
            o = (_dot(p1.astype(BF16), vw) + _dot(p2.astype(BF16), vc)) * inv
            g = g_ref[:, ln].astype(F32)
            o_ref[:, ln] = o.astype(BF16)
            y_ref[:, ln] = (o * (g * _sigmoid(g))).astype(BF16)

    tile = pl.BlockSpec((None, TQ, NH * NA_DH), lambda hp, b, m: (b, m, hp))
    return pl.pallas_call(
        body, name="na_fwd", grid=(8 // NH, B, nm),
        in_specs=[q_spec, k_spec, v_spec, g_spec, bias_spec],
        out_specs=(tile, tile),
        out_shape=(jax.ShapeDtypeStruct((B, L, 512), BF16),) * 2,
        compiler_params=_params(("arbitrary",) * 3))(P, P, P, P, bias)


def _na_bwd_call(P, bias, dY, o_na, L, LC):
    B, T, _ = P.shape
    rows = L // GRID_W
    NH = 4
    W = NH * NA_DH
    nm, q_spec, k_spec, v_spec, g_spec, bias_spec = _na_specs(L, T, rows, NH)
    scale = NA_DH ** -0.5

    RB = 32

    def body(q_ref, k_ref, v_ref, g_ref, bias_ref, dy_ref, o_ref, dq_ref, dg_ref, dk_ref, dv_ref, dt_ref,
             db_ref, s1_ref, s2_ref, dp1_ref, dp2_ref, p1_ref, p2_ref, ds1_ref, ds2_ref, dkt_ref, dvt_ref):
        b, m = pl.program_id(1), pl.program_id(2)
        typ, start = _na_tile(m, nm, rows)

        @pl.when(m == 0)
        def _():
            dkt_ref[...] = jnp.zeros_like(dkt_ref)
            dvt_ref[...] = jnp.zeros_like(dvt_ref)

        @pl.when((m == 0) & (b == 0))
        def _():
            db_ref[...] = jnp.zeros_like(db_ref)

        for hh in range(NH):
            ln = slice(hh * NA_DH, (hh + 1) * NA_DH)
            q = q_ref[:, ln]
            kw, vw = k_ref[pl.ds(start, KW), ln], v_ref[pl.ds(start, KW), ln]
            kc, vc = k_ref[L:L + LC, ln], v_ref[L:L + LC, ln]
            g = g_ref[:, ln].astype(F32)
            sg = _sigmoid(g)
            dy = dy_ref[:, ln].astype(F32)
            do = (dy * (g * sg)).astype(BF16)
            s1_ref[hh] = _dot_nt(q, kw)
            s2_ref[hh] = _dot_nt(q, kc)
            dp1_ref[hh] = _dot_nt(do, vw)
            dp2_ref[hh] = _dot_nt(do, vc)

            def rows_pass(r, carry, hh=hh):
                rw = pl.ds(pl.multiple_of(r * RB, RB), RB)
                a = s1_ref[hh, rw, :] + bias_ref[hh, typ, rw, :]
                c = s2_ref[hh, rw, :]
                mx = jnp.maximum(jnp.max(a, axis=-1, keepdims=True), jnp.max(c, axis=-1, keepdims=True))
                e1, e2 = jnp.exp(a - mx), jnp.exp(c - mx)
                inv = 1.0 / (jnp.sum(e1, axis=-1, keepdims=True) + jnp.sum(e2, axis=-1, keepdims=True))
                p1, p2 = e1 * inv, e2 * inv
                p1_ref[hh, rw, :] = p1.astype(BF16)
                p2_ref[hh, rw, :] = p2.astype(BF16)
                dp1, dp2 = dp1_ref[hh, rw, :], dp2_ref[hh, rw, :]
                delta = jnp.sum(p1 * dp1, axis=-1, keepdims=True) + jnp.sum(p2 * dp2, axis=-1, keepdims=True)
                ds1 = p1 * (dp1 - delta)
                db_ref[hh, typ, rw, :] += ds1
                ds1_ref[hh, rw, :] = ds1.astype(BF16)
                ds2_ref[hh, rw, :] = (p2 * (dp2 - delta)).astype(BF16)
                return carry

            lax.fori_loop(0, TQ // RB, rows_pass, 0, unroll=True)
            p1b, p2b, ds1b, ds2b = p1_ref[hh], p2_ref[hh], ds1_ref[hh], ds2_ref[hh]
            dg_ref[:, ln] = (dy * o_ref[:, ln].astype(F32) * (sg * (1.0 + g * (1.0 - sg)))).astype(BF16)
            dq_ref[:, ln] = ((_dot(ds1b, kw) + _dot(ds2b, kc)) * scale).astype(BF16)
            dkt_ref[ln, pl.ds(start, KW)] += _dot_tn(q, ds1b)
            dvt_ref[ln, pl.ds(start, KW)] += _dot_tn(do, p1b)
            dkt_ref[ln, L:L + LC] += _dot_tn(q, ds2b)
            dvt_ref[ln, L:L + LC] += _dot_tn(do, p2b)

        @pl.when(m == nm - 1)
        def _():
            dk_ref[...] = dkt_ref[...].T
            dv_ref[...] = dvt_ref[...].T

        @pl.when((m == nm - 1) & (b == B - 1))
        def _():
            for hh in range(NH):
                for dr, t in _bias_tile_sums(db_ref, hh).items():
                    dt_ref[hh, dr + 7] = t

    tile = pl.BlockSpec((None, TQ, W), lambda hp, b, m: (b, m, hp))
    kv_out = pl.BlockSpec((None, T, W), lambda hp, b, m: (b, 0, hp))
    wide, narrow = (NH, TQ, KW), (NH, TQ, LC)
    return pl.pallas_call(
        body, name="na_bwd", grid=(8 // NH, B, nm),
        in_specs=[q_spec, k_spec, v_spec, g_spec, bias_spec, tile, tile],
        out_specs=(tile, tile, kv_out, kv_out,
                   pl.BlockSpec((NH, 15, GRID_W, GRID_W), lambda hp, b, m: (hp, 0, 0, 0))),
        out_shape=(jax.ShapeDtypeStruct((B, L, 512), BF16), jax.ShapeDtypeStruct((B, L, 512), BF16),
                   jax.ShapeDtypeStruct((B, T, 512), F32), jax.ShapeDtypeStruct((B, T, 512), F32),
                   jax.ShapeDtypeStruct((bias.shape[0], 15, GRID_W, GRID_W), F32)),
        scratch_shapes=[pltpu.VMEM((NH,) + bias.shape[1:], F32),
                        pltpu.VMEM(wide, F32), pltpu.VMEM(narrow, F32), pltpu.VMEM(wide, F32), pltpu.VMEM(narrow, F32),
                        pltpu.VMEM(wide, BF16), pltpu.VMEM(narrow, BF16), pltpu.VMEM(wide, BF16),
                        pltpu.VMEM(narrow, BF16), pltpu.VMEM((W, T), F32), pltpu.VMEM((W, T), F32)],
        compiler_params=_params(("arbitrary",) * 3, vmem_mb=60))(P, P, P, P, bias, dY, o_na)


def _head_scalar(dec_ref, h):
    lane = lax.broadcasted_iota(jnp.int32, dec_ref.shape, 1)
    return -jnp.sum(jnp.where(lane == h, jnp.exp(dec_ref[...]), 0.0), axis=1, keepdims=True)


def _chunk_decay(lgf, lgb):
    tau = lax.broadcasted_iota(jnp.int32, (TQ, 1), 0).astype(F32)
    sig = lax.broadcasted_iota(jnp.int32, (1, TQ), 1).astype(F32)
    dist = tau - sig
    dm = jnp.exp(dist * jnp.where(dist > 0, lgf, -lgb)) * jnp.where(dist == 0, 2.0, 1.0)
    return tau, dist, dm


def _ret_states_call(P, dec_f, dec_b, L, LC):
    B, T, _ = P.shape
    n = L // TQ

    def body(df_ref, db_ref, k_ref, v_ref, sf_ref, sb_ref):
        h = pl.program_id(1)
        lgf, lgb = _head_scalar(df_ref, h), _head_scalar(db_ref, h)
        tau = lax.broadcasted_iota(jnp.int32, (TQ, 1), 0).astype(F32)
        jc = lax.broadcasted_iota(jnp.int32, (LC, 1), 0).astype(F32)
        wf, wb = jnp.exp(lgf * (TQ - 1.0 - tau)), jnp.exp(lgb * tau)
        gcf, gcb = jnp.exp(lgf * float(TQ)), jnp.exp(lgb * float(TQ))
        kc, vc = k_ref[L:L + LC, :].astype(F32), v_ref[L:L + LC, :]

        def chunk_state(i, w):
            ks = pl.multiple_of(i * TQ, TQ)
            return _dot_tn((k_ref[pl.ds(ks, TQ), :].astype(F32) * w).astype(BF16), v_ref[pl.ds(ks, TQ), :])

        def fwd(i, s):
            sf_ref[i] = s
            return gcf * s + chunk_state(i, wf)

        lax.fori_loop(0, n, fwd, _dot_tn((kc * jnp.exp(lgf * (LC - 1.0 - jc))).astype(BF16), vc), unroll=True)

        def bwd(r, s):
            i = n - 1 - r
            sb_ref[i] = s
            return gcb * s + chunk_state(i, wb)

        lax.fori_loop(0, n, bwd, _dot_tn((kc * jnp.exp(lgb * jc)).astype(BF16), vc), unroll=True)

    st = pl.BlockSpec((None, None, n, RET_DK, RET_DK), lambda b, h: (b, h, 0, 0, 0))
    return pl.pallas_call(
        body, name="ret_states", grid=(B, 4),
        in_specs=[pl.BlockSpec((1, 4), lambda b, h: (0, 0)), pl.BlockSpec((1, 4), lambda b, h: (0, 0)),
                  pl.BlockSpec((None, T, 128), lambda b, h: (b, 0, 20 + h)),
                  pl.BlockSpec((None, T, 128), lambda b, h: (b, 0, 24 + h))],
        out_specs=(st, st),
        out_shape=(jax.ShapeDtypeStruct((B, 4, n, RET_DK, RET_DK), F32),) * 2,
        compiler_params=_params(("arbitrary",) * 2))(dec_f, dec_b, P, P)


def _retc_fwd_call(P, sf, sb, dec_f, dec_b, ret_norm_g, L):
    B, T, _ = P.shape
    sec = lambda k: pl.BlockSpec((None, TQ, 512), lambda b, i: (b, i, k))
    dec_spec = pl.BlockSpec((1, 4), lambda b, i: (0, 0))
    st_spec = pl.BlockSpec((None, 4, None, RET_DK, RET_DK), lambda b, i: (b, 0, i, 0, 0))

    def body(df_ref, db_ref, q_ref, k_ref, v_ref, g_ref, gn_ref, sf_ref, sb_ref, y_ref, o_ref):
        for h in range(4):
            ln = slice(h * RET_DK, (h + 1) * RET_DK)
            lgf, lgb = _head_scalar(df_ref, h), _head_scalar(db_ref, h)
            tau, _, dm = _chunk_decay(lgf, lgb)
            q = q_ref[:, ln]
            qf = q.astype(F32)
            acc = _dot((_dot_nt(q, k_ref[:, ln]) * dm).astype(BF16), v_ref[:, ln])
            acc = acc + _dot((qf * jnp.exp(lgf * (tau + 1.0))).astype(BF16), sf_ref[h].astype(BF16))
            acc = acc + _dot((qf * jnp.exp(lgb * (TQ - tau))).astype(BF16), sb_ref[h].astype(BF16))
            o_ref[:, ln] = acc
            rn = lax.rsqrt(jnp.mean(acc * acc, axis=-1, keepdims=True) + EPS)
            g = g_ref[:, ln].astype(F32)
            y_ref[:, ln] = ((acc * rn * gn_ref[:, ln]) * (g * _sigmoid(g))).astype(BF16)

    tile = pl.BlockSpec((None, TQ, 512), lambda b, i: (b, i, 0))
    return pl.pallas_call(
        body, name="ret_fwd", grid=(B, L // TQ),
        in_specs=[dec_spec, dec_spec, sec(4), sec(5), sec(6), sec(7),
                  pl.BlockSpec((1, 512), lambda b, i: (0, 0)), st_spec, st_spec],
        out_specs=(tile, tile),
        out_shape=(jax.ShapeDtypeStruct((B, L, 512), BF16), jax.ShapeDtypeStruct((B, L, 512), F32)),
        compiler_params=_params(("arbitrary",) * 2))(dec_f, dec_b, P, P, P, P, ret_norm_g, sf, sb)


def _retc_bwd_call(P, sf, sb, dec_f, dec_b, ret_norm_g, o_ret, dY, cos2, sin2, L, LC):
    B, T, _ = P.shape
    n = L // TQ
    C = float(TQ)
    kscale = RET_DK ** -0.5
    st_spec = pl.BlockSpec((None, 4, n, RET_DK, RET_DK), lambda b, i: (b, 0, 0, 0, 0))

    def body(df_ref, db_ref, q_ref, k_ref, v_ref, g_ref, gn_ref, o_ref, dy_ref, cos_ref, sin_ref, sf_ref, sb_ref,
             dq_ref, dg_ref, dk_ref, dv_ref, dgn_ref, dlg_ref, dsf_ref, dsb_ref):
        i = pl.program_id(1)

        @pl.when(i == 0)
        def _():
            dk_ref[...] = jnp.zeros_like(dk_ref)
            dv_ref[...] = jnp.zeros_like(dv_ref)
            dgn_ref[...] = jnp.zeros_like(dgn_ref)
            dlg_ref[...] = jnp.zeros_like(dlg_ref)

        rows = pl.ds(pl.multiple_of(i * TQ, TQ), TQ)
        cs, sn = cos_ref[rows, :], sin_ref[rows, :]

        def one_head(h):
            ln = slice(h * RET_DK, (h + 1) * RET_DK)
            lgf, lgb = _head_scalar(df_ref, h), _head_scalar(db_ref, h)
            tau, dist, dm = _chunk_decay(lgf, lgb)

            def add_lg(row, x):
                csum = jnp.sum(x, axis=0, keepdims=True)
                tot = csum[:, 0:128]
                for part in range(1, x.shape[1] // 128):
                    tot = tot + csum[:, part * 128:(part + 1) * 128]
                dlg_ref[h, row:row + 1, :] += tot

            q = q_ref[:, ln]
            qf = q.astype(F32)
            o = o_ref[:, ln]
            g = g_ref[:, ln].astype(F32)
            dy = dy_ref[:, ln].astype(F32)
            gn = gn_ref[:, ln]
            sg = _sigmoid(g)
            rn = lax.rsqrt(jnp.mean(o * o, axis=-1, keepdims=True) + EPS)
            nrm = o * rn
            dg_ref[:, ln] = (dy * (nrm * gn) * (sg * (1.0 + g * (1.0 - sg)))).astype(BF16)
            dhn = dy * (g * sg)
            dgn_ref[:, ln] += jnp.sum(dhn * nrm, axis=0, keepdims=True)
            dnrm = dhn * gn
            do = rn * (dnrm - nrm * jnp.mean(dnrm * nrm, axis=-1, keepdims=True))
            dob = do.astype(BF16)
            ki, vi = k_ref[rows, ln], v_ref[rows, ln]
            s = _dot_nt(q, ki)
            dsv = _dot_nt(dob, vi)
            dsb = (dsv * dm).astype(BF16)
            dk_ref[rows, ln] += _dot_tn(dsb, q)
            dv_ref[rows, ln] += _dot_tn((s * dm).astype(BF16), dob)
            xw = s * dsv * dm * jnp.abs(dist)
            fpart = jnp.where(dist > 0, xw, 0.0)
            add_lg(0, fpart)
            add_lg(1, xw - fpart)
            dq = _dot(dsb, ki)
            af, ab = jnp.exp(lgf * (tau + 1.0)), jnp.exp(lgb * (C - tau))
            qa, qb = (qf * af).astype(BF16), (qf * ab).astype(BF16)
            sfi, sbi = sf_ref[h, i].astype(BF16), sb_ref[h, i].astype(BF16)
            dq = dq + af * _dot_nt(dob, sfi) + ab * _dot_nt(dob, sbi)
            dsf_ref[h, i] = _dot_tn(qa, dob)
            dsb_ref[h, i] = _dot_tn(qb, dob)
            add_lg(0, (tau + 1.0) * (_dot(qa, sfi) * do))
            add_lg(1, (C - tau) * (_dot(qb, sbi) * do))
            dq_ref[:, ln] = (dq * cs - pltpu.roll(dq, 64, 1) * sn).astype(BF16)

            @pl.when(i == n - 1)
            def _():
                jc = lax.broadcasted_iota(jnp.int32, (LC, 1), 0).astype(F32)
                crow = pl.ds(L, LC)

                def through_state(rws, w, dw, gst, row):
                    kk, vv = k_ref[rws, ln].astype(F32), v_ref[rws, ln]
                    gb = gst.astype(BF16)
                    vg = _dot_nt(vv, gb)
                    kw = kk * w
                    dk_ref[rws, ln] += w * vg
                    dv_ref[rws, ln] += _dot(kw.astype(BF16), gb)
                    add_lg(row, dw * (kw * vg))

                def scan(gc, w, dw, st_ref, dst_ref, order, row):
                    def step(r, gst):
                        j = order(r)
                        through_state(pl.ds(pl.multiple_of(j * TQ, TQ), TQ), w, dw, gst, row)
                        add_lg(row, (C * gc) * (gst * st_ref[h, j]))
                        return dst_ref[h, j] + gc * gst
                    return lax.fori_loop(0, n, step, jnp.zeros((RET_DK, RET_DK), F32), unroll=True)

                gcf, gcb = jnp.exp(lgf * C), jnp.exp(lgb * C)
                g0 = scan(gcf, jnp.exp(lgf * (C - 1.0 - tau)), C - 1.0 - tau, sf_ref, dsf_ref,
                          lambda r: n - 1 - r, 0)
                through_state(crow, jnp.exp(lgf * (LC - 1.0 - jc)), LC - 1.0 - jc, g0, 0)
                g1 = scan(gcb, jnp.exp(lgb * tau), tau, sb_ref, dsb_ref, lambda r: r, 1)
                through_state(crow, jnp.exp(lgb * jc), jc, g1, 1)
                dk = dk_ref[:, ln]
                dk_ref[:, ln] = (dk * cos_ref[...] - pltpu.roll(dk, 64, 1) * sin_ref[...]) * kscale

        for h in range(4):
            one_head(h)

    sec = lambda k: pl.BlockSpec((None, TQ, 512), lambda b, i: (b, i, k))
    full = lambda k: pl.BlockSpec((None, T, 512), lambda b, i: (b, 0, k))
    dec_spec = pl.BlockSpec((1, 4), lambda b, i: (0, 0))
    tab = pl.BlockSpec((T, RET_DK), lambda b, i: (0, 0))
    return pl.pallas_call(
        body, name="ret_bwd", grid=(B, n),
        in_specs=[dec_spec, dec_spec, sec(4), full(5), full(6), sec(7),
                  pl.BlockSpec((1, 512), lambda b, i: (0, 0)), sec(0), sec(1), tab, tab, st_spec, st_spec],
        out_specs=(sec(0), sec(0), full(0), full(0),
                   pl.BlockSpec((None, 1, 512), lambda b, i: (b, 0, 0)),
                   pl.BlockSpec((None, 4, 8, 128), lambda b, i: (b, 0, 0, 0))),
        out_shape=(jax.ShapeDtypeStruct((B, L, 512), BF16), jax.ShapeDtypeStruct((B, L, 512), BF16),
                   jax.ShapeDtypeStruct((B, T, 512), F32), jax.ShapeDtypeStruct((B, T, 512), F32),
                   jax.ShapeDtypeStruct((B, 1, 512), F32), jax.ShapeDtypeStruct((B, 4, 8, 128), F32)),
        scratch_shapes=[pltpu.VMEM((4, n, RET_DK, RET_DK), F32), pltpu.VMEM((4, n, RET_DK, RET_DK), F32)],
        compiler_params=_params(("arbitrary",) * 2, vmem_mb=56))(
            dec_f, dec_b, P, P, P, P, ret_norm_g, o_ret, dY, cos2, sin2, sf, sb)


def _out_call(y_na, y_ret, x, target, mod, final_g, wout_f):
    B, L, _ = x.shape
    TO = 2 * TQ

    def body(yn_ref, yr_ref, x_ref, t_ref, mod_ref, gf_ref, w_ref, dy_ref, dx2_ref, dwb_ref, sm_ref, dw_ref):
        b, i = pl.program_id(0), pl.program_id(1)

        @pl.when((b == 0) & (i == 0))
        def _():
            dw_ref[...] = jnp.zeros_like(dw_ref)
            sm_ref[...] = jnp.zeros_like(sm_ref)

        gate = mod_ref[pl.ds(b, 1), 2 * D:3 * D]
        gf = gf_ref[...]
        yn, yr = yn_ref[...], yr_ref[...]
        ylat = _dot(yn, w_ref[0:512, :]) + _dot(yr, w_ref[512:1024, :])
        x2 = x_ref[...] + gate * ylat
        r = lax.rsqrt(jnp.mean(x2 * x2, axis=-1, keepdims=True) + EPS)
        xr = x2 * r
        err = xr * gf - t_ref[...]
        sm_ref[1:2, :] += jnp.sum(err * err, axis=0, keepdims=True)
        dout = err * (1.0 / D)
        sm_ref[0:1, :] += jnp.sum(dout * xr, axis=0, keepdims=True)
        gd = dout * gf
        dx2 = r * (gd - xr * jnp.mean(gd * xr, axis=-1, keepdims=True))
        dx2_ref[...] = dx2
        sm_ref[pl.ds(2 + b, 1), :] += jnp.sum(dx2 * ylat, axis=0, keepdims=True)
        dyl = (gate * dx2).astype(BF16)
        dy_ref[:, 0:512] = _dot_nt(dyl, w_ref[0:512, :]).astype(BF16)
        dy_ref[:, 512:1024] = _dot_nt(dyl, w_ref[512:1024, :]).astype(BF16)
        dw_ref[0:512, :] += _dot_tn(yn, dyl)
        dw_ref[512:1024, :] += _dot_tn(yr, dyl)

        @pl.when((b == B - 1) & (i == L // TO - 1))
        def _():
            dwb_ref[...] = dw_ref[...].astype(BF16)

    half = pl.BlockSpec((None, TO, 512), lambda b, i: (b, i, 0))
    full = pl.BlockSpec((None, TO, D), lambda b, i: (b, i, 0))
    return pl.pallas_call(
        body, name="out_proj_loss", grid=(B, L // TO),
        in_specs=[half, half, full, full,
                  pl.BlockSpec((8, 3 * D), lambda b, i: (0, 0)),
                  pl.BlockSpec((1, D), lambda b, i: (0, 0)),
                  pl.BlockSpec((D, D), lambda b, i: (0, 0))],
        out_specs=(full, full, pl.BlockSpec((D, D), lambda b, i: (0, 0)),
                   pl.BlockSpec((8, D), lambda b, i: (0, 0))),
        out_shape=(jax.ShapeDtypeStruct((B, L, D), BF16), jax.ShapeDtypeStruct((B, L, D), F32),
                   jax.ShapeDtypeStruct((D, D), BF16), jax.ShapeDtypeStruct((8, D), F32)),
        scratch_shapes=[pltpu.VMEM((D, D), F32)],
        compiler_params=_params(("arbitrary",) * 2))(y_na, y_ret, x, target, mod, final_g, wout_f)


def _dh_call(dsec, win_f, x, ctx, dx2, mod, norm_g, cp_in, cp_out):
    B, L, _ = x.shape
    LC = ctx.shape[1]
    nl = L // TQ

    def body(d0, d1, d2, d3, d4, d5, d6, d7, w_ref, x_ref, ctx_ref, dx2_ref, mod_ref, g_ref, cpi_ref, cpo_ref,
             gx_ref, sm_ref, sli_ref, slo_ref, ssem, rsem, lsem):
        drefs = (d0, d1, d2, d3, d4, d5, d6, d7)
        b, t = pl.program_id(0), pl.program_id(1)
        is_lat = t < nl

        @pl.when((b == 0) & (t == 0))
        def _():
            sm_ref[...] = jnp.zeros_like(sm_ref)

        def dh_of(secs):
            acc = jnp.zeros((TQ, D), F32)
            for sec in secs:
                s, half = divmod(sec, 2)
                acc = acc + _dot_nt(drefs[sec][...].astype(BF16), w_ref[s, :, half * 512:(half + 1) * 512])
            return acc

        def norm_bwd(dh, xt, mrow):
            scale = mrow[:, D:2 * D]
            g = g_ref[...]
            rstd = lax.rsqrt(jnp.mean(xt * xt, axis=-1, keepdims=True) + EPS)
            xn = xt * rstd
            dshift = jnp.sum(dh, axis=0, keepdims=True)
            dscale = jnp.sum(dh * (xn * g), axis=0, keepdims=True)
            dhn = dh * (1.0 + scale)
            sm_ref[0:1, :] += jnp.sum(dhn * xn, axis=0, keepdims=True)
            dxn = dhn * g
            dx = rstd * (dxn - xn * jnp.mean(dxn * xn, axis=-1, keepdims=True))
            return dshift, dscale, dx

        @pl.when(is_lat)
        def _():
            dshift, dscale, dx = norm_bwd(dh_of(range(8)), x_ref[...], mod_ref[pl.ds(b, 1), :])
            sm_ref[pl.ds(3 + b, 1), :] += dshift
            sm_ref[pl.ds(3 + B + b, 1), :] += dscale
            gx_ref[...] = dx2_ref[...] + dx

        @pl.when(jnp.logical_not(is_lat))
        def _():
            dshift, dscale, _ = norm_bwd(dh_of((1, 2, 5, 6)), ctx_ref[...], mod_ref[B:B + 1, :])
            sm_ref[1:2, :] += dshift
            sm_ref[2:3, :] += dscale

        mx, my, mc = _mesh_pos()
        s = 2 * mx + my
        cps, sls = (cpi_ref, cpo_ref), (sli_ref, slo_ref)
        own = [pltpu.make_async_copy(cps[a].at[s], sls[a].at[s], lsem.at[a]) for a in range(2)]
        sends, recvs, k = [], [], 0
        for px, py in _other_chips(mx, my):
            ps = 2 * px + py
            for a in range(2):
                sends.append(_remote(cps[a].at[ps], sls[a].at[s], ssem, rsem, k, (px, py, mc)))
                recvs.append(_remote(cps[a].at[s], sls[a].at[ps], ssem, rsem, k, (px, py, mc)))
                k += 1

        @pl.when((b == 0) & (t == 0))
        def _():
            for cp in own + sends:
                cp.start()

        @pl.when((b == B - 1) & (t == nl))
        def _():
            _finish(own, sends, recvs)

    lat = lambda b, t: (b, jnp.minimum(t, nl - 1), 0)
    tok = lambda b, t: (b, t, 0)
    sec_specs = [pl.BlockSpec((None, TQ, 512), lat if sec in (0, 3, 4, 7) else tok) for sec in range(8)]
    return pl.pallas_call(
        body, name="dh_norm_bwd", grid=(B, nl + 1),
        in_specs=sec_specs + [
            pl.BlockSpec((N_SHARD, D, D), lambda b, t: (0, 0, 0)),
            pl.BlockSpec((None, TQ, D), lat),
            pl.BlockSpec((None, LC, D), lambda b, t: (b, 0, 0)),
            pl.BlockSpec((None, TQ, D), lat),
            pl.BlockSpec((8, 3 * D), lambda b, t: (0, 0)),
            pl.BlockSpec((1, D), lambda b, t: (0, 0)), ANY, ANY],
        out_specs=(pl.BlockSpec((None, TQ, D), lat), pl.BlockSpec((8, D), lambda b, t: (0, 0)), ANY, ANY),
        out_shape=(jax.ShapeDtypeStruct((B, L, D), F32), jax.ShapeDtypeStruct((8, D), F32),
                   jax.ShapeDtypeStruct(cp_in.shape, cp_in.dtype), jax.ShapeDtypeStruct(cp_out.shape, cp_out.dtype)),
        scratch_shapes=[pltpu.SemaphoreType.DMA((6,)), pltpu.SemaphoreType.DMA((6,)),
                        pltpu.SemaphoreType.DMA((2,))],
        compiler_params=_params(("arbitrary",) * 2))(*dsec, win_f, x, ctx, dx2, mod, norm_g, cp_in, cp_out)


def _dw_call(dsec, h, L):
    B, T, _ = h.shape
    TW = 2 * TQ
    nl = L // TW
    KV = (1, 2, 5, 6)

    def body(d0, d1, d2, d3, d4, d5, d6, d7, c1, c2, c5, c6, h_ref, hc_ref, dw_ref, acc_ref):
        drefs = (d0, d1, d2, d3, d4, d5, d6, d7)
        crefs = dict(zip(KV, (c1, c2, c5, c6)))
        b, t = pl.program_id(0), pl.program_id(1)

        @pl.when((b == 0) & (t == 0))
        def _():
            acc_ref[...] = jnp.zeros_like(acc_ref)

        def add(hb, refs, secs):
            for sec in secs:
                s, half = divmod(sec, 2)
                acc_ref[s, :, half * 512:(half + 1) * 512] += _dot_tn(hb, refs[sec][...].astype(BF16))

        @pl.when(t < nl)
        def _():
            add(h_ref[...], drefs, range(8))

        @pl.when(t == nl)
        def _():
            add(hc_ref[...], crefs, KV)

        @pl.when((b == B - 1) & (t == nl))
        def _():
            dw_ref[...] = acc_ref[...].astype(BF16)

    lat = lambda b, t: (b, jnp.minimum(t, nl - 1), 0)
    ctx = lambda b, t: (b, L // TQ, 0)
    return pl.pallas_call(
        body, name="dw_in", grid=(B, nl + 1),
        in_specs=[pl.BlockSpec((None, TW, 512), lat)] * 8 + [pl.BlockSpec((None, TQ, 512), ctx)] * 4
        + [pl.BlockSpec((None, TW, D), lat), pl.BlockSpec((None, TQ, D), ctx)],
        out_specs=pl.BlockSpec((N_SHARD, D, D), lambda b, t: (0, 0, 0)),
        out_shape=jax.ShapeDtypeStruct((N_SHARD, D, D), BF16),
        scratch_shapes=[pltpu.VMEM((N_SHARD, D, D), F32)],
        compiler_params=_params(("arbitrary",) * 2, vmem_mb=60))(*dsec, *[dsec[k] for k in KV], h, h)


def _mesh_pos():
    return lax.axis_index("x"), lax.axis_index("y"), lax.axis_index("c")


def _flip(v, f):
    return 1 - v if f else v


def _remote(src, dst, ssem, rsem, k, peer):
    return pltpu.make_async_remote_copy(src_ref=src, dst_ref=dst, send_sem=ssem.at[k], recv_sem=rsem.at[k],
                                        device_id=peer, device_id_type=MESH)


def _other_chips(x, y):
    return [(_flip(x, fx), _flip(y, fy)) for fx, fy in ((1, 0), (0, 1), (1, 1))]


def _all_to_all_small(src, dst_all, ssem, rsem, k0, x, y, cc):
    me = 4 * x + 2 * y + cc
    sends, recvs = [], []
    for f in range(1, N_DEV):
        px, py, pc = _flip(x, f & 4), _flip(y, f & 2), _flip(cc, f & 1)
        sends.append(_remote(src, dst_all.at[me], ssem, rsem, k0 + f - 1, (px, py, pc)))
        recvs.append(_remote(src, dst_all.at[4 * px + 2 * py + pc], ssem, rsem, k0 + f - 1, (px, py, pc)))
    return sends, recvs


def _finish(local, sends, recvs):
    for cp in recvs:
        cp.wait_recv()
    for cp in sends:
        cp.wait_send()
    for cp in local:
        cp.wait()


def _gather_call(wada_b, c, rpb_flat):
    arrs = (wada_b,)
    na = len(arrs)
    hrs = [a.shape[0] // 2 for a in arrs]

    def body(wada, c_ref, r_ref, wada_f, c_all, bias_out, bias_ref, et_ref, ssem, rsem, lsem):
        x, y, cc = _mesh_pos()
        s, me = 2 * x + y, 4 * x + 2 * y + cc
        sib = (x, y, 1 - cc)
        srcs, dsts = (wada,), (wada_f,)

        def half(a, shard, hc):
            return dsts[a].at[shard, pl.ds(hc * hrs[a], hrs[a])]

        local = [pltpu.make_async_copy(srcs[a], dsts[a].at[s], lsem.at[a]) for a in range(na)]
        local.append(pltpu.make_async_copy(c_ref, c_all.at[me], lsem.at[na]))
        ici_send, ici_recv, fwd_send, fwd_recv, k = [], [], [], [], 0
        for px, py in _other_chips(x, y):
            ps = 2 * px + py
            for a in range(na):
                mine = srcs[a].at[pl.ds(cc * hrs[a], hrs[a])]
                ici_send.append(_remote(mine, half(a, s, cc), ssem, rsem, k, (px, py, cc)))
                ici_recv.append(_remote(mine, half(a, ps, cc), ssem, rsem, k, (px, py, cc)))
                fwd_send.append(_remote(half(a, ps, cc), half(a, ps, cc), ssem, rsem, 3 * na + k, sib))
                fwd_recv.append(_remote(half(a, ps, 1 - cc), half(a, ps, 1 - cc), ssem, rsem, 3 * na + k, sib))
                k += 1
        c_send, c_recv = _all_to_all_small(c_ref, c_all, ssem, rsem, 6 * na, x, y, cc)
        for cp in local + ici_send + c_send:
            cp.start()
        bias_out_copies = _bias_body(r_ref, bias_ref, et_ref, bias_out, lsem.at[na + 1])
        for got, fwd in zip(ici_recv, fwd_send):
            got.wait_recv()
            fwd.start()
        _finish(local + bias_out_copies, ici_send + fwd_send + c_send, fwd_recv + c_recv)

    bias_shape = (rpb_flat.shape[0], 3, TQ, KW)
    return pl.pallas_call(
        body, name="weight_gather",
        in_specs=[pl.BlockSpec(memory_space=pltpu.VMEM)] * 2 + [pl.BlockSpec(memory_space=pltpu.SMEM)],
        out_specs=(pl.BlockSpec(memory_space=pltpu.VMEM),) * 2 + (ANY,),
        out_shape=tuple(jax.ShapeDtypeStruct((N_SHARD,) + a.shape, a.dtype) for a in arrs)
        + (jax.ShapeDtypeStruct((N_DEV,) + c.shape, c.dtype), jax.ShapeDtypeStruct(bias_shape, F32)),
        scratch_shapes=[pltpu.VMEM(bias_shape, F32), pltpu.VMEM((15, GRID_W, GRID_W), F32),
                        pltpu.SemaphoreType.DMA((6 * na + 7,)), pltpu.SemaphoreType.DMA((6 * na + 7,)),
                        pltpu.SemaphoreType.DMA((na + 2,))],
        compiler_params=pltpu.CompilerParams(vmem_limit_bytes=56 << 20))(wada_b, c, rpb_flat)


VROWS = 32


def _grad_halves_call(dwin_b, dwout_b, dbias, dlg):
    arrs = (dwin_b, dwout_b)
    hrs = [a.shape[1] // 2 for a in arrs]

    def body(din, dout, db_ref, dlg_ref, cp_in, cp_out, drpb_ref, dlgo_ref, got_in, got_out, p_ref, ssem, rsem):
        x, y, cc = _mesh_pos()
        sib = (x, y, 1 - cc)
        srcs, gots, cps = (din, dout), (got_in, got_out), (cp_in, cp_out)
        halves = [_remote(srcs[a].at[:, pl.ds((1 - cc) * hrs[a], hrs[a])], gots[a], ssem, rsem, a, sib)
                  for a in range(2)]
        for cp in halves:
            cp.start()
        _small_reduce_body(db_ref, dlg_ref, drpb_ref, dlgo_ref, p_ref)
        for cp in halves:
            cp.wait_recv()
        for a in range(2):
            for j in range(N_SHARD):
                def add(i, carry, a=a, j=j):
                    r = pl.multiple_of(i * VROWS, VROWS)
                    mine = srcs[a][j, pl.ds(pl.multiple_of(cc * hrs[a] + r, VROWS), VROWS), :].astype(F32)
                    cps[a][j, pl.ds(r, VROWS), :] = (
                        mine + gots[a][j, pl.ds(r, VROWS), :].astype(F32)).astype(BF16)
                    return carry
                lax.fori_loop(0, hrs[a] // VROWS, add, 0)
        for cp in halves:
            cp.wait_send()

    vmem = pl.BlockSpec(memory_space=pltpu.VMEM)
    half_shapes = [(N_SHARD, hrs[a], arrs[a].shape[2]) for a in range(2)]
    return pl.pallas_call(
        body, name="grad_halves",
        in_specs=[vmem] * 4, out_specs=(vmem,) * 4,
        out_shape=(jax.ShapeDtypeStruct(half_shapes[0], BF16), jax.ShapeDtypeStruct(half_shapes[1], BF16),
                   jax.ShapeDtypeStruct((dbias.shape[0], 16, 32), F32), jax.ShapeDtypeStruct((32, 128), F32)),
        scratch_shapes=[pltpu.VMEM(half_shapes[0], BF16), pltpu.VMEM(half_shapes[1], BF16),
                        pltpu.VMEM((32, GRID_W), F32),
                        pltpu.SemaphoreType.DMA((2,)), pltpu.SemaphoreType.DMA((2,))],
        compiler_params=pltpu.CompilerParams(vmem_limit_bytes=56 << 20))(dwin_b, dwout_b, dbias, dlg)


def _grad_finish_call(sl_in, sl_out, small):
    arrs = (sl_in, sl_out)

    def body(sin, sout, sm, gin, gout, sm_all, h_in, h_out, ssem, rsem, lsem):
        x, y, cc = _mesh_pos()
        me = 4 * x + 2 * y + cc
        sib = (x, y, 1 - cc)
        sls, hs, gs = (sin, sout), (h_in, h_out), (gin, gout)
        sm_send, sm_recv = _all_to_all_small(sm, sm_all, ssem, rsem, 2, x, y, cc)
        sm_own = pltpu.make_async_copy(sm, sm_all.at[me], lsem.at[0])
        for cp in sm_send + [sm_own]:
            cp.start()
        for a in range(2):
            def total(i, carry, a=a):
                rows = pl.ds(pl.multiple_of(i * VROWS, VROWS), VROWS)
                sl = sls[a]
                hs[a][rows, :] = ((sl[0, rows, :].astype(F32) + sl[1, rows, :].astype(F32))
                                  + sl[2, rows, :].astype(F32)) + sl[3, rows, :].astype(F32)
                return carry
            lax.fori_loop(0, arrs[a].shape[1] // VROWS, total, 0)
        mine = [pltpu.make_async_copy(hs[a], gs[a].at[cc], lsem.at[1 + a]) for a in range(2)]
        back = [_remote(hs[a], gs[a].at[cc], ssem, rsem, a, sib) for a in range(2)]
        back_recv = [_remote(hs[a], gs[a].at[1 - cc], ssem, rsem, a, sib) for a in range(2)]
        for cp in mine + back:
            cp.start()
        _finish(mine + [sm_own], back + sm_send, back_recv + sm_recv)

    vmem = pl.BlockSpec(memory_space=pltpu.VMEM)
    return pl.pallas_call(
        body, name="grad_finish",
        in_specs=[vmem] * 3, out_specs=(vmem,) * 3,
        out_shape=(jax.ShapeDtypeStruct((2,) + sl_in.shape[1:], F32),
                   jax.ShapeDtypeStruct((2,) + sl_out.shape[1:], F32),
                   jax.ShapeDtypeStruct((N_DEV,) + small.shape, F32)),
        scratch_shapes=[pltpu.VMEM(sl_in.shape[1:], F32), pltpu.VMEM(sl_out.shape[1:], F32),
                        pltpu.SemaphoreType.DMA((9,)), pltpu.SemaphoreType.DMA((9,)),
                        pltpu.SemaphoreType.DMA((3,))],
        compiler_params=pltpu.CompilerParams(vmem_limit_bytes=48 << 20))(sl_in, sl_out, small)


def _adamw(w, g, m, v):
    m = ADAM_B1 * m + (1.0 - ADAM_B1) * g
    v = ADAM_B2 * v + (1.0 - ADAM_B2) * (g * g)
    m_hat = m / (1.0 - ADAM_B1 ** ADAM_STEP)
    v_hat = v / (1.0 - ADAM_B2 ** ADAM_STEP)
    return -ADAM_LR * (m_hat / (jnp.sqrt(v_hat) + ADAM_EPS) + ADAM_WD * w), m, v


def _adam_call(w, m, v, g, name):
    R, C = w.shape
    tr = 256

    def body(w_ref, m_ref, v_ref, g_ref, go_ref, d_ref, mo_ref, vo_ref):
        g = g_ref[...]
        go_ref[...] = g
        d_ref[...], mo_ref[...], vo_ref[...] = _adamw(w_ref[...], g, m_ref[...], v_ref[...])

    spec = pl.BlockSpec((tr, C), lambda i: (i, 0))
    return pl.pallas_call(
        body, name=name, grid=(R // tr,), in_specs=[spec] * 4,
        out_specs=(spec,) * 4, out_shape=(jax.ShapeDtypeStruct((R, C), F32),) * 4,
        compiler_params=_params(("arbitrary",)))(w, m, v, g)


R_GF, R_NG, R_LOSS, R_RNG, R_LGF, R_LGB, R_SHIFT, R_SCALE, R_GATE, R_SHIFT_C, R_SCALE_C, R_RNG2, R_RPB = (
    0, 1, 2, 3, 4, 5, 6, 8, 10, 12, 13, 14, 16)
W_GF, W_NG, W_CCTX, W_RNG, W_DF, W_DB, W_BADA, W_RPB = 0, 1, 2, 3, 4, 5, 6, 9


SMALL = (("final_norm_g", W_GF, 1, D), ("norm_g", W_NG, 1, D), ("c_ctx", W_CCTX, 1, D),
         ("ret_norm_g", W_RNG, 1, 512), ("ret_decay_fwd", W_DF, 1, 4), ("ret_decay_bwd", W_DB, 1, 4),
         ("b_ada", W_BADA, 3, D), ("na_rpb", W_RPB, 4, D))
N_SMALL = len(SMALL)


def _small_final_call(sm_all, c_t, wada_f, wada, m_ada, v_ada, small_w, small_m, small_v, B):
    ws = wada.shape[1]
    NB = N_DEV * B

    def body(*refs):
        sm_ref, ct_ref, wf_ref, wa_ref, ma_ref, va_ref = refs[:6]
        ins = refs[6:6 + 3 * N_SMALL]
        outs = refs[6 + 3 * N_SMALL:6 + 7 * N_SMALL]
        ga_ref, da_ref, mao_ref, vao_ref, loss_ref, dmod_ref, pk_ref = refs[6 + 7 * N_SMALL:]
        x, y, _ = _mesh_pos()
        s = 2 * x + y
        tot = sm_ref[0]
        for dv in range(1, N_DEV):
            tot = tot + sm_ref[dv]
        pk_ref[...] = jnp.zeros_like(pk_ref)
        for kind in range(3):
            for i, (_, row, nrow, width) in enumerate(SMALL):
                ref = ins[kind * N_SMALL + i]
                if nrow == 3:
                    for part in range(3):
                        pk_ref[kind, row + part:row + part + 1, :] = ref[:, part * D:(part + 1) * D]
                else:
                    pk_ref[kind, row:row + nrow, 0:width] = ref[...]
        w = pk_ref[0]
        cctx_ref = ins[2]
        for dv in range(N_DEV):
            for b in range(B):
                r = dv * B + b
                for part, row in enumerate((R_SHIFT, R_SCALE, R_GATE)):
                    dmod_ref[r:r + 1, part * D:(part + 1) * D] = sm_ref[dv, row + b:row + b + 1, :]
        dmod_ref[NB:NB + 1, 0:D] = tot[R_SHIFT_C:R_SHIFT_C + 1, :]
        dmod_ref[NB:NB + 1, D:2 * D] = tot[R_SCALE_C:R_SCALE_C + 1, :]
        dmod_ref[NB:NB + 1, 2 * D:3 * D] = jnp.zeros((1, D), F32)
        dmod_ref[NB + 1:, :] = jnp.zeros((dmod_ref.shape[0] - NB - 1, 3 * D), F32)
        dmod = dmod_ref[...]
        cc = cctx_ref[...]
        scc = _sigmoid(cc)
        ct = ct_ref[...]
        act_t = ct * _sigmoid(ct)
        dmc = dmod[NB:NB + 1, :].astype(BF16)
        dact = jnp.zeros((1, D), F32)
        for sh in range(N_SHARD):
            dact = dact + _dot_nt(dmc[:, sh * ws:(sh + 1) * ws], wf_ref[sh])
        g = jnp.zeros((16, D), F32)
        rows = lax.broadcasted_iota(jnp.int32, (16, D), 0)

        def put(g, row, val):
            return jnp.where(rows == row, val, g)

        g = put(g, W_GF, tot[R_GF:R_GF + 1, :])
        g = put(g, W_NG, tot[R_NG:R_NG + 1, :])
        g = put(g, W_CCTX, dact * (scc * (1.0 + cc * (1.0 - scc))))
        g = put(g, W_RNG, tot[R_RNG:R_RNG + 1, :] + tot[R_RNG2:R_RNG2 + 1, :])
        g = put(g, W_DF, tot[R_LGF:R_LGF + 1, :] * (-jnp.exp(w[W_DF:W_DF + 1, :])))
        g = put(g, W_DB, tot[R_LGB:R_LGB + 1, :] * (-jnp.exp(w[W_DB:W_DB + 1, :])))
        db = jnp.sum(dmod, axis=0, keepdims=True)
        for part in range(3):
            g = put(g, W_BADA + part, db[:, part * D:(part + 1) * D])
        for part in range(4):
            g = put(g, W_RPB + part, tot[R_RPB + part:R_RPB + part + 1, :])
        for kind, val in enumerate((g,) + _adamw(w, g, pk_ref[1], pk_ref[2])):
            for i, (_, row, nrow, width) in enumerate(SMALL):
                out = outs[kind * N_SMALL + i]
                if nrow == 3:
                    for part in range(3):
                        out[:, part * D:(part + 1) * D] = val[row + part:row + part + 1, :]
                else:
                    out[...] = val[row:row + nrow, 0:width]
        loss_ref[...] = jnp.broadcast_to(
            (0.5 / D) * jnp.sum(tot[R_LOSS:R_LOSS + 1, :], axis=1, keepdims=True), (8, 128))
        for sh in range(N_SHARD):
            @pl.when(s == sh)
            def _():
                ga = jnp.dot(act_t, dmod[:, sh * ws:(sh + 1) * ws], precision=HIGHEST,
                             preferred_element_type=F32)
                ga_ref[...] = ga
                da_ref[...], mao_ref[...], vao_ref[...] = _adamw(wa_ref[...], ga, ma_ref[...], va_ref[...])

    sh_small = tuple(jax.ShapeDtypeStruct(a.shape, F32) for a in small_w)
    sh_ada = jax.ShapeDtypeStruct(wada.shape, F32)
    res = pl.pallas_call(
        body, name="small_final",
        out_shape=sh_small * 4 + (sh_ada,) * 4 + (jax.ShapeDtypeStruct((8, 128), F32),),
        scratch_shapes=[pltpu.VMEM((NB + 8, 3 * D), F32), pltpu.VMEM((3, 16, D), F32)],
        compiler_params=_params(vmem_mb=56))(
            sm_all, c_t, wada_f, wada, m_ada, v_ada, *small_w, *small_m, *small_v)
    smalls = [res[k * N_SMALL:(k + 1) * N_SMALL] for k in range(4)]
    return smalls, res[4 * N_SMALL:4 * N_SMALL + 4], res[4 * N_SMALL + 4]


def _local_step(order, x, c, ctx, c_ctx, norm_g, wada_f, b_ada, win_b, bias, dec_f, dec_b, ret_norm_g,
                wout_b, final_g, target):
    B, L, _ = x.shape
    LC = ctx.shape[1]
    assert B == 2
    cos2, sin2 = _rope_tables(L, LC)
    c8 = jnp.concatenate([c, c_ctx[None, :], jnp.zeros((8 - B - 1, D), F32)], axis=0)
    mod = _mod_call(c8, wada_f, b_ada)
    P, h, win_f, wout_f = _inproj_gather_call(order, x, ctx, mod, norm_g, win_b, wout_b, cos2, sin2)
    y_na, o_na = _na_fwd_call(P, bias, L, LC)
    sf, sb = _ret_states_call(P, dec_f, dec_b, L, LC)
    y_ret, o_ret = _retc_fwd_call(P, sf, sb, dec_f, dec_b, ret_norm_g, L)
    dY, dx2, dwout_p, sm_out = _out_call(y_na, y_ret, x, target, mod, final_g, wout_f.reshape(D, D))
    dnq, dng, dnk, dnv, dbias = _na_bwd_call(P, bias, dY, o_na, L, LC)
    drq, drg, drk, drv, dgn, dlg = _retc_bwd_call(P, sf, sb, dec_f, dec_b, ret_norm_g, o_ret, dY, cos2, sin2, L, LC)
    dsec = (dnq, dnk, dnv, dng, drq, drk, drv, drg)
    dwin_b = _dw_call(dsec, h, L)
    cp_in, cp_out, drpb, dlg_sum = _grad_halves_call(
        dwin_b, dwout_p.reshape(N_SHARD, D // N_SHARD, D), dbias, dlg)
    grad_x, sm_dh, sl_in, sl_out = _dh_call(dsec, win_f, x, ctx, dx2, mod, norm_g, cp_in, cp_out)
    z = jnp.zeros((1, D), F32)
    pad = lambda v: jnp.pad(v.reshape(1, -1), ((0, 0), (0, D - v.size)))
    dlg_sum = dlg_sum.reshape(4, 8, 128)
    rpb_rows = jnp.pad(drpb[:, :15, :31].reshape(-1), (0, 4 * D - drpb.shape[0] * 465)).reshape(4, D)
    small = jnp.concatenate([
        sm_out[0:1], sm_dh[0:1], sm_out[1:2], pad(dgn[0]), pad(dlg_sum[:, 0, 0]), pad(dlg_sum[:, 1, 0]),
        sm_dh[3:5], sm_dh[5:7], sm_out[2:4], sm_dh[1:2], sm_dh[2:3], pad(dgn[1]), z, rpb_rows,
        jnp.zeros((SM_ROWS - 20, D), F32)], axis=0)
    return grad_x, sl_in, sl_out, small


def kernel(x, c, ctx, c_ctx, norm_g, w_ada, b_ada, w_in, na_rpb, ret_decay_fwd, ret_decay_bwd, ret_norm_g, w_out, final_norm_g, loss_target, m_c_ctx, m_norm_g, m_w_ada, m_b_ada, m_w_in, m_na_rpb, m_ret_decay_fwd, m_ret_decay_bwd, m_ret_norm_g, m_w_out, m_final_norm_g, v_c_ctx, v_norm_g, v_w_ada, v_b_ada, v_w_in, v_na_rpb, v_ret_decay_fwd, v_ret_decay_bwd, v_ret_norm_g, v_w_out, v_final_norm_g):
    B = x.shape[0]
    wada_f, c_all, bias = _gather_call(w_ada[0].astype(BF16), c, na_rpb[0].reshape(na_rpb.shape[1], -1))
    mx, my = lax.axis_index("x"), lax.axis_index("y")
    order = jnp.stack([2 * mx + my, 2 * (1 - mx) + my, 2 * mx + (1 - my),
                       2 * (1 - mx) + (1 - my)]).astype(jnp.int32)
    grad_x, sl_in, sl_out, small = _local_step(
        order, x, c, ctx, c_ctx, norm_g, wada_f, b_ada, w_in[0].astype(BF16), bias, ret_decay_fwd,
        ret_decay_bwd, ret_norm_g, w_out[0].astype(BF16), final_norm_g.reshape(1, D), loss_target)
    gin, gout, sm_all = _grad_finish_call(sl_in, sl_out, small)
    g_win, d_win, nm_win, nv_win = _adam_call(
        w_in[0], m_w_in[0], v_w_in[0], gin.reshape(w_in.shape[1:]), "adam_w_in")
    g_wout, d_wout, nm_wout, nv_wout = _adam_call(
        w_out[0], m_w_out[0], v_w_out[0], gout.reshape(w_out.shape[1:]), "adam_w_out")

    def small_inputs(gf, ng, cc, rng, df, db, bada, rpb):
        return (gf.reshape(1, D), ng, cc.reshape(1, D), rng, df, db, bada,
                jnp.pad(rpb.reshape(-1), (0, 4 * D - rpb.size)).reshape(4, D))

    c_t = jnp.concatenate([c_all.reshape(N_DEV * B, D), c_ctx.reshape(1, D), jnp.zeros((7, D), F32)], axis=0).T
    smalls, adas, loss = _small_final_call(
        sm_all, c_t, wada_f, w_ada[0], m_w_ada[0], v_w_ada[0],
        small_inputs(final_norm_g, norm_g, c_ctx, ret_norm_g, ret_decay_fwd, ret_decay_bwd, b_ada, na_rpb),
        small_inputs(m_final_norm_g, m_norm_g, m_c_ctx, m_ret_norm_g, m_ret_decay_fwd, m_ret_decay_bwd, m_b_ada,
                     m_na_rpb),
        small_inputs(v_final_norm_g, v_norm_g, v_c_ctx, v_ret_norm_g, v_ret_decay_fwd, v_ret_decay_bwd, v_b_ada,
                     v_na_rpb), B)
    res = []
    for p, ada, win_o, wout_o in zip(smalls, adas, (g_win, d_win, nm_win, nv_win),
                                     (g_wout, d_wout, nm_wout, nv_wout)):
        gf, ng, cc, rng, df, db, bada, rpb = p
        res.append([cc.reshape(D), ng, ada[None], bada, win_o[None],
                    rpb.reshape(-1)[:na_rpb.size].reshape(na_rpb.shape), df, db, rng, wout_o[None], gf.reshape(D)])
    return (loss[0, 0], grad_x, *res[0], *res[1], *res[2], *res[3])
```

```python
import numpy as np
import jax
import jax.numpy as jnp
from jax import lax
from jax.experimental import pallas as pl
from jax.experimental.pallas import tpu as pltpu

F32 = jnp.float32
BF16 = jnp.bfloat16
HIGHEST = lax.Precision.HIGHEST

D = 1024
GRID_W = 64
NA_DH = 64
RET_DK = 128
ROPE_BASE = 10000.0
EPS = 1e-6
NEG = -1e30
TQ = 256
KW = 12 * GRID_W
N_SHARD = 4
N_DEV = 8
SM_ROWS = 24

ADAM_LR = 0.001
ADAM_B1 = 0.9
ADAM_B2 = 0.999
ADAM_EPS = 1e-08
ADAM_WD = 0.01
ADAM_STEP = 10

MESH = pl.DeviceIdType.MESH
ANY = pl.BlockSpec(memory_space=pl.ANY)


def _params(sem=None, vmem_mb=48):
    return pltpu.CompilerParams(dimension_semantics=sem, vmem_limit_bytes=vmem_mb << 20)


def _dot(a, b):
    return jnp.dot(a, b, preferred_element_type=F32)


def _dot_nt(a, b):
    return lax.dot_general(a, b, (((1,), (1,)), ((), ())), preferred_element_type=F32)


def _dot_tn(a, b):
    return lax.dot_general(a, b, (((0,), (0,)), ((), ())), preferred_element_type=F32)


def _sigmoid(x):
    return 1.0 / (1.0 + jnp.exp(-x))


def _rope_tables(L, LC):
    half = RET_DK // 2
    nf = half // 2
    t = np.arange(L)
    row = (t // GRID_W).astype(np.float32)
    col = (t % GRID_W).astype(np.float32)
    inv = (np.float32(ROPE_BASE) ** (-np.arange(nf, dtype=np.float32) / np.float32(nf))).astype(np.float32)
    ang = np.concatenate([row[:, None] * inv, col[:, None] * inv], axis=-1).astype(np.float32)
    cos, sin = np.cos(ang).astype(np.float32), np.sin(ang).astype(np.float32)
    cos2 = np.concatenate([cos, cos], axis=-1)
    sin2 = np.concatenate([-sin, sin], axis=-1)
    cos2 = np.concatenate([cos2, np.ones((LC, RET_DK), np.float32)], axis=0)
    sin2 = np.concatenate([sin2, np.zeros((LC, RET_DK), np.float32)], axis=0)
    return jnp.asarray(cos2), jnp.asarray(sin2)


def _mod_call(c8, wada_f, b_ada):
    ws = wada_f.shape[2]

    def body(c_ref, w_ref, b_ref, o_ref):
        a = c_ref[...]
        a = (a * _sigmoid(a)).astype(BF16)
        for s in range(N_SHARD):
            o_ref[:, s * ws:(s + 1) * ws] = _dot(a, w_ref[s]) + b_ref[:, s * ws:(s + 1) * ws]

    return pl.pallas_call(
        body, name="ada_mod", out_shape=jax.ShapeDtypeStruct((8, 3 * D), F32),
        compiler_params=_params())(c8, wada_f, b_ada)


def _dc_masks():
    cq = lax.broadcasted_iota(jnp.int32, (GRID_W, GRID_W), 0)
    ck = lax.broadcasted_iota(jnp.int32, (GRID_W, GRID_W), 1)
    dc = jnp.clip(ck - cq + 15, 0, 30)
    c0 = jnp.clip(cq - 8, 0, GRID_W - 16)
    col_ok = (ck >= c0) & (ck < c0 + 16)
    return dc, col_ok


def _bias_blocks():
    out = []
    for typ, delta in enumerate((4, 0, -4)):
        for rq in range(4):
            for rkk in range(12):
                dr = rkk + delta - rq - 4
                if typ == 0:
                    ok = -rq <= dr <= 7 - rq
                elif typ == 1:
                    ok = -4 <= dr <= 3
                else:
                    ok = -4 - rq <= dr <= 3 - rq
                out.append((typ, rq, rkk, dr if ok else None))
    return out


def _bias_body(r_ref, bias_ref, et_ref, out_ref, sem):
    dc, col_ok = _dc_masks()
    masks = [(dc == j).astype(F32) for j in range(31)]
    nh = bias_ref.shape[0]

    def per_h(h, carry):
        for dr in range(15):
            t = jnp.zeros((GRID_W, GRID_W), F32)
            for j in range(31):
                t = t + masks[j] * r_ref[h, dr * 31 + j]
            et_ref[dr] = jnp.where(col_ok, t, NEG)
        neg = jnp.full((GRID_W, GRID_W), NEG, F32)
        for typ, rq, rkk, dr in _bias_blocks():
            blk = neg if dr is None else et_ref[dr + 7]
            bias_ref[h, typ, rq * 64:(rq + 1) * 64, rkk * 64:(rkk + 1) * 64] = blk
        pltpu.make_async_copy(bias_ref.at[h], out_ref.at[h], sem).start()
        return carry

    lax.fori_loop(0, nh, per_h, 0)
    return [pltpu.make_async_copy(bias_ref.at[h], out_ref.at[h], sem) for h in range(nh)]


def _bias_tile_sums(db_ref, hh):
    acc = {}
    for typ, rq, rkk, dr in _bias_blocks():
        if dr is None:
            continue
        blk = db_ref[hh, typ, rq * 64:(rq + 1) * 64, rkk * 64:(rkk + 1) * 64]
        acc[dr] = blk if dr not in acc else acc[dr] + blk
    return acc


def _small_reduce_body(dt_ref, dlg_ref, drpb_ref, dlgo_ref, p_ref):
    dc, _ = _dc_masks()
    masks = [(dc == j).astype(F32) for j in range(31)]
    ones = jnp.ones((8, GRID_W), F32)
    p_ref[...] = jnp.zeros_like(p_ref)
    drpb_ref[...] = jnp.zeros_like(drpb_ref)

    def per_h(h, carry):
        for dr in range(-7, 8):
            t = dt_ref[h, dr + 7]
            for j in range(31):
                p_ref[j:j + 1, :] = jnp.sum(t * masks[j], axis=0, keepdims=True)
            red = lax.dot_general(ones, p_ref[...], (((1,), (1,)), ((), ())),
                                  precision=HIGHEST, preferred_element_type=F32)
            drpb_ref[h, dr + 7:dr + 8, :] = red[0:1, :]
        return carry

    lax.fori_loop(0, dt_ref.shape[0], per_h, 0)
    x = dlg_ref[0]
    for b in range(1, dlg_ref.shape[0]):
        x = x + dlg_ref[b]
    x = x.reshape(4 * 8, x.shape[-1])
    dlgo_ref[...] = jnp.dot(x, jnp.ones((x.shape[-1], 128), F32), precision=HIGHEST,
                            preferred_element_type=F32)


def _inproj_gather_call(order, x, ctx, mod, norm_g, win_b, wout_b, cos2, sin2):
    B, L, _ = x.shape
    LC = ctx.shape[1]
    T = L + LC
    TI = 2 * TQ
    nl = L // TI
    nt = nl + 1
    assert LC == TQ and L % TI == 0
    kscale = RET_DK ** -0.5
    HR = D // 2
    pad_rows = nt * TI - T
    cos2 = jnp.pad(cos2, ((0, pad_rows), (0, 0)))
    sin2 = jnp.pad(sin2, ((0, pad_rows), (0, 0)))

    def body(ord_ref, x_ref, ctx_ref, mod_ref, g_ref, wown_ref, woown_ref, cos_ref, sin_ref,
             p_ref, h_ref, wf_ref, wof_ref, w_all, wo_all, hs_ref, ssem, rsem, lsem):
        j, b, t = pl.program_id(0), pl.program_id(1), pl.program_id(2)
        first = (b == 0) & (t == 0)
        mx, my, mc = _mesh_pos()
        s = 2 * mx + my
        sib = (mx, my, 1 - mc)

        def gather_copies(own_ref, all_ref, out_ref, hr, k0, l0):
            own = pltpu.make_async_copy(own_ref, all_ref.at[s], lsem.at[l0])
            send, recv, fsend, frecv = [], [], [], []
            outs = [pltpu.make_async_copy(all_ref.at[s], out_ref.at[s], lsem.at[l0 + 1])]
            for k, (px, py) in enumerate(_other_chips(mx, my)):
                ps = 2 * px + py
                mine = all_ref.at[s, pl.ds(mc * hr, hr)]
                send.append(_remote(mine, mine, ssem, rsem, k0 + k, (px, py, mc)))
                got = all_ref.at[ps, pl.ds(mc * hr, hr)]
                recv.append(_remote(mine, got, ssem, rsem, k0 + k, (px, py, mc)))
                fsend.append(_remote(got, got, ssem, rsem, k0 + 3 + k, sib))
                theirs = all_ref.at[ps, pl.ds((1 - mc) * hr, hr)]
                frecv.append(_remote(theirs, theirs, ssem, rsem, k0 + 3 + k, sib))
                outs.append(pltpu.make_async_copy(all_ref.at[ps], out_ref.at[ps], lsem.at[l0 + 2 + k]))
            return own, send, recv, fsend, frecv, outs

        own, ici_send, ici_recv, fwd_send, fwd_recv, outs = gather_copies(wown_ref, w_all, wf_ref, HR, 0, 0)
        oown, o_send, o_recv, o_fsend, o_frecv, o_outs = gather_copies(
            woown_ref, wo_all, wof_ref, woown_ref.shape[0] // 2, 6, 5)

        @pl.when(first & (j == 0))
        def _():
            own.start()
            oown.start()
            own.wait()
            ici_send[0].start()
            ici_send[1].start()
            outs[0].start()
            oown.wait()

        for k in range(3):
            @pl.when(first & (j == k + 1))
            def _(k=k):
                ici_recv[k].wait_recv()
                if k == 0:
                    ici_send[2].start()
                fwd_send[k].start()
                fwd_recv[k].wait_recv()
                outs[1 + k].start()
                if k == 1:
                    for cp in o_send:
                        cp.start()
                if k == 2:
                    for got, fwd in zip(o_recv, o_fsend):
                        got.wait_recv()
                        fwd.start()

        tile = b * nt + t

        @pl.when(j == 0)
        def _():
            is_lat = t < nl
            ctx_tile = jnp.concatenate([ctx_ref[...], jnp.zeros((TI - LC, D), F32)], axis=0)
            xt = jnp.where(is_lat, x_ref[...], ctx_tile)
            mrow = mod_ref[pl.ds(jnp.where(is_lat, b, B), 1), :]
            shift, scale = mrow[:, 0:D], mrow[:, D:2 * D]
            rstd = lax.rsqrt(jnp.mean(xt * xt, axis=-1, keepdims=True) + EPS)
            h0 = ((xt * rstd * g_ref[...]) * (1.0 + scale) + shift).astype(BF16)
            h_ref[...] = h0
            hs_ref[tile] = h0

        hb = hs_ref[tile]
        cs, sn = cos_ref[...], sin_ref[...]
        shard = ord_ref[j]
        for sh in range(N_SHARD):
            @pl.when(shard == sh)
            def _(sh=sh):
                for half in range(2):
                    sec = 2 * sh + half
                    acc = _dot(hb, w_all[sh, :, half * 512:(half + 1) * 512])
                    if sec == 0:
                        acc = acc * (NA_DH ** -0.5)
                    if sec in (4, 5):
                        for q in range(4):
                            a = acc[:, q * 128:(q + 1) * 128]
                            r = a * cs + pltpu.roll(a, 64, 1) * sn
                            if sec == 5:
                                r = r * kscale
                            p_ref[:, half * 512 + q * 128:half * 512 + (q + 1) * 128] = r.astype(BF16)
                    else:
                        p_ref[:, half * 512:(half + 1) * 512] = acc.astype(BF16)

        @pl.when((j == N_SHARD - 1) & (b == B - 1) & (t == nt - 1))
        def _():
            for cp in o_frecv:
                cp.wait_recv()
            for cp in o_outs:
                cp.start()
            _finish(outs + o_outs, ici_send + fwd_send + o_send + o_fsend, [])

    tok = lambda j, b, t, o: (jnp.where(j == 0, b, B - 1), jnp.where(j == 0, jnp.minimum(t, nl - 1), nl - 1), 0)
    grid_spec = pltpu.PrefetchScalarGridSpec(
        num_scalar_prefetch=1, grid=(N_SHARD, B, nt),
        in_specs=[
            pl.BlockSpec((None, TI, D), tok),
            pl.BlockSpec((None, LC, D), lambda j, b, t, o: (jnp.where(j == 0, b, B - 1), 0, 0)),
            pl.BlockSpec((8, 3 * D), lambda j, b, t, o: (0, 0)),
            pl.BlockSpec((1, D), lambda j, b, t, o: (0, 0)),
            ANY, ANY,
            pl.BlockSpec((TI, RET_DK), lambda j, b, t, o: (t, 0)),
            pl.BlockSpec((TI, RET_DK), lambda j, b, t, o: (t, 0)),
        ],
        out_specs=(pl.BlockSpec((None, TI, D), lambda j, b, t, o: (b, t, o[j])),
                   pl.BlockSpec((None, TI, D), lambda j, b, t, o: (
                       jnp.where(j == 0, b, B - 1), jnp.where(j == 0, t, nt - 1), 0)), ANY, ANY),
        scratch_shapes=[pltpu.VMEM((N_SHARD, D, D), BF16), pltpu.VMEM((N_SHARD,) + wout_b.shape, BF16),
                        pltpu.VMEM((B * nt, TI, D), BF16),
                        pltpu.SemaphoreType.DMA((12,)), pltpu.SemaphoreType.DMA((12,)),
                        pltpu.SemaphoreType.DMA((10,))])
    return pl.pallas_call(
        body, name="in_proj", grid_spec=grid_spec,
        out_shape=(jax.ShapeDtypeStruct((B, T, 4 * D), BF16), jax.ShapeDtypeStruct((B, T, D), BF16),
                   jax.ShapeDtypeStruct((N_SHARD, D, D), BF16),
                   jax.ShapeDtypeStruct((N_SHARD,) + wout_b.shape, BF16)),
        compiler_params=_params(("arbitrary",) * 3, vmem_mb=56))(
            order, x, ctx, mod, norm_g, win_b, wout_b, cos2, sin2)


def _na_specs(L, T, rows, nh=2):
    nm = rows // 4
    w = nh * NA_DH
    per = 512 // w
    q_spec = pl.BlockSpec((None, TQ, w), lambda hp, b, m: (b, m, hp))
    k_spec = pl.BlockSpec((None, T, w), lambda hp, b, m: (b, 0, per + hp))
    v_spec = pl.BlockSpec((None, T, w), lambda hp, b, m: (b, 0, 2 * per + hp))
    g_spec = pl.BlockSpec((None, TQ, w), lambda hp, b, m: (b, m, 3 * per + hp))
    bias_spec = pl.BlockSpec((nh, 3, TQ, KW), lambda hp, b, m: (hp, 0, 0, 0))
    return nm, q_spec, k_spec, v_spec, g_spec, bias_spec


def _na_tile(m, nm, rows):
    typ = jnp.where(m == 0, 0, jnp.where(m == nm - 1, 2, 1))
    start = pl.multiple_of(jnp.clip(4 * m - 4, 0, rows - 12) * GRID_W, TQ)
    return typ, start


def _na_fwd_call(P, bias, L, LC):
    B, T, _ = P.shape
    rows = L // GRID_W
    NH = 4
    nm, q_spec, k_spec, v_spec, g_spec, bias_spec = _na_specs(L, T, rows, NH)

    def body(q_ref, k_ref, v_ref, g_ref, bias_ref, y_ref, o_ref):
        typ, start = _na_tile(pl.program_id(2), nm, rows)
        for hh in range(NH):
            ln = slice(hh * NA_DH, (hh + 1) * NA_DH)
            q = q_ref[:, ln]
            kw, vw = k_ref[pl.ds(start, KW), ln], v_ref[pl.ds(start, KW), ln]
            kc, vc = k_ref[L:L + LC, ln], v_ref[L:L + LC, ln]
            s1 = _dot_nt(q, kw) + bias_ref[hh, typ]
            s2 = _dot_nt(q, kc)
            mx = jnp.maximum(jnp.max(s1, axis=-1, keepdims=True), jnp.max(s2, axis=-1, keepdims=True))
            p1, p2 = jnp.exp(s1 - mx), jnp.exp(s2 - mx)
            inv = 1.0 / (jnp.sum(p1, axis=-1, keepdims=True) + jnp.sum(p2, axis=-1, keepdims=True))
            o = (_dot(p1.astype(BF16), vw) + _dot(p2.astype(BF16), vc)) * inv
            g = g_ref[:, ln].astype(F32)
            o_ref[:, ln] = o.astype(BF16)
            y_ref[:, ln] = (o * (g * _sigmoid(g))).astype(BF16)

    tile = pl.BlockSpec((None, TQ, NH * NA_DH), lambda hp, b, m: (b, m, hp))
    return pl.pallas_call(
        body, name="na_fwd", grid=(8 // NH, B, nm),
        in_specs=[q_spec, k_spec, v_spec, g_spec, bias_spec],
        out_specs=(tile, tile),
        out_shape=(jax.ShapeDtypeStruct((B, L, 512), BF16),) * 2,
        compiler_params=_params(("arbitrary",) * 3))(P, P, P, P, bias)


def _na_bwd_call(P, bias, dY, o_na, L, LC):
    B, T, _ = P.shape
    rows = L // GRID_W
    NH = 4
    W = NH * NA_DH
    nm, q_spec, k_spec, v_spec, g_spec, bias_spec = _na_specs(L, T, rows, NH)
    scale = NA_DH ** -0.5

    RB = 32

    def body(q_ref, k_ref, v_ref, g_ref, bias_ref, dy_ref, o_ref, dq_ref, dg_ref, dk_ref, dv_ref, dt_ref,
             db_ref, s1_ref, s2_ref, dp1_ref, dp2_ref, p1_ref, p2_ref, ds1_ref, ds2_ref, dkt_ref, dvt_ref):
        b, m = pl.program_id(1), pl.program_id(2)
        typ, start = _na_tile(m, nm, rows)

        @pl.when(m == 0)
        def _():
            dkt_ref[...] = jnp.zeros_like(dkt_ref)
            dvt_ref[...] = jnp.zeros_like(dvt_ref)

        @pl.when((m == 0) & (b == 0))
        def _():
            db_ref[...] = jnp.zeros_like(db_ref)

        for hh in range(NH):
            ln = slice(hh * NA_DH, (hh + 1) * NA_DH)
            q = q_ref[:, ln]
            kw, vw = k_ref[pl.ds(start, KW), ln], v_ref[pl.ds(start, KW), ln]
            kc, vc = k_ref[L:L + LC, ln], v_ref[L:L + LC, ln]
            g = g_ref[:, ln].astype(F32)
            sg = _sigmoid(g)
            dy = dy_ref[:, ln].astype(F32)
            do = (dy * (g * sg)).astype(BF16)
            s1_ref[hh] = _dot_nt(q, kw)
            s2_ref[hh] = _dot_nt(q, kc)
            dp1_ref[hh] = _dot_nt(do, vw)
            dp2_ref[hh] = _dot_nt(do, vc)

            def rows_pass(r, carry, hh=hh):
                rw = pl.ds(pl.multiple_of(r * RB, RB), RB)
                a = s1_ref[hh, rw, :] + bias_ref[hh, typ, rw, :]
                c = s2_ref[hh, rw, :]
                mx = jnp.maximum(jnp.max(a, axis=-1, keepdims=True), jnp.max(c, axis=-1, keepdims=True))
                e1, e2 = jnp.exp(a - mx), jnp.exp(c - mx)
                inv = 1.0 / (jnp.sum(e1, axis=-1, keepdims=True) + jnp.sum(e2, axis=-1, keepdims=True))
                p1, p2 = e1 * inv, e2 * inv
                p1_ref[hh, rw, :] = p1.astype(BF16)
                p2_ref[hh, rw, :] = p2.astype(BF16)
                dp1, dp2 = dp1_ref[hh, rw, :], dp2_ref[hh, rw, :]
                delta = jnp.sum(p1 * dp1, axis=-1, keepdims=True) + jnp.sum(p2 * dp2, axis=-1, keepdims=True)
                ds1 = p1 * (dp1 - delta)
                db_ref[hh, typ, rw, :] += ds1
                ds1_ref[hh, rw, :] = ds1.astype(BF16)
                ds2_ref[hh, rw, :] = (p2 * (dp2 - delta)).astype(BF16)
                return carry

            lax.fori_loop(0, TQ // RB, rows_pass, 0, unroll=True)
            p1b, p2b, ds1b, ds2b = p1_ref[hh], p2_ref[hh], ds1_ref[hh], ds2_ref[hh]
            dg_ref[:, ln] = (dy * o_ref[:, ln].astype(F32) * (sg * (1.0 + g * (1.0 - sg)))).astype(BF16)
            dq_ref[:, ln] = ((_dot(ds1b, kw) + _dot(ds2b, kc)) * scale).astype(BF16)
            dkt_ref[ln, pl.ds(start, KW)] += _dot_tn(q, ds1b)
            dvt_ref[ln, pl.ds(start, KW)] += _dot_tn(do, p1b)
            dkt_ref[ln, L:L + LC] += _dot_tn(q, ds2b)
            dvt_ref[ln, L:L + LC] += _dot_tn(do, p2b)

        @pl.when(m == nm - 1)
        def _():
            dk_ref[...] = dkt_ref[...].T
            dv_ref[...] = dvt_ref[...].T

        @pl.when((m == nm - 1) & (b == B - 1))
        def _():
            for hh in range(NH):
                for dr, t in _bias_tile_sums(db_ref, hh).items():
                    dt_ref[hh, dr + 7] = t

    tile = pl.BlockSpec((None, TQ, W), lambda hp, b, m: (b, m, hp))
    kv_out = pl.BlockSpec((None, T, W), lambda hp, b, m: (b, 0, hp))
    wide, narrow = (NH, TQ, KW), (NH, TQ, LC)
    return pl.pallas_call(
        body, name="na_bwd", grid=(8 // NH, B, nm),
        in_specs=[q_spec, k_spec, v_spec, g_spec, bias_spec, tile, tile],
        out_specs=(tile, tile, kv_out, kv_out,
                   pl.BlockSpec((NH, 15, GRID_W, GRID_W), lambda hp, b, m: (hp, 0, 0, 0))),
        out_shape=(jax.ShapeDtypeStruct((B, L, 512), BF16), jax.ShapeDtypeStruct((B, L, 512), BF16),
                   jax.ShapeDtypeStruct((B, T, 512), F32), jax.ShapeDtypeStruct((B, T, 512), F32),
                   jax.ShapeDtypeStruct((bias.shape[0], 15, GRID_W, GRID_W), F32)),
        scratch_shapes=[pltpu.VMEM((NH,) + bias.shape[1:], F32),
                        pltpu.VMEM(wide, F32), pltpu.VMEM(narrow, F32), pltpu.VMEM(wide, F32), pltpu.VMEM(narrow, F32),
                        pltpu.VMEM(wide, BF16), pltpu.VMEM(narrow, BF16), pltpu.VMEM(wide, BF16),
                        pltpu.VMEM(narrow, BF16), pltpu.VMEM((W, T), F32), pltpu.VMEM((W, T), F32)],
        compiler_params=_params(("arbitrary",) * 3, vmem_mb=60))(P, P, P, P, bias, dY, o_na)


def _head_scalar(dec_ref, h):
    lane = lax.broadcasted_iota(jnp.int32, dec_ref.shape, 1)
    return -jnp.sum(jnp.where(lane == h, jnp.exp(dec_ref[...]), 0.0), axis=1, keepdims=True)


def _chunk_decay(lgf, lgb):
    tau = lax.broadcasted_iota(jnp.int32, (TQ, 1), 0).astype(F32)
    sig = lax.broadcasted_iota(jnp.int32, (1, TQ), 1).astype(F32)
    dist = tau - sig
    dm = jnp.exp(dist * jnp.where(dist > 0, lgf, -lgb)) * jnp.where(dist == 0, 2.0, 1.0)
    return tau, dist, dm


def _ret_states_call(P, dec_f, dec_b, L, LC):
    B, T, _ = P.shape
    n = L // TQ

    def body(df_ref, db_ref, k_ref, v_ref, sf_ref, sb_ref):
        h = pl.program_id(1)
        lgf, lgb = _head_scalar(df_ref, h), _head_scalar(db_ref, h)
        tau = lax.broadcasted_iota(jnp.int32, (TQ, 1), 0).astype(F32)
        jc = lax.broadcasted_iota(jnp.int32, (LC, 1), 0).astype(F32)
        wf, wb = jnp.exp(lgf * (TQ - 1.0 - tau)), jnp.exp(lgb * tau)
        gcf, gcb = jnp.exp(lgf * float(TQ)), jnp.exp(lgb * float(TQ))
        kc, vc = k_ref[L:L + LC, :].astype(F32), v_ref[L:L + LC, :]

        def chunk_state(i, w):
            ks = pl.multiple_of(i * TQ, TQ)
            return _dot_tn((k_ref[pl.ds(ks, TQ), :].astype(F32) * w).astype(BF16), v_ref[pl.ds(ks, TQ), :])

        def fwd(i, s):
            sf_ref[i] = s
            return gcf * s + chunk_state(i, wf)

        lax.fori_loop(0, n, fwd, _dot_tn((kc * jnp.exp(lgf * (LC - 1.0 - jc))).astype(BF16), vc), unroll=True)

        def bwd(r, s):
            i = n - 1 - r
            sb_ref[i] = s
            return gcb * s + chunk_state(i, wb)

        lax.fori_loop(0, n, bwd, _dot_tn((kc * jnp.exp(lgb * jc)).astype(BF16), vc), unroll=True)

    st = pl.BlockSpec((None, None, n, RET_DK, RET_DK), lambda b, h: (b, h, 0, 0, 0))
    return pl.pallas_call(
        body, name="ret_states", grid=(B, 4),
        in_specs=[pl.BlockSpec((1, 4), lambda b, h: (0, 0)), pl.BlockSpec((1, 4), lambda b, h: (0, 0)),
                  pl.BlockSpec((None, T, 128), lambda b, h: (b, 0, 20 + h)),
                  pl.BlockSpec((None, T, 128), lambda b, h: (b, 0, 24 + h))],
        out_specs=(st, st),
        out_shape=(jax.ShapeDtypeStruct((B, 4, n, RET_DK, RET_DK), F32),) * 2,
        compiler_params=_params(("arbitrary",) * 2))(dec_f, dec_b, P, P)


def _retc_fwd_call(P, sf, sb, dec_f, dec_b, ret_norm_g, L):
    B, T, _ = P.shape
    sec = lambda k: pl.BlockSpec((None, TQ, 512), lambda b, i: (b, i, k))
    dec_spec = pl.BlockSpec((1, 4), lambda b, i: (0, 0))
    st_spec = pl.BlockSpec((None, 4, None, RET_DK, RET_DK), lambda b, i: (b, 0, i, 0, 0))

    def body(df_ref, db_ref, q_ref, k_ref, v_ref, g_ref, gn_ref, sf_ref, sb_ref, y_ref, o_ref):
        for h in range(4):
            ln = slice(h * RET_DK, (h + 1) * RET_DK)
            lgf, lgb = _head_scalar(df_ref, h), _head_scalar(db_ref, h)
            tau, _, dm = _chunk_decay(lgf, lgb)
            q = q_ref[:, ln]
            qf = q.astype(F32)
            acc = _dot((_dot_nt(q, k_ref[:, ln]) * dm).astype(BF16), v_ref[:, ln])
            acc = acc + _dot((qf * jnp.exp(lgf * (tau + 1.0))).astype(BF16), sf_ref[h].astype(BF16))
            acc = acc + _dot((qf * jnp.exp(lgb * (TQ - tau))).astype(BF16), sb_ref[h].astype(BF16))
            o_ref[:, ln] = acc
            rn = lax.rsqrt(jnp.mean(acc * acc, axis=-1, keepdims=True) + EPS)
            g = g_ref[:, ln].astype(F32)
            y_ref[:, ln] = ((acc * rn * gn_ref[:, ln]) * (g * _sigmoid(g))).astype(BF16)

    tile = pl.BlockSpec((None, TQ, 512), lambda b, i: (b, i, 0))
    return pl.pallas_call(
        body, name="ret_fwd", grid=(B, L // TQ),
        in_specs=[dec_spec, dec_spec, sec(4), sec(5), sec(6), sec(7),
                  pl.BlockSpec((1, 512), lambda b, i: (0, 0)), st_spec, st_spec],
        out_specs=(tile, tile),
        out_shape=(jax.ShapeDtypeStruct((B, L, 512), BF16), jax.ShapeDtypeStruct((B, L, 512), F32)),
        compiler_params=_params(("arbitrary",) * 2))(dec_f, dec_b, P, P, P, P, ret_norm_g, sf, sb)


def _retc_bwd_call(P, sf, sb, dec_f, dec_b, ret_norm_g, o_ret, dY, cos2, sin2, L, LC):
    B, T, _ = P.shape
    n = L // TQ
    C = float(TQ)
    kscale = RET_DK ** -0.5
    st_spec = pl.BlockSpec((None, 4, n, RET_DK, RET_DK), lambda b, i: (b, 0, 0, 0, 0))

    def body(df_ref, db_ref, q_ref, k_ref, v_ref, g_ref, gn_ref, o_ref, dy_ref, cos_ref, sin_ref, sf_ref, sb_ref,
             dq_ref, dg_ref, dk_ref, dv_ref, dgn_ref, dlg_ref, dsf_ref, dsb_ref):
        i = pl.program_id(1)

        @pl.when(i == 0)
        def _():
            dk_ref[...] = jnp.zeros_like(dk_ref)
            dv_ref[...] = jnp.zeros_like(dv_ref)
            dgn_ref[...] = jnp.zeros_like(dgn_ref)
            dlg_ref[...] = jnp.zeros_like(dlg_ref)

        rows = pl.ds(pl.multiple_of(i * TQ, TQ), TQ)
        cs, sn = cos_ref[rows, :], sin_ref[rows, :]

        def one_head(h):
            ln = slice(h * RET_DK, (h + 1) * RET_DK)
            lgf, lgb = _head_scalar(df_ref, h), _head_scalar(db_ref, h)
            tau, dist, dm = _chunk_decay(lgf, lgb)

            def add_lg(row, x):
                csum = jnp.sum(x, axis=0, keepdims=True)
                tot = csum[:, 0:128]
                for part in range(1, x.shape[1] // 128):
                    tot = tot + csum[:, part * 128:(part + 1) * 128]
                dlg_ref[h, row:row + 1, :] += tot

            q = q_ref[:, ln]
            qf = q.astype(F32)
            o = o_ref[:, ln]
            g = g_ref[:, ln].astype(F32)
            dy = dy_ref[:, ln].astype(F32)
            gn = gn_ref[:, ln]
            sg = _sigmoid(g)
            rn = lax.rsqrt(jnp.mean(o * o, axis=-1, keepdims=True) + EPS)
            nrm = o * rn
            dg_ref[:, ln] = (dy * (nrm * gn) * (sg * (1.0 + g * (1.0 - sg)))).astype(BF16)
            dhn = dy * (g * sg)
            dgn_ref[:, ln] += jnp.sum(dhn * nrm, axis=0, keepdims=True)
            dnrm = dhn * gn
            do = rn * (dnrm - nrm * jnp.mean(dnrm * nrm, axis=-1, keepdims=True))
            dob = do.astype(BF16)
            ki, vi = k_ref[rows, ln], v_ref[rows, ln]
            s = _dot_nt(q, ki)
            dsv = _dot_nt(dob, vi)
            dsb = (dsv * dm).astype(BF16)
            dk_ref[rows, ln] += _dot_tn(dsb, q)
            dv_ref[rows, ln] += _dot_tn((s * dm).astype(BF16), dob)
            xw = s * dsv * dm * jnp.abs(dist)
            fpart = jnp.where(dist > 0, xw, 0.0)
            add_lg(0, fpart)
            add_lg(1, xw - fpart)
            dq = _dot(dsb, ki)
            af, ab = jnp.exp(lgf * (tau + 1.0)), jnp.exp(lgb * (C - tau))
            qa, qb = (qf * af).astype(BF16), (qf * ab).astype(BF16)
            sfi, sbi = sf_ref[h, i].astype(BF16), sb_ref[h, i].astype(BF16)
            dq = dq + af * _dot_nt(dob, sfi) + ab * _dot_nt(dob, sbi)
            dsf_ref[h, i] = _dot_tn(qa, dob)
            dsb_ref[h, i] = _dot_tn(qb, dob)
            add_lg(0, (tau + 1.0) * (_dot(qa, sfi) * do))
            add_lg(1, (C - tau) * (_dot(qb, sbi) * do))
            dq_ref[:, ln] = (dq * cs - pltpu.roll(dq, 64, 1) * sn).astype(BF16)

            @pl.when(i == n - 1)
            def _():
                jc = lax.broadcasted_iota(jnp.int32, (LC, 1), 0).astype(F32)
                crow = pl.ds(L, LC)

                def through_state(rws, w, dw, gst, row):
                    kk, vv = k_ref[rws, ln].astype(F32), v_ref[rws, ln]
                    gb = gst.astype(BF16)
                    vg = _dot_nt(vv, gb)
                    kw = kk * w
                    dk_ref[rws, ln] += w * vg
                    dv_ref[rws, ln] += _dot(kw.astype(BF16), gb)
                    add_lg(row, dw * (kw * vg))

                def scan(gc, w, dw, st_ref, dst_ref, order, row):
                    def step(r, gst):
                        j = order(r)
                        through_state(pl.ds(pl.multiple_of(j * TQ, TQ), TQ), w, dw, gst, row)
                        add_lg(row, (C * gc) * (gst * st_ref[h, j]))
                        return dst_ref[h, j] + gc * gst
                    return lax.fori_loop(0, n, step, jnp.zeros((RET_DK, RET_DK), F32), unroll=True)

                gcf, gcb = jnp.exp(lgf * C), jnp.exp(lgb * C)
                g0 = scan(gcf, jnp.exp(lgf * (C - 1.0 - tau)), C - 1.0 - tau, sf_ref, dsf_ref,
                          lambda r: n - 1 - r, 0)
                through_state(crow, jnp.exp(lgf * (LC - 1.0 - jc)), LC - 1.0 - jc, g0, 0)
                g1 = scan(gcb, jnp.exp(lgb * tau), tau, sb_ref, dsb_ref, lambda r: r, 1)
                through_state(crow, jnp.exp(lgb * jc), jc, g1, 1)
                dk = dk_ref[:, ln]
                dk_ref[:, ln] = (dk * cos_ref[...] - pltpu.roll(dk, 64, 1) * sin_ref[...]) * kscale

        for h in range(4):
            one_head(h)

    sec = lambda k: pl.BlockSpec((None, TQ, 512), lambda b, i: (b, i, k))
    full = lambda k: pl.BlockSpec((None, T, 512), lambda b, i: (b, 0, k))
    dec_spec = pl.BlockSpec((1, 4), lambda b, i: (0, 0))
    tab = pl.BlockSpec((T, RET_DK), lambda b, i: (0, 0))
    return pl.pallas_call(
        body, name="ret_bwd", grid=(B, n),
        in_specs=[dec_spec, dec_spec, sec(4), full(5), full(6), sec(7),
                  pl.BlockSpec((1, 512), lambda b, i: (0, 0)), sec(0), sec(1), tab, tab, st_spec, st_spec],
        out_specs=(sec(0), sec(0), full(0), full(0),
                   pl.BlockSpec((None, 1, 512), lambda b, i: (b, 0, 0)),
                   pl.BlockSpec((None, 4, 8, 128), lambda b, i: (b, 0, 0, 0))),
        out_shape=(jax.ShapeDtypeStruct((B, L, 512), BF16), jax.ShapeDtypeStruct((B, L, 512), BF16),
                   jax.ShapeDtypeStruct((B, T, 512), F32), jax.ShapeDtypeStruct((B, T, 512), F32),
                   jax.ShapeDtypeStruct((B, 1, 512), F32), jax.ShapeDtypeStruct((B, 4, 8, 128), F32)),
        scratch_shapes=[pltpu.VMEM((4, n, RET_DK, RET_DK), F32), pltpu.VMEM((4, n, RET_DK, RET_DK), F32)],
        compiler_params=_params(("arbitrary",) * 2, vmem_mb=56))(
            dec_f, dec_b, P, P, P, P, ret_norm_g, o_ret, dY, cos2, sin2, sf, sb)


def _out_call(y_na, y_ret, x, target, mod, final_g, wout_f):
    B, L, _ = x.shape
    TO = 2 * TQ

    def body(yn_ref, yr_ref, x_ref, t_ref, mod_ref, gf_ref, w_ref, dy_ref, dx2_ref, dwb_ref, sm_ref, dw_ref):
        b, i = pl.program_id(0), pl.program_id(1)

        @pl.when((b == 0) & (i == 0))
        def _():
            dw_ref[...] = jnp.zeros_like(dw_ref)
            sm_ref[...] = jnp.zeros_like(sm_ref)

        gate = mod_ref[pl.ds(b, 1), 2 * D:3 * D]
        gf = gf_ref[...]
        yn, yr = yn_ref[...], yr_ref[...]
        ylat = _dot(yn, w_ref[0:512, :]) + _dot(yr, w_ref[512:1024, :])
        x2 = x_ref[...] + gate * ylat
        r = lax.rsqrt(jnp.mean(x2 * x2, axis=-1, keepdims=True) + EPS)
        xr = x2 * r
        err = xr * gf - t_ref[...]
        sm_ref[1:2, :] += jnp.sum(err * err, axis=0, keepdims=True)
        dout = err * (1.0 / D)
        sm_ref[0:1, :] += jnp.sum(dout * xr, axis=0, keepdims=True)
        gd = dout * gf
        dx2 = r * (gd - xr * jnp.mean(gd * xr, axis=-1, keepdims=True))
        dx2_ref[...] = dx2
        sm_ref[pl.ds(2 + b, 1), :] += jnp.sum(dx2 * ylat, axis=0, keepdims=True)
        dyl = (gate * dx2).astype(BF16)
        dy_ref[:, 0:512] = _dot_nt(dyl, w_ref[0:512, :]).astype(BF16)
        dy_ref[:, 512:1024] = _dot_nt(dyl, w_ref[512:1024, :]).astype(BF16)
        dw_ref[0:512, :] += _dot_tn(yn, dyl)
        dw_ref[512:1024, :] += _dot_tn(yr, dyl)

        @pl.when((b == B - 1) & (i == L // TO - 1))
        def _():
            dwb_ref[...] = dw_ref[...].astype(BF16)

    half = pl.BlockSpec((None, TO, 512), lambda b, i: (b, i, 0))
    full = pl.BlockSpec((None, TO, D), lambda b, i: (b, i, 0))
    return pl.pallas_call(
        body, name="out_proj_loss", grid=(B, L // TO),
        in_specs=[half, half, full, full,
                  pl.BlockSpec((8, 3 * D), lambda b, i: (0, 0)),
                  pl.BlockSpec((1, D), lambda b, i: (0, 0)),
                  pl.BlockSpec((D, D), lambda b, i: (0, 0))],
        out_specs=(full, full, pl.BlockSpec((D, D), lambda b, i: (0, 0)),
                   pl.BlockSpec((8, D), lambda b, i: (0, 0))),
        out_shape=(jax.ShapeDtypeStruct((B, L, D), BF16), jax.ShapeDtypeStruct((B, L, D), F32),
                   jax.ShapeDtypeStruct((D, D), BF16), jax.ShapeDtypeStruct((8, D), F32)),
        scratch_shapes=[pltpu.VMEM((D, D), F32)],
        compiler_params=_params(("arbitrary",) * 2))(y_na, y_ret, x, target, mod, final_g, wout_f)


def _dh_call(dsec, win_f, x, ctx, dx2, mod, norm_g, cp_in, cp_out):
    B, L, _ = x.shape
    LC = ctx.shape[1]
    nl = L // TQ

    def body(d0, d1, d2, d3, d4, d5, d6, d7, w_ref, x_ref, ctx_ref, dx2_ref, mod_ref, g_ref, cpi_ref, cpo_ref,
             gx_ref, sm_ref, sli_ref, slo_ref, cvi_ref, cvo_ref, svi_ref, svo_ref, ssem, rsem, lsem):
        drefs = (d0, d1, d2, d3, d4, d5, d6, d7)
        b, t = pl.program_id(0), pl.program_id(1)
        is_lat = t < nl

        @pl.when((b == 0) & (t == 0))
        def _():
            sm_ref[...] = jnp.zeros_like(sm_ref)

        def dh_of(secs):
            acc = jnp.zeros((TQ, D), F32)
            for sec in secs:
                s, half = divmod(sec, 2)
                acc = acc + _dot_nt(drefs[sec][...].astype(BF16), w_ref[s, :, half * 512:(half + 1) * 512])
            return acc

        def norm_bwd(dh, xt, mrow):
            scale = mrow[:, D:2 * D]
            g = g_ref[...]
            rstd = lax.rsqrt(jnp.mean(xt * xt, axis=-1, keepdims=True) + EPS)
            xn = xt * rstd
            dshift = jnp.sum(dh, axis=0, keepdims=True)
            dscale = jnp.sum(dh * (xn * g), axis=0, keepdims=True)
            dhn = dh * (1.0 + scale)
            sm_ref[0:1, :] += jnp.sum(dhn * xn, axis=0, keepdims=True)
            dxn = dhn * g
            dx = rstd * (dxn - xn * jnp.mean(dxn * xn, axis=-1, keepdims=True))
            return dshift, dscale, dx

        @pl.when(is_lat)
        def _():
            dshift, dscale, dx = norm_bwd(dh_of(range(8)), x_ref[...], mod_ref[pl.ds(b, 1), :])
            sm_ref[pl.ds(3 + b, 1), :] += dshift
            sm_ref[pl.ds(3 + B + b, 1), :] += dscale
            gx_ref[...] = dx2_ref[...] + dx

        @pl.when(jnp.logical_not(is_lat))
        def _():
            dshift, dscale, _ = norm_bwd(dh_of((1, 2, 5, 6)), ctx_ref[...], mod_ref[B:B + 1, :])
            sm_ref[1:2, :] += dshift
            sm_ref[2:3, :] += dscale

        mx, my, mc = _mesh_pos()
        s = 2 * mx + my
        cps, cvs, svs, sls = (cpi_ref, cpo_ref), (cvi_ref, cvo_ref), (svi_ref, svo_ref), (sli_ref, slo_ref)
        stage = [pltpu.make_async_copy(cps[a], cvs[a], lsem.at[a]) for a in range(2)]
        out = [pltpu.make_async_copy(svs[a], sls[a], lsem.at[2 + a]) for a in range(2)]
        sends, recvs, k = [], [], 0
        for px, py in _other_chips(mx, my):
            ps = 2 * px + py
            for a in range(2):
                sends.append(_remote(cvs[a].at[ps], svs[a].at[s], ssem, rsem, k, (px, py, mc)))
                recvs.append(_remote(cvs[a].at[s], svs[a].at[ps], ssem, rsem, k, (px, py, mc)))
                k += 1

        @pl.when((b == 0) & (t == 0))
        def _():
            for cp in stage:
                cp.start()
            for cp in stage:
                cp.wait()
            for cp in sends:
                cp.start()
            for a in range(2):
                pltpu.sync_copy(cvs[a].at[s], svs[a].at[s])

        @pl.when((b == B - 1) & (t == nl))
        def _():
            for cp in recvs:
                cp.wait_recv()
            for cp in out:
                cp.start()
            _finish(out, sends, [])

    lat = lambda b, t: (b, jnp.minimum(t, nl - 1), 0)
    tok = lambda b, t: (b, t, 0)
    sec_specs = [pl.BlockSpec((None, TQ, 512), lat if sec in (0, 3, 4, 7) else tok) for sec in range(8)]
    return pl.pallas_call(
        body, name="dh_norm_bwd", grid=(B, nl + 1),
        in_specs=sec_specs + [
            pl.BlockSpec((N_SHARD, D, D), lambda b, t: (0, 0, 0)),
            pl.BlockSpec((None, TQ, D), lat),
            pl.BlockSpec((None, LC, D), lambda b, t: (b, 0, 0)),
            pl.BlockSpec((None, TQ, D), lat),
            pl.BlockSpec((8, 3 * D), lambda b, t: (0, 0)),
            pl.BlockSpec((1, D), lambda b, t: (0, 0)), ANY, ANY],
        out_specs=(pl.BlockSpec((None, TQ, D), lat), pl.BlockSpec((8, D), lambda b, t: (0, 0)), ANY, ANY),
        out_shape=(jax.ShapeDtypeStruct((B, L, D), F32), jax.ShapeDtypeStruct((8, D), F32),
                   jax.ShapeDtypeStruct(cp_in.shape, cp_in.dtype), jax.ShapeDtypeStruct(cp_out.shape, cp_out.dtype)),
        scratch_shapes=[pltpu.VMEM(cp_in.shape, cp_in.dtype), pltpu.VMEM(cp_out.shape, cp_out.dtype),
                        pltpu.VMEM(cp_in.shape, cp_in.dtype), pltpu.VMEM(cp_out.shape, cp_out.dtype),
                        pltpu.SemaphoreType.DMA((6,)), pltpu.SemaphoreType.DMA((6,)),
                        pltpu.SemaphoreType.DMA((4,))],
        compiler_params=_params(("arbitrary",) * 2, vmem_mb=60))(
            *dsec, win_f, x, ctx, dx2, mod, norm_g, cp_in, cp_out)


def _dw_call(dsec, h, L):
    B, T, _ = h.shape
    TW = 2 * TQ
    nl = L // TW
    KV = (1, 2, 5, 6)

    def body(d0, d1, d2, d3, d4, d5, d6, d7, c1, c2, c5, c6, h_ref, hc_ref, dw_ref, acc_ref):
        drefs = (d0, d1, d2, d3, d4, d5, d6, d7)
        crefs = dict(zip(KV, (c1, c2, c5, c6)))
        b, t = pl.program_id(0), pl.program_id(1)

        @pl.when((b == 0) & (t == 0))
        def _():
            acc_ref[...] = jnp.zeros_like(acc_ref)

        def add(hb, refs, secs):
            for sec in secs:
                s, half = divmod(sec, 2)
                acc_ref[s, :, half * 512:(half + 1) * 512] += _dot_tn(hb, refs[sec][...].astype(BF16))

        @pl.when(t < nl)
        def _():
            add(h_ref[...], drefs, range(8))

        @pl.when(t == nl)
        def _():
            add(hc_ref[...], crefs, KV)

        @pl.when((b == B - 1) & (t == nl))
        def _():
            dw_ref[...] = acc_ref[...].astype(BF16)

    lat = lambda b, t: (b, jnp.minimum(t, nl - 1), 0)
    ctx = lambda b, t: (b, L // TQ, 0)
    return pl.pallas_call(
        body, name="dw_in", grid=(B, nl + 1),
        in_specs=[pl.BlockSpec((None, TW, 512), lat)] * 8 + [pl.BlockSpec((None, TQ, 512), ctx)] * 4
        + [pl.BlockSpec((None, TW, D), lat), pl.BlockSpec((None, TQ, D), ctx)],
        out_specs=pl.BlockSpec((N_SHARD, D, D), lambda b, t: (0, 0, 0)),
        out_shape=jax.ShapeDtypeStruct((N_SHARD, D, D), BF16),
        scratch_shapes=[pltpu.VMEM((N_SHARD, D, D), F32)],
        compiler_params=_params(("arbitrary",) * 2, vmem_mb=60))(*dsec, *[dsec[k] for k in KV], h, h)


def _mesh_pos():
    return lax.axis_index("x"), lax.axis_index("y"), lax.axis_index("c")


def _flip(v, f):
    return 1 - v if f else v


def _remote(src, dst, ssem, rsem, k, peer):
    return pltpu.make_async_remote_copy(src_ref=src, dst_ref=dst, send_sem=ssem.at[k], recv_sem=rsem.at[k],
                                        device_id=peer, device_id_type=MESH)


def _other_chips(x, y):
    return [(_flip(x, fx), _flip(y, fy)) for fx, fy in ((1, 0), (0, 1), (1, 1))]


def _all_to_all_small(src, dst_all, ssem, rsem, k0, x, y, cc):
    me = 4 * x + 2 * y + cc
    sends, recvs = [], []
    for f in range(1, N_DEV):
        px, py, pc = _flip(x, f & 4), _flip(y, f & 2), _flip(cc, f & 1)
        sends.append(_remote(src, dst_all.at[me], ssem, rsem, k0 + f - 1, (px, py, pc)))
        recvs.append(_remote(src, dst_all.at[4 * px + 2 * py + pc], ssem, rsem, k0 + f - 1, (px, py, pc)))
    return sends, recvs


def _finish(local, sends, recvs):
    for cp in recvs:
        cp.wait_recv()
    for cp in sends:
        cp.wait_send()
    for cp in local:
        cp.wait()


def _gather_call(wada_b, c, rpb_flat):
    arrs = (wada_b,)
    na = len(arrs)
    hrs = [a.shape[0] // 2 for a in arrs]

    def body(wada, c_ref, r_ref, wada_f, c_all, bias_out, bias_ref, et_ref, ssem, rsem, lsem):
        x, y, cc = _mesh_pos()
        s, me = 2 * x + y, 4 * x + 2 * y + cc
        sib = (x, y, 1 - cc)
        srcs, dsts = (wada,), (wada_f,)

        def half(a, shard, hc):
            return dsts[a].at[shard, pl.ds(hc * hrs[a], hrs[a])]

        local = [pltpu.make_async_copy(srcs[a], dsts[a].at[s], lsem.at[a]) for a in range(na)]
        local.append(pltpu.make_async_copy(c_ref, c_all.at[me], lsem.at[na]))
        ici_send, ici_recv, fwd_send, fwd_recv, k = [], [], [], [], 0
        for px, py in _other_chips(x, y):
            ps = 2 * px + py
            for a in range(na):
                mine = srcs[a].at[pl.ds(cc * hrs[a], hrs[a])]
                ici_send.append(_remote(mine, half(a, s, cc), ssem, rsem, k, (px, py, cc)))
                ici_recv.append(_remote(mine, half(a, ps, cc), ssem, rsem, k, (px, py, cc)))
                fwd_send.append(_remote(half(a, ps, cc), half(a, ps, cc), ssem, rsem, 3 * na + k, sib))
                fwd_recv.append(_remote(half(a, ps, 1 - cc), half(a, ps, 1 - cc), ssem, rsem, 3 * na + k, sib))
                k += 1
        c_send, c_recv = _all_to_all_small(c_ref, c_all, ssem, rsem, 6 * na, x, y, cc)
        for cp in local + ici_send + c_send:
            cp.start()
        bias_out_copies = _bias_body(r_ref, bias_ref, et_ref, bias_out, lsem.at[na + 1])
        for got, fwd in zip(ici_recv, fwd_send):
            got.wait_recv()
            fwd.start()
        _finish(local + bias_out_copies, ici_send + fwd_send + c_send, fwd_recv + c_recv)

    bias_shape = (rpb_flat.shape[0], 3, TQ, KW)
    return pl.pallas_call(
        body, name="weight_gather",
        in_specs=[pl.BlockSpec(memory_space=pltpu.VMEM)] * 2 + [pl.BlockSpec(memory_space=pltpu.SMEM)],
        out_specs=(pl.BlockSpec(memory_space=pltpu.VMEM),) * 2 + (ANY,),
        out_shape=tuple(jax.ShapeDtypeStruct((N_SHARD,) + a.shape, a.dtype) for a in arrs)
        + (jax.ShapeDtypeStruct((N_DEV,) + c.shape, c.dtype), jax.ShapeDtypeStruct(bias_shape, F32)),
        scratch_shapes=[pltpu.VMEM(bias_shape, F32), pltpu.VMEM((15, GRID_W, GRID_W), F32),
                        pltpu.SemaphoreType.DMA((6 * na + 7,)), pltpu.SemaphoreType.DMA((6 * na + 7,)),
                        pltpu.SemaphoreType.DMA((na + 2,))],
        compiler_params=pltpu.CompilerParams(vmem_limit_bytes=56 << 20))(wada_b, c, rpb_flat)


VROWS = 32


def _grad_halves_call(dwin_b, dwout_b, dbias, dlg):
    arrs = (dwin_b, dwout_b)
    hrs = [a.shape[1] // 2 for a in arrs]

    def body(din, dout, db_ref, dlg_ref, cp_in, cp_out, drpb_ref, dlgo_ref, got_in, got_out, p_ref, ssem, rsem):
        x, y, cc = _mesh_pos()
        sib = (x, y, 1 - cc)
        srcs, gots, cps = (din, dout), (got_in, got_out), (cp_in, cp_out)
        halves = [_remote(srcs[a].at[:, pl.ds((1 - cc) * hrs[a], hrs[a])], gots[a], ssem, rsem, a, sib)
                  for a in range(2)]
        for cp in halves:
            cp.start()
        _small_reduce_body(db_ref, dlg_ref, drpb_ref, dlgo_ref, p_ref)
        for cp in halves:
            cp.wait_recv()
        for a in range(2):
            for j in range(N_SHARD):
                def add(i, carry, a=a, j=j):
                    r = pl.multiple_of(i * VROWS, VROWS)
                    mine = srcs[a][j, pl.ds(pl.multiple_of(cc * hrs[a] + r, VROWS), VROWS), :].astype(F32)
                    cps[a][j, pl.ds(r, VROWS), :] = (
                        mine + gots[a][j, pl.ds(r, VROWS), :].astype(F32)).astype(BF16)
                    return carry
                lax.fori_loop(0, hrs[a] // VROWS, add, 0)
        for cp in halves:
            cp.wait_send()

    vmem = pl.BlockSpec(memory_space=pltpu.VMEM)
    half_shapes = [(N_SHARD, hrs[a], arrs[a].shape[2]) for a in range(2)]
    return pl.pallas_call(
        body, name="grad_halves",
        in_specs=[vmem] * 4, out_specs=(vmem,) * 4,
        out_shape=(jax.ShapeDtypeStruct(half_shapes[0], BF16), jax.ShapeDtypeStruct(half_shapes[1], BF16),
                   jax.ShapeDtypeStruct((dbias.shape[0], 16, 32), F32), jax.ShapeDtypeStruct((32, 128), F32)),
        scratch_shapes=[pltpu.VMEM(half_shapes[0], BF16), pltpu.VMEM(half_shapes[1], BF16),
                        pltpu.VMEM((32, GRID_W), F32),
                        pltpu.SemaphoreType.DMA((2,)), pltpu.SemaphoreType.DMA((2,))],
        compiler_params=pltpu.CompilerParams(vmem_limit_bytes=56 << 20))(dwin_b, dwout_b, dbias, dlg)


def _grad_finish_call(sl_in, sl_out, small):
    arrs = (sl_in, sl_out)

    def body(sin, sout, sm, gin, gout, sm_all, h_in, h_out, ssem, rsem, lsem):
        x, y, cc = _mesh_pos()
        me = 4 * x + 2 * y + cc
        sib = (x, y, 1 - cc)
        sls, hs, gs = (sin, sout), (h_in, h_out), (gin, gout)
        sm_send, sm_recv = _all_to_all_small(sm, sm_all, ssem, rsem, 2, x, y, cc)
        sm_own = pltpu.make_async_copy(sm, sm_all.at[me], lsem.at[0])
        for cp in sm_send + [sm_own]:
            cp.start()
        for a in range(2):
            def total(i, carry, a=a):
                rows = pl.ds(pl.multiple_of(i * VROWS, VROWS), VROWS)
                sl = sls[a]
                hs[a][rows, :] = ((sl[0, rows, :].astype(F32) + sl[1, rows, :].astype(F32))
                                  + sl[2, rows, :].astype(F32)) + sl[3, rows, :].astype(F32)
                return carry
            lax.fori_loop(0, arrs[a].shape[1] // VROWS, total, 0)
        mine = [pltpu.make_async_copy(hs[a], gs[a].at[cc], lsem.at[1 + a]) for a in range(2)]
        back = [_remote(hs[a], gs[a].at[cc], ssem, rsem, a, sib) for a in range(2)]
        back_recv = [_remote(hs[a], gs[a].at[1 - cc], ssem, rsem, a, sib) for a in range(2)]
        for cp in mine + back:
            cp.start()
        _finish(mine + [sm_own], back + sm_send, back_recv + sm_recv)

    vmem = pl.BlockSpec(memory_space=pltpu.VMEM)
    return pl.pallas_call(
        body, name="grad_finish",
        in_specs=[vmem] * 3, out_specs=(vmem,) * 3,
        out_shape=(jax.ShapeDtypeStruct((2,) + sl_in.shape[1:], F32),
                   jax.ShapeDtypeStruct((2,) + sl_out.shape[1:], F32),
                   jax.ShapeDtypeStruct((N_DEV,) + small.shape, F32)),
        scratch_shapes=[pltpu.VMEM(sl_in.shape[1:], F32), pltpu.VMEM(sl_out.shape[1:], F32),
                        pltpu.SemaphoreType.DMA((9,)), pltpu.SemaphoreType.DMA((9,)),
                        pltpu.SemaphoreType.DMA((3,))],
        compiler_params=pltpu.CompilerParams(vmem_limit_bytes=48 << 20))(sl_in, sl_out, small)


def _adamw(w, g, m, v):
    m = ADAM_B1 * m + (1.0 - ADAM_B1) * g
    v = ADAM_B2 * v + (1.0 - ADAM_B2) * (g * g)
    m_hat = m / (1.0 - ADAM_B1 ** ADAM_STEP)
    v_hat = v / (1.0 - ADAM_B2 ** ADAM_STEP)
    return -ADAM_LR * (m_hat / (jnp.sqrt(v_hat) + ADAM_EPS) + ADAM_WD * w), m, v


def _adam_call(w, m, v, g, name):
    R, C = w.shape
    tr = 256

    def body(w_ref, m_ref, v_ref, g_ref, go_ref, d_ref, mo_ref, vo_ref):
        g = g_ref[...]
        go_ref[...] = g
        d_ref[...], mo_ref[...], vo_ref[...] = _adamw(w_ref[...], g, m_ref[...], v_ref[...])

    spec = pl.BlockSpec((tr, C), lambda i: (i, 0))
    return pl.pallas_call(
        body, name=name, grid=(R // tr,), in_specs=[spec] * 4,
        out_specs=(spec,) * 4, out_shape=(jax.ShapeDtypeStruct((R, C), F32),) * 4,
        compiler_params=_params(("arbitrary",)))(w, m, v, g)


R_GF, R_NG, R_LOSS, R_RNG, R_LGF, R_LGB, R_SHIFT, R_SCALE, R_GATE, R_SHIFT_C, R_SCALE_C, R_RNG2, R_RPB = (
    0, 1, 2, 3, 4, 5, 6, 8, 10, 12, 13, 14, 16)
W_GF, W_NG, W_CCTX, W_RNG, W_DF, W_DB, W_BADA, W_RPB = 0, 1, 2, 3, 4, 5, 6, 9


SMALL = (("final_norm_g", W_GF, 1, D), ("norm_g", W_NG, 1, D), ("c_ctx", W_CCTX, 1, D),
         ("ret_norm_g", W_RNG, 1, 512), ("ret_decay_fwd", W_DF, 1, 4), ("ret_decay_bwd", W_DB, 1, 4),
         ("b_ada", W_BADA, 3, D), ("na_rpb", W_RPB, 4, D))
N_SMALL = len(SMALL)


def _small_final_call(sm_all, c_t, wada_f, wada, m_ada, v_ada, small_w, small_m, small_v, B):
    ws = wada.shape[1]
    NB = N_DEV * B

    def body(*refs):
        sm_ref, ct_ref, wf_ref, wa_ref, ma_ref, va_ref = refs[:6]
        ins = refs[6:6 + 3 * N_SMALL]
        outs = refs[6 + 3 * N_SMALL:6 + 7 * N_SMALL]
        ga_ref, da_ref, mao_ref, vao_ref, loss_ref, dmod_ref, pk_ref = refs[6 + 7 * N_SMALL:]
        x, y, _ = _mesh_pos()
        s = 2 * x + y
        tot = sm_ref[0]
        for dv in range(1, N_DEV):
            tot = tot + sm_ref[dv]
        pk_ref[...] = jnp.zeros_like(pk_ref)
        for kind in range(3):
            for i, (_, row, nrow, width) in enumerate(SMALL):
                ref = ins[kind * N_SMALL + i]
                if nrow == 3:
                    for part in range(3):
                        pk_ref[kind, row + part:row + part + 1, :] = ref[:, part * D:(part + 1) * D]
                else:
                    pk_ref[kind, row:row + nrow, 0:width] = ref[...]
        w = pk_ref[0]
        cctx_ref = ins[2]
        for dv in range(N_DEV):
            for b in range(B):
                r = dv * B + b
                for part, row in enumerate((R_SHIFT, R_SCALE, R_GATE)):
                    dmod_ref[r:r + 1, part * D:(part + 1) * D] = sm_ref[dv, row + b:row + b + 1, :]
        dmod_ref[NB:NB + 1, 0:D] = tot[R_SHIFT_C:R_SHIFT_C + 1, :]
        dmod_ref[NB:NB + 1, D:2 * D] = tot[R_SCALE_C:R_SCALE_C + 1, :]
        dmod_ref[NB:NB + 1, 2 * D:3 * D] = jnp.zeros((1, D), F32)
        dmod_ref[NB + 1:, :] = jnp.zeros((dmod_ref.shape[0] - NB - 1, 3 * D), F32)
        dmod = dmod_ref[...]
        cc = cctx_ref[...]
        scc = _sigmoid(cc)
        ct = ct_ref[...]
        act_t = ct * _sigmoid(ct)
        dmc = dmod[NB:NB + 1, :].astype(BF16)
        dact = jnp.zeros((1, D), F32)
        for sh in range(N_SHARD):
            dact = dact + _dot_nt(dmc[:, sh * ws:(sh + 1) * ws], wf_ref[sh])
        g = jnp.zeros((16, D), F32)
        rows = lax.broadcasted_iota(jnp.int32, (16, D), 0)

        def put(g, row, val):
            return jnp.where(rows == row, val, g)

        g = put(g, W_GF, tot[R_GF:R_GF + 1, :])
        g = put(g, W_NG, tot[R_NG:R_NG + 1, :])
        g = put(g, W_CCTX, dact * (scc * (1.0 + cc * (1.0 - scc))))
        g = put(g, W_RNG, tot[R_RNG:R_RNG + 1, :] + tot[R_RNG2:R_RNG2 + 1, :])
        g = put(g, W_DF, tot[R_LGF:R_LGF + 1, :] * (-jnp.exp(w[W_DF:W_DF + 1, :])))
        g = put(g, W_DB, tot[R_LGB:R_LGB + 1, :] * (-jnp.exp(w[W_DB:W_DB + 1, :])))
        db = jnp.sum(dmod, axis=0, keepdims=True)
        for part in range(3):
            g = put(g, W_BADA + part, db[:, part * D:(part + 1) * D])
        for part in range(4):
            g = put(g, W_RPB + part, tot[R_RPB + part:R_RPB + part + 1, :])
        for kind, val in enumerate((g,) + _adamw(w, g, pk_ref[1], pk_ref[2])):
            for i, (_, row, nrow, width) in enumerate(SMALL):
                out = outs[kind * N_SMALL + i]
                if nrow == 3:
                    for part in range(3):
                        out[:, part * D:(part + 1) * D] = val[row + part:row + part + 1, :]
                else:
                    out[...] = val[row:row + nrow, 0:width]
        loss_ref[...] = jnp.broadcast_to(
            (0.5 / D) * jnp.sum(tot[R_LOSS:R_LOSS + 1, :], axis=1, keepdims=True), (8, 128))
        for sh in range(N_SHARD):
            @pl.when(s == sh)
            def _():
                ga = jnp.dot(act_t, dmod[:, sh * ws:(sh + 1) * ws], precision=HIGHEST,
                             preferred_element_type=F32)
                ga_ref[...] = ga
                da_ref[...], mao_ref[...], vao_ref[...] = _adamw(wa_ref[...], ga, ma_ref[...], va_ref[...])

    sh_small = tuple(jax.ShapeDtypeStruct(a.shape, F32) for a in small_w)
    sh_ada = jax.ShapeDtypeStruct(wada.shape, F32)
    res = pl.pallas_call(
        body, name="small_final",
        out_shape=sh_small * 4 + (sh_ada,) * 4 + (jax.ShapeDtypeStruct((8, 128), F32),),
        scratch_shapes=[pltpu.VMEM((NB + 8, 3 * D), F32), pltpu.VMEM((3, 16, D), F32)],
        compiler_params=_params(vmem_mb=56))(
            sm_all, c_t, wada_f, wada, m_ada, v_ada, *small_w, *small_m, *small_v)
    smalls = [res[k * N_SMALL:(k + 1) * N_SMALL] for k in range(4)]
    return smalls, res[4 * N_SMALL:4 * N_SMALL + 4], res[4 * N_SMALL + 4]


def _local_step(order, x, c, ctx, c_ctx, norm_g, wada_f, b_ada, win_b, bias, dec_f, dec_b, ret_norm_g,
                wout_b, final_g, target):
    B, L, _ = x.shape
    LC = ctx.shape[1]
    assert B == 2
    cos2, sin2 = _rope_tables(L, LC)
    c8 = jnp.concatenate([c, c_ctx[None, :], jnp.zeros((8 - B - 1, D), F32)], axis=0)
    mod = _mod_call(c8, wada_f, b_ada)
    P, h, win_f, wout_f = _inproj_gather_call(order, x, ctx, mod, norm_g, win_b, wout_b, cos2, sin2)
    y_na, o_na = _na_fwd_call(P, bias, L, LC)
    sf, sb = _ret_states_call(P, dec_f, dec_b, L, LC)
    y_ret, o_ret = _retc_fwd_call(P, sf, sb, dec_f, dec_b, ret_norm_g, L)
    dY, dx2, dwout_p, sm_out = _out_call(y_na, y_ret, x, target, mod, final_g, wout_f.reshape(D, D))
    dnq, dng, dnk, dnv, dbias = _na_bwd_call(P, bias, dY, o_na, L, LC)
    drq, drg, drk, drv, dgn, dlg = _retc_bwd_call(P, sf, sb, dec_f, dec_b, ret_norm_g, o_ret, dY, cos2, sin2, L, LC)
    dsec = (dnq, dnk, dnv, dng, drq, drk, drv, drg)
    dwin_b = _dw_call(dsec, h, L)
    cp_in, cp_out, drpb, dlg_sum = _grad_halves_call(
        dwin_b, dwout_p.reshape(N_SHARD, D // N_SHARD, D), dbias, dlg)
    grad_x, sm_dh, sl_in, sl_out = _dh_call(dsec, win_f, x, ctx, dx2, mod, norm_g, cp_in, cp_out)
    z = jnp.zeros((1, D), F32)
    pad = lambda v: jnp.pad(v.reshape(1, -1), ((0, 0), (0, D - v.size)))
    dlg_sum = dlg_sum.reshape(4, 8, 128)
    rpb_rows = jnp.pad(drpb[:, :15, :31].reshape(-1), (0, 4 * D - drpb.shape[0] * 465)).reshape(4, D)
    small = jnp.concatenate([
        sm_out[0:1], sm_dh[0:1], sm_out[1:2], pad(dgn[0]), pad(dlg_sum[:, 0, 0]), pad(dlg_sum[:, 1, 0]),
        sm_dh[3:5], sm_dh[5:7], sm_out[2:4], sm_dh[1:2], sm_dh[2:3], pad(dgn[1]), z, rpb_rows,
        jnp.zeros((SM_ROWS - 20, D), F32)], axis=0)
    return grad_x, sl_in, sl_out, small


def kernel(x, c, ctx, c_ctx, norm_g, w_ada, b_ada, w_in, na_rpb, ret_decay_fwd, ret_decay_bwd, ret_norm_g, w_out, final_norm_g, loss_target, m_c_ctx, m_norm_g, m_w_ada, m_b_ada, m_w_in, m_na_rpb, m_ret_decay_fwd, m_ret_decay_bwd, m_ret_norm_g, m_w_out, m_final_norm_g, v_c_ctx, v_norm_g, v_w_ada, v_b_ada, v_w_in, v_na_rpb, v_ret_decay_fwd, v_ret_decay_bwd, v_ret_norm_g, v_w_out, v_final_norm_g):
    B = x.shape[0]
    wada_f, c_all, bias = _gather_call(w_ada[0].astype(BF16), c, na_rpb[0].reshape(na_rpb.shape[1], -1))
    mx, my = lax.axis_index("x"), lax.axis_index("y")
    order = jnp.stack([2 * mx + my, 2 * (1 - mx) + my, 2 * mx + (1 - my),
                       2 * (1 - mx) + (1 - my)]).astype(jnp.int32)
    grad_x, sl_in, sl_out, small = _local_step(
        order, x, c, ctx, c_ctx, norm_g, wada_f, b_ada, w_in[0].astype(BF16), bias, ret_decay_fwd,
        ret_decay_bwd, ret_norm_g, w_out[0].astype(BF16), final_norm_g.reshape(1, D), loss_target)
    gin, gout, sm_all = _grad_finish_call(sl_in, sl_out, small)
    g_win, d_win, nm_win, nv_win = _adam_call(
        w_in[0], m_w_in[0], v_w_in[0], gin.reshape(w_in.shape[1:]), "adam_w_in")
    g_wout, d_wout, nm_wout, nv_wout = _adam_call(
        w_out[0], m_w_out[0], v_w_out[0], gout.reshape(w_out.shape[1:]), "adam_w_out")

    def small_inputs(gf, ng, cc, rng, df, db, bada, rpb):
        return (gf.reshape(1, D), ng, cc.reshape(1, D), rng, df, db, bada,
                jnp.pad(rpb.reshape(-1), (0, 4 * D - rpb.size)).reshape(4, D))

    c_t = jnp.concatenate([c_all.reshape(N_DEV * B, D), c_ctx.reshape(1, D), jnp.zeros((7, D), F32)], axis=0).T
    smalls, adas, loss = _small_final_call(
        sm_all, c_t, wada_f, w_ada[0], m_w_ada[0], v_w_ada[0],
        small_inputs(final_norm_g, norm_g, c_ctx, ret_norm_g, ret_decay_fwd, ret_decay_bwd, b_ada, na_rpb),
        small_inputs(m_final_norm_g, m_norm_g, m_c_ctx, m_ret_norm_g, m_ret_decay_fwd, m_ret_decay_bwd, m_b_ada,
                     m_na_rpb),
        small_inputs(v_final_norm_g, v_norm_g, v_c_ctx, v_ret_norm_g, v_ret_decay_fwd, v_ret_decay_bwd, v_b_ada,
                     v_na_rpb), B)
    res = []
    for p, ada, win_o, wout_o in zip(smalls, adas, (g_win, d_win, nm_win, nv_win),
                                     (g_wout, d_wout, nm_wout, nv_wout)):
        gf, ng, cc, rng, df, db, bada, rpb = p
        res.append([cc.reshape(D), ng, ada[None], bada, win_o[None],
                    rpb.reshape(-1)[:na_rpb.size].reshape(na_rpb.shape), df, db, rng, wout_o[None], gf.reshape(D)])
    return (loss[0, 0], grad_x, *res[0], *res[1], *res[2], *res[3])
```

```python
import numpy as np
import jax
import jax.numpy as jnp
from jax import lax
from jax.experimental import pallas as pl
from jax.experimental.pallas import tpu as pltpu

F32 = jnp.float32
BF16 = jnp.bfloat16
HIGHEST = lax.Precision.HIGHEST

D = 1024
GRID_W = 64
NA_DH = 64
RET_DK = 128
ROPE_BASE = 10000.0
EPS = 1e-6
NEG = -1e30
TQ = 256
KW = 12 * GRID_W
N_SHARD = 4
N_DEV = 8
SM_ROWS = 24

ADAM_LR = 0.001
ADAM_B1 = 0.9
ADAM_B2 = 0.999
ADAM_EPS = 1e-08
ADAM_WD = 0.01
ADAM_STEP = 10

MESH = pl.DeviceIdType.MESH
ANY = pl.BlockSpec(memory_space=pl.ANY)


def _params(sem=None, vmem_mb=48):
    return pltpu.CompilerParams(dimension_semantics=sem, vmem_limit_bytes=vmem_mb << 20)


def _dot(a, b):
    return jnp.dot(a, b, preferred_element_type=F32)


def _dot_nt(a, b):
    return lax.dot_general(a, b, (((1,), (1,)), ((), ())), preferred_element_type=F32)


def _dot_tn(a, b):
    return lax.dot_general(a, b, (((0,), (0,)), ((), ())), preferred_element_type=F32)


def _sigmoid(x):
    return 1.0 / (1.0 + jnp.exp(-x))


def _rope_tables(L, LC):
    half = RET_DK // 2
    nf = half // 2
    t = np.arange(L)
    row = (t // GRID_W).astype(np.float32)
    col = (t % GRID_W).astype(np.float32)
    inv = (np.float32(ROPE_BASE) ** (-np.arange(nf, dtype=np.float32) / np.float32(nf))).astype(np.float32)
    ang = np.concatenate([row[:, None] * inv, col[:, None] * inv], axis=-1).astype(np.float32)
    cos, sin = np.cos(ang).astype(np.float32), np.sin(ang).astype(np.float32)
    cos2 = np.concatenate([cos, cos], axis=-1)
    sin2 = np.concatenate([-sin, sin], axis=-1)
    cos2 = np.concatenate([cos2, np.ones((LC, RET_DK), np.float32)], axis=0)
    sin2 = np.concatenate([sin2, np.zeros((LC, RET_DK), np.float32)], axis=0)
    return jnp.asarray(cos2), jnp.asarray(sin2)


def _mod_part_call(c_rows, wada_b, b_shard):
    def body(c_ref, w_ref, b_ref, o_ref):
        a = c_ref[...]
        o_ref[...] = _dot((a * _sigmoid(a)).astype(BF16), w_ref[...]) + b_ref[...]

    return pl.pallas_call(
        body, name="ada_mod", out_shape=jax.ShapeDtypeStruct((c_rows.shape[0], wada_b.shape[1]), F32),
        compiler_params=_params())(c_rows, wada_b, b_shard)


def _dc_masks():
    cq = lax.broadcasted_iota(jnp.int32, (GRID_W, GRID_W), 0)
    ck = lax.broadcasted_iota(jnp.int32, (GRID_W, GRID_W), 1)
    dc = jnp.clip(ck - cq + 15, 0, 30)
    c0 = jnp.clip(cq - 8, 0, GRID_W - 16)
    col_ok = (ck >= c0) & (ck < c0 + 16)
    return dc, col_ok


def _bias_blocks():
    out = []
    for typ, delta in enumerate((4, 0, -4)):
        for rq in range(4):
            for rkk in range(12):
                dr = rkk + delta - rq - 4
                if typ == 0:
                    ok = -rq <= dr <= 7 - rq
                elif typ == 1:
                    ok = -4 <= dr <= 3
                else:
                    ok = -4 - rq <= dr <= 3 - rq
                out.append((typ, rq, rkk, dr if ok else None))
    return out


def _bias_body(r_ref, bias_ref, et_ref, out_ref, sem):
    dc, col_ok = _dc_masks()
    masks = [(dc == j).astype(F32) for j in range(31)]
    nh = bias_ref.shape[0]

    def per_h(h, carry):
        for dr in range(15):
            t = jnp.zeros((GRID_W, GRID_W), F32)
            for j in range(31):
                t = t + masks[j] * r_ref[h, dr * 31 + j]
            et_ref[dr] = jnp.where(col_ok, t, NEG)
        neg = jnp.full((GRID_W, GRID_W), NEG, F32)
        for typ, rq, rkk, dr in _bias_blocks():
            blk = neg if dr is None else et_ref[dr + 7]
            bias_ref[h, typ, rq * 64:(rq + 1) * 64, rkk * 64:(rkk + 1) * 64] = blk
        pltpu.make_async_copy(bias_ref.at[h], out_ref.at[h], sem).start()
        return carry

    lax.fori_loop(0, nh, per_h, 0)
    return [pltpu.make_async_copy(bias_ref.at[h], out_ref.at[h], sem) for h in range(nh)]


def _bias_tile_sums(db_ref, hh):
    acc = {}
    for typ, rq, rkk, dr in _bias_blocks():
        if dr is None:
            continue
        blk = db_ref[hh, typ, rq * 64:(rq + 1) * 64, rkk * 64:(rkk + 1) * 64]
        acc[dr] = blk if dr not in acc else acc[dr] + blk
    return acc


def _small_reduce_body(dt_ref, dlg_ref, drpb_ref, dlgo_ref, p_ref):
    dc, _ = _dc_masks()
    masks = [(dc == j).astype(F32) for j in range(31)]
    ones = jnp.ones((8, GRID_W), F32)
    p_ref[...] = jnp.zeros_like(p_ref)
    drpb_ref[...] = jnp.zeros_like(drpb_ref)

    def per_h(h, carry):
        for dr in range(-7, 8):
            t = dt_ref[h, dr + 7]
            for j in range(31):
                p_ref[j:j + 1, :] = jnp.sum(t * masks[j], axis=0, keepdims=True)
            red = lax.dot_general(ones, p_ref[...], (((1,), (1,)), ((), ())),
                                  precision=HIGHEST, preferred_element_type=F32)
            drpb_ref[h, dr + 7:dr + 8, :] = red[0:1, :]
        return carry

    lax.fori_loop(0, dt_ref.shape[0], per_h, 0)
    x = dlg_ref[0]
    for b in range(1, dlg_ref.shape[0]):
        x = x + dlg_ref[b]
    x = x.reshape(4 * 8, x.shape[-1])
    dlgo_ref[...] = jnp.dot(x, jnp.ones((x.shape[-1], 128), F32), precision=HIGHEST,
                            preferred_element_type=F32)


def _inproj_gather_call(order, x, ctx, mod_part, norm_g, win_b, wout_b, cos2, sin2):
    B, L, _ = x.shape
    LC = ctx.shape[1]
    T = L + LC
    TI = 2 * TQ
    nl = L // TI
    nt = nl + 1
    assert LC == TQ and L % TI == 0
    kscale = RET_DK ** -0.5
    HR = D // 2
    pad_rows = nt * TI - T
    cos2 = jnp.pad(cos2, ((0, pad_rows), (0, 0)))
    sin2 = jnp.pad(sin2, ((0, pad_rows), (0, 0)))

    MW = mod_part.shape[1]
    NB = N_DEV * B

    def body(ord_ref, x_ref, ctx_ref, mp_ref, g_ref, wown_ref, woown_ref, cos_ref, sin_ref,
             p_ref, h_ref, wf_ref, wof_ref, modo_ref, w_all, wo_all, hs_ref, mp_all, mod_ref, ssem, rsem, lsem):
        j, b, t = pl.program_id(0), pl.program_id(1), pl.program_id(2)
        first = (b == 0) & (t == 0)
        mx, my, mc = _mesh_pos()
        s = 2 * mx + my

        m_send = [_remote(mp_ref, mp_all.at[s], ssem, rsem, 12 + k, (px, py, mc))
                  for k, (px, py) in enumerate(_other_chips(mx, my))]
        m_recv = [_remote(mp_ref, mp_all.at[2 * px + py], ssem, rsem, 12 + k, (px, py, mc))
                  for k, (px, py) in enumerate(_other_chips(mx, my))]

        @pl.when(first & (j == 0))
        def _():
            for cp in m_send:
                cp.start()
            mp_all[s] = mp_ref[...]
            for cp in m_recv:
                cp.wait_recv()
            me = 4 * mx + 2 * my + mc
            mod_ref[...] = jnp.zeros_like(mod_ref)
            for p in range(N_SHARD):
                for r in range(B):
                    mod_ref[r:r + 1, p * MW:(p + 1) * MW] = mp_all[p, pl.ds(B * me + r, 1), :]
                mod_ref[B:B + 1, p * MW:(p + 1) * MW] = mp_all[p, NB:NB + 1, :]
            modo_ref[...] = mod_ref[...]

        sems = (ssem, rsem, lsem)
        own, ici_send, ici_recv, fwd_send, fwd_recv, outs = _gather_copies(wown_ref, w_all, wf_ref, HR, sems, 0, 0)
        oown, o_send, o_recv, o_fsend, o_frecv, o_outs = _gather_copies(
            woown_ref, wo_all, wof_ref, woown_ref.shape[0] // 2, sems, 6, 5)

        @pl.when(first & (j == 0))
        def _():
            own.start()
            oown.start()
            own.wait()
            ici_send[0].start()
            ici_send[1].start()
            outs[0].start()
            oown.wait()

        for k in range(3):
            @pl.when(first & (j == k + 1))
            def _(k=k):
                ici_recv[k].wait_recv()
                if k == 0:
                    ici_send[2].start()
                fwd_send[k].start()
                fwd_recv[k].wait_recv()
                outs[1 + k].start()
                if k == 1:
                    for cp in o_send:
                        cp.start()
                if k == 2:
                    for got, fwd in zip(o_recv, o_fsend):
                        got.wait_recv()
                        fwd.start()

        tile = b * nt + t

        @pl.when(j == 0)
        def _():
            is_lat = t < nl
            ctx_tile = jnp.concatenate([ctx_ref[...], jnp.zeros((TI - LC, D), F32)], axis=0)
            xt = jnp.where(is_lat, x_ref[...], ctx_tile)
            mrow = mod_ref[pl.ds(jnp.where(is_lat, b, B), 1), :]
            shift, scale = mrow[:, 0:D], mrow[:, D:2 * D]
            rstd = lax.rsqrt(jnp.mean(xt * xt, axis=-1, keepdims=True) + EPS)
            h0 = ((xt * rstd * g_ref[...]) * (1.0 + scale) + shift).astype(BF16)
            h_ref[...] = h0
            hs_ref[tile] = h0

        hb = hs_ref[tile]
        cs, sn = cos_ref[...], sin_ref[...]
        shard = ord_ref[j]
        for sh in range(N_SHARD):
            @pl.when(shard == sh)
            def _(sh=sh):
                for half in range(2):
                    sec = 2 * sh + half
                    acc = _dot(hb, w_all[sh, :, half * 512:(half + 1) * 512])
                    if sec == 0:
                        acc = acc * (NA_DH ** -0.5)
                    if sec in (4, 5):
                        for q in range(4):
                            a = acc[:, q * 128:(q + 1) * 128]
                            r = a * cs + pltpu.roll(a, 64, 1) * sn
                            if sec == 5:
                                r = r * kscale
                            p_ref[:, half * 512 + q * 128:half * 512 + (q + 1) * 128] = r.astype(BF16)
                    else:
                        p_ref[:, half * 512:(half + 1) * 512] = acc.astype(BF16)

        @pl.when((j == N_SHARD - 1) & (b == B - 1) & (t == nt - 1))
        def _():
            for cp in o_frecv:
                cp.wait_recv()
            for cp in o_outs:
                cp.start()
            _finish(outs + o_outs, ici_send + fwd_send + o_send + o_fsend + m_send, [])

    tok = lambda j, b, t, o: (jnp.where(j == 0, b, B - 1), jnp.where(j == 0, jnp.minimum(t, nl - 1), nl - 1), 0)
    grid_spec = pltpu.PrefetchScalarGridSpec(
        num_scalar_prefetch=1, grid=(N_SHARD, B, nt),
        in_specs=[
            pl.BlockSpec((None, TI, D), tok),
            pl.BlockSpec((None, LC, D), lambda j, b, t, o: (jnp.where(j == 0, b, B - 1), 0, 0)),
            pl.BlockSpec(mod_part.shape, lambda j, b, t, o: (0, 0)),
            pl.BlockSpec((1, D), lambda j, b, t, o: (0, 0)),
            ANY, ANY,
            pl.BlockSpec((TI, RET_DK), lambda j, b, t, o: (t, 0)),
            pl.BlockSpec((TI, RET_DK), lambda j, b, t, o: (t, 0)),
        ],
        out_specs=(pl.BlockSpec((None, TI, D), lambda j, b, t, o: (b, t, o[j])),
                   pl.BlockSpec((None, TI, D), lambda j, b, t, o: (
                       jnp.where(j == 0, b, B - 1), jnp.where(j == 0, t, nt - 1), 0)), ANY, ANY,
                   pl.BlockSpec((8, 3 * D), lambda j, b, t, o: (0, 0))),
        scratch_shapes=[pltpu.VMEM((N_SHARD, D, D), BF16), pltpu.VMEM((N_SHARD,) + wout_b.shape, BF16),
                        pltpu.VMEM((B * nt, TI, D), BF16),
                        pltpu.VMEM((N_SHARD,) + mod_part.shape, F32), pltpu.VMEM((8, 3 * D), F32),
                        pltpu.SemaphoreType.DMA((15,)), pltpu.SemaphoreType.DMA((15,)),
                        pltpu.SemaphoreType.DMA((10,))])
    return pl.pallas_call(
        body, name="in_proj", grid_spec=grid_spec,
        out_shape=(jax.ShapeDtypeStruct((B, T, 4 * D), BF16), jax.ShapeDtypeStruct((B, T, D), BF16),
                   jax.ShapeDtypeStruct((N_SHARD, D, D), BF16),
                   jax.ShapeDtypeStruct((N_SHARD,) + wout_b.shape, BF16),
                   jax.ShapeDtypeStruct((8, 3 * D), F32)),
        compiler_params=_params(("arbitrary",) * 3, vmem_mb=56))(
            order, x, ctx, mod_part, norm_g, win_b, wout_b, cos2, sin2)


def _na_specs(L, T, rows, nh=2):
    nm = rows // 4
    w = nh * NA_DH
    per = 512 // w
    q_spec = pl.BlockSpec((None, TQ, w), lambda hp, b, m: (b, m, hp))
    k_spec = pl.BlockSpec((None, T, w), lambda hp, b, m: (b, 0, per + hp))
    v_spec = pl.BlockSpec((None, T, w), lambda hp, b, m: (b, 0, 2 * per + hp))
    g_spec = pl.BlockSpec((None, TQ, w), lambda hp, b, m: (b, m, 3 * per + hp))
    bias_spec = pl.BlockSpec((nh, 3, TQ, KW), lambda hp, b, m: (hp, 0, 0, 0))
    return nm, q_spec, k_spec, v_spec, g_spec, bias_spec


def _na_tile(m, nm, rows):
    typ = jnp.where(m == 0, 0, jnp.where(m == nm - 1, 2, 1))
    start = pl.multiple_of(jnp.clip(4 * m - 4, 0, rows - 12) * GRID_W, TQ)
    return typ, start


def _na_fwd_call(P, bias, wada_b, L, LC):
    B, T, _ = P.shape
    rows = L // GRID_W
    NH = 4
    nm, q_spec, k_spec, v_spec, g_spec, bias_spec = _na_specs(L, T, rows, NH)
    ngrp = 8 // NH

    def body(q_ref, k_ref, v_ref, g_ref, bias_ref, wown_ref, y_ref, o_ref, wf_ref, w_all, ssem, rsem, lsem):
        hp, b, m = pl.program_id(0), pl.program_id(1), pl.program_id(2)
        own, send, recv, fsend, frecv, outs = _gather_copies(
            wown_ref, w_all, wf_ref, wown_ref.shape[0] // 2, (ssem, rsem, lsem), 0, 0)
        first = (b == 0) & (m == 0)

        @pl.when(first & (hp == 0))
        def _():
            own.start()
            own.wait()
            for cp in send:
                cp.start()
            outs[0].start()

        @pl.when(first & (hp == ngrp - 1))
        def _():
            for got, fwd in zip(recv, fsend):
                got.wait_recv()
                fwd.start()

        @pl.when((hp == ngrp - 1) & (b == B - 1) & (m == nm - 1))
        def _():
            for cp in frecv:
                cp.wait_recv()
            for cp in outs[1:]:
                cp.start()
            _finish(outs, send + fsend, [])

        typ, start = _na_tile(m, nm, rows)
        for hh in range(NH):
            ln = slice(hh * NA_DH, (hh + 1) * NA_DH)
            q = q_ref[:, ln]
            kw, vw = k_ref[pl.ds(start, KW), ln], v_ref[pl.ds(start, KW), ln]
            kc, vc = k_ref[L:L + LC, ln], v_ref[L:L + LC, ln]
            s1 = _dot_nt(q, kw) + bias_ref[hh, typ]
            s2 = _dot_nt(q, kc)
            mx = jnp.maximum(jnp.max(s1, axis=-1, keepdims=True), jnp.max(s2, axis=-1, keepdims=True))
            p1, p2 = jnp.exp(s1 - mx), jnp.exp(s2 - mx)
            inv = 1.0 / (jnp.sum(p1, axis=-1, keepdims=True) + jnp.sum(p2, axis=-1, keepdims=True))
            o = (_dot(p1.astype(BF16), vw) + _dot(p2.astype(BF16), vc)) * inv
            g = g_ref[:, ln].astype(F32)
            o_ref[:, ln] = o.astype(BF16)
            y_ref[:, ln] = (o * (g * _sigmoid(g))).astype(BF16)

    tile = pl.BlockSpec((None, TQ, NH * NA_DH), lambda hp, b, m: (b, m, hp))
    return pl.pallas_call(
        body, name="na_fwd", grid=(ngrp, B, nm),
        in_specs=[q_spec, k_spec, v_spec, g_spec, bias_spec, ANY],
        out_specs=(tile, tile, ANY),
        out_shape=(jax.ShapeDtypeStruct((B, L, 512), BF16),) * 2
        + (jax.ShapeDtypeStruct((N_SHARD,) + wada_b.shape, BF16),),
        scratch_shapes=[pltpu.VMEM((N_SHARD,) + wada_b.shape, BF16),
                        pltpu.SemaphoreType.DMA((6,)), pltpu.SemaphoreType.DMA((6,)),
                        pltpu.SemaphoreType.DMA((5,))],
        compiler_params=_params(("arbitrary",) * 3, vmem_mb=56))(P, P, P, P, bias, wada_b)


def _na_bwd_call(P, bias, dY, o_na, L, LC):
    B, T, _ = P.shape
    rows = L // GRID_W
    NH = 4
    W = NH * NA_DH
    nm, q_spec, k_spec, v_spec, g_spec, bias_spec = _na_specs(L, T, rows, NH)
    scale = NA_DH ** -0.5

    RB = 32

    def body(q_ref, k_ref, v_ref, g_ref, bias_ref, dy_ref, o_ref, dq_ref, dg_ref, dk_ref, dv_ref, dt_ref,
             db_ref, s1_ref, s2_ref, dp1_ref, dp2_ref, p1_ref, p2_ref, ds1_ref, ds2_ref, dkt_ref, dvt_ref):
        b, m = pl.program_id(1), pl.program_id(2)
        typ, start = _na_tile(m, nm, rows)

        @pl.when(m == 0)
        def _():
            dkt_ref[...] = jnp.zeros_like(dkt_ref)
            dvt_ref[...] = jnp.zeros_like(dvt_ref)

        @pl.when((m == 0) & (b == 0))
        def _():
            db_ref[...] = jnp.zeros_like(db_ref)

        for hh in range(NH):
            ln = slice(hh * NA_DH, (hh + 1) * NA_DH)
            q = q_ref[:, ln]
            kw, vw = k_ref[pl.ds(start, KW), ln], v_ref[pl.ds(start, KW), ln]
            kc, vc = k_ref[L:L + LC, ln], v_ref[L:L + LC, ln]
            g = g_ref[:, ln].astype(F32)
            sg = _sigmoid(g)
            dy = dy_ref[:, ln].astype(F32)
            do = (dy * (g * sg)).astype(BF16)
            s1_ref[hh] = _dot_nt(q, kw)
            s2_ref[hh] = _dot_nt(q, kc)
            dp1_ref[hh] = _dot_nt(do, vw)
            dp2_ref[hh] = _dot_nt(do, vc)

            def rows_pass(r, carry, hh=hh):
                rw = pl.ds(pl.multiple_of(r * RB, RB), RB)
                a = s1_ref[hh, rw, :] + bias_ref[hh, typ, rw, :]
                c = s2_ref[hh, rw, :]
                mx = jnp.maximum(jnp.max(a, axis=-1, keepdims=True), jnp.max(c, axis=-1, keepdims=True))
                e1, e2 = jnp.exp(a - mx), jnp.exp(c - mx)
                inv = 1.0 / (jnp.sum(e1, axis=-1, keepdims=True) + jnp.sum(e2, axis=-1, keepdims=True))
                p1, p2 = e1 * inv, e2 * inv
                p1_ref[hh, rw, :] = p1.astype(BF16)
                p2_ref[hh, rw, :] = p2.astype(BF16)
                dp1, dp2 = dp1_ref[hh, rw, :], dp2_ref[hh, rw, :]
                delta = jnp.sum(p1 * dp1, axis=-1, keepdims=True) + jnp.sum(p2 * dp2, axis=-1, keepdims=True)
                ds1 = p1 * (dp1 - delta)
                db_ref[hh, typ, rw, :] += ds1
                ds1_ref[hh, rw, :] = ds1.astype(BF16)
                ds2_ref[hh, rw, :] = (p2 * (dp2 - delta)).astype(BF16)
                return carry

            lax.fori_loop(0, TQ // RB, rows_pass, 0, unroll=True)
            p1b, p2b, ds1b, ds2b = p1_ref[hh], p2_ref[hh], ds1_ref[hh], ds2_ref[hh]
            dg_ref[:, ln] = (dy * o_ref[:, ln].astype(F32) * (sg * (1.0 + g * (1.0 - sg)))).astype(BF16)
            dq_ref[:, ln] = ((_dot(ds1b, kw) + _dot(ds2b, kc)) * scale).astype(BF16)
            dkt_ref[ln, pl.ds(start, KW)] += _dot_tn(q, ds1b)
            dvt_ref[ln, pl.ds(start, KW)] += _dot_tn(do, p1b)
            dkt_ref[ln, L:L + LC] += _dot_tn(q, ds2b)
            dvt_ref[ln, L:L + LC] += _dot_tn(do, p2b)

        @pl.when(m == nm - 1)
        def _():
            dk_ref[...] = dkt_ref[...].T
            dv_ref[...] = dvt_ref[...].T

        @pl.when((m == nm - 1) & (b == B - 1))
        def _():
            for hh in range(NH):
                for dr, t in _bias_tile_sums(db_ref, hh).items():
                    dt_ref[hh, dr + 7] = t

    tile = pl.BlockSpec((None, TQ, W), lambda hp, b, m: (b, m, hp))
    kv_out = pl.BlockSpec((None, T, W), lambda hp, b, m: (b, 0, hp))
    wide, narrow = (NH, TQ, KW), (NH, TQ, LC)
    return pl.pallas_call(
        body, name="na_bwd", grid=(8 // NH, B, nm),
        in_specs=[q_spec, k_spec, v_spec, g_spec, bias_spec, tile, tile],
        out_specs=(tile, tile, kv_out, kv_out,
                   pl.BlockSpec((NH, 15, GRID_W, GRID_W), lambda hp, b, m: (hp, 0, 0, 0))),
        out_shape=(jax.ShapeDtypeStruct((B, L, 512), BF16), jax.ShapeDtypeStruct((B, L, 512), BF16),
                   jax.ShapeDtypeStruct((B, T, 512), F32), jax.ShapeDtypeStruct((B, T, 512), F32),
                   jax.ShapeDtypeStruct((bias.shape[0], 15, GRID_W, GRID_W), F32)),
        scratch_shapes=[pltpu.VMEM((NH,) + bias.shape[1:], F32),
                        pltpu.VMEM(wide, F32), pltpu.VMEM(narrow, F32), pltpu.VMEM(wide, F32), pltpu.VMEM(narrow, F32),
                        pltpu.VMEM(wide, BF16), pltpu.VMEM(narrow, BF16), pltpu.VMEM(wide, BF16),
                        pltpu.VMEM(narrow, BF16), pltpu.VMEM((W, T), F32), pltpu.VMEM((W, T), F32)],
        compiler_params=_params(("arbitrary",) * 3, vmem_mb=60))(P, P, P, P, bias, dY, o_na)


def _head_scalar(dec_ref, h):
    lane = lax.broadcasted_iota(jnp.int32, dec_ref.shape, 1)
    return -jnp.sum(jnp.where(lane == h, jnp.exp(dec_ref[...]), 0.0), axis=1, keepdims=True)


def _chunk_decay(lgf, lgb):
    tau = lax.broadcasted_iota(jnp.int32, (TQ, 1), 0).astype(F32)
    sig = lax.broadcasted_iota(jnp.int32, (1, TQ), 1).astype(F32)
    dist = tau - sig
    dm = jnp.exp(dist * jnp.where(dist > 0, lgf, -lgb)) * jnp.where(dist == 0, 2.0, 1.0)
    return tau, dist, dm


def _ret_states_call(P, dec_f, dec_b, L, LC):
    B, T, _ = P.shape
    n = L // TQ

    def body(df_ref, db_ref, k_ref, v_ref, sf_ref, sb_ref):
        h = pl.program_id(1)
        lgf, lgb = _head_scalar(df_ref, h), _head_scalar(db_ref, h)
        tau = lax.broadcasted_iota(jnp.int32, (TQ, 1), 0).astype(F32)
        jc = lax.broadcasted_iota(jnp.int32, (LC, 1), 0).astype(F32)
        wf, wb = jnp.exp(lgf * (TQ - 1.0 - tau)), jnp.exp(lgb * tau)
        gcf, gcb = jnp.exp(lgf * float(TQ)), jnp.exp(lgb * float(TQ))
        kc, vc = k_ref[L:L + LC, :].astype(F32), v_ref[L:L + LC, :]

        def chunk_state(i, w):
            ks = pl.multiple_of(i * TQ, TQ)
            return _dot_tn((k_ref[pl.ds(ks, TQ), :].astype(F32) * w).astype(BF16), v_ref[pl.ds(ks, TQ), :])

        def fwd(i, s):
            sf_ref[i] = s
            return gcf * s + chunk_state(i, wf)

        lax.fori_loop(0, n, fwd, _dot_tn((kc * jnp.exp(lgf * (LC - 1.0 - jc))).astype(BF16), vc), unroll=True)

        def bwd(r, s):
            i = n - 1 - r
            sb_ref[i] = s
            return gcb * s + chunk_state(i, wb)

        lax.fori_loop(0, n, bwd, _dot_tn((kc * jnp.exp(lgb * jc)).astype(BF16), vc), unroll=True)

    st = pl.BlockSpec((None, None, n, RET_DK, RET_DK), lambda b, h: (b, h, 0, 0, 0))
    return pl.pallas_call(
        body, name="ret_states", grid=(B, 4),
        in_specs=[pl.BlockSpec((1, 4), lambda b, h: (0, 0)), pl.BlockSpec((1, 4), lambda b, h: (0, 0)),
                  pl.BlockSpec((None, T, 128), lambda b, h: (b, 0, 20 + h)),
                  pl.BlockSpec((None, T, 128), lambda b, h: (b, 0, 24 + h))],
        out_specs=(st, st),
        out_shape=(jax.ShapeDtypeStruct((B, 4, n, RET_DK, RET_DK), F32),) * 2,
        compiler_params=_params(("arbitrary",) * 2))(dec_f, dec_b, P, P)


def _retc_fwd_call(P, sf, sb, dec_f, dec_b, ret_norm_g, L):
    B, T, _ = P.shape
    sec = lambda k: pl.BlockSpec((None, TQ, 512), lambda b, i: (b, i, k))
    dec_spec = pl.BlockSpec((1, 4), lambda b, i: (0, 0))
    st_spec = pl.BlockSpec((None, 4, None, RET_DK, RET_DK), lambda b, i: (b, 0, i, 0, 0))

    def body(df_ref, db_ref, q_ref, k_ref, v_ref, g_ref, gn_ref, sf_ref, sb_ref, y_ref, o_ref):
        for h in range(4):
            ln = slice(h * RET_DK, (h + 1) * RET_DK)
            lgf, lgb = _head_scalar(df_ref, h), _head_scalar(db_ref, h)
            tau, _, dm = _chunk_decay(lgf, lgb)
            q = q_ref[:, ln]
            qf = q.astype(F32)
            acc = _dot((_dot_nt(q, k_ref[:, ln]) * dm).astype(BF16), v_ref[:, ln])
            acc = acc + _dot((qf * jnp.exp(lgf * (tau + 1.0))).astype(BF16), sf_ref[h].astype(BF16))
            acc = acc + _dot((qf * jnp.exp(lgb * (TQ - tau))).astype(BF16), sb_ref[h].astype(BF16))
            o_ref[:, ln] = acc
            rn = lax.rsqrt(jnp.mean(acc * acc, axis=-1, keepdims=True) + EPS)
            g = g_ref[:, ln].astype(F32)
            y_ref[:, ln] = ((acc * rn * gn_ref[:, ln]) * (g * _sigmoid(g))).astype(BF16)

    tile = pl.BlockSpec((None, TQ, 512), lambda b, i: (b, i, 0))
    return pl.pallas_call(
        body, name="ret_fwd", grid=(B, L // TQ),
        in_specs=[dec_spec, dec_spec, sec(4), sec(5), sec(6), sec(7),
                  pl.BlockSpec((1, 512), lambda b, i: (0, 0)), st_spec, st_spec],
        out_specs=(tile, tile),
        out_shape=(jax.ShapeDtypeStruct((B, L, 512), BF16), jax.ShapeDtypeStruct((B, L, 512), F32)),
        compiler_params=_params(("arbitrary",) * 2))(dec_f, dec_b, P, P, P, P, ret_norm_g, sf, sb)


def _retc_bwd_call(P, sf, sb, dec_f, dec_b, ret_norm_g, o_ret, dY, cos2, sin2, L, LC):
    B, T, _ = P.shape
    n = L // TQ
    C = float(TQ)
    kscale = RET_DK ** -0.5
    st_spec = pl.BlockSpec((None, 4, n, RET_DK, RET_DK), lambda b, i: (b, 0, 0, 0, 0))

    def body(df_ref, db_ref, q_ref, k_ref, v_ref, g_ref, gn_ref, o_ref, dy_ref, cos_ref, sin_ref, sf_ref, sb_ref,
             dq_ref, dg_ref, dk_ref, dv_ref, dgn_ref, dlg_ref, dsf_ref, dsb_ref):
        i = pl.program_id(1)

        @pl.when(i == 0)
        def _():
            dk_ref[...] = jnp.zeros_like(dk_ref)
            dv_ref[...] = jnp.zeros_like(dv_ref)
            dgn_ref[...] = jnp.zeros_like(dgn_ref)
            dlg_ref[...] = jnp.zeros_like(dlg_ref)

        rows = pl.ds(pl.multiple_of(i * TQ, TQ), TQ)
        cs, sn = cos_ref[rows, :], sin_ref[rows, :]

        def one_head(h):
            ln = slice(h * RET_DK, (h + 1) * RET_DK)
            lgf, lgb = _head_scalar(df_ref, h), _head_scalar(db_ref, h)
            tau, dist, dm = _chunk_decay(lgf, lgb)

            def add_lg(row, x):
                csum = jnp.sum(x, axis=0, keepdims=True)
                tot = csum[:, 0:128]
                for part in range(1, x.shape[1] // 128):
                    tot = tot + csum[:, part * 128:(part + 1) * 128]
                dlg_ref[h, row:row + 1, :] += tot

            q = q_ref[:, ln]
            qf = q.astype(F32)
            o = o_ref[:, ln]
            g = g_ref[:, ln].astype(F32)
            dy = dy_ref[:, ln].astype(F32)
            gn = gn_ref[:, ln]
            sg = _sigmoid(g)
            rn = lax.rsqrt(jnp.mean(o * o, axis=-1, keepdims=True) + EPS)
            nrm = o * rn
            dg_ref[:, ln] = (dy * (nrm * gn) * (sg * (1.0 + g * (1.0 - sg)))).astype(BF16)
            dhn = dy * (g * sg)
            dgn_ref[:, ln] += jnp.sum(dhn * nrm, axis=0, keepdims=True)
            dnrm = dhn * gn
            do = rn * (dnrm - nrm * jnp.mean(dnrm * nrm, axis=-1, keepdims=True))
            dob = do.astype(BF16)
            ki, vi = k_ref[rows, ln], v_ref[rows, ln]
            s = _dot_nt(q, ki)
            dsv = _dot_nt(dob, vi)
            dsb = (dsv * dm).astype(BF16)
            dk_ref[rows, ln] += _dot_tn(dsb, q)
            dv_ref[rows, ln] += _dot_tn((s * dm).astype(BF16), dob)
            xw = s * dsv * dm * jnp.abs(dist)
            fpart = jnp.where(dist > 0, xw, 0.0)
            add_lg(0, fpart)
            add_lg(1, xw - fpart)
            dq = _dot(dsb, ki)
            af, ab = jnp.exp(lgf * (tau + 1.0)), jnp.exp(lgb * (C - tau))
            qa, qb = (qf * af).astype(BF16), (qf * ab).astype(BF16)
            sfi, sbi = sf_ref[h, i].astype(BF16), sb_ref[h, i].astype(BF16)
            dq = dq + af * _dot_nt(dob, sfi) + ab * _dot_nt(dob, sbi)
            dsf_ref[h, i] = _dot_tn(qa, dob)
            dsb_ref[h, i] = _dot_tn(qb, dob)
            add_lg(0, (tau + 1.0) * (_dot(qa, sfi) * do))
            add_lg(1, (C - tau) * (_dot(qb, sbi) * do))
            dq_ref[:, ln] = (dq * cs - pltpu.roll(dq, 64, 1) * sn).astype(BF16)

            @pl.when(i == n - 1)
            def _():
                jc = lax.broadcasted_iota(jnp.int32, (LC, 1), 0).astype(F32)
                crow = pl.ds(L, LC)

                def through_state(rws, w, dw, gst, row):
                    kk, vv = k_ref[rws, ln].astype(F32), v_ref[rws, ln]
                    gb = gst.astype(BF16)
                    vg = _dot_nt(vv, gb)
                    kw = kk * w
                    dk_ref[rws, ln] += w * vg
                    dv_ref[rws, ln] += _dot(kw.astype(BF16), gb)
                    add_lg(row, dw * (kw * vg))

                def scan(gc, w, dw, st_ref, dst_ref, order, row):
                    def step(r, gst):
                        j = order(r)
                        through_state(pl.ds(pl.multiple_of(j * TQ, TQ), TQ), w, dw, gst, row)
                        add_lg(row, (C * gc) * (gst * st_ref[h, j]))
                        return dst_ref[h, j] + gc * gst
                    return lax.fori_loop(0, n, step, jnp.zeros((RET_DK, RET_DK), F32), unroll=True)

                gcf, gcb = jnp.exp(lgf * C), jnp.exp(lgb * C)
                g0 = scan(gcf, jnp.exp(lgf * (C - 1.0 - tau)), C - 1.0 - tau, sf_ref, dsf_ref,
                          lambda r: n - 1 - r, 0)
                through_state(crow, jnp.exp(lgf * (LC - 1.0 - jc)), LC - 1.0 - jc, g0, 0)
                g1 = scan(gcb, jnp.exp(lgb * tau), tau, sb_ref, dsb_ref, lambda r: r, 1)
                through_state(crow, jnp.exp(lgb * jc), jc, g1, 1)
                dk = dk_ref[:, ln]
                dk_ref[:, ln] = (dk * cos_ref[...] - pltpu.roll(dk, 64, 1) * sin_ref[...]) * kscale

        for h in range(4):
            one_head(h)

    sec = lambda k: pl.BlockSpec((None, TQ, 512), lambda b, i: (b, i, k))
    full = lambda k: pl.BlockSpec((None, T, 512), lambda b, i: (b, 0, k))
    dec_spec = pl.BlockSpec((1, 4), lambda b, i: (0, 0))
    tab = pl.BlockSpec((T, RET_DK), lambda b, i: (0, 0))
    return pl.pallas_call(
        body, name="ret_bwd", grid=(B, n),
        in_specs=[dec_spec, dec_spec, sec(4), full(5), full(6), sec(7),
                  pl.BlockSpec((1, 512), lambda b, i: (0, 0)), sec(0), sec(1), tab, tab, st_spec, st_spec],
        out_specs=(sec(0), sec(0), full(0), full(0),
                   pl.BlockSpec((None, 1, 512), lambda b, i: (b, 0, 0)),
                   pl.BlockSpec((None, 4, 8, 128), lambda b, i: (b, 0, 0, 0))),
        out_shape=(jax.ShapeDtypeStruct((B, L, 512), BF16), jax.ShapeDtypeStruct((B, L, 512), BF16),
                   jax.ShapeDtypeStruct((B, T, 512), F32), jax.ShapeDtypeStruct((B, T, 512), F32),
                   jax.ShapeDtypeStruct((B, 1, 512), F32), jax.ShapeDtypeStruct((B, 4, 8, 128), F32)),
        scratch_shapes=[pltpu.VMEM((4, n, RET_DK, RET_DK), F32), pltpu.VMEM((4, n, RET_DK, RET_DK), F32)],
        compiler_params=_params(("arbitrary",) * 2, vmem_mb=56))(
            dec_f, dec_b, P, P, P, P, ret_norm_g, o_ret, dY, cos2, sin2, sf, sb)


def _out_call(y_na, y_ret, x, target, mod, final_g, wout_f):
    B, L, _ = x.shape
    TO = 2 * TQ

    def body(yn_ref, yr_ref, x_ref, t_ref, mod_ref, gf_ref, w_ref, dy_ref, dx2_ref, dwb_ref, sm_ref, dw_ref):
        b, i = pl.program_id(0), pl.program_id(1)

        @pl.when((b == 0) & (i == 0))
        def _():
            dw_ref[...] = jnp.zeros_like(dw_ref)
            sm_ref[...] = jnp.zeros_like(sm_ref)

        gate = mod_ref[pl.ds(b, 1), 2 * D:3 * D]
        gf = gf_ref[...]
        yn, yr = yn_ref[...], yr_ref[...]
        ylat = _dot(yn, w_ref[0:512, :]) + _dot(yr, w_ref[512:1024, :])
        x2 = x_ref[...] + gate * ylat
        r = lax.rsqrt(jnp.mean(x2 * x2, axis=-1, keepdims=True) + EPS)
        xr = x2 * r
        err = xr * gf - t_ref[...]
        sm_ref[1:2, :] += jnp.sum(err * err, axis=0, keepdims=True)
        dout = err * (1.0 / D)
        sm_ref[0:1, :] += jnp.sum(dout * xr, axis=0, keepdims=True)
        gd = dout * gf
        dx2 = r * (gd - xr * jnp.mean(gd * xr, axis=-1, keepdims=True))
        dx2_ref[...] = dx2
        sm_ref[pl.ds(2 + b, 1), :] += jnp.sum(dx2 * ylat, axis=0, keepdims=True)
        dyl = (gate * dx2).astype(BF16)
        dy_ref[:, 0:512] = _dot_nt(dyl, w_ref[0:512, :]).astype(BF16)
        dy_ref[:, 512:1024] = _dot_nt(dyl, w_ref[512:1024, :]).astype(BF16)
        dw_ref[0:512, :] += _dot_tn(yn, dyl)
        dw_ref[512:1024, :] += _dot_tn(yr, dyl)

        @pl.when((b == B - 1) & (i == L // TO - 1))
        def _():
            dwb_ref[...] = dw_ref[...].astype(BF16)

    half = pl.BlockSpec((None, TO, 512), lambda b, i: (b, i, 0))
    full = pl.BlockSpec((None, TO, D), lambda b, i: (b, i, 0))
    return pl.pallas_call(
        body, name="out_proj_loss", grid=(B, L // TO),
        in_specs=[half, half, full, full,
                  pl.BlockSpec((8, 3 * D), lambda b, i: (0, 0)),
                  pl.BlockSpec((1, D), lambda b, i: (0, 0)),
                  pl.BlockSpec((D, D), lambda b, i: (0, 0))],
        out_specs=(full, full, pl.BlockSpec((D, D), lambda b, i: (0, 0)),
                   pl.BlockSpec((8, D), lambda b, i: (0, 0))),
        out_shape=(jax.ShapeDtypeStruct((B, L, D), BF16), jax.ShapeDtypeStruct((B, L, D), F32),
                   jax.ShapeDtypeStruct((D, D), BF16), jax.ShapeDtypeStruct((8, D), F32)),
        scratch_shapes=[pltpu.VMEM((D, D), F32)],
        compiler_params=_params(("arbitrary",) * 2))(y_na, y_ret, x, target, mod, final_g, wout_f)


def _dh_call(dsec, win_f, x, ctx, dx2, mod, norm_g, cp_in, cp_out):
    B, L, _ = x.shape
    LC = ctx.shape[1]
    nl = L // TQ

    def body(d0, d1, d2, d3, d4, d5, d6, d7, w_ref, x_ref, ctx_ref, dx2_ref, mod_ref, g_ref, cpi_ref, cpo_ref,
             gx_ref, sm_ref, sli_ref, slo_ref, ssem, rsem, lsem):
        drefs = (d0, d1, d2, d3, d4, d5, d6, d7)
        b, t = pl.program_id(0), pl.program_id(1)
        is_lat = t < nl

        @pl.when((b == 0) & (t == 0))
        def _():
            sm_ref[...] = jnp.zeros_like(sm_ref)

        def dh_of(secs):
            acc = jnp.zeros((TQ, D), F32)
            for sec in secs:
                s, half = divmod(sec, 2)
                acc = acc + _dot_nt(drefs[sec][...].astype(BF16), w_ref[s, :, half * 512:(half + 1) * 512])
            return acc

        def norm_bwd(dh, xt, mrow):
            scale = mrow[:, D:2 * D]
            g = g_ref[...]
            rstd = lax.rsqrt(jnp.mean(xt * xt, axis=-1, keepdims=True) + EPS)
            xn = xt * rstd
            dshift = jnp.sum(dh, axis=0, keepdims=True)
            dscale = jnp.sum(dh * (xn * g), axis=0, keepdims=True)
            dhn = dh * (1.0 + scale)
            sm_ref[0:1, :] += jnp.sum(dhn * xn, axis=0, keepdims=True)
            dxn = dhn * g
            dx = rstd * (dxn - xn * jnp.mean(dxn * xn, axis=-1, keepdims=True))
            return dshift, dscale, dx

        @pl.when(is_lat)
        def _():
            dshift, dscale, dx = norm_bwd(dh_of(range(8)), x_ref[...], mod_ref[pl.ds(b, 1), :])
            sm_ref[pl.ds(3 + b, 1), :] += dshift
            sm_ref[pl.ds(3 + B + b, 1), :] += dscale
            gx_ref[...] = dx2_ref[...] + dx

        @pl.when(jnp.logical_not(is_lat))
        def _():
            dshift, dscale, _ = norm_bwd(dh_of((1, 2, 5, 6)), ctx_ref[...], mod_ref[B:B + 1, :])
            sm_ref[1:2, :] += dshift
            sm_ref[2:3, :] += dscale

        mx, my, mc = _mesh_pos()
        s = 2 * mx + my
        cps, sls = (cpi_ref, cpo_ref), (sli_ref, slo_ref)
        own = [pltpu.make_async_copy(cps[a].at[s], sls[a].at[s], lsem.at[a]) for a in range(2)]
        sends, recvs, k = [], [], 0
        for px, py in _other_chips(mx, my):
            ps = 2 * px + py
            for a in range(2):
                sends.append(_remote(cps[a].at[ps], sls[a].at[s], ssem, rsem, k, (px, py, mc)))
                recvs.append(_remote(cps[a].at[s], sls[a].at[ps], ssem, rsem, k, (px, py, mc)))
                k += 1

        @pl.when((b == 0) & (t == 0))
        def _():
            for cp in own + sends:
                cp.start()

        @pl.when((b == B - 1) & (t == nl))
        def _():
            _finish(own, sends, recvs)

    lat = lambda b, t: (b, jnp.minimum(t, nl - 1), 0)
    tok = lambda b, t: (b, t, 0)
    sec_specs = [pl.BlockSpec((None, TQ, 512), lat if sec in (0, 3, 4, 7) else tok) for sec in range(8)]
    return pl.pallas_call(
        body, name="dh_norm_bwd", grid=(B, nl + 1),
        in_specs=sec_specs + [
            pl.BlockSpec((N_SHARD, D, D), lambda b, t: (0, 0, 0)),
            pl.BlockSpec((None, TQ, D), lat),
            pl.BlockSpec((None, LC, D), lambda b, t: (b, 0, 0)),
            pl.BlockSpec((None, TQ, D), lat),
            pl.BlockSpec((8, 3 * D), lambda b, t: (0, 0)),
            pl.BlockSpec((1, D), lambda b, t: (0, 0)), ANY, ANY],
        out_specs=(pl.BlockSpec((None, TQ, D), lat), pl.BlockSpec((8, D), lambda b, t: (0, 0)), ANY, ANY),
        out_shape=(jax.ShapeDtypeStruct((B, L, D), F32), jax.ShapeDtypeStruct((8, D), F32),
                   jax.ShapeDtypeStruct(cp_in.shape, cp_in.dtype), jax.ShapeDtypeStruct(cp_out.shape, cp_out.dtype)),
        scratch_shapes=[pltpu.SemaphoreType.DMA((6,)), pltpu.SemaphoreType.DMA((6,)),
                        pltpu.SemaphoreType.DMA((2,))],
        compiler_params=_params(("arbitrary",) * 2))(*dsec, win_f, x, ctx, dx2, mod, norm_g, cp_in, cp_out)


def _dw_call(dsec, h, L):
    B, T, _ = h.shape
    TW = 2 * TQ
    nl = L // TW
    KV = (1, 2, 5, 6)

    def body(d0, d1, d2, d3, d4, d5, d6, d7, c1, c2, c5, c6, h_ref, hc_ref, dw_ref, acc_ref):
        drefs = (d0, d1, d2, d3, d4, d5, d6, d7)
        crefs = dict(zip(KV, (c1, c2, c5, c6)))
        b, t = pl.program_id(0), pl.program_id(1)

        @pl.when((b == 0) & (t == 0))
        def _():
            acc_ref[...] = jnp.zeros_like(acc_ref)

        def add(hb, refs, secs):
            for sec in secs:
                s, half = divmod(sec, 2)
                acc_ref[s, :, half * 512:(half + 1) * 512] += _dot_tn(hb, refs[sec][...].astype(BF16))

        @pl.when(t < nl)
        def _():
            add(h_ref[...], drefs, range(8))

        @pl.when(t == nl)
        def _():
            add(hc_ref[...], crefs, KV)

        @pl.when((b == B - 1) & (t == nl))
        def _():
            dw_ref[...] = acc_ref[...].astype(BF16)

    lat = lambda b, t: (b, jnp.minimum(t, nl - 1), 0)
    ctx = lambda b, t: (b, L // TQ, 0)
    return pl.pallas_call(
        body, name="dw_in", grid=(B, nl + 1),
        in_specs=[pl.BlockSpec((None, TW, 512), lat)] * 8 + [pl.BlockSpec((None, TQ, 512), ctx)] * 4
        + [pl.BlockSpec((None, TW, D), lat), pl.BlockSpec((None, TQ, D), ctx)],
        out_specs=pl.BlockSpec((N_SHARD, D, D), lambda b, t: (0, 0, 0)),
        out_shape=jax.ShapeDtypeStruct((N_SHARD, D, D), BF16),
        scratch_shapes=[pltpu.VMEM((N_SHARD, D, D), F32)],
        compiler_params=_params(("arbitrary",) * 2, vmem_mb=60))(*dsec, *[dsec[k] for k in KV], h, h)


def _mesh_pos():
    return lax.axis_index("x"), lax.axis_index("y"), lax.axis_index("c")


def _flip(v, f):
    return 1 - v if f else v


def _remote(src, dst, ssem, rsem, k, peer):
    return pltpu.make_async_remote_copy(src_ref=src, dst_ref=dst, send_sem=ssem.at[k], recv_sem=rsem.at[k],
                                        device_id=peer, device_id_type=MESH)


def _other_chips(x, y):
    return [(_flip(x, fx), _flip(y, fy)) for fx, fy in ((1, 0), (0, 1), (1, 1))]


def _gather_copies(own_ref, all_ref, out_ref, hr, sems, k0, l0):
    ssem, rsem, lsem = sems
    mx, my, mc = _mesh_pos()
    s = 2 * mx + my
    sib = (mx, my, 1 - mc)
    own = pltpu.make_async_copy(own_ref, all_ref.at[s], lsem.at[l0])
    send, recv, fsend, frecv = [], [], [], []
    outs = [pltpu.make_async_copy(all_ref.at[s], out_ref.at[s], lsem.at[l0 + 1])]
    for k, (px, py) in enumerate(_other_chips(mx, my)):
        ps = 2 * px + py
        mine = all_ref.at[s, pl.ds(mc * hr, hr)]
        send.append(_remote(mine, mine, ssem, rsem, k0 + k, (px, py, mc)))
        got = all_ref.at[ps, pl.ds(mc * hr, hr)]
        recv.append(_remote(mine, got, ssem, rsem, k0 + k, (px, py, mc)))
        fsend.append(_remote(got, got, ssem, rsem, k0 + 3 + k, sib))
        theirs = all_ref.at[ps, pl.ds((1 - mc) * hr, hr)]
        frecv.append(_remote(theirs, theirs, ssem, rsem, k0 + 3 + k, sib))
        outs.append(pltpu.make_async_copy(all_ref.at[ps], out_ref.at[ps], lsem.at[l0 + 2 + k]))
    return own, send, recv, fsend, frecv, outs


def _all_to_all_small(src, dst_all, ssem, rsem, k0, x, y, cc):
    me = 4 * x + 2 * y + cc
    sends, recvs = [], []
    for f in range(1, N_DEV):
        px, py, pc = _flip(x, f & 4), _flip(y, f & 2), _flip(cc, f & 1)
        sends.append(_remote(src, dst_all.at[me], ssem, rsem, k0 + f - 1, (px, py, pc)))
        recvs.append(_remote(src, dst_all.at[4 * px + 2 * py + pc], ssem, rsem, k0 + f - 1, (px, py, pc)))
    return sends, recvs


def _finish(local, sends, recvs):
    for cp in recvs:
        cp.wait_recv()
    for cp in sends:
        cp.wait_send()
    for cp in local:
        cp.wait()


def _c_gather_call(c, rpb_flat):
    def body(c_ref, r_ref, c_all, bias_out, bias_ref, et_ref, ssem, rsem, lsem):
        x, y, cc = _mesh_pos()
        me = 4 * x + 2 * y + cc
        local = [pltpu.make_async_copy(c_ref, c_all.at[me], lsem.at[0])]
        c_send, c_recv = _all_to_all_small(c_ref, c_all, ssem, rsem, 0, x, y, cc)
        for cp in local + c_send:
            cp.start()
        bias_out_copies = _bias_body(r_ref, bias_ref, et_ref, bias_out, lsem.at[1])
        _finish(local + bias_out_copies, c_send, c_recv)

    bias_shape = (rpb_flat.shape[0], 3, TQ, KW)
    return pl.pallas_call(
        body, name="c_gather",
        in_specs=[pl.BlockSpec(memory_space=pltpu.VMEM), pl.BlockSpec(memory_space=pltpu.SMEM)],
        out_specs=(pl.BlockSpec(memory_space=pltpu.VMEM), ANY),
        out_shape=(jax.ShapeDtypeStruct((N_DEV,) + c.shape, c.dtype), jax.ShapeDtypeStruct(bias_shape, F32)),
        scratch_shapes=[pltpu.VMEM(bias_shape, F32), pltpu.VMEM((15, GRID_W, GRID_W), F32),
                        pltpu.SemaphoreType.DMA((N_DEV - 1,)), pltpu.SemaphoreType.DMA((N_DEV - 1,)),
                        pltpu.SemaphoreType.DMA((2,))],
        compiler_params=pltpu.CompilerParams(vmem_limit_bytes=56 << 20))(c, rpb_flat)


VROWS = 32


def _grad_halves_call(dwin_b, dwout_b, dbias, dlg):
    arrs = (dwin_b, dwout_b)
    hrs = [a.shape[1] // 2 for a in arrs]

    def body(din, dout, db_ref, dlg_ref, cp_in, cp_out, drpb_ref, dlgo_ref, got_in, got_out, p_ref, ssem, rsem):
        x, y, cc = _mesh_pos()
        sib = (x, y, 1 - cc)
        srcs, gots, cps = (din, dout), (got_in, got_out), (cp_in, cp_out)
        halves = [_remote(srcs[a].at[:, pl.ds((1 - cc) * hrs[a], hrs[a])], gots[a], ssem, rsem, a, sib)
                  for a in range(2)]
        for cp in halves:
            cp.start()
        _small_reduce_body(db_ref, dlg_ref, drpb_ref, dlgo_ref, p_ref)
        for cp in halves:
            cp.wait_recv()
        for a in range(2):
            for j in range(N_SHARD):
                def add(i, carry, a=a, j=j):
                    r = pl.multiple_of(i * VROWS, VROWS)
                    mine = srcs[a][j, pl.ds(pl.multiple_of(cc * hrs[a] + r, VROWS), VROWS), :].astype(F32)
                    cps[a][j, pl.ds(r, VROWS), :] = (
                        mine + gots[a][j, pl.ds(r, VROWS), :].astype(F32)).astype(BF16)
                    return carry
                lax.fori_loop(0, hrs[a] // VROWS, add, 0)
        for cp in halves:
            cp.wait_send()

    vmem = pl.BlockSpec(memory_space=pltpu.VMEM)
    half_shapes = [(N_SHARD, hrs[a], arrs[a].shape[2]) for a in range(2)]
    return pl.pallas_call(
        body, name="grad_halves",
        in_specs=[vmem] * 4, out_specs=(vmem,) * 4,
        out_shape=(jax.ShapeDtypeStruct(half_shapes[0], BF16), jax.ShapeDtypeStruct(half_shapes[1], BF16),
                   jax.ShapeDtypeStruct((dbias.shape[0], 16, 32), F32), jax.ShapeDtypeStruct((32, 128), F32)),
        scratch_shapes=[pltpu.VMEM(half_shapes[0], BF16), pltpu.VMEM(half_shapes[1], BF16),
                        pltpu.VMEM((32, GRID_W), F32),
                        pltpu.SemaphoreType.DMA((2,)), pltpu.SemaphoreType.DMA((2,))],
        compiler_params=pltpu.CompilerParams(vmem_limit_bytes=56 << 20))(dwin_b, dwout_b, dbias, dlg)


def _grad_finish_call(sl_in, sl_out, small):
    arrs = (sl_in, sl_out)

    def body(sin, sout, sm, gin, gout, sm_all, h_in, h_out, ssem, rsem, lsem):
        x, y, cc = _mesh_pos()
        me = 4 * x + 2 * y + cc
        sib = (x, y, 1 - cc)
        sls, hs, gs = (sin, sout), (h_in, h_out), (gin, gout)
        sm_send, sm_recv = _all_to_all_small(sm, sm_all, ssem, rsem, 2, x, y, cc)
        sm_own = pltpu.make_async_copy(sm, sm_all.at[me], lsem.at[0])
        for cp in sm_send + [sm_own]:
            cp.start()
        for a in range(2):
            def total(i, carry, a=a):
                rows = pl.ds(pl.multiple_of(i * VROWS, VROWS), VROWS)
                sl = sls[a]
                hs[a][rows, :] = ((sl[0, rows, :].astype(F32) + sl[1, rows, :].astype(F32))
                                  + sl[2, rows, :].astype(F32)) + sl[3, rows, :].astype(F32)
                return carry
            lax.fori_loop(0, arrs[a].shape[1] // VROWS, total, 0)
        mine = [pltpu.make_async_copy(hs[a], gs[a].at[cc], lsem.at[1 + a]) for a in range(2)]
        back = [_remote(hs[a], gs[a].at[cc], ssem, rsem, a, sib) for a in range(2)]
        back_recv = [_remote(hs[a], gs[a].at[1 - cc], ssem, rsem, a, sib) for a in range(2)]
        for cp in mine + back:
            cp.start()
        _finish(mine + [sm_own], back + sm_send, back_recv + sm_recv)

    vmem = pl.BlockSpec(memory_space=pltpu.VMEM)
    return pl.pallas_call(
        body, name="grad_finish",
        in_specs=[vmem] * 3, out_specs=(vmem,) * 3,
        out_shape=(jax.ShapeDtypeStruct((2,) + sl_in.shape[1:], F32),
                   jax.ShapeDtypeStruct((2,) + sl_out.shape[1:], F32),
                   jax.ShapeDtypeStruct((N_DEV,) + small.shape, F32)),
        scratch_shapes=[pltpu.VMEM(sl_in.shape[1:], F32), pltpu.VMEM(sl_out.shape[1:], F32),
                        pltpu.SemaphoreType.DMA((9,)), pltpu.SemaphoreType.DMA((9,)),
                        pltpu.SemaphoreType.DMA((3,))],
        compiler_params=pltpu.CompilerParams(vmem_limit_bytes=48 << 20))(sl_in, sl_out, small)


def _adamw(w, g, m, v):
    m = ADAM_B1 * m + (1.0 - ADAM_B1) * g
    v = ADAM_B2 * v + (1.0 - ADAM_B2) * (g * g)
    m_hat = m / (1.0 - ADAM_B1 ** ADAM_STEP)
    v_hat = v / (1.0 - ADAM_B2 ** ADAM_STEP)
    return -ADAM_LR * (m_hat / (jnp.sqrt(v_hat) + ADAM_EPS) + ADAM_WD * w), m, v


def _adam_call(w, m, v, g, name):
    R, C = w.shape
    tr = 256

    def body(w_ref, m_ref, v_ref, g_ref, go_ref, d_ref, mo_ref, vo_ref):
        g = g_ref[...]
        go_ref[...] = g
        d_ref[...], mo_ref[...], vo_ref[...] = _adamw(w_ref[...], g, m_ref[...], v_ref[...])

    spec = pl.BlockSpec((tr, C), lambda i: (i, 0))
    return pl.pallas_call(
        body, name=name, grid=(R // tr,), in_specs=[spec] * 4,
        out_specs=(spec,) * 4, out_shape=(jax.ShapeDtypeStruct((R, C), F32),) * 4,
        compiler_params=_params(("arbitrary",)))(w, m, v, g)


R_GF, R_NG, R_LOSS, R_RNG, R_LGF, R_LGB, R_SHIFT, R_SCALE, R_GATE, R_SHIFT_C, R_SCALE_C, R_RNG2, R_RPB = (
    0, 1, 2, 3, 4, 5, 6, 8, 10, 12, 13, 14, 16)
W_GF, W_NG, W_CCTX, W_RNG, W_DF, W_DB, W_BADA, W_RPB = 0, 1, 2, 3, 4, 5, 6, 9


SMALL = (("final_norm_g", W_GF, 1, D), ("norm_g", W_NG, 1, D), ("c_ctx", W_CCTX, 1, D),
         ("ret_norm_g", W_RNG, 1, 512), ("ret_decay_fwd", W_DF, 1, 4), ("ret_decay_bwd", W_DB, 1, 4),
         ("b_ada", W_BADA, 3, D), ("na_rpb", W_RPB, 4, D))
N_SMALL = len(SMALL)


def _small_final_call(sm_all, c_t, wada_f, wada, m_ada, v_ada, small_w, small_m, small_v, B):
    ws = wada.shape[1]
    NB = N_DEV * B

    def body(*refs):
        sm_ref, ct_ref, wf_ref, wa_ref, ma_ref, va_ref = refs[:6]
        ins = refs[6:6 + 3 * N_SMALL]
        outs = refs[6 + 3 * N_SMALL:6 + 7 * N_SMALL]
        ga_ref, da_ref, mao_ref, vao_ref, loss_ref, dmod_ref, pk_ref = refs[6 + 7 * N_SMALL:]
        x, y, _ = _mesh_pos()
        s = 2 * x + y
        tot = sm_ref[0]
        for dv in range(1, N_DEV):
            tot = tot + sm_ref[dv]
        pk_ref[...] = jnp.zeros_like(pk_ref)
        for kind in range(3):
            for i, (_, row, nrow, width) in enumerate(SMALL):
                ref = ins[kind * N_SMALL + i]
                if nrow == 3:
                    for part in range(3):
                        pk_ref[kind, row + part:row + part + 1, :] = ref[:, part * D:(part + 1) * D]
                else:
                    pk_ref[kind, row:row + nrow, 0:width] = ref[...]
        w = pk_ref[0]
        cctx_ref = ins[2]
        for dv in range(N_DEV):
            for b in range(B):
                r = dv * B + b
                for part, row in enumerate((R_SHIFT, R_SCALE, R_GATE)):
                    dmod_ref[r:r + 1, part * D:(part + 1) * D] = sm_ref[dv, row + b:row + b + 1, :]
        dmod_ref[NB:NB + 1, 0:D] = tot[R_SHIFT_C:R_SHIFT_C + 1, :]
        dmod_ref[NB:NB + 1, D:2 * D] = tot[R_SCALE_C:R_SCALE_C + 1, :]
        dmod_ref[NB:NB + 1, 2 * D:3 * D] = jnp.zeros((1, D), F32)
        dmod_ref[NB + 1:, :] = jnp.zeros((dmod_ref.shape[0] - NB - 1, 3 * D), F32)
        dmod = dmod_ref[...]
        cc = cctx_ref[...]
        scc = _sigmoid(cc)
        ct = ct_ref[...]
        act_t = ct * _sigmoid(ct)
        dmc = dmod[NB:NB + 1, :].astype(BF16)
        dact = jnp.zeros((1, D), F32)
        for sh in range(N_SHARD):
            dact = dact + _dot_nt(dmc[:, sh * ws:(sh + 1) * ws], wf_ref[sh])
        g = jnp.zeros((16, D), F32)
        rows = lax.broadcasted_iota(jnp.int32, (16, D), 0)

        def put(g, row, val):
            return jnp.where(rows == row, val, g)

        g = put(g, W_GF, tot[R_GF:R_GF + 1, :])
        g = put(g, W_NG, tot[R_NG:R_NG + 1, :])
        g = put(g, W_CCTX, dact * (scc * (1.0 + cc * (1.0 - scc))))
        g = put(g, W_RNG, tot[R_RNG:R_RNG + 1, :] + tot[R_RNG2:R_RNG2 + 1, :])
        g = put(g, W_DF, tot[R_LGF:R_LGF + 1, :] * (-jnp.exp(w[W_DF:W_DF + 1, :])))
        g = put(g, W_DB, tot[R_LGB:R_LGB + 1, :] * (-jnp.exp(w[W_DB:W_DB + 1, :])))
        db = jnp.sum(dmod, axis=0, keepdims=True)
        for part in range(3):
            g = put(g, W_BADA + part, db[:, part * D:(part + 1) * D])
        for part in range(4):
            g = put(g, W_RPB + part, tot[R_RPB + part:R_RPB + part + 1, :])
        for kind, val in enumerate((g,) + _adamw(w, g, pk_ref[1], pk_ref[2])):
            for i, (_, row, nrow, width) in enumerate(SMALL):
                out = outs[kind * N_SMALL + i]
                if nrow == 3:
                    for part in range(3):
                        out[:, part * D:(part + 1) * D] = val[row + part:row + part + 1, :]
                else:
                    out[...] = val[row:row + nrow, 0:width]
        loss_ref[...] = jnp.broadcast_to(
            (0.5 / D) * jnp.sum(tot[R_LOSS:R_LOSS + 1, :], axis=1, keepdims=True), (8, 128))
        for sh in range(N_SHARD):
            @pl.when(s == sh)
            def _():
                ga = jnp.dot(act_t, dmod[:, sh * ws:(sh + 1) * ws], precision=HIGHEST,
                             preferred_element_type=F32)
                ga_ref[...] = ga
                da_ref[...], mao_ref[...], vao_ref[...] = _adamw(wa_ref[...], ga, ma_ref[...], va_ref[...])

    sh_small = tuple(jax.ShapeDtypeStruct(a.shape, F32) for a in small_w)
    sh_ada = jax.ShapeDtypeStruct(wada.shape, F32)
    res = pl.pallas_call(
        body, name="small_final",
        out_shape=sh_small * 4 + (sh_ada,) * 4 + (jax.ShapeDtypeStruct((8, 128), F32),),
        scratch_shapes=[pltpu.VMEM((NB + 8, 3 * D), F32), pltpu.VMEM((3, 16, D), F32)],
        compiler_params=_params(vmem_mb=56))(
            sm_all, c_t, wada_f, wada, m_ada, v_ada, *small_w, *small_m, *small_v)
    smalls = [res[k * N_SMALL:(k + 1) * N_SMALL] for k in range(4)]
    return smalls, res[4 * N_SMALL:4 * N_SMALL + 4], res[4 * N_SMALL + 4]


def _local_step(order, x, ctx, c_rows, norm_g, wada_b, b_shard, win_b, bias, dec_f, dec_b, ret_norm_g,
                wout_b, final_g, target):
    B, L, _ = x.shape
    LC = ctx.shape[1]
    assert B == 2
    cos2, sin2 = _rope_tables(L, LC)
    mod_part = _mod_part_call(c_rows, wada_b, b_shard)
    P, h, win_f, wout_f, mod = _inproj_gather_call(order, x, ctx, mod_part, norm_g, win_b, wout_b, cos2, sin2)
    y_na, o_na, wada_f = _na_fwd_call(P, bias, wada_b, L, LC)
    sf, sb = _ret_states_call(P, dec_f, dec_b, L, LC)
    y_ret, o_ret = _retc_fwd_call(P, sf, sb, dec_f, dec_b, ret_norm_g, L)
    dY, dx2, dwout_p, sm_out = _out_call(y_na, y_ret, x, target, mod, final_g, wout_f.reshape(D, D))
    dnq, dng, dnk, dnv, dbias = _na_bwd_call(P, bias, dY, o_na, L, LC)
    drq, drg, drk, drv, dgn, dlg = _retc_bwd_call(P, sf, sb, dec_f, dec_b, ret_norm_g, o_ret, dY, cos2, sin2, L, LC)
    dsec = (dnq, dnk, dnv, dng, drq, drk, drv, drg)
    dwin_b = _dw_call(dsec, h, L)
    cp_in, cp_out, drpb, dlg_sum = _grad_halves_call(
        dwin_b, dwout_p.reshape(N_SHARD, D // N_SHARD, D), dbias, dlg)
    grad_x, sm_dh, sl_in, sl_out = _dh_call(dsec, win_f, x, ctx, dx2, mod, norm_g, cp_in, cp_out)
    z = jnp.zeros((1, D), F32)
    pad = lambda v: jnp.pad(v.reshape(1, -1), ((0, 0), (0, D - v.size)))
    dlg_sum = dlg_sum.reshape(4, 8, 128)
    rpb_rows = jnp.pad(drpb[:, :15, :31].reshape(-1), (0, 4 * D - drpb.shape[0] * 465)).reshape(4, D)
    small = jnp.concatenate([
        sm_out[0:1], sm_dh[0:1], sm_out[1:2], pad(dgn[0]), pad(dlg_sum[:, 0, 0]), pad(dlg_sum[:, 1, 0]),
        sm_dh[3:5], sm_dh[5:7], sm_out[2:4], sm_dh[1:2], sm_dh[2:3], pad(dgn[1]), z, rpb_rows,
        jnp.zeros((SM_ROWS - 20, D), F32)], axis=0)
    return grad_x, sl_in, sl_out, small, wada_f


def kernel(x, c, ctx, c_ctx, norm_g, w_ada, b_ada, w_in, na_rpb, ret_decay_fwd, ret_decay_bwd, ret_norm_g, w_out, final_norm_g, loss_target, m_c_ctx, m_norm_g, m_w_ada, m_b_ada, m_w_in, m_na_rpb, m_ret_decay_fwd, m_ret_decay_bwd, m_ret_norm_g, m_w_out, m_final_norm_g, v_c_ctx, v_norm_g, v_w_ada, v_b_ada, v_w_in, v_na_rpb, v_ret_decay_fwd, v_ret_decay_bwd, v_ret_norm_g, v_w_out, v_final_norm_g):
    B = x.shape[0]
    c_all, bias = _c_gather_call(c, na_rpb[0].reshape(na_rpb.shape[1], -1))
    c_rows = jnp.concatenate([c_all.reshape(N_DEV * B, D), c_ctx.reshape(1, D), jnp.zeros((7, D), F32)], axis=0)
    mx, my = lax.axis_index("x"), lax.axis_index("y")
    order = jnp.stack([2 * mx + my, 2 * (1 - mx) + my, 2 * mx + (1 - my),
                       2 * (1 - mx) + (1 - my)]).astype(jnp.int32)
    ws = w_ada.shape[2]
    b_shard = lax.dynamic_slice(b_ada, (0, (2 * mx + my) * ws), (1, ws))
    grad_x, sl_in, sl_out, small, wada_f = _local_step(
        order, x, ctx, c_rows, norm_g, w_ada[0].astype(BF16), b_shard, w_in[0].astype(BF16), bias, ret_decay_fwd,
        ret_decay_bwd, ret_norm_g, w_out[0].astype(BF16), final_norm_g.reshape(1, D), loss_target)
    gin, gout, sm_all = _grad_finish_call(sl_in, sl_out, small)
    g_win, d_win, nm_win, nv_win = _adam_call(
        w_in[0], m_w_in[0], v_w_in[0], gin.reshape(w_in.shape[1:]), "adam_w_in")
    g_wout, d_wout, nm_wout, nv_wout = _adam_call(
        w_out[0], m_w_out[0], v_w_out[0], gout.reshape(w_out.shape[1:]), "adam_w_out")

    def small_inputs(gf, ng, cc, rng, df, db, bada, rpb):
        return (gf.reshape(1, D), ng, cc.reshape(1, D), rng, df, db, bada,
                jnp.pad(rpb.reshape(-1), (0, 4 * D - rpb.size)).reshape(4, D))

    c_t = c_rows.T
    smalls, adas, loss = _small_final_call(
        sm_all, c_t, wada_f, w_ada[0], m_w_ada[0], v_w_ada[0],
        small_inputs(final_norm_g, norm_g, c_ctx, ret_norm_g, ret_decay_fwd, ret_decay_bwd, b_ada, na_rpb),
        small_inputs(m_final_norm_g, m_norm_g, m_c_ctx, m_ret_norm_g, m_ret_decay_fwd, m_ret_decay_bwd, m_b_ada,
                     m_na_rpb),
        small_inputs(v_final_norm_g, v_norm_g, v_c_ctx, v_ret_norm_g, v_ret_decay_fwd, v_ret_decay_bwd, v_b_ada,
                     v_na_rpb), B)
    res = []
    for p, ada, win_o, wout_o in zip(smalls, adas, (g_win, d_win, nm_win, nv_win),
                                     (g_wout, d_wout, nm_wout, nv_wout)):
        gf, ng, cc, rng, df, db, bada, rpb = p
        res.append([cc.reshape(D), ng, ada[None], bada, win_o[None],
                    rpb.reshape(-1)[:na_rpb.size].reshape(na_rpb.shape), df, db, rng, wout_o[None], gf.reshape(D)])
    return (loss[0, 0], grad_x, *res[0], *res[1], *res[2], *res[3])
```

```python
import numpy as np
import jax
import jax.numpy as jnp
from jax import lax
from jax.experimental import pallas as pl
from jax.experimental.pallas import tpu as pltpu

F32 = jnp.float32
BF16 = jnp.bfloat16
HIGHEST = lax.Precision.HIGHEST

D = 1024
GRID_W = 64
NA_DH = 64
RET_DK = 128
ROPE_BASE = 10000.0
EPS = 1e-6
NEG = -1e30
TQ = 256
KW = 12 * GRID_W
N_SHARD = 4
N_DEV = 8
SM_ROWS = 24

ADAM_LR = 0.001
ADAM_B1 = 0.9
ADAM_B2 = 0.999
ADAM_EPS = 1e-08
ADAM_WD = 0.01
ADAM_STEP = 10

MESH = pl.DeviceIdType.MESH
ANY = pl.BlockSpec(memory_space=pl.ANY)


def _params(sem=None, vmem_mb=48):
    return pltpu.CompilerParams(dimension_semantics=sem, vmem_limit_bytes=vmem_mb << 20)


def _dot(a, b):
    return jnp.dot(a, b, preferred_element_type=F32)


def _dot_nt(a, b):
    return lax.dot_general(a, b, (((1,), (1,)), ((), ())), preferred_element_type=F32)


def _dot_tn(a, b):
    return lax.dot_general(a, b, (((0,), (0,)), ((), ())), preferred_element_type=F32)


def _sigmoid(x):
    return 1.0 / (1.0 + jnp.exp(-x))


def _rope_tables(L, LC):
    half = RET_DK // 2
    nf = half // 2
    t = np.arange(L)
    row = (t // GRID_W).astype(np.float32)
    col = (t % GRID_W).astype(np.float32)
    inv = (np.float32(ROPE_BASE) ** (-np.arange(nf, dtype=np.float32) / np.float32(nf))).astype(np.float32)
    ang = np.concatenate([row[:, None] * inv, col[:, None] * inv], axis=-1).astype(np.float32)
    cos, sin = np.cos(ang).astype(np.float32), np.sin(ang).astype(np.float32)
    cos2 = np.concatenate([cos, cos], axis=-1)
    sin2 = np.concatenate([-sin, sin], axis=-1)
    cos2 = np.concatenate([cos2, np.ones((LC, RET_DK), np.float32)], axis=0)
    sin2 = np.concatenate([sin2, np.zeros((LC, RET_DK), np.float32)], axis=0)
    return jnp.asarray(cos2), jnp.asarray(sin2)


def _mod_part_call(c_rows, wada_b, b_shard):
    def body(c_ref, w_ref, b_ref, o_ref):
        a = c_ref[...]
        o_ref[...] = _dot((a * _sigmoid(a)).astype(BF16), w_ref[...]) + b_ref[...]

    return pl.pallas_call(
        body, name="ada_mod", out_shape=jax.ShapeDtypeStruct((c_rows.shape[0], wada_b.shape[1]), F32),
        compiler_params=_params())(c_rows, wada_b, b_shard)


def _dc_masks():
    cq = lax.broadcasted_iota(jnp.int32, (GRID_W, GRID_W), 0)
    ck = lax.broadcasted_iota(jnp.int32, (GRID_W, GRID_W), 1)
    dc = jnp.clip(ck - cq + 15, 0, 30)
    c0 = jnp.clip(cq - 8, 0, GRID_W - 16)
    col_ok = (ck >= c0) & (ck < c0 + 16)
    return dc, col_ok


def _bias_blocks():
    out = []
    for typ, delta in enumerate((4, 0, -4)):
        for rq in range(4):
            for rkk in range(12):
                dr = rkk + delta - rq - 4
                if typ == 0:
                    ok = -rq <= dr <= 7 - rq
                elif typ == 1:
                    ok = -4 <= dr <= 3
                else:
                    ok = -4 - rq <= dr <= 3 - rq
                out.append((typ, rq, rkk, dr if ok else None))
    return out


def _bias_body(r_ref, bias_ref, et_ref, out_ref, sem):
    dc, col_ok = _dc_masks()
    masks = [(dc == j).astype(F32) for j in range(31)]
    nh = bias_ref.shape[0]

    def per_h(h, carry):
        for dr in range(15):
            t = jnp.zeros((GRID_W, GRID_W), F32)
            for j in range(31):
                t = t + masks[j] * r_ref[h, dr * 31 + j]
            et_ref[dr] = jnp.where(col_ok, t, NEG)
        neg = jnp.full((GRID_W, GRID_W), NEG, F32)
        for typ, rq, rkk, dr in _bias_blocks():
            blk = neg if dr is None else et_ref[dr + 7]
            bias_ref[h, typ, rq * 64:(rq + 1) * 64, rkk * 64:(rkk + 1) * 64] = blk
        pltpu.make_async_copy(bias_ref.at[h], out_ref.at[h], sem).start()
        return carry

    lax.fori_loop(0, nh, per_h, 0)
    return [pltpu.make_async_copy(bias_ref.at[h], out_ref.at[h], sem) for h in range(nh)]


def _bias_tile_sums(db_ref, hh):
    acc = {}
    for typ, rq, rkk, dr in _bias_blocks():
        if dr is None:
            continue
        blk = db_ref[hh, typ, rq * 64:(rq + 1) * 64, rkk * 64:(rkk + 1) * 64]
        acc[dr] = blk if dr not in acc else acc[dr] + blk
    return acc


def _small_reduce_body(dt_ref, dlg_ref, drpb_ref, dlgo_ref, p_ref):
    dc, _ = _dc_masks()
    masks = [(dc == j).astype(F32) for j in range(31)]
    ones = jnp.ones((8, GRID_W), F32)
    p_ref[...] = jnp.zeros_like(p_ref)
    drpb_ref[...] = jnp.zeros_like(drpb_ref)

    def per_h(h, carry):
        for dr in range(-7, 8):
            t = dt_ref[h, dr + 7]
            for j in range(31):
                p_ref[j:j + 1, :] = jnp.sum(t * masks[j], axis=0, keepdims=True)
            red = lax.dot_general(ones, p_ref[...], (((1,), (1,)), ((), ())),
                                  precision=HIGHEST, preferred_element_type=F32)
            drpb_ref[h, dr + 7:dr + 8, :] = red[0:1, :]
        return carry

    lax.fori_loop(0, dt_ref.shape[0], per_h, 0)
    x = dlg_ref[0]
    for b in range(1, dlg_ref.shape[0]):
        x = x + dlg_ref[b]
    x = x.reshape(4 * 8, x.shape[-1])
    dlgo_ref[...] = jnp.dot(x, jnp.ones((x.shape[-1], 128), F32), precision=HIGHEST,
                            preferred_element_type=F32)


def _inproj_gather_call(order, x, ctx, mod_part, norm_g, win_b, wout_b, cos2, sin2):
    B, L, _ = x.shape
    LC = ctx.shape[1]
    T = L + LC
    TI = 2 * TQ
    nl = L // TI
    nt = nl + 1
    assert LC == TQ and L % TI == 0
    kscale = RET_DK ** -0.5
    HR = D // 2
    pad_rows = nt * TI - T
    cos2 = jnp.pad(cos2, ((0, pad_rows), (0, 0)))
    sin2 = jnp.pad(sin2, ((0, pad_rows), (0, 0)))

    MW = mod_part.shape[1]
    NB = N_DEV * B

    def body(ord_ref, x_ref, ctx_ref, mp_ref, g_ref, wown_ref, woown_ref, cos_ref, sin_ref,
             p_ref, h_ref, wf_ref, wof_ref, modo_ref, w_all, wo_all, hs_ref, mp_all, mod_ref, ssem, rsem, lsem):
        j, b, t = pl.program_id(0), pl.program_id(1), pl.program_id(2)
        first = (b == 0) & (t == 0)
        mx, my, mc = _mesh_pos()
        s = 2 * mx + my

        m_send = [_remote(mp_ref, mp_all.at[s], ssem, rsem, 12 + k, (px, py, mc))
                  for k, (px, py) in enumerate(_other_chips(mx, my))]
        m_recv = [_remote(mp_ref, mp_all.at[2 * px + py], ssem, rsem, 12 + k, (px, py, mc))
                  for k, (px, py) in enumerate(_other_chips(mx, my))]

        @pl.when(first & (j == 0))
        def _():
            for cp in m_send:
                cp.start()
            mp_all[s] = mp_ref[...]
            for cp in m_recv:
                cp.wait_recv()
            me = 4 * mx + 2 * my + mc
            mod_ref[...] = jnp.zeros_like(mod_ref)
            for p in range(N_SHARD):
                for r in range(B):
                    mod_ref[r:r + 1, p * MW:(p + 1) * MW] = mp_all[p, pl.ds(B * me + r, 1), :]
                mod_ref[B:B + 1, p * MW:(p + 1) * MW] = mp_all[p, NB:NB + 1, :]
            modo_ref[...] = mod_ref[...]

        sems = (ssem, rsem, lsem)
        own, ici_send, ici_recv, fwd_send, fwd_recv, outs = _gather_copies(wown_ref, w_all, wf_ref, HR, sems, 0, 0)
        oown, o_send, o_recv, o_fsend, o_frecv, o_outs = _gather_copies(
            woown_ref, wo_all, wof_ref, woown_ref.shape[0] // 2, sems, 6, 5)

        @pl.when(first & (j == 0))
        def _():
            own.start()
            oown.start()
            own.wait()
            ici_send[0].start()
            ici_send[1].start()
            outs[0].start()
            oown.wait()

        for k in range(3):
            @pl.when(first & (j == k + 1))
            def _(k=k):
                ici_recv[k].wait_recv()
                if k == 0:
                    ici_send[2].start()
                fwd_send[k].start()
                fwd_recv[k].wait_recv()
                outs[1 + k].start()
                if k == 1:
                    for cp in o_send:
                        cp.start()
                if k == 2:
                    for got, fwd in zip(o_recv, o_fsend):
                        got.wait_recv()
                        fwd.start()

        tile = b * nt + t

        @pl.when(j == 0)
        def _():
            is_lat = t < nl
            ctx_tile = jnp.concatenate([ctx_ref[...], jnp.zeros((TI - LC, D), F32)], axis=0)
            xt = jnp.where(is_lat, x_ref[...], ctx_tile)
            mrow = mod_ref[pl.ds(jnp.where(is_lat, b, B), 1), :]
            shift, scale = mrow[:, 0:D], mrow[:, D:2 * D]
            rstd = lax.rsqrt(jnp.mean(xt * xt, axis=-1, keepdims=True) + EPS)
            h0 = ((xt * rstd * g_ref[...]) * (1.0 + scale) + shift).astype(BF16)
            h_ref[...] = h0
            hs_ref[tile] = h0

        hb = hs_ref[tile]
        cs, sn = cos_ref[...], sin_ref[...]
        shard = ord_ref[j]
        for sh in range(N_SHARD):
            @pl.when(shard == sh)
            def _(sh=sh):
                for half in range(2):
                    sec = 2 * sh + half
                    acc = _dot(hb, w_all[sh, :, half * 512:(half + 1) * 512])
                    if sec == 0:
                        acc = acc * (NA_DH ** -0.5)
                    if sec in (4, 5):
                        for q in range(4):
                            a = acc[:, q * 128:(q + 1) * 128]
                            r = a * cs + pltpu.roll(a, 64, 1) * sn
                            if sec == 5:
                                r = r * kscale
                            p_ref[:, half * 512 + q * 128:half * 512 + (q + 1) * 128] = r.astype(BF16)
                    else:
                        p_ref[:, half * 512:(half + 1) * 512] = acc.astype(BF16)

        @pl.when((j == N_SHARD - 1) & (b == B - 1) & (t == nt - 1))
        def _():
            for cp in o_frecv:
                cp.wait_recv()
            for cp in o_outs:
                cp.start()
            _finish(outs + o_outs, ici_send + fwd_send + o_send + o_fsend + m_send, [])

    tok = lambda j, b, t, o: (jnp.where(j == 0, b, B - 1), jnp.where(j == 0, jnp.minimum(t, nl - 1), nl - 1), 0)
    grid_spec = pltpu.PrefetchScalarGridSpec(
        num_scalar_prefetch=1, grid=(N_SHARD, B, nt),
        in_specs=[
            pl.BlockSpec((None, TI, D), tok),
            pl.BlockSpec((None, LC, D), lambda j, b, t, o: (jnp.where(j == 0, b, B - 1), 0, 0)),
            pl.BlockSpec(mod_part.shape, lambda j, b, t, o: (0, 0)),
            pl.BlockSpec((1, D), lambda j, b, t, o: (0, 0)),
            ANY, ANY,
            pl.BlockSpec((TI, RET_DK), lambda j, b, t, o: (t, 0)),
            pl.BlockSpec((TI, RET_DK), lambda j, b, t, o: (t, 0)),
        ],
        out_specs=(pl.BlockSpec((None, TI, D), lambda j, b, t, o: (b, t, o[j])),
                   pl.BlockSpec((None, TI, D), lambda j, b, t, o: (
                       jnp.where(j == 0, b, B - 1), jnp.where(j == 0, t, nt - 1), 0)), ANY, ANY,
                   pl.BlockSpec((8, 3 * D), lambda j, b, t, o: (0, 0))),
        scratch_shapes=[pltpu.VMEM((N_SHARD, D, D), BF16), pltpu.VMEM((N_SHARD,) + wout_b.shape, BF16),
                        pltpu.VMEM((B * nt, TI, D), BF16),
                        pltpu.VMEM((N_SHARD,) + mod_part.shape, F32), pltpu.VMEM((8, 3 * D), F32),
                        pltpu.SemaphoreType.DMA((15,)), pltpu.SemaphoreType.DMA((15,)),
                        pltpu.SemaphoreType.DMA((10,))])
    return pl.pallas_call(
        body, name="in_proj", grid_spec=grid_spec,
        out_shape=(jax.ShapeDtypeStruct((B, T, 4 * D), BF16), jax.ShapeDtypeStruct((B, T, D), BF16),
                   jax.ShapeDtypeStruct((N_SHARD, D, D), BF16),
                   jax.ShapeDtypeStruct((N_SHARD,) + wout_b.shape, BF16),
                   jax.ShapeDtypeStruct((8, 3 * D), F32)),
        compiler_params=_params(("arbitrary",) * 3, vmem_mb=56))(
            order, x, ctx, mod_part, norm_g, win_b, wout_b, cos2, sin2)


def _na_specs(L, T, rows, nh=2):
    nm = rows // 4
    w = nh * NA_DH
    per = 512 // w
    q_spec = pl.BlockSpec((None, TQ, w), lambda hp, b, m: (b, m, hp))
    k_spec = pl.BlockSpec((None, T, w), lambda hp, b, m: (b, 0, per + hp))
    v_spec = pl.BlockSpec((None, T, w), lambda hp, b, m: (b, 0, 2 * per + hp))
    g_spec = pl.BlockSpec((None, TQ, w), lambda hp, b, m: (b, m, 3 * per + hp))
    bias_spec = pl.BlockSpec((nh, 3, TQ, KW), lambda hp, b, m: (hp, 0, 0, 0))
    return nm, q_spec, k_spec, v_spec, g_spec, bias_spec


def _na_tile(m, nm, rows):
    typ = jnp.where(m == 0, 0, jnp.where(m == nm - 1, 2, 1))
    start = pl.multiple_of(jnp.clip(4 * m - 4, 0, rows - 12) * GRID_W, TQ)
    return typ, start


def _na_fwd_call(P, bias, wada_b, L, LC):
    B, T, _ = P.shape
    rows = L // GRID_W
    NH = 4
    nm, q_spec, k_spec, v_spec, g_spec, bias_spec = _na_specs(L, T, rows, NH)
    ngrp = 8 // NH

    def body(q_ref, k_ref, v_ref, g_ref, bias_ref, wown_ref, y_ref, o_ref, wf_ref, w_all, ssem, rsem, lsem):
        hp, b, m = pl.program_id(0), pl.program_id(1), pl.program_id(2)
        own, send, recv, fsend, frecv, outs = _gather_copies(
            wown_ref, w_all, wf_ref, wown_ref.shape[0] // 2, (ssem, rsem, lsem), 0, 0)
        first = (b == 0) & (m == 0)

        @pl.when(first & (hp == 0))
        def _():
            own.start()
            own.wait()
            for cp in send:
                cp.start()
            outs[0].start()

        @pl.when((hp == ngrp - 1) & (b == B - 1) & (m == 0))
        def _():
            for got, fwd in zip(recv, fsend):
                got.wait_recv()
                fwd.start()

        @pl.when((hp == ngrp - 1) & (b == B - 1) & (m == nm - 1))
        def _():
            for cp in frecv:
                cp.wait_recv()
            for cp in outs[1:]:
                cp.start()
            _finish(outs, send + fsend, [])

        typ, start = _na_tile(m, nm, rows)
        for hh in range(NH):
            ln = slice(hh * NA_DH, (hh + 1) * NA_DH)
            q = q_ref[:, ln]
            kw, vw = k_ref[pl.ds(start, KW), ln], v_ref[pl.ds(start, KW), ln]
            kc, vc = k_ref[L:L + LC, ln], v_ref[L:L + LC, ln]
            s1 = _dot_nt(q, kw) + bias_ref[hh, typ]
            s2 = _dot_nt(q, kc)
            mx = jnp.maximum(jnp.max(s1, axis=-1, keepdims=True), jnp.max(s2, axis=-1, keepdims=True))
            p1, p2 = jnp.exp(s1 - mx), jnp.exp(s2 - mx)
            inv = 1.0 / (jnp.sum(p1, axis=-1, keepdims=True) + jnp.sum(p2, axis=-1, keepdims=True))
            o = (_dot(p1.astype(BF16), vw) + _dot(p2.astype(BF16), vc)) * inv
            g = g_ref[:, ln].astype(F32)
            o_ref[:, ln] = o.astype(BF16)
            y_ref[:, ln] = (o * (g * _sigmoid(g))).astype(BF16)

    tile = pl.BlockSpec((None, TQ, NH * NA_DH), lambda hp, b, m: (b, m, hp))
    return pl.pallas_call(
        body, name="na_fwd", grid=(ngrp, B, nm),
        in_specs=[q_spec, k_spec, v_spec, g_spec, bias_spec, ANY],
        out_specs=(tile, tile, ANY),
        out_shape=(jax.ShapeDtypeStruct((B, L, 512), BF16),) * 2
        + (jax.ShapeDtypeStruct((N_SHARD,) + wada_b.shape, BF16),),
        scratch_shapes=[pltpu.VMEM((N_SHARD,) + wada_b.shape, BF16),
                        pltpu.SemaphoreType.DMA((6,)), pltpu.SemaphoreType.DMA((6,)),
                        pltpu.SemaphoreType.DMA((5,))],
        compiler_params=_params(("arbitrary",) * 3, vmem_mb=56))(P, P, P, P, bias, wada_b)


def _na_bwd_call(P, bias, dY, o_na, L, LC):
    B, T, _ = P.shape
    rows = L // GRID_W
    NH = 4
    W = NH * NA_DH
    nm, q_spec, k_spec, v_spec, g_spec, bias_spec = _na_specs(L, T, rows, NH)
    scale = NA_DH ** -0.5

    RB = 32

    def body(q_ref, k_ref, v_ref, g_ref, bias_ref, dy_ref, o_ref, dq_ref, dg_ref, dk_ref, dv_ref, dt_ref,
             db_ref, s1_ref, s2_ref, dp1_ref, dp2_ref, p1_ref, p2_ref, ds1_ref, ds2_ref, dkt_ref, dvt_ref):
        b, m = pl.program_id(1), pl.program_id(2)
        typ, start = _na_tile(m, nm, rows)

        @pl.when(m == 0)
        def _():
            dkt_ref[...] = jnp.zeros_like(dkt_ref)
            dvt_ref[...] = jnp.zeros_like(dvt_ref)

        @pl.when((m == 0) & (b == 0))
        def _():
            db_ref[...] = jnp.zeros_like(db_ref)

        for hh in range(NH):
            ln = slice(hh * NA_DH, (hh + 1) * NA_DH)
            q = q_ref[:, ln]
            kw, vw = k_ref[pl.ds(start, KW), ln], v_ref[pl.ds(start, KW), ln]
            kc, vc = k_ref[L:L + LC, ln], v_ref[L:L + LC, ln]
            g = g_ref[:, ln].astype(F32)
            sg = _sigmoid(g)
            dy = dy_ref[:, ln].astype(F32)
            do = (dy * (g * sg)).astype(BF16)
            s1_ref[hh] = _dot_nt(q, kw)
            s2_ref[hh] = _dot_nt(q, kc)
            dp1_ref[hh] = _dot_nt(do, vw)
            dp2_ref[hh] = _dot_nt(do, vc)

            def rows_pass(r, carry, hh=hh):
                rw = pl.ds(pl.multiple_of(r * RB, RB), RB)
                a = s1_ref[hh, rw, :] + bias_ref[hh, typ, rw, :]
                c = s2_ref[hh, rw, :]
                mx = jnp.maximum(jnp.max(a, axis=-1, keepdims=True), jnp.max(c, axis=-1, keepdims=True))
                e1, e2 = jnp.exp(a - mx), jnp.exp(c - mx)
                inv = 1.0 / (jnp.sum(e1, axis=-1, keepdims=True) + jnp.sum(e2, axis=-1, keepdims=True))
                p1, p2 = e1 * inv, e2 * inv
                p1_ref[hh, rw, :] = p1.astype(BF16)
                p2_ref[hh, rw, :] = p2.astype(BF16)
                dp1, dp2 = dp1_ref[hh, rw, :], dp2_ref[hh, rw, :]
                delta = jnp.sum(p1 * dp1, axis=-1, keepdims=True) + jnp.sum(p2 * dp2, axis=-1, keepdims=True)
                ds1 = p1 * (dp1 - delta)
                db_ref[hh, typ, rw, :] += ds1
                ds1_ref[hh, rw, :] = ds1.astype(BF16)
                ds2_ref[hh, rw, :] = (p2 * (dp2 - delta)).astype(BF16)
                return carry

            lax.fori_loop(0, TQ // RB, rows_pass, 0, unroll=True)
            p1b, p2b, ds1b, ds2b = p1_ref[hh], p2_ref[hh], ds1_ref[hh], ds2_ref[hh]
            dg_ref[:, ln] = (dy * o_ref[:, ln].astype(F32) * (sg * (1.0 + g * (1.0 - sg)))).astype(BF16)
            dq_ref[:, ln] = ((_dot(ds1b, kw) + _dot(ds2b, kc)) * scale).astype(BF16)
            dkt_ref[ln, pl.ds(start, KW)] += _dot_tn(q, ds1b)
            dvt_ref[ln, pl.ds(start, KW)] += _dot_tn(do, p1b)
            dkt_ref[ln, L:L + LC] += _dot_tn(q, ds2b)
            dvt_ref[ln, L:L + LC] += _dot_tn(do, p2b)

        @pl.when(m == nm - 1)
        def _():
            dk_ref[...] = dkt_ref[...].T
            dv_ref[...] = dvt_ref[...].T

        @pl.when((m == nm - 1) & (b == B - 1))
        def _():
            for hh in range(NH):
                for dr, t in _bias_tile_sums(db_ref, hh).items():
                    dt_ref[hh, dr + 7] = t

    tile = pl.BlockSpec((None, TQ, W), lambda hp, b, m: (b, m, hp))
    kv_out = pl.BlockSpec((None, T, W), lambda hp, b, m: (b, 0, hp))
    wide, narrow = (NH, TQ, KW), (NH, TQ, LC)
    return pl.pallas_call(
        body, name="na_bwd", grid=(8 // NH, B, nm),
        in_specs=[q_spec, k_spec, v_spec, g_spec, bias_spec, tile, tile],
        out_specs=(tile, tile, kv_out, kv_out,
                   pl.BlockSpec((NH, 15, GRID_W, GRID_W), lambda hp, b, m: (hp, 0, 0, 0))),
        out_shape=(jax.ShapeDtypeStruct((B, L, 512), BF16), jax.ShapeDtypeStruct((B, L, 512), BF16),
                   jax.ShapeDtypeStruct((B, T, 512), F32), jax.ShapeDtypeStruct((B, T, 512), F32),
                   jax.ShapeDtypeStruct((bias.shape[0], 15, GRID_W, GRID_W), F32)),
        scratch_shapes=[pltpu.VMEM((NH,) + bias.shape[1:], F32),
                        pltpu.VMEM(wide, F32), pltpu.VMEM(narrow, F32), pltpu.VMEM(wide, F32), pltpu.VMEM(narrow, F32),
                        pltpu.VMEM(wide, BF16), pltpu.VMEM(narrow, BF16), pltpu.VMEM(wide, BF16),
                        pltpu.VMEM(narrow, BF16), pltpu.VMEM((W, T), F32), pltpu.VMEM((W, T), F32)],
        compiler_params=_params(("arbitrary",) * 3, vmem_mb=60))(P, P, P, P, bias, dY, o_na)


def _head_scalar(dec_ref, h):
    lane = lax.broadcasted_iota(jnp.int32, dec_ref.shape, 1)
    return -jnp.sum(jnp.where(lane == h, jnp.exp(dec_ref[...]), 0.0), axis=1, keepdims=True)


def _chunk_decay(lgf, lgb):
    tau = lax.broadcasted_iota(jnp.int32, (TQ, 1), 0).astype(F32)
    sig = lax.broadcasted_iota(jnp.int32, (1, TQ), 1).astype(F32)
    dist = tau - sig
    dm = jnp.exp(dist * jnp.where(dist > 0, lgf, -lgb)) * jnp.where(dist == 0, 2.0, 1.0)
    return tau, dist, dm


def _ret_states_call(P, dec_f, dec_b, L, LC):
    B, T, _ = P.shape
    n = L // TQ

    def body(df_ref, db_ref, k_ref, v_ref, sf_ref, sb_ref):
        h = pl.program_id(1)
        lgf, lgb = _head_scalar(df_ref, h), _head_scalar(db_ref, h)
        tau = lax.broadcasted_iota(jnp.int32, (TQ, 1), 0).astype(F32)
        jc = lax.broadcasted_iota(jnp.int32, (LC, 1), 0).astype(F32)
        wf, wb = jnp.exp(lgf * (TQ - 1.0 - tau)), jnp.exp(lgb * tau)
        gcf, gcb = jnp.exp(lgf * float(TQ)), jnp.exp(lgb * float(TQ))
        kc, vc = k_ref[L:L + LC, :].astype(F32), v_ref[L:L + LC, :]

        def chunk_state(i, w):
            ks = pl.multiple_of(i * TQ, TQ)
            return _dot_tn((k_ref[pl.ds(ks, TQ), :].astype(F32) * w).astype(BF16), v_ref[pl.ds(ks, TQ), :])

        def fwd(i, s):
            sf_ref[i] = s
            return gcf * s + chunk_state(i, wf)

        lax.fori_loop(0, n, fwd, _dot_tn((kc * jnp.exp(lgf * (LC - 1.0 - jc))).astype(BF16), vc), unroll=True)

        def bwd(r, s):
            i = n - 1 - r
            sb_ref[i] = s
            return gcb * s + chunk_state(i, wb)

        lax.fori_loop(0, n, bwd, _dot_tn((kc * jnp.exp(lgb * jc)).astype(BF16), vc), unroll=True)

    st = pl.BlockSpec((None, None, n, RET_DK, RET_DK), lambda b, h: (b, h, 0, 0, 0))
    return pl.pallas_call(
        body, name="ret_states", grid=(B, 4),
        in_specs=[pl.BlockSpec((1, 4), lambda b, h: (0, 0)), pl.BlockSpec((1, 4), lambda b, h: (0, 0)),
                  pl.BlockSpec((None, T, 128), lambda b, h: (b, 0, 20 + h)),
                  pl.BlockSpec((None, T, 128), lambda b, h: (b, 0, 24 + h))],
        out_specs=(st, st),
        out_shape=(jax.ShapeDtypeStruct((B, 4, n, RET_DK, RET_DK), F32),) * 2,
        compiler_params=_params(("arbitrary",) * 2))(dec_f, dec_b, P, P)


def _retc_fwd_call(P, sf, sb, dec_f, dec_b, ret_norm_g, L):
    B, T, _ = P.shape
    sec = lambda k: pl.BlockSpec((None, TQ, 512), lambda b, i: (b, i, k))
    dec_spec = pl.BlockSpec((1, 4), lambda b, i: (0, 0))
    st_spec = pl.BlockSpec((None, 4, None, RET_DK, RET_DK), lambda b, i: (b, 0, i, 0, 0))

    def body(df_ref, db_ref, q_ref, k_ref, v_ref, g_ref, gn_ref, sf_ref, sb_ref, y_ref, o_ref):
        for h in range(4):
            ln = slice(h * RET_DK, (h + 1) * RET_DK)
            lgf, lgb = _head_scalar(df_ref, h), _head_scalar(db_ref, h)
            tau, _, dm = _chunk_decay(lgf, lgb)
            q = q_ref[:, ln]
            qf = q.astype(F32)
            acc = _dot((_dot_nt(q, k_ref[:, ln]) * dm).astype(BF16), v_ref[:, ln])
            acc = acc + _dot((qf * jnp.exp(lgf * (tau + 1.0))).astype(BF16), sf_ref[h].astype(BF16))
            acc = acc + _dot((qf * jnp.exp(lgb * (TQ - tau))).astype(BF16), sb_ref[h].astype(BF16))
            o_ref[:, ln] = acc
            rn = lax.rsqrt(jnp.mean(acc * acc, axis=-1, keepdims=True) + EPS)
            g = g_ref[:, ln].astype(F32)
            y_ref[:, ln] = ((acc * rn * gn_ref[:, ln]) * (g * _sigmoid(g))).astype(BF16)

    tile = pl.BlockSpec((None, TQ, 512), lambda b, i: (b, i, 0))
    return pl.pallas_call(
        body, name="ret_fwd", grid=(B, L // TQ),
        in_specs=[dec_spec, dec_spec, sec(4), sec(5), sec(6), sec(7),
                  pl.BlockSpec((1, 512), lambda b, i: (0, 0)), st_spec, st_spec],
        out_specs=(tile, tile),
        out_shape=(jax.ShapeDtypeStruct((B, L, 512), BF16), jax.ShapeDtypeStruct((B, L, 512), F32)),
        compiler_params=_params(("arbitrary",) * 2))(dec_f, dec_b, P, P, P, P, ret_norm_g, sf, sb)


def _retc_bwd_call(P, sf, sb, dec_f, dec_b, ret_norm_g, o_ret, dY, cos2, sin2, L, LC):
    B, T, _ = P.shape
    n = L // TQ
    C = float(TQ)
    kscale = RET_DK ** -0.5
    st_spec = pl.BlockSpec((None, 4, n, RET_DK, RET_DK), lambda b, i: (b, 0, 0, 0, 0))

    def body(df_ref, db_ref, q_ref, k_ref, v_ref, g_ref, gn_ref, o_ref, dy_ref, cos_ref, sin_ref, sf_ref, sb_ref,
             dq_ref, dg_ref, dk_ref, dv_ref, dgn_ref, dlg_ref, dsf_ref, dsb_ref):
        i = pl.program_id(1)

        @pl.when(i == 0)
        def _():
            dk_ref[...] = jnp.zeros_like(dk_ref)
            dv_ref[...] = jnp.zeros_like(dv_ref)
            dgn_ref[...] = jnp.zeros_like(dgn_ref)
            dlg_ref[...] = jnp.zeros_like(dlg_ref)

        rows = pl.ds(pl.multiple_of(i * TQ, TQ), TQ)
        cs, sn = cos_ref[rows, :], sin_ref[rows, :]

        def one_head(h):
            ln = slice(h * RET_DK, (h + 1) * RET_DK)
            lgf, lgb = _head_scalar(df_ref, h), _head_scalar(db_ref, h)
            tau, dist, dm = _chunk_decay(lgf, lgb)

            def add_lg(row, x):
                csum = jnp.sum(x, axis=0, keepdims=True)
                tot = csum[:, 0:128]
                for part in range(1, x.shape[1] // 128):
                    tot = tot + csum[:, part * 128:(part + 1) * 128]
                dlg_ref[h, row:row + 1, :] += tot

            q = q_ref[:, ln]
            qf = q.astype(F32)
            o = o_ref[:, ln]
            g = g_ref[:, ln].astype(F32)
            dy = dy_ref[:, ln].astype(F32)
            gn = gn_ref[:, ln]
            sg = _sigmoid(g)
            rn = lax.rsqrt(jnp.mean(o * o, axis=-1, keepdims=True) + EPS)
            nrm = o * rn
            dg_ref[:, ln] = (dy * (nrm * gn) * (sg * (1.0 + g * (1.0 - sg)))).astype(BF16)
            dhn = dy * (g * sg)
            dgn_ref[:, ln] += jnp.sum(dhn * nrm, axis=0, keepdims=True)
            dnrm = dhn * gn
            do = rn * (dnrm - nrm * jnp.mean(dnrm * nrm, axis=-1, keepdims=True))
            dob = do.astype(BF16)
            ki, vi = k_ref[rows, ln], v_ref[rows, ln]
            s = _dot_nt(q, ki)
            dsv = _dot_nt(dob, vi)
            dsb = (dsv * dm).astype(BF16)
            dk_ref[rows, ln] += _dot_tn(dsb, q)
            dv_ref[rows, ln] += _dot_tn((s * dm).astype(BF16), dob)
            xw = s * dsv * dm * jnp.abs(dist)
            fpart = jnp.where(dist > 0, xw, 0.0)
            add_lg(0, fpart)
            add_lg(1, xw - fpart)
            dq = _dot(dsb, ki)
            af, ab = jnp.exp(lgf * (tau + 1.0)), jnp.exp(lgb * (C - tau))
            qa, qb = (qf * af).astype(BF16), (qf * ab).astype(BF16)
            sfi, sbi = sf_ref[h, i].astype(BF16), sb_ref[h, i].astype(BF16)
            dq = dq + af * _dot_nt(dob, sfi) + ab * _dot_nt(dob, sbi)
            dsf_ref[h, i] = _dot_tn(qa, dob)
            dsb_ref[h, i] = _dot_tn(qb, dob)
            add_lg(0, (tau + 1.0) * (_dot(qa, sfi) * do))
            add_lg(1, (C - tau) * (_dot(qb, sbi) * do))
            dq_ref[:, ln] = (dq * cs - pltpu.roll(dq, 64, 1) * sn).astype(BF16)

            @pl.when(i == n - 1)
            def _():
                jc = lax.broadcasted_iota(jnp.int32, (LC, 1), 0).astype(F32)
                crow = pl.ds(L, LC)

                def through_state(rws, w, dw, gst, row):
                    kk, vv = k_ref[rws, ln].astype(F32), v_ref[rws, ln]
                    gb = gst.astype(BF16)
                    vg = _dot_nt(vv, gb)
                    kw = kk * w
                    dk_ref[rws, ln] += w * vg
                    dv_ref[rws, ln] += _dot(kw.astype(BF16), gb)
                    add_lg(row, dw * (kw * vg))

                def scan(gc, w, dw, st_ref, dst_ref, order, row):
                    def step(r, gst):
                        j = order(r)
                        through_state(pl.ds(pl.multiple_of(j * TQ, TQ), TQ), w, dw, gst, row)
                        add_lg(row, (C * gc) * (gst * st_ref[h, j]))
                        return dst_ref[h, j] + gc * gst
                    return lax.fori_loop(0, n, step, jnp.zeros((RET_DK, RET_DK), F32), unroll=True)

                gcf, gcb = jnp.exp(lgf * C), jnp.exp(lgb * C)
                g0 = scan(gcf, jnp.exp(lgf * (C - 1.0 - tau)), C - 1.0 - tau, sf_ref, dsf_ref,
                          lambda r: n - 1 - r, 0)
                through_state(crow, jnp.exp(lgf * (LC - 1.0 - jc)), LC - 1.0 - jc, g0, 0)
                g1 = scan(gcb, jnp.exp(lgb * tau), tau, sb_ref, dsb_ref, lambda r: r, 1)
                through_state(crow, jnp.exp(lgb * jc), jc, g1, 1)
                dk = dk_ref[:, ln]
                dk_ref[:, ln] = (dk * cos_ref[...] - pltpu.roll(dk, 64, 1) * sin_ref[...]) * kscale

        for h in range(4):
            one_head(h)

    sec = lambda k: pl.BlockSpec((None, TQ, 512), lambda b, i: (b, i, k))
    full = lambda k: pl.BlockSpec((None, T, 512), lambda b, i: (b, 0, k))
    dec_spec = pl.BlockSpec((1, 4), lambda b, i: (0, 0))
    tab = pl.BlockSpec((T, RET_DK), lambda b, i: (0, 0))
    return pl.pallas_call(
        body, name="ret_bwd", grid=(B, n),
        in_specs=[dec_spec, dec_spec, sec(4), full(5), full(6), sec(7),
                  pl.BlockSpec((1, 512), lambda b, i: (0, 0)), sec(0), sec(1), tab, tab, st_spec, st_spec],
        out_specs=(sec(0), sec(0), full(0), full(0),
                   pl.BlockSpec((None, 1, 512), lambda b, i: (b, 0, 0)),
                   pl.BlockSpec((None, 4, 8, 128), lambda b, i: (b, 0, 0, 0))),
        out_shape=(jax.ShapeDtypeStruct((B, L, 512), BF16), jax.ShapeDtypeStruct((B, L, 512), BF16),
                   jax.ShapeDtypeStruct((B, T, 512), F32), jax.ShapeDtypeStruct((B, T, 512), F32),
                   jax.ShapeDtypeStruct((B, 1, 512), F32), jax.ShapeDtypeStruct((B, 4, 8, 128), F32)),
        scratch_shapes=[pltpu.VMEM((4, n, RET_DK, RET_DK), F32), pltpu.VMEM((4, n, RET_DK, RET_DK), F32)],
        compiler_params=_params(("arbitrary",) * 2, vmem_mb=56))(
            dec_f, dec_b, P, P, P, P, ret_norm_g, o_ret, dY, cos2, sin2, sf, sb)


def _out_call(y_na, y_ret, x, target, mod, final_g, wout_f):
    B, L, _ = x.shape
    TO = 2 * TQ

    def body(yn_ref, yr_ref, x_ref, t_ref, mod_ref, gf_ref, w_ref, dy_ref, dx2_ref, dwb_ref, sm_ref, dw_ref):
        b, i = pl.program_id(0), pl.program_id(1)

        @pl.when((b == 0) & (i == 0))
        def _():
            dw_ref[...] = jnp.zeros_like(dw_ref)
            sm_ref[...] = jnp.zeros_like(sm_ref)

        gate = mod_ref[pl.ds(b, 1), 2 * D:3 * D]
        gf = gf_ref[...]
        yn, yr = yn_ref[...], yr_ref[...]
        ylat = _dot(yn, w_ref[0:512, :]) + _dot(yr, w_ref[512:1024, :])
        x2 = x_ref[...] + gate * ylat
        r = lax.rsqrt(jnp.mean(x2 * x2, axis=-1, keepdims=True) + EPS)
        xr = x2 * r
        err = xr * gf - t_ref[...]
        sm_ref[1:2, :] += jnp.sum(err * err, axis=0, keepdims=True)
        dout = err * (1.0 / D)
        sm_ref[0:1, :] += jnp.sum(dout * xr, axis=0, keepdims=True)
        gd = dout * gf
        dx2 = r * (gd - xr * jnp.mean(gd * xr, axis=-1, keepdims=True))
        dx2_ref[...] = dx2
        sm_ref[pl.ds(2 + b, 1), :] += jnp.sum(dx2 * ylat, axis=0, keepdims=True)
        dyl = (gate * dx2).astype(BF16)
        dy_ref[:, 0:512] = _dot_nt(dyl, w_ref[0:512, :]).astype(BF16)
        dy_ref[:, 512:1024] = _dot_nt(dyl, w_ref[512:1024, :]).astype(BF16)
        dw_ref[0:512, :] += _dot_tn(yn, dyl)
        dw_ref[512:1024, :] += _dot_tn(yr, dyl)

        @pl.when((b == B - 1) & (i == L // TO - 1))
        def _():
            dwb_ref[...] = dw_ref[...].astype(BF16)

    half = pl.BlockSpec((None, TO, 512), lambda b, i: (b, i, 0))
    full = pl.BlockSpec((None, TO, D), lambda b, i: (b, i, 0))
    return pl.pallas_call(
        body, name="out_proj_loss", grid=(B, L // TO),
        in_specs=[half, half, full, full,
                  pl.BlockSpec((8, 3 * D), lambda b, i: (0, 0)),
                  pl.BlockSpec((1, D), lambda b, i: (0, 0)),
                  pl.BlockSpec((D, D), lambda b, i: (0, 0))],
        out_specs=(full, full, pl.BlockSpec((D, D), lambda b, i: (0, 0)),
                   pl.BlockSpec((8, D), lambda b, i: (0, 0))),
        out_shape=(jax.ShapeDtypeStruct((B, L, D), BF16), jax.ShapeDtypeStruct((B, L, D), F32),
                   jax.ShapeDtypeStruct((D, D), BF16), jax.ShapeDtypeStruct((8, D), F32)),
        scratch_shapes=[pltpu.VMEM((D, D), F32)],
        compiler_params=_params(("arbitrary",) * 2))(y_na, y_ret, x, target, mod, final_g, wout_f)


def _dh_call(dsec, win_f, x, ctx, dx2, mod, norm_g, cp_in, cp_out):
    B, L, _ = x.shape
    LC = ctx.shape[1]
    nl = L // TQ

    def body(d0, d1, d2, d3, d4, d5, d6, d7, w_ref, x_ref, ctx_ref, dx2_ref, mod_ref, g_ref, cpi_ref, cpo_ref,
             gx_ref, sm_ref, sli_ref, slo_ref, ssem, rsem, lsem):
        drefs = (d0, d1, d2, d3, d4, d5, d6, d7)
        b, t = pl.program_id(0), pl.program_id(1)
        is_lat = t < nl

        @pl.when((b == 0) & (t == 0))
        def _():
            sm_ref[...] = jnp.zeros_like(sm_ref)

        def dh_of(secs):
            acc = jnp.zeros((TQ, D), F32)
            for sec in secs:
                s, half = divmod(sec, 2)
                acc = acc + _dot_nt(drefs[sec][...].astype(BF16), w_ref[s, :, half * 512:(half + 1) * 512])
            return acc

        def norm_bwd(dh, xt, mrow):
            scale = mrow[:, D:2 * D]
            g = g_ref[...]
            rstd = lax.rsqrt(jnp.mean(xt * xt, axis=-1, keepdims=True) + EPS)
            xn = xt * rstd
            dshift = jnp.sum(dh, axis=0, keepdims=True)
            dscale = jnp.sum(dh * (xn * g), axis=0, keepdims=True)
            dhn = dh * (1.0 + scale)
            sm_ref[0:1, :] += jnp.sum(dhn * xn, axis=0, keepdims=True)
            dxn = dhn * g
            dx = rstd * (dxn - xn * jnp.mean(dxn * xn, axis=-1, keepdims=True))
            return dshift, dscale, dx

        @pl.when(is_lat)
        def _():
            dshift, dscale, dx = norm_bwd(dh_of(range(8)), x_ref[...], mod_ref[pl.ds(b, 1), :])
            sm_ref[pl.ds(3 + b, 1), :] += dshift
            sm_ref[pl.ds(3 + B + b, 1), :] += dscale
            gx_ref[...] = dx2_ref[...] + dx

        @pl.when(jnp.logical_not(is_lat))
        def _():
            dshift, dscale, _ = norm_bwd(dh_of((1, 2, 5, 6)), ctx_ref[...], mod_ref[B:B + 1, :])
            sm_ref[1:2, :] += dshift
            sm_ref[2:3, :] += dscale

        mx, my, mc = _mesh_pos()
        s = 2 * mx + my
        cps, sls = (cpi_ref, cpo_ref), (sli_ref, slo_ref)
        own = [pltpu.make_async_copy(cps[a].at[s], sls[a].at[s], lsem.at[a]) for a in range(2)]
        sends, recvs, k = [], [], 0
        for px, py in _other_chips(mx, my):
            ps = 2 * px + py
            for a in range(2):
                sends.append(_remote(cps[a].at[ps], sls[a].at[s], ssem, rsem, k, (px, py, mc)))
                recvs.append(_remote(cps[a].at[s], sls[a].at[ps], ssem, rsem, k, (px, py, mc)))
                k += 1

        @pl.when((b == 0) & (t == 0))
        def _():
            for cp in own + sends:
                cp.start()

        @pl.when((b == B - 1) & (t == nl))
        def _():
            _finish(own, sends, recvs)

    lat = lambda b, t: (b, jnp.minimum(t, nl - 1), 0)
    tok = lambda b, t: (b, t, 0)
    sec_specs = [pl.BlockSpec((None, TQ, 512), lat if sec in (0, 3, 4, 7) else tok) for sec in range(8)]
    return pl.pallas_call(
        body, name="dh_norm_bwd", grid=(B, nl + 1),
        in_specs=sec_specs + [
            pl.BlockSpec((N_SHARD, D, D), lambda b, t: (0, 0, 0)),
            pl.BlockSpec((None, TQ, D), lat),
            pl.BlockSpec((None, LC, D), lambda b, t: (b, 0, 0)),
            pl.BlockSpec((None, TQ, D), lat),
            pl.BlockSpec((8, 3 * D), lambda b, t: (0, 0)),
            pl.BlockSpec((1, D), lambda b, t: (0, 0)), ANY, ANY],
        out_specs=(pl.BlockSpec((None, TQ, D), lat), pl.BlockSpec((8, D), lambda b, t: (0, 0)), ANY, ANY),
        out_shape=(jax.ShapeDtypeStruct((B, L, D), F32), jax.ShapeDtypeStruct((8, D), F32),
                   jax.ShapeDtypeStruct(cp_in.shape, cp_in.dtype), jax.ShapeDtypeStruct(cp_out.shape, cp_out.dtype)),
        scratch_shapes=[pltpu.SemaphoreType.DMA((6,)), pltpu.SemaphoreType.DMA((6,)),
                        pltpu.SemaphoreType.DMA((2,))],
        compiler_params=_params(("arbitrary",) * 2))(*dsec, win_f, x, ctx, dx2, mod, norm_g, cp_in, cp_out)


def _dw_call(dsec, h, L):
    B, T, _ = h.shape
    TW = 2 * TQ
    nl = L // TW
    KV = (1, 2, 5, 6)

    def body(d0, d1, d2, d3, d4, d5, d6, d7, c1, c2, c5, c6, h_ref, hc_ref, dw_ref, acc_ref):
        drefs = (d0, d1, d2, d3, d4, d5, d6, d7)
        crefs = dict(zip(KV, (c1, c2, c5, c6)))
        b, t = pl.program_id(0), pl.program_id(1)

        @pl.when((b == 0) & (t == 0))
        def _():
            acc_ref[...] = jnp.zeros_like(acc_ref)

        def add(hb, refs, secs):
            for sec in secs:
                s, half = divmod(sec, 2)
                acc_ref[s, :, half * 512:(half + 1) * 512] += _dot_tn(hb, refs[sec][...].astype(BF16))

        @pl.when(t < nl)
        def _():
            add(h_ref[...], drefs, range(8))

        @pl.when(t == nl)
        def _():
            add(hc_ref[...], crefs, KV)

        @pl.when((b == B - 1) & (t == nl))
        def _():
            dw_ref[...] = acc_ref[...].astype(BF16)

    lat = lambda b, t: (b, jnp.minimum(t, nl - 1), 0)
    ctx = lambda b, t: (b, L // TQ, 0)
    return pl.pallas_call(
        body, name="dw_in", grid=(B, nl + 1),
        in_specs=[pl.BlockSpec((None, TW, 512), lat)] * 8 + [pl.BlockSpec((None, TQ, 512), ctx)] * 4
        + [pl.BlockSpec((None, TW, D), lat), pl.BlockSpec((None, TQ, D), ctx)],
        out_specs=pl.BlockSpec((N_SHARD, D, D), lambda b, t: (0, 0, 0)),
        out_shape=jax.ShapeDtypeStruct((N_SHARD, D, D), BF16),
        scratch_shapes=[pltpu.VMEM((N_SHARD, D, D), F32)],
        compiler_params=_params(("arbitrary",) * 2, vmem_mb=60))(*dsec, *[dsec[k] for k in KV], h, h)


def _mesh_pos():
    return lax.axis_index("x"), lax.axis_index("y"), lax.axis_index("c")


def _flip(v, f):
    return 1 - v if f else v


def _remote(src, dst, ssem, rsem, k, peer):
    return pltpu.make_async_remote_copy(src_ref=src, dst_ref=dst, send_sem=ssem.at[k], recv_sem=rsem.at[k],
                                        device_id=peer, device_id_type=MESH)


def _other_chips(x, y):
    return [(_flip(x, fx), _flip(y, fy)) for fx, fy in ((1, 0), (0, 1), (1, 1))]


def _gather_copies(own_ref, all_ref, out_ref, hr, sems, k0, l0):
    ssem, rsem, lsem = sems
    mx, my, mc = _mesh_pos()
    s = 2 * mx + my
    sib = (mx, my, 1 - mc)
    own = pltpu.make_async_copy(own_ref, all_ref.at[s], lsem.at[l0])
    send, recv, fsend, frecv = [], [], [], []
    outs = [pltpu.make_async_copy(all_ref.at[s], out_ref.at[s], lsem.at[l0 + 1])]
    for k, (px, py) in enumerate(_other_chips(mx, my)):
        ps = 2 * px + py
        mine = all_ref.at[s, pl.ds(mc * hr, hr)]
        send.append(_remote(mine, mine, ssem, rsem, k0 + k, (px, py, mc)))
        got = all_ref.at[ps, pl.ds(mc * hr, hr)]
        recv.append(_remote(mine, got, ssem, rsem, k0 + k, (px, py, mc)))
        fsend.append(_remote(got, got, ssem, rsem, k0 + 3 + k, sib))
        theirs = all_ref.at[ps, pl.ds((1 - mc) * hr, hr)]
        frecv.append(_remote(theirs, theirs, ssem, rsem, k0 + 3 + k, sib))
        outs.append(pltpu.make_async_copy(all_ref.at[ps], out_ref.at[ps], lsem.at[l0 + 2 + k]))
    return own, send, recv, fsend, frecv, outs


def _all_to_all_small(src, dst_all, ssem, rsem, k0, x, y, cc):
    me = 4 * x + 2 * y + cc
    sends, recvs = [], []
    for f in range(1, N_DEV):
        px, py, pc = _flip(x, f & 4), _flip(y, f & 2), _flip(cc, f & 1)
        sends.append(_remote(src, dst_all.at[me], ssem, rsem, k0 + f - 1, (px, py, pc)))
        recvs.append(_remote(src, dst_all.at[4 * px + 2 * py + pc], ssem, rsem, k0 + f - 1, (px, py, pc)))
    return sends, recvs


def _finish(local, sends, recvs):
    for cp in recvs:
        cp.wait_recv()
    for cp in sends:
        cp.wait_send()
    for cp in local:
        cp.wait()


def _c_gather_call(c, rpb_flat):
    def body(c_ref, r_ref, c_all, bias_out, bias_ref, et_ref, ssem, rsem, lsem):
        x, y, cc = _mesh_pos()
        me = 4 * x + 2 * y + cc
        local = [pltpu.make_async_copy(c_ref, c_all.at[me], lsem.at[0])]
        c_send, c_recv = _all_to_all_small(c_ref, c_all, ssem, rsem, 0, x, y, cc)
        for cp in local + c_send:
            cp.start()
        bias_out_copies = _bias_body(r_ref, bias_ref, et_ref, bias_out, lsem.at[1])
        _finish(local + bias_out_copies, c_send, c_recv)

    bias_shape = (rpb_flat.shape[0], 3, TQ, KW)
    return pl.pallas_call(
        body, name="c_gather",
        in_specs=[pl.BlockSpec(memory_space=pltpu.VMEM), pl.BlockSpec(memory_space=pltpu.SMEM)],
        out_specs=(pl.BlockSpec(memory_space=pltpu.VMEM), ANY),
        out_shape=(jax.ShapeDtypeStruct((N_DEV,) + c.shape, c.dtype), jax.ShapeDtypeStruct(bias_shape, F32)),
        scratch_shapes=[pltpu.VMEM(bias_shape, F32), pltpu.VMEM((15, GRID_W, GRID_W), F32),
                        pltpu.SemaphoreType.DMA((N_DEV - 1,)), pltpu.SemaphoreType.DMA((N_DEV - 1,)),
                        pltpu.SemaphoreType.DMA((2,))],
        compiler_params=pltpu.CompilerParams(vmem_limit_bytes=56 << 20))(c, rpb_flat)


VROWS = 32


def _grad_halves_call(dwin_b, dwout_b, dbias, dlg):
    arrs = (dwin_b, dwout_b)
    hrs = [a.shape[1] // 2 for a in arrs]

    def body(din, dout, db_ref, dlg_ref, cp_in, cp_out, drpb_ref, dlgo_ref, got_in, got_out, p_ref, ssem, rsem):
        x, y, cc = _mesh_pos()
        sib = (x, y, 1 - cc)
        srcs, gots, cps = (din, dout), (got_in, got_out), (cp_in, cp_out)
        halves = [_remote(srcs[a].at[:, pl.ds((1 - cc) * hrs[a], hrs[a])], gots[a], ssem, rsem, a, sib)
                  for a in range(2)]
        for cp in halves:
            cp.start()
        _small_reduce_body(db_ref, dlg_ref, drpb_ref, dlgo_ref, p_ref)
        for cp in halves:
            cp.wait_recv()
        for a in range(2):
            for j in range(N_SHARD):
                def add(i, carry, a=a, j=j):
                    r = pl.multiple_of(i * VROWS, VROWS)
                    mine = srcs[a][j, pl.ds(pl.multiple_of(cc * hrs[a] + r, VROWS), VROWS), :].astype(F32)
                    cps[a][j, pl.ds(r, VROWS), :] = (
                        mine + gots[a][j, pl.ds(r, VROWS), :].astype(F32)).astype(BF16)
                    return carry
                lax.fori_loop(0, hrs[a] // VROWS, add, 0)
        for cp in halves:
            cp.wait_send()

    vmem = pl.BlockSpec(memory_space=pltpu.VMEM)
    half_shapes = [(N_SHARD, hrs[a], arrs[a].shape[2]) for a in range(2)]
    return pl.pallas_call(
        body, name="grad_halves",
        in_specs=[vmem] * 4, out_specs=(vmem,) * 4,
        out_shape=(jax.ShapeDtypeStruct(half_shapes[0], BF16), jax.ShapeDtypeStruct(half_shapes[1], BF16),
                   jax.ShapeDtypeStruct((dbias.shape[0], 16, 32), F32), jax.ShapeDtypeStruct((32, 128), F32)),
        scratch_shapes=[pltpu.VMEM(half_shapes[0], BF16), pltpu.VMEM(half_shapes[1], BF16),
                        pltpu.VMEM((32, GRID_W), F32),
                        pltpu.SemaphoreType.DMA((2,)), pltpu.SemaphoreType.DMA((2,))],
        compiler_params=pltpu.CompilerParams(vmem_limit_bytes=56 << 20))(dwin_b, dwout_b, dbias, dlg)


def _grad_finish_call(sl_in, sl_out, small):
    arrs = (sl_in, sl_out)

    def body(sin, sout, sm, gin, gout, sm_all, h_in, h_out, ssem, rsem, lsem):
        x, y, cc = _mesh_pos()
        me = 4 * x + 2 * y + cc
        sib = (x, y, 1 - cc)
        sls, hs, gs = (sin, sout), (h_in, h_out), (gin, gout)
        sm_send, sm_recv = _all_to_all_small(sm, sm_all, ssem, rsem, 2, x, y, cc)
        sm_own = pltpu.make_async_copy(sm, sm_all.at[me], lsem.at[0])
        for cp in sm_send + [sm_own]:
            cp.start()
        for a in range(2):
            def total(i, carry, a=a):
                rows = pl.ds(pl.multiple_of(i * VROWS, VROWS), VROWS)
                sl = sls[a]
                hs[a][rows, :] = ((sl[0, rows, :].astype(F32) + sl[1, rows, :].astype(F32))
                                  + sl[2, rows, :].astype(F32)) + sl[3, rows, :].astype(F32)
                return carry
            lax.fori_loop(0, arrs[a].shape[1] // VROWS, total, 0)
        mine = [pltpu.make_async_copy(hs[a], gs[a].at[cc], lsem.at[1 + a]) for a in range(2)]
        back = [_remote(hs[a], gs[a].at[cc], ssem, rsem, a, sib) for a in range(2)]
        back_recv = [_remote(hs[a], gs[a].at[1 - cc], ssem, rsem, a, sib) for a in range(2)]
        for cp in mine + back:
            cp.start()
        _finish(mine + [sm_own], back + sm_send, back_recv + sm_recv)

    vmem = pl.BlockSpec(memory_space=pltpu.VMEM)
    return pl.pallas_call(
        body, name="grad_finish",
        in_specs=[vmem] * 3, out_specs=(vmem,) * 3,
        out_shape=(jax.ShapeDtypeStruct((2,) + sl_in.shape[1:], F32),
                   jax.ShapeDtypeStruct((2,) + sl_out.shape[1:], F32),
                   jax.ShapeDtypeStruct((N_DEV,) + small.shape, F32)),
        scratch_shapes=[pltpu.VMEM(sl_in.shape[1:], F32), pltpu.VMEM(sl_out.shape[1:], F32),
                        pltpu.SemaphoreType.DMA((9,)), pltpu.SemaphoreType.DMA((9,)),
                        pltpu.SemaphoreType.DMA((3,))],
        compiler_params=pltpu.CompilerParams(vmem_limit_bytes=48 << 20))(sl_in, sl_out, small)


def _adamw(w, g, m, v):
    m = ADAM_B1 * m + (1.0 - ADAM_B1) * g
    v = ADAM_B2 * v + (1.0 - ADAM_B2) * (g * g)
    m_hat = m / (1.0 - ADAM_B1 ** ADAM_STEP)
    v_hat = v / (1.0 - ADAM_B2 ** ADAM_STEP)
    return -ADAM_LR * (m_hat / (jnp.sqrt(v_hat) + ADAM_EPS) + ADAM_WD * w), m, v


def _adam_call(w, m, v, g, name):
    R, C = w.shape
    tr = 256

    def body(w_ref, m_ref, v_ref, g_ref, go_ref, d_ref, mo_ref, vo_ref):
        g = g_ref[...]
        go_ref[...] = g
        d_ref[...], mo_ref[...], vo_ref[...] = _adamw(w_ref[...], g, m_ref[...], v_ref[...])

    spec = pl.BlockSpec((tr, C), lambda i: (i, 0))
    return pl.pallas_call(
        body, name=name, grid=(R // tr,), in_specs=[spec] * 4,
        out_specs=(spec,) * 4, out_shape=(jax.ShapeDtypeStruct((R, C), F32),) * 4,
        compiler_params=_params(("arbitrary",)))(w, m, v, g)


R_GF, R_NG, R_LOSS, R_RNG, R_LGF, R_LGB, R_SHIFT, R_SCALE, R_GATE, R_SHIFT_C, R_SCALE_C, R_RNG2, R_RPB = (
    0, 1, 2, 3, 4, 5, 6, 8, 10, 12, 13, 14, 16)
W_GF, W_NG, W_CCTX, W_RNG, W_DF, W_DB, W_BADA, W_RPB = 0, 1, 2, 3, 4, 5, 6, 9


SMALL = (("final_norm_g", W_GF, 1, D), ("norm_g", W_NG, 1, D), ("c_ctx", W_CCTX, 1, D),
         ("ret_norm_g", W_RNG, 1, 512), ("ret_decay_fwd", W_DF, 1, 4), ("ret_decay_bwd", W_DB, 1, 4),
         ("b_ada", W_BADA, 3, D), ("na_rpb", W_RPB, 4, D))
N_SMALL = len(SMALL)


def _small_final_call(sm_all, c_t, wada_f, wada, m_ada, v_ada, small_w, small_m, small_v, B):
    ws = wada.shape[1]
    NB = N_DEV * B

    def body(*refs):
        sm_ref, ct_ref, wf_ref, wa_ref, ma_ref, va_ref = refs[:6]
        ins = refs[6:6 + 3 * N_SMALL]
        outs = refs[6 + 3 * N_SMALL:6 + 7 * N_SMALL]
        ga_ref, da_ref, mao_ref, vao_ref, loss_ref, dmod_ref, pk_ref = refs[6 + 7 * N_SMALL:]
        x, y, _ = _mesh_pos()
        s = 2 * x + y
        tot = sm_ref[0]
        for dv in range(1, N_DEV):
            tot = tot + sm_ref[dv]
        pk_ref[...] = jnp.zeros_like(pk_ref)
        for kind in range(3):
            for i, (_, row, nrow, width) in enumerate(SMALL):
                ref = ins[kind * N_SMALL + i]
                if nrow == 3:
                    for part in range(3):
                        pk_ref[kind, row + part:row + part + 1, :] = ref[:, part * D:(part + 1) * D]
                else:
                    pk_ref[kind, row:row + nrow, 0:width] = ref[...]
        w = pk_ref[0]
        cctx_ref = ins[2]
        for dv in range(N_DEV):
            for b in range(B):
                r = dv * B + b
                for part, row in enumerate((R_SHIFT, R_SCALE, R_GATE)):
                    dmod_ref[r:r + 1, part * D:(part + 1) * D] = sm_ref[dv, row + b:row + b + 1, :]
        dmod_ref[NB:NB + 1, 0:D] = tot[R_SHIFT_C:R_SHIFT_C + 1, :]
        dmod_ref[NB:NB + 1, D:2 * D] = tot[R_SCALE_C:R_SCALE_C + 1, :]
        dmod_ref[NB:NB + 1, 2 * D:3 * D] = jnp.zeros((1, D), F32)
        dmod_ref[NB + 1:, :] = jnp.zeros((dmod_ref.shape[0] - NB - 1, 3 * D), F32)
        dmod = dmod_ref[...]
        cc = cctx_ref[...]
        scc = _sigmoid(cc)
        ct = ct_ref[...]
        act_t = ct * _sigmoid(ct)
        dmc = dmod[NB:NB + 1, :].astype(BF16)
        dact = jnp.zeros((1, D), F32)
        for sh in range(N_SHARD):
            dact = dact + _dot_nt(dmc[:, sh * ws:(sh + 1) * ws], wf_ref[sh])
        g = jnp.zeros((16, D), F32)
        rows = lax.broadcasted_iota(jnp.int32, (16, D), 0)

        def put(g, row, val):
            return jnp.where(rows == row, val, g)

        g = put(g, W_GF, tot[R_GF:R_GF + 1, :])
        g = put(g, W_NG, tot[R_NG:R_NG + 1, :])
        g = put(g, W_CCTX, dact * (scc * (1.0 + cc * (1.0 - scc))))
        g = put(g, W_RNG, tot[R_RNG:R_RNG + 1, :] + tot[R_RNG2:R_RNG2 + 1, :])
        g = put(g, W_DF, tot[R_LGF:R_LGF + 1, :] * (-jnp.exp(w[W_DF:W_DF + 1, :])))
        g = put(g, W_DB, tot[R_LGB:R_LGB + 1, :] * (-jnp.exp(w[W_DB:W_DB + 1, :])))
        db = jnp.sum(dmod, axis=0, keepdims=True)
        for part in range(3):
            g = put(g, W_BADA + part, db[:, part * D:(part + 1) * D])
        for part in range(4):
            g = put(g, W_RPB + part, tot[R_RPB + part:R_RPB + part + 1, :])
        for kind, val in enumerate((g,) + _adamw(w, g, pk_ref[1], pk_ref[2])):
            for i, (_, row, nrow, width) in enumerate(SMALL):
                out = outs[kind * N_SMALL + i]
                if nrow == 3:
                    for part in range(3):
                        out[:, part * D:(part + 1) * D] = val[row + part:row + part + 1, :]
                else:
                    out[...] = val[row:row + nrow, 0:width]
        loss_ref[...] = jnp.broadcast_to(
            (0.5 / D) * jnp.sum(tot[R_LOSS:R_LOSS + 1, :], axis=1, keepdims=True), (8, 128))
        for sh in range(N_SHARD):
            @pl.when(s == sh)
            def _():
                ga = jnp.dot(act_t, dmod[:, sh * ws:(sh + 1) * ws], precision=HIGHEST,
                             preferred_element_type=F32)
                ga_ref[...] = ga
                da_ref[...], mao_ref[...], vao_ref[...] = _adamw(wa_ref[...], ga, ma_ref[...], va_ref[...])

    sh_small = tuple(jax.ShapeDtypeStruct(a.shape, F32) for a in small_w)
    sh_ada = jax.ShapeDtypeStruct(wada.shape, F32)
    res = pl.pallas_call(
        body, name="small_final",
        out_shape=sh_small * 4 + (sh_ada,) * 4 + (jax.ShapeDtypeStruct((8, 128), F32),),
        scratch_shapes=[pltpu.VMEM((NB + 8, 3 * D), F32), pltpu.VMEM((3, 16, D), F32)],
        compiler_params=_params(vmem_mb=56))(
            sm_all, c_t, wada_f, wada, m_ada, v_ada, *small_w, *small_m, *small_v)
    smalls = [res[k * N_SMALL:(k + 1) * N_SMALL] for k in range(4)]
    return smalls, res[4 * N_SMALL:4 * N_SMALL + 4], res[4 * N_SMALL + 4]


def _local_step(order, x, ctx, c_rows, norm_g, wada_b, b_shard, win_b, bias, dec_f, dec_b, ret_norm_g,
                wout_b, final_g, target):
    B, L, _ = x.shape
    LC = ctx.shape[1]
    assert B == 2
    cos2, sin2 = _rope_tables(L, LC)
    mod_part = _mod_part_call(c_rows, wada_b, b_shard)
    P, h, win_f, wout_f, mod = _inproj_gather_call(order, x, ctx, mod_part, norm_g, win_b, wout_b, cos2, sin2)
    y_na, o_na, wada_f = _na_fwd_call(P, bias, wada_b, L, LC)
    sf, sb = _ret_states_call(P, dec_f, dec_b, L, LC)
    y_ret, o_ret = _retc_fwd_call(P, sf, sb, dec_f, dec_b, ret_norm_g, L)
    dY, dx2, dwout_p, sm_out = _out_call(y_na, y_ret, x, target, mod, final_g, wout_f.reshape(D, D))
    dnq, dng, dnk, dnv, dbias = _na_bwd_call(P, bias, dY, o_na, L, LC)
    drq, drg, drk, drv, dgn, dlg = _retc_bwd_call(P, sf, sb, dec_f, dec_b, ret_norm_g, o_ret, dY, cos2, sin2, L, LC)
    dsec = (dnq, dnk, dnv, dng, drq, drk, drv, drg)
    dwin_b = _dw_call(dsec, h, L)
    cp_in, cp_out, drpb, dlg_sum = _grad_halves_call(
        dwin_b, dwout_p.reshape(N_SHARD, D // N_SHARD, D), dbias, dlg)
    grad_x, sm_dh, sl_in, sl_out = _dh_call(dsec, win_f, x, ctx, dx2, mod, norm_g, cp_in, cp_out)
    z = jnp.zeros((1, D), F32)
    pad = lambda v: jnp.pad(v.reshape(1, -1), ((0, 0), (0, D - v.size)))
    dlg_sum = dlg_sum.reshape(4, 8, 128)
    rpb_rows = jnp.pad(drpb[:, :15, :31].reshape(-1), (0, 4 * D - drpb.shape[0] * 465)).reshape(4, D)
    small = jnp.concatenate([
        sm_out[0:1], sm_dh[0:1], sm_out[1:2], pad(dgn[0]), pad(dlg_sum[:, 0, 0]), pad(dlg_sum[:, 1, 0]),
        sm_dh[3:5], sm_dh[5:7], sm_out[2:4], sm_dh[1:2], sm_dh[2:3], pad(dgn[1]), z, rpb_rows,
        jnp.zeros((SM_ROWS - 20, D), F32)], axis=0)
    return grad_x, sl_in, sl_out, small, wada_f


def kernel(x, c, ctx, c_ctx, norm_g, w_ada, b_ada, w_in, na_rpb, ret_decay_fwd, ret_decay_bwd, ret_norm_g, w_out, final_norm_g, loss_target, m_c_ctx, m_norm_g, m_w_ada, m_b_ada, m_w_in, m_na_rpb, m_ret_decay_fwd, m_ret_decay_bwd, m_ret_norm_g, m_w_out, m_final_norm_g, v_c_ctx, v_norm_g, v_w_ada, v_b_ada, v_w_in, v_na_rpb, v_ret_decay_fwd, v_ret_decay_bwd, v_ret_norm_g, v_w_out, v_final_norm_g):
    B = x.shape[0]
    c_all, bias = _c_gather_call(c, na_rpb[0].reshape(na_rpb.shape[1], -1))
    c_rows = jnp.concatenate([c_all.reshape(N_DEV * B, D), c_ctx.reshape(1, D), jnp.zeros((7, D), F32)], axis=0)
    mx, my = lax.axis_index("x"), lax.axis_index("y")
    order = jnp.stack([2 * mx + my, 2 * (1 - mx) + my, 2 * mx + (1 - my),
                       2 * (1 - mx) + (1 - my)]).astype(jnp.int32)
    ws = w_ada.shape[2]
    b_shard = lax.dynamic_slice(b_ada, (0, (2 * mx + my) * ws), (1, ws))
    grad_x, sl_in, sl_out, small, wada_f = _local_step(
        order, x, ctx, c_rows, norm_g, w_ada[0].astype(BF16), b_shard, w_in[0].astype(BF16), bias, ret_decay_fwd,
        ret_decay_bwd, ret_norm_g, w_out[0].astype(BF16), final_norm_g.reshape(1, D), loss_target)
    gin, gout, sm_all = _grad_finish_call(sl_in, sl_out, small)
    g_win, d_win, nm_win, nv_win = _adam_call(
        w_in[0], m_w_in[0], v_w_in[0], gin.reshape(w_in.shape[1:]), "adam_w_in")
    g_wout, d_wout, nm_wout, nv_wout = _adam_call(
        w_out[0], m_w_out[0], v_w_out[0], gout.reshape(w_out.shape[1:]), "adam_w_out")

    def small_inputs(gf, ng, cc, rng, df, db, bada, rpb):
        return (gf.reshape(1, D), ng, cc.reshape(1, D), rng, df, db, bada,
                jnp.pad(rpb.reshape(-1), (0, 4 * D - rpb.size)).reshape(4, D))

    c_t = c_rows.T
    smalls, adas, loss = _small_final_call(
        sm_all, c_t, wada_f, w_ada[0], m_w_ada[0], v_w_ada[0],
        small_inputs(final_norm_g, norm_g, c_ctx, ret_norm_g, ret_decay_fwd, ret_decay_bwd, b_ada, na_rpb),
        small_inputs(m_final_norm_g, m_norm_g, m_c_ctx, m_ret_norm_g, m_ret_decay_fwd, m_ret_decay_bwd, m_b_ada,
                     m_na_rpb),
        small_inputs(v_final_norm_g, v_norm_g, v_c_ctx, v_ret_norm_g, v_ret_decay_fwd, v_ret_decay_bwd, v_b_ada,
                     v_na_rpb), B)
    res = []
    for p, ada, win_o, wout_o in zip(smalls, adas, (g_win, d_win, nm_win, nv_win),
                                     (g_wout, d_wout, nm_wout, nv_wout)):
        gf, ng, cc, rng, df, db, bada, rpb = p
        res.append([cc.reshape(D), ng, ada[None], bada, win_o[None],
                    rpb.reshape(-1)[:na_rpb.size].reshape(na_rpb.shape), df, db, rng, wout_o[None], gf.reshape(D)])
    return (loss[0, 0], grad_x, *res[0], *res[1], *res[2], *res[3])
```

```python
import numpy as np
import jax
import jax.numpy as jnp
from jax import lax
from jax.experimental import pallas as pl
from jax.experimental.pallas import tpu as pltpu

F32 = jnp.float32
BF16 = jnp.bfloat16
HIGHEST = lax.Precision.HIGHEST

D = 1024
GRID_W = 64
NA_DH = 64
RET_DK = 128
ROPE_BASE = 10000.0
EPS = 1e-6
NEG = -1e30
TQ = 256
KW = 12 * GRID_W
N_SHARD = 4
N_DEV = 8
SM_ROWS = 24

ADAM_LR = 0.001
ADAM_B1 = 0.9
ADAM_B2 = 0.999
ADAM_EPS = 1e-08
ADAM_WD = 0.01
ADAM_STEP = 10

MESH = pl.DeviceIdType.MESH
ANY = pl.BlockSpec(memory_space=pl.ANY)


def _params(sem=None, vmem_mb=48):
    return pltpu.CompilerParams(dimension_semantics=sem, vmem_limit_bytes=vmem_mb << 20)


def _dot(a, b):
    return jnp.dot(a, b, preferred_element_type=F32)


def _dot_nt(a, b):
    return lax.dot_general(a, b, (((1,), (1,)), ((), ())), preferred_element_type=F32)


def _dot_tn(a, b):
    return lax.dot_general(a, b, (((0,), (0,)), ((), ())), preferred_element_type=F32)


def _sigmoid(x):
    return 1.0 / (1.0 + jnp.exp(-x))


def _rope_tables(L, LC):
    half = RET_DK // 2
    nf = half // 2
    t = np.arange(L)
    row = (t // GRID_W).astype(np.float32)
    col = (t % GRID_W).astype(np.float32)
    inv = (np.float32(ROPE_BASE) ** (-np.arange(nf, dtype=np.float32) / np.float32(nf))).astype(np.float32)
    ang = np.concatenate([row[:, None] * inv, col[:, None] * inv], axis=-1).astype(np.float32)
    cos, sin = np.cos(ang).astype(np.float32), np.sin(ang).astype(np.float32)
    cos2 = np.concatenate([cos, cos], axis=-1)
    sin2 = np.concatenate([-sin, sin], axis=-1)
    cos2 = np.concatenate([cos2, np.ones((LC, RET_DK), np.float32)], axis=0)
    sin2 = np.concatenate([sin2, np.zeros((LC, RET_DK), np.float32)], axis=0)
    return jnp.asarray(cos2), jnp.asarray(sin2)


def _mod_part_call(c_rows, wada_b, b_shard):
    def body(c_ref, w_ref, b_ref, o_ref):
        a = c_ref[...]
        o_ref[...] = _dot((a * _sigmoid(a)).astype(BF16), w_ref[...]) + b_ref[...]

    return pl.pallas_call(
        body, name="ada_mod", out_shape=jax.ShapeDtypeStruct((c_rows.shape[0], wada_b.shape[1]), F32),
        compiler_params=_params())(c_rows, wada_b, b_shard)


def _dc_masks():
    cq = lax.broadcasted_iota(jnp.int32, (GRID_W, GRID_W), 0)
    ck = lax.broadcasted_iota(jnp.int32, (GRID_W, GRID_W), 1)
    dc = jnp.clip(ck - cq + 15, 0, 30)
    c0 = jnp.clip(cq - 8, 0, GRID_W - 16)
    col_ok = (ck >= c0) & (ck < c0 + 16)
    return dc, col_ok


def _bias_blocks():
    out = []
    for typ, delta in enumerate((4, 0, -4)):
        for rq in range(4):
            for rkk in range(12):
                dr = rkk + delta - rq - 4
                if typ == 0:
                    ok = -rq <= dr <= 7 - rq
                elif typ == 1:
                    ok = -4 <= dr <= 3
                else:
                    ok = -4 - rq <= dr <= 3 - rq
                out.append((typ, rq, rkk, dr if ok else None))
    return out


def _bias_body(r_ref, bias_ref, et_ref, out_ref, sem):
    dc, col_ok = _dc_masks()
    masks = [(dc == j).astype(F32) for j in range(31)]
    nh = bias_ref.shape[0]

    def per_h(h, carry):
        for dr in range(15):
            t = jnp.zeros((GRID_W, GRID_W), F32)
            for j in range(31):
                t = t + masks[j] * r_ref[h, dr * 31 + j]
            et_ref[dr] = jnp.where(col_ok, t, NEG)
        neg = jnp.full((GRID_W, GRID_W), NEG, F32)
        for typ, rq, rkk, dr in _bias_blocks():
            blk = neg if dr is None else et_ref[dr + 7]
            bias_ref[h, typ, rq * 64:(rq + 1) * 64, rkk * 64:(rkk + 1) * 64] = blk
        pltpu.make_async_copy(bias_ref.at[h], out_ref.at[h], sem).start()
        return carry

    lax.fori_loop(0, nh, per_h, 0)
    return [pltpu.make_async_copy(bias_ref.at[h], out_ref.at[h], sem) for h in range(nh)]


def _bias_tile_sums(db_ref, hh):
    acc = {}
    for typ, rq, rkk, dr in _bias_blocks():
        if dr is None:
            continue
        blk = db_ref[hh, typ, rq * 64:(rq + 1) * 64, rkk * 64:(rkk + 1) * 64]
        acc[dr] = blk if dr not in acc else acc[dr] + blk
    return acc


def _small_reduce_body(dt_ref, dlg_ref, drpb_ref, dlgo_ref, p_ref):
    dc, _ = _dc_masks()
    masks = [(dc == j).astype(F32) for j in range(31)]
    ones = jnp.ones((8, GRID_W), F32)
    p_ref[...] = jnp.zeros_like(p_ref)
    drpb_ref[...] = jnp.zeros_like(drpb_ref)

    def per_h(h, carry):
        for dr in range(-7, 8):
            t = dt_ref[h, dr + 7]
            for j in range(31):
                p_ref[j:j + 1, :] = jnp.sum(t * masks[j], axis=0, keepdims=True)
            red = lax.dot_general(ones, p_ref[...], (((1,), (1,)), ((), ())),
                                  precision=HIGHEST, preferred_element_type=F32)
            drpb_ref[h, dr + 7:dr + 8, :] = red[0:1, :]
        return carry

    lax.fori_loop(0, dt_ref.shape[0], per_h, 0)
    x = dlg_ref[0]
    for b in range(1, dlg_ref.shape[0]):
        x = x + dlg_ref[b]
    x = x.reshape(4 * 8, x.shape[-1])
    dlgo_ref[...] = jnp.dot(x, jnp.ones((x.shape[-1], 128), F32), precision=HIGHEST,
                            preferred_element_type=F32)


def _inproj_gather_call(order, x, ctx, mod_part, norm_g, win_b, wout_b, cos2, sin2):
    B, L, _ = x.shape
    LC = ctx.shape[1]
    T = L + LC
    TI = 2 * TQ
    nl = L // TI
    nt = nl + 1
    assert LC == TQ and L % TI == 0
    kscale = RET_DK ** -0.5
    HR = D // 2
    pad_rows = nt * TI - T
    cos2 = jnp.pad(cos2, ((0, pad_rows), (0, 0)))
    sin2 = jnp.pad(sin2, ((0, pad_rows), (0, 0)))

    MW = mod_part.shape[1]
    NB = N_DEV * B

    def body(ord_ref, x_ref, ctx_ref, mp_ref, g_ref, wown_ref, woown_ref, cos_ref, sin_ref,
             p_ref, h_ref, wf_ref, wof_ref, modo_ref, w_all, wo_all, hs_ref, mp_all, mod_ref, ssem, rsem, lsem):
        j, b, t = pl.program_id(0), pl.program_id(1), pl.program_id(2)
        first = (b == 0) & (t == 0)
        mx, my, mc = _mesh_pos()
        s = 2 * mx + my

        m_send = [_remote(mp_ref, mp_all.at[s], ssem, rsem, 12 + k, (px, py, mc))
                  for k, (px, py) in enumerate(_other_chips(mx, my))]
        m_recv = [_remote(mp_ref, mp_all.at[2 * px + py], ssem, rsem, 12 + k, (px, py, mc))
                  for k, (px, py) in enumerate(_other_chips(mx, my))]

        sems = (ssem, rsem, lsem)
        own, ici_send, ici_recv, fwd_send, fwd_recv, outs = _gather_copies(wown_ref, w_all, wf_ref, HR, sems, 0, 0)
        oown, o_send, o_recv, o_fsend, o_frecv, o_outs = _gather_copies(
            woown_ref, wo_all, wof_ref, woown_ref.shape[0] // 2, sems, 6, 5)

        @pl.when(first & (j == 0))
        def _():
            for cp in m_send:
                cp.start()
            own.start()
            oown.start()
            mp_all[s] = mp_ref[...]
            own.wait()
            ici_send[0].start()
            ici_send[1].start()
            outs[0].start()
            oown.wait()
            for cp in m_recv:
                cp.wait_recv()
            me = 4 * mx + 2 * my + mc
            mod_ref[...] = jnp.zeros_like(mod_ref)
            for p in range(N_SHARD):
                for r in range(B):
                    mod_ref[r:r + 1, p * MW:(p + 1) * MW] = mp_all[p, pl.ds(B * me + r, 1), :]
                mod_ref[B:B + 1, p * MW:(p + 1) * MW] = mp_all[p, NB:NB + 1, :]
            modo_ref[...] = mod_ref[...]

        for k in range(3):
            @pl.when(first & (j == k + 1))
            def _(k=k):
                ici_recv[k].wait_recv()
                if k == 0:
                    ici_send[2].start()
                fwd_send[k].start()
                fwd_recv[k].wait_recv()
                outs[1 + k].start()
                if k == 1:
                    for cp in o_send:
                        cp.start()
                if k == 2:
                    for got, fwd in zip(o_recv, o_fsend):
                        got.wait_recv()
                        fwd.start()

        tile = b * nt + t

        @pl.when(j == 0)
        def _():
            is_lat = t < nl
            ctx_tile = jnp.concatenate([ctx_ref[...], jnp.zeros((TI - LC, D), F32)], axis=0)
            xt = jnp.where(is_lat, x_ref[...], ctx_tile)
            mrow = mod_ref[pl.ds(jnp.where(is_lat, b, B), 1), :]
            shift, scale = mrow[:, 0:D], mrow[:, D:2 * D]
            rstd = lax.rsqrt(jnp.mean(xt * xt, axis=-1, keepdims=True) + EPS)
            h0 = ((xt * rstd * g_ref[...]) * (1.0 + scale) + shift).astype(BF16)
            h_ref[...] = h0
            hs_ref[tile] = h0

        hb = hs_ref[tile]
        cs, sn = cos_ref[...], sin_ref[...]
        shard = ord_ref[j]
        for sh in range(N_SHARD):
            @pl.when(shard == sh)
            def _(sh=sh):
                for half in range(2):
                    sec = 2 * sh + half
                    acc = _dot(hb, w_all[sh, :, half * 512:(half + 1) * 512])
                    if sec == 0:
                        acc = acc * (NA_DH ** -0.5)
                    if sec in (4, 5):
                        for q in range(4):
                            a = acc[:, q * 128:(q + 1) * 128]
                            r = a * cs + pltpu.roll(a, 64, 1) * sn
                            if sec == 5:
                                r = r * kscale
                            p_ref[:, half * 512 + q * 128:half * 512 + (q + 1) * 128] = r.astype(BF16)
                    else:
                        p_ref[:, half * 512:(half + 1) * 512] = acc.astype(BF16)

        @pl.when((j == N_SHARD - 1) & (b == B - 1) & (t == nt - 1))
        def _():
            for cp in o_frecv:
                cp.wait_recv()
            for cp in o_outs:
                cp.start()
            _finish(outs + o_outs, ici_send + fwd_send + o_send + o_fsend + m_send, [])

    tok = lambda j, b, t, o: (jnp.where(j == 0, b, B - 1), jnp.where(j == 0, jnp.minimum(t, nl - 1), nl - 1), 0)
    grid_spec = pltpu.PrefetchScalarGridSpec(
        num_scalar_prefetch=1, grid=(N_SHARD, B, nt),
        in_specs=[
            pl.BlockSpec((None, TI, D), tok),
            pl.BlockSpec((None, LC, D), lambda j, b, t, o: (jnp.where(j == 0, b, B - 1), 0, 0)),
            pl.BlockSpec(mod_part.shape, lambda j, b, t, o: (0, 0)),
            pl.BlockSpec((1, D), lambda j, b, t, o: (0, 0)),
            ANY, ANY,
            pl.BlockSpec((TI, RET_DK), lambda j, b, t, o: (t, 0)),
            pl.BlockSpec((TI, RET_DK), lambda j, b, t, o: (t, 0)),
        ],
        out_specs=(pl.BlockSpec((None, TI, D), lambda j, b, t, o: (b, t, o[j])),
                   pl.BlockSpec((None, TI, D), lambda j, b, t, o: (
                       jnp.where(j == 0, b, B - 1), jnp.where(j == 0, t, nt - 1), 0)), ANY, ANY,
                   pl.BlockSpec((8, 3 * D), lambda j, b, t, o: (0, 0))),
        scratch_shapes=[pltpu.VMEM((N_SHARD, D, D), BF16), pltpu.VMEM((N_SHARD,) + wout_b.shape, BF16),
                        pltpu.VMEM((B * nt, TI, D), BF16),
                        pltpu.VMEM((N_SHARD,) + mod_part.shape, F32), pltpu.VMEM((8, 3 * D), F32),
                        pltpu.SemaphoreType.DMA((15,)), pltpu.SemaphoreType.DMA((15,)),
                        pltpu.SemaphoreType.DMA((10,))])
    return pl.pallas_call(
        body, name="in_proj", grid_spec=grid_spec,
        out_shape=(jax.ShapeDtypeStruct((B, T, 4 * D), BF16), jax.ShapeDtypeStruct((B, T, D), BF16),
                   jax.ShapeDtypeStruct((N_SHARD, D, D), BF16),
                   jax.ShapeDtypeStruct((N_SHARD,) + wout_b.shape, BF16),
                   jax.ShapeDtypeStruct((8, 3 * D), F32)),
        compiler_params=_params(("arbitrary",) * 3, vmem_mb=56))(
            order, x, ctx, mod_part, norm_g, win_b, wout_b, cos2, sin2)


def _na_specs(L, T, rows, nh=2):
    nm = rows // 4
    w = nh * NA_DH
    per = 512 // w
    q_spec = pl.BlockSpec((None, TQ, w), lambda hp, b, m: (b, m, hp))
    k_spec = pl.BlockSpec((None, T, w), lambda hp, b, m: (b, 0, per + hp))
    v_spec = pl.BlockSpec((None, T, w), lambda hp, b, m: (b, 0, 2 * per + hp))
    g_spec = pl.BlockSpec((None, TQ, w), lambda hp, b, m: (b, m, 3 * per + hp))
    bias_spec = pl.BlockSpec((nh, 3, TQ, KW), lambda hp, b, m: (hp, 0, 0, 0))
    return nm, q_spec, k_spec, v_spec, g_spec, bias_spec


def _na_tile(m, nm, rows):
    typ = jnp.where(m == 0, 0, jnp.where(m == nm - 1, 2, 1))
    start = pl.multiple_of(jnp.clip(4 * m - 4, 0, rows - 12) * GRID_W, TQ)
    return typ, start


def _na_fwd_call(P, bias, wada_b, L, LC):
    B, T, _ = P.shape
    rows = L // GRID_W
    NH = 4
    nm, q_spec, k_spec, v_spec, g_spec, bias_spec = _na_specs(L, T, rows, NH)
    ngrp = 8 // NH

    def body(q_ref, k_ref, v_ref, g_ref, bias_ref, wown_ref, y_ref, o_ref, wf_ref, w_all, ssem, rsem, lsem):
        hp, b, m = pl.program_id(0), pl.program_id(1), pl.program_id(2)
        own, send, recv, fsend, frecv, outs = _gather_copies(
            wown_ref, w_all, wf_ref, wown_ref.shape[0] // 2, (ssem, rsem, lsem), 0, 0)
        first = (b == 0) & (m == 0)

        @pl.when(first & (hp == 0))
        def _():
            own.start()
            own.wait()
            for cp in send:
                cp.start()
            outs[0].start()

        @pl.when((hp == ngrp - 1) & (b == B - 1) & (m == 0))
        def _():
            for got, fwd in zip(recv, fsend):
                got.wait_recv()
                fwd.start()

        @pl.when((hp == ngrp - 1) & (b == B - 1) & (m == nm - 1))
        def _():
            for cp in frecv:
                cp.wait_recv()
            for cp in outs[1:]:
                cp.start()
            _finish(outs, send + fsend, [])

        typ, start = _na_tile(m, nm, rows)
        for hh in range(NH):
            ln = slice(hh * NA_DH, (hh + 1) * NA_DH)
            q = q_ref[:, ln]
            kw, vw = k_ref[pl.ds(start, KW), ln], v_ref[pl.ds(start, KW), ln]
            kc, vc = k_ref[L:L + LC, ln], v_ref[L:L + LC, ln]
            s1 = _dot_nt(q, kw) + bias_ref[hh, typ]
            s2 = _dot_nt(q, kc)
            mx = jnp.maximum(jnp.max(s1, axis=-1, keepdims=True), jnp.max(s2, axis=-1, keepdims=True))
            p1, p2 = jnp.exp(s1 - mx), jnp.exp(s2 - mx)
            inv = 1.0 / (jnp.sum(p1, axis=-1, keepdims=True) + jnp.sum(p2, axis=-1, keepdims=True))
            o = (_dot(p1.astype(BF16), vw) + _dot(p2.astype(BF16), vc)) * inv
            g = g_ref[:, ln].astype(F32)
            o_ref[:, ln] = o.astype(BF16)
            y_ref[:, ln] = (o * (g * _sigmoid(g))).astype(BF16)

    tile = pl.BlockSpec((None, TQ, NH * NA_DH), lambda hp, b, m: (b, m, hp))
    return pl.pallas_call(
        body, name="na_fwd", grid=(ngrp, B, nm),
        in_specs=[q_spec, k_spec, v_spec, g_spec, bias_spec, ANY],
        out_specs=(tile, tile, ANY),
        out_shape=(jax.ShapeDtypeStruct((B, L, 512), BF16),) * 2
        + (jax.ShapeDtypeStruct((N_SHARD,) + wada_b.shape, BF16),),
        scratch_shapes=[pltpu.VMEM((N_SHARD,) + wada_b.shape, BF16),
                        pltpu.SemaphoreType.DMA((6,)), pltpu.SemaphoreType.DMA((6,)),
                        pltpu.SemaphoreType.DMA((5,))],
        compiler_params=_params(("arbitrary",) * 3, vmem_mb=56))(P, P, P, P, bias, wada_b)


def _na_bwd_call(P, bias, dY, o_na, L, LC):
    B, T, _ = P.shape
    rows = L // GRID_W
    NH = 4
    W = NH * NA_DH
    nm, q_spec, k_spec, v_spec, g_spec, bias_spec = _na_specs(L, T, rows, NH)
    scale = NA_DH ** -0.5

    RB = 32

    def body(q_ref, k_ref, v_ref, g_ref, bias_ref, dy_ref, o_ref, dq_ref, dg_ref, dk_ref, dv_ref, dt_ref,
             db_ref, s1_ref, s2_ref, dp1_ref, dp2_ref, p1_ref, p2_ref, ds1_ref, ds2_ref, dkt_ref, dvt_ref):
        b, m = pl.program_id(1), pl.program_id(2)
        typ, start = _na_tile(m, nm, rows)

        @pl.when(m == 0)
        def _():
            dkt_ref[...] = jnp.zeros_like(dkt_ref)
            dvt_ref[...] = jnp.zeros_like(dvt_ref)

        @pl.when((m == 0) & (b == 0))
        def _():
            db_ref[...] = jnp.zeros_like(db_ref)

        for hh in range(NH):
            ln = slice(hh * NA_DH, (hh + 1) * NA_DH)
            q = q_ref[:, ln]
            kw, vw = k_ref[pl.ds(start, KW), ln], v_ref[pl.ds(start, KW), ln]
            kc, vc = k_ref[L:L + LC, ln], v_ref[L:L + LC, ln]
            g = g_ref[:, ln].astype(F32)
            sg = _sigmoid(g)
            dy = dy_ref[:, ln].astype(F32)
            do = (dy * (g * sg)).astype(BF16)
            s1_ref[hh] = _dot_nt(q, kw)
            s2_ref[hh] = _dot_nt(q, kc)
            dp1_ref[hh] = _dot_nt(do, vw)
            dp2_ref[hh] = _dot_nt(do, vc)

            def rows_pass(r, carry, hh=hh):
                rw = pl.ds(pl.multiple_of(r * RB, RB), RB)
                a = s1_ref[hh, rw, :] + bias_ref[hh, typ, rw, :]
                c = s2_ref[hh, rw, :]
                mx = jnp.maximum(jnp.max(a, axis=-1, keepdims=True), jnp.max(c, axis=-1, keepdims=True))
                e1, e2 = jnp.exp(a - mx), jnp.exp(c - mx)
                inv = 1.0 / (jnp.sum(e1, axis=-1, keepdims=True) + jnp.sum(e2, axis=-1, keepdims=True))
                p1, p2 = e1 * inv, e2 * inv
                p1_ref[hh, rw, :] = p1.astype(BF16)
                p2_ref[hh, rw, :] = p2.astype(BF16)
                dp1, dp2 = dp1_ref[hh, rw, :], dp2_ref[hh, rw, :]
                delta = jnp.sum(p1 * dp1, axis=-1, keepdims=True) + jnp.sum(p2 * dp2, axis=-1, keepdims=True)
                ds1 = p1 * (dp1 - delta)
                db_ref[hh, typ, rw, :] += ds1
                ds1_ref[hh, rw, :] = ds1.astype(BF16)
                ds2_ref[hh, rw, :] = (p2 * (dp2 - delta)).astype(BF16)
                return carry

            lax.fori_loop(0, TQ // RB, rows_pass, 0, unroll=True)
            p1b, p2b, ds1b, ds2b = p1_ref[hh], p2_ref[hh], ds1_ref[hh], ds2_ref[hh]
            dg_ref[:, ln] = (dy * o_ref[:, ln].astype(F32) * (sg * (1.0 + g * (1.0 - sg)))).astype(BF16)
            dq_ref[:, ln] = ((_dot(ds1b, kw) + _dot(ds2b, kc)) * scale).astype(BF16)
            dkt_ref[ln, pl.ds(start, KW)] += _dot_tn(q, ds1b)
            dvt_ref[ln, pl.ds(start, KW)] += _dot_tn(do, p1b)
            dkt_ref[ln, L:L + LC] += _dot_tn(q, ds2b)
            dvt_ref[ln, L:L + LC] += _dot_tn(do, p2b)

        @pl.when(m == nm - 1)
        def _():
            dk_ref[...] = dkt_ref[...].T
            dv_ref[...] = dvt_ref[...].T

        @pl.when((m == nm - 1) & (b == B - 1))
        def _():
            for hh in range(NH):
                for dr, t in _bias_tile_sums(db_ref, hh).items():
                    dt_ref[hh, dr + 7] = t

    tile = pl.BlockSpec((None, TQ, W), lambda hp, b, m: (b, m, hp))
    kv_out = pl.BlockSpec((None, T, W), lambda hp, b, m: (b, 0, hp))
    wide, narrow = (NH, TQ, KW), (NH, TQ, LC)
    return pl.pallas_call(
        body, name="na_bwd", grid=(8 // NH, B, nm),
        in_specs=[q_spec, k_spec, v_spec, g_spec, bias_spec, tile, tile],
        out_specs=(tile, tile, kv_out, kv_out,
                   pl.BlockSpec((NH, 15, GRID_W, GRID_W), lambda hp, b, m: (hp, 0, 0, 0))),
        out_shape=(jax.ShapeDtypeStruct((B, L, 512), BF16), jax.ShapeDtypeStruct((B, L, 512), BF16),
                   jax.ShapeDtypeStruct((B, T, 512), F32), jax.ShapeDtypeStruct((B, T, 512), F32),
                   jax.ShapeDtypeStruct((bias.shape[0], 15, GRID_W, GRID_W), F32)),
        scratch_shapes=[pltpu.VMEM((NH,) + bias.shape[1:], F32),
                        pltpu.VMEM(wide, F32), pltpu.VMEM(narrow, F32), pltpu.VMEM(wide, F32), pltpu.VMEM(narrow, F32),
                        pltpu.VMEM(wide, BF16), pltpu.VMEM(narrow, BF16), pltpu.VMEM(wide, BF16),
                        pltpu.VMEM(narrow, BF16), pltpu.VMEM((W, T), F32), pltpu.VMEM((W, T), F32)],
        compiler_params=_params(("arbitrary",) * 3, vmem_mb=60))(P, P, P, P, bias, dY, o_na)


def _head_scalar(dec_ref, h):
    lane = lax.broadcasted_iota(jnp.int32, dec_ref.shape, 1)
    return -jnp.sum(jnp.where(lane == h, jnp.exp(dec_ref[...]), 0.0), axis=1, keepdims=True)


def _chunk_decay(lgf, lgb):
    tau = lax.broadcasted_iota(jnp.int32, (TQ, 1), 0).astype(F32)
    sig = lax.broadcasted_iota(jnp.int32, (1, TQ), 1).astype(F32)
    dist = tau - sig
    dm = jnp.exp(dist * jnp.where(dist > 0, lgf, -lgb)) * jnp.where(dist == 0, 2.0, 1.0)
    return tau, dist, dm


def _ret_states_call(P, dec_f, dec_b, L, LC):
    B, T, _ = P.shape
    n = L // TQ

    def body(df_ref, db_ref, k_ref, v_ref, sf_ref, sb_ref):
        h = pl.program_id(1)
        lgf, lgb = _head_scalar(df_ref, h), _head_scalar(db_ref, h)
        tau = lax.broadcasted_iota(jnp.int32, (TQ, 1), 0).astype(F32)
        jc = lax.broadcasted_iota(jnp.int32, (LC, 1), 0).astype(F32)
        wf, wb = jnp.exp(lgf * (TQ - 1.0 - tau)), jnp.exp(lgb * tau)
        gcf, gcb = jnp.exp(lgf * float(TQ)), jnp.exp(lgb * float(TQ))
        kc, vc = k_ref[L:L + LC, :].astype(F32), v_ref[L:L + LC, :]

        def chunk_state(i, w):
            ks = pl.multiple_of(i * TQ, TQ)
            return _dot_tn((k_ref[pl.ds(ks, TQ), :].astype(F32) * w).astype(BF16), v_ref[pl.ds(ks, TQ), :])

        def fwd(i, s):
            sf_ref[i] = s
            return gcf * s + chunk_state(i, wf)

        lax.fori_loop(0, n, fwd, _dot_tn((kc * jnp.exp(lgf * (LC - 1.0 - jc))).astype(BF16), vc), unroll=True)

        def bwd(r, s):
            i = n - 1 - r
            sb_ref[i] = s
            return gcb * s + chunk_state(i, wb)

        lax.fori_loop(0, n, bwd, _dot_tn((kc * jnp.exp(lgb * jc)).astype(BF16), vc), unroll=True)

    st = pl.BlockSpec((None, None, n, RET_DK, RET_DK), lambda b, h: (b, h, 0, 0, 0))
    return pl.pallas_call(
        body, name="ret_states", grid=(B, 4),
        in_specs=[pl.BlockSpec((1, 4), lambda b, h: (0, 0)), pl.BlockSpec((1, 4), lambda b, h: (0, 0)),
                  pl.BlockSpec((None, T, 128), lambda b, h: (b, 0, 20 + h)),
                  pl.BlockSpec((None, T, 128), lambda b, h: (b, 0, 24 + h))],
        out_specs=(st, st),
        out_shape=(jax.ShapeDtypeStruct((B, 4, n, RET_DK, RET_DK), F32),) * 2,
        compiler_params=_params(("arbitrary",) * 2))(dec_f, dec_b, P, P)


def _retc_fwd_call(P, sf, sb, dec_f, dec_b, ret_norm_g, L):
    B, T, _ = P.shape
    sec = lambda k: pl.BlockSpec((None, TQ, 512), lambda b, i: (b, i, k))
    dec_spec = pl.BlockSpec((1, 4), lambda b, i: (0, 0))
    st_spec = pl.BlockSpec((None, 4, None, RET_DK, RET_DK), lambda b, i: (b, 0, i, 0, 0))

    def body(df_ref, db_ref, q_ref, k_ref, v_ref, g_ref, gn_ref, sf_ref, sb_ref, y_ref, o_ref):
        for h in range(4):
            ln = slice(h * RET_DK, (h + 1) * RET_DK)
            lgf, lgb = _head_scalar(df_ref, h), _head_scalar(db_ref, h)
            tau, _, dm = _chunk_decay(lgf, lgb)
            q = q_ref[:, ln]
            qf = q.astype(F32)
            acc = _dot((_dot_nt(q, k_ref[:, ln]) * dm).astype(BF16), v_ref[:, ln])
            acc = acc + _dot((qf * jnp.exp(lgf * (tau + 1.0))).astype(BF16), sf_ref[h].astype(BF16))
            acc = acc + _dot((qf * jnp.exp(lgb * (TQ - tau))).astype(BF16), sb_ref[h].astype(BF16))
            o_ref[:, ln] = acc
            rn = lax.rsqrt(jnp.mean(acc * acc, axis=-1, keepdims=True) + EPS)
            g = g_ref[:, ln].astype(F32)
            y_ref[:, ln] = ((acc * rn * gn_ref[:, ln]) * (g * _sigmoid(g))).astype(BF16)

    tile = pl.BlockSpec((None, TQ, 512), lambda b, i: (b, i, 0))
    return pl.pallas_call(
        body, name="ret_fwd", grid=(B, L // TQ),
        in_specs=[dec_spec, dec_spec, sec(4), sec(5), sec(6), sec(7),
                  pl.BlockSpec((1, 512), lambda b, i: (0, 0)), st_spec, st_spec],
        out_specs=(tile, tile),
        out_shape=(jax.ShapeDtypeStruct((B, L, 512), BF16), jax.ShapeDtypeStruct((B, L, 512), F32)),
        compiler_params=_params(("arbitrary",) * 2))(dec_f, dec_b, P, P, P, P, ret_norm_g, sf, sb)


def _retc_bwd_call(P, sf, sb, dec_f, dec_b, ret_norm_g, o_ret, dY, cos2, sin2, L, LC):
    B, T, _ = P.shape
    n = L // TQ
    C = float(TQ)
    kscale = RET_DK ** -0.5
    st_spec = pl.BlockSpec((None, 4, n, RET_DK, RET_DK), lambda b, i: (b, 0, 0, 0, 0))

    def body(df_ref, db_ref, q_ref, k_ref, v_ref, g_ref, gn_ref, o_ref, dy_ref, cos_ref, sin_ref, sf_ref, sb_ref,
             dq_ref, dg_ref, dk_ref, dv_ref, dgn_ref, dlg_ref, dsf_ref, dsb_ref):
        i = pl.program_id(1)

        @pl.when(i == 0)
        def _():
            dk_ref[...] = jnp.zeros_like(dk_ref)
            dv_ref[...] = jnp.zeros_like(dv_ref)
            dgn_ref[...] = jnp.zeros_like(dgn_ref)
            dlg_ref[...] = jnp.zeros_like(dlg_ref)

        rows = pl.ds(pl.multiple_of(i * TQ, TQ), TQ)
        cs, sn = cos_ref[rows, :], sin_ref[rows, :]

        def one_head(h):
            ln = slice(h * RET_DK, (h + 1) * RET_DK)
            lgf, lgb = _head_scalar(df_ref, h), _head_scalar(db_ref, h)
            tau, dist, dm = _chunk_decay(lgf, lgb)

            def add_lg(row, x):
                csum = jnp.sum(x, axis=0, keepdims=True)
                tot = csum[:, 0:128]
                for part in range(1, x.shape[1] // 128):
                    tot = tot + csum[:, part * 128:(part + 1) * 128]
                dlg_ref[h, row:row + 1, :] += tot

            q = q_ref[:, ln]
            qf = q.astype(F32)
            o = o_ref[:, ln]
            g = g_ref[:, ln].astype(F32)
            dy = dy_ref[:, ln].astype(F32)
            gn = gn_ref[:, ln]
            sg = _sigmoid(g)
            rn = lax.rsqrt(jnp.mean(o * o, axis=-1, keepdims=True) + EPS)
            nrm = o * rn
            dg_ref[:, ln] = (dy * (nrm * gn) * (sg * (1.0 + g * (1.0 - sg)))).astype(BF16)
            dhn = dy * (g * sg)
            dgn_ref[:, ln] += jnp.sum(dhn * nrm, axis=0, keepdims=True)
            dnrm = dhn * gn
            do = rn * (dnrm - nrm * jnp.mean(dnrm * nrm, axis=-1, keepdims=True))
            dob = do.astype(BF16)
            ki, vi = k_ref[rows, ln], v_ref[rows, ln]
            s = _dot_nt(q, ki)
            dsv = _dot_nt(dob, vi)
            dsb = (dsv * dm).astype(BF16)
            dk_ref[rows, ln] += _dot_tn(dsb, q)
            dv_ref[rows, ln] += _dot_tn((s * dm).astype(BF16), dob)
            xw = s * dsv * dm * jnp.abs(dist)
            fpart = jnp.where(dist > 0, xw, 0.0)
            add_lg(0, fpart)
            add_lg(1, xw - fpart)
            dq = _dot(dsb, ki)
            af, ab = jnp.exp(lgf * (tau + 1.0)), jnp.exp(lgb * (C - tau))
            qa, qb = (qf * af).astype(BF16), (qf * ab).astype(BF16)
            sfi, sbi = sf_ref[h, i].astype(BF16), sb_ref[h, i].astype(BF16)
            dq = dq + af * _dot_nt(dob, sfi) + ab * _dot_nt(dob, sbi)
            dsf_ref[h, i] = _dot_tn(qa, dob)
            dsb_ref[h, i] = _dot_tn(qb, dob)
            add_lg(0, (tau + 1.0) * (_dot(qa, sfi) * do))
            add_lg(1, (C - tau) * (_dot(qb, sbi) * do))
            dq_ref[:, ln] = (dq * cs - pltpu.roll(dq, 64, 1) * sn).astype(BF16)

            @pl.when(i == n - 1)
            def _():
                jc = lax.broadcasted_iota(jnp.int32, (LC, 1), 0).astype(F32)
                crow = pl.ds(L, LC)

                def through_state(rws, w, dw, gst, row):
                    kk, vv = k_ref[rws, ln].astype(F32), v_ref[rws, ln]
                    gb = gst.astype(BF16)
                    vg = _dot_nt(vv, gb)
                    kw = kk * w
                    dk_ref[rws, ln] += w * vg
                    dv_ref[rws, ln] += _dot(kw.astype(BF16), gb)
                    add_lg(row, dw * (kw * vg))

                def scan(gc, w, dw, st_ref, dst_ref, order, row):
                    def step(r, gst):
                        j = order(r)
                        through_state(pl.ds(pl.multiple_of(j * TQ, TQ), TQ), w, dw, gst, row)
                        add_lg(row, (C * gc) * (gst * st_ref[h, j]))
                        return dst_ref[h, j] + gc * gst
                    return lax.fori_loop(0, n, step, jnp.zeros((RET_DK, RET_DK), F32), unroll=True)

                gcf, gcb = jnp.exp(lgf * C), jnp.exp(lgb * C)
                g0 = scan(gcf, jnp.exp(lgf * (C - 1.0 - tau)), C - 1.0 - tau, sf_ref, dsf_ref,
                          lambda r: n - 1 - r, 0)
                through_state(crow, jnp.exp(lgf * (LC - 1.0 - jc)), LC - 1.0 - jc, g0, 0)
                g1 = scan(gcb, jnp.exp(lgb * tau), tau, sb_ref, dsb_ref, lambda r: r, 1)
                through_state(crow, jnp.exp(lgb * jc), jc, g1, 1)
                dk = dk_ref[:, ln]
                dk_ref[:, ln] = (dk * cos_ref[...] - pltpu.roll(dk, 64, 1) * sin_ref[...]) * kscale

        for h in range(4):
            one_head(h)

    sec = lambda k: pl.BlockSpec((None, TQ, 512), lambda b, i: (b, i, k))
    full = lambda k: pl.BlockSpec((None, T, 512), lambda b, i: (b, 0, k))
    dec_spec = pl.BlockSpec((1, 4), lambda b, i: (0, 0))
    tab = pl.BlockSpec((T, RET_DK), lambda b, i: (0, 0))
    return pl.pallas_call(
        body, name="ret_bwd", grid=(B, n),
        in_specs=[dec_spec, dec_spec, sec(4), full(5), full(6), sec(7),
                  pl.BlockSpec((1, 512), lambda b, i: (0, 0)), sec(0), sec(1), tab, tab, st_spec, st_spec],
        out_specs=(sec(0), sec(0), full(0), full(0),
                   pl.BlockSpec((None, 1, 512), lambda b, i: (b, 0, 0)),
                   pl.BlockSpec((None, 4, 8, 128), lambda b, i: (b, 0, 0, 0))),
        out_shape=(jax.ShapeDtypeStruct((B, L, 512), BF16), jax.ShapeDtypeStruct((B, L, 512), BF16),
                   jax.ShapeDtypeStruct((B, T, 512), F32), jax.ShapeDtypeStruct((B, T, 512), F32),
                   jax.ShapeDtypeStruct((B, 1, 512), F32), jax.ShapeDtypeStruct((B, 4, 8, 128), F32)),
        scratch_shapes=[pltpu.VMEM((4, n, RET_DK, RET_DK), F32), pltpu.VMEM((4, n, RET_DK, RET_DK), F32)],
        compiler_params=_params(("arbitrary",) * 2, vmem_mb=56))(
            dec_f, dec_b, P, P, P, P, ret_norm_g, o_ret, dY, cos2, sin2, sf, sb)


def _out_call(y_na, y_ret, x, target, mod, final_g, wout_f):
    B, L, _ = x.shape
    TO = 2 * TQ

    def body(yn_ref, yr_ref, x_ref, t_ref, mod_ref, gf_ref, w_ref, dy_ref, dx2_ref, dwb_ref, sm_ref, dw_ref):
        b, i = pl.program_id(0), pl.program_id(1)

        @pl.when((b == 0) & (i == 0))
        def _():
            dw_ref[...] = jnp.zeros_like(dw_ref)
            sm_ref[...] = jnp.zeros_like(sm_ref)

        gate = mod_ref[pl.ds(b, 1), 2 * D:3 * D]
        gf = gf_ref[...]
        yn, yr = yn_ref[...], yr_ref[...]
        ylat = _dot(yn, w_ref[0:512, :]) + _dot(yr, w_ref[512:1024, :])
        x2 = x_ref[...] + gate * ylat
        r = lax.rsqrt(jnp.mean(x2 * x2, axis=-1, keepdims=True) + EPS)
        xr = x2 * r
        err = xr * gf - t_ref[...]
        sm_ref[1:2, :] += jnp.sum(err * err, axis=0, keepdims=True)
        dout = err * (1.0 / D)
        sm_ref[0:1, :] += jnp.sum(dout * xr, axis=0, keepdims=True)
        gd = dout * gf
        dx2 = r * (gd - xr * jnp.mean(gd * xr, axis=-1, keepdims=True))
        dx2_ref[...] = dx2
        sm_ref[pl.ds(2 + b, 1), :] += jnp.sum(dx2 * ylat, axis=0, keepdims=True)
        dyl = (gate * dx2).astype(BF16)
        dy_ref[:, 0:512] = _dot_nt(dyl, w_ref[0:512, :]).astype(BF16)
        dy_ref[:, 512:1024] = _dot_nt(dyl, w_ref[512:1024, :]).astype(BF16)
        dw_ref[0:512, :] += _dot_tn(yn, dyl)
        dw_ref[512:1024, :] += _dot_tn(yr, dyl)

        @pl.when((b == B - 1) & (i == L // TO - 1))
        def _():
            dwb_ref[...] = dw_ref[...].astype(BF16)

    half = pl.BlockSpec((None, TO, 512), lambda b, i: (b, i, 0))
    full = pl.BlockSpec((None, TO, D), lambda b, i: (b, i, 0))
    return pl.pallas_call(
        body, name="out_proj_loss", grid=(B, L // TO),
        in_specs=[half, half, full, full,
                  pl.BlockSpec((8, 3 * D), lambda b, i: (0, 0)),
                  pl.BlockSpec((1, D), lambda b, i: (0, 0)),
                  pl.BlockSpec((D, D), lambda b, i: (0, 0))],
        out_specs=(full, full, pl.BlockSpec((D, D), lambda b, i: (0, 0)),
                   pl.BlockSpec((8, D), lambda b, i: (0, 0))),
        out_shape=(jax.ShapeDtypeStruct((B, L, D), BF16), jax.ShapeDtypeStruct((B, L, D), F32),
                   jax.ShapeDtypeStruct((D, D), BF16), jax.ShapeDtypeStruct((8, D), F32)),
        scratch_shapes=[pltpu.VMEM((D, D), F32)],
        compiler_params=_params(("arbitrary",) * 2))(y_na, y_ret, x, target, mod, final_g, wout_f)


def _dh_call(dsec, win_f, x, ctx, dx2, mod, norm_g, cp_in, cp_out):
    B, L, _ = x.shape
    LC = ctx.shape[1]
    nl = L // TQ

    def body(d0, d1, d2, d3, d4, d5, d6, d7, w_ref, x_ref, ctx_ref, dx2_ref, mod_ref, g_ref, cpi_ref, cpo_ref,
             gx_ref, sm_ref, sli_ref, slo_ref, ssem, rsem, lsem):
        drefs = (d0, d1, d2, d3, d4, d5, d6, d7)
        b, t = pl.program_id(0), pl.program_id(1)
        is_lat = t < nl

        @pl.when((b == 0) & (t == 0))
        def _():
            sm_ref[...] = jnp.zeros_like(sm_ref)

        def dh_of(secs):
            acc = jnp.zeros((TQ, D), F32)
            for sec in secs:
                s, half = divmod(sec, 2)
                acc = acc + _dot_nt(drefs[sec][...].astype(BF16), w_ref[s, :, half * 512:(half + 1) * 512])
            return acc

        def norm_bwd(dh, xt, mrow):
            scale = mrow[:, D:2 * D]
            g = g_ref[...]
            rstd = lax.rsqrt(jnp.mean(xt * xt, axis=-1, keepdims=True) + EPS)
            xn = xt * rstd
            dshift = jnp.sum(dh, axis=0, keepdims=True)
            dscale = jnp.sum(dh * (xn * g), axis=0, keepdims=True)
            dhn = dh * (1.0 + scale)
            sm_ref[0:1, :] += jnp.sum(dhn * xn, axis=0, keepdims=True)
            dxn = dhn * g
            dx = rstd * (dxn - xn * jnp.mean(dxn * xn, axis=-1, keepdims=True))
            return dshift, dscale, dx

        @pl.when(is_lat)
        def _():
            dshift, dscale, dx = norm_bwd(dh_of(range(8)), x_ref[...], mod_ref[pl.ds(b, 1), :])
            sm_ref[pl.ds(3 + b, 1), :] += dshift
            sm_ref[pl.ds(3 + B + b, 1), :] += dscale
            gx_ref[...] = dx2_ref[...] + dx

        @pl.when(jnp.logical_not(is_lat))
        def _():
            dshift, dscale, _ = norm_bwd(dh_of((1, 2, 5, 6)), ctx_ref[...], mod_ref[B:B + 1, :])
            sm_ref[1:2, :] += dshift
            sm_ref[2:3, :] += dscale

        mx, my, mc = _mesh_pos()
        s = 2 * mx + my
        cps, sls = (cpi_ref, cpo_ref), (sli_ref, slo_ref)
        own = [pltpu.make_async_copy(cps[a].at[s], sls[a].at[s], lsem.at[a]) for a in range(2)]
        sends, recvs, k = [], [], 0
        for px, py in _other_chips(mx, my):
            ps = 2 * px + py
            for a in range(2):
                sends.append(_remote(cps[a].at[ps], sls[a].at[s], ssem, rsem, k, (px, py, mc)))
                recvs.append(_remote(cps[a].at[s], sls[a].at[ps], ssem, rsem, k, (px, py, mc)))
                k += 1

        @pl.when((b == 0) & (t == 0))
        def _():
            for cp in own + sends:
                cp.start()

        @pl.when((b == B - 1) & (t == nl))
        def _():
            _finish(own, sends, recvs)

    lat = lambda b, t: (b, jnp.minimum(t, nl - 1), 0)
    tok = lambda b, t: (b, t, 0)
    sec_specs = [pl.BlockSpec((None, TQ, 512), lat if sec in (0, 3, 4, 7) else tok) for sec in range(8)]
    return pl.pallas_call(
        body, name="dh_norm_bwd", grid=(B, nl + 1),
        in_specs=sec_specs + [
            pl.BlockSpec((N_SHARD, D, D), lambda b, t: (0, 0, 0)),
            pl.BlockSpec((None, TQ, D), lat),
            pl.BlockSpec((None, LC, D), lambda b, t: (b, 0, 0)),
            pl.BlockSpec((None, TQ, D), lat),
            pl.BlockSpec((8, 3 * D), lambda b, t: (0, 0)),
            pl.BlockSpec((1, D), lambda b, t: (0, 0)), ANY, ANY],
        out_specs=(pl.BlockSpec((None, TQ, D), lat), pl.BlockSpec((8, D), lambda b, t: (0, 0)), ANY, ANY),
        out_shape=(jax.ShapeDtypeStruct((B, L, D), F32), jax.ShapeDtypeStruct((8, D), F32),
                   jax.ShapeDtypeStruct(cp_in.shape, cp_in.dtype), jax.ShapeDtypeStruct(cp_out.shape, cp_out.dtype)),
        scratch_shapes=[pltpu.SemaphoreType.DMA((6,)), pltpu.SemaphoreType.DMA((6,)),
                        pltpu.SemaphoreType.DMA((2,))],
        compiler_params=_params(("arbitrary",) * 2))(*dsec, win_f, x, ctx, dx2, mod, norm_g, cp_in, cp_out)


def _dw_call(dsec, h, L):
    B, T, _ = h.shape
    TW = 2 * TQ
    nl = L // TW
    KV = (1, 2, 5, 6)

    def body(d0, d1, d2, d3, d4, d5, d6, d7, c1, c2, c5, c6, h_ref, hc_ref, dw_ref, acc_ref):
        drefs = (d0, d1, d2, d3, d4, d5, d6, d7)
        crefs = dict(zip(KV, (c1, c2, c5, c6)))
        b, t = pl.program_id(0), pl.program_id(1)

        @pl.when((b == 0) & (t == 0))
        def _():
            acc_ref[...] = jnp.zeros_like(acc_ref)

        def add(hb, refs, secs):
            for sec in secs:
                s, half = divmod(sec, 2)
                acc_ref[s, :, half * 512:(half + 1) * 512] += _dot_tn(hb, refs[sec][...].astype(BF16))

        @pl.when(t < nl)
        def _():
            add(h_ref[...], drefs, range(8))

        @pl.when(t == nl)
        def _():
            add(hc_ref[...], crefs, KV)

        @pl.when((b == B - 1) & (t == nl))
        def _():
            dw_ref[...] = acc_ref[...].astype(BF16)

    lat = lambda b, t: (b, jnp.minimum(t, nl - 1), 0)
    ctx = lambda b, t: (b, L // TQ, 0)
    return pl.pallas_call(
        body, name="dw_in", grid=(B, nl + 1),
        in_specs=[pl.BlockSpec((None, TW, 512), lat)] * 8 + [pl.BlockSpec((None, TQ, 512), ctx)] * 4
        + [pl.BlockSpec((None, TW, D), lat), pl.BlockSpec((None, TQ, D), ctx)],
        out_specs=pl.BlockSpec((N_SHARD, D, D), lambda b, t: (0, 0, 0)),
        out_shape=jax.ShapeDtypeStruct((N_SHARD, D, D), BF16),
        scratch_shapes=[pltpu.VMEM((N_SHARD, D, D), F32)],
        compiler_params=_params(("arbitrary",) * 2, vmem_mb=60))(*dsec, *[dsec[k] for k in KV], h, h)


def _mesh_pos():
    return lax.axis_index("x"), lax.axis_index("y"), lax.axis_index("c")


def _flip(v, f):
    return 1 - v if f else v


def _remote(src, dst, ssem, rsem, k, peer):
    return pltpu.make_async_remote_copy(src_ref=src, dst_ref=dst, send_sem=ssem.at[k], recv_sem=rsem.at[k],
                                        device_id=peer, device_id_type=MESH)


def _other_chips(x, y):
    return [(_flip(x, fx), _flip(y, fy)) for fx, fy in ((1, 0), (0, 1), (1, 1))]


def _gather_copies(own_ref, all_ref, out_ref, hr, sems, k0, l0):
    ssem, rsem, lsem = sems
    mx, my, mc = _mesh_pos()
    s = 2 * mx + my
    sib = (mx, my, 1 - mc)
    own = pltpu.make_async_copy(own_ref, all_ref.at[s], lsem.at[l0])
    send, recv, fsend, frecv = [], [], [], []
    outs = [pltpu.make_async_copy(all_ref.at[s], out_ref.at[s], lsem.at[l0 + 1])]
    for k, (px, py) in enumerate(_other_chips(mx, my)):
        ps = 2 * px + py
        mine = all_ref.at[s, pl.ds(mc * hr, hr)]
        send.append(_remote(mine, mine, ssem, rsem, k0 + k, (px, py, mc)))
        got = all_ref.at[ps, pl.ds(mc * hr, hr)]
        recv.append(_remote(mine, got, ssem, rsem, k0 + k, (px, py, mc)))
        fsend.append(_remote(got, got, ssem, rsem, k0 + 3 + k, sib))
        theirs = all_ref.at[ps, pl.ds((1 - mc) * hr, hr)]
        frecv.append(_remote(theirs, theirs, ssem, rsem, k0 + 3 + k, sib))
        outs.append(pltpu.make_async_copy(all_ref.at[ps], out_ref.at[ps], lsem.at[l0 + 2 + k]))
    return own, send, recv, fsend, frecv, outs


def _all_to_all_small(src, dst_all, ssem, rsem, k0, x, y, cc):
    me = 4 * x + 2 * y + cc
    sends, recvs = [], []
    for f in range(1, N_DEV):
        px, py, pc = _flip(x, f & 4), _flip(y, f & 2), _flip(cc, f & 1)
        sends.append(_remote(src, dst_all.at[me], ssem, rsem, k0 + f - 1, (px, py, pc)))
        recvs.append(_remote(src, dst_all.at[4 * px + 2 * py + pc], ssem, rsem, k0 + f - 1, (px, py, pc)))
    return sends, recvs


def _finish(local, sends, recvs):
    for cp in recvs:
        cp.wait_recv()
    for cp in sends:
        cp.wait_send()
    for cp in local:
        cp.wait()


def _c_gather_call(c, rpb_flat):
    def body(c_ref, r_ref, c_all, bias_out, bias_ref, et_ref, ssem, rsem, lsem):
        x, y, cc = _mesh_pos()
        me = 4 * x + 2 * y + cc
        local = [pltpu.make_async_copy(c_ref, c_all.at[me], lsem.at[0])]
        c_send, c_recv = _all_to_all_small(c_ref, c_all, ssem, rsem, 0, x, y, cc)
        for cp in local + c_send:
            cp.start()
        bias_out_copies = _bias_body(r_ref, bias_ref, et_ref, bias_out, lsem.at[1])
        _finish(local + bias_out_copies, c_send, c_recv)

    bias_shape = (rpb_flat.shape[0], 3, TQ, KW)
    return pl.pallas_call(
        body, name="c_gather",
        in_specs=[pl.BlockSpec(memory_space=pltpu.VMEM), pl.BlockSpec(memory_space=pltpu.SMEM)],
        out_specs=(pl.BlockSpec(memory_space=pltpu.VMEM), ANY),
        out_shape=(jax.ShapeDtypeStruct((N_DEV,) + c.shape, c.dtype), jax.ShapeDtypeStruct(bias_shape, F32)),
        scratch_shapes=[pltpu.VMEM(bias_shape, F32), pltpu.VMEM((15, GRID_W, GRID_W), F32),
                        pltpu.SemaphoreType.DMA((N_DEV - 1,)), pltpu.SemaphoreType.DMA((N_DEV - 1,)),
                        pltpu.SemaphoreType.DMA((2,))],
        compiler_params=pltpu.CompilerParams(vmem_limit_bytes=56 << 20))(c, rpb_flat)


VROWS = 32


def _grad_halves_call(dwin_b, dwout_b, dbias, dlg):
    arrs = (dwin_b, dwout_b)
    hrs = [a.shape[1] // 2 for a in arrs]

    def body(din, dout, db_ref, dlg_ref, cp_in, cp_out, drpb_ref, dlgo_ref, got_in, got_out, p_ref, ssem, rsem):
        x, y, cc = _mesh_pos()
        sib = (x, y, 1 - cc)
        srcs, gots, cps = (din, dout), (got_in, got_out), (cp_in, cp_out)
        halves = [_remote(srcs[a].at[:, pl.ds((1 - cc) * hrs[a], hrs[a])], gots[a], ssem, rsem, a, sib)
                  for a in range(2)]
        for cp in halves:
            cp.start()
        _small_reduce_body(db_ref, dlg_ref, drpb_ref, dlgo_ref, p_ref)
        for cp in halves:
            cp.wait_recv()
        for a in range(2):
            for j in range(N_SHARD):
                def add(i, carry, a=a, j=j):
                    r = pl.multiple_of(i * VROWS, VROWS)
                    mine = srcs[a][j, pl.ds(pl.multiple_of(cc * hrs[a] + r, VROWS), VROWS), :].astype(F32)
                    cps[a][j, pl.ds(r, VROWS), :] = (
                        mine + gots[a][j, pl.ds(r, VROWS), :].astype(F32)).astype(BF16)
                    return carry
                lax.fori_loop(0, hrs[a] // VROWS, add, 0)
        for cp in halves:
            cp.wait_send()

    vmem = pl.BlockSpec(memory_space=pltpu.VMEM)
    half_shapes = [(N_SHARD, hrs[a], arrs[a].shape[2]) for a in range(2)]
    return pl.pallas_call(
        body, name="grad_halves",
        in_specs=[vmem] * 4, out_specs=(vmem,) * 4,
        out_shape=(jax.ShapeDtypeStruct(half_shapes[0], BF16), jax.ShapeDtypeStruct(half_shapes[1], BF16),
                   jax.ShapeDtypeStruct((dbias.shape[0], 16, 32), F32), jax.ShapeDtypeStruct((32, 128), F32)),
        scratch_shapes=[pltpu.VMEM(half_shapes[0], BF16), pltpu.VMEM(half_shapes[1], BF16),
                        pltpu.VMEM((32, GRID_W), F32),
                        pltpu.SemaphoreType.DMA((2,)), pltpu.SemaphoreType.DMA((2,))],
        compiler_params=pltpu.CompilerParams(vmem_limit_bytes=56 << 20))(dwin_b, dwout_b, dbias, dlg)


def _grad_finish_call(sl_in, sl_out, small):
    arrs = (sl_in, sl_out)

    def body(sin, sout, sm, gin, gout, sm_all, h_in, h_out, ssem, rsem, lsem):
        x, y, cc = _mesh_pos()
        me = 4 * x + 2 * y + cc
        sib = (x, y, 1 - cc)
        sls, hs, gs = (sin, sout), (h_in, h_out), (gin, gout)
        sm_send, sm_recv = _all_to_all_small(sm, sm_all, ssem, rsem, 2, x, y, cc)
        sm_own = pltpu.make_async_copy(sm, sm_all.at[me], lsem.at[0])
        for cp in sm_send + [sm_own]:
            cp.start()
        for a in range(2):
            def total(i, carry, a=a):
                rows = pl.ds(pl.multiple_of(i * VROWS, VROWS), VROWS)
                sl = sls[a]
                hs[a][rows, :] = ((sl[0, rows, :].astype(F32) + sl[1, rows, :].astype(F32))
                                  + sl[2, rows, :].astype(F32)) + sl[3, rows, :].astype(F32)
                return carry
            lax.fori_loop(0, arrs[a].shape[1] // VROWS, total, 0)
        mine = [pltpu.make_async_copy(hs[a], gs[a].at[cc], lsem.at[1 + a]) for a in range(2)]
        back = [_remote(hs[a], gs[a].at[cc], ssem, rsem, a, sib) for a in range(2)]
        back_recv = [_remote(hs[a], gs[a].at[1 - cc], ssem, rsem, a, sib) for a in range(2)]
        for cp in mine + back:
            cp.start()
        _finish(mine + [sm_own], back + sm_send, back_recv + sm_recv)

    vmem = pl.BlockSpec(memory_space=pltpu.VMEM)
    return pl.pallas_call(
        body, name="grad_finish",
        in_specs=[vmem] * 3, out_specs=(vmem,) * 3,
        out_shape=(jax.ShapeDtypeStruct((2,) + sl_in.shape[1:], F32),
                   jax.ShapeDtypeStruct((2,) + sl_out.shape[1:], F32),
                   jax.ShapeDtypeStruct((N_DEV,) + small.shape, F32)),
        scratch_shapes=[pltpu.VMEM(sl_in.shape[1:], F32), pltpu.VMEM(sl_out.shape[1:], F32),
                        pltpu.SemaphoreType.DMA((9,)), pltpu.SemaphoreType.DMA((9,)),
                        pltpu.SemaphoreType.DMA((3,))],
        compiler_params=pltpu.CompilerParams(vmem_limit_bytes=48 << 20))(sl_in, sl_out, small)


def _adamw(w, g, m, v):
    m = ADAM_B1 * m + (1.0 - ADAM_B1) * g
    v = ADAM_B2 * v + (1.0 - ADAM_B2) * (g * g)
    m_hat = m / (1.0 - ADAM_B1 ** ADAM_STEP)
    v_hat = v / (1.0 - ADAM_B2 ** ADAM_STEP)
    return -ADAM_LR * (m_hat / (jnp.sqrt(v_hat) + ADAM_EPS) + ADAM_WD * w), m, v


def _adam_call(w, m, v, g, name):
    R, C = w.shape
    tr = 256

    def body(w_ref, m_ref, v_ref, g_ref, go_ref, d_ref, mo_ref, vo_ref):
        g = g_ref[...]
        go_ref[...] = g
        d_ref[...], mo_ref[...], vo_ref[...] = _adamw(w_ref[...], g, m_ref[...], v_ref[...])

    spec = pl.BlockSpec((tr, C), lambda i: (i, 0))
    return pl.pallas_call(
        body, name=name, grid=(R // tr,), in_specs=[spec] * 4,
        out_specs=(spec,) * 4, out_shape=(jax.ShapeDtypeStruct((R, C), F32),) * 4,
        compiler_params=_params(("arbitrary",)))(w, m, v, g)


R_GF, R_NG, R_LOSS, R_RNG, R_LGF, R_LGB, R_SHIFT, R_SCALE, R_GATE, R_SHIFT_C, R_SCALE_C, R_RNG2, R_RPB = (
    0, 1, 2, 3, 4, 5, 6, 8, 10, 12, 13, 14, 16)
W_GF, W_NG, W_CCTX, W_RNG, W_DF, W_DB, W_BADA, W_RPB = 0, 1, 2, 3, 4, 5, 6, 9


SMALL = (("final_norm_g", W_GF, 1, D), ("norm_g", W_NG, 1, D), ("c_ctx", W_CCTX, 1, D),
         ("ret_norm_g", W_RNG, 1, 512), ("ret_decay_fwd", W_DF, 1, 4), ("ret_decay_bwd", W_DB, 1, 4),
         ("b_ada", W_BADA, 3, D), ("na_rpb", W_RPB, 4, D))
N_SMALL = len(SMALL)


def _small_final_call(sm_all, c_t, wada_f, wada, m_ada, v_ada, small_w, small_m, small_v, B):
    ws = wada.shape[1]
    NB = N_DEV * B

    def body(*refs):
        sm_ref, ct_ref, wf_ref, wa_ref, ma_ref, va_ref = refs[:6]
        ins = refs[6:6 + 3 * N_SMALL]
        outs = refs[6 + 3 * N_SMALL:6 + 7 * N_SMALL]
        ga_ref, da_ref, mao_ref, vao_ref, loss_ref, dmod_ref, pk_ref = refs[6 + 7 * N_SMALL:]
        x, y, _ = _mesh_pos()
        s = 2 * x + y
        tot = sm_ref[0]
        for dv in range(1, N_DEV):
            tot = tot + sm_ref[dv]
        pk_ref[...] = jnp.zeros_like(pk_ref)
        for kind in range(3):
            for i, (_, row, nrow, width) in enumerate(SMALL):
                ref = ins[kind * N_SMALL + i]
                if nrow == 3:
                    for part in range(3):
                        pk_ref[kind, row + part:row + part + 1, :] = ref[:, part * D:(part + 1) * D]
                else:
                    pk_ref[kind, row:row + nrow, 0:width] = ref[...]
        w = pk_ref[0]
        cctx_ref = ins[2]
        for dv in range(N_DEV):
            for b in range(B):
                r = dv * B + b
                for part, row in enumerate((R_SHIFT, R_SCALE, R_GATE)):
                    dmod_ref[r:r + 1, part * D:(part + 1) * D] = sm_ref[dv, row + b:row + b + 1, :]
        dmod_ref[NB:NB + 1, 0:D] = tot[R_SHIFT_C:R_SHIFT_C + 1, :]
        dmod_ref[NB:NB + 1, D:2 * D] = tot[R_SCALE_C:R_SCALE_C + 1, :]
        dmod_ref[NB:NB + 1, 2 * D:3 * D] = jnp.zeros((1, D), F32)
        dmod_ref[NB + 1:, :] = jnp.zeros((dmod_ref.shape[0] - NB - 1, 3 * D), F32)
        dmod = dmod_ref[...]
        cc = cctx_ref[...]
        scc = _sigmoid(cc)
        ct = ct_ref[...]
        act_t = ct * _sigmoid(ct)
        dmc = dmod[NB:NB + 1, :].astype(BF16)
        dact = jnp.zeros((1, D), F32)
        for sh in range(N_SHARD):
            dact = dact + _dot_nt(dmc[:, sh * ws:(sh + 1) * ws], wf_ref[sh])
        g = jnp.zeros((16, D), F32)
        rows = lax.broadcasted_iota(jnp.int32, (16, D), 0)

        def put(g, row, val):
            return jnp.where(rows == row, val, g)

        g = put(g, W_GF, tot[R_GF:R_GF + 1, :])
        g = put(g, W_NG, tot[R_NG:R_NG + 1, :])
        g = put(g, W_CCTX, dact * (scc * (1.0 + cc * (1.0 - scc))))
        g = put(g, W_RNG, tot[R_RNG:R_RNG + 1, :] + tot[R_RNG2:R_RNG2 + 1, :])
        g = put(g, W_DF, tot[R_LGF:R_LGF + 1, :] * (-jnp.exp(w[W_DF:W_DF + 1, :])))
        g = put(g, W_DB, tot[R_LGB:R_LGB + 1, :] * (-jnp.exp(w[W_DB:W_DB + 1, :])))
        db = jnp.sum(dmod, axis=0, keepdims=True)
        for part in range(3):
            g = put(g, W_BADA + part, db[:, part * D:(part + 1) * D])
        for part in range(4):
            g = put(g, W_RPB + part, tot[R_RPB + part:R_RPB + part + 1, :])
        for kind, val in enumerate((g,) + _adamw(w, g, pk_ref[1], pk_ref[2])):
            for i, (_, row, nrow, width) in enumerate(SMALL):
                out = outs[kind * N_SMALL + i]
                if nrow == 3:
                    for part in range(3):
                        out[:, part * D:(part + 1) * D] = val[row + part:row + part + 1, :]
                else:
                    out[...] = val[row:row + nrow, 0:width]
        loss_ref[...] = jnp.broadcast_to(
            (0.5 / D) * jnp.sum(tot[R_LOSS:R_LOSS + 1, :], axis=1, keepdims=True), (8, 128))
        for sh in range(N_SHARD):
            @pl.when(s == sh)
            def _():
                ga = jnp.dot(act_t, dmod[:, sh * ws:(sh + 1) * ws], precision=HIGHEST,
                             preferred_element_type=F32)
                ga_ref[...] = ga
                da_ref[...], mao_ref[...], vao_ref[...] = _adamw(wa_ref[...], ga, ma_ref[...], va_ref[...])

    sh_small = tuple(jax.ShapeDtypeStruct(a.shape, F32) for a in small_w)
    sh_ada = jax.ShapeDtypeStruct(wada.shape, F32)
    res = pl.pallas_call(
        body, name="small_final",
        out_shape=sh_small * 4 + (sh_ada,) * 4 + (jax.ShapeDtypeStruct((8, 128), F32),),
        scratch_shapes=[pltpu.VMEM((NB + 8, 3 * D), F32), pltpu.VMEM((3, 16, D), F32)],
        compiler_params=_params(vmem_mb=56))(
            sm_all, c_t, wada_f, wada, m_ada, v_ada, *small_w, *small_m, *small_v)
    smalls = [res[k * N_SMALL:(k + 1) * N_SMALL] for k in range(4)]
    return smalls, res[4 * N_SMALL:4 * N_SMALL + 4], res[4 * N_SMALL + 4]


def _local_step(order, x, ctx, c_rows, norm_g, wada_b, b_shard, win_b, bias, dec_f, dec_b, ret_norm_g,
                wout_b, final_g, target):
    B, L, _ = x.shape
    LC = ctx.shape[1]
    assert B == 2
    cos2, sin2 = _rope_tables(L, LC)
    mod_part = _mod_part_call(c_rows, wada_b, b_shard)
    P, h, win_f, wout_f, mod = _inproj_gather_call(order, x, ctx, mod_part, norm_g, win_b, wout_b, cos2, sin2)
    y_na, o_na, wada_f = _na_fwd_call(P, bias, wada_b, L, LC)
    sf, sb = _ret_states_call(P, dec_f, dec_b, L, LC)
    y_ret, o_ret = _retc_fwd_call(P, sf, sb, dec_f, dec_b, ret_norm_g, L)
    dY, dx2, dwout_p, sm_out = _out_call(y_na, y_ret, x, target, mod, final_g, wout_f.reshape(D, D))
    dnq, dng, dnk, dnv, dbias = _na_bwd_call(P, bias, dY, o_na, L, LC)
    drq, drg, drk, drv, dgn, dlg = _retc_bwd_call(P, sf, sb, dec_f, dec_b, ret_norm_g, o_ret, dY, cos2, sin2, L, LC)
    dsec = (dnq, dnk, dnv, dng, drq, drk, drv, drg)
    dwin_b = _dw_call(dsec, h, L)
    cp_in, cp_out, drpb, dlg_sum = _grad_halves_call(
        dwin_b, dwout_p.reshape(N_SHARD, D // N_SHARD, D), dbias, dlg)
    grad_x, sm_dh, sl_in, sl_out = _dh_call(dsec, win_f, x, ctx, dx2, mod, norm_g, cp_in, cp_out)
    z = jnp.zeros((1, D), F32)
    pad = lambda v: jnp.pad(v.reshape(1, -1), ((0, 0), (0, D - v.size)))
    dlg_sum = dlg_sum.reshape(4, 8, 128)
    rpb_rows = jnp.pad(drpb[:, :15, :31].reshape(-1), (0, 4 * D - drpb.shape[0] * 465)).reshape(4, D)
    small = jnp.concatenate([
        sm_out[0:1], sm_dh[0:1], sm_out[1:2], pad(dgn[0]), pad(dlg_sum[:, 0, 0]), pad(dlg_sum[:, 1, 0]),
        sm_dh[3:5], sm_dh[5:7], sm_out[2:4], sm_dh[1:2], sm_dh[2:3], pad(dgn[1]), z, rpb_rows,
        jnp.zeros((SM_ROWS - 20, D), F32)], axis=0)
    return grad_x, sl_in, sl_out, small, wada_f


def kernel(x, c, ctx, c_ctx, norm_g, w_ada, b_ada, w_in, na_rpb, ret_decay_fwd, ret_decay_bwd, ret_norm_g, w_out, final_norm_g, loss_target, m_c_ctx, m_norm_g, m_w_ada, m_b_ada, m_w_in, m_na_rpb, m_ret_decay_fwd, m_ret_decay_bwd, m_ret_norm_g, m_w_out, m_final_norm_g, v_c_ctx, v_norm_g, v_w_ada, v_b_ada, v_w_in, v_na_rpb, v_ret_decay_fwd, v_ret_decay_bwd, v_ret_norm_g, v_w_out, v_final_norm_g):
    B = x.shape[0]
    c_all, bias = _c_gather_call(c, na_rpb[0].reshape(na_rpb.shape[1], -1))
    c_rows = jnp.concatenate([c_all.reshape(N_DEV * B, D), c_ctx.reshape(1, D), jnp.zeros((7, D), F32)], axis=0)
    mx, my = lax.axis_index("x"), lax.axis_index("y")
    order = jnp.stack([2 * mx + my, 2 * (1 - mx) + my, 2 * mx + (1 - my),
                       2 * (1 - mx) + (1 - my)]).astype(jnp.int32)
    ws = w_ada.shape[2]
    b_shard = lax.dynamic_slice(b_ada, (0, (2 * mx + my) * ws), (1, ws))
    grad_x, sl_in, sl_out, small, wada_f = _local_step(
        order, x, ctx, c_rows, norm_g, w_ada[0].astype(BF16), b_shard, w_in[0].astype(BF16), bias, ret_decay_fwd,
        ret_decay_bwd, ret_norm_g, w_out[0].astype(BF16), final_norm_g.reshape(1, D), loss_target)
    gin, gout, sm_all = _grad_finish_call(sl_in, sl_out, small)
    g_win, d_win, nm_win, nv_win = _adam_call(
        w_in[0], m_w_in[0], v_w_in[0], gin.reshape(w_in.shape[1:]), "adam_w_in")
    g_wout, d_wout, nm_wout, nv_wout = _adam_call(
        w_out[0], m_w_out[0], v_w_out[0], gout.reshape(w_out.shape[1:]), "adam_w_out")

    def small_inputs(gf, ng, cc, rng, df, db, bada, rpb):
        return (gf.reshape(1, D), ng, cc.reshape(1, D), rng, df, db, bada,
                jnp.pad(rpb.reshape(-1), (0, 4 * D - rpb.size)).reshape(4, D))

    c_t = c_rows.T
    smalls, adas, loss = _small_final_call(
        sm_all, c_t, wada_f, w_ada[0], m_w_ada[0], v_w_ada[0],
        small_inputs(final_norm_g, norm_g, c_ctx, ret_norm_g, ret_decay_fwd, ret_decay_bwd, b_ada, na_rpb),
        small_inputs(m_final_norm_g, m_norm_g, m_c_ctx, m_ret_norm_g, m_ret_decay_fwd, m_ret_decay_bwd, m_b_ada,
                     m_na_rpb),
        small_inputs(v_final_norm_g, v_norm_g, v_c_ctx, v_ret_norm_g, v_ret_decay_fwd, v_ret_decay_bwd, v_b_ada,
                     v_na_rpb), B)
    res = []
    for p, ada, win_o, wout_o in zip(smalls, adas, (g_win, d_win, nm_win, nv_win),
                                     (g_wout, d_wout, nm_wout, nv_wout)):
        gf, ng, cc, rng, df, db, bada, rpb = p
        res.append([cc.reshape(D), ng, ada[None], bada, win_o[None],
                    rpb.reshape(-1)[:na_rpb.size].reshape(na_rpb.shape), df, db, rng, wout_o[None], gf.reshape(D)])
    return (loss[0, 0], grad_x, *res[0], *res[1], *res[2], *res[3])
```

```python
import numpy as np
import jax
import jax.numpy as jnp
from jax import lax
from jax.experimental import pallas as pl
from jax.experimental.pallas import tpu as pltpu

F32 = jnp.float32
BF16 = jnp.bfloat16
HIGHEST = lax.Precision.HIGHEST

D = 1024
GRID_W = 64
NA_DH = 64
RET_DK = 128
ROPE_BASE = 10000.0
EPS = 1e-6
NEG = -1e30
TQ = 256
KW = 12 * GRID_W
N_SHARD = 4
N_DEV = 8
SM_ROWS = 24

ADAM_LR = 0.001
ADAM_B1 = 0.9
ADAM_B2 = 0.999
ADAM_EPS = 1e-08
ADAM_WD = 0.01
ADAM_STEP = 10

MESH = pl.DeviceIdType.MESH
ANY = pl.BlockSpec(memory_space=pl.ANY)


def _params(sem=None, vmem_mb=48):
    return pltpu.CompilerParams(dimension_semantics=sem, vmem_limit_bytes=vmem_mb << 20)


def _dot(a, b):
    return jnp.dot(a, b, preferred_element_type=F32)


def _dot_nt(a, b):
    return lax.dot_general(a, b, (((1,), (1,)), ((), ())), preferred_element_type=F32)


def _dot_tn(a, b):
    return lax.dot_general(a, b, (((0,), (0,)), ((), ())), preferred_element_type=F32)


def _sigmoid(x):
    return 1.0 / (1.0 + jnp.exp(-x))


def _rope_tables(L, LC):
    half = RET_DK // 2
    nf = half // 2
    t = np.arange(L)
    row = (t // GRID_W).astype(np.float32)
    col = (t % GRID_W).astype(np.float32)
    inv = (np.float32(ROPE_BASE) ** (-np.arange(nf, dtype=np.float32) / np.float32(nf))).astype(np.float32)
    ang = np.concatenate([row[:, None] * inv, col[:, None] * inv], axis=-1).astype(np.float32)
    cos, sin = np.cos(ang).astype(np.float32), np.sin(ang).astype(np.float32)
    cos2 = np.concatenate([cos, cos], axis=-1)
    sin2 = np.concatenate([-sin, sin], axis=-1)
    cos2 = np.concatenate([cos2, np.ones((LC, RET_DK), np.float32)], axis=0)
    sin2 = np.concatenate([sin2, np.zeros((LC, RET_DK), np.float32)], axis=0)
    return jnp.asarray(cos2), jnp.asarray(sin2)


def _mod_part_call(c_rows, wada_b, b_shard):
    def body(c_ref, w_ref, b_ref, o_ref):
        a = c_ref[...]
        o_ref[...] = _dot((a * _sigmoid(a)).astype(BF16), w_ref[...]) + b_ref[...]

    return pl.pallas_call(
        body, name="ada_mod", out_shape=jax.ShapeDtypeStruct((c_rows.shape[0], wada_b.shape[1]), F32),
        compiler_params=_params())(c_rows, wada_b, b_shard)


def _dc_masks():
    cq = lax.broadcasted_iota(jnp.int32, (GRID_W, GRID_W), 0)
    ck = lax.broadcasted_iota(jnp.int32, (GRID_W, GRID_W), 1)
    dc = jnp.clip(ck - cq + 15, 0, 30)
    c0 = jnp.clip(cq - 8, 0, GRID_W - 16)
    col_ok = (ck >= c0) & (ck < c0 + 16)
    return dc, col_ok


def _bias_blocks():
    out = []
    for typ, delta in enumerate((4, 0, -4)):
        for rq in range(4):
            for rkk in range(12):
                dr = rkk + delta - rq - 4
                if typ == 0:
                    ok = -rq <= dr <= 7 - rq
                elif typ == 1:
                    ok = -4 <= dr <= 3
                else:
                    ok = -4 - rq <= dr <= 3 - rq
                out.append((typ, rq, rkk, dr if ok else None))
    return out


def _bias_body(r_ref, bias_ref, et_ref, out_ref, sem):
    dc, col_ok = _dc_masks()
    masks = [(dc == j).astype(F32) for j in range(31)]
    nh = bias_ref.shape[0]

    def per_h(h, carry):
        for dr in range(15):
            t = jnp.zeros((GRID_W, GRID_W), F32)
            for j in range(31):
                t = t + masks[j] * r_ref[h, dr * 31 + j]
            et_ref[dr] = jnp.where(col_ok, t, NEG)
        neg = jnp.full((GRID_W, GRID_W), NEG, F32)
        for typ, rq, rkk, dr in _bias_blocks():
            blk = neg if dr is None else et_ref[dr + 7]
            bias_ref[h, typ, rq * 64:(rq + 1) * 64, rkk * 64:(rkk + 1) * 64] = blk
        pltpu.make_async_copy(bias_ref.at[h], out_ref.at[h], sem).start()
        return carry

    lax.fori_loop(0, nh, per_h, 0)
    return [pltpu.make_async_copy(bias_ref.at[h], out_ref.at[h], sem) for h in range(nh)]


def _bias_tile_sums(db_ref, hh):
    acc = {}
    for typ, rq, rkk, dr in _bias_blocks():
        if dr is None:
            continue
        blk = db_ref[hh, typ, rq * 64:(rq + 1) * 64, rkk * 64:(rkk + 1) * 64]
        acc[dr] = blk if dr not in acc else acc[dr] + blk
    return acc


def _small_reduce_body(dt_ref, dlg_ref, drpb_ref, dlgo_ref, p_ref):
    dc, _ = _dc_masks()
    masks = [(dc == j).astype(F32) for j in range(31)]
    ones = jnp.ones((8, GRID_W), F32)
    p_ref[...] = jnp.zeros_like(p_ref)
    drpb_ref[...] = jnp.zeros_like(drpb_ref)

    def per_h(h, carry):
        for dr in range(-7, 8):
            t = dt_ref[h, dr + 7]
            for j in range(31):
                p_ref[j:j + 1, :] = jnp.sum(t * masks[j], axis=0, keepdims=True)
            red = lax.dot_general(ones, p_ref[...], (((1,), (1,)), ((), ())),
                                  precision=HIGHEST, preferred_element_type=F32)
            drpb_ref[h, dr + 7:dr + 8, :] = red[0:1, :]
        return carry

    lax.fori_loop(0, dt_ref.shape[0], per_h, 0)
    x = dlg_ref[0]
    for b in range(1, dlg_ref.shape[0]):
        x = x + dlg_ref[b]
    x = x.reshape(4 * 8, x.shape[-1])
    dlgo_ref[...] = jnp.dot(x, jnp.ones((x.shape[-1], 128), F32), precision=HIGHEST,
                            preferred_element_type=F32)


def _inproj_gather_call(order, x, ctx, mod_part, norm_g, win_b, wout_b, cos2, sin2):
    B, L, _ = x.shape
    LC = ctx.shape[1]
    T = L + LC
    TI = 2 * TQ
    nl = L // TI
    nt = nl + 1
    assert LC == TQ and L % TI == 0
    kscale = RET_DK ** -0.5
    HR = D // 2
    pad_rows = nt * TI - T
    cos2 = jnp.pad(cos2, ((0, pad_rows), (0, 0)))
    sin2 = jnp.pad(sin2, ((0, pad_rows), (0, 0)))

    MW = mod_part.shape[1]
    NB = N_DEV * B

    def body(ord_ref, x_ref, ctx_ref, mp_ref, g_ref, wown_ref, woown_ref, cos_ref, sin_ref,
             p_ref, h_ref, wf_ref, wof_ref, modo_ref, w_all, wo_all, hs_ref, mp_all, mod_ref, ssem, rsem, lsem):
        j, b, t = pl.program_id(0), pl.program_id(1), pl.program_id(2)
        first = (b == 0) & (t == 0)
        mx, my, mc = _mesh_pos()
        s = 2 * mx + my

        m_send = [_remote(mp_ref, mp_all.at[s], ssem, rsem, 12 + k, (px, py, mc))
                  for k, (px, py) in enumerate(_other_chips(mx, my))]
        m_recv = [_remote(mp_ref, mp_all.at[2 * px + py], ssem, rsem, 12 + k, (px, py, mc))
                  for k, (px, py) in enumerate(_other_chips(mx, my))]

        sems = (ssem, rsem, lsem)
        own, ici_send, ici_recv, fwd_send, fwd_recv, outs = _gather_copies(wown_ref, w_all, wf_ref, HR, sems, 0, 0)
        oown, o_send, o_recv, o_fsend, o_frecv, o_outs = _gather_copies(
            woown_ref, wo_all, wof_ref, woown_ref.shape[0] // 2, sems, 6, 5)

        @pl.when(first & (j == 0))
        def _():
            for cp in m_send:
                cp.start()
            own.start()
            oown.start()
            mp_all[s] = mp_ref[...]
            own.wait()
            ici_send[0].start()
            ici_send[1].start()
            outs[0].start()
            oown.wait()
            for cp in m_recv:
                cp.wait_recv()
            me = 4 * mx + 2 * my + mc
            mod_ref[...] = jnp.zeros_like(mod_ref)
            for p in range(N_SHARD):
                for r in range(B):
                    mod_ref[r:r + 1, p * MW:(p + 1) * MW] = mp_all[p, pl.ds(B * me + r, 1), :]
                mod_ref[B:B + 1, p * MW:(p + 1) * MW] = mp_all[p, NB:NB + 1, :]
            modo_ref[...] = mod_ref[...]

        for k in range(3):
            @pl.when(first & (j == k + 1))
            def _(k=k):
                ici_recv[k].wait_recv()
                if k == 0:
                    ici_send[2].start()
                fwd_send[k].start()
                fwd_recv[k].wait_recv()
                outs[1 + k].start()
                if k == 1:
                    for cp in o_send:
                        cp.start()
                if k == 2:
                    for got, fwd in zip(o_recv, o_fsend):
                        got.wait_recv()
                        fwd.start()

        tile = b * nt + t

        @pl.when(j == 0)
        def _():
            is_lat = t < nl
            ctx_tile = jnp.concatenate([ctx_ref[...], jnp.zeros((TI - LC, D), F32)], axis=0)
            xt = jnp.where(is_lat, x_ref[...], ctx_tile)
            mrow = mod_ref[pl.ds(jnp.where(is_lat, b, B), 1), :]
            shift, scale = mrow[:, 0:D], mrow[:, D:2 * D]
            rstd = lax.rsqrt(jnp.mean(xt * xt, axis=-1, keepdims=True) + EPS)
            h0 = ((xt * rstd * g_ref[...]) * (1.0 + scale) + shift).astype(BF16)
            h_ref[...] = h0
            hs_ref[tile] = h0

        hb = hs_ref[tile]
        cs, sn = cos_ref[...], sin_ref[...]
        shard = ord_ref[j]
        for sh in range(N_SHARD):
            @pl.when(shard == sh)
            def _(sh=sh):
                for half in range(2):
                    sec = 2 * sh + half
                    acc = _dot(hb, w_all[sh, :, half * 512:(half + 1) * 512])
                    if sec == 0:
                        acc = acc * (NA_DH ** -0.5)
                    if sec in (4, 5):
                        for q in range(4):
                            a = acc[:, q * 128:(q + 1) * 128]
                            r = a * cs + pltpu.roll(a, 64, 1) * sn
                            if sec == 5:
                                r = r * kscale
                            p_ref[:, half * 512 + q * 128:half * 512 + (q + 1) * 128] = r.astype(BF16)
                    else:
                        p_ref[:, half * 512:(half + 1) * 512] = acc.astype(BF16)

        @pl.when((j == N_SHARD - 1) & (b == B - 1) & (t == nt - 1))
        def _():
            for cp in o_frecv:
                cp.wait_recv()
            for cp in o_outs:
                cp.start()
            _finish(outs + o_outs, ici_send + fwd_send + o_send + o_fsend + m_send, [])

    tok = lambda j, b, t, o: (jnp.where(j == 0, b, B - 1), jnp.where(j == 0, jnp.minimum(t, nl - 1), nl - 1), 0)
    grid_spec = pltpu.PrefetchScalarGridSpec(
        num_scalar_prefetch=1, grid=(N_SHARD, B, nt),
        in_specs=[
            pl.BlockSpec((None, TI, D), tok),
            pl.BlockSpec((None, LC, D), lambda j, b, t, o: (jnp.where(j == 0, b, B - 1), 0, 0)),
            pl.BlockSpec(mod_part.shape, lambda j, b, t, o: (0, 0)),
            pl.BlockSpec((1, D), lambda j, b, t, o: (0, 0)),
            ANY, ANY,
            pl.BlockSpec((TI, RET_DK), lambda j, b, t, o: (t, 0)),
            pl.BlockSpec((TI, RET_DK), lambda j, b, t, o: (t, 0)),
        ],
        out_specs=(pl.BlockSpec((None, TI, D), lambda j, b, t, o: (b, t, o[j])),
                   pl.BlockSpec((None, TI, D), lambda j, b, t, o: (
                       jnp.where(j == 0, b, B - 1), jnp.where(j == 0, t, nt - 1), 0)), ANY, ANY,
                   pl.BlockSpec((8, 3 * D), lambda j, b, t, o: (0, 0))),
        scratch_shapes=[pltpu.VMEM((N_SHARD, D, D), BF16), pltpu.VMEM((N_SHARD,) + wout_b.shape, BF16),
                        pltpu.VMEM((B * nt, TI, D), BF16),
                        pltpu.VMEM((N_SHARD,) + mod_part.shape, F32), pltpu.VMEM((8, 3 * D), F32),
                        pltpu.SemaphoreType.DMA((15,)), pltpu.SemaphoreType.DMA((15,)),
                        pltpu.SemaphoreType.DMA((10,))])
    return pl.pallas_call(
        body, name="in_proj", grid_spec=grid_spec,
        out_shape=(jax.ShapeDtypeStruct((B, T, 4 * D), BF16), jax.ShapeDtypeStruct((B, T, D), BF16),
                   jax.ShapeDtypeStruct((N_SHARD, D, D), BF16),
                   jax.ShapeDtypeStruct((N_SHARD,) + wout_b.shape, BF16),
                   jax.ShapeDtypeStruct((8, 3 * D), F32)),
        compiler_params=_params(("arbitrary",) * 3, vmem_mb=56))(
            order, x, ctx, mod_part, norm_g, win_b, wout_b, cos2, sin2)


def _na_specs(L, T, rows, nh=2):
    nm = rows // 4
    w = nh * NA_DH
    per = 512 // w
    q_spec = pl.BlockSpec((None, TQ, w), lambda hp, b, m: (b, m, hp))
    k_spec = pl.BlockSpec((None, T, w), lambda hp, b, m: (b, 0, per + hp))
    v_spec = pl.BlockSpec((None, T, w), lambda hp, b, m: (b, 0, 2 * per + hp))
    g_spec = pl.BlockSpec((None, TQ, w), lambda hp, b, m: (b, m, 3 * per + hp))
    bias_spec = pl.BlockSpec((nh, 3, TQ, KW), lambda hp, b, m: (hp, 0, 0, 0))
    return nm, q_spec, k_spec, v_spec, g_spec, bias_spec


def _na_tile(m, nm, rows):
    typ = jnp.where(m == 0, 0, jnp.where(m == nm - 1, 2, 1))
    start = pl.multiple_of(jnp.clip(4 * m - 4, 0, rows - 12) * GRID_W, TQ)
    return typ, start


def _na_fwd_call(P, bias, L, LC):
    B, T, _ = P.shape
    rows = L // GRID_W
    NH = 4
    nm, q_spec, k_spec, v_spec, g_spec, bias_spec = _na_specs(L, T, rows, NH)

    def body(q_ref, k_ref, v_ref, g_ref, bias_ref, y_ref, o_ref):
        typ, start = _na_tile(pl.program_id(2), nm, rows)
        for hh in range(NH):
            ln = slice(hh * NA_DH, (hh + 1) * NA_DH)
            q = q_ref[:, ln]
            kw, vw = k_ref[pl.ds(start, KW), ln], v_ref[pl.ds(start, KW), ln]
            kc, vc = k_ref[L:L + LC, ln], v_ref[L:L + LC, ln]
            s1 = _dot_nt(q, kw) + bias_ref[hh, typ]
            s2 = _dot_nt(q, kc)
            mx = jnp.maximum(jnp.max(s1, axis=-1, keepdims=True), jnp.max(s2, axis=-1, keepdims=True))
            p1, p2 = jnp.exp(s1 - mx), jnp.exp(s2 - mx)
            inv = 1.0 / (jnp.sum(p1, axis=-1, keepdims=True) + jnp.sum(p2, axis=-1, keepdims=True))
            o = (_dot(p1.astype(BF16), vw) + _dot(p2.astype(BF16), vc)) * inv
            g = g_ref[:, ln].astype(F32)
            o_ref[:, ln] = o.astype(BF16)
            y_ref[:, ln] = (o * (g * _sigmoid(g))).astype(BF16)

    tile = pl.BlockSpec((None, TQ, NH * NA_DH), lambda hp, b, m: (b, m, hp))
    return pl.pallas_call(
        body, name="na_fwd", grid=(8 // NH, B, nm),
        in_specs=[q_spec, k_spec, v_spec, g_spec, bias_spec],
        out_specs=(tile, tile),
        out_shape=(jax.ShapeDtypeStruct((B, L, 512), BF16),) * 2,
        compiler_params=_params(("arbitrary",) * 3))(P, P, P, P, bias)


def _na_bwd_call(P, bias, dY, o_na, L, LC):
    B, T, _ = P.shape
    rows = L // GRID_W
    NH = 4
    W = NH * NA_DH
    nm, q_spec, k_spec, v_spec, g_spec, bias_spec = _na_specs(L, T, rows, NH)
    scale = NA_DH ** -0.5

    RB = 32

    def body(q_ref, k_ref, v_ref, g_ref, bias_ref, dy_ref, o_ref, dq_ref, dg_ref, dk_ref, dv_ref, dt_ref,
             db_ref, s1_ref, s2_ref, dp1_ref, dp2_ref, p1_ref, p2_ref, ds1_ref, ds2_ref, dkt_ref, dvt_ref):
        b, m = pl.program_id(1), pl.program_id(2)
        typ, start = _na_tile(m, nm, rows)

        @pl.when(m == 0)
        def _():
            dkt_ref[...] = jnp.zeros_like(dkt_ref)
            dvt_ref[...] = jnp.zeros_like(dvt_ref)

        @pl.when((m == 0) & (b == 0))
        def _():
            db_ref[...] = jnp.zeros_like(db_ref)

        for hh in range(NH):
            ln = slice(hh * NA_DH, (hh + 1) * NA_DH)
            q = q_ref[:, ln]
            kw, vw = k_ref[pl.ds(start, KW), ln], v_ref[pl.ds(start, KW), ln]
            kc, vc = k_ref[L:L + LC, ln], v_ref[L:L + LC, ln]
            g = g_ref[:, ln].astype(F32)
            sg = _sigmoid(g)
            dy = dy_ref[:, ln].astype(F32)
            do = (dy * (g * sg)).astype(BF16)
            s1_ref[hh] = _dot_nt(q, kw)
            s2_ref[hh] = _dot_nt(q, kc)
            dp1_ref[hh] = _dot_nt(do, vw)
            dp2_ref[hh] = _dot_nt(do, vc)

            def rows_pass(r, carry, hh=hh):
                rw = pl.ds(pl.multiple_of(r * RB, RB), RB)
                a = s1_ref[hh, rw, :] + bias_ref[hh, typ, rw, :]
                c = s2_ref[hh, rw, :]
                mx = jnp.maximum(jnp.max(a, axis=-1, keepdims=True), jnp.max(c, axis=-1, keepdims=True))
                e1, e2 = jnp.exp(a - mx), jnp.exp(c - mx)
                inv = 1.0 / (jnp.sum(e1, axis=-1, keepdims=True) + jnp.sum(e2, axis=-1, keepdims=True))
                p1, p2 = e1 * inv, e2 * inv
                p1_ref[hh, rw, :] = p1.astype(BF16)
                p2_ref[hh, rw, :] = p2.astype(BF16)
                dp1, dp2 = dp1_ref[hh, rw, :], dp2_ref[hh, rw, :]
                delta = jnp.sum(p1 * dp1, axis=-1, keepdims=True) + jnp.sum(p2 * dp2, axis=-1, keepdims=True)
                ds1 = p1 * (dp1 - delta)
                db_ref[hh, typ, rw, :] += ds1
                ds1_ref[hh, rw, :] = ds1.astype(BF16)
                ds2_ref[hh, rw, :] = (p2 * (dp2 - delta)).astype(BF16)
                return carry

            lax.fori_loop(0, TQ // RB, rows_pass, 0, unroll=True)
            p1b, p2b, ds1b, ds2b = p1_ref[hh], p2_ref[hh], ds1_ref[hh], ds2_ref[hh]
            dg_ref[:, ln] = (dy * o_ref[:, ln].astype(F32) * (sg * (1.0 + g * (1.0 - sg)))).astype(BF16)
            dq_ref[:, ln] = ((_dot(ds1b, kw) + _dot(ds2b, kc)) * scale).astype(BF16)
            dkt_ref[ln, pl.ds(start, KW)] += _dot_tn(q, ds1b)
            dvt_ref[ln, pl.ds(start, KW)] += _dot_tn(do, p1b)
            dkt_ref[ln, L:L + LC] += _dot_tn(q, ds2b)
            dvt_ref[ln, L:L + LC] += _dot_tn(do, p2b)

        @pl.when(m == nm - 1)
        def _():
            dk_ref[...] = dkt_ref[...].T
            dv_ref[...] = dvt_ref[...].T

        @pl.when((m == nm - 1) & (b == B - 1))
        def _():
            for hh in range(NH):
                for dr, t in _bias_tile_sums(db_ref, hh).items():
                    dt_ref[hh, dr + 7] = t

    tile = pl.BlockSpec((None, TQ, W), lambda hp, b, m: (b, m, hp))
    kv_out = pl.BlockSpec((None, T, W), lambda hp, b, m: (b, 0, hp))
    wide, narrow = (NH, TQ, KW), (NH, TQ, LC)
    return pl.pallas_call(
        body, name="na_bwd", grid=(8 // NH, B, nm),
        in_specs=[q_spec, k_spec, v_spec, g_spec, bias_spec, tile, tile],
        out_specs=(tile, tile, kv_out, kv_out,
                   pl.BlockSpec((NH, 15, GRID_W, GRID_W), lambda hp, b, m: (hp, 0, 0, 0))),
        out_shape=(jax.ShapeDtypeStruct((B, L, 512), BF16), jax.ShapeDtypeStruct((B, L, 512), BF16),
                   jax.ShapeDtypeStruct((B, T, 512), F32), jax.ShapeDtypeStruct((B, T, 512), F32),
                   jax.ShapeDtypeStruct((bias.shape[0], 15, GRID_W, GRID_W), F32)),
        scratch_shapes=[pltpu.VMEM((NH,) + bias.shape[1:], F32),
                        pltpu.VMEM(wide, F32), pltpu.VMEM(narrow, F32), pltpu.VMEM(wide, F32), pltpu.VMEM(narrow, F32),
                        pltpu.VMEM(wide, BF16), pltpu.VMEM(narrow, BF16), pltpu.VMEM(wide, BF16),
                        pltpu.VMEM(narrow, BF16), pltpu.VMEM((W, T), F32), pltpu.VMEM((W, T), F32)],
        compiler_params=_params(("arbitrary",) * 3, vmem_mb=60))(P, P, P, P, bias, dY, o_na)


def _head_scalar(dec_ref, h):
    lane = lax.broadcasted_iota(jnp.int32, dec_ref.shape, 1)
    return -jnp.sum(jnp.where(lane == h, jnp.exp(dec_ref[...]), 0.0), axis=1, keepdims=True)


def _chunk_decay(lgf, lgb):
    tau = lax.broadcasted_iota(jnp.int32, (TQ, 1), 0).astype(F32)
    sig = lax.broadcasted_iota(jnp.int32, (1, TQ), 1).astype(F32)
    dist = tau - sig
    dm = jnp.exp(dist * jnp.where(dist > 0, lgf, -lgb)) * jnp.where(dist == 0, 2.0, 1.0)
    return tau, dist, dm


def _ret_states_call(P, dec_f, dec_b, L, LC):
    B, T, _ = P.shape
    n = L // TQ

    def body(df_ref, db_ref, k_ref, v_ref, sf_ref, sb_ref):
        h = pl.program_id(1)
        lgf, lgb = _head_scalar(df_ref, h), _head_scalar(db_ref, h)
        tau = lax.broadcasted_iota(jnp.int32, (TQ, 1), 0).astype(F32)
        jc = lax.broadcasted_iota(jnp.int32, (LC, 1), 0).astype(F32)
        wf, wb = jnp.exp(lgf * (TQ - 1.0 - tau)), jnp.exp(lgb * tau)
        gcf, gcb = jnp.exp(lgf * float(TQ)), jnp.exp(lgb * float(TQ))
        kc, vc = k_ref[L:L + LC, :].astype(F32), v_ref[L:L + LC, :]

        def chunk_state(i, w):
            ks = pl.multiple_of(i * TQ, TQ)
            return _dot_tn((k_ref[pl.ds(ks, TQ), :].astype(F32) * w).astype(BF16), v_ref[pl.ds(ks, TQ), :])

        def fwd(i, s):
            sf_ref[i] = s
            return gcf * s + chunk_state(i, wf)

        lax.fori_loop(0, n, fwd, _dot_tn((kc * jnp.exp(lgf * (LC - 1.0 - jc))).astype(BF16), vc), unroll=True)

        def bwd(r, s):
            i = n - 1 - r
            sb_ref[i] = s
            return gcb * s + chunk_state(i, wb)

        lax.fori_loop(0, n, bwd, _dot_tn((kc * jnp.exp(lgb * jc)).astype(BF16), vc), unroll=True)

    st = pl.BlockSpec((None, None, n, RET_DK, RET_DK), lambda b, h: (b, h, 0, 0, 0))
    return pl.pallas_call(
        body, name="ret_states", grid=(B, 4),
        in_specs=[pl.BlockSpec((1, 4), lambda b, h: (0, 0)), pl.BlockSpec((1, 4), lambda b, h: (0, 0)),
                  pl.BlockSpec((None, T, 128), lambda b, h: (b, 0, 20 + h)),
                  pl.BlockSpec((None, T, 128), lambda b, h: (b, 0, 24 + h))],
        out_specs=(st, st),
        out_shape=(jax.ShapeDtypeStruct((B, 4, n, RET_DK, RET_DK), F32),) * 2,
        compiler_params=_params(("arbitrary",) * 2))(dec_f, dec_b, P, P)


def _retc_fwd_call(P, sf, sb, dec_f, dec_b, ret_norm_g, L):
    B, T, _ = P.shape
    sec = lambda k: pl.BlockSpec((None, TQ, 512), lambda b, i: (b, i, k))
    dec_spec = pl.BlockSpec((1, 4), lambda b, i: (0, 0))
    st_spec = pl.BlockSpec((None, 4, None, RET_DK, RET_DK), lambda b, i: (b, 0, i, 0, 0))

    def body(df_ref, db_ref, q_ref, k_ref, v_ref, g_ref, gn_ref, sf_ref, sb_ref, y_ref, o_ref):
        for h in range(4):
            ln = slice(h * RET_DK, (h + 1) * RET_DK)
            lgf, lgb = _head_scalar(df_ref, h), _head_scalar(db_ref, h)
            tau, _, dm = _chunk_decay(lgf, lgb)
            q = q_ref[:, ln]
            qf = q.astype(F32)
            acc = _dot((_dot_nt(q, k_ref[:, ln]) * dm).astype(BF16), v_ref[:, ln])
            acc = acc + _dot((qf * jnp.exp(lgf * (tau + 1.0))).astype(BF16), sf_ref[h].astype(BF16))
            acc = acc + _dot((qf * jnp.exp(lgb * (TQ - tau))).astype(BF16), sb_ref[h].astype(BF16))
            o_ref[:, ln] = acc
            rn = lax.rsqrt(jnp.mean(acc * acc, axis=-1, keepdims=True) + EPS)
            g = g_ref[:, ln].astype(F32)
            y_ref[:, ln] = ((acc * rn * gn_ref[:, ln]) * (g * _sigmoid(g))).astype(BF16)

    tile = pl.BlockSpec((None, TQ, 512), lambda b, i: (b, i, 0))
    return pl.pallas_call(
        body, name="ret_fwd", grid=(B, L // TQ),
        in_specs=[dec_spec, dec_spec, sec(4), sec(5), sec(6), sec(7),
                  pl.BlockSpec((1, 512), lambda b, i: (0, 0)), st_spec, st_spec],
        out_specs=(tile, tile),
        out_shape=(jax.ShapeDtypeStruct((B, L, 512), BF16), jax.ShapeDtypeStruct((B, L, 512), F32)),
        compiler_params=_params(("arbitrary",) * 2))(dec_f, dec_b, P, P, P, P, ret_norm_g, sf, sb)


def _retc_bwd_call(P, sf, sb, dec_f, dec_b, ret_norm_g, o_ret, dY, cos2, sin2, L, LC):
    B, T, _ = P.shape
    n = L // TQ
    C = float(TQ)
    kscale = RET_DK ** -0.5
    st_spec = pl.BlockSpec((None, 4, n, RET_DK, RET_DK), lambda b, i: (b, 0, 0, 0, 0))

    def body(df_ref, db_ref, q_ref, k_ref, v_ref, g_ref, gn_ref, o_ref, dy_ref, cos_ref, sin_ref, sf_ref, sb_ref,
             dq_ref, dg_ref, dk_ref, dv_ref, dgn_ref, dlg_ref, dsf_ref, dsb_ref):
        i = pl.program_id(1)

        @pl.when(i == 0)
        def _():
            dk_ref[...] = jnp.zeros_like(dk_ref)
            dv_ref[...] = jnp.zeros_like(dv_ref)
            dgn_ref[...] = jnp.zeros_like(dgn_ref)
            dlg_ref[...] = jnp.zeros_like(dlg_ref)

        rows = pl.ds(pl.multiple_of(i * TQ, TQ), TQ)
        cs, sn = cos_ref[rows, :], sin_ref[rows, :]

        def one_head(h):
            ln = slice(h * RET_DK, (h + 1) * RET_DK)
            lgf, lgb = _head_scalar(df_ref, h), _head_scalar(db_ref, h)
            tau, dist, dm = _chunk_decay(lgf, lgb)

            def add_lg(row, x):
                csum = jnp.sum(x, axis=0, keepdims=True)
                tot = csum[:, 0:128]
                for part in range(1, x.shape[1] // 128):
                    tot = tot + csum[:, part * 128:(part + 1) * 128]
                dlg_ref[h, row:row + 1, :] += tot

            q = q_ref[:, ln]
            qf = q.astype(F32)
            o = o_ref[:, ln]
            g = g_ref[:, ln].astype(F32)
            dy = dy_ref[:, ln].astype(F32)
            gn = gn_ref[:, ln]
            sg = _sigmoid(g)
            rn = lax.rsqrt(jnp.mean(o * o, axis=-1, keepdims=True) + EPS)
            nrm = o * rn
            dg_ref[:, ln] = (dy * (nrm * gn) * (sg * (1.0 + g * (1.0 - sg)))).astype(BF16)
            dhn = dy * (g * sg)
            dgn_ref[:, ln] += jnp.sum(dhn * nrm, axis=0, keepdims=True)
            dnrm = dhn * gn
            do = rn * (dnrm - nrm * jnp.mean(dnrm * nrm, axis=-1, keepdims=True))
            dob = do.astype(BF16)
            ki, vi = k_ref[rows, ln], v_ref[rows, ln]
            s = _dot_nt(q, ki)
            dsv = _dot_nt(dob, vi)
            dsb = (dsv * dm).astype(BF16)
            dk_ref[rows, ln] += _dot_tn(dsb, q)
            dv_ref[rows, ln] += _dot_tn((s * dm).astype(BF16), dob)
            xw = s * dsv * dm * jnp.abs(dist)
            fpart = jnp.where(dist > 0, xw, 0.0)
            add_lg(0, fpart)
            add_lg(1, xw - fpart)
            dq = _dot(dsb, ki)
            af, ab = jnp.exp(lgf * (tau + 1.0)), jnp.exp(lgb * (C - tau))
            qa, qb = (qf * af).astype(BF16), (qf * ab).astype(BF16)
            sfi, sbi = sf_ref[h, i].astype(BF16), sb_ref[h, i].astype(BF16)
            dq = dq + af * _dot_nt(dob, sfi) + ab * _dot_nt(dob, sbi)
            dsf_ref[h, i] = _dot_tn(qa, dob)
            dsb_ref[h, i] = _dot_tn(qb, dob)
            add_lg(0, (tau + 1.0) * (_dot(qa, sfi) * do))
            add_lg(1, (C - tau) * (_dot(qb, sbi) * do))
            dq_ref[:, ln] = (dq * cs - pltpu.roll(dq, 64, 1) * sn).astype(BF16)

            @pl.when(i == n - 1)
            def _():
                jc = lax.broadcasted_iota(jnp.int32, (LC, 1), 0).astype(F32)
                crow = pl.ds(L, LC)

                def through_state(rws, w, dw, gst, row):
                    kk, vv = k_ref[rws, ln].astype(F32), v_ref[rws, ln]
                    gb = gst.astype(BF16)
                    vg = _dot_nt(vv, gb)
                    kw = kk * w
                    dk_ref[rws, ln] += w * vg
                    dv_ref[rws, ln] += _dot(kw.astype(BF16), gb)
                    add_lg(row, dw * (kw * vg))

                def scan(gc, w, dw, st_ref, dst_ref, order, row):
                    def step(r, gst):
                        j = order(r)
                        through_state(pl.ds(pl.multiple_of(j * TQ, TQ), TQ), w, dw, gst, row)
                        add_lg(row, (C * gc) * (gst * st_ref[h, j]))
                        return dst_ref[h, j] + gc * gst
                    return lax.fori_loop(0, n, step, jnp.zeros((RET_DK, RET_DK), F32), unroll=True)

                gcf, gcb = jnp.exp(lgf * C), jnp.exp(lgb * C)
                g0 = scan(gcf, jnp.exp(lgf * (C - 1.0 - tau)), C - 1.0 - tau, sf_ref, dsf_ref,
                          lambda r: n - 1 - r, 0)
                through_state(crow, jnp.exp(lgf * (LC - 1.0 - jc)), LC - 1.0 - jc, g0, 0)
                g1 = scan(gcb, jnp.exp(lgb * tau), tau, sb_ref, dsb_ref, lambda r: r, 1)
                through_state(crow, jnp.exp(lgb * jc), jc, g1, 1)
                dk = dk_ref[:, ln]
                dk_ref[:, ln] = (dk * cos_ref[...] - pltpu.roll(dk, 64, 1) * sin_ref[...]) * kscale

        for h in range(4):
            one_head(h)

    sec = lambda k: pl.BlockSpec((None, TQ, 512), lambda b, i: (b, i, k))
    full = lambda k: pl.BlockSpec((None, T, 512), lambda b, i: (b, 0, k))
    dec_spec = pl.BlockSpec((1, 4), lambda b, i: (0, 0))
    tab = pl.BlockSpec((T, RET_DK), lambda b, i: (0, 0))
    return pl.pallas_call(
        body, name="ret_bwd", grid=(B, n),
        in_specs=[dec_spec, dec_spec, sec(4), full(5), full(6), sec(7),
                  pl.BlockSpec((1, 512), lambda b, i: (0, 0)), sec(0), sec(1), tab, tab, st_spec, st_spec],
        out_specs=(sec(0), sec(0), full(0), full(0),
                   pl.BlockSpec((None, 1, 512), lambda b, i: (b, 0, 0)),
                   pl.BlockSpec((None, 4, 8, 128), lambda b, i: (b, 0, 0, 0))),
        out_shape=(jax.ShapeDtypeStruct((B, L, 512), BF16), jax.ShapeDtypeStruct((B, L, 512), BF16),
                   jax.ShapeDtypeStruct((B, T, 512), F32), jax.ShapeDtypeStruct((B, T, 512), F32),
                   jax.ShapeDtypeStruct((B, 1, 512), F32), jax.ShapeDtypeStruct((B, 4, 8, 128), F32)),
        scratch_shapes=[pltpu.VMEM((4, n, RET_DK, RET_DK), F32), pltpu.VMEM((4, n, RET_DK, RET_DK), F32)],
        compiler_params=_params(("arbitrary",) * 2, vmem_mb=56))(
            dec_f, dec_b, P, P, P, P, ret_norm_g, o_ret, dY, cos2, sin2, sf, sb)


def _out_call(y_na, y_ret, x, target, mod, final_g, wout_f):
    B, L, _ = x.shape
    TO = 2 * TQ

    def body(yn_ref, yr_ref, x_ref, t_ref, mod_ref, gf_ref, w_ref, dy_ref, dx2_ref, dwb_ref, sm_ref, dw_ref):
        b, i = pl.program_id(0), pl.program_id(1)

        @pl.when((b == 0) & (i == 0))
        def _():
            dw_ref[...] = jnp.zeros_like(dw_ref)
            sm_ref[...] = jnp.zeros_like(sm_ref)

        gate = mod_ref[pl.ds(b, 1), 2 * D:3 * D]
        gf = gf_ref[...]
        yn, yr = yn_ref[...], yr_ref[...]
        ylat = _dot(yn, w_ref[0:512, :]) + _dot(yr, w_ref[512:1024, :])
        x2 = x_ref[...] + gate * ylat
        r = lax.rsqrt(jnp.mean(x2 * x2, axis=-1, keepdims=True) + EPS)
        xr = x2 * r
        err = xr * gf - t_ref[...]
        sm_ref[1:2, :] += jnp.sum(err * err, axis=0, keepdims=True)
        dout = err * (1.0 / D)
        sm_ref[0:1, :] += jnp.sum(dout * xr, axis=0, keepdims=True)
        gd = dout * gf
        dx2 = r * (gd - xr * jnp.mean(gd * xr, axis=-1, keepdims=True))
        dx2_ref[...] = dx2
        sm_ref[pl.ds(2 + b, 1), :] += jnp.sum(dx2 * ylat, axis=0, keepdims=True)
        dyl = (gate * dx2).astype(BF16)
        dy_ref[:, 0:512] = _dot_nt(dyl, w_ref[0:512, :]).astype(BF16)
        dy_ref[:, 512:1024] = _dot_nt(dyl, w_ref[512:1024, :]).astype(BF16)
        dw_ref[0:512, :] += _dot_tn(yn, dyl)
        dw_ref[512:1024, :] += _dot_tn(yr, dyl)

        @pl.when((b == B - 1) & (i == L // TO - 1))
        def _():
            dwb_ref[...] = dw_ref[...].astype(BF16)

    half = pl.BlockSpec((None, TO, 512), lambda b, i: (b, i, 0))
    full = pl.BlockSpec((None, TO, D), lambda b, i: (b, i, 0))
    return pl.pallas_call(
        body, name="out_proj_loss", grid=(B, L // TO),
        in_specs=[half, half, full, full,
                  pl.BlockSpec((8, 3 * D), lambda b, i: (0, 0)),
                  pl.BlockSpec((1, D), lambda b, i: (0, 0)),
                  pl.BlockSpec((D, D), lambda b, i: (0, 0))],
        out_specs=(full, full, pl.BlockSpec((D, D), lambda b, i: (0, 0)),
                   pl.BlockSpec((8, D), lambda b, i: (0, 0))),
        out_shape=(jax.ShapeDtypeStruct((B, L, D), BF16), jax.ShapeDtypeStruct((B, L, D), F32),
                   jax.ShapeDtypeStruct((D, D), BF16), jax.ShapeDtypeStruct((8, D), F32)),
        scratch_shapes=[pltpu.VMEM((D, D), F32)],
        compiler_params=_params(("arbitrary",) * 2))(y_na, y_ret, x, target, mod, final_g, wout_f)


def _dh_call(dsec, win_f, x, ctx, dx2, mod, norm_g, cp_in, cp_out):
    B, L, _ = x.shape
    LC = ctx.shape[1]
    nl = L // TQ

    def body(d0, d1, d2, d3, d4, d5, d6, d7, w_ref, x_ref, ctx_ref, dx2_ref, mod_ref, g_ref, cpi_ref, cpo_ref,
             gx_ref, sm_ref, sli_ref, slo_ref, ssem, rsem, lsem):
        drefs = (d0, d1, d2, d3, d4, d5, d6, d7)
        b, t = pl.program_id(0), pl.program_id(1)
        is_lat = t < nl

        @pl.when((b == 0) & (t == 0))
        def _():
            sm_ref[...] = jnp.zeros_like(sm_ref)

        def dh_of(secs):
            acc = jnp.zeros((TQ, D), F32)
            for sec in secs:
                s, half = divmod(sec, 2)
                acc = acc + _dot_nt(drefs[sec][...].astype(BF16), w_ref[s, :, half * 512:(half + 1) * 512])
            return acc

        def norm_bwd(dh, xt, mrow):
            scale = mrow[:, D:2 * D]
            g = g_ref[...]
            rstd = lax.rsqrt(jnp.mean(xt * xt, axis=-1, keepdims=True) + EPS)
            xn = xt * rstd
            dshift = jnp.sum(dh, axis=0, keepdims=True)
            dscale = jnp.sum(dh * (xn * g), axis=0, keepdims=True)
            dhn = dh * (1.0 + scale)
            sm_ref[0:1, :] += jnp.sum(dhn * xn, axis=0, keepdims=True)
            dxn = dhn * g
            dx = rstd * (dxn - xn * jnp.mean(dxn * xn, axis=-1, keepdims=True))
            return dshift, dscale, dx

        @pl.when(is_lat)
        def _():
            dshift, dscale, dx = norm_bwd(dh_of(range(8)), x_ref[...], mod_ref[pl.ds(b, 1), :])
            sm_ref[pl.ds(3 + b, 1), :] += dshift
            sm_ref[pl.ds(3 + B + b, 1), :] += dscale
            gx_ref[...] = dx2_ref[...] + dx

        @pl.when(jnp.logical_not(is_lat))
        def _():
            dshift, dscale, _ = norm_bwd(dh_of((1, 2, 5, 6)), ctx_ref[...], mod_ref[B:B + 1, :])
            sm_ref[1:2, :] += dshift
            sm_ref[2:3, :] += dscale

        mx, my, mc = _mesh_pos()
        s = 2 * mx + my
        cps, sls = (cpi_ref, cpo_ref), (sli_ref, slo_ref)
        own = [pltpu.make_async_copy(cps[a].at[s], sls[a].at[s], lsem.at[a]) for a in range(2)]
        sends, recvs, k = [], [], 0
        for px, py in _other_chips(mx, my):
            ps = 2 * px + py
            for a in range(2):
                sends.append(_remote(cps[a].at[ps], sls[a].at[s], ssem, rsem, k, (px, py, mc)))
                recvs.append(_remote(cps[a].at[s], sls[a].at[ps], ssem, rsem, k, (px, py, mc)))
                k += 1

        @pl.when((b == 0) & (t == 0))
        def _():
            for cp in own + sends:
                cp.start()

        @pl.when((b == B - 1) & (t == nl))
        def _():
            _finish(own, sends, recvs)

    lat = lambda b, t: (b, jnp.minimum(t, nl - 1), 0)
    tok = lambda b, t: (b, t, 0)
    sec_specs = [pl.BlockSpec((None, TQ, 512), lat if sec in (0, 3, 4, 7) else tok) for sec in range(8)]
    return pl.pallas_call(
        body, name="dh_norm_bwd", grid=(B, nl + 1),
        in_specs=sec_specs + [
            pl.BlockSpec((N_SHARD, D, D), lambda b, t: (0, 0, 0)),
            pl.BlockSpec((None, TQ, D), lat),
            pl.BlockSpec((None, LC, D), lambda b, t: (b, 0, 0)),
            pl.BlockSpec((None, TQ, D), lat),
            pl.BlockSpec((8, 3 * D), lambda b, t: (0, 0)),
            pl.BlockSpec((1, D), lambda b, t: (0, 0)), ANY, ANY],
        out_specs=(pl.BlockSpec((None, TQ, D), lat), pl.BlockSpec((8, D), lambda b, t: (0, 0)), ANY, ANY),
        out_shape=(jax.ShapeDtypeStruct((B, L, D), F32), jax.ShapeDtypeStruct((8, D), F32),
                   jax.ShapeDtypeStruct(cp_in.shape, cp_in.dtype), jax.ShapeDtypeStruct(cp_out.shape, cp_out.dtype)),
        scratch_shapes=[pltpu.SemaphoreType.DMA((6,)), pltpu.SemaphoreType.DMA((6,)),
                        pltpu.SemaphoreType.DMA((2,))],
        compiler_params=_params(("arbitrary",) * 2))(*dsec, win_f, x, ctx, dx2, mod, norm_g, cp_in, cp_out)


def _dw_call(dsec, h, L):
    B, T, _ = h.shape
    TW = 2 * TQ
    nl = L // TW
    KV = (1, 2, 5, 6)

    def body(d0, d1, d2, d3, d4, d5, d6, d7, c1, c2, c5, c6, h_ref, hc_ref, dw_ref, acc_ref):
        drefs = (d0, d1, d2, d3, d4, d5, d6, d7)
        crefs = dict(zip(KV, (c1, c2, c5, c6)))
        b, t = pl.program_id(0), pl.program_id(1)

        @pl.when((b == 0) & (t == 0))
        def _():
            acc_ref[...] = jnp.zeros_like(acc_ref)

        def add(hb, refs, secs):
            for sec in secs:
                s, half = divmod(sec, 2)
                acc_ref[s, :, half * 512:(half + 1) * 512] += _dot_tn(hb, refs[sec][...].astype(BF16))

        @pl.when(t < nl)
        def _():
            add(h_ref[...], drefs, range(8))

        @pl.when(t == nl)
        def _():
            add(hc_ref[...], crefs, KV)

        @pl.when((b == B - 1) & (t == nl))
        def _():
            dw_ref[...] = acc_ref[...].astype(BF16)

    lat = lambda b, t: (b, jnp.minimum(t, nl - 1), 0)
    ctx = lambda b, t: (b, L // TQ, 0)
    return pl.pallas_call(
        body, name="dw_in", grid=(B, nl + 1),
        in_specs=[pl.BlockSpec((None, TW, 512), lat)] * 8 + [pl.BlockSpec((None, TQ, 512), ctx)] * 4
        + [pl.BlockSpec((None, TW, D), lat), pl.BlockSpec((None, TQ, D), ctx)],
        out_specs=pl.BlockSpec((N_SHARD, D, D), lambda b, t: (0, 0, 0)),
        out_shape=jax.ShapeDtypeStruct((N_SHARD, D, D), BF16),
        scratch_shapes=[pltpu.VMEM((N_SHARD, D, D), F32)],
        compiler_params=_params(("arbitrary",) * 2, vmem_mb=60))(*dsec, *[dsec[k] for k in KV], h, h)


def _mesh_pos():
    return lax.axis_index("x"), lax.axis_index("y"), lax.axis_index("c")


def _flip(v, f):
    return 1 - v if f else v


def _remote(src, dst, ssem, rsem, k, peer):
    return pltpu.make_async_remote_copy(src_ref=src, dst_ref=dst, send_sem=ssem.at[k], recv_sem=rsem.at[k],
                                        device_id=peer, device_id_type=MESH)


def _other_chips(x, y):
    return [(_flip(x, fx), _flip(y, fy)) for fx, fy in ((1, 0), (0, 1), (1, 1))]


def _gather_copies(own_ref, all_ref, out_ref, hr, sems, k0, l0):
    ssem, rsem, lsem = sems
    mx, my, mc = _mesh_pos()
    s = 2 * mx + my
    sib = (mx, my, 1 - mc)
    own = pltpu.make_async_copy(own_ref, all_ref.at[s], lsem.at[l0])
    send, recv, fsend, frecv = [], [], [], []
    outs = [pltpu.make_async_copy(all_ref.at[s], out_ref.at[s], lsem.at[l0 + 1])]
    for k, (px, py) in enumerate(_other_chips(mx, my)):
        ps = 2 * px + py
        mine = all_ref.at[s, pl.ds(mc * hr, hr)]
        send.append(_remote(mine, mine, ssem, rsem, k0 + k, (px, py, mc)))
        got = all_ref.at[ps, pl.ds(mc * hr, hr)]
        recv.append(_remote(mine, got, ssem, rsem, k0 + k, (px, py, mc)))
        fsend.append(_remote(got, got, ssem, rsem, k0 + 3 + k, sib))
        theirs = all_ref.at[ps, pl.ds((1 - mc) * hr, hr)]
        frecv.append(_remote(theirs, theirs, ssem, rsem, k0 + 3 + k, sib))
        outs.append(pltpu.make_async_copy(all_ref.at[ps], out_ref.at[ps], lsem.at[l0 + 2 + k]))
    return own, send, recv, fsend, frecv, outs


def _all_to_all_small(src, dst_all, ssem, rsem, k0, x, y, cc):
    me = 4 * x + 2 * y + cc
    sends, recvs = [], []
    for f in range(1, N_DEV):
        px, py, pc = _flip(x, f & 4), _flip(y, f & 2), _flip(cc, f & 1)
        sends.append(_remote(src, dst_all.at[me], ssem, rsem, k0 + f - 1, (px, py, pc)))
        recvs.append(_remote(src, dst_all.at[4 * px + 2 * py + pc], ssem, rsem, k0 + f - 1, (px, py, pc)))
    return sends, recvs


def _finish(local, sends, recvs):
    for cp in recvs:
        cp.wait_recv()
    for cp in sends:
        cp.wait_send()
    for cp in local:
        cp.wait()


def _c_gather_call(c, rpb_flat):
    def body(c_ref, r_ref, c_all, bias_out, bias_ref, et_ref, ssem, rsem, lsem):
        x, y, cc = _mesh_pos()
        me = 4 * x + 2 * y + cc
        local = [pltpu.make_async_copy(c_ref, c_all.at[me], lsem.at[0])]
        c_send, c_recv = _all_to_all_small(c_ref, c_all, ssem, rsem, 0, x, y, cc)
        for cp in local + c_send:
            cp.start()
        bias_out_copies = _bias_body(r_ref, bias_ref, et_ref, bias_out, lsem.at[1])
        _finish(local + bias_out_copies, c_send, c_recv)

    bias_shape = (rpb_flat.shape[0], 3, TQ, KW)
    return pl.pallas_call(
        body, name="c_gather",
        in_specs=[pl.BlockSpec(memory_space=pltpu.VMEM), pl.BlockSpec(memory_space=pltpu.SMEM)],
        out_specs=(pl.BlockSpec(memory_space=pltpu.VMEM), ANY),
        out_shape=(jax.ShapeDtypeStruct((N_DEV,) + c.shape, c.dtype), jax.ShapeDtypeStruct(bias_shape, F32)),
        scratch_shapes=[pltpu.VMEM(bias_shape, F32), pltpu.VMEM((15, GRID_W, GRID_W), F32),
                        pltpu.SemaphoreType.DMA((N_DEV - 1,)), pltpu.SemaphoreType.DMA((N_DEV - 1,)),
                        pltpu.SemaphoreType.DMA((2,))],
        compiler_params=pltpu.CompilerParams(vmem_limit_bytes=56 << 20))(c, rpb_flat)


VROWS = 32


def _grad_halves_call(dwin_b, dwout_b, dbias, dlg):
    arrs = (dwin_b, dwout_b)
    hrs = [a.shape[1] // 2 for a in arrs]

    def body(din, dout, db_ref, dlg_ref, cp_in, cp_out, drpb_ref, dlgo_ref, got_in, got_out, p_ref, ssem, rsem):
        x, y, cc = _mesh_pos()
        sib = (x, y, 1 - cc)
        srcs, gots, cps = (din, dout), (got_in, got_out), (cp_in, cp_out)
        halves = [_remote(srcs[a].at[:, pl.ds((1 - cc) * hrs[a], hrs[a])], gots[a], ssem, rsem, a, sib)
                  for a in range(2)]
        for cp in halves:
            cp.start()
        _small_reduce_body(db_ref, dlg_ref, drpb_ref, dlgo_ref, p_ref)
        for cp in halves:
            cp.wait_recv()
        for a in range(2):
            for j in range(N_SHARD):
                def add(i, carry, a=a, j=j):
                    r = pl.multiple_of(i * VROWS, VROWS)
                    mine = srcs[a][j, pl.ds(pl.multiple_of(cc * hrs[a] + r, VROWS), VROWS), :].astype(F32)
                    cps[a][j, pl.ds(r, VROWS), :] = (
                        mine + gots[a][j, pl.ds(r, VROWS), :].astype(F32)).astype(BF16)
                    return carry
                lax.fori_loop(0, hrs[a] // VROWS, add, 0)
        for cp in halves:
            cp.wait_send()

    vmem = pl.BlockSpec(memory_space=pltpu.VMEM)
    half_shapes = [(N_SHARD, hrs[a], arrs[a].shape[2]) for a in range(2)]
    return pl.pallas_call(
        body, name="grad_halves",
        in_specs=[vmem] * 4, out_specs=(vmem,) * 4,
        out_shape=(jax.ShapeDtypeStruct(half_shapes[0], BF16), jax.ShapeDtypeStruct(half_shapes[1], BF16),
                   jax.ShapeDtypeStruct((dbias.shape[0], 16, 32), F32), jax.ShapeDtypeStruct((32, 128), F32)),
        scratch_shapes=[pltpu.VMEM(half_shapes[0], BF16), pltpu.VMEM(half_shapes[1], BF16),
                        pltpu.VMEM((32, GRID_W), F32),
                        pltpu.SemaphoreType.DMA((2,)), pltpu.SemaphoreType.DMA((2,))],
        compiler_params=pltpu.CompilerParams(vmem_limit_bytes=56 << 20))(dwin_b, dwout_b, dbias, dlg)


def _grad_finish_call(sl_in, sl_out, small, wada_b):
    arrs = (sl_in, sl_out)
    ws = wada_b.shape[1]

    def body(sin, sout, sm, wa_ref, gin, gout, sm_all, dparts, h_in, h_out, dp_own, ssem, rsem, lsem):
        x, y, cc = _mesh_pos()
        me = 4 * x + 2 * y + cc
        s = 2 * x + y
        sib = (x, y, 1 - cc)
        sls, hs, gs = (sin, sout), (h_in, h_out), (gin, gout)
        sm_send, sm_recv = _all_to_all_small(sm, sm_all, ssem, rsem, 2, x, y, cc)
        sm_own = pltpu.make_async_copy(sm, sm_all.at[me], lsem.at[0])
        for cp in sm_send + [sm_own]:
            cp.start()
        for a in range(2):
            def total(i, carry, a=a):
                rows = pl.ds(pl.multiple_of(i * VROWS, VROWS), VROWS)
                sl = sls[a]
                hs[a][rows, :] = ((sl[0, rows, :].astype(F32) + sl[1, rows, :].astype(F32))
                                  + sl[2, rows, :].astype(F32)) + sl[3, rows, :].astype(F32)
                return carry
            lax.fori_loop(0, arrs[a].shape[1] // VROWS, total, 0)
        mine = [pltpu.make_async_copy(hs[a], gs[a].at[cc], lsem.at[1 + a]) for a in range(2)]
        back = [_remote(hs[a], gs[a].at[cc], ssem, rsem, a, sib) for a in range(2)]
        back_recv = [_remote(hs[a], gs[a].at[1 - cc], ssem, rsem, a, sib) for a in range(2)]
        for cp in mine + back:
            cp.start()
        sm_own.wait()
        for cp in sm_recv:
            cp.wait_recv()
        shift_c = sm_all[0, R_SHIFT_C:R_SHIFT_C + 1, :]
        scale_c = sm_all[0, R_SCALE_C:R_SCALE_C + 1, :]
        for dv in range(1, N_DEV):
            shift_c = shift_c + sm_all[dv, R_SHIFT_C:R_SHIFT_C + 1, :]
            scale_c = scale_c + sm_all[dv, R_SCALE_C:R_SCALE_C + 1, :]
        dmc = jnp.concatenate([shift_c, scale_c, jnp.zeros((1, D), F32)], axis=1).astype(BF16)
        dmc = jnp.broadcast_to(dmc, (8, 3 * D))
        for sh in range(N_SHARD):
            @pl.when(s == sh)
            def _(sh=sh):
                dp_own[...] = _dot_nt(dmc[:, sh * ws:(sh + 1) * ws], wa_ref[...])
        dparts[s] = dp_own[...]
        d_send = [_remote(dp_own, dparts.at[s], ssem, rsem, 9 + k, (px, py, cc))
                  for k, (px, py) in enumerate(_other_chips(x, y))]
        d_recv = [_remote(dp_own, dparts.at[2 * px + py], ssem, rsem, 9 + k, (px, py, cc))
                  for k, (px, py) in enumerate(_other_chips(x, y))]
        for cp in d_send:
            cp.start()
        _finish(mine, back + sm_send + d_send, back_recv + d_recv)

    vmem = pl.BlockSpec(memory_space=pltpu.VMEM)
    return pl.pallas_call(
        body, name="grad_finish",
        in_specs=[vmem] * 4, out_specs=(vmem,) * 4,
        out_shape=(jax.ShapeDtypeStruct((2,) + sl_in.shape[1:], F32),
                   jax.ShapeDtypeStruct((2,) + sl_out.shape[1:], F32),
                   jax.ShapeDtypeStruct((N_DEV,) + small.shape, F32),
                   jax.ShapeDtypeStruct((N_SHARD, 8, D), F32)),
        scratch_shapes=[pltpu.VMEM(sl_in.shape[1:], F32), pltpu.VMEM(sl_out.shape[1:], F32),
                        pltpu.VMEM((8, D), F32),
                        pltpu.SemaphoreType.DMA((12,)), pltpu.SemaphoreType.DMA((12,)),
                        pltpu.SemaphoreType.DMA((3,))],
        compiler_params=pltpu.CompilerParams(vmem_limit_bytes=48 << 20))(sl_in, sl_out, small, wada_b)


def _adamw(w, g, m, v):
    m = ADAM_B1 * m + (1.0 - ADAM_B1) * g
    v = ADAM_B2 * v + (1.0 - ADAM_B2) * (g * g)
    m_hat = m / (1.0 - ADAM_B1 ** ADAM_STEP)
    v_hat = v / (1.0 - ADAM_B2 ** ADAM_STEP)
    return -ADAM_LR * (m_hat / (jnp.sqrt(v_hat) + ADAM_EPS) + ADAM_WD * w), m, v


def _adam_call(w, m, v, g, name):
    R, C = w.shape
    tr = 256

    def body(w_ref, m_ref, v_ref, g_ref, go_ref, d_ref, mo_ref, vo_ref):
        g = g_ref[...]
        go_ref[...] = g
        d_ref[...], mo_ref[...], vo_ref[...] = _adamw(w_ref[...], g, m_ref[...], v_ref[...])

    spec = pl.BlockSpec((tr, C), lambda i: (i, 0))
    return pl.pallas_call(
        body, name=name, grid=(R // tr,), in_specs=[spec] * 4,
        out_specs=(spec,) * 4, out_shape=(jax.ShapeDtypeStruct((R, C), F32),) * 4,
        compiler_params=_params(("arbitrary",)))(w, m, v, g)


R_GF, R_NG, R_LOSS, R_RNG, R_LGF, R_LGB, R_SHIFT, R_SCALE, R_GATE, R_SHIFT_C, R_SCALE_C, R_RNG2, R_RPB = (
    0, 1, 2, 3, 4, 5, 6, 8, 10, 12, 13, 14, 16)
W_GF, W_NG, W_CCTX, W_RNG, W_DF, W_DB, W_BADA, W_RPB = 0, 1, 2, 3, 4, 5, 6, 9


SMALL = (("final_norm_g", W_GF, 1, D), ("norm_g", W_NG, 1, D), ("c_ctx", W_CCTX, 1, D),
         ("ret_norm_g", W_RNG, 1, 512), ("ret_decay_fwd", W_DF, 1, 4), ("ret_decay_bwd", W_DB, 1, 4),
         ("b_ada", W_BADA, 3, D), ("na_rpb", W_RPB, 4, D))
N_SMALL = len(SMALL)


def _small_final_call(sm_all, c_t, dact_parts, wada, m_ada, v_ada, small_w, small_m, small_v, B):
    ws = wada.shape[1]
    NB = N_DEV * B

    def body(*refs):
        sm_ref, ct_ref, wf_ref, wa_ref, ma_ref, va_ref = refs[:6]
        ins = refs[6:6 + 3 * N_SMALL]
        outs = refs[6 + 3 * N_SMALL:6 + 7 * N_SMALL]
        ga_ref, da_ref, mao_ref, vao_ref, loss_ref, dmod_ref, pk_ref = refs[6 + 7 * N_SMALL:]
        x, y, _ = _mesh_pos()
        s = 2 * x + y
        tot = sm_ref[0]
        for dv in range(1, N_DEV):
            tot = tot + sm_ref[dv]
        pk_ref[...] = jnp.zeros_like(pk_ref)
        for kind in range(3):
            for i, (_, row, nrow, width) in enumerate(SMALL):
                ref = ins[kind * N_SMALL + i]
                if nrow == 3:
                    for part in range(3):
                        pk_ref[kind, row + part:row + part + 1, :] = ref[:, part * D:(part + 1) * D]
                else:
                    pk_ref[kind, row:row + nrow, 0:width] = ref[...]
        w = pk_ref[0]
        cctx_ref = ins[2]
        for dv in range(N_DEV):
            for b in range(B):
                r = dv * B + b
                for part, row in enumerate((R_SHIFT, R_SCALE, R_GATE)):
                    dmod_ref[r:r + 1, part * D:(part + 1) * D] = sm_ref[dv, row + b:row + b + 1, :]
        dmod_ref[NB:NB + 1, 0:D] = tot[R_SHIFT_C:R_SHIFT_C + 1, :]
        dmod_ref[NB:NB + 1, D:2 * D] = tot[R_SCALE_C:R_SCALE_C + 1, :]
        dmod_ref[NB:NB + 1, 2 * D:3 * D] = jnp.zeros((1, D), F32)
        dmod_ref[NB + 1:, :] = jnp.zeros((dmod_ref.shape[0] - NB - 1, 3 * D), F32)
        dmod = dmod_ref[...]
        cc = cctx_ref[...]
        scc = _sigmoid(cc)
        ct = ct_ref[...]
        act_t = ct * _sigmoid(ct)
        dact = wf_ref[0, 0:1, :]
        for sh in range(1, N_SHARD):
            dact = dact + wf_ref[sh, 0:1, :]
        g = jnp.zeros((16, D), F32)
        rows = lax.broadcasted_iota(jnp.int32, (16, D), 0)

        def put(g, row, val):
            return jnp.where(rows == row, val, g)

        g = put(g, W_GF, tot[R_GF:R_GF + 1, :])
        g = put(g, W_NG, tot[R_NG:R_NG + 1, :])
        g = put(g, W_CCTX, dact * (scc * (1.0 + cc * (1.0 - scc))))
        g = put(g, W_RNG, tot[R_RNG:R_RNG + 1, :] + tot[R_RNG2:R_RNG2 + 1, :])
        g = put(g, W_DF, tot[R_LGF:R_LGF + 1, :] * (-jnp.exp(w[W_DF:W_DF + 1, :])))
        g = put(g, W_DB, tot[R_LGB:R_LGB + 1, :] * (-jnp.exp(w[W_DB:W_DB + 1, :])))
        db = jnp.sum(dmod, axis=0, keepdims=True)
        for part in range(3):
            g = put(g, W_BADA + part, db[:, part * D:(part + 1) * D])
        for part in range(4):
            g = put(g, W_RPB + part, tot[R_RPB + part:R_RPB + part + 1, :])
        for kind, val in enumerate((g,) + _adamw(w, g, pk_ref[1], pk_ref[2])):
            for i, (_, row, nrow, width) in enumerate(SMALL):
                out = outs[kind * N_SMALL + i]
                if nrow == 3:
                    for part in range(3):
                        out[:, part * D:(part + 1) * D] = val[row + part:row + part + 1, :]
                else:
                    out[...] = val[row:row + nrow, 0:width]
        loss_ref[...] = jnp.broadcast_to(
            (0.5 / D) * jnp.sum(tot[R_LOSS:R_LOSS + 1, :], axis=1, keepdims=True), (8, 128))
        for sh in range(N_SHARD):
            @pl.when(s == sh)
            def _():
                ga = jnp.dot(act_t, dmod[:, sh * ws:(sh + 1) * ws], precision=HIGHEST,
                             preferred_element_type=F32)
                ga_ref[...] = ga
                da_ref[...], mao_ref[...], vao_ref[...] = _adamw(wa_ref[...], ga, ma_ref[...], va_ref[...])

    sh_small = tuple(jax.ShapeDtypeStruct(a.shape, F32) for a in small_w)
    sh_ada = jax.ShapeDtypeStruct(wada.shape, F32)
    res = pl.pallas_call(
        body, name="small_final",
        out_shape=sh_small * 4 + (sh_ada,) * 4 + (jax.ShapeDtypeStruct((8, 128), F32),),
        scratch_shapes=[pltpu.VMEM((NB + 8, 3 * D), F32), pltpu.VMEM((3, 16, D), F32)],
        compiler_params=_params(vmem_mb=56))(
            sm_all, c_t, dact_parts, wada, m_ada, v_ada, *small_w, *small_m, *small_v)
    smalls = [res[k * N_SMALL:(k + 1) * N_SMALL] for k in range(4)]
    return smalls, res[4 * N_SMALL:4 * N_SMALL + 4], res[4 * N_SMALL + 4]


def _local_step(order, x, ctx, c_rows, norm_g, wada_b, b_shard, win_b, bias, dec_f, dec_b, ret_norm_g,
                wout_b, final_g, target):
    B, L, _ = x.shape
    LC = ctx.shape[1]
    assert B == 2
    cos2, sin2 = _rope_tables(L, LC)
    mod_part = _mod_part_call(c_rows, wada_b, b_shard)
    P, h, win_f, wout_f, mod = _inproj_gather_call(order, x, ctx, mod_part, norm_g, win_b, wout_b, cos2, sin2)
    y_na, o_na = _na_fwd_call(P, bias, L, LC)
    sf, sb = _ret_states_call(P, dec_f, dec_b, L, LC)
    y_ret, o_ret = _retc_fwd_call(P, sf, sb, dec_f, dec_b, ret_norm_g, L)
    dY, dx2, dwout_p, sm_out = _out_call(y_na, y_ret, x, target, mod, final_g, wout_f.reshape(D, D))
    dnq, dng, dnk, dnv, dbias = _na_bwd_call(P, bias, dY, o_na, L, LC)
    drq, drg, drk, drv, dgn, dlg = _retc_bwd_call(P, sf, sb, dec_f, dec_b, ret_norm_g, o_ret, dY, cos2, sin2, L, LC)
    dsec = (dnq, dnk, dnv, dng, drq, drk, drv, drg)
    dwin_b = _dw_call(dsec, h, L)
    cp_in, cp_out, drpb, dlg_sum = _grad_halves_call(
        dwin_b, dwout_p.reshape(N_SHARD, D // N_SHARD, D), dbias, dlg)
    grad_x, sm_dh, sl_in, sl_out = _dh_call(dsec, win_f, x, ctx, dx2, mod, norm_g, cp_in, cp_out)
    z = jnp.zeros((1, D), F32)
    pad = lambda v: jnp.pad(v.reshape(1, -1), ((0, 0), (0, D - v.size)))
    dlg_sum = dlg_sum.reshape(4, 8, 128)
    rpb_rows = jnp.pad(drpb[:, :15, :31].reshape(-1), (0, 4 * D - drpb.shape[0] * 465)).reshape(4, D)
    small = jnp.concatenate([
        sm_out[0:1], sm_dh[0:1], sm_out[1:2], pad(dgn[0]), pad(dlg_sum[:, 0, 0]), pad(dlg_sum[:, 1, 0]),
        sm_dh[3:5], sm_dh[5:7], sm_out[2:4], sm_dh[1:2], sm_dh[2:3], pad(dgn[1]), z, rpb_rows,
        jnp.zeros((SM_ROWS - 20, D), F32)], axis=0)
    return grad_x, sl_in, sl_out, small


def kernel(x, c, ctx, c_ctx, norm_g, w_ada, b_ada, w_in, na_rpb, ret_decay_fwd, ret_decay_bwd, ret_norm_g, w_out, final_norm_g, loss_target, m_c_ctx, m_norm_g, m_w_ada, m_b_ada, m_w_in, m_na_rpb, m_ret_decay_fwd, m_ret_decay_bwd, m_ret_norm_g, m_w_out, m_final_norm_g, v_c_ctx, v_norm_g, v_w_ada, v_b_ada, v_w_in, v_na_rpb, v_ret_decay_fwd, v_ret_decay_bwd, v_ret_norm_g, v_w_out, v_final_norm_g):
    B = x.shape[0]
    c_all, bias = _c_gather_call(c, na_rpb[0].reshape(na_rpb.shape[1], -1))
    c_rows = jnp.concatenate([c_all.reshape(N_DEV * B, D), c_ctx.reshape(1, D), jnp.zeros((7, D), F32)], axis=0)
    mx, my = lax.axis_index("x"), lax.axis_index("y")
    order = jnp.stack([2 * mx + my, 2 * (1 - mx) + my, 2 * mx + (1 - my),
                       2 * (1 - mx) + (1 - my)]).astype(jnp.int32)
    ws = w_ada.shape[2]
    b_shard = lax.dynamic_slice(b_ada, (0, (2 * mx + my) * ws), (1, ws))
    wada_b = w_ada[0].astype(BF16)
    grad_x, sl_in, sl_out, small = _local_step(
        order, x, ctx, c_rows, norm_g, wada_b, b_shard, w_in[0].astype(BF16), bias, ret_decay_fwd,
        ret_decay_bwd, ret_norm_g, w_out[0].astype(BF16), final_norm_g.reshape(1, D), loss_target)
    gin, gout, sm_all, dact_parts = _grad_finish_call(sl_in, sl_out, small, wada_b)
    g_win, d_win, nm_win, nv_win = _adam_call(
        w_in[0], m_w_in[0], v_w_in[0], gin.reshape(w_in.shape[1:]), "adam_w_in")
    g_wout, d_wout, nm_wout, nv_wout = _adam_call(
        w_out[0], m_w_out[0], v_w_out[0], gout.reshape(w_out.shape[1:]), "adam_w_out")

    def small_inputs(gf, ng, cc, rng, df, db, bada, rpb):
        return (gf.reshape(1, D), ng, cc.reshape(1, D), rng, df, db, bada,
                jnp.pad(rpb.reshape(-1), (0, 4 * D - rpb.size)).reshape(4, D))

    c_t = c_rows.T
    smalls, adas, loss = _small_final_call(
        sm_all, c_t, dact_parts, w_ada[0], m_w_ada[0], v_w_ada[0],
        small_inputs(final_norm_g, norm_g, c_ctx, ret_norm_g, ret_decay_fwd, ret_decay_bwd, b_ada, na_rpb),
        small_inputs(m_final_norm_g, m_norm_g, m_c_ctx, m_ret_norm_g, m_ret_decay_fwd, m_ret_decay_bwd, m_b_ada,
                     m_na_rpb),
        small_inputs(v_final_norm_g, v_norm_g, v_c_ctx, v_ret_norm_g, v_ret_decay_fwd, v_ret_decay_bwd, v_b_ada,
                     v_na_rpb), B)
    res = []
    for p, ada, win_o, wout_o in zip(smalls, adas, (g_win, d_win, nm_win, nv_win),
                                     (g_wout, d_wout, nm_wout, nv_wout)):
        gf, ng, cc, rng, df, db, bada, rpb = p
        res.append([cc.reshape(D), ng, ada[None], bada, win_o[None],
                    rpb.reshape(-1)[:na_rpb.size].reshape(na_rpb.shape), df, db, rng, wout_o[None], gf.reshape(D)])
    return (loss[0, 0], grad_x, *res[0], *res[1], *res[2], *res[3])
```

```python
import numpy as np
import jax
import jax.numpy as jnp
from jax import lax
from jax.experimental import pallas as pl
from jax.experimental.pallas import tpu as pltpu

F32 = jnp.float32
BF16 = jnp.bfloat16
HIGHEST = lax.Precision.HIGHEST

D = 1024
GRID_W = 64
NA_DH = 64
RET_DK = 128
ROPE_BASE = 10000.0
EPS = 1e-6
NEG = -1e30
TQ = 256
KW = 12 * GRID_W
N_SHARD = 4
N_DEV = 8
SM_ROWS = 24

ADAM_LR = 0.001
ADAM_B1 = 0.9
ADAM_B2 = 0.999
ADAM_EPS = 1e-08
ADAM_WD = 0.01
ADAM_STEP = 10

MESH = pl.DeviceIdType.MESH
ANY = pl.BlockSpec(memory_space=pl.ANY)


def _params(sem=None, vmem_mb=48):
    return pltpu.CompilerParams(dimension_semantics=sem, vmem_limit_bytes=vmem_mb << 20)


def _dot(a, b):
    return jnp.dot(a, b, preferred_element_type=F32)


def _dot_nt(a, b):
    return lax.dot_general(a, b, (((1,), (1,)), ((), ())), preferred_element_type=F32)


def _dot_tn(a, b):
    return lax.dot_general(a, b, (((0,), (0,)), ((), ())), preferred_element_type=F32)


def _sigmoid(x):
    return 1.0 / (1.0 + jnp.exp(-x))


def _rope_tables(L, LC):
    half = RET_DK // 2
    nf = half // 2
    t = np.arange(L)
    row = (t // GRID_W).astype(np.float32)
    col = (t % GRID_W).astype(np.float32)
    inv = (np.float32(ROPE_BASE) ** (-np.arange(nf, dtype=np.float32) / np.float32(nf))).astype(np.float32)
    ang = np.concatenate([row[:, None] * inv, col[:, None] * inv], axis=-1).astype(np.float32)
    cos, sin = np.cos(ang).astype(np.float32), np.sin(ang).astype(np.float32)
    cos2 = np.concatenate([cos, cos], axis=-1)
    sin2 = np.concatenate([-sin, sin], axis=-1)
    cos2 = np.concatenate([cos2, np.ones((LC, RET_DK), np.float32)], axis=0)
    sin2 = np.concatenate([sin2, np.zeros((LC, RET_DK), np.float32)], axis=0)
    return jnp.asarray(cos2), jnp.asarray(sin2)


def _mod_part_call(c_rows, wada_b, b_shard):
    def body(c_ref, w_ref, b_ref, o_ref):
        a = c_ref[...]
        o_ref[...] = _dot((a * _sigmoid(a)).astype(BF16), w_ref[...]) + b_ref[...]

    return pl.pallas_call(
        body, name="ada_mod", out_shape=jax.ShapeDtypeStruct((c_rows.shape[0], wada_b.shape[1]), F32),
        compiler_params=_params())(c_rows, wada_b, b_shard)


def _dc_masks():
    cq = lax.broadcasted_iota(jnp.int32, (GRID_W, GRID_W), 0)
    ck = lax.broadcasted_iota(jnp.int32, (GRID_W, GRID_W), 1)
    dc = jnp.clip(ck - cq + 15, 0, 30)
    c0 = jnp.clip(cq - 8, 0, GRID_W - 16)
    col_ok = (ck >= c0) & (ck < c0 + 16)
    return dc, col_ok


def _bias_blocks():
    out = []
    for typ, delta in enumerate((4, 0, -4)):
        for rq in range(4):
            for rkk in range(12):
                dr = rkk + delta - rq - 4
                if typ == 0:
                    ok = -rq <= dr <= 7 - rq
                elif typ == 1:
                    ok = -4 <= dr <= 3
                else:
                    ok = -4 - rq <= dr <= 3 - rq
                out.append((typ, rq, rkk, dr if ok else None))
    return out


def _bias_body(r_ref, bias_ref, et_ref, out_ref, sem):
    dc, col_ok = _dc_masks()
    masks = [(dc == j).astype(F32) for j in range(31)]
    nh = bias_ref.shape[0]

    def per_h(h, carry):
        for dr in range(15):
            t = jnp.zeros((GRID_W, GRID_W), F32)
            for j in range(31):
                t = t + masks[j] * r_ref[h, dr * 31 + j]
            et_ref[dr] = jnp.where(col_ok, t, NEG)
        neg = jnp.full((GRID_W, GRID_W), NEG, F32)
        for typ, rq, rkk, dr in _bias_blocks():
            blk = neg if dr is None else et_ref[dr + 7]
            bias_ref[h, typ, rq * 64:(rq + 1) * 64, rkk * 64:(rkk + 1) * 64] = blk
        pltpu.make_async_copy(bias_ref.at[h], out_ref.at[h], sem).start()
        return carry

    lax.fori_loop(0, nh, per_h, 0)
    return [pltpu.make_async_copy(bias_ref.at[h], out_ref.at[h], sem) for h in range(nh)]


def _bias_tile_sums(db_ref, hh):
    acc = {}
    for typ, rq, rkk, dr in _bias_blocks():
        if dr is None:
            continue
        blk = db_ref[hh, typ, rq * 64:(rq + 1) * 64, rkk * 64:(rkk + 1) * 64]
        acc[dr] = blk if dr not in acc else acc[dr] + blk
    return acc


def _small_reduce_body(dt_ref, dlg_ref, drpb_ref, dlgo_ref, p_ref):
    dc, _ = _dc_masks()
    masks = [(dc == j).astype(F32) for j in range(31)]
    ones = jnp.ones((8, GRID_W), F32)
    p_ref[...] = jnp.zeros_like(p_ref)
    drpb_ref[...] = jnp.zeros_like(drpb_ref)

    def per_h(h, carry):
        for dr in range(-7, 8):
            t = dt_ref[h, dr + 7]
            for j in range(31):
                p_ref[j:j + 1, :] = jnp.sum(t * masks[j], axis=0, keepdims=True)
            red = lax.dot_general(ones, p_ref[...], (((1,), (1,)), ((), ())),
                                  precision=HIGHEST, preferred_element_type=F32)
            drpb_ref[h, dr + 7:dr + 8, :] = red[0:1, :]
        return carry

    lax.fori_loop(0, dt_ref.shape[0], per_h, 0)
    x = dlg_ref[0]
    for b in range(1, dlg_ref.shape[0]):
        x = x + dlg_ref[b]
    x = x.reshape(4 * 8, x.shape[-1])
    dlgo_ref[...] = jnp.dot(x, jnp.ones((x.shape[-1], 128), F32), precision=HIGHEST,
                            preferred_element_type=F32)


def _inproj_gather_call(order, x, ctx, mod_part, norm_g, win_b, wout_b, cos2, sin2):
    B, L, _ = x.shape
    LC = ctx.shape[1]
    T = L + LC
    TI = 2 * TQ
    nl = L // TI
    nt = nl + 1
    assert LC == TQ and L % TI == 0
    kscale = RET_DK ** -0.5
    HR = D // 2
    pad_rows = nt * TI - T
    cos2 = jnp.pad(cos2, ((0, pad_rows), (0, 0)))
    sin2 = jnp.pad(sin2, ((0, pad_rows), (0, 0)))

    MW = mod_part.shape[1]
    NB = N_DEV * B

    def body(ord_ref, x_ref, ctx_ref, mp_ref, g_ref, wown_ref, woown_ref, cos_ref, sin_ref,
             p_ref, h_ref, wf_ref, wof_ref, modo_ref, w_all, wo_all, hs_ref, mp_all, mod_ref, ssem, rsem, lsem):
        j, b, t = pl.program_id(0), pl.program_id(1), pl.program_id(2)
        first = (b == 0) & (t == 0)
        mx, my, mc = _mesh_pos()
        s = 2 * mx + my

        m_send = [_remote(mp_ref, mp_all.at[s], ssem, rsem, 12 + k, (px, py, mc))
                  for k, (px, py) in enumerate(_other_chips(mx, my))]
        m_recv = [_remote(mp_ref, mp_all.at[2 * px + py], ssem, rsem, 12 + k, (px, py, mc))
                  for k, (px, py) in enumerate(_other_chips(mx, my))]

        sems = (ssem, rsem, lsem)
        own, ici_send, ici_recv, fwd_send, fwd_recv, outs = _gather_copies(wown_ref, w_all, wf_ref, HR, sems, 0, 0)
        oown, o_send, o_recv, o_fsend, o_frecv, o_outs = _gather_copies(
            woown_ref, wo_all, wof_ref, woown_ref.shape[0] // 2, sems, 6, 5)

        @pl.when(first & (j == 0))
        def _():
            for cp in m_send:
                cp.start()
            own.start()
            oown.start()
            mp_all[s] = mp_ref[...]
            own.wait()
            ici_send[0].start()
            ici_send[1].start()
            outs[0].start()
            oown.wait()
            for cp in m_recv:
                cp.wait_recv()
            me = 4 * mx + 2 * my + mc
            mod_ref[...] = jnp.zeros_like(mod_ref)
            for p in range(N_SHARD):
                for r in range(B):
                    mod_ref[r:r + 1, p * MW:(p + 1) * MW] = mp_all[p, pl.ds(B * me + r, 1), :]
                mod_ref[B:B + 1, p * MW:(p + 1) * MW] = mp_all[p, NB:NB + 1, :]
            modo_ref[...] = mod_ref[...]

        for k in range(3):
            @pl.when(first & (j == k + 1))
            def _(k=k):
                ici_recv[k].wait_recv()
                if k == 0:
                    ici_send[2].start()
                fwd_send[k].start()
                fwd_recv[k].wait_recv()
                outs[1 + k].start()
                if k == 1:
                    for cp in o_send:
                        cp.start()
                if k == 2:
                    for got, fwd in zip(o_recv, o_fsend):
                        got.wait_recv()
                        fwd.start()

        tile = b * nt + t

        @pl.when(j == 0)
        def _():
            is_lat = t < nl
            ctx_tile = jnp.concatenate([ctx_ref[...], jnp.zeros((TI - LC, D), F32)], axis=0)
            xt = jnp.where(is_lat, x_ref[...], ctx_tile)
            mrow = mod_ref[pl.ds(jnp.where(is_lat, b, B), 1), :]
            shift, scale = mrow[:, 0:D], mrow[:, D:2 * D]
            rstd = lax.rsqrt(jnp.mean(xt * xt, axis=-1, keepdims=True) + EPS)
            h0 = ((xt * rstd * g_ref[...]) * (1.0 + scale) + shift).astype(BF16)
            h_ref[...] = h0
            hs_ref[tile] = h0

        shard = ord_ref[j]

        def project(sh, nrows):
            hb = hs_ref[tile, 0:nrows, :]
            cs, sn = cos_ref[0:nrows, :], sin_ref[0:nrows, :]
            for half in range(2):
                sec = 2 * sh + half
                acc = _dot(hb, w_all[sh, :, half * 512:(half + 1) * 512])
                if sec == 0:
                    acc = acc * (NA_DH ** -0.5)
                if sec in (4, 5):
                    for q in range(4):
                        a = acc[:, q * 128:(q + 1) * 128]
                        r = a * cs + pltpu.roll(a, 64, 1) * sn
                        if sec == 5:
                            r = r * kscale
                        p_ref[0:nrows, half * 512 + q * 128:half * 512 + (q + 1) * 128] = r.astype(BF16)
                else:
                    p_ref[0:nrows, half * 512:(half + 1) * 512] = acc.astype(BF16)

        for sh in range(N_SHARD):
            @pl.when((shard == sh) & (t < nl))
            def _(sh=sh):
                project(sh, TI)

            @pl.when((shard == sh) & (t == nl))
            def _(sh=sh):
                project(sh, LC)

        @pl.when((j == N_SHARD - 1) & (b == B - 1) & (t == nt - 1))
        def _():
            for cp in o_frecv:
                cp.wait_recv()
            for cp in o_outs:
                cp.start()
            _finish(outs + o_outs, ici_send + fwd_send + o_send + o_fsend + m_send, [])

    tok = lambda j, b, t, o: (jnp.where(j == 0, b, B - 1), jnp.where(j == 0, jnp.minimum(t, nl - 1), nl - 1), 0)
    grid_spec = pltpu.PrefetchScalarGridSpec(
        num_scalar_prefetch=1, grid=(N_SHARD, B, nt),
        in_specs=[
            pl.BlockSpec((None, TI, D), tok),
            pl.BlockSpec((None, LC, D), lambda j, b, t, o: (jnp.where(j == 0, b, B - 1), 0, 0)),
            pl.BlockSpec(mod_part.shape, lambda j, b, t, o: (0, 0)),
            pl.BlockSpec((1, D), lambda j, b, t, o: (0, 0)),
            ANY, ANY,
            pl.BlockSpec((TI, RET_DK), lambda j, b, t, o: (t, 0)),
            pl.BlockSpec((TI, RET_DK), lambda j, b, t, o: (t, 0)),
        ],
        out_specs=(pl.BlockSpec((None, TI, D), lambda j, b, t, o: (b, t, o[j])),
                   pl.BlockSpec((None, TI, D), lambda j, b, t, o: (
                       jnp.where(j == 0, b, B - 1), jnp.where(j == 0, t, nt - 1), 0)), ANY, ANY,
                   pl.BlockSpec((8, 3 * D), lambda j, b, t, o: (0, 0))),
        scratch_shapes=[pltpu.VMEM((N_SHARD, D, D), BF16), pltpu.VMEM((N_SHARD,) + wout_b.shape, BF16),
                        pltpu.VMEM((B * nt, TI, D), BF16),
                        pltpu.VMEM((N_SHARD,) + mod_part.shape, F32), pltpu.VMEM((8, 3 * D), F32),
                        pltpu.SemaphoreType.DMA((15,)), pltpu.SemaphoreType.DMA((15,)),
                        pltpu.SemaphoreType.DMA((10,))])
    return pl.pallas_call(
        body, name="in_proj", grid_spec=grid_spec,
        out_shape=(jax.ShapeDtypeStruct((B, T, 4 * D), BF16), jax.ShapeDtypeStruct((B, T, D), BF16),
                   jax.ShapeDtypeStruct((N_SHARD, D, D), BF16),
                   jax.ShapeDtypeStruct((N_SHARD,) + wout_b.shape, BF16),
                   jax.ShapeDtypeStruct((8, 3 * D), F32)),
        compiler_params=_params(("arbitrary",) * 3, vmem_mb=56))(
            order, x, ctx, mod_part, norm_g, win_b, wout_b, cos2, sin2)


def _na_specs(L, T, rows, nh=2):
    nm = rows // 4
    w = nh * NA_DH
    per = 512 // w
    q_spec = pl.BlockSpec((None, TQ, w), lambda hp, b, m: (b, m, hp))
    k_spec = pl.BlockSpec((None, T, w), lambda hp, b, m: (b, 0, per + hp))
    v_spec = pl.BlockSpec((None, T, w), lambda hp, b, m: (b, 0, 2 * per + hp))
    g_spec = pl.BlockSpec((None, TQ, w), lambda hp, b, m: (b, m, 3 * per + hp))
    bias_spec = pl.BlockSpec((nh, 3, TQ, KW), lambda hp, b, m: (hp, 0, 0, 0))
    return nm, q_spec, k_spec, v_spec, g_spec, bias_spec


def _na_tile(m, nm, rows):
    typ = jnp.where(m == 0, 0, jnp.where(m == nm - 1, 2, 1))
    start = pl.multiple_of(jnp.clip(4 * m - 4, 0, rows - 12) * GRID_W, TQ)
    return typ, start


def _na_fwd_call(P, bias, L, LC):
    B, T, _ = P.shape
    rows = L // GRID_W
    NH = 4
    nm, q_spec, k_spec, v_spec, g_spec, bias_spec = _na_specs(L, T, rows, NH)

    def body(q_ref, k_ref, v_ref, g_ref, bias_ref, y_ref, o_ref):
        typ, start = _na_tile(pl.program_id(2), nm, rows)
        for hh in range(NH):
            ln = slice(hh * NA_DH, (hh + 1) * NA_DH)
            q = q_ref[:, ln]
            kw, vw = k_ref[pl.ds(start, KW), ln], v_ref[pl.ds(start, KW), ln]
            kc, vc = k_ref[L:L + LC, ln], v_ref[L:L + LC, ln]
            s1 = _dot_nt(q, kw) + bias_ref[hh, typ]
            s2 = _dot_nt(q, kc)
            mx = jnp.maximum(jnp.max(s1, axis=-1, keepdims=True), jnp.max(s2, axis=-1, keepdims=True))
            p1, p2 = jnp.exp(s1 - mx), jnp.exp(s2 - mx)
            inv = 1.0 / (jnp.sum(p1, axis=-1, keepdims=True) + jnp.sum(p2, axis=-1, keepdims=True))
            o = (_dot(p1.astype(BF16), vw) + _dot(p2.astype(BF16), vc)) * inv
            g = g_ref[:, ln].astype(F32)
            o_ref[:, ln] = o.astype(BF16)
            y_ref[:, ln] = (o * (g * _sigmoid(g))).astype(BF16)

    tile = pl.BlockSpec((None, TQ, NH * NA_DH), lambda hp, b, m: (b, m, hp))
    return pl.pallas_call(
        body, name="na_fwd", grid=(8 // NH, B, nm),
        in_specs=[q_spec, k_spec, v_spec, g_spec, bias_spec],
        out_specs=(tile, tile),
        out_shape=(jax.ShapeDtypeStruct((B, L, 512), BF16),) * 2,
        compiler_params=_params(("arbitrary",) * 3))(P, P, P, P, bias)


def _na_bwd_call(P, bias, dY, o_na, L, LC):
    B, T, _ = P.shape
    rows = L // GRID_W
    NH = 4
    W = NH * NA_DH
    nm, q_spec, k_spec, v_spec, g_spec, bias_spec = _na_specs(L, T, rows, NH)
    scale = NA_DH ** -0.5

    RB = 32

    def body(q_ref, k_ref, v_ref, g_ref, bias_ref, dy_ref, o_ref, dq_ref, dg_ref, dk_ref, dv_ref, dt_ref,
             db_ref, s1_ref, s2_ref, dp1_ref, dp2_ref, p1_ref, p2_ref, ds1_ref, ds2_ref, dkt_ref, dvt_ref):
        b, m = pl.program_id(1), pl.program_id(2)
        typ, start = _na_tile(m, nm, rows)

        @pl.when(m == 0)
        def _():
            dkt_ref[...] = jnp.zeros_like(dkt_ref)
            dvt_ref[...] = jnp.zeros_like(dvt_ref)

        @pl.when((m == 0) & (b == 0))
        def _():
            db_ref[...] = jnp.zeros_like(db_ref)

        for hh in range(NH):
            ln = slice(hh * NA_DH, (hh + 1) * NA_DH)
            q = q_ref[:, ln]
            kw, vw = k_ref[pl.ds(start, KW), ln], v_ref[pl.ds(start, KW), ln]
            kc, vc = k_ref[L:L + LC, ln], v_ref[L:L + LC, ln]
            g = g_ref[:, ln].astype(F32)
            sg = _sigmoid(g)
            dy = dy_ref[:, ln].astype(F32)
            do = (dy * (g * sg)).astype(BF16)
            s1_ref[hh] = _dot_nt(q, kw)
            s2_ref[hh] = _dot_nt(q, kc)
            dp1_ref[hh] = _dot_nt(do, vw)
            dp2_ref[hh] = _dot_nt(do, vc)

            def rows_pass(r, carry, hh=hh):
                rw = pl.ds(pl.multiple_of(r * RB, RB), RB)
                a = s1_ref[hh, rw, :] + bias_ref[hh, typ, rw, :]
                c = s2_ref[hh, rw, :]
                mx = jnp.maximum(jnp.max(a, axis=-1, keepdims=True), jnp.max(c, axis=-1, keepdims=True))
                e1, e2 = jnp.exp(a - mx), jnp.exp(c - mx)
                inv = 1.0 / (jnp.sum(e1, axis=-1, keepdims=True) + jnp.sum(e2, axis=-1, keepdims=True))
                p1, p2 = e1 * inv, e2 * inv
                p1_ref[hh, rw, :] = p1.astype(BF16)
                p2_ref[hh, rw, :] = p2.astype(BF16)
                dp1, dp2 = dp1_ref[hh, rw, :], dp2_ref[hh, rw, :]
                delta = jnp.sum(p1 * dp1, axis=-1, keepdims=True) + jnp.sum(p2 * dp2, axis=-1, keepdims=True)
                ds1 = p1 * (dp1 - delta)
                db_ref[hh, typ, rw, :] += ds1
                ds1_ref[hh, rw, :] = ds1.astype(BF16)
                ds2_ref[hh, rw, :] = (p2 * (dp2 - delta)).astype(BF16)
                return carry

            lax.fori_loop(0, TQ // RB, rows_pass, 0, unroll=True)
            p1b, p2b, ds1b, ds2b = p1_ref[hh], p2_ref[hh], ds1_ref[hh], ds2_ref[hh]
            dg_ref[:, ln] = (dy * o_ref[:, ln].astype(F32) * (sg * (1.0 + g * (1.0 - sg)))).astype(BF16)
            dq_ref[:, ln] = ((_dot(ds1b, kw) + _dot(ds2b, kc)) * scale).astype(BF16)
            dkt_ref[ln, pl.ds(start, KW)] += _dot_tn(q, ds1b)
            dvt_ref[ln, pl.ds(start, KW)] += _dot_tn(do, p1b)
            dkt_ref[ln, L:L + LC] += _dot_tn(q, ds2b)
            dvt_ref[ln, L:L + LC] += _dot_tn(do, p2b)

        @pl.when(m == nm - 1)
        def _():
            dk_ref[...] = dkt_ref[...].T
            dv_ref[...] = dvt_ref[...].T

        @pl.when((m == nm - 1) & (b == B - 1))
        def _():
            for hh in range(NH):
                for dr, t in _bias_tile_sums(db_ref, hh).items():
                    dt_ref[hh, dr + 7] = t

    tile = pl.BlockSpec((None, TQ, W), lambda hp, b, m: (b, m, hp))
    kv_out = pl.BlockSpec((None, T, W), lambda hp, b, m: (b, 0, hp))
    wide, narrow = (NH, TQ, KW), (NH, TQ, LC)
    return pl.pallas_call(
        body, name="na_bwd", grid=(8 // NH, B, nm),
        in_specs=[q_spec, k_spec, v_spec, g_spec, bias_spec, tile, tile],
        out_specs=(tile, tile, kv_out, kv_out,
                   pl.BlockSpec((NH, 15, GRID_W, GRID_W), lambda hp, b, m: (hp, 0, 0, 0))),
        out_shape=(jax.ShapeDtypeStruct((B, L, 512), BF16), jax.ShapeDtypeStruct((B, L, 512), BF16),
                   jax.ShapeDtypeStruct((B, T, 512), F32), jax.ShapeDtypeStruct((B, T, 512), F32),
                   jax.ShapeDtypeStruct((bias.shape[0], 15, GRID_W, GRID_W), F32)),
        scratch_shapes=[pltpu.VMEM((NH,) + bias.shape[1:], F32),
                        pltpu.VMEM(wide, F32), pltpu.VMEM(narrow, F32), pltpu.VMEM(wide, F32), pltpu.VMEM(narrow, F32),
                        pltpu.VMEM(wide, BF16), pltpu.VMEM(narrow, BF16), pltpu.VMEM(wide, BF16),
                        pltpu.VMEM(narrow, BF16), pltpu.VMEM((W, T), F32), pltpu.VMEM((W, T), F32)],
        compiler_params=_params(("arbitrary",) * 3, vmem_mb=60))(P, P, P, P, bias, dY, o_na)


def _head_scalar(dec_ref, h):
    lane = lax.broadcasted_iota(jnp.int32, dec_ref.shape, 1)
    return -jnp.sum(jnp.where(lane == h, jnp.exp(dec_ref[...]), 0.0), axis=1, keepdims=True)


def _chunk_decay(lgf, lgb):
    tau = lax.broadcasted_iota(jnp.int32, (TQ, 1), 0).astype(F32)
    sig = lax.broadcasted_iota(jnp.int32, (1, TQ), 1).astype(F32)
    dist = tau - sig
    dm = jnp.exp(dist * jnp.where(dist > 0, lgf, -lgb)) * jnp.where(dist == 0, 2.0, 1.0)
    return tau, dist, dm


def _ret_states_call(P, dec_f, dec_b, L, LC):
    B, T, _ = P.shape
    n = L // TQ

    def body(df_ref, db_ref, k_ref, v_ref, sf_ref, sb_ref):
        h = pl.program_id(1)
        lgf, lgb = _head_scalar(df_ref, h), _head_scalar(db_ref, h)
        tau = lax.broadcasted_iota(jnp.int32, (TQ, 1), 0).astype(F32)
        jc = lax.broadcasted_iota(jnp.int32, (LC, 1), 0).astype(F32)
        wf, wb = jnp.exp(lgf * (TQ - 1.0 - tau)), jnp.exp(lgb * tau)
        gcf, gcb = jnp.exp(lgf * float(TQ)), jnp.exp(lgb * float(TQ))
        kc, vc = k_ref[L:L + LC, :].astype(F32), v_ref[L:L + LC, :]

        def chunk_state(i, w):
            ks = pl.multiple_of(i * TQ, TQ)
            return _dot_tn((k_ref[pl.ds(ks, TQ), :].astype(F32) * w).astype(BF16), v_ref[pl.ds(ks, TQ), :])

        def fwd(i, s):
            sf_ref[i] = s
            return gcf * s + chunk_state(i, wf)

        lax.fori_loop(0, n, fwd, _dot_tn((kc * jnp.exp(lgf * (LC - 1.0 - jc))).astype(BF16), vc), unroll=True)

        def bwd(r, s):
            i = n - 1 - r
            sb_ref[i] = s
            return gcb * s + chunk_state(i, wb)

        lax.fori_loop(0, n, bwd, _dot_tn((kc * jnp.exp(lgb * jc)).astype(BF16), vc), unroll=True)

    st = pl.BlockSpec((None, None, n, RET_DK, RET_DK), lambda b, h: (b, h, 0, 0, 0))
    return pl.pallas_call(
        body, name="ret_states", grid=(B, 4),
        in_specs=[pl.BlockSpec((1, 4), lambda b, h: (0, 0)), pl.BlockSpec((1, 4), lambda b, h: (0, 0)),
                  pl.BlockSpec((None, T, 128), lambda b, h: (b, 0, 20 + h)),
                  pl.BlockSpec((None, T, 128), lambda b, h: (b, 0, 24 + h))],
        out_specs=(st, st),
        out_shape=(jax.ShapeDtypeStruct((B, 4, n, RET_DK, RET_DK), F32),) * 2,
        compiler_params=_params(("arbitrary",) * 2))(dec_f, dec_b, P, P)


def _retc_fwd_call(P, sf, sb, dec_f, dec_b, ret_norm_g, L):
    B, T, _ = P.shape
    sec = lambda k: pl.BlockSpec((None, TQ, 512), lambda b, i: (b, i, k))
    dec_spec = pl.BlockSpec((1, 4), lambda b, i: (0, 0))
    st_spec = pl.BlockSpec((None, 4, None, RET_DK, RET_DK), lambda b, i: (b, 0, i, 0, 0))

    def body(df_ref, db_ref, q_ref, k_ref, v_ref, g_ref, gn_ref, sf_ref, sb_ref, y_ref, o_ref):
        for h in range(4):
            ln = slice(h * RET_DK, (h + 1) * RET_DK)
            lgf, lgb = _head_scalar(df_ref, h), _head_scalar(db_ref, h)
            tau, _, dm = _chunk_decay(lgf, lgb)
            q = q_ref[:, ln]
            qf = q.astype(F32)
            acc = _dot((_dot_nt(q, k_ref[:, ln]) * dm).astype(BF16), v_ref[:, ln])
            acc = acc + _dot((qf * jnp.exp(lgf * (tau + 1.0))).astype(BF16), sf_ref[h].astype(BF16))
            acc = acc + _dot((qf * jnp.exp(lgb * (TQ - tau))).astype(BF16), sb_ref[h].astype(BF16))
            o_ref[:, ln] = acc
            rn = lax.rsqrt(jnp.mean(acc * acc, axis=-1, keepdims=True) + EPS)
            g = g_ref[:, ln].astype(F32)
            y_ref[:, ln] = ((acc * rn * gn_ref[:, ln]) * (g * _sigmoid(g))).astype(BF16)

    tile = pl.BlockSpec((None, TQ, 512), lambda b, i: (b, i, 0))
    return pl.pallas_call(
        body, name="ret_fwd", grid=(B, L // TQ),
        in_specs=[dec_spec, dec_spec, sec(4), sec(5), sec(6), sec(7),
                  pl.BlockSpec((1, 512), lambda b, i: (0, 0)), st_spec, st_spec],
        out_specs=(tile, tile),
        out_shape=(jax.ShapeDtypeStruct((B, L, 512), BF16), jax.ShapeDtypeStruct((B, L, 512), F32)),
        compiler_params=_params(("arbitrary",) * 2))(dec_f, dec_b, P, P, P, P, ret_norm_g, sf, sb)


def _retc_bwd_call(P, sf, sb, dec_f, dec_b, ret_norm_g, o_ret, dY, cos2, sin2, L, LC):
    B, T, _ = P.shape
    n = L // TQ
    C = float(TQ)
    kscale = RET_DK ** -0.5
    st_spec = pl.BlockSpec((None, 4, n, RET_DK, RET_DK), lambda b, i: (b, 0, 0, 0, 0))

    def body(df_ref, db_ref, q_ref, k_ref, v_ref, g_ref, gn_ref, o_ref, dy_ref, cos_ref, sin_ref, sf_ref, sb_ref,
             dq_ref, dg_ref, dk_ref, dv_ref, dgn_ref, dlg_ref, dsf_ref, dsb_ref):
        i = pl.program_id(1)

        @pl.when(i == 0)
        def _():
            dk_ref[...] = jnp.zeros_like(dk_ref)
            dv_ref[...] = jnp.zeros_like(dv_ref)
            dgn_ref[...] = jnp.zeros_like(dgn_ref)
            dlg_ref[...] = jnp.zeros_like(dlg_ref)

        rows = pl.ds(pl.multiple_of(i * TQ, TQ), TQ)
        cs, sn = cos_ref[rows, :], sin_ref[rows, :]

        def one_head(h):
            ln = slice(h * RET_DK, (h + 1) * RET_DK)
            lgf, lgb = _head_scalar(df_ref, h), _head_scalar(db_ref, h)
            tau, dist, dm = _chunk_decay(lgf, lgb)

            def add_lg(row, x):
                csum = jnp.sum(x, axis=0, keepdims=True)
                tot = csum[:, 0:128]
                for part in range(1, x.shape[1] // 128):
                    tot = tot + csum[:, part * 128:(part + 1) * 128]
                dlg_ref[h, row:row + 1, :] += tot

            q = q_ref[:, ln]
            qf = q.astype(F32)
            o = o_ref[:, ln]
            g = g_ref[:, ln].astype(F32)
            dy = dy_ref[:, ln].astype(F32)
            gn = gn_ref[:, ln]
            sg = _sigmoid(g)
            rn = lax.rsqrt(jnp.mean(o * o, axis=-1, keepdims=True) + EPS)
            nrm = o * rn
            dg_ref[:, ln] = (dy * (nrm * gn) * (sg * (1.0 + g * (1.0 - sg)))).astype(BF16)
            dhn = dy * (g * sg)
            dgn_ref[:, ln] += jnp.sum(dhn * nrm, axis=0, keepdims=True)
            dnrm = dhn * gn
            do = rn * (dnrm - nrm * jnp.mean(dnrm * nrm, axis=-1, keepdims=True))
            dob = do.astype(BF16)
            ki, vi = k_ref[rows, ln], v_ref[rows, ln]
            s = _dot_nt(q, ki)
            dsv = _dot_nt(dob, vi)
            dsb = (dsv * dm).astype(BF16)
            dk_ref[rows, ln] += _dot_tn(dsb, q)
            dv_ref[rows, ln] += _dot_tn((s * dm).astype(BF16), dob)
            xw = s * dsv * dm * jnp.abs(dist)
            fpart = jnp.where(dist > 0, xw, 0.0)
            add_lg(0, fpart)
            add_lg(1, xw - fpart)
            dq = _dot(dsb, ki)
            af, ab = jnp.exp(lgf * (tau + 1.0)), jnp.exp(lgb * (C - tau))
            qa, qb = (qf * af).astype(BF16), (qf * ab).astype(BF16)
            sfi, sbi = sf_ref[h, i].astype(BF16), sb_ref[h, i].astype(BF16)
            dq = dq + af * _dot_nt(dob, sfi) + ab * _dot_nt(dob, sbi)
            dsf_ref[h, i] = _dot_tn(qa, dob)
            dsb_ref[h, i] = _dot_tn(qb, dob)
            add_lg(0, (tau + 1.0) * (_dot(qa, sfi) * do))
            add_lg(1, (C - tau) * (_dot(qb, sbi) * do))
            dq_ref[:, ln] = (dq * cs - pltpu.roll(dq, 64, 1) * sn).astype(BF16)

            @pl.when(i == n - 1)
            def _():
                jc = lax.broadcasted_iota(jnp.int32, (LC, 1), 0).astype(F32)
                crow = pl.ds(L, LC)

                def through_state(rws, w, dw, gst, row):
                    kk, vv = k_ref[rws, ln].astype(F32), v_ref[rws, ln]
                    gb = gst.astype(BF16)
                    vg = _dot_nt(vv, gb)
                    kw = kk * w
                    dk_ref[rws, ln] += w * vg
                    dv_ref[rws, ln] += _dot(kw.astype(BF16), gb)
                    add_lg(row, dw * (kw * vg))

                def scan(gc, w, dw, st_ref, dst_ref, order, row):
                    def step(r, gst):
                        j = order(r)
                        through_state(pl.ds(pl.multiple_of(j * TQ, TQ), TQ), w, dw, gst, row)
                        add_lg(row, (C * gc) * (gst * st_ref[h, j]))
                        return dst_ref[h, j] + gc * gst
                    return lax.fori_loop(0, n, step, jnp.zeros((RET_DK, RET_DK), F32), unroll=True)

                gcf, gcb = jnp.exp(lgf * C), jnp.exp(lgb * C)
                g0 = scan(gcf, jnp.exp(lgf * (C - 1.0 - tau)), C - 1.0 - tau, sf_ref, dsf_ref,
                          lambda r: n - 1 - r, 0)
                through_state(crow, jnp.exp(lgf * (LC - 1.0 - jc)), LC - 1.0 - jc, g0, 0)
                g1 = scan(gcb, jnp.exp(lgb * tau), tau, sb_ref, dsb_ref, lambda r: r, 1)
                through_state(crow, jnp.exp(lgb * jc), jc, g1, 1)
                dk = dk_ref[:, ln]
                dk_ref[:, ln] = (dk * cos_ref[...] - pltpu.roll(dk, 64, 1) * sin_ref[...]) * kscale

        for h in range(4):
            one_head(h)

    sec = lambda k: pl.BlockSpec((None, TQ, 512), lambda b, i: (b, i, k))
    full = lambda k: pl.BlockSpec((None, T, 512), lambda b, i: (b, 0, k))
    dec_spec = pl.BlockSpec((1, 4), lambda b, i: (0, 0))
    tab = pl.BlockSpec((T, RET_DK), lambda b, i: (0, 0))
    return pl.pallas_call(
        body, name="ret_bwd", grid=(B, n),
        in_specs=[dec_spec, dec_spec, sec(4), full(5), full(6), sec(7),
                  pl.BlockSpec((1, 512), lambda b, i: (0, 0)), sec(0), sec(1), tab, tab, st_spec, st_spec],
        out_specs=(sec(0), sec(0), full(0), full(0),
                   pl.BlockSpec((None, 1, 512), lambda b, i: (b, 0, 0)),
                   pl.BlockSpec((None, 4, 8, 128), lambda b, i: (b, 0, 0, 0))),
        out_shape=(jax.ShapeDtypeStruct((B, L, 512), BF16), jax.ShapeDtypeStruct((B, L, 512), BF16),
                   jax.ShapeDtypeStruct((B, T, 512), F32), jax.ShapeDtypeStruct((B, T, 512), F32),
                   jax.ShapeDtypeStruct((B, 1, 512), F32), jax.ShapeDtypeStruct((B, 4, 8, 128), F32)),
        scratch_shapes=[pltpu.VMEM((4, n, RET_DK, RET_DK), F32), pltpu.VMEM((4, n, RET_DK, RET_DK), F32)],
        compiler_params=_params(("arbitrary",) * 2, vmem_mb=56))(
            dec_f, dec_b, P, P, P, P, ret_norm_g, o_ret, dY, cos2, sin2, sf, sb)


def _out_call(y_na, y_ret, x, target, mod, final_g, wout_f):
    B, L, _ = x.shape
    TO = 2 * TQ

    def body(yn_ref, yr_ref, x_ref, t_ref, mod_ref, gf_ref, w_ref, dy_ref, dx2_ref, dwb_ref, sm_ref, dw_ref):
        b, i = pl.program_id(0), pl.program_id(1)

        @pl.when((b == 0) & (i == 0))
        def _():
            dw_ref[...] = jnp.zeros_like(dw_ref)
            sm_ref[...] = jnp.zeros_like(sm_ref)

        gate = mod_ref[pl.ds(b, 1), 2 * D:3 * D]
        gf = gf_ref[...]
        yn, yr = yn_ref[...], yr_ref[...]
        ylat = _dot(yn, w_ref[0:512, :]) + _dot(yr, w_ref[512:1024, :])
        x2 = x_ref[...] + gate * ylat
        r = lax.rsqrt(jnp.mean(x2 * x2, axis=-1, keepdims=True) + EPS)
        xr = x2 * r
        err = xr * gf - t_ref[...]
        sm_ref[1:2, :] += jnp.sum(err * err, axis=0, keepdims=True)
        dout = err * (1.0 / D)
        sm_ref[0:1, :] += jnp.sum(dout * xr, axis=0, keepdims=True)
        gd = dout * gf
        dx2 = r * (gd - xr * jnp.mean(gd * xr, axis=-1, keepdims=True))
        dx2_ref[...] = dx2
        sm_ref[pl.ds(2 + b, 1), :] += jnp.sum(dx2 * ylat, axis=0, keepdims=True)
        dyl = (gate * dx2).astype(BF16)
        dy_ref[:, 0:512] = _dot_nt(dyl, w_ref[0:512, :]).astype(BF16)
        dy_ref[:, 512:1024] = _dot_nt(dyl, w_ref[512:1024, :]).astype(BF16)
        dw_ref[0:512, :] += _dot_tn(yn, dyl)
        dw_ref[512:1024, :] += _dot_tn(yr, dyl)

        @pl.when((b == B - 1) & (i == L // TO - 1))
        def _():
            dwb_ref[...] = dw_ref[...].astype(BF16)

    half = pl.BlockSpec((None, TO, 512), lambda b, i: (b, i, 0))
    full = pl.BlockSpec((None, TO, D), lambda b, i: (b, i, 0))
    return pl.pallas_call(
        body, name="out_proj_loss", grid=(B, L // TO),
        in_specs=[half, half, full, full,
                  pl.BlockSpec((8, 3 * D), lambda b, i: (0, 0)),
                  pl.BlockSpec((1, D), lambda b, i: (0, 0)),
                  pl.BlockSpec((D, D), lambda b, i: (0, 0))],
        out_specs=(full, full, pl.BlockSpec((D, D), lambda b, i: (0, 0)),
                   pl.BlockSpec((8, D), lambda b, i: (0, 0))),
        out_shape=(jax.ShapeDtypeStruct((B, L, D), BF16), jax.ShapeDtypeStruct((B, L, D), F32),
                   jax.ShapeDtypeStruct((D, D), BF16), jax.ShapeDtypeStruct((8, D), F32)),
        scratch_shapes=[pltpu.VMEM((D, D), F32)],
        compiler_params=_params(("arbitrary",) * 2))(y_na, y_ret, x, target, mod, final_g, wout_f)


def _dh_call(dsec, win_f, x, ctx, dx2, mod, norm_g, cp_in, cp_out):
    B, L, _ = x.shape
    LC = ctx.shape[1]
    nl = L // TQ

    def body(d0, d1, d2, d3, d4, d5, d6, d7, w_ref, x_ref, ctx_ref, dx2_ref, mod_ref, g_ref, cpi_ref, cpo_ref,
             gx_ref, sm_ref, sli_ref, slo_ref, ssem, rsem, lsem):
        drefs = (d0, d1, d2, d3, d4, d5, d6, d7)
        b, t = pl.program_id(0), pl.program_id(1)
        is_lat = t < nl

        @pl.when((b == 0) & (t == 0))
        def _():
            sm_ref[...] = jnp.zeros_like(sm_ref)

        def dh_of(secs):
            acc = jnp.zeros((TQ, D), F32)
            for sec in secs:
                s, half = divmod(sec, 2)
                acc = acc + _dot_nt(drefs[sec][...].astype(BF16), w_ref[s, :, half * 512:(half + 1) * 512])
            return acc

        def norm_bwd(dh, xt, mrow):
            scale = mrow[:, D:2 * D]
            g = g_ref[...]
            rstd = lax.rsqrt(jnp.mean(xt * xt, axis=-1, keepdims=True) + EPS)
            xn = xt * rstd
            dshift = jnp.sum(dh, axis=0, keepdims=True)
            dscale = jnp.sum(dh * (xn * g), axis=0, keepdims=True)
            dhn = dh * (1.0 + scale)
            sm_ref[0:1, :] += jnp.sum(dhn * xn, axis=0, keepdims=True)
            dxn = dhn * g
            dx = rstd * (dxn - xn * jnp.mean(dxn * xn, axis=-1, keepdims=True))
            return dshift, dscale, dx

        @pl.when(is_lat)
        def _():
            dshift, dscale, dx = norm_bwd(dh_of(range(8)), x_ref[...], mod_ref[pl.ds(b, 1), :])
            sm_ref[pl.ds(3 + b, 1), :] += dshift
            sm_ref[pl.ds(3 + B + b, 1), :] += dscale
            gx_ref[...] = dx2_ref[...] + dx

        @pl.when(jnp.logical_not(is_lat))
        def _():
            dshift, dscale, _ = norm_bwd(dh_of((1, 2, 5, 6)), ctx_ref[...], mod_ref[B:B + 1, :])
            sm_ref[1:2, :] += dshift
            sm_ref[2:3, :] += dscale

        mx, my, mc = _mesh_pos()
        s = 2 * mx + my
        cps, sls = (cpi_ref, cpo_ref), (sli_ref, slo_ref)
        own = [pltpu.make_async_copy(cps[a].at[s], sls[a].at[s], lsem.at[a]) for a in range(2)]
        sends, recvs, k = [], [], 0
        for px, py in _other_chips(mx, my):
            ps = 2 * px + py
            for a in range(2):
                sends.append(_remote(cps[a].at[ps], sls[a].at[s], ssem, rsem, k, (px, py, mc)))
                recvs.append(_remote(cps[a].at[s], sls[a].at[ps], ssem, rsem, k, (px, py, mc)))
                k += 1

        @pl.when((b == 0) & (t == 0))
        def _():
            for cp in own + sends:
                cp.start()

        @pl.when((b == B - 1) & (t == nl))
        def _():
            _finish(own, sends, recvs)

    lat = lambda b, t: (b, jnp.minimum(t, nl - 1), 0)
    tok = lambda b, t: (b, t, 0)
    sec_specs = [pl.BlockSpec((None, TQ, 512), lat if sec in (0, 3, 4, 7) else tok) for sec in range(8)]
    return pl.pallas_call(
        body, name="dh_norm_bwd", grid=(B, nl + 1),
        in_specs=sec_specs + [
            pl.BlockSpec((N_SHARD, D, D), lambda b, t: (0, 0, 0)),
            pl.BlockSpec((None, TQ, D), lat),
            pl.BlockSpec((None, LC, D), lambda b, t: (b, 0, 0)),
            pl.BlockSpec((None, TQ, D), lat),
            pl.BlockSpec((8, 3 * D), lambda b, t: (0, 0)),
            pl.BlockSpec((1, D), lambda b, t: (0, 0)), ANY, ANY],
        out_specs=(pl.BlockSpec((None, TQ, D), lat), pl.BlockSpec((8, D), lambda b, t: (0, 0)), ANY, ANY),
        out_shape=(jax.ShapeDtypeStruct((B, L, D), F32), jax.ShapeDtypeStruct((8, D), F32),
                   jax.ShapeDtypeStruct(cp_in.shape, cp_in.dtype), jax.ShapeDtypeStruct(cp_out.shape, cp_out.dtype)),
        scratch_shapes=[pltpu.SemaphoreType.DMA((6,)), pltpu.SemaphoreType.DMA((6,)),
                        pltpu.SemaphoreType.DMA((2,))],
        compiler_params=_params(("arbitrary",) * 2))(*dsec, win_f, x, ctx, dx2, mod, norm_g, cp_in, cp_out)


def _dw_call(dsec, h, L):
    B, T, _ = h.shape
    TW = 2 * TQ
    nl = L // TW
    KV = (1, 2, 5, 6)

    def body(d0, d1, d2, d3, d4, d5, d6, d7, c1, c2, c5, c6, h_ref, hc_ref, dw_ref, acc_ref):
        drefs = (d0, d1, d2, d3, d4, d5, d6, d7)
        crefs = dict(zip(KV, (c1, c2, c5, c6)))
        b, t = pl.program_id(0), pl.program_id(1)

        @pl.when((b == 0) & (t == 0))
        def _():
            acc_ref[...] = jnp.zeros_like(acc_ref)

        def add(hb, refs, secs):
            for sec in secs:
                s, half = divmod(sec, 2)
                acc_ref[s, :, half * 512:(half + 1) * 512] += _dot_tn(hb, refs[sec][...].astype(BF16))

        @pl.when(t < nl)
        def _():
            add(h_ref[...], drefs, range(8))

        @pl.when(t == nl)
        def _():
            add(hc_ref[...], crefs, KV)

        @pl.when((b == B - 1) & (t == nl))
        def _():
            dw_ref[...] = acc_ref[...].astype(BF16)

    lat = lambda b, t: (b, jnp.minimum(t, nl - 1), 0)
    ctx = lambda b, t: (b, L // TQ, 0)
    return pl.pallas_call(
        body, name="dw_in", grid=(B, nl + 1),
        in_specs=[pl.BlockSpec((None, TW, 512), lat)] * 8 + [pl.BlockSpec((None, TQ, 512), ctx)] * 4
        + [pl.BlockSpec((None, TW, D), lat), pl.BlockSpec((None, TQ, D), ctx)],
        out_specs=pl.BlockSpec((N_SHARD, D, D), lambda b, t: (0, 0, 0)),
        out_shape=jax.ShapeDtypeStruct((N_SHARD, D, D), BF16),
        scratch_shapes=[pltpu.VMEM((N_SHARD, D, D), F32)],
        compiler_params=_params(("arbitrary",) * 2, vmem_mb=60))(*dsec, *[dsec[k] for k in KV], h, h)


def _mesh_pos():
    return lax.axis_index("x"), lax.axis_index("y"), lax.axis_index("c")


def _flip(v, f):
    return 1 - v if f else v


def _remote(src, dst, ssem, rsem, k, peer):
    return pltpu.make_async_remote_copy(src_ref=src, dst_ref=dst, send_sem=ssem.at[k], recv_sem=rsem.at[k],
                                        device_id=peer, device_id_type=MESH)


def _other_chips(x, y):
    return [(_flip(x, fx), _flip(y, fy)) for fx, fy in ((1, 0), (0, 1), (1, 1))]


def _gather_copies(own_ref, all_ref, out_ref, hr, sems, k0, l0):
    ssem, rsem, lsem = sems
    mx, my, mc = _mesh_pos()
    s = 2 * mx + my
    sib = (mx, my, 1 - mc)
    own = pltpu.make_async_copy(own_ref, all_ref.at[s], lsem.at[l0])
    send, recv, fsend, frecv = [], [], [], []
    outs = [pltpu.make_async_copy(all_ref.at[s], out_ref.at[s], lsem.at[l0 + 1])]
    for k, (px, py) in enumerate(_other_chips(mx, my)):
        ps = 2 * px + py
        mine = all_ref.at[s, pl.ds(mc * hr, hr)]
        send.append(_remote(mine, mine, ssem, rsem, k0 + k, (px, py, mc)))
        got = all_ref.at[ps, pl.ds(mc * hr, hr)]
        recv.append(_remote(mine, got, ssem, rsem, k0 + k, (px, py, mc)))
        fsend.append(_remote(got, got, ssem, rsem, k0 + 3 + k, sib))
        theirs = all_ref.at[ps, pl.ds((1 - mc) * hr, hr)]
        frecv.append(_remote(theirs, theirs, ssem, rsem, k0 + 3 + k, sib))
        outs.append(pltpu.make_async_copy(all_ref.at[ps], out_ref.at[ps], lsem.at[l0 + 2 + k]))
    return own, send, recv, fsend, frecv, outs


def _all_to_all_small(src, dst_all, ssem, rsem, k0, x, y, cc):
    me = 4 * x + 2 * y + cc
    sends, recvs = [], []
    for f in range(1, N_DEV):
        px, py, pc = _flip(x, f & 4), _flip(y, f & 2), _flip(cc, f & 1)
        sends.append(_remote(src, dst_all.at[me], ssem, rsem, k0 + f - 1, (px, py, pc)))
        recvs.append(_remote(src, dst_all.at[4 * px + 2 * py + pc], ssem, rsem, k0 + f - 1, (px, py, pc)))
    return sends, recvs


def _finish(local, sends, recvs):
    for cp in recvs:
        cp.wait_recv()
    for cp in sends:
        cp.wait_send()
    for cp in local:
        cp.wait()


def _c_gather_call(c, rpb_flat):
    def body(c_ref, r_ref, c_all, bias_out, bias_ref, et_ref, ssem, rsem, lsem):
        x, y, cc = _mesh_pos()
        me = 4 * x + 2 * y + cc
        local = [pltpu.make_async_copy(c_ref, c_all.at[me], lsem.at[0])]
        c_send, c_recv = _all_to_all_small(c_ref, c_all, ssem, rsem, 0, x, y, cc)
        for cp in local + c_send:
            cp.start()
        bias_out_copies = _bias_body(r_ref, bias_ref, et_ref, bias_out, lsem.at[1])
        _finish(local + bias_out_copies, c_send, c_recv)

    bias_shape = (rpb_flat.shape[0], 3, TQ, KW)
    return pl.pallas_call(
        body, name="c_gather",
        in_specs=[pl.BlockSpec(memory_space=pltpu.VMEM), pl.BlockSpec(memory_space=pltpu.SMEM)],
        out_specs=(pl.BlockSpec(memory_space=pltpu.VMEM), ANY),
        out_shape=(jax.ShapeDtypeStruct((N_DEV,) + c.shape, c.dtype), jax.ShapeDtypeStruct(bias_shape, F32)),
        scratch_shapes=[pltpu.VMEM(bias_shape, F32), pltpu.VMEM((15, GRID_W, GRID_W), F32),
                        pltpu.SemaphoreType.DMA((N_DEV - 1,)), pltpu.SemaphoreType.DMA((N_DEV - 1,)),
                        pltpu.SemaphoreType.DMA((2,))],
        compiler_params=pltpu.CompilerParams(vmem_limit_bytes=56 << 20))(c, rpb_flat)


VROWS = 32


def _grad_halves_call(dwin_b, dwout_b, dbias, dlg):
    arrs = (dwin_b, dwout_b)
    hrs = [a.shape[1] // 2 for a in arrs]

    def body(din, dout, db_ref, dlg_ref, cp_in, cp_out, drpb_ref, dlgo_ref, got_in, got_out, p_ref, ssem, rsem):
        x, y, cc = _mesh_pos()
        sib = (x, y, 1 - cc)
        srcs, gots, cps = (din, dout), (got_in, got_out), (cp_in, cp_out)
        halves = [_remote(srcs[a].at[:, pl.ds((1 - cc) * hrs[a], hrs[a])], gots[a], ssem, rsem, a, sib)
                  for a in range(2)]
        for cp in halves:
            cp.start()
        _small_reduce_body(db_ref, dlg_ref, drpb_ref, dlgo_ref, p_ref)
        for cp in halves:
            cp.wait_recv()
        for a in range(2):
            for j in range(N_SHARD):
                def add(i, carry, a=a, j=j):
                    r = pl.multiple_of(i * VROWS, VROWS)
                    mine = srcs[a][j, pl.ds(pl.multiple_of(cc * hrs[a] + r, VROWS), VROWS), :].astype(F32)
                    cps[a][j, pl.ds(r, VROWS), :] = (
                        mine + gots[a][j, pl.ds(r, VROWS), :].astype(F32)).astype(BF16)
                    return carry
                lax.fori_loop(0, hrs[a] // VROWS, add, 0)
        for cp in halves:
            cp.wait_send()

    vmem = pl.BlockSpec(memory_space=pltpu.VMEM)
    half_shapes = [(N_SHARD, hrs[a], arrs[a].shape[2]) for a in range(2)]
    return pl.pallas_call(
        body, name="grad_halves",
        in_specs=[vmem] * 4, out_specs=(vmem,) * 4,
        out_shape=(jax.ShapeDtypeStruct(half_shapes[0], BF16), jax.ShapeDtypeStruct(half_shapes[1], BF16),
                   jax.ShapeDtypeStruct((dbias.shape[0], 16, 32), F32), jax.ShapeDtypeStruct((32, 128), F32)),
        scratch_shapes=[pltpu.VMEM(half_shapes[0], BF16), pltpu.VMEM(half_shapes[1], BF16),
                        pltpu.VMEM((32, GRID_W), F32),
                        pltpu.SemaphoreType.DMA((2,)), pltpu.SemaphoreType.DMA((2,))],
        compiler_params=pltpu.CompilerParams(vmem_limit_bytes=56 << 20))(dwin_b, dwout_b, dbias, dlg)


def _grad_finish_call(sl_in, sl_out, small, wada_b):
    arrs = (sl_in, sl_out)
    ws = wada_b.shape[1]

    def body(sin, sout, sm, wa_ref, gin, gout, sm_all, dparts, h_in, h_out, dp_own, ssem, rsem, lsem):
        x, y, cc = _mesh_pos()
        me = 4 * x + 2 * y + cc
        s = 2 * x + y
        sib = (x, y, 1 - cc)
        sls, hs, gs = (sin, sout), (h_in, h_out), (gin, gout)
        sm_send, sm_recv = _all_to_all_small(sm, sm_all, ssem, rsem, 2, x, y, cc)
        sm_own = pltpu.make_async_copy(sm, sm_all.at[me], lsem.at[0])
        for cp in sm_send + [sm_own]:
            cp.start()
        for a in range(2):
            def total(i, carry, a=a):
                rows = pl.ds(pl.multiple_of(i * VROWS, VROWS), VROWS)
                sl = sls[a]
                hs[a][rows, :] = ((sl[0, rows, :].astype(F32) + sl[1, rows, :].astype(F32))
                                  + sl[2, rows, :].astype(F32)) + sl[3, rows, :].astype(F32)
                return carry
            lax.fori_loop(0, arrs[a].shape[1] // VROWS, total, 0)
        mine = [pltpu.make_async_copy(hs[a], gs[a].at[cc], lsem.at[1 + a]) for a in range(2)]
        back = [_remote(hs[a], gs[a].at[cc], ssem, rsem, a, sib) for a in range(2)]
        back_recv = [_remote(hs[a], gs[a].at[1 - cc], ssem, rsem, a, sib) for a in range(2)]
        for cp in mine + back:
            cp.start()
        sm_own.wait()
        for cp in sm_recv:
            cp.wait_recv()
        shift_c = sm_all[0, R_SHIFT_C:R_SHIFT_C + 1, :]
        scale_c = sm_all[0, R_SCALE_C:R_SCALE_C + 1, :]
        for dv in range(1, N_DEV):
            shift_c = shift_c + sm_all[dv, R_SHIFT_C:R_SHIFT_C + 1, :]
            scale_c = scale_c + sm_all[dv, R_SCALE_C:R_SCALE_C + 1, :]
        dmc = jnp.concatenate([shift_c, scale_c, jnp.zeros((1, D), F32)], axis=1).astype(BF16)
        dmc = jnp.broadcast_to(dmc, (8, 3 * D))
        for sh in range(N_SHARD):
            @pl.when(s == sh)
            def _(sh=sh):
                dp_own[...] = _dot_nt(dmc[:, sh * ws:(sh + 1) * ws], wa_ref[...])
        dparts[s] = dp_own[...]
        d_send = [_remote(dp_own, dparts.at[s], ssem, rsem, 9 + k, (px, py, cc))
                  for k, (px, py) in enumerate(_other_chips(x, y))]
        d_recv = [_remote(dp_own, dparts.at[2 * px + py], ssem, rsem, 9 + k, (px, py, cc))
                  for k, (px, py) in enumerate(_other_chips(x, y))]
        for cp in d_send:
            cp.start()
        _finish(mine, back + sm_send + d_send, back_recv + d_recv)

    vmem = pl.BlockSpec(memory_space=pltpu.VMEM)
    return pl.pallas_call(
        body, name="grad_finish",
        in_specs=[vmem] * 4, out_specs=(vmem,) * 4,
        out_shape=(jax.ShapeDtypeStruct((2,) + sl_in.shape[1:], F32),
                   jax.ShapeDtypeStruct((2,) + sl_out.shape[1:], F32),
                   jax.ShapeDtypeStruct((N_DEV,) + small.shape, F32),
                   jax.ShapeDtypeStruct((N_SHARD, 8, D), F32)),
        scratch_shapes=[pltpu.VMEM(sl_in.shape[1:], F32), pltpu.VMEM(sl_out.shape[1:], F32),
                        pltpu.VMEM((8, D), F32),
                        pltpu.SemaphoreType.DMA((12,)), pltpu.SemaphoreType.DMA((12,)),
                        pltpu.SemaphoreType.DMA((3,))],
        compiler_params=pltpu.CompilerParams(vmem_limit_bytes=48 << 20))(sl_in, sl_out, small, wada_b)


def _adamw(w, g, m, v):
    m = ADAM_B1 * m + (1.0 - ADAM_B1) * g
    v = ADAM_B2 * v + (1.0 - ADAM_B2) * (g * g)
    m_hat = m / (1.0 - ADAM_B1 ** ADAM_STEP)
    v_hat = v / (1.0 - ADAM_B2 ** ADAM_STEP)
    return -ADAM_LR * (m_hat / (jnp.sqrt(v_hat) + ADAM_EPS) + ADAM_WD * w), m, v


def _adam_call(w, m, v, g, name):
    R, C = w.shape
    tr = 256

    def body(w_ref, m_ref, v_ref, g_ref, go_ref, d_ref, mo_ref, vo_ref):
        g = g_ref[...]
        go_ref[...] = g
        d_ref[...], mo_ref[...], vo_ref[...] = _adamw(w_ref[...], g, m_ref[...], v_ref[...])

    spec = pl.BlockSpec((tr, C), lambda i: (i, 0))
    return pl.pallas_call(
        body, name=name, grid=(R // tr,), in_specs=[spec] * 4,
        out_specs=(spec,) * 4, out_shape=(jax.ShapeDtypeStruct((R, C), F32),) * 4,
        compiler_params=_params(("arbitrary",)))(w, m, v, g)


R_GF, R_NG, R_LOSS, R_RNG, R_LGF, R_LGB, R_SHIFT, R_SCALE, R_GATE, R_SHIFT_C, R_SCALE_C, R_RNG2, R_RPB = (
    0, 1, 2, 3, 4, 5, 6, 8, 10, 12, 13, 14, 16)
W_GF, W_NG, W_CCTX, W_RNG, W_DF, W_DB, W_BADA, W_RPB = 0, 1, 2, 3, 4, 5, 6, 9


SMALL = (("final_norm_g", W_GF, 1, D), ("norm_g", W_NG, 1, D), ("c_ctx", W_CCTX, 1, D),
         ("ret_norm_g", W_RNG, 1, 512), ("ret_decay_fwd", W_DF, 1, 4), ("ret_decay_bwd", W_DB, 1, 4),
         ("b_ada", W_BADA, 3, D), ("na_rpb", W_RPB, 4, D))
N_SMALL = len(SMALL)


def _small_final_call(sm_all, c_t, dact_parts, wada, m_ada, v_ada, small_w, small_m, small_v, B):
    ws = wada.shape[1]
    NB = N_DEV * B

    def body(*refs):
        sm_ref, ct_ref, wf_ref, wa_ref, ma_ref, va_ref = refs[:6]
        ins = refs[6:6 + 3 * N_SMALL]
        outs = refs[6 + 3 * N_SMALL:6 + 7 * N_SMALL]
        ga_ref, da_ref, mao_ref, vao_ref, loss_ref, dmod_ref, pk_ref = refs[6 + 7 * N_SMALL:]
        x, y, _ = _mesh_pos()
        s = 2 * x + y
        tot = sm_ref[0]
        for dv in range(1, N_DEV):
            tot = tot + sm_ref[dv]
        pk_ref[...] = jnp.zeros_like(pk_ref)
        for kind in range(3):
            for i, (_, row, nrow, width) in enumerate(SMALL):
                ref = ins[kind * N_SMALL + i]
                if nrow == 3:
                    for part in range(3):
                        pk_ref[kind, row + part:row + part + 1, :] = ref[:, part * D:(part + 1) * D]
                else:
                    pk_ref[kind, row:row + nrow, 0:width] = ref[...]
        w = pk_ref[0]
        cctx_ref = ins[2]
        for dv in range(N_DEV):
            for b in range(B):
                r = dv * B + b
                for part, row in enumerate((R_SHIFT, R_SCALE, R_GATE)):
                    dmod_ref[r:r + 1, part * D:(part + 1) * D] = sm_ref[dv, row + b:row + b + 1, :]
        dmod_ref[NB:NB + 1, 0:D] = tot[R_SHIFT_C:R_SHIFT_C + 1, :]
        dmod_ref[NB:NB + 1, D:2 * D] = tot[R_SCALE_C:R_SCALE_C + 1, :]
        dmod_ref[NB:NB + 1, 2 * D:3 * D] = jnp.zeros((1, D), F32)
        dmod_ref[NB + 1:, :] = jnp.zeros((dmod_ref.shape[0] - NB - 1, 3 * D), F32)
        dmod = dmod_ref[...]
        cc = cctx_ref[...]
        scc = _sigmoid(cc)
        ct = ct_ref[...]
        act_t = ct * _sigmoid(ct)
        dact = wf_ref[0, 0:1, :]
        for sh in range(1, N_SHARD):
            dact = dact + wf_ref[sh, 0:1, :]
        g = jnp.zeros((16, D), F32)
        rows = lax.broadcasted_iota(jnp.int32, (16, D), 0)

        def put(g, row, val):
            return jnp.where(rows == row, val, g)

        g = put(g, W_GF, tot[R_GF:R_GF + 1, :])
        g = put(g, W_NG, tot[R_NG:R_NG + 1, :])
        g = put(g, W_CCTX, dact * (scc * (1.0 + cc * (1.0 - scc))))
        g = put(g, W_RNG, tot[R_RNG:R_RNG + 1, :] + tot[R_RNG2:R_RNG2 + 1, :])
        g = put(g, W_DF, tot[R_LGF:R_LGF + 1, :] * (-jnp.exp(w[W_DF:W_DF + 1, :])))
        g = put(g, W_DB, tot[R_LGB:R_LGB + 1, :] * (-jnp.exp(w[W_DB:W_DB + 1, :])))
        db = jnp.sum(dmod, axis=0, keepdims=True)
        for part in range(3):
            g = put(g, W_BADA + part, db[:, part * D:(part + 1) * D])
        for part in range(4):
            g = put(g, W_RPB + part, tot[R_RPB + part:R_RPB + part + 1, :])
        for kind, val in enumerate((g,) + _adamw(w, g, pk_ref[1], pk_ref[2])):
            for i, (_, row, nrow, width) in enumerate(SMALL):
                out = outs[kind * N_SMALL + i]
                if nrow == 3:
                    for part in range(3):
                        out[:, part * D:(part + 1) * D] = val[row + part:row + part + 1, :]
                else:
                    out[...] = val[row:row + nrow, 0:width]
        loss_ref[...] = jnp.broadcast_to(
            (0.5 / D) * jnp.sum(tot[R_LOSS:R_LOSS + 1, :], axis=1, keepdims=True), (8, 128))
        for sh in range(N_SHARD):
            @pl.when(s == sh)
            def _():
                ga = jnp.dot(act_t, dmod[:, sh * ws:(sh + 1) * ws], precision=HIGHEST,
                             preferred_element_type=F32)
                ga_ref[...] = ga
                da_ref[...], mao_ref[...], vao_ref[...] = _adamw(wa_ref[...], ga, ma_ref[...], va_ref[...])

    sh_small = tuple(jax.ShapeDtypeStruct(a.shape, F32) for a in small_w)
    sh_ada = jax.ShapeDtypeStruct(wada.shape, F32)
    res = pl.pallas_call(
        body, name="small_final",
        out_shape=sh_small * 4 + (sh_ada,) * 4 + (jax.ShapeDtypeStruct((8, 128), F32),),
        scratch_shapes=[pltpu.VMEM((NB + 8, 3 * D), F32), pltpu.VMEM((3, 16, D), F32)],
        compiler_params=_params(vmem_mb=56))(
            sm_all, c_t, dact_parts, wada, m_ada, v_ada, *small_w, *small_m, *small_v)
    smalls = [res[k * N_SMALL:(k + 1) * N_SMALL] for k in range(4)]
    return smalls, res[4 * N_SMALL:4 * N_SMALL + 4], res[4 * N_SMALL + 4]


def _local_step(order, x, ctx, c_rows, norm_g, wada_b, b_shard, win_b, bias, dec_f, dec_b, ret_norm_g,
                wout_b, final_g, target):
    B, L, _ = x.shape
    LC = ctx.shape[1]
    assert B == 2
    cos2, sin2 = _rope_tables(L, LC)
    mod_part = _mod_part_call(c_rows, wada_b, b_shard)
    P, h, win_f, wout_f, mod = _inproj_gather_call(order, x, ctx, mod_part, norm_g, win_b, wout_b, cos2, sin2)
    y_na, o_na = _na_fwd_call(P, bias, L, LC)
    sf, sb = _ret_states_call(P, dec_f, dec_b, L, LC)
    y_ret, o_ret = _retc_fwd_call(P, sf, sb, dec_f, dec_b, ret_norm_g, L)
    dY, dx2, dwout_p, sm_out = _out_call(y_na, y_ret, x, target, mod, final_g, wout_f.reshape(D, D))
    dnq, dng, dnk, dnv, dbias = _na_bwd_call(P, bias, dY, o_na, L, LC)
    drq, drg, drk, drv, dgn, dlg = _retc_bwd_call(P, sf, sb, dec_f, dec_b, ret_norm_g, o_ret, dY, cos2, sin2, L, LC)
    dsec = (dnq, dnk, dnv, dng, drq, drk, drv, drg)
    dwin_b = _dw_call(dsec, h, L)
    cp_in, cp_out, drpb, dlg_sum = _grad_halves_call(
        dwin_b, dwout_p.reshape(N_SHARD, D // N_SHARD, D), dbias, dlg)
    grad_x, sm_dh, sl_in, sl_out = _dh_call(dsec, win_f, x, ctx, dx2, mod, norm_g, cp_in, cp_out)
    z = jnp.zeros((1, D), F32)
    pad = lambda v: jnp.pad(v.reshape(1, -1), ((0, 0), (0, D - v.size)))
    dlg_sum = dlg_sum.reshape(4, 8, 128)
    rpb_rows = jnp.pad(drpb[:, :15, :31].reshape(-1), (0, 4 * D - drpb.shape[0] * 465)).reshape(4, D)
    small = jnp.concatenate([
        sm_out[0:1], sm_dh[0:1], sm_out[1:2], pad(dgn[0]), pad(dlg_sum[:, 0, 0]), pad(dlg_sum[:, 1, 0]),
        sm_dh[3:5], sm_dh[5:7], sm_out[2:4], sm_dh[1:2], sm_dh[2:3], pad(dgn[1]), z, rpb_rows,
        jnp.zeros((SM_ROWS - 20, D), F32)], axis=0)
    return grad_x, sl_in, sl_out, small


def kernel(x, c, ctx, c_ctx, norm_g, w_ada, b_ada, w_in, na_rpb, ret_decay_fwd, ret_decay_bwd, ret_norm_g, w_out, final_norm_g, loss_target, m_c_ctx, m_norm_g, m_w_ada, m_b_ada, m_w_in, m_na_rpb, m_ret_decay_fwd, m_ret_decay_bwd, m_ret_norm_g, m_w_out, m_final_norm_g, v_c_ctx, v_norm_g, v_w_ada, v_b_ada, v_w_in, v_na_rpb, v_ret_decay_fwd, v_ret_decay_bwd, v_ret_norm_g, v_w_out, v_final_norm_g):
    B = x.shape[0]
    c_all, bias = _c_gather_call(c, na_rpb[0].reshape(na_rpb.shape[1], -1))
    c_rows = jnp.concatenate([c_all.reshape(N_DEV * B, D), c_ctx.reshape(1, D), jnp.zeros((7, D), F32)], axis=0)
    mx, my = lax.axis_index("x"), lax.axis_index("y")
    order = jnp.stack([2 * mx + my, 2 * (1 - mx) + my, 2 * mx + (1 - my),
                       2 * (1 - mx) + (1 - my)]).astype(jnp.int32)
    ws = w_ada.shape[2]
    b_shard = lax.dynamic_slice(b_ada, (0, (2 * mx + my) * ws), (1, ws))
    wada_b = w_ada[0].astype(BF16)
    grad_x, sl_in, sl_out, small = _local_step(
        order, x, ctx, c_rows, norm_g, wada_b, b_shard, w_in[0].astype(BF16), bias, ret_decay_fwd,
        ret_decay_bwd, ret_norm_g, w_out[0].astype(BF16), final_norm_g.reshape(1, D), loss_target)
    gin, gout, sm_all, dact_parts = _grad_finish_call(sl_in, sl_out, small, wada_b)
    g_win, d_win, nm_win, nv_win = _adam_call(
        w_in[0], m_w_in[0], v_w_in[0], gin.reshape(w_in.shape[1:]), "adam_w_in")
    g_wout, d_wout, nm_wout, nv_wout = _adam_call(
        w_out[0], m_w_out[0], v_w_out[0], gout.reshape(w_out.shape[1:]), "adam_w_out")

    def small_inputs(gf, ng, cc, rng, df, db, bada, rpb):
        return (gf.reshape(1, D), ng, cc.reshape(1, D), rng, df, db, bada,
                jnp.pad(rpb.reshape(-1), (0, 4 * D - rpb.size)).reshape(4, D))

    c_t = c_rows.T
    smalls, adas, loss = _small_final_call(
        sm_all, c_t, dact_parts, w_ada[0], m_w_ada[0], v_w_ada[0],
        small_inputs(final_norm_g, norm_g, c_ctx, ret_norm_g, ret_decay_fwd, ret_decay_bwd, b_ada, na_rpb),
        small_inputs(m_final_norm_g, m_norm_g, m_c_ctx, m_ret_norm_g, m_ret_decay_fwd, m_ret_decay_bwd, m_b_ada,
                     m_na_rpb),
        small_inputs(v_final_norm_g, v_norm_g, v_c_ctx, v_ret_norm_g, v_ret_decay_fwd, v_ret_decay_bwd, v_b_ada,
                     v_na_rpb), B)
    res = []
    for p, ada, win_o, wout_o in zip(smalls, adas, (g_win, d_win, nm_win, nv_win),
                                     (g_wout, d_wout, nm_wout, nv_wout)):
        gf, ng, cc, rng, df, db, bada, rpb = p
        res.append([cc.reshape(D), ng, ada[None], bada, win_o[None],
                    rpb.reshape(-1)[:na_rpb.size].reshape(na_rpb.shape), df, db, rng, wout_o[None], gf.reshape(D)])
    return (loss[0, 0], grad_x, *res[0], *res[1], *res[2], *res[3])
```

```python
import numpy as np
import jax
import jax.numpy as jnp
from jax import lax
from jax.experimental import pallas as pl
from jax.experimental.pallas import tpu as pltpu

F32 = jnp.float32
BF16 = jnp.bfloat16
HIGHEST = lax.Precision.HIGHEST

D = 1024
GRID_W = 64
NA_DH = 64
RET_DK = 128
ROPE_BASE = 10000.0
EPS = 1e-6
NEG = -1e30
TQ = 256
KW = 12 * GRID_W
N_SHARD = 4
N_DEV = 8
SM_ROWS = 24

ADAM_LR = 0.001
ADAM_B1 = 0.9
ADAM_B2 = 0.999
ADAM_EPS = 1e-08
ADAM_WD = 0.01
ADAM_STEP = 10

MESH = pl.DeviceIdType.MESH
ANY = pl.BlockSpec(memory_space=pl.ANY)


def _params(sem=None, vmem_mb=48):
    return pltpu.CompilerParams(dimension_semantics=sem, vmem_limit_bytes=vmem_mb << 20)


def _dot(a, b):
    return jnp.dot(a, b, preferred_element_type=F32)


def _dot_nt(a, b):
    return lax.dot_general(a, b, (((1,), (1,)), ((), ())), preferred_element_type=F32)


def _dot_tn(a, b):
    return lax.dot_general(a, b, (((0,), (0,)), ((), ())), preferred_element_type=F32)


def _sigmoid(x):
    return 1.0 / (1.0 + jnp.exp(-x))


def _rope_tables(L, LC):
    half = RET_DK // 2
    nf = half // 2
    t = np.arange(L)
    row = (t // GRID_W).astype(np.float32)
    col = (t % GRID_W).astype(np.float32)
    inv = (np.float32(ROPE_BASE) ** (-np.arange(nf, dtype=np.float32) / np.float32(nf))).astype(np.float32)
    ang = np.concatenate([row[:, None] * inv, col[:, None] * inv], axis=-1).astype(np.float32)
    cos, sin = np.cos(ang).astype(np.float32), np.sin(ang).astype(np.float32)
    cos2 = np.concatenate([cos, cos], axis=-1)
    sin2 = np.concatenate([-sin, sin], axis=-1)
    cos2 = np.concatenate([cos2, np.ones((LC, RET_DK), np.float32)], axis=0)
    sin2 = np.concatenate([sin2, np.zeros((LC, RET_DK), np.float32)], axis=0)
    return jnp.asarray(cos2), jnp.asarray(sin2)


def _mod_part_call(c_rows, wada_b, b_shard):
    def body(c_ref, w_ref, b_ref, o_ref):
        a = c_ref[...]
        o_ref[...] = _dot((a * _sigmoid(a)).astype(BF16), w_ref[...]) + b_ref[...]

    return pl.pallas_call(
        body, name="ada_mod", out_shape=jax.ShapeDtypeStruct((c_rows.shape[0], wada_b.shape[1]), F32),
        compiler_params=_params())(c_rows, wada_b, b_shard)


def _dc_masks():
    cq = lax.broadcasted_iota(jnp.int32, (GRID_W, GRID_W), 0)
    ck = lax.broadcasted_iota(jnp.int32, (GRID_W, GRID_W), 1)
    dc = jnp.clip(ck - cq + 15, 0, 30)
    c0 = jnp.clip(cq - 8, 0, GRID_W - 16)
    col_ok = (ck >= c0) & (ck < c0 + 16)
    return dc, col_ok


def _bias_blocks():
    out = []
    for typ, delta in enumerate((4, 0, -4)):
        for rq in range(4):
            for rkk in range(12):
                dr = rkk + delta - rq - 4
                if typ == 0:
                    ok = -rq <= dr <= 7 - rq
                elif typ == 1:
                    ok = -4 <= dr <= 3
                else:
                    ok = -4 - rq <= dr <= 3 - rq
                out.append((typ, rq, rkk, dr if ok else None))
    return out


def _bias_body(r_ref, bias_ref, et_ref, out_ref, sem):
    dc, col_ok = _dc_masks()
    masks = [(dc == j).astype(F32) for j in range(31)]
    nh = bias_ref.shape[0]

    def per_h(h, carry):
        for dr in range(15):
            t = jnp.zeros((GRID_W, GRID_W), F32)
            for j in range(31):
                t = t + masks[j] * r_ref[h, dr * 31 + j]
            et_ref[dr] = jnp.where(col_ok, t, NEG)
        neg = jnp.full((GRID_W, GRID_W), NEG, F32)
        for typ, rq, rkk, dr in _bias_blocks():
            blk = neg if dr is None else et_ref[dr + 7]
            bias_ref[h, typ, rq * 64:(rq + 1) * 64, rkk * 64:(rkk + 1) * 64] = blk
        pltpu.make_async_copy(bias_ref.at[h], out_ref.at[h], sem).start()
        return carry

    lax.fori_loop(0, nh, per_h, 0)
    return [pltpu.make_async_copy(bias_ref.at[h], out_ref.at[h], sem) for h in range(nh)]


def _bias_tile_sums(db_ref, hh):
    acc = {}
    for typ, rq, rkk, dr in _bias_blocks():
        if dr is None:
            continue
        blk = db_ref[hh, typ, rq * 64:(rq + 1) * 64, rkk * 64:(rkk + 1) * 64]
        acc[dr] = blk if dr not in acc else acc[dr] + blk
    return acc


def _small_reduce_body(dt_ref, dlg_ref, drpb_ref, dlgo_ref, p_ref):
    dc, _ = _dc_masks()
    masks = [(dc == j).astype(F32) for j in range(31)]
    ones = jnp.ones((8, GRID_W), F32)
    p_ref[...] = jnp.zeros_like(p_ref)
    drpb_ref[...] = jnp.zeros_like(drpb_ref)

    def per_h(h, carry):
        for dr in range(-7, 8):
            t = dt_ref[h, dr + 7]
            for j in range(31):
                p_ref[j:j + 1, :] = jnp.sum(t * masks[j], axis=0, keepdims=True)
            red = lax.dot_general(ones, p_ref[...], (((1,), (1,)), ((), ())),
                                  precision=HIGHEST, preferred_element_type=F32)
            drpb_ref[h, dr + 7:dr + 8, :] = red[0:1, :]
        return carry

    lax.fori_loop(0, dt_ref.shape[0], per_h, 0)
    x = dlg_ref[0]
    for b in range(1, dlg_ref.shape[0]):
        x = x + dlg_ref[b]
    x = x.reshape(4 * 8, x.shape[-1])
    dlgo_ref[...] = jnp.dot(x, jnp.ones((x.shape[-1], 128), F32), precision=HIGHEST,
                            preferred_element_type=F32)


def _inproj_gather_call(order, x, ctx, mod_part, norm_g, win_b, wout_b, cos2, sin2):
    B, L, _ = x.shape
    LC = ctx.shape[1]
    T = L + LC
    TI = 2 * TQ
    nl = L // TI
    nt = nl + 1
    assert LC == TQ and L % TI == 0
    kscale = RET_DK ** -0.5
    HR = D // 2
    pad_rows = nt * TI - T
    cos2 = jnp.pad(cos2, ((0, pad_rows), (0, 0)))
    sin2 = jnp.pad(sin2, ((0, pad_rows), (0, 0)))

    MW = mod_part.shape[1]
    NB = N_DEV * B

    def body(ord_ref, x_ref, ctx_ref, mp_ref, g_ref, wown_ref, woown_ref, cos_ref, sin_ref,
             p_ref, h_ref, wf_ref, wof_ref, modo_ref, w_all, wo_all, hs_ref, mp_all, mod_ref, ssem, rsem, lsem):
        j, b, t = pl.program_id(0), pl.program_id(1), pl.program_id(2)
        first = (b == 0) & (t == 0)
        mx, my, mc = _mesh_pos()
        s = 2 * mx + my

        m_send = [_remote(mp_ref, mp_all.at[s], ssem, rsem, 12 + k, (px, py, mc))
                  for k, (px, py) in enumerate(_other_chips(mx, my))]
        m_recv = [_remote(mp_ref, mp_all.at[2 * px + py], ssem, rsem, 12 + k, (px, py, mc))
                  for k, (px, py) in enumerate(_other_chips(mx, my))]

        sems = (ssem, rsem, lsem)
        own, ici_send, ici_recv, fwd_send, fwd_recv, outs = _gather_copies(wown_ref, w_all, wf_ref, HR, sems, 0, 0)
        oown, o_send, o_recv, o_fsend, o_frecv, o_outs = _gather_copies(
            woown_ref, wo_all, wof_ref, woown_ref.shape[0] // 2, sems, 6, 5)

        @pl.when(first & (j == 0))
        def _():
            for cp in m_send:
                cp.start()
            own.start()
            oown.start()
            mp_all[s] = mp_ref[...]
            own.wait()
            ici_send[0].start()
            ici_send[1].start()
            outs[0].start()
            oown.wait()
            for cp in m_recv:
                cp.wait_recv()
            me = 4 * mx + 2 * my + mc
            mod_ref[...] = jnp.zeros_like(mod_ref)
            for p in range(N_SHARD):
                for r in range(B):
                    mod_ref[r:r + 1, p * MW:(p + 1) * MW] = mp_all[p, pl.ds(B * me + r, 1), :]
                mod_ref[B:B + 1, p * MW:(p + 1) * MW] = mp_all[p, NB:NB + 1, :]
            modo_ref[...] = mod_ref[...]

        for k in range(3):
            @pl.when(first & (j == k + 1))
            def _(k=k):
                ici_recv[k].wait_recv()
                if k == 0:
                    ici_send[2].start()
                fwd_send[k].start()
                fwd_recv[k].wait_recv()
                outs[1 + k].start()
                if k == 1:
                    for cp in o_send:
                        cp.start()
                if k == 2:
                    for got, fwd in zip(o_recv, o_fsend):
                        got.wait_recv()
                        fwd.start()

        tile = b * nt + t

        @pl.when(j == 0)
        def _():
            is_lat = t < nl
            ctx_tile = jnp.concatenate([ctx_ref[...], jnp.zeros((TI - LC, D), F32)], axis=0)
            xt = jnp.where(is_lat, x_ref[...], ctx_tile)
            mrow = mod_ref[pl.ds(jnp.where(is_lat, b, B), 1), :]
            shift, scale = mrow[:, 0:D], mrow[:, D:2 * D]
            rstd = lax.rsqrt(jnp.mean(xt * xt, axis=-1, keepdims=True) + EPS)
            h0 = ((xt * rstd * g_ref[...]) * (1.0 + scale) + shift).astype(BF16)
            h_ref[...] = h0
            hs_ref[tile] = h0

        shard = ord_ref[j]

        def project(sh, nrows):
            hb = hs_ref[tile, 0:nrows, :]
            cs, sn = cos_ref[0:nrows, :], sin_ref[0:nrows, :]
            for half in range(2):
                sec = 2 * sh + half
                acc = _dot(hb, w_all[sh, :, half * 512:(half + 1) * 512])
                if sec == 0:
                    acc = acc * (NA_DH ** -0.5)
                if sec in (4, 5):
                    for q in range(4):
                        a = acc[:, q * 128:(q + 1) * 128]
                        r = a * cs + pltpu.roll(a, 64, 1) * sn
                        if sec == 5:
                            r = r * kscale
                        p_ref[0:nrows, half * 512 + q * 128:half * 512 + (q + 1) * 128] = r.astype(BF16)
                else:
                    p_ref[0:nrows, half * 512:(half + 1) * 512] = acc.astype(BF16)

        for sh in range(N_SHARD):
            @pl.when((shard == sh) & (t < nl))
            def _(sh=sh):
                project(sh, TI)

            @pl.when((shard == sh) & (t == nl))
            def _(sh=sh):
                project(sh, LC)

        @pl.when((j == N_SHARD - 1) & (b == B - 1) & (t == nt - 1))
        def _():
            for cp in o_frecv:
                cp.wait_recv()
            for cp in o_outs:
                cp.start()
            _finish(outs + o_outs, ici_send + fwd_send + o_send + o_fsend + m_send, [])

    tok = lambda j, b, t, o: (jnp.where(j == 0, b, B - 1), jnp.where(j == 0, jnp.minimum(t, nl - 1), nl - 1), 0)
    grid_spec = pltpu.PrefetchScalarGridSpec(
        num_scalar_prefetch=1, grid=(N_SHARD, B, nt),
        in_specs=[
            pl.BlockSpec((None, TI, D), tok),
            pl.BlockSpec((None, LC, D), lambda j, b, t, o: (jnp.where(j == 0, b, B - 1), 0, 0)),
            pl.BlockSpec(mod_part.shape, lambda j, b, t, o: (0, 0)),
            pl.BlockSpec((1, D), lambda j, b, t, o: (0, 0)),
            ANY, ANY,
            pl.BlockSpec((TI, RET_DK), lambda j, b, t, o: (t, 0)),
            pl.BlockSpec((TI, RET_DK), lambda j, b, t, o: (t, 0)),
        ],
        out_specs=(pl.BlockSpec((None, TI, D), lambda j, b, t, o: (b, t, o[j])),
                   pl.BlockSpec((None, TI, D), lambda j, b, t, o: (
                       jnp.where(j == 0, b, B - 1), jnp.where(j == 0, t, nt - 1), 0)), ANY, ANY,
                   pl.BlockSpec((8, 3 * D), lambda j, b, t, o: (0, 0))),
        scratch_shapes=[pltpu.VMEM((N_SHARD, D, D), BF16), pltpu.VMEM((N_SHARD,) + wout_b.shape, BF16),
                        pltpu.VMEM((B * nt, TI, D), BF16),
                        pltpu.VMEM((N_SHARD,) + mod_part.shape, F32), pltpu.VMEM((8, 3 * D), F32),
                        pltpu.SemaphoreType.DMA((15,)), pltpu.SemaphoreType.DMA((15,)),
                        pltpu.SemaphoreType.DMA((10,))])
    return pl.pallas_call(
        body, name="in_proj", grid_spec=grid_spec,
        out_shape=(jax.ShapeDtypeStruct((B, T, 4 * D), BF16), jax.ShapeDtypeStruct((B, T, D), BF16),
                   jax.ShapeDtypeStruct((N_SHARD, D, D), BF16),
                   jax.ShapeDtypeStruct((N_SHARD,) + wout_b.shape, BF16),
                   jax.ShapeDtypeStruct((8, 3 * D), F32)),
        compiler_params=_params(("arbitrary",) * 3, vmem_mb=56))(
            order, x, ctx, mod_part, norm_g, win_b, wout_b, cos2, sin2)


def _na_specs(L, T, rows, nh=2):
    nm = rows // 4
    w = nh * NA_DH
    per = 512 // w
    q_spec = pl.BlockSpec((None, TQ, w), lambda hp, b, m: (b, m, hp))
    k_spec = pl.BlockSpec((None, T, w), lambda hp, b, m: (b, 0, per + hp))
    v_spec = pl.BlockSpec((None, T, w), lambda hp, b, m: (b, 0, 2 * per + hp))
    g_spec = pl.BlockSpec((None, TQ, w), lambda hp, b, m: (b, m, 3 * per + hp))
    bias_spec = pl.BlockSpec((nh, 3, TQ, KW), lambda hp, b, m: (hp, 0, 0, 0))
    return nm, q_spec, k_spec, v_spec, g_spec, bias_spec


def _na_tile(m, nm, rows):
    typ = jnp.where(m == 0, 0, jnp.where(m == nm - 1, 2, 1))
    start = pl.multiple_of(jnp.clip(4 * m - 4, 0, rows - 12) * GRID_W, TQ)
    return typ, start


def _na_fwd_call(P, bias, L, LC):
    B, T, _ = P.shape
    rows = L // GRID_W
    NH = 4
    nm, q_spec, k_spec, v_spec, g_spec, bias_spec = _na_specs(L, T, rows, NH)

    def body(q_ref, k_ref, v_ref, g_ref, bias_ref, y_ref, o_ref):
        typ, start = _na_tile(pl.program_id(2), nm, rows)
        for hh in range(NH):
            ln = slice(hh * NA_DH, (hh + 1) * NA_DH)
            q = q_ref[:, ln]
            kw, vw = k_ref[pl.ds(start, KW), ln], v_ref[pl.ds(start, KW), ln]
            kc, vc = k_ref[L:L + LC, ln], v_ref[L:L + LC, ln]
            s1 = _dot_nt(q, kw) + bias_ref[hh, typ]
            s2 = _dot_nt(q, kc)
            mx = jnp.maximum(jnp.max(s1, axis=-1, keepdims=True), jnp.max(s2, axis=-1, keepdims=True))
            p1, p2 = jnp.exp(s1 - mx), jnp.exp(s2 - mx)
            inv = 1.0 / (jnp.sum(p1, axis=-1, keepdims=True) + jnp.sum(p2, axis=-1, keepdims=True))
            o = (_dot(p1.astype(BF16), vw) + _dot(p2.astype(BF16), vc)) * inv
            g = g_ref[:, ln].astype(F32)
            o_ref[:, ln] = o.astype(BF16)
            y_ref[:, ln] = (o * (g * _sigmoid(g))).astype(BF16)

    tile = pl.BlockSpec((None, TQ, NH * NA_DH), lambda hp, b, m: (b, m, hp))
    return pl.pallas_call(
        body, name="na_fwd", grid=(8 // NH, B, nm),
        in_specs=[q_spec, k_spec, v_spec, g_spec, bias_spec],
        out_specs=(tile, tile),
        out_shape=(jax.ShapeDtypeStruct((B, L, 512), BF16),) * 2,
        compiler_params=_params(("arbitrary",) * 3))(P, P, P, P, bias)


def _na_bwd_call(P, bias, dY, o_na, L, LC):
    B, T, _ = P.shape
    rows = L // GRID_W
    NH = 4
    W = NH * NA_DH
    nm, q_spec, k_spec, v_spec, g_spec, bias_spec = _na_specs(L, T, rows, NH)
    scale = NA_DH ** -0.5

    RB = 32

    def body(q_ref, k_ref, v_ref, g_ref, bias_ref, dy_ref, o_ref, dq_ref, dg_ref, dk_ref, dv_ref, dt_ref,
             db_ref, s1_ref, s2_ref, dp1_ref, dp2_ref, p1_ref, p2_ref, ds1_ref, ds2_ref, dkt_ref, dvt_ref):
        b, m = pl.program_id(1), pl.program_id(2)
        typ, start = _na_tile(m, nm, rows)

        @pl.when(m == 0)
        def _():
            dkt_ref[...] = jnp.zeros_like(dkt_ref)
            dvt_ref[...] = jnp.zeros_like(dvt_ref)

        @pl.when((m == 0) & (b == 0))
        def _():
            db_ref[...] = jnp.zeros_like(db_ref)

        for hh in range(NH):
            ln = slice(hh * NA_DH, (hh + 1) * NA_DH)
            q = q_ref[:, ln]
            kw, vw = k_ref[pl.ds(start, KW), ln], v_ref[pl.ds(start, KW), ln]
            kc, vc = k_ref[L:L + LC, ln], v_ref[L:L + LC, ln]
            g = g_ref[:, ln].astype(F32)
            sg = _sigmoid(g)
            dy = dy_ref[:, ln].astype(F32)
            do = (dy * (g * sg)).astype(BF16)
            s1_ref[hh] = _dot_nt(q, kw)
            s2_ref[hh] = _dot_nt(q, kc)
            dp1_ref[hh] = _dot_nt(do, vw)
            dp2_ref[hh] = _dot_nt(do, vc)

            def rows_pass(r, carry, hh=hh):
                rw = pl.ds(pl.multiple_of(r * RB, RB), RB)
                a = s1_ref[hh, rw, :] + bias_ref[hh, typ, rw, :]
                c = s2_ref[hh, rw, :]
                mx = jnp.maximum(jnp.max(a, axis=-1, keepdims=True), jnp.max(c, axis=-1, keepdims=True))
                e1, e2 = jnp.exp(a - mx), jnp.exp(c - mx)
                inv = 1.0 / (jnp.sum(e1, axis=-1, keepdims=True) + jnp.sum(e2, axis=-1, keepdims=True))
                p1, p2 = e1 * inv, e2 * inv
                p1_ref[hh, rw, :] = p1.astype(BF16)
                p2_ref[hh, rw, :] = p2.astype(BF16)
                dp1, dp2 = dp1_ref[hh, rw, :], dp2_ref[hh, rw, :]
                delta = jnp.sum(p1 * dp1, axis=-1, keepdims=True) + jnp.sum(p2 * dp2, axis=-1, keepdims=True)
                ds1 = p1 * (dp1 - delta)
                db_ref[hh, typ, rw, :] += ds1
                ds1_ref[hh, rw, :] = ds1.astype(BF16)
                ds2_ref[hh, rw, :] = (p2 * (dp2 - delta)).astype(BF16)
                return carry

            lax.fori_loop(0, TQ // RB, rows_pass, 0, unroll=True)
            p1b, p2b, ds1b, ds2b = p1_ref[hh], p2_ref[hh], ds1_ref[hh], ds2_ref[hh]
            dg_ref[:, ln] = (dy * o_ref[:, ln].astype(F32) * (sg * (1.0 + g * (1.0 - sg)))).astype(BF16)
            dq_ref[:, ln] = ((_dot(ds1b, kw) + _dot(ds2b, kc)) * scale).astype(BF16)
            dkt_ref[ln, pl.ds(start, KW)] += _dot_tn(q, ds1b)
            dvt_ref[ln, pl.ds(start, KW)] += _dot_tn(do, p1b)
            dkt_ref[ln, L:L + LC] += _dot_tn(q, ds2b)
            dvt_ref[ln, L:L + LC] += _dot_tn(do, p2b)

        @pl.when(m == nm - 1)
        def _():
            dk_ref[...] = dkt_ref[...].T
            dv_ref[...] = dvt_ref[...].T

        @pl.when((m == nm - 1) & (b == B - 1))
        def _():
            for hh in range(NH):
                for dr, t in _bias_tile_sums(db_ref, hh).items():
                    dt_ref[hh, dr + 7] = t

    tile = pl.BlockSpec((None, TQ, W), lambda hp, b, m: (b, m, hp))
    kv_out = pl.BlockSpec((None, T, W), lambda hp, b, m: (b, 0, hp))
    wide, narrow = (NH, TQ, KW), (NH, TQ, LC)
    return pl.pallas_call(
        body, name="na_bwd", grid=(8 // NH, B, nm),
        in_specs=[q_spec, k_spec, v_spec, g_spec, bias_spec, tile, tile],
        out_specs=(tile, tile, kv_out, kv_out,
                   pl.BlockSpec((NH, 15, GRID_W, GRID_W), lambda hp, b, m: (hp, 0, 0, 0))),
        out_shape=(jax.ShapeDtypeStruct((B, L, 512), BF16), jax.ShapeDtypeStruct((B, L, 512), BF16),
                   jax.ShapeDtypeStruct((B, T, 512), F32), jax.ShapeDtypeStruct((B, T, 512), F32),
                   jax.ShapeDtypeStruct((bias.shape[0], 15, GRID_W, GRID_W), F32)),
        scratch_shapes=[pltpu.VMEM((NH,) + bias.shape[1:], F32),
                        pltpu.VMEM(wide, F32), pltpu.VMEM(narrow, F32), pltpu.VMEM(wide, F32), pltpu.VMEM(narrow, F32),
                        pltpu.VMEM(wide, BF16), pltpu.VMEM(narrow, BF16), pltpu.VMEM(wide, BF16),
                        pltpu.VMEM(narrow, BF16), pltpu.VMEM((W, T), F32), pltpu.VMEM((W, T), F32)],
        compiler_params=_params(("arbitrary",) * 3, vmem_mb=60))(P, P, P, P, bias, dY, o_na)


def _head_scalar(dec_ref, h):
    lane = lax.broadcasted_iota(jnp.int32, dec_ref.shape, 1)
    return -jnp.sum(jnp.where(lane == h, jnp.exp(dec_ref[...]), 0.0), axis=1, keepdims=True)


def _chunk_decay(lgf, lgb):
    tau = lax.broadcasted_iota(jnp.int32, (TQ, 1), 0).astype(F32)
    sig = lax.broadcasted_iota(jnp.int32, (1, TQ), 1).astype(F32)
    dist = tau - sig
    dm = jnp.exp(dist * jnp.where(dist > 0, lgf, -lgb)) * jnp.where(dist == 0, 2.0, 1.0)
    return tau, dist, dm


def _ret_states_call(P, dec_f, dec_b, L, LC):
    B, T, _ = P.shape
    n = L // TQ

    def body(df_ref, db_ref, k_ref, v_ref, sf_ref, sb_ref):
        h = pl.program_id(1)
        lgf, lgb = _head_scalar(df_ref, h), _head_scalar(db_ref, h)
        tau = lax.broadcasted_iota(jnp.int32, (TQ, 1), 0).astype(F32)
        jc = lax.broadcasted_iota(jnp.int32, (LC, 1), 0).astype(F32)
        wf, wb = jnp.exp(lgf * (TQ - 1.0 - tau)), jnp.exp(lgb * tau)
        gcf, gcb = jnp.exp(lgf * float(TQ)), jnp.exp(lgb * float(TQ))
        kc, vc = k_ref[L:L + LC, :].astype(F32), v_ref[L:L + LC, :]

        def chunk_state(i, w):
            ks = pl.multiple_of(i * TQ, TQ)
            return _dot_tn((k_ref[pl.ds(ks, TQ), :].astype(F32) * w).astype(BF16), v_ref[pl.ds(ks, TQ), :])

        def fwd(i, s):
            sf_ref[i] = s
            return gcf * s + chunk_state(i, wf)

        lax.fori_loop(0, n, fwd, _dot_tn((kc * jnp.exp(lgf * (LC - 1.0 - jc))).astype(BF16), vc), unroll=True)

        def bwd(r, s):
            i = n - 1 - r
            sb_ref[i] = s
            return gcb * s + chunk_state(i, wb)

        lax.fori_loop(0, n, bwd, _dot_tn((kc * jnp.exp(lgb * jc)).astype(BF16), vc), unroll=True)

    st = pl.BlockSpec((None, None, n, RET_DK, RET_DK), lambda b, h: (b, h, 0, 0, 0))
    return pl.pallas_call(
        body, name="ret_states", grid=(B, 4),
        in_specs=[pl.BlockSpec((1, 4), lambda b, h: (0, 0)), pl.BlockSpec((1, 4), lambda b, h: (0, 0)),
                  pl.BlockSpec((None, T, 128), lambda b, h: (b, 0, 20 + h)),
                  pl.BlockSpec((None, T, 128), lambda b, h: (b, 0, 24 + h))],
        out_specs=(st, st),
        out_shape=(jax.ShapeDtypeStruct((B, 4, n, RET_DK, RET_DK), F32),) * 2,
        compiler_params=_params(("arbitrary",) * 2))(dec_f, dec_b, P, P)


def _retc_fwd_call(P, sf, sb, dec_f, dec_b, ret_norm_g, L):
    B, T, _ = P.shape
    sec = lambda k: pl.BlockSpec((None, TQ, 512), lambda b, i: (b, i, k))
    dec_spec = pl.BlockSpec((1, 4), lambda b, i: (0, 0))
    st_spec = pl.BlockSpec((None, 4, None, RET_DK, RET_DK), lambda b, i: (b, 0, i, 0, 0))

    def body(df_ref, db_ref, q_ref, k_ref, v_ref, g_ref, gn_ref, sf_ref, sb_ref, y_ref, o_ref):
        for h in range(4):
            ln = slice(h * RET_DK, (h + 1) * RET_DK)
            lgf, lgb = _head_scalar(df_ref, h), _head_scalar(db_ref, h)
            tau, _, dm = _chunk_decay(lgf, lgb)
            q = q_ref[:, ln]
            qf = q.astype(F32)
            acc = _dot((_dot_nt(q, k_ref[:, ln]) * dm).astype(BF16), v_ref[:, ln])
            acc = acc + _dot((qf * jnp.exp(lgf * (tau + 1.0))).astype(BF16), sf_ref[h].astype(BF16))
            acc = acc + _dot((qf * jnp.exp(lgb * (TQ - tau))).astype(BF16), sb_ref[h].astype(BF16))
            o_ref[:, ln] = acc
            rn = lax.rsqrt(jnp.mean(acc * acc, axis=-1, keepdims=True) + EPS)
            g = g_ref[:, ln].astype(F32)
            y_ref[:, ln] = ((acc * rn * gn_ref[:, ln]) * (g * _sigmoid(g))).astype(BF16)

    tile = pl.BlockSpec((None, TQ, 512), lambda b, i: (b, i, 0))
    return pl.pallas_call(
        body, name="ret_fwd", grid=(B, L // TQ),
        in_specs=[dec_spec, dec_spec, sec(4), sec(5), sec(6), sec(7),
                  pl.BlockSpec((1, 512), lambda b, i: (0, 0)), st_spec, st_spec],
        out_specs=(tile, tile),
        out_shape=(jax.ShapeDtypeStruct((B, L, 512), BF16), jax.ShapeDtypeStruct((B, L, 512), F32)),
        compiler_params=_params(("arbitrary",) * 2))(dec_f, dec_b, P, P, P, P, ret_norm_g, sf, sb)


def _retc_bwd_call(P, sf, sb, dec_f, dec_b, ret_norm_g, o_ret, dY, cos2, sin2, L, LC):
    B, T, _ = P.shape
    n = L // TQ
    C = float(TQ)
    kscale = RET_DK ** -0.5
    st_spec = pl.BlockSpec((None, 4, n, RET_DK, RET_DK), lambda b, i: (b, 0, 0, 0, 0))

    def body(df_ref, db_ref, q_ref, k_ref, v_ref, g_ref, gn_ref, o_ref, dy_ref, cos_ref, sin_ref, sf_ref, sb_ref,
             dq_ref, dg_ref, dk_ref, dv_ref, dgn_ref, dlg_ref, dsf_ref, dsb_ref):
        i = pl.program_id(1)

        @pl.when(i == 0)
        def _():
            dk_ref[...] = jnp.zeros_like(dk_ref)
            dv_ref[...] = jnp.zeros_like(dv_ref)
            dgn_ref[...] = jnp.zeros_like(dgn_ref)
            dlg_ref[...] = jnp.zeros_like(dlg_ref)

        rows = pl.ds(pl.multiple_of(i * TQ, TQ), TQ)
        cs, sn = cos_ref[rows, :], sin_ref[rows, :]

        def one_head(h):
            ln = slice(h * RET_DK, (h + 1) * RET_DK)
            lgf, lgb = _head_scalar(df_ref, h), _head_scalar(db_ref, h)
            tau, dist, dm = _chunk_decay(lgf, lgb)

            def add_lg(row, x):
                csum = jnp.sum(x, axis=0, keepdims=True)
                tot = csum[:, 0:128]
                for part in range(1, x.shape[1] // 128):
                    tot = tot + csum[:, part * 128:(part + 1) * 128]
                dlg_ref[h, row:row + 1, :] += tot

            q = q_ref[:, ln]
            qf = q.astype(F32)
            o = o_ref[:, ln]
            g = g_ref[:, ln].astype(F32)
            dy = dy_ref[:, ln].astype(F32)
            gn = gn_ref[:, ln]
            sg = _sigmoid(g)
            rn = lax.rsqrt(jnp.mean(o * o, axis=-1, keepdims=True) + EPS)
            nrm = o * rn
            dg_ref[:, ln] = (dy * (nrm * gn) * (sg * (1.0 + g * (1.0 - sg)))).astype(BF16)
            dhn = dy * (g * sg)
            dgn_ref[:, ln] += jnp.sum(dhn * nrm, axis=0, keepdims=True)
            dnrm = dhn * gn
            do = rn * (dnrm - nrm * jnp.mean(dnrm * nrm, axis=-1, keepdims=True))
            dob = do.astype(BF16)
            ki, vi = k_ref[rows, ln], v_ref[rows, ln]
            s = _dot_nt(q, ki)
            dsv = _dot_nt(dob, vi)
            dsb = (dsv * dm).astype(BF16)
            dk_ref[rows, ln] += _dot_tn(dsb, q)
            dv_ref[rows, ln] += _dot_tn((s * dm).astype(BF16), dob)
            xw = s * dsv * dm * jnp.abs(dist)
            fpart = jnp.where(dist > 0, xw, 0.0)
            add_lg(0, fpart)
            add_lg(1, xw - fpart)
            dq = _dot(dsb, ki)
            af, ab = jnp.exp(lgf * (tau + 1.0)), jnp.exp(lgb * (C - tau))
            qa, qb = (qf * af).astype(BF16), (qf * ab).astype(BF16)
            sfi, sbi = sf_ref[h, i].astype(BF16), sb_ref[h, i].astype(BF16)
            dq = dq + af * _dot_nt(dob, sfi) + ab * _dot_nt(dob, sbi)
            dsf_ref[h, i] = _dot_tn(qa, dob)
            dsb_ref[h, i] = _dot_tn(qb, dob)
            add_lg(0, (tau + 1.0) * (_dot(qa, sfi) * do))
            add_lg(1, (C - tau) * (_dot(qb, sbi) * do))
            dq_ref[:, ln] = (dq * cs - pltpu.roll(dq, 64, 1) * sn).astype(BF16)

            @pl.when(i == n - 1)
            def _():
                jc = lax.broadcasted_iota(jnp.int32, (LC, 1), 0).astype(F32)
                crow = pl.ds(L, LC)

                def through_state(rws, w, dw, gst, row):
                    kk, vv = k_ref[rws, ln].astype(F32), v_ref[rws, ln]
                    gb = gst.astype(BF16)
                    vg = _dot_nt(vv, gb)
                    kw = kk * w
                    dk_ref[rws, ln] += w * vg
                    dv_ref[rws, ln] += _dot(kw.astype(BF16), gb)
                    add_lg(row, dw * (kw * vg))

                def scan(gc, w, dw, st_ref, dst_ref, order, row):
                    def step(r, gst):
                        j = order(r)
                        through_state(pl.ds(pl.multiple_of(j * TQ, TQ), TQ), w, dw, gst, row)
                        add_lg(row, (C * gc) * (gst * st_ref[h, j]))
                        return dst_ref[h, j] + gc * gst
                    return lax.fori_loop(0, n, step, jnp.zeros((RET_DK, RET_DK), F32), unroll=True)

                gcf, gcb = jnp.exp(lgf * C), jnp.exp(lgb * C)
                g0 = scan(gcf, jnp.exp(lgf * (C - 1.0 - tau)), C - 1.0 - tau, sf_ref, dsf_ref,
                          lambda r: n - 1 - r, 0)
                through_state(crow, jnp.exp(lgf * (LC - 1.0 - jc)), LC - 1.0 - jc, g0, 0)
                g1 = scan(gcb, jnp.exp(lgb * tau), tau, sb_ref, dsb_ref, lambda r: r, 1)
                through_state(crow, jnp.exp(lgb * jc), jc, g1, 1)
                dk = dk_ref[:, ln]
                dk_ref[:, ln] = (dk * cos_ref[...] - pltpu.roll(dk, 64, 1) * sin_ref[...]) * kscale

        for h in range(4):
            one_head(h)

    sec = lambda k: pl.BlockSpec((None, TQ, 512), lambda b, i: (b, i, k))
    full = lambda k: pl.BlockSpec((None, T, 512), lambda b, i: (b, 0, k))
    dec_spec = pl.BlockSpec((1, 4), lambda b, i: (0, 0))
    tab = pl.BlockSpec((T, RET_DK), lambda b, i: (0, 0))
    return pl.pallas_call(
        body, name="ret_bwd", grid=(B, n),
        in_specs=[dec_spec, dec_spec, sec(4), full(5), full(6), sec(7),
                  pl.BlockSpec((1, 512), lambda b, i: (0, 0)), sec(0), sec(1), tab, tab, st_spec, st_spec],
        out_specs=(sec(0), sec(0), full(0), full(0),
                   pl.BlockSpec((None, 1, 512), lambda b, i: (b, 0, 0)),
                   pl.BlockSpec((None, 4, 8, 128), lambda b, i: (b, 0, 0, 0))),
        out_shape=(jax.ShapeDtypeStruct((B, L, 512), BF16), jax.ShapeDtypeStruct((B, L, 512), BF16),
                   jax.ShapeDtypeStruct((B, T, 512), F32), jax.ShapeDtypeStruct((B, T, 512), F32),
                   jax.ShapeDtypeStruct((B, 1, 512), F32), jax.ShapeDtypeStruct((B, 4, 8, 128), F32)),
        scratch_shapes=[pltpu.VMEM((4, n, RET_DK, RET_DK), F32), pltpu.VMEM((4, n, RET_DK, RET_DK), F32)],
        compiler_params=_params(("arbitrary",) * 2, vmem_mb=56))(
            dec_f, dec_b, P, P, P, P, ret_norm_g, o_ret, dY, cos2, sin2, sf, sb)


def _out_call(y_na, y_ret, x, target, mod, final_g, wout_f):
    B, L, _ = x.shape
    TO = 4 * TQ

    def body(yn_ref, yr_ref, x_ref, t_ref, mod_ref, gf_ref, w_ref, dy_ref, dx2_ref, dwb_ref, sm_ref, dw_ref):
        b, i = pl.program_id(0), pl.program_id(1)

        @pl.when((b == 0) & (i == 0))
        def _():
            dw_ref[...] = jnp.zeros_like(dw_ref)
            sm_ref[...] = jnp.zeros_like(sm_ref)

        gate = mod_ref[pl.ds(b, 1), 2 * D:3 * D]
        gf = gf_ref[...]
        yn, yr = yn_ref[...], yr_ref[...]
        ylat = _dot(yn, w_ref[0:512, :]) + _dot(yr, w_ref[512:1024, :])
        x2 = x_ref[...] + gate * ylat
        r = lax.rsqrt(jnp.mean(x2 * x2, axis=-1, keepdims=True) + EPS)
        xr = x2 * r
        err = xr * gf - t_ref[...]
        sm_ref[1:2, :] += jnp.sum(err * err, axis=0, keepdims=True)
        dout = err * (1.0 / D)
        sm_ref[0:1, :] += jnp.sum(dout * xr, axis=0, keepdims=True)
        gd = dout * gf
        dx2 = r * (gd - xr * jnp.mean(gd * xr, axis=-1, keepdims=True))
        dx2_ref[...] = dx2
        sm_ref[pl.ds(2 + b, 1), :] += jnp.sum(dx2 * ylat, axis=0, keepdims=True)
        dyl = (gate * dx2).astype(BF16)
        dy_ref[:, 0:512] = _dot_nt(dyl, w_ref[0:512, :]).astype(BF16)
        dy_ref[:, 512:1024] = _dot_nt(dyl, w_ref[512:1024, :]).astype(BF16)
        dw_ref[0:512, :] += _dot_tn(yn, dyl)
        dw_ref[512:1024, :] += _dot_tn(yr, dyl)

        @pl.when((b == B - 1) & (i == L // TO - 1))
        def _():
            dwb_ref[...] = dw_ref[...].astype(BF16)

    half = pl.BlockSpec((None, TO, 512), lambda b, i: (b, i, 0))
    full = pl.BlockSpec((None, TO, D), lambda b, i: (b, i, 0))
    return pl.pallas_call(
        body, name="out_proj_loss", grid=(B, L // TO),
        in_specs=[half, half, full, full,
                  pl.BlockSpec((8, 3 * D), lambda b, i: (0, 0)),
                  pl.BlockSpec((1, D), lambda b, i: (0, 0)),
                  pl.BlockSpec((D, D), lambda b, i: (0, 0))],
        out_specs=(full, full, pl.BlockSpec((D, D), lambda b, i: (0, 0)),
                   pl.BlockSpec((8, D), lambda b, i: (0, 0))),
        out_shape=(jax.ShapeDtypeStruct((B, L, D), BF16), jax.ShapeDtypeStruct((B, L, D), F32),
                   jax.ShapeDtypeStruct((D, D), BF16), jax.ShapeDtypeStruct((8, D), F32)),
        scratch_shapes=[pltpu.VMEM((D, D), F32)],
        compiler_params=_params(("arbitrary",) * 2))(y_na, y_ret, x, target, mod, final_g, wout_f)


def _dh_call(dsec, win_f, x, ctx, dx2, mod, norm_g, cp_in, cp_out):
    B, L, _ = x.shape
    LC = ctx.shape[1]
    nl = L // TQ

    def body(d0, d1, d2, d3, d4, d5, d6, d7, w_ref, x_ref, ctx_ref, dx2_ref, mod_ref, g_ref, cpi_ref, cpo_ref,
             gx_ref, sm_ref, sli_ref, slo_ref, ssem, rsem, lsem):
        drefs = (d0, d1, d2, d3, d4, d5, d6, d7)
        b, t = pl.program_id(0), pl.program_id(1)
        is_lat = t < nl

        @pl.when((b == 0) & (t == 0))
        def _():
            sm_ref[...] = jnp.zeros_like(sm_ref)

        def dh_of(secs):
            acc = jnp.zeros((TQ, D), F32)
            for sec in secs:
                s, half = divmod(sec, 2)
                acc = acc + _dot_nt(drefs[sec][...].astype(BF16), w_ref[s, :, half * 512:(half + 1) * 512])
            return acc

        def norm_bwd(dh, xt, mrow):
            scale = mrow[:, D:2 * D]
            g = g_ref[...]
            rstd = lax.rsqrt(jnp.mean(xt * xt, axis=-1, keepdims=True) + EPS)
            xn = xt * rstd
            dshift = jnp.sum(dh, axis=0, keepdims=True)
            dscale = jnp.sum(dh * (xn * g), axis=0, keepdims=True)
            dhn = dh * (1.0 + scale)
            sm_ref[0:1, :] += jnp.sum(dhn * xn, axis=0, keepdims=True)
            dxn = dhn * g
            dx = rstd * (dxn - xn * jnp.mean(dxn * xn, axis=-1, keepdims=True))
            return dshift, dscale, dx

        @pl.when(is_lat)
        def _():
            dshift, dscale, dx = norm_bwd(dh_of(range(8)), x_ref[...], mod_ref[pl.ds(b, 1), :])
            sm_ref[pl.ds(3 + b, 1), :] += dshift
            sm_ref[pl.ds(3 + B + b, 1), :] += dscale
            gx_ref[...] = dx2_ref[...] + dx

        @pl.when(jnp.logical_not(is_lat))
        def _():
            dshift, dscale, _ = norm_bwd(dh_of((1, 2, 5, 6)), ctx_ref[...], mod_ref[B:B + 1, :])
            sm_ref[1:2, :] += dshift
            sm_ref[2:3, :] += dscale

        mx, my, mc = _mesh_pos()
        s = 2 * mx + my
        cps, sls = (cpi_ref, cpo_ref), (sli_ref, slo_ref)
        own = [pltpu.make_async_copy(cps[a].at[s], sls[a].at[s], lsem.at[a]) for a in range(2)]
        sends, recvs, k = [], [], 0
        for px, py in _other_chips(mx, my):
            ps = 2 * px + py
            for a in range(2):
                sends.append(_remote(cps[a].at[ps], sls[a].at[s], ssem, rsem, k, (px, py, mc)))
                recvs.append(_remote(cps[a].at[s], sls[a].at[ps], ssem, rsem, k, (px, py, mc)))
                k += 1

        @pl.when((b == 0) & (t == 0))
        def _():
            for cp in own + sends:
                cp.start()

        @pl.when((b == B - 1) & (t == nl))
        def _():
            _finish(own, sends, recvs)

    lat = lambda b, t: (b, jnp.minimum(t, nl - 1), 0)
    tok = lambda b, t: (b, t, 0)
    sec_specs = [pl.BlockSpec((None, TQ, 512), lat if sec in (0, 3, 4, 7) else tok) for sec in range(8)]
    return pl.pallas_call(
        body, name="dh_norm_bwd", grid=(B, nl + 1),
        in_specs=sec_specs + [
            pl.BlockSpec((N_SHARD, D, D), lambda b, t: (0, 0, 0)),
            pl.BlockSpec((None, TQ, D), lat),
            pl.BlockSpec((None, LC, D), lambda b, t: (b, 0, 0)),
            pl.BlockSpec((None, TQ, D), lat),
            pl.BlockSpec((8, 3 * D), lambda b, t: (0, 0)),
            pl.BlockSpec((1, D), lambda b, t: (0, 0)), ANY, ANY],
        out_specs=(pl.BlockSpec((None, TQ, D), lat), pl.BlockSpec((8, D), lambda b, t: (0, 0)), ANY, ANY),
        out_shape=(jax.ShapeDtypeStruct((B, L, D), F32), jax.ShapeDtypeStruct((8, D), F32),
                   jax.ShapeDtypeStruct(cp_in.shape, cp_in.dtype), jax.ShapeDtypeStruct(cp_out.shape, cp_out.dtype)),
        scratch_shapes=[pltpu.SemaphoreType.DMA((6,)), pltpu.SemaphoreType.DMA((6,)),
                        pltpu.SemaphoreType.DMA((2,))],
        compiler_params=_params(("arbitrary",) * 2))(*dsec, win_f, x, ctx, dx2, mod, norm_g, cp_in, cp_out)


def _dw_call(dsec, h, L):
    B, T, _ = h.shape
    TW = 2 * TQ
    nl = L // TW
    KV = (1, 2, 5, 6)

    def body(d0, d1, d2, d3, d4, d5, d6, d7, c1, c2, c5, c6, h_ref, hc_ref, dw_ref, acc_ref):
        drefs = (d0, d1, d2, d3, d4, d5, d6, d7)
        crefs = dict(zip(KV, (c1, c2, c5, c6)))
        b, t = pl.program_id(0), pl.program_id(1)

        @pl.when((b == 0) & (t == 0))
        def _():
            acc_ref[...] = jnp.zeros_like(acc_ref)

        def add(hb, refs, secs):
            for sec in secs:
                s, half = divmod(sec, 2)
                acc_ref[s, :, half * 512:(half + 1) * 512] += _dot_tn(hb, refs[sec][...].astype(BF16))

        @pl.when(t < nl)
        def _():
            add(h_ref[...], drefs, range(8))

        @pl.when(t == nl)
        def _():
            add(hc_ref[...], crefs, KV)

        @pl.when((b == B - 1) & (t == nl))
        def _():
            dw_ref[...] = acc_ref[...].astype(BF16)

    lat = lambda b, t: (b, jnp.minimum(t, nl - 1), 0)
    ctx = lambda b, t: (b, L // TQ, 0)
    return pl.pallas_call(
        body, name="dw_in", grid=(B, nl + 1),
        in_specs=[pl.BlockSpec((None, TW, 512), lat)] * 8 + [pl.BlockSpec((None, TQ, 512), ctx)] * 4
        + [pl.BlockSpec((None, TW, D), lat), pl.BlockSpec((None, TQ, D), ctx)],
        out_specs=pl.BlockSpec((N_SHARD, D, D), lambda b, t: (0, 0, 0)),
        out_shape=jax.ShapeDtypeStruct((N_SHARD, D, D), BF16),
        scratch_shapes=[pltpu.VMEM((N_SHARD, D, D), F32)],
        compiler_params=_params(("arbitrary",) * 2, vmem_mb=60))(*dsec, *[dsec[k] for k in KV], h, h)


def _mesh_pos():
    return lax.axis_index("x"), lax.axis_index("y"), lax.axis_index("c")


def _flip(v, f):
    return 1 - v if f else v


def _remote(src, dst, ssem, rsem, k, peer):
    return pltpu.make_async_remote_copy(src_ref=src, dst_ref=dst, send_sem=ssem.at[k], recv_sem=rsem.at[k],
                                        device_id=peer, device_id_type=MESH)


def _other_chips(x, y):
    return [(_flip(x, fx), _flip(y, fy)) for fx, fy in ((1, 0), (0, 1), (1, 1))]


def _gather_copies(own_ref, all_ref, out_ref, hr, sems, k0, l0):
    ssem, rsem, lsem = sems
    mx, my, mc = _mesh_pos()
    s = 2 * mx + my
    sib = (mx, my, 1 - mc)
    own = pltpu.make_async_copy(own_ref, all_ref.at[s], lsem.at[l0])
    send, recv, fsend, frecv = [], [], [], []
    outs = [pltpu.make_async_copy(all_ref.at[s], out_ref.at[s], lsem.at[l0 + 1])]
    for k, (px, py) in enumerate(_other_chips(mx, my)):
        ps = 2 * px + py
        mine = all_ref.at[s, pl.ds(mc * hr, hr)]
        send.append(_remote(mine, mine, ssem, rsem, k0 + k, (px, py, mc)))
        got = all_ref.at[ps, pl.ds(mc * hr, hr)]
        recv.append(_remote(mine, got, ssem, rsem, k0 + k, (px, py, mc)))
        fsend.append(_remote(got, got, ssem, rsem, k0 + 3 + k, sib))
        theirs = all_ref.at[ps, pl.ds((1 - mc) * hr, hr)]
        frecv.append(_remote(theirs, theirs, ssem, rsem, k0 + 3 + k, sib))
        outs.append(pltpu.make_async_copy(all_ref.at[ps], out_ref.at[ps], lsem.at[l0 + 2 + k]))
    return own, send, recv, fsend, frecv, outs


def _all_to_all_small(src, dst_all, ssem, rsem, k0, x, y, cc):
    me = 4 * x + 2 * y + cc
    sends, recvs = [], []
    for f in range(1, N_DEV):
        px, py, pc = _flip(x, f & 4), _flip(y, f & 2), _flip(cc, f & 1)
        sends.append(_remote(src, dst_all.at[me], ssem, rsem, k0 + f - 1, (px, py, pc)))
        recvs.append(_remote(src, dst_all.at[4 * px + 2 * py + pc], ssem, rsem, k0 + f - 1, (px, py, pc)))
    return sends, recvs


def _finish(local, sends, recvs):
    for cp in recvs:
        cp.wait_recv()
    for cp in sends:
        cp.wait_send()
    for cp in local:
        cp.wait()


def _c_gather_call(c, rpb_flat):
    def body(c_ref, r_ref, c_all, bias_out, bias_ref, et_ref, ssem, rsem, lsem):
        x, y, cc = _mesh_pos()
        me = 4 * x + 2 * y + cc
        local = [pltpu.make_async_copy(c_ref, c_all.at[me], lsem.at[0])]
        c_send, c_recv = _all_to_all_small(c_ref, c_all, ssem, rsem, 0, x, y, cc)
        for cp in local + c_send:
            cp.start()
        bias_out_copies = _bias_body(r_ref, bias_ref, et_ref, bias_out, lsem.at[1])
        _finish(local + bias_out_copies, c_send, c_recv)

    bias_shape = (rpb_flat.shape[0], 3, TQ, KW)
    return pl.pallas_call(
        body, name="c_gather",
        in_specs=[pl.BlockSpec(memory_space=pltpu.VMEM), pl.BlockSpec(memory_space=pltpu.SMEM)],
        out_specs=(pl.BlockSpec(memory_space=pltpu.VMEM), ANY),
        out_shape=(jax.ShapeDtypeStruct((N_DEV,) + c.shape, c.dtype), jax.ShapeDtypeStruct(bias_shape, F32)),
        scratch_shapes=[pltpu.VMEM(bias_shape, F32), pltpu.VMEM((15, GRID_W, GRID_W), F32),
                        pltpu.SemaphoreType.DMA((N_DEV - 1,)), pltpu.SemaphoreType.DMA((N_DEV - 1,)),
                        pltpu.SemaphoreType.DMA((2,))],
        compiler_params=pltpu.CompilerParams(vmem_limit_bytes=56 << 20))(c, rpb_flat)


VROWS = 32


def _grad_halves_call(dwin_b, dwout_b, dbias, dlg):
    arrs = (dwin_b, dwout_b)
    hrs = [a.shape[1] // 2 for a in arrs]

    def body(din, dout, db_ref, dlg_ref, cp_in, cp_out, drpb_ref, dlgo_ref, got_in, got_out, p_ref, ssem, rsem):
        x, y, cc = _mesh_pos()
        sib = (x, y, 1 - cc)
        srcs, gots, cps = (din, dout), (got_in, got_out), (cp_in, cp_out)
        halves = [_remote(srcs[a].at[:, pl.ds((1 - cc) * hrs[a], hrs[a])], gots[a], ssem, rsem, a, sib)
                  for a in range(2)]
        for cp in halves:
            cp.start()
        _small_reduce_body(db_ref, dlg_ref, drpb_ref, dlgo_ref, p_ref)
        for cp in halves:
            cp.wait_recv()
        for a in range(2):
            for j in range(N_SHARD):
                def add(i, carry, a=a, j=j):
                    r = pl.multiple_of(i * VROWS, VROWS)
                    mine = srcs[a][j, pl.ds(pl.multiple_of(cc * hrs[a] + r, VROWS), VROWS), :].astype(F32)
                    cps[a][j, pl.ds(r, VROWS), :] = (
                        mine + gots[a][j, pl.ds(r, VROWS), :].astype(F32)).astype(BF16)
                    return carry
                lax.fori_loop(0, hrs[a] // VROWS, add, 0)
        for cp in halves:
            cp.wait_send()

    vmem = pl.BlockSpec(memory_space=pltpu.VMEM)
    half_shapes = [(N_SHARD, hrs[a], arrs[a].shape[2]) for a in range(2)]
    return pl.pallas_call(
        body, name="grad_halves",
        in_specs=[vmem] * 4, out_specs=(vmem,) * 4,
        out_shape=(jax.ShapeDtypeStruct(half_shapes[0], BF16), jax.ShapeDtypeStruct(half_shapes[1], BF16),
                   jax.ShapeDtypeStruct((dbias.shape[0], 16, 32), F32), jax.ShapeDtypeStruct((32, 128), F32)),
        scratch_shapes=[pltpu.VMEM(half_shapes[0], BF16), pltpu.VMEM(half_shapes[1], BF16),
                        pltpu.VMEM((32, GRID_W), F32),
                        pltpu.SemaphoreType.DMA((2,)), pltpu.SemaphoreType.DMA((2,))],
        compiler_params=pltpu.CompilerParams(vmem_limit_bytes=56 << 20))(dwin_b, dwout_b, dbias, dlg)


def _grad_finish_call(sl_in, sl_out, small, wada_b):
    arrs = (sl_in, sl_out)
    ws = wada_b.shape[1]

    def body(sin, sout, sm, wa_ref, gin, gout, sm_all, dparts, h_in, h_out, dp_own, ssem, rsem, lsem):
        x, y, cc = _mesh_pos()
        me = 4 * x + 2 * y + cc
        s = 2 * x + y
        sib = (x, y, 1 - cc)
        sls, hs, gs = (sin, sout), (h_in, h_out), (gin, gout)
        sm_send, sm_recv = _all_to_all_small(sm, sm_all, ssem, rsem, 2, x, y, cc)
        sm_own = pltpu.make_async_copy(sm, sm_all.at[me], lsem.at[0])
        for cp in sm_send + [sm_own]:
            cp.start()
        for a in range(2):
            def total(i, carry, a=a):
                rows = pl.ds(pl.multiple_of(i * VROWS, VROWS), VROWS)
                sl = sls[a]
                hs[a][rows, :] = ((sl[0, rows, :].astype(F32) + sl[1, rows, :].astype(F32))
                                  + sl[2, rows, :].astype(F32)) + sl[3, rows, :].astype(F32)
                return carry
            lax.fori_loop(0, arrs[a].shape[1] // VROWS, total, 0)
        mine = [pltpu.make_async_copy(hs[a], gs[a].at[cc], lsem.at[1 + a]) for a in range(2)]
        back = [_remote(hs[a], gs[a].at[cc], ssem, rsem, a, sib) for a in range(2)]
        back_recv = [_remote(hs[a], gs[a].at[1 - cc], ssem, rsem, a, sib) for a in range(2)]
        for cp in mine + back:
            cp.start()
        sm_own.wait()
        for cp in sm_recv:
            cp.wait_recv()
        shift_c = sm_all[0, R_SHIFT_C:R_SHIFT_C + 1, :]
        scale_c = sm_all[0, R_SCALE_C:R_SCALE_C + 1, :]
        for dv in range(1, N_DEV):
            shift_c = shift_c + sm_all[dv, R_SHIFT_C:R_SHIFT_C + 1, :]
            scale_c = scale_c + sm_all[dv, R_SCALE_C:R_SCALE_C + 1, :]
        dmc = jnp.concatenate([shift_c, scale_c, jnp.zeros((1, D), F32)], axis=1).astype(BF16)
        dmc = jnp.broadcast_to(dmc, (8, 3 * D))
        for sh in range(N_SHARD):
            @pl.when(s == sh)
            def _(sh=sh):
                dp_own[...] = _dot_nt(dmc[:, sh * ws:(sh + 1) * ws], wa_ref[...])
        dparts[s] = dp_own[...]
        d_send = [_remote(dp_own, dparts.at[s], ssem, rsem, 9 + k, (px, py, cc))
                  for k, (px, py) in enumerate(_other_chips(x, y))]
        d_recv = [_remote(dp_own, dparts.at[2 * px + py], ssem, rsem, 9 + k, (px, py, cc))
                  for k, (px, py) in enumerate(_other_chips(x, y))]
        for cp in d_send:
            cp.start()
        _finish(mine, back + sm_send + d_send, back_recv + d_recv)

    vmem = pl.BlockSpec(memory_space=pltpu.VMEM)
    return pl.pallas_call(
        body, name="grad_finish",
        in_specs=[vmem] * 4, out_specs=(vmem,) * 4,
        out_shape=(jax.ShapeDtypeStruct((2,) + sl_in.shape[1:], F32),
                   jax.ShapeDtypeStruct((2,) + sl_out.shape[1:], F32),
                   jax.ShapeDtypeStruct((N_DEV,) + small.shape, F32),
                   jax.ShapeDtypeStruct((N_SHARD, 8, D), F32)),
        scratch_shapes=[pltpu.VMEM(sl_in.shape[1:], F32), pltpu.VMEM(sl_out.shape[1:], F32),
                        pltpu.VMEM((8, D), F32),
                        pltpu.SemaphoreType.DMA((12,)), pltpu.SemaphoreType.DMA((12,)),
                        pltpu.SemaphoreType.DMA((3,))],
        compiler_params=pltpu.CompilerParams(vmem_limit_bytes=48 << 20))(sl_in, sl_out, small, wada_b)


def _adamw(w, g, m, v):
    m = ADAM_B1 * m + (1.0 - ADAM_B1) * g
    v = ADAM_B2 * v + (1.0 - ADAM_B2) * (g * g)
    m_hat = m / (1.0 - ADAM_B1 ** ADAM_STEP)
    v_hat = v / (1.0 - ADAM_B2 ** ADAM_STEP)
    return -ADAM_LR * (m_hat / (jnp.sqrt(v_hat) + ADAM_EPS) + ADAM_WD * w), m, v


def _adam_call(w, m, v, g, name):
    R, C = w.shape
    tr = min(R, 512)

    def body(w_ref, m_ref, v_ref, g_ref, go_ref, d_ref, mo_ref, vo_ref):
        g = g_ref[...]
        go_ref[...] = g
        d_ref[...], mo_ref[...], vo_ref[...] = _adamw(w_ref[...], g, m_ref[...], v_ref[...])

    spec = pl.BlockSpec((tr, C), lambda i: (i, 0))
    return pl.pallas_call(
        body, name=name, grid=(R // tr,), in_specs=[spec] * 4,
        out_specs=(spec,) * 4, out_shape=(jax.ShapeDtypeStruct((R, C), F32),) * 4,
        compiler_params=_params(("arbitrary",)))(w, m, v, g)


R_GF, R_NG, R_LOSS, R_RNG, R_LGF, R_LGB, R_SHIFT, R_SCALE, R_GATE, R_SHIFT_C, R_SCALE_C, R_RNG2, R_RPB = (
    0, 1, 2, 3, 4, 5, 6, 8, 10, 12, 13, 14, 16)
W_GF, W_NG, W_CCTX, W_RNG, W_DF, W_DB, W_BADA, W_RPB = 0, 1, 2, 3, 4, 5, 6, 9


SMALL = (("final_norm_g", W_GF, 1, D), ("norm_g", W_NG, 1, D), ("c_ctx", W_CCTX, 1, D),
         ("ret_norm_g", W_RNG, 1, 512), ("ret_decay_fwd", W_DF, 1, 4), ("ret_decay_bwd", W_DB, 1, 4),
         ("b_ada", W_BADA, 3, D), ("na_rpb", W_RPB, 4, D))
N_SMALL = len(SMALL)


def _small_final_call(sm_all, c_t, dact_parts, wada, m_ada, v_ada, small_w, small_m, small_v, B):
    ws = wada.shape[1]
    NB = N_DEV * B

    def body(*refs):
        sm_ref, ct_ref, wf_ref, wa_ref, ma_ref, va_ref = refs[:6]
        ins = refs[6:6 + 3 * N_SMALL]
        outs = refs[6 + 3 * N_SMALL:6 + 7 * N_SMALL]
        ga_ref, da_ref, mao_ref, vao_ref, loss_ref, dmod_ref, pk_ref = refs[6 + 7 * N_SMALL:]
        x, y, _ = _mesh_pos()
        s = 2 * x + y
        tot = sm_ref[0]
        for dv in range(1, N_DEV):
            tot = tot + sm_ref[dv]
        pk_ref[...] = jnp.zeros_like(pk_ref)
        for kind in range(3):
            for i, (_, row, nrow, width) in enumerate(SMALL):
                ref = ins[kind * N_SMALL + i]
                if nrow == 3:
                    for part in range(3):
                        pk_ref[kind, row + part:row + part + 1, :] = ref[:, part * D:(part + 1) * D]
                else:
                    pk_ref[kind, row:row + nrow, 0:width] = ref[...]
        w = pk_ref[0]
        cctx_ref = ins[2]
        for dv in range(N_DEV):
            for b in range(B):
                r = dv * B + b
                for part, row in enumerate((R_SHIFT, R_SCALE, R_GATE)):
                    dmod_ref[r:r + 1, part * D:(part + 1) * D] = sm_ref[dv, row + b:row + b + 1, :]
        dmod_ref[NB:NB + 1, 0:D] = tot[R_SHIFT_C:R_SHIFT_C + 1, :]
        dmod_ref[NB:NB + 1, D:2 * D] = tot[R_SCALE_C:R_SCALE_C + 1, :]
        dmod_ref[NB:NB + 1, 2 * D:3 * D] = jnp.zeros((1, D), F32)
        dmod_ref[NB + 1:, :] = jnp.zeros((dmod_ref.shape[0] - NB - 1, 3 * D), F32)
        dmod = dmod_ref[...]
        cc = cctx_ref[...]
        scc = _sigmoid(cc)
        ct = ct_ref[...]
        act_t = ct * _sigmoid(ct)
        dact = wf_ref[0, 0:1, :]
        for sh in range(1, N_SHARD):
            dact = dact + wf_ref[sh, 0:1, :]
        g = jnp.zeros((16, D), F32)
        rows = lax.broadcasted_iota(jnp.int32, (16, D), 0)

        def put(g, row, val):
            return jnp.where(rows == row, val, g)

        g = put(g, W_GF, tot[R_GF:R_GF + 1, :])
        g = put(g, W_NG, tot[R_NG:R_NG + 1, :])
        g = put(g, W_CCTX, dact * (scc * (1.0 + cc * (1.0 - scc))))
        g = put(g, W_RNG, tot[R_RNG:R_RNG + 1, :] + tot[R_RNG2:R_RNG2 + 1, :])
        g = put(g, W_DF, tot[R_LGF:R_LGF + 1, :] * (-jnp.exp(w[W_DF:W_DF + 1, :])))
        g = put(g, W_DB, tot[R_LGB:R_LGB + 1, :] * (-jnp.exp(w[W_DB:W_DB + 1, :])))
        db = jnp.sum(dmod, axis=0, keepdims=True)
        for part in range(3):
            g = put(g, W_BADA + part, db[:, part * D:(part + 1) * D])
        for part in range(4):
            g = put(g, W_RPB + part, tot[R_RPB + part:R_RPB + part + 1, :])
        for kind, val in enumerate((g,) + _adamw(w, g, pk_ref[1], pk_ref[2])):
            for i, (_, row, nrow, width) in enumerate(SMALL):
                out = outs[kind * N_SMALL + i]
                if nrow == 3:
                    for part in range(3):
                        out[:, part * D:(part + 1) * D] = val[row + part:row + part + 1, :]
                else:
                    out[...] = val[row:row + nrow, 0:width]
        loss_ref[...] = jnp.broadcast_to(
            (0.5 / D) * jnp.sum(tot[R_LOSS:R_LOSS + 1, :], axis=1, keepdims=True), (8, 128))
        for sh in range(N_SHARD):
            @pl.when(s == sh)
            def _():
                ga = jnp.dot(act_t, dmod[:, sh * ws:(sh + 1) * ws], precision=HIGHEST,
                             preferred_element_type=F32)
                ga_ref[...] = ga
                da_ref[...], mao_ref[...], vao_ref[...] = _adamw(wa_ref[...], ga, ma_ref[...], va_ref[...])

    sh_small = tuple(jax.ShapeDtypeStruct(a.shape, F32) for a in small_w)
    sh_ada = jax.ShapeDtypeStruct(wada.shape, F32)
    res = pl.pallas_call(
        body, name="small_final",
        out_shape=sh_small * 4 + (sh_ada,) * 4 + (jax.ShapeDtypeStruct((8, 128), F32),),
        scratch_shapes=[pltpu.VMEM((NB + 8, 3 * D), F32), pltpu.VMEM((3, 16, D), F32)],
        compiler_params=_params(vmem_mb=56))(
            sm_all, c_t, dact_parts, wada, m_ada, v_ada, *small_w, *small_m, *small_v)
    smalls = [res[k * N_SMALL:(k + 1) * N_SMALL] for k in range(4)]
    return smalls, res[4 * N_SMALL:4 * N_SMALL + 4], res[4 * N_SMALL + 4]


def _local_step(order, x, ctx, c_rows, norm_g, wada_b, b_shard, win_b, bias, dec_f, dec_b, ret_norm_g,
                wout_b, final_g, target):
    B, L, _ = x.shape
    LC = ctx.shape[1]
    assert B == 2
    cos2, sin2 = _rope_tables(L, LC)
    mod_part = _mod_part_call(c_rows, wada_b, b_shard)
    P, h, win_f, wout_f, mod = _inproj_gather_call(order, x, ctx, mod_part, norm_g, win_b, wout_b, cos2, sin2)
    y_na, o_na = _na_fwd_call(P, bias, L, LC)
    sf, sb = _ret_states_call(P, dec_f, dec_b, L, LC)
    y_ret, o_ret = _retc_fwd_call(P, sf, sb, dec_f, dec_b, ret_norm_g, L)
    dY, dx2, dwout_p, sm_out = _out_call(y_na, y_ret, x, target, mod, final_g, wout_f.reshape(D, D))
    dnq, dng, dnk, dnv, dbias = _na_bwd_call(P, bias, dY, o_na, L, LC)
    drq, drg, drk, drv, dgn, dlg = _retc_bwd_call(P, sf, sb, dec_f, dec_b, ret_norm_g, o_ret, dY, cos2, sin2, L, LC)
    dsec = (dnq, dnk, dnv, dng, drq, drk, drv, drg)
    dwin_b = _dw_call(dsec, h, L)
    cp_in, cp_out, drpb, dlg_sum = _grad_halves_call(
        dwin_b, dwout_p.reshape(N_SHARD, D // N_SHARD, D), dbias, dlg)
    grad_x, sm_dh, sl_in, sl_out = _dh_call(dsec, win_f, x, ctx, dx2, mod, norm_g, cp_in, cp_out)
    z = jnp.zeros((1, D), F32)
    pad = lambda v: jnp.pad(v.reshape(1, -1), ((0, 0), (0, D - v.size)))
    dlg_sum = dlg_sum.reshape(4, 8, 128)
    rpb_rows = jnp.pad(drpb[:, :15, :31].reshape(-1), (0, 4 * D - drpb.shape[0] * 465)).reshape(4, D)
    small = jnp.concatenate([
        sm_out[0:1], sm_dh[0:1], sm_out[1:2], pad(dgn[0]), pad(dlg_sum[:, 0, 0]), pad(dlg_sum[:, 1, 0]),
        sm_dh[3:5], sm_dh[5:7], sm_out[2:4], sm_dh[1:2], sm_dh[2:3], pad(dgn[1]), z, rpb_rows,
        jnp.zeros((SM_ROWS - 20, D), F32)], axis=0)
    return grad_x, sl_in, sl_out, small


def kernel(x, c, ctx, c_ctx, norm_g, w_ada, b_ada, w_in, na_rpb, ret_decay_fwd, ret_decay_bwd, ret_norm_g, w_out, final_norm_g, loss_target, m_c_ctx, m_norm_g, m_w_ada, m_b_ada, m_w_in, m_na_rpb, m_ret_decay_fwd, m_ret_decay_bwd, m_ret_norm_g, m_w_out, m_final_norm_g, v_c_ctx, v_norm_g, v_w_ada, v_b_ada, v_w_in, v_na_rpb, v_ret_decay_fwd, v_ret_decay_bwd, v_ret_norm_g, v_w_out, v_final_norm_g):
    B = x.shape[0]
    c_all, bias = _c_gather_call(c, na_rpb[0].reshape(na_rpb.shape[1], -1))
    c_rows = jnp.concatenate([c_all.reshape(N_DEV * B, D), c_ctx.reshape(1, D), jnp.zeros((7, D), F32)], axis=0)
    mx, my = lax.axis_index("x"), lax.axis_index("y")
    order = jnp.stack([2 * mx + my, 2 * (1 - mx) + my, 2 * mx + (1 - my),
                       2 * (1 - mx) + (1 - my)]).astype(jnp.int32)
    ws = w_ada.shape[2]
    b_shard = lax.dynamic_slice(b_ada, (0, (2 * mx + my) * ws), (1, ws))
    wada_b = w_ada[0].astype(BF16)
    grad_x, sl_in, sl_out, small = _local_step(
        order, x, ctx, c_rows, norm_g, wada_b, b_shard, w_in[0].astype(BF16), bias, ret_decay_fwd,
        ret_decay_bwd, ret_norm_g, w_out[0].astype(BF16), final_norm_g.reshape(1, D), loss_target)
    gin, gout, sm_all, dact_parts = _grad_finish_call(sl_in, sl_out, small, wada_b)
    g_win, d_win, nm_win, nv_win = _adam_call(
        w_in[0], m_w_in[0], v_w_in[0], gin.reshape(w_in.shape[1:]), "adam_w_in")
    g_wout, d_wout, nm_wout, nv_wout = _adam_call(
        w_out[0], m_w_out[0], v_w_out[0], gout.reshape(w_out.shape[1:]), "adam_w_out")

    def small_inputs(gf, ng, cc, rng, df, db, bada, rpb):
        return (gf.reshape(1, D), ng, cc.reshape(1, D), rng, df, db, bada,
                jnp.pad(rpb.reshape(-1), (0, 4 * D - rpb.size)).reshape(4, D))

    c_t = c_rows.T
    smalls, adas, loss = _small_final_call(
        sm_all, c_t, dact_parts, w_ada[0], m_w_ada[0], v_w_ada[0],
        small_inputs(final_norm_g, norm_g, c_ctx, ret_norm_g, ret_decay_fwd, ret_decay_bwd, b_ada, na_rpb),
        small_inputs(m_final_norm_g, m_norm_g, m_c_ctx, m_ret_norm_g, m_ret_decay_fwd, m_ret_decay_bwd, m_b_ada,
                     m_na_rpb),
        small_inputs(v_final_norm_g, v_norm_g, v_c_ctx, v_ret_norm_g, v_ret_decay_fwd, v_ret_decay_bwd, v_b_ada,
                     v_na_rpb), B)
    res = []
    for p, ada, win_o, wout_o in zip(smalls, adas, (g_win, d_win, nm_win, nv_win),
                                     (g_wout, d_wout, nm_wout, nv_wout)):
        gf, ng, cc, rng, df, db, bada, rpb = p
        res.append([cc.reshape(D), ng, ada[None], bada, win_o[None],
                    rpb.reshape(-1)[:na_rpb.size].reshape(na_rpb.shape), df, db, rng, wout_o[None], gf.reshape(D)])
    return (loss[0, 0], grad_x, *res[0], *res[1], *res[2], *res[3])
```

```python
import numpy as np
import jax
import jax.numpy as jnp
from jax import lax
from jax.experimental import pallas as pl
from jax.experimental.pallas import tpu as pltpu

F32 = jnp.float32
BF16 = jnp.bfloat16
HIGHEST = lax.Precision.HIGHEST

D = 1024
GRID_W = 64
NA_DH = 64
RET_DK = 128
ROPE_BASE = 10000.0
EPS = 1e-6
NEG = -1e30
TQ = 256
KW = 12 * GRID_W
N_SHARD = 4
N_DEV = 8
SM_ROWS = 24

ADAM_LR = 0.001
ADAM_B1 = 0.9
ADAM_B2 = 0.999
ADAM_EPS = 1e-08
ADAM_WD = 0.01
ADAM_STEP = 10

MESH = pl.DeviceIdType.MESH
ANY = pl.BlockSpec(memory_space=pl.ANY)


def _params(sem=None, vmem_mb=48):
    return pltpu.CompilerParams(dimension_semantics=sem, vmem_limit_bytes=vmem_mb << 20)


def _dot(a, b):
    return jnp.dot(a, b, preferred_element_type=F32)


def _dot_nt(a, b):
    return lax.dot_general(a, b, (((1,), (1,)), ((), ())), preferred_element_type=F32)


def _dot_tn(a, b):
    return lax.dot_general(a, b, (((0,), (0,)), ((), ())), preferred_element_type=F32)


def _sigmoid(x):
    return 1.0 / (1.0 + jnp.exp(-x))


def _rope_tables(L, LC):
    half = RET_DK // 2
    nf = half // 2
    t = np.arange(L)
    row = (t // GRID_W).astype(np.float32)
    col = (t % GRID_W).astype(np.float32)
    inv = (np.float32(ROPE_BASE) ** (-np.arange(nf, dtype=np.float32) / np.float32(nf))).astype(np.float32)
    ang = np.concatenate([row[:, None] * inv, col[:, None] * inv], axis=-1).astype(np.float32)
    cos, sin = np.cos(ang).astype(np.float32), np.sin(ang).astype(np.float32)
    cos2 = np.concatenate([cos, cos], axis=-1)
    sin2 = np.concatenate([-sin, sin], axis=-1)
    cos2 = np.concatenate([cos2, np.ones((LC, RET_DK), np.float32)], axis=0)
    sin2 = np.concatenate([sin2, np.zeros((LC, RET_DK), np.float32)], axis=0)
    return jnp.asarray(cos2), jnp.asarray(sin2)


def _mod_part_call(c_rows, wada_b, b_shard):
    def body(c_ref, w_ref, b_ref, o_ref):
        a = c_ref[...]
        o_ref[...] = _dot((a * _sigmoid(a)).astype(BF16), w_ref[...]) + b_ref[...]

    return pl.pallas_call(
        body, name="ada_mod", out_shape=jax.ShapeDtypeStruct((c_rows.shape[0], wada_b.shape[1]), F32),
        compiler_params=_params())(c_rows, wada_b, b_shard)


def _dc_masks():
    cq = lax.broadcasted_iota(jnp.int32, (GRID_W, GRID_W), 0)
    ck = lax.broadcasted_iota(jnp.int32, (GRID_W, GRID_W), 1)
    dc = jnp.clip(ck - cq + 15, 0, 30)
    c0 = jnp.clip(cq - 8, 0, GRID_W - 16)
    col_ok = (ck >= c0) & (ck < c0 + 16)
    return dc, col_ok


def _bias_blocks():
    out = []
    for typ, delta in enumerate((4, 0, -4)):
        for rq in range(4):
            for rkk in range(12):
                dr = rkk + delta - rq - 4
                if typ == 0:
                    ok = -rq <= dr <= 7 - rq
                elif typ == 1:
                    ok = -4 <= dr <= 3
                else:
                    ok = -4 - rq <= dr <= 3 - rq
                out.append((typ, rq, rkk, dr if ok else None))
    return out


def _bias_body(r_ref, bias_ref, et_ref, out_ref, sem):
    dc, col_ok = _dc_masks()
    masks = [(dc == j).astype(F32) for j in range(31)]
    nh = out_ref.shape[0]

    def per_h(h, carry):
        slot = h % 2

        @pl.when(h >= 2)
        def _():
            pltpu.make_async_copy(bias_ref.at[slot], out_ref.at[h - 2], sem.at[slot]).wait()

        for dr in range(15):
            t = jnp.zeros((GRID_W, GRID_W), F32)
            for j in range(31):
                t = t + masks[j] * r_ref[h, dr * 31 + j]
            et_ref[dr] = jnp.where(col_ok, t, NEG)
        neg = jnp.full((GRID_W, GRID_W), NEG, F32)
        for typ, rq, rkk, dr in _bias_blocks():
            blk = neg if dr is None else et_ref[dr + 7]
            bias_ref[slot, typ, rq * 64:(rq + 1) * 64, rkk * 64:(rkk + 1) * 64] = blk
        pltpu.make_async_copy(bias_ref.at[slot], out_ref.at[h], sem.at[slot]).start()
        return carry

    lax.fori_loop(0, nh, per_h, 0)
    return [pltpu.make_async_copy(bias_ref.at[h % 2], out_ref.at[h], sem.at[h % 2]) for h in range(nh - 2, nh)]


def _bias_tile_sums(db_ref, hh):
    acc = {}
    for typ, rq, rkk, dr in _bias_blocks():
        if dr is None:
            continue
        blk = db_ref[hh, typ, rq * 64:(rq + 1) * 64, rkk * 64:(rkk + 1) * 64]
        acc[dr] = blk if dr not in acc else acc[dr] + blk
    return acc


def _small_reduce_body(dt_ref, dlg_ref, drpb_ref, dlgo_ref, p_ref):
    dc, _ = _dc_masks()
    masks = [(dc == j).astype(F32) for j in range(31)]
    ones = jnp.ones((8, GRID_W), F32)
    p_ref[...] = jnp.zeros_like(p_ref)
    drpb_ref[...] = jnp.zeros_like(drpb_ref)

    def per_h(h, carry):
        for dr in range(-7, 8):
            t = dt_ref[h, dr + 7]
            for j in range(31):
                p_ref[j:j + 1, :] = jnp.sum(t * masks[j], axis=0, keepdims=True)
            red = lax.dot_general(ones, p_ref[...], (((1,), (1,)), ((), ())),
                                  precision=HIGHEST, preferred_element_type=F32)
            drpb_ref[h, dr + 7:dr + 8, :] = red[0:1, :]
        return carry

    lax.fori_loop(0, dt_ref.shape[0], per_h, 0)
    x = dlg_ref[0]
    for b in range(1, dlg_ref.shape[0]):
        x = x + dlg_ref[b]
    x = x.reshape(4 * 8, x.shape[-1])
    dlgo_ref[...] = jnp.dot(x, jnp.ones((x.shape[-1], 128), F32), precision=HIGHEST,
                            preferred_element_type=F32)


def _inproj_gather_call(order, x, ctx, mod_part, norm_g, win_b, wout_b, cos2, sin2, rpb_flat):
    B, L, _ = x.shape
    LC = ctx.shape[1]
    T = L + LC
    TI = 2 * TQ
    nl = L // TI
    nt = nl + 1
    assert LC == TQ and L % TI == 0
    kscale = RET_DK ** -0.5
    HR = D // 2
    pad_rows = nt * TI - T
    cos2 = jnp.pad(cos2, ((0, pad_rows), (0, 0)))
    sin2 = jnp.pad(sin2, ((0, pad_rows), (0, 0)))

    MW = mod_part.shape[1]
    NB = N_DEV * B

    def body(ord_ref, x_ref, ctx_ref, mp_ref, g_ref, wown_ref, woown_ref, cos_ref, sin_ref, r_ref,
             p_ref, h_ref, wf_ref, wof_ref, modo_ref, bias_out, w_all, wo_all, hs_ref, mp_all, mod_ref,
             bias_v, et_ref, ssem, rsem, lsem, bsem):
        j, b, t = pl.program_id(0), pl.program_id(1), pl.program_id(2)
        first = (b == 0) & (t == 0)
        mx, my, mc = _mesh_pos()
        s = 2 * mx + my

        m_send = [_remote(mp_ref, mp_all.at[s], ssem, rsem, 12 + k, (px, py, mc))
                  for k, (px, py) in enumerate(_other_chips(mx, my))]
        m_recv = [_remote(mp_ref, mp_all.at[2 * px + py], ssem, rsem, 12 + k, (px, py, mc))
                  for k, (px, py) in enumerate(_other_chips(mx, my))]

        sems = (ssem, rsem, lsem)
        own, ici_send, ici_recv, fwd_send, fwd_recv, outs = _gather_copies(wown_ref, w_all, wf_ref, HR, sems, 0, 0)
        oown, o_send, o_recv, o_fsend, o_frecv, o_outs = _gather_copies(
            woown_ref, wo_all, wof_ref, woown_ref.shape[0] // 2, sems, 6, 5)

        @pl.when(first & (j == 0))
        def _():
            for cp in m_send:
                cp.start()
            own.start()
            oown.start()
            mp_all[s] = mp_ref[...]
            own.wait()
            ici_send[0].start()
            ici_send[1].start()
            outs[0].start()
            oown.wait()
            for cp in m_recv:
                cp.wait_recv()
            me = 4 * mx + 2 * my + mc
            mod_ref[...] = jnp.zeros_like(mod_ref)
            for p in range(N_SHARD):
                for r in range(B):
                    mod_ref[r:r + 1, p * MW:(p + 1) * MW] = mp_all[p, pl.ds(B * me + r, 1), :]
                mod_ref[B:B + 1, p * MW:(p + 1) * MW] = mp_all[p, NB:NB + 1, :]
            modo_ref[...] = mod_ref[...]

        for k in range(3):
            @pl.when(first & (j == k + 1))
            def _(k=k):
                if k == 0:
                    for cp in _bias_body(r_ref, bias_v, et_ref, bias_out, bsem):
                        cp.wait()
                ici_recv[k].wait_recv()
                if k == 0:
                    ici_send[2].start()
                fwd_send[k].start()
                fwd_recv[k].wait_recv()
                outs[1 + k].start()
                if k == 1:
                    for cp in o_send:
                        cp.start()
                if k == 2:
                    for got, fwd in zip(o_recv, o_fsend):
                        got.wait_recv()
                        fwd.start()

        tile = b * nt + t

        @pl.when(j == 0)
        def _():
            is_lat = t < nl
            ctx_tile = jnp.concatenate([ctx_ref[...], jnp.zeros((TI - LC, D), F32)], axis=0)
            xt = jnp.where(is_lat, x_ref[...], ctx_tile)
            mrow = mod_ref[pl.ds(jnp.where(is_lat, b, B), 1), :]
            shift, scale = mrow[:, 0:D], mrow[:, D:2 * D]
            rstd = lax.rsqrt(jnp.mean(xt * xt, axis=-1, keepdims=True) + EPS)
            h0 = ((xt * rstd * g_ref[...]) * (1.0 + scale) + shift).astype(BF16)
            h_ref[...] = h0
            hs_ref[tile] = h0

        shard = ord_ref[j]

        def project(sh, nrows):
            hb = hs_ref[tile, 0:nrows, :]
            cs, sn = cos_ref[0:nrows, :], sin_ref[0:nrows, :]
            for half in range(2):
                sec = 2 * sh + half
                acc = _dot(hb, w_all[sh, :, half * 512:(half + 1) * 512])
                if sec == 0:
                    acc = acc * (NA_DH ** -0.5)
                if sec in (4, 5):
                    for q in range(4):
                        a = acc[:, q * 128:(q + 1) * 128]
                        r = a * cs + pltpu.roll(a, 64, 1) * sn
                        if sec == 5:
                            r = r * kscale
                        p_ref[0:nrows, half * 512 + q * 128:half * 512 + (q + 1) * 128] = r.astype(BF16)
                else:
                    p_ref[0:nrows, half * 512:(half + 1) * 512] = acc.astype(BF16)

        for sh in range(N_SHARD):
            @pl.when((shard == sh) & (t < nl))
            def _(sh=sh):
                project(sh, TI)

            @pl.when((shard == sh) & (t == nl))
            def _(sh=sh):
                project(sh, LC)

        @pl.when((j == N_SHARD - 1) & (b == B - 1) & (t == nt - 1))
        def _():
            for cp in o_frecv:
                cp.wait_recv()
            for cp in o_outs:
                cp.start()
            _finish(outs + o_outs, ici_send + fwd_send + o_send + o_fsend + m_send, [])

    tok = lambda j, b, t, o: (jnp.where(j == 0, b, B - 1), jnp.where(j == 0, jnp.minimum(t, nl - 1), nl - 1), 0)
    grid_spec = pltpu.PrefetchScalarGridSpec(
        num_scalar_prefetch=1, grid=(N_SHARD, B, nt),
        in_specs=[
            pl.BlockSpec((None, TI, D), tok),
            pl.BlockSpec((None, LC, D), lambda j, b, t, o: (jnp.where(j == 0, b, B - 1), 0, 0)),
            pl.BlockSpec(mod_part.shape, lambda j, b, t, o: (0, 0)),
            pl.BlockSpec((1, D), lambda j, b, t, o: (0, 0)),
            ANY, ANY,
            pl.BlockSpec((TI, RET_DK), lambda j, b, t, o: (t, 0)),
            pl.BlockSpec((TI, RET_DK), lambda j, b, t, o: (t, 0)),
            pl.BlockSpec(memory_space=pltpu.SMEM),
        ],
        out_specs=(pl.BlockSpec((None, TI, D), lambda j, b, t, o: (b, t, o[j])),
                   pl.BlockSpec((None, TI, D), lambda j, b, t, o: (
                       jnp.where(j == 0, b, B - 1), jnp.where(j == 0, t, nt - 1), 0)), ANY, ANY,
                   pl.BlockSpec((8, 3 * D), lambda j, b, t, o: (0, 0)), ANY),
        scratch_shapes=[pltpu.VMEM((N_SHARD, D, D), BF16), pltpu.VMEM((N_SHARD,) + wout_b.shape, BF16),
                        pltpu.VMEM((B * nt, TI, D), BF16),
                        pltpu.VMEM((N_SHARD,) + mod_part.shape, F32), pltpu.VMEM((8, 3 * D), F32),
                        pltpu.VMEM((2, 3, TQ, KW), F32), pltpu.VMEM((15, GRID_W, GRID_W), F32),
                        pltpu.SemaphoreType.DMA((15,)), pltpu.SemaphoreType.DMA((15,)),
                        pltpu.SemaphoreType.DMA((10,)), pltpu.SemaphoreType.DMA((2,))])
    return pl.pallas_call(
        body, name="in_proj", grid_spec=grid_spec,
        out_shape=(jax.ShapeDtypeStruct((B, T, 4 * D), BF16), jax.ShapeDtypeStruct((B, T, D), BF16),
                   jax.ShapeDtypeStruct((N_SHARD, D, D), BF16),
                   jax.ShapeDtypeStruct((N_SHARD,) + wout_b.shape, BF16),
                   jax.ShapeDtypeStruct((8, 3 * D), F32),
                   jax.ShapeDtypeStruct((rpb_flat.shape[0], 3, TQ, KW), F32)),
        compiler_params=_params(("arbitrary",) * 3, vmem_mb=56))(
            order, x, ctx, mod_part, norm_g, win_b, wout_b, cos2, sin2, rpb_flat)


def _na_specs(L, T, rows, nh=2):
    nm = rows // 4
    w = nh * NA_DH
    per = 512 // w
    q_spec = pl.BlockSpec((None, TQ, w), lambda hp, b, m: (b, m, hp))
    k_spec = pl.BlockSpec((None, T, w), lambda hp, b, m: (b, 0, per + hp))
    v_spec = pl.BlockSpec((None, T, w), lambda hp, b, m: (b, 0, 2 * per + hp))
    g_spec = pl.BlockSpec((None, TQ, w), lambda hp, b, m: (b, m, 3 * per + hp))
    bias_spec = pl.BlockSpec((nh, 3, TQ, KW), lambda hp, b, m: (hp, 0, 0, 0))
    return nm, q_spec, k_spec, v_spec, g_spec, bias_spec


def _na_tile(m, nm, rows):
    typ = jnp.where(m == 0, 0, jnp.where(m == nm - 1, 2, 1))
    start = pl.multiple_of(jnp.clip(4 * m - 4, 0, rows - 12) * GRID_W, TQ)
    return typ, start


def _na_fwd_call(P, bias, L, LC):
    B, T, _ = P.shape
    rows = L // GRID_W
    NH = 4
    nm, q_spec, k_spec, v_spec, g_spec, bias_spec = _na_specs(L, T, rows, NH)

    def body(q_ref, k_ref, v_ref, g_ref, bias_ref, y_ref, o_ref):
        typ, start = _na_tile(pl.program_id(2), nm, rows)
        for hh in range(NH):
            ln = slice(hh * NA_DH, (hh + 1) * NA_DH)
            q = q_ref[:, ln]
            kw, vw = k_ref[pl.ds(start, KW), ln], v_ref[pl.ds(start, KW), ln]
            kc, vc = k_ref[L:L + LC, ln], v_ref[L:L + LC, ln]
            s1 = _dot_nt(q, kw) + bias_ref[hh, typ]
            s2 = _dot_nt(q, kc)
            mx = jnp.maximum(jnp.max(s1, axis=-1, keepdims=True), jnp.max(s2, axis=-1, keepdims=True))
            p1, p2 = jnp.exp(s1 - mx), jnp.exp(s2 - mx)
            inv = 1.0 / (jnp.sum(p1, axis=-1, keepdims=True) + jnp.sum(p2, axis=-1, keepdims=True))
            o = (_dot(p1.astype(BF16), vw) + _dot(p2.astype(BF16), vc)) * inv
            g = g_ref[:, ln].astype(F32)
            o_ref[:, ln] = o.astype(BF16)
            y_ref[:, ln] = (o * (g * _sigmoid(g))).astype(BF16)

    tile = pl.BlockSpec((None, TQ, NH * NA_DH), lambda hp, b, m: (b, m, hp))
    return pl.pallas_call(
        body, name="na_fwd", grid=(8 // NH, B, nm),
        in_specs=[q_spec, k_spec, v_spec, g_spec, bias_spec],
        out_specs=(tile, tile),
        out_shape=(jax.ShapeDtypeStruct((B, L, 512), BF16),) * 2,
        compiler_params=_params(("arbitrary",) * 3))(P, P, P, P, bias)


def _na_bwd_call(P, bias, dY, o_na, L, LC):
    B, T, _ = P.shape
    rows = L // GRID_W
    NH = 4
    W = NH * NA_DH
    nm, q_spec, k_spec, v_spec, g_spec, bias_spec = _na_specs(L, T, rows, NH)
    scale = NA_DH ** -0.5

    RB = 32

    def body(q_ref, k_ref, v_ref, g_ref, bias_ref, dy_ref, o_ref, dq_ref, dg_ref, dk_ref, dv_ref, dt_ref,
             db_ref, s1_ref, s2_ref, dp1_ref, dp2_ref, p1_ref, p2_ref, ds1_ref, ds2_ref, dkt_ref, dvt_ref):
        b, m = pl.program_id(1), pl.program_id(2)
        typ, start = _na_tile(m, nm, rows)

        @pl.when(m == 0)
        def _():
            dkt_ref[...] = jnp.zeros_like(dkt_ref)
            dvt_ref[...] = jnp.zeros_like(dvt_ref)

        @pl.when((m == 0) & (b == 0))
        def _():
            db_ref[...] = jnp.zeros_like(db_ref)

        for hh in range(NH):
            ln = slice(hh * NA_DH, (hh + 1) * NA_DH)
            q = q_ref[:, ln]
            kw, vw = k_ref[pl.ds(start, KW), ln], v_ref[pl.ds(start, KW), ln]
            kc, vc = k_ref[L:L + LC, ln], v_ref[L:L + LC, ln]
            g = g_ref[:, ln].astype(F32)
            sg = _sigmoid(g)
            dy = dy_ref[:, ln].astype(F32)
            do = (dy * (g * sg)).astype(BF16)
            s1_ref[hh] = _dot_nt(q, kw)
            s2_ref[hh] = _dot_nt(q, kc)
            dp1_ref[hh] = _dot_nt(do, vw)
            dp2_ref[hh] = _dot_nt(do, vc)

            def rows_pass(r, carry, hh=hh):
                rw = pl.ds(pl.multiple_of(r * RB, RB), RB)
                a = s1_ref[hh, rw, :] + bias_ref[hh, typ, rw, :]
                c = s2_ref[hh, rw, :]
                mx = jnp.maximum(jnp.max(a, axis=-1, keepdims=True), jnp.max(c, axis=-1, keepdims=True))
                e1, e2 = jnp.exp(a - mx), jnp.exp(c - mx)
                inv = 1.0 / (jnp.sum(e1, axis=-1, keepdims=True) + jnp.sum(e2, axis=-1, keepdims=True))
                p1, p2 = e1 * inv, e2 * inv
                p1_ref[hh, rw, :] = p1.astype(BF16)
                p2_ref[hh, rw, :] = p2.astype(BF16)
                dp1, dp2 = dp1_ref[hh, rw, :], dp2_ref[hh, rw, :]
                delta = jnp.sum(p1 * dp1, axis=-1, keepdims=True) + jnp.sum(p2 * dp2, axis=-1, keepdims=True)
                ds1 = p1 * (dp1 - delta)
                db_ref[hh, typ, rw, :] += ds1
                ds1_ref[hh, rw, :] = ds1.astype(BF16)
                ds2_ref[hh, rw, :] = (p2 * (dp2 - delta)).astype(BF16)
                return carry

            lax.fori_loop(0, TQ // RB, rows_pass, 0, unroll=True)
            p1b, p2b, ds1b, ds2b = p1_ref[hh], p2_ref[hh], ds1_ref[hh], ds2_ref[hh]
            dg_ref[:, ln] = (dy * o_ref[:, ln].astype(F32) * (sg * (1.0 + g * (1.0 - sg)))).astype(BF16)
            dq_ref[:, ln] = ((_dot(ds1b, kw) + _dot(ds2b, kc)) * scale).astype(BF16)
            dkt_ref[ln, pl.ds(start, KW)] += _dot_tn(q, ds1b)
            dvt_ref[ln, pl.ds(start, KW)] += _dot_tn(do, p1b)
            dkt_ref[ln, L:L + LC] += _dot_tn(q, ds2b)
            dvt_ref[ln, L:L + LC] += _dot_tn(do, p2b)

        @pl.when(m == nm - 1)
        def _():
            dk_ref[...] = dkt_ref[...].T
            dv_ref[...] = dvt_ref[...].T

        @pl.when((m == nm - 1) & (b == B - 1))
        def _():
            for hh in range(NH):
                for dr, t in _bias_tile_sums(db_ref, hh).items():
                    dt_ref[hh, dr + 7] = t

    tile = pl.BlockSpec((None, TQ, W), lambda hp, b, m: (b, m, hp))
    kv_out = pl.BlockSpec((None, T, W), lambda hp, b, m: (b, 0, hp))
    wide, narrow = (NH, TQ, KW), (NH, TQ, LC)
    return pl.pallas_call(
        body, name="na_bwd", grid=(8 // NH, B, nm),
        in_specs=[q_spec, k_spec, v_spec, g_spec, bias_spec, tile, tile],
        out_specs=(tile, tile, kv_out, kv_out,
                   pl.BlockSpec((NH, 15, GRID_W, GRID_W), lambda hp, b, m: (hp, 0, 0, 0))),
        out_shape=(jax.ShapeDtypeStruct((B, L, 512), BF16), jax.ShapeDtypeStruct((B, L, 512), BF16),
                   jax.ShapeDtypeStruct((B, T, 512), F32), jax.ShapeDtypeStruct((B, T, 512), F32),
                   jax.ShapeDtypeStruct((bias.shape[0], 15, GRID_W, GRID_W), F32)),
        scratch_shapes=[pltpu.VMEM((NH,) + bias.shape[1:], F32),
                        pltpu.VMEM(wide, F32), pltpu.VMEM(narrow, F32), pltpu.VMEM(wide, F32), pltpu.VMEM(narrow, F32),
                        pltpu.VMEM(wide, BF16), pltpu.VMEM(narrow, BF16), pltpu.VMEM(wide, BF16),
                        pltpu.VMEM(narrow, BF16), pltpu.VMEM((W, T), F32), pltpu.VMEM((W, T), F32)],
        compiler_params=_params(("arbitrary",) * 3, vmem_mb=60))(P, P, P, P, bias, dY, o_na)


def _head_scalar(dec_ref, h):
    lane = lax.broadcasted_iota(jnp.int32, dec_ref.shape, 1)
    return -jnp.sum(jnp.where(lane == h, jnp.exp(dec_ref[...]), 0.0), axis=1, keepdims=True)


def _chunk_decay(lgf, lgb):
    tau = lax.broadcasted_iota(jnp.int32, (TQ, 1), 0).astype(F32)
    sig = lax.broadcasted_iota(jnp.int32, (1, TQ), 1).astype(F32)
    dist = tau - sig
    dm = jnp.exp(dist * jnp.where(dist > 0, lgf, -lgb)) * jnp.where(dist == 0, 2.0, 1.0)
    return tau, dist, dm


def _ret_states_call(P, dec_f, dec_b, L, LC):
    B, T, _ = P.shape
    n = L // TQ

    def body(df_ref, db_ref, k_ref, v_ref, sf_ref, sb_ref):
        h = pl.program_id(1)
        lgf, lgb = _head_scalar(df_ref, h), _head_scalar(db_ref, h)
        tau = lax.broadcasted_iota(jnp.int32, (TQ, 1), 0).astype(F32)
        jc = lax.broadcasted_iota(jnp.int32, (LC, 1), 0).astype(F32)
        wf, wb = jnp.exp(lgf * (TQ - 1.0 - tau)), jnp.exp(lgb * tau)
        gcf, gcb = jnp.exp(lgf * float(TQ)), jnp.exp(lgb * float(TQ))
        kc, vc = k_ref[L:L + LC, :].astype(F32), v_ref[L:L + LC, :]

        def chunk_state(i, w):
            ks = pl.multiple_of(i * TQ, TQ)
            return _dot_tn((k_ref[pl.ds(ks, TQ), :].astype(F32) * w).astype(BF16), v_ref[pl.ds(ks, TQ), :])

        def fwd(i, s):
            sf_ref[i] = s
            return gcf * s + chunk_state(i, wf)

        lax.fori_loop(0, n, fwd, _dot_tn((kc * jnp.exp(lgf * (LC - 1.0 - jc))).astype(BF16), vc), unroll=True)

        def bwd(r, s):
            i = n - 1 - r
            sb_ref[i] = s
            return gcb * s + chunk_state(i, wb)

        lax.fori_loop(0, n, bwd, _dot_tn((kc * jnp.exp(lgb * jc)).astype(BF16), vc), unroll=True)

    st = pl.BlockSpec((None, None, n, RET_DK, RET_DK), lambda b, h: (b, h, 0, 0, 0))
    return pl.pallas_call(
        body, name="ret_states", grid=(B, 4),
        in_specs=[pl.BlockSpec((1, 4), lambda b, h: (0, 0)), pl.BlockSpec((1, 4), lambda b, h: (0, 0)),
                  pl.BlockSpec((None, T, 128), lambda b, h: (b, 0, 20 + h)),
                  pl.BlockSpec((None, T, 128), lambda b, h: (b, 0, 24 + h))],
        out_specs=(st, st),
        out_shape=(jax.ShapeDtypeStruct((B, 4, n, RET_DK, RET_DK), F32),) * 2,
        compiler_params=_params(("arbitrary",) * 2))(dec_f, dec_b, P, P)


def _retc_fwd_call(P, sf, sb, dec_f, dec_b, ret_norm_g, L):
    B, T, _ = P.shape
    sec = lambda k: pl.BlockSpec((None, TQ, 512), lambda b, i: (b, i, k))
    dec_spec = pl.BlockSpec((1, 4), lambda b, i: (0, 0))
    st_spec = pl.BlockSpec((None, 4, None, RET_DK, RET_DK), lambda b, i: (b, 0, i, 0, 0))

    def body(df_ref, db_ref, q_ref, k_ref, v_ref, g_ref, gn_ref, sf_ref, sb_ref, y_ref, o_ref):
        for h in range(4):
            ln = slice(h * RET_DK, (h + 1) * RET_DK)
            lgf, lgb = _head_scalar(df_ref, h), _head_scalar(db_ref, h)
            tau, _, dm = _chunk_decay(lgf, lgb)
            q = q_ref[:, ln]
            qf = q.astype(F32)
            acc = _dot((_dot_nt(q, k_ref[:, ln]) * dm).astype(BF16), v_ref[:, ln])
            acc = acc + _dot((qf * jnp.exp(lgf * (tau + 1.0))).astype(BF16), sf_ref[h].astype(BF16))
            acc = acc + _dot((qf * jnp.exp(lgb * (TQ - tau))).astype(BF16), sb_ref[h].astype(BF16))
            o_ref[:, ln] = acc
            rn = lax.rsqrt(jnp.mean(acc * acc, axis=-1, keepdims=True) + EPS)
            g = g_ref[:, ln].astype(F32)
            y_ref[:, ln] = ((acc * rn * gn_ref[:, ln]) * (g * _sigmoid(g))).astype(BF16)

    tile = pl.BlockSpec((None, TQ, 512), lambda b, i: (b, i, 0))
    return pl.pallas_call(
        body, name="ret_fwd", grid=(B, L // TQ),
        in_specs=[dec_spec, dec_spec, sec(4), sec(5), sec(6), sec(7),
                  pl.BlockSpec((1, 512), lambda b, i: (0, 0)), st_spec, st_spec],
        out_specs=(tile, tile),
        out_shape=(jax.ShapeDtypeStruct((B, L, 512), BF16), jax.ShapeDtypeStruct((B, L, 512), F32)),
        compiler_params=_params(("arbitrary",) * 2))(dec_f, dec_b, P, P, P, P, ret_norm_g, sf, sb)


def _retc_bwd_call(P, sf, sb, dec_f, dec_b, ret_norm_g, o_ret, dY, cos2, sin2, L, LC):
    B, T, _ = P.shape
    n = L // TQ
    C = float(TQ)
    kscale = RET_DK ** -0.5
    st_spec = pl.BlockSpec((None, 4, n, RET_DK, RET_DK), lambda b, i: (b, 0, 0, 0, 0))

    def body(df_ref, db_ref, q_ref, k_ref, v_ref, g_ref, gn_ref, o_ref, dy_ref, cos_ref, sin_ref, sf_ref, sb_ref,
             dq_ref, dg_ref, dk_ref, dv_ref, dgn_ref, dlg_ref, dsf_ref, dsb_ref):
        i = pl.program_id(1)

        @pl.when(i == 0)
        def _():
            dk_ref[...] = jnp.zeros_like(dk_ref)
            dv_ref[...] = jnp.zeros_like(dv_ref)
            dgn_ref[...] = jnp.zeros_like(dgn_ref)
            dlg_ref[...] = jnp.zeros_like(dlg_ref)

        rows = pl.ds(pl.multiple_of(i * TQ, TQ), TQ)
        cs, sn = cos_ref[rows, :], sin_ref[rows, :]

        def one_head(h):
            ln = slice(h * RET_DK, (h + 1) * RET_DK)
            lgf, lgb = _head_scalar(df_ref, h), _head_scalar(db_ref, h)
            tau, dist, dm = _chunk_decay(lgf, lgb)

            def add_lg(row, x):
                csum = jnp.sum(x, axis=0, keepdims=True)
                tot = csum[:, 0:128]
                for part in range(1, x.shape[1] // 128):
                    tot = tot + csum[:, part * 128:(part + 1) * 128]
                dlg_ref[h, row:row + 1, :] += tot

            q = q_ref[:, ln]
            qf = q.astype(F32)
            o = o_ref[:, ln]
            g = g_ref[:, ln].astype(F32)
            dy = dy_ref[:, ln].astype(F32)
            gn = gn_ref[:, ln]
            sg = _sigmoid(g)
            rn = lax.rsqrt(jnp.mean(o * o, axis=-1, keepdims=True) + EPS)
            nrm = o * rn
            dg_ref[:, ln] = (dy * (nrm * gn) * (sg * (1.0 + g * (1.0 - sg)))).astype(BF16)
            dhn = dy * (g * sg)
            dgn_ref[:, ln] += jnp.sum(dhn * nrm, axis=0, keepdims=True)
            dnrm = dhn * gn
            do = rn * (dnrm - nrm * jnp.mean(dnrm * nrm, axis=-1, keepdims=True))
            dob = do.astype(BF16)
            ki, vi = k_ref[rows, ln], v_ref[rows, ln]
            s = _dot_nt(q, ki)
            dsv = _dot_nt(dob, vi)
            dsb = (dsv * dm).astype(BF16)
            dk_ref[rows, ln] += _dot_tn(dsb, q)
            dv_ref[rows, ln] += _dot_tn((s * dm).astype(BF16), dob)
            xw = s * dsv * dm * jnp.abs(dist)
            fpart = jnp.where(dist > 0, xw, 0.0)
            add_lg(0, fpart)
            add_lg(1, xw - fpart)
            dq = _dot(dsb, ki)
            af, ab = jnp.exp(lgf * (tau + 1.0)), jnp.exp(lgb * (C - tau))
            qa, qb = (qf * af).astype(BF16), (qf * ab).astype(BF16)
            sfi, sbi = sf_ref[h, i].astype(BF16), sb_ref[h, i].astype(BF16)
            dq = dq + af * _dot_nt(dob, sfi) + ab * _dot_nt(dob, sbi)
            dsf_ref[h, i] = _dot_tn(qa, dob)
            dsb_ref[h, i] = _dot_tn(qb, dob)
            add_lg(0, (tau + 1.0) * (_dot(qa, sfi) * do))
            add_lg(1, (C - tau) * (_dot(qb, sbi) * do))
            dq_ref[:, ln] = (dq * cs - pltpu.roll(dq, 64, 1) * sn).astype(BF16)

            @pl.when(i == n - 1)
            def _():
                jc = lax.broadcasted_iota(jnp.int32, (LC, 1), 0).astype(F32)
                crow = pl.ds(L, LC)

                def through_state(rws, w, dw, gst, row):
                    kk, vv = k_ref[rws, ln].astype(F32), v_ref[rws, ln]
                    gb = gst.astype(BF16)
                    vg = _dot_nt(vv, gb)
                    kw = kk * w
                    dk_ref[rws, ln] += w * vg
                    dv_ref[rws, ln] += _dot(kw.astype(BF16), gb)
                    add_lg(row, dw * (kw * vg))

                def scan(gc, w, dw, st_ref, dst_ref, order, row):
                    def step(r, gst):
                        j = order(r)
                        through_state(pl.ds(pl.multiple_of(j * TQ, TQ), TQ), w, dw, gst, row)
                        add_lg(row, (C * gc) * (gst * st_ref[h, j]))
                        return dst_ref[h, j] + gc * gst
                    return lax.fori_loop(0, n, step, jnp.zeros((RET_DK, RET_DK), F32), unroll=True)

                gcf, gcb = jnp.exp(lgf * C), jnp.exp(lgb * C)
                g0 = scan(gcf, jnp.exp(lgf * (C - 1.0 - tau)), C - 1.0 - tau, sf_ref, dsf_ref,
                          lambda r: n - 1 - r, 0)
                through_state(crow, jnp.exp(lgf * (LC - 1.0 - jc)), LC - 1.0 - jc, g0, 0)
                g1 = scan(gcb, jnp.exp(lgb * tau), tau, sb_ref, dsb_ref, lambda r: r, 1)
                through_state(crow, jnp.exp(lgb * jc), jc, g1, 1)
                dk = dk_ref[:, ln]
                dk_ref[:, ln] = (dk * cos_ref[...] - pltpu.roll(dk, 64, 1) * sin_ref[...]) * kscale

        for h in range(4):
            one_head(h)

    sec = lambda k: pl.BlockSpec((None, TQ, 512), lambda b, i: (b, i, k))
    full = lambda k: pl.BlockSpec((None, T, 512), lambda b, i: (b, 0, k))
    dec_spec = pl.BlockSpec((1, 4), lambda b, i: (0, 0))
    tab = pl.BlockSpec((T, RET_DK), lambda b, i: (0, 0))
    return pl.pallas_call(
        body, name="ret_bwd", grid=(B, n),
        in_specs=[dec_spec, dec_spec, sec(4), full(5), full(6), sec(7),
                  pl.BlockSpec((1, 512), lambda b, i: (0, 0)), sec(0), sec(1), tab, tab, st_spec, st_spec],
        out_specs=(sec(0), sec(0), full(0), full(0),
                   pl.BlockSpec((None, 1, 512), lambda b, i: (b, 0, 0)),
                   pl.BlockSpec((None, 4, 8, 128), lambda b, i: (b, 0, 0, 0))),
        out_shape=(jax.ShapeDtypeStruct((B, L, 512), BF16), jax.ShapeDtypeStruct((B, L, 512), BF16),
                   jax.ShapeDtypeStruct((B, T, 512), F32), jax.ShapeDtypeStruct((B, T, 512), F32),
                   jax.ShapeDtypeStruct((B, 1, 512), F32), jax.ShapeDtypeStruct((B, 4, 8, 128), F32)),
        scratch_shapes=[pltpu.VMEM((4, n, RET_DK, RET_DK), F32), pltpu.VMEM((4, n, RET_DK, RET_DK), F32)],
        compiler_params=_params(("arbitrary",) * 2, vmem_mb=56))(
            dec_f, dec_b, P, P, P, P, ret_norm_g, o_ret, dY, cos2, sin2, sf, sb)


def _out_call(y_na, y_ret, x, target, mod, final_g, wout_f):
    B, L, _ = x.shape
    TO = 4 * TQ

    def body(yn_ref, yr_ref, x_ref, t_ref, mod_ref, gf_ref, w_ref, dy_ref, dx2_ref, dwb_ref, sm_ref, dw_ref):
        b, i = pl.program_id(0), pl.program_id(1)

        @pl.when((b == 0) & (i == 0))
        def _():
            dw_ref[...] = jnp.zeros_like(dw_ref)
            sm_ref[...] = jnp.zeros_like(sm_ref)

        gate = mod_ref[pl.ds(b, 1), 2 * D:3 * D]
        gf = gf_ref[...]
        yn, yr = yn_ref[...], yr_ref[...]
        ylat = _dot(yn, w_ref[0:512, :]) + _dot(yr, w_ref[512:1024, :])
        x2 = x_ref[...] + gate * ylat
        r = lax.rsqrt(jnp.mean(x2 * x2, axis=-1, keepdims=True) + EPS)
        xr = x2 * r
        err = xr * gf - t_ref[...]
        sm_ref[1:2, :] += jnp.sum(err * err, axis=0, keepdims=True)
        dout = err * (1.0 / D)
        sm_ref[0:1, :] += jnp.sum(dout * xr, axis=0, keepdims=True)
        gd = dout * gf
        dx2 = r * (gd - xr * jnp.mean(gd * xr, axis=-1, keepdims=True))
        dx2_ref[...] = dx2
        sm_ref[pl.ds(2 + b, 1), :] += jnp.sum(dx2 * ylat, axis=0, keepdims=True)
        dyl = (gate * dx2).astype(BF16)
        dy_ref[:, 0:512] = _dot_nt(dyl, w_ref[0:512, :]).astype(BF16)
        dy_ref[:, 512:1024] = _dot_nt(dyl, w_ref[512:1024, :]).astype(BF16)
        dw_ref[0:512, :] += _dot_tn(yn, dyl)
        dw_ref[512:1024, :] += _dot_tn(yr, dyl)

        @pl.when((b == B - 1) & (i == L // TO - 1))
        def _():
            dwb_ref[...] = dw_ref[...].astype(BF16)

    half = pl.BlockSpec((None, TO, 512), lambda b, i: (b, i, 0))
    full = pl.BlockSpec((None, TO, D), lambda b, i: (b, i, 0))
    return pl.pallas_call(
        body, name="out_proj_loss", grid=(B, L // TO),
        in_specs=[half, half, full, full,
                  pl.BlockSpec((8, 3 * D), lambda b, i: (0, 0)),
                  pl.BlockSpec((1, D), lambda b, i: (0, 0)),
                  pl.BlockSpec((D, D), lambda b, i: (0, 0))],
        out_specs=(full, full, pl.BlockSpec((D, D), lambda b, i: (0, 0)),
                   pl.BlockSpec((8, D), lambda b, i: (0, 0))),
        out_shape=(jax.ShapeDtypeStruct((B, L, D), BF16), jax.ShapeDtypeStruct((B, L, D), F32),
                   jax.ShapeDtypeStruct((D, D), BF16), jax.ShapeDtypeStruct((8, D), F32)),
        scratch_shapes=[pltpu.VMEM((D, D), F32)],
        compiler_params=_params(("arbitrary",) * 2))(y_na, y_ret, x, target, mod, final_g, wout_f)


def _dh_call(dsec, win_f, x, ctx, dx2, mod, norm_g, cp_in, cp_out):
    B, L, _ = x.shape
    LC = ctx.shape[1]
    nl = L // TQ

    def body(d0, d1, d2, d3, d4, d5, d6, d7, w_ref, x_ref, ctx_ref, dx2_ref, mod_ref, g_ref, cpi_ref, cpo_ref,
             gx_ref, sm_ref, sli_ref, slo_ref, ssem, rsem, lsem):
        drefs = (d0, d1, d2, d3, d4, d5, d6, d7)
        b, t = pl.program_id(0), pl.program_id(1)
        is_lat = t < nl

        @pl.when((b == 0) & (t == 0))
        def _():
            sm_ref[...] = jnp.zeros_like(sm_ref)

        def dh_of(secs):
            acc = jnp.zeros((TQ, D), F32)
            for sec in secs:
                s, half = divmod(sec, 2)
                acc = acc + _dot_nt(drefs[sec][...].astype(BF16), w_ref[s, :, half * 512:(half + 1) * 512])
            return acc

        def norm_bwd(dh, xt, mrow):
            scale = mrow[:, D:2 * D]
            g = g_ref[...]
            rstd = lax.rsqrt(jnp.mean(xt * xt, axis=-1, keepdims=True) + EPS)
            xn = xt * rstd
            dshift = jnp.sum(dh, axis=0, keepdims=True)
            dscale = jnp.sum(dh * (xn * g), axis=0, keepdims=True)
            dhn = dh * (1.0 + scale)
            sm_ref[0:1, :] += jnp.sum(dhn * xn, axis=0, keepdims=True)
            dxn = dhn * g
            dx = rstd * (dxn - xn * jnp.mean(dxn * xn, axis=-1, keepdims=True))
            return dshift, dscale, dx

        @pl.when(is_lat)
        def _():
            dshift, dscale, dx = norm_bwd(dh_of(range(8)), x_ref[...], mod_ref[pl.ds(b, 1), :])
            sm_ref[pl.ds(3 + b, 1), :] += dshift
            sm_ref[pl.ds(3 + B + b, 1), :] += dscale
            gx_ref[...] = dx2_ref[...] + dx

        @pl.when(jnp.logical_not(is_lat))
        def _():
            dshift, dscale, _ = norm_bwd(dh_of((1, 2, 5, 6)), ctx_ref[...], mod_ref[B:B + 1, :])
            sm_ref[1:2, :] += dshift
            sm_ref[2:3, :] += dscale

        mx, my, mc = _mesh_pos()
        s = 2 * mx + my
        cps, sls = (cpi_ref, cpo_ref), (sli_ref, slo_ref)
        own = [pltpu.make_async_copy(cps[a].at[s], sls[a].at[s], lsem.at[a]) for a in range(2)]
        sends, recvs, k = [], [], 0
        for px, py in _other_chips(mx, my):
            ps = 2 * px + py
            for a in range(2):
                sends.append(_remote(cps[a].at[ps], sls[a].at[s], ssem, rsem, k, (px, py, mc)))
                recvs.append(_remote(cps[a].at[s], sls[a].at[ps], ssem, rsem, k, (px, py, mc)))
                k += 1

        @pl.when((b == 0) & (t == 0))
        def _():
            for cp in own + sends:
                cp.start()

        @pl.when((b == B - 1) & (t == nl))
        def _():
            _finish(own, sends, recvs)

    lat = lambda b, t: (b, jnp.minimum(t, nl - 1), 0)
    tok = lambda b, t: (b, t, 0)
    sec_specs = [pl.BlockSpec((None, TQ, 512), lat if sec in (0, 3, 4, 7) else tok) for sec in range(8)]
    return pl.pallas_call(
        body, name="dh_norm_bwd", grid=(B, nl + 1),
        in_specs=sec_specs + [
            pl.BlockSpec((N_SHARD, D, D), lambda b, t: (0, 0, 0)),
            pl.BlockSpec((None, TQ, D), lat),
            pl.BlockSpec((None, LC, D), lambda b, t: (b, 0, 0)),
            pl.BlockSpec((None, TQ, D), lat),
            pl.BlockSpec((8, 3 * D), lambda b, t: (0, 0)),
            pl.BlockSpec((1, D), lambda b, t: (0, 0)), ANY, ANY],
        out_specs=(pl.BlockSpec((None, TQ, D), lat), pl.BlockSpec((8, D), lambda b, t: (0, 0)), ANY, ANY),
        out_shape=(jax.ShapeDtypeStruct((B, L, D), F32), jax.ShapeDtypeStruct((8, D), F32),
                   jax.ShapeDtypeStruct(cp_in.shape, cp_in.dtype), jax.ShapeDtypeStruct(cp_out.shape, cp_out.dtype)),
        scratch_shapes=[pltpu.SemaphoreType.DMA((6,)), pltpu.SemaphoreType.DMA((6,)),
                        pltpu.SemaphoreType.DMA((2,))],
        compiler_params=_params(("arbitrary",) * 2))(*dsec, win_f, x, ctx, dx2, mod, norm_g, cp_in, cp_out)


def _dw_call(dsec, h, L):
    B, T, _ = h.shape
    TW = 2 * TQ
    nl = L // TW
    KV = (1, 2, 5, 6)

    def body(d0, d1, d2, d3, d4, d5, d6, d7, c1, c2, c5, c6, h_ref, hc_ref, dw_ref, acc_ref):
        drefs = (d0, d1, d2, d3, d4, d5, d6, d7)
        crefs = dict(zip(KV, (c1, c2, c5, c6)))
        b, t = pl.program_id(0), pl.program_id(1)

        @pl.when((b == 0) & (t == 0))
        def _():
            acc_ref[...] = jnp.zeros_like(acc_ref)

        def add(hb, refs, secs):
            for sec in secs:
                s, half = divmod(sec, 2)
                acc_ref[s, :, half * 512:(half + 1) * 512] += _dot_tn(hb, refs[sec][...].astype(BF16))

        @pl.when(t < nl)
        def _():
            add(h_ref[...], drefs, range(8))

        @pl.when(t == nl)
        def _():
            add(hc_ref[...], crefs, KV)

        @pl.when((b == B - 1) & (t == nl))
        def _():
            dw_ref[...] = acc_ref[...].astype(BF16)

    lat = lambda b, t: (b, jnp.minimum(t, nl - 1), 0)
    ctx = lambda b, t: (b, L // TQ, 0)
    return pl.pallas_call(
        body, name="dw_in", grid=(B, nl + 1),
        in_specs=[pl.BlockSpec((None, TW, 512), lat)] * 8 + [pl.BlockSpec((None, TQ, 512), ctx)] * 4
        + [pl.BlockSpec((None, TW, D), lat), pl.BlockSpec((None, TQ, D), ctx)],
        out_specs=pl.BlockSpec((N_SHARD, D, D), lambda b, t: (0, 0, 0)),
        out_shape=jax.ShapeDtypeStruct((N_SHARD, D, D), BF16),
        scratch_shapes=[pltpu.VMEM((N_SHARD, D, D), F32)],
        compiler_params=_params(("arbitrary",) * 2, vmem_mb=60))(*dsec, *[dsec[k] for k in KV], h, h)


def _mesh_pos():
    return lax.axis_index("x"), lax.axis_index("y"), lax.axis_index("c")


def _flip(v, f):
    return 1 - v if f else v


def _remote(src, dst, ssem, rsem, k, peer):
    return pltpu.make_async_remote_copy(src_ref=src, dst_ref=dst, send_sem=ssem.at[k], recv_sem=rsem.at[k],
                                        device_id=peer, device_id_type=MESH)


def _other_chips(x, y):
    return [(_flip(x, fx), _flip(y, fy)) for fx, fy in ((1, 0), (0, 1), (1, 1))]


def _gather_copies(own_ref, all_ref, out_ref, hr, sems, k0, l0):
    ssem, rsem, lsem = sems
    mx, my, mc = _mesh_pos()
    s = 2 * mx + my
    sib = (mx, my, 1 - mc)
    own = pltpu.make_async_copy(own_ref, all_ref.at[s], lsem.at[l0])
    send, recv, fsend, frecv = [], [], [], []
    outs = [pltpu.make_async_copy(all_ref.at[s], out_ref.at[s], lsem.at[l0 + 1])]
    for k, (px, py) in enumerate(_other_chips(mx, my)):
        ps = 2 * px + py
        mine = all_ref.at[s, pl.ds(mc * hr, hr)]
        send.append(_remote(mine, mine, ssem, rsem, k0 + k, (px, py, mc)))
        got = all_ref.at[ps, pl.ds(mc * hr, hr)]
        recv.append(_remote(mine, got, ssem, rsem, k0 + k, (px, py, mc)))
        fsend.append(_remote(got, got, ssem, rsem, k0 + 3 + k, sib))
        theirs = all_ref.at[ps, pl.ds((1 - mc) * hr, hr)]
        frecv.append(_remote(theirs, theirs, ssem, rsem, k0 + 3 + k, sib))
        outs.append(pltpu.make_async_copy(all_ref.at[ps], out_ref.at[ps], lsem.at[l0 + 2 + k]))
    return own, send, recv, fsend, frecv, outs


def _all_to_all_small(src, dst_all, ssem, rsem, k0, x, y, cc):
    me = 4 * x + 2 * y + cc
    sends, recvs = [], []
    for f in range(1, N_DEV):
        px, py, pc = _flip(x, f & 4), _flip(y, f & 2), _flip(cc, f & 1)
        sends.append(_remote(src, dst_all.at[me], ssem, rsem, k0 + f - 1, (px, py, pc)))
        recvs.append(_remote(src, dst_all.at[4 * px + 2 * py + pc], ssem, rsem, k0 + f - 1, (px, py, pc)))
    return sends, recvs


def _finish(local, sends, recvs):
    for cp in recvs:
        cp.wait_recv()
    for cp in sends:
        cp.wait_send()
    for cp in local:
        cp.wait()


def _c_gather_call(c):
    def body(c_ref, c_all, ssem, rsem, lsem):
        x, y, cc = _mesh_pos()
        me = 4 * x + 2 * y + cc
        local = [pltpu.make_async_copy(c_ref, c_all.at[me], lsem.at[0])]
        c_send, c_recv = _all_to_all_small(c_ref, c_all, ssem, rsem, 0, x, y, cc)
        for cp in local + c_send:
            cp.start()
        _finish(local, c_send, c_recv)

    vmem = pl.BlockSpec(memory_space=pltpu.VMEM)
    return pl.pallas_call(
        body, name="c_gather", in_specs=[vmem], out_specs=vmem,
        out_shape=jax.ShapeDtypeStruct((N_DEV,) + c.shape, c.dtype),
        scratch_shapes=[pltpu.SemaphoreType.DMA((N_DEV - 1,)), pltpu.SemaphoreType.DMA((N_DEV - 1,)),
                        pltpu.SemaphoreType.DMA((1,))])(c)


VROWS = 32


def _grad_halves_call(dwin_b, dwout_b, dbias, dlg):
    arrs = (dwin_b, dwout_b)
    hrs = [a.shape[1] // 2 for a in arrs]

    def body(din, dout, db_ref, dlg_ref, cp_in, cp_out, drpb_ref, dlgo_ref, got_in, got_out, p_ref, ssem, rsem):
        x, y, cc = _mesh_pos()
        sib = (x, y, 1 - cc)
        srcs, gots, cps = (din, dout), (got_in, got_out), (cp_in, cp_out)
        halves = [_remote(srcs[a].at[:, pl.ds((1 - cc) * hrs[a], hrs[a])], gots[a], ssem, rsem, a, sib)
                  for a in range(2)]
        for cp in halves:
            cp.start()
        _small_reduce_body(db_ref, dlg_ref, drpb_ref, dlgo_ref, p_ref)
        for cp in halves:
            cp.wait_recv()
        for a in range(2):
            for j in range(N_SHARD):
                def add(i, carry, a=a, j=j):
                    r = pl.multiple_of(i * VROWS, VROWS)
                    mine = srcs[a][j, pl.ds(pl.multiple_of(cc * hrs[a] + r, VROWS), VROWS), :].astype(F32)
                    cps[a][j, pl.ds(r, VROWS), :] = (
                        mine + gots[a][j, pl.ds(r, VROWS), :].astype(F32)).astype(BF16)
                    return carry
                lax.fori_loop(0, hrs[a] // VROWS, add, 0)
        for cp in halves:
            cp.wait_send()

    vmem = pl.BlockSpec(memory_space=pltpu.VMEM)
    half_shapes = [(N_SHARD, hrs[a], arrs[a].shape[2]) for a in range(2)]
    return pl.pallas_call(
        body, name="grad_halves",
        in_specs=[vmem] * 4, out_specs=(vmem,) * 4,
        out_shape=(jax.ShapeDtypeStruct(half_shapes[0], BF16), jax.ShapeDtypeStruct(half_shapes[1], BF16),
                   jax.ShapeDtypeStruct((dbias.shape[0], 16, 32), F32), jax.ShapeDtypeStruct((32, 128), F32)),
        scratch_shapes=[pltpu.VMEM(half_shapes[0], BF16), pltpu.VMEM(half_shapes[1], BF16),
                        pltpu.VMEM((32, GRID_W), F32),
                        pltpu.SemaphoreType.DMA((2,)), pltpu.SemaphoreType.DMA((2,))],
        compiler_params=pltpu.CompilerParams(vmem_limit_bytes=56 << 20))(dwin_b, dwout_b, dbias, dlg)


def _grad_finish_call(sl_in, sl_out, small, wada_b):
    arrs = (sl_in, sl_out)
    ws = wada_b.shape[1]

    def body(sin, sout, sm, wa_ref, gin, gout, sm_all, dparts, h_in, h_out, dp_own, ssem, rsem, lsem):
        x, y, cc = _mesh_pos()
        me = 4 * x + 2 * y + cc
        s = 2 * x + y
        sib = (x, y, 1 - cc)
        sls, hs, gs = (sin, sout), (h_in, h_out), (gin, gout)
        sm_send, sm_recv = _all_to_all_small(sm, sm_all, ssem, rsem, 2, x, y, cc)
        sm_own = pltpu.make_async_copy(sm, sm_all.at[me], lsem.at[0])
        for cp in sm_send + [sm_own]:
            cp.start()
        for a in range(2):
            def total(i, carry, a=a):
                rows = pl.ds(pl.multiple_of(i * VROWS, VROWS), VROWS)
                sl = sls[a]
                hs[a][rows, :] = ((sl[0, rows, :].astype(F32) + sl[1, rows, :].astype(F32))
                                  + sl[2, rows, :].astype(F32)) + sl[3, rows, :].astype(F32)
                return carry
            lax.fori_loop(0, arrs[a].shape[1] // VROWS, total, 0)
        mine = [pltpu.make_async_copy(hs[a], gs[a].at[cc], lsem.at[1 + a]) for a in range(2)]
        back = [_remote(hs[a], gs[a].at[cc], ssem, rsem, a, sib) for a in range(2)]
        back_recv = [_remote(hs[a], gs[a].at[1 - cc], ssem, rsem, a, sib) for a in range(2)]
        for cp in mine + back:
            cp.start()
        sm_own.wait()
        for cp in sm_recv:
            cp.wait_recv()
        shift_c = sm_all[0, R_SHIFT_C:R_SHIFT_C + 1, :]
        scale_c = sm_all[0, R_SCALE_C:R_SCALE_C + 1, :]
        for dv in range(1, N_DEV):
            shift_c = shift_c + sm_all[dv, R_SHIFT_C:R_SHIFT_C + 1, :]
            scale_c = scale_c + sm_all[dv, R_SCALE_C:R_SCALE_C + 1, :]
        dmc = jnp.concatenate([shift_c, scale_c, jnp.zeros((1, D), F32)], axis=1).astype(BF16)
        dmc = jnp.broadcast_to(dmc, (8, 3 * D))
        for sh in range(N_SHARD):
            @pl.when(s == sh)
            def _(sh=sh):
                dp_own[...] = _dot_nt(dmc[:, sh * ws:(sh + 1) * ws], wa_ref[...])
        dparts[s] = dp_own[...]
        d_send = [_remote(dp_own, dparts.at[s], ssem, rsem, 9 + k, (px, py, cc))
                  for k, (px, py) in enumerate(_other_chips(x, y))]
        d_recv = [_remote(dp_own, dparts.at[2 * px + py], ssem, rsem, 9 + k, (px, py, cc))
                  for k, (px, py) in enumerate(_other_chips(x, y))]
        for cp in d_send:
            cp.start()
        _finish(mine, back + sm_send + d_send, back_recv + d_recv)

    vmem = pl.BlockSpec(memory_space=pltpu.VMEM)
    return pl.pallas_call(
        body, name="grad_finish",
        in_specs=[vmem] * 4, out_specs=(vmem,) * 4,
        out_shape=(jax.ShapeDtypeStruct((2,) + sl_in.shape[1:], F32),
                   jax.ShapeDtypeStruct((2,) + sl_out.shape[1:], F32),
                   jax.ShapeDtypeStruct((N_DEV,) + small.shape, F32),
                   jax.ShapeDtypeStruct((N_SHARD, 8, D), F32)),
        scratch_shapes=[pltpu.VMEM(sl_in.shape[1:], F32), pltpu.VMEM(sl_out.shape[1:], F32),
                        pltpu.VMEM((8, D), F32),
                        pltpu.SemaphoreType.DMA((12,)), pltpu.SemaphoreType.DMA((12,)),
                        pltpu.SemaphoreType.DMA((3,))],
        compiler_params=pltpu.CompilerParams(vmem_limit_bytes=48 << 20))(sl_in, sl_out, small, wada_b)


def _adamw(w, g, m, v):
    m = ADAM_B1 * m + (1.0 - ADAM_B1) * g
    v = ADAM_B2 * v + (1.0 - ADAM_B2) * (g * g)
    m_hat = m / (1.0 - ADAM_B1 ** ADAM_STEP)
    v_hat = v / (1.0 - ADAM_B2 ** ADAM_STEP)
    return -ADAM_LR * (m_hat / (jnp.sqrt(v_hat) + ADAM_EPS) + ADAM_WD * w), m, v


def _adam_call(w, m, v, g, name):
    R, C = w.shape
    tr = min(R, 512)

    def body(w_ref, m_ref, v_ref, g_ref, go_ref, d_ref, mo_ref, vo_ref):
        g = g_ref[...]
        go_ref[...] = g
        d_ref[...], mo_ref[...], vo_ref[...] = _adamw(w_ref[...], g, m_ref[...], v_ref[...])

    spec = pl.BlockSpec((tr, C), lambda i: (i, 0))
    return pl.pallas_call(
        body, name=name, grid=(R // tr,), in_specs=[spec] * 4,
        out_specs=(spec,) * 4, out_shape=(jax.ShapeDtypeStruct((R, C), F32),) * 4,
        compiler_params=_params(("arbitrary",)))(w, m, v, g)


R_GF, R_NG, R_LOSS, R_RNG, R_LGF, R_LGB, R_SHIFT, R_SCALE, R_GATE, R_SHIFT_C, R_SCALE_C, R_RNG2, R_RPB = (
    0, 1, 2, 3, 4, 5, 6, 8, 10, 12, 13, 14, 16)
W_GF, W_NG, W_CCTX, W_RNG, W_DF, W_DB, W_BADA, W_RPB = 0, 1, 2, 3, 4, 5, 6, 9


SMALL = (("final_norm_g", W_GF, 1, D), ("norm_g", W_NG, 1, D), ("c_ctx", W_CCTX, 1, D),
         ("ret_norm_g", W_RNG, 1, 512), ("ret_decay_fwd", W_DF, 1, 4), ("ret_decay_bwd", W_DB, 1, 4),
         ("b_ada", W_BADA, 3, D), ("na_rpb", W_RPB, 4, D))
N_SMALL = len(SMALL)


def _small_final_call(sm_all, c_t, dact_parts, wada, m_ada, v_ada, small_w, small_m, small_v, B):
    ws = wada.shape[1]
    NB = N_DEV * B

    def body(*refs):
        sm_ref, ct_ref, wf_ref, wa_ref, ma_ref, va_ref = refs[:6]
        ins = refs[6:6 + 3 * N_SMALL]
        outs = refs[6 + 3 * N_SMALL:6 + 7 * N_SMALL]
        ga_ref, da_ref, mao_ref, vao_ref, loss_ref, dmod_ref, pk_ref = refs[6 + 7 * N_SMALL:]
        x, y, _ = _mesh_pos()
        s = 2 * x + y
        tot = sm_ref[0]
        for dv in range(1, N_DEV):
            tot = tot + sm_ref[dv]
        pk_ref[...] = jnp.zeros_like(pk_ref)
        for kind in range(3):
            for i, (_, row, nrow, width) in enumerate(SMALL):
                ref = ins[kind * N_SMALL + i]
                if nrow == 3:
                    for part in range(3):
                        pk_ref[kind, row + part:row + part + 1, :] = ref[:, part * D:(part + 1) * D]
                else:
                    pk_ref[kind, row:row + nrow, 0:width] = ref[...]
        w = pk_ref[0]
        cctx_ref = ins[2]
        for dv in range(N_DEV):
            for b in range(B):
                r = dv * B + b
                for part, row in enumerate((R_SHIFT, R_SCALE, R_GATE)):
                    dmod_ref[r:r + 1, part * D:(part + 1) * D] = sm_ref[dv, row + b:row + b + 1, :]
        dmod_ref[NB:NB + 1, 0:D] = tot[R_SHIFT_C:R_SHIFT_C + 1, :]
        dmod_ref[NB:NB + 1, D:2 * D] = tot[R_SCALE_C:R_SCALE_C + 1, :]
        dmod_ref[NB:NB + 1, 2 * D:3 * D] = jnp.zeros((1, D), F32)
        dmod_ref[NB + 1:, :] = jnp.zeros((dmod_ref.shape[0] - NB - 1, 3 * D), F32)
        dmod = dmod_ref[...]
        cc = cctx_ref[...]
        scc = _sigmoid(cc)
        ct = ct_ref[...]
        act_t = ct * _sigmoid(ct)
        dact = wf_ref[0, 0:1, :]
        for sh in range(1, N_SHARD):
            dact = dact + wf_ref[sh, 0:1, :]
        g = jnp.zeros((16, D), F32)
        rows = lax.broadcasted_iota(jnp.int32, (16, D), 0)

        def put(g, row, val):
            return jnp.where(rows == row, val, g)

        g = put(g, W_GF, tot[R_GF:R_GF + 1, :])
        g = put(g, W_NG, tot[R_NG:R_NG + 1, :])
        g = put(g, W_CCTX, dact * (scc * (1.0 + cc * (1.0 - scc))))
        g = put(g, W_RNG, tot[R_RNG:R_RNG + 1, :] + tot[R_RNG2:R_RNG2 + 1, :])
        g = put(g, W_DF, tot[R_LGF:R_LGF + 1, :] * (-jnp.exp(w[W_DF:W_DF + 1, :])))
        g = put(g, W_DB, tot[R_LGB:R_LGB + 1, :] * (-jnp.exp(w[W_DB:W_DB + 1, :])))
        db = jnp.sum(dmod, axis=0, keepdims=True)
        for part in range(3):
            g = put(g, W_BADA + part, db[:, part * D:(part + 1) * D])
        for part in range(4):
            g = put(g, W_RPB + part, tot[R_RPB + part:R_RPB + part + 1, :])
        for kind, val in enumerate((g,) + _adamw(w, g, pk_ref[1], pk_ref[2])):
            for i, (_, row, nrow, width) in enumerate(SMALL):
                out = outs[kind * N_SMALL + i]
                if nrow == 3:
                    for part in range(3):
                        out[:, part * D:(part + 1) * D] = val[row + part:row + part + 1, :]
                else:
                    out[...] = val[row:row + nrow, 0:width]
        loss_ref[...] = jnp.broadcast_to(
            (0.5 / D) * jnp.sum(tot[R_LOSS:R_LOSS + 1, :], axis=1, keepdims=True), (8, 128))
        for sh in range(N_SHARD):
            @pl.when(s == sh)
            def _():
                ga = jnp.dot(act_t, dmod[:, sh * ws:(sh + 1) * ws], precision=HIGHEST,
                             preferred_element_type=F32)
                ga_ref[...] = ga
                da_ref[...], mao_ref[...], vao_ref[...] = _adamw(wa_ref[...], ga, ma_ref[...], va_ref[...])

    sh_small = tuple(jax.ShapeDtypeStruct(a.shape, F32) for a in small_w)
    sh_ada = jax.ShapeDtypeStruct(wada.shape, F32)
    res = pl.pallas_call(
        body, name="small_final",
        out_shape=sh_small * 4 + (sh_ada,) * 4 + (jax.ShapeDtypeStruct((8, 128), F32),),
        scratch_shapes=[pltpu.VMEM((NB + 8, 3 * D), F32), pltpu.VMEM((3, 16, D), F32)],
        compiler_params=_params(vmem_mb=56))(
            sm_all, c_t, dact_parts, wada, m_ada, v_ada, *small_w, *small_m, *small_v)
    smalls = [res[k * N_SMALL:(k + 1) * N_SMALL] for k in range(4)]
    return smalls, res[4 * N_SMALL:4 * N_SMALL + 4], res[4 * N_SMALL + 4]


def _local_step(order, x, ctx, c_rows, norm_g, wada_b, b_shard, win_b, rpb_flat, dec_f, dec_b, ret_norm_g,
                wout_b, final_g, target):
    B, L, _ = x.shape
    LC = ctx.shape[1]
    assert B == 2
    cos2, sin2 = _rope_tables(L, LC)
    mod_part = _mod_part_call(c_rows, wada_b, b_shard)
    P, h, win_f, wout_f, mod, bias = _inproj_gather_call(
        order, x, ctx, mod_part, norm_g, win_b, wout_b, cos2, sin2, rpb_flat)
    y_na, o_na = _na_fwd_call(P, bias, L, LC)
    sf, sb = _ret_states_call(P, dec_f, dec_b, L, LC)
    y_ret, o_ret = _retc_fwd_call(P, sf, sb, dec_f, dec_b, ret_norm_g, L)
    dY, dx2, dwout_p, sm_out = _out_call(y_na, y_ret, x, target, mod, final_g, wout_f.reshape(D, D))
    dnq, dng, dnk, dnv, dbias = _na_bwd_call(P, bias, dY, o_na, L, LC)
    drq, drg, drk, drv, dgn, dlg = _retc_bwd_call(P, sf, sb, dec_f, dec_b, ret_norm_g, o_ret, dY, cos2, sin2, L, LC)
    dsec = (dnq, dnk, dnv, dng, drq, drk, drv, drg)
    dwin_b = _dw_call(dsec, h, L)
    cp_in, cp_out, drpb, dlg_sum = _grad_halves_call(
        dwin_b, dwout_p.reshape(N_SHARD, D // N_SHARD, D), dbias, dlg)
    grad_x, sm_dh, sl_in, sl_out = _dh_call(dsec, win_f, x, ctx, dx2, mod, norm_g, cp_in, cp_out)
    z = jnp.zeros((1, D), F32)
    pad = lambda v: jnp.pad(v.reshape(1, -1), ((0, 0), (0, D - v.size)))
    dlg_sum = dlg_sum.reshape(4, 8, 128)
    rpb_rows = jnp.pad(drpb[:, :15, :31].reshape(-1), (0, 4 * D - drpb.shape[0] * 465)).reshape(4, D)
    small = jnp.concatenate([
        sm_out[0:1], sm_dh[0:1], sm_out[1:2], pad(dgn[0]), pad(dlg_sum[:, 0, 0]), pad(dlg_sum[:, 1, 0]),
        sm_dh[3:5], sm_dh[5:7], sm_out[2:4], sm_dh[1:2], sm_dh[2:3], pad(dgn[1]), z, rpb_rows,
        jnp.zeros((SM_ROWS - 20, D), F32)], axis=0)
    return grad_x, sl_in, sl_out, small


def kernel(x, c, ctx, c_ctx, norm_g, w_ada, b_ada, w_in, na_rpb, ret_decay_fwd, ret_decay_bwd, ret_norm_g, w_out, final_norm_g, loss_target, m_c_ctx, m_norm_g, m_w_ada, m_b_ada, m_w_in, m_na_rpb, m_ret_decay_fwd, m_ret_decay_bwd, m_ret_norm_g, m_w_out, m_final_norm_g, v_c_ctx, v_norm_g, v_w_ada, v_b_ada, v_w_in, v_na_rpb, v_ret_decay_fwd, v_ret_decay_bwd, v_ret_norm_g, v_w_out, v_final_norm_g):
    B = x.shape[0]
    c_all = _c_gather_call(c)
    c_rows = jnp.concatenate([c_all.reshape(N_DEV * B, D), c_ctx.reshape(1, D), jnp.zeros((7, D), F32)], axis=0)
    mx, my = lax.axis_index("x"), lax.axis_index("y")
    order = jnp.stack([2 * mx + my, 2 * (1 - mx) + my, 2 * mx + (1 - my),
                       2 * (1 - mx) + (1 - my)]).astype(jnp.int32)
    ws = w_ada.shape[2]
    b_shard = lax.dynamic_slice(b_ada, (0, (2 * mx + my) * ws), (1, ws))
    wada_b = w_ada[0].astype(BF16)
    grad_x, sl_in, sl_out, small = _local_step(
        order, x, ctx, c_rows, norm_g, wada_b, b_shard, w_in[0].astype(BF16),
        na_rpb[0].reshape(na_rpb.shape[1], -1), ret_decay_fwd,
        ret_decay_bwd, ret_norm_g, w_out[0].astype(BF16), final_norm_g.reshape(1, D), loss_target)
    gin, gout, sm_all, dact_parts = _grad_finish_call(sl_in, sl_out, small, wada_b)
    g_win, d_win, nm_win, nv_win = _adam_call(
        w_in[0], m_w_in[0], v_w_in[0], gin.reshape(w_in.shape[1:]), "adam_w_in")
    g_wout, d_wout, nm_wout, nv_wout = _adam_call(
        w_out[0], m_w_out[0], v_w_out[0], gout.reshape(w_out.shape[1:]), "adam_w_out")

    def small_inputs(gf, ng, cc, rng, df, db, bada, rpb):
        return (gf.reshape(1, D), ng, cc.reshape(1, D), rng, df, db, bada,
                jnp.pad(rpb.reshape(-1), (0, 4 * D - rpb.size)).reshape(4, D))

    c_t = c_rows.T
    smalls, adas, loss = _small_final_call(
        sm_all, c_t, dact_parts, w_ada[0], m_w_ada[0], v_w_ada[0],
        small_inputs(final_norm_g, norm_g, c_ctx, ret_norm_g, ret_decay_fwd, ret_decay_bwd, b_ada, na_rpb),
        small_inputs(m_final_norm_g, m_norm_g, m_c_ctx, m_ret_norm_g, m_ret_decay_fwd, m_ret_decay_bwd, m_b_ada,
                     m_na_rpb),
        small_inputs(v_final_norm_g, v_norm_g, v_c_ctx, v_ret_norm_g, v_ret_decay_fwd, v_ret_decay_bwd, v_b_ada,
                     v_na_rpb), B)
    res = []
    for p, ada, win_o, wout_o in zip(smalls, adas, (g_win, d_win, nm_win, nv_win),
                                     (g_wout, d_wout, nm_wout, nv_wout)):
        gf, ng, cc, rng, df, db, bada, rpb = p
        res.append([cc.reshape(D), ng, ada[None], bada, win_o[None],
                    rpb.reshape(-1)[:na_rpb.size].reshape(na_rpb.shape), df, db, rng, wout_o[None], gf.reshape(D)])
    return (loss[0, 0], grad_x, *res[0], *res[1], *res[2], *res[3])
```

```python
import numpy as np
import jax
import jax.numpy as jnp
from jax import lax
from jax.experimental import pallas as pl
from jax.experimental.pallas import tpu as pltpu

F32 = jnp.float32
BF16 = jnp.bfloat16
HIGHEST = lax.Precision.HIGHEST

D = 1024
GRID_W = 64
NA_DH = 64
RET_DK = 128
ROPE_BASE = 10000.0
EPS = 1e-6
NEG = -1e30
TQ = 256
KW = 12 * GRID_W
N_SHARD = 4
N_DEV = 8
SM_ROWS = 24

ADAM_LR = 0.001
ADAM_B1 = 0.9
ADAM_B2 = 0.999
ADAM_EPS = 1e-08
ADAM_WD = 0.01
ADAM_STEP = 10

MESH = pl.DeviceIdType.MESH
ANY = pl.BlockSpec(memory_space=pl.ANY)


def _params(sem=None, vmem_mb=48):
    return pltpu.CompilerParams(dimension_semantics=sem, vmem_limit_bytes=vmem_mb << 20)


def _dot(a, b):
    return jnp.dot(a, b, preferred_element_type=F32)


def _dot_nt(a, b):
    return lax.dot_general(a, b, (((1,), (1,)), ((), ())), preferred_element_type=F32)


def _dot_tn(a, b):
    return lax.dot_general(a, b, (((0,), (0,)), ((), ())), preferred_element_type=F32)


def _sigmoid(x):
    return 1.0 / (1.0 + jnp.exp(-x))


def _rope_tables(L, LC):
    half = RET_DK // 2
    nf = half // 2
    t = np.arange(L)
    row = (t // GRID_W).astype(np.float32)
    col = (t % GRID_W).astype(np.float32)
    inv = (np.float32(ROPE_BASE) ** (-np.arange(nf, dtype=np.float32) / np.float32(nf))).astype(np.float32)
    ang = np.concatenate([row[:, None] * inv, col[:, None] * inv], axis=-1).astype(np.float32)
    cos, sin = np.cos(ang).astype(np.float32), np.sin(ang).astype(np.float32)
    cos2 = np.concatenate([cos, cos], axis=-1)
    sin2 = np.concatenate([-sin, sin], axis=-1)
    cos2 = np.concatenate([cos2, np.ones((LC, RET_DK), np.float32)], axis=0)
    sin2 = np.concatenate([sin2, np.zeros((LC, RET_DK), np.float32)], axis=0)
    return jnp.asarray(cos2), jnp.asarray(sin2)


def _mod_part_call(c_rows, wada_b, b_shard):
    def body(c_ref, w_ref, b_ref, o_ref):
        a = c_ref[...]
        o_ref[...] = _dot((a * _sigmoid(a)).astype(BF16), w_ref[...]) + b_ref[...]

    return pl.pallas_call(
        body, name="ada_mod", out_shape=jax.ShapeDtypeStruct((c_rows.shape[0], wada_b.shape[1]), F32),
        compiler_params=_params())(c_rows, wada_b, b_shard)


def _dc_masks():
    cq = lax.broadcasted_iota(jnp.int32, (GRID_W, GRID_W), 0)
    ck = lax.broadcasted_iota(jnp.int32, (GRID_W, GRID_W), 1)
    dc = jnp.clip(ck - cq + 15, 0, 30)
    c0 = jnp.clip(cq - 8, 0, GRID_W - 16)
    col_ok = (ck >= c0) & (ck < c0 + 16)
    return dc, col_ok


def _bias_blocks():
    out = []
    for typ, delta in enumerate((4, 0, -4)):
        for rq in range(4):
            for rkk in range(12):
                dr = rkk + delta - rq - 4
                if typ == 0:
                    ok = -rq <= dr <= 7 - rq
                elif typ == 1:
                    ok = -4 <= dr <= 3
                else:
                    ok = -4 - rq <= dr <= 3 - rq
                out.append((typ, rq, rkk, dr if ok else None))
    return out


def _bias_body(r_ref, bias_ref, et_ref, out_ref, sem):
    dc, col_ok = _dc_masks()
    masks = [(dc == j).astype(F32) for j in range(31)]
    nh = bias_ref.shape[0]

    def per_h(h, carry):
        for dr in range(15):
            t = jnp.zeros((GRID_W, GRID_W), F32)
            for j in range(31):
                t = t + masks[j] * r_ref[h, dr * 31 + j]
            et_ref[dr] = jnp.where(col_ok, t, NEG)
        neg = jnp.full((GRID_W, GRID_W), NEG, F32)
        for typ, rq, rkk, dr in _bias_blocks():
            blk = neg if dr is None else et_ref[dr + 7]
            bias_ref[h, typ, rq * 64:(rq + 1) * 64, rkk * 64:(rkk + 1) * 64] = blk
        pltpu.make_async_copy(bias_ref.at[h], out_ref.at[h], sem).start()
        return carry

    lax.fori_loop(0, nh, per_h, 0)
    return [pltpu.make_async_copy(bias_ref.at[h], out_ref.at[h], sem) for h in range(nh)]


def _bias_tile_sums(db_ref, hh):
    acc = {}
    for typ, rq, rkk, dr in _bias_blocks():
        if dr is None:
            continue
        blk = db_ref[hh, typ, rq * 64:(rq + 1) * 64, rkk * 64:(rkk + 1) * 64]
        acc[dr] = blk if dr not in acc else acc[dr] + blk
    return acc


def _small_reduce_body(dt_ref, dlg_ref, drpb_ref, dlgo_ref, p_ref):
    dc, _ = _dc_masks()
    masks = [(dc == j).astype(F32) for j in range(31)]
    ones = jnp.ones((8, GRID_W), F32)
    p_ref[...] = jnp.zeros_like(p_ref)
    drpb_ref[...] = jnp.zeros_like(drpb_ref)

    def per_h(h, carry):
        for dr in range(-7, 8):
            t = dt_ref[h, dr + 7]
            for j in range(31):
                p_ref[j:j + 1, :] = jnp.sum(t * masks[j], axis=0, keepdims=True)
            red = lax.dot_general(ones, p_ref[...], (((1,), (1,)), ((), ())),
                                  precision=HIGHEST, preferred_element_type=F32)
            drpb_ref[h, dr + 7:dr + 8, :] = red[0:1, :]
        return carry

    lax.fori_loop(0, dt_ref.shape[0], per_h, 0)
    x = dlg_ref[0]
    for b in range(1, dlg_ref.shape[0]):
        x = x + dlg_ref[b]
    x = x.reshape(4 * 8, x.shape[-1])
    dlgo_ref[...] = jnp.dot(x, jnp.ones((x.shape[-1], 128), F32), precision=HIGHEST,
                            preferred_element_type=F32)


def _inproj_gather_call(order, x, ctx, mod_part, norm_g, win_b, wout_b, cos2, sin2):
    B, L, _ = x.shape
    LC = ctx.shape[1]
    T = L + LC
    TI = 2 * TQ
    nl = L // TI
    nt = nl + 1
    assert LC == TQ and L % TI == 0
    kscale = RET_DK ** -0.5
    HR = D // 2
    pad_rows = nt * TI - T
    cos2 = jnp.pad(cos2, ((0, pad_rows), (0, 0)))
    sin2 = jnp.pad(sin2, ((0, pad_rows), (0, 0)))

    MW = mod_part.shape[1]
    NB = N_DEV * B

    def body(ord_ref, x_ref, ctx_ref, mp_ref, g_ref, wown_ref, woown_ref, cos_ref, sin_ref,
             p_ref, h_ref, wf_ref, wof_ref, modo_ref, w_all, wo_all, hs_ref, mp_all, mod_ref, ssem, rsem, lsem):
        j, b, t = pl.program_id(0), pl.program_id(1), pl.program_id(2)
        first = (b == 0) & (t == 0)
        mx, my, mc = _mesh_pos()
        s = 2 * mx + my

        m_send = [_remote(mp_ref, mp_all.at[s], ssem, rsem, 12 + k, (px, py, mc))
                  for k, (px, py) in enumerate(_other_chips(mx, my))]
        m_recv = [_remote(mp_ref, mp_all.at[2 * px + py], ssem, rsem, 12 + k, (px, py, mc))
                  for k, (px, py) in enumerate(_other_chips(mx, my))]

        sems = (ssem, rsem, lsem)
        own, ici_send, ici_recv, fwd_send, fwd_recv, outs = _gather_copies(wown_ref, w_all, wf_ref, HR, sems, 0, 0)
        oown, o_send, o_recv, o_fsend, o_frecv, o_outs = _gather_copies(
            woown_ref, wo_all, wof_ref, woown_ref.shape[0] // 2, sems, 6, 5)

        @pl.when(first & (j == 0))
        def _():
            for cp in m_send:
                cp.start()
            own.start()
            oown.start()
            mp_all[s] = mp_ref[...]
            own.wait()
            ici_send[0].start()
            ici_send[1].start()
            outs[0].start()
            oown.wait()
            for cp in m_recv:
                cp.wait_recv()
            me = 4 * mx + 2 * my + mc
            mod_ref[...] = jnp.zeros_like(mod_ref)
            for p in range(N_SHARD):
                for r in range(B):
                    mod_ref[r:r + 1, p * MW:(p + 1) * MW] = mp_all[p, pl.ds(B * me + r, 1), :]
                mod_ref[B:B + 1, p * MW:(p + 1) * MW] = mp_all[p, NB:NB + 1, :]
            modo_ref[...] = mod_ref[...]

        for k in range(3):
            @pl.when(first & (j == k + 1))
            def _(k=k):
                ici_recv[k].wait_recv()
                if k == 0:
                    ici_send[2].start()
                fwd_send[k].start()
                fwd_recv[k].wait_recv()
                outs[1 + k].start()
                if k == 1:
                    for cp in o_send:
                        cp.start()
                if k == 2:
                    for got, fwd in zip(o_recv, o_fsend):
                        got.wait_recv()
                        fwd.start()

        tile = b * nt + t

        @pl.when(j == 0)
        def _():
            is_lat = t < nl
            ctx_tile = jnp.concatenate([ctx_ref[...], jnp.zeros((TI - LC, D), F32)], axis=0)
            xt = jnp.where(is_lat, x_ref[...], ctx_tile)
            mrow = mod_ref[pl.ds(jnp.where(is_lat, b, B), 1), :]
            shift, scale = mrow[:, 0:D], mrow[:, D:2 * D]
            rstd = lax.rsqrt(jnp.mean(xt * xt, axis=-1, keepdims=True) + EPS)
            h0 = ((xt * rstd * g_ref[...]) * (1.0 + scale) + shift).astype(BF16)
            h_ref[...] = h0
            hs_ref[tile] = h0

        shard = ord_ref[j]

        def project(sh, nrows):
            hb = hs_ref[tile, 0:nrows, :]
            cs, sn = cos_ref[0:nrows, :], sin_ref[0:nrows, :]
            for half in range(2):
                sec = 2 * sh + half
                acc = _dot(hb, w_all[sh, :, half * 512:(half + 1) * 512])
                if sec == 0:
                    acc = acc * (NA_DH ** -0.5)
                if sec in (4, 5):
                    for q in range(4):
                        a = acc[:, q * 128:(q + 1) * 128]
                        r = a * cs + pltpu.roll(a, 64, 1) * sn
                        if sec == 5:
                            r = r * kscale
                        p_ref[0:nrows, half * 512 + q * 128:half * 512 + (q + 1) * 128] = r.astype(BF16)
                else:
                    p_ref[0:nrows, half * 512:(half + 1) * 512] = acc.astype(BF16)

        for sh in range(N_SHARD):
            @pl.when((shard == sh) & (t < nl))
            def _(sh=sh):
                project(sh, TI)

            @pl.when((shard == sh) & (t == nl))
            def _(sh=sh):
                project(sh, LC)

        @pl.when((j == N_SHARD - 1) & (b == B - 1) & (t == nt - 1))
        def _():
            for cp in o_frecv:
                cp.wait_recv()
            for cp in o_outs:
                cp.start()
            _finish(outs + o_outs, ici_send + fwd_send + o_send + o_fsend + m_send, [])

    tok = lambda j, b, t, o: (jnp.where(j == 0, b, B - 1), jnp.where(j == 0, jnp.minimum(t, nl - 1), nl - 1), 0)
    grid_spec = pltpu.PrefetchScalarGridSpec(
        num_scalar_prefetch=1, grid=(N_SHARD, B, nt),
        in_specs=[
            pl.BlockSpec((None, TI, D), tok),
            pl.BlockSpec((None, LC, D), lambda j, b, t, o: (jnp.where(j == 0, b, B - 1), 0, 0)),
            pl.BlockSpec(mod_part.shape, lambda j, b, t, o: (0, 0)),
            pl.BlockSpec((1, D), lambda j, b, t, o: (0, 0)),
            ANY, ANY,
            pl.BlockSpec((TI, RET_DK), lambda j, b, t, o: (t, 0)),
            pl.BlockSpec((TI, RET_DK), lambda j, b, t, o: (t, 0)),
        ],
        out_specs=(pl.BlockSpec((None, TI, D), lambda j, b, t, o: (b, t, o[j])),
                   pl.BlockSpec((None, TI, D), lambda j, b, t, o: (
                       jnp.where(j == 0, b, B - 1), jnp.where(j == 0, t, nt - 1), 0)), ANY, ANY,
                   pl.BlockSpec((8, 3 * D), lambda j, b, t, o: (0, 0))),
        scratch_shapes=[pltpu.VMEM((N_SHARD, D, D), BF16), pltpu.VMEM((N_SHARD,) + wout_b.shape, BF16),
                        pltpu.VMEM((B * nt, TI, D), BF16),
                        pltpu.VMEM((N_SHARD,) + mod_part.shape, F32), pltpu.VMEM((8, 3 * D), F32),
                        pltpu.SemaphoreType.DMA((15,)), pltpu.SemaphoreType.DMA((15,)),
                        pltpu.SemaphoreType.DMA((10,))])
    return pl.pallas_call(
        body, name="in_proj", grid_spec=grid_spec,
        out_shape=(jax.ShapeDtypeStruct((B, T, 4 * D), BF16), jax.ShapeDtypeStruct((B, T, D), BF16),
                   jax.ShapeDtypeStruct((N_SHARD, D, D), BF16),
                   jax.ShapeDtypeStruct((N_SHARD,) + wout_b.shape, BF16),
                   jax.ShapeDtypeStruct((8, 3 * D), F32)),
        compiler_params=_params(("arbitrary",) * 3, vmem_mb=56))(
            order, x, ctx, mod_part, norm_g, win_b, wout_b, cos2, sin2)


def _na_specs(L, T, rows, nh=2):
    nm = rows // 4
    w = nh * NA_DH
    per = 512 // w
    q_spec = pl.BlockSpec((None, TQ, w), lambda hp, b, m: (b, m, hp))
    k_spec = pl.BlockSpec((None, T, w), lambda hp, b, m: (b, 0, per + hp))
    v_spec = pl.BlockSpec((None, T, w), lambda hp, b, m: (b, 0, 2 * per + hp))
    g_spec = pl.BlockSpec((None, TQ, w), lambda hp, b, m: (b, m, 3 * per + hp))
    bias_spec = pl.BlockSpec((nh, 3, TQ, KW), lambda hp, b, m: (hp, 0, 0, 0))
    return nm, q_spec, k_spec, v_spec, g_spec, bias_spec


def _na_tile(m, nm, rows):
    typ = jnp.where(m == 0, 0, jnp.where(m == nm - 1, 2, 1))
    start = pl.multiple_of(jnp.clip(4 * m - 4, 0, rows - 12) * GRID_W, TQ)
    return typ, start


def _na_fwd_call(P, bias, L, LC):
    B, T, _ = P.shape
    rows = L // GRID_W
    NH = 4
    nm, q_spec, k_spec, v_spec, g_spec, bias_spec = _na_specs(L, T, rows, NH)

    def body(q_ref, k_ref, v_ref, g_ref, bias_ref, y_ref, o_ref):
        typ, start = _na_tile(pl.program_id(2), nm, rows)
        for hh in range(NH):
            ln = slice(hh * NA_DH, (hh + 1) * NA_DH)
            q = q_ref[:, ln]
            kw, vw = k_ref[pl.ds(start, KW), ln], v_ref[pl.ds(start, KW), ln]
            kc, vc = k_ref[L:L + LC, ln], v_ref[L:L + LC, ln]
            s1 = _dot_nt(q, kw) + bias_ref[hh, typ]
            s2 = _dot_nt(q, kc)
            mx = jnp.maximum(jnp.max(s1, axis=-1, keepdims=True), jnp.max(s2, axis=-1, keepdims=True))
            p1, p2 = jnp.exp(s1 - mx), jnp.exp(s2 - mx)
            inv = 1.0 / (jnp.sum(p1, axis=-1, keepdims=True) + jnp.sum(p2, axis=-1, keepdims=True))
            o = (_dot(p1.astype(BF16), vw) + _dot(p2.astype(BF16), vc)) * inv
            g = g_ref[:, ln].astype(F32)
            o_ref[:, ln] = o.astype(BF16)
            y_ref[:, ln] = (o * (g * _sigmoid(g))).astype(BF16)

    tile = pl.BlockSpec((None, TQ, NH * NA_DH), lambda hp, b, m: (b, m, hp))
    return pl.pallas_call(
        body, name="na_fwd", grid=(8 // NH, B, nm),
        in_specs=[q_spec, k_spec, v_spec, g_spec, bias_spec],
        out_specs=(tile, tile),
        out_shape=(jax.ShapeDtypeStruct((B, L, 512), BF16),) * 2,
        compiler_params=_params(("arbitrary",) * 3))(P, P, P, P, bias)


def _na_bwd_call(P, bias, dY, o_na, L, LC):
    B, T, _ = P.shape
    rows = L // GRID_W
    NH = 4
    W = NH * NA_DH
    nm, q_spec, k_spec, v_spec, g_spec, bias_spec = _na_specs(L, T, rows, NH)
    scale = NA_DH ** -0.5

    RB = 32

    def body(q_ref, k_ref, v_ref, g_ref, bias_ref, dy_ref, o_ref, dq_ref, dg_ref, dk_ref, dv_ref, dt_ref,
             db_ref, s1_ref, s2_ref, dp1_ref, dp2_ref, p1_ref, p2_ref, ds1_ref, ds2_ref, dkt_ref, dvt_ref):
        b, m = pl.program_id(1), pl.program_id(2)
        typ, start = _na_tile(m, nm, rows)

        @pl.when(m == 0)
        def _():
            dkt_ref[...] = jnp.zeros_like(dkt_ref)
            dvt_ref[...] = jnp.zeros_like(dvt_ref)

        @pl.when((m == 0) & (b == 0))
        def _():
            db_ref[...] = jnp.zeros_like(db_ref)

        for hh in range(NH):
            ln = slice(hh * NA_DH, (hh + 1) * NA_DH)
            q = q_ref[:, ln]
            kw, vw = k_ref[pl.ds(start, KW), ln], v_ref[pl.ds(start, KW), ln]
            kc, vc = k_ref[L:L + LC, ln], v_ref[L:L + LC, ln]
            g = g_ref[:, ln].astype(F32)
            sg = _sigmoid(g)
            dy = dy_ref[:, ln].astype(F32)
            do = (dy * (g * sg)).astype(BF16)
            s1_ref[hh] = _dot_nt(q, kw)
            s2_ref[hh] = _dot_nt(q, kc)
            dp1_ref[hh] = _dot_nt(do, vw)
            dp2_ref[hh] = _dot_nt(do, vc)

            def rows_pass(r, carry, hh=hh):
                rw = pl.ds(pl.multiple_of(r * RB, RB), RB)
                a = s1_ref[hh, rw, :] + bias_ref[hh, typ, rw, :]
                c = s2_ref[hh, rw, :]
                mx = jnp.maximum(jnp.max(a, axis=-1, keepdims=True), jnp.max(c, axis=-1, keepdims=True))
                e1, e2 = jnp.exp(a - mx), jnp.exp(c - mx)
                inv = 1.0 / (jnp.sum(e1, axis=-1, keepdims=True) + jnp.sum(e2, axis=-1, keepdims=True))
                p1, p2 = e1 * inv, e2 * inv
                p1_ref[hh, rw, :] = p1.astype(BF16)
                p2_ref[hh, rw, :] = p2.astype(BF16)
                dp1, dp2 = dp1_ref[hh, rw, :], dp2_ref[hh, rw, :]
                delta = jnp.sum(p1 * dp1, axis=-1, keepdims=True) + jnp.sum(p2 * dp2, axis=-1, keepdims=True)
                ds1 = p1 * (dp1 - delta)
                db_ref[hh, typ, rw, :] += ds1
                ds1_ref[hh, rw, :] = ds1.astype(BF16)
                ds2_ref[hh, rw, :] = (p2 * (dp2 - delta)).astype(BF16)
                return carry

            lax.fori_loop(0, TQ // RB, rows_pass, 0, unroll=True)
            p1b, p2b, ds1b, ds2b = p1_ref[hh], p2_ref[hh], ds1_ref[hh], ds2_ref[hh]
            dg_ref[:, ln] = (dy * o_ref[:, ln].astype(F32) * (sg * (1.0 + g * (1.0 - sg)))).astype(BF16)
            dq_ref[:, ln] = ((_dot(ds1b, kw) + _dot(ds2b, kc)) * scale).astype(BF16)
            dkt_ref[ln, pl.ds(start, KW)] += _dot_tn(q, ds1b)
            dvt_ref[ln, pl.ds(start, KW)] += _dot_tn(do, p1b)
            dkt_ref[ln, L:L + LC] += _dot_tn(q, ds2b)
            dvt_ref[ln, L:L + LC] += _dot_tn(do, p2b)

        @pl.when(m == nm - 1)
        def _():
            dk_ref[...] = dkt_ref[...].T.astype(BF16)
            dv_ref[...] = dvt_ref[...].T.astype(BF16)

        @pl.when((m == nm - 1) & (b == B - 1))
        def _():
            for hh in range(NH):
                for dr, t in _bias_tile_sums(db_ref, hh).items():
                    dt_ref[hh, dr + 7] = t

    tile = pl.BlockSpec((None, TQ, W), lambda hp, b, m: (b, m, hp))
    kv_out = pl.BlockSpec((None, T, W), lambda hp, b, m: (b, 0, hp))
    wide, narrow = (NH, TQ, KW), (NH, TQ, LC)
    return pl.pallas_call(
        body, name="na_bwd", grid=(8 // NH, B, nm),
        in_specs=[q_spec, k_spec, v_spec, g_spec, bias_spec, tile, tile],
        out_specs=(tile, tile, kv_out, kv_out,
                   pl.BlockSpec((NH, 15, GRID_W, GRID_W), lambda hp, b, m: (hp, 0, 0, 0))),
        out_shape=(jax.ShapeDtypeStruct((B, L, 512), BF16), jax.ShapeDtypeStruct((B, L, 512), BF16),
                   jax.ShapeDtypeStruct((B, T, 512), BF16), jax.ShapeDtypeStruct((B, T, 512), BF16),
                   jax.ShapeDtypeStruct((bias.shape[0], 15, GRID_W, GRID_W), F32)),
        scratch_shapes=[pltpu.VMEM((NH,) + bias.shape[1:], F32),
                        pltpu.VMEM(wide, F32), pltpu.VMEM(narrow, F32), pltpu.VMEM(wide, F32), pltpu.VMEM(narrow, F32),
                        pltpu.VMEM(wide, BF16), pltpu.VMEM(narrow, BF16), pltpu.VMEM(wide, BF16),
                        pltpu.VMEM(narrow, BF16), pltpu.VMEM((W, T), F32), pltpu.VMEM((W, T), F32)],
        compiler_params=_params(("arbitrary",) * 3, vmem_mb=60))(P, P, P, P, bias, dY, o_na)


def _head_scalar(dec_ref, h):
    lane = lax.broadcasted_iota(jnp.int32, dec_ref.shape, 1)
    return -jnp.sum(jnp.where(lane == h, jnp.exp(dec_ref[...]), 0.0), axis=1, keepdims=True)


def _chunk_decay(lgf, lgb):
    tau = lax.broadcasted_iota(jnp.int32, (TQ, 1), 0).astype(F32)
    sig = lax.broadcasted_iota(jnp.int32, (1, TQ), 1).astype(F32)
    dist = tau - sig
    dm = jnp.exp(dist * jnp.where(dist > 0, lgf, -lgb)) * jnp.where(dist == 0, 2.0, 1.0)
    return tau, dist, dm


def _ret_states_call(P, dec_f, dec_b, L, LC):
    B, T, _ = P.shape
    n = L // TQ

    def body(df_ref, db_ref, k_ref, v_ref, sf_ref, sb_ref):
        h = pl.program_id(1)
        lgf, lgb = _head_scalar(df_ref, h), _head_scalar(db_ref, h)
        tau = lax.broadcasted_iota(jnp.int32, (TQ, 1), 0).astype(F32)
        jc = lax.broadcasted_iota(jnp.int32, (LC, 1), 0).astype(F32)
        wf, wb = jnp.exp(lgf * (TQ - 1.0 - tau)), jnp.exp(lgb * tau)
        gcf, gcb = jnp.exp(lgf * float(TQ)), jnp.exp(lgb * float(TQ))
        kc, vc = k_ref[L:L + LC, :].astype(F32), v_ref[L:L + LC, :]

        def chunk_state(i, w):
            ks = pl.multiple_of(i * TQ, TQ)
            return _dot_tn((k_ref[pl.ds(ks, TQ), :].astype(F32) * w).astype(BF16), v_ref[pl.ds(ks, TQ), :])

        def fwd(i, s):
            sf_ref[i] = s
            return gcf * s + chunk_state(i, wf)

        lax.fori_loop(0, n, fwd, _dot_tn((kc * jnp.exp(lgf * (LC - 1.0 - jc))).astype(BF16), vc), unroll=True)

        def bwd(r, s):
            i = n - 1 - r
            sb_ref[i] = s
            return gcb * s + chunk_state(i, wb)

        lax.fori_loop(0, n, bwd, _dot_tn((kc * jnp.exp(lgb * jc)).astype(BF16), vc), unroll=True)

    st = pl.BlockSpec((None, None, n, RET_DK, RET_DK), lambda b, h: (b, h, 0, 0, 0))
    return pl.pallas_call(
        body, name="ret_states", grid=(B, 4),
        in_specs=[pl.BlockSpec((1, 4), lambda b, h: (0, 0)), pl.BlockSpec((1, 4), lambda b, h: (0, 0)),
                  pl.BlockSpec((None, T, 128), lambda b, h: (b, 0, 20 + h)),
                  pl.BlockSpec((None, T, 128), lambda b, h: (b, 0, 24 + h))],
        out_specs=(st, st),
        out_shape=(jax.ShapeDtypeStruct((B, 4, n, RET_DK, RET_DK), F32),) * 2,
        compiler_params=_params(("arbitrary",) * 2))(dec_f, dec_b, P, P)


def _retc_fwd_call(P, sf, sb, dec_f, dec_b, ret_norm_g, L):
    B, T, _ = P.shape
    sec = lambda k: pl.BlockSpec((None, TQ, 512), lambda b, i: (b, i, k))
    dec_spec = pl.BlockSpec((1, 4), lambda b, i: (0, 0))
    st_spec = pl.BlockSpec((None, 4, None, RET_DK, RET_DK), lambda b, i: (b, 0, i, 0, 0))

    def body(df_ref, db_ref, q_ref, k_ref, v_ref, g_ref, gn_ref, sf_ref, sb_ref, y_ref, o_ref):
        for h in range(4):
            ln = slice(h * RET_DK, (h + 1) * RET_DK)
            lgf, lgb = _head_scalar(df_ref, h), _head_scalar(db_ref, h)
            tau, _, dm = _chunk_decay(lgf, lgb)
            q = q_ref[:, ln]
            qf = q.astype(F32)
            acc = _dot((_dot_nt(q, k_ref[:, ln]) * dm).astype(BF16), v_ref[:, ln])
            acc = acc + _dot((qf * jnp.exp(lgf * (tau + 1.0))).astype(BF16), sf_ref[h].astype(BF16))
            acc = acc + _dot((qf * jnp.exp(lgb * (TQ - tau))).astype(BF16), sb_ref[h].astype(BF16))
            o_ref[:, ln] = acc
            rn = lax.rsqrt(jnp.mean(acc * acc, axis=-1, keepdims=True) + EPS)
            g = g_ref[:, ln].astype(F32)
            y_ref[:, ln] = ((acc * rn * gn_ref[:, ln]) * (g * _sigmoid(g))).astype(BF16)

    tile = pl.BlockSpec((None, TQ, 512), lambda b, i: (b, i, 0))
    return pl.pallas_call(
        body, name="ret_fwd", grid=(B, L // TQ),
        in_specs=[dec_spec, dec_spec, sec(4), sec(5), sec(6), sec(7),
                  pl.BlockSpec((1, 512), lambda b, i: (0, 0)), st_spec, st_spec],
        out_specs=(tile, tile),
        out_shape=(jax.ShapeDtypeStruct((B, L, 512), BF16), jax.ShapeDtypeStruct((B, L, 512), F32)),
        compiler_params=_params(("arbitrary",) * 2))(dec_f, dec_b, P, P, P, P, ret_norm_g, sf, sb)


def _retc_bwd_call(P, sf, sb, dec_f, dec_b, ret_norm_g, o_ret, dY, cos2, sin2, L, LC):
    B, T, _ = P.shape
    n = L // TQ
    C = float(TQ)
    kscale = RET_DK ** -0.5
    st_spec = pl.BlockSpec((None, 4, n, RET_DK, RET_DK), lambda b, i: (b, 0, 0, 0, 0))

    def body(df_ref, db_ref, q_ref, k_ref, v_ref, g_ref, gn_ref, o_ref, dy_ref, cos_ref, sin_ref, sf_ref, sb_ref,
             dq_ref, dg_ref, dk_ref, dv_ref, dgn_ref, dlg_ref, dsf_ref, dsb_ref):
        i = pl.program_id(1)

        @pl.when(i == 0)
        def _():
            dk_ref[...] = jnp.zeros_like(dk_ref)
            dv_ref[...] = jnp.zeros_like(dv_ref)
            dgn_ref[...] = jnp.zeros_like(dgn_ref)
            dlg_ref[...] = jnp.zeros_like(dlg_ref)

        rows = pl.ds(pl.multiple_of(i * TQ, TQ), TQ)
        cs, sn = cos_ref[rows, :], sin_ref[rows, :]

        def one_head(h):
            ln = slice(h * RET_DK, (h + 1) * RET_DK)
            lgf, lgb = _head_scalar(df_ref, h), _head_scalar(db_ref, h)
            tau, dist, dm = _chunk_decay(lgf, lgb)

            def add_lg(row, x):
                csum = jnp.sum(x, axis=0, keepdims=True)
                tot = csum[:, 0:128]
                for part in range(1, x.shape[1] // 128):
                    tot = tot + csum[:, part * 128:(part + 1) * 128]
                dlg_ref[h, row:row + 1, :] += tot

            q = q_ref[:, ln]
            qf = q.astype(F32)
            o = o_ref[:, ln]
            g = g_ref[:, ln].astype(F32)
            dy = dy_ref[:, ln].astype(F32)
            gn = gn_ref[:, ln]
            sg = _sigmoid(g)
            rn = lax.rsqrt(jnp.mean(o * o, axis=-1, keepdims=True) + EPS)
            nrm = o * rn
            dg_ref[:, ln] = (dy * (nrm * gn) * (sg * (1.0 + g * (1.0 - sg)))).astype(BF16)
            dhn = dy * (g * sg)
            dgn_ref[:, ln] += jnp.sum(dhn * nrm, axis=0, keepdims=True)
            dnrm = dhn * gn
            do = rn * (dnrm - nrm * jnp.mean(dnrm * nrm, axis=-1, keepdims=True))
            dob = do.astype(BF16)
            ki, vi = k_ref[rows, ln], v_ref[rows, ln]
            s = _dot_nt(q, ki)
            dsv = _dot_nt(dob, vi)
            dsb = (dsv * dm).astype(BF16)
            dk_ref[rows, ln] += _dot_tn(dsb, q)
            dv_ref[rows, ln] += _dot_tn((s * dm).astype(BF16), dob)
            xw = s * dsv * dm * jnp.abs(dist)
            fpart = jnp.where(dist > 0, xw, 0.0)
            add_lg(0, fpart)
            add_lg(1, xw - fpart)
            dq = _dot(dsb, ki)
            af, ab = jnp.exp(lgf * (tau + 1.0)), jnp.exp(lgb * (C - tau))
            qa, qb = (qf * af).astype(BF16), (qf * ab).astype(BF16)
            sfi, sbi = sf_ref[h, i].astype(BF16), sb_ref[h, i].astype(BF16)
            dq = dq + af * _dot_nt(dob, sfi) + ab * _dot_nt(dob, sbi)
            dsf_ref[h, i] = _dot_tn(qa, dob)
            dsb_ref[h, i] = _dot_tn(qb, dob)
            add_lg(0, (tau + 1.0) * (_dot(qa, sfi) * do))
            add_lg(1, (C - tau) * (_dot(qb, sbi) * do))
            dq_ref[:, ln] = (dq * cs - pltpu.roll(dq, 64, 1) * sn).astype(BF16)

            @pl.when(i == n - 1)
            def _():
                jc = lax.broadcasted_iota(jnp.int32, (LC, 1), 0).astype(F32)
                crow = pl.ds(L, LC)

                def through_state(rws, w, dw, gst, row):
                    kk, vv = k_ref[rws, ln].astype(F32), v_ref[rws, ln]
                    gb = gst.astype(BF16)
                    vg = _dot_nt(vv, gb)
                    kw = kk * w
                    dk_ref[rws, ln] += w * vg
                    dv_ref[rws, ln] += _dot(kw.astype(BF16), gb)
                    add_lg(row, dw * (kw * vg))

                def scan(gc, w, dw, st_ref, dst_ref, order, row):
                    def step(r, gst):
                        j = order(r)
                        through_state(pl.ds(pl.multiple_of(j * TQ, TQ), TQ), w, dw, gst, row)
                        add_lg(row, (C * gc) * (gst * st_ref[h, j]))
                        return dst_ref[h, j] + gc * gst
                    return lax.fori_loop(0, n, step, jnp.zeros((RET_DK, RET_DK), F32), unroll=True)

                gcf, gcb = jnp.exp(lgf * C), jnp.exp(lgb * C)
                g0 = scan(gcf, jnp.exp(lgf * (C - 1.0 - tau)), C - 1.0 - tau, sf_ref, dsf_ref,
                          lambda r: n - 1 - r, 0)
                through_state(crow, jnp.exp(lgf * (LC - 1.0 - jc)), LC - 1.0 - jc, g0, 0)
                g1 = scan(gcb, jnp.exp(lgb * tau), tau, sb_ref, dsb_ref, lambda r: r, 1)
                through_state(crow, jnp.exp(lgb * jc), jc, g1, 1)
                dk = dk_ref[:, ln]
                dk_ref[:, ln] = (dk * cos_ref[...] - pltpu.roll(dk, 64, 1) * sin_ref[...]) * kscale

        for h in range(4):
            one_head(h)

    sec = lambda k: pl.BlockSpec((None, TQ, 512), lambda b, i: (b, i, k))
    full = lambda k: pl.BlockSpec((None, T, 512), lambda b, i: (b, 0, k))
    dec_spec = pl.BlockSpec((1, 4), lambda b, i: (0, 0))
    tab = pl.BlockSpec((T, RET_DK), lambda b, i: (0, 0))
    return pl.pallas_call(
        body, name="ret_bwd", grid=(B, n),
        in_specs=[dec_spec, dec_spec, sec(4), full(5), full(6), sec(7),
                  pl.BlockSpec((1, 512), lambda b, i: (0, 0)), sec(0), sec(1), tab, tab, st_spec, st_spec],
        out_specs=(sec(0), sec(0), full(0), full(0),
                   pl.BlockSpec((None, 1, 512), lambda b, i: (b, 0, 0)),
                   pl.BlockSpec((None, 4, 8, 128), lambda b, i: (b, 0, 0, 0))),
        out_shape=(jax.ShapeDtypeStruct((B, L, 512), BF16), jax.ShapeDtypeStruct((B, L, 512), BF16),
                   jax.ShapeDtypeStruct((B, T, 512), F32), jax.ShapeDtypeStruct((B, T, 512), F32),
                   jax.ShapeDtypeStruct((B, 1, 512), F32), jax.ShapeDtypeStruct((B, 4, 8, 128), F32)),
        scratch_shapes=[pltpu.VMEM((4, n, RET_DK, RET_DK), F32), pltpu.VMEM((4, n, RET_DK, RET_DK), F32)],
        compiler_params=_params(("arbitrary",) * 2, vmem_mb=56))(
            dec_f, dec_b, P, P, P, P, ret_norm_g, o_ret, dY, cos2, sin2, sf, sb)


def _out_call(y_na, y_ret, x, target, mod, final_g, wout_f):
    B, L, _ = x.shape
    TO = 4 * TQ

    def body(yn_ref, yr_ref, x_ref, t_ref, mod_ref, gf_ref, w_ref, dy_ref, dx2_ref, dwb_ref, sm_ref, dw_ref):
        b, i = pl.program_id(0), pl.program_id(1)

        @pl.when((b == 0) & (i == 0))
        def _():
            dw_ref[...] = jnp.zeros_like(dw_ref)
            sm_ref[...] = jnp.zeros_like(sm_ref)

        gate = mod_ref[pl.ds(b, 1), 2 * D:3 * D]
        gf = gf_ref[...]
        yn, yr = yn_ref[...], yr_ref[...]
        ylat = _dot(yn, w_ref[0:512, :]) + _dot(yr, w_ref[512:1024, :])
        x2 = x_ref[...] + gate * ylat
        r = lax.rsqrt(jnp.mean(x2 * x2, axis=-1, keepdims=True) + EPS)
        xr = x2 * r
        err = xr * gf - t_ref[...]
        sm_ref[1:2, :] += jnp.sum(err * err, axis=0, keepdims=True)
        dout = err * (1.0 / D)
        sm_ref[0:1, :] += jnp.sum(dout * xr, axis=0, keepdims=True)
        gd = dout * gf
        dx2 = r * (gd - xr * jnp.mean(gd * xr, axis=-1, keepdims=True))
        dx2_ref[...] = dx2
        sm_ref[pl.ds(2 + b, 1), :] += jnp.sum(dx2 * ylat, axis=0, keepdims=True)
        dyl = (gate * dx2).astype(BF16)
        dy_ref[:, 0:512] = _dot_nt(dyl, w_ref[0:512, :]).astype(BF16)
        dy_ref[:, 512:1024] = _dot_nt(dyl, w_ref[512:1024, :]).astype(BF16)
        dw_ref[0:512, :] += _dot_tn(yn, dyl)
        dw_ref[512:1024, :] += _dot_tn(yr, dyl)

        @pl.when((b == B - 1) & (i == L // TO - 1))
        def _():
            dwb_ref[...] = dw_ref[...].astype(BF16)

    half = pl.BlockSpec((None, TO, 512), lambda b, i: (b, i, 0))
    full = pl.BlockSpec((None, TO, D), lambda b, i: (b, i, 0))
    return pl.pallas_call(
        body, name="out_proj_loss", grid=(B, L // TO),
        in_specs=[half, half, full, full,
                  pl.BlockSpec((8, 3 * D), lambda b, i: (0, 0)),
                  pl.BlockSpec((1, D), lambda b, i: (0, 0)),
                  pl.BlockSpec((D, D), lambda b, i: (0, 0))],
        out_specs=(full, full, pl.BlockSpec((D, D), lambda b, i: (0, 0)),
                   pl.BlockSpec((8, D), lambda b, i: (0, 0))),
        out_shape=(jax.ShapeDtypeStruct((B, L, D), BF16), jax.ShapeDtypeStruct((B, L, D), F32),
                   jax.ShapeDtypeStruct((D, D), BF16), jax.ShapeDtypeStruct((8, D), F32)),
        scratch_shapes=[pltpu.VMEM((D, D), F32)],
        compiler_params=_params(("arbitrary",) * 2))(y_na, y_ret, x, target, mod, final_g, wout_f)


def _dh_call(dsec, win_f, x, ctx, dx2, mod, norm_g, cp_in, cp_out):
    B, L, _ = x.shape
    LC = ctx.shape[1]
    nl = L // TQ

    def body(d0, d1, d2, d3, d4, d5, d6, d7, w_ref, x_ref, ctx_ref, dx2_ref, mod_ref, g_ref, cpi_ref, cpo_ref,
             gx_ref, sm_ref, sli_ref, slo_ref, ssem, rsem, lsem):
        drefs = (d0, d1, d2, d3, d4, d5, d6, d7)
        b, t = pl.program_id(0), pl.program_id(1)
        is_lat = t < nl

        @pl.when((b == 0) & (t == 0))
        def _():
            sm_ref[...] = jnp.zeros_like(sm_ref)

        def dh_of(secs):
            acc = jnp.zeros((TQ, D), F32)
            for sec in secs:
                s, half = divmod(sec, 2)
                acc = acc + _dot_nt(drefs[sec][...].astype(BF16), w_ref[s, :, half * 512:(half + 1) * 512])
            return acc

        def norm_bwd(dh, xt, mrow):
            scale = mrow[:, D:2 * D]
            g = g_ref[...]
            rstd = lax.rsqrt(jnp.mean(xt * xt, axis=-1, keepdims=True) + EPS)
            xn = xt * rstd
            dshift = jnp.sum(dh, axis=0, keepdims=True)
            dscale = jnp.sum(dh * (xn * g), axis=0, keepdims=True)
            dhn = dh * (1.0 + scale)
            sm_ref[0:1, :] += jnp.sum(dhn * xn, axis=0, keepdims=True)
            dxn = dhn * g
            dx = rstd * (dxn - xn * jnp.mean(dxn * xn, axis=-1, keepdims=True))
            return dshift, dscale, dx

        @pl.when(is_lat)
        def _():
            dshift, dscale, dx = norm_bwd(dh_of(range(8)), x_ref[...], mod_ref[pl.ds(b, 1), :])
            sm_ref[pl.ds(3 + b, 1), :] += dshift
            sm_ref[pl.ds(3 + B + b, 1), :] += dscale
            gx_ref[...] = dx2_ref[...] + dx

        @pl.when(jnp.logical_not(is_lat))
        def _():
            dshift, dscale, _ = norm_bwd(dh_of((1, 2, 5, 6)), ctx_ref[...], mod_ref[B:B + 1, :])
            sm_ref[1:2, :] += dshift
            sm_ref[2:3, :] += dscale

        mx, my, mc = _mesh_pos()
        s = 2 * mx + my
        cps, sls = (cpi_ref, cpo_ref), (sli_ref, slo_ref)
        own = [pltpu.make_async_copy(cps[a].at[s], sls[a].at[s], lsem.at[a]) for a in range(2)]
        sends, recvs, k = [], [], 0
        for px, py in _other_chips(mx, my):
            ps = 2 * px + py
            for a in range(2):
                sends.append(_remote(cps[a].at[ps], sls[a].at[s], ssem, rsem, k, (px, py, mc)))
                recvs.append(_remote(cps[a].at[s], sls[a].at[ps], ssem, rsem, k, (px, py, mc)))
                k += 1

        @pl.when((b == 0) & (t == 0))
        def _():
            for cp in own + sends:
                cp.start()

        @pl.when((b == B - 1) & (t == nl))
        def _():
            _finish(own, sends, recvs)

    lat = lambda b, t: (b, jnp.minimum(t, nl - 1), 0)
    tok = lambda b, t: (b, t, 0)
    sec_specs = [pl.BlockSpec((None, TQ, 512), lat if sec in (0, 3, 4, 7) else tok) for sec in range(8)]
    return pl.pallas_call(
        body, name="dh_norm_bwd", grid=(B, nl + 1),
        in_specs=sec_specs + [
            pl.BlockSpec((N_SHARD, D, D), lambda b, t: (0, 0, 0)),
            pl.BlockSpec((None, TQ, D), lat),
            pl.BlockSpec((None, LC, D), lambda b, t: (b, 0, 0)),
            pl.BlockSpec((None, TQ, D), lat),
            pl.BlockSpec((8, 3 * D), lambda b, t: (0, 0)),
            pl.BlockSpec((1, D), lambda b, t: (0, 0)), ANY, ANY],
        out_specs=(pl.BlockSpec((None, TQ, D), lat), pl.BlockSpec((8, D), lambda b, t: (0, 0)), ANY, ANY),
        out_shape=(jax.ShapeDtypeStruct((B, L, D), F32), jax.ShapeDtypeStruct((8, D), F32),
                   jax.ShapeDtypeStruct(cp_in.shape, cp_in.dtype), jax.ShapeDtypeStruct(cp_out.shape, cp_out.dtype)),
        scratch_shapes=[pltpu.SemaphoreType.DMA((6,)), pltpu.SemaphoreType.DMA((6,)),
                        pltpu.SemaphoreType.DMA((2,))],
        compiler_params=_params(("arbitrary",) * 2))(*dsec, win_f, x, ctx, dx2, mod, norm_g, cp_in, cp_out)


def _dw_call(dsec, h, L):
    B, T, _ = h.shape
    TW = 2 * TQ
    nl = L // TW
    KV = (1, 2, 5, 6)

    def body(d0, d1, d2, d3, d4, d5, d6, d7, c1, c2, c5, c6, h_ref, hc_ref, dw_ref, acc_ref):
        drefs = (d0, d1, d2, d3, d4, d5, d6, d7)
        crefs = dict(zip(KV, (c1, c2, c5, c6)))
        b, t = pl.program_id(0), pl.program_id(1)

        @pl.when((b == 0) & (t == 0))
        def _():
            acc_ref[...] = jnp.zeros_like(acc_ref)

        def add(hb, refs, secs):
            for sec in secs:
                s, half = divmod(sec, 2)
                acc_ref[s, :, half * 512:(half + 1) * 512] += _dot_tn(hb, refs[sec][...].astype(BF16))

        @pl.when(t < nl)
        def _():
            add(h_ref[...], drefs, range(8))

        @pl.when(t == nl)
        def _():
            add(hc_ref[...], crefs, KV)

        @pl.when((b == B - 1) & (t == nl))
        def _():
            dw_ref[...] = acc_ref[...].astype(BF16)

    lat = lambda b, t: (b, jnp.minimum(t, nl - 1), 0)
    ctx = lambda b, t: (b, L // TQ, 0)
    return pl.pallas_call(
        body, name="dw_in", grid=(B, nl + 1),
        in_specs=[pl.BlockSpec((None, TW, 512), lat)] * 8 + [pl.BlockSpec((None, TQ, 512), ctx)] * 4
        + [pl.BlockSpec((None, TW, D), lat), pl.BlockSpec((None, TQ, D), ctx)],
        out_specs=pl.BlockSpec((N_SHARD, D, D), lambda b, t: (0, 0, 0)),
        out_shape=jax.ShapeDtypeStruct((N_SHARD, D, D), BF16),
        scratch_shapes=[pltpu.VMEM((N_SHARD, D, D), F32)],
        compiler_params=_params(("arbitrary",) * 2, vmem_mb=60))(*dsec, *[dsec[k] for k in KV], h, h)


def _mesh_pos():
    return lax.axis_index("x"), lax.axis_index("y"), lax.axis_index("c")


def _flip(v, f):
    return 1 - v if f else v


def _remote(src, dst, ssem, rsem, k, peer):
    return pltpu.make_async_remote_copy(src_ref=src, dst_ref=dst, send_sem=ssem.at[k], recv_sem=rsem.at[k],
                                        device_id=peer, device_id_type=MESH)


def _other_chips(x, y):
    return [(_flip(x, fx), _flip(y, fy)) for fx, fy in ((1, 0), (0, 1), (1, 1))]


def _gather_copies(own_ref, all_ref, out_ref, hr, sems, k0, l0):
    ssem, rsem, lsem = sems
    mx, my, mc = _mesh_pos()
    s = 2 * mx + my
    sib = (mx, my, 1 - mc)
    own = pltpu.make_async_copy(own_ref, all_ref.at[s], lsem.at[l0])
    send, recv, fsend, frecv = [], [], [], []
    outs = [pltpu.make_async_copy(all_ref.at[s], out_ref.at[s], lsem.at[l0 + 1])]
    for k, (px, py) in enumerate(_other_chips(mx, my)):
        ps = 2 * px + py
        mine = all_ref.at[s, pl.ds(mc * hr, hr)]
        send.append(_remote(mine, mine, ssem, rsem, k0 + k, (px, py, mc)))
        got = all_ref.at[ps, pl.ds(mc * hr, hr)]
        recv.append(_remote(mine, got, ssem, rsem, k0 + k, (px, py, mc)))
        fsend.append(_remote(got, got, ssem, rsem, k0 + 3 + k, sib))
        theirs = all_ref.at[ps, pl.ds((1 - mc) * hr, hr)]
        frecv.append(_remote(theirs, theirs, ssem, rsem, k0 + 3 + k, sib))
        outs.append(pltpu.make_async_copy(all_ref.at[ps], out_ref.at[ps], lsem.at[l0 + 2 + k]))
    return own, send, recv, fsend, frecv, outs


def _all_to_all_small(src, dst_all, ssem, rsem, k0, x, y, cc):
    me = 4 * x + 2 * y + cc
    sends, recvs = [], []
    for f in range(1, N_DEV):
        px, py, pc = _flip(x, f & 4), _flip(y, f & 2), _flip(cc, f & 1)
        sends.append(_remote(src, dst_all.at[me], ssem, rsem, k0 + f - 1, (px, py, pc)))
        recvs.append(_remote(src, dst_all.at[4 * px + 2 * py + pc], ssem, rsem, k0 + f - 1, (px, py, pc)))
    return sends, recvs


def _finish(local, sends, recvs):
    for cp in recvs:
        cp.wait_recv()
    for cp in sends:
        cp.wait_send()
    for cp in local:
        cp.wait()


def _c_gather_call(c, rpb_flat):
    def body(c_ref, r_ref, c_all, bias_out, bias_ref, et_ref, ssem, rsem, lsem):
        x, y, cc = _mesh_pos()
        me = 4 * x + 2 * y + cc
        local = [pltpu.make_async_copy(c_ref, c_all.at[me], lsem.at[0])]
        c_send, c_recv = _all_to_all_small(c_ref, c_all, ssem, rsem, 0, x, y, cc)
        for cp in local + c_send:
            cp.start()
        bias_out_copies = _bias_body(r_ref, bias_ref, et_ref, bias_out, lsem.at[1])
        _finish(local + bias_out_copies, c_send, c_recv)

    bias_shape = (rpb_flat.shape[0], 3, TQ, KW)
    return pl.pallas_call(
        body, name="c_gather",
        in_specs=[pl.BlockSpec(memory_space=pltpu.VMEM), pl.BlockSpec(memory_space=pltpu.SMEM)],
        out_specs=(pl.BlockSpec(memory_space=pltpu.VMEM), ANY),
        out_shape=(jax.ShapeDtypeStruct((N_DEV,) + c.shape, c.dtype), jax.ShapeDtypeStruct(bias_shape, F32)),
        scratch_shapes=[pltpu.VMEM(bias_shape, F32), pltpu.VMEM((15, GRID_W, GRID_W), F32),
                        pltpu.SemaphoreType.DMA((N_DEV - 1,)), pltpu.SemaphoreType.DMA((N_DEV - 1,)),
                        pltpu.SemaphoreType.DMA((2,))],
        compiler_params=pltpu.CompilerParams(vmem_limit_bytes=56 << 20))(c, rpb_flat)


VROWS = 32


def _grad_halves_call(dwin_b, dwout_b, dbias, dlg):
    arrs = (dwin_b, dwout_b)
    hrs = [a.shape[1] // 2 for a in arrs]

    def body(din, dout, db_ref, dlg_ref, cp_in, cp_out, drpb_ref, dlgo_ref, got_in, got_out, p_ref, ssem, rsem):
        x, y, cc = _mesh_pos()
        sib = (x, y, 1 - cc)
        srcs, gots, cps = (din, dout), (got_in, got_out), (cp_in, cp_out)
        halves = [_remote(srcs[a].at[:, pl.ds((1 - cc) * hrs[a], hrs[a])], gots[a], ssem, rsem, a, sib)
                  for a in range(2)]
        for cp in halves:
            cp.start()
        _small_reduce_body(db_ref, dlg_ref, drpb_ref, dlgo_ref, p_ref)
        for cp in halves:
            cp.wait_recv()
        for a in range(2):
            for j in range(N_SHARD):
                def add(i, carry, a=a, j=j):
                    r = pl.multiple_of(i * VROWS, VROWS)
                    mine = srcs[a][j, pl.ds(pl.multiple_of(cc * hrs[a] + r, VROWS), VROWS), :].astype(F32)
                    cps[a][j, pl.ds(r, VROWS), :] = (
                        mine + gots[a][j, pl.ds(r, VROWS), :].astype(F32)).astype(BF16)
                    return carry
                lax.fori_loop(0, hrs[a] // VROWS, add, 0)
        for cp in halves:
            cp.wait_send()

    vmem = pl.BlockSpec(memory_space=pltpu.VMEM)
    half_shapes = [(N_SHARD, hrs[a], arrs[a].shape[2]) for a in range(2)]
    return pl.pallas_call(
        body, name="grad_halves",
        in_specs=[vmem] * 4, out_specs=(vmem,) * 4,
        out_shape=(jax.ShapeDtypeStruct(half_shapes[0], BF16), jax.ShapeDtypeStruct(half_shapes[1], BF16),
                   jax.ShapeDtypeStruct((dbias.shape[0], 16, 32), F32), jax.ShapeDtypeStruct((32, 128), F32)),
        scratch_shapes=[pltpu.VMEM(half_shapes[0], BF16), pltpu.VMEM(half_shapes[1], BF16),
                        pltpu.VMEM((32, GRID_W), F32),
                        pltpu.SemaphoreType.DMA((2,)), pltpu.SemaphoreType.DMA((2,))],
        compiler_params=pltpu.CompilerParams(vmem_limit_bytes=56 << 20))(dwin_b, dwout_b, dbias, dlg)


def _grad_finish_call(sl_in, sl_out, small, wada_b):
    arrs = (sl_in, sl_out)
    ws = wada_b.shape[1]

    def body(sin, sout, sm, wa_ref, gin, gout, sm_all, dparts, h_in, h_out, dp_own, ssem, rsem, lsem):
        x, y, cc = _mesh_pos()
        me = 4 * x + 2 * y + cc
        s = 2 * x + y
        sib = (x, y, 1 - cc)
        sls, hs, gs = (sin, sout), (h_in, h_out), (gin, gout)
        sm_send, sm_recv = _all_to_all_small(sm, sm_all, ssem, rsem, 2, x, y, cc)
        sm_own = pltpu.make_async_copy(sm, sm_all.at[me], lsem.at[0])
        for cp in sm_send + [sm_own]:
            cp.start()
        for a in range(2):
            def total(i, carry, a=a):
                rows = pl.ds(pl.multiple_of(i * VROWS, VROWS), VROWS)
                sl = sls[a]
                hs[a][rows, :] = ((sl[0, rows, :].astype(F32) + sl[1, rows, :].astype(F32))
                                  + sl[2, rows, :].astype(F32)) + sl[3, rows, :].astype(F32)
                return carry
            lax.fori_loop(0, arrs[a].shape[1] // VROWS, total, 0)
        mine = [pltpu.make_async_copy(hs[a], gs[a].at[cc], lsem.at[1 + a]) for a in range(2)]
        back = [_remote(hs[a], gs[a].at[cc], ssem, rsem, a, sib) for a in range(2)]
        back_recv = [_remote(hs[a], gs[a].at[1 - cc], ssem, rsem, a, sib) for a in range(2)]
        for cp in mine + back:
            cp.start()
        sm_own.wait()
        for cp in sm_recv:
            cp.wait_recv()
        shift_c = sm_all[0, R_SHIFT_C:R_SHIFT_C + 1, :]
        scale_c = sm_all[0, R_SCALE_C:R_SCALE_C + 1, :]
        for dv in range(1, N_DEV):
            shift_c = shift_c + sm_all[dv, R_SHIFT_C:R_SHIFT_C + 1, :]
            scale_c = scale_c + sm_all[dv, R_SCALE_C:R_SCALE_C + 1, :]
        dmc = jnp.concatenate([shift_c, scale_c, jnp.zeros((1, D), F32)], axis=1).astype(BF16)
        dmc = jnp.broadcast_to(dmc, (8, 3 * D))
        for sh in range(N_SHARD):
            @pl.when(s == sh)
            def _(sh=sh):
                dp_own[...] = _dot_nt(dmc[:, sh * ws:(sh + 1) * ws], wa_ref[...])
        dparts[s] = dp_own[...]
        d_send = [_remote(dp_own, dparts.at[s], ssem, rsem, 9 + k, (px, py, cc))
                  for k, (px, py) in enumerate(_other_chips(x, y))]
        d_recv = [_remote(dp_own, dparts.at[2 * px + py], ssem, rsem, 9 + k, (px, py, cc))
                  for k, (px, py) in enumerate(_other_chips(x, y))]
        for cp in d_send:
            cp.start()
        _finish(mine, back + sm_send + d_send, back_recv + d_recv)

    vmem = pl.BlockSpec(memory_space=pltpu.VMEM)
    return pl.pallas_call(
        body, name="grad_finish",
        in_specs=[vmem] * 4, out_specs=(vmem,) * 4,
        out_shape=(jax.ShapeDtypeStruct((2,) + sl_in.shape[1:], F32),
                   jax.ShapeDtypeStruct((2,) + sl_out.shape[1:], F32),
                   jax.ShapeDtypeStruct((N_DEV,) + small.shape, F32),
                   jax.ShapeDtypeStruct((N_SHARD, 8, D), F32)),
        scratch_shapes=[pltpu.VMEM(sl_in.shape[1:], F32), pltpu.VMEM(sl_out.shape[1:], F32),
                        pltpu.VMEM((8, D), F32),
                        pltpu.SemaphoreType.DMA((12,)), pltpu.SemaphoreType.DMA((12,)),
                        pltpu.SemaphoreType.DMA((3,))],
        compiler_params=pltpu.CompilerParams(vmem_limit_bytes=48 << 20))(sl_in, sl_out, small, wada_b)


def _adamw(w, g, m, v):
    m = ADAM_B1 * m + (1.0 - ADAM_B1) * g
    v = ADAM_B2 * v + (1.0 - ADAM_B2) * (g * g)
    m_hat = m / (1.0 - ADAM_B1 ** ADAM_STEP)
    v_hat = v / (1.0 - ADAM_B2 ** ADAM_STEP)
    return -ADAM_LR * (m_hat / (jnp.sqrt(v_hat) + ADAM_EPS) + ADAM_WD * w), m, v


def _adam_call(w, m, v, g, name):
    R, C = w.shape
    tr = min(R, 512)

    def body(w_ref, m_ref, v_ref, g_ref, go_ref, d_ref, mo_ref, vo_ref):
        g = g_ref[...]
        go_ref[...] = g
        d_ref[...], mo_ref[...], vo_ref[...] = _adamw(w_ref[...], g, m_ref[...], v_ref[...])

    spec = pl.BlockSpec((tr, C), lambda i: (i, 0))
    return pl.pallas_call(
        body, name=name, grid=(R // tr,), in_specs=[spec] * 4,
        out_specs=(spec,) * 4, out_shape=(jax.ShapeDtypeStruct((R, C), F32),) * 4,
        compiler_params=_params(("arbitrary",)))(w, m, v, g)


R_GF, R_NG, R_LOSS, R_RNG, R_LGF, R_LGB, R_SHIFT, R_SCALE, R_GATE, R_SHIFT_C, R_SCALE_C, R_RNG2, R_RPB = (
    0, 1, 2, 3, 4, 5, 6, 8, 10, 12, 13, 14, 16)
W_GF, W_NG, W_CCTX, W_RNG, W_DF, W_DB, W_BADA, W_RPB = 0, 1, 2, 3, 4, 5, 6, 9


SMALL = (("final_norm_g", W_GF, 1, D), ("norm_g", W_NG, 1, D), ("c_ctx", W_CCTX, 1, D),
         ("ret_norm_g", W_RNG, 1, 512), ("ret_decay_fwd", W_DF, 1, 4), ("ret_decay_bwd", W_DB, 1, 4),
         ("b_ada", W_BADA, 3, D), ("na_rpb", W_RPB, 4, D))
N_SMALL = len(SMALL)


def _small_final_call(sm_all, c_t, dact_parts, wada, m_ada, v_ada, small_w, small_m, small_v, B):
    ws = wada.shape[1]
    NB = N_DEV * B

    def body(*refs):
        sm_ref, ct_ref, wf_ref, wa_ref, ma_ref, va_ref = refs[:6]
        ins = refs[6:6 + 3 * N_SMALL]
        outs = refs[6 + 3 * N_SMALL:6 + 7 * N_SMALL]
        ga_ref, da_ref, mao_ref, vao_ref, loss_ref, dmod_ref, pk_ref = refs[6 + 7 * N_SMALL:]
        x, y, _ = _mesh_pos()
        s = 2 * x + y
        tot = sm_ref[0]
        for dv in range(1, N_DEV):
            tot = tot + sm_ref[dv]
        pk_ref[...] = jnp.zeros_like(pk_ref)
        for kind in range(3):
            for i, (_, row, nrow, width) in enumerate(SMALL):
                ref = ins[kind * N_SMALL + i]
                if nrow == 3:
                    for part in range(3):
                        pk_ref[kind, row + part:row + part + 1, :] = ref[:, part * D:(part + 1) * D]
                else:
                    pk_ref[kind, row:row + nrow, 0:width] = ref[...]
        w = pk_ref[0]
        cctx_ref = ins[2]
        for dv in range(N_DEV):
            for b in range(B):
                r = dv * B + b
                for part, row in enumerate((R_SHIFT, R_SCALE, R_GATE)):
                    dmod_ref[r:r + 1, part * D:(part + 1) * D] = sm_ref[dv, row + b:row + b + 1, :]
        dmod_ref[NB:NB + 1, 0:D] = tot[R_SHIFT_C:R_SHIFT_C + 1, :]
        dmod_ref[NB:NB + 1, D:2 * D] = tot[R_SCALE_C:R_SCALE_C + 1, :]
        dmod_ref[NB:NB + 1, 2 * D:3 * D] = jnp.zeros((1, D), F32)
        dmod_ref[NB + 1:, :] = jnp.zeros((dmod_ref.shape[0] - NB - 1, 3 * D), F32)
        dmod = dmod_ref[...]
        cc = cctx_ref[...]
        scc = _sigmoid(cc)
        ct = ct_ref[...]
        act_t = ct * _sigmoid(ct)
        dact = wf_ref[0, 0:1, :]
        for sh in range(1, N_SHARD):
            dact = dact + wf_ref[sh, 0:1, :]
        g = jnp.zeros((16, D), F32)
        rows = lax.broadcasted_iota(jnp.int32, (16, D), 0)

        def put(g, row, val):
            return jnp.where(rows == row, val, g)

        g = put(g, W_GF, tot[R_GF:R_GF + 1, :])
        g = put(g, W_NG, tot[R_NG:R_NG + 1, :])
        g = put(g, W_CCTX, dact * (scc * (1.0 + cc * (1.0 - scc))))
        g = put(g, W_RNG, tot[R_RNG:R_RNG + 1, :] + tot[R_RNG2:R_RNG2 + 1, :])
        g = put(g, W_DF, tot[R_LGF:R_LGF + 1, :] * (-jnp.exp(w[W_DF:W_DF + 1, :])))
        g = put(g, W_DB, tot[R_LGB:R_LGB + 1, :] * (-jnp.exp(w[W_DB:W_DB + 1, :])))
        db = jnp.sum(dmod, axis=0, keepdims=True)
        for part in range(3):
            g = put(g, W_BADA + part, db[:, part * D:(part + 1) * D])
        for part in range(4):
            g = put(g, W_RPB + part, tot[R_RPB + part:R_RPB + part + 1, :])
        for kind, val in enumerate((g,) + _adamw(w, g, pk_ref[1], pk_ref[2])):
            for i, (_, row, nrow, width) in enumerate(SMALL):
                out = outs[kind * N_SMALL + i]
                if nrow == 3:
                    for part in range(3):
                        out[:, part * D:(part + 1) * D] = val[row + part:row + part + 1, :]
                else:
                    out[...] = val[row:row + nrow, 0:width]
        loss_ref[...] = jnp.broadcast_to(
            (0.5 / D) * jnp.sum(tot[R_LOSS:R_LOSS + 1, :], axis=1, keepdims=True), (8, 128))
        for sh in range(N_SHARD):
            @pl.when(s == sh)
            def _():
                ga = jnp.dot(act_t, dmod[:, sh * ws:(sh + 1) * ws], precision=HIGHEST,
                             preferred_element_type=F32)
                ga_ref[...] = ga
                da_ref[...], mao_ref[...], vao_ref[...] = _adamw(wa_ref[...], ga, ma_ref[...], va_ref[...])

    sh_small = tuple(jax.ShapeDtypeStruct(a.shape, F32) for a in small_w)
    sh_ada = jax.ShapeDtypeStruct(wada.shape, F32)
    res = pl.pallas_call(
        body, name="small_final",
        out_shape=sh_small * 4 + (sh_ada,) * 4 + (jax.ShapeDtypeStruct((8, 128), F32),),
        scratch_shapes=[pltpu.VMEM((NB + 8, 3 * D), F32), pltpu.VMEM((3, 16, D), F32)],
        compiler_params=_params(vmem_mb=56))(
            sm_all, c_t, dact_parts, wada, m_ada, v_ada, *small_w, *small_m, *small_v)
    smalls = [res[k * N_SMALL:(k + 1) * N_SMALL] for k in range(4)]
    return smalls, res[4 * N_SMALL:4 * N_SMALL + 4], res[4 * N_SMALL + 4]


def _local_step(order, x, ctx, c_rows, norm_g, wada_b, b_shard, win_b, bias, dec_f, dec_b, ret_norm_g,
                wout_b, final_g, target):
    B, L, _ = x.shape
    LC = ctx.shape[1]
    assert B == 2
    cos2, sin2 = _rope_tables(L, LC)
    mod_part = _mod_part_call(c_rows, wada_b, b_shard)
    P, h, win_f, wout_f, mod = _inproj_gather_call(order, x, ctx, mod_part, norm_g, win_b, wout_b, cos2, sin2)
    y_na, o_na = _na_fwd_call(P, bias, L, LC)
    sf, sb = _ret_states_call(P, dec_f, dec_b, L, LC)
    y_ret, o_ret = _retc_fwd_call(P, sf, sb, dec_f, dec_b, ret_norm_g, L)
    dY, dx2, dwout_p, sm_out = _out_call(y_na, y_ret, x, target, mod, final_g, wout_f.reshape(D, D))
    dnq, dng, dnk, dnv, dbias = _na_bwd_call(P, bias, dY, o_na, L, LC)
    drq, drg, drk, drv, dgn, dlg = _retc_bwd_call(P, sf, sb, dec_f, dec_b, ret_norm_g, o_ret, dY, cos2, sin2, L, LC)
    dsec = (dnq, dnk, dnv, dng, drq, drk, drv, drg)
    dwin_b = _dw_call(dsec, h, L)
    cp_in, cp_out, drpb, dlg_sum = _grad_halves_call(
        dwin_b, dwout_p.reshape(N_SHARD, D // N_SHARD, D), dbias, dlg)
    grad_x, sm_dh, sl_in, sl_out = _dh_call(dsec, win_f, x, ctx, dx2, mod, norm_g, cp_in, cp_out)
    z = jnp.zeros((1, D), F32)
    pad = lambda v: jnp.pad(v.reshape(1, -1), ((0, 0), (0, D - v.size)))
    dlg_sum = dlg_sum.reshape(4, 8, 128)
    rpb_rows = jnp.pad(drpb[:, :15, :31].reshape(-1), (0, 4 * D - drpb.shape[0] * 465)).reshape(4, D)
    small = jnp.concatenate([
        sm_out[0:1], sm_dh[0:1], sm_out[1:2], pad(dgn[0]), pad(dlg_sum[:, 0, 0]), pad(dlg_sum[:, 1, 0]),
        sm_dh[3:5], sm_dh[5:7], sm_out[2:4], sm_dh[1:2], sm_dh[2:3], pad(dgn[1]), z, rpb_rows,
        jnp.zeros((SM_ROWS - 20, D), F32)], axis=0)
    return grad_x, sl_in, sl_out, small


def kernel(x, c, ctx, c_ctx, norm_g, w_ada, b_ada, w_in, na_rpb, ret_decay_fwd, ret_decay_bwd, ret_norm_g, w_out, final_norm_g, loss_target, m_c_ctx, m_norm_g, m_w_ada, m_b_ada, m_w_in, m_na_rpb, m_ret_decay_fwd, m_ret_decay_bwd, m_ret_norm_g, m_w_out, m_final_norm_g, v_c_ctx, v_norm_g, v_w_ada, v_b_ada, v_w_in, v_na_rpb, v_ret_decay_fwd, v_ret_decay_bwd, v_ret_norm_g, v_w_out, v_final_norm_g):
    B = x.shape[0]
    c_all, bias = _c_gather_call(c, na_rpb[0].reshape(na_rpb.shape[1], -1))
    c_rows = jnp.concatenate([c_all.reshape(N_DEV * B, D), c_ctx.reshape(1, D), jnp.zeros((7, D), F32)], axis=0)
    mx, my = lax.axis_index("x"), lax.axis_index("y")
    order = jnp.stack([2 * mx + my, 2 * (1 - mx) + my, 2 * mx + (1 - my),
                       2 * (1 - mx) + (1 - my)]).astype(jnp.int32)
    ws = w_ada.shape[2]
    b_shard = lax.dynamic_slice(b_ada, (0, (2 * mx + my) * ws), (1, ws))
    wada_b = w_ada[0].astype(BF16)
    grad_x, sl_in, sl_out, small = _local_step(
        order, x, ctx, c_rows, norm_g, wada_b, b_shard, w_in[0].astype(BF16), bias, ret_decay_fwd,
        ret_decay_bwd, ret_norm_g, w_out[0].astype(BF16), final_norm_g.reshape(1, D), loss_target)
    gin, gout, sm_all, dact_parts = _grad_finish_call(sl_in, sl_out, small, wada_b)
    g_win, d_win, nm_win, nv_win = _adam_call(
        w_in[0], m_w_in[0], v_w_in[0], gin.reshape(w_in.shape[1:]), "adam_w_in")
    g_wout, d_wout, nm_wout, nv_wout = _adam_call(
        w_out[0], m_w_out[0], v_w_out[0], gout.reshape(w_out.shape[1:]), "adam_w_out")

    def small_inputs(gf, ng, cc, rng, df, db, bada, rpb):
        return (gf.reshape(1, D), ng, cc.reshape(1, D), rng, df, db, bada,
                jnp.pad(rpb.reshape(-1), (0, 4 * D - rpb.size)).reshape(4, D))

    c_t = c_rows.T
    smalls, adas, loss = _small_final_call(
        sm_all, c_t, dact_parts, w_ada[0], m_w_ada[0], v_w_ada[0],
        small_inputs(final_norm_g, norm_g, c_ctx, ret_norm_g, ret_decay_fwd, ret_decay_bwd, b_ada, na_rpb),
        small_inputs(m_final_norm_g, m_norm_g, m_c_ctx, m_ret_norm_g, m_ret_decay_fwd, m_ret_decay_bwd, m_b_ada,
                     m_na_rpb),
        small_inputs(v_final_norm_g, v_norm_g, v_c_ctx, v_ret_norm_g, v_ret_decay_fwd, v_ret_decay_bwd, v_b_ada,
                     v_na_rpb), B)
    res = []
    for p, ada, win_o, wout_o in zip(smalls, adas, (g_win, d_win, nm_win, nv_win),
                                     (g_wout, d_wout, nm_wout, nv_wout)):
        gf, ng, cc, rng, df, db, bada, rpb = p
        res.append([cc.reshape(D), ng, ada[None], bada, win_o[None],
                    rpb.reshape(-1)[:na_rpb.size].reshape(na_rpb.shape), df, db, rng, wout_o[None], gf.reshape(D)])
    return (loss[0, 0], grad_x, *res[0], *res[1], *res[2], *res[3])
```

```python
import numpy as np
import jax
import jax.numpy as jnp
from jax import lax
from jax.experimental import pallas as pl
from jax.experimental.pallas import tpu as pltpu

F32 = jnp.float32
BF16 = jnp.bfloat16
HIGHEST = lax.Precision.HIGHEST

D = 1024
GRID_W = 64
NA_DH = 64
RET_DK = 128
ROPE_BASE = 10000.0
EPS = 1e-6
NEG = -1e30
TQ = 256
KW = 12 * GRID_W
N_SHARD = 4
N_DEV = 8
SM_ROWS = 24

ADAM_LR = 0.001
ADAM_B1 = 0.9
ADAM_B2 = 0.999
ADAM_EPS = 1e-08
ADAM_WD = 0.01
ADAM_STEP = 10

MESH = pl.DeviceIdType.MESH
ANY = pl.BlockSpec(memory_space=pl.ANY)


def _params(sem=None, vmem_mb=48):
    return pltpu.CompilerParams(dimension_semantics=sem, vmem_limit_bytes=vmem_mb << 20)


def _dot(a, b):
    return jnp.dot(a, b, preferred_element_type=F32)


def _dot_nt(a, b):
    return lax.dot_general(a, b, (((1,), (1,)), ((), ())), preferred_element_type=F32)


def _dot_tn(a, b):
    return lax.dot_general(a, b, (((0,), (0,)), ((), ())), preferred_element_type=F32)


def _sigmoid(x):
    return 1.0 / (1.0 + jnp.exp(-x))


def _rope_tables(L, LC):
    half = RET_DK // 2
    nf = half // 2
    t = np.arange(L)
    row = (t // GRID_W).astype(np.float32)
    col = (t % GRID_W).astype(np.float32)
    inv = (np.float32(ROPE_BASE) ** (-np.arange(nf, dtype=np.float32) / np.float32(nf))).astype(np.float32)
    ang = np.concatenate([row[:, None] * inv, col[:, None] * inv], axis=-1).astype(np.float32)
    cos, sin = np.cos(ang).astype(np.float32), np.sin(ang).astype(np.float32)
    cos2 = np.concatenate([cos, cos], axis=-1)
    sin2 = np.concatenate([-sin, sin], axis=-1)
    cos2 = np.concatenate([cos2, np.ones((LC, RET_DK), np.float32)], axis=0)
    sin2 = np.concatenate([sin2, np.zeros((LC, RET_DK), np.float32)], axis=0)
    return jnp.asarray(cos2), jnp.asarray(sin2)


def _mod_part_call(c_rows, wada_b, b_shard):
    def body(c_ref, w_ref, b_ref, o_ref):
        a = c_ref[...]
        o_ref[...] = _dot((a * _sigmoid(a)).astype(BF16), w_ref[...]) + b_ref[...]

    return pl.pallas_call(
        body, name="ada_mod", out_shape=jax.ShapeDtypeStruct((c_rows.shape[0], wada_b.shape[1]), F32),
        compiler_params=_params())(c_rows, wada_b, b_shard)


def _dc_masks():
    cq = lax.broadcasted_iota(jnp.int32, (GRID_W, GRID_W), 0)
    ck = lax.broadcasted_iota(jnp.int32, (GRID_W, GRID_W), 1)
    dc = jnp.clip(ck - cq + 15, 0, 30)
    c0 = jnp.clip(cq - 8, 0, GRID_W - 16)
    col_ok = (ck >= c0) & (ck < c0 + 16)
    return dc, col_ok


def _bias_blocks():
    out = []
    for typ, delta in enumerate((4, 0, -4)):
        for rq in range(4):
            for rkk in range(12):
                dr = rkk + delta - rq - 4
                if typ == 0:
                    ok = -rq <= dr <= 7 - rq
                elif typ == 1:
                    ok = -4 <= dr <= 3
                else:
                    ok = -4 - rq <= dr <= 3 - rq
                out.append((typ, rq, rkk, dr if ok else None))
    return out


def _bias_body(r_ref, bias_ref, et_ref, out_ref, sem):
    dc, col_ok = _dc_masks()
    masks = [(dc == j).astype(F32) for j in range(31)]
    nh = bias_ref.shape[0]

    def per_h(h, carry):
        for dr in range(15):
            t = jnp.zeros((GRID_W, GRID_W), F32)
            for j in range(31):
                t = t + masks[j] * r_ref[h, dr * 31 + j]
            et_ref[dr] = jnp.where(col_ok, t, NEG)
        neg = jnp.full((GRID_W, GRID_W), NEG, F32)
        for typ, rq, rkk, dr in _bias_blocks():
            blk = neg if dr is None else et_ref[dr + 7]
            bias_ref[h, typ, rq * 64:(rq + 1) * 64, rkk * 64:(rkk + 1) * 64] = blk
        pltpu.make_async_copy(bias_ref.at[h], out_ref.at[h], sem).start()
        return carry

    lax.fori_loop(0, nh, per_h, 0)
    return [pltpu.make_async_copy(bias_ref.at[h], out_ref.at[h], sem) for h in range(nh)]


def _bias_tile_sums(db_ref, hh):
    acc = {}
    for typ, rq, rkk, dr in _bias_blocks():
        if dr is None:
            continue
        blk = db_ref[hh, typ, rq * 64:(rq + 1) * 64, rkk * 64:(rkk + 1) * 64]
        acc[dr] = blk if dr not in acc else acc[dr] + blk
    return acc


def _small_reduce_body(dt_ref, dlg_ref, drpb_ref, dlgo_ref, p_ref):
    dc, _ = _dc_masks()
    masks = [(dc == j).astype(F32) for j in range(31)]
    ones = jnp.ones((8, GRID_W), F32)
    p_ref[...] = jnp.zeros_like(p_ref)
    drpb_ref[...] = jnp.zeros_like(drpb_ref)

    def per_h(h, carry):
        for dr in range(-7, 8):
            t = dt_ref[h, dr + 7]
            for j in range(31):
                p_ref[j:j + 1, :] = jnp.sum(t * masks[j], axis=0, keepdims=True)
            red = lax.dot_general(ones, p_ref[...], (((1,), (1,)), ((), ())),
                                  precision=HIGHEST, preferred_element_type=F32)
            drpb_ref[h, dr + 7:dr + 8, :] = red[0:1, :]
        return carry

    lax.fori_loop(0, dt_ref.shape[0], per_h, 0)
    x = dlg_ref[0]
    for b in range(1, dlg_ref.shape[0]):
        x = x + dlg_ref[b]
    x = x.reshape(4 * 8, x.shape[-1])
    dlgo_ref[...] = jnp.dot(x, jnp.ones((x.shape[-1], 128), F32), precision=HIGHEST,
                            preferred_element_type=F32)


def _inproj_gather_call(order, x, ctx, mod_part, norm_g, win_b, wout_b, cos2, sin2):
    B, L, _ = x.shape
    LC = ctx.shape[1]
    T = L + LC
    TI = 2 * TQ
    nl = L // TI
    nt = nl + 1
    assert LC == TQ and L % TI == 0
    kscale = RET_DK ** -0.5
    HR = D // 2
    pad_rows = nt * TI - T
    cos2 = jnp.pad(cos2, ((0, pad_rows), (0, 0)))
    sin2 = jnp.pad(sin2, ((0, pad_rows), (0, 0)))

    MW = mod_part.shape[1]
    NB = N_DEV * B

    def body(ord_ref, x_ref, ctx_ref, mp_ref, g_ref, wown_ref, woown_ref, cos_ref, sin_ref,
             p_ref, h_ref, wf_ref, wof_ref, modo_ref, w_all, wo_all, hs_ref, mp_all, mod_ref, ssem, rsem, lsem):
        j, b, t = pl.program_id(0), pl.program_id(1), pl.program_id(2)
        first = (b == 0) & (t == 0)
        mx, my, mc = _mesh_pos()
        s = 2 * mx + my

        m_send = [_remote(mp_ref, mp_all.at[s], ssem, rsem, 12 + k, (px, py, mc))
                  for k, (px, py) in enumerate(_other_chips(mx, my))]
        m_recv = [_remote(mp_ref, mp_all.at[2 * px + py], ssem, rsem, 12 + k, (px, py, mc))
                  for k, (px, py) in enumerate(_other_chips(mx, my))]

        sems = (ssem, rsem, lsem)
        own, ici_send, ici_recv, fwd_send, fwd_recv, outs = _gather_copies(wown_ref, w_all, wf_ref, HR, sems, 0, 0)
        oown, o_send, o_recv, o_fsend, o_frecv, o_outs = _gather_copies(
            woown_ref, wo_all, wof_ref, woown_ref.shape[0] // 2, sems, 6, 5)

        @pl.when(first & (j == 0))
        def _():
            for cp in m_send:
                cp.start()
            own.start()
            oown.start()
            mp_all[s] = mp_ref[...]
            own.wait()
            ici_send[0].start()
            ici_send[1].start()
            outs[0].start()
            oown.wait()
            for cp in m_recv:
                cp.wait_recv()
            me = 4 * mx + 2 * my + mc
            mod_ref[...] = jnp.zeros_like(mod_ref)
            for p in range(N_SHARD):
                for r in range(B):
                    mod_ref[r:r + 1, p * MW:(p + 1) * MW] = mp_all[p, pl.ds(B * me + r, 1), :]
                mod_ref[B:B + 1, p * MW:(p + 1) * MW] = mp_all[p, NB:NB + 1, :]
            modo_ref[...] = mod_ref[...]

        for k in range(3):
            @pl.when(first & (j == k + 1))
            def _(k=k):
                ici_recv[k].wait_recv()
                if k == 0:
                    ici_send[2].start()
                fwd_send[k].start()
                fwd_recv[k].wait_recv()
                outs[1 + k].start()
                if k == 1:
                    for cp in o_send:
                        cp.start()
                if k == 2:
                    for got, fwd in zip(o_recv, o_fsend):
                        got.wait_recv()
                        fwd.start()

        tile = b * nt + t

        @pl.when(j == 0)
        def _():
            is_lat = t < nl
            ctx_tile = jnp.concatenate([ctx_ref[...], jnp.zeros((TI - LC, D), F32)], axis=0)
            xt = jnp.where(is_lat, x_ref[...], ctx_tile)
            mrow = mod_ref[pl.ds(jnp.where(is_lat, b, B), 1), :]
            shift, scale = mrow[:, 0:D], mrow[:, D:2 * D]
            rstd = lax.rsqrt(jnp.mean(xt * xt, axis=-1, keepdims=True) + EPS)
            h0 = ((xt * rstd * g_ref[...]) * (1.0 + scale) + shift).astype(BF16)
            h_ref[...] = h0
            hs_ref[tile] = h0

        shard = ord_ref[j]

        def project(sh, nrows):
            hb = hs_ref[tile, 0:nrows, :]
            cs, sn = cos_ref[0:nrows, :], sin_ref[0:nrows, :]
            for half in range(2):
                sec = 2 * sh + half
                acc = _dot(hb, w_all[sh, :, half * 512:(half + 1) * 512])
                if sec == 0:
                    acc = acc * (NA_DH ** -0.5)
                if sec in (4, 5):
                    for q in range(4):
                        a = acc[:, q * 128:(q + 1) * 128]
                        r = a * cs + pltpu.roll(a, 64, 1) * sn
                        if sec == 5:
                            r = r * kscale
                        p_ref[0:nrows, half * 512 + q * 128:half * 512 + (q + 1) * 128] = r.astype(BF16)
                else:
                    p_ref[0:nrows, half * 512:(half + 1) * 512] = acc.astype(BF16)

        for sh in range(N_SHARD):
            @pl.when((shard == sh) & (t < nl))
            def _(sh=sh):
                project(sh, TI)

            @pl.when((shard == sh) & (t == nl))
            def _(sh=sh):
                project(sh, LC)

        @pl.when((j == N_SHARD - 1) & (b == B - 1) & (t == nt - 1))
        def _():
            for cp in o_frecv:
                cp.wait_recv()
            for cp in o_outs:
                cp.start()
            _finish(outs + o_outs, ici_send + fwd_send + o_send + o_fsend + m_send, [])

    tok = lambda j, b, t, o: (jnp.where(j == 0, b, B - 1), jnp.where(j == 0, jnp.minimum(t, nl - 1), nl - 1), 0)
    grid_spec = pltpu.PrefetchScalarGridSpec(
        num_scalar_prefetch=1, grid=(N_SHARD, B, nt),
        in_specs=[
            pl.BlockSpec((None, TI, D), tok),
            pl.BlockSpec((None, LC, D), lambda j, b, t, o: (jnp.where(j == 0, b, B - 1), 0, 0)),
            pl.BlockSpec(mod_part.shape, lambda j, b, t, o: (0, 0)),
            pl.BlockSpec((1, D), lambda j, b, t, o: (0, 0)),
            ANY, ANY,
            pl.BlockSpec((TI, RET_DK), lambda j, b, t, o: (t, 0)),
            pl.BlockSpec((TI, RET_DK), lambda j, b, t, o: (t, 0)),
        ],
        out_specs=(pl.BlockSpec((None, TI, D), lambda j, b, t, o: (b, t, o[j])),
                   pl.BlockSpec((None, TI, D), lambda j, b, t, o: (
                       jnp.where(j == 0, b, B - 1), jnp.where(j == 0, t, nt - 1), 0)), ANY, ANY,
                   pl.BlockSpec((8, 3 * D), lambda j, b, t, o: (0, 0))),
        scratch_shapes=[pltpu.VMEM((N_SHARD, D, D), BF16), pltpu.VMEM((N_SHARD,) + wout_b.shape, BF16),
                        pltpu.VMEM((B * nt, TI, D), BF16),
                        pltpu.VMEM((N_SHARD,) + mod_part.shape, F32), pltpu.VMEM((8, 3 * D), F32),
                        pltpu.SemaphoreType.DMA((15,)), pltpu.SemaphoreType.DMA((15,)),
                        pltpu.SemaphoreType.DMA((10,))])
    return pl.pallas_call(
        body, name="in_proj", grid_spec=grid_spec,
        out_shape=(jax.ShapeDtypeStruct((B, T, 4 * D), BF16), jax.ShapeDtypeStruct((B, T, D), BF16),
                   jax.ShapeDtypeStruct((N_SHARD, D, D), BF16),
                   jax.ShapeDtypeStruct((N_SHARD,) + wout_b.shape, BF16),
                   jax.ShapeDtypeStruct((8, 3 * D), F32)),
        compiler_params=_params(("arbitrary",) * 3, vmem_mb=56))(
            order, x, ctx, mod_part, norm_g, win_b, wout_b, cos2, sin2)


def _na_specs(L, T, rows, nh=2):
    nm = rows // 4
    w = nh * NA_DH
    per = 512 // w
    q_spec = pl.BlockSpec((None, TQ, w), lambda hp, b, m: (b, m, hp))
    k_spec = pl.BlockSpec((None, T, w), lambda hp, b, m: (b, 0, per + hp))
    v_spec = pl.BlockSpec((None, T, w), lambda hp, b, m: (b, 0, 2 * per + hp))
    g_spec = pl.BlockSpec((None, TQ, w), lambda hp, b, m: (b, m, 3 * per + hp))
    bias_spec = pl.BlockSpec((nh, 3, TQ, KW), lambda hp, b, m: (hp, 0, 0, 0))
    return nm, q_spec, k_spec, v_spec, g_spec, bias_spec


def _na_tile(m, nm, rows):
    typ = jnp.where(m == 0, 0, jnp.where(m == nm - 1, 2, 1))
    start = pl.multiple_of(jnp.clip(4 * m - 4, 0, rows - 12) * GRID_W, TQ)
    return typ, start


def _na_fwd_call(P, bias, L, LC):
    B, T, _ = P.shape
    rows = L // GRID_W
    NH = 4
    nm, q_spec, k_spec, v_spec, g_spec, bias_spec = _na_specs(L, T, rows, NH)

    def body(q_ref, k_ref, v_ref, g_ref, bias_ref, y_ref, o_ref):
        typ, start = _na_tile(pl.program_id(2), nm, rows)
        for hh in range(NH):
            ln = slice(hh * NA_DH, (hh + 1) * NA_DH)
            q = q_ref[:, ln]
            kw, vw = k_ref[pl.ds(start, KW), ln], v_ref[pl.ds(start, KW), ln]
            kc, vc = k_ref[L:L + LC, ln], v_ref[L:L + LC, ln]
            s1 = _dot_nt(q, kw) + bias_ref[hh, typ]
            s2 = _dot_nt(q, kc)
            mx = jnp.maximum(jnp.max(s1, axis=-1, keepdims=True), jnp.max(s2, axis=-1, keepdims=True))
            p1, p2 = jnp.exp(s1 - mx), jnp.exp(s2 - mx)
            inv = 1.0 / (jnp.sum(p1, axis=-1, keepdims=True) + jnp.sum(p2, axis=-1, keepdims=True))
            o = (_dot(p1.astype(BF16), vw) + _dot(p2.astype(BF16), vc)) * inv
            g = g_ref[:, ln].astype(F32)
            o_ref[:, ln] = o.astype(BF16)
            y_ref[:, ln] = (o * (g * _sigmoid(g))).astype(BF16)

    tile = pl.BlockSpec((None, TQ, NH * NA_DH), lambda hp, b, m: (b, m, hp))
    return pl.pallas_call(
        body, name="na_fwd", grid=(8 // NH, B, nm),
        in_specs=[q_spec, k_spec, v_spec, g_spec, bias_spec],
        out_specs=(tile, tile),
        out_shape=(jax.ShapeDtypeStruct((B, L, 512), BF16),) * 2,
        compiler_params=_params(("arbitrary",) * 3))(P, P, P, P, bias)


def _na_bwd_call(P, bias, dY, o_na, L, LC):
    B, T, _ = P.shape
    rows = L // GRID_W
    NH = 4
    W = NH * NA_DH
    nm, q_spec, k_spec, v_spec, g_spec, bias_spec = _na_specs(L, T, rows, NH)
    scale = NA_DH ** -0.5

    RB = 32

    def body(q_ref, k_ref, v_ref, g_ref, bias_ref, dy_ref, o_ref, dq_ref, dg_ref, dk_ref, dv_ref, dt_ref,
             db_ref, s1_ref, s2_ref, dp1_ref, dp2_ref, p1_ref, p2_ref, ds1_ref, ds2_ref, dkt_ref, dvt_ref):
        b, m = pl.program_id(1), pl.program_id(2)
        typ, start = _na_tile(m, nm, rows)

        @pl.when(m == 0)
        def _():
            dkt_ref[...] = jnp.zeros_like(dkt_ref)
            dvt_ref[...] = jnp.zeros_like(dvt_ref)

        @pl.when((m == 0) & (b == 0))
        def _():
            db_ref[...] = jnp.zeros_like(db_ref)

        for hh in range(NH):
            ln = slice(hh * NA_DH, (hh + 1) * NA_DH)
            q = q_ref[:, ln]
            kw, vw = k_ref[pl.ds(start, KW), ln], v_ref[pl.ds(start, KW), ln]
            kc, vc = k_ref[L:L + LC, ln], v_ref[L:L + LC, ln]
            g = g_ref[:, ln].astype(F32)
            sg = _sigmoid(g)
            dy = dy_ref[:, ln].astype(F32)
            do = (dy * (g * sg)).astype(BF16)
            s1_ref[hh] = _dot_nt(q, kw)
            s2_ref[hh] = _dot_nt(q, kc)
            dp1_ref[hh] = _dot_nt(do, vw)
            dp2_ref[hh] = _dot_nt(do, vc)

            def rows_pass(r, carry, hh=hh):
                rw = pl.ds(pl.multiple_of(r * RB, RB), RB)
                a = s1_ref[hh, rw, :] + bias_ref[hh, typ, rw, :]
                c = s2_ref[hh, rw, :]
                mx = jnp.maximum(jnp.max(a, axis=-1, keepdims=True), jnp.max(c, axis=-1, keepdims=True))
                e1, e2 = jnp.exp(a - mx), jnp.exp(c - mx)
                inv = 1.0 / (jnp.sum(e1, axis=-1, keepdims=True) + jnp.sum(e2, axis=-1, keepdims=True))
                p1, p2 = e1 * inv, e2 * inv
                p1_ref[hh, rw, :] = p1.astype(BF16)
                p2_ref[hh, rw, :] = p2.astype(BF16)
                dp1, dp2 = dp1_ref[hh, rw, :], dp2_ref[hh, rw, :]
                delta = jnp.sum(p1 * dp1, axis=-1, keepdims=True) + jnp.sum(p2 * dp2, axis=-1, keepdims=True)
                ds1 = p1 * (dp1 - delta)
                db_ref[hh, typ, rw, :] += ds1
                ds1_ref[hh, rw, :] = ds1.astype(BF16)
                ds2_ref[hh, rw, :] = (p2 * (dp2 - delta)).astype(BF16)
                return carry

            lax.fori_loop(0, TQ // RB, rows_pass, 0, unroll=True)
            p1b, p2b, ds1b, ds2b = p1_ref[hh], p2_ref[hh], ds1_ref[hh], ds2_ref[hh]
            dg_ref[:, ln] = (dy * o_ref[:, ln].astype(F32) * (sg * (1.0 + g * (1.0 - sg)))).astype(BF16)
            dq_ref[:, ln] = ((_dot(ds1b, kw) + _dot(ds2b, kc)) * scale).astype(BF16)
            dkt_ref[ln, pl.ds(start, KW)] += _dot_tn(q, ds1b)
            dvt_ref[ln, pl.ds(start, KW)] += _dot_tn(do, p1b)
            dkt_ref[ln, L:L + LC] += _dot_tn(q, ds2b)
            dvt_ref[ln, L:L + LC] += _dot_tn(do, p2b)

        @pl.when(m == nm - 1)
        def _():
            dk_ref[...] = dkt_ref[...].T.astype(BF16)
            dv_ref[...] = dvt_ref[...].T.astype(BF16)

        @pl.when((m == nm - 1) & (b == B - 1))
        def _():
            for hh in range(NH):
                for dr, t in _bias_tile_sums(db_ref, hh).items():
                    dt_ref[hh, dr + 7] = t

    tile = pl.BlockSpec((None, TQ, W), lambda hp, b, m: (b, m, hp))
    kv_out = pl.BlockSpec((None, T, W), lambda hp, b, m: (b, 0, hp))
    wide, narrow = (NH, TQ, KW), (NH, TQ, LC)
    return pl.pallas_call(
        body, name="na_bwd", grid=(8 // NH, B, nm),
        in_specs=[q_spec, k_spec, v_spec, g_spec, bias_spec, tile, tile],
        out_specs=(tile, tile, kv_out, kv_out,
                   pl.BlockSpec((NH, 15, GRID_W, GRID_W), lambda hp, b, m: (hp, 0, 0, 0))),
        out_shape=(jax.ShapeDtypeStruct((B, L, 512), BF16), jax.ShapeDtypeStruct((B, L, 512), BF16),
                   jax.ShapeDtypeStruct((B, T, 512), BF16), jax.ShapeDtypeStruct((B, T, 512), BF16),
                   jax.ShapeDtypeStruct((bias.shape[0], 15, GRID_W, GRID_W), F32)),
        scratch_shapes=[pltpu.VMEM((NH,) + bias.shape[1:], F32),
                        pltpu.VMEM(wide, F32), pltpu.VMEM(narrow, F32), pltpu.VMEM(wide, F32), pltpu.VMEM(narrow, F32),
                        pltpu.VMEM(wide, BF16), pltpu.VMEM(narrow, BF16), pltpu.VMEM(wide, BF16),
                        pltpu.VMEM(narrow, BF16), pltpu.VMEM((W, T), F32), pltpu.VMEM((W, T), F32)],
        compiler_params=_params(("arbitrary",) * 3, vmem_mb=60))(P, P, P, P, bias, dY, o_na)


def _head_scalar(dec_ref, h):
    lane = lax.broadcasted_iota(jnp.int32, dec_ref.shape, 1)
    return -jnp.sum(jnp.where(lane == h, jnp.exp(dec_ref[...]), 0.0), axis=1, keepdims=True)


def _chunk_decay(lgf, lgb):
    tau = lax.broadcasted_iota(jnp.int32, (TQ, 1), 0).astype(F32)
    sig = lax.broadcasted_iota(jnp.int32, (1, TQ), 1).astype(F32)
    dist = tau - sig
    dm = jnp.exp(dist * jnp.where(dist > 0, lgf, -lgb)) * jnp.where(dist == 0, 2.0, 1.0)
    return tau, dist, dm


def _ret_states_call(P, dec_f, dec_b, L, LC):
    B, T, _ = P.shape
    n = L // TQ

    def body(df_ref, db_ref, k_ref, v_ref, sf_ref, sb_ref):
        h = pl.program_id(1)
        lgf, lgb = _head_scalar(df_ref, h), _head_scalar(db_ref, h)
        tau = lax.broadcasted_iota(jnp.int32, (TQ, 1), 0).astype(F32)
        jc = lax.broadcasted_iota(jnp.int32, (LC, 1), 0).astype(F32)
        wf, wb = jnp.exp(lgf * (TQ - 1.0 - tau)), jnp.exp(lgb * tau)
        gcf, gcb = jnp.exp(lgf * float(TQ)), jnp.exp(lgb * float(TQ))
        kc, vc = k_ref[L:L + LC, :].astype(F32), v_ref[L:L + LC, :]

        def chunk_state(i, w):
            ks = pl.multiple_of(i * TQ, TQ)
            return _dot_tn((k_ref[pl.ds(ks, TQ), :].astype(F32) * w).astype(BF16), v_ref[pl.ds(ks, TQ), :])

        def fwd(i, s):
            sf_ref[i] = s
            return gcf * s + chunk_state(i, wf)

        lax.fori_loop(0, n, fwd, _dot_tn((kc * jnp.exp(lgf * (LC - 1.0 - jc))).astype(BF16), vc), unroll=True)

        def bwd(r, s):
            i = n - 1 - r
            sb_ref[i] = s
            return gcb * s + chunk_state(i, wb)

        lax.fori_loop(0, n, bwd, _dot_tn((kc * jnp.exp(lgb * jc)).astype(BF16), vc), unroll=True)

    st = pl.BlockSpec((None, None, n, RET_DK, RET_DK), lambda b, h: (b, h, 0, 0, 0))
    return pl.pallas_call(
        body, name="ret_states", grid=(B, 4),
        in_specs=[pl.BlockSpec((1, 4), lambda b, h: (0, 0)), pl.BlockSpec((1, 4), lambda b, h: (0, 0)),
                  pl.BlockSpec((None, T, 128), lambda b, h: (b, 0, 20 + h)),
                  pl.BlockSpec((None, T, 128), lambda b, h: (b, 0, 24 + h))],
        out_specs=(st, st),
        out_shape=(jax.ShapeDtypeStruct((B, 4, n, RET_DK, RET_DK), F32),) * 2,
        compiler_params=_params(("arbitrary",) * 2))(dec_f, dec_b, P, P)


def _retc_fwd_call(P, sf, sb, dec_f, dec_b, ret_norm_g, L):
    B, T, _ = P.shape
    sec = lambda k: pl.BlockSpec((None, TQ, 512), lambda b, i: (b, i, k))
    dec_spec = pl.BlockSpec((1, 4), lambda b, i: (0, 0))
    st_spec = pl.BlockSpec((None, 4, None, RET_DK, RET_DK), lambda b, i: (b, 0, i, 0, 0))

    def body(df_ref, db_ref, q_ref, k_ref, v_ref, g_ref, gn_ref, sf_ref, sb_ref, y_ref, o_ref):
        for h in range(4):
            ln = slice(h * RET_DK, (h + 1) * RET_DK)
            lgf, lgb = _head_scalar(df_ref, h), _head_scalar(db_ref, h)
            tau, _, dm = _chunk_decay(lgf, lgb)
            q = q_ref[:, ln]
            qf = q.astype(F32)
            acc = _dot((_dot_nt(q, k_ref[:, ln]) * dm).astype(BF16), v_ref[:, ln])
            acc = acc + _dot((qf * jnp.exp(lgf * (tau + 1.0))).astype(BF16), sf_ref[h].astype(BF16))
            acc = acc + _dot((qf * jnp.exp(lgb * (TQ - tau))).astype(BF16), sb_ref[h].astype(BF16))
            o_ref[:, ln] = acc
            rn = lax.rsqrt(jnp.mean(acc * acc, axis=-1, keepdims=True) + EPS)
            g = g_ref[:, ln].astype(F32)
            y_ref[:, ln] = ((acc * rn * gn_ref[:, ln]) * (g * _sigmoid(g))).astype(BF16)

    tile = pl.BlockSpec((None, TQ, 512), lambda b, i: (b, i, 0))
    return pl.pallas_call(
        body, name="ret_fwd", grid=(B, L // TQ),
        in_specs=[dec_spec, dec_spec, sec(4), sec(5), sec(6), sec(7),
                  pl.BlockSpec((1, 512), lambda b, i: (0, 0)), st_spec, st_spec],
        out_specs=(tile, tile),
        out_shape=(jax.ShapeDtypeStruct((B, L, 512), BF16), jax.ShapeDtypeStruct((B, L, 512), F32)),
        compiler_params=_params(("arbitrary",) * 2))(dec_f, dec_b, P, P, P, P, ret_norm_g, sf, sb)


def _retc_bwd_call(P, sf, sb, dec_f, dec_b, ret_norm_g, o_ret, dY, cos2, sin2, L, LC):
    B, T, _ = P.shape
    n = L // TQ
    C = float(TQ)
    kscale = RET_DK ** -0.5
    st_spec = pl.BlockSpec((None, 4, n, RET_DK, RET_DK), lambda b, i: (b, 0, 0, 0, 0))

    def body(df_ref, db_ref, q_ref, k_ref, v_ref, g_ref, gn_ref, o_ref, dy_ref, cos_ref, sin_ref, sf_ref, sb_ref,
             dq_ref, dg_ref, dko_ref, dvo_ref, dgn_ref, dlg_ref, dsf_ref, dsb_ref, dk_ref, dv_ref):
        i = pl.program_id(1)

        @pl.when(i == 0)
        def _():
            dk_ref[...] = jnp.zeros_like(dk_ref)
            dv_ref[...] = jnp.zeros_like(dv_ref)
            dgn_ref[...] = jnp.zeros_like(dgn_ref)
            dlg_ref[...] = jnp.zeros_like(dlg_ref)

        rows = pl.ds(pl.multiple_of(i * TQ, TQ), TQ)
        cs, sn = cos_ref[rows, :], sin_ref[rows, :]

        def one_head(h):
            ln = slice(h * RET_DK, (h + 1) * RET_DK)
            lgf, lgb = _head_scalar(df_ref, h), _head_scalar(db_ref, h)
            tau, dist, dm = _chunk_decay(lgf, lgb)

            def add_lg(row, x):
                csum = jnp.sum(x, axis=0, keepdims=True)
                tot = csum[:, 0:128]
                for part in range(1, x.shape[1] // 128):
                    tot = tot + csum[:, part * 128:(part + 1) * 128]
                dlg_ref[h, row:row + 1, :] += tot

            q = q_ref[:, ln]
            qf = q.astype(F32)
            o = o_ref[:, ln]
            g = g_ref[:, ln].astype(F32)
            dy = dy_ref[:, ln].astype(F32)
            gn = gn_ref[:, ln]
            sg = _sigmoid(g)
            rn = lax.rsqrt(jnp.mean(o * o, axis=-1, keepdims=True) + EPS)
            nrm = o * rn
            dg_ref[:, ln] = (dy * (nrm * gn) * (sg * (1.0 + g * (1.0 - sg)))).astype(BF16)
            dhn = dy * (g * sg)
            dgn_ref[:, ln] += jnp.sum(dhn * nrm, axis=0, keepdims=True)
            dnrm = dhn * gn
            do = rn * (dnrm - nrm * jnp.mean(dnrm * nrm, axis=-1, keepdims=True))
            dob = do.astype(BF16)
            ki, vi = k_ref[rows, ln], v_ref[rows, ln]
            s = _dot_nt(q, ki)
            dsv = _dot_nt(dob, vi)
            dsb = (dsv * dm).astype(BF16)
            dk_ref[rows, ln] += _dot_tn(dsb, q)
            dv_ref[rows, ln] += _dot_tn((s * dm).astype(BF16), dob)
            xw = s * dsv * dm * jnp.abs(dist)
            fpart = jnp.where(dist > 0, xw, 0.0)
            add_lg(0, fpart)
            add_lg(1, xw - fpart)
            dq = _dot(dsb, ki)
            af, ab = jnp.exp(lgf * (tau + 1.0)), jnp.exp(lgb * (C - tau))
            qa, qb = (qf * af).astype(BF16), (qf * ab).astype(BF16)
            sfi, sbi = sf_ref[h, i].astype(BF16), sb_ref[h, i].astype(BF16)
            dq = dq + af * _dot_nt(dob, sfi) + ab * _dot_nt(dob, sbi)
            dsf_ref[h, i] = _dot_tn(qa, dob)
            dsb_ref[h, i] = _dot_tn(qb, dob)
            add_lg(0, (tau + 1.0) * (_dot(qa, sfi) * do))
            add_lg(1, (C - tau) * (_dot(qb, sbi) * do))
            dq_ref[:, ln] = (dq * cs - pltpu.roll(dq, 64, 1) * sn).astype(BF16)

            @pl.when(i == n - 1)
            def _():
                jc = lax.broadcasted_iota(jnp.int32, (LC, 1), 0).astype(F32)
                crow = pl.ds(L, LC)

                def through_state(rws, w, dw, gst, row):
                    kk, vv = k_ref[rws, ln].astype(F32), v_ref[rws, ln]
                    gb = gst.astype(BF16)
                    vg = _dot_nt(vv, gb)
                    kw = kk * w
                    dk_ref[rws, ln] += w * vg
                    dv_ref[rws, ln] += _dot(kw.astype(BF16), gb)
                    add_lg(row, dw * (kw * vg))

                def scan(gc, w, dw, st_ref, dst_ref, order, row):
                    def step(r, gst):
                        j = order(r)
                        through_state(pl.ds(pl.multiple_of(j * TQ, TQ), TQ), w, dw, gst, row)
                        add_lg(row, (C * gc) * (gst * st_ref[h, j]))
                        return dst_ref[h, j] + gc * gst
                    return lax.fori_loop(0, n, step, jnp.zeros((RET_DK, RET_DK), F32), unroll=True)

                gcf, gcb = jnp.exp(lgf * C), jnp.exp(lgb * C)
                g0 = scan(gcf, jnp.exp(lgf * (C - 1.0 - tau)), C - 1.0 - tau, sf_ref, dsf_ref,
                          lambda r: n - 1 - r, 0)
                through_state(crow, jnp.exp(lgf * (LC - 1.0 - jc)), LC - 1.0 - jc, g0, 0)
                g1 = scan(gcb, jnp.exp(lgb * tau), tau, sb_ref, dsb_ref, lambda r: r, 1)
                through_state(crow, jnp.exp(lgb * jc), jc, g1, 1)
                dk = dk_ref[:, ln]
                dko_ref[:, ln] = ((dk * cos_ref[...] - pltpu.roll(dk, 64, 1) * sin_ref[...]) * kscale).astype(BF16)
                dvo_ref[:, ln] = dv_ref[:, ln].astype(BF16)

        for h in range(4):
            one_head(h)

    sec = lambda k: pl.BlockSpec((None, TQ, 512), lambda b, i: (b, i, k))
    full = lambda k: pl.BlockSpec((None, T, 512), lambda b, i: (b, 0, k))
    dec_spec = pl.BlockSpec((1, 4), lambda b, i: (0, 0))
    tab = pl.BlockSpec((T, RET_DK), lambda b, i: (0, 0))
    return pl.pallas_call(
        body, name="ret_bwd", grid=(B, n),
        in_specs=[dec_spec, dec_spec, sec(4), full(5), full(6), sec(7),
                  pl.BlockSpec((1, 512), lambda b, i: (0, 0)), sec(0), sec(1), tab, tab, st_spec, st_spec],
        out_specs=(sec(0), sec(0), full(0), full(0),
                   pl.BlockSpec((None, 1, 512), lambda b, i: (b, 0, 0)),
                   pl.BlockSpec((None, 4, 8, 128), lambda b, i: (b, 0, 0, 0))),
        out_shape=(jax.ShapeDtypeStruct((B, L, 512), BF16), jax.ShapeDtypeStruct((B, L, 512), BF16),
                   jax.ShapeDtypeStruct((B, T, 512), BF16), jax.ShapeDtypeStruct((B, T, 512), BF16),
                   jax.ShapeDtypeStruct((B, 1, 512), F32), jax.ShapeDtypeStruct((B, 4, 8, 128), F32)),
        scratch_shapes=[pltpu.VMEM((4, n, RET_DK, RET_DK), F32), pltpu.VMEM((4, n, RET_DK, RET_DK), F32),
                        pltpu.VMEM((T, 512), F32), pltpu.VMEM((T, 512), F32)],
        compiler_params=_params(("arbitrary",) * 2, vmem_mb=56))(
            dec_f, dec_b, P, P, P, P, ret_norm_g, o_ret, dY, cos2, sin2, sf, sb)


def _out_call(y_na, y_ret, x, target, mod, final_g, wout_f):
    B, L, _ = x.shape
    TO = 4 * TQ

    def body(yn_ref, yr_ref, x_ref, t_ref, mod_ref, gf_ref, w_ref, dy_ref, dx2_ref, dwb_ref, sm_ref, dw_ref):
        b, i = pl.program_id(0), pl.program_id(1)

        @pl.when((b == 0) & (i == 0))
        def _():
            dw_ref[...] = jnp.zeros_like(dw_ref)
            sm_ref[...] = jnp.zeros_like(sm_ref)

        gate = mod_ref[pl.ds(b, 1), 2 * D:3 * D]
        gf = gf_ref[...]
        yn, yr = yn_ref[...], yr_ref[...]
        ylat = _dot(yn, w_ref[0:512, :]) + _dot(yr, w_ref[512:1024, :])
        x2 = x_ref[...] + gate * ylat
        r = lax.rsqrt(jnp.mean(x2 * x2, axis=-1, keepdims=True) + EPS)
        xr = x2 * r
        err = xr * gf - t_ref[...]
        sm_ref[1:2, :] += jnp.sum(err * err, axis=0, keepdims=True)
        dout = err * (1.0 / D)
        sm_ref[0:1, :] += jnp.sum(dout * xr, axis=0, keepdims=True)
        gd = dout * gf
        dx2 = r * (gd - xr * jnp.mean(gd * xr, axis=-1, keepdims=True))
        dx2_ref[...] = dx2
        sm_ref[pl.ds(2 + b, 1), :] += jnp.sum(dx2 * ylat, axis=0, keepdims=True)
        dyl = (gate * dx2).astype(BF16)
        dy_ref[:, 0:512] = _dot_nt(dyl, w_ref[0:512, :]).astype(BF16)
        dy_ref[:, 512:1024] = _dot_nt(dyl, w_ref[512:1024, :]).astype(BF16)
        dw_ref[0:512, :] += _dot_tn(yn, dyl)
        dw_ref[512:1024, :] += _dot_tn(yr, dyl)

        @pl.when((b == B - 1) & (i == L // TO - 1))
        def _():
            dwb_ref[...] = dw_ref[...].astype(BF16)

    half = pl.BlockSpec((None, TO, 512), lambda b, i: (b, i, 0))
    full = pl.BlockSpec((None, TO, D), lambda b, i: (b, i, 0))
    return pl.pallas_call(
        body, name="out_proj_loss", grid=(B, L // TO),
        in_specs=[half, half, full, full,
                  pl.BlockSpec((8, 3 * D), lambda b, i: (0, 0)),
                  pl.BlockSpec((1, D), lambda b, i: (0, 0)),
                  pl.BlockSpec((D, D), lambda b, i: (0, 0))],
        out_specs=(full, full, pl.BlockSpec((D, D), lambda b, i: (0, 0)),
                   pl.BlockSpec((8, D), lambda b, i: (0, 0))),
        out_shape=(jax.ShapeDtypeStruct((B, L, D), BF16), jax.ShapeDtypeStruct((B, L, D), F32),
                   jax.ShapeDtypeStruct((D, D), BF16), jax.ShapeDtypeStruct((8, D), F32)),
        scratch_shapes=[pltpu.VMEM((D, D), F32)],
        compiler_params=_params(("arbitrary",) * 2))(y_na, y_ret, x, target, mod, final_g, wout_f)


def _dh_call(dsec, win_f, x, ctx, dx2, mod, norm_g, cp_in, cp_out):
    B, L, _ = x.shape
    LC = ctx.shape[1]
    nl = L // TQ

    def body(d0, d1, d2, d3, d4, d5, d6, d7, w_ref, x_ref, ctx_ref, dx2_ref, mod_ref, g_ref, cpi_ref, cpo_ref,
             gx_ref, sm_ref, sli_ref, slo_ref, ssem, rsem, lsem):
        drefs = (d0, d1, d2, d3, d4, d5, d6, d7)
        b, t = pl.program_id(0), pl.program_id(1)
        is_lat = t < nl

        @pl.when((b == 0) & (t == 0))
        def _():
            sm_ref[...] = jnp.zeros_like(sm_ref)

        def dh_of(secs):
            acc = jnp.zeros((TQ, D), F32)
            for sec in secs:
                s, half = divmod(sec, 2)
                acc = acc + _dot_nt(drefs[sec][...].astype(BF16), w_ref[s, :, half * 512:(half + 1) * 512])
            return acc

        def norm_bwd(dh, xt, mrow):
            scale = mrow[:, D:2 * D]
            g = g_ref[...]
            rstd = lax.rsqrt(jnp.mean(xt * xt, axis=-1, keepdims=True) + EPS)
            xn = xt * rstd
            dshift = jnp.sum(dh, axis=0, keepdims=True)
            dscale = jnp.sum(dh * (xn * g), axis=0, keepdims=True)
            dhn = dh * (1.0 + scale)
            sm_ref[0:1, :] += jnp.sum(dhn * xn, axis=0, keepdims=True)
            dxn = dhn * g
            dx = rstd * (dxn - xn * jnp.mean(dxn * xn, axis=-1, keepdims=True))
            return dshift, dscale, dx

        @pl.when(is_lat)
        def _():
            dshift, dscale, dx = norm_bwd(dh_of(range(8)), x_ref[...], mod_ref[pl.ds(b, 1), :])
            sm_ref[pl.ds(3 + b, 1), :] += dshift
            sm_ref[pl.ds(3 + B + b, 1), :] += dscale
            gx_ref[...] = dx2_ref[...] + dx

        @pl.when(jnp.logical_not(is_lat))
        def _():
            dshift, dscale, _ = norm_bwd(dh_of((1, 2, 5, 6)), ctx_ref[...], mod_ref[B:B + 1, :])
            sm_ref[1:2, :] += dshift
            sm_ref[2:3, :] += dscale

        mx, my, mc = _mesh_pos()
        s = 2 * mx + my
        cps, sls = (cpi_ref, cpo_ref), (sli_ref, slo_ref)
        own = [pltpu.make_async_copy(cps[a].at[s], sls[a].at[s], lsem.at[a]) for a in range(2)]
        sends, recvs, k = [], [], 0
        for px, py in _other_chips(mx, my):
            ps = 2 * px + py
            for a in range(2):
                sends.append(_remote(cps[a].at[ps], sls[a].at[s], ssem, rsem, k, (px, py, mc)))
                recvs.append(_remote(cps[a].at[s], sls[a].at[ps], ssem, rsem, k, (px, py, mc)))
                k += 1

        @pl.when((b == 0) & (t == 0))
        def _():
            for cp in own + sends:
                cp.start()

        @pl.when((b == B - 1) & (t == nl))
        def _():
            _finish(own, sends, recvs)

    lat = lambda b, t: (b, jnp.minimum(t, nl - 1), 0)
    tok = lambda b, t: (b, t, 0)
    sec_specs = [pl.BlockSpec((None, TQ, 512), lat if sec in (0, 3, 4, 7) else tok) for sec in range(8)]
    return pl.pallas_call(
        body, name="dh_norm_bwd", grid=(B, nl + 1),
        in_specs=sec_specs + [
            pl.BlockSpec((N_SHARD, D, D), lambda b, t: (0, 0, 0)),
            pl.BlockSpec((None, TQ, D), lat),
            pl.BlockSpec((None, LC, D), lambda b, t: (b, 0, 0)),
            pl.BlockSpec((None, TQ, D), lat),
            pl.BlockSpec((8, 3 * D), lambda b, t: (0, 0)),
            pl.BlockSpec((1, D), lambda b, t: (0, 0)), ANY, ANY],
        out_specs=(pl.BlockSpec((None, TQ, D), lat), pl.BlockSpec((8, D), lambda b, t: (0, 0)), ANY, ANY),
        out_shape=(jax.ShapeDtypeStruct((B, L, D), F32), jax.ShapeDtypeStruct((8, D), F32),
                   jax.ShapeDtypeStruct(cp_in.shape, cp_in.dtype), jax.ShapeDtypeStruct(cp_out.shape, cp_out.dtype)),
        scratch_shapes=[pltpu.SemaphoreType.DMA((6,)), pltpu.SemaphoreType.DMA((6,)),
                        pltpu.SemaphoreType.DMA((2,))],
        compiler_params=_params(("arbitrary",) * 2))(*dsec, win_f, x, ctx, dx2, mod, norm_g, cp_in, cp_out)


def _dw_call(dsec, h, L):
    B, T, _ = h.shape
    TW = 2 * TQ
    nl = L // TW
    KV = (1, 2, 5, 6)

    def body(d0, d1, d2, d3, d4, d5, d6, d7, c1, c2, c5, c6, h_ref, hc_ref, dw_ref, acc_ref):
        drefs = (d0, d1, d2, d3, d4, d5, d6, d7)
        crefs = dict(zip(KV, (c1, c2, c5, c6)))
        b, t = pl.program_id(0), pl.program_id(1)

        @pl.when((b == 0) & (t == 0))
        def _():
            acc_ref[...] = jnp.zeros_like(acc_ref)

        def add(hb, refs, secs):
            for sec in secs:
                s, half = divmod(sec, 2)
                acc_ref[s, :, half * 512:(half + 1) * 512] += _dot_tn(hb, refs[sec][...].astype(BF16))

        @pl.when(t < nl)
        def _():
            add(h_ref[...], drefs, range(8))

        @pl.when(t == nl)
        def _():
            add(hc_ref[...], crefs, KV)

        @pl.when((b == B - 1) & (t == nl))
        def _():
            dw_ref[...] = acc_ref[...].astype(BF16)

    lat = lambda b, t: (b, jnp.minimum(t, nl - 1), 0)
    ctx = lambda b, t: (b, L // TQ, 0)
    return pl.pallas_call(
        body, name="dw_in", grid=(B, nl + 1),
        in_specs=[pl.BlockSpec((None, TW, 512), lat)] * 8 + [pl.BlockSpec((None, TQ, 512), ctx)] * 4
        + [pl.BlockSpec((None, TW, D), lat), pl.BlockSpec((None, TQ, D), ctx)],
        out_specs=pl.BlockSpec((N_SHARD, D, D), lambda b, t: (0, 0, 0)),
        out_shape=jax.ShapeDtypeStruct((N_SHARD, D, D), BF16),
        scratch_shapes=[pltpu.VMEM((N_SHARD, D, D), F32)],
        compiler_params=_params(("arbitrary",) * 2, vmem_mb=60))(*dsec, *[dsec[k] for k in KV], h, h)


def _mesh_pos():
    return lax.axis_index("x"), lax.axis_index("y"), lax.axis_index("c")


def _flip(v, f):
    return 1 - v if f else v


def _remote(src, dst, ssem, rsem, k, peer):
    return pltpu.make_async_remote_copy(src_ref=src, dst_ref=dst, send_sem=ssem.at[k], recv_sem=rsem.at[k],
                                        device_id=peer, device_id_type=MESH)


def _other_chips(x, y):
    return [(_flip(x, fx), _flip(y, fy)) for fx, fy in ((1, 0), (0, 1), (1, 1))]


def _gather_copies(own_ref, all_ref, out_ref, hr, sems, k0, l0):
    ssem, rsem, lsem = sems
    mx, my, mc = _mesh_pos()
    s = 2 * mx + my
    sib = (mx, my, 1 - mc)
    own = pltpu.make_async_copy(own_ref, all_ref.at[s], lsem.at[l0])
    send, recv, fsend, frecv = [], [], [], []
    outs = [pltpu.make_async_copy(all_ref.at[s], out_ref.at[s], lsem.at[l0 + 1])]
    for k, (px, py) in enumerate(_other_chips(mx, my)):
        ps = 2 * px + py
        mine = all_ref.at[s, pl.ds(mc * hr, hr)]
        send.append(_remote(mine, mine, ssem, rsem, k0 + k, (px, py, mc)))
        got = all_ref.at[ps, pl.ds(mc * hr, hr)]
        recv.append(_remote(mine, got, ssem, rsem, k0 + k, (px, py, mc)))
        fsend.append(_remote(got, got, ssem, rsem, k0 + 3 + k, sib))
        theirs = all_ref.at[ps, pl.ds((1 - mc) * hr, hr)]
        frecv.append(_remote(theirs, theirs, ssem, rsem, k0 + 3 + k, sib))
        outs.append(pltpu.make_async_copy(all_ref.at[ps], out_ref.at[ps], lsem.at[l0 + 2 + k]))
    return own, send, recv, fsend, frecv, outs


def _all_to_all_small(src, dst_all, ssem, rsem, k0, x, y, cc):
    me = 4 * x + 2 * y + cc
    sends, recvs = [], []
    for f in range(1, N_DEV):
        px, py, pc = _flip(x, f & 4), _flip(y, f & 2), _flip(cc, f & 1)
        sends.append(_remote(src, dst_all.at[me], ssem, rsem, k0 + f - 1, (px, py, pc)))
        recvs.append(_remote(src, dst_all.at[4 * px + 2 * py + pc], ssem, rsem, k0 + f - 1, (px, py, pc)))
    return sends, recvs


def _finish(local, sends, recvs):
    for cp in recvs:
        cp.wait_recv()
    for cp in sends:
        cp.wait_send()
    for cp in local:
        cp.wait()


def _c_gather_call(c, rpb_flat):
    def body(c_ref, r_ref, c_all, bias_out, bias_ref, et_ref, ssem, rsem, lsem):
        x, y, cc = _mesh_pos()
        me = 4 * x + 2 * y + cc
        local = [pltpu.make_async_copy(c_ref, c_all.at[me], lsem.at[0])]
        c_send, c_recv = _all_to_all_small(c_ref, c_all, ssem, rsem, 0, x, y, cc)
        for cp in local + c_send:
            cp.start()
        bias_out_copies = _bias_body(r_ref, bias_ref, et_ref, bias_out, lsem.at[1])
        _finish(local + bias_out_copies, c_send, c_recv)

    bias_shape = (rpb_flat.shape[0], 3, TQ, KW)
    return pl.pallas_call(
        body, name="c_gather",
        in_specs=[pl.BlockSpec(memory_space=pltpu.VMEM), pl.BlockSpec(memory_space=pltpu.SMEM)],
        out_specs=(pl.BlockSpec(memory_space=pltpu.VMEM), ANY),
        out_shape=(jax.ShapeDtypeStruct((N_DEV,) + c.shape, c.dtype), jax.ShapeDtypeStruct(bias_shape, F32)),
        scratch_shapes=[pltpu.VMEM(bias_shape, F32), pltpu.VMEM((15, GRID_W, GRID_W), F32),
                        pltpu.SemaphoreType.DMA((N_DEV - 1,)), pltpu.SemaphoreType.DMA((N_DEV - 1,)),
                        pltpu.SemaphoreType.DMA((2,))],
        compiler_params=pltpu.CompilerParams(vmem_limit_bytes=56 << 20))(c, rpb_flat)


VROWS = 32


def _grad_halves_call(dwin_b, dwout_b, dbias, dlg):
    arrs = (dwin_b, dwout_b)
    hrs = [a.shape[1] // 2 for a in arrs]

    def body(din, dout, db_ref, dlg_ref, cp_in, cp_out, drpb_ref, dlgo_ref, got_in, got_out, p_ref, ssem, rsem):
        x, y, cc = _mesh_pos()
        sib = (x, y, 1 - cc)
        srcs, gots, cps = (din, dout), (got_in, got_out), (cp_in, cp_out)
        halves = [_remote(srcs[a].at[:, pl.ds((1 - cc) * hrs[a], hrs[a])], gots[a], ssem, rsem, a, sib)
                  for a in range(2)]
        for cp in halves:
            cp.start()
        _small_reduce_body(db_ref, dlg_ref, drpb_ref, dlgo_ref, p_ref)
        for cp in halves:
            cp.wait_recv()
        for a in range(2):
            for j in range(N_SHARD):
                def add(i, carry, a=a, j=j):
                    r = pl.multiple_of(i * VROWS, VROWS)
                    mine = srcs[a][j, pl.ds(pl.multiple_of(cc * hrs[a] + r, VROWS), VROWS), :].astype(F32)
                    cps[a][j, pl.ds(r, VROWS), :] = (
                        mine + gots[a][j, pl.ds(r, VROWS), :].astype(F32)).astype(BF16)
                    return carry
                lax.fori_loop(0, hrs[a] // VROWS, add, 0)
        for cp in halves:
            cp.wait_send()

    vmem = pl.BlockSpec(memory_space=pltpu.VMEM)
    half_shapes = [(N_SHARD, hrs[a], arrs[a].shape[2]) for a in range(2)]
    return pl.pallas_call(
        body, name="grad_halves",
        in_specs=[vmem] * 4, out_specs=(vmem,) * 4,
        out_shape=(jax.ShapeDtypeStruct(half_shapes[0], BF16), jax.ShapeDtypeStruct(half_shapes[1], BF16),
                   jax.ShapeDtypeStruct((dbias.shape[0], 16, 32), F32), jax.ShapeDtypeStruct((32, 128), F32)),
        scratch_shapes=[pltpu.VMEM(half_shapes[0], BF16), pltpu.VMEM(half_shapes[1], BF16),
                        pltpu.VMEM((32, GRID_W), F32),
                        pltpu.SemaphoreType.DMA((2,)), pltpu.SemaphoreType.DMA((2,))],
        compiler_params=pltpu.CompilerParams(vmem_limit_bytes=56 << 20))(dwin_b, dwout_b, dbias, dlg)


def _grad_finish_call(sl_in, sl_out, small, wada_b):
    arrs = (sl_in, sl_out)
    ws = wada_b.shape[1]

    def body(sin, sout, sm, wa_ref, gin, gout, sm_all, dparts, h_in, h_out, dp_own, ssem, rsem, lsem):
        x, y, cc = _mesh_pos()
        me = 4 * x + 2 * y + cc
        s = 2 * x + y
        sib = (x, y, 1 - cc)
        sls, hs, gs = (sin, sout), (h_in, h_out), (gin, gout)
        sm_send, sm_recv = _all_to_all_small(sm, sm_all, ssem, rsem, 2, x, y, cc)
        sm_own = pltpu.make_async_copy(sm, sm_all.at[me], lsem.at[0])
        for cp in sm_send + [sm_own]:
            cp.start()
        for a in range(2):
            def total(i, carry, a=a):
                rows = pl.ds(pl.multiple_of(i * VROWS, VROWS), VROWS)
                sl = sls[a]
                hs[a][rows, :] = ((sl[0, rows, :].astype(F32) + sl[1, rows, :].astype(F32))
                                  + sl[2, rows, :].astype(F32)) + sl[3, rows, :].astype(F32)
                return carry
            lax.fori_loop(0, arrs[a].shape[1] // VROWS, total, 0)
        mine = [pltpu.make_async_copy(hs[a], gs[a].at[cc], lsem.at[1 + a]) for a in range(2)]
        back = [_remote(hs[a], gs[a].at[cc], ssem, rsem, a, sib) for a in range(2)]
        back_recv = [_remote(hs[a], gs[a].at[1 - cc], ssem, rsem, a, sib) for a in range(2)]
        for cp in mine + back:
            cp.start()
        sm_own.wait()
        for cp in sm_recv:
            cp.wait_recv()
        shift_c = sm_all[0, R_SHIFT_C:R_SHIFT_C + 1, :]
        scale_c = sm_all[0, R_SCALE_C:R_SCALE_C + 1, :]
        for dv in range(1, N_DEV):
            shift_c = shift_c + sm_all[dv, R_SHIFT_C:R_SHIFT_C + 1, :]
            scale_c = scale_c + sm_all[dv, R_SCALE_C:R_SCALE_C + 1, :]
        dmc = jnp.concatenate([shift_c, scale_c, jnp.zeros((1, D), F32)], axis=1).astype(BF16)
        dmc = jnp.broadcast_to(dmc, (8, 3 * D))
        for sh in range(N_SHARD):
            @pl.when(s == sh)
            def _(sh=sh):
                dp_own[...] = _dot_nt(dmc[:, sh * ws:(sh + 1) * ws], wa_ref[...])
        dparts[s] = dp_own[...]
        d_send = [_remote(dp_own, dparts.at[s], ssem, rsem, 9 + k, (px, py, cc))
                  for k, (px, py) in enumerate(_other_chips(x, y))]
        d_recv = [_remote(dp_own, dparts.at[2 * px + py], ssem, rsem, 9 + k, (px, py, cc))
                  for k, (px, py) in enumerate(_other_chips(x, y))]
        for cp in d_send:
            cp.start()
        _finish(mine, back + sm_send + d_send, back_recv + d_recv)

    vmem = pl.BlockSpec(memory_space=pltpu.VMEM)
    return pl.pallas_call(
        body, name="grad_finish",
        in_specs=[vmem] * 4, out_specs=(vmem,) * 4,
        out_shape=(jax.ShapeDtypeStruct((2,) + sl_in.shape[1:], F32),
                   jax.ShapeDtypeStruct((2,) + sl_out.shape[1:], F32),
                   jax.ShapeDtypeStruct((N_DEV,) + small.shape, F32),
                   jax.ShapeDtypeStruct((N_SHARD, 8, D), F32)),
        scratch_shapes=[pltpu.VMEM(sl_in.shape[1:], F32), pltpu.VMEM(sl_out.shape[1:], F32),
                        pltpu.VMEM((8, D), F32),
                        pltpu.SemaphoreType.DMA((12,)), pltpu.SemaphoreType.DMA((12,)),
                        pltpu.SemaphoreType.DMA((3,))],
        compiler_params=pltpu.CompilerParams(vmem_limit_bytes=48 << 20))(sl_in, sl_out, small, wada_b)


def _adamw(w, g, m, v):
    m = ADAM_B1 * m + (1.0 - ADAM_B1) * g
    v = ADAM_B2 * v + (1.0 - ADAM_B2) * (g * g)
    m_hat = m / (1.0 - ADAM_B1 ** ADAM_STEP)
    v_hat = v / (1.0 - ADAM_B2 ** ADAM_STEP)
    return -ADAM_LR * (m_hat / (jnp.sqrt(v_hat) + ADAM_EPS) + ADAM_WD * w), m, v


def _adam_call(w, m, v, g, name):
    R, C = w.shape
    tr = min(R, 512)

    def body(w_ref, m_ref, v_ref, g_ref, go_ref, d_ref, mo_ref, vo_ref):
        g = g_ref[...]
        go_ref[...] = g
        d_ref[...], mo_ref[...], vo_ref[...] = _adamw(w_ref[...], g, m_ref[...], v_ref[...])

    spec = pl.BlockSpec((tr, C), lambda i: (i, 0))
    return pl.pallas_call(
        body, name=name, grid=(R // tr,), in_specs=[spec] * 4,
        out_specs=(spec,) * 4, out_shape=(jax.ShapeDtypeStruct((R, C), F32),) * 4,
        compiler_params=_params(("arbitrary",)))(w, m, v, g)


R_GF, R_NG, R_LOSS, R_RNG, R_LGF, R_LGB, R_SHIFT, R_SCALE, R_GATE, R_SHIFT_C, R_SCALE_C, R_RNG2, R_RPB = (
    0, 1, 2, 3, 4, 5, 6, 8, 10, 12, 13, 14, 16)
W_GF, W_NG, W_CCTX, W_RNG, W_DF, W_DB, W_BADA, W_RPB = 0, 1, 2, 3, 4, 5, 6, 9


SMALL = (("final_norm_g", W_GF, 1, D), ("norm_g", W_NG, 1, D), ("c_ctx", W_CCTX, 1, D),
         ("ret_norm_g", W_RNG, 1, 512), ("ret_decay_fwd", W_DF, 1, 4), ("ret_decay_bwd", W_DB, 1, 4),
         ("b_ada", W_BADA, 3, D), ("na_rpb", W_RPB, 4, D))
N_SMALL = len(SMALL)


def _small_final_call(sm_all, c_t, dact_parts, wada, m_ada, v_ada, small_w, small_m, small_v, B):
    ws = wada.shape[1]
    NB = N_DEV * B

    def body(*refs):
        sm_ref, ct_ref, wf_ref, wa_ref, ma_ref, va_ref = refs[:6]
        ins = refs[6:6 + 3 * N_SMALL]
        outs = refs[6 + 3 * N_SMALL:6 + 7 * N_SMALL]
        ga_ref, da_ref, mao_ref, vao_ref, loss_ref, dmod_ref, pk_ref = refs[6 + 7 * N_SMALL:]
        x, y, _ = _mesh_pos()
        s = 2 * x + y
        tot = sm_ref[0]
        for dv in range(1, N_DEV):
            tot = tot + sm_ref[dv]
        pk_ref[...] = jnp.zeros_like(pk_ref)
        for kind in range(3):
            for i, (_, row, nrow, width) in enumerate(SMALL):
                ref = ins[kind * N_SMALL + i]
                if nrow == 3:
                    for part in range(3):
                        pk_ref[kind, row + part:row + part + 1, :] = ref[:, part * D:(part + 1) * D]
                else:
                    pk_ref[kind, row:row + nrow, 0:width] = ref[...]
        w = pk_ref[0]
        cctx_ref = ins[2]
        for dv in range(N_DEV):
            for b in range(B):
                r = dv * B + b
                for part, row in enumerate((R_SHIFT, R_SCALE, R_GATE)):
                    dmod_ref[r:r + 1, part * D:(part + 1) * D] = sm_ref[dv, row + b:row + b + 1, :]
        dmod_ref[NB:NB + 1, 0:D] = tot[R_SHIFT_C:R_SHIFT_C + 1, :]
        dmod_ref[NB:NB + 1, D:2 * D] = tot[R_SCALE_C:R_SCALE_C + 1, :]
        dmod_ref[NB:NB + 1, 2 * D:3 * D] = jnp.zeros((1, D), F32)
        dmod_ref[NB + 1:, :] = jnp.zeros((dmod_ref.shape[0] - NB - 1, 3 * D), F32)
        dmod = dmod_ref[...]
        cc = cctx_ref[...]
        scc = _sigmoid(cc)
        ct = ct_ref[...]
        act_t = ct * _sigmoid(ct)
        dact = wf_ref[0, 0:1, :]
        for sh in range(1, N_SHARD):
            dact = dact + wf_ref[sh, 0:1, :]
        g = jnp.zeros((16, D), F32)
        rows = lax.broadcasted_iota(jnp.int32, (16, D), 0)

        def put(g, row, val):
            return jnp.where(rows == row, val, g)

        g = put(g, W_GF, tot[R_GF:R_GF + 1, :])
        g = put(g, W_NG, tot[R_NG:R_NG + 1, :])
        g = put(g, W_CCTX, dact * (scc * (1.0 + cc * (1.0 - scc))))
        g = put(g, W_RNG, tot[R_RNG:R_RNG + 1, :] + tot[R_RNG2:R_RNG2 + 1, :])
        g = put(g, W_DF, tot[R_LGF:R_LGF + 1, :] * (-jnp.exp(w[W_DF:W_DF + 1, :])))
        g = put(g, W_DB, tot[R_LGB:R_LGB + 1, :] * (-jnp.exp(w[W_DB:W_DB + 1, :])))
        db = jnp.sum(dmod, axis=0, keepdims=True)
        for part in range(3):
            g = put(g, W_BADA + part, db[:, part * D:(part + 1) * D])
        for part in range(4):
            g = put(g, W_RPB + part, tot[R_RPB + part:R_RPB + part + 1, :])
        for kind, val in enumerate((g,) + _adamw(w, g, pk_ref[1], pk_ref[2])):
            for i, (_, row, nrow, width) in enumerate(SMALL):
                out = outs[kind * N_SMALL + i]
                if nrow == 3:
                    for part in range(3):
                        out[:, part * D:(part + 1) * D] = val[row + part:row + part + 1, :]
                else:
                    out[...] = val[row:row + nrow, 0:width]
        loss_ref[...] = jnp.broadcast_to(
            (0.5 / D) * jnp.sum(tot[R_LOSS:R_LOSS + 1, :], axis=1, keepdims=True), (8, 128))
        for sh in range(N_SHARD):
            @pl.when(s == sh)
            def _():
                ga = jnp.dot(act_t, dmod[:, sh * ws:(sh + 1) * ws], precision=HIGHEST,
                             preferred_element_type=F32)
                ga_ref[...] = ga
                da_ref[...], mao_ref[...], vao_ref[...] = _adamw(wa_ref[...], ga, ma_ref[...], va_ref[...])

    sh_small = tuple(jax.ShapeDtypeStruct(a.shape, F32) for a in small_w)
    sh_ada = jax.ShapeDtypeStruct(wada.shape, F32)
    res = pl.pallas_call(
        body, name="small_final",
        out_shape=sh_small * 4 + (sh_ada,) * 4 + (jax.ShapeDtypeStruct((8, 128), F32),),
        scratch_shapes=[pltpu.VMEM((NB + 8, 3 * D), F32), pltpu.VMEM((3, 16, D), F32)],
        compiler_params=_params(vmem_mb=56))(
            sm_all, c_t, dact_parts, wada, m_ada, v_ada, *small_w, *small_m, *small_v)
    smalls = [res[k * N_SMALL:(k + 1) * N_SMALL] for k in range(4)]
    return smalls, res[4 * N_SMALL:4 * N_SMALL + 4], res[4 * N_SMALL + 4]


def _local_step(order, x, ctx, c_rows, norm_g, wada_b, b_shard, win_b, bias, dec_f, dec_b, ret_norm_g,
                wout_b, final_g, target):
    B, L, _ = x.shape
    LC = ctx.shape[1]
    assert B == 2
    cos2, sin2 = _rope_tables(L, LC)
    mod_part = _mod_part_call(c_rows, wada_b, b_shard)
    P, h, win_f, wout_f, mod = _inproj_gather_call(order, x, ctx, mod_part, norm_g, win_b, wout_b, cos2, sin2)
    y_na, o_na = _na_fwd_call(P, bias, L, LC)
    sf, sb = _ret_states_call(P, dec_f, dec_b, L, LC)
    y_ret, o_ret = _retc_fwd_call(P, sf, sb, dec_f, dec_b, ret_norm_g, L)
    dY, dx2, dwout_p, sm_out = _out_call(y_na, y_ret, x, target, mod, final_g, wout_f.reshape(D, D))
    dnq, dng, dnk, dnv, dbias = _na_bwd_call(P, bias, dY, o_na, L, LC)
    drq, drg, drk, drv, dgn, dlg = _retc_bwd_call(P, sf, sb, dec_f, dec_b, ret_norm_g, o_ret, dY, cos2, sin2, L, LC)
    dsec = (dnq, dnk, dnv, dng, drq, drk, drv, drg)
    dwin_b = _dw_call(dsec, h, L)
    cp_in, cp_out, drpb, dlg_sum = _grad_halves_call(
        dwin_b, dwout_p.reshape(N_SHARD, D // N_SHARD, D), dbias, dlg)
    grad_x, sm_dh, sl_in, sl_out = _dh_call(dsec, win_f, x, ctx, dx2, mod, norm_g, cp_in, cp_out)
    z = jnp.zeros((1, D), F32)
    pad = lambda v: jnp.pad(v.reshape(1, -1), ((0, 0), (0, D - v.size)))
    dlg_sum = dlg_sum.reshape(4, 8, 128)
    rpb_rows = jnp.pad(drpb[:, :15, :31].reshape(-1), (0, 4 * D - drpb.shape[0] * 465)).reshape(4, D)
    small = jnp.concatenate([
        sm_out[0:1], sm_dh[0:1], sm_out[1:2], pad(dgn[0]), pad(dlg_sum[:, 0, 0]), pad(dlg_sum[:, 1, 0]),
        sm_dh[3:5], sm_dh[5:7], sm_out[2:4], sm_dh[1:2], sm_dh[2:3], pad(dgn[1]), z, rpb_rows,
        jnp.zeros((SM_ROWS - 20, D), F32)], axis=0)
    return grad_x, sl_in, sl_out, small


def kernel(x, c, ctx, c_ctx, norm_g, w_ada, b_ada, w_in, na_rpb, ret_decay_fwd, ret_decay_bwd, ret_norm_g, w_out, final_norm_g, loss_target, m_c_ctx, m_norm_g, m_w_ada, m_b_ada, m_w_in, m_na_rpb, m_ret_decay_fwd, m_ret_decay_bwd, m_ret_norm_g, m_w_out, m_final_norm_g, v_c_ctx, v_norm_g, v_w_ada, v_b_ada, v_w_in, v_na_rpb, v_ret_decay_fwd, v_ret_decay_bwd, v_ret_norm_g, v_w_out, v_final_norm_g):
    B = x.shape[0]
    c_all, bias = _c_gather_call(c, na_rpb[0].reshape(na_rpb.shape[1], -1))
    c_rows = jnp.concatenate([c_all.reshape(N_DEV * B, D), c_ctx.reshape(1, D), jnp.zeros((7, D), F32)], axis=0)
    mx, my = lax.axis_index("x"), lax.axis_index("y")
    order = jnp.stack([2 * mx + my, 2 * (1 - mx) + my, 2 * mx + (1 - my),
                       2 * (1 - mx) + (1 - my)]).astype(jnp.int32)
    ws = w_ada.shape[2]
    b_shard = lax.dynamic_slice(b_ada, (0, (2 * mx + my) * ws), (1, ws))
    wada_b = w_ada[0].astype(BF16)
    grad_x, sl_in, sl_out, small = _local_step(
        order, x, ctx, c_rows, norm_g, wada_b, b_shard, w_in[0].astype(BF16), bias, ret_decay_fwd,
        ret_decay_bwd, ret_norm_g, w_out[0].astype(BF16), final_norm_g.reshape(1, D), loss_target)
    gin, gout, sm_all, dact_parts = _grad_finish_call(sl_in, sl_out, small, wada_b)
    g_win, d_win, nm_win, nv_win = _adam_call(
        w_in[0], m_w_in[0], v_w_in[0], gin.reshape(w_in.shape[1:]), "adam_w_in")
    g_wout, d_wout, nm_wout, nv_wout = _adam_call(
        w_out[0], m_w_out[0], v_w_out[0], gout.reshape(w_out.shape[1:]), "adam_w_out")

    def small_inputs(gf, ng, cc, rng, df, db, bada, rpb):
        return (gf.reshape(1, D), ng, cc.reshape(1, D), rng, df, db, bada,
                jnp.pad(rpb.reshape(-1), (0, 4 * D - rpb.size)).reshape(4, D))

    c_t = c_rows.T
    smalls, adas, loss = _small_final_call(
        sm_all, c_t, dact_parts, w_ada[0], m_w_ada[0], v_w_ada[0],
        small_inputs(final_norm_g, norm_g, c_ctx, ret_norm_g, ret_decay_fwd, ret_decay_bwd, b_ada, na_rpb),
        small_inputs(m_final_norm_g, m_norm_g, m_c_ctx, m_ret_norm_g, m_ret_decay_fwd, m_ret_decay_bwd, m_b_ada,
                     m_na_rpb),
        small_inputs(v_final_norm_g, v_norm_g, v_c_ctx, v_ret_norm_g, v_ret_decay_fwd, v_ret_decay_bwd, v_b_ada,
                     v_na_rpb), B)
    res = []
    for p, ada, win_o, wout_o in zip(smalls, adas, (g_win, d_win, nm_win, nv_win),
                                     (g_wout, d_wout, nm_wout, nv_wout)):
        gf, ng, cc, rng, df, db, bada, rpb = p
        res.append([cc.reshape(D), ng, ada[None], bada, win_o[None],
                    rpb.reshape(-1)[:na_rpb.size].reshape(na_rpb.shape), df, db, rng, wout_o[None], gf.reshape(D)])
    return (loss[0, 0], grad_x, *res[0], *res[1], *res[2], *res[3])
```
